```python
import math
import jax, jax.numpy as jnp
from jax import lax
import numpy as np

D_MODEL = 2048
BATCH = 8
SEQ = 2048
DEPTH = 1

HEAD_DIM = 128
N_HEADS_A = 8
D_A = N_HEADS_A * HEAD_DIM
DILATION_PATTERNS = ((128, 1), (512, 4), (2048, 16))
N_BUCKETS = 32
MAX_DISTANCE = 2048
CHUNK = 128
N_GROUPS_B = 8
D_GROUP_B = 128
D_B = N_GROUPS_B * D_GROUP_B
D_FF = 4 * D_MODEL
D_IN = 3 * D_A + 2 * D_B + 2 * D_MODEL
SPLITS = tuple(np.cumsum([D_A, D_A, D_A, D_B, D_B, D_MODEL]).tolist())
ALPHA = (2 * DEPTH) ** 0.25
BETA = (8 * DEPTH) ** -0.25
LN_EPS = 1e-5
NEG_INF = -1e30

kernel_name = "hybrid_dilated_attn_gmlp_block"


def layer_norm(x, gain, bias):
    xf = x.astype(jnp.float32)
    mean = jnp.mean(xf, axis=-1, keepdims=True)
    var = jnp.mean(jnp.square(xf - mean), axis=-1, keepdims=True)
    y = (xf - mean) * lax.rsqrt(var + LN_EPS) * gain.astype(jnp.float32) + bias.astype(jnp.float32)
    return y.astype(x.dtype)


def t5_causal_bucket(n):
    max_exact = N_BUCKETS // 2
    nf = jnp.maximum(n, 1).astype(jnp.float32)
    large = max_exact + (jnp.log(nf / max_exact) / math.log(MAX_DISTANCE / max_exact)
                         * (N_BUCKETS - max_exact)).astype(jnp.int32)
    large = jnp.minimum(large, N_BUCKETS - 1)
    return jnp.where(n < max_exact, n, large)


def dilated_window_attention(q, k, v, rel_bias, window, dilation):
    B, S, H, Dh = q.shape
    nb = window // dilation
    L = S // dilation
    nblk = -(-L // nb)
    Lp = nblk * nb

    def to_sub(t):
        t = t.reshape(B, L, dilation, H, Dh).transpose(0, 2, 1, 3, 4)
        t = jnp.pad(t, ((0, 0), (0, 0), (0, Lp - L), (0, 0), (0, 0)))
        return t.reshape(B, dilation, nblk, nb, H, Dh)

    def with_prev(t):
        prev = jnp.concatenate([jnp.zeros_like(t[:, :, :1]), t[:, :, :-1]], axis=2)
        return jnp.concatenate([prev, t], axis=3)

    qs = to_sub(q)
    kw = with_prev(to_sub(k))
    vw = with_prev(to_sub(v))

    scores = jnp.einsum('brnqhd,brnkhd->brnhqk', qs, kw).astype(jnp.float32) * (HEAD_DIM ** -0.5)

    qi = jnp.arange(nb)[:, None]
    kj = jnp.arange(2 * nb)[None, :]
    steps = nb + qi - kj
    band = (steps >= 0) & (steps <= nb)
    blk = jnp.arange(nblk)[:, None, None]
    key_ok = (blk * nb + kj[None] - nb) >= 0
    mask = band[None] & key_ok
    bucket = t5_causal_bucket(jnp.clip(steps, 0, nb) * dilation)
    bias = jnp.transpose(rel_bias[bucket].astype(jnp.float32), (2, 0, 1))

    scores = jnp.where(mask[None, None, :, None], scores + bias[None, None, None], NEG_INF)
    m = jnp.max(scores, axis=-1, keepdims=True)
    p = jnp.exp(scores - m)
    den = jnp.sum(p, axis=-1, keepdims=True)
    out = jnp.einsum('brnhqk,brnkhd->brnqhd', p, vw.astype(jnp.float32))
    den_t = jnp.transpose(den[..., 0], (0, 1, 2, 4, 3))
    out = out / den_t[..., None]
    lse = jnp.transpose(m[..., 0], (0, 1, 2, 4, 3)) + jnp.log(den_t)

    def from_sub(t):
        rest = t.shape[5:]
        t = t.reshape((B, dilation, Lp, H) + rest)[:, :, :L]
        t = jnp.moveaxis(t, 1, 2)
        return t.reshape((B, S, H) + rest)

    return from_sub(out), from_sub(lse)


def token_mixers(x, w_in, rel_bias, ln_v_gain, ln_v_bias, w_spatial, b_spatial,
                 w_proj_a, w_proj_b, w_out):
    B, S, _ = x.shape
    proj = jnp.einsum('bsd,de->bse', x, w_in)
    q, k, v, u, vb, ga, gb = jnp.split(proj, SPLITS, axis=-1)

    q = q.reshape(B, S, N_HEADS_A, HEAD_DIM)
    k = k.reshape(B, S, N_HEADS_A, HEAD_DIM)
    v = v.reshape(B, S, N_HEADS_A, HEAD_DIM)
    outs, lses = [], []
    for window, dilation in DILATION_PATTERNS:
        o, l = dilated_window_attention(q, k, v, rel_bias, window, dilation)
        outs.append(o)
        lses.append(l)
    mix_w = jax.nn.softmax(jnp.stack(lses, axis=0), axis=0)
    attn = jnp.sum(mix_w[..., None] * jnp.stack(outs, axis=0), axis=0)
    attn = attn.astype(x.dtype).reshape(B, S, D_A)

    u = jax.nn.gelu(u)
    vb = layer_norm(jax.nn.gelu(vb), ln_v_gain, ln_v_bias)
    nc = S // CHUNK
    vr = vb.reshape(B, nc, CHUNK, N_GROUPS_B, D_GROUP_B)
    causal = jnp.tril(jnp.ones((CHUNK, CHUNK), dtype=bool))
    ws = jnp.where(causal[None], w_spatial, 0.0).astype(vr.dtype)
    z = jnp.einsum('gij,bcjgd->bcigd', ws, vr) + jnp.transpose(b_spatial)[None, None, :, :, None]
    gmlp = (u.reshape(B, nc, CHUNK, N_GROUPS_B, D_GROUP_B) * z).reshape(B, S, D_B)

    y_a = jnp.einsum('bse,ed->bsd', attn, w_proj_a)
    y_b = jnp.einsum('bse,ed->bsd', gmlp, w_proj_b)
    merged = jax.nn.sigmoid(ga) * y_a + jax.nn.sigmoid(gb) * y_b
    return jnp.einsum('bsd,de->bse', merged, w_out)


def squared_relu_mlp(h, w_ff1, b_ff1, w_ff2, b_ff2):
    a = jnp.square(jax.nn.relu(jnp.einsum('bsd,df->bsf', h, w_ff1) + b_ff1))
    return jnp.einsum('bsf,fd->bsd', a, w_ff2) + b_ff2


def _fwd_setup_inputs(seed: int = 0) -> dict:
    key = jax.random.key(seed)
    ks = jax.random.split(key, 20)
    nrm = jax.random.normal
    f32 = jnp.float32
    x = nrm(ks[0], (BATCH, SEQ, D_MODEL), f32)
    col_scale = jnp.concatenate([
        jnp.ones((2 * D_A,), f32),
        BETA * jnp.ones((D_A,), f32),
        jnp.ones((2 * D_B + 2 * D_MODEL,), f32)]) * (D_MODEL ** -0.5)
    w_in = nrm(ks[1], (DEPTH, D_MODEL, D_IN), f32) * col_scale
    rel_bias = 0.1 * nrm(ks[2], (N_BUCKETS, N_HEADS_A), f32)
    ln_v_gain = 1.0 + 0.01 * nrm(ks[3], (DEPTH, D_B), f32)
    ln_v_bias = 0.01 * nrm(ks[4], (DEPTH, D_B), f32)
    w_spatial = nrm(ks[5], (DEPTH, N_GROUPS_B, CHUNK, CHUNK), f32) * (CHUNK ** -0.5)
    b_spatial = 1.0 + 0.01 * nrm(ks[6], (DEPTH, N_GROUPS_B, CHUNK), f32)
    w_proj_a = nrm(ks[7], (DEPTH, D_A, D_MODEL), f32) * (D_A ** -0.5) * BETA
    w_proj_b = nrm(ks[8], (DEPTH, D_B, D_MODEL), f32) * (D_B ** -0.5) * BETA
    w_out = nrm(ks[9], (DEPTH, D_MODEL, D_MODEL), f32) * (D_MODEL ** -0.5) * BETA
    ln1_gain = 1.0 + 0.01 * nrm(ks[10], (DEPTH, D_MODEL), f32)
    ln1_bias = 0.01 * nrm(ks[11], (DEPTH, D_MODEL), f32)
    w_ff1 = nrm(ks[12], (DEPTH, D_MODEL, D_FF), f32) * (D_MODEL ** -0.5) * BETA
    b_ff1 = 0.01 * nrm(ks[13], (DEPTH, D_FF), f32)
    w_ff2 = nrm(ks[14], (DEPTH, D_FF, D_MODEL), f32) * (D_FF ** -0.5) * BETA
    b_ff2 = 0.01 * nrm(ks[15], (DEPTH, D_MODEL), f32)
    ln2_gain = 1.0 + 0.01 * nrm(ks[16], (DEPTH, D_MODEL), f32)
    ln2_bias = 0.01 * nrm(ks[17], (DEPTH, D_MODEL), f32)
    return {"x": x, "w_in": w_in, "rel_bias": rel_bias, "ln_v_gain": ln_v_gain,
            "ln_v_bias": ln_v_bias, "w_spatial": w_spatial, "b_spatial": b_spatial,
            "w_proj_a": w_proj_a, "w_proj_b": w_proj_b, "w_out": w_out,
            "ln1_gain": ln1_gain, "ln1_bias": ln1_bias, "w_ff1": w_ff1, "b_ff1": b_ff1,
            "w_ff2": w_ff2, "b_ff2": b_ff2, "ln2_gain": ln2_gain, "ln2_bias": ln2_bias}


def _fwd_reference(x, w_in, rel_bias, ln_v_gain, ln_v_bias, w_spatial, b_spatial,
              w_proj_a, w_proj_b, w_out, ln1_gain, ln1_bias, w_ff1, b_ff1,
              w_ff2, b_ff2, ln2_gain, ln2_bias):
    h = x
    for layer in range(DEPTH):
        mix = token_mixers(h, w_in[layer], rel_bias, ln_v_gain[layer], ln_v_bias[layer],
                           w_spatial[layer], b_spatial[layer], w_proj_a[layer],
                           w_proj_b[layer], w_out[layer])
        h = layer_norm(ALPHA * h + mix, ln1_gain[layer], ln1_bias[layer])
        ff = squared_relu_mlp(h, w_ff1[layer], b_ff1[layer], w_ff2[layer], b_ff2[layer])
        h = layer_norm(ALPHA * h + ff, ln2_gain[layer], ln2_bias[layer])
    return h


import jax as _jax
import jax.numpy as _jnp

TWIN_FORMAT = 'train_step'
FWD_PARAMS = ['x', 'w_in', 'rel_bias', 'ln_v_gain', 'ln_v_bias', 'w_spatial', 'b_spatial', 'w_proj_a', 'w_proj_b', 'w_out', 'ln1_gain', 'ln1_bias', 'w_ff1', 'b_ff1', 'w_ff2', 'b_ff2', 'ln2_gain', 'ln2_bias']
TWIN_WEIGHTS = ['w_in', 'rel_bias', 'ln_v_gain', 'ln_v_bias', 'w_spatial', 'b_spatial', 'w_proj_a', 'w_proj_b', 'w_out', 'ln1_gain', 'ln1_bias', 'w_ff1', 'b_ff1', 'w_ff2', 'b_ff2', 'ln2_gain', 'ln2_bias']
TWIN_DIFF_INPUT = 'x'
TWIN_INPUTS = ['x', 'w_in', 'rel_bias', 'ln_v_gain', 'ln_v_bias', 'w_spatial', 'b_spatial', 'w_proj_a', 'w_proj_b', 'w_out', 'ln1_gain', 'ln1_bias', 'w_ff1', 'b_ff1', 'w_ff2', 'b_ff2', 'ln2_gain', 'ln2_bias', 'loss_target', 'm_w_in', 'm_rel_bias', 'm_ln_v_gain', 'm_ln_v_bias', 'm_w_spatial', 'm_b_spatial', 'm_w_proj_a', 'm_w_proj_b', 'm_w_out', 'm_ln1_gain', 'm_ln1_bias', 'm_w_ff1', 'm_b_ff1', 'm_w_ff2', 'm_b_ff2', 'm_ln2_gain', 'm_ln2_bias', 'v_w_in', 'v_rel_bias', 'v_ln_v_gain', 'v_ln_v_bias', 'v_w_spatial', 'v_b_spatial', 'v_w_proj_a', 'v_w_proj_b', 'v_w_out', 'v_ln1_gain', 'v_ln1_bias', 'v_w_ff1', 'v_b_ff1', 'v_w_ff2', 'v_b_ff2', 'v_ln2_gain', 'v_ln2_bias']
TWIN_OUTPUTS = ['loss', 'grad_x', 'grad_w_in', 'grad_rel_bias', 'grad_ln_v_gain', 'grad_ln_v_bias', 'grad_w_spatial', 'grad_b_spatial', 'grad_w_proj_a', 'grad_w_proj_b', 'grad_w_out', 'grad_ln1_gain', 'grad_ln1_bias', 'grad_w_ff1', 'grad_b_ff1', 'grad_w_ff2', 'grad_b_ff2', 'grad_ln2_gain', 'grad_ln2_bias', 'delta_w_in', 'delta_rel_bias', 'delta_ln_v_gain', 'delta_ln_v_bias', 'delta_w_spatial', 'delta_b_spatial', 'delta_w_proj_a', 'delta_w_proj_b', 'delta_w_out', 'delta_ln1_gain', 'delta_ln1_bias', 'delta_w_ff1', 'delta_b_ff1', 'delta_w_ff2', 'delta_b_ff2', 'delta_ln2_gain', 'delta_ln2_bias', 'new_m_w_in', 'new_m_rel_bias', 'new_m_ln_v_gain', 'new_m_ln_v_bias', 'new_m_w_spatial', 'new_m_b_spatial', 'new_m_w_proj_a', 'new_m_w_proj_b', 'new_m_w_out', 'new_m_ln1_gain', 'new_m_ln1_bias', 'new_m_w_ff1', 'new_m_b_ff1', 'new_m_w_ff2', 'new_m_b_ff2', 'new_m_ln2_gain', 'new_m_ln2_bias', 'new_v_w_in', 'new_v_rel_bias', 'new_v_ln_v_gain', 'new_v_ln_v_bias', 'new_v_w_spatial', 'new_v_b_spatial', 'new_v_w_proj_a', 'new_v_w_proj_b', 'new_v_w_out', 'new_v_ln1_gain', 'new_v_ln1_bias', 'new_v_w_ff1', 'new_v_b_ff1', 'new_v_w_ff2', 'new_v_b_ff2', 'new_v_ln2_gain', 'new_v_ln2_bias']
TWIN_LEAF_KINDS = {'loss': 'loss', 'grad_x': 'grad_x', 'grad_w_in': 'grad_w', 'grad_rel_bias': 'grad_w', 'grad_ln_v_gain': 'grad_w', 'grad_ln_v_bias': 'grad_w', 'grad_w_spatial': 'grad_w', 'grad_b_spatial': 'grad_w', 'grad_w_proj_a': 'grad_w', 'grad_w_proj_b': 'grad_w', 'grad_w_out': 'grad_w', 'grad_ln1_gain': 'grad_w', 'grad_ln1_bias': 'grad_w', 'grad_w_ff1': 'grad_w', 'grad_b_ff1': 'grad_w', 'grad_w_ff2': 'grad_w', 'grad_b_ff2': 'grad_w', 'grad_ln2_gain': 'grad_w', 'grad_ln2_bias': 'grad_w', 'delta_w_in': 'delta_w', 'delta_rel_bias': 'delta_w', 'delta_ln_v_gain': 'delta_w', 'delta_ln_v_bias': 'delta_w', 'delta_w_spatial': 'delta_w', 'delta_b_spatial': 'delta_w', 'delta_w_proj_a': 'delta_w', 'delta_w_proj_b': 'delta_w', 'delta_w_out': 'delta_w', 'delta_ln1_gain': 'delta_w', 'delta_ln1_bias': 'delta_w', 'delta_w_ff1': 'delta_w', 'delta_b_ff1': 'delta_w', 'delta_w_ff2': 'delta_w', 'delta_b_ff2': 'delta_w', 'delta_ln2_gain': 'delta_w', 'delta_ln2_bias': 'delta_w', 'new_m_w_in': 'new_m', 'new_m_rel_bias': 'new_m', 'new_m_ln_v_gain': 'new_m', 'new_m_ln_v_bias': 'new_m', 'new_m_w_spatial': 'new_m', 'new_m_b_spatial': 'new_m', 'new_m_w_proj_a': 'new_m', 'new_m_w_proj_b': 'new_m', 'new_m_w_out': 'new_m', 'new_m_ln1_gain': 'new_m', 'new_m_ln1_bias': 'new_m', 'new_m_w_ff1': 'new_m', 'new_m_b_ff1': 'new_m', 'new_m_w_ff2': 'new_m', 'new_m_b_ff2': 'new_m', 'new_m_ln2_gain': 'new_m', 'new_m_ln2_bias': 'new_m', 'new_v_w_in': 'new_v', 'new_v_rel_bias': 'new_v', 'new_v_ln_v_gain': 'new_v', 'new_v_ln_v_bias': 'new_v', 'new_v_w_spatial': 'new_v', 'new_v_b_spatial': 'new_v', 'new_v_w_proj_a': 'new_v', 'new_v_w_proj_b': 'new_v', 'new_v_w_out': 'new_v', 'new_v_ln1_gain': 'new_v', 'new_v_ln1_bias': 'new_v', 'new_v_w_ff1': 'new_v', 'new_v_b_ff1': 'new_v', 'new_v_w_ff2': 'new_v', 'new_v_b_ff2': 'new_v', 'new_v_ln2_gain': 'new_v', 'new_v_ln2_bias': 'new_v'}


def _forward(args):
    return _fwd_reference(*[args[k] for k in FWD_PARAMS])


def _output_shape():
    out = _jax.eval_shape(lambda: _forward(_fwd_setup_inputs(0)))
    return out.shape, out.dtype

N_MICROBATCH = 1
ADAM_LR = 0.001
ADAM_B1 = 0.9
ADAM_B2 = 0.999
ADAM_EPS = 1e-08
ADAM_WD = 0.01
ADAM_STEP = 10
PER_EXAMPLE_BATCH_AXIS = {'x': 0, 'loss_target': 0}
SHARED_INPUTS = []
_WEIGHT_DTYPES = {'w_in': _jnp.float32, 'rel_bias': _jnp.float32, 'ln_v_gain': _jnp.float32, 'ln_v_bias': _jnp.float32, 'w_spatial': _jnp.float32, 'b_spatial': _jnp.float32, 'w_proj_a': _jnp.float32, 'w_proj_b': _jnp.float32, 'w_out': _jnp.float32, 'ln1_gain': _jnp.float32, 'ln1_bias': _jnp.float32, 'w_ff1': _jnp.float32, 'b_ff1': _jnp.float32, 'w_ff2': _jnp.float32, 'b_ff2': _jnp.float32, 'ln2_gain': _jnp.float32, 'ln2_bias': _jnp.float32}
MOMENT_SCALE = {'w_in': 5.086502e-03, 'rel_bias': 3.362058e-03, 'ln_v_gain': 6.952222e-03, 'ln_v_bias': 6.393416e-03, 'w_spatial': 6.632960e-03, 'b_spatial': 9.501544e-03, 'w_proj_a': 2.369067e-03, 'w_proj_b': 1.401504e-02, 'w_out': 1.406737e-02, 'ln1_gain': 1.350581e-01, 'ln1_bias': 8.723166e-02, 'w_ff1': 1.285398e-02, 'b_ff1': 1.447783e-02, 'w_ff2': 2.379691e-02, 'b_ff2': 7.233267e-02, 'ln2_gain': 8.001405e+00, 'ln2_bias': 7.331175e-01}


def _to_microbatches(a, axis):
    t = _jnp.moveaxis(a, axis, 0)
    t = t.reshape((N_MICROBATCH, t.shape[0] // N_MICROBATCH) + t.shape[1:])
    return _jnp.moveaxis(t, 1, axis + 1)


def setup_inputs(seed: int = 0) -> dict:
    inp = _fwd_setup_inputs(seed)
    key = _jax.random.fold_in(_jax.random.key(seed), 7919)
    shape, _ = _output_shape()
    out = dict(inp)
    out["loss_target"] = _jax.random.normal(_jax.random.fold_in(key, 0), shape, _jnp.float32)
    for i, name in enumerate(TWIN_WEIGHTS):
        w = inp[name].astype(_jnp.float32)
        if MOMENT_SCALE is None:
            s = _jnp.sqrt(_jnp.mean(_jnp.square(w)) + 1e-30)
        else:
            s = MOMENT_SCALE[name]
        km, kv = _jax.random.split(_jax.random.fold_in(key, i + 1))
        out[name] = w
        out["m_" + name] = s * _jax.random.normal(km, w.shape, _jnp.float32)
        out["v_" + name] = (s * s) * _jax.random.uniform(kv, w.shape, _jnp.float32, 0.5, 1.5)
    if N_MICROBATCH > 1:
        for name, axis in PER_EXAMPLE_BATCH_AXIS.items():
            out[name] = _to_microbatches(out[name], axis)
    return {'x': out['x'], 'w_in': out['w_in'], 'rel_bias': out['rel_bias'], 'ln_v_gain': out['ln_v_gain'], 'ln_v_bias': out['ln_v_bias'], 'w_spatial': out['w_spatial'], 'b_spatial': out['b_spatial'], 'w_proj_a': out['w_proj_a'], 'w_proj_b': out['w_proj_b'], 'w_out': out['w_out'], 'ln1_gain': out['ln1_gain'], 'ln1_bias': out['ln1_bias'], 'w_ff1': out['w_ff1'], 'b_ff1': out['b_ff1'], 'w_ff2': out['w_ff2'], 'b_ff2': out['b_ff2'], 'ln2_gain': out['ln2_gain'], 'ln2_bias': out['ln2_bias'], 'loss_target': out['loss_target'], 'm_w_in': out['m_w_in'], 'm_rel_bias': out['m_rel_bias'], 'm_ln_v_gain': out['m_ln_v_gain'], 'm_ln_v_bias': out['m_ln_v_bias'], 'm_w_spatial': out['m_w_spatial'], 'm_b_spatial': out['m_b_spatial'], 'm_w_proj_a': out['m_w_proj_a'], 'm_w_proj_b': out['m_w_proj_b'], 'm_w_out': out['m_w_out'], 'm_ln1_gain': out['m_ln1_gain'], 'm_ln1_bias': out['m_ln1_bias'], 'm_w_ff1': out['m_w_ff1'], 'm_b_ff1': out['m_b_ff1'], 'm_w_ff2': out['m_w_ff2'], 'm_b_ff2': out['m_b_ff2'], 'm_ln2_gain': out['m_ln2_gain'], 'm_ln2_bias': out['m_ln2_bias'], 'v_w_in': out['v_w_in'], 'v_rel_bias': out['v_rel_bias'], 'v_ln_v_gain': out['v_ln_v_gain'], 'v_ln_v_bias': out['v_ln_v_bias'], 'v_w_spatial': out['v_w_spatial'], 'v_b_spatial': out['v_b_spatial'], 'v_w_proj_a': out['v_w_proj_a'], 'v_w_proj_b': out['v_w_proj_b'], 'v_w_out': out['v_w_out'], 'v_ln1_gain': out['v_ln1_gain'], 'v_ln1_bias': out['v_ln1_bias'], 'v_w_ff1': out['v_w_ff1'], 'v_b_ff1': out['v_b_ff1'], 'v_w_ff2': out['v_w_ff2'], 'v_b_ff2': out['v_b_ff2'], 'v_ln2_gain': out['v_ln2_gain'], 'v_ln2_bias': out['v_ln2_bias']}


def _loss(weights, diff, rest, loss_target):
    with _jax.named_scope("forward"):
        args = {**rest, TWIN_DIFF_INPUT: diff, **{k: w.astype(_WEIGHT_DTYPES[k]) for k, w in weights.items()}}
        y = _forward(args)
    with _jax.named_scope("loss_head"):
        err = _jnp.square(y.astype(_jnp.float32) - loss_target)
        return 0.5 * _jnp.sum(_jnp.mean(err, axis=-1)) if err.ndim else 0.5 * err


def _adamw(w, g, m, v):
    m = ADAM_B1 * m + (1.0 - ADAM_B1) * g
    v = ADAM_B2 * v + (1.0 - ADAM_B2) * _jnp.square(g)
    m_hat = m / (1.0 - ADAM_B1 ** ADAM_STEP)
    v_hat = v / (1.0 - ADAM_B2 ** ADAM_STEP)
    delta = -ADAM_LR * (m_hat / (_jnp.sqrt(v_hat) + ADAM_EPS) + ADAM_WD * w)
    return delta, m, v


def reference(x, w_in, rel_bias, ln_v_gain, ln_v_bias, w_spatial, b_spatial, w_proj_a, w_proj_b, w_out, ln1_gain, ln1_bias, w_ff1, b_ff1, w_ff2, b_ff2, ln2_gain, ln2_bias, loss_target, m_w_in, m_rel_bias, m_ln_v_gain, m_ln_v_bias, m_w_spatial, m_b_spatial, m_w_proj_a, m_w_proj_b, m_w_out, m_ln1_gain, m_ln1_bias, m_w_ff1, m_b_ff1, m_w_ff2, m_b_ff2, m_ln2_gain, m_ln2_bias, v_w_in, v_rel_bias, v_ln_v_gain, v_ln_v_bias, v_w_spatial, v_b_spatial, v_w_proj_a, v_w_proj_b, v_w_out, v_ln1_gain, v_ln1_bias, v_w_ff1, v_b_ff1, v_w_ff2, v_b_ff2, v_ln2_gain, v_ln2_bias):
    given = dict(x=x, w_in=w_in, rel_bias=rel_bias, ln_v_gain=ln_v_gain, ln_v_bias=ln_v_bias, w_spatial=w_spatial, b_spatial=b_spatial, w_proj_a=w_proj_a, w_proj_b=w_proj_b, w_out=w_out, ln1_gain=ln1_gain, ln1_bias=ln1_bias, w_ff1=w_ff1, b_ff1=b_ff1, w_ff2=w_ff2, b_ff2=b_ff2, ln2_gain=ln2_gain, ln2_bias=ln2_bias, loss_target=loss_target, m_w_in=m_w_in, m_rel_bias=m_rel_bias, m_ln_v_gain=m_ln_v_gain, m_ln_v_bias=m_ln_v_bias, m_w_spatial=m_w_spatial, m_b_spatial=m_b_spatial, m_w_proj_a=m_w_proj_a, m_w_proj_b=m_w_proj_b, m_w_out=m_w_out, m_ln1_gain=m_ln1_gain, m_ln1_bias=m_ln1_bias, m_w_ff1=m_w_ff1, m_b_ff1=m_b_ff1, m_w_ff2=m_w_ff2, m_b_ff2=m_b_ff2, m_ln2_gain=m_ln2_gain, m_ln2_bias=m_ln2_bias, v_w_in=v_w_in, v_rel_bias=v_rel_bias, v_ln_v_gain=v_ln_v_gain, v_ln_v_bias=v_ln_v_bias, v_w_spatial=v_w_spatial, v_b_spatial=v_b_spatial, v_w_proj_a=v_w_proj_a, v_w_proj_b=v_w_proj_b, v_w_out=v_w_out, v_ln1_gain=v_ln1_gain, v_ln1_bias=v_ln1_bias, v_w_ff1=v_w_ff1, v_b_ff1=v_b_ff1, v_w_ff2=v_w_ff2, v_b_ff2=v_b_ff2, v_ln2_gain=v_ln2_gain, v_ln2_bias=v_ln2_bias)
    weights = {n: given[n] for n in TWIN_WEIGHTS}
    shared = {n: given[n] for n in SHARED_INPUTS}
    per_example = {n: given[n] for n in ['x']}
    grad_fn = _jax.value_and_grad(_loss, argnums=(0, 1))

    def one_microbatch(ex, loss_target):
        ex = dict(ex)
        diff = ex.pop(TWIN_DIFF_INPUT)
        return grad_fn(weights, diff, {**shared, **ex}, loss_target)

    if N_MICROBATCH == 1:
        loss, (grad_w, grad_x) = one_microbatch(per_example, given["loss_target"])
    else:
        def body(carry, xs):
            loss_sum, grad_sum = carry
            l_k, (gw_k, gx_k) = one_microbatch(xs[0], xs[1])
            with _jax.named_scope("update"):
                return (loss_sum + l_k, _jax.tree.map(_jnp.add, grad_sum, gw_k)), gx_k

        init = (_jnp.zeros((), _jnp.float32), _jax.tree.map(_jnp.zeros_like, weights))
        (loss, grad_w), grad_x = _jax.lax.scan(body, init, (per_example, given["loss_target"]))
    with _jax.named_scope("update"):
        delta_w, new_m, new_v = {}, {}, {}
        for n in TWIN_WEIGHTS:
            delta_w[n], new_m[n], new_v[n] = _adamw(weights[n], grad_w[n], given["m_" + n], given["v_" + n])
    return (loss, grad_x, *[grad_w[n] for n in TWIN_WEIGHTS], *[delta_w[n] for n in TWIN_WEIGHTS],
            *[new_m[n] for n in TWIN_WEIGHTS], *[new_v[n] for n in TWIN_WEIGHTS])
```

```python
import math

import jax
import jax.numpy as jnp
import numpy as np
from jax import lax
from jax.experimental import pallas as pl
from jax.experimental.pallas import tpu as pltpu

F32 = jnp.float32
BF16 = jnp.bfloat16
I32 = jnp.int32

SEQ = 2048
D_MODEL = 2048
HEAD_DIM = 128
N_HEADS = 8
N_GROUPS = 8
D_FF = 4 * D_MODEL
BLOCK = 128
DILATIONS = (1, 4, 16)
N_BUCKETS = 32
MAX_DISTANCE = 2048
ALPHA = 2.0 ** 0.25
LN_EPS = 1e-5
NEG_INF = -1e30
N_DEV = 8
N_CHIPS = 4
N_SUB = 16
ADAM_LR, ADAM_B1, ADAM_B2, ADAM_EPS, ADAM_WD, ADAM_STEP = 0.001, 0.9, 0.999, 1e-08, 0.01, 10
LANES = 128
SUBLANES = 8
VMEM_LIMIT = 56 * 1024 * 1024
MESH = pl.DeviceIdType.MESH
WEIGHT_ORDER = ("w_in", "rel_bias", "ln_v_gain", "ln_v_bias", "w_spatial", "b_spatial", "w_proj_a", "w_proj_b", "w_out",
                "ln1_gain", "ln1_bias", "w_ff1", "b_ff1", "w_ff2", "b_ff2", "ln2_gain", "ln2_bias")
BIG = (("w_in", "col"), ("w_proj_a", "col"), ("w_proj_b", "col"), ("w_out", "row"), ("w_ff1", "col"), ("w_ff2", "row"))


def _d_a():
    return N_HEADS * HEAD_DIM


def _d_b():
    return N_GROUPS * BLOCK


def _d_in():
    return 3 * _d_a() + 2 * _d_b() + 2 * D_MODEL


def _params(sem=None, **kw):
    return pltpu.CompilerParams(dimension_semantics=sem, vmem_limit_bytes=VMEM_LIMIT, **kw)


def _pick(t, n, *others):
    if n <= t and all(o % n == 0 for o in others):
        return n
    for c in range(min(t, n) // LANES * LANES, 0, -LANES):
        if n % c == 0 and all(o % c == 0 for o in others):
            return c
    raise ValueError((t, n, others))


_GELU_C = math.sqrt(2.0 / math.pi)


def _gelu(x):
    return 0.5 * x * (1.0 + jnp.tanh(_GELU_C * (x + 0.044715 * x * x * x)))


def _gelu_grad(x):
    t = jnp.tanh(_GELU_C * (x + 0.044715 * x * x * x))
    return 0.5 * (1.0 + t) + 0.5 * x * (1.0 - t * t) * (_GELU_C * (1.0 + 3.0 * 0.044715 * x * x))


def _sigmoid(x):
    return 1.0 / (1.0 + jnp.exp(-x))


def _dot(a, b, mode):
    dims = {"nn": (((1,), (0,)), ((), ())), "nt": (((1,), (1,)), ((), ())), "tn": (((0,), (0,)), ((), ()))}[mode]
    return lax.dot_general(a.astype(BF16), b.astype(BF16), dims, preferred_element_type=F32)


def _matmul(name, a, b, mode, outs, epi=None, extras=(), colsums=(), tm=512, tn=512, tk=2048, b_off=0, n=None):
    if mode == "tn":
        kk, m = a.shape
    else:
        m, kk = a.shape
    n = (b.shape[0] if mode == "nt" else b.shape[1]) if n is None else n
    tm, tk = _pick(tm, m), _pick(tk, kk)
    tn = _pick(tn, n, b_off, *[off for _, _, off in extras])
    boff = b_off // tn
    nm, nn_, nk = m // tm, n // tn, kk // tk
    col_major = bool(colsums)
    grid = (nn_, nm, nk) if col_major else (nm, nn_, nk)

    def imap(f):
        if col_major:
            return lambda g0, g1, k: f(g1, g0, k)
        return f

    a_spec = pl.BlockSpec((tk, tm), imap(lambda i, j, k: (k, i))) if mode == "tn" else pl.BlockSpec((tm, tk), imap(lambda i, j, k: (i, k)))
    b_spec = (pl.BlockSpec((tn, tk), imap(lambda i, j, k: (j + boff, k))) if mode == "nt"
              else pl.BlockSpec((tk, tn), imap(lambda i, j, k: (k, j + boff))))
    in_specs, operands = [a_spec, b_spec], [a, b]
    for arr, kind, off in extras:
        o = off // tn
        if kind == "mn":
            in_specs.append(pl.BlockSpec((tm, tn), imap(lambda i, j, k, o=o: (i, j + o))))
        else:
            in_specs.append(pl.BlockSpec((1, tn), imap(lambda i, j, k, o=o: (0, j + o))))
        operands.append(arr)
    out_shape = [jax.ShapeDtypeStruct((m, n), dt) for dt in outs] + [jax.ShapeDtypeStruct((1, n), F32) for _ in colsums]
    out_specs = ([pl.BlockSpec((tm, tn), imap(lambda i, j, k: (i, j))) for _ in outs]
                 + [pl.BlockSpec((1, tn), imap(lambda i, j, k: (0, j))) for _ in colsums])
    n_ex, n_out, n_cs = len(extras), len(outs), len(colsums)

    def body(*refs):
        a_ref, b_ref = refs[:2]
        ex_refs = refs[2:2 + n_ex]
        out_refs = refs[2 + n_ex:2 + n_ex + n_out]
        cs_refs = refs[2 + n_ex + n_out:2 + n_ex + n_out + n_cs]
        part = _dot(a_ref[...], b_ref[...], mode)

        def finish(acc):
            res = epi(acc, *[r[...] for r in ex_refs]) if epi is not None else (acc,)
            for r, v in zip(out_refs, res[:n_out]):
                r[...] = v.astype(r.dtype)
            if n_cs:
                @pl.when(pl.program_id(1) == 0)
                def _():
                    for r in cs_refs:
                        r[...] = jnp.zeros_like(r)

                for r, idx in zip(cs_refs, colsums):
                    r[...] += jnp.sum(res[idx], axis=0, keepdims=True)

        if nk == 1:
            finish(part)
        else:
            acc_ref = refs[-1]
            k = pl.program_id(2)

            @pl.when(k == 0)
            def _():
                acc_ref[...] = part

            @pl.when(k > 0)
            def _():
                acc_ref[...] += part

            @pl.when(k == nk - 1)
            def _():
                finish(acc_ref[...])

    sem = ("arbitrary", "arbitrary", "arbitrary") if col_major else ("parallel", "parallel", "arbitrary")
    res = pl.pallas_call(
        body, name=name, grid=grid, in_specs=in_specs, out_specs=out_specs, out_shape=out_shape,
        scratch_shapes=[pltpu.VMEM((tm, tn), F32)] if nk > 1 else [],
        compiler_params=_params(sem),
    )(*operands)
    return res


def _row_spec(tr, c):
    return pl.BlockSpec((tr, c), lambda i: (i, 0))


def _fix_spec(shape):
    return pl.BlockSpec(shape, lambda *_: tuple(0 for _ in shape))


def _cast_bf16(name, x, tr=512):
    r, c = x.shape
    tr = _pick(tr, r)

    def body(x_ref, o_ref):
        o_ref[...] = x_ref[...].astype(BF16)

    return pl.pallas_call(
        body, name=name, grid=(r // tr,), in_specs=[_row_spec(tr, c)], out_specs=_row_spec(tr, c),
        out_shape=jax.ShapeDtypeStruct((r, c), BF16), compiler_params=_params(("parallel",)),
    )(x)


def _layer_norm_stats(x):
    mean = jnp.mean(x, axis=-1, keepdims=True)
    xc = x - mean
    var = jnp.mean(xc * xc, axis=-1, keepdims=True)
    rstd = lax.rsqrt(var + LN_EPS)
    return xc * rstd, rstd


def _layer_norm_bwd(dxhat, xhat, rstd):
    m1 = jnp.mean(dxhat, axis=-1, keepdims=True)
    m2 = jnp.mean(dxhat * xhat, axis=-1, keepdims=True)
    return rstd * (dxhat - m1 - xhat * m2)


def _ln1_fwd(pre1, g1, b1, tr=256):
    s, d = pre1.shape
    tr = _pick(tr, s)

    def body(p_ref, g_ref, b_ref, xh_ref, rs_ref, h_ref):
        xhat, rstd = _layer_norm_stats(p_ref[...])
        xh_ref[...] = xhat
        rs_ref[...] = rstd
        h_ref[...] = (xhat * g_ref[...] + b_ref[...]).astype(BF16)

    return pl.pallas_call(
        body, name="ln1_fwd", grid=(s // tr,),
        in_specs=[_row_spec(tr, d), _fix_spec((1, d)), _fix_spec((1, d))],
        out_specs=[_row_spec(tr, d), _row_spec(tr, 1), _row_spec(tr, d)],
        out_shape=[jax.ShapeDtypeStruct((s, d), F32), jax.ShapeDtypeStruct((s, 1), F32), jax.ShapeDtypeStruct((s, d), BF16)],
        compiler_params=_params(("parallel",)),
    )(pre1, g1, b1)


def _ln2_loss_bwd(ff, xhat1, g1, b1, g2, b2, target, tr=256):
    s, d = ff.shape
    tr = _pick(tr, s)

    def body(ff_ref, xh1_ref, g1_ref, b1_ref, g2_ref, b2_ref, t_ref, dp_ref, dpb_ref, dg_ref, db_ref, dbf_ref, loss_ref):
        @pl.when(pl.program_id(0) == 0)
        def _():
            dg_ref[...] = jnp.zeros_like(dg_ref)
            db_ref[...] = jnp.zeros_like(db_ref)
            dbf_ref[...] = jnp.zeros_like(dbf_ref)
            loss_ref[...] = jnp.zeros_like(loss_ref)

        h1 = xh1_ref[...] * g1_ref[...] + b1_ref[...]
        xhat, rstd = _layer_norm_stats(ALPHA * h1 + ff_ref[...])
        err = xhat * g2_ref[...] + b2_ref[...] - t_ref[...]
        row = jnp.mean(err * err, axis=-1, keepdims=True)
        loss_ref[...] += 0.5 * jnp.sum(row, axis=0, keepdims=True)
        dy = err / d
        dg_ref[...] += jnp.sum(dy * xhat, axis=0, keepdims=True)
        db_ref[...] += jnp.sum(dy, axis=0, keepdims=True)
        dpre = _layer_norm_bwd(dy * g2_ref[...], xhat, rstd)
        dbf_ref[...] += jnp.sum(dpre, axis=0, keepdims=True)
        dp_ref[...] = dpre
        dpb_ref[...] = dpre.astype(BF16)

    vec = _fix_spec((1, d))
    return pl.pallas_call(
        body, name="ln2_loss_bwd", grid=(s // tr,),
        in_specs=[_row_spec(tr, d), _row_spec(tr, d), vec, vec, vec, vec, _row_spec(tr, d)],
        out_specs=[_row_spec(tr, d), _row_spec(tr, d), vec, vec, vec, _fix_spec((1, 1))],
        out_shape=[jax.ShapeDtypeStruct((s, d), F32), jax.ShapeDtypeStruct((s, d), BF16)]
        + [jax.ShapeDtypeStruct((1, d), F32)] * 3 + [jax.ShapeDtypeStruct((1, 1), F32)],
        compiler_params=_params(("arbitrary",)),
    )(ff, xhat1, g1, b1, g2, b2, target)


def _ln1_bwd(dh1, xhat1, rstd1, g1, tr=256):
    s, d = dh1.shape
    tr = _pick(tr, s)

    def body(dh_ref, xh_ref, rs_ref, g_ref, dp_ref, dpb_ref, dg_ref, db_ref):
        @pl.when(pl.program_id(0) == 0)
        def _():
            dg_ref[...] = jnp.zeros_like(dg_ref)
            db_ref[...] = jnp.zeros_like(db_ref)

        dh, xhat = dh_ref[...], xh_ref[...]
        dg_ref[...] += jnp.sum(dh * xhat, axis=0, keepdims=True)
        db_ref[...] += jnp.sum(dh, axis=0, keepdims=True)
        dpre = _layer_norm_bwd(dh * g_ref[...], xhat, rs_ref[...])
        dp_ref[...] = dpre
        dpb_ref[...] = dpre.astype(BF16)

    vec = _fix_spec((1, d))
    return pl.pallas_call(
        body, name="ln1_bwd", grid=(s // tr,),
        in_specs=[_row_spec(tr, d), _row_spec(tr, d), _row_spec(tr, 1), vec],
        out_specs=[_row_spec(tr, d), _row_spec(tr, d), vec, vec],
        out_shape=[jax.ShapeDtypeStruct((s, d), F32), jax.ShapeDtypeStruct((s, d), BF16)] + [jax.ShapeDtypeStruct((1, d), F32)] * 2,
        compiler_params=_params(("arbitrary",)),
    )(dh1, xhat1, rstd1, g1)


def _to_perm(x):
    return x.reshape(SEQ // N_SUB, N_SUB, -1).transpose(1, 0, 2).reshape(SEQ, -1)


def _from_perm(x):
    return x.reshape(N_SUB, SEQ // N_SUB, -1).transpose(1, 0, 2).reshape(SEQ, -1)


def _local_index(p):
    rho = np.arange(BLOCK)
    if p == 0:
        return 16 * (rho % 8) + rho // 8
    if p == 1:
        return 4 * (rho % 32) + rho // 32
    return rho


def _tile_view(x, p):
    c = x.shape[1]
    if p == 1:
        return x.reshape(4, 4, BLOCK, c)
    return x.reshape(N_SUB, BLOCK, c)


def _view_shape(c, p):
    return (4, 4, BLOCK, c) if p == 1 else (N_SUB, BLOCK, c)


def _tile_spec(p, width, col, shift=0):
    nblk = SEQ // DILATIONS[p] // BLOCK

    def blk(n):
        return jnp.clip(n + shift, 0, nblk - 1)

    if p == 0:
        return pl.BlockSpec((N_SUB, SUBLANES, width), lambda s, n: (0, blk(n), col))
    if p == 1:
        return pl.BlockSpec((4, None, 32, width), lambda s, n: (0, s, blk(n), col))
    return pl.BlockSpec((None, BLOCK, width), lambda s, n: (s, 0, col))


def _tile_grid(p):
    return ((1, 16), (4, 4), (16, 1))[p]


def _t5_bucket(n):
    max_exact = N_BUCKETS // 2
    nf = np.maximum(n, 1).astype(np.float32)
    large = max_exact + (np.log(nf / np.float32(max_exact)) / np.float32(math.log(MAX_DISTANCE / max_exact))
                         * np.float32(N_BUCKETS - max_exact)).astype(np.int32)
    large = np.minimum(large, N_BUCKETS - 1)
    return np.where(n < max_exact, n, large).astype(np.int32)


def _bucket_tables():
    tabs = np.zeros((3, 2, BLOCK, BLOCK), np.int32)
    for p, d in enumerate(DILATIONS):
        i = _local_index(p)
        diff = i[:, None] - i[None, :]
        tabs[p, 0] = np.where(diff <= 0, _t5_bucket((BLOCK + diff) * d), -1)
        tabs[p, 1] = np.where(diff >= 0, _t5_bucket(np.maximum(diff, 0) * d), -1)
    return tabs


def _bias_expand(rel_bias, buckets):
    nh = N_HEADS

    def body(rb_ref, bk_ref, o_ref):
        for w in range(2):
            bk = bk_ref[0, w]
            for h in range(nh):
                val = jnp.zeros((BLOCK, BLOCK), F32)
                for b in range(N_BUCKETS):
                    val = jnp.where(bk == b, rb_ref[b, h], val)
                o_ref[0, h, w] = jnp.where(bk < 0, NEG_INF, val)

    return pl.pallas_call(
        body, name="bias_expand", grid=(3,),
        in_specs=[pl.BlockSpec(memory_space=pltpu.SMEM), pl.BlockSpec((1, 2, BLOCK, BLOCK), lambda p: (p, 0, 0, 0))],
        out_specs=pl.BlockSpec((1, nh, 2, BLOCK, BLOCK), lambda p: (p, 0, 0, 0, 0)),
        out_shape=jax.ShapeDtypeStruct((3, nh, 2, BLOCK, BLOCK), F32),
        compiler_params=_params(("parallel",)),
    )(rel_bias, buckets)


def _heads_to_lanes(cols):
    lane = lax.broadcasted_iota(I32, (BLOCK, LANES), 1)
    out = jnp.zeros((BLOCK, LANES), F32)
    for h, c in enumerate(cols):
        out = jnp.where(lane == h, c, out)
    return out


def _attn_fwd(qkv, bias, p):
    d_a = _d_a()
    has_prev = SEQ // DILATIONS[p] // BLOCK > 1
    scale = HEAD_DIM ** -0.5
    view = _tile_view(qkv, p)

    def body(q_ref, kc_ref, kp_ref, vc_ref, vp_ref, b_ref, o_ref, l_ref):
        n = pl.program_id(1)
        q_all = q_ref[...].reshape(BLOCK, d_a).astype(BF16)
        kc_all = kc_ref[...].reshape(BLOCK, d_a).astype(BF16)
        vc_all = vc_ref[...].reshape(BLOCK, d_a).astype(BF16)
        if has_prev:
            kp_all = kp_ref[...].reshape(BLOCK, d_a).astype(BF16)
            vp_all = vp_ref[...].reshape(BLOCK, d_a).astype(BF16)
        outs, lses = [], []
        for h in range(N_HEADS):
            sl = slice(h * HEAD_DIM, (h + 1) * HEAD_DIM)
            q = q_all[:, sl]
            sc = _dot(q, kc_all[:, sl], "nt") * scale + b_ref[0, h, 1]
            m = jnp.max(sc, axis=-1, keepdims=True)
            if has_prev:
                sp = _dot(q, kp_all[:, sl], "nt") * scale + b_ref[0, h, 0]
                sp = jnp.where(n > 0, sp, NEG_INF)
                m = jnp.maximum(m, jnp.max(sp, axis=-1, keepdims=True))
                pp = jnp.exp(sp - m)
            pc = jnp.exp(sc - m)
            den = jnp.sum(pc, axis=-1, keepdims=True)
            o = _dot(pc, vc_all[:, sl], "nn")
            if has_prev:
                den = den + jnp.sum(pp, axis=-1, keepdims=True)
                o = o + _dot(pp, vp_all[:, sl], "nn")
            outs.append(o / den)
            lses.append(m + jnp.log(den))
        o_ref[...] = jnp.concatenate(outs, axis=-1).reshape(o_ref.shape)
        l_ref[...] = _heads_to_lanes(lses).reshape(l_ref.shape)

    o, l = pl.pallas_call(
        body, name=f"attn_fwd{p}", grid=_tile_grid(p),
        in_specs=[_tile_spec(p, d_a, 0), _tile_spec(p, d_a, 1), _tile_spec(p, d_a, 1, -1), _tile_spec(p, d_a, 2), _tile_spec(p, d_a, 2, -1),
                  pl.BlockSpec((1, N_HEADS, 2, BLOCK, BLOCK), lambda s, n: (p, 0, 0, 0, 0))],
        out_specs=[_tile_spec(p, d_a, 0), _tile_spec(p, LANES, 0)],
        out_shape=[jax.ShapeDtypeStruct(_view_shape(d_a, p), F32), jax.ShapeDtypeStruct(_view_shape(LANES, p), F32)],
        compiler_params=_params(("parallel", "parallel")),
    )(view, view, view, view, view, bias)
    return o.reshape(SEQ, d_a), l.reshape(SEQ, LANES)


def _attn_combine(os_, ls_, tr=256):
    d_a = _d_a()
    tr = _pick(tr, SEQ)

    def body(o0, o1, o2, l0, l1, l2, a_ref, ab_ref, lt_ref):
        l = [l0[...], l1[...], l2[...]]
        m = jnp.maximum(jnp.maximum(l[0], l[1]), l[2])
        w = [jnp.exp(x - m) for x in l]
        tot = w[0] + w[1] + w[2]
        lt_ref[...] = m + jnp.log(tot)
        w = [x / tot for x in w]
        for h in range(N_HEADS):
            sl = slice(h * HEAD_DIM, (h + 1) * HEAD_DIM)
            acc = w[0][:, h:h + 1] * o0[:, sl] + w[1][:, h:h + 1] * o1[:, sl] + w[2][:, h:h + 1] * o2[:, sl]
            a_ref[:, sl] = acc
            ab_ref[:, sl] = acc.astype(BF16)

    return pl.pallas_call(
        body, name="attn_combine", grid=(SEQ // tr,),
        in_specs=[_row_spec(tr, d_a)] * 3 + [_row_spec(tr, LANES)] * 3,
        out_specs=[_row_spec(tr, d_a), _row_spec(tr, d_a), _row_spec(tr, LANES)],
        out_shape=[jax.ShapeDtypeStruct((SEQ, d_a), F32), jax.ShapeDtypeStruct((SEQ, d_a), BF16), jax.ShapeDtypeStruct((SEQ, LANES), F32)],
        compiler_params=_params(("parallel",)),
    )(*os_, *ls_)


def _attn_delta(dattn, attn, tr=256):
    d_a = _d_a()
    tr = _pick(tr, SEQ)

    def body(d_ref, a_ref, o_ref):
        prod = d_ref[...] * a_ref[...]
        lane = lax.broadcasted_iota(I32, (tr, LANES), 1)
        out = jnp.zeros((tr, LANES), F32)
        for h in range(N_HEADS):
            out = jnp.where(lane == h, jnp.sum(prod[:, h * HEAD_DIM:(h + 1) * HEAD_DIM], axis=-1, keepdims=True), out)
        o_ref[...] = out

    return pl.pallas_call(
        body, name="attn_delta", grid=(SEQ // tr,), in_specs=[_row_spec(tr, d_a)] * 2, out_specs=_row_spec(tr, LANES),
        out_shape=jax.ShapeDtypeStruct((SEQ, LANES), F32), compiler_params=_params(("parallel",)),
    )(dattn, attn)


def _attn_bwd(qkv, dattn, lse, delta, bias, p):
    d_a = _d_a()
    nblk = SEQ // DILATIONS[p] // BLOCK
    has_next = nblk > 1
    scale = HEAD_DIM ** -0.5
    qv, dov, lv, tv = (_tile_view(x, p) for x in (qkv, dattn, lse, delta))

    def body(q_ref, qn_ref, k_ref, v_ref, do_ref, don_ref, l_ref, ln_ref, t_ref, tn_ref, b_ref, dq_ref, dk_ref, dv_ref, db_ref, carry_ref):
        j = pl.program_id(1)

        @pl.when((pl.program_id(0) == 0) & (j == 0))
        def _():
            db_ref[...] = jnp.zeros_like(db_ref)

        k_all = k_ref[...].reshape(BLOCK, d_a).astype(BF16)
        v_all = v_ref[...].reshape(BLOCK, d_a).astype(BF16)

        def side(qr, dor, lr, tr_, w):
            q_all = qr[...].reshape(BLOCK, d_a).astype(BF16)
            do_all = dor[...].reshape(BLOCK, d_a).astype(BF16)
            l_all = lr[...].reshape(BLOCK, LANES)
            t_all = tr_[...].reshape(BLOCK, LANES)
            dqs, dks, dvs = [], [], []
            for h in range(N_HEADS):
                sl = slice(h * HEAD_DIM, (h + 1) * HEAD_DIM)
                s = _dot(q_all[:, sl], k_all[:, sl], "nt") * scale + b_ref[0, h, w]
                pr = jnp.exp(s - l_all[:, h:h + 1])
                dp = _dot(do_all[:, sl], v_all[:, sl], "nt")
                ds = pr * (dp - t_all[:, h:h + 1])
                db_ref[h, w] += ds
                dqs.append(_dot(ds, k_all[:, sl], "nn") * scale)
                dks.append(_dot(ds, q_all[:, sl], "tn") * scale)
                dvs.append(_dot(pr, do_all[:, sl], "tn"))
            return [jnp.concatenate(x, axis=-1) for x in (dqs, dks, dvs)]

        dq_c, dk_c, dv_c = side(q_ref, do_ref, l_ref, t_ref, 1)
        if has_next:
            dq_ref[...] = (jnp.where(j > 0, carry_ref[...], 0.0) + dq_c).reshape(dq_ref.shape)
            not_last = j < nblk - 1

            @pl.when(not_last)
            def _():
                dq_n, dk_n, dv_n = side(qn_ref, don_ref, ln_ref, tn_ref, 0)
                carry_ref[...] = dq_n
                dk_ref[...] = (dk_c + dk_n).reshape(dk_ref.shape)
                dv_ref[...] = (dv_c + dv_n).reshape(dv_ref.shape)

            @pl.when(jnp.logical_not(not_last))
            def _():
                dk_ref[...] = dk_c.reshape(dk_ref.shape)
                dv_ref[...] = dv_c.reshape(dv_ref.shape)
        else:
            dq_ref[...] = dq_c.reshape(dq_ref.shape)
            dk_ref[...] = dk_c.reshape(dk_ref.shape)
            dv_ref[...] = dv_c.reshape(dv_ref.shape)

    def big(col, shift=0):
        return _tile_spec(p, d_a, col, shift)

    def small(shift=0):
        return _tile_spec(p, LANES, 0, shift)

    dq, dk, dv, dbias = pl.pallas_call(
        body, name=f"attn_bwd{p}", grid=_tile_grid(p),
        in_specs=[big(0), big(0, 1), big(1), big(2), big(0), big(0, 1), small(), small(1), small(), small(1),
                  pl.BlockSpec((1, N_HEADS, 2, BLOCK, BLOCK), lambda s, n: (p, 0, 0, 0, 0))],
        out_specs=[big(0), big(0), big(0), pl.BlockSpec((N_HEADS, 2, BLOCK, BLOCK), lambda s, n: (0, 0, 0, 0))],
        out_shape=[jax.ShapeDtypeStruct(_view_shape(d_a, p), F32)] * 3 + [jax.ShapeDtypeStruct((N_HEADS, 2, BLOCK, BLOCK), F32)],
        scratch_shapes=[pltpu.VMEM((BLOCK, d_a), F32)],
        compiler_params=_params(("arbitrary", "arbitrary")),
    )(qv, qv, qv, qv, dov, dov, lv, lv, tv, tv, bias)
    return dq.reshape(SEQ, d_a), dk.reshape(SEQ, d_a), dv.reshape(SEQ, d_a), dbias


def _rel_bias_grad(dbias, buckets):
    nh = N_HEADS

    def body(d0, d1, d2, bk_ref, o_ref, t_ref):
        ds = (d0, d1, d2)

        def per_bucket(b, carry):
            for h in range(nh):
                acc = jnp.zeros((BLOCK, BLOCK), F32)
                for p in range(3):
                    for w in range(2):
                        acc = acc + jnp.where(bk_ref[p, w] == b, ds[p][h, w], 0.0)
                t_ref[pl.ds(b * nh + h, 1), :] = jnp.sum(acc, axis=0, keepdims=True)
            return carry

        lax.fori_loop(0, N_BUCKETS, per_bucket, 0)
        o_ref[...] = jnp.sum(t_ref[...], axis=-1, keepdims=True)

    return pl.pallas_call(
        body, name="rel_bias_grad", grid=(1,),
        in_specs=[_fix_spec((nh, 2, BLOCK, BLOCK))] * 3 + [_fix_spec((3, 2, BLOCK, BLOCK))],
        out_specs=_fix_spec((N_BUCKETS * nh, 1)), out_shape=jax.ShapeDtypeStruct((N_BUCKETS * nh, 1), F32),
        scratch_shapes=[pltpu.VMEM((N_BUCKETS * nh, LANES), F32)],
        compiler_params=_params(("arbitrary",)),
    )(*dbias, buckets)


def _gmlp_fwd(rest, gain, bias, ws, bs, causal):
    d_b = _d_b()

    def body(u_ref, v_ref, g_ref, b_ref, ws_ref, bs_ref, c_ref, o_ref):
        u = u_ref[...].reshape(BLOCK, d_b)
        xhat, _ = _layer_norm_stats(_gelu(v_ref[...].reshape(BLOCK, d_b)))
        vn = (xhat * g_ref[...] + b_ref[...]).astype(BF16)
        outs = []
        for g in range(N_GROUPS):
            sl = slice(g * BLOCK, (g + 1) * BLOCK)
            w = jnp.where(c_ref[...] > 0, ws_ref[g], 0.0)
            z = _dot(w, vn[:, sl], "nn") + bs_ref[:, g:g + 1]
            outs.append(_gelu(u[:, sl]) * z)
        o_ref[...] = jnp.concatenate(outs, axis=-1).reshape(o_ref.shape)

    out = pl.pallas_call(
        body, name="gmlp_fwd", grid=(1, SEQ // BLOCK),
        in_specs=[_tile_spec(0, d_b, 0), _tile_spec(0, d_b, 1), _fix_spec((1, d_b)), _fix_spec((1, d_b)),
                  _fix_spec((N_GROUPS, BLOCK, BLOCK)), _fix_spec((BLOCK, N_GROUPS)), _fix_spec((BLOCK, BLOCK))],
        out_specs=_tile_spec(0, d_b, 0), out_shape=jax.ShapeDtypeStruct(_view_shape(d_b, 0), F32),
        compiler_params=_params(("parallel", "parallel")),
    )(_tile_view(rest, 0), _tile_view(rest, 0), gain, bias, ws, bs, causal)
    return out.reshape(SEQ, d_b)


def _gmlp_bwd(rest, dgmlp, gain, bias, ws, bs, causal):
    d_b = _d_b()
    nchunk = SEQ // BLOCK

    def body(u_ref, v_ref, dg_ref, g_ref, b_ref, ws_ref, bs_ref, c_ref, du_ref, dv_ref, dws_ref, dbs_ref, dgain_ref, dbias_ref):
        c = pl.program_id(1)

        @pl.when(c == 0)
        def _():
            dws_ref[...] = jnp.zeros_like(dws_ref)
            dbs_ref[...] = jnp.zeros_like(dbs_ref)
            dgain_ref[...] = jnp.zeros_like(dgain_ref)
            dbias_ref[...] = jnp.zeros_like(dbias_ref)

        u = u_ref[...].reshape(BLOCK, d_b)
        v = v_ref[...].reshape(BLOCK, d_b)
        dgm = dg_ref[...].reshape(BLOCK, d_b)
        xhat, rstd = _layer_norm_stats(_gelu(v))
        vn = (xhat * g_ref[...] + b_ref[...]).astype(BF16)
        lane = lax.broadcasted_iota(I32, (BLOCK, LANES), 1)
        dus, dvns = [], []
        dbs = dbs_ref[...]
        for g in range(N_GROUPS):
            sl = slice(g * BLOCK, (g + 1) * BLOCK)
            w = jnp.where(c_ref[...] > 0, ws_ref[g], 0.0).astype(BF16)
            z = _dot(w, vn[:, sl], "nn") + bs_ref[:, g:g + 1]
            dz = dgm[:, sl] * _gelu(u[:, sl])
            dus.append(dgm[:, sl] * z * _gelu_grad(u[:, sl]))
            dws_ref[g] += _dot(dz, vn[:, sl], "nt")
            dbs = dbs + jnp.where(lane == g, jnp.sum(dz, axis=-1, keepdims=True), 0.0)
            dvns.append(_dot(w, dz, "tn"))
        dbs_ref[...] = dbs
        dvn = jnp.concatenate(dvns, axis=-1)
        dgain_ref[...] += jnp.sum(dvn * xhat, axis=0, keepdims=True)
        dbias_ref[...] += jnp.sum(dvn, axis=0, keepdims=True)
        dvg = _layer_norm_bwd(dvn * g_ref[...], xhat, rstd)
        du_ref[...] = jnp.concatenate(dus, axis=-1).reshape(du_ref.shape)
        dv_ref[...] = (dvg * _gelu_grad(v)).reshape(dv_ref.shape)

        @pl.when(c == nchunk - 1)
        def _():
            for g in range(N_GROUPS):
                dws_ref[g] = jnp.where(c_ref[...] > 0, dws_ref[g], 0.0)

    du, dv, dws, dbs, dgain, dbias = pl.pallas_call(
        body, name="gmlp_bwd", grid=(1, nchunk),
        in_specs=[_tile_spec(0, d_b, 0), _tile_spec(0, d_b, 1), _tile_spec(0, d_b, 0), _fix_spec((1, d_b)), _fix_spec((1, d_b)),
                  _fix_spec((N_GROUPS, BLOCK, BLOCK)), _fix_spec((BLOCK, N_GROUPS)), _fix_spec((BLOCK, BLOCK))],
        out_specs=[_tile_spec(0, d_b, 0), _tile_spec(0, d_b, 0), _fix_spec((N_GROUPS, BLOCK, BLOCK)), _fix_spec((BLOCK, LANES)),
                   _fix_spec((1, d_b)), _fix_spec((1, d_b))],
        out_shape=[jax.ShapeDtypeStruct(_view_shape(d_b, 0), F32)] * 2
        + [jax.ShapeDtypeStruct((N_GROUPS, BLOCK, BLOCK), F32), jax.ShapeDtypeStruct((BLOCK, LANES), F32)]
        + [jax.ShapeDtypeStruct((1, d_b), F32)] * 2,
        compiler_params=_params(("arbitrary", "arbitrary")),
    )(_tile_view(rest, 0), _tile_view(rest, 0), _tile_view(dgmlp, 0), gain, bias, ws, bs, causal)
    return du.reshape(SEQ, d_b), dv.reshape(SEQ, d_b), dws, dbs, dgain, dbias


def _assemble_dproj(dqkv, du, dv, dga, dgb, tr=128):
    d_a, d_b, d_in = _d_a(), _d_b(), _d_in()
    tr = _pick(tr, SEQ)

    def body(*refs):
        att, (du_ref, dv_ref, dga_ref, dgb_ref, o_ref) = refs[:9], refs[9:]
        for i in range(3):
            o_ref[:, i * d_a:(i + 1) * d_a] = (att[3 * i][...] + att[3 * i + 1][...] + att[3 * i + 2][...]).astype(BF16)
        o_ref[:, 3 * d_a:3 * d_a + d_b] = du_ref[...].astype(BF16)
        o_ref[:, 3 * d_a + d_b:3 * d_a + 2 * d_b] = dv_ref[...].astype(BF16)
        o_ref[:, 3 * d_a + 2 * d_b:3 * d_a + 2 * d_b + D_MODEL] = dga_ref[...]
        o_ref[:, 3 * d_a + 2 * d_b + D_MODEL:] = dgb_ref[...]

    return pl.pallas_call(
        body, name="assemble_dproj", grid=(SEQ // tr,),
        in_specs=[_row_spec(tr, d_a)] * 9 + [_row_spec(tr, d_b)] * 2 + [_row_spec(tr, D_MODEL)] * 2,
        out_specs=_row_spec(tr, d_in), out_shape=jax.ShapeDtypeStruct((SEQ, d_in), BF16),
        compiler_params=_params(("parallel",)),
    )(*dqkv, du, dv, dga, dgb)


def _coords():
    return lax.axis_index("x"), lax.axis_index("y"), lax.axis_index("c")


def _other_chips(x, y):
    return ((1 - x, y), (x, 1 - y), (1 - x, 1 - y))


def _shard(ref, kind, j, n):
    if kind == "col":
        return ref.at[:, pl.ds(pl.multiple_of(j * n, LANES), n)]
    return ref.at[pl.ds(pl.multiple_of(j * n, 2 * SUBLANES), n), :]


def _shard_size(full_shape, kind):
    return (full_shape[1] if kind == "col" else full_shape[0]) // N_DEV


def _full_shape(shard_shape, kind):
    r, c = shard_shape
    return (r, c * N_DEV) if kind == "col" else (r * N_DEV, c)


def _all_gather(shards):
    nw = len(shards)
    kinds = [k for _, k in BIG]
    sizes = [s.shape[1] if k == "col" else s.shape[0] for s, k in zip(shards, kinds)]

    def body(*refs):
        ins, outs = refs[:nw], refs[nw:2 * nw]
        send_sems, recv_sems, local_sems = refs[2 * nw:]
        x, y, c = _coords()
        me, sibling = (x, y, c), (x, y, 1 - c)
        chips = _other_chips(x, y)

        def place(w, dev):
            px, py, pc = dev
            return _shard(outs[w], kinds[w], 4 * px + 2 * py + pc, sizes[w])

        def copy(w, k, block, to, src=None):
            return pltpu.make_async_remote_copy(
                src_ref=place(w, block) if src is None else src, dst_ref=place(w, block),
                send_sem=send_sems.at[w, k], recv_sem=recv_sems.at[w, k], device_id=to, device_id_type=MESH)

        own, sent = [], []
        for w in range(nw):
            mine = pltpu.make_async_copy(ins[w], place(w, me), local_sems.at[w])
            mine.start()
            own.append(mine)
            first = [copy(w, 0, me, sibling, src=ins[w])] + [copy(w, 1 + j, me, (*chip, c), src=ins[w]) for j, chip in enumerate(chips)]
            for cp in first:
                cp.start()
            sent += first
        for w in range(nw):
            for j, chip in enumerate(chips):
                copy(w, 1 + j, (*chip, c), me).wait_recv()
                fwd = copy(w, 4 + j, (*chip, c), sibling)
                fwd.start()
                sent.append(fwd)
        for w in range(nw):
            copy(w, 0, sibling, me).wait_recv()
            for j, chip in enumerate(chips):
                copy(w, 4 + j, (*chip, 1 - c), me).wait_recv()
        for cp in sent:
            cp.wait_send()
        for cp in own:
            cp.wait()

    any_spec = pl.BlockSpec(memory_space=pl.ANY)
    return pl.pallas_call(
        body, name="all_gather", in_specs=[any_spec] * nw, out_specs=[any_spec] * nw,
        out_shape=[jax.ShapeDtypeStruct(_full_shape(s.shape, k), BF16) for s, k in zip(shards, kinds)],
        scratch_shapes=[pltpu.SemaphoreType.DMA((nw, 7)), pltpu.SemaphoreType.DMA((nw, 7)), pltpu.SemaphoreType.DMA((nw,))],
        compiler_params=_params(),
    )(*shards)


def _exchange_sibling(grads):
    nw = len(grads)
    kinds = [k for _, k in BIG]
    sizes = [_shard_size(g.shape, k) for g, k in zip(grads, kinds)]

    def body(*refs):
        ins, outs = refs[:nw], refs[nw:2 * nw]
        send_sems, recv_sems = refs[2 * nw:]
        x, y, c = _coords()
        copies = []
        for w in range(nw):
            for q in range(N_CHIPS):
                copies.append(pltpu.make_async_remote_copy(
                    src_ref=_shard(ins[w], kinds[w], 2 * q + (1 - c), sizes[w]), dst_ref=outs[w].at[q],
                    send_sem=send_sems.at[w, q], recv_sem=recv_sems.at[w, q], device_id=(x, y, 1 - c), device_id_type=MESH))
        for cp in copies:
            cp.start()
        for cp in copies:
            cp.wait()

    def recv_shape(g, kind, n):
        return (N_CHIPS, g.shape[0], n) if kind == "col" else (N_CHIPS, n, g.shape[1])

    any_spec = pl.BlockSpec(memory_space=pl.ANY)
    return pl.pallas_call(
        body, name="exchange_sibling", in_specs=[any_spec] * nw, out_specs=[any_spec] * nw,
        out_shape=[jax.ShapeDtypeStruct(recv_shape(g, k, n), BF16) for g, k, n in zip(grads, kinds, sizes)],
        scratch_shapes=[pltpu.SemaphoreType.DMA((nw, N_CHIPS)), pltpu.SemaphoreType.DMA((nw, N_CHIPS))],
        compiler_params=_params(),
    )(*grads)


def _pair_sum(name, grad, recv, kind, core, tr=256):
    _, rs, cs = recv.shape
    tr = _pick(tr, rs)
    nb = rs // tr
    if kind == "col":
        g_spec = pl.BlockSpec((tr, cs), lambda q, i, c_ref: (i, 2 * q + c_ref[0]))
    else:
        g_spec = pl.BlockSpec((tr, cs), lambda q, i, c_ref: ((2 * q + c_ref[0]) * nb + i, 0))
    r_spec = pl.BlockSpec((None, tr, cs), lambda q, i, c_ref: (q, i, 0))

    def body(c_ref, g_ref, r_ref, o_ref):
        o_ref[...] = (g_ref[...].astype(F32) + r_ref[...].astype(F32)).astype(BF16)

    return pl.pallas_call(
        body, name=name, out_shape=jax.ShapeDtypeStruct(recv.shape, BF16),
        grid_spec=pltpu.PrefetchScalarGridSpec(num_scalar_prefetch=1, grid=(N_CHIPS, nb), in_specs=[g_spec, r_spec], out_specs=r_spec),
        compiler_params=_params(("parallel", "parallel")),
    )(core, grad, recv)


def _exchange_chips(sums):
    nw = len(sums)

    def body(*refs):
        ins, outs = refs[:nw], refs[nw:2 * nw]
        send_sems, recv_sems, local_sems = refs[2 * nw:]
        x, y, c = _coords()
        mine = 2 * x + y
        own, sent = [], []
        for w in range(nw):
            cp = pltpu.make_async_copy(ins[w].at[mine], outs[w].at[mine], local_sems.at[w])
            cp.start()
            own.append(cp)
            for k, (px, py) in enumerate(_other_chips(x, y)):
                rc = pltpu.make_async_remote_copy(
                    src_ref=ins[w].at[2 * px + py], dst_ref=outs[w].at[mine],
                    send_sem=send_sems.at[w, k], recv_sem=recv_sems.at[w, k], device_id=(px, py, c), device_id_type=MESH)
                rc.start()
                sent.append(rc)
        for w in range(nw):
            for k, (px, py) in enumerate(_other_chips(x, y)):
                pltpu.make_async_remote_copy(
                    src_ref=ins[w].at[2 * px + py], dst_ref=outs[w].at[2 * px + py],
                    send_sem=send_sems.at[w, k], recv_sem=recv_sems.at[w, k], device_id=(px, py, c), device_id_type=MESH).wait_recv()
        for cp in sent:
            cp.wait_send()
        for cp in own:
            cp.wait()

    any_spec = pl.BlockSpec(memory_space=pl.ANY)
    return pl.pallas_call(
        body, name="exchange_chips", in_specs=[any_spec] * nw, out_specs=[any_spec] * nw,
        out_shape=[jax.ShapeDtypeStruct(s.shape, BF16) for s in sums],
        scratch_shapes=[pltpu.SemaphoreType.DMA((nw, 3)), pltpu.SemaphoreType.DMA((nw, 3)), pltpu.SemaphoreType.DMA((nw,))],
        compiler_params=_params(),
    )(*sums)


def _adamw(w, g, m, v):
    m = ADAM_B1 * m + (1.0 - ADAM_B1) * g
    v = ADAM_B2 * v + (1.0 - ADAM_B2) * (g * g)
    m_hat = m / (1.0 - ADAM_B1 ** ADAM_STEP)
    v_hat = v / (1.0 - ADAM_B2 ** ADAM_STEP)
    delta = -ADAM_LR * (m_hat / (jnp.sqrt(v_hat) + ADAM_EPS) + ADAM_WD * w)
    return delta, m, v


def _adam_shard(name, chip_sums, w, m, v, tr=256):
    rs, cs = w.shape
    tr = _pick(tr, rs)

    def body(s_ref, w_ref, m_ref, v_ref, g_ref, d_ref, nm_ref, nv_ref):
        g = s_ref[0].astype(F32)
        for q in range(1, N_CHIPS):
            g = g + s_ref[q].astype(F32)
        d, nm, nv = _adamw(w_ref[...], g, m_ref[...], v_ref[...])
        g_ref[...], d_ref[...], nm_ref[...], nv_ref[...] = g, d, nm, nv

    spec = _row_spec(tr, cs)
    return pl.pallas_call(
        body, name=name, grid=(rs // tr,),
        in_specs=[pl.BlockSpec((N_CHIPS, tr, cs), lambda i: (0, i, 0)), spec, spec, spec], out_specs=[spec] * 4,
        out_shape=[jax.ShapeDtypeStruct((rs, cs), F32)] * 4, compiler_params=_params(("parallel",)),
    )(chip_sums, w, m, v)


def _gather_small(part):
    rows = part.shape[0]

    def body(in_ref, out_ref, send_sems, recv_sems, local_sem):
        x, y, c = _coords()
        mine = 4 * x + 2 * y + c
        own = pltpu.make_async_copy(in_ref, out_ref.at[mine], local_sem)
        own.start()
        sends = []
        for k in range(1, N_DEV):
            px = 1 - x if k & 4 else x
            py = 1 - y if k & 2 else y
            pc = 1 - c if k & 1 else c
            cp = pltpu.make_async_remote_copy(
                src_ref=in_ref, dst_ref=out_ref.at[mine], send_sem=send_sems.at[k - 1], recv_sem=recv_sems.at[k - 1],
                device_id=(px, py, pc), device_id_type=MESH)
            cp.start()
            sends.append((cp, 4 * px + 2 * py + pc))
        for k, (cp, peer) in enumerate(sends):
            pltpu.make_async_remote_copy(
                src_ref=in_ref, dst_ref=out_ref.at[peer], send_sem=send_sems.at[k], recv_sem=recv_sems.at[k],
                device_id=(x, y, c), device_id_type=MESH).wait_recv()
        for cp, _ in sends:
            cp.wait_send()
        own.wait()

    any_spec = pl.BlockSpec(memory_space=pl.ANY)
    return pl.pallas_call(
        body, name="gather_small", in_specs=[any_spec], out_specs=any_spec,
        out_shape=jax.ShapeDtypeStruct((N_DEV, rows, LANES), F32),
        scratch_shapes=[pltpu.SemaphoreType.DMA((N_DEV - 1,)), pltpu.SemaphoreType.DMA((N_DEV - 1,)), pltpu.SemaphoreType.DMA],
        compiler_params=_params(),
    )(part)


def _adam_small(parts, w, m, v):
    rows = w.shape[0]

    def body(p_ref, w_ref, m_ref, v_ref, g_ref, d_ref, nm_ref, nv_ref):
        g = p_ref[0]
        for j in range(1, N_DEV):
            g = g + p_ref[j]
        d, nm, nv = _adamw(w_ref[...], g, m_ref[...], v_ref[...])
        g_ref[...], d_ref[...], nm_ref[...], nv_ref[...] = g, d, nm, nv

    spec = _fix_spec((rows, LANES))
    return pl.pallas_call(
        body, name="adam_small", grid=(1,), in_specs=[_fix_spec((N_DEV, rows, LANES)), spec, spec, spec], out_specs=[spec] * 4,
        out_shape=[jax.ShapeDtypeStruct((rows, LANES), F32)] * 4, compiler_params=_params(("arbitrary",)),
    )(parts, w, m, v)


def _small_sizes():
    d_b = _d_b()
    return (("loss", 1), ("rel_bias", N_BUCKETS * N_HEADS), ("ln_v_gain", d_b), ("ln_v_bias", d_b),
            ("w_spatial", N_GROUPS * BLOCK * BLOCK), ("b_spatial", N_GROUPS * BLOCK), ("ln1_gain", D_MODEL), ("ln1_bias", D_MODEL),
            ("b_ff1", D_FF), ("b_ff2", D_MODEL), ("ln2_gain", D_MODEL), ("ln2_bias", D_MODEL))


def _pack(vals):
    pieces = []
    for name, size in _small_sizes():
        flat = vals[name].reshape(-1).astype(F32)
        padded = -(-size // (SUBLANES * LANES)) * SUBLANES * LANES
        pieces.append(jnp.pad(flat, (0, padded - size)).reshape(-1, LANES))
    return jnp.concatenate(pieces, axis=0)


def _unpack(buf):
    out, row = {}, 0
    for name, size in _small_sizes():
        rows = -(-size // (SUBLANES * LANES)) * SUBLANES
        out[name] = buf[row:row + rows].reshape(-1)[:size]
        row += rows
    return out


def kernel(x, w_in, rel_bias, ln_v_gain, ln_v_bias, w_spatial, b_spatial, w_proj_a, w_proj_b, w_out, ln1_gain, ln1_bias, w_ff1, b_ff1, w_ff2, b_ff2, ln2_gain, ln2_bias, loss_target, m_w_in, m_rel_bias, m_ln_v_gain, m_ln_v_bias, m_w_spatial, m_b_spatial, m_w_proj_a, m_w_proj_b, m_w_out, m_ln1_gain, m_ln1_bias, m_w_ff1, m_b_ff1, m_w_ff2, m_b_ff2, m_ln2_gain, m_ln2_bias, v_w_in, v_rel_bias, v_ln_v_gain, v_ln_v_bias, v_w_spatial, v_b_spatial, v_w_proj_a, v_w_proj_b, v_w_out, v_ln1_gain, v_ln1_bias, v_w_ff1, v_b_ff1, v_w_ff2, v_b_ff2, v_ln2_gain, v_ln2_bias):
    d_a, d_b, d_in = _d_a(), _d_b(), _d_in()
    weights = dict(w_in=w_in, rel_bias=rel_bias, ln_v_gain=ln_v_gain, ln_v_bias=ln_v_bias, w_spatial=w_spatial, b_spatial=b_spatial,
                   w_proj_a=w_proj_a, w_proj_b=w_proj_b, w_out=w_out, ln1_gain=ln1_gain, ln1_bias=ln1_bias, w_ff1=w_ff1, b_ff1=b_ff1,
                   w_ff2=w_ff2, b_ff2=b_ff2, ln2_gain=ln2_gain, ln2_bias=ln2_bias)
    mom1 = dict(w_in=m_w_in, rel_bias=m_rel_bias, ln_v_gain=m_ln_v_gain, ln_v_bias=m_ln_v_bias, w_spatial=m_w_spatial,
                b_spatial=m_b_spatial, w_proj_a=m_w_proj_a, w_proj_b=m_w_proj_b, w_out=m_w_out, ln1_gain=m_ln1_gain,
                ln1_bias=m_ln1_bias, w_ff1=m_w_ff1, b_ff1=m_b_ff1, w_ff2=m_w_ff2, b_ff2=m_b_ff2, ln2_gain=m_ln2_gain, ln2_bias=m_ln2_bias)
    mom2 = dict(w_in=v_w_in, rel_bias=v_rel_bias, ln_v_gain=v_ln_v_gain, ln_v_bias=v_ln_v_bias, w_spatial=v_w_spatial,
                b_spatial=v_b_spatial, w_proj_a=v_w_proj_a, w_proj_b=v_w_proj_b, w_out=v_w_out, ln1_gain=v_ln1_gain,
                ln1_bias=v_ln1_bias, w_ff1=v_w_ff1, b_ff1=v_b_ff1, w_ff2=v_w_ff2, b_ff2=v_b_ff2, ln2_gain=v_ln2_gain, ln2_bias=v_ln2_bias)
    big_names = [n for n, _ in BIG]
    kinds = dict(BIG)

    shards = [_cast_bf16(f"cast_{n}", weights[n][0]) for n in big_names]
    win, wpa, wpb, wout, w1, w2 = _all_gather(shards)

    xs = _to_perm(x[0])
    target = _to_perm(loss_target[0])
    xb = _cast_bf16("cast_x", xs)

    g8 = BLOCK // N_SUB
    ws_t = w_spatial[0].reshape(N_GROUPS, g8, N_SUB, g8, N_SUB).transpose(0, 2, 1, 4, 3).reshape(N_GROUPS, BLOCK, BLOCK)
    bs_t = b_spatial[0].reshape(N_GROUPS, g8, N_SUB).transpose(2, 1, 0).reshape(BLOCK, N_GROUPS)
    idx = _local_index(0)
    causal = jnp.asarray((idx[:, None] >= idx[None, :]).astype(np.float32))
    buckets = jnp.asarray(_bucket_tables())
    bias = _bias_expand(rel_bias, buckets)

    qkv, = _matmul("proj_qkv", xb, win, "nn", [F32], n=3 * d_a)
    rest, = _matmul("proj_rest", xb, win, "nn", [F32], b_off=3 * d_a, n=d_in - 3 * d_a)
    fwd = [_attn_fwd(qkv, bias, p) for p in range(3)]
    attn, attn_b, lse = _attn_combine([o for o, _ in fwd], [l for _, l in fwd])
    gmlp = _gmlp_fwd(rest, ln_v_gain, ln_v_bias, ws_t, bs_t, causal)
    ya, = _matmul("proj_a", attn_b, wpa, "nn", [F32])
    gate_a, gate_b = 2 * d_b, 2 * d_b + D_MODEL

    def merge(acc, ya_, ga, gb):
        return acc, _sigmoid(ga) * ya_ + _sigmoid(gb) * acc

    yb, merged = _matmul("proj_b_merge", gmlp, wpb, "nn", [F32, BF16], merge, [(ya, "mn", 0), (rest, "mn", gate_a), (rest, "mn", gate_b)])
    pre1, = _matmul("out_proj", merged, wout, "nn", [F32], lambda acc, x_: (ALPHA * x_ + acc,), [(xs, "mn", 0)])
    xhat1, rstd1, h1b = _ln1_fwd(pre1, ln1_gain, ln1_bias)

    def relu2(acc, b_):
        r = jnp.maximum(acc + b_, 0.0)
        return r, r * r

    relu, fb = _matmul("ff1", h1b, w1, "nn", [F32, BF16], relu2, [(b_ff1, "row", 0)])
    ff, = _matmul("ff2", fb, w2, "nn", [F32], lambda acc, b_: (acc + b_,), [(b_ff2, "row", 0)])

    dpre2, dpre2b, g_ln2_gain, g_ln2_bias, g_b_ff2, loss_part = _ln2_loss_bwd(ff, xhat1, ln1_gain, ln1_bias, ln2_gain, ln2_bias, target)
    grads = {}
    grads["w_ff2"], = _matmul("dw_ff2", fb, dpre2b, "tn", [BF16])

    def relu2_bwd(acc, r):
        da = acc * (2.0 * r)
        return da, da

    dab, g_b_ff1 = _matmul("d_ff1", dpre2b, w2, "nt", [BF16], relu2_bwd, [(relu, "mn", 0)], colsums=(1,))
    grads["w_ff1"], = _matmul("dw_ff1", h1b, dab, "tn", [BF16])
    dh1, = _matmul("d_h1", dab, w1, "nt", [F32], lambda acc, d_: (acc + ALPHA * d_,), [(dpre2, "mn", 0)])
    dpre1, dpre1b, g_ln1_gain, g_ln1_bias = _ln1_bwd(dh1, xhat1, rstd1, ln1_gain)
    grads["w_out"], = _matmul("dw_out", merged, dpre1b, "tn", [BF16])

    def merge_bwd(acc, ga, gb, ya_, yb_):
        sa, sb = _sigmoid(ga), _sigmoid(gb)
        return acc * sa, acc * sb, acc * ya_ * (sa * (1.0 - sa)), acc * yb_ * (sb * (1.0 - sb))

    dya, dyb, dga, dgb = _matmul("d_merge", dpre1b, wout, "nt", [BF16] * 4, merge_bwd,
                                 [(rest, "mn", gate_a), (rest, "mn", gate_b), (ya, "mn", 0), (yb, "mn", 0)])
    grads["w_proj_a"], = _matmul("dw_proj_a", attn_b, dya, "tn", [BF16])
    grads["w_proj_b"], = _matmul("dw_proj_b", gmlp, dyb, "tn", [BF16])
    dattn, = _matmul("d_attn", dya, wpa, "nt", [F32])
    dgmlp, = _matmul("d_gmlp", dyb, wpb, "nt", [F32])
    du, dvb, dws_t, dbs_t, g_lnv_gain, g_lnv_bias = _gmlp_bwd(rest, dgmlp, ln_v_gain, ln_v_bias, ws_t, bs_t, causal)
    delta = _attn_delta(dattn, attn)
    bwd = [_attn_bwd(qkv, dattn, lse, delta, bias, p) for p in range(3)]
    g_rel_bias = _rel_bias_grad([b[3] for b in bwd], buckets)
    dproj = _assemble_dproj([b[i] for i in range(3) for b in bwd], du, dvb, dga, dgb)
    grads["w_in"], = _matmul("dw_in", xb, dproj, "tn", [BF16])
    dx, = _matmul("d_x", dproj, win, "nt", [F32], lambda acc, d_: (acc + ALPHA * d_,), [(dpre1, "mn", 0)], tk=1024)
    grad_x = _from_perm(dx)[None]

    core = lax.axis_index("c").astype(I32).reshape(1)
    from_sibling = _exchange_sibling([grads[n] for n in big_names])
    pair = [_pair_sum(f"pair_sum_{n}", grads[n], r, kinds[n], core) for n, r in zip(big_names, from_sibling)]
    chip_sums = _exchange_chips(pair)
    out_g, out_d, out_m, out_v = {}, {}, {}, {}
    for n, s in zip(big_names, chip_sums):
        g, d, nm, nv = _adam_shard(f"adam_{n}", s, weights[n][0], mom1[n][0], mom2[n][0])
        out_g[n], out_d[n], out_m[n], out_v[n] = g[None], d[None], nm[None], nv[None]

    g_w_spatial = dws_t.reshape(N_GROUPS, N_SUB, g8, N_SUB, g8).transpose(0, 2, 1, 4, 3)
    g_b_spatial = dbs_t[:, :N_GROUPS].reshape(N_SUB, g8, N_GROUPS).transpose(2, 1, 0)
    part = _pack(dict(loss=loss_part, rel_bias=g_rel_bias, ln_v_gain=g_lnv_gain, ln_v_bias=g_lnv_bias, w_spatial=g_w_spatial,
                      b_spatial=g_b_spatial, ln1_gain=g_ln1_gain, ln1_bias=g_ln1_bias, b_ff1=g_b_ff1, b_ff2=g_b_ff2,
                      ln2_gain=g_ln2_gain, ln2_bias=g_ln2_bias))
    zero = jnp.zeros((1,), F32)
    parts = _gather_small(part)
    sg, sd, sm, sv = (_unpack(b) for b in _adam_small(
        parts, _pack({**weights, "loss": zero}), _pack({**mom1, "loss": zero}), _pack({**mom2, "loss": zero})))
    for n in WEIGHT_ORDER:
        if n not in kinds:
            shape = weights[n].shape
            out_g[n], out_d[n], out_m[n], out_v[n] = (t[n].reshape(shape) for t in (sg, sd, sm, sv))
    loss = sg["loss"].reshape(())
    return (loss, grad_x, *[out_g[n] for n in WEIGHT_ORDER], *[out_d[n] for n in WEIGHT_ORDER],
            *[out_m[n] for n in WEIGHT_ORDER], *[out_v[n] for n in WEIGHT_ORDER])
```

```python
import math

import jax
import jax.numpy as jnp
import numpy as np
from jax import lax
from jax.experimental import pallas as pl
from jax.experimental.pallas import tpu as pltpu

F32 = jnp.float32
BF16 = jnp.bfloat16
I32 = jnp.int32

SEQ = 2048
D_MODEL = 2048
HEAD_DIM = 128
N_HEADS = 8
N_GROUPS = 8
D_FF = 4 * D_MODEL
BLOCK = 128
DILATIONS = (1, 4, 16)
N_BUCKETS = 32
MAX_DISTANCE = 2048
ALPHA = 2.0 ** 0.25
LN_EPS = 1e-5
NEG_INF = -1e30
N_DEV = 8
N_CHIPS = 4
N_SUB = 16
ADAM_LR, ADAM_B1, ADAM_B2, ADAM_EPS, ADAM_WD, ADAM_STEP = 0.001, 0.9, 0.999, 1e-08, 0.01, 10
LANES = 128
SUBLANES = 8
VMEM_LIMIT = 56 * 1024 * 1024
MESH = pl.DeviceIdType.MESH
ANY = pl.BlockSpec(memory_space=pl.ANY)
WEIGHT_ORDER = ("w_in", "rel_bias", "ln_v_gain", "ln_v_bias", "w_spatial", "b_spatial", "w_proj_a", "w_proj_b", "w_out",
                "ln1_gain", "ln1_bias", "w_ff1", "b_ff1", "w_ff2", "b_ff2", "ln2_gain", "ln2_bias")
KINDS = {"w_in": "col", "w_proj_a": "col", "w_proj_b": "col", "w_out": "row", "w_ff1": "col", "w_ff2": "row"}


def _d_a():
    return N_HEADS * HEAD_DIM


def _d_b():
    return N_GROUPS * BLOCK


def _d_in():
    return 3 * _d_a() + 2 * _d_b() + 2 * D_MODEL


def _pick(t, n, *others):
    if n <= t and all(o % n == 0 for o in others):
        return n
    for c in range(min(t, n) // LANES * LANES, 0, -LANES):
        if n % c == 0 and all(o % c == 0 for o in others):
            return c
    raise ValueError((t, n, others))


class _Stage:
    def __init__(self, ins, outs, alias, sems, start, finish):
        self.ins, self.outs, self.alias, self.sems, self.start, self.finish = ins, outs, alias, sems, start, finish


def _call(name, body, grid, in_specs, out_specs, out_shape, operands, scratch=(), sem=None, stages=(), sequential=False):
    n_in, n_out, n_sc = len(in_specs), len(out_specs), len(scratch)
    st_in = [len(s.ins) for s in stages]
    st_out = [len(s.outs) for s in stages]
    st_sem = [len(s.sems) for s in stages]
    aliases, ioff, ooff = {}, n_in, n_out
    for s, ni, no in zip(stages, st_in, st_out):
        for i, o in s.alias.items():
            aliases[ioff + i] = ooff + o
        ioff, ooff = ioff + ni, ooff + no

    def split(refs, counts):
        out, at = [], 0
        for c in counts:
            out.append(refs[at:at + c])
            at += c
        return out

    def wrapped(*refs):
        ins, sins, outs, souts, sc, ssems = split(refs, [n_in, sum(st_in), n_out, sum(st_out), n_sc, sum(st_sem)])
        parts = list(zip(stages, split(sins, st_in), split(souts, st_out), split(ssems, st_sem)))
        if sequential:
            for s, a, b, c in parts:
                s.start(a, b, c)
                s.finish(a, b, c)
            return
        if parts:
            first = _all_of([pl.program_id(i) == 0 for i in range(len(grid))])
            last = _all_of([pl.program_id(i) == g - 1 for i, g in enumerate(grid)])

            @pl.when(first)
            def _():
                for s, a, b, c in parts:
                    s.start(a, b, c)

        body(*ins, *outs, *sc)
        if parts:
            @pl.when(last)
            def _():
                for s, a, b, c in parts:
                    s.finish(a, b, c)

    if stages or sem is None:
        sem = ("arbitrary",) * len(grid)
    res = pl.pallas_call(
        wrapped, name=name, grid=grid,
        in_specs=list(in_specs) + [ANY] * sum(st_in), out_specs=list(out_specs) + [ANY] * sum(st_out),
        out_shape=list(out_shape) + [o for s in stages for o in s.outs],
        scratch_shapes=list(scratch) + [x for s in stages for x in s.sems],
        input_output_aliases=aliases,
        compiler_params=pltpu.CompilerParams(dimension_semantics=sem, vmem_limit_bytes=VMEM_LIMIT),
    )(*operands, *[a for s in stages for a in s.ins])
    res = list(res)
    return res[:n_out], split(res[n_out:], st_out)


def _all_of(conds):
    out = conds[0]
    for c in conds[1:]:
        out = out & c
    return out


def _coords():
    return lax.axis_index("x"), lax.axis_index("y"), lax.axis_index("c")


def _other_chips(x, y):
    return ((1 - x, y), (x, 1 - y), (1 - x, 1 - y))


def _lin(dev):
    return 4 * dev[0] + 2 * dev[1] + dev[2]


def _piece(total, i, n):
    assert total % n == 0
    return i * (total // n), total // n


def _remote(src, dst, send, recv, to):
    return pltpu.make_async_remote_copy(src_ref=src, dst_ref=dst, send_sem=send, recv_sem=recv, device_id=to, device_id_type=MESH)


def _placer(kind, n, lo, cnt):
    def place(ref, dev):
        if kind == "col":
            return ref.at[pl.ds(lo, cnt), pl.ds(pl.multiple_of(_lin(dev) * n, LANES), n)]
        return ref.at[pl.ds(pl.multiple_of(_lin(dev) * n + lo, 2 * SUBLANES), cnt), :]
    return place


def _spread_stage(shard, full, kind, piece=(0, 1)):
    n = shard.shape[1] if kind == "col" else shard.shape[0]
    lo, cnt = _piece(shard.shape[0], *piece)
    place = _placer(kind, n, lo, cnt)

    def copies(ins, outs, sems):
        send, recv, local = sems
        x, y, c = _coords()
        me = (x, y, c)
        src = ins[0].at[pl.ds(lo, cnt), :]
        peers = [(x, y, 1 - c)] + [(*chip, c) for chip in _other_chips(x, y)]
        own = pltpu.make_async_copy(src, place(outs[0], me), local)
        out = [_remote(src, place(outs[0], me), send.at[k], recv.at[k], t) for k, t in enumerate(peers)]
        arrive = [_remote(src, place(outs[0], t), send.at[k], recv.at[k], t) for k, t in enumerate(peers)]
        return own, out, arrive

    def start(ins, outs, sems):
        own, out, _ = copies(ins, outs, sems)
        own.start()
        for cp in out:
            cp.start()

    def finish(ins, outs, sems):
        own, out, arrive = copies(ins, outs, sems)
        for cp in arrive:
            cp.wait_recv()
        for cp in out:
            cp.wait_send()
        own.wait()

    return _Stage([shard, full], [jax.ShapeDtypeStruct(full.shape, full.dtype)], {1: 0},
                  [pltpu.SemaphoreType.DMA((4,)), pltpu.SemaphoreType.DMA((4,)), pltpu.SemaphoreType.DMA], start, finish)


def _forward_stage(full, kind, piece=(0, 1)):
    n = (full.shape[1] if kind == "col" else full.shape[0]) // N_DEV
    lo, cnt = _piece(full.shape[0] if kind == "col" else n, *piece)
    place = _placer(kind, n, lo, cnt)

    def copies(ins, outs, sems):
        send, recv = sems
        x, y, c = _coords()
        chips = _other_chips(x, y)
        out = [_remote(place(outs[0], (*chip, c)), place(outs[0], (*chip, c)), send.at[k], recv.at[k], (x, y, 1 - c)) for k, chip in enumerate(chips)]
        arrive = [_remote(place(outs[0], (*chip, 1 - c)), place(outs[0], (*chip, 1 - c)), send.at[k], recv.at[k], (x, y, 1 - c))
                  for k, chip in enumerate(chips)]
        return out, arrive

    def start(ins, outs, sems):
        for cp in copies(ins, outs, sems)[0]:
            cp.start()

    def finish(ins, outs, sems):
        out, arrive = copies(ins, outs, sems)
        for cp in arrive:
            cp.wait_recv()
        for cp in out:
            cp.wait_send()

    return _Stage([full], [jax.ShapeDtypeStruct(full.shape, full.dtype)], {0: 0},
                  [pltpu.SemaphoreType.DMA((3,)), pltpu.SemaphoreType.DMA((3,))], start, finish)


def _to_sibling_stage(grad, kind):
    n = (grad.shape[1] if kind == "col" else grad.shape[0]) // N_DEV
    shape = (N_CHIPS, grad.shape[0], n) if kind == "col" else (N_CHIPS, n, grad.shape[1])
    place = _placer(kind, n, 0, grad.shape[0] if kind == "col" else n)

    def copies(ins, outs, sems):
        send, recv = sems
        x, y, c = _coords()
        return [_remote(place(ins[0], (q // 2, q % 2, 1 - c)), outs[0].at[q], send.at[q], recv.at[q], (x, y, 1 - c)) for q in range(N_CHIPS)]

    def start(ins, outs, sems):
        for cp in copies(ins, outs, sems):
            cp.start()

    def finish(ins, outs, sems):
        for cp in copies(ins, outs, sems):
            cp.wait()

    return _Stage([grad], [jax.ShapeDtypeStruct(shape, BF16)], {},
                  [pltpu.SemaphoreType.DMA((N_CHIPS,)), pltpu.SemaphoreType.DMA((N_CHIPS,))], start, finish)


def _to_chips_stage(pair, dst, piece=(0, 1)):
    lo, cnt = _piece(pair.shape[1], *piece)

    def copies(ins, outs, sems):
        send, recv, local = sems
        x, y, c = _coords()
        mine = 2 * x + y
        chips = _other_chips(x, y)
        own = pltpu.make_async_copy(ins[0].at[mine, pl.ds(lo, cnt), :], outs[0].at[mine, pl.ds(lo, cnt), :], local)
        out = [_remote(ins[0].at[2 * px + py, pl.ds(lo, cnt), :], outs[0].at[mine, pl.ds(lo, cnt), :], send.at[k], recv.at[k], (px, py, c))
               for k, (px, py) in enumerate(chips)]
        arrive = [_remote(ins[0].at[2 * px + py, pl.ds(lo, cnt), :], outs[0].at[2 * px + py, pl.ds(lo, cnt), :], send.at[k], recv.at[k], (px, py, c))
                  for k, (px, py) in enumerate(chips)]
        return own, out, arrive

    def start(ins, outs, sems):
        own, out, _ = copies(ins, outs, sems)
        own.start()
        for cp in out:
            cp.start()

    def finish(ins, outs, sems):
        own, out, arrive = copies(ins, outs, sems)
        for cp in arrive:
            cp.wait_recv()
        for cp in out:
            cp.wait_send()
        own.wait()

    return _Stage([pair, dst], [jax.ShapeDtypeStruct(dst.shape, dst.dtype)], {1: 0},
                  [pltpu.SemaphoreType.DMA((3,)), pltpu.SemaphoreType.DMA((3,)), pltpu.SemaphoreType.DMA], start, finish)


def _small_stage(part):
    def copies(ins, outs, sems):
        send, recv, local = sems
        x, y, c = _coords()
        me = (x, y, c)
        own = pltpu.make_async_copy(ins[0], outs[0].at[_lin(me)], local)
        peers = [(1 - x if k & 4 else x, 1 - y if k & 2 else y, 1 - c if k & 1 else c) for k in range(1, N_DEV)]
        out = [_remote(ins[0], outs[0].at[_lin(me)], send.at[k], recv.at[k], t) for k, t in enumerate(peers)]
        arrive = [_remote(ins[0], outs[0].at[_lin(t)], send.at[k], recv.at[k], t) for k, t in enumerate(peers)]
        return own, out, arrive

    def start(ins, outs, sems):
        own, out, _ = copies(ins, outs, sems)
        own.start()
        for cp in out:
            cp.start()

    def finish(ins, outs, sems):
        own, out, arrive = copies(ins, outs, sems)
        for cp in arrive:
            cp.wait_recv()
        for cp in out:
            cp.wait_send()
        own.wait()

    return _Stage([part], [jax.ShapeDtypeStruct((N_DEV, *part.shape), F32)], {},
                  [pltpu.SemaphoreType.DMA((N_DEV - 1,)), pltpu.SemaphoreType.DMA((N_DEV - 1,)), pltpu.SemaphoreType.DMA], start, finish)


def _comm_only(name, stages):
    return _call(name, lambda: None, (1,), [], [], [], [], stages=stages, sequential=True)[1]


_GELU_C = math.sqrt(2.0 / math.pi)


def _gelu(x):
    return 0.5 * x * (1.0 + jnp.tanh(_GELU_C * (x + 0.044715 * x * x * x)))


def _gelu_grad(x):
    t = jnp.tanh(_GELU_C * (x + 0.044715 * x * x * x))
    return 0.5 * (1.0 + t) + 0.5 * x * (1.0 - t * t) * (_GELU_C * (1.0 + 3.0 * 0.044715 * x * x))


def _sigmoid(x):
    return 1.0 / (1.0 + jnp.exp(-x))


def _dot(a, b, mode):
    dims = {"nn": (((1,), (0,)), ((), ())), "nt": (((1,), (1,)), ((), ())), "tn": (((0,), (0,)), ((), ()))}[mode]
    return lax.dot_general(a.astype(BF16), b.astype(BF16), dims, preferred_element_type=F32)


def _matmul(name, a, b, mode, outs, epi=None, extras=(), colsums=(), tm=512, tn=512, tk=2048, b_off=0, n=None, m_off=0, m=None, stages=()):
    if mode == "tn":
        kk, mfull = a.shape
    else:
        mfull, kk = a.shape
    m = mfull if m is None else m
    n = (b.shape[0] if mode == "nt" else b.shape[1]) if n is None else n
    tm, tk = _pick(tm, m, m_off), _pick(tk, kk)
    tn = _pick(tn, n, b_off, *[off for _, _, off in extras])
    boff, moff = b_off // tn, m_off // tm
    nm, nn_, nk = m // tm, n // tn, kk // tk
    col_major = bool(colsums)
    grid = (nn_, nm, nk) if col_major else (nm, nn_, nk)

    def imap(f):
        if col_major:
            return lambda g0, g1, k: f(g1, g0, k)
        return f

    a_spec = (pl.BlockSpec((tk, tm), imap(lambda i, j, k: (k, i + moff))) if mode == "tn"
              else pl.BlockSpec((tm, tk), imap(lambda i, j, k: (i + moff, k))))
    b_spec = (pl.BlockSpec((tn, tk), imap(lambda i, j, k: (j + boff, k))) if mode == "nt"
              else pl.BlockSpec((tk, tn), imap(lambda i, j, k: (k, j + boff))))
    in_specs, operands = [a_spec, b_spec], [a, b]
    for arr, kind, off in extras:
        o = off // tn
        if kind == "mn":
            in_specs.append(pl.BlockSpec((tm, tn), imap(lambda i, j, k, o=o: (i + moff, j + o))))
        else:
            in_specs.append(pl.BlockSpec((1, tn), imap(lambda i, j, k, o=o: (0, j + o))))
        operands.append(arr)
    out_shape = [jax.ShapeDtypeStruct((m, n), dt) for dt in outs] + [jax.ShapeDtypeStruct((1, n), F32) for _ in colsums]
    out_specs = ([pl.BlockSpec((tm, tn), imap(lambda i, j, k: (i, j))) for _ in outs]
                 + [pl.BlockSpec((1, tn), imap(lambda i, j, k: (0, j))) for _ in colsums])
    n_ex, n_out, n_cs = len(extras), len(outs), len(colsums)

    def body(*refs):
        a_ref, b_ref = refs[:2]
        ex_refs = refs[2:2 + n_ex]
        out_refs = refs[2 + n_ex:2 + n_ex + n_out]
        cs_refs = refs[2 + n_ex + n_out:2 + n_ex + n_out + n_cs]
        part = _dot(a_ref[...], b_ref[...], mode)

        def finish(acc):
            res = epi(acc, *[r[...] for r in ex_refs]) if epi is not None else (acc,)
            for r, v in zip(out_refs, res[:n_out]):
                r[...] = v.astype(r.dtype)
            if n_cs:
                @pl.when(pl.program_id(1) == 0)
                def _():
                    for r in cs_refs:
                        r[...] = jnp.zeros_like(r)

                for r, idx in zip(cs_refs, colsums):
                    r[...] += jnp.sum(res[idx], axis=0, keepdims=True)

        if nk == 1:
            finish(part)
        else:
            acc_ref = refs[-1]
            k = pl.program_id(2)

            @pl.when(k == 0)
            def _():
                acc_ref[...] = part

            @pl.when(k > 0)
            def _():
                acc_ref[...] += part

            @pl.when(k == nk - 1)
            def _():
                finish(acc_ref[...])

    sem = ("arbitrary", "arbitrary", "arbitrary") if col_major else ("parallel", "parallel", "arbitrary")
    return _call(name, body, grid, in_specs, out_specs, out_shape, operands,
                 scratch=[pltpu.VMEM((tm, tn), F32)] if nk > 1 else [], sem=sem, stages=stages)


def _row_spec(tr, c):
    return pl.BlockSpec((tr, c), lambda i: (i, 0))


def _fix_spec(shape):
    return pl.BlockSpec(shape, lambda *_: tuple(0 for _ in shape))


def _cast_bf16(name, x, tr=512):
    r, c = x.shape
    tr = _pick(tr, r)

    def body(x_ref, o_ref):
        o_ref[...] = x_ref[...].astype(BF16)

    return _call(name, body, (r // tr,), [_row_spec(tr, c)], [_row_spec(tr, c)], [jax.ShapeDtypeStruct((r, c), BF16)], [x],
                 sem=("parallel",))[0][0]


def _layer_norm_stats(x):
    mean = jnp.mean(x, axis=-1, keepdims=True)
    xc = x - mean
    var = jnp.mean(xc * xc, axis=-1, keepdims=True)
    rstd = lax.rsqrt(var + LN_EPS)
    return xc * rstd, rstd


def _layer_norm_bwd(dxhat, xhat, rstd):
    m1 = jnp.mean(dxhat, axis=-1, keepdims=True)
    m2 = jnp.mean(dxhat * xhat, axis=-1, keepdims=True)
    return rstd * (dxhat - m1 - xhat * m2)


def _ln1_fwd(pre1, g1, b1, tr=256, stages=()):
    s, d = pre1.shape
    tr = _pick(tr, s)

    def body(p_ref, g_ref, b_ref, xh_ref, rs_ref, h_ref):
        xhat, rstd = _layer_norm_stats(p_ref[...])
        xh_ref[...] = xhat
        rs_ref[...] = rstd
        h_ref[...] = (xhat * g_ref[...] + b_ref[...]).astype(BF16)

    return _call("ln1_fwd", body, (s // tr,), [_row_spec(tr, d), _fix_spec((1, d)), _fix_spec((1, d))],
                 [_row_spec(tr, d), _row_spec(tr, 1), _row_spec(tr, d)],
                 [jax.ShapeDtypeStruct((s, d), F32), jax.ShapeDtypeStruct((s, 1), F32), jax.ShapeDtypeStruct((s, d), BF16)],
                 [pre1, g1, b1], sem=("parallel",), stages=stages)


def _ln2_loss_bwd(ff, xhat1, g1, b1, g2, b2, target, tr=256):
    s, d = ff.shape
    tr = _pick(tr, s)

    def body(ff_ref, xh1_ref, g1_ref, b1_ref, g2_ref, b2_ref, t_ref, dp_ref, dpb_ref, dg_ref, db_ref, dbf_ref, loss_ref):
        @pl.when(pl.program_id(0) == 0)
        def _():
            dg_ref[...] = jnp.zeros_like(dg_ref)
            db_ref[...] = jnp.zeros_like(db_ref)
            dbf_ref[...] = jnp.zeros_like(dbf_ref)
            loss_ref[...] = jnp.zeros_like(loss_ref)

        h1 = xh1_ref[...] * g1_ref[...] + b1_ref[...]
        xhat, rstd = _layer_norm_stats(ALPHA * h1 + ff_ref[...])
        err = xhat * g2_ref[...] + b2_ref[...] - t_ref[...]
        row = jnp.mean(err * err, axis=-1, keepdims=True)
        loss_ref[...] += 0.5 * jnp.sum(row, axis=0, keepdims=True)
        dy = err / d
        dg_ref[...] += jnp.sum(dy * xhat, axis=0, keepdims=True)
        db_ref[...] += jnp.sum(dy, axis=0, keepdims=True)
        dpre = _layer_norm_bwd(dy * g2_ref[...], xhat, rstd)
        dbf_ref[...] += jnp.sum(dpre, axis=0, keepdims=True)
        dp_ref[...] = dpre
        dpb_ref[...] = dpre.astype(BF16)

    vec = _fix_spec((1, d))
    return _call("ln2_loss_bwd", body, (s // tr,), [_row_spec(tr, d), _row_spec(tr, d), vec, vec, vec, vec, _row_spec(tr, d)],
                 [_row_spec(tr, d), _row_spec(tr, d), vec, vec, vec, _fix_spec((1, 1))],
                 [jax.ShapeDtypeStruct((s, d), F32), jax.ShapeDtypeStruct((s, d), BF16)]
                 + [jax.ShapeDtypeStruct((1, d), F32)] * 3 + [jax.ShapeDtypeStruct((1, 1), F32)],
                 [ff, xhat1, g1, b1, g2, b2, target])[0]


def _ln1_bwd(dh1, xhat1, rstd1, g1, tr=256, stages=()):
    s, d = dh1.shape
    tr = _pick(tr, s)

    def body(dh_ref, xh_ref, rs_ref, g_ref, dp_ref, dpb_ref, dg_ref, db_ref):
        @pl.when(pl.program_id(0) == 0)
        def _():
            dg_ref[...] = jnp.zeros_like(dg_ref)
            db_ref[...] = jnp.zeros_like(db_ref)

        dh, xhat = dh_ref[...], xh_ref[...]
        dg_ref[...] += jnp.sum(dh * xhat, axis=0, keepdims=True)
        db_ref[...] += jnp.sum(dh, axis=0, keepdims=True)
        dpre = _layer_norm_bwd(dh * g_ref[...], xhat, rs_ref[...])
        dp_ref[...] = dpre
        dpb_ref[...] = dpre.astype(BF16)

    vec = _fix_spec((1, d))
    return _call("ln1_bwd", body, (s // tr,), [_row_spec(tr, d), _row_spec(tr, d), _row_spec(tr, 1), vec],
                 [_row_spec(tr, d), _row_spec(tr, d), vec, vec],
                 [jax.ShapeDtypeStruct((s, d), F32), jax.ShapeDtypeStruct((s, d), BF16)] + [jax.ShapeDtypeStruct((1, d), F32)] * 2,
                 [dh1, xhat1, rstd1, g1], stages=stages)


def _to_perm(x):
    return x.reshape(SEQ // N_SUB, N_SUB, -1).transpose(1, 0, 2).reshape(SEQ, -1)


def _from_perm(x):
    return x.reshape(N_SUB, SEQ // N_SUB, -1).transpose(1, 0, 2).reshape(SEQ, -1)


def _local_index(p):
    rho = np.arange(BLOCK)
    if p == 0:
        return 16 * (rho % 8) + rho // 8
    if p == 1:
        return 4 * (rho % 32) + rho // 32
    return rho


def _tile_view(x, p):
    c = x.shape[1]
    if p == 1:
        return x.reshape(4, 4, BLOCK, c)
    return x.reshape(N_SUB, BLOCK, c)


def _view_shape(c, p):
    return (4, 4, BLOCK, c) if p == 1 else (N_SUB, BLOCK, c)


def _tile_spec(p, width, col, shift=0):
    nblk = SEQ // DILATIONS[p] // BLOCK

    def blk(n):
        return jnp.clip(n + shift, 0, nblk - 1)

    if p == 0:
        return pl.BlockSpec((N_SUB, SUBLANES, width), lambda s, n: (0, blk(n), col))
    if p == 1:
        return pl.BlockSpec((4, None, 32, width), lambda s, n: (0, s, blk(n), col))
    return pl.BlockSpec((None, BLOCK, width), lambda s, n: (s, 0, col))


def _tile_grid(p):
    return ((1, 16), (4, 4), (16, 1))[p]


def _t5_bucket(n):
    max_exact = N_BUCKETS // 2
    nf = np.maximum(n, 1).astype(np.float32)
    large = max_exact + (np.log(nf / np.float32(max_exact)) / np.float32(math.log(MAX_DISTANCE / max_exact))
                         * np.float32(N_BUCKETS - max_exact)).astype(np.int32)
    large = np.minimum(large, N_BUCKETS - 1)
    return np.where(n < max_exact, n, large).astype(np.int32)


def _bucket_tables():
    tabs = np.zeros((3, 2, BLOCK, BLOCK), np.int32)
    for p, d in enumerate(DILATIONS):
        i = _local_index(p)
        diff = i[:, None] - i[None, :]
        tabs[p, 0] = np.where(diff <= 0, _t5_bucket((BLOCK + diff) * d), -1)
        tabs[p, 1] = np.where(diff >= 0, _t5_bucket(np.maximum(diff, 0) * d), -1)
    return tabs


def _bias_expand(rel_bias, buckets):
    nh = N_HEADS

    def body(rb_ref, bk_ref, o_ref):
        for w in range(2):
            bk = bk_ref[0, w]
            for h in range(nh):
                val = jnp.zeros((BLOCK, BLOCK), F32)
                for b in range(N_BUCKETS):
                    val = jnp.where(bk == b, rb_ref[b, h], val)
                o_ref[0, h, w] = jnp.where(bk < 0, NEG_INF, val)

    return _call("bias_expand", body, (3,),
                 [pl.BlockSpec(memory_space=pltpu.SMEM), pl.BlockSpec((1, 2, BLOCK, BLOCK), lambda p: (p, 0, 0, 0))],
                 [pl.BlockSpec((1, nh, 2, BLOCK, BLOCK), lambda p: (p, 0, 0, 0, 0))],
                 [jax.ShapeDtypeStruct((3, nh, 2, BLOCK, BLOCK), F32)], [rel_bias, buckets], sem=("parallel",))[0][0]


def _heads_to_lanes(cols):
    lane = lax.broadcasted_iota(I32, (BLOCK, LANES), 1)
    out = jnp.zeros((BLOCK, LANES), F32)
    for h, c in enumerate(cols):
        out = jnp.where(lane == h, c, out)
    return out


def _attn_fwd(qkv, bias, p, stages=()):
    d_a = _d_a()
    has_prev = SEQ // DILATIONS[p] // BLOCK > 1
    scale = HEAD_DIM ** -0.5
    view = _tile_view(qkv, p)

    def body(q_ref, kc_ref, kp_ref, vc_ref, vp_ref, b_ref, o_ref, l_ref):
        n = pl.program_id(1)
        q_all = q_ref[...].reshape(BLOCK, d_a).astype(BF16)
        kc_all = kc_ref[...].reshape(BLOCK, d_a).astype(BF16)
        vc_all = vc_ref[...].reshape(BLOCK, d_a).astype(BF16)
        if has_prev:
            kp_all = kp_ref[...].reshape(BLOCK, d_a).astype(BF16)
            vp_all = vp_ref[...].reshape(BLOCK, d_a).astype(BF16)
        outs, lses = [], []
        for h in range(N_HEADS):
            sl = slice(h * HEAD_DIM, (h + 1) * HEAD_DIM)
            q = q_all[:, sl]
            sc = _dot(q, kc_all[:, sl], "nt") * scale + b_ref[0, h, 1]
            m = jnp.max(sc, axis=-1, keepdims=True)
            if has_prev:
                sp = _dot(q, kp_all[:, sl], "nt") * scale + b_ref[0, h, 0]
                sp = jnp.where(n > 0, sp, NEG_INF)
                m = jnp.maximum(m, jnp.max(sp, axis=-1, keepdims=True))
                pp = jnp.exp(sp - m)
            pc = jnp.exp(sc - m)
            den = jnp.sum(pc, axis=-1, keepdims=True)
            o = _dot(pc, vc_all[:, sl], "nn")
            if has_prev:
                den = den + jnp.sum(pp, axis=-1, keepdims=True)
                o = o + _dot(pp, vp_all[:, sl], "nn")
            outs.append(o / den)
            lses.append(m + jnp.log(den))
        o_ref[...] = jnp.concatenate(outs, axis=-1).reshape(o_ref.shape)
        l_ref[...] = _heads_to_lanes(lses).reshape(l_ref.shape)

    (o, l), st = _call(
        f"attn_fwd{p}", body, _tile_grid(p),
        [_tile_spec(p, d_a, 0), _tile_spec(p, d_a, 1), _tile_spec(p, d_a, 1, -1), _tile_spec(p, d_a, 2), _tile_spec(p, d_a, 2, -1),
         pl.BlockSpec((1, N_HEADS, 2, BLOCK, BLOCK), lambda s, n: (p, 0, 0, 0, 0))],
        [_tile_spec(p, d_a, 0), _tile_spec(p, LANES, 0)],
        [jax.ShapeDtypeStruct(_view_shape(d_a, p), F32), jax.ShapeDtypeStruct(_view_shape(LANES, p), F32)],
        [view, view, view, view, view, bias], sem=("parallel", "parallel"), stages=stages)
    return (o.reshape(SEQ, d_a), l.reshape(SEQ, LANES)), st


def _attn_combine(os_, ls_, tr=256):
    d_a = _d_a()
    tr = _pick(tr, SEQ)

    def body(o0, o1, o2, l0, l1, l2, a_ref, ab_ref, lt_ref):
        l = [l0[...], l1[...], l2[...]]
        m = jnp.maximum(jnp.maximum(l[0], l[1]), l[2])
        w = [jnp.exp(x - m) for x in l]
        tot = w[0] + w[1] + w[2]
        lt_ref[...] = m + jnp.log(tot)
        w = [x / tot for x in w]
        for h in range(N_HEADS):
            sl = slice(h * HEAD_DIM, (h + 1) * HEAD_DIM)
            acc = w[0][:, h:h + 1] * o0[:, sl] + w[1][:, h:h + 1] * o1[:, sl] + w[2][:, h:h + 1] * o2[:, sl]
            a_ref[:, sl] = acc
            ab_ref[:, sl] = acc.astype(BF16)

    return _call("attn_combine", body, (SEQ // tr,), [_row_spec(tr, d_a)] * 3 + [_row_spec(tr, LANES)] * 3,
                 [_row_spec(tr, d_a), _row_spec(tr, d_a), _row_spec(tr, LANES)],
                 [jax.ShapeDtypeStruct((SEQ, d_a), F32), jax.ShapeDtypeStruct((SEQ, d_a), BF16), jax.ShapeDtypeStruct((SEQ, LANES), F32)],
                 [*os_, *ls_], sem=("parallel",))[0]


def _attn_delta(dattn, attn, tr=256):
    d_a = _d_a()
    tr = _pick(tr, SEQ)

    def body(d_ref, a_ref, o_ref):
        prod = d_ref[...] * a_ref[...]
        lane = lax.broadcasted_iota(I32, (tr, LANES), 1)
        out = jnp.zeros((tr, LANES), F32)
        for h in range(N_HEADS):
            out = jnp.where(lane == h, jnp.sum(prod[:, h * HEAD_DIM:(h + 1) * HEAD_DIM], axis=-1, keepdims=True), out)
        o_ref[...] = out

    return _call("attn_delta", body, (SEQ // tr,), [_row_spec(tr, d_a)] * 2, [_row_spec(tr, LANES)],
                 [jax.ShapeDtypeStruct((SEQ, LANES), F32)], [dattn, attn], sem=("parallel",))[0][0]


def _attn_bwd(qkv, dattn, lse, delta, bias, p, stages=()):
    d_a = _d_a()
    nblk = SEQ // DILATIONS[p] // BLOCK
    has_next = nblk > 1
    scale = HEAD_DIM ** -0.5
    qv, dov, lv, tv = (_tile_view(x, p) for x in (qkv, dattn, lse, delta))

    def body(q_ref, qn_ref, k_ref, v_ref, do_ref, don_ref, l_ref, ln_ref, t_ref, tn_ref, b_ref, dq_ref, dk_ref, dv_ref, db_ref, carry_ref):
        j = pl.program_id(1)

        @pl.when((pl.program_id(0) == 0) & (j == 0))
        def _():
            db_ref[...] = jnp.zeros_like(db_ref)

        k_all = k_ref[...].reshape(BLOCK, d_a).astype(BF16)
        v_all = v_ref[...].reshape(BLOCK, d_a).astype(BF16)

        def side(qr, dor, lr, tr_, w):
            q_all = qr[...].reshape(BLOCK, d_a).astype(BF16)
            do_all = dor[...].reshape(BLOCK, d_a).astype(BF16)
            l_all = lr[...].reshape(BLOCK, LANES)
            t_all = tr_[...].reshape(BLOCK, LANES)
            dqs, dks, dvs = [], [], []
            for h in range(N_HEADS):
                sl = slice(h * HEAD_DIM, (h + 1) * HEAD_DIM)
                s = _dot(q_all[:, sl], k_all[:, sl], "nt") * scale + b_ref[0, h, w]
                pr = jnp.exp(s - l_all[:, h:h + 1])
                dp = _dot(do_all[:, sl], v_all[:, sl], "nt")
                ds = pr * (dp - t_all[:, h:h + 1])
                db_ref[h, w] += ds
                dqs.append(_dot(ds, k_all[:, sl], "nn") * scale)
                dks.append(_dot(ds, q_all[:, sl], "tn") * scale)
                dvs.append(_dot(pr, do_all[:, sl], "tn"))
            return [jnp.concatenate(x, axis=-1) for x in (dqs, dks, dvs)]

        dq_c, dk_c, dv_c = side(q_ref, do_ref, l_ref, t_ref, 1)
        if has_next:
            dq_ref[...] = (jnp.where(j > 0, carry_ref[...], 0.0) + dq_c).reshape(dq_ref.shape)
            not_last = j < nblk - 1

            @pl.when(not_last)
            def _():
                dq_n, dk_n, dv_n = side(qn_ref, don_ref, ln_ref, tn_ref, 0)
                carry_ref[...] = dq_n
                dk_ref[...] = (dk_c + dk_n).reshape(dk_ref.shape)
                dv_ref[...] = (dv_c + dv_n).reshape(dv_ref.shape)

            @pl.when(jnp.logical_not(not_last))
            def _():
                dk_ref[...] = dk_c.reshape(dk_ref.shape)
                dv_ref[...] = dv_c.reshape(dv_ref.shape)
        else:
            dq_ref[...] = dq_c.reshape(dq_ref.shape)
            dk_ref[...] = dk_c.reshape(dk_ref.shape)
            dv_ref[...] = dv_c.reshape(dv_ref.shape)

    def big(col, shift=0):
        return _tile_spec(p, d_a, col, shift)

    def small(shift=0):
        return _tile_spec(p, LANES, 0, shift)

    (dq, dk, dv, dbias), st = _call(
        f"attn_bwd{p}", body, _tile_grid(p),
        [big(0), big(0, 1), big(1), big(2), big(0), big(0, 1), small(), small(1), small(), small(1),
         pl.BlockSpec((1, N_HEADS, 2, BLOCK, BLOCK), lambda s, n: (p, 0, 0, 0, 0))],
        [big(0), big(0), big(0), pl.BlockSpec((N_HEADS, 2, BLOCK, BLOCK), lambda s, n: (0, 0, 0, 0))],
        [jax.ShapeDtypeStruct(_view_shape(d_a, p), F32)] * 3 + [jax.ShapeDtypeStruct((N_HEADS, 2, BLOCK, BLOCK), F32)],
        [qv, qv, qv, qv, dov, dov, lv, lv, tv, tv, bias], scratch=[pltpu.VMEM((BLOCK, d_a), F32)], stages=stages)
    return (dq.reshape(SEQ, d_a), dk.reshape(SEQ, d_a), dv.reshape(SEQ, d_a), dbias), st


def _rel_bias_grad(dbias, buckets):
    nh = N_HEADS

    def body(d0, d1, d2, bk_ref, o_ref, t_ref):
        ds = (d0, d1, d2)

        def per_bucket(b, carry):
            for h in range(nh):
                acc = jnp.zeros((BLOCK, BLOCK), F32)
                for p in range(3):
                    for w in range(2):
                        acc = acc + jnp.where(bk_ref[p, w] == b, ds[p][h, w], 0.0)
                t_ref[pl.ds(b * nh + h, 1), :] = jnp.sum(acc, axis=0, keepdims=True)
            return carry

        lax.fori_loop(0, N_BUCKETS, per_bucket, 0)
        o_ref[...] = jnp.sum(t_ref[...], axis=-1, keepdims=True)

    return _call("rel_bias_grad", body, (1,), [_fix_spec((nh, 2, BLOCK, BLOCK))] * 3 + [_fix_spec((3, 2, BLOCK, BLOCK))],
                 [_fix_spec((N_BUCKETS * nh, 1))], [jax.ShapeDtypeStruct((N_BUCKETS * nh, 1), F32)], [*dbias, buckets],
                 scratch=[pltpu.VMEM((N_BUCKETS * nh, LANES), F32)])[0][0]


def _gmlp_fwd(rest, gain, bias, ws, bs, causal, stages=()):
    d_b = _d_b()

    def body(u_ref, v_ref, g_ref, b_ref, ws_ref, bs_ref, c_ref, o_ref):
        u = u_ref[...].reshape(BLOCK, d_b)
        xhat, _ = _layer_norm_stats(_gelu(v_ref[...].reshape(BLOCK, d_b)))
        vn = (xhat * g_ref[...] + b_ref[...]).astype(BF16)
        outs = []
        for g in range(N_GROUPS):
            sl = slice(g * BLOCK, (g + 1) * BLOCK)
            w = jnp.where(c_ref[...] > 0, ws_ref[g], 0.0)
            z = _dot(w, vn[:, sl], "nn") + bs_ref[:, g:g + 1]
            outs.append(_gelu(u[:, sl]) * z)
        o_ref[...] = jnp.concatenate(outs, axis=-1).reshape(o_ref.shape)

    (out,), st = _call(
        "gmlp_fwd", body, (1, SEQ // BLOCK),
        [_tile_spec(0, d_b, 0), _tile_spec(0, d_b, 1), _fix_spec((1, d_b)), _fix_spec((1, d_b)),
         _fix_spec((N_GROUPS, BLOCK, BLOCK)), _fix_spec((BLOCK, N_GROUPS)), _fix_spec((BLOCK, BLOCK))],
        [_tile_spec(0, d_b, 0)], [jax.ShapeDtypeStruct(_view_shape(d_b, 0), F32)],
        [_tile_view(rest, 0), _tile_view(rest, 0), gain, bias, ws, bs, causal], sem=("parallel", "parallel"), stages=stages)
    return out.reshape(SEQ, d_b), st


def _gmlp_bwd(rest, dgmlp, gain, bias, ws, bs, causal, stages=()):
    d_b = _d_b()
    nchunk = SEQ // BLOCK

    def body(u_ref, v_ref, dg_ref, g_ref, b_ref, ws_ref, bs_ref, c_ref, du_ref, dv_ref, dws_ref, dbs_ref, dgain_ref, dbias_ref):
        c = pl.program_id(1)

        @pl.when(c == 0)
        def _():
            dws_ref[...] = jnp.zeros_like(dws_ref)
            dbs_ref[...] = jnp.zeros_like(dbs_ref)
            dgain_ref[...] = jnp.zeros_like(dgain_ref)
            dbias_ref[...] = jnp.zeros_like(dbias_ref)

        u = u_ref[...].reshape(BLOCK, d_b)
        v = v_ref[...].reshape(BLOCK, d_b)
        dgm = dg_ref[...].reshape(BLOCK, d_b)
        xhat, rstd = _layer_norm_stats(_gelu(v))
        vn = (xhat * g_ref[...] + b_ref[...]).astype(BF16)
        lane = lax.broadcasted_iota(I32, (BLOCK, LANES), 1)
        dus, dvns = [], []
        dbs = dbs_ref[...]
        for g in range(N_GROUPS):
            sl = slice(g * BLOCK, (g + 1) * BLOCK)
            w = jnp.where(c_ref[...] > 0, ws_ref[g], 0.0).astype(BF16)
            z = _dot(w, vn[:, sl], "nn") + bs_ref[:, g:g + 1]
            dz = dgm[:, sl] * _gelu(u[:, sl])
            dus.append(dgm[:, sl] * z * _gelu_grad(u[:, sl]))
            dws_ref[g] += _dot(dz, vn[:, sl], "nt")
            dbs = dbs + jnp.where(lane == g, jnp.sum(dz, axis=-1, keepdims=True), 0.0)
            dvns.append(_dot(w, dz, "tn"))
        dbs_ref[...] = dbs
        dvn = jnp.concatenate(dvns, axis=-1)
        dgain_ref[...] += jnp.sum(dvn * xhat, axis=0, keepdims=True)
        dbias_ref[...] += jnp.sum(dvn, axis=0, keepdims=True)
        dvg = _layer_norm_bwd(dvn * g_ref[...], xhat, rstd)
        du_ref[...] = jnp.concatenate(dus, axis=-1).reshape(du_ref.shape)
        dv_ref[...] = (dvg * _gelu_grad(v)).reshape(dv_ref.shape)

        @pl.when(c == nchunk - 1)
        def _():
            for g in range(N_GROUPS):
                dws_ref[g] = jnp.where(c_ref[...] > 0, dws_ref[g], 0.0)

    (du, dv, dws, dbs, dgain, dbias), st = _call(
        "gmlp_bwd", body, (1, nchunk),
        [_tile_spec(0, d_b, 0), _tile_spec(0, d_b, 1), _tile_spec(0, d_b, 0), _fix_spec((1, d_b)), _fix_spec((1, d_b)),
         _fix_spec((N_GROUPS, BLOCK, BLOCK)), _fix_spec((BLOCK, N_GROUPS)), _fix_spec((BLOCK, BLOCK))],
        [_tile_spec(0, d_b, 0), _tile_spec(0, d_b, 0), _fix_spec((N_GROUPS, BLOCK, BLOCK)), _fix_spec((BLOCK, LANES)),
         _fix_spec((1, d_b)), _fix_spec((1, d_b))],
        [jax.ShapeDtypeStruct(_view_shape(d_b, 0), F32)] * 2
        + [jax.ShapeDtypeStruct((N_GROUPS, BLOCK, BLOCK), F32), jax.ShapeDtypeStruct((BLOCK, LANES), F32)]
        + [jax.ShapeDtypeStruct((1, d_b), F32)] * 2,
        [_tile_view(rest, 0), _tile_view(rest, 0), _tile_view(dgmlp, 0), gain, bias, ws, bs, causal], stages=stages)
    return (du.reshape(SEQ, d_b), dv.reshape(SEQ, d_b), dws, dbs, dgain, dbias), st


def _assemble_dproj(dqkv, du, dv, dga, dgb, tr=128):
    d_a, d_b, d_in = _d_a(), _d_b(), _d_in()
    tr = _pick(tr, SEQ)

    def body(*refs):
        att, (du_ref, dv_ref, dga_ref, dgb_ref, o_ref) = refs[:9], refs[9:]
        for i in range(3):
            o_ref[:, i * d_a:(i + 1) * d_a] = (att[3 * i][...] + att[3 * i + 1][...] + att[3 * i + 2][...]).astype(BF16)
        o_ref[:, 3 * d_a:3 * d_a + d_b] = du_ref[...].astype(BF16)
        o_ref[:, 3 * d_a + d_b:3 * d_a + 2 * d_b] = dv_ref[...].astype(BF16)
        o_ref[:, 3 * d_a + 2 * d_b:3 * d_a + 2 * d_b + D_MODEL] = dga_ref[...]
        o_ref[:, 3 * d_a + 2 * d_b + D_MODEL:] = dgb_ref[...]

    return _call("assemble_dproj", body, (SEQ // tr,), [_row_spec(tr, d_a)] * 9 + [_row_spec(tr, d_b)] * 2 + [_row_spec(tr, D_MODEL)] * 2,
                 [_row_spec(tr, d_in)], [jax.ShapeDtypeStruct((SEQ, d_in), BF16)], [*dqkv, du, dv, dga, dgb], sem=("parallel",))[0][0]


def _pair_sum(name, grad, recv, kind, core, tr=256):
    _, rs, cs = recv.shape
    tr = _pick(tr, rs)
    nb = rs // tr
    if kind == "col":
        g_spec = pl.BlockSpec((tr, cs), lambda q, i, c_ref: (i, 2 * q + c_ref[0]))
    else:
        g_spec = pl.BlockSpec((tr, cs), lambda q, i, c_ref: ((2 * q + c_ref[0]) * nb + i, 0))
    r_spec = pl.BlockSpec((None, tr, cs), lambda q, i, c_ref: (q, i, 0))

    def body(c_ref, g_ref, r_ref, o_ref):
        o_ref[...] = (g_ref[...].astype(F32) + r_ref[...].astype(F32)).astype(BF16)

    return pl.pallas_call(
        body, name=name, out_shape=jax.ShapeDtypeStruct(recv.shape, BF16),
        grid_spec=pltpu.PrefetchScalarGridSpec(num_scalar_prefetch=1, grid=(N_CHIPS, nb), in_specs=[g_spec, r_spec], out_specs=r_spec),
        compiler_params=pltpu.CompilerParams(dimension_semantics=("parallel", "parallel"), vmem_limit_bytes=VMEM_LIMIT),
    )(core, grad, recv)


def _adamw(w, g, m, v):
    m = ADAM_B1 * m + (1.0 - ADAM_B1) * g
    v = ADAM_B2 * v + (1.0 - ADAM_B2) * (g * g)
    m_hat = m / (1.0 - ADAM_B1 ** ADAM_STEP)
    v_hat = v / (1.0 - ADAM_B2 ** ADAM_STEP)
    delta = -ADAM_LR * (m_hat / (jnp.sqrt(v_hat) + ADAM_EPS) + ADAM_WD * w)
    return delta, m, v


def _adam_shard(name, chip_sums, w, m, v, tr=256, stages=()):
    rs, cs = w.shape
    tr = _pick(tr, rs)

    def body(s_ref, w_ref, m_ref, v_ref, g_ref, d_ref, nm_ref, nv_ref):
        g = s_ref[0].astype(F32)
        for q in range(1, N_CHIPS):
            g = g + s_ref[q].astype(F32)
        d, nm, nv = _adamw(w_ref[...], g, m_ref[...], v_ref[...])
        g_ref[...], d_ref[...], nm_ref[...], nv_ref[...] = g, d, nm, nv

    spec = _row_spec(tr, cs)
    return _call(name, body, (rs // tr,), [pl.BlockSpec((N_CHIPS, tr, cs), lambda i: (0, i, 0)), spec, spec, spec], [spec] * 4,
                 [jax.ShapeDtypeStruct((rs, cs), F32)] * 4, [chip_sums, w, m, v], sem=("parallel",), stages=stages)


def _adam_small(parts, w, m, v):
    rows = w.shape[0]

    def body(p_ref, w_ref, m_ref, v_ref, g_ref, d_ref, nm_ref, nv_ref):
        g = p_ref[0]
        for j in range(1, N_DEV):
            g = g + p_ref[j]
        d, nm, nv = _adamw(w_ref[...], g, m_ref[...], v_ref[...])
        g_ref[...], d_ref[...], nm_ref[...], nv_ref[...] = g, d, nm, nv

    spec = _fix_spec((rows, LANES))
    return _call("adam_small", body, (1,), [_fix_spec((N_DEV, rows, LANES)), spec, spec, spec], [spec] * 4,
                 [jax.ShapeDtypeStruct((rows, LANES), F32)] * 4, [parts, w, m, v])[0]


def _small_sizes():
    d_b = _d_b()
    return (("loss", 1), ("rel_bias", N_BUCKETS * N_HEADS), ("ln_v_gain", d_b), ("ln_v_bias", d_b),
            ("w_spatial", N_GROUPS * BLOCK * BLOCK), ("b_spatial", N_GROUPS * BLOCK), ("ln1_gain", D_MODEL), ("ln1_bias", D_MODEL),
            ("b_ff1", D_FF), ("b_ff2", D_MODEL), ("ln2_gain", D_MODEL), ("ln2_bias", D_MODEL))


def _pack(vals):
    pieces = []
    for name, size in _small_sizes():
        flat = vals[name].reshape(-1).astype(F32)
        padded = -(-size // (SUBLANES * LANES)) * SUBLANES * LANES
        pieces.append(jnp.pad(flat, (0, padded - size)).reshape(-1, LANES))
    return jnp.concatenate(pieces, axis=0)


def _unpack(buf):
    out, row = {}, 0
    for name, size in _small_sizes():
        rows = -(-size // (SUBLANES * LANES)) * SUBLANES
        out[name] = buf[row:row + rows].reshape(-1)[:size]
        row += rows
    return out


def kernel(x, w_in, rel_bias, ln_v_gain, ln_v_bias, w_spatial, b_spatial, w_proj_a, w_proj_b, w_out, ln1_gain, ln1_bias, w_ff1, b_ff1, w_ff2, b_ff2, ln2_gain, ln2_bias, loss_target, m_w_in, m_rel_bias, m_ln_v_gain, m_ln_v_bias, m_w_spatial, m_b_spatial, m_w_proj_a, m_w_proj_b, m_w_out, m_ln1_gain, m_ln1_bias, m_w_ff1, m_b_ff1, m_w_ff2, m_b_ff2, m_ln2_gain, m_ln2_bias, v_w_in, v_rel_bias, v_ln_v_gain, v_ln_v_bias, v_w_spatial, v_b_spatial, v_w_proj_a, v_w_proj_b, v_w_out, v_ln1_gain, v_ln1_bias, v_w_ff1, v_b_ff1, v_w_ff2, v_b_ff2, v_ln2_gain, v_ln2_bias):
    d_a, d_b, d_in = _d_a(), _d_b(), _d_in()
    weights = dict(w_in=w_in, rel_bias=rel_bias, ln_v_gain=ln_v_gain, ln_v_bias=ln_v_bias, w_spatial=w_spatial, b_spatial=b_spatial,
                   w_proj_a=w_proj_a, w_proj_b=w_proj_b, w_out=w_out, ln1_gain=ln1_gain, ln1_bias=ln1_bias, w_ff1=w_ff1, b_ff1=b_ff1,
                   w_ff2=w_ff2, b_ff2=b_ff2, ln2_gain=ln2_gain, ln2_bias=ln2_bias)
    mom1 = dict(w_in=m_w_in, rel_bias=m_rel_bias, ln_v_gain=m_ln_v_gain, ln_v_bias=m_ln_v_bias, w_spatial=m_w_spatial,
                b_spatial=m_b_spatial, w_proj_a=m_w_proj_a, w_proj_b=m_w_proj_b, w_out=m_w_out, ln1_gain=m_ln1_gain,
                ln1_bias=m_ln1_bias, w_ff1=m_w_ff1, b_ff1=m_b_ff1, w_ff2=m_w_ff2, b_ff2=m_b_ff2, ln2_gain=m_ln2_gain, ln2_bias=m_ln2_bias)
    mom2 = dict(w_in=v_w_in, rel_bias=v_rel_bias, ln_v_gain=v_ln_v_gain, ln_v_bias=v_ln_v_bias, w_spatial=v_w_spatial,
                b_spatial=v_b_spatial, w_proj_a=v_w_proj_a, w_proj_b=v_w_proj_b, w_out=v_w_out, ln1_gain=v_ln1_gain,
                ln1_bias=v_ln1_bias, w_ff1=v_w_ff1, b_ff1=v_b_ff1, w_ff2=v_w_ff2, b_ff2=v_b_ff2, ln2_gain=v_ln2_gain, ln2_bias=v_ln2_bias)

    shard = {n: _cast_bf16(f"cast_{n}", weights[n][0]) for n in KINDS}
    full = {}
    for n, kind in KINDS.items():
        r, c = shard[n].shape
        full[n] = lax.empty((r, c * N_DEV) if kind == "col" else (r * N_DEV, c), BF16)

    def spread(n, piece=(0, 1)):
        return _spread_stage(shard[n], full[n], KINDS[n], piece)

    def forward(n):
        return _forward_stage(full[n], KINDS[n])

    def landed(names, outs):
        for n, o in zip(names, outs):
            full[n] = o[0]

    landed(["w_in"], _comm_only("gather_w_in_chips", [spread("w_in")]))
    landed(["w_in"], _comm_only("gather_w_in_sibling", [forward("w_in")]))

    xs = _to_perm(x[0])
    target = _to_perm(loss_target[0])
    xb = _cast_bf16("cast_x", xs)
    g8 = BLOCK // N_SUB
    ws_t = w_spatial[0].reshape(N_GROUPS, g8, N_SUB, g8, N_SUB).transpose(0, 2, 1, 4, 3).reshape(N_GROUPS, BLOCK, BLOCK)
    bs_t = b_spatial[0].reshape(N_GROUPS, g8, N_SUB).transpose(2, 1, 0).reshape(BLOCK, N_GROUPS)
    idx = _local_index(0)
    causal = jnp.asarray((idx[:, None] >= idx[None, :]).astype(np.float32))
    buckets = jnp.asarray(_bucket_tables())
    bias = _bias_expand(rel_bias, buckets)

    (qkv,), st = _matmul("proj_qkv", xb, full["w_in"], "nn", [F32], n=3 * d_a, stages=[spread("w_proj_a"), spread("w_proj_b")])
    landed(["w_proj_a", "w_proj_b"], st)
    (rest,), st = _matmul("proj_rest", xb, full["w_in"], "nn", [F32], b_off=3 * d_a, n=d_in - 3 * d_a,
                          stages=[forward("w_proj_a"), forward("w_proj_b"), spread("w_out")])
    landed(["w_proj_a", "w_proj_b", "w_out"], st)
    fwd = []
    for p in range(3):
        res, st = _attn_fwd(qkv, bias, p, stages=[spread("w_ff1", (p, 4))] + ([forward("w_out")] if p == 0 else []))
        landed(["w_ff1", "w_out"], st)
        fwd.append(res)
    attn, attn_b, lse = _attn_combine([o for o, _ in fwd], [l for _, l in fwd])
    gmlp, st = _gmlp_fwd(rest, ln_v_gain, ln_v_bias, ws_t, bs_t, causal, stages=[spread("w_ff1", (3, 4))])
    landed(["w_ff1"], st)
    (ya,), st = _matmul("proj_a", attn_b, full["w_proj_a"], "nn", [F32], stages=[forward("w_ff1")])
    landed(["w_ff1"], st)
    gate_a, gate_b = 2 * d_b, 2 * d_b + D_MODEL

    def merge(acc, ya_, ga, gb):
        return acc, _sigmoid(ga) * ya_ + _sigmoid(gb) * acc

    (yb, merged), st = _matmul("proj_b_merge", gmlp, full["w_proj_b"], "nn", [F32, BF16], merge,
                               [(ya, "mn", 0), (rest, "mn", gate_a), (rest, "mn", gate_b)], stages=[spread("w_ff2", (0, 4))])
    landed(["w_ff2"], st)
    (pre1,), st = _matmul("out_proj", merged, full["w_out"], "nn", [F32], lambda acc, x_: (ALPHA * x_ + acc,), [(xs, "mn", 0)],
                          stages=[spread("w_ff2", (1, 4))])
    landed(["w_ff2"], st)
    (xhat1, rstd1, h1b), st = _ln1_fwd(pre1, ln1_gain, ln1_bias, stages=[spread("w_ff2", (2, 4))])
    landed(["w_ff2"], st)

    def relu2(acc, b_):
        r = jnp.maximum(acc + b_, 0.0)
        return r, r * r

    (relu, fb), st = _matmul("ff1", h1b, full["w_ff1"], "nn", [F32, BF16], relu2, [(b_ff1, "row", 0)], stages=[spread("w_ff2", (3, 4))])
    landed(["w_ff2"], st)
    landed(["w_ff2"], _comm_only("gather_w_ff2_sibling", [forward("w_ff2")]))
    (ff,), _ = _matmul("ff2", fb, full["w_ff2"], "nn", [F32], lambda acc, b_: (acc + b_,), [(b_ff2, "row", 0)])

    core = lax.axis_index("c").astype(I32).reshape(1)
    grads, sib, pair, chips = {}, {}, {}, {}

    def to_sibling(n):
        return _to_sibling_stage(grads[n], KINDS[n])

    def pair_up(n):
        pair[n] = _pair_sum(f"pair_sum_{n}", grads[n], sib[n], KINDS[n], core)
        chips[n] = lax.empty(pair[n].shape, BF16)

    def to_chips(n, piece=(0, 1)):
        return _to_chips_stage(pair[n], chips[n], piece)

    dpre2, dpre2b, g_ln2_gain, g_ln2_bias, g_b_ff2, loss_part = _ln2_loss_bwd(ff, xhat1, ln1_gain, ln1_bias, ln2_gain, ln2_bias, target)
    (grads["w_ff2"],), _ = _matmul("dw_ff2", fb, dpre2b, "tn", [BF16])

    def relu2_bwd(acc, r):
        da = acc * (2.0 * r)
        return da, da

    (dab, g_b_ff1), st = _matmul("d_ff1", dpre2b, full["w_ff2"], "nt", [BF16], relu2_bwd, [(relu, "mn", 0)], colsums=(1,),
                                 stages=[to_sibling("w_ff2")])
    sib["w_ff2"] = st[0][0]
    pair_up("w_ff2")
    (grads["w_ff1"],), st = _matmul("dw_ff1", h1b, dab, "tn", [BF16], stages=[to_chips("w_ff2", (0, 2))])
    chips["w_ff2"] = st[0][0]
    (dh1,), st = _matmul("d_h1", dab, full["w_ff1"], "nt", [F32], lambda acc, d_: (acc + ALPHA * d_,), [(dpre2, "mn", 0)],
                         stages=[to_chips("w_ff2", (1, 2)), to_sibling("w_ff1")])
    chips["w_ff2"], sib["w_ff1"] = st[0][0], st[1][0]
    pair_up("w_ff1")
    (dpre1, dpre1b, g_ln1_gain, g_ln1_bias), st = _ln1_bwd(dh1, xhat1, rstd1, ln1_gain, stages=[to_chips("w_ff1", (0, 4))])
    chips["w_ff1"] = st[0][0]
    (grads["w_out"],), st = _matmul("dw_out", merged, dpre1b, "tn", [BF16], stages=[to_chips("w_ff1", (1, 4))])
    chips["w_ff1"] = st[0][0]

    def merge_bwd(acc, ga, gb, ya_, yb_):
        sa, sb = _sigmoid(ga), _sigmoid(gb)
        return acc * sa, acc * sb, acc * ya_ * (sa * (1.0 - sa)), acc * yb_ * (sb * (1.0 - sb))

    (dya, dyb, dga, dgb), st = _matmul("d_merge", dpre1b, full["w_out"], "nt", [BF16] * 4, merge_bwd,
                                       [(rest, "mn", gate_a), (rest, "mn", gate_b), (ya, "mn", 0), (yb, "mn", 0)],
                                       stages=[to_chips("w_ff1", (2, 4)), to_sibling("w_out")])
    chips["w_ff1"], sib["w_out"] = st[0][0], st[1][0]
    pair_up("w_out")
    (grads["w_proj_a"],), st = _matmul("dw_proj_a", attn_b, dya, "tn", [BF16], stages=[to_chips("w_ff1", (3, 4))])
    chips["w_ff1"] = st[0][0]
    (grads["w_proj_b"],), st = _matmul("dw_proj_b", gmlp, dyb, "tn", [BF16], stages=[to_chips("w_out")])
    chips["w_out"] = st[0][0]
    (dattn,), st = _matmul("d_attn", dya, full["w_proj_a"], "nt", [F32], stages=[to_sibling("w_proj_a"), to_sibling("w_proj_b")])
    sib["w_proj_a"], sib["w_proj_b"] = st[0][0], st[1][0]
    pair_up("w_proj_a")
    pair_up("w_proj_b")
    (dgmlp,), st = _matmul("d_gmlp", dyb, full["w_proj_b"], "nt", [F32], stages=[to_chips("w_proj_a"), to_chips("w_proj_b")])
    chips["w_proj_a"], chips["w_proj_b"] = st[0][0], st[1][0]
    (du, dvb, dws_t, dbs_t, g_lnv_gain, g_lnv_bias), _ = _gmlp_bwd(rest, dgmlp, ln_v_gain, ln_v_bias, ws_t, bs_t, causal)
    delta = _attn_delta(dattn, attn)
    bwd = [_attn_bwd(qkv, dattn, lse, delta, bias, p)[0] for p in range(3)]
    g_rel_bias = _rel_bias_grad([b[3] for b in bwd], buckets)
    dproj = _assemble_dproj([b[i] for i in range(3) for b in bwd], du, dvb, dga, dgb)

    g_w_spatial = dws_t.reshape(N_GROUPS, N_SUB, g8, N_SUB, g8).transpose(0, 2, 1, 4, 3)
    g_b_spatial = dbs_t[:, :N_GROUPS].reshape(N_SUB, g8, N_GROUPS).transpose(2, 1, 0)
    part = _pack(dict(loss=loss_part, rel_bias=g_rel_bias, ln_v_gain=g_lnv_gain, ln_v_bias=g_lnv_bias, w_spatial=g_w_spatial,
                      b_spatial=g_b_spatial, ln1_gain=g_ln1_gain, ln1_bias=g_ln1_bias, b_ff1=g_b_ff1, b_ff2=g_b_ff2,
                      ln2_gain=g_ln2_gain, ln2_bias=g_ln2_bias))
    (grads["w_in"],), st = _matmul("dw_in", xb, dproj, "tn", [BF16], stages=[_small_stage(part)])
    parts = st[0][0]
    half = SEQ // 2

    def add_residual(acc, d_):
        return (acc + ALPHA * d_,)

    (dx0,), st = _matmul("d_x0", dproj, full["w_in"], "nt", [F32], add_residual, [(dpre1, "mn", 0)], tk=1024, m=half,
                         stages=[to_sibling("w_in")])
    sib["w_in"] = st[0][0]
    pair_up("w_in")
    (dx1,), st = _matmul("d_x1", dproj, full["w_in"], "nt", [F32], add_residual, [(dpre1, "mn", 0)], tk=1024, m_off=half, m=half,
                         stages=[to_chips("w_in", (0, 2))])
    chips["w_in"] = st[0][0]
    grad_x = _from_perm(jnp.concatenate([dx0, dx1], axis=0))[None]

    out_g, out_d, out_m, out_v = {}, {}, {}, {}
    for n in ("w_ff2", "w_ff1", "w_out", "w_proj_a", "w_proj_b", "w_in"):
        stages = [to_chips("w_in", (1, 2))] if n == "w_ff2" else []
        (g, d, nm, nv), st = _adam_shard(f"adam_{n}", chips[n], weights[n][0], mom1[n][0], mom2[n][0], stages=stages)
        if stages:
            chips["w_in"] = st[0][0]
        out_g[n], out_d[n], out_m[n], out_v[n] = g[None], d[None], nm[None], nv[None]

    zero = jnp.zeros((1,), F32)
    sg, sd, sm, sv = (_unpack(b) for b in _adam_small(
        parts, _pack({**weights, "loss": zero}), _pack({**mom1, "loss": zero}), _pack({**mom2, "loss": zero})))
    for n in WEIGHT_ORDER:
        if n not in KINDS:
            shape = weights[n].shape
            out_g[n], out_d[n], out_m[n], out_v[n] = (t[n].reshape(shape) for t in (sg, sd, sm, sv))
    loss = sg["loss"].reshape(())
    return (loss, grad_x, *[out_g[n] for n in WEIGHT_ORDER], *[out_d[n] for n in WEIGHT_ORDER],
            *[out_m[n] for n in WEIGHT_ORDER], *[out_v[n] for n in WEIGHT_ORDER])
```

```python
import math

import jax
import jax.numpy as jnp
import numpy as np
from jax import lax
from jax.experimental import pallas as pl
from jax.experimental.pallas import tpu as pltpu

F32 = jnp.float32
BF16 = jnp.bfloat16
I32 = jnp.int32

SEQ = 2048
D_MODEL = 2048
HEAD_DIM = 128
N_HEADS = 8
N_GROUPS = 8
D_FF = 4 * D_MODEL
BLOCK = 128
DILATIONS = (1, 4, 16)
N_BUCKETS = 32
MAX_DISTANCE = 2048
ALPHA = 2.0 ** 0.25
LN_EPS = 1e-5
NEG_INF = -1e30
N_DEV = 8
N_CHIPS = 4
N_SUB = 16
ADAM_LR, ADAM_B1, ADAM_B2, ADAM_EPS, ADAM_WD, ADAM_STEP = 0.001, 0.9, 0.999, 1e-08, 0.01, 10
LANES = 128
SUBLANES = 8
VMEM_LIMIT = 56 * 1024 * 1024
MESH = pl.DeviceIdType.MESH
ANY = pl.BlockSpec(memory_space=pl.ANY)
WEIGHT_ORDER = ("w_in", "rel_bias", "ln_v_gain", "ln_v_bias", "w_spatial", "b_spatial", "w_proj_a", "w_proj_b", "w_out",
                "ln1_gain", "ln1_bias", "w_ff1", "b_ff1", "w_ff2", "b_ff2", "ln2_gain", "ln2_bias")
KINDS = {"w_in": "col", "w_proj_a": "col", "w_proj_b": "col", "w_out": "row", "w_ff1": "col", "w_ff2": "row"}


def _d_a():
    return N_HEADS * HEAD_DIM


def _d_b():
    return N_GROUPS * BLOCK


def _d_in():
    return 3 * _d_a() + 2 * _d_b() + 2 * D_MODEL


def _pick(t, n, *others):
    if n <= t and all(o % n == 0 for o in others):
        return n
    for c in range(min(t, n) // LANES * LANES, 0, -LANES):
        if n % c == 0 and all(o % c == 0 for o in others):
            return c
    raise ValueError((t, n, others))


class _Stage:
    def __init__(self, ins, outs, alias, sems, start, finish):
        self.ins, self.outs, self.alias, self.sems, self.start, self.finish = ins, outs, alias, sems, start, finish


def _call(name, body, grid, in_specs, out_specs, out_shape, operands, scratch=(), sem=None, stages=(), sequential=False):
    n_in, n_out, n_sc = len(in_specs), len(out_specs), len(scratch)
    st_in = [len(s.ins) for s in stages]
    st_out = [len(s.outs) for s in stages]
    st_sem = [len(s.sems) for s in stages]
    aliases, ioff, ooff = {}, n_in, n_out
    for s, ni, no in zip(stages, st_in, st_out):
        for i, o in s.alias.items():
            aliases[ioff + i] = ooff + o
        ioff, ooff = ioff + ni, ooff + no

    def split(refs, counts):
        out, at = [], 0
        for c in counts:
            out.append(refs[at:at + c])
            at += c
        return out

    def wrapped(*refs):
        ins, sins, outs, souts, sc, ssems = split(refs, [n_in, sum(st_in), n_out, sum(st_out), n_sc, sum(st_sem)])
        parts = list(zip(stages, split(sins, st_in), split(souts, st_out), split(ssems, st_sem)))
        if sequential:
            for s, a, b, c in parts:
                s.start(a, b, c)
                s.finish(a, b, c)
            return
        if parts:
            first = _all_of([pl.program_id(i) == 0 for i in range(len(grid))])
            last = _all_of([pl.program_id(i) == g - 1 for i, g in enumerate(grid)])

            @pl.when(first)
            def _():
                for s, a, b, c in parts:
                    s.start(a, b, c)

        body(*ins, *outs, *sc)
        if parts:
            @pl.when(last)
            def _():
                for s, a, b, c in parts:
                    s.finish(a, b, c)

    if stages or sem is None:
        sem = ("arbitrary",) * len(grid)
    res = pl.pallas_call(
        wrapped, name=name, grid=grid,
        in_specs=list(in_specs) + [ANY] * sum(st_in), out_specs=list(out_specs) + [ANY] * sum(st_out),
        out_shape=list(out_shape) + [o for s in stages for o in s.outs],
        scratch_shapes=list(scratch) + [x for s in stages for x in s.sems],
        input_output_aliases=aliases,
        compiler_params=pltpu.CompilerParams(dimension_semantics=sem, vmem_limit_bytes=VMEM_LIMIT),
    )(*operands, *[a for s in stages for a in s.ins])
    res = list(res)
    return res[:n_out], split(res[n_out:], st_out)


def _all_of(conds):
    out = conds[0]
    for c in conds[1:]:
        out = out & c
    return out


def _coords():
    return lax.axis_index("x"), lax.axis_index("y"), lax.axis_index("c")


def _other_chips(x, y):
    return ((1 - x, y), (x, 1 - y), (1 - x, 1 - y))


def _lin(dev):
    return 4 * dev[0] + 2 * dev[1] + dev[2]


def _piece(total, i, n):
    assert total % n == 0
    return i * (total // n), total // n


def _remote(src, dst, send, recv, to):
    return pltpu.make_async_remote_copy(src_ref=src, dst_ref=dst, send_sem=send, recv_sem=recv, device_id=to, device_id_type=MESH)


def _placer(kind, n, lo, cnt):
    def place(ref, dev):
        if kind == "col":
            return ref.at[pl.ds(lo, cnt), pl.ds(pl.multiple_of(_lin(dev) * n, LANES), n)]
        return ref.at[pl.ds(pl.multiple_of(_lin(dev) * n + lo, 2 * SUBLANES), cnt), :]
    return place


def _spread_stage(shard, full, kind, piece=(0, 1)):
    n = shard.shape[1] if kind == "col" else shard.shape[0]
    lo, cnt = _piece(shard.shape[0], *piece)
    place = _placer(kind, n, lo, cnt)

    def copies(ins, outs, sems):
        send, recv, local = sems
        x, y, c = _coords()
        me = (x, y, c)
        src = ins[0].at[pl.ds(lo, cnt), :]
        peers = [(x, y, 1 - c)] + [(*chip, c) for chip in _other_chips(x, y)]
        own = pltpu.make_async_copy(src, place(outs[0], me), local)
        out = [_remote(src, place(outs[0], me), send.at[k], recv.at[k], t) for k, t in enumerate(peers)]
        arrive = [_remote(src, place(outs[0], t), send.at[k], recv.at[k], t) for k, t in enumerate(peers)]
        return own, out, arrive

    def start(ins, outs, sems):
        own, out, _ = copies(ins, outs, sems)
        own.start()
        for cp in out:
            cp.start()

    def finish(ins, outs, sems):
        own, out, arrive = copies(ins, outs, sems)
        for cp in arrive:
            cp.wait_recv()
        for cp in out:
            cp.wait_send()
        own.wait()

    return _Stage([shard, full], [jax.ShapeDtypeStruct(full.shape, full.dtype)], {1: 0},
                  [pltpu.SemaphoreType.DMA((4,)), pltpu.SemaphoreType.DMA((4,)), pltpu.SemaphoreType.DMA], start, finish)


def _forward_stage(full, kind, piece=(0, 1)):
    n = (full.shape[1] if kind == "col" else full.shape[0]) // N_DEV
    lo, cnt = _piece(full.shape[0] if kind == "col" else n, *piece)
    place = _placer(kind, n, lo, cnt)

    def copies(ins, outs, sems):
        send, recv = sems
        x, y, c = _coords()
        chips = _other_chips(x, y)
        out = [_remote(place(outs[0], (*chip, c)), place(outs[0], (*chip, c)), send.at[k], recv.at[k], (x, y, 1 - c)) for k, chip in enumerate(chips)]
        arrive = [_remote(place(outs[0], (*chip, 1 - c)), place(outs[0], (*chip, 1 - c)), send.at[k], recv.at[k], (x, y, 1 - c))
                  for k, chip in enumerate(chips)]
        return out, arrive

    def start(ins, outs, sems):
        for cp in copies(ins, outs, sems)[0]:
            cp.start()

    def finish(ins, outs, sems):
        out, arrive = copies(ins, outs, sems)
        for cp in arrive:
            cp.wait_recv()
        for cp in out:
            cp.wait_send()

    return _Stage([full], [jax.ShapeDtypeStruct(full.shape, full.dtype)], {0: 0},
                  [pltpu.SemaphoreType.DMA((3,)), pltpu.SemaphoreType.DMA((3,))], start, finish)


def _to_sibling_stage(grad, kind):
    n = (grad.shape[1] if kind == "col" else grad.shape[0]) // N_DEV
    shape = (N_CHIPS, grad.shape[0], n) if kind == "col" else (N_CHIPS, n, grad.shape[1])
    place = _placer(kind, n, 0, grad.shape[0] if kind == "col" else n)

    def copies(ins, outs, sems):
        send, recv = sems
        x, y, c = _coords()
        return [_remote(place(ins[0], (q // 2, q % 2, 1 - c)), outs[0].at[q], send.at[q], recv.at[q], (x, y, 1 - c)) for q in range(N_CHIPS)]

    def start(ins, outs, sems):
        for cp in copies(ins, outs, sems):
            cp.start()

    def finish(ins, outs, sems):
        for cp in copies(ins, outs, sems):
            cp.wait()

    return _Stage([grad], [jax.ShapeDtypeStruct(shape, BF16)], {},
                  [pltpu.SemaphoreType.DMA((N_CHIPS,)), pltpu.SemaphoreType.DMA((N_CHIPS,))], start, finish)


def _to_chips_stage(pair, dst, piece=(0, 1)):
    lo, cnt = _piece(pair.shape[1], *piece)

    def copies(ins, outs, sems):
        send, recv, local = sems
        x, y, c = _coords()
        mine = 2 * x + y
        chips = _other_chips(x, y)
        own = pltpu.make_async_copy(ins[0].at[mine, pl.ds(lo, cnt), :], outs[0].at[mine, pl.ds(lo, cnt), :], local)
        out = [_remote(ins[0].at[2 * px + py, pl.ds(lo, cnt), :], outs[0].at[mine, pl.ds(lo, cnt), :], send.at[k], recv.at[k], (px, py, c))
               for k, (px, py) in enumerate(chips)]
        arrive = [_remote(ins[0].at[2 * px + py, pl.ds(lo, cnt), :], outs[0].at[2 * px + py, pl.ds(lo, cnt), :], send.at[k], recv.at[k], (px, py, c))
                  for k, (px, py) in enumerate(chips)]
        return own, out, arrive

    def start(ins, outs, sems):
        own, out, _ = copies(ins, outs, sems)
        own.start()
        for cp in out:
            cp.start()

    def finish(ins, outs, sems):
        own, out, arrive = copies(ins, outs, sems)
        for cp in arrive:
            cp.wait_recv()
        for cp in out:
            cp.wait_send()
        own.wait()

    return _Stage([pair, dst], [jax.ShapeDtypeStruct(dst.shape, dst.dtype)], {1: 0},
                  [pltpu.SemaphoreType.DMA((3,)), pltpu.SemaphoreType.DMA((3,)), pltpu.SemaphoreType.DMA], start, finish)


def _small_stage(part):
    def copies(ins, outs, sems):
        send, recv, local = sems
        x, y, c = _coords()
        me = (x, y, c)
        own = pltpu.make_async_copy(ins[0], outs[0].at[_lin(me)], local)
        peers = [(1 - x if k & 4 else x, 1 - y if k & 2 else y, 1 - c if k & 1 else c) for k in range(1, N_DEV)]
        out = [_remote(ins[0], outs[0].at[_lin(me)], send.at[k], recv.at[k], t) for k, t in enumerate(peers)]
        arrive = [_remote(ins[0], outs[0].at[_lin(t)], send.at[k], recv.at[k], t) for k, t in enumerate(peers)]
        return own, out, arrive

    def start(ins, outs, sems):
        own, out, _ = copies(ins, outs, sems)
        own.start()
        for cp in out:
            cp.start()

    def finish(ins, outs, sems):
        own, out, arrive = copies(ins, outs, sems)
        for cp in arrive:
            cp.wait_recv()
        for cp in out:
            cp.wait_send()
        own.wait()

    return _Stage([part], [jax.ShapeDtypeStruct((N_DEV, *part.shape), F32)], {},
                  [pltpu.SemaphoreType.DMA((N_DEV - 1,)), pltpu.SemaphoreType.DMA((N_DEV - 1,)), pltpu.SemaphoreType.DMA], start, finish)


def _comm_only(name, stages):
    return _call(name, lambda: None, (1,), [], [], [], [], stages=stages, sequential=True)[1]


_GELU_C = math.sqrt(2.0 / math.pi)


def _gelu(x):
    return 0.5 * x * (1.0 + jnp.tanh(_GELU_C * (x + 0.044715 * x * x * x)))


def _gelu_grad(x):
    t = jnp.tanh(_GELU_C * (x + 0.044715 * x * x * x))
    return 0.5 * (1.0 + t) + 0.5 * x * (1.0 - t * t) * (_GELU_C * (1.0 + 3.0 * 0.044715 * x * x))


def _sigmoid(x):
    return 1.0 / (1.0 + jnp.exp(-x))


def _dot(a, b, mode):
    dims = {"nn": (((1,), (0,)), ((), ())), "nt": (((1,), (1,)), ((), ())), "tn": (((0,), (0,)), ((), ()))}[mode]
    return lax.dot_general(a.astype(BF16), b.astype(BF16), dims, preferred_element_type=F32)


def _matmul(name, a, b, mode, outs, epi=None, extras=(), colsums=(), tm=2048, tn=512, tk=2048, b_off=0, n=None, m_off=0, m=None, stages=()):
    if mode == "tn":
        kk, mfull = a.shape
    else:
        mfull, kk = a.shape
    m = mfull if m is None else m
    n = (b.shape[0] if mode == "nt" else b.shape[1]) if n is None else n
    tm, tk = _pick(tm, m, m_off), _pick(tk, kk)
    tn = _pick(tn, n, b_off, *[off for _, _, off in extras])
    boff, moff = b_off // tn, m_off // tm
    nm, nn_, nk = m // tm, n // tn, kk // tk
    col_major = bool(colsums)
    grid = (nn_, nm, nk) if col_major else (nm, nn_, nk)

    def imap(f):
        if col_major:
            return lambda g0, g1, k: f(g1, g0, k)
        return f

    a_spec = (pl.BlockSpec((tk, tm), imap(lambda i, j, k: (k, i + moff))) if mode == "tn"
              else pl.BlockSpec((tm, tk), imap(lambda i, j, k: (i + moff, k))))
    b_spec = (pl.BlockSpec((tn, tk), imap(lambda i, j, k: (j + boff, k))) if mode == "nt"
              else pl.BlockSpec((tk, tn), imap(lambda i, j, k: (k, j + boff))))
    in_specs, operands = [a_spec, b_spec], [a, b]
    for arr, kind, off in extras:
        o = off // tn
        if kind == "mn":
            in_specs.append(pl.BlockSpec((tm, tn), imap(lambda i, j, k, o=o: (i + moff, j + o))))
        else:
            in_specs.append(pl.BlockSpec((1, tn), imap(lambda i, j, k, o=o: (0, j + o))))
        operands.append(arr)
    out_shape = [jax.ShapeDtypeStruct((m, n), dt) for dt in outs] + [jax.ShapeDtypeStruct((1, n), F32) for _ in colsums]
    out_specs = ([pl.BlockSpec((tm, tn), imap(lambda i, j, k: (i, j))) for _ in outs]
                 + [pl.BlockSpec((1, tn), imap(lambda i, j, k: (0, j))) for _ in colsums])
    n_ex, n_out, n_cs = len(extras), len(outs), len(colsums)

    def body(*refs):
        a_ref, b_ref = refs[:2]
        ex_refs = refs[2:2 + n_ex]
        out_refs = refs[2 + n_ex:2 + n_ex + n_out]
        cs_refs = refs[2 + n_ex + n_out:2 + n_ex + n_out + n_cs]
        part = _dot(a_ref[...], b_ref[...], mode)

        def finish(acc):
            res = epi(acc, *[r[...] for r in ex_refs]) if epi is not None else (acc,)
            for r, v in zip(out_refs, res[:n_out]):
                r[...] = v.astype(r.dtype)
            if n_cs:
                @pl.when(pl.program_id(1) == 0)
                def _():
                    for r in cs_refs:
                        r[...] = jnp.zeros_like(r)

                for r, idx in zip(cs_refs, colsums):
                    r[...] += jnp.sum(res[idx], axis=0, keepdims=True)

        if nk == 1:
            finish(part)
        else:
            acc_ref = refs[-1]
            k = pl.program_id(2)

            @pl.when(k == 0)
            def _():
                acc_ref[...] = part

            @pl.when(k > 0)
            def _():
                acc_ref[...] += part

            @pl.when(k == nk - 1)
            def _():
                finish(acc_ref[...])

    sem = ("arbitrary", "arbitrary", "arbitrary") if col_major else ("parallel", "parallel", "arbitrary")
    return _call(name, body, grid, in_specs, out_specs, out_shape, operands,
                 scratch=[pltpu.VMEM((tm, tn), F32)] if nk > 1 else [], sem=sem, stages=stages)


def _row_spec(tr, c):
    return pl.BlockSpec((tr, c), lambda i: (i, 0))


def _fix_spec(shape):
    return pl.BlockSpec(shape, lambda *_: tuple(0 for _ in shape))


def _cast_bf16(name, x, tr=512):
    r, c = x.shape
    tr = _pick(tr, r)

    def body(x_ref, o_ref):
        o_ref[...] = x_ref[...].astype(BF16)

    return _call(name, body, (r // tr,), [_row_spec(tr, c)], [_row_spec(tr, c)], [jax.ShapeDtypeStruct((r, c), BF16)], [x],
                 sem=("parallel",))[0][0]


def _layer_norm_stats(x):
    mean = jnp.mean(x, axis=-1, keepdims=True)
    xc = x - mean
    var = jnp.mean(xc * xc, axis=-1, keepdims=True)
    rstd = lax.rsqrt(var + LN_EPS)
    return xc * rstd, rstd


def _layer_norm_bwd(dxhat, xhat, rstd):
    m1 = jnp.mean(dxhat, axis=-1, keepdims=True)
    m2 = jnp.mean(dxhat * xhat, axis=-1, keepdims=True)
    return rstd * (dxhat - m1 - xhat * m2)


def _ln1_fwd(pre1, g1, b1, tr=256, stages=()):
    s, d = pre1.shape
    tr = _pick(tr, s)

    def body(p_ref, g_ref, b_ref, xh_ref, rs_ref, h_ref):
        xhat, rstd = _layer_norm_stats(p_ref[...])
        xh_ref[...] = xhat
        rs_ref[...] = rstd
        h_ref[...] = (xhat * g_ref[...] + b_ref[...]).astype(BF16)

    return _call("ln1_fwd", body, (s // tr,), [_row_spec(tr, d), _fix_spec((1, d)), _fix_spec((1, d))],
                 [_row_spec(tr, d), _row_spec(tr, 1), _row_spec(tr, d)],
                 [jax.ShapeDtypeStruct((s, d), F32), jax.ShapeDtypeStruct((s, 1), F32), jax.ShapeDtypeStruct((s, d), BF16)],
                 [pre1, g1, b1], sem=("parallel",), stages=stages)


def _ln2_loss_bwd(ff, xhat1, g1, b1, g2, b2, target, tr=256):
    s, d = ff.shape
    tr = _pick(tr, s)

    def body(ff_ref, xh1_ref, g1_ref, b1_ref, g2_ref, b2_ref, t_ref, dp_ref, dpb_ref, dg_ref, db_ref, dbf_ref, loss_ref):
        @pl.when(pl.program_id(0) == 0)
        def _():
            dg_ref[...] = jnp.zeros_like(dg_ref)
            db_ref[...] = jnp.zeros_like(db_ref)
            dbf_ref[...] = jnp.zeros_like(dbf_ref)
            loss_ref[...] = jnp.zeros_like(loss_ref)

        h1 = xh1_ref[...] * g1_ref[...] + b1_ref[...]
        xhat, rstd = _layer_norm_stats(ALPHA * h1 + ff_ref[...])
        err = xhat * g2_ref[...] + b2_ref[...] - t_ref[...]
        row = jnp.mean(err * err, axis=-1, keepdims=True)
        loss_ref[...] += 0.5 * jnp.sum(row, axis=0, keepdims=True)
        dy = err / d
        dg_ref[...] += jnp.sum(dy * xhat, axis=0, keepdims=True)
        db_ref[...] += jnp.sum(dy, axis=0, keepdims=True)
        dpre = _layer_norm_bwd(dy * g2_ref[...], xhat, rstd)
        dbf_ref[...] += jnp.sum(dpre, axis=0, keepdims=True)
        dp_ref[...] = dpre
        dpb_ref[...] = dpre.astype(BF16)

    vec = _fix_spec((1, d))
    return _call("ln2_loss_bwd", body, (s // tr,), [_row_spec(tr, d), _row_spec(tr, d), vec, vec, vec, vec, _row_spec(tr, d)],
                 [_row_spec(tr, d), _row_spec(tr, d), vec, vec, vec, _fix_spec((1, 1))],
                 [jax.ShapeDtypeStruct((s, d), F32), jax.ShapeDtypeStruct((s, d), BF16)]
                 + [jax.ShapeDtypeStruct((1, d), F32)] * 3 + [jax.ShapeDtypeStruct((1, 1), F32)],
                 [ff, xhat1, g1, b1, g2, b2, target])[0]


def _ln1_bwd(dh1, xhat1, rstd1, g1, tr=256, stages=()):
    s, d = dh1.shape
    tr = _pick(tr, s)

    def body(dh_ref, xh_ref, rs_ref, g_ref, dp_ref, dpb_ref, dg_ref, db_ref):
        @pl.when(pl.program_id(0) == 0)
        def _():
            dg_ref[...] = jnp.zeros_like(dg_ref)
            db_ref[...] = jnp.zeros_like(db_ref)

        dh, xhat = dh_ref[...], xh_ref[...]
        dg_ref[...] += jnp.sum(dh * xhat, axis=0, keepdims=True)
        db_ref[...] += jnp.sum(dh, axis=0, keepdims=True)
        dpre = _layer_norm_bwd(dh * g_ref[...], xhat, rs_ref[...])
        dp_ref[...] = dpre
        dpb_ref[...] = dpre.astype(BF16)

    vec = _fix_spec((1, d))
    return _call("ln1_bwd", body, (s // tr,), [_row_spec(tr, d), _row_spec(tr, d), _row_spec(tr, 1), vec],
                 [_row_spec(tr, d), _row_spec(tr, d), vec, vec],
                 [jax.ShapeDtypeStruct((s, d), F32), jax.ShapeDtypeStruct((s, d), BF16)] + [jax.ShapeDtypeStruct((1, d), F32)] * 2,
                 [dh1, xhat1, rstd1, g1], stages=stages)


def _to_perm(x):
    return x.reshape(SEQ // N_SUB, N_SUB, -1).transpose(1, 0, 2).reshape(SEQ, -1)


def _from_perm(x):
    return x.reshape(N_SUB, SEQ // N_SUB, -1).transpose(1, 0, 2).reshape(SEQ, -1)


def _local_index(p):
    rho = np.arange(BLOCK)
    if p == 0:
        return 16 * (rho % 8) + rho // 8
    if p == 1:
        return 4 * (rho % 32) + rho // 32
    return rho


def _tile_view(x, p):
    c = x.shape[1]
    if p == 1:
        return x.reshape(4, 4, BLOCK, c)
    return x.reshape(N_SUB, BLOCK, c)


def _view_shape(c, p):
    return (4, 4, BLOCK, c) if p == 1 else (N_SUB, BLOCK, c)


def _tile_spec(p, width, col, shift=0):
    nblk = SEQ // DILATIONS[p] // BLOCK

    def blk(n):
        return jnp.clip(n + shift, 0, nblk - 1)

    if p == 0:
        return pl.BlockSpec((N_SUB, SUBLANES, width), lambda s, n: (0, blk(n), col))
    if p == 1:
        return pl.BlockSpec((4, None, 32, width), lambda s, n: (0, s, blk(n), col))
    return pl.BlockSpec((None, BLOCK, width), lambda s, n: (s, 0, col))


def _tile_grid(p):
    return ((1, 16), (4, 4), (16, 1))[p]


def _t5_bucket(n):
    max_exact = N_BUCKETS // 2
    nf = np.maximum(n, 1).astype(np.float32)
    large = max_exact + (np.log(nf / np.float32(max_exact)) / np.float32(math.log(MAX_DISTANCE / max_exact))
                         * np.float32(N_BUCKETS - max_exact)).astype(np.int32)
    large = np.minimum(large, N_BUCKETS - 1)
    return np.where(n < max_exact, n, large).astype(np.int32)


def _bucket_tables():
    tabs = np.zeros((3, 2, BLOCK, BLOCK), np.int32)
    for p, d in enumerate(DILATIONS):
        i = _local_index(p)
        diff = i[:, None] - i[None, :]
        tabs[p, 0] = np.where(diff <= 0, _t5_bucket((BLOCK + diff) * d), -1)
        tabs[p, 1] = np.where(diff >= 0, _t5_bucket(np.maximum(diff, 0) * d), -1)
    return tabs


def _bias_expand(rel_bias, buckets):
    nh = N_HEADS

    def body(rb_ref, bk_ref, o_ref):
        for w in range(2):
            bk = bk_ref[0, w]
            for h in range(nh):
                val = jnp.zeros((BLOCK, BLOCK), F32)
                for b in range(N_BUCKETS):
                    val = jnp.where(bk == b, rb_ref[b, h], val)
                o_ref[0, h, w] = jnp.where(bk < 0, NEG_INF, val)

    return _call("bias_expand", body, (3,),
                 [pl.BlockSpec(memory_space=pltpu.SMEM), pl.BlockSpec((1, 2, BLOCK, BLOCK), lambda p: (p, 0, 0, 0))],
                 [pl.BlockSpec((1, nh, 2, BLOCK, BLOCK), lambda p: (p, 0, 0, 0, 0))],
                 [jax.ShapeDtypeStruct((3, nh, 2, BLOCK, BLOCK), F32)], [rel_bias, buckets], sem=("parallel",))[0][0]


def _heads_to_lanes(cols):
    lane = lax.broadcasted_iota(I32, (BLOCK, LANES), 1)
    out = jnp.zeros((BLOCK, LANES), F32)
    for h, c in enumerate(cols):
        out = jnp.where(lane == h, c, out)
    return out


def _attn_fwd(qkv, bias, p, stages=()):
    d_a = _d_a()
    has_prev = SEQ // DILATIONS[p] // BLOCK > 1
    scale = HEAD_DIM ** -0.5
    view = _tile_view(qkv, p)

    def body(q_ref, kc_ref, kp_ref, vc_ref, vp_ref, b_ref, o_ref, l_ref):
        n = pl.program_id(1)
        q_all = q_ref[...].reshape(BLOCK, d_a).astype(BF16)
        kc_all = kc_ref[...].reshape(BLOCK, d_a).astype(BF16)
        vc_all = vc_ref[...].reshape(BLOCK, d_a).astype(BF16)
        if has_prev:
            kp_all = kp_ref[...].reshape(BLOCK, d_a).astype(BF16)
            vp_all = vp_ref[...].reshape(BLOCK, d_a).astype(BF16)
        outs, lses = [], []
        for h in range(N_HEADS):
            sl = slice(h * HEAD_DIM, (h + 1) * HEAD_DIM)
            q = q_all[:, sl]
            sc = _dot(q, kc_all[:, sl], "nt") * scale + b_ref[0, h, 1]
            m = jnp.max(sc, axis=-1, keepdims=True)
            if has_prev:
                sp = _dot(q, kp_all[:, sl], "nt") * scale + b_ref[0, h, 0]
                sp = jnp.where(n > 0, sp, NEG_INF)
                m = jnp.maximum(m, jnp.max(sp, axis=-1, keepdims=True))
                pp = jnp.exp(sp - m)
            pc = jnp.exp(sc - m)
            den = jnp.sum(pc, axis=-1, keepdims=True)
            o = _dot(pc, vc_all[:, sl], "nn")
            if has_prev:
                den = den + jnp.sum(pp, axis=-1, keepdims=True)
                o = o + _dot(pp, vp_all[:, sl], "nn")
            outs.append(o / den)
            lses.append(m + jnp.log(den))
        o_ref[...] = jnp.concatenate(outs, axis=-1).reshape(o_ref.shape)
        l_ref[...] = _heads_to_lanes(lses).reshape(l_ref.shape)

    (o, l), st = _call(
        f"attn_fwd{p}", body, _tile_grid(p),
        [_tile_spec(p, d_a, 0), _tile_spec(p, d_a, 1), _tile_spec(p, d_a, 1, -1), _tile_spec(p, d_a, 2), _tile_spec(p, d_a, 2, -1),
         pl.BlockSpec((1, N_HEADS, 2, BLOCK, BLOCK), lambda s, n: (p, 0, 0, 0, 0))],
        [_tile_spec(p, d_a, 0), _tile_spec(p, LANES, 0)],
        [jax.ShapeDtypeStruct(_view_shape(d_a, p), F32), jax.ShapeDtypeStruct(_view_shape(LANES, p), F32)],
        [view, view, view, view, view, bias], sem=("parallel", "parallel"), stages=stages)
    return (o.reshape(SEQ, d_a), l.reshape(SEQ, LANES)), st


def _attn_combine(os_, ls_, tr=256):
    d_a = _d_a()
    tr = _pick(tr, SEQ)

    def body(o0, o1, o2, l0, l1, l2, a_ref, ab_ref, lt_ref):
        l = [l0[...], l1[...], l2[...]]
        m = jnp.maximum(jnp.maximum(l[0], l[1]), l[2])
        w = [jnp.exp(x - m) for x in l]
        tot = w[0] + w[1] + w[2]
        lt_ref[...] = m + jnp.log(tot)
        w = [x / tot for x in w]
        for h in range(N_HEADS):
            sl = slice(h * HEAD_DIM, (h + 1) * HEAD_DIM)
            acc = w[0][:, h:h + 1] * o0[:, sl] + w[1][:, h:h + 1] * o1[:, sl] + w[2][:, h:h + 1] * o2[:, sl]
            a_ref[:, sl] = acc
            ab_ref[:, sl] = acc.astype(BF16)

    return _call("attn_combine", body, (SEQ // tr,), [_row_spec(tr, d_a)] * 3 + [_row_spec(tr, LANES)] * 3,
                 [_row_spec(tr, d_a), _row_spec(tr, d_a), _row_spec(tr, LANES)],
                 [jax.ShapeDtypeStruct((SEQ, d_a), F32), jax.ShapeDtypeStruct((SEQ, d_a), BF16), jax.ShapeDtypeStruct((SEQ, LANES), F32)],
                 [*os_, *ls_], sem=("parallel",))[0]


def _attn_delta(dattn, attn, tr=256):
    d_a = _d_a()
    tr = _pick(tr, SEQ)

    def body(d_ref, a_ref, o_ref):
        prod = d_ref[...] * a_ref[...]
        lane = lax.broadcasted_iota(I32, (tr, LANES), 1)
        out = jnp.zeros((tr, LANES), F32)
        for h in range(N_HEADS):
            out = jnp.where(lane == h, jnp.sum(prod[:, h * HEAD_DIM:(h + 1) * HEAD_DIM], axis=-1, keepdims=True), out)
        o_ref[...] = out

    return _call("attn_delta", body, (SEQ // tr,), [_row_spec(tr, d_a)] * 2, [_row_spec(tr, LANES)],
                 [jax.ShapeDtypeStruct((SEQ, LANES), F32)], [dattn, attn], sem=("parallel",))[0][0]


def _attn_bwd(qkv, dattn, lse, delta, bias, p, stages=()):
    d_a = _d_a()
    nblk = SEQ // DILATIONS[p] // BLOCK
    has_next = nblk > 1
    scale = HEAD_DIM ** -0.5
    qv, dov, lv, tv = (_tile_view(x, p) for x in (qkv, dattn, lse, delta))

    def body(q_ref, qn_ref, k_ref, v_ref, do_ref, don_ref, l_ref, ln_ref, t_ref, tn_ref, b_ref, dq_ref, dk_ref, dv_ref, db_ref, carry_ref):
        j = pl.program_id(1)

        @pl.when((pl.program_id(0) == 0) & (j == 0))
        def _():
            db_ref[...] = jnp.zeros_like(db_ref)

        k_all = k_ref[...].reshape(BLOCK, d_a).astype(BF16)
        v_all = v_ref[...].reshape(BLOCK, d_a).astype(BF16)

        def side(qr, dor, lr, tr_, w):
            q_all = qr[...].reshape(BLOCK, d_a).astype(BF16)
            do_all = dor[...].reshape(BLOCK, d_a).astype(BF16)
            l_all = lr[...].reshape(BLOCK, LANES)
            t_all = tr_[...].reshape(BLOCK, LANES)
            dqs, dks, dvs = [], [], []
            for h in range(N_HEADS):
                sl = slice(h * HEAD_DIM, (h + 1) * HEAD_DIM)
                s = _dot(q_all[:, sl], k_all[:, sl], "nt") * scale + b_ref[0, h, w]
                pr = jnp.exp(s - l_all[:, h:h + 1])
                dp = _dot(do_all[:, sl], v_all[:, sl], "nt")
                ds = pr * (dp - t_all[:, h:h + 1])
                db_ref[h, w] += ds
                dqs.append(_dot(ds, k_all[:, sl], "nn") * scale)
                dks.append(_dot(ds, q_all[:, sl], "tn") * scale)
                dvs.append(_dot(pr, do_all[:, sl], "tn"))
            return [jnp.concatenate(x, axis=-1) for x in (dqs, dks, dvs)]

        dq_c, dk_c, dv_c = side(q_ref, do_ref, l_ref, t_ref, 1)
        if has_next:
            dq_ref[...] = (jnp.where(j > 0, carry_ref[...], 0.0) + dq_c).reshape(dq_ref.shape)
            not_last = j < nblk - 1

            @pl.when(not_last)
            def _():
                dq_n, dk_n, dv_n = side(qn_ref, don_ref, ln_ref, tn_ref, 0)
                carry_ref[...] = dq_n
                dk_ref[...] = (dk_c + dk_n).reshape(dk_ref.shape)
                dv_ref[...] = (dv_c + dv_n).reshape(dv_ref.shape)

            @pl.when(jnp.logical_not(not_last))
            def _():
                dk_ref[...] = dk_c.reshape(dk_ref.shape)
                dv_ref[...] = dv_c.reshape(dv_ref.shape)
        else:
            dq_ref[...] = dq_c.reshape(dq_ref.shape)
            dk_ref[...] = dk_c.reshape(dk_ref.shape)
            dv_ref[...] = dv_c.reshape(dv_ref.shape)

    def big(col, shift=0):
        return _tile_spec(p, d_a, col, shift)

    def small(shift=0):
        return _tile_spec(p, LANES, 0, shift)

    (dq, dk, dv, dbias), st = _call(
        f"attn_bwd{p}", body, _tile_grid(p),
        [big(0), big(0, 1), big(1), big(2), big(0), big(0, 1), small(), small(1), small(), small(1),
         pl.BlockSpec((1, N_HEADS, 2, BLOCK, BLOCK), lambda s, n: (p, 0, 0, 0, 0))],
        [big(0), big(0), big(0), pl.BlockSpec((N_HEADS, 2, BLOCK, BLOCK), lambda s, n: (0, 0, 0, 0))],
        [jax.ShapeDtypeStruct(_view_shape(d_a, p), F32)] * 3 + [jax.ShapeDtypeStruct((N_HEADS, 2, BLOCK, BLOCK), F32)],
        [qv, qv, qv, qv, dov, dov, lv, lv, tv, tv, bias], scratch=[pltpu.VMEM((BLOCK, d_a), F32)], stages=stages)
    return (dq.reshape(SEQ, d_a), dk.reshape(SEQ, d_a), dv.reshape(SEQ, d_a), dbias), st


def _rel_bias_grad(dbias, buckets):
    nh = N_HEADS

    def body(d0, d1, d2, bk_ref, o_ref, t_ref):
        ds = (d0, d1, d2)

        def per_bucket(b, carry):
            for h in range(nh):
                acc = jnp.zeros((BLOCK, BLOCK), F32)
                for p in range(3):
                    for w in range(2):
                        acc = acc + jnp.where(bk_ref[p, w] == b, ds[p][h, w], 0.0)
                t_ref[pl.ds(b * nh + h, 1), :] = jnp.sum(acc, axis=0, keepdims=True)
            return carry

        lax.fori_loop(0, N_BUCKETS, per_bucket, 0)
        o_ref[...] = jnp.sum(t_ref[...], axis=-1, keepdims=True)

    return _call("rel_bias_grad", body, (1,), [_fix_spec((nh, 2, BLOCK, BLOCK))] * 3 + [_fix_spec((3, 2, BLOCK, BLOCK))],
                 [_fix_spec((N_BUCKETS * nh, 1))], [jax.ShapeDtypeStruct((N_BUCKETS * nh, 1), F32)], [*dbias, buckets],
                 scratch=[pltpu.VMEM((N_BUCKETS * nh, LANES), F32)])[0][0]


def _gmlp_fwd(rest, gain, bias, ws, bs, causal, stages=()):
    d_b = _d_b()

    def body(u_ref, v_ref, g_ref, b_ref, ws_ref, bs_ref, c_ref, o_ref):
        u = u_ref[...].reshape(BLOCK, d_b)
        xhat, _ = _layer_norm_stats(_gelu(v_ref[...].reshape(BLOCK, d_b)))
        vn = (xhat * g_ref[...] + b_ref[...]).astype(BF16)
        outs = []
        for g in range(N_GROUPS):
            sl = slice(g * BLOCK, (g + 1) * BLOCK)
            w = jnp.where(c_ref[...] > 0, ws_ref[g], 0.0)
            z = _dot(w, vn[:, sl], "nn") + bs_ref[:, g:g + 1]
            outs.append(_gelu(u[:, sl]) * z)
        o_ref[...] = jnp.concatenate(outs, axis=-1).reshape(o_ref.shape)

    (out,), st = _call(
        "gmlp_fwd", body, (1, SEQ // BLOCK),
        [_tile_spec(0, d_b, 0), _tile_spec(0, d_b, 1), _fix_spec((1, d_b)), _fix_spec((1, d_b)),
         _fix_spec((N_GROUPS, BLOCK, BLOCK)), _fix_spec((BLOCK, N_GROUPS)), _fix_spec((BLOCK, BLOCK))],
        [_tile_spec(0, d_b, 0)], [jax.ShapeDtypeStruct(_view_shape(d_b, 0), F32)],
        [_tile_view(rest, 0), _tile_view(rest, 0), gain, bias, ws, bs, causal], sem=("parallel", "parallel"), stages=stages)
    return out.reshape(SEQ, d_b), st


def _gmlp_bwd(rest, dgmlp, gain, bias, ws, bs, causal, stages=()):
    d_b = _d_b()
    nchunk = SEQ // BLOCK

    def body(u_ref, v_ref, dg_ref, g_ref, b_ref, ws_ref, bs_ref, c_ref, du_ref, dv_ref, dws_ref, dbs_ref, dgain_ref, dbias_ref):
        c = pl.program_id(1)

        @pl.when(c == 0)
        def _():
            dws_ref[...] = jnp.zeros_like(dws_ref)
            dbs_ref[...] = jnp.zeros_like(dbs_ref)
            dgain_ref[...] = jnp.zeros_like(dgain_ref)
            dbias_ref[...] = jnp.zeros_like(dbias_ref)

        u = u_ref[...].reshape(BLOCK, d_b)
        v = v_ref[...].reshape(BLOCK, d_b)
        dgm = dg_ref[...].reshape(BLOCK, d_b)
        xhat, rstd = _layer_norm_stats(_gelu(v))
        vn = (xhat * g_ref[...] + b_ref[...]).astype(BF16)
        lane = lax.broadcasted_iota(I32, (BLOCK, LANES), 1)
        dus, dvns = [], []
        dbs = dbs_ref[...]
        for g in range(N_GROUPS):
            sl = slice(g * BLOCK, (g + 1) * BLOCK)
            w = jnp.where(c_ref[...] > 0, ws_ref[g], 0.0).astype(BF16)
            z = _dot(w, vn[:, sl], "nn") + bs_ref[:, g:g + 1]
            dz = dgm[:, sl] * _gelu(u[:, sl])
            dus.append(dgm[:, sl] * z * _gelu_grad(u[:, sl]))
            dws_ref[g] += _dot(dz, vn[:, sl], "nt")
            dbs = dbs + jnp.where(lane == g, jnp.sum(dz, axis=-1, keepdims=True), 0.0)
            dvns.append(_dot(w, dz, "tn"))
        dbs_ref[...] = dbs
        dvn = jnp.concatenate(dvns, axis=-1)
        dgain_ref[...] += jnp.sum(dvn * xhat, axis=0, keepdims=True)
        dbias_ref[...] += jnp.sum(dvn, axis=0, keepdims=True)
        dvg = _layer_norm_bwd(dvn * g_ref[...], xhat, rstd)
        du_ref[...] = jnp.concatenate(dus, axis=-1).reshape(du_ref.shape)
        dv_ref[...] = (dvg * _gelu_grad(v)).reshape(dv_ref.shape)

        @pl.when(c == nchunk - 1)
        def _():
            for g in range(N_GROUPS):
                dws_ref[g] = jnp.where(c_ref[...] > 0, dws_ref[g], 0.0)

    (du, dv, dws, dbs, dgain, dbias), st = _call(
        "gmlp_bwd", body, (1, nchunk),
        [_tile_spec(0, d_b, 0), _tile_spec(0, d_b, 1), _tile_spec(0, d_b, 0), _fix_spec((1, d_b)), _fix_spec((1, d_b)),
         _fix_spec((N_GROUPS, BLOCK, BLOCK)), _fix_spec((BLOCK, N_GROUPS)), _fix_spec((BLOCK, BLOCK))],
        [_tile_spec(0, d_b, 0), _tile_spec(0, d_b, 0), _fix_spec((N_GROUPS, BLOCK, BLOCK)), _fix_spec((BLOCK, LANES)),
         _fix_spec((1, d_b)), _fix_spec((1, d_b))],
        [jax.ShapeDtypeStruct(_view_shape(d_b, 0), F32)] * 2
        + [jax.ShapeDtypeStruct((N_GROUPS, BLOCK, BLOCK), F32), jax.ShapeDtypeStruct((BLOCK, LANES), F32)]
        + [jax.ShapeDtypeStruct((1, d_b), F32)] * 2,
        [_tile_view(rest, 0), _tile_view(rest, 0), _tile_view(dgmlp, 0), gain, bias, ws, bs, causal], stages=stages)
    return (du.reshape(SEQ, d_b), dv.reshape(SEQ, d_b), dws, dbs, dgain, dbias), st


def _assemble_dproj(dqkv, du, dv, dga, dgb, tr=128):
    d_a, d_b, d_in = _d_a(), _d_b(), _d_in()
    tr = _pick(tr, SEQ)

    def body(*refs):
        att, (du_ref, dv_ref, dga_ref, dgb_ref, o_ref) = refs[:9], refs[9:]
        for i in range(3):
            o_ref[:, i * d_a:(i + 1) * d_a] = (att[3 * i][...] + att[3 * i + 1][...] + att[3 * i + 2][...]).astype(BF16)
        o_ref[:, 3 * d_a:3 * d_a + d_b] = du_ref[...].astype(BF16)
        o_ref[:, 3 * d_a + d_b:3 * d_a + 2 * d_b] = dv_ref[...].astype(BF16)
        o_ref[:, 3 * d_a + 2 * d_b:3 * d_a + 2 * d_b + D_MODEL] = dga_ref[...]
        o_ref[:, 3 * d_a + 2 * d_b + D_MODEL:] = dgb_ref[...]

    return _call("assemble_dproj", body, (SEQ // tr,), [_row_spec(tr, d_a)] * 9 + [_row_spec(tr, d_b)] * 2 + [_row_spec(tr, D_MODEL)] * 2,
                 [_row_spec(tr, d_in)], [jax.ShapeDtypeStruct((SEQ, d_in), BF16)], [*dqkv, du, dv, dga, dgb], sem=("parallel",))[0][0]


def _pair_sum(name, grad, recv, kind, core, tr=256):
    _, rs, cs = recv.shape
    tr = _pick(tr, rs)
    nb = rs // tr
    if kind == "col":
        g_spec = pl.BlockSpec((tr, cs), lambda q, i, c_ref: (i, 2 * q + c_ref[0]))
    else:
        g_spec = pl.BlockSpec((tr, cs), lambda q, i, c_ref: ((2 * q + c_ref[0]) * nb + i, 0))
    r_spec = pl.BlockSpec((None, tr, cs), lambda q, i, c_ref: (q, i, 0))

    def body(c_ref, g_ref, r_ref, o_ref):
        o_ref[...] = (g_ref[...].astype(F32) + r_ref[...].astype(F32)).astype(BF16)

    return pl.pallas_call(
        body, name=name, out_shape=jax.ShapeDtypeStruct(recv.shape, BF16),
        grid_spec=pltpu.PrefetchScalarGridSpec(num_scalar_prefetch=1, grid=(N_CHIPS, nb), in_specs=[g_spec, r_spec], out_specs=r_spec),
        compiler_params=pltpu.CompilerParams(dimension_semantics=("parallel", "parallel"), vmem_limit_bytes=VMEM_LIMIT),
    )(core, grad, recv)


def _adamw(w, g, m, v):
    m = ADAM_B1 * m + (1.0 - ADAM_B1) * g
    v = ADAM_B2 * v + (1.0 - ADAM_B2) * (g * g)
    m_hat = m / (1.0 - ADAM_B1 ** ADAM_STEP)
    v_hat = v / (1.0 - ADAM_B2 ** ADAM_STEP)
    delta = -ADAM_LR * (m_hat / (jnp.sqrt(v_hat) + ADAM_EPS) + ADAM_WD * w)
    return delta, m, v


def _adam_shard(name, chip_sums, w, m, v, tr=256, stages=()):
    rs, cs = w.shape
    tr = _pick(tr, rs)

    def body(s_ref, w_ref, m_ref, v_ref, g_ref, d_ref, nm_ref, nv_ref):
        g = s_ref[0].astype(F32)
        for q in range(1, N_CHIPS):
            g = g + s_ref[q].astype(F32)
        d, nm, nv = _adamw(w_ref[...], g, m_ref[...], v_ref[...])
        g_ref[...], d_ref[...], nm_ref[...], nv_ref[...] = g, d, nm, nv

    spec = _row_spec(tr, cs)
    return _call(name, body, (rs // tr,), [pl.BlockSpec((N_CHIPS, tr, cs), lambda i: (0, i, 0)), spec, spec, spec], [spec] * 4,
                 [jax.ShapeDtypeStruct((rs, cs), F32)] * 4, [chip_sums, w, m, v], sem=("parallel",), stages=stages)


def _adam_small(parts, w, m, v):
    rows = w.shape[0]

    def body(p_ref, w_ref, m_ref, v_ref, g_ref, d_ref, nm_ref, nv_ref):
        g = p_ref[0]
        for j in range(1, N_DEV):
            g = g + p_ref[j]
        d, nm, nv = _adamw(w_ref[...], g, m_ref[...], v_ref[...])
        g_ref[...], d_ref[...], nm_ref[...], nv_ref[...] = g, d, nm, nv

    spec = _fix_spec((rows, LANES))
    return _call("adam_small", body, (1,), [_fix_spec((N_DEV, rows, LANES)), spec, spec, spec], [spec] * 4,
                 [jax.ShapeDtypeStruct((rows, LANES), F32)] * 4, [parts, w, m, v])[0]


def _small_sizes():
    d_b = _d_b()
    return (("loss", 1), ("rel_bias", N_BUCKETS * N_HEADS), ("ln_v_gain", d_b), ("ln_v_bias", d_b),
            ("w_spatial", N_GROUPS * BLOCK * BLOCK), ("b_spatial", N_GROUPS * BLOCK), ("ln1_gain", D_MODEL), ("ln1_bias", D_MODEL),
            ("b_ff1", D_FF), ("b_ff2", D_MODEL), ("ln2_gain", D_MODEL), ("ln2_bias", D_MODEL))


def _pack(vals):
    pieces = []
    for name, size in _small_sizes():
        flat = vals[name].reshape(-1).astype(F32)
        padded = -(-size // (SUBLANES * LANES)) * SUBLANES * LANES
        pieces.append(jnp.pad(flat, (0, padded - size)).reshape(-1, LANES))
    return jnp.concatenate(pieces, axis=0)


def _unpack(buf):
    out, row = {}, 0
    for name, size in _small_sizes():
        rows = -(-size // (SUBLANES * LANES)) * SUBLANES
        out[name] = buf[row:row + rows].reshape(-1)[:size]
        row += rows
    return out


def kernel(x, w_in, rel_bias, ln_v_gain, ln_v_bias, w_spatial, b_spatial, w_proj_a, w_proj_b, w_out, ln1_gain, ln1_bias, w_ff1, b_ff1, w_ff2, b_ff2, ln2_gain, ln2_bias, loss_target, m_w_in, m_rel_bias, m_ln_v_gain, m_ln_v_bias, m_w_spatial, m_b_spatial, m_w_proj_a, m_w_proj_b, m_w_out, m_ln1_gain, m_ln1_bias, m_w_ff1, m_b_ff1, m_w_ff2, m_b_ff2, m_ln2_gain, m_ln2_bias, v_w_in, v_rel_bias, v_ln_v_gain, v_ln_v_bias, v_w_spatial, v_b_spatial, v_w_proj_a, v_w_proj_b, v_w_out, v_ln1_gain, v_ln1_bias, v_w_ff1, v_b_ff1, v_w_ff2, v_b_ff2, v_ln2_gain, v_ln2_bias):
    d_a, d_b, d_in = _d_a(), _d_b(), _d_in()
    weights = dict(w_in=w_in, rel_bias=rel_bias, ln_v_gain=ln_v_gain, ln_v_bias=ln_v_bias, w_spatial=w_spatial, b_spatial=b_spatial,
                   w_proj_a=w_proj_a, w_proj_b=w_proj_b, w_out=w_out, ln1_gain=ln1_gain, ln1_bias=ln1_bias, w_ff1=w_ff1, b_ff1=b_ff1,
                   w_ff2=w_ff2, b_ff2=b_ff2, ln2_gain=ln2_gain, ln2_bias=ln2_bias)
    mom1 = dict(w_in=m_w_in, rel_bias=m_rel_bias, ln_v_gain=m_ln_v_gain, ln_v_bias=m_ln_v_bias, w_spatial=m_w_spatial,
                b_spatial=m_b_spatial, w_proj_a=m_w_proj_a, w_proj_b=m_w_proj_b, w_out=m_w_out, ln1_gain=m_ln1_gain,
                ln1_bias=m_ln1_bias, w_ff1=m_w_ff1, b_ff1=m_b_ff1, w_ff2=m_w_ff2, b_ff2=m_b_ff2, ln2_gain=m_ln2_gain, ln2_bias=m_ln2_bias)
    mom2 = dict(w_in=v_w_in, rel_bias=v_rel_bias, ln_v_gain=v_ln_v_gain, ln_v_bias=v_ln_v_bias, w_spatial=v_w_spatial,
                b_spatial=v_b_spatial, w_proj_a=v_w_proj_a, w_proj_b=v_w_proj_b, w_out=v_w_out, ln1_gain=v_ln1_gain,
                ln1_bias=v_ln1_bias, w_ff1=v_w_ff1, b_ff1=v_b_ff1, w_ff2=v_w_ff2, b_ff2=v_b_ff2, ln2_gain=v_ln2_gain, ln2_bias=v_ln2_bias)

    shard = {n: _cast_bf16(f"cast_{n}", weights[n][0]) for n in KINDS}
    full = {}
    for n, kind in KINDS.items():
        r, c = shard[n].shape
        full[n] = lax.empty((r, c * N_DEV) if kind == "col" else (r * N_DEV, c), BF16)

    def spread(n, piece=(0, 1)):
        return _spread_stage(shard[n], full[n], KINDS[n], piece)

    def forward(n):
        return _forward_stage(full[n], KINDS[n])

    def landed(names, outs):
        for n, o in zip(names, outs):
            full[n] = o[0]

    landed(["w_in"], _comm_only("gather_w_in_chips", [spread("w_in")]))
    landed(["w_in"], _comm_only("gather_w_in_sibling", [forward("w_in")]))

    xs = _to_perm(x[0])
    target = _to_perm(loss_target[0])
    xb = _cast_bf16("cast_x", xs)
    g8 = BLOCK // N_SUB
    ws_t = w_spatial[0].reshape(N_GROUPS, g8, N_SUB, g8, N_SUB).transpose(0, 2, 1, 4, 3).reshape(N_GROUPS, BLOCK, BLOCK)
    bs_t = b_spatial[0].reshape(N_GROUPS, g8, N_SUB).transpose(2, 1, 0).reshape(BLOCK, N_GROUPS)
    idx = _local_index(0)
    causal = jnp.asarray((idx[:, None] >= idx[None, :]).astype(np.float32))
    buckets = jnp.asarray(_bucket_tables())
    bias = _bias_expand(rel_bias, buckets)

    (qkv,), st = _matmul("proj_qkv", xb, full["w_in"], "nn", [F32], n=3 * d_a, stages=[spread("w_proj_a"), spread("w_proj_b")])
    landed(["w_proj_a", "w_proj_b"], st)
    (rest,), st = _matmul("proj_rest", xb, full["w_in"], "nn", [F32], b_off=3 * d_a, n=d_in - 3 * d_a,
                          stages=[forward("w_proj_a"), forward("w_proj_b"), spread("w_out")])
    landed(["w_proj_a", "w_proj_b", "w_out"], st)
    fwd = []
    for p in range(3):
        res, st = _attn_fwd(qkv, bias, p, stages=[spread("w_ff1", (p, 4))] + ([forward("w_out")] if p == 0 else []))
        landed(["w_ff1", "w_out"], st)
        fwd.append(res)
    attn, attn_b, lse = _attn_combine([o for o, _ in fwd], [l for _, l in fwd])
    gmlp, st = _gmlp_fwd(rest, ln_v_gain, ln_v_bias, ws_t, bs_t, causal, stages=[spread("w_ff1", (3, 4))])
    landed(["w_ff1"], st)
    (ya,), st = _matmul("proj_a", attn_b, full["w_proj_a"], "nn", [F32], stages=[forward("w_ff1")])
    landed(["w_ff1"], st)
    gate_a, gate_b = 2 * d_b, 2 * d_b + D_MODEL

    def merge(acc, ya_, ga, gb):
        return acc, _sigmoid(ga) * ya_ + _sigmoid(gb) * acc

    (yb, merged), st = _matmul("proj_b_merge", gmlp, full["w_proj_b"], "nn", [F32, BF16], merge,
                               [(ya, "mn", 0), (rest, "mn", gate_a), (rest, "mn", gate_b)], tn=256, stages=[spread("w_ff2", (0, 4))])
    landed(["w_ff2"], st)
    (pre1,), st = _matmul("out_proj", merged, full["w_out"], "nn", [F32], lambda acc, x_: (ALPHA * x_ + acc,), [(xs, "mn", 0)],
                          stages=[spread("w_ff2", (1, 4))])
    landed(["w_ff2"], st)
    (xhat1, rstd1, h1b), st = _ln1_fwd(pre1, ln1_gain, ln1_bias, stages=[spread("w_ff2", (2, 4))])
    landed(["w_ff2"], st)

    def relu2(acc, b_):
        r = jnp.maximum(acc + b_, 0.0)
        return r, r * r

    (relu, fb), st = _matmul("ff1", h1b, full["w_ff1"], "nn", [F32, BF16], relu2, [(b_ff1, "row", 0)], stages=[spread("w_ff2", (3, 4))])
    landed(["w_ff2"], st)
    landed(["w_ff2"], _comm_only("gather_w_ff2_sibling", [forward("w_ff2")]))
    (ff,), _ = _matmul("ff2", fb, full["w_ff2"], "nn", [F32], lambda acc, b_: (acc + b_,), [(b_ff2, "row", 0)], tn=1024, tk=1024)

    core = lax.axis_index("c").astype(I32).reshape(1)
    grads, sib, pair, chips = {}, {}, {}, {}

    def to_sibling(n):
        return _to_sibling_stage(grads[n], KINDS[n])

    def pair_up(n):
        pair[n] = _pair_sum(f"pair_sum_{n}", grads[n], sib[n], KINDS[n], core)
        chips[n] = lax.empty(pair[n].shape, BF16)

    def to_chips(n, piece=(0, 1)):
        return _to_chips_stage(pair[n], chips[n], piece)

    dpre2, dpre2b, g_ln2_gain, g_ln2_bias, g_b_ff2, loss_part = _ln2_loss_bwd(ff, xhat1, ln1_gain, ln1_bias, ln2_gain, ln2_bias, target)
    (grads["w_ff2"],), _ = _matmul("dw_ff2", fb, dpre2b, "tn", [BF16])

    def relu2_bwd(acc, r):
        da = acc * (2.0 * r)
        return da, da

    (dab, g_b_ff1), st = _matmul("d_ff1", dpre2b, full["w_ff2"], "nt", [BF16], relu2_bwd, [(relu, "mn", 0)], colsums=(1,),
                                 stages=[to_sibling("w_ff2")])
    sib["w_ff2"] = st[0][0]
    pair_up("w_ff2")
    (grads["w_ff1"],), st = _matmul("dw_ff1", h1b, dab, "tn", [BF16], stages=[to_chips("w_ff2", (0, 2))])
    chips["w_ff2"] = st[0][0]
    (dh1,), st = _matmul("d_h1", dab, full["w_ff1"], "nt", [F32], lambda acc, d_: (acc + ALPHA * d_,), [(dpre2, "mn", 0)],
                         stages=[to_chips("w_ff2", (1, 2)), to_sibling("w_ff1")])
    chips["w_ff2"], sib["w_ff1"] = st[0][0], st[1][0]
    pair_up("w_ff1")
    (dpre1, dpre1b, g_ln1_gain, g_ln1_bias), st = _ln1_bwd(dh1, xhat1, rstd1, ln1_gain, stages=[to_chips("w_ff1", (0, 4))])
    chips["w_ff1"] = st[0][0]
    (grads["w_out"],), st = _matmul("dw_out", merged, dpre1b, "tn", [BF16], stages=[to_chips("w_ff1", (1, 4))])
    chips["w_ff1"] = st[0][0]

    def merge_bwd(acc, ga, gb, ya_, yb_):
        sa, sb = _sigmoid(ga), _sigmoid(gb)
        return acc * sa, acc * sb, acc * ya_ * (sa * (1.0 - sa)), acc * yb_ * (sb * (1.0 - sb))

    (dya, dyb, dga, dgb), st = _matmul("d_merge", dpre1b, full["w_out"], "nt", [BF16] * 4, merge_bwd,
                                       [(rest, "mn", gate_a), (rest, "mn", gate_b), (ya, "mn", 0), (yb, "mn", 0)], tn=256,
                                       stages=[to_chips("w_ff1", (2, 4)), to_sibling("w_out")])
    chips["w_ff1"], sib["w_out"] = st[0][0], st[1][0]
    pair_up("w_out")
    (grads["w_proj_a"],), st = _matmul("dw_proj_a", attn_b, dya, "tn", [BF16], stages=[to_chips("w_ff1", (3, 4))])
    chips["w_ff1"] = st[0][0]
    (grads["w_proj_b"],), st = _matmul("dw_proj_b", gmlp, dyb, "tn", [BF16], stages=[to_chips("w_out")])
    chips["w_out"] = st[0][0]
    (dattn,), st = _matmul("d_attn", dya, full["w_proj_a"], "nt", [F32], stages=[to_sibling("w_proj_a"), to_sibling("w_proj_b")])
    sib["w_proj_a"], sib["w_proj_b"] = st[0][0], st[1][0]
    pair_up("w_proj_a")
    pair_up("w_proj_b")
    (dgmlp,), st = _matmul("d_gmlp", dyb, full["w_proj_b"], "nt", [F32], stages=[to_chips("w_proj_a"), to_chips("w_proj_b")])
    chips["w_proj_a"], chips["w_proj_b"] = st[0][0], st[1][0]
    (du, dvb, dws_t, dbs_t, g_lnv_gain, g_lnv_bias), _ = _gmlp_bwd(rest, dgmlp, ln_v_gain, ln_v_bias, ws_t, bs_t, causal)
    delta = _attn_delta(dattn, attn)
    bwd = [_attn_bwd(qkv, dattn, lse, delta, bias, p)[0] for p in range(3)]
    g_rel_bias = _rel_bias_grad([b[3] for b in bwd], buckets)
    dproj = _assemble_dproj([b[i] for i in range(3) for b in bwd], du, dvb, dga, dgb)

    g_w_spatial = dws_t.reshape(N_GROUPS, N_SUB, g8, N_SUB, g8).transpose(0, 2, 1, 4, 3)
    g_b_spatial = dbs_t[:, :N_GROUPS].reshape(N_SUB, g8, N_GROUPS).transpose(2, 1, 0)
    part = _pack(dict(loss=loss_part, rel_bias=g_rel_bias, ln_v_gain=g_lnv_gain, ln_v_bias=g_lnv_bias, w_spatial=g_w_spatial,
                      b_spatial=g_b_spatial, ln1_gain=g_ln1_gain, ln1_bias=g_ln1_bias, b_ff1=g_b_ff1, b_ff2=g_b_ff2,
                      ln2_gain=g_ln2_gain, ln2_bias=g_ln2_bias))
    (grads["w_in"],), st = _matmul("dw_in", xb, dproj, "tn", [BF16], stages=[_small_stage(part)])
    parts = st[0][0]
    half = SEQ // 2

    def add_residual(acc, d_):
        return (acc + ALPHA * d_,)

    (dx0,), st = _matmul("d_x0", dproj, full["w_in"], "nt", [F32], add_residual, [(dpre1, "mn", 0)], tm=1024, tn=1024, tk=3072, m=half,
                         stages=[to_sibling("w_in")])
    sib["w_in"] = st[0][0]
    pair_up("w_in")
    (dx1,), st = _matmul("d_x1", dproj, full["w_in"], "nt", [F32], add_residual, [(dpre1, "mn", 0)], tm=1024, tn=1024, tk=3072, m_off=half, m=half,
                         stages=[to_chips("w_in", (0, 2))])
    chips["w_in"] = st[0][0]
    grad_x = _from_perm(jnp.concatenate([dx0, dx1], axis=0))[None]

    out_g, out_d, out_m, out_v = {}, {}, {}, {}
    for n in ("w_ff2", "w_ff1", "w_out", "w_proj_a", "w_proj_b", "w_in"):
        stages = [to_chips("w_in", (1, 2))] if n == "w_ff2" else []
        (g, d, nm, nv), st = _adam_shard(f"adam_{n}", chips[n], weights[n][0], mom1[n][0], mom2[n][0], stages=stages)
        if stages:
            chips["w_in"] = st[0][0]
        out_g[n], out_d[n], out_m[n], out_v[n] = g[None], d[None], nm[None], nv[None]

    zero = jnp.zeros((1,), F32)
    sg, sd, sm, sv = (_unpack(b) for b in _adam_small(
        parts, _pack({**weights, "loss": zero}), _pack({**mom1, "loss": zero}), _pack({**mom2, "loss": zero})))
    for n in WEIGHT_ORDER:
        if n not in KINDS:
            shape = weights[n].shape
            out_g[n], out_d[n], out_m[n], out_v[n] = (t[n].reshape(shape) for t in (sg, sd, sm, sv))
    loss = sg["loss"].reshape(())
    return (loss, grad_x, *[out_g[n] for n in WEIGHT_ORDER], *[out_d[n] for n in WEIGHT_ORDER],
            *[out_m[n] for n in WEIGHT_ORDER], *[out_v[n] for n in WEIGHT_ORDER])
```

```python
import math

import jax
import jax.numpy as jnp
import numpy as np
from jax import lax
from jax.experimental import pallas as pl
from jax.experimental.pallas import tpu as pltpu

F32 = jnp.float32
BF16 = jnp.bfloat16
I32 = jnp.int32

SEQ = 2048
D_MODEL = 2048
HEAD_DIM = 128
N_HEADS = 8
N_GROUPS = 8
D_FF = 4 * D_MODEL
BLOCK = 128
DILATIONS = (1, 4, 16)
N_BUCKETS = 32
MAX_DISTANCE = 2048
ALPHA = 2.0 ** 0.25
LN_EPS = 1e-5
NEG_INF = -1e30
N_DEV = 8
N_CHIPS = 4
N_SUB = 16
ADAM_LR, ADAM_B1, ADAM_B2, ADAM_EPS, ADAM_WD, ADAM_STEP = 0.001, 0.9, 0.999, 1e-08, 0.01, 10
LANES = 128
SUBLANES = 8
VMEM_LIMIT = 56 * 1024 * 1024
MESH = pl.DeviceIdType.MESH
ANY = pl.BlockSpec(memory_space=pl.ANY)
WEIGHT_ORDER = ("w_in", "rel_bias", "ln_v_gain", "ln_v_bias", "w_spatial", "b_spatial", "w_proj_a", "w_proj_b", "w_out",
                "ln1_gain", "ln1_bias", "w_ff1", "b_ff1", "w_ff2", "b_ff2", "ln2_gain", "ln2_bias")
KINDS = {"w_in": "col", "w_proj_a": "col", "w_proj_b": "col", "w_out": "row", "w_ff1": "col", "w_ff2": "row"}


def _d_a():
    return N_HEADS * HEAD_DIM


def _d_b():
    return N_GROUPS * BLOCK


def _d_in():
    return 3 * _d_a() + 2 * _d_b() + 2 * D_MODEL


def _pick(t, n, *others):
    if n <= t and all(o % n == 0 for o in others):
        return n
    for c in range(min(t, n) // LANES * LANES, 0, -LANES):
        if n % c == 0 and all(o % c == 0 for o in others):
            return c
    raise ValueError((t, n, others))


class _Stage:
    def __init__(self, ins, outs, alias, sems, start, finish):
        self.ins, self.outs, self.alias, self.sems, self.start, self.finish = ins, outs, alias, sems, start, finish


def _call(name, body, grid, in_specs, out_specs, out_shape, operands, scratch=(), sem=None, stages=(), sequential=False):
    n_in, n_out, n_sc = len(in_specs), len(out_specs), len(scratch)
    st_in = [len(s.ins) for s in stages]
    st_out = [len(s.outs) for s in stages]
    st_sem = [len(s.sems) for s in stages]
    aliases, ioff, ooff = {}, n_in, n_out
    for s, ni, no in zip(stages, st_in, st_out):
        for i, o in s.alias.items():
            aliases[ioff + i] = ooff + o
        ioff, ooff = ioff + ni, ooff + no

    def split(refs, counts):
        out, at = [], 0
        for c in counts:
            out.append(refs[at:at + c])
            at += c
        return out

    def wrapped(*refs):
        ins, sins, outs, souts, sc, ssems = split(refs, [n_in, sum(st_in), n_out, sum(st_out), n_sc, sum(st_sem)])
        parts = list(zip(stages, split(sins, st_in), split(souts, st_out), split(ssems, st_sem)))
        if sequential:
            for s, a, b, c in parts:
                s.start(a, b, c)
                s.finish(a, b, c)
            return
        if parts:
            first = _all_of([pl.program_id(i) == 0 for i in range(len(grid))])
            last = _all_of([pl.program_id(i) == g - 1 for i, g in enumerate(grid)])

            @pl.when(first)
            def _():
                for s, a, b, c in parts:
                    s.start(a, b, c)

        body(*ins, *outs, *sc)
        if parts:
            @pl.when(last)
            def _():
                for s, a, b, c in parts:
                    s.finish(a, b, c)

    if stages or sem is None:
        sem = ("arbitrary",) * len(grid)
    res = pl.pallas_call(
        wrapped, name=name, grid=grid,
        in_specs=list(in_specs) + [ANY] * sum(st_in), out_specs=list(out_specs) + [ANY] * sum(st_out),
        out_shape=list(out_shape) + [o for s in stages for o in s.outs],
        scratch_shapes=list(scratch) + [x for s in stages for x in s.sems],
        input_output_aliases=aliases,
        compiler_params=pltpu.CompilerParams(dimension_semantics=sem, vmem_limit_bytes=VMEM_LIMIT),
    )(*operands, *[a for s in stages for a in s.ins])
    res = list(res)
    return res[:n_out], split(res[n_out:], st_out)


def _all_of(conds):
    out = conds[0]
    for c in conds[1:]:
        out = out & c
    return out


def _coords():
    return lax.axis_index("x"), lax.axis_index("y"), lax.axis_index("c")


def _other_chips(x, y):
    return ((1 - x, y), (x, 1 - y), (1 - x, 1 - y))


def _lin(dev):
    return 4 * dev[0] + 2 * dev[1] + dev[2]


def _piece(total, lo, n, units=16):
    assert total % units == 0
    return lo * (total // units), n * (total // units)


def _remote(src, dst, send, recv, to):
    return pltpu.make_async_remote_copy(src_ref=src, dst_ref=dst, send_sem=send, recv_sem=recv, device_id=to, device_id_type=MESH)


def _placer(kind, n, lo, cnt):
    def place(ref, dev):
        if kind == "col":
            return ref.at[pl.ds(lo, cnt), pl.ds(pl.multiple_of(_lin(dev) * n, LANES), n)]
        return ref.at[pl.ds(pl.multiple_of(_lin(dev) * n + lo, 2 * SUBLANES), cnt), :]
    return place


def _spread_stage(shard, full, kind, piece=(0, 16)):
    n = shard.shape[1] if kind == "col" else shard.shape[0]
    lo, cnt = _piece(shard.shape[0], *piece)
    place = _placer(kind, n, lo, cnt)

    def copies(ins, outs, sems):
        send, recv, local = sems
        x, y, c = _coords()
        me = (x, y, c)
        src = ins[0].at[pl.ds(lo, cnt), :]
        peers = [(x, y, 1 - c)] + [(*chip, c) for chip in _other_chips(x, y)]
        own = pltpu.make_async_copy(src, place(outs[0], me), local)
        out = [_remote(src, place(outs[0], me), send.at[k], recv.at[k], t) for k, t in enumerate(peers)]
        arrive = [_remote(src, place(outs[0], t), send.at[k], recv.at[k], t) for k, t in enumerate(peers)]
        return own, out, arrive

    def start(ins, outs, sems):
        own, out, _ = copies(ins, outs, sems)
        own.start()
        for cp in out:
            cp.start()

    def finish(ins, outs, sems):
        own, out, arrive = copies(ins, outs, sems)
        for cp in arrive:
            cp.wait_recv()
        for cp in out:
            cp.wait_send()
        own.wait()

    return _Stage([shard, full], [jax.ShapeDtypeStruct(full.shape, full.dtype)], {1: 0},
                  [pltpu.SemaphoreType.DMA((4,)), pltpu.SemaphoreType.DMA((4,)), pltpu.SemaphoreType.DMA], start, finish)


def _forward_stage(full, kind, piece=(0, 16)):
    n = (full.shape[1] if kind == "col" else full.shape[0]) // N_DEV
    lo, cnt = _piece(full.shape[0] if kind == "col" else n, *piece)
    place = _placer(kind, n, lo, cnt)

    def copies(ins, outs, sems):
        send, recv = sems
        x, y, c = _coords()
        chips = _other_chips(x, y)
        out = [_remote(place(outs[0], (*chip, c)), place(outs[0], (*chip, c)), send.at[k], recv.at[k], (x, y, 1 - c)) for k, chip in enumerate(chips)]
        arrive = [_remote(place(outs[0], (*chip, 1 - c)), place(outs[0], (*chip, 1 - c)), send.at[k], recv.at[k], (x, y, 1 - c))
                  for k, chip in enumerate(chips)]
        return out, arrive

    def start(ins, outs, sems):
        for cp in copies(ins, outs, sems)[0]:
            cp.start()

    def finish(ins, outs, sems):
        out, arrive = copies(ins, outs, sems)
        for cp in arrive:
            cp.wait_recv()
        for cp in out:
            cp.wait_send()

    return _Stage([full], [jax.ShapeDtypeStruct(full.shape, full.dtype)], {0: 0},
                  [pltpu.SemaphoreType.DMA((3,)), pltpu.SemaphoreType.DMA((3,))], start, finish)


def _to_sibling_stage(grad, kind):
    n = (grad.shape[1] if kind == "col" else grad.shape[0]) // N_DEV
    shape = (N_CHIPS, grad.shape[0], n) if kind == "col" else (N_CHIPS, n, grad.shape[1])
    place = _placer(kind, n, 0, grad.shape[0] if kind == "col" else n)

    def copies(ins, outs, sems):
        send, recv = sems
        x, y, c = _coords()
        return [_remote(place(ins[0], (q // 2, q % 2, 1 - c)), outs[0].at[q], send.at[q], recv.at[q], (x, y, 1 - c)) for q in range(N_CHIPS)]

    def start(ins, outs, sems):
        for cp in copies(ins, outs, sems):
            cp.start()

    def finish(ins, outs, sems):
        for cp in copies(ins, outs, sems):
            cp.wait()

    return _Stage([grad], [jax.ShapeDtypeStruct(shape, BF16)], {},
                  [pltpu.SemaphoreType.DMA((N_CHIPS,)), pltpu.SemaphoreType.DMA((N_CHIPS,))], start, finish)


def _to_chips_stage(pair, dst, piece=(0, 16)):
    lo, cnt = _piece(pair.shape[1], *piece)

    def copies(ins, outs, sems):
        send, recv, local = sems
        x, y, c = _coords()
        mine = 2 * x + y
        chips = _other_chips(x, y)
        own = pltpu.make_async_copy(ins[0].at[mine, pl.ds(lo, cnt), :], outs[0].at[mine, pl.ds(lo, cnt), :], local)
        out = [_remote(ins[0].at[2 * px + py, pl.ds(lo, cnt), :], outs[0].at[mine, pl.ds(lo, cnt), :], send.at[k], recv.at[k], (px, py, c))
               for k, (px, py) in enumerate(chips)]
        arrive = [_remote(ins[0].at[2 * px + py, pl.ds(lo, cnt), :], outs[0].at[2 * px + py, pl.ds(lo, cnt), :], send.at[k], recv.at[k], (px, py, c))
                  for k, (px, py) in enumerate(chips)]
        return own, out, arrive

    def start(ins, outs, sems):
        own, out, _ = copies(ins, outs, sems)
        own.start()
        for cp in out:
            cp.start()

    def finish(ins, outs, sems):
        own, out, arrive = copies(ins, outs, sems)
        for cp in arrive:
            cp.wait_recv()
        for cp in out:
            cp.wait_send()
        own.wait()

    return _Stage([pair, dst], [jax.ShapeDtypeStruct(dst.shape, dst.dtype)], {1: 0},
                  [pltpu.SemaphoreType.DMA((3,)), pltpu.SemaphoreType.DMA((3,)), pltpu.SemaphoreType.DMA], start, finish)


def _small_stage(part):
    def copies(ins, outs, sems):
        send, recv, local = sems
        x, y, c = _coords()
        me = (x, y, c)
        own = pltpu.make_async_copy(ins[0], outs[0].at[_lin(me)], local)
        peers = [(1 - x if k & 4 else x, 1 - y if k & 2 else y, 1 - c if k & 1 else c) for k in range(1, N_DEV)]
        out = [_remote(ins[0], outs[0].at[_lin(me)], send.at[k], recv.at[k], t) for k, t in enumerate(peers)]
        arrive = [_remote(ins[0], outs[0].at[_lin(t)], send.at[k], recv.at[k], t) for k, t in enumerate(peers)]
        return own, out, arrive

    def start(ins, outs, sems):
        own, out, _ = copies(ins, outs, sems)
        own.start()
        for cp in out:
            cp.start()

    def finish(ins, outs, sems):
        own, out, arrive = copies(ins, outs, sems)
        for cp in arrive:
            cp.wait_recv()
        for cp in out:
            cp.wait_send()
        own.wait()

    return _Stage([part], [jax.ShapeDtypeStruct((N_DEV, *part.shape), F32)], {},
                  [pltpu.SemaphoreType.DMA((N_DEV - 1,)), pltpu.SemaphoreType.DMA((N_DEV - 1,)), pltpu.SemaphoreType.DMA], start, finish)


def _comm_only(name, stages):
    return _call(name, lambda: None, (1,), [], [], [], [], stages=stages, sequential=True)[1]


_GELU_C = math.sqrt(2.0 / math.pi)


def _gelu(x):
    return 0.5 * x * (1.0 + jnp.tanh(_GELU_C * (x + 0.044715 * x * x * x)))


def _gelu_grad(x):
    t = jnp.tanh(_GELU_C * (x + 0.044715 * x * x * x))
    return 0.5 * (1.0 + t) + 0.5 * x * (1.0 - t * t) * (_GELU_C * (1.0 + 3.0 * 0.044715 * x * x))


def _sigmoid(x):
    return 1.0 / (1.0 + jnp.exp(-x))


def _dot(a, b, mode):
    dims = {"nn": (((1,), (0,)), ((), ())), "nt": (((1,), (1,)), ((), ())), "tn": (((0,), (0,)), ((), ()))}[mode]
    return lax.dot_general(a.astype(BF16), b.astype(BF16), dims, preferred_element_type=F32)


def _matmul(name, a, b, mode, outs, epi=None, extras=(), colsums=(), tm=2048, tn=512, tk=2048, b_off=0, n=None, m_off=0, m=None, stages=()):
    if mode == "tn":
        kk, mfull = a.shape
    else:
        mfull, kk = a.shape
    m = mfull if m is None else m
    n = (b.shape[0] if mode == "nt" else b.shape[1]) if n is None else n
    tm, tk = _pick(tm, m, m_off), _pick(tk, kk)
    tn = _pick(tn, n, b_off, *[off for _, _, off in extras])
    boff, moff = b_off // tn, m_off // tm
    nm, nn_, nk = m // tm, n // tn, kk // tk
    col_major = bool(colsums)
    grid = (nn_, nm, nk) if col_major else (nm, nn_, nk)

    def imap(f):
        if col_major:
            return lambda g0, g1, k: f(g1, g0, k)
        return f

    a_spec = (pl.BlockSpec((tk, tm), imap(lambda i, j, k: (k, i + moff))) if mode == "tn"
              else pl.BlockSpec((tm, tk), imap(lambda i, j, k: (i + moff, k))))
    b_spec = (pl.BlockSpec((tn, tk), imap(lambda i, j, k: (j + boff, k))) if mode == "nt"
              else pl.BlockSpec((tk, tn), imap(lambda i, j, k: (k, j + boff))))
    in_specs, operands = [a_spec, b_spec], [a, b]
    for arr, kind, off in extras:
        o = off // tn
        if kind == "mn":
            in_specs.append(pl.BlockSpec((tm, tn), imap(lambda i, j, k, o=o: (i + moff, j + o))))
        else:
            in_specs.append(pl.BlockSpec((1, tn), imap(lambda i, j, k, o=o: (0, j + o))))
        operands.append(arr)
    out_shape = [jax.ShapeDtypeStruct((m, n), dt) for dt in outs] + [jax.ShapeDtypeStruct((1, n), F32) for _ in colsums]
    out_specs = ([pl.BlockSpec((tm, tn), imap(lambda i, j, k: (i, j))) for _ in outs]
                 + [pl.BlockSpec((1, tn), imap(lambda i, j, k: (0, j))) for _ in colsums])
    n_ex, n_out, n_cs = len(extras), len(outs), len(colsums)

    def body(*refs):
        a_ref, b_ref = refs[:2]
        ex_refs = refs[2:2 + n_ex]
        out_refs = refs[2 + n_ex:2 + n_ex + n_out]
        cs_refs = refs[2 + n_ex + n_out:2 + n_ex + n_out + n_cs]
        part = _dot(a_ref[...], b_ref[...], mode)

        def finish(acc):
            res = epi(acc, *[r[...] for r in ex_refs]) if epi is not None else (acc,)
            for r, v in zip(out_refs, res[:n_out]):
                r[...] = v.astype(r.dtype)
            if n_cs:
                @pl.when(pl.program_id(1) == 0)
                def _():
                    for r in cs_refs:
                        r[...] = jnp.zeros_like(r)

                for r, idx in zip(cs_refs, colsums):
                    r[...] += jnp.sum(res[idx], axis=0, keepdims=True)

        if nk == 1:
            finish(part)
        else:
            acc_ref = refs[-1]
            k = pl.program_id(2)

            @pl.when(k == 0)
            def _():
                acc_ref[...] = part

            @pl.when(k > 0)
            def _():
                acc_ref[...] += part

            @pl.when(k == nk - 1)
            def _():
                finish(acc_ref[...])

    sem = ("arbitrary", "arbitrary", "arbitrary") if col_major else ("parallel", "parallel", "arbitrary")
    return _call(name, body, grid, in_specs, out_specs, out_shape, operands,
                 scratch=[pltpu.VMEM((tm, tn), F32)] if nk > 1 else [], sem=sem, stages=stages)


def _row_spec(tr, c):
    return pl.BlockSpec((tr, c), lambda i: (i, 0))


def _fix_spec(shape):
    return pl.BlockSpec(shape, lambda *_: tuple(0 for _ in shape))


def _cast_bf16(name, x, tr=512):
    r, c = x.shape
    tr = _pick(tr, r)

    def body(x_ref, o_ref):
        o_ref[...] = x_ref[...].astype(BF16)

    return _call(name, body, (r // tr,), [_row_spec(tr, c)], [_row_spec(tr, c)], [jax.ShapeDtypeStruct((r, c), BF16)], [x],
                 sem=("parallel",))[0][0]


def _layer_norm_stats(x):
    mean = jnp.mean(x, axis=-1, keepdims=True)
    xc = x - mean
    var = jnp.mean(xc * xc, axis=-1, keepdims=True)
    rstd = lax.rsqrt(var + LN_EPS)
    return xc * rstd, rstd


def _layer_norm_bwd(dxhat, xhat, rstd):
    m1 = jnp.mean(dxhat, axis=-1, keepdims=True)
    m2 = jnp.mean(dxhat * xhat, axis=-1, keepdims=True)
    return rstd * (dxhat - m1 - xhat * m2)


def _ln1_fwd(pre1, g1, b1, tr=256, stages=()):
    s, d = pre1.shape
    tr = _pick(tr, s)

    def body(p_ref, g_ref, b_ref, xh_ref, rs_ref, h_ref):
        xhat, rstd = _layer_norm_stats(p_ref[...])
        xh_ref[...] = xhat
        rs_ref[...] = rstd
        h_ref[...] = (xhat * g_ref[...] + b_ref[...]).astype(BF16)

    return _call("ln1_fwd", body, (s // tr,), [_row_spec(tr, d), _fix_spec((1, d)), _fix_spec((1, d))],
                 [_row_spec(tr, d), _row_spec(tr, 1), _row_spec(tr, d)],
                 [jax.ShapeDtypeStruct((s, d), F32), jax.ShapeDtypeStruct((s, 1), F32), jax.ShapeDtypeStruct((s, d), BF16)],
                 [pre1, g1, b1], sem=("parallel",), stages=stages)


def _ln2_loss_bwd(ff, xhat1, g1, b1, g2, b2, target, tr=256):
    s, d = ff.shape
    tr = _pick(tr, s)

    def body(ff_ref, xh1_ref, g1_ref, b1_ref, g2_ref, b2_ref, t_ref, dp_ref, dpb_ref, dg_ref, db_ref, dbf_ref, loss_ref):
        @pl.when(pl.program_id(0) == 0)
        def _():
            dg_ref[...] = jnp.zeros_like(dg_ref)
            db_ref[...] = jnp.zeros_like(db_ref)
            dbf_ref[...] = jnp.zeros_like(dbf_ref)
            loss_ref[...] = jnp.zeros_like(loss_ref)

        h1 = xh1_ref[...] * g1_ref[...] + b1_ref[...]
        xhat, rstd = _layer_norm_stats(ALPHA * h1 + ff_ref[...])
        err = xhat * g2_ref[...] + b2_ref[...] - t_ref[...]
        row = jnp.mean(err * err, axis=-1, keepdims=True)
        loss_ref[...] += 0.5 * jnp.sum(row, axis=0, keepdims=True)
        dy = err / d
        dg_ref[...] += jnp.sum(dy * xhat, axis=0, keepdims=True)
        db_ref[...] += jnp.sum(dy, axis=0, keepdims=True)
        dpre = _layer_norm_bwd(dy * g2_ref[...], xhat, rstd)
        dbf_ref[...] += jnp.sum(dpre, axis=0, keepdims=True)
        dp_ref[...] = dpre
        dpb_ref[...] = dpre.astype(BF16)

    vec = _fix_spec((1, d))
    return _call("ln2_loss_bwd", body, (s // tr,), [_row_spec(tr, d), _row_spec(tr, d), vec, vec, vec, vec, _row_spec(tr, d)],
                 [_row_spec(tr, d), _row_spec(tr, d), vec, vec, vec, _fix_spec((1, 1))],
                 [jax.ShapeDtypeStruct((s, d), F32), jax.ShapeDtypeStruct((s, d), BF16)]
                 + [jax.ShapeDtypeStruct((1, d), F32)] * 3 + [jax.ShapeDtypeStruct((1, 1), F32)],
                 [ff, xhat1, g1, b1, g2, b2, target])[0]


def _ln1_bwd(dh1, xhat1, rstd1, g1, tr=256, stages=()):
    s, d = dh1.shape
    tr = _pick(tr, s)

    def body(dh_ref, xh_ref, rs_ref, g_ref, dp_ref, dpb_ref, dg_ref, db_ref):
        @pl.when(pl.program_id(0) == 0)
        def _():
            dg_ref[...] = jnp.zeros_like(dg_ref)
            db_ref[...] = jnp.zeros_like(db_ref)

        dh, xhat = dh_ref[...], xh_ref[...]
        dg_ref[...] += jnp.sum(dh * xhat, axis=0, keepdims=True)
        db_ref[...] += jnp.sum(dh, axis=0, keepdims=True)
        dpre = _layer_norm_bwd(dh * g_ref[...], xhat, rs_ref[...])
        dp_ref[...] = dpre
        dpb_ref[...] = dpre.astype(BF16)

    vec = _fix_spec((1, d))
    return _call("ln1_bwd", body, (s // tr,), [_row_spec(tr, d), _row_spec(tr, d), _row_spec(tr, 1), vec],
                 [_row_spec(tr, d), _row_spec(tr, d), vec, vec],
                 [jax.ShapeDtypeStruct((s, d), F32), jax.ShapeDtypeStruct((s, d), BF16)] + [jax.ShapeDtypeStruct((1, d), F32)] * 2,
                 [dh1, xhat1, rstd1, g1], stages=stages)


def _to_perm(x):
    return x.reshape(SEQ // N_SUB, N_SUB, -1).transpose(1, 0, 2).reshape(SEQ, -1)


def _from_perm(x):
    return x.reshape(N_SUB, SEQ // N_SUB, -1).transpose(1, 0, 2).reshape(SEQ, -1)


def _local_index(p):
    rho = np.arange(BLOCK)
    if p == 0:
        return 16 * (rho % 8) + rho // 8
    if p == 1:
        return 4 * (rho % 32) + rho // 32
    return rho


def _tile_view(x, p):
    c = x.shape[1]
    if p == 1:
        return x.reshape(4, 4, BLOCK, c)
    return x.reshape(N_SUB, BLOCK, c)


def _view_shape(c, p):
    return (4, 4, BLOCK, c) if p == 1 else (N_SUB, BLOCK, c)


def _tile_spec(p, width, col, shift=0):
    nblk = SEQ // DILATIONS[p] // BLOCK

    def blk(n):
        return jnp.clip(n + shift, 0, nblk - 1)

    if p == 0:
        return pl.BlockSpec((N_SUB, SUBLANES, width), lambda s, n: (0, blk(n), col))
    if p == 1:
        return pl.BlockSpec((4, None, 32, width), lambda s, n: (0, s, blk(n), col))
    return pl.BlockSpec((None, BLOCK, width), lambda s, n: (s, 0, col))


def _tile_grid(p):
    return ((1, 16), (4, 4), (16, 1))[p]


def _t5_bucket(n):
    max_exact = N_BUCKETS // 2
    nf = np.maximum(n, 1).astype(np.float32)
    large = max_exact + (np.log(nf / np.float32(max_exact)) / np.float32(math.log(MAX_DISTANCE / max_exact))
                         * np.float32(N_BUCKETS - max_exact)).astype(np.int32)
    large = np.minimum(large, N_BUCKETS - 1)
    return np.where(n < max_exact, n, large).astype(np.int32)


def _bucket_tables():
    tabs = np.zeros((3, 2, BLOCK, BLOCK), np.int32)
    for p, d in enumerate(DILATIONS):
        i = _local_index(p)
        diff = i[:, None] - i[None, :]
        tabs[p, 0] = np.where(diff <= 0, _t5_bucket((BLOCK + diff) * d), -1)
        tabs[p, 1] = np.where(diff >= 0, _t5_bucket(np.maximum(diff, 0) * d), -1)
    return tabs


def _bias_expand(rel_bias, buckets):
    nh = N_HEADS

    def body(rb_ref, bk_ref, o_ref):
        for w in range(2):
            bk = bk_ref[0, w]
            for h in range(nh):
                val = jnp.zeros((BLOCK, BLOCK), F32)
                for b in range(N_BUCKETS):
                    val = jnp.where(bk == b, rb_ref[b, h], val)
                o_ref[0, h, w] = jnp.where(bk < 0, NEG_INF, val)

    return _call("bias_expand", body, (3,),
                 [pl.BlockSpec(memory_space=pltpu.SMEM), pl.BlockSpec((1, 2, BLOCK, BLOCK), lambda p: (p, 0, 0, 0))],
                 [pl.BlockSpec((1, nh, 2, BLOCK, BLOCK), lambda p: (p, 0, 0, 0, 0))],
                 [jax.ShapeDtypeStruct((3, nh, 2, BLOCK, BLOCK), F32)], [rel_bias, buckets], sem=("parallel",))[0][0]


def _heads_to_lanes(cols):
    lane = lax.broadcasted_iota(I32, (BLOCK, LANES), 1)
    out = jnp.zeros((BLOCK, LANES), F32)
    for h, c in enumerate(cols):
        out = jnp.where(lane == h, c, out)
    return out


def _attn_fwd(qkv, bias, p, stages=()):
    d_a = _d_a()
    has_prev = SEQ // DILATIONS[p] // BLOCK > 1
    scale = HEAD_DIM ** -0.5
    view = _tile_view(qkv, p)

    def body(q_ref, kc_ref, kp_ref, vc_ref, vp_ref, b_ref, o_ref, l_ref):
        n = pl.program_id(1)
        q_all = q_ref[...].reshape(BLOCK, d_a).astype(BF16)
        kc_all = kc_ref[...].reshape(BLOCK, d_a).astype(BF16)
        vc_all = vc_ref[...].reshape(BLOCK, d_a).astype(BF16)
        if has_prev:
            kp_all = kp_ref[...].reshape(BLOCK, d_a).astype(BF16)
            vp_all = vp_ref[...].reshape(BLOCK, d_a).astype(BF16)
        outs, lses = [], []
        for h in range(N_HEADS):
            sl = slice(h * HEAD_DIM, (h + 1) * HEAD_DIM)
            q = q_all[:, sl]
            sc = _dot(q, kc_all[:, sl], "nt") * scale + b_ref[0, h, 1]
            m = jnp.max(sc, axis=-1, keepdims=True)
            if has_prev:
                sp = _dot(q, kp_all[:, sl], "nt") * scale + b_ref[0, h, 0]
                sp = jnp.where(n > 0, sp, NEG_INF)
                m = jnp.maximum(m, jnp.max(sp, axis=-1, keepdims=True))
                pp = jnp.exp(sp - m)
            pc = jnp.exp(sc - m)
            den = jnp.sum(pc, axis=-1, keepdims=True)
            o = _dot(pc, vc_all[:, sl], "nn")
            if has_prev:
                den = den + jnp.sum(pp, axis=-1, keepdims=True)
                o = o + _dot(pp, vp_all[:, sl], "nn")
            outs.append(o / den)
            lses.append(m + jnp.log(den))
        o_ref[...] = jnp.concatenate(outs, axis=-1).reshape(o_ref.shape)
        l_ref[...] = _heads_to_lanes(lses).reshape(l_ref.shape)

    (o, l), st = _call(
        f"attn_fwd{p}", body, _tile_grid(p),
        [_tile_spec(p, d_a, 0), _tile_spec(p, d_a, 1), _tile_spec(p, d_a, 1, -1), _tile_spec(p, d_a, 2), _tile_spec(p, d_a, 2, -1),
         pl.BlockSpec((1, N_HEADS, 2, BLOCK, BLOCK), lambda s, n: (p, 0, 0, 0, 0))],
        [_tile_spec(p, d_a, 0), _tile_spec(p, LANES, 0)],
        [jax.ShapeDtypeStruct(_view_shape(d_a, p), F32), jax.ShapeDtypeStruct(_view_shape(LANES, p), F32)],
        [view, view, view, view, view, bias], sem=("parallel", "parallel"), stages=stages)
    return (o.reshape(SEQ, d_a), l.reshape(SEQ, LANES)), st


def _attn_combine(os_, ls_, tr=256, stages=()):
    d_a = _d_a()
    tr = _pick(tr, SEQ)

    def body(o0, o1, o2, l0, l1, l2, a_ref, ab_ref, lt_ref):
        l = [l0[...], l1[...], l2[...]]
        m = jnp.maximum(jnp.maximum(l[0], l[1]), l[2])
        w = [jnp.exp(x - m) for x in l]
        tot = w[0] + w[1] + w[2]
        lt_ref[...] = m + jnp.log(tot)
        w = [x / tot for x in w]
        for h in range(N_HEADS):
            sl = slice(h * HEAD_DIM, (h + 1) * HEAD_DIM)
            acc = w[0][:, h:h + 1] * o0[:, sl] + w[1][:, h:h + 1] * o1[:, sl] + w[2][:, h:h + 1] * o2[:, sl]
            a_ref[:, sl] = acc
            ab_ref[:, sl] = acc.astype(BF16)

    return _call("attn_combine", body, (SEQ // tr,), [_row_spec(tr, d_a)] * 3 + [_row_spec(tr, LANES)] * 3,
                 [_row_spec(tr, d_a), _row_spec(tr, d_a), _row_spec(tr, LANES)],
                 [jax.ShapeDtypeStruct((SEQ, d_a), F32), jax.ShapeDtypeStruct((SEQ, d_a), BF16), jax.ShapeDtypeStruct((SEQ, LANES), F32)],
                 [*os_, *ls_], sem=("parallel",), stages=stages)


def _attn_delta(dattn, attn, tr=256):
    d_a = _d_a()
    tr = _pick(tr, SEQ)

    def body(d_ref, a_ref, o_ref):
        prod = d_ref[...] * a_ref[...]
        lane = lax.broadcasted_iota(I32, (tr, LANES), 1)
        out = jnp.zeros((tr, LANES), F32)
        for h in range(N_HEADS):
            out = jnp.where(lane == h, jnp.sum(prod[:, h * HEAD_DIM:(h + 1) * HEAD_DIM], axis=-1, keepdims=True), out)
        o_ref[...] = out

    return _call("attn_delta", body, (SEQ // tr,), [_row_spec(tr, d_a)] * 2, [_row_spec(tr, LANES)],
                 [jax.ShapeDtypeStruct((SEQ, LANES), F32)], [dattn, attn], sem=("parallel",))[0][0]


def _attn_bwd(qkv, dattn, lse, delta, bias, p, stages=()):
    d_a = _d_a()
    nblk = SEQ // DILATIONS[p] // BLOCK
    has_next = nblk > 1
    scale = HEAD_DIM ** -0.5
    qv, dov, lv, tv = (_tile_view(x, p) for x in (qkv, dattn, lse, delta))

    def body(q_ref, qn_ref, k_ref, v_ref, do_ref, don_ref, l_ref, ln_ref, t_ref, tn_ref, b_ref, dq_ref, dk_ref, dv_ref, db_ref, carry_ref):
        j = pl.program_id(1)

        @pl.when((pl.program_id(0) == 0) & (j == 0))
        def _():
            db_ref[...] = jnp.zeros_like(db_ref)

        k_all = k_ref[...].reshape(BLOCK, d_a).astype(BF16)
        v_all = v_ref[...].reshape(BLOCK, d_a).astype(BF16)

        def side(qr, dor, lr, tr_, w):
            q_all = qr[...].reshape(BLOCK, d_a).astype(BF16)
            do_all = dor[...].reshape(BLOCK, d_a).astype(BF16)
            l_all = lr[...].reshape(BLOCK, LANES)
            t_all = tr_[...].reshape(BLOCK, LANES)
            dqs, dks, dvs = [], [], []
            for h in range(N_HEADS):
                sl = slice(h * HEAD_DIM, (h + 1) * HEAD_DIM)
                s = _dot(q_all[:, sl], k_all[:, sl], "nt") * scale + b_ref[0, h, w]
                pr = jnp.exp(s - l_all[:, h:h + 1])
                dp = _dot(do_all[:, sl], v_all[:, sl], "nt")
                ds = pr * (dp - t_all[:, h:h + 1])
                db_ref[h, w] += ds
                dqs.append(_dot(ds, k_all[:, sl], "nn") * scale)
                dks.append(_dot(ds, q_all[:, sl], "tn") * scale)
                dvs.append(_dot(pr, do_all[:, sl], "tn"))
            return [jnp.concatenate(x, axis=-1) for x in (dqs, dks, dvs)]

        dq_c, dk_c, dv_c = side(q_ref, do_ref, l_ref, t_ref, 1)
        if has_next:
            dq_ref[...] = (jnp.where(j > 0, carry_ref[...], 0.0) + dq_c).reshape(dq_ref.shape)
            not_last = j < nblk - 1

            @pl.when(not_last)
            def _():
                dq_n, dk_n, dv_n = side(qn_ref, don_ref, ln_ref, tn_ref, 0)
                carry_ref[...] = dq_n
                dk_ref[...] = (dk_c + dk_n).reshape(dk_ref.shape)
                dv_ref[...] = (dv_c + dv_n).reshape(dv_ref.shape)

            @pl.when(jnp.logical_not(not_last))
            def _():
                dk_ref[...] = dk_c.reshape(dk_ref.shape)
                dv_ref[...] = dv_c.reshape(dv_ref.shape)
        else:
            dq_ref[...] = dq_c.reshape(dq_ref.shape)
            dk_ref[...] = dk_c.reshape(dk_ref.shape)
            dv_ref[...] = dv_c.reshape(dv_ref.shape)

    def big(col, shift=0):
        return _tile_spec(p, d_a, col, shift)

    def small(shift=0):
        return _tile_spec(p, LANES, 0, shift)

    (dq, dk, dv, dbias), st = _call(
        f"attn_bwd{p}", body, _tile_grid(p),
        [big(0), big(0, 1), big(1), big(2), big(0), big(0, 1), small(), small(1), small(), small(1),
         pl.BlockSpec((1, N_HEADS, 2, BLOCK, BLOCK), lambda s, n: (p, 0, 0, 0, 0))],
        [big(0), big(0), big(0), pl.BlockSpec((N_HEADS, 2, BLOCK, BLOCK), lambda s, n: (0, 0, 0, 0))],
        [jax.ShapeDtypeStruct(_view_shape(d_a, p), F32)] * 3 + [jax.ShapeDtypeStruct((N_HEADS, 2, BLOCK, BLOCK), F32)],
        [qv, qv, qv, qv, dov, dov, lv, lv, tv, tv, bias], scratch=[pltpu.VMEM((BLOCK, d_a), F32)], stages=stages)
    return (dq.reshape(SEQ, d_a), dk.reshape(SEQ, d_a), dv.reshape(SEQ, d_a), dbias), st


def _rel_bias_grad(dbias, buckets):
    nh = N_HEADS

    def body(d0, d1, d2, bk_ref, o_ref, t_ref):
        ds = (d0, d1, d2)

        def per_bucket(b, carry):
            for h in range(nh):
                acc = jnp.zeros((BLOCK, BLOCK), F32)
                for p in range(3):
                    for w in range(2):
                        acc = acc + jnp.where(bk_ref[p, w] == b, ds[p][h, w], 0.0)
                t_ref[pl.ds(b * nh + h, 1), :] = jnp.sum(acc, axis=0, keepdims=True)
            return carry

        lax.fori_loop(0, N_BUCKETS, per_bucket, 0)
        o_ref[...] = jnp.sum(t_ref[...], axis=-1, keepdims=True)

    return _call("rel_bias_grad", body, (1,), [_fix_spec((nh, 2, BLOCK, BLOCK))] * 3 + [_fix_spec((3, 2, BLOCK, BLOCK))],
                 [_fix_spec((N_BUCKETS * nh, 1))], [jax.ShapeDtypeStruct((N_BUCKETS * nh, 1), F32)], [*dbias, buckets],
                 scratch=[pltpu.VMEM((N_BUCKETS * nh, LANES), F32)])[0][0]


def _gmlp_fwd(rest, gain, bias, ws, bs, causal, stages=()):
    d_b = _d_b()

    def body(u_ref, v_ref, g_ref, b_ref, ws_ref, bs_ref, c_ref, o_ref):
        u = u_ref[...].reshape(BLOCK, d_b)
        xhat, _ = _layer_norm_stats(_gelu(v_ref[...].reshape(BLOCK, d_b)))
        vn = (xhat * g_ref[...] + b_ref[...]).astype(BF16)
        outs = []
        for g in range(N_GROUPS):
            sl = slice(g * BLOCK, (g + 1) * BLOCK)
            w = jnp.where(c_ref[...] > 0, ws_ref[g], 0.0)
            z = _dot(w, vn[:, sl], "nn") + bs_ref[:, g:g + 1]
            outs.append(_gelu(u[:, sl]) * z)
        o_ref[...] = jnp.concatenate(outs, axis=-1).reshape(o_ref.shape)

    (out,), st = _call(
        "gmlp_fwd", body, (1, SEQ // BLOCK),
        [_tile_spec(0, d_b, 0), _tile_spec(0, d_b, 1), _fix_spec((1, d_b)), _fix_spec((1, d_b)),
         _fix_spec((N_GROUPS, BLOCK, BLOCK)), _fix_spec((BLOCK, N_GROUPS)), _fix_spec((BLOCK, BLOCK))],
        [_tile_spec(0, d_b, 0)], [jax.ShapeDtypeStruct(_view_shape(d_b, 0), F32)],
        [_tile_view(rest, 0), _tile_view(rest, 0), gain, bias, ws, bs, causal], sem=("parallel", "parallel"), stages=stages)
    return out.reshape(SEQ, d_b), st


def _gmlp_bwd(rest, dgmlp, gain, bias, ws, bs, causal, stages=()):
    d_b = _d_b()
    nchunk = SEQ // BLOCK

    def body(u_ref, v_ref, dg_ref, g_ref, b_ref, ws_ref, bs_ref, c_ref, du_ref, dv_ref, dws_ref, dbs_ref, dgain_ref, dbias_ref):
        c = pl.program_id(1)

        @pl.when(c == 0)
        def _():
            dws_ref[...] = jnp.zeros_like(dws_ref)
            dbs_ref[...] = jnp.zeros_like(dbs_ref)
            dgain_ref[...] = jnp.zeros_like(dgain_ref)
            dbias_ref[...] = jnp.zeros_like(dbias_ref)

        u = u_ref[...].reshape(BLOCK, d_b)
        v = v_ref[...].reshape(BLOCK, d_b)
        dgm = dg_ref[...].reshape(BLOCK, d_b)
        xhat, rstd = _layer_norm_stats(_gelu(v))
        vn = (xhat * g_ref[...] + b_ref[...]).astype(BF16)
        lane = lax.broadcasted_iota(I32, (BLOCK, LANES), 1)
        dus, dvns = [], []
        dbs = dbs_ref[...]
        for g in range(N_GROUPS):
            sl = slice(g * BLOCK, (g + 1) * BLOCK)
            w = jnp.where(c_ref[...] > 0, ws_ref[g], 0.0).astype(BF16)
            z = _dot(w, vn[:, sl], "nn") + bs_ref[:, g:g + 1]
            dz = dgm[:, sl] * _gelu(u[:, sl])
            dus.append(dgm[:, sl] * z * _gelu_grad(u[:, sl]))
            dws_ref[g] += _dot(dz, vn[:, sl], "nt")
            dbs = dbs + jnp.where(lane == g, jnp.sum(dz, axis=-1, keepdims=True), 0.0)
            dvns.append(_dot(w, dz, "tn"))
        dbs_ref[...] = dbs
        dvn = jnp.concatenate(dvns, axis=-1)
        dgain_ref[...] += jnp.sum(dvn * xhat, axis=0, keepdims=True)
        dbias_ref[...] += jnp.sum(dvn, axis=0, keepdims=True)
        dvg = _layer_norm_bwd(dvn * g_ref[...], xhat, rstd)
        du_ref[...] = jnp.concatenate(dus, axis=-1).reshape(du_ref.shape)
        dv_ref[...] = (dvg * _gelu_grad(v)).reshape(dv_ref.shape)

        @pl.when(c == nchunk - 1)
        def _():
            for g in range(N_GROUPS):
                dws_ref[g] = jnp.where(c_ref[...] > 0, dws_ref[g], 0.0)

    (du, dv, dws, dbs, dgain, dbias), st = _call(
        "gmlp_bwd", body, (1, nchunk),
        [_tile_spec(0, d_b, 0), _tile_spec(0, d_b, 1), _tile_spec(0, d_b, 0), _fix_spec((1, d_b)), _fix_spec((1, d_b)),
         _fix_spec((N_GROUPS, BLOCK, BLOCK)), _fix_spec((BLOCK, N_GROUPS)), _fix_spec((BLOCK, BLOCK))],
        [_tile_spec(0, d_b, 0), _tile_spec(0, d_b, 0), _fix_spec((N_GROUPS, BLOCK, BLOCK)), _fix_spec((BLOCK, LANES)),
         _fix_spec((1, d_b)), _fix_spec((1, d_b))],
        [jax.ShapeDtypeStruct(_view_shape(d_b, 0), F32)] * 2
        + [jax.ShapeDtypeStruct((N_GROUPS, BLOCK, BLOCK), F32), jax.ShapeDtypeStruct((BLOCK, LANES), F32)]
        + [jax.ShapeDtypeStruct((1, d_b), F32)] * 2,
        [_tile_view(rest, 0), _tile_view(rest, 0), _tile_view(dgmlp, 0), gain, bias, ws, bs, causal], stages=stages)
    return (du.reshape(SEQ, d_b), dv.reshape(SEQ, d_b), dws, dbs, dgain, dbias), st


def _assemble_dproj(dqkv, du, dv, dga, dgb, tr=128):
    d_a, d_b, d_in = _d_a(), _d_b(), _d_in()
    tr = _pick(tr, SEQ)

    def body(*refs):
        att, (du_ref, dv_ref, dga_ref, dgb_ref, o_ref) = refs[:9], refs[9:]
        for i in range(3):
            o_ref[:, i * d_a:(i + 1) * d_a] = (att[3 * i][...] + att[3 * i + 1][...] + att[3 * i + 2][...]).astype(BF16)
        o_ref[:, 3 * d_a:3 * d_a + d_b] = du_ref[...].astype(BF16)
        o_ref[:, 3 * d_a + d_b:3 * d_a + 2 * d_b] = dv_ref[...].astype(BF16)
        o_ref[:, 3 * d_a + 2 * d_b:3 * d_a + 2 * d_b + D_MODEL] = dga_ref[...]
        o_ref[:, 3 * d_a + 2 * d_b + D_MODEL:] = dgb_ref[...]

    return _call("assemble_dproj", body, (SEQ // tr,), [_row_spec(tr, d_a)] * 9 + [_row_spec(tr, d_b)] * 2 + [_row_spec(tr, D_MODEL)] * 2,
                 [_row_spec(tr, d_in)], [jax.ShapeDtypeStruct((SEQ, d_in), BF16)], [*dqkv, du, dv, dga, dgb], sem=("parallel",))[0][0]


def _pair_sum(name, grad, recv, kind, core, tr=256):
    _, rs, cs = recv.shape
    tr = _pick(tr, rs)
    nb = rs // tr
    if kind == "col":
        g_spec = pl.BlockSpec((tr, cs), lambda q, i, c_ref: (i, 2 * q + c_ref[0]))
    else:
        g_spec = pl.BlockSpec((tr, cs), lambda q, i, c_ref: ((2 * q + c_ref[0]) * nb + i, 0))
    r_spec = pl.BlockSpec((None, tr, cs), lambda q, i, c_ref: (q, i, 0))

    def body(c_ref, g_ref, r_ref, o_ref):
        o_ref[...] = (g_ref[...].astype(F32) + r_ref[...].astype(F32)).astype(BF16)

    return pl.pallas_call(
        body, name=name, out_shape=jax.ShapeDtypeStruct(recv.shape, BF16),
        grid_spec=pltpu.PrefetchScalarGridSpec(num_scalar_prefetch=1, grid=(N_CHIPS, nb), in_specs=[g_spec, r_spec], out_specs=r_spec),
        compiler_params=pltpu.CompilerParams(dimension_semantics=("parallel", "parallel"), vmem_limit_bytes=VMEM_LIMIT),
    )(core, grad, recv)


def _adamw(w, g, m, v):
    m = ADAM_B1 * m + (1.0 - ADAM_B1) * g
    v = ADAM_B2 * v + (1.0 - ADAM_B2) * (g * g)
    m_hat = m / (1.0 - ADAM_B1 ** ADAM_STEP)
    v_hat = v / (1.0 - ADAM_B2 ** ADAM_STEP)
    delta = -ADAM_LR * (m_hat / (jnp.sqrt(v_hat) + ADAM_EPS) + ADAM_WD * w)
    return delta, m, v


def _adam_shard(name, chip_sums, w, m, v, tr=256, stages=()):
    rs, cs = w.shape
    tr = _pick(tr, rs)

    def body(s_ref, w_ref, m_ref, v_ref, g_ref, d_ref, nm_ref, nv_ref):
        g = s_ref[0].astype(F32)
        for q in range(1, N_CHIPS):
            g = g + s_ref[q].astype(F32)
        d, nm, nv = _adamw(w_ref[...], g, m_ref[...], v_ref[...])
        g_ref[...], d_ref[...], nm_ref[...], nv_ref[...] = g, d, nm, nv

    spec = _row_spec(tr, cs)
    return _call(name, body, (rs // tr,), [pl.BlockSpec((N_CHIPS, tr, cs), lambda i: (0, i, 0)), spec, spec, spec], [spec] * 4,
                 [jax.ShapeDtypeStruct((rs, cs), F32)] * 4, [chip_sums, w, m, v], sem=("parallel",), stages=stages)


def _adam_small(parts, w, m, v):
    rows = w.shape[0]

    def body(p_ref, w_ref, m_ref, v_ref, g_ref, d_ref, nm_ref, nv_ref):
        g = p_ref[0]
        for j in range(1, N_DEV):
            g = g + p_ref[j]
        d, nm, nv = _adamw(w_ref[...], g, m_ref[...], v_ref[...])
        g_ref[...], d_ref[...], nm_ref[...], nv_ref[...] = g, d, nm, nv

    spec = _fix_spec((rows, LANES))
    return _call("adam_small", body, (1,), [_fix_spec((N_DEV, rows, LANES)), spec, spec, spec], [spec] * 4,
                 [jax.ShapeDtypeStruct((rows, LANES), F32)] * 4, [parts, w, m, v])[0]


def _small_sizes():
    d_b = _d_b()
    return (("loss", 1), ("rel_bias", N_BUCKETS * N_HEADS), ("ln_v_gain", d_b), ("ln_v_bias", d_b),
            ("w_spatial", N_GROUPS * BLOCK * BLOCK), ("b_spatial", N_GROUPS * BLOCK), ("ln1_gain", D_MODEL), ("ln1_bias", D_MODEL),
            ("b_ff1", D_FF), ("b_ff2", D_MODEL), ("ln2_gain", D_MODEL), ("ln2_bias", D_MODEL))


def _pack(vals):
    pieces = []
    for name, size in _small_sizes():
        flat = vals[name].reshape(-1).astype(F32)
        padded = -(-size // (SUBLANES * LANES)) * SUBLANES * LANES
        pieces.append(jnp.pad(flat, (0, padded - size)).reshape(-1, LANES))
    return jnp.concatenate(pieces, axis=0)


def _unpack(buf):
    out, row = {}, 0
    for name, size in _small_sizes():
        rows = -(-size // (SUBLANES * LANES)) * SUBLANES
        out[name] = buf[row:row + rows].reshape(-1)[:size]
        row += rows
    return out


def kernel(x, w_in, rel_bias, ln_v_gain, ln_v_bias, w_spatial, b_spatial, w_proj_a, w_proj_b, w_out, ln1_gain, ln1_bias, w_ff1, b_ff1, w_ff2, b_ff2, ln2_gain, ln2_bias, loss_target, m_w_in, m_rel_bias, m_ln_v_gain, m_ln_v_bias, m_w_spatial, m_b_spatial, m_w_proj_a, m_w_proj_b, m_w_out, m_ln1_gain, m_ln1_bias, m_w_ff1, m_b_ff1, m_w_ff2, m_b_ff2, m_ln2_gain, m_ln2_bias, v_w_in, v_rel_bias, v_ln_v_gain, v_ln_v_bias, v_w_spatial, v_b_spatial, v_w_proj_a, v_w_proj_b, v_w_out, v_ln1_gain, v_ln1_bias, v_w_ff1, v_b_ff1, v_w_ff2, v_b_ff2, v_ln2_gain, v_ln2_bias):
    d_a, d_b, d_in = _d_a(), _d_b(), _d_in()
    weights = dict(w_in=w_in, rel_bias=rel_bias, ln_v_gain=ln_v_gain, ln_v_bias=ln_v_bias, w_spatial=w_spatial, b_spatial=b_spatial,
                   w_proj_a=w_proj_a, w_proj_b=w_proj_b, w_out=w_out, ln1_gain=ln1_gain, ln1_bias=ln1_bias, w_ff1=w_ff1, b_ff1=b_ff1,
                   w_ff2=w_ff2, b_ff2=b_ff2, ln2_gain=ln2_gain, ln2_bias=ln2_bias)
    mom1 = dict(w_in=m_w_in, rel_bias=m_rel_bias, ln_v_gain=m_ln_v_gain, ln_v_bias=m_ln_v_bias, w_spatial=m_w_spatial,
                b_spatial=m_b_spatial, w_proj_a=m_w_proj_a, w_proj_b=m_w_proj_b, w_out=m_w_out, ln1_gain=m_ln1_gain,
                ln1_bias=m_ln1_bias, w_ff1=m_w_ff1, b_ff1=m_b_ff1, w_ff2=m_w_ff2, b_ff2=m_b_ff2, ln2_gain=m_ln2_gain, ln2_bias=m_ln2_bias)
    mom2 = dict(w_in=v_w_in, rel_bias=v_rel_bias, ln_v_gain=v_ln_v_gain, ln_v_bias=v_ln_v_bias, w_spatial=v_w_spatial,
                b_spatial=v_b_spatial, w_proj_a=v_w_proj_a, w_proj_b=v_w_proj_b, w_out=v_w_out, ln1_gain=v_ln1_gain,
                ln1_bias=v_ln1_bias, w_ff1=v_w_ff1, b_ff1=v_b_ff1, w_ff2=v_w_ff2, b_ff2=v_b_ff2, ln2_gain=v_ln2_gain, ln2_bias=v_ln2_bias)

    shard = {n: _cast_bf16(f"cast_{n}", weights[n][0]) for n in KINDS}
    full, sent = {}, {n: 0 for n in KINDS}
    for n, kind in KINDS.items():
        r, c = shard[n].shape
        full[n] = lax.empty((r, c * N_DEV) if kind == "col" else (r * N_DEV, c), BF16)

    def keep(table, n):
        def store(outs):
            table[n] = outs[0]
        return store

    def gather(n, units=16):
        st = _spread_stage(shard[n], full[n], KINDS[n], (sent[n], units))
        sent[n] += units
        st.store = keep(full, n)
        return st

    def pass_on(n):
        st = _forward_stage(full[n], KINDS[n])
        st.store = keep(full, n)
        return st

    def settle(stages, outs):
        for st, o in zip(stages, outs):
            st.store(o)

    def alone(name, stage):
        settle([stage], _comm_only(name, [stage]))

    alone("gather_w_in_chips", gather("w_in"))
    alone("gather_w_in_sibling", pass_on("w_in"))

    xs = _to_perm(x[0])
    target = _to_perm(loss_target[0])
    xb = _cast_bf16("cast_x", xs)
    g8 = BLOCK // N_SUB
    ws_t = w_spatial[0].reshape(N_GROUPS, g8, N_SUB, g8, N_SUB).transpose(0, 2, 1, 4, 3).reshape(N_GROUPS, BLOCK, BLOCK)
    bs_t = b_spatial[0].reshape(N_GROUPS, g8, N_SUB).transpose(2, 1, 0).reshape(BLOCK, N_GROUPS)
    idx = _local_index(0)
    causal = jnp.asarray((idx[:, None] >= idx[None, :]).astype(np.float32))
    buckets = jnp.asarray(_bucket_tables())
    bias = _bias_expand(rel_bias, buckets)

    hosted = [gather("w_proj_a")]
    (qkv,), st = _matmul("proj_qkv", xb, full["w_in"], "nn", [F32], n=3 * d_a, stages=hosted)
    settle(hosted, st)
    hosted = [gather("w_proj_b"), gather("w_out", 12)]
    (rest,), st = _matmul("proj_rest", xb, full["w_in"], "nn", [F32], b_off=3 * d_a, n=d_in - 3 * d_a, stages=hosted)
    settle(hosted, st)
    fwd = []
    for p in range(3):
        hosted = [gather("w_ff1", 2 if p == 0 else 3)]
        if p == 0:
            hosted += [pass_on("w_proj_a"), pass_on("w_proj_b"), gather("w_out", 4)]
        if p == 1:
            hosted.append(pass_on("w_out"))
        res, st = _attn_fwd(qkv, bias, p, stages=hosted)
        settle(hosted, st)
        fwd.append(res)
    hosted = [gather("w_ff1", 1)]
    (attn, attn_b, lse), st = _attn_combine([o for o, _ in fwd], [l for _, l in fwd], stages=hosted)
    settle(hosted, st)
    hosted = [gather("w_ff1", 1)]
    gmlp, st = _gmlp_fwd(rest, ln_v_gain, ln_v_bias, ws_t, bs_t, causal, stages=hosted)
    settle(hosted, st)
    hosted = [gather("w_ff1", 2)]
    (ya,), st = _matmul("proj_a", attn_b, full["w_proj_a"], "nn", [F32], stages=hosted)
    settle(hosted, st)
    gate_a, gate_b = 2 * d_b, 2 * d_b + D_MODEL

    def merge(acc, ya_, ga, gb):
        return acc, _sigmoid(ga) * ya_ + _sigmoid(gb) * acc

    hosted = [gather("w_ff1", 3)]
    (yb, merged), st = _matmul("proj_b_merge", gmlp, full["w_proj_b"], "nn", [F32, BF16], merge,
                               [(ya, "mn", 0), (rest, "mn", gate_a), (rest, "mn", gate_b)], tn=256, stages=hosted)
    settle(hosted, st)
    hosted = [gather("w_ff1", 1), gather("w_ff2", 2)]
    (pre1,), st = _matmul("out_proj", merged, full["w_out"], "nn", [F32], lambda acc, x_: (ALPHA * x_ + acc,), [(xs, "mn", 0)], stages=hosted)
    settle(hosted, st)
    hosted = [pass_on("w_ff1"), gather("w_ff2", 1)]
    (xhat1, rstd1, h1b), st = _ln1_fwd(pre1, ln1_gain, ln1_bias, stages=hosted)
    settle(hosted, st)

    def relu2(acc, b_):
        r = jnp.maximum(acc + b_, 0.0)
        return r, r * r

    hosted = [gather("w_ff2", 7)]
    (relu, fb), st = _matmul("ff1", h1b, full["w_ff1"], "nn", [F32, BF16], relu2, [(b_ff1, "row", 0)], stages=hosted)
    settle(hosted, st)
    alone("gather_w_ff2_chips", gather("w_ff2", 6))
    alone("gather_w_ff2_sibling", pass_on("w_ff2"))
    assert all(v == 16 for v in sent.values()), sent
    (ff,), _ = _matmul("ff2", fb, full["w_ff2"], "nn", [F32], lambda acc, b_: (acc + b_,), [(b_ff2, "row", 0)], tn=1024, tk=1024)

    core = lax.axis_index("c").astype(I32).reshape(1)
    grads, sib, pair, chips, reduced = {}, {}, {}, {}, {n: 0 for n in KINDS}

    def to_sibling(n):
        st = _to_sibling_stage(grads[n], KINDS[n])
        st.store = keep(sib, n)
        return st

    def pair_up(n):
        pair[n] = _pair_sum(f"pair_sum_{n}", grads[n], sib[n], KINDS[n], core)
        chips[n] = lax.empty(pair[n].shape, BF16)

    def to_chips(n, units=16):
        st = _to_chips_stage(pair[n], chips[n], (reduced[n], units))
        reduced[n] += units
        st.store = keep(chips, n)
        return st

    dpre2, dpre2b, g_ln2_gain, g_ln2_bias, g_b_ff2, loss_part = _ln2_loss_bwd(ff, xhat1, ln1_gain, ln1_bias, ln2_gain, ln2_bias, target)
    (grads["w_ff2"],), _ = _matmul("dw_ff2", fb, dpre2b, "tn", [BF16])

    def relu2_bwd(acc, r):
        da = acc * (2.0 * r)
        return da, da

    hosted = [to_sibling("w_ff2")]
    (dab, g_b_ff1), st = _matmul("d_ff1", dpre2b, full["w_ff2"], "nt", [BF16], relu2_bwd, [(relu, "mn", 0)], colsums=(1,), stages=hosted)
    settle(hosted, st)
    pair_up("w_ff2")
    hosted = [to_chips("w_ff2", 7)]
    (grads["w_ff1"],), st = _matmul("dw_ff1", h1b, dab, "tn", [BF16], stages=hosted)
    settle(hosted, st)
    hosted = [to_chips("w_ff2", 9), to_sibling("w_ff1")]
    (dh1,), st = _matmul("d_h1", dab, full["w_ff1"], "nt", [F32], lambda acc, d_: (acc + ALPHA * d_,), [(dpre2, "mn", 0)], stages=hosted)
    settle(hosted, st)
    pair_up("w_ff1")
    hosted = [to_chips("w_ff1", 2)]
    (dpre1, dpre1b, g_ln1_gain, g_ln1_bias), st = _ln1_bwd(dh1, xhat1, rstd1, ln1_gain, stages=hosted)
    settle(hosted, st)
    hosted = [to_chips("w_ff1", 2)]
    (grads["w_out"],), st = _matmul("dw_out", merged, dpre1b, "tn", [BF16], stages=hosted)
    settle(hosted, st)

    def merge_bwd(acc, ga, gb, ya_, yb_):
        sa, sb = _sigmoid(ga), _sigmoid(gb)
        return acc * sa, acc * sb, acc * ya_ * (sa * (1.0 - sa)), acc * yb_ * (sb * (1.0 - sb))

    hosted = [to_chips("w_ff1", 5), to_sibling("w_out")]
    (dya, dyb, dga, dgb), st = _matmul("d_merge", dpre1b, full["w_out"], "nt", [BF16] * 4, merge_bwd,
                                       [(rest, "mn", gate_a), (rest, "mn", gate_b), (ya, "mn", 0), (yb, "mn", 0)], tn=256, stages=hosted)
    settle(hosted, st)
    pair_up("w_out")
    hosted = [to_chips("w_ff1", 1)]
    (grads["w_proj_a"],), st = _matmul("dw_proj_a", attn_b, dya, "tn", [BF16], stages=hosted)
    settle(hosted, st)
    hosted = [to_chips("w_ff1", 1)]
    (grads["w_proj_b"],), st = _matmul("dw_proj_b", gmlp, dyb, "tn", [BF16], stages=hosted)
    settle(hosted, st)
    hosted = [to_chips("w_ff1", 1), to_sibling("w_proj_a"), to_sibling("w_proj_b")]
    (dattn,), st = _matmul("d_attn", dya, full["w_proj_a"], "nt", [F32], stages=hosted)
    settle(hosted, st)
    pair_up("w_proj_a")
    pair_up("w_proj_b")
    hosted = [to_chips("w_ff1", 1)]
    (dgmlp,), st = _matmul("d_gmlp", dyb, full["w_proj_b"], "nt", [F32], stages=hosted)
    settle(hosted, st)
    hosted = [to_chips("w_ff1", 2)]
    (du, dvb, dws_t, dbs_t, g_lnv_gain, g_lnv_bias), st = _gmlp_bwd(rest, dgmlp, ln_v_gain, ln_v_bias, ws_t, bs_t, causal, stages=hosted)
    settle(hosted, st)
    delta = _attn_delta(dattn, attn)
    bwd = []
    for p in range(3):
        hosted = []
        if p == 0:
            hosted = [to_chips("w_ff1", 1), to_chips("w_out")]
        if p == 1:
            hosted = [to_chips("w_proj_a"), to_chips("w_proj_b")]
        res, st = _attn_bwd(qkv, dattn, lse, delta, bias, p, stages=hosted)
        settle(hosted, st)
        bwd.append(res)
    g_rel_bias = _rel_bias_grad([b[3] for b in bwd], buckets)
    dproj = _assemble_dproj([b[i] for i in range(3) for b in bwd], du, dvb, dga, dgb)

    g_w_spatial = dws_t.reshape(N_GROUPS, N_SUB, g8, N_SUB, g8).transpose(0, 2, 1, 4, 3)
    g_b_spatial = dbs_t[:, :N_GROUPS].reshape(N_SUB, g8, N_GROUPS).transpose(2, 1, 0)
    part = _pack(dict(loss=loss_part, rel_bias=g_rel_bias, ln_v_gain=g_lnv_gain, ln_v_bias=g_lnv_bias, w_spatial=g_w_spatial,
                      b_spatial=g_b_spatial, ln1_gain=g_ln1_gain, ln1_bias=g_ln1_bias, b_ff1=g_b_ff1, b_ff2=g_b_ff2,
                      ln2_gain=g_ln2_gain, ln2_bias=g_ln2_bias))
    (grads["w_in"],), st = _matmul("dw_in", xb, dproj, "tn", [BF16], stages=[_small_stage(part)])
    parts = st[0][0]
    half = SEQ // 2

    def add_residual(acc, d_):
        return (acc + ALPHA * d_,)

    hosted = [to_sibling("w_in")]
    (dx0,), st = _matmul("d_x0", dproj, full["w_in"], "nt", [F32], add_residual, [(dpre1, "mn", 0)], tm=1024, tn=1024, tk=3072, m=half,
                         stages=hosted)
    settle(hosted, st)
    pair_up("w_in")
    hosted = [to_chips("w_in", 4)]
    (dx1,), st = _matmul("d_x1", dproj, full["w_in"], "nt", [F32], add_residual, [(dpre1, "mn", 0)], tm=1024, tn=1024, tk=3072, m_off=half, m=half,
                         stages=hosted)
    settle(hosted, st)
    grad_x = _from_perm(jnp.concatenate([dx0, dx1], axis=0))[None]

    out_g, out_d, out_m, out_v = {}, {}, {}, {}
    for n, units in (("w_ff2", 4), ("w_ff1", 4), ("w_out", 2), ("w_proj_a", 1), ("w_proj_b", 1), ("w_in", 0)):
        hosted = [to_chips("w_in", units)] if units else []
        (g, d, nm, nv), st = _adam_shard(f"adam_{n}", chips[n], weights[n][0], mom1[n][0], mom2[n][0], stages=hosted)
        settle(hosted, st)
        out_g[n], out_d[n], out_m[n], out_v[n] = g[None], d[None], nm[None], nv[None]
    assert all(v == 16 for v in reduced.values()), reduced

    zero = jnp.zeros((1,), F32)
    sg, sd, sm, sv = (_unpack(b) for b in _adam_small(
        parts, _pack({**weights, "loss": zero}), _pack({**mom1, "loss": zero}), _pack({**mom2, "loss": zero})))
    for n in WEIGHT_ORDER:
        if n not in KINDS:
            shape = weights[n].shape
            out_g[n], out_d[n], out_m[n], out_v[n] = (t[n].reshape(shape) for t in (sg, sd, sm, sv))
    loss = sg["loss"].reshape(())
    return (loss, grad_x, *[out_g[n] for n in WEIGHT_ORDER], *[out_d[n] for n in WEIGHT_ORDER],
            *[out_m[n] for n in WEIGHT_ORDER], *[out_v[n] for n in WEIGHT_ORDER])
```

```python
import math

import jax
import jax.numpy as jnp
import numpy as np
from jax import lax
from jax.experimental import pallas as pl
from jax.experimental.pallas import tpu as pltpu

F32 = jnp.float32
BF16 = jnp.bfloat16
I32 = jnp.int32

SEQ = 2048
D_MODEL = 2048
HEAD_DIM = 128
N_HEADS = 8
N_GROUPS = 8
D_FF = 4 * D_MODEL
BLOCK = 128
DILATIONS = (1, 4, 16)
N_BUCKETS = 32
MAX_DISTANCE = 2048
ALPHA = 2.0 ** 0.25
LN_EPS = 1e-5
NEG_INF = -1e30
N_DEV = 8
N_CHIPS = 4
N_SUB = 16
ADAM_LR, ADAM_B1, ADAM_B2, ADAM_EPS, ADAM_WD, ADAM_STEP = 0.001, 0.9, 0.999, 1e-08, 0.01, 10
LANES = 128
SUBLANES = 8
VMEM_LIMIT = 56 * 1024 * 1024
MESH = pl.DeviceIdType.MESH
ANY = pl.BlockSpec(memory_space=pl.ANY)
WEIGHT_ORDER = ("w_in", "rel_bias", "ln_v_gain", "ln_v_bias", "w_spatial", "b_spatial", "w_proj_a", "w_proj_b", "w_out",
                "ln1_gain", "ln1_bias", "w_ff1", "b_ff1", "w_ff2", "b_ff2", "ln2_gain", "ln2_bias")
KINDS = {"w_in": "col", "w_proj_a": "col", "w_proj_b": "col", "w_out": "row", "w_ff1": "col", "w_ff2": "row"}


def _d_a():
    return N_HEADS * HEAD_DIM


def _d_b():
    return N_GROUPS * BLOCK


def _d_in():
    return 3 * _d_a() + 2 * _d_b() + 2 * D_MODEL


def _pick(t, n, *others):
    if n <= t and all(o % n == 0 for o in others):
        return n
    for c in range(min(t, n) // LANES * LANES, 0, -LANES):
        if n % c == 0 and all(o % c == 0 for o in others):
            return c
    raise ValueError((t, n, others))


class _Stage:
    def __init__(self, ins, outs, alias, sems, start, finish):
        self.ins, self.outs, self.alias, self.sems, self.start, self.finish = ins, outs, alias, sems, start, finish


def _call(name, body, grid, in_specs, out_specs, out_shape, operands, scratch=(), sem=None, stages=(), sequential=False, prefetch=None):
    n_in, n_out, n_sc = len(in_specs), len(out_specs), len(scratch)
    st_in = [len(s.ins) for s in stages]
    st_out = [len(s.outs) for s in stages]
    st_sem = [len(s.sems) for s in stages]
    n_pre = 0 if prefetch is None else 1
    aliases, ioff, ooff = {}, n_in + n_pre, n_out
    for s, ni, no in zip(stages, st_in, st_out):
        for i, o in s.alias.items():
            aliases[ioff + i] = ooff + o
        ioff, ooff = ioff + ni, ooff + no

    def split(refs, counts):
        out, at = [], 0
        for c in counts:
            out.append(refs[at:at + c])
            at += c
        return out

    def wrapped(*refs):
        ins, sins, outs, souts, sc, ssems = split(refs[n_pre:], [n_in, sum(st_in), n_out, sum(st_out), n_sc, sum(st_sem)])
        parts = list(zip(stages, split(sins, st_in), split(souts, st_out), split(ssems, st_sem)))
        if sequential:
            for s, a, b, c in parts:
                s.start(a, b, c)
                s.finish(a, b, c)
            return
        if parts:
            first = _all_of([pl.program_id(i) == 0 for i in range(len(grid))])
            last = _all_of([pl.program_id(i) == g - 1 for i, g in enumerate(grid)])

            @pl.when(first)
            def _():
                for s, a, b, c in parts:
                    s.start(a, b, c)

        body(*ins, *outs, *sc)
        if parts:
            @pl.when(last)
            def _():
                for s, a, b, c in parts:
                    s.finish(a, b, c)

    if stages or sem is None:
        sem = ("arbitrary",) * len(grid)
    specs = dict(grid=grid, in_specs=list(in_specs) + [ANY] * sum(st_in), out_specs=list(out_specs) + [ANY] * sum(st_out),
                 scratch_shapes=list(scratch) + [x for s in stages for x in s.sems])
    if prefetch is not None:
        specs = dict(grid_spec=pltpu.PrefetchScalarGridSpec(num_scalar_prefetch=1, **specs))
    res = pl.pallas_call(
        wrapped, name=name, out_shape=list(out_shape) + [o for s in stages for o in s.outs], input_output_aliases=aliases,
        compiler_params=pltpu.CompilerParams(dimension_semantics=sem, vmem_limit_bytes=VMEM_LIMIT), **specs,
    )(*([prefetch] if n_pre else []), *operands, *[a for s in stages for a in s.ins])
    res = list(res)
    return res[:n_out], split(res[n_out:], st_out)


def _all_of(conds):
    out = conds[0]
    for c in conds[1:]:
        out = out & c
    return out


def _coords():
    return lax.axis_index("x"), lax.axis_index("y"), lax.axis_index("c")


def _other_chips(x, y):
    return ((1 - x, y), (x, 1 - y), (1 - x, 1 - y))


def _lin(dev):
    return 4 * dev[0] + 2 * dev[1] + dev[2]


def _piece(total, lo, n, units=16):
    assert total % units == 0
    return lo * (total // units), n * (total // units)


def _remote(src, dst, send, recv, to):
    return pltpu.make_async_remote_copy(src_ref=src, dst_ref=dst, send_sem=send, recv_sem=recv, device_id=to, device_id_type=MESH)


def _placer(kind, n, lo, cnt):
    def place(ref, dev):
        if kind == "col":
            return ref.at[pl.ds(lo, cnt), pl.ds(pl.multiple_of(_lin(dev) * n, LANES), n)]
        return ref.at[pl.ds(pl.multiple_of(_lin(dev) * n + lo, 2 * SUBLANES), cnt), :]
    return place


def _spread_stage(shard, full, kind, piece=(0, 16)):
    n = shard.shape[1] if kind == "col" else shard.shape[0]
    lo, cnt = _piece(shard.shape[0], *piece)
    place = _placer(kind, n, lo, cnt)

    def copies(ins, outs, sems):
        send, recv, local = sems
        x, y, c = _coords()
        me = (x, y, c)
        src = ins[0].at[pl.ds(lo, cnt), :]
        peers = [(x, y, 1 - c)] + [(*chip, c) for chip in _other_chips(x, y)]
        own = pltpu.make_async_copy(src, place(outs[0], me), local)
        out = [_remote(src, place(outs[0], me), send.at[k], recv.at[k], t) for k, t in enumerate(peers)]
        arrive = [_remote(src, place(outs[0], t), send.at[k], recv.at[k], t) for k, t in enumerate(peers)]
        return own, out, arrive

    def start(ins, outs, sems):
        own, out, _ = copies(ins, outs, sems)
        own.start()
        for cp in out:
            cp.start()

    def finish(ins, outs, sems):
        own, out, arrive = copies(ins, outs, sems)
        for cp in arrive:
            cp.wait_recv()
        for cp in out:
            cp.wait_send()
        own.wait()

    return _Stage([shard, full], [jax.ShapeDtypeStruct(full.shape, full.dtype)], {1: 0},
                  [pltpu.SemaphoreType.DMA((4,)), pltpu.SemaphoreType.DMA((4,)), pltpu.SemaphoreType.DMA], start, finish)


def _forward_stage(full, kind, piece=(0, 16)):
    n = (full.shape[1] if kind == "col" else full.shape[0]) // N_DEV
    lo, cnt = _piece(full.shape[0] if kind == "col" else n, *piece)
    place = _placer(kind, n, lo, cnt)

    def copies(ins, outs, sems):
        send, recv = sems
        x, y, c = _coords()
        chips = _other_chips(x, y)
        out = [_remote(place(outs[0], (*chip, c)), place(outs[0], (*chip, c)), send.at[k], recv.at[k], (x, y, 1 - c)) for k, chip in enumerate(chips)]
        arrive = [_remote(place(outs[0], (*chip, 1 - c)), place(outs[0], (*chip, 1 - c)), send.at[k], recv.at[k], (x, y, 1 - c))
                  for k, chip in enumerate(chips)]
        return out, arrive

    def start(ins, outs, sems):
        for cp in copies(ins, outs, sems)[0]:
            cp.start()

    def finish(ins, outs, sems):
        out, arrive = copies(ins, outs, sems)
        for cp in arrive:
            cp.wait_recv()
        for cp in out:
            cp.wait_send()

    return _Stage([full], [jax.ShapeDtypeStruct(full.shape, full.dtype)], {0: 0},
                  [pltpu.SemaphoreType.DMA((3,)), pltpu.SemaphoreType.DMA((3,))], start, finish)


def _to_sibling_stage(theirs):
    def copies(ins, outs, sems):
        send, recv = sems
        x, y, c = _coords()
        return [_remote(ins[0].at[q], outs[0].at[q], send.at[q], recv.at[q], (x, y, 1 - c)) for q in range(N_CHIPS)]

    def start(ins, outs, sems):
        for cp in copies(ins, outs, sems):
            cp.start()

    def finish(ins, outs, sems):
        for cp in copies(ins, outs, sems):
            cp.wait()

    return _Stage([theirs], [jax.ShapeDtypeStruct(theirs.shape, BF16)], {},
                  [pltpu.SemaphoreType.DMA((N_CHIPS,)), pltpu.SemaphoreType.DMA((N_CHIPS,))], start, finish)


def _to_chips_stage(pair, dst, piece=(0, 16)):
    lo, cnt = _piece(pair.shape[1], *piece)

    def copies(ins, outs, sems):
        send, recv, local = sems
        x, y, c = _coords()
        mine = 2 * x + y
        chips = _other_chips(x, y)
        own = pltpu.make_async_copy(ins[0].at[mine, pl.ds(lo, cnt), :], outs[0].at[mine, pl.ds(lo, cnt), :], local)
        out = [_remote(ins[0].at[2 * px + py, pl.ds(lo, cnt), :], outs[0].at[mine, pl.ds(lo, cnt), :], send.at[k], recv.at[k], (px, py, c))
               for k, (px, py) in enumerate(chips)]
        arrive = [_remote(ins[0].at[2 * px + py, pl.ds(lo, cnt), :], outs[0].at[2 * px + py, pl.ds(lo, cnt), :], send.at[k], recv.at[k], (px, py, c))
                  for k, (px, py) in enumerate(chips)]
        return own, out, arrive

    def start(ins, outs, sems):
        own, out, _ = copies(ins, outs, sems)
        own.start()
        for cp in out:
            cp.start()

    def finish(ins, outs, sems):
        own, out, arrive = copies(ins, outs, sems)
        for cp in arrive:
            cp.wait_recv()
        for cp in out:
            cp.wait_send()
        own.wait()

    return _Stage([pair, dst], [jax.ShapeDtypeStruct(dst.shape, dst.dtype)], {1: 0},
                  [pltpu.SemaphoreType.DMA((3,)), pltpu.SemaphoreType.DMA((3,)), pltpu.SemaphoreType.DMA], start, finish)


def _small_stage(part):
    def copies(ins, outs, sems):
        send, recv, local = sems
        x, y, c = _coords()
        me = (x, y, c)
        own = pltpu.make_async_copy(ins[0], outs[0].at[_lin(me)], local)
        peers = [(1 - x if k & 4 else x, 1 - y if k & 2 else y, 1 - c if k & 1 else c) for k in range(1, N_DEV)]
        out = [_remote(ins[0], outs[0].at[_lin(me)], send.at[k], recv.at[k], t) for k, t in enumerate(peers)]
        arrive = [_remote(ins[0], outs[0].at[_lin(t)], send.at[k], recv.at[k], t) for k, t in enumerate(peers)]
        return own, out, arrive

    def start(ins, outs, sems):
        own, out, _ = copies(ins, outs, sems)
        own.start()
        for cp in out:
            cp.start()

    def finish(ins, outs, sems):
        own, out, arrive = copies(ins, outs, sems)
        for cp in arrive:
            cp.wait_recv()
        for cp in out:
            cp.wait_send()
        own.wait()

    return _Stage([part], [jax.ShapeDtypeStruct((N_DEV, *part.shape), F32)], {},
                  [pltpu.SemaphoreType.DMA((N_DEV - 1,)), pltpu.SemaphoreType.DMA((N_DEV - 1,)), pltpu.SemaphoreType.DMA], start, finish)


def _comm_only(name, stages):
    return _call(name, lambda: None, (1,), [], [], [], [], stages=stages, sequential=True)[1]


_GELU_C = math.sqrt(2.0 / math.pi)


def _gelu(x):
    return 0.5 * x * (1.0 + jnp.tanh(_GELU_C * (x + 0.044715 * x * x * x)))


def _gelu_grad(x):
    t = jnp.tanh(_GELU_C * (x + 0.044715 * x * x * x))
    return 0.5 * (1.0 + t) + 0.5 * x * (1.0 - t * t) * (_GELU_C * (1.0 + 3.0 * 0.044715 * x * x))


def _sigmoid(x):
    return 1.0 / (1.0 + jnp.exp(-x))


def _dot(a, b, mode):
    dims = {"nn": (((1,), (0,)), ((), ())), "nt": (((1,), (1,)), ((), ())), "tn": (((0,), (0,)), ((), ()))}[mode]
    return lax.dot_general(a.astype(BF16), b.astype(BF16), dims, preferred_element_type=F32)


def _matmul(name, a, b, mode, outs, epi=None, extras=(), colsums=(), tm=2048, tn=512, tk=2048, b_off=0, n=None, m_off=0, m=None, stages=()):
    if mode == "tn":
        kk, mfull = a.shape
    else:
        mfull, kk = a.shape
    m = mfull if m is None else m
    n = (b.shape[0] if mode == "nt" else b.shape[1]) if n is None else n
    tm, tk = _pick(tm, m, m_off), _pick(tk, kk)
    tn = _pick(tn, n, b_off, *[off for _, _, off in extras])
    boff, moff = b_off // tn, m_off // tm
    nm, nn_, nk = m // tm, n // tn, kk // tk
    col_major = bool(colsums)
    grid = (nn_, nm, nk) if col_major else (nm, nn_, nk)

    def imap(f):
        if col_major:
            return lambda g0, g1, k: f(g1, g0, k)
        return f

    a_spec = (pl.BlockSpec((tk, tm), imap(lambda i, j, k: (k, i + moff))) if mode == "tn"
              else pl.BlockSpec((tm, tk), imap(lambda i, j, k: (i + moff, k))))
    b_spec = (pl.BlockSpec((tn, tk), imap(lambda i, j, k: (j + boff, k))) if mode == "nt"
              else pl.BlockSpec((tk, tn), imap(lambda i, j, k: (k, j + boff))))
    in_specs, operands = [a_spec, b_spec], [a, b]
    for arr, kind, off in extras:
        o = off // tn
        if kind == "mn":
            in_specs.append(pl.BlockSpec((tm, tn), imap(lambda i, j, k, o=o: (i + moff, j + o))))
        else:
            in_specs.append(pl.BlockSpec((1, tn), imap(lambda i, j, k, o=o: (0, j + o))))
        operands.append(arr)
    out_shape = [jax.ShapeDtypeStruct((m, n), dt) for dt in outs] + [jax.ShapeDtypeStruct((1, n), F32) for _ in colsums]
    out_specs = ([pl.BlockSpec((tm, tn), imap(lambda i, j, k: (i, j))) for _ in outs]
                 + [pl.BlockSpec((1, tn), imap(lambda i, j, k: (0, j))) for _ in colsums])
    n_ex, n_out, n_cs = len(extras), len(outs), len(colsums)

    def body(*refs):
        a_ref, b_ref = refs[:2]
        ex_refs = refs[2:2 + n_ex]
        out_refs = refs[2 + n_ex:2 + n_ex + n_out]
        cs_refs = refs[2 + n_ex + n_out:2 + n_ex + n_out + n_cs]
        part = _dot(a_ref[...], b_ref[...], mode)

        def finish(acc):
            res = epi(acc, *[r[...] for r in ex_refs]) if epi is not None else (acc,)
            for r, v in zip(out_refs, res[:n_out]):
                r[...] = v.astype(r.dtype)
            if n_cs:
                @pl.when(pl.program_id(1) == 0)
                def _():
                    for r in cs_refs:
                        r[...] = jnp.zeros_like(r)

                for r, idx in zip(cs_refs, colsums):
                    r[...] += jnp.sum(res[idx], axis=0, keepdims=True)

        if nk == 1:
            finish(part)
        else:
            acc_ref = refs[-1]
            k = pl.program_id(2)

            @pl.when(k == 0)
            def _():
                acc_ref[...] = part

            @pl.when(k > 0)
            def _():
                acc_ref[...] += part

            @pl.when(k == nk - 1)
            def _():
                finish(acc_ref[...])

    sem = ("arbitrary", "arbitrary", "arbitrary") if col_major else ("parallel", "parallel", "arbitrary")
    return _call(name, body, grid, in_specs, out_specs, out_shape, operands,
                 scratch=[pltpu.VMEM((tm, tn), F32)] if nk > 1 else [], sem=sem, stages=stages)


def _row_spec(tr, c):
    return pl.BlockSpec((tr, c), lambda i: (i, 0))


def _fix_spec(shape):
    return pl.BlockSpec(shape, lambda *_: tuple(0 for _ in shape))


def _cast_bf16(name, x, tr=512):
    r, c = x.shape
    tr = _pick(tr, r)

    def body(x_ref, o_ref):
        o_ref[...] = x_ref[...].astype(BF16)

    return _call(name, body, (r // tr,), [_row_spec(tr, c)], [_row_spec(tr, c)], [jax.ShapeDtypeStruct((r, c), BF16)], [x],
                 sem=("parallel",))[0][0]


def _layer_norm_stats(x):
    mean = jnp.mean(x, axis=-1, keepdims=True)
    xc = x - mean
    var = jnp.mean(xc * xc, axis=-1, keepdims=True)
    rstd = lax.rsqrt(var + LN_EPS)
    return xc * rstd, rstd


def _layer_norm_bwd(dxhat, xhat, rstd):
    m1 = jnp.mean(dxhat, axis=-1, keepdims=True)
    m2 = jnp.mean(dxhat * xhat, axis=-1, keepdims=True)
    return rstd * (dxhat - m1 - xhat * m2)


def _ln1_fwd(pre1, g1, b1, tr=256, stages=()):
    s, d = pre1.shape
    tr = _pick(tr, s)

    def body(p_ref, g_ref, b_ref, xh_ref, rs_ref, h_ref):
        xhat, rstd = _layer_norm_stats(p_ref[...])
        xh_ref[...] = xhat
        rs_ref[...] = rstd
        h_ref[...] = (xhat * g_ref[...] + b_ref[...]).astype(BF16)

    return _call("ln1_fwd", body, (s // tr,), [_row_spec(tr, d), _fix_spec((1, d)), _fix_spec((1, d))],
                 [_row_spec(tr, d), _row_spec(tr, 1), _row_spec(tr, d)],
                 [jax.ShapeDtypeStruct((s, d), F32), jax.ShapeDtypeStruct((s, 1), F32), jax.ShapeDtypeStruct((s, d), BF16)],
                 [pre1, g1, b1], sem=("parallel",), stages=stages)


def _ln2_loss_bwd(ff, xhat1, g1, b1, g2, b2, target, tr=256):
    s, d = ff.shape
    tr = _pick(tr, s)

    def body(ff_ref, xh1_ref, g1_ref, b1_ref, g2_ref, b2_ref, t_ref, dp_ref, dpb_ref, dg_ref, db_ref, dbf_ref, loss_ref):
        @pl.when(pl.program_id(0) == 0)
        def _():
            dg_ref[...] = jnp.zeros_like(dg_ref)
            db_ref[...] = jnp.zeros_like(db_ref)
            dbf_ref[...] = jnp.zeros_like(dbf_ref)
            loss_ref[...] = jnp.zeros_like(loss_ref)

        h1 = xh1_ref[...] * g1_ref[...] + b1_ref[...]
        xhat, rstd = _layer_norm_stats(ALPHA * h1 + ff_ref[...])
        err = xhat * g2_ref[...] + b2_ref[...] - t_ref[...]
        row = jnp.mean(err * err, axis=-1, keepdims=True)
        loss_ref[...] += 0.5 * jnp.sum(row, axis=0, keepdims=True)
        dy = err / d
        dg_ref[...] += jnp.sum(dy * xhat, axis=0, keepdims=True)
        db_ref[...] += jnp.sum(dy, axis=0, keepdims=True)
        dpre = _layer_norm_bwd(dy * g2_ref[...], xhat, rstd)
        dbf_ref[...] += jnp.sum(dpre, axis=0, keepdims=True)
        dp_ref[...] = dpre
        dpb_ref[...] = dpre.astype(BF16)

    vec = _fix_spec((1, d))
    return _call("ln2_loss_bwd", body, (s // tr,), [_row_spec(tr, d), _row_spec(tr, d), vec, vec, vec, vec, _row_spec(tr, d)],
                 [_row_spec(tr, d), _row_spec(tr, d), vec, vec, vec, _fix_spec((1, 1))],
                 [jax.ShapeDtypeStruct((s, d), F32), jax.ShapeDtypeStruct((s, d), BF16)]
                 + [jax.ShapeDtypeStruct((1, d), F32)] * 3 + [jax.ShapeDtypeStruct((1, 1), F32)],
                 [ff, xhat1, g1, b1, g2, b2, target])[0]


def _ln1_bwd(dh1, xhat1, rstd1, g1, tr=256, stages=()):
    s, d = dh1.shape
    tr = _pick(tr, s)

    def body(dh_ref, xh_ref, rs_ref, g_ref, dp_ref, dpb_ref, dg_ref, db_ref):
        @pl.when(pl.program_id(0) == 0)
        def _():
            dg_ref[...] = jnp.zeros_like(dg_ref)
            db_ref[...] = jnp.zeros_like(db_ref)

        dh, xhat = dh_ref[...], xh_ref[...]
        dg_ref[...] += jnp.sum(dh * xhat, axis=0, keepdims=True)
        db_ref[...] += jnp.sum(dh, axis=0, keepdims=True)
        dpre = _layer_norm_bwd(dh * g_ref[...], xhat, rs_ref[...])
        dp_ref[...] = dpre
        dpb_ref[...] = dpre.astype(BF16)

    vec = _fix_spec((1, d))
    return _call("ln1_bwd", body, (s // tr,), [_row_spec(tr, d), _row_spec(tr, d), _row_spec(tr, 1), vec],
                 [_row_spec(tr, d), _row_spec(tr, d), vec, vec],
                 [jax.ShapeDtypeStruct((s, d), F32), jax.ShapeDtypeStruct((s, d), BF16)] + [jax.ShapeDtypeStruct((1, d), F32)] * 2,
                 [dh1, xhat1, rstd1, g1], stages=stages)


def _to_perm(x):
    return x.reshape(SEQ // N_SUB, N_SUB, -1).transpose(1, 0, 2).reshape(SEQ, -1)


def _from_perm(x):
    return x.reshape(N_SUB, SEQ // N_SUB, -1).transpose(1, 0, 2).reshape(SEQ, -1)


def _local_index(p):
    rho = np.arange(BLOCK)
    if p == 0:
        return 16 * (rho % 8) + rho // 8
    if p == 1:
        return 4 * (rho % 32) + rho // 32
    return rho


def _tile_view(x, p):
    c = x.shape[1]
    if p == 1:
        return x.reshape(4, 4, BLOCK, c)
    return x.reshape(N_SUB, BLOCK, c)


def _view_shape(c, p):
    return (4, 4, BLOCK, c) if p == 1 else (N_SUB, BLOCK, c)


def _tile_spec(p, width, col, shift=0):
    nblk = SEQ // DILATIONS[p] // BLOCK

    def blk(n):
        return jnp.clip(n + shift, 0, nblk - 1)

    if p == 0:
        return pl.BlockSpec((N_SUB, SUBLANES, width), lambda s, n: (0, blk(n), col))
    if p == 1:
        return pl.BlockSpec((4, None, 32, width), lambda s, n: (0, s, blk(n), col))
    return pl.BlockSpec((None, BLOCK, width), lambda s, n: (s, 0, col))


def _tile_grid(p):
    return ((1, 16), (4, 4), (16, 1))[p]


def _t5_bucket(n):
    max_exact = N_BUCKETS // 2
    nf = np.maximum(n, 1).astype(np.float32)
    large = max_exact + (np.log(nf / np.float32(max_exact)) / np.float32(math.log(MAX_DISTANCE / max_exact))
                         * np.float32(N_BUCKETS - max_exact)).astype(np.int32)
    large = np.minimum(large, N_BUCKETS - 1)
    return np.where(n < max_exact, n, large).astype(np.int32)


def _bucket_tables():
    tabs = np.zeros((3, 2, BLOCK, BLOCK), np.int32)
    for p, d in enumerate(DILATIONS):
        i = _local_index(p)
        diff = i[:, None] - i[None, :]
        tabs[p, 0] = np.where(diff <= 0, _t5_bucket((BLOCK + diff) * d), -1)
        tabs[p, 1] = np.where(diff >= 0, _t5_bucket(np.maximum(diff, 0) * d), -1)
    return tabs


def _bias_expand(rel_bias, buckets):
    nh = N_HEADS

    def body(rb_ref, bk_ref, o_ref):
        for w in range(2):
            bk = bk_ref[0, w]
            for h in range(nh):
                val = jnp.zeros((BLOCK, BLOCK), F32)
                for b in range(N_BUCKETS):
                    val = jnp.where(bk == b, rb_ref[b, h], val)
                o_ref[0, h, w] = jnp.where(bk < 0, NEG_INF, val)

    return _call("bias_expand", body, (3,),
                 [pl.BlockSpec(memory_space=pltpu.SMEM), pl.BlockSpec((1, 2, BLOCK, BLOCK), lambda p: (p, 0, 0, 0))],
                 [pl.BlockSpec((1, nh, 2, BLOCK, BLOCK), lambda p: (p, 0, 0, 0, 0))],
                 [jax.ShapeDtypeStruct((3, nh, 2, BLOCK, BLOCK), F32)], [rel_bias, buckets], sem=("parallel",))[0][0]


def _heads_to_lanes(cols):
    lane = lax.broadcasted_iota(I32, (BLOCK, LANES), 1)
    out = jnp.zeros((BLOCK, LANES), F32)
    for h, c in enumerate(cols):
        out = jnp.where(lane == h, c, out)
    return out


def _attn_fwd(qkv, bias, p, stages=()):
    d_a = _d_a()
    has_prev = SEQ // DILATIONS[p] // BLOCK > 1
    scale = HEAD_DIM ** -0.5
    view = _tile_view(qkv, p)

    def body(q_ref, kc_ref, kp_ref, vc_ref, vp_ref, b_ref, o_ref, l_ref):
        n = pl.program_id(1)
        q_all = q_ref[...].reshape(BLOCK, d_a).astype(BF16)
        kc_all = kc_ref[...].reshape(BLOCK, d_a).astype(BF16)
        vc_all = vc_ref[...].reshape(BLOCK, d_a).astype(BF16)
        if has_prev:
            kp_all = kp_ref[...].reshape(BLOCK, d_a).astype(BF16)
            vp_all = vp_ref[...].reshape(BLOCK, d_a).astype(BF16)
        outs, lses = [], []
        for h in range(N_HEADS):
            sl = slice(h * HEAD_DIM, (h + 1) * HEAD_DIM)
            q = q_all[:, sl]
            sc = _dot(q, kc_all[:, sl], "nt") * scale + b_ref[0, h, 1]
            m = jnp.max(sc, axis=-1, keepdims=True)
            if has_prev:
                sp = _dot(q, kp_all[:, sl], "nt") * scale + b_ref[0, h, 0]
                sp = jnp.where(n > 0, sp, NEG_INF)
                m = jnp.maximum(m, jnp.max(sp, axis=-1, keepdims=True))
                pp = jnp.exp(sp - m)
            pc = jnp.exp(sc - m)
            den = jnp.sum(pc, axis=-1, keepdims=True)
            o = _dot(pc, vc_all[:, sl], "nn")
            if has_prev:
                den = den + jnp.sum(pp, axis=-1, keepdims=True)
                o = o + _dot(pp, vp_all[:, sl], "nn")
            outs.append(o / den)
            lses.append(m + jnp.log(den))
        o_ref[...] = jnp.concatenate(outs, axis=-1).reshape(o_ref.shape)
        l_ref[...] = _heads_to_lanes(lses).reshape(l_ref.shape)

    (o, l), st = _call(
        f"attn_fwd{p}", body, _tile_grid(p),
        [_tile_spec(p, d_a, 0), _tile_spec(p, d_a, 1), _tile_spec(p, d_a, 1, -1), _tile_spec(p, d_a, 2), _tile_spec(p, d_a, 2, -1),
         pl.BlockSpec((1, N_HEADS, 2, BLOCK, BLOCK), lambda s, n: (p, 0, 0, 0, 0))],
        [_tile_spec(p, d_a, 0), _tile_spec(p, LANES, 0)],
        [jax.ShapeDtypeStruct(_view_shape(d_a, p), F32), jax.ShapeDtypeStruct(_view_shape(LANES, p), F32)],
        [view, view, view, view, view, bias], sem=("parallel", "parallel"), stages=stages)
    return (o.reshape(SEQ, d_a), l.reshape(SEQ, LANES)), st


def _attn_combine(os_, ls_, tr=256, stages=()):
    d_a = _d_a()
    tr = _pick(tr, SEQ)

    def body(o0, o1, o2, l0, l1, l2, a_ref, ab_ref, lt_ref):
        l = [l0[...], l1[...], l2[...]]
        m = jnp.maximum(jnp.maximum(l[0], l[1]), l[2])
        w = [jnp.exp(x - m) for x in l]
        tot = w[0] + w[1] + w[2]
        lt_ref[...] = m + jnp.log(tot)
        w = [x / tot for x in w]
        for h in range(N_HEADS):
            sl = slice(h * HEAD_DIM, (h + 1) * HEAD_DIM)
            acc = w[0][:, h:h + 1] * o0[:, sl] + w[1][:, h:h + 1] * o1[:, sl] + w[2][:, h:h + 1] * o2[:, sl]
            a_ref[:, sl] = acc
            ab_ref[:, sl] = acc.astype(BF16)

    return _call("attn_combine", body, (SEQ // tr,), [_row_spec(tr, d_a)] * 3 + [_row_spec(tr, LANES)] * 3,
                 [_row_spec(tr, d_a), _row_spec(tr, d_a), _row_spec(tr, LANES)],
                 [jax.ShapeDtypeStruct((SEQ, d_a), F32), jax.ShapeDtypeStruct((SEQ, d_a), BF16), jax.ShapeDtypeStruct((SEQ, LANES), F32)],
                 [*os_, *ls_], sem=("parallel",), stages=stages)


def _attn_delta(dattn, attn, tr=256):
    d_a = _d_a()
    tr = _pick(tr, SEQ)

    def body(d_ref, a_ref, o_ref):
        prod = d_ref[...] * a_ref[...]
        lane = lax.broadcasted_iota(I32, (tr, LANES), 1)
        out = jnp.zeros((tr, LANES), F32)
        for h in range(N_HEADS):
            out = jnp.where(lane == h, jnp.sum(prod[:, h * HEAD_DIM:(h + 1) * HEAD_DIM], axis=-1, keepdims=True), out)
        o_ref[...] = out

    return _call("attn_delta", body, (SEQ // tr,), [_row_spec(tr, d_a)] * 2, [_row_spec(tr, LANES)],
                 [jax.ShapeDtypeStruct((SEQ, LANES), F32)], [dattn, attn], sem=("parallel",))[0][0]


def _attn_bwd(qkv, dattn, lse, delta, bias, p, stages=()):
    d_a = _d_a()
    nblk = SEQ // DILATIONS[p] // BLOCK
    has_next = nblk > 1
    scale = HEAD_DIM ** -0.5
    qv, dov, lv, tv = (_tile_view(x, p) for x in (qkv, dattn, lse, delta))

    def body(q_ref, qn_ref, k_ref, v_ref, do_ref, don_ref, l_ref, ln_ref, t_ref, tn_ref, b_ref, dq_ref, dk_ref, dv_ref, db_ref, carry_ref):
        j = pl.program_id(1)

        @pl.when((pl.program_id(0) == 0) & (j == 0))
        def _():
            db_ref[...] = jnp.zeros_like(db_ref)

        k_all = k_ref[...].reshape(BLOCK, d_a).astype(BF16)
        v_all = v_ref[...].reshape(BLOCK, d_a).astype(BF16)

        def side(qr, dor, lr, tr_, w):
            q_all = qr[...].reshape(BLOCK, d_a).astype(BF16)
            do_all = dor[...].reshape(BLOCK, d_a).astype(BF16)
            l_all = lr[...].reshape(BLOCK, LANES)
            t_all = tr_[...].reshape(BLOCK, LANES)
            dqs, dks, dvs = [], [], []
            for h in range(N_HEADS):
                sl = slice(h * HEAD_DIM, (h + 1) * HEAD_DIM)
                s = _dot(q_all[:, sl], k_all[:, sl], "nt") * scale + b_ref[0, h, w]
                pr = jnp.exp(s - l_all[:, h:h + 1])
                dp = _dot(do_all[:, sl], v_all[:, sl], "nt")
                ds = pr * (dp - t_all[:, h:h + 1])
                db_ref[h, w] += ds
                dqs.append(_dot(ds, k_all[:, sl], "nn") * scale)
                dks.append(_dot(ds, q_all[:, sl], "tn") * scale)
                dvs.append(_dot(pr, do_all[:, sl], "tn"))
            return [jnp.concatenate(x, axis=-1) for x in (dqs, dks, dvs)]

        dq_c, dk_c, dv_c = side(q_ref, do_ref, l_ref, t_ref, 1)
        if has_next:
            dq_ref[...] = (jnp.where(j > 0, carry_ref[...], 0.0) + dq_c).reshape(dq_ref.shape)
            not_last = j < nblk - 1

            @pl.when(not_last)
            def _():
                dq_n, dk_n, dv_n = side(qn_ref, don_ref, ln_ref, tn_ref, 0)
                carry_ref[...] = dq_n
                dk_ref[...] = (dk_c + dk_n).reshape(dk_ref.shape)
                dv_ref[...] = (dv_c + dv_n).reshape(dv_ref.shape)

            @pl.when(jnp.logical_not(not_last))
            def _():
                dk_ref[...] = dk_c.reshape(dk_ref.shape)
                dv_ref[...] = dv_c.reshape(dv_ref.shape)
        else:
            dq_ref[...] = dq_c.reshape(dq_ref.shape)
            dk_ref[...] = dk_c.reshape(dk_ref.shape)
            dv_ref[...] = dv_c.reshape(dv_ref.shape)

    def big(col, shift=0):
        return _tile_spec(p, d_a, col, shift)

    def small(shift=0):
        return _tile_spec(p, LANES, 0, shift)

    (dq, dk, dv, dbias), st = _call(
        f"attn_bwd{p}", body, _tile_grid(p),
        [big(0), big(0, 1), big(1), big(2), big(0), big(0, 1), small(), small(1), small(), small(1),
         pl.BlockSpec((1, N_HEADS, 2, BLOCK, BLOCK), lambda s, n: (p, 0, 0, 0, 0))],
        [big(0), big(0), big(0), pl.BlockSpec((N_HEADS, 2, BLOCK, BLOCK), lambda s, n: (0, 0, 0, 0))],
        [jax.ShapeDtypeStruct(_view_shape(d_a, p), F32)] * 3 + [jax.ShapeDtypeStruct((N_HEADS, 2, BLOCK, BLOCK), F32)],
        [qv, qv, qv, qv, dov, dov, lv, lv, tv, tv, bias], scratch=[pltpu.VMEM((BLOCK, d_a), F32)], stages=stages)
    return (dq.reshape(SEQ, d_a), dk.reshape(SEQ, d_a), dv.reshape(SEQ, d_a), dbias), st


def _rel_bias_grad(dbias, buckets):
    nh = N_HEADS

    def body(d0, d1, d2, bk_ref, o_ref, t_ref):
        ds = (d0, d1, d2)

        def per_bucket(b, carry):
            for h in range(nh):
                acc = jnp.zeros((BLOCK, BLOCK), F32)
                for p in range(3):
                    for w in range(2):
                        acc = acc + jnp.where(bk_ref[p, w] == b, ds[p][h, w], 0.0)
                t_ref[pl.ds(b * nh + h, 1), :] = jnp.sum(acc, axis=0, keepdims=True)
            return carry

        lax.fori_loop(0, N_BUCKETS, per_bucket, 0)
        o_ref[...] = jnp.sum(t_ref[...], axis=-1, keepdims=True)

    return _call("rel_bias_grad", body, (1,), [_fix_spec((nh, 2, BLOCK, BLOCK))] * 3 + [_fix_spec((3, 2, BLOCK, BLOCK))],
                 [_fix_spec((N_BUCKETS * nh, 1))], [jax.ShapeDtypeStruct((N_BUCKETS * nh, 1), F32)], [*dbias, buckets],
                 scratch=[pltpu.VMEM((N_BUCKETS * nh, LANES), F32)])[0][0]


def _gmlp_fwd(rest, gain, bias, ws, bs, causal, stages=()):
    d_b = _d_b()

    def body(u_ref, v_ref, g_ref, b_ref, ws_ref, bs_ref, c_ref, o_ref):
        u = u_ref[...].reshape(BLOCK, d_b)
        xhat, _ = _layer_norm_stats(_gelu(v_ref[...].reshape(BLOCK, d_b)))
        vn = (xhat * g_ref[...] + b_ref[...]).astype(BF16)
        outs = []
        for g in range(N_GROUPS):
            sl = slice(g * BLOCK, (g + 1) * BLOCK)
            w = jnp.where(c_ref[...] > 0, ws_ref[g], 0.0)
            z = _dot(w, vn[:, sl], "nn") + bs_ref[:, g:g + 1]
            outs.append(_gelu(u[:, sl]) * z)
        o_ref[...] = jnp.concatenate(outs, axis=-1).reshape(o_ref.shape)

    (out,), st = _call(
        "gmlp_fwd", body, (1, SEQ // BLOCK),
        [_tile_spec(0, d_b, 0), _tile_spec(0, d_b, 1), _fix_spec((1, d_b)), _fix_spec((1, d_b)),
         _fix_spec((N_GROUPS, BLOCK, BLOCK)), _fix_spec((BLOCK, N_GROUPS)), _fix_spec((BLOCK, BLOCK))],
        [_tile_spec(0, d_b, 0)], [jax.ShapeDtypeStruct(_view_shape(d_b, 0), F32)],
        [_tile_view(rest, 0), _tile_view(rest, 0), gain, bias, ws, bs, causal], sem=("parallel", "parallel"), stages=stages)
    return out.reshape(SEQ, d_b), st


def _gmlp_bwd(rest, dgmlp, gain, bias, ws, bs, causal, stages=()):
    d_b = _d_b()
    nchunk = SEQ // BLOCK

    def body(u_ref, v_ref, dg_ref, g_ref, b_ref, ws_ref, bs_ref, c_ref, du_ref, dv_ref, dws_ref, dbs_ref, dgain_ref, dbias_ref):
        c = pl.program_id(1)

        @pl.when(c == 0)
        def _():
            dws_ref[...] = jnp.zeros_like(dws_ref)
            dbs_ref[...] = jnp.zeros_like(dbs_ref)
            dgain_ref[...] = jnp.zeros_like(dgain_ref)
            dbias_ref[...] = jnp.zeros_like(dbias_ref)

        u = u_ref[...].reshape(BLOCK, d_b)
        v = v_ref[...].reshape(BLOCK, d_b)
        dgm = dg_ref[...].reshape(BLOCK, d_b)
        xhat, rstd = _layer_norm_stats(_gelu(v))
        vn = (xhat * g_ref[...] + b_ref[...]).astype(BF16)
        lane = lax.broadcasted_iota(I32, (BLOCK, LANES), 1)
        dus, dvns = [], []
        dbs = dbs_ref[...]
        for g in range(N_GROUPS):
            sl = slice(g * BLOCK, (g + 1) * BLOCK)
            w = jnp.where(c_ref[...] > 0, ws_ref[g], 0.0).astype(BF16)
            z = _dot(w, vn[:, sl], "nn") + bs_ref[:, g:g + 1]
            dz = dgm[:, sl] * _gelu(u[:, sl])
            dus.append(dgm[:, sl] * z * _gelu_grad(u[:, sl]))
            dws_ref[g] += _dot(dz, vn[:, sl], "nt")
            dbs = dbs + jnp.where(lane == g, jnp.sum(dz, axis=-1, keepdims=True), 0.0)
            dvns.append(_dot(w, dz, "tn"))
        dbs_ref[...] = dbs
        dvn = jnp.concatenate(dvns, axis=-1)
        dgain_ref[...] += jnp.sum(dvn * xhat, axis=0, keepdims=True)
        dbias_ref[...] += jnp.sum(dvn, axis=0, keepdims=True)
        dvg = _layer_norm_bwd(dvn * g_ref[...], xhat, rstd)
        du_ref[...] = jnp.concatenate(dus, axis=-1).reshape(du_ref.shape)
        dv_ref[...] = (dvg * _gelu_grad(v)).reshape(dv_ref.shape)

        @pl.when(c == nchunk - 1)
        def _():
            for g in range(N_GROUPS):
                dws_ref[g] = jnp.where(c_ref[...] > 0, dws_ref[g], 0.0)

    (du, dv, dws, dbs, dgain, dbias), st = _call(
        "gmlp_bwd", body, (1, nchunk),
        [_tile_spec(0, d_b, 0), _tile_spec(0, d_b, 1), _tile_spec(0, d_b, 0), _fix_spec((1, d_b)), _fix_spec((1, d_b)),
         _fix_spec((N_GROUPS, BLOCK, BLOCK)), _fix_spec((BLOCK, N_GROUPS)), _fix_spec((BLOCK, BLOCK))],
        [_tile_spec(0, d_b, 0), _tile_spec(0, d_b, 0), _fix_spec((N_GROUPS, BLOCK, BLOCK)), _fix_spec((BLOCK, LANES)),
         _fix_spec((1, d_b)), _fix_spec((1, d_b))],
        [jax.ShapeDtypeStruct(_view_shape(d_b, 0), F32)] * 2
        + [jax.ShapeDtypeStruct((N_GROUPS, BLOCK, BLOCK), F32), jax.ShapeDtypeStruct((BLOCK, LANES), F32)]
        + [jax.ShapeDtypeStruct((1, d_b), F32)] * 2,
        [_tile_view(rest, 0), _tile_view(rest, 0), _tile_view(dgmlp, 0), gain, bias, ws, bs, causal], stages=stages)
    return (du.reshape(SEQ, d_b), dv.reshape(SEQ, d_b), dws, dbs, dgain, dbias), st


def _assemble_dproj(dqkv, du, dv, dga, dgb, tr=128):
    d_a, d_b, d_in = _d_a(), _d_b(), _d_in()
    tr = _pick(tr, SEQ)

    def body(*refs):
        att, (du_ref, dv_ref, dga_ref, dgb_ref, o_ref) = refs[:9], refs[9:]
        for i in range(3):
            o_ref[:, i * d_a:(i + 1) * d_a] = (att[3 * i][...] + att[3 * i + 1][...] + att[3 * i + 2][...]).astype(BF16)
        o_ref[:, 3 * d_a:3 * d_a + d_b] = du_ref[...].astype(BF16)
        o_ref[:, 3 * d_a + d_b:3 * d_a + 2 * d_b] = dv_ref[...].astype(BF16)
        o_ref[:, 3 * d_a + 2 * d_b:3 * d_a + 2 * d_b + D_MODEL] = dga_ref[...]
        o_ref[:, 3 * d_a + 2 * d_b + D_MODEL:] = dgb_ref[...]

    return _call("assemble_dproj", body, (SEQ // tr,), [_row_spec(tr, d_a)] * 9 + [_row_spec(tr, d_b)] * 2 + [_row_spec(tr, D_MODEL)] * 2,
                 [_row_spec(tr, d_in)], [jax.ShapeDtypeStruct((SEQ, d_in), BF16)], [*dqkv, du, dv, dga, dgb], sem=("parallel",))[0][0]


def _dw(name, a, b, kind, core, mine, add=None, tn=1152, stages=()):
    s, m = a.shape
    n = b.shape[1]
    rs, cs = (m, n // N_DEV) if kind == "col" else (m // N_DEV, n)
    tn = _pick(tn if kind == "col" else 512, cs)
    nj = cs // tn

    def shard(q, c_ref):
        return 2 * q + (c_ref[0] if mine else 1 - c_ref[0])

    if kind == "col":
        a_spec = pl.BlockSpec((s, m), lambda q, j, c_ref: (0, 0))
        b_spec = pl.BlockSpec((s, tn), lambda q, j, c_ref: (0, shard(q, c_ref) * nj + j))
    else:
        a_spec = pl.BlockSpec((s, rs), lambda q, j, c_ref: (0, shard(q, c_ref)))
        b_spec = pl.BlockSpec((s, tn), lambda q, j, c_ref: (0, j))
    o_spec = pl.BlockSpec((None, rs, tn), lambda q, j, c_ref: (q, 0, j))

    def body(a_ref, b_ref, *rest):
        acc = _dot(a_ref[...], b_ref[...], "tn")
        if add is not None:
            acc = acc + rest[0][...].astype(F32)
        rest[-1][...] = acc.astype(BF16)

    (out,), st = _call(name, body, (N_CHIPS, nj), [a_spec, b_spec] + ([o_spec] if add is not None else []), [o_spec],
                       [jax.ShapeDtypeStruct((N_CHIPS, rs, cs), BF16)], [a, b] + ([add] if add is not None else []),
                       sem=("parallel", "parallel"), stages=stages, prefetch=core)
    return out, st


def _adamw(w, g, m, v):
    m = ADAM_B1 * m + (1.0 - ADAM_B1) * g
    v = ADAM_B2 * v + (1.0 - ADAM_B2) * (g * g)
    m_hat = m / (1.0 - ADAM_B1 ** ADAM_STEP)
    v_hat = v / (1.0 - ADAM_B2 ** ADAM_STEP)
    delta = -ADAM_LR * (m_hat / (jnp.sqrt(v_hat) + ADAM_EPS) + ADAM_WD * w)
    return delta, m, v


def _adam_shard(name, chip_sums, w, m, v, tr=256, stages=()):
    rs, cs = w.shape
    tr = _pick(tr, rs)

    def body(s_ref, w_ref, m_ref, v_ref, g_ref, d_ref, nm_ref, nv_ref):
        g = s_ref[0].astype(F32)
        for q in range(1, N_CHIPS):
            g = g + s_ref[q].astype(F32)
        d, nm, nv = _adamw(w_ref[...], g, m_ref[...], v_ref[...])
        g_ref[...], d_ref[...], nm_ref[...], nv_ref[...] = g, d, nm, nv

    spec = _row_spec(tr, cs)
    return _call(name, body, (rs // tr,), [pl.BlockSpec((N_CHIPS, tr, cs), lambda i: (0, i, 0)), spec, spec, spec], [spec] * 4,
                 [jax.ShapeDtypeStruct((rs, cs), F32)] * 4, [chip_sums, w, m, v], sem=("parallel",), stages=stages)


def _adam_small(parts, w, m, v):
    rows = w.shape[0]

    def body(p_ref, w_ref, m_ref, v_ref, g_ref, d_ref, nm_ref, nv_ref):
        g = p_ref[0]
        for j in range(1, N_DEV):
            g = g + p_ref[j]
        d, nm, nv = _adamw(w_ref[...], g, m_ref[...], v_ref[...])
        g_ref[...], d_ref[...], nm_ref[...], nv_ref[...] = g, d, nm, nv

    spec = _fix_spec((rows, LANES))
    return _call("adam_small", body, (1,), [_fix_spec((N_DEV, rows, LANES)), spec, spec, spec], [spec] * 4,
                 [jax.ShapeDtypeStruct((rows, LANES), F32)] * 4, [parts, w, m, v])[0]


def _small_sizes():
    d_b = _d_b()
    return (("loss", 1), ("rel_bias", N_BUCKETS * N_HEADS), ("ln_v_gain", d_b), ("ln_v_bias", d_b),
            ("w_spatial", N_GROUPS * BLOCK * BLOCK), ("b_spatial", N_GROUPS * BLOCK), ("ln1_gain", D_MODEL), ("ln1_bias", D_MODEL),
            ("b_ff1", D_FF), ("b_ff2", D_MODEL), ("ln2_gain", D_MODEL), ("ln2_bias", D_MODEL))


def _pack(vals):
    pieces = []
    for name, size in _small_sizes():
        flat = vals[name].reshape(-1).astype(F32)
        padded = -(-size // (SUBLANES * LANES)) * SUBLANES * LANES
        pieces.append(jnp.pad(flat, (0, padded - size)).reshape(-1, LANES))
    return jnp.concatenate(pieces, axis=0)


def _unpack(buf):
    out, row = {}, 0
    for name, size in _small_sizes():
        rows = -(-size // (SUBLANES * LANES)) * SUBLANES
        out[name] = buf[row:row + rows].reshape(-1)[:size]
        row += rows
    return out


def kernel(x, w_in, rel_bias, ln_v_gain, ln_v_bias, w_spatial, b_spatial, w_proj_a, w_proj_b, w_out, ln1_gain, ln1_bias, w_ff1, b_ff1, w_ff2, b_ff2, ln2_gain, ln2_bias, loss_target, m_w_in, m_rel_bias, m_ln_v_gain, m_ln_v_bias, m_w_spatial, m_b_spatial, m_w_proj_a, m_w_proj_b, m_w_out, m_ln1_gain, m_ln1_bias, m_w_ff1, m_b_ff1, m_w_ff2, m_b_ff2, m_ln2_gain, m_ln2_bias, v_w_in, v_rel_bias, v_ln_v_gain, v_ln_v_bias, v_w_spatial, v_b_spatial, v_w_proj_a, v_w_proj_b, v_w_out, v_ln1_gain, v_ln1_bias, v_w_ff1, v_b_ff1, v_w_ff2, v_b_ff2, v_ln2_gain, v_ln2_bias):
    d_a, d_b, d_in = _d_a(), _d_b(), _d_in()
    weights = dict(w_in=w_in, rel_bias=rel_bias, ln_v_gain=ln_v_gain, ln_v_bias=ln_v_bias, w_spatial=w_spatial, b_spatial=b_spatial,
                   w_proj_a=w_proj_a, w_proj_b=w_proj_b, w_out=w_out, ln1_gain=ln1_gain, ln1_bias=ln1_bias, w_ff1=w_ff1, b_ff1=b_ff1,
                   w_ff2=w_ff2, b_ff2=b_ff2, ln2_gain=ln2_gain, ln2_bias=ln2_bias)
    mom1 = dict(w_in=m_w_in, rel_bias=m_rel_bias, ln_v_gain=m_ln_v_gain, ln_v_bias=m_ln_v_bias, w_spatial=m_w_spatial,
                b_spatial=m_b_spatial, w_proj_a=m_w_proj_a, w_proj_b=m_w_proj_b, w_out=m_w_out, ln1_gain=m_ln1_gain,
                ln1_bias=m_ln1_bias, w_ff1=m_w_ff1, b_ff1=m_b_ff1, w_ff2=m_w_ff2, b_ff2=m_b_ff2, ln2_gain=m_ln2_gain, ln2_bias=m_ln2_bias)
    mom2 = dict(w_in=v_w_in, rel_bias=v_rel_bias, ln_v_gain=v_ln_v_gain, ln_v_bias=v_ln_v_bias, w_spatial=v_w_spatial,
                b_spatial=v_b_spatial, w_proj_a=v_w_proj_a, w_proj_b=v_w_proj_b, w_out=v_w_out, ln1_gain=v_ln1_gain,
                ln1_bias=v_ln1_bias, w_ff1=v_w_ff1, b_ff1=v_b_ff1, w_ff2=v_w_ff2, b_ff2=v_b_ff2, ln2_gain=v_ln2_gain, ln2_bias=v_ln2_bias)

    shard = {n: _cast_bf16(f"cast_{n}", weights[n][0]) for n in KINDS}
    full, sent = {}, {n: 0 for n in KINDS}
    for n, kind in KINDS.items():
        r, c = shard[n].shape
        full[n] = lax.empty((r, c * N_DEV) if kind == "col" else (r * N_DEV, c), BF16)

    def keep(table, n):
        def store(outs):
            table[n] = outs[0]
        return store

    def gather(n, units=16):
        st = _spread_stage(shard[n], full[n], KINDS[n], (sent[n], units))
        sent[n] += units
        st.store = keep(full, n)
        return st

    def pass_on(n):
        st = _forward_stage(full[n], KINDS[n])
        st.store = keep(full, n)
        return st

    def settle(stages, outs):
        for st, o in zip(stages, outs):
            st.store(o)

    def alone(name, stage):
        settle([stage], _comm_only(name, [stage]))

    alone("gather_w_in_chips", gather("w_in"))
    alone("gather_w_in_sibling", pass_on("w_in"))

    xs = _to_perm(x[0])
    target = _to_perm(loss_target[0])
    xb = _cast_bf16("cast_x", xs)
    g8 = BLOCK // N_SUB
    ws_t = w_spatial[0].reshape(N_GROUPS, g8, N_SUB, g8, N_SUB).transpose(0, 2, 1, 4, 3).reshape(N_GROUPS, BLOCK, BLOCK)
    bs_t = b_spatial[0].reshape(N_GROUPS, g8, N_SUB).transpose(2, 1, 0).reshape(BLOCK, N_GROUPS)
    idx = _local_index(0)
    causal = jnp.asarray((idx[:, None] >= idx[None, :]).astype(np.float32))
    buckets = jnp.asarray(_bucket_tables())
    bias = _bias_expand(rel_bias, buckets)

    hosted = [gather("w_proj_a")]
    (qkv,), st = _matmul("proj_qkv", xb, full["w_in"], "nn", [F32], n=3 * d_a, stages=hosted)
    settle(hosted, st)
    hosted = [gather("w_proj_b"), gather("w_out", 12)]
    (rest,), st = _matmul("proj_rest", xb, full["w_in"], "nn", [F32], b_off=3 * d_a, n=d_in - 3 * d_a, stages=hosted)
    settle(hosted, st)
    fwd = []
    for p in range(3):
        hosted = [gather("w_ff1", 2 if p == 0 else 3)]
        if p == 0:
            hosted += [pass_on("w_proj_a"), pass_on("w_proj_b"), gather("w_out", 4)]
        if p == 1:
            hosted.append(pass_on("w_out"))
        res, st = _attn_fwd(qkv, bias, p, stages=hosted)
        settle(hosted, st)
        fwd.append(res)
    hosted = [gather("w_ff1", 1)]
    (attn, attn_b, lse), st = _attn_combine([o for o, _ in fwd], [l for _, l in fwd], stages=hosted)
    settle(hosted, st)
    hosted = [gather("w_ff1", 1)]
    gmlp, st = _gmlp_fwd(rest, ln_v_gain, ln_v_bias, ws_t, bs_t, causal, stages=hosted)
    settle(hosted, st)
    hosted = [gather("w_ff1", 2)]
    (ya,), st = _matmul("proj_a", attn_b, full["w_proj_a"], "nn", [F32], stages=hosted)
    settle(hosted, st)
    gate_a, gate_b = 2 * d_b, 2 * d_b + D_MODEL

    def merge(acc, ya_, ga, gb):
        return acc, _sigmoid(ga) * ya_ + _sigmoid(gb) * acc

    hosted = [gather("w_ff1", 3)]
    (yb, merged), st = _matmul("proj_b_merge", gmlp, full["w_proj_b"], "nn", [F32, BF16], merge,
                               [(ya, "mn", 0), (rest, "mn", gate_a), (rest, "mn", gate_b)], tn=256, stages=hosted)
    settle(hosted, st)
    hosted = [gather("w_ff1", 1), gather("w_ff2", 2)]
    (pre1,), st = _matmul("out_proj", merged, full["w_out"], "nn", [F32], lambda acc, x_: (ALPHA * x_ + acc,), [(xs, "mn", 0)], stages=hosted)
    settle(hosted, st)
    hosted = [pass_on("w_ff1"), gather("w_ff2", 1)]
    (xhat1, rstd1, h1b), st = _ln1_fwd(pre1, ln1_gain, ln1_bias, stages=hosted)
    settle(hosted, st)

    def relu2(acc, b_):
        r = jnp.maximum(acc + b_, 0.0)
        return r, r * r

    hosted = [gather("w_ff2", 7)]
    (relu, fb), st = _matmul("ff1", h1b, full["w_ff1"], "nn", [F32, BF16], relu2, [(b_ff1, "row", 0)], stages=hosted)
    settle(hosted, st)
    alone("gather_w_ff2_chips", gather("w_ff2", 6))
    alone("gather_w_ff2_sibling", pass_on("w_ff2"))
    assert all(v == 16 for v in sent.values()), sent
    (ff,), _ = _matmul("ff2", fb, full["w_ff2"], "nn", [F32], lambda acc, b_: (acc + b_,), [(b_ff2, "row", 0)], tn=1024, tk=1024)

    core = lax.axis_index("c").astype(I32).reshape(1)
    factors, theirs, sib, pair, chips, reduced = {}, {}, {}, {}, {}, {n: 0 for n in KINDS}

    def grad_for_sibling(n, a, b, stages=()):
        factors[n] = (a, b)
        theirs[n], outs = _dw(f"dw_{n}_sibling", a, b, KINDS[n], core, False, stages=stages)
        settle(stages, outs)

    def to_sibling(n):
        st = _to_sibling_stage(theirs[n])
        st.store = keep(sib, n)
        return st

    def grad_own(n, stages=()):
        pair[n], outs = _dw(f"dw_{n}_own", *factors[n], KINDS[n], core, True, add=sib[n], stages=stages)
        settle(stages, outs)
        chips[n] = lax.empty(pair[n].shape, BF16)

    def to_chips(n, units=16):
        st = _to_chips_stage(pair[n], chips[n], (reduced[n], units))
        reduced[n] += units
        st.store = keep(chips, n)
        return st

    dpre2, dpre2b, g_ln2_gain, g_ln2_bias, g_b_ff2, loss_part = _ln2_loss_bwd(ff, xhat1, ln1_gain, ln1_bias, ln2_gain, ln2_bias, target)
    grad_for_sibling("w_ff2", fb, dpre2b)

    def relu2_bwd(acc, r):
        da = acc * (2.0 * r)
        return da, da

    hosted = [to_sibling("w_ff2")]
    (dab, g_b_ff1), st = _matmul("d_ff1", dpre2b, full["w_ff2"], "nt", [BF16], relu2_bwd, [(relu, "mn", 0)], colsums=(1,), stages=hosted)
    settle(hosted, st)
    grad_own("w_ff2")
    grad_for_sibling("w_ff1", h1b, dab, [to_chips("w_ff2", 4)])
    hosted = [to_chips("w_ff2", 9), to_sibling("w_ff1")]
    (dh1,), st = _matmul("d_h1", dab, full["w_ff1"], "nt", [F32], lambda acc, d_: (acc + ALPHA * d_,), [(dpre2, "mn", 0)], stages=hosted)
    settle(hosted, st)
    grad_own("w_ff1", [to_chips("w_ff2", 3)])
    hosted = [to_chips("w_ff1", 2)]
    (dpre1, dpre1b, g_ln1_gain, g_ln1_bias), st = _ln1_bwd(dh1, xhat1, rstd1, ln1_gain, stages=hosted)
    settle(hosted, st)
    grad_for_sibling("w_out", merged, dpre1b, [to_chips("w_ff1", 1)])

    def merge_bwd(acc, ga, gb, ya_, yb_):
        sa, sb = _sigmoid(ga), _sigmoid(gb)
        return acc * sa, acc * sb, acc * ya_ * (sa * (1.0 - sa)), acc * yb_ * (sb * (1.0 - sb))

    hosted = [to_chips("w_ff1", 5), to_sibling("w_out")]
    (dya, dyb, dga, dgb), st = _matmul("d_merge", dpre1b, full["w_out"], "nt", [BF16] * 4, merge_bwd,
                                       [(rest, "mn", gate_a), (rest, "mn", gate_b), (ya, "mn", 0), (yb, "mn", 0)], tn=256, stages=hosted)
    settle(hosted, st)
    grad_own("w_out", [to_chips("w_ff1", 1)])
    grad_for_sibling("w_proj_a", attn_b, dya, [to_chips("w_ff1", 1)])
    grad_for_sibling("w_proj_b", gmlp, dyb)
    hosted = [to_chips("w_ff1", 1), to_sibling("w_proj_a"), to_sibling("w_proj_b")]
    (dattn,), st = _matmul("d_attn", dya, full["w_proj_a"], "nt", [F32], stages=hosted)
    settle(hosted, st)
    grad_own("w_proj_a")
    grad_own("w_proj_b")
    hosted = [to_chips("w_ff1", 1)]
    (dgmlp,), st = _matmul("d_gmlp", dyb, full["w_proj_b"], "nt", [F32], stages=hosted)
    settle(hosted, st)
    hosted = [to_chips("w_ff1", 2)]
    (du, dvb, dws_t, dbs_t, g_lnv_gain, g_lnv_bias), st = _gmlp_bwd(rest, dgmlp, ln_v_gain, ln_v_bias, ws_t, bs_t, causal, stages=hosted)
    settle(hosted, st)
    delta = _attn_delta(dattn, attn)
    bwd = []
    for p in range(3):
        hosted = []
        if p == 0:
            hosted = [to_chips("w_ff1", 2), to_chips("w_out")]
        if p == 1:
            hosted = [to_chips("w_proj_a"), to_chips("w_proj_b")]
        res, st = _attn_bwd(qkv, dattn, lse, delta, bias, p, stages=hosted)
        settle(hosted, st)
        bwd.append(res)
    g_rel_bias = _rel_bias_grad([b[3] for b in bwd], buckets)
    dproj = _assemble_dproj([b[i] for i in range(3) for b in bwd], du, dvb, dga, dgb)

    g_w_spatial = dws_t.reshape(N_GROUPS, N_SUB, g8, N_SUB, g8).transpose(0, 2, 1, 4, 3)
    g_b_spatial = dbs_t[:, :N_GROUPS].reshape(N_SUB, g8, N_GROUPS).transpose(2, 1, 0)
    part = _pack(dict(loss=loss_part, rel_bias=g_rel_bias, ln_v_gain=g_lnv_gain, ln_v_bias=g_lnv_bias, w_spatial=g_w_spatial,
                      b_spatial=g_b_spatial, ln1_gain=g_ln1_gain, ln1_bias=g_ln1_bias, b_ff1=g_b_ff1, b_ff2=g_b_ff2,
                      ln2_gain=g_ln2_gain, ln2_bias=g_ln2_bias))
    small = _small_stage(part)
    small.store = keep(sib, "small")
    grad_for_sibling("w_in", xb, dproj, [small])
    parts = sib["small"]
    half = SEQ // 2

    def add_residual(acc, d_):
        return (acc + ALPHA * d_,)

    hosted = [to_sibling("w_in")]
    (dx0,), st = _matmul("d_x0", dproj, full["w_in"], "nt", [F32], add_residual, [(dpre1, "mn", 0)], tm=1024, tn=1024, tk=3072, m=half,
                         stages=hosted)
    settle(hosted, st)
    grad_own("w_in")
    hosted = [to_chips("w_in", 4)]
    (dx1,), st = _matmul("d_x1", dproj, full["w_in"], "nt", [F32], add_residual, [(dpre1, "mn", 0)], tm=1024, tn=1024, tk=3072, m_off=half, m=half,
                         stages=hosted)
    settle(hosted, st)
    grad_x = _from_perm(jnp.concatenate([dx0, dx1], axis=0))[None]

    out_g, out_d, out_m, out_v = {}, {}, {}, {}
    for n, units in (("w_ff2", 4), ("w_ff1", 4), ("w_out", 2), ("w_proj_a", 1), ("w_proj_b", 1), ("w_in", 0)):
        hosted = [to_chips("w_in", units)] if units else []
        (g, d, nm, nv), st = _adam_shard(f"adam_{n}", chips[n], weights[n][0], mom1[n][0], mom2[n][0], stages=hosted)
        settle(hosted, st)
        out_g[n], out_d[n], out_m[n], out_v[n] = g[None], d[None], nm[None], nv[None]
    assert all(v == 16 for v in reduced.values()), reduced

    zero = jnp.zeros((1,), F32)
    sg, sd, sm, sv = (_unpack(b) for b in _adam_small(
        parts, _pack({**weights, "loss": zero}), _pack({**mom1, "loss": zero}), _pack({**mom2, "loss": zero})))
    for n in WEIGHT_ORDER:
        if n not in KINDS:
            shape = weights[n].shape
            out_g[n], out_d[n], out_m[n], out_v[n] = (t[n].reshape(shape) for t in (sg, sd, sm, sv))
    loss = sg["loss"].reshape(())
    return (loss, grad_x, *[out_g[n] for n in WEIGHT_ORDER], *[out_d[n] for n in WEIGHT_ORDER],
            *[out_m[n] for n in WEIGHT_ORDER], *[out_v[n] for n in WEIGHT_ORDER])
```

```python
import math

import jax
import jax.numpy as jnp
import numpy as np
from jax import lax
from jax.experimental import pallas as pl
from jax.experimental.pallas import tpu as pltpu

F32 = jnp.float32
BF16 = jnp.bfloat16
I32 = jnp.int32

SEQ = 2048
D_MODEL = 2048
HEAD_DIM = 128
N_HEADS = 8
N_GROUPS = 8
D_FF = 4 * D_MODEL
BLOCK = 128
DILATIONS = (1, 4, 16)
N_BUCKETS = 32
MAX_DISTANCE = 2048
ALPHA = 2.0 ** 0.25
LN_EPS = 1e-5
NEG_INF = -1e30
N_DEV = 8
N_CHIPS = 4
N_SUB = 16
ADAM_LR, ADAM_B1, ADAM_B2, ADAM_EPS, ADAM_WD, ADAM_STEP = 0.001, 0.9, 0.999, 1e-08, 0.01, 10
LANES = 128
SUBLANES = 8
VMEM_LIMIT = 56 * 1024 * 1024
MESH = pl.DeviceIdType.MESH
ANY = pl.BlockSpec(memory_space=pl.ANY)
WEIGHT_ORDER = ("w_in", "rel_bias", "ln_v_gain", "ln_v_bias", "w_spatial", "b_spatial", "w_proj_a", "w_proj_b", "w_out",
                "ln1_gain", "ln1_bias", "w_ff1", "b_ff1", "w_ff2", "b_ff2", "ln2_gain", "ln2_bias")
KINDS = {"w_in": "col", "w_proj_a": "col", "w_proj_b": "col", "w_out": "row", "w_ff1": "col", "w_ff2": "row"}


def _d_a():
    return N_HEADS * HEAD_DIM


def _d_b():
    return N_GROUPS * BLOCK


def _d_in():
    return 3 * _d_a() + 2 * _d_b() + 2 * D_MODEL


def _pick(t, n, *others):
    if n <= t and all(o % n == 0 for o in others):
        return n
    for c in range(min(t, n) // LANES * LANES, 0, -LANES):
        if n % c == 0 and all(o % c == 0 for o in others):
            return c
    raise ValueError((t, n, others))


class _Stage:
    def __init__(self, ins, outs, alias, sems, start, finish):
        self.ins, self.outs, self.alias, self.sems, self.start, self.finish = ins, outs, alias, sems, start, finish


def _call(name, body, grid, in_specs, out_specs, out_shape, operands, scratch=(), sem=None, stages=(), sequential=False, prefetch=None):
    n_in, n_out, n_sc = len(in_specs), len(out_specs), len(scratch)
    st_in = [len(s.ins) for s in stages]
    st_out = [len(s.outs) for s in stages]
    st_sem = [len(s.sems) for s in stages]
    n_pre = 0 if prefetch is None else 1
    aliases, ioff, ooff = {}, n_in + n_pre, n_out
    for s, ni, no in zip(stages, st_in, st_out):
        for i, o in s.alias.items():
            aliases[ioff + i] = ooff + o
        ioff, ooff = ioff + ni, ooff + no

    def split(refs, counts):
        out, at = [], 0
        for c in counts:
            out.append(refs[at:at + c])
            at += c
        return out

    def wrapped(*refs):
        ins, sins, outs, souts, sc, ssems = split(refs[n_pre:], [n_in, sum(st_in), n_out, sum(st_out), n_sc, sum(st_sem)])
        parts = list(zip(stages, split(sins, st_in), split(souts, st_out), split(ssems, st_sem)))
        if sequential:
            for s, a, b, c in parts:
                s.start(a, b, c)
                s.finish(a, b, c)
            return
        if parts:
            first = _all_of([pl.program_id(i) == 0 for i in range(len(grid))])
            last = _all_of([pl.program_id(i) == g - 1 for i, g in enumerate(grid)])

            @pl.when(first)
            def _():
                for s, a, b, c in parts:
                    s.start(a, b, c)

        body(*ins, *outs, *sc)
        if parts:
            @pl.when(last)
            def _():
                for s, a, b, c in parts:
                    s.finish(a, b, c)

    if stages or sem is None:
        sem = ("arbitrary",) * len(grid)
    specs = dict(grid=grid, in_specs=list(in_specs) + [ANY] * sum(st_in), out_specs=list(out_specs) + [ANY] * sum(st_out),
                 scratch_shapes=list(scratch) + [x for s in stages for x in s.sems])
    if prefetch is not None:
        specs = dict(grid_spec=pltpu.PrefetchScalarGridSpec(num_scalar_prefetch=1, **specs))
    res = pl.pallas_call(
        wrapped, name=name, out_shape=list(out_shape) + [o for s in stages for o in s.outs], input_output_aliases=aliases,
        compiler_params=pltpu.CompilerParams(dimension_semantics=sem, vmem_limit_bytes=VMEM_LIMIT), **specs,
    )(*([prefetch] if n_pre else []), *operands, *[a for s in stages for a in s.ins])
    res = list(res)
    return res[:n_out], split(res[n_out:], st_out)


def _all_of(conds):
    out = conds[0]
    for c in conds[1:]:
        out = out & c
    return out


def _coords():
    return lax.axis_index("x"), lax.axis_index("y"), lax.axis_index("c")


def _other_chips(x, y):
    return ((1 - x, y), (x, 1 - y), (1 - x, 1 - y))


def _lin(dev):
    return 4 * dev[0] + 2 * dev[1] + dev[2]


def _piece(total, lo, n, units=16):
    assert total % units == 0
    return lo * (total // units), n * (total // units)


def _remote(src, dst, send, recv, to):
    return pltpu.make_async_remote_copy(src_ref=src, dst_ref=dst, send_sem=send, recv_sem=recv, device_id=to, device_id_type=MESH)


def _placer(kind, n, lo, cnt):
    def place(ref, dev):
        if kind == "col":
            return ref.at[pl.ds(lo, cnt), pl.ds(pl.multiple_of(_lin(dev) * n, LANES), n)]
        return ref.at[pl.ds(pl.multiple_of(_lin(dev) * n + lo, 2 * SUBLANES), cnt), :]
    return place


def _spread_stage(shard, full, kind, piece=(0, 16)):
    n = shard.shape[1] if kind == "col" else shard.shape[0]
    lo, cnt = _piece(shard.shape[0], *piece)
    place = _placer(kind, n, lo, cnt)

    def copies(ins, outs, sems):
        send, recv, local = sems
        x, y, c = _coords()
        me = (x, y, c)
        src = ins[0].at[pl.ds(lo, cnt), :]
        peers = [(x, y, 1 - c), (1 - x, y, c), (x, 1 - y, c)]
        own = pltpu.make_async_copy(src, place(outs[0], me), local)
        out = [_remote(src, place(outs[0], me), send.at[k], recv.at[k], t) for k, t in enumerate(peers)]
        arrive = [_remote(src, place(outs[0], t), send.at[k], recv.at[k], t) for k, t in enumerate(peers)]
        return own, out, arrive

    def start(ins, outs, sems):
        own, out, _ = copies(ins, outs, sems)
        own.start()
        for cp in out:
            cp.start()

    def finish(ins, outs, sems):
        own, out, arrive = copies(ins, outs, sems)
        for cp in arrive:
            cp.wait_recv()
        for cp in out:
            cp.wait_send()
        own.wait()

    return _Stage([shard, full], [jax.ShapeDtypeStruct(full.shape, full.dtype)], {1: 0},
                  [pltpu.SemaphoreType.DMA((3,)), pltpu.SemaphoreType.DMA((3,)), pltpu.SemaphoreType.DMA], start, finish)


def _relay_stage(full, kind, piece=(0, 16)):
    n = (full.shape[1] if kind == "col" else full.shape[0]) // N_DEV
    lo, cnt = _piece(full.shape[0] if kind == "col" else n, *piece)
    half = cnt // 2
    assert half % (2 * SUBLANES) == 0, (cnt, kind)
    top, bottom = _placer(kind, n, lo, half), _placer(kind, n, lo + half, half)

    def copies(ins, outs, sems):
        send, recv = sems
        x, y, c = _coords()
        xn, yn, dg = (1 - x, y, c), (x, 1 - y, c), (1 - x, 1 - y, c)
        out = [_remote(top(outs[0], yn), top(outs[0], yn), send.at[0], recv.at[0], xn),
               _remote(bottom(outs[0], xn), bottom(outs[0], xn), send.at[1], recv.at[1], yn)]
        arrive = [_remote(top(outs[0], dg), top(outs[0], dg), send.at[0], recv.at[0], xn),
                  _remote(bottom(outs[0], dg), bottom(outs[0], dg), send.at[1], recv.at[1], yn)]
        return out, arrive

    def start(ins, outs, sems):
        for cp in copies(ins, outs, sems)[0]:
            cp.start()

    def finish(ins, outs, sems):
        out, arrive = copies(ins, outs, sems)
        for cp in arrive:
            cp.wait_recv()
        for cp in out:
            cp.wait_send()

    return _Stage([full], [jax.ShapeDtypeStruct(full.shape, full.dtype)], {0: 0},
                  [pltpu.SemaphoreType.DMA((2,)), pltpu.SemaphoreType.DMA((2,))], start, finish)


def _forward_stage(full, kind, piece=(0, 16)):
    n = (full.shape[1] if kind == "col" else full.shape[0]) // N_DEV
    lo, cnt = _piece(full.shape[0] if kind == "col" else n, *piece)
    place = _placer(kind, n, lo, cnt)

    def copies(ins, outs, sems):
        send, recv = sems
        x, y, c = _coords()
        chips = _other_chips(x, y)
        out = [_remote(place(outs[0], (*chip, c)), place(outs[0], (*chip, c)), send.at[k], recv.at[k], (x, y, 1 - c)) for k, chip in enumerate(chips)]
        arrive = [_remote(place(outs[0], (*chip, 1 - c)), place(outs[0], (*chip, 1 - c)), send.at[k], recv.at[k], (x, y, 1 - c))
                  for k, chip in enumerate(chips)]
        return out, arrive

    def start(ins, outs, sems):
        for cp in copies(ins, outs, sems)[0]:
            cp.start()

    def finish(ins, outs, sems):
        out, arrive = copies(ins, outs, sems)
        for cp in arrive:
            cp.wait_recv()
        for cp in out:
            cp.wait_send()

    return _Stage([full], [jax.ShapeDtypeStruct(full.shape, full.dtype)], {0: 0},
                  [pltpu.SemaphoreType.DMA((3,)), pltpu.SemaphoreType.DMA((3,))], start, finish)


def _to_sibling_stage(theirs):
    def copies(ins, outs, sems):
        send, recv = sems
        x, y, c = _coords()
        return [_remote(ins[0].at[q], outs[0].at[q], send.at[q], recv.at[q], (x, y, 1 - c)) for q in range(N_CHIPS)]

    def start(ins, outs, sems):
        for cp in copies(ins, outs, sems):
            cp.start()

    def finish(ins, outs, sems):
        for cp in copies(ins, outs, sems):
            cp.wait()

    return _Stage([theirs], [jax.ShapeDtypeStruct(theirs.shape, BF16)], {},
                  [pltpu.SemaphoreType.DMA((N_CHIPS,)), pltpu.SemaphoreType.DMA((N_CHIPS,))], start, finish)


def _to_chips_stage(pair, dst, transit, piece=(0, 16)):
    lo, cnt = _piece(pair.shape[1], *piece)
    half = cnt // 2
    assert half % (2 * SUBLANES) == 0, cnt
    rows, top, bottom = pl.ds(lo, cnt), pl.ds(lo, half), pl.ds(lo + half, half)

    def copies(ins, outs, sems):
        send, recv, local = sems
        x, y, c = _coords()
        mine, qx, qy, qd = 2 * x + y, 2 * (1 - x) + y, 2 * x + 1 - y, 2 * (1 - x) + 1 - y
        xn, yn = (1 - x, y, c), (x, 1 - y, c)
        pair_ref, (dst_ref, transit_ref) = ins[0], outs
        own = pltpu.make_async_copy(pair_ref.at[mine, rows, :], dst_ref.at[mine, rows, :], local)
        out = [_remote(pair_ref.at[qx, rows, :], dst_ref.at[mine, rows, :], send.at[0], recv.at[0], xn),
               _remote(pair_ref.at[qy, rows, :], dst_ref.at[mine, rows, :], send.at[1], recv.at[1], yn),
               _remote(pair_ref.at[qd, top, :], transit_ref.at[0, top, :], send.at[2], recv.at[2], xn),
               _remote(pair_ref.at[qd, bottom, :], transit_ref.at[1, bottom, :], send.at[3], recv.at[3], yn)]
        arrive = [_remote(pair_ref.at[qx, rows, :], dst_ref.at[qx, rows, :], send.at[0], recv.at[0], xn),
                  _remote(pair_ref.at[qy, rows, :], dst_ref.at[qy, rows, :], send.at[1], recv.at[1], yn),
                  _remote(pair_ref.at[qd, top, :], transit_ref.at[0, top, :], send.at[2], recv.at[2], xn),
                  _remote(pair_ref.at[qd, bottom, :], transit_ref.at[1, bottom, :], send.at[3], recv.at[3], yn)]
        return own, out, arrive

    def start(ins, outs, sems):
        own, out, _ = copies(ins, outs, sems)
        own.start()
        for cp in out:
            cp.start()

    def finish(ins, outs, sems):
        own, out, arrive = copies(ins, outs, sems)
        for cp in arrive:
            cp.wait_recv()
        for cp in out:
            cp.wait_send()
        own.wait()

    return _Stage([pair, dst, transit], [jax.ShapeDtypeStruct(dst.shape, dst.dtype), jax.ShapeDtypeStruct(transit.shape, transit.dtype)],
                  {1: 0, 2: 1}, [pltpu.SemaphoreType.DMA((4,)), pltpu.SemaphoreType.DMA((4,)), pltpu.SemaphoreType.DMA], start, finish)


def _to_chips_relay_stage(dst, transit, piece=(0, 16)):
    lo, cnt = _piece(dst.shape[1], *piece)
    half = cnt // 2
    top, bottom = pl.ds(lo, half), pl.ds(lo + half, half)

    def copies(ins, outs, sems):
        send, recv = sems
        x, y, c = _coords()
        qx, qy, qd = 2 * (1 - x) + y, 2 * x + 1 - y, 2 * (1 - x) + 1 - y
        xn, yn = (1 - x, y, c), (x, 1 - y, c)
        transit_ref, dst_ref = ins[1], outs[0]
        out = [_remote(transit_ref.at[0, top, :], dst_ref.at[qx, top, :], send.at[0], recv.at[0], yn),
               _remote(transit_ref.at[1, bottom, :], dst_ref.at[qy, bottom, :], send.at[1], recv.at[1], xn)]
        arrive = [_remote(transit_ref.at[0, top, :], dst_ref.at[qd, top, :], send.at[0], recv.at[0], yn),
                  _remote(transit_ref.at[1, bottom, :], dst_ref.at[qd, bottom, :], send.at[1], recv.at[1], xn)]
        return out, arrive

    def start(ins, outs, sems):
        for cp in copies(ins, outs, sems)[0]:
            cp.start()

    def finish(ins, outs, sems):
        out, arrive = copies(ins, outs, sems)
        for cp in arrive:
            cp.wait_recv()
        for cp in out:
            cp.wait_send()

    return _Stage([dst, transit], [jax.ShapeDtypeStruct(dst.shape, dst.dtype)], {0: 0},
                  [pltpu.SemaphoreType.DMA((2,)), pltpu.SemaphoreType.DMA((2,))], start, finish)


def _fuse(parts, ins, outs, alias):
    parts = [p for p in parts if p is not None]
    sems = [x for st, _, _ in parts for x in st.sems]

    def run(which):
        def go(i, o, s):
            refs, at = list(i) + list(o), 0
            for st, pi, po in parts:
                getattr(st, which)([refs[k] for k in pi], [refs[k] for k in po], s[at:at + len(st.sems)])
                at += len(st.sems)
        return go

    return _Stage(ins, [jax.ShapeDtypeStruct(o.shape, o.dtype) for o in outs], alias, sems, run("start"), run("finish"))


def _gather_stage(shard, full, kind, new=None, relay=None, forward=None):
    return _fuse([(_spread_stage(shard, full, kind, new), [0, 1], [2]) if new else None,
                  (_relay_stage(full, kind, relay), [1], [2]) if relay else None,
                  (_forward_stage(full, kind, forward), [1], [2]) if forward else None], [shard, full], [full], {1: 0})


def _reduce_stage(pair, dst, transit, new=None, relay=None):
    return _fuse([(_to_chips_stage(pair, dst, transit, new), [0, 1, 2], [3, 4]) if new else None,
                  (_to_chips_relay_stage(dst, transit, relay), [3, 4], [3]) if relay else None], [pair, dst, transit], [dst, transit],
                 {1: 0, 2: 1})


def _small_stage(part):
    def copies(ins, outs, sems):
        send, recv, local = sems
        x, y, c = _coords()
        me = (x, y, c)
        own = pltpu.make_async_copy(ins[0], outs[0].at[_lin(me)], local)
        peers = [(1 - x if k & 4 else x, 1 - y if k & 2 else y, 1 - c if k & 1 else c) for k in range(1, N_DEV)]
        out = [_remote(ins[0], outs[0].at[_lin(me)], send.at[k], recv.at[k], t) for k, t in enumerate(peers)]
        arrive = [_remote(ins[0], outs[0].at[_lin(t)], send.at[k], recv.at[k], t) for k, t in enumerate(peers)]
        return own, out, arrive

    def start(ins, outs, sems):
        own, out, _ = copies(ins, outs, sems)
        own.start()
        for cp in out:
            cp.start()

    def finish(ins, outs, sems):
        own, out, arrive = copies(ins, outs, sems)
        for cp in arrive:
            cp.wait_recv()
        for cp in out:
            cp.wait_send()
        own.wait()

    return _Stage([part], [jax.ShapeDtypeStruct((N_DEV, *part.shape), F32)], {},
                  [pltpu.SemaphoreType.DMA((N_DEV - 1,)), pltpu.SemaphoreType.DMA((N_DEV - 1,)), pltpu.SemaphoreType.DMA], start, finish)


def _comm_only(name, stages):
    return _call(name, lambda: None, (1,), [], [], [], [], stages=stages, sequential=True)[1]


_GELU_C = math.sqrt(2.0 / math.pi)


def _gelu(x):
    return 0.5 * x * (1.0 + jnp.tanh(_GELU_C * (x + 0.044715 * x * x * x)))


def _gelu_grad(x):
    t = jnp.tanh(_GELU_C * (x + 0.044715 * x * x * x))
    return 0.5 * (1.0 + t) + 0.5 * x * (1.0 - t * t) * (_GELU_C * (1.0 + 3.0 * 0.044715 * x * x))


def _sigmoid(x):
    return 1.0 / (1.0 + jnp.exp(-x))


def _dot(a, b, mode):
    dims = {"nn": (((1,), (0,)), ((), ())), "nt": (((1,), (1,)), ((), ())), "tn": (((0,), (0,)), ((), ()))}[mode]
    return lax.dot_general(a.astype(BF16), b.astype(BF16), dims, preferred_element_type=F32)


def _matmul(name, a, b, mode, outs, epi=None, extras=(), colsums=(), tm=2048, tn=512, tk=2048, b_off=0, n=None, m_off=0, m=None, stages=()):
    if mode == "tn":
        kk, mfull = a.shape
    else:
        mfull, kk = a.shape
    m = mfull if m is None else m
    n = (b.shape[0] if mode == "nt" else b.shape[1]) if n is None else n
    tm, tk = _pick(tm, m, m_off), _pick(tk, kk)
    tn = _pick(tn, n, b_off, *[off for _, _, off in extras])
    boff, moff = b_off // tn, m_off // tm
    nm, nn_, nk = m // tm, n // tn, kk // tk
    col_major = bool(colsums)
    grid = (nn_, nm, nk) if col_major else (nm, nn_, nk)

    def imap(f):
        if col_major:
            return lambda g0, g1, k: f(g1, g0, k)
        return f

    a_spec = (pl.BlockSpec((tk, tm), imap(lambda i, j, k: (k, i + moff))) if mode == "tn"
              else pl.BlockSpec((tm, tk), imap(lambda i, j, k: (i + moff, k))))
    b_spec = (pl.BlockSpec((tn, tk), imap(lambda i, j, k: (j + boff, k))) if mode == "nt"
              else pl.BlockSpec((tk, tn), imap(lambda i, j, k: (k, j + boff))))
    in_specs, operands = [a_spec, b_spec], [a, b]
    for arr, kind, off in extras:
        o = off // tn
        if kind == "mn":
            in_specs.append(pl.BlockSpec((tm, tn), imap(lambda i, j, k, o=o: (i + moff, j + o))))
        else:
            in_specs.append(pl.BlockSpec((1, tn), imap(lambda i, j, k, o=o: (0, j + o))))
        operands.append(arr)
    out_shape = [jax.ShapeDtypeStruct((m, n), dt) for dt in outs] + [jax.ShapeDtypeStruct((1, n), F32) for _ in colsums]
    out_specs = ([pl.BlockSpec((tm, tn), imap(lambda i, j, k: (i, j))) for _ in outs]
                 + [pl.BlockSpec((1, tn), imap(lambda i, j, k: (0, j))) for _ in colsums])
    n_ex, n_out, n_cs = len(extras), len(outs), len(colsums)

    def body(*refs):
        a_ref, b_ref = refs[:2]
        ex_refs = refs[2:2 + n_ex]
        out_refs = refs[2 + n_ex:2 + n_ex + n_out]
        cs_refs = refs[2 + n_ex + n_out:2 + n_ex + n_out + n_cs]
        part = _dot(a_ref[...], b_ref[...], mode)

        def finish(acc):
            res = epi(acc, *[r[...] for r in ex_refs]) if epi is not None else (acc,)
            for r, v in zip(out_refs, res[:n_out]):
                r[...] = v.astype(r.dtype)
            if n_cs:
                @pl.when(pl.program_id(1) == 0)
                def _():
                    for r in cs_refs:
                        r[...] = jnp.zeros_like(r)

                for r, idx in zip(cs_refs, colsums):
                    r[...] += jnp.sum(res[idx], axis=0, keepdims=True)

        if nk == 1:
            finish(part)
        else:
            acc_ref = refs[-1]
            k = pl.program_id(2)

            @pl.when(k == 0)
            def _():
                acc_ref[...] = part

            @pl.when(k > 0)
            def _():
                acc_ref[...] += part

            @pl.when(k == nk - 1)
            def _():
                finish(acc_ref[...])

    sem = ("arbitrary", "arbitrary", "arbitrary") if col_major else ("parallel", "parallel", "arbitrary")
    return _call(name, body, grid, in_specs, out_specs, out_shape, operands,
                 scratch=[pltpu.VMEM((tm, tn), F32)] if nk > 1 else [], sem=sem, stages=stages)


def _row_spec(tr, c):
    return pl.BlockSpec((tr, c), lambda i: (i, 0))


def _fix_spec(shape):
    return pl.BlockSpec(shape, lambda *_: tuple(0 for _ in shape))


def _cast_bf16(name, x, tr=512):
    r, c = x.shape
    tr = _pick(tr, r)

    def body(x_ref, o_ref):
        o_ref[...] = x_ref[...].astype(BF16)

    return _call(name, body, (r // tr,), [_row_spec(tr, c)], [_row_spec(tr, c)], [jax.ShapeDtypeStruct((r, c), BF16)], [x],
                 sem=("parallel",))[0][0]


def _layer_norm_stats(x):
    mean = jnp.mean(x, axis=-1, keepdims=True)
    xc = x - mean
    var = jnp.mean(xc * xc, axis=-1, keepdims=True)
    rstd = lax.rsqrt(var + LN_EPS)
    return xc * rstd, rstd


def _layer_norm_bwd(dxhat, xhat, rstd):
    m1 = jnp.mean(dxhat, axis=-1, keepdims=True)
    m2 = jnp.mean(dxhat * xhat, axis=-1, keepdims=True)
    return rstd * (dxhat - m1 - xhat * m2)


def _ln1_fwd(pre1, g1, b1, tr=256, stages=()):
    s, d = pre1.shape
    tr = _pick(tr, s)

    def body(p_ref, g_ref, b_ref, xh_ref, rs_ref, h_ref):
        xhat, rstd = _layer_norm_stats(p_ref[...])
        xh_ref[...] = xhat
        rs_ref[...] = rstd
        h_ref[...] = (xhat * g_ref[...] + b_ref[...]).astype(BF16)

    return _call("ln1_fwd", body, (s // tr,), [_row_spec(tr, d), _fix_spec((1, d)), _fix_spec((1, d))],
                 [_row_spec(tr, d), _row_spec(tr, 1), _row_spec(tr, d)],
                 [jax.ShapeDtypeStruct((s, d), F32), jax.ShapeDtypeStruct((s, 1), F32), jax.ShapeDtypeStruct((s, d), BF16)],
                 [pre1, g1, b1], sem=("parallel",), stages=stages)


def _ln2_loss_bwd(ff, xhat1, g1, b1, g2, b2, target, tr=256):
    s, d = ff.shape
    tr = _pick(tr, s)

    def body(ff_ref, xh1_ref, g1_ref, b1_ref, g2_ref, b2_ref, t_ref, dp_ref, dpb_ref, dg_ref, db_ref, dbf_ref, loss_ref):
        @pl.when(pl.program_id(0) == 0)
        def _():
            dg_ref[...] = jnp.zeros_like(dg_ref)
            db_ref[...] = jnp.zeros_like(db_ref)
            dbf_ref[...] = jnp.zeros_like(dbf_ref)
            loss_ref[...] = jnp.zeros_like(loss_ref)

        h1 = xh1_ref[...] * g1_ref[...] + b1_ref[...]
        xhat, rstd = _layer_norm_stats(ALPHA * h1 + ff_ref[...])
        err = xhat * g2_ref[...] + b2_ref[...] - t_ref[...]
        row = jnp.mean(err * err, axis=-1, keepdims=True)
        loss_ref[...] += 0.5 * jnp.sum(row, axis=0, keepdims=True)
        dy = err / d
        dg_ref[...] += jnp.sum(dy * xhat, axis=0, keepdims=True)
        db_ref[...] += jnp.sum(dy, axis=0, keepdims=True)
        dpre = _layer_norm_bwd(dy * g2_ref[...], xhat, rstd)
        dbf_ref[...] += jnp.sum(dpre, axis=0, keepdims=True)
        dp_ref[...] = dpre
        dpb_ref[...] = dpre.astype(BF16)

    vec = _fix_spec((1, d))
    return _call("ln2_loss_bwd", body, (s // tr,), [_row_spec(tr, d), _row_spec(tr, d), vec, vec, vec, vec, _row_spec(tr, d)],
                 [_row_spec(tr, d), _row_spec(tr, d), vec, vec, vec, _fix_spec((1, 1))],
                 [jax.ShapeDtypeStruct((s, d), F32), jax.ShapeDtypeStruct((s, d), BF16)]
                 + [jax.ShapeDtypeStruct((1, d), F32)] * 3 + [jax.ShapeDtypeStruct((1, 1), F32)],
                 [ff, xhat1, g1, b1, g2, b2, target])[0]


def _ln1_bwd(dh1, xhat1, rstd1, g1, tr=256, stages=()):
    s, d = dh1.shape
    tr = _pick(tr, s)

    def body(dh_ref, xh_ref, rs_ref, g_ref, dp_ref, dpb_ref, dg_ref, db_ref):
        @pl.when(pl.program_id(0) == 0)
        def _():
            dg_ref[...] = jnp.zeros_like(dg_ref)
            db_ref[...] = jnp.zeros_like(db_ref)

        dh, xhat = dh_ref[...], xh_ref[...]
        dg_ref[...] += jnp.sum(dh * xhat, axis=0, keepdims=True)
        db_ref[...] += jnp.sum(dh, axis=0, keepdims=True)
        dpre = _layer_norm_bwd(dh * g_ref[...], xhat, rs_ref[...])
        dp_ref[...] = dpre
        dpb_ref[...] = dpre.astype(BF16)

    vec = _fix_spec((1, d))
    return _call("ln1_bwd", body, (s // tr,), [_row_spec(tr, d), _row_spec(tr, d), _row_spec(tr, 1), vec],
                 [_row_spec(tr, d), _row_spec(tr, d), vec, vec],
                 [jax.ShapeDtypeStruct((s, d), F32), jax.ShapeDtypeStruct((s, d), BF16)] + [jax.ShapeDtypeStruct((1, d), F32)] * 2,
                 [dh1, xhat1, rstd1, g1], stages=stages)


def _to_perm(x):
    return x.reshape(SEQ // N_SUB, N_SUB, -1).transpose(1, 0, 2).reshape(SEQ, -1)


def _from_perm(x):
    return x.reshape(N_SUB, SEQ // N_SUB, -1).transpose(1, 0, 2).reshape(SEQ, -1)


def _local_index(p):
    rho = np.arange(BLOCK)
    if p == 0:
        return 16 * (rho % 8) + rho // 8
    if p == 1:
        return 4 * (rho % 32) + rho // 32
    return rho


def _tile_view(x, p):
    c = x.shape[1]
    if p == 1:
        return x.reshape(4, 4, BLOCK, c)
    return x.reshape(N_SUB, BLOCK, c)


def _view_shape(c, p):
    return (4, 4, BLOCK, c) if p == 1 else (N_SUB, BLOCK, c)


def _tile_spec(p, width, col, shift=0):
    nblk = SEQ // DILATIONS[p] // BLOCK

    def blk(n):
        return jnp.clip(n + shift, 0, nblk - 1)

    if p == 0:
        return pl.BlockSpec((N_SUB, SUBLANES, width), lambda s, n: (0, blk(n), col))
    if p == 1:
        return pl.BlockSpec((4, None, 32, width), lambda s, n: (0, s, blk(n), col))
    return pl.BlockSpec((None, BLOCK, width), lambda s, n: (s, 0, col))


def _tile_grid(p):
    return ((1, 16), (4, 4), (16, 1))[p]


def _t5_bucket(n):
    max_exact = N_BUCKETS // 2
    nf = np.maximum(n, 1).astype(np.float32)
    large = max_exact + (np.log(nf / np.float32(max_exact)) / np.float32(math.log(MAX_DISTANCE / max_exact))
                         * np.float32(N_BUCKETS - max_exact)).astype(np.int32)
    large = np.minimum(large, N_BUCKETS - 1)
    return np.where(n < max_exact, n, large).astype(np.int32)


def _bucket_tables():
    tabs = np.zeros((3, 2, BLOCK, BLOCK), np.int32)
    for p, d in enumerate(DILATIONS):
        i = _local_index(p)
        diff = i[:, None] - i[None, :]
        tabs[p, 0] = np.where(diff <= 0, _t5_bucket((BLOCK + diff) * d), -1)
        tabs[p, 1] = np.where(diff >= 0, _t5_bucket(np.maximum(diff, 0) * d), -1)
    return tabs


def _bias_expand(rel_bias, buckets):
    nh = N_HEADS

    def body(rb_ref, bk_ref, o_ref):
        for w in range(2):
            bk = bk_ref[0, w]
            for h in range(nh):
                val = jnp.zeros((BLOCK, BLOCK), F32)
                for b in range(N_BUCKETS):
                    val = jnp.where(bk == b, rb_ref[b, h], val)
                o_ref[0, h, w] = jnp.where(bk < 0, NEG_INF, val)

    return _call("bias_expand", body, (3,),
                 [pl.BlockSpec(memory_space=pltpu.SMEM), pl.BlockSpec((1, 2, BLOCK, BLOCK), lambda p: (p, 0, 0, 0))],
                 [pl.BlockSpec((1, nh, 2, BLOCK, BLOCK), lambda p: (p, 0, 0, 0, 0))],
                 [jax.ShapeDtypeStruct((3, nh, 2, BLOCK, BLOCK), F32)], [rel_bias, buckets], sem=("parallel",))[0][0]


def _heads_to_lanes(cols):
    lane = lax.broadcasted_iota(I32, (BLOCK, LANES), 1)
    out = jnp.zeros((BLOCK, LANES), F32)
    for h, c in enumerate(cols):
        out = jnp.where(lane == h, c, out)
    return out


def _attn_fwd(qkv, bias, p, stages=()):
    d_a = _d_a()
    has_prev = SEQ // DILATIONS[p] // BLOCK > 1
    scale = HEAD_DIM ** -0.5
    view = _tile_view(qkv, p)

    def body(q_ref, kc_ref, kp_ref, vc_ref, vp_ref, b_ref, o_ref, l_ref):
        n = pl.program_id(1)
        q_all = q_ref[...].reshape(BLOCK, d_a).astype(BF16)
        kc_all = kc_ref[...].reshape(BLOCK, d_a).astype(BF16)
        vc_all = vc_ref[...].reshape(BLOCK, d_a).astype(BF16)
        if has_prev:
            kp_all = kp_ref[...].reshape(BLOCK, d_a).astype(BF16)
            vp_all = vp_ref[...].reshape(BLOCK, d_a).astype(BF16)
        outs, lses = [], []
        for h in range(N_HEADS):
            sl = slice(h * HEAD_DIM, (h + 1) * HEAD_DIM)
            q = q_all[:, sl]
            sc = _dot(q, kc_all[:, sl], "nt") * scale + b_ref[0, h, 1]
            m = jnp.max(sc, axis=-1, keepdims=True)
            if has_prev:
                sp = _dot(q, kp_all[:, sl], "nt") * scale + b_ref[0, h, 0]
                sp = jnp.where(n > 0, sp, NEG_INF)
                m = jnp.maximum(m, jnp.max(sp, axis=-1, keepdims=True))
                pp = jnp.exp(sp - m)
            pc = jnp.exp(sc - m)
            den = jnp.sum(pc, axis=-1, keepdims=True)
            o = _dot(pc, vc_all[:, sl], "nn")
            if has_prev:
                den = den + jnp.sum(pp, axis=-1, keepdims=True)
                o = o + _dot(pp, vp_all[:, sl], "nn")
            outs.append(o / den)
            lses.append(m + jnp.log(den))
        o_ref[...] = jnp.concatenate(outs, axis=-1).reshape(o_ref.shape)
        l_ref[...] = _heads_to_lanes(lses).reshape(l_ref.shape)

    (o, l), st = _call(
        f"attn_fwd{p}", body, _tile_grid(p),
        [_tile_spec(p, d_a, 0), _tile_spec(p, d_a, 1), _tile_spec(p, d_a, 1, -1), _tile_spec(p, d_a, 2), _tile_spec(p, d_a, 2, -1),
         pl.BlockSpec((1, N_HEADS, 2, BLOCK, BLOCK), lambda s, n: (p, 0, 0, 0, 0))],
        [_tile_spec(p, d_a, 0), _tile_spec(p, LANES, 0)],
        [jax.ShapeDtypeStruct(_view_shape(d_a, p), F32), jax.ShapeDtypeStruct(_view_shape(LANES, p), F32)],
        [view, view, view, view, view, bias], sem=("parallel", "parallel"), stages=stages)
    return (o.reshape(SEQ, d_a), l.reshape(SEQ, LANES)), st


def _attn_combine(os_, ls_, tr=256, stages=()):
    d_a = _d_a()
    tr = _pick(tr, SEQ)

    def body(o0, o1, o2, l0, l1, l2, a_ref, ab_ref, lt_ref):
        l = [l0[...], l1[...], l2[...]]
        m = jnp.maximum(jnp.maximum(l[0], l[1]), l[2])
        w = [jnp.exp(x - m) for x in l]
        tot = w[0] + w[1] + w[2]
        lt_ref[...] = m + jnp.log(tot)
        w = [x / tot for x in w]
        for h in range(N_HEADS):
            sl = slice(h * HEAD_DIM, (h + 1) * HEAD_DIM)
            acc = w[0][:, h:h + 1] * o0[:, sl] + w[1][:, h:h + 1] * o1[:, sl] + w[2][:, h:h + 1] * o2[:, sl]
            a_ref[:, sl] = acc
            ab_ref[:, sl] = acc.astype(BF16)

    return _call("attn_combine", body, (SEQ // tr,), [_row_spec(tr, d_a)] * 3 + [_row_spec(tr, LANES)] * 3,
                 [_row_spec(tr, d_a), _row_spec(tr, d_a), _row_spec(tr, LANES)],
                 [jax.ShapeDtypeStruct((SEQ, d_a), F32), jax.ShapeDtypeStruct((SEQ, d_a), BF16), jax.ShapeDtypeStruct((SEQ, LANES), F32)],
                 [*os_, *ls_], sem=("parallel",), stages=stages)


def _attn_delta(dattn, attn, tr=256):
    d_a = _d_a()
    tr = _pick(tr, SEQ)

    def body(d_ref, a_ref, o_ref):
        prod = d_ref[...] * a_ref[...]
        lane = lax.broadcasted_iota(I32, (tr, LANES), 1)
        out = jnp.zeros((tr, LANES), F32)
        for h in range(N_HEADS):
            out = jnp.where(lane == h, jnp.sum(prod[:, h * HEAD_DIM:(h + 1) * HEAD_DIM], axis=-1, keepdims=True), out)
        o_ref[...] = out

    return _call("attn_delta", body, (SEQ // tr,), [_row_spec(tr, d_a)] * 2, [_row_spec(tr, LANES)],
                 [jax.ShapeDtypeStruct((SEQ, LANES), F32)], [dattn, attn], sem=("parallel",))[0][0]


def _attn_bwd(qkv, dattn, lse, delta, bias, p, stages=()):
    d_a = _d_a()
    nblk = SEQ // DILATIONS[p] // BLOCK
    has_next = nblk > 1
    scale = HEAD_DIM ** -0.5
    qv, dov, lv, tv = (_tile_view(x, p) for x in (qkv, dattn, lse, delta))

    def body(q_ref, qn_ref, k_ref, v_ref, do_ref, don_ref, l_ref, ln_ref, t_ref, tn_ref, b_ref, dq_ref, dk_ref, dv_ref, db_ref, carry_ref):
        j = pl.program_id(1)

        @pl.when((pl.program_id(0) == 0) & (j == 0))
        def _():
            db_ref[...] = jnp.zeros_like(db_ref)

        k_all = k_ref[...].reshape(BLOCK, d_a).astype(BF16)
        v_all = v_ref[...].reshape(BLOCK, d_a).astype(BF16)

        def side(qr, dor, lr, tr_, w):
            q_all = qr[...].reshape(BLOCK, d_a).astype(BF16)
            do_all = dor[...].reshape(BLOCK, d_a).astype(BF16)
            l_all = lr[...].reshape(BLOCK, LANES)
            t_all = tr_[...].reshape(BLOCK, LANES)
            dqs, dks, dvs = [], [], []
            for h in range(N_HEADS):
                sl = slice(h * HEAD_DIM, (h + 1) * HEAD_DIM)
                s = _dot(q_all[:, sl], k_all[:, sl], "nt") * scale + b_ref[0, h, w]
                pr = jnp.exp(s - l_all[:, h:h + 1])
                dp = _dot(do_all[:, sl], v_all[:, sl], "nt")
                ds = pr * (dp - t_all[:, h:h + 1])
                db_ref[h, w] += ds
                dqs.append(_dot(ds, k_all[:, sl], "nn") * scale)
                dks.append(_dot(ds, q_all[:, sl], "tn") * scale)
                dvs.append(_dot(pr, do_all[:, sl], "tn"))
            return [jnp.concatenate(x, axis=-1) for x in (dqs, dks, dvs)]

        dq_c, dk_c, dv_c = side(q_ref, do_ref, l_ref, t_ref, 1)
        if has_next:
            dq_ref[...] = (jnp.where(j > 0, carry_ref[...], 0.0) + dq_c).reshape(dq_ref.shape)
            not_last = j < nblk - 1

            @pl.when(not_last)
            def _():
                dq_n, dk_n, dv_n = side(qn_ref, don_ref, ln_ref, tn_ref, 0)
                carry_ref[...] = dq_n
                dk_ref[...] = (dk_c + dk_n).reshape(dk_ref.shape)
                dv_ref[...] = (dv_c + dv_n).reshape(dv_ref.shape)

            @pl.when(jnp.logical_not(not_last))
            def _():
                dk_ref[...] = dk_c.reshape(dk_ref.shape)
                dv_ref[...] = dv_c.reshape(dv_ref.shape)
        else:
            dq_ref[...] = dq_c.reshape(dq_ref.shape)
            dk_ref[...] = dk_c.reshape(dk_ref.shape)
            dv_ref[...] = dv_c.reshape(dv_ref.shape)

    def big(col, shift=0):
        return _tile_spec(p, d_a, col, shift)

    def small(shift=0):
        return _tile_spec(p, LANES, 0, shift)

    (dq, dk, dv, dbias), st = _call(
        f"attn_bwd{p}", body, _tile_grid(p),
        [big(0), big(0, 1), big(1), big(2), big(0), big(0, 1), small(), small(1), small(), small(1),
         pl.BlockSpec((1, N_HEADS, 2, BLOCK, BLOCK), lambda s, n: (p, 0, 0, 0, 0))],
        [big(0), big(0), big(0), pl.BlockSpec((N_HEADS, 2, BLOCK, BLOCK), lambda s, n: (0, 0, 0, 0))],
        [jax.ShapeDtypeStruct(_view_shape(d_a, p), F32)] * 3 + [jax.ShapeDtypeStruct((N_HEADS, 2, BLOCK, BLOCK), F32)],
        [qv, qv, qv, qv, dov, dov, lv, lv, tv, tv, bias], scratch=[pltpu.VMEM((BLOCK, d_a), F32)], stages=stages)
    return (dq.reshape(SEQ, d_a), dk.reshape(SEQ, d_a), dv.reshape(SEQ, d_a), dbias), st


def _rel_bias_grad(dbias, buckets):
    nh = N_HEADS

    def body(d0, d1, d2, bk_ref, o_ref, t_ref):
        ds = (d0, d1, d2)

        def per_bucket(b, carry):
            for h in range(nh):
                acc = jnp.zeros((BLOCK, BLOCK), F32)
                for p in range(3):
                    for w in range(2):
                        acc = acc + jnp.where(bk_ref[p, w] == b, ds[p][h, w], 0.0)
                t_ref[pl.ds(b * nh + h, 1), :] = jnp.sum(acc, axis=0, keepdims=True)
            return carry

        lax.fori_loop(0, N_BUCKETS, per_bucket, 0)
        o_ref[...] = jnp.sum(t_ref[...], axis=-1, keepdims=True)

    return _call("rel_bias_grad", body, (1,), [_fix_spec((nh, 2, BLOCK, BLOCK))] * 3 + [_fix_spec((3, 2, BLOCK, BLOCK))],
                 [_fix_spec((N_BUCKETS * nh, 1))], [jax.ShapeDtypeStruct((N_BUCKETS * nh, 1), F32)], [*dbias, buckets],
                 scratch=[pltpu.VMEM((N_BUCKETS * nh, LANES), F32)])[0][0]


def _gmlp_fwd(rest, gain, bias, ws, bs, causal, stages=()):
    d_b = _d_b()

    def body(u_ref, v_ref, g_ref, b_ref, ws_ref, bs_ref, c_ref, o_ref):
        u = u_ref[...].reshape(BLOCK, d_b)
        xhat, _ = _layer_norm_stats(_gelu(v_ref[...].reshape(BLOCK, d_b)))
        vn = (xhat * g_ref[...] + b_ref[...]).astype(BF16)
        outs = []
        for g in range(N_GROUPS):
            sl = slice(g * BLOCK, (g + 1) * BLOCK)
            w = jnp.where(c_ref[...] > 0, ws_ref[g], 0.0)
            z = _dot(w, vn[:, sl], "nn") + bs_ref[:, g:g + 1]
            outs.append(_gelu(u[:, sl]) * z)
        o_ref[...] = jnp.concatenate(outs, axis=-1).reshape(o_ref.shape)

    (out,), st = _call(
        "gmlp_fwd", body, (1, SEQ // BLOCK),
        [_tile_spec(0, d_b, 0), _tile_spec(0, d_b, 1), _fix_spec((1, d_b)), _fix_spec((1, d_b)),
         _fix_spec((N_GROUPS, BLOCK, BLOCK)), _fix_spec((BLOCK, N_GROUPS)), _fix_spec((BLOCK, BLOCK))],
        [_tile_spec(0, d_b, 0)], [jax.ShapeDtypeStruct(_view_shape(d_b, 0), F32)],
        [_tile_view(rest, 0), _tile_view(rest, 0), gain, bias, ws, bs, causal], sem=("parallel", "parallel"), stages=stages)
    return out.reshape(SEQ, d_b), st


def _gmlp_bwd(rest, dgmlp, gain, bias, ws, bs, causal, stages=()):
    d_b = _d_b()
    nchunk = SEQ // BLOCK

    def body(u_ref, v_ref, dg_ref, g_ref, b_ref, ws_ref, bs_ref, c_ref, du_ref, dv_ref, dws_ref, dbs_ref, dgain_ref, dbias_ref):
        c = pl.program_id(1)

        @pl.when(c == 0)
        def _():
            dws_ref[...] = jnp.zeros_like(dws_ref)
            dbs_ref[...] = jnp.zeros_like(dbs_ref)
            dgain_ref[...] = jnp.zeros_like(dgain_ref)
            dbias_ref[...] = jnp.zeros_like(dbias_ref)

        u = u_ref[...].reshape(BLOCK, d_b)
        v = v_ref[...].reshape(BLOCK, d_b)
        dgm = dg_ref[...].reshape(BLOCK, d_b)
        xhat, rstd = _layer_norm_stats(_gelu(v))
        vn = (xhat * g_ref[...] + b_ref[...]).astype(BF16)
        lane = lax.broadcasted_iota(I32, (BLOCK, LANES), 1)
        dus, dvns = [], []
        dbs = dbs_ref[...]
        for g in range(N_GROUPS):
            sl = slice(g * BLOCK, (g + 1) * BLOCK)
            w = jnp.where(c_ref[...] > 0, ws_ref[g], 0.0).astype(BF16)
            z = _dot(w, vn[:, sl], "nn") + bs_ref[:, g:g + 1]
            dz = dgm[:, sl] * _gelu(u[:, sl])
            dus.append(dgm[:, sl] * z * _gelu_grad(u[:, sl]))
            dws_ref[g] += _dot(dz, vn[:, sl], "nt")
            dbs = dbs + jnp.where(lane == g, jnp.sum(dz, axis=-1, keepdims=True), 0.0)
            dvns.append(_dot(w, dz, "tn"))
        dbs_ref[...] = dbs
        dvn = jnp.concatenate(dvns, axis=-1)
        dgain_ref[...] += jnp.sum(dvn * xhat, axis=0, keepdims=True)
        dbias_ref[...] += jnp.sum(dvn, axis=0, keepdims=True)
        dvg = _layer_norm_bwd(dvn * g_ref[...], xhat, rstd)
        du_ref[...] = jnp.concatenate(dus, axis=-1).reshape(du_ref.shape)
        dv_ref[...] = (dvg * _gelu_grad(v)).reshape(dv_ref.shape)

        @pl.when(c == nchunk - 1)
        def _():
            for g in range(N_GROUPS):
                dws_ref[g] = jnp.where(c_ref[...] > 0, dws_ref[g], 0.0)

    (du, dv, dws, dbs, dgain, dbias), st = _call(
        "gmlp_bwd", body, (1, nchunk),
        [_tile_spec(0, d_b, 0), _tile_spec(0, d_b, 1), _tile_spec(0, d_b, 0), _fix_spec((1, d_b)), _fix_spec((1, d_b)),
         _fix_spec((N_GROUPS, BLOCK, BLOCK)), _fix_spec((BLOCK, N_GROUPS)), _fix_spec((BLOCK, BLOCK))],
        [_tile_spec(0, d_b, 0), _tile_spec(0, d_b, 0), _fix_spec((N_GROUPS, BLOCK, BLOCK)), _fix_spec((BLOCK, LANES)),
         _fix_spec((1, d_b)), _fix_spec((1, d_b))],
        [jax.ShapeDtypeStruct(_view_shape(d_b, 0), F32)] * 2
        + [jax.ShapeDtypeStruct((N_GROUPS, BLOCK, BLOCK), F32), jax.ShapeDtypeStruct((BLOCK, LANES), F32)]
        + [jax.ShapeDtypeStruct((1, d_b), F32)] * 2,
        [_tile_view(rest, 0), _tile_view(rest, 0), _tile_view(dgmlp, 0), gain, bias, ws, bs, causal], stages=stages)
    return (du.reshape(SEQ, d_b), dv.reshape(SEQ, d_b), dws, dbs, dgain, dbias), st


def _assemble_dproj(dqkv, du, dv, dga, dgb, tr=128):
    d_a, d_b, d_in = _d_a(), _d_b(), _d_in()
    tr = _pick(tr, SEQ)

    def body(*refs):
        att, (du_ref, dv_ref, dga_ref, dgb_ref, o_ref) = refs[:9], refs[9:]
        for i in range(3):
            o_ref[:, i * d_a:(i + 1) * d_a] = (att[3 * i][...] + att[3 * i + 1][...] + att[3 * i + 2][...]).astype(BF16)
        o_ref[:, 3 * d_a:3 * d_a + d_b] = du_ref[...].astype(BF16)
        o_ref[:, 3 * d_a + d_b:3 * d_a + 2 * d_b] = dv_ref[...].astype(BF16)
        o_ref[:, 3 * d_a + 2 * d_b:3 * d_a + 2 * d_b + D_MODEL] = dga_ref[...]
        o_ref[:, 3 * d_a + 2 * d_b + D_MODEL:] = dgb_ref[...]

    return _call("assemble_dproj", body, (SEQ // tr,), [_row_spec(tr, d_a)] * 9 + [_row_spec(tr, d_b)] * 2 + [_row_spec(tr, D_MODEL)] * 2,
                 [_row_spec(tr, d_in)], [jax.ShapeDtypeStruct((SEQ, d_in), BF16)], [*dqkv, du, dv, dga, dgb], sem=("parallel",))[0][0]


def _dw(name, a, b, kind, core, mine, add=None, tn=1152, stages=()):
    s, m = a.shape
    n = b.shape[1]
    rs, cs = (m, n // N_DEV) if kind == "col" else (m // N_DEV, n)
    tn = _pick(tn if kind == "col" else 512, cs)
    nj = cs // tn

    def shard(q, c_ref):
        return 2 * q + (c_ref[0] if mine else 1 - c_ref[0])

    if kind == "col":
        a_spec = pl.BlockSpec((s, m), lambda q, j, c_ref: (0, 0))
        b_spec = pl.BlockSpec((s, tn), lambda q, j, c_ref: (0, shard(q, c_ref) * nj + j))
    else:
        a_spec = pl.BlockSpec((s, rs), lambda q, j, c_ref: (0, shard(q, c_ref)))
        b_spec = pl.BlockSpec((s, tn), lambda q, j, c_ref: (0, j))
    o_spec = pl.BlockSpec((None, rs, tn), lambda q, j, c_ref: (q, 0, j))

    def body(a_ref, b_ref, *rest):
        acc = _dot(a_ref[...], b_ref[...], "tn")
        if add is not None:
            acc = acc + rest[0][...].astype(F32)
        rest[-1][...] = acc.astype(BF16)

    (out,), st = _call(name, body, (N_CHIPS, nj), [a_spec, b_spec] + ([o_spec] if add is not None else []), [o_spec],
                       [jax.ShapeDtypeStruct((N_CHIPS, rs, cs), BF16)], [a, b] + ([add] if add is not None else []),
                       sem=("parallel", "parallel"), stages=stages, prefetch=core)
    return out, st


def _adamw(w, g, m, v):
    m = ADAM_B1 * m + (1.0 - ADAM_B1) * g
    v = ADAM_B2 * v + (1.0 - ADAM_B2) * (g * g)
    m_hat = m / (1.0 - ADAM_B1 ** ADAM_STEP)
    v_hat = v / (1.0 - ADAM_B2 ** ADAM_STEP)
    delta = -ADAM_LR * (m_hat / (jnp.sqrt(v_hat) + ADAM_EPS) + ADAM_WD * w)
    return delta, m, v


def _adam_shard(name, chip_sums, w, m, v, tr=256, stages=()):
    rs, cs = w.shape
    tr = _pick(tr, rs)

    def body(s_ref, w_ref, m_ref, v_ref, g_ref, d_ref, nm_ref, nv_ref):
        g = s_ref[0].astype(F32)
        for q in range(1, N_CHIPS):
            g = g + s_ref[q].astype(F32)
        d, nm, nv = _adamw(w_ref[...], g, m_ref[...], v_ref[...])
        g_ref[...], d_ref[...], nm_ref[...], nv_ref[...] = g, d, nm, nv

    spec = _row_spec(tr, cs)
    return _call(name, body, (rs // tr,), [pl.BlockSpec((N_CHIPS, tr, cs), lambda i: (0, i, 0)), spec, spec, spec], [spec] * 4,
                 [jax.ShapeDtypeStruct((rs, cs), F32)] * 4, [chip_sums, w, m, v], sem=("parallel",), stages=stages)


def _adam_small(parts, w, m, v):
    rows = w.shape[0]

    def body(p_ref, w_ref, m_ref, v_ref, g_ref, d_ref, nm_ref, nv_ref):
        g = p_ref[0]
        for j in range(1, N_DEV):
            g = g + p_ref[j]
        d, nm, nv = _adamw(w_ref[...], g, m_ref[...], v_ref[...])
        g_ref[...], d_ref[...], nm_ref[...], nv_ref[...] = g, d, nm, nv

    spec = _fix_spec((rows, LANES))
    return _call("adam_small", body, (1,), [_fix_spec((N_DEV, rows, LANES)), spec, spec, spec], [spec] * 4,
                 [jax.ShapeDtypeStruct((rows, LANES), F32)] * 4, [parts, w, m, v])[0]


def _small_sizes():
    d_b = _d_b()
    return (("loss", 1), ("rel_bias", N_BUCKETS * N_HEADS), ("ln_v_gain", d_b), ("ln_v_bias", d_b),
            ("w_spatial", N_GROUPS * BLOCK * BLOCK), ("b_spatial", N_GROUPS * BLOCK), ("ln1_gain", D_MODEL), ("ln1_bias", D_MODEL),
            ("b_ff1", D_FF), ("b_ff2", D_MODEL), ("ln2_gain", D_MODEL), ("ln2_bias", D_MODEL))


def _pack(vals):
    pieces = []
    for name, size in _small_sizes():
        flat = vals[name].reshape(-1).astype(F32)
        padded = -(-size // (SUBLANES * LANES)) * SUBLANES * LANES
        pieces.append(jnp.pad(flat, (0, padded - size)).reshape(-1, LANES))
    return jnp.concatenate(pieces, axis=0)


def _unpack(buf):
    out, row = {}, 0
    for name, size in _small_sizes():
        rows = -(-size // (SUBLANES * LANES)) * SUBLANES
        out[name] = buf[row:row + rows].reshape(-1)[:size]
        row += rows
    return out


def kernel(x, w_in, rel_bias, ln_v_gain, ln_v_bias, w_spatial, b_spatial, w_proj_a, w_proj_b, w_out, ln1_gain, ln1_bias, w_ff1, b_ff1, w_ff2, b_ff2, ln2_gain, ln2_bias, loss_target, m_w_in, m_rel_bias, m_ln_v_gain, m_ln_v_bias, m_w_spatial, m_b_spatial, m_w_proj_a, m_w_proj_b, m_w_out, m_ln1_gain, m_ln1_bias, m_w_ff1, m_b_ff1, m_w_ff2, m_b_ff2, m_ln2_gain, m_ln2_bias, v_w_in, v_rel_bias, v_ln_v_gain, v_ln_v_bias, v_w_spatial, v_b_spatial, v_w_proj_a, v_w_proj_b, v_w_out, v_ln1_gain, v_ln1_bias, v_w_ff1, v_b_ff1, v_w_ff2, v_b_ff2, v_ln2_gain, v_ln2_bias):
    d_a, d_b, d_in = _d_a(), _d_b(), _d_in()
    weights = dict(w_in=w_in, rel_bias=rel_bias, ln_v_gain=ln_v_gain, ln_v_bias=ln_v_bias, w_spatial=w_spatial, b_spatial=b_spatial,
                   w_proj_a=w_proj_a, w_proj_b=w_proj_b, w_out=w_out, ln1_gain=ln1_gain, ln1_bias=ln1_bias, w_ff1=w_ff1, b_ff1=b_ff1,
                   w_ff2=w_ff2, b_ff2=b_ff2, ln2_gain=ln2_gain, ln2_bias=ln2_bias)
    mom1 = dict(w_in=m_w_in, rel_bias=m_rel_bias, ln_v_gain=m_ln_v_gain, ln_v_bias=m_ln_v_bias, w_spatial=m_w_spatial,
                b_spatial=m_b_spatial, w_proj_a=m_w_proj_a, w_proj_b=m_w_proj_b, w_out=m_w_out, ln1_gain=m_ln1_gain,
                ln1_bias=m_ln1_bias, w_ff1=m_w_ff1, b_ff1=m_b_ff1, w_ff2=m_w_ff2, b_ff2=m_b_ff2, ln2_gain=m_ln2_gain, ln2_bias=m_ln2_bias)
    mom2 = dict(w_in=v_w_in, rel_bias=v_rel_bias, ln_v_gain=v_ln_v_gain, ln_v_bias=v_ln_v_bias, w_spatial=v_w_spatial,
                b_spatial=v_b_spatial, w_proj_a=v_w_proj_a, w_proj_b=v_w_proj_b, w_out=v_w_out, ln1_gain=v_ln1_gain,
                ln1_bias=v_ln1_bias, w_ff1=v_w_ff1, b_ff1=v_b_ff1, w_ff2=v_w_ff2, b_ff2=v_b_ff2, ln2_gain=v_ln2_gain, ln2_bias=v_ln2_bias)

    shard = {n: _cast_bf16(f"cast_{n}", weights[n][0]) for n in KINDS}
    full, sent = {}, {n: (0, 0, 0) for n in KINDS}
    for n, kind in KINDS.items():
        r, c = shard[n].shape
        full[n] = lax.empty((r, c * N_DEV) if kind == "col" else (r * N_DEV, c), BF16)

    def keep(table, n):
        def store(outs):
            table[n] = outs[0]
        return store

    def gathering(**new):
        stages = []
        for n in KINDS:
            out, relayed, passed = sent[n]
            units = new.get(n, 0)
            if units or relayed < out or passed < relayed:
                st = _gather_stage(shard[n], full[n], KINDS[n], (out, units) if units else None,
                                   (relayed, out - relayed) if relayed < out else None, (passed, relayed - passed) if passed < relayed else None)
                st.store = keep(full, n)
                sent[n] = (out + units, out, relayed)
                stages.append(st)
        return stages

    def settle(stages, outs):
        for st, o in zip(stages, outs):
            st.store(o)

    def alone(name, stages):
        settle(stages, _comm_only(name, stages))

    def here(n):
        assert sent[n] == (16, 16, 16), (n, sent[n])
        return full[n]

    alone("gather_w_in_near", gathering(w_in=16))
    alone("gather_w_in_relay", gathering())
    alone("gather_w_in_sibling", gathering())

    xs = _to_perm(x[0])
    target = _to_perm(loss_target[0])
    xb = _cast_bf16("cast_x", xs)
    g8 = BLOCK // N_SUB
    ws_t = w_spatial[0].reshape(N_GROUPS, g8, N_SUB, g8, N_SUB).transpose(0, 2, 1, 4, 3).reshape(N_GROUPS, BLOCK, BLOCK)
    bs_t = b_spatial[0].reshape(N_GROUPS, g8, N_SUB).transpose(2, 1, 0).reshape(BLOCK, N_GROUPS)
    idx = _local_index(0)
    causal = jnp.asarray((idx[:, None] >= idx[None, :]).astype(np.float32))
    buckets = jnp.asarray(_bucket_tables())
    bias = _bias_expand(rel_bias, buckets)

    hosted = gathering(w_proj_a=16, w_proj_b=16, w_ff1=1)
    (qkv,), st = _matmul("proj_qkv", xb, here("w_in"), "nn", [F32], n=3 * d_a, stages=hosted)
    settle(hosted, st)
    hosted = gathering(w_out=16, w_ff1=4)
    (rest,), st = _matmul("proj_rest", xb, here("w_in"), "nn", [F32], b_off=3 * d_a, n=d_in - 3 * d_a, stages=hosted)
    settle(hosted, st)
    fwd = []
    for p in range(3):
        hosted = gathering(w_ff1=(3, 5, 3)[p])
        res, st = _attn_fwd(qkv, bias, p, stages=hosted)
        settle(hosted, st)
        fwd.append(res)
    hosted = gathering(w_ff2=1)
    (attn, attn_b, lse), st = _attn_combine([o for o, _ in fwd], [l for _, l in fwd], stages=hosted)
    settle(hosted, st)
    hosted = gathering(w_ff2=2)
    gmlp, st = _gmlp_fwd(rest, ln_v_gain, ln_v_bias, ws_t, bs_t, causal, stages=hosted)
    settle(hosted, st)
    hosted = gathering(w_ff2=3)
    (ya,), st = _matmul("proj_a", attn_b, here("w_proj_a"), "nn", [F32], stages=hosted)
    settle(hosted, st)
    gate_a, gate_b = 2 * d_b, 2 * d_b + D_MODEL

    def merge(acc, ya_, ga, gb):
        return acc, _sigmoid(ga) * ya_ + _sigmoid(gb) * acc

    hosted = gathering(w_ff2=5)
    (yb, merged), st = _matmul("proj_b_merge", gmlp, here("w_proj_b"), "nn", [F32, BF16], merge,
                               [(ya, "mn", 0), (rest, "mn", gate_a), (rest, "mn", gate_b)], tn=256, stages=hosted)
    settle(hosted, st)
    hosted = gathering(w_ff2=3)
    (pre1,), st = _matmul("out_proj", merged, here("w_out"), "nn", [F32], lambda acc, x_: (ALPHA * x_ + acc,), [(xs, "mn", 0)], stages=hosted)
    settle(hosted, st)
    hosted = gathering(w_ff2=2)
    (xhat1, rstd1, h1b), st = _ln1_fwd(pre1, ln1_gain, ln1_bias, stages=hosted)
    settle(hosted, st)

    def relu2(acc, b_):
        r = jnp.maximum(acc + b_, 0.0)
        return r, r * r

    hosted = gathering()
    (relu, fb), st = _matmul("ff1", h1b, here("w_ff1"), "nn", [F32, BF16], relu2, [(b_ff1, "row", 0)], stages=hosted)
    settle(hosted, st)
    alone("gather_w_ff2_sibling", gathering())
    (ff,), _ = _matmul("ff2", fb, here("w_ff2"), "nn", [F32], lambda acc, b_: (acc + b_,), [(b_ff2, "row", 0)], tn=1024, tk=1024)

    core = lax.axis_index("c").astype(I32).reshape(1)
    factors, theirs, sib, pair, chips, transit, reduced = {}, {}, {}, {}, {}, {}, {}

    def grad_for_sibling(n, a, b, stages=()):
        factors[n] = (a, b)
        theirs[n], outs = _dw(f"dw_{n}_sibling", a, b, KINDS[n], core, False, stages=stages)
        settle(stages, outs)

    def to_sibling(n):
        st = _to_sibling_stage(theirs[n])
        st.store = keep(sib, n)
        return st

    def grad_own(n, stages=()):
        pair[n], outs = _dw(f"dw_{n}_own", *factors[n], KINDS[n], core, True, add=sib[n], stages=stages)
        settle(stages, outs)
        chips[n] = lax.empty(pair[n].shape, BF16)
        transit[n] = lax.empty((2, *pair[n].shape[1:]), BF16)
        reduced[n] = (0, 0)

    def reducing(**new):
        stages = []
        for n in list(reduced):
            out, relayed = reduced[n]
            units = new.get(n, 0)
            if units or relayed < out:
                st = _reduce_stage(pair[n], chips[n], transit[n], (out, units) if units else None,
                                   (relayed, out - relayed) if relayed < out else None)

                def store(outs, n=n):
                    chips[n], transit[n] = outs

                st.store = store
                reduced[n] = (out + units, out)
                stages.append(st)
        return stages

    def summed(n):
        assert reduced[n] == (16, 16), (n, reduced[n])
        return chips[n]

    dpre2, dpre2b, g_ln2_gain, g_ln2_bias, g_b_ff2, loss_part = _ln2_loss_bwd(ff, xhat1, ln1_gain, ln1_bias, ln2_gain, ln2_bias, target)
    grad_for_sibling("w_ff2", fb, dpre2b)

    def relu2_bwd(acc, r):
        da = acc * (2.0 * r)
        return da, da

    hosted = [to_sibling("w_ff2")]
    (dab, g_b_ff1), st = _matmul("d_ff1", dpre2b, here("w_ff2"), "nt", [BF16], relu2_bwd, [(relu, "mn", 0)], colsums=(1,), stages=hosted)
    settle(hosted, st)
    grad_own("w_ff2")
    grad_for_sibling("w_ff1", h1b, dab, reducing(w_ff2=6))
    hosted = reducing(w_ff2=10) + [to_sibling("w_ff1")]
    (dh1,), st = _matmul("d_h1", dab, here("w_ff1"), "nt", [F32], lambda acc, d_: (acc + ALPHA * d_,), [(dpre2, "mn", 0)], stages=hosted)
    settle(hosted, st)
    grad_own("w_ff1", reducing())
    hosted = reducing(w_ff1=4)
    (dpre1, dpre1b, g_ln1_gain, g_ln1_bias), st = _ln1_bwd(dh1, xhat1, rstd1, ln1_gain, stages=hosted)
    settle(hosted, st)
    grad_for_sibling("w_out", merged, dpre1b, reducing(w_ff1=2))

    def merge_bwd(acc, ga, gb, ya_, yb_):
        sa, sb = _sigmoid(ga), _sigmoid(gb)
        return acc * sa, acc * sb, acc * ya_ * (sa * (1.0 - sa)), acc * yb_ * (sb * (1.0 - sb))

    hosted = reducing(w_ff1=8) + [to_sibling("w_out")]
    (dya, dyb, dga, dgb), st = _matmul("d_merge", dpre1b, here("w_out"), "nt", [BF16] * 4, merge_bwd,
                                       [(rest, "mn", gate_a), (rest, "mn", gate_b), (ya, "mn", 0), (yb, "mn", 0)], tn=256, stages=hosted)
    settle(hosted, st)
    grad_own("w_out", reducing())
    grad_for_sibling("w_proj_a", attn_b, dya, reducing(w_ff1=2))
    grad_for_sibling("w_proj_b", gmlp, dyb, reducing())
    hosted = [to_sibling("w_proj_a"), to_sibling("w_proj_b")]
    (dattn,), st = _matmul("d_attn", dya, here("w_proj_a"), "nt", [F32], stages=hosted)
    settle(hosted, st)
    grad_own("w_proj_a")
    grad_own("w_proj_b")
    hosted = reducing(w_out=16)
    (dgmlp,), st = _matmul("d_gmlp", dyb, here("w_proj_b"), "nt", [F32], stages=hosted)
    settle(hosted, st)
    hosted = reducing(w_proj_a=16)
    (du, dvb, dws_t, dbs_t, g_lnv_gain, g_lnv_bias), st = _gmlp_bwd(rest, dgmlp, ln_v_gain, ln_v_bias, ws_t, bs_t, causal, stages=hosted)
    settle(hosted, st)
    delta = _attn_delta(dattn, attn)
    bwd = []
    for p in range(3):
        hosted = reducing(w_proj_b=16) if p == 0 else reducing()
        res, st = _attn_bwd(qkv, dattn, lse, delta, bias, p, stages=hosted)
        settle(hosted, st)
        bwd.append(res)
    g_rel_bias = _rel_bias_grad([b[3] for b in bwd], buckets)
    dproj = _assemble_dproj([b[i] for i in range(3) for b in bwd], du, dvb, dga, dgb)

    g_w_spatial = dws_t.reshape(N_GROUPS, N_SUB, g8, N_SUB, g8).transpose(0, 2, 1, 4, 3)
    g_b_spatial = dbs_t[:, :N_GROUPS].reshape(N_SUB, g8, N_GROUPS).transpose(2, 1, 0)
    part = _pack(dict(loss=loss_part, rel_bias=g_rel_bias, ln_v_gain=g_lnv_gain, ln_v_bias=g_lnv_bias, w_spatial=g_w_spatial,
                      b_spatial=g_b_spatial, ln1_gain=g_ln1_gain, ln1_bias=g_ln1_bias, b_ff1=g_b_ff1, b_ff2=g_b_ff2,
                      ln2_gain=g_ln2_gain, ln2_bias=g_ln2_bias))
    small = _small_stage(part)
    small.store = keep(sib, "small")
    grad_for_sibling("w_in", xb, dproj, [small])
    parts = sib["small"]
    half = SEQ // 2

    def add_residual(acc, d_):
        return (acc + ALPHA * d_,)

    hosted = [to_sibling("w_in")]
    (dx0,), st = _matmul("d_x0", dproj, here("w_in"), "nt", [F32], add_residual, [(dpre1, "mn", 0)], tm=1024, tn=1024, tk=3072, m=half,
                         stages=hosted)
    settle(hosted, st)
    grad_own("w_in")
    hosted = reducing(w_in=8)
    (dx1,), st = _matmul("d_x1", dproj, here("w_in"), "nt", [F32], add_residual, [(dpre1, "mn", 0)], tm=1024, tn=1024, tk=3072, m_off=half, m=half,
                         stages=hosted)
    settle(hosted, st)
    grad_x = _from_perm(jnp.concatenate([dx0, dx1], axis=0))[None]

    out_g, out_d, out_m, out_v = {}, {}, {}, {}
    for n, units in (("w_ff2", 4), ("w_ff1", 4), ("w_out", 0), ("w_proj_a", 0), ("w_proj_b", 0), ("w_in", 0)):
        hosted = reducing(w_in=units) if n != "w_in" else []
        (g, d, nm, nv), st = _adam_shard(f"adam_{n}", summed(n), weights[n][0], mom1[n][0], mom2[n][0], stages=hosted)
        settle(hosted, st)
        out_g[n], out_d[n], out_m[n], out_v[n] = g[None], d[None], nm[None], nv[None]

    zero = jnp.zeros((1,), F32)
    sg, sd, sm, sv = (_unpack(b) for b in _adam_small(
        parts, _pack({**weights, "loss": zero}), _pack({**mom1, "loss": zero}), _pack({**mom2, "loss": zero})))
    for n in WEIGHT_ORDER:
        if n not in KINDS:
            shape = weights[n].shape
            out_g[n], out_d[n], out_m[n], out_v[n] = (t[n].reshape(shape) for t in (sg, sd, sm, sv))
    loss = sg["loss"].reshape(())
    return (loss, grad_x, *[out_g[n] for n in WEIGHT_ORDER], *[out_d[n] for n in WEIGHT_ORDER],
            *[out_m[n] for n in WEIGHT_ORDER], *[out_v[n] for n in WEIGHT_ORDER])
```

```python
import math

import jax
import jax.numpy as jnp
import numpy as np
from jax import lax
from jax.experimental import pallas as pl
from jax.experimental.pallas import tpu as pltpu

F32 = jnp.float32
BF16 = jnp.bfloat16
I32 = jnp.int32

SEQ = 2048
D_MODEL = 2048
HEAD_DIM = 128
N_HEADS = 8
N_GROUPS = 8
D_FF = 4 * D_MODEL
BLOCK = 128
DILATIONS = (1, 4, 16)
N_BUCKETS = 32
MAX_DISTANCE = 2048
ALPHA = 2.0 ** 0.25
LN_EPS = 1e-5
NEG_INF = -1e30
N_DEV = 8
N_CHIPS = 4
N_SUB = 16
ADAM_LR, ADAM_B1, ADAM_B2, ADAM_EPS, ADAM_WD, ADAM_STEP = 0.001, 0.9, 0.999, 1e-08, 0.01, 10
LANES = 128
SUBLANES = 8
VMEM_LIMIT = 56 * 1024 * 1024
MESH = pl.DeviceIdType.MESH
ANY = pl.BlockSpec(memory_space=pl.ANY)
WEIGHT_ORDER = ("w_in", "rel_bias", "ln_v_gain", "ln_v_bias", "w_spatial", "b_spatial", "w_proj_a", "w_proj_b", "w_out",
                "ln1_gain", "ln1_bias", "w_ff1", "b_ff1", "w_ff2", "b_ff2", "ln2_gain", "ln2_bias")
KINDS = {"w_in": "col", "w_proj_a": "col", "w_proj_b": "col", "w_out": "row", "w_ff1": "col", "w_ff2": "row"}


def _d_a():
    return N_HEADS * HEAD_DIM


def _d_b():
    return N_GROUPS * BLOCK


def _d_in():
    return 3 * _d_a() + 2 * _d_b() + 2 * D_MODEL


def _pick(t, n, *others):
    if n <= t and all(o % n == 0 for o in others):
        return n
    for c in range(min(t, n) // LANES * LANES, 0, -LANES):
        if n % c == 0 and all(o % c == 0 for o in others):
            return c
    raise ValueError((t, n, others))


class _Stage:
    def __init__(self, ins, outs, alias, sems, start, finish):
        self.ins, self.outs, self.alias, self.sems, self.start, self.finish = ins, outs, alias, sems, start, finish


def _call(name, body, grid, in_specs, out_specs, out_shape, operands, scratch=(), sem=None, stages=(), sequential=False, prefetch=None):
    n_in, n_out, n_sc = len(in_specs), len(out_specs), len(scratch)
    st_in = [len(s.ins) for s in stages]
    st_out = [len(s.outs) for s in stages]
    st_sem = [len(s.sems) for s in stages]
    n_pre = 0 if prefetch is None else 1
    aliases, ioff, ooff = {}, n_in + n_pre, n_out
    for s, ni, no in zip(stages, st_in, st_out):
        for i, o in s.alias.items():
            aliases[ioff + i] = ooff + o
        ioff, ooff = ioff + ni, ooff + no

    def split(refs, counts):
        out, at = [], 0
        for c in counts:
            out.append(refs[at:at + c])
            at += c
        return out

    def wrapped(*refs):
        ins, sins, outs, souts, sc, ssems = split(refs[n_pre:], [n_in, sum(st_in), n_out, sum(st_out), n_sc, sum(st_sem)])
        parts = list(zip(stages, split(sins, st_in), split(souts, st_out), split(ssems, st_sem)))
        if sequential:
            for s, a, b, c in parts:
                s.start(a, b, c)
                s.finish(a, b, c)
            return
        if parts:
            first = _all_of([pl.program_id(i) == 0 for i in range(len(grid))])
            last = _all_of([pl.program_id(i) == g - 1 for i, g in enumerate(grid)])

            @pl.when(first)
            def _():
                for s, a, b, c in parts:
                    s.start(a, b, c)

        body(*ins, *outs, *sc)
        if parts:
            @pl.when(last)
            def _():
                for s, a, b, c in parts:
                    s.finish(a, b, c)

    if stages or sem is None:
        sem = ("arbitrary",) * len(grid)
    specs = dict(grid=grid, in_specs=list(in_specs) + [ANY] * sum(st_in), out_specs=list(out_specs) + [ANY] * sum(st_out),
                 scratch_shapes=list(scratch) + [x for s in stages for x in s.sems])
    if prefetch is not None:
        specs = dict(grid_spec=pltpu.PrefetchScalarGridSpec(num_scalar_prefetch=1, **specs))
    res = pl.pallas_call(
        wrapped, name=name, out_shape=list(out_shape) + [o for s in stages for o in s.outs], input_output_aliases=aliases,
        compiler_params=pltpu.CompilerParams(dimension_semantics=sem, vmem_limit_bytes=VMEM_LIMIT), **specs,
    )(*([prefetch] if n_pre else []), *operands, *[a for s in stages for a in s.ins])
    res = list(res)
    return res[:n_out], split(res[n_out:], st_out)


def _all_of(conds):
    out = conds[0]
    for c in conds[1:]:
        out = out & c
    return out


def _coords():
    return lax.axis_index("x"), lax.axis_index("y"), lax.axis_index("c")


def _other_chips(x, y):
    return ((1 - x, y), (x, 1 - y), (1 - x, 1 - y))


def _lin(dev):
    return 4 * dev[0] + 2 * dev[1] + dev[2]


def _piece(total, lo, n, units=16):
    assert total % units == 0
    return lo * (total // units), n * (total // units)


FLOWS = 4


def _split(lo, cnt):
    k = next(k for k in (FLOWS, 2, 1) if cnt % (2 * SUBLANES * k) == 0)
    return [(lo + i * (cnt // k), cnt // k) for i in range(k)]


def _remote(src, dst, send, recv, to):
    return pltpu.make_async_remote_copy(src_ref=src, dst_ref=dst, send_sem=send, recv_sem=recv, device_id=to, device_id_type=MESH)


def _placer(kind, n, lo, cnt):
    def place(ref, dev):
        if kind == "col":
            return ref.at[pl.ds(lo, cnt), pl.ds(pl.multiple_of(_lin(dev) * n, LANES), n)]
        return ref.at[pl.ds(pl.multiple_of(_lin(dev) * n + lo, 2 * SUBLANES), cnt), :]
    return place


def _spread_stage(shard, full, kind, piece=(0, 16)):
    n = shard.shape[1] if kind == "col" else shard.shape[0]
    lo, cnt = _piece(shard.shape[0], *piece)
    parts = _split(lo, cnt)

    def copies(ins, outs, sems):
        send, recv, local = sems
        x, y, c = _coords()
        me = (x, y, c)
        peers = [(x, y, 1 - c), (1 - x, y, c), (x, 1 - y, c)]
        own = pltpu.make_async_copy(ins[0].at[pl.ds(lo, cnt), :], _placer(kind, n, lo, cnt)(outs[0], me), local)
        out, arrive = [], []
        for i, (plo, pcnt) in enumerate(parts):
            src, place = ins[0].at[pl.ds(plo, pcnt), :], _placer(kind, n, plo, pcnt)
            for k, t in enumerate(peers):
                out.append(_remote(src, place(outs[0], me), send.at[i, k], recv.at[i, k], t))
                arrive.append(_remote(src, place(outs[0], t), send.at[i, k], recv.at[i, k], t))
        return own, out, arrive

    def start(ins, outs, sems):
        own, out, _ = copies(ins, outs, sems)
        own.start()
        for cp in out:
            cp.start()

    def finish(ins, outs, sems):
        own, out, arrive = copies(ins, outs, sems)
        for cp in arrive:
            cp.wait_recv()
        for cp in out:
            cp.wait_send()
        own.wait()

    return _Stage([shard, full], [jax.ShapeDtypeStruct(full.shape, full.dtype)], {1: 0},
                  [pltpu.SemaphoreType.DMA((len(parts), 3)), pltpu.SemaphoreType.DMA((len(parts), 3)), pltpu.SemaphoreType.DMA], start, finish)


def _relay_stage(full, kind, piece=(0, 16)):
    n = (full.shape[1] if kind == "col" else full.shape[0]) // N_DEV
    lo, cnt = _piece(full.shape[0] if kind == "col" else n, *piece)
    half = cnt // 2
    assert half % (2 * SUBLANES) == 0, (cnt, kind)
    tops, bottoms = _split(lo, half), _split(lo + half, half)

    def copies(ins, outs, sems):
        send, recv = sems
        x, y, c = _coords()
        xn, yn, dg = (1 - x, y, c), (x, 1 - y, c), (1 - x, 1 - y, c)
        out, arrive, k = [], [], 0
        for came_from, to, parts in ((yn, xn, tops), (xn, yn, bottoms)):
            for plo, pcnt in parts:
                place = _placer(kind, n, plo, pcnt)
                out.append(_remote(place(outs[0], came_from), place(outs[0], came_from), send.at[k], recv.at[k], to))
                arrive.append(_remote(place(outs[0], dg), place(outs[0], dg), send.at[k], recv.at[k], to))
                k += 1
        return out, arrive

    def start(ins, outs, sems):
        for cp in copies(ins, outs, sems)[0]:
            cp.start()

    def finish(ins, outs, sems):
        out, arrive = copies(ins, outs, sems)
        for cp in arrive:
            cp.wait_recv()
        for cp in out:
            cp.wait_send()

    return _Stage([full], [jax.ShapeDtypeStruct(full.shape, full.dtype)], {0: 0},
                  [pltpu.SemaphoreType.DMA((len(tops) + len(bottoms),)), pltpu.SemaphoreType.DMA((len(tops) + len(bottoms),))], start, finish)


def _forward_stage(full, kind, piece=(0, 16)):
    n = (full.shape[1] if kind == "col" else full.shape[0]) // N_DEV
    lo, cnt = _piece(full.shape[0] if kind == "col" else n, *piece)
    place = _placer(kind, n, lo, cnt)

    def copies(ins, outs, sems):
        send, recv = sems
        x, y, c = _coords()
        chips = _other_chips(x, y)
        out = [_remote(place(outs[0], (*chip, c)), place(outs[0], (*chip, c)), send.at[k], recv.at[k], (x, y, 1 - c)) for k, chip in enumerate(chips)]
        arrive = [_remote(place(outs[0], (*chip, 1 - c)), place(outs[0], (*chip, 1 - c)), send.at[k], recv.at[k], (x, y, 1 - c))
                  for k, chip in enumerate(chips)]
        return out, arrive

    def start(ins, outs, sems):
        for cp in copies(ins, outs, sems)[0]:
            cp.start()

    def finish(ins, outs, sems):
        out, arrive = copies(ins, outs, sems)
        for cp in arrive:
            cp.wait_recv()
        for cp in out:
            cp.wait_send()

    return _Stage([full], [jax.ShapeDtypeStruct(full.shape, full.dtype)], {0: 0},
                  [pltpu.SemaphoreType.DMA((3,)), pltpu.SemaphoreType.DMA((3,))], start, finish)


def _to_sibling_stage(theirs):
    def copies(ins, outs, sems):
        send, recv = sems
        x, y, c = _coords()
        return [_remote(ins[0].at[q], outs[0].at[q], send.at[q], recv.at[q], (x, y, 1 - c)) for q in range(N_CHIPS)]

    def start(ins, outs, sems):
        for cp in copies(ins, outs, sems):
            cp.start()

    def finish(ins, outs, sems):
        for cp in copies(ins, outs, sems):
            cp.wait()

    return _Stage([theirs], [jax.ShapeDtypeStruct(theirs.shape, BF16)], {},
                  [pltpu.SemaphoreType.DMA((N_CHIPS,)), pltpu.SemaphoreType.DMA((N_CHIPS,))], start, finish)


def _to_chips_stage(pair, dst, transit, piece=(0, 16)):
    lo, cnt = _piece(pair.shape[1], *piece)
    half = cnt // 2
    assert half % (2 * SUBLANES) == 0, cnt
    whole, tops, bottoms = _split(lo, cnt), _split(lo, half), _split(lo + half, half)
    nsem = 2 * len(whole) + len(tops) + len(bottoms)

    def copies(ins, outs, sems):
        send, recv, local = sems
        x, y, c = _coords()
        mine, qx, qy, qd = 2 * x + y, 2 * (1 - x) + y, 2 * x + 1 - y, 2 * (1 - x) + 1 - y
        xn, yn = (1 - x, y, c), (x, 1 - y, c)
        pair_ref, (dst_ref, transit_ref) = ins[0], outs
        own = pltpu.make_async_copy(pair_ref.at[mine, pl.ds(lo, cnt), :], dst_ref.at[mine, pl.ds(lo, cnt), :], local)
        out, arrive, k = [], [], 0
        for q, to, parts in ((qx, xn, whole), (qy, yn, whole)):
            for plo, pcnt in parts:
                rows = pl.ds(plo, pcnt)
                out.append(_remote(pair_ref.at[q, rows, :], dst_ref.at[mine, rows, :], send.at[k], recv.at[k], to))
                arrive.append(_remote(pair_ref.at[q, rows, :], dst_ref.at[q, rows, :], send.at[k], recv.at[k], to))
                k += 1
        for slot, to, parts in ((0, xn, tops), (1, yn, bottoms)):
            for plo, pcnt in parts:
                rows = pl.ds(plo, pcnt)
                out.append(_remote(pair_ref.at[qd, rows, :], transit_ref.at[slot, rows, :], send.at[k], recv.at[k], to))
                arrive.append(_remote(pair_ref.at[qd, rows, :], transit_ref.at[slot, rows, :], send.at[k], recv.at[k], to))
                k += 1
        return own, out, arrive

    def start(ins, outs, sems):
        own, out, _ = copies(ins, outs, sems)
        own.start()
        for cp in out:
            cp.start()

    def finish(ins, outs, sems):
        own, out, arrive = copies(ins, outs, sems)
        for cp in arrive:
            cp.wait_recv()
        for cp in out:
            cp.wait_send()
        own.wait()

    return _Stage([pair, dst, transit], [jax.ShapeDtypeStruct(dst.shape, dst.dtype), jax.ShapeDtypeStruct(transit.shape, transit.dtype)],
                  {1: 0, 2: 1}, [pltpu.SemaphoreType.DMA((nsem,)), pltpu.SemaphoreType.DMA((nsem,)), pltpu.SemaphoreType.DMA], start, finish)


def _to_chips_relay_stage(dst, transit, piece=(0, 16)):
    lo, cnt = _piece(dst.shape[1], *piece)
    half = cnt // 2
    tops, bottoms = _split(lo, half), _split(lo + half, half)

    def copies(ins, outs, sems):
        send, recv = sems
        x, y, c = _coords()
        qx, qy, qd = 2 * (1 - x) + y, 2 * x + 1 - y, 2 * (1 - x) + 1 - y
        xn, yn = (1 - x, y, c), (x, 1 - y, c)
        transit_ref, dst_ref = ins[1], outs[0]
        out, arrive, k = [], [], 0
        for slot, q, to, parts in ((0, qx, yn, tops), (1, qy, xn, bottoms)):
            for plo, pcnt in parts:
                rows = pl.ds(plo, pcnt)
                out.append(_remote(transit_ref.at[slot, rows, :], dst_ref.at[q, rows, :], send.at[k], recv.at[k], to))
                arrive.append(_remote(transit_ref.at[slot, rows, :], dst_ref.at[qd, rows, :], send.at[k], recv.at[k], to))
                k += 1
        return out, arrive

    def start(ins, outs, sems):
        for cp in copies(ins, outs, sems)[0]:
            cp.start()

    def finish(ins, outs, sems):
        out, arrive = copies(ins, outs, sems)
        for cp in arrive:
            cp.wait_recv()
        for cp in out:
            cp.wait_send()

    return _Stage([dst, transit], [jax.ShapeDtypeStruct(dst.shape, dst.dtype)], {0: 0},
                  [pltpu.SemaphoreType.DMA((len(tops) + len(bottoms),)), pltpu.SemaphoreType.DMA((len(tops) + len(bottoms),))], start, finish)


def _fuse(parts, ins, outs, alias):
    parts = [p for p in parts if p is not None]
    sems = [x for st, _, _ in parts for x in st.sems]

    def run(which):
        def go(i, o, s):
            refs, at = list(i) + list(o), 0
            for st, pi, po in parts:
                getattr(st, which)([refs[k] for k in pi], [refs[k] for k in po], s[at:at + len(st.sems)])
                at += len(st.sems)
        return go

    return _Stage(ins, [jax.ShapeDtypeStruct(o.shape, o.dtype) for o in outs], alias, sems, run("start"), run("finish"))


def _gather_stage(shard, full, kind, new=None, relay=None, forward=None):
    return _fuse([(_spread_stage(shard, full, kind, new), [0, 1], [2]) if new else None,
                  (_relay_stage(full, kind, relay), [1], [2]) if relay else None,
                  (_forward_stage(full, kind, forward), [1], [2]) if forward else None], [shard, full], [full], {1: 0})


def _reduce_stage(pair, dst, transit, new=None, relay=None):
    return _fuse([(_to_chips_stage(pair, dst, transit, new), [0, 1, 2], [3, 4]) if new else None,
                  (_to_chips_relay_stage(dst, transit, relay), [3, 4], [3]) if relay else None], [pair, dst, transit], [dst, transit],
                 {1: 0, 2: 1})


def _small_stage(part):
    def copies(ins, outs, sems):
        send, recv, local = sems
        x, y, c = _coords()
        me = (x, y, c)
        own = pltpu.make_async_copy(ins[0], outs[0].at[_lin(me)], local)
        peers = [(1 - x if k & 4 else x, 1 - y if k & 2 else y, 1 - c if k & 1 else c) for k in range(1, N_DEV)]
        out = [_remote(ins[0], outs[0].at[_lin(me)], send.at[k], recv.at[k], t) for k, t in enumerate(peers)]
        arrive = [_remote(ins[0], outs[0].at[_lin(t)], send.at[k], recv.at[k], t) for k, t in enumerate(peers)]
        return own, out, arrive

    def start(ins, outs, sems):
        own, out, _ = copies(ins, outs, sems)
        own.start()
        for cp in out:
            cp.start()

    def finish(ins, outs, sems):
        own, out, arrive = copies(ins, outs, sems)
        for cp in arrive:
            cp.wait_recv()
        for cp in out:
            cp.wait_send()
        own.wait()

    return _Stage([part], [jax.ShapeDtypeStruct((N_DEV, *part.shape), F32)], {},
                  [pltpu.SemaphoreType.DMA((N_DEV - 1,)), pltpu.SemaphoreType.DMA((N_DEV - 1,)), pltpu.SemaphoreType.DMA], start, finish)


def _comm_only(name, stages):
    return _call(name, lambda: None, (1,), [], [], [], [], stages=stages, sequential=True)[1]


_GELU_C = math.sqrt(2.0 / math.pi)


def _gelu(x):
    return 0.5 * x * (1.0 + jnp.tanh(_GELU_C * (x + 0.044715 * x * x * x)))


def _gelu_grad(x):
    t = jnp.tanh(_GELU_C * (x + 0.044715 * x * x * x))
    return 0.5 * (1.0 + t) + 0.5 * x * (1.0 - t * t) * (_GELU_C * (1.0 + 3.0 * 0.044715 * x * x))


def _sigmoid(x):
    return 1.0 / (1.0 + jnp.exp(-x))


def _dot(a, b, mode):
    dims = {"nn": (((1,), (0,)), ((), ())), "nt": (((1,), (1,)), ((), ())), "tn": (((0,), (0,)), ((), ()))}[mode]
    return lax.dot_general(a.astype(BF16), b.astype(BF16), dims, preferred_element_type=F32)


def _matmul(name, a, b, mode, outs, epi=None, extras=(), colsums=(), tm=2048, tn=512, tk=2048, b_off=0, n=None, m_off=0, m=None, stages=()):
    if mode == "tn":
        kk, mfull = a.shape
    else:
        mfull, kk = a.shape
    m = mfull if m is None else m
    n = (b.shape[0] if mode == "nt" else b.shape[1]) if n is None else n
    tm, tk = _pick(tm, m, m_off), _pick(tk, kk)
    tn = _pick(tn, n, b_off, *[off for _, _, off in extras])
    boff, moff = b_off // tn, m_off // tm
    nm, nn_, nk = m // tm, n // tn, kk // tk
    col_major = bool(colsums)
    grid = (nn_, nm, nk) if col_major else (nm, nn_, nk)

    def imap(f):
        if col_major:
            return lambda g0, g1, k: f(g1, g0, k)
        return f

    a_spec = (pl.BlockSpec((tk, tm), imap(lambda i, j, k: (k, i + moff))) if mode == "tn"
              else pl.BlockSpec((tm, tk), imap(lambda i, j, k: (i + moff, k))))
    b_spec = (pl.BlockSpec((tn, tk), imap(lambda i, j, k: (j + boff, k))) if mode == "nt"
              else pl.BlockSpec((tk, tn), imap(lambda i, j, k: (k, j + boff))))
    in_specs, operands = [a_spec, b_spec], [a, b]
    for arr, kind, off in extras:
        o = off // tn
        if kind == "mn":
            in_specs.append(pl.BlockSpec((tm, tn), imap(lambda i, j, k, o=o: (i + moff, j + o))))
        else:
            in_specs.append(pl.BlockSpec((1, tn), imap(lambda i, j, k, o=o: (0, j + o))))
        operands.append(arr)
    out_shape = [jax.ShapeDtypeStruct((m, n), dt) for dt in outs] + [jax.ShapeDtypeStruct((1, n), F32) for _ in colsums]
    out_specs = ([pl.BlockSpec((tm, tn), imap(lambda i, j, k: (i, j))) for _ in outs]
                 + [pl.BlockSpec((1, tn), imap(lambda i, j, k: (0, j))) for _ in colsums])
    n_ex, n_out, n_cs = len(extras), len(outs), len(colsums)

    def body(*refs):
        a_ref, b_ref = refs[:2]
        ex_refs = refs[2:2 + n_ex]
        out_refs = refs[2 + n_ex:2 + n_ex + n_out]
        cs_refs = refs[2 + n_ex + n_out:2 + n_ex + n_out + n_cs]
        part = _dot(a_ref[...], b_ref[...], mode)

        def finish(acc):
            res = epi(acc, *[r[...] for r in ex_refs]) if epi is not None else (acc,)
            for r, v in zip(out_refs, res[:n_out]):
                r[...] = v.astype(r.dtype)
            if n_cs:
                @pl.when(pl.program_id(1) == 0)
                def _():
                    for r in cs_refs:
                        r[...] = jnp.zeros_like(r)

                for r, idx in zip(cs_refs, colsums):
                    r[...] += jnp.sum(res[idx], axis=0, keepdims=True)

        if nk == 1:
            finish(part)
        else:
            acc_ref = refs[-1]
            k = pl.program_id(2)

            @pl.when(k == 0)
            def _():
                acc_ref[...] = part

            @pl.when(k > 0)
            def _():
                acc_ref[...] += part

            @pl.when(k == nk - 1)
            def _():
                finish(acc_ref[...])

    sem = ("arbitrary", "arbitrary", "arbitrary") if col_major else ("parallel", "parallel", "arbitrary")
    return _call(name, body, grid, in_specs, out_specs, out_shape, operands,
                 scratch=[pltpu.VMEM((tm, tn), F32)] if nk > 1 else [], sem=sem, stages=stages)


def _row_spec(tr, c):
    return pl.BlockSpec((tr, c), lambda i: (i, 0))


def _fix_spec(shape):
    return pl.BlockSpec(shape, lambda *_: tuple(0 for _ in shape))


def _cast_bf16(name, x, tr=512):
    r, c = x.shape
    tr = _pick(tr, r)

    def body(x_ref, o_ref):
        o_ref[...] = x_ref[...].astype(BF16)

    return _call(name, body, (r // tr,), [_row_spec(tr, c)], [_row_spec(tr, c)], [jax.ShapeDtypeStruct((r, c), BF16)], [x],
                 sem=("parallel",))[0][0]


def _layer_norm_stats(x):
    mean = jnp.mean(x, axis=-1, keepdims=True)
    xc = x - mean
    var = jnp.mean(xc * xc, axis=-1, keepdims=True)
    rstd = lax.rsqrt(var + LN_EPS)
    return xc * rstd, rstd


def _layer_norm_bwd(dxhat, xhat, rstd):
    m1 = jnp.mean(dxhat, axis=-1, keepdims=True)
    m2 = jnp.mean(dxhat * xhat, axis=-1, keepdims=True)
    return rstd * (dxhat - m1 - xhat * m2)


def _ln1_fwd(pre1, g1, b1, tr=256, stages=()):
    s, d = pre1.shape
    tr = _pick(tr, s)

    def body(p_ref, g_ref, b_ref, xh_ref, rs_ref, h_ref):
        xhat, rstd = _layer_norm_stats(p_ref[...])
        xh_ref[...] = xhat
        rs_ref[...] = rstd
        h_ref[...] = (xhat * g_ref[...] + b_ref[...]).astype(BF16)

    return _call("ln1_fwd", body, (s // tr,), [_row_spec(tr, d), _fix_spec((1, d)), _fix_spec((1, d))],
                 [_row_spec(tr, d), _row_spec(tr, 1), _row_spec(tr, d)],
                 [jax.ShapeDtypeStruct((s, d), F32), jax.ShapeDtypeStruct((s, 1), F32), jax.ShapeDtypeStruct((s, d), BF16)],
                 [pre1, g1, b1], sem=("parallel",), stages=stages)


def _ln2_loss_bwd(ff, xhat1, g1, b1, g2, b2, target, tr=256):
    s, d = ff.shape
    tr = _pick(tr, s)

    def body(ff_ref, xh1_ref, g1_ref, b1_ref, g2_ref, b2_ref, t_ref, dp_ref, dpb_ref, dg_ref, db_ref, dbf_ref, loss_ref):
        @pl.when(pl.program_id(0) == 0)
        def _():
            dg_ref[...] = jnp.zeros_like(dg_ref)
            db_ref[...] = jnp.zeros_like(db_ref)
            dbf_ref[...] = jnp.zeros_like(dbf_ref)
            loss_ref[...] = jnp.zeros_like(loss_ref)

        h1 = xh1_ref[...] * g1_ref[...] + b1_ref[...]
        xhat, rstd = _layer_norm_stats(ALPHA * h1 + ff_ref[...])
        err = xhat * g2_ref[...] + b2_ref[...] - t_ref[...]
        row = jnp.mean(err * err, axis=-1, keepdims=True)
        loss_ref[...] += 0.5 * jnp.sum(row, axis=0, keepdims=True)
        dy = err / d
        dg_ref[...] += jnp.sum(dy * xhat, axis=0, keepdims=True)
        db_ref[...] += jnp.sum(dy, axis=0, keepdims=True)
        dpre = _layer_norm_bwd(dy * g2_ref[...], xhat, rstd)
        dbf_ref[...] += jnp.sum(dpre, axis=0, keepdims=True)
        dp_ref[...] = dpre
        dpb_ref[...] = dpre.astype(BF16)

    vec = _fix_spec((1, d))
    return _call("ln2_loss_bwd", body, (s // tr,), [_row_spec(tr, d), _row_spec(tr, d), vec, vec, vec, vec, _row_spec(tr, d)],
                 [_row_spec(tr, d), _row_spec(tr, d), vec, vec, vec, _fix_spec((1, 1))],
                 [jax.ShapeDtypeStruct((s, d), F32), jax.ShapeDtypeStruct((s, d), BF16)]
                 + [jax.ShapeDtypeStruct((1, d), F32)] * 3 + [jax.ShapeDtypeStruct((1, 1), F32)],
                 [ff, xhat1, g1, b1, g2, b2, target])[0]


def _ln1_bwd(dh1, xhat1, rstd1, g1, tr=256, stages=()):
    s, d = dh1.shape
    tr = _pick(tr, s)

    def body(dh_ref, xh_ref, rs_ref, g_ref, dp_ref, dpb_ref, dg_ref, db_ref):
        @pl.when(pl.program_id(0) == 0)
        def _():
            dg_ref[...] = jnp.zeros_like(dg_ref)
            db_ref[...] = jnp.zeros_like(db_ref)

        dh, xhat = dh_ref[...], xh_ref[...]
        dg_ref[...] += jnp.sum(dh * xhat, axis=0, keepdims=True)
        db_ref[...] += jnp.sum(dh, axis=0, keepdims=True)
        dpre = _layer_norm_bwd(dh * g_ref[...], xhat, rs_ref[...])
        dp_ref[...] = dpre
        dpb_ref[...] = dpre.astype(BF16)

    vec = _fix_spec((1, d))
    return _call("ln1_bwd", body, (s // tr,), [_row_spec(tr, d), _row_spec(tr, d), _row_spec(tr, 1), vec],
                 [_row_spec(tr, d), _row_spec(tr, d), vec, vec],
                 [jax.ShapeDtypeStruct((s, d), F32), jax.ShapeDtypeStruct((s, d), BF16)] + [jax.ShapeDtypeStruct((1, d), F32)] * 2,
                 [dh1, xhat1, rstd1, g1], stages=stages)


def _to_perm(x):
    return x.reshape(SEQ // N_SUB, N_SUB, -1).transpose(1, 0, 2).reshape(SEQ, -1)


def _from_perm(x):
    return x.reshape(N_SUB, SEQ // N_SUB, -1).transpose(1, 0, 2).reshape(SEQ, -1)


def _local_index(p):
    rho = np.arange(BLOCK)
    if p == 0:
        return 16 * (rho % 8) + rho // 8
    if p == 1:
        return 4 * (rho % 32) + rho // 32
    return rho


def _tile_view(x, p):
    c = x.shape[1]
    if p == 1:
        return x.reshape(4, 4, BLOCK, c)
    return x.reshape(N_SUB, BLOCK, c)


def _view_shape(c, p):
    return (4, 4, BLOCK, c) if p == 1 else (N_SUB, BLOCK, c)


def _tile_spec(p, width, col, shift=0):
    nblk = SEQ // DILATIONS[p] // BLOCK

    def blk(n):
        return jnp.clip(n + shift, 0, nblk - 1)

    if p == 0:
        return pl.BlockSpec((N_SUB, SUBLANES, width), lambda s, n: (0, blk(n), col))
    if p == 1:
        return pl.BlockSpec((4, None, 32, width), lambda s, n: (0, s, blk(n), col))
    return pl.BlockSpec((None, BLOCK, width), lambda s, n: (s, 0, col))


def _tile_grid(p):
    return ((1, 16), (4, 4), (16, 1))[p]


def _t5_bucket(n):
    max_exact = N_BUCKETS // 2
    nf = np.maximum(n, 1).astype(np.float32)
    large = max_exact + (np.log(nf / np.float32(max_exact)) / np.float32(math.log(MAX_DISTANCE / max_exact))
                         * np.float32(N_BUCKETS - max_exact)).astype(np.int32)
    large = np.minimum(large, N_BUCKETS - 1)
    return np.where(n < max_exact, n, large).astype(np.int32)


def _bucket_tables():
    tabs = np.zeros((3, 2, BLOCK, BLOCK), np.int32)
    for p, d in enumerate(DILATIONS):
        i = _local_index(p)
        diff = i[:, None] - i[None, :]
        tabs[p, 0] = np.where(diff <= 0, _t5_bucket((BLOCK + diff) * d), -1)
        tabs[p, 1] = np.where(diff >= 0, _t5_bucket(np.maximum(diff, 0) * d), -1)
    return tabs


def _bias_expand(rel_bias, buckets):
    nh = N_HEADS

    def body(rb_ref, bk_ref, o_ref):
        for w in range(2):
            bk = bk_ref[0, w]
            for h in range(nh):
                val = jnp.zeros((BLOCK, BLOCK), F32)
                for b in range(N_BUCKETS):
                    val = jnp.where(bk == b, rb_ref[b, h], val)
                o_ref[0, h, w] = jnp.where(bk < 0, NEG_INF, val)

    return _call("bias_expand", body, (3,),
                 [pl.BlockSpec(memory_space=pltpu.SMEM), pl.BlockSpec((1, 2, BLOCK, BLOCK), lambda p: (p, 0, 0, 0))],
                 [pl.BlockSpec((1, nh, 2, BLOCK, BLOCK), lambda p: (p, 0, 0, 0, 0))],
                 [jax.ShapeDtypeStruct((3, nh, 2, BLOCK, BLOCK), F32)], [rel_bias, buckets], sem=("parallel",))[0][0]


def _heads_to_lanes(cols):
    lane = lax.broadcasted_iota(I32, (BLOCK, LANES), 1)
    out = jnp.zeros((BLOCK, LANES), F32)
    for h, c in enumerate(cols):
        out = jnp.where(lane == h, c, out)
    return out


def _attn_fwd(qkv, bias, p, stages=()):
    d_a = _d_a()
    has_prev = SEQ // DILATIONS[p] // BLOCK > 1
    scale = HEAD_DIM ** -0.5
    view = _tile_view(qkv, p)

    def body(q_ref, kc_ref, kp_ref, vc_ref, vp_ref, b_ref, o_ref, l_ref):
        n = pl.program_id(1)
        q_all = q_ref[...].reshape(BLOCK, d_a).astype(BF16)
        kc_all = kc_ref[...].reshape(BLOCK, d_a).astype(BF16)
        vc_all = vc_ref[...].reshape(BLOCK, d_a).astype(BF16)
        if has_prev:
            kp_all = kp_ref[...].reshape(BLOCK, d_a).astype(BF16)
            vp_all = vp_ref[...].reshape(BLOCK, d_a).astype(BF16)
        outs, lses = [], []
        for h in range(N_HEADS):
            sl = slice(h * HEAD_DIM, (h + 1) * HEAD_DIM)
            q = q_all[:, sl]
            sc = _dot(q, kc_all[:, sl], "nt") * scale + b_ref[0, h, 1]
            m = jnp.max(sc, axis=-1, keepdims=True)
            if has_prev:
                sp = _dot(q, kp_all[:, sl], "nt") * scale + b_ref[0, h, 0]
                sp = jnp.where(n > 0, sp, NEG_INF)
                m = jnp.maximum(m, jnp.max(sp, axis=-1, keepdims=True))
                pp = jnp.exp(sp - m)
            pc = jnp.exp(sc - m)
            den = jnp.sum(pc, axis=-1, keepdims=True)
            o = _dot(pc, vc_all[:, sl], "nn")
            if has_prev:
                den = den + jnp.sum(pp, axis=-1, keepdims=True)
                o = o + _dot(pp, vp_all[:, sl], "nn")
            outs.append(o / den)
            lses.append(m + jnp.log(den))
        o_ref[...] = jnp.concatenate(outs, axis=-1).reshape(o_ref.shape)
        l_ref[...] = _heads_to_lanes(lses).reshape(l_ref.shape)

    (o, l), st = _call(
        f"attn_fwd{p}", body, _tile_grid(p),
        [_tile_spec(p, d_a, 0), _tile_spec(p, d_a, 1), _tile_spec(p, d_a, 1, -1), _tile_spec(p, d_a, 2), _tile_spec(p, d_a, 2, -1),
         pl.BlockSpec((1, N_HEADS, 2, BLOCK, BLOCK), lambda s, n: (p, 0, 0, 0, 0))],
        [_tile_spec(p, d_a, 0), _tile_spec(p, LANES, 0)],
        [jax.ShapeDtypeStruct(_view_shape(d_a, p), F32), jax.ShapeDtypeStruct(_view_shape(LANES, p), F32)],
        [view, view, view, view, view, bias], sem=("parallel", "parallel"), stages=stages)
    return (o.reshape(SEQ, d_a), l.reshape(SEQ, LANES)), st


def _attn_combine(os_, ls_, tr=256, stages=()):
    d_a = _d_a()
    tr = _pick(tr, SEQ)

    def body(o0, o1, o2, l0, l1, l2, a_ref, ab_ref, lt_ref):
        l = [l0[...], l1[...], l2[...]]
        m = jnp.maximum(jnp.maximum(l[0], l[1]), l[2])
        w = [jnp.exp(x - m) for x in l]
        tot = w[0] + w[1] + w[2]
        lt_ref[...] = m + jnp.log(tot)
        w = [x / tot for x in w]
        for h in range(N_HEADS):
            sl = slice(h * HEAD_DIM, (h + 1) * HEAD_DIM)
            acc = w[0][:, h:h + 1] * o0[:, sl] + w[1][:, h:h + 1] * o1[:, sl] + w[2][:, h:h + 1] * o2[:, sl]
            a_ref[:, sl] = acc
            ab_ref[:, sl] = acc.astype(BF16)

    return _call("attn_combine", body, (SEQ // tr,), [_row_spec(tr, d_a)] * 3 + [_row_spec(tr, LANES)] * 3,
                 [_row_spec(tr, d_a), _row_spec(tr, d_a), _row_spec(tr, LANES)],
                 [jax.ShapeDtypeStruct((SEQ, d_a), F32), jax.ShapeDtypeStruct((SEQ, d_a), BF16), jax.ShapeDtypeStruct((SEQ, LANES), F32)],
                 [*os_, *ls_], sem=("parallel",), stages=stages)


def _attn_delta(dattn, attn, tr=256):
    d_a = _d_a()
    tr = _pick(tr, SEQ)

    def body(d_ref, a_ref, o_ref):
        prod = d_ref[...] * a_ref[...]
        lane = lax.broadcasted_iota(I32, (tr, LANES), 1)
        out = jnp.zeros((tr, LANES), F32)
        for h in range(N_HEADS):
            out = jnp.where(lane == h, jnp.sum(prod[:, h * HEAD_DIM:(h + 1) * HEAD_DIM], axis=-1, keepdims=True), out)
        o_ref[...] = out

    return _call("attn_delta", body, (SEQ // tr,), [_row_spec(tr, d_a)] * 2, [_row_spec(tr, LANES)],
                 [jax.ShapeDtypeStruct((SEQ, LANES), F32)], [dattn, attn], sem=("parallel",))[0][0]


def _attn_bwd(qkv, dattn, lse, delta, bias, p, stages=()):
    d_a = _d_a()
    nblk = SEQ // DILATIONS[p] // BLOCK
    has_next = nblk > 1
    scale = HEAD_DIM ** -0.5
    qv, dov, lv, tv = (_tile_view(x, p) for x in (qkv, dattn, lse, delta))

    def body(q_ref, qn_ref, k_ref, v_ref, do_ref, don_ref, l_ref, ln_ref, t_ref, tn_ref, b_ref, dq_ref, dk_ref, dv_ref, db_ref, carry_ref):
        j = pl.program_id(1)

        @pl.when((pl.program_id(0) == 0) & (j == 0))
        def _():
            db_ref[...] = jnp.zeros_like(db_ref)

        k_all = k_ref[...].reshape(BLOCK, d_a).astype(BF16)
        v_all = v_ref[...].reshape(BLOCK, d_a).astype(BF16)

        def side(qr, dor, lr, tr_, w):
            q_all = qr[...].reshape(BLOCK, d_a).astype(BF16)
            do_all = dor[...].reshape(BLOCK, d_a).astype(BF16)
            l_all = lr[...].reshape(BLOCK, LANES)
            t_all = tr_[...].reshape(BLOCK, LANES)
            dqs, dks, dvs = [], [], []
            for h in range(N_HEADS):
                sl = slice(h * HEAD_DIM, (h + 1) * HEAD_DIM)
                s = _dot(q_all[:, sl], k_all[:, sl], "nt") * scale + b_ref[0, h, w]
                pr = jnp.exp(s - l_all[:, h:h + 1])
                dp = _dot(do_all[:, sl], v_all[:, sl], "nt")
                ds = pr * (dp - t_all[:, h:h + 1])
                db_ref[h, w] += ds
                dqs.append(_dot(ds, k_all[:, sl], "nn") * scale)
                dks.append(_dot(ds, q_all[:, sl], "tn") * scale)
                dvs.append(_dot(pr, do_all[:, sl], "tn"))
            return [jnp.concatenate(x, axis=-1) for x in (dqs, dks, dvs)]

        dq_c, dk_c, dv_c = side(q_ref, do_ref, l_ref, t_ref, 1)
        if has_next:
            dq_ref[...] = (jnp.where(j > 0, carry_ref[...], 0.0) + dq_c).reshape(dq_ref.shape)
            not_last = j < nblk - 1

            @pl.when(not_last)
            def _():
                dq_n, dk_n, dv_n = side(qn_ref, don_ref, ln_ref, tn_ref, 0)
                carry_ref[...] = dq_n
                dk_ref[...] = (dk_c + dk_n).reshape(dk_ref.shape)
                dv_ref[...] = (dv_c + dv_n).reshape(dv_ref.shape)

            @pl.when(jnp.logical_not(not_last))
            def _():
                dk_ref[...] = dk_c.reshape(dk_ref.shape)
                dv_ref[...] = dv_c.reshape(dv_ref.shape)
        else:
            dq_ref[...] = dq_c.reshape(dq_ref.shape)
            dk_ref[...] = dk_c.reshape(dk_ref.shape)
            dv_ref[...] = dv_c.reshape(dv_ref.shape)

    def big(col, shift=0):
        return _tile_spec(p, d_a, col, shift)

    def small(shift=0):
        return _tile_spec(p, LANES, 0, shift)

    (dq, dk, dv, dbias), st = _call(
        f"attn_bwd{p}", body, _tile_grid(p),
        [big(0), big(0, 1), big(1), big(2), big(0), big(0, 1), small(), small(1), small(), small(1),
         pl.BlockSpec((1, N_HEADS, 2, BLOCK, BLOCK), lambda s, n: (p, 0, 0, 0, 0))],
        [big(0), big(0), big(0), pl.BlockSpec((N_HEADS, 2, BLOCK, BLOCK), lambda s, n: (0, 0, 0, 0))],
        [jax.ShapeDtypeStruct(_view_shape(d_a, p), F32)] * 3 + [jax.ShapeDtypeStruct((N_HEADS, 2, BLOCK, BLOCK), F32)],
        [qv, qv, qv, qv, dov, dov, lv, lv, tv, tv, bias], scratch=[pltpu.VMEM((BLOCK, d_a), F32)], stages=stages)
    return (dq.reshape(SEQ, d_a), dk.reshape(SEQ, d_a), dv.reshape(SEQ, d_a), dbias), st


def _rel_bias_grad(dbias, buckets):
    nh = N_HEADS

    def body(d0, d1, d2, bk_ref, o_ref, t_ref):
        ds = (d0, d1, d2)

        def per_bucket(b, carry):
            for h in range(nh):
                acc = jnp.zeros((BLOCK, BLOCK), F32)
                for p in range(3):
                    for w in range(2):
                        acc = acc + jnp.where(bk_ref[p, w] == b, ds[p][h, w], 0.0)
                t_ref[pl.ds(b * nh + h, 1), :] = jnp.sum(acc, axis=0, keepdims=True)
            return carry

        lax.fori_loop(0, N_BUCKETS, per_bucket, 0)
        o_ref[...] = jnp.sum(t_ref[...], axis=-1, keepdims=True)

    return _call("rel_bias_grad", body, (1,), [_fix_spec((nh, 2, BLOCK, BLOCK))] * 3 + [_fix_spec((3, 2, BLOCK, BLOCK))],
                 [_fix_spec((N_BUCKETS * nh, 1))], [jax.ShapeDtypeStruct((N_BUCKETS * nh, 1), F32)], [*dbias, buckets],
                 scratch=[pltpu.VMEM((N_BUCKETS * nh, LANES), F32)])[0][0]


def _gmlp_fwd(rest, gain, bias, ws, bs, causal, stages=()):
    d_b = _d_b()

    def body(u_ref, v_ref, g_ref, b_ref, ws_ref, bs_ref, c_ref, o_ref):
        u = u_ref[...].reshape(BLOCK, d_b)
        xhat, _ = _layer_norm_stats(_gelu(v_ref[...].reshape(BLOCK, d_b)))
        vn = (xhat * g_ref[...] + b_ref[...]).astype(BF16)
        outs = []
        for g in range(N_GROUPS):
            sl = slice(g * BLOCK, (g + 1) * BLOCK)
            w = jnp.where(c_ref[...] > 0, ws_ref[g], 0.0)
            z = _dot(w, vn[:, sl], "nn") + bs_ref[:, g:g + 1]
            outs.append(_gelu(u[:, sl]) * z)
        o_ref[...] = jnp.concatenate(outs, axis=-1).reshape(o_ref.shape)

    (out,), st = _call(
        "gmlp_fwd", body, (1, SEQ // BLOCK),
        [_tile_spec(0, d_b, 0), _tile_spec(0, d_b, 1), _fix_spec((1, d_b)), _fix_spec((1, d_b)),
         _fix_spec((N_GROUPS, BLOCK, BLOCK)), _fix_spec((BLOCK, N_GROUPS)), _fix_spec((BLOCK, BLOCK))],
        [_tile_spec(0, d_b, 0)], [jax.ShapeDtypeStruct(_view_shape(d_b, 0), F32)],
        [_tile_view(rest, 0), _tile_view(rest, 0), gain, bias, ws, bs, causal], sem=("parallel", "parallel"), stages=stages)
    return out.reshape(SEQ, d_b), st


def _gmlp_bwd(rest, dgmlp, gain, bias, ws, bs, causal, stages=()):
    d_b = _d_b()
    nchunk = SEQ // BLOCK

    def body(u_ref, v_ref, dg_ref, g_ref, b_ref, ws_ref, bs_ref, c_ref, du_ref, dv_ref, dws_ref, dbs_ref, dgain_ref, dbias_ref):
        c = pl.program_id(1)

        @pl.when(c == 0)
        def _():
            dws_ref[...] = jnp.zeros_like(dws_ref)
            dbs_ref[...] = jnp.zeros_like(dbs_ref)
            dgain_ref[...] = jnp.zeros_like(dgain_ref)
            dbias_ref[...] = jnp.zeros_like(dbias_ref)

        u = u_ref[...].reshape(BLOCK, d_b)
        v = v_ref[...].reshape(BLOCK, d_b)
        dgm = dg_ref[...].reshape(BLOCK, d_b)
        xhat, rstd = _layer_norm_stats(_gelu(v))
        vn = (xhat * g_ref[...] + b_ref[...]).astype(BF16)
        lane = lax.broadcasted_iota(I32, (BLOCK, LANES), 1)
        dus, dvns = [], []
        dbs = dbs_ref[...]
        for g in range(N_GROUPS):
            sl = slice(g * BLOCK, (g + 1) * BLOCK)
            w = jnp.where(c_ref[...] > 0, ws_ref[g], 0.0).astype(BF16)
            z = _dot(w, vn[:, sl], "nn") + bs_ref[:, g:g + 1]
            dz = dgm[:, sl] * _gelu(u[:, sl])
            dus.append(dgm[:, sl] * z * _gelu_grad(u[:, sl]))
            dws_ref[g] += _dot(dz, vn[:, sl], "nt")
            dbs = dbs + jnp.where(lane == g, jnp.sum(dz, axis=-1, keepdims=True), 0.0)
            dvns.append(_dot(w, dz, "tn"))
        dbs_ref[...] = dbs
        dvn = jnp.concatenate(dvns, axis=-1)
        dgain_ref[...] += jnp.sum(dvn * xhat, axis=0, keepdims=True)
        dbias_ref[...] += jnp.sum(dvn, axis=0, keepdims=True)
        dvg = _layer_norm_bwd(dvn * g_ref[...], xhat, rstd)
        du_ref[...] = jnp.concatenate(dus, axis=-1).reshape(du_ref.shape)
        dv_ref[...] = (dvg * _gelu_grad(v)).reshape(dv_ref.shape)

        @pl.when(c == nchunk - 1)
        def _():
            for g in range(N_GROUPS):
                dws_ref[g] = jnp.where(c_ref[...] > 0, dws_ref[g], 0.0)

    (du, dv, dws, dbs, dgain, dbias), st = _call(
        "gmlp_bwd", body, (1, nchunk),
        [_tile_spec(0, d_b, 0), _tile_spec(0, d_b, 1), _tile_spec(0, d_b, 0), _fix_spec((1, d_b)), _fix_spec((1, d_b)),
         _fix_spec((N_GROUPS, BLOCK, BLOCK)), _fix_spec((BLOCK, N_GROUPS)), _fix_spec((BLOCK, BLOCK))],
        [_tile_spec(0, d_b, 0), _tile_spec(0, d_b, 0), _fix_spec((N_GROUPS, BLOCK, BLOCK)), _fix_spec((BLOCK, LANES)),
         _fix_spec((1, d_b)), _fix_spec((1, d_b))],
        [jax.ShapeDtypeStruct(_view_shape(d_b, 0), F32)] * 2
        + [jax.ShapeDtypeStruct((N_GROUPS, BLOCK, BLOCK), F32), jax.ShapeDtypeStruct((BLOCK, LANES), F32)]
        + [jax.ShapeDtypeStruct((1, d_b), F32)] * 2,
        [_tile_view(rest, 0), _tile_view(rest, 0), _tile_view(dgmlp, 0), gain, bias, ws, bs, causal], stages=stages)
    return (du.reshape(SEQ, d_b), dv.reshape(SEQ, d_b), dws, dbs, dgain, dbias), st


def _assemble_dproj(dqkv, du, dv, dga, dgb, tr=128):
    d_a, d_b, d_in = _d_a(), _d_b(), _d_in()
    tr = _pick(tr, SEQ)

    def body(*refs):
        att, (du_ref, dv_ref, dga_ref, dgb_ref, o_ref) = refs[:9], refs[9:]
        for i in range(3):
            o_ref[:, i * d_a:(i + 1) * d_a] = (att[3 * i][...] + att[3 * i + 1][...] + att[3 * i + 2][...]).astype(BF16)
        o_ref[:, 3 * d_a:3 * d_a + d_b] = du_ref[...].astype(BF16)
        o_ref[:, 3 * d_a + d_b:3 * d_a + 2 * d_b] = dv_ref[...].astype(BF16)
        o_ref[:, 3 * d_a + 2 * d_b:3 * d_a + 2 * d_b + D_MODEL] = dga_ref[...]
        o_ref[:, 3 * d_a + 2 * d_b + D_MODEL:] = dgb_ref[...]

    return _call("assemble_dproj", body, (SEQ // tr,), [_row_spec(tr, d_a)] * 9 + [_row_spec(tr, d_b)] * 2 + [_row_spec(tr, D_MODEL)] * 2,
                 [_row_spec(tr, d_in)], [jax.ShapeDtypeStruct((SEQ, d_in), BF16)], [*dqkv, du, dv, dga, dgb], sem=("parallel",))[0][0]


def _dw(name, a, b, kind, core, mine, add=None, tn=1152, stages=()):
    s, m = a.shape
    n = b.shape[1]
    rs, cs = (m, n // N_DEV) if kind == "col" else (m // N_DEV, n)
    tn = _pick(tn if kind == "col" else 512, cs)
    nj = cs // tn

    def shard(q, c_ref):
        return 2 * q + (c_ref[0] if mine else 1 - c_ref[0])

    if kind == "col":
        a_spec = pl.BlockSpec((s, m), lambda q, j, c_ref: (0, 0))
        b_spec = pl.BlockSpec((s, tn), lambda q, j, c_ref: (0, shard(q, c_ref) * nj + j))
    else:
        a_spec = pl.BlockSpec((s, rs), lambda q, j, c_ref: (0, shard(q, c_ref)))
        b_spec = pl.BlockSpec((s, tn), lambda q, j, c_ref: (0, j))
    o_spec = pl.BlockSpec((None, rs, tn), lambda q, j, c_ref: (q, 0, j))

    def body(a_ref, b_ref, *rest):
        acc = _dot(a_ref[...], b_ref[...], "tn")
        if add is not None:
            acc = acc + rest[0][...].astype(F32)
        rest[-1][...] = acc.astype(BF16)

    (out,), st = _call(name, body, (N_CHIPS, nj), [a_spec, b_spec] + ([o_spec] if add is not None else []), [o_spec],
                       [jax.ShapeDtypeStruct((N_CHIPS, rs, cs), BF16)], [a, b] + ([add] if add is not None else []),
                       sem=("parallel", "parallel"), stages=stages, prefetch=core)
    return out, st


def _adamw(w, g, m, v):
    m = ADAM_B1 * m + (1.0 - ADAM_B1) * g
    v = ADAM_B2 * v + (1.0 - ADAM_B2) * (g * g)
    m_hat = m / (1.0 - ADAM_B1 ** ADAM_STEP)
    v_hat = v / (1.0 - ADAM_B2 ** ADAM_STEP)
    delta = -ADAM_LR * (m_hat / (jnp.sqrt(v_hat) + ADAM_EPS) + ADAM_WD * w)
    return delta, m, v


def _adam_shard(name, chip_sums, w, m, v, tr=256, stages=()):
    rs, cs = w.shape
    tr = _pick(tr, rs)

    def body(s_ref, w_ref, m_ref, v_ref, g_ref, d_ref, nm_ref, nv_ref):
        g = s_ref[0].astype(F32)
        for q in range(1, N_CHIPS):
            g = g + s_ref[q].astype(F32)
        d, nm, nv = _adamw(w_ref[...], g, m_ref[...], v_ref[...])
        g_ref[...], d_ref[...], nm_ref[...], nv_ref[...] = g, d, nm, nv

    spec = _row_spec(tr, cs)
    return _call(name, body, (rs // tr,), [pl.BlockSpec((N_CHIPS, tr, cs), lambda i: (0, i, 0)), spec, spec, spec], [spec] * 4,
                 [jax.ShapeDtypeStruct((rs, cs), F32)] * 4, [chip_sums, w, m, v], sem=("parallel",), stages=stages)


def _adam_small(parts, w, m, v):
    rows = w.shape[0]

    def body(p_ref, w_ref, m_ref, v_ref, g_ref, d_ref, nm_ref, nv_ref):
        g = p_ref[0]
        for j in range(1, N_DEV):
            g = g + p_ref[j]
        d, nm, nv = _adamw(w_ref[...], g, m_ref[...], v_ref[...])
        g_ref[...], d_ref[...], nm_ref[...], nv_ref[...] = g, d, nm, nv

    spec = _fix_spec((rows, LANES))
    return _call("adam_small", body, (1,), [_fix_spec((N_DEV, rows, LANES)), spec, spec, spec], [spec] * 4,
                 [jax.ShapeDtypeStruct((rows, LANES), F32)] * 4, [parts, w, m, v])[0]


def _small_sizes():
    d_b = _d_b()
    return (("loss", 1), ("rel_bias", N_BUCKETS * N_HEADS), ("ln_v_gain", d_b), ("ln_v_bias", d_b),
            ("w_spatial", N_GROUPS * BLOCK * BLOCK), ("b_spatial", N_GROUPS * BLOCK), ("ln1_gain", D_MODEL), ("ln1_bias", D_MODEL),
            ("b_ff1", D_FF), ("b_ff2", D_MODEL), ("ln2_gain", D_MODEL), ("ln2_bias", D_MODEL))


def _pack(vals):
    pieces = []
    for name, size in _small_sizes():
        flat = vals[name].reshape(-1).astype(F32)
        padded = -(-size // (SUBLANES * LANES)) * SUBLANES * LANES
        pieces.append(jnp.pad(flat, (0, padded - size)).reshape(-1, LANES))
    return jnp.concatenate(pieces, axis=0)


def _unpack(buf):
    out, row = {}, 0
    for name, size in _small_sizes():
        rows = -(-size // (SUBLANES * LANES)) * SUBLANES
        out[name] = buf[row:row + rows].reshape(-1)[:size]
        row += rows
    return out


def kernel(x, w_in, rel_bias, ln_v_gain, ln_v_bias, w_spatial, b_spatial, w_proj_a, w_proj_b, w_out, ln1_gain, ln1_bias, w_ff1, b_ff1, w_ff2, b_ff2, ln2_gain, ln2_bias, loss_target, m_w_in, m_rel_bias, m_ln_v_gain, m_ln_v_bias, m_w_spatial, m_b_spatial, m_w_proj_a, m_w_proj_b, m_w_out, m_ln1_gain, m_ln1_bias, m_w_ff1, m_b_ff1, m_w_ff2, m_b_ff2, m_ln2_gain, m_ln2_bias, v_w_in, v_rel_bias, v_ln_v_gain, v_ln_v_bias, v_w_spatial, v_b_spatial, v_w_proj_a, v_w_proj_b, v_w_out, v_ln1_gain, v_ln1_bias, v_w_ff1, v_b_ff1, v_w_ff2, v_b_ff2, v_ln2_gain, v_ln2_bias):
    d_a, d_b, d_in = _d_a(), _d_b(), _d_in()
    weights = dict(w_in=w_in, rel_bias=rel_bias, ln_v_gain=ln_v_gain, ln_v_bias=ln_v_bias, w_spatial=w_spatial, b_spatial=b_spatial,
                   w_proj_a=w_proj_a, w_proj_b=w_proj_b, w_out=w_out, ln1_gain=ln1_gain, ln1_bias=ln1_bias, w_ff1=w_ff1, b_ff1=b_ff1,
                   w_ff2=w_ff2, b_ff2=b_ff2, ln2_gain=ln2_gain, ln2_bias=ln2_bias)
    mom1 = dict(w_in=m_w_in, rel_bias=m_rel_bias, ln_v_gain=m_ln_v_gain, ln_v_bias=m_ln_v_bias, w_spatial=m_w_spatial,
                b_spatial=m_b_spatial, w_proj_a=m_w_proj_a, w_proj_b=m_w_proj_b, w_out=m_w_out, ln1_gain=m_ln1_gain,
                ln1_bias=m_ln1_bias, w_ff1=m_w_ff1, b_ff1=m_b_ff1, w_ff2=m_w_ff2, b_ff2=m_b_ff2, ln2_gain=m_ln2_gain, ln2_bias=m_ln2_bias)
    mom2 = dict(w_in=v_w_in, rel_bias=v_rel_bias, ln_v_gain=v_ln_v_gain, ln_v_bias=v_ln_v_bias, w_spatial=v_w_spatial,
                b_spatial=v_b_spatial, w_proj_a=v_w_proj_a, w_proj_b=v_w_proj_b, w_out=v_w_out, ln1_gain=v_ln1_gain,
                ln1_bias=v_ln1_bias, w_ff1=v_w_ff1, b_ff1=v_b_ff1, w_ff2=v_w_ff2, b_ff2=v_b_ff2, ln2_gain=v_ln2_gain, ln2_bias=v_ln2_bias)

    shard = {n: _cast_bf16(f"cast_{n}", weights[n][0]) for n in KINDS}
    full, sent = {}, {n: (0, 0, 0) for n in KINDS}
    for n, kind in KINDS.items():
        r, c = shard[n].shape
        full[n] = lax.empty((r, c * N_DEV) if kind == "col" else (r * N_DEV, c), BF16)

    def keep(table, n):
        def store(outs):
            table[n] = outs[0]
        return store

    def gathering(**new):
        stages = []
        for n in KINDS:
            out, relayed, passed = sent[n]
            units = new.get(n, 0)
            if units or relayed < out or passed < relayed:
                st = _gather_stage(shard[n], full[n], KINDS[n], (out, units) if units else None,
                                   (relayed, out - relayed) if relayed < out else None, (passed, relayed - passed) if passed < relayed else None)
                st.store = keep(full, n)
                sent[n] = (out + units, out, relayed)
                stages.append(st)
        return stages

    def settle(stages, outs):
        for st, o in zip(stages, outs):
            st.store(o)

    def alone(name, stages):
        settle(stages, _comm_only(name, stages))

    def here(n):
        assert sent[n] == (16, 16, 16), (n, sent[n])
        return full[n]

    alone("gather_w_in_near", gathering(w_in=16))
    alone("gather_w_in_relay", gathering())
    alone("gather_w_in_sibling", gathering())

    xs = _to_perm(x[0])
    target = _to_perm(loss_target[0])
    xb = _cast_bf16("cast_x", xs)
    g8 = BLOCK // N_SUB
    ws_t = w_spatial[0].reshape(N_GROUPS, g8, N_SUB, g8, N_SUB).transpose(0, 2, 1, 4, 3).reshape(N_GROUPS, BLOCK, BLOCK)
    bs_t = b_spatial[0].reshape(N_GROUPS, g8, N_SUB).transpose(2, 1, 0).reshape(BLOCK, N_GROUPS)
    idx = _local_index(0)
    causal = jnp.asarray((idx[:, None] >= idx[None, :]).astype(np.float32))
    buckets = jnp.asarray(_bucket_tables())
    bias = _bias_expand(rel_bias, buckets)

    hosted = gathering(w_proj_a=16, w_proj_b=16, w_ff1=1)
    (qkv,), st = _matmul("proj_qkv", xb, here("w_in"), "nn", [F32], n=3 * d_a, stages=hosted)
    settle(hosted, st)
    hosted = gathering(w_out=16, w_ff1=4)
    (rest,), st = _matmul("proj_rest", xb, here("w_in"), "nn", [F32], b_off=3 * d_a, n=d_in - 3 * d_a, stages=hosted)
    settle(hosted, st)
    fwd = []
    for p in range(3):
        hosted = gathering(w_ff1=(3, 5, 3)[p])
        res, st = _attn_fwd(qkv, bias, p, stages=hosted)
        settle(hosted, st)
        fwd.append(res)
    hosted = gathering(w_ff2=1)
    (attn, attn_b, lse), st = _attn_combine([o for o, _ in fwd], [l for _, l in fwd], stages=hosted)
    settle(hosted, st)
    hosted = gathering(w_ff2=2)
    gmlp, st = _gmlp_fwd(rest, ln_v_gain, ln_v_bias, ws_t, bs_t, causal, stages=hosted)
    settle(hosted, st)
    hosted = gathering(w_ff2=3)
    (ya,), st = _matmul("proj_a", attn_b, here("w_proj_a"), "nn", [F32], stages=hosted)
    settle(hosted, st)
    gate_a, gate_b = 2 * d_b, 2 * d_b + D_MODEL

    def merge(acc, ya_, ga, gb):
        return acc, _sigmoid(ga) * ya_ + _sigmoid(gb) * acc

    hosted = gathering(w_ff2=5)
    (yb, merged), st = _matmul("proj_b_merge", gmlp, here("w_proj_b"), "nn", [F32, BF16], merge,
                               [(ya, "mn", 0), (rest, "mn", gate_a), (rest, "mn", gate_b)], tn=256, stages=hosted)
    settle(hosted, st)
    hosted = gathering(w_ff2=3)
    (pre1,), st = _matmul("out_proj", merged, here("w_out"), "nn", [F32], lambda acc, x_: (ALPHA * x_ + acc,), [(xs, "mn", 0)], stages=hosted)
    settle(hosted, st)
    hosted = gathering(w_ff2=2)
    (xhat1, rstd1, h1b), st = _ln1_fwd(pre1, ln1_gain, ln1_bias, stages=hosted)
    settle(hosted, st)

    def relu2(acc, b_):
        r = jnp.maximum(acc + b_, 0.0)
        return r, r * r

    hosted = gathering()
    (relu, fb), st = _matmul("ff1", h1b, here("w_ff1"), "nn", [F32, BF16], relu2, [(b_ff1, "row", 0)], stages=hosted)
    settle(hosted, st)
    alone("gather_w_ff2_sibling", gathering())
    (ff,), _ = _matmul("ff2", fb, here("w_ff2"), "nn", [F32], lambda acc, b_: (acc + b_,), [(b_ff2, "row", 0)], tn=1024, tk=1024)

    core = lax.axis_index("c").astype(I32).reshape(1)
    factors, theirs, sib, pair, chips, transit, reduced = {}, {}, {}, {}, {}, {}, {}

    def grad_for_sibling(n, a, b, stages=()):
        factors[n] = (a, b)
        theirs[n], outs = _dw(f"dw_{n}_sibling", a, b, KINDS[n], core, False, stages=stages)
        settle(stages, outs)

    def to_sibling(n):
        st = _to_sibling_stage(theirs[n])
        st.store = keep(sib, n)
        return st

    def grad_own(n, stages=()):
        pair[n], outs = _dw(f"dw_{n}_own", *factors[n], KINDS[n], core, True, add=sib[n], stages=stages)
        settle(stages, outs)
        chips[n] = lax.empty(pair[n].shape, BF16)
        transit[n] = lax.empty((2, *pair[n].shape[1:]), BF16)
        reduced[n] = (0, 0)

    def reducing(**new):
        stages = []
        for n in list(reduced):
            out, relayed = reduced[n]
            units = new.get(n, 0)
            if units or relayed < out:
                st = _reduce_stage(pair[n], chips[n], transit[n], (out, units) if units else None,
                                   (relayed, out - relayed) if relayed < out else None)

                def store(outs, n=n):
                    chips[n], transit[n] = outs

                st.store = store
                reduced[n] = (out + units, out)
                stages.append(st)
        return stages

    def summed(n):
        assert reduced[n] == (16, 16), (n, reduced[n])
        return chips[n]

    dpre2, dpre2b, g_ln2_gain, g_ln2_bias, g_b_ff2, loss_part = _ln2_loss_bwd(ff, xhat1, ln1_gain, ln1_bias, ln2_gain, ln2_bias, target)
    grad_for_sibling("w_ff2", fb, dpre2b)

    def relu2_bwd(acc, r):
        da = acc * (2.0 * r)
        return da, da

    hosted = [to_sibling("w_ff2")]
    (dab, g_b_ff1), st = _matmul("d_ff1", dpre2b, here("w_ff2"), "nt", [BF16], relu2_bwd, [(relu, "mn", 0)], colsums=(1,), stages=hosted)
    settle(hosted, st)
    grad_own("w_ff2")
    grad_for_sibling("w_ff1", h1b, dab, reducing(w_ff2=6))
    hosted = reducing(w_ff2=10) + [to_sibling("w_ff1")]
    (dh1,), st = _matmul("d_h1", dab, here("w_ff1"), "nt", [F32], lambda acc, d_: (acc + ALPHA * d_,), [(dpre2, "mn", 0)], stages=hosted)
    settle(hosted, st)
    grad_own("w_ff1", reducing())
    hosted = reducing(w_ff1=4)
    (dpre1, dpre1b, g_ln1_gain, g_ln1_bias), st = _ln1_bwd(dh1, xhat1, rstd1, ln1_gain, stages=hosted)
    settle(hosted, st)
    grad_for_sibling("w_out", merged, dpre1b, reducing(w_ff1=2))

    def merge_bwd(acc, ga, gb, ya_, yb_):
        sa, sb = _sigmoid(ga), _sigmoid(gb)
        return acc * sa, acc * sb, acc * ya_ * (sa * (1.0 - sa)), acc * yb_ * (sb * (1.0 - sb))

    hosted = reducing(w_ff1=8) + [to_sibling("w_out")]
    (dya, dyb, dga, dgb), st = _matmul("d_merge", dpre1b, here("w_out"), "nt", [BF16] * 4, merge_bwd,
                                       [(rest, "mn", gate_a), (rest, "mn", gate_b), (ya, "mn", 0), (yb, "mn", 0)], tn=256, stages=hosted)
    settle(hosted, st)
    grad_own("w_out", reducing())
    grad_for_sibling("w_proj_a", attn_b, dya, reducing(w_ff1=2))
    grad_for_sibling("w_proj_b", gmlp, dyb, reducing())
    hosted = [to_sibling("w_proj_a"), to_sibling("w_proj_b")]
    (dattn,), st = _matmul("d_attn", dya, here("w_proj_a"), "nt", [F32], stages=hosted)
    settle(hosted, st)
    grad_own("w_proj_a")
    grad_own("w_proj_b")
    hosted = reducing(w_out=16)
    (dgmlp,), st = _matmul("d_gmlp", dyb, here("w_proj_b"), "nt", [F32], stages=hosted)
    settle(hosted, st)
    hosted = reducing(w_proj_a=16)
    (du, dvb, dws_t, dbs_t, g_lnv_gain, g_lnv_bias), st = _gmlp_bwd(rest, dgmlp, ln_v_gain, ln_v_bias, ws_t, bs_t, causal, stages=hosted)
    settle(hosted, st)
    delta = _attn_delta(dattn, attn)
    bwd = []
    for p in range(3):
        hosted = reducing(w_proj_b=16) if p == 0 else reducing()
        res, st = _attn_bwd(qkv, dattn, lse, delta, bias, p, stages=hosted)
        settle(hosted, st)
        bwd.append(res)
    g_rel_bias = _rel_bias_grad([b[3] for b in bwd], buckets)
    dproj = _assemble_dproj([b[i] for i in range(3) for b in bwd], du, dvb, dga, dgb)

    g_w_spatial = dws_t.reshape(N_GROUPS, N_SUB, g8, N_SUB, g8).transpose(0, 2, 1, 4, 3)
    g_b_spatial = dbs_t[:, :N_GROUPS].reshape(N_SUB, g8, N_GROUPS).transpose(2, 1, 0)
    part = _pack(dict(loss=loss_part, rel_bias=g_rel_bias, ln_v_gain=g_lnv_gain, ln_v_bias=g_lnv_bias, w_spatial=g_w_spatial,
                      b_spatial=g_b_spatial, ln1_gain=g_ln1_gain, ln1_bias=g_ln1_bias, b_ff1=g_b_ff1, b_ff2=g_b_ff2,
                      ln2_gain=g_ln2_gain, ln2_bias=g_ln2_bias))
    small = _small_stage(part)
    small.store = keep(sib, "small")
    grad_for_sibling("w_in", xb, dproj, [small])
    parts = sib["small"]
    half = SEQ // 2

    def add_residual(acc, d_):
        return (acc + ALPHA * d_,)

    hosted = [to_sibling("w_in")]
    (dx0,), st = _matmul("d_x0", dproj, here("w_in"), "nt", [F32], add_residual, [(dpre1, "mn", 0)], tm=1024, tn=1024, tk=3072, m=half,
                         stages=hosted)
    settle(hosted, st)
    grad_own("w_in")
    hosted = reducing(w_in=8)
    (dx1,), st = _matmul("d_x1", dproj, here("w_in"), "nt", [F32], add_residual, [(dpre1, "mn", 0)], tm=1024, tn=1024, tk=3072, m_off=half, m=half,
                         stages=hosted)
    settle(hosted, st)
    grad_x = _from_perm(jnp.concatenate([dx0, dx1], axis=0))[None]

    out_g, out_d, out_m, out_v = {}, {}, {}, {}
    for n, units in (("w_ff2", 4), ("w_ff1", 4), ("w_out", 0), ("w_proj_a", 0), ("w_proj_b", 0), ("w_in", 0)):
        hosted = reducing(w_in=units) if n != "w_in" else []
        (g, d, nm, nv), st = _adam_shard(f"adam_{n}", summed(n), weights[n][0], mom1[n][0], mom2[n][0], stages=hosted)
        settle(hosted, st)
        out_g[n], out_d[n], out_m[n], out_v[n] = g[None], d[None], nm[None], nv[None]

    zero = jnp.zeros((1,), F32)
    sg, sd, sm, sv = (_unpack(b) for b in _adam_small(
        parts, _pack({**weights, "loss": zero}), _pack({**mom1, "loss": zero}), _pack({**mom2, "loss": zero})))
    for n in WEIGHT_ORDER:
        if n not in KINDS:
            shape = weights[n].shape
            out_g[n], out_d[n], out_m[n], out_v[n] = (t[n].reshape(shape) for t in (sg, sd, sm, sv))
    loss = sg["loss"].reshape(())
    return (loss, grad_x, *[out_g[n] for n in WEIGHT_ORDER], *[out_d[n] for n in WEIGHT_ORDER],
            *[out_m[n] for n in WEIGHT_ORDER], *[out_v[n] for n in WEIGHT_ORDER])
```

```python
import math

import jax
import jax.numpy as jnp
import numpy as np
from jax import lax
from jax.experimental import pallas as pl
from jax.experimental.pallas import tpu as pltpu

F32 = jnp.float32
BF16 = jnp.bfloat16
I32 = jnp.int32

SEQ = 2048
D_MODEL = 2048
HEAD_DIM = 128
N_HEADS = 8
N_GROUPS = 8
D_FF = 4 * D_MODEL
BLOCK = 128
DILATIONS = (1, 4, 16)
N_BUCKETS = 32
MAX_DISTANCE = 2048
ALPHA = 2.0 ** 0.25
LN_EPS = 1e-5
NEG_INF = -1e30
N_DEV = 8
N_CHIPS = 4
N_SUB = 16
ADAM_LR, ADAM_B1, ADAM_B2, ADAM_EPS, ADAM_WD, ADAM_STEP = 0.001, 0.9, 0.999, 1e-08, 0.01, 10
LANES = 128
SUBLANES = 8
VMEM_LIMIT = 56 * 1024 * 1024
MESH = pl.DeviceIdType.MESH
ANY = pl.BlockSpec(memory_space=pl.ANY)
WEIGHT_ORDER = ("w_in", "rel_bias", "ln_v_gain", "ln_v_bias", "w_spatial", "b_spatial", "w_proj_a", "w_proj_b", "w_out",
                "ln1_gain", "ln1_bias", "w_ff1", "b_ff1", "w_ff2", "b_ff2", "ln2_gain", "ln2_bias")
KINDS = {"w_in": "col", "w_proj_a": "col", "w_proj_b": "col", "w_out": "row", "w_ff1": "col", "w_ff2": "row"}


def _d_a():
    return N_HEADS * HEAD_DIM


def _d_b():
    return N_GROUPS * BLOCK


def _d_in():
    return 3 * _d_a() + 2 * _d_b() + 2 * D_MODEL


def _pick(t, n, *others):
    if n <= t and all(o % n == 0 for o in others):
        return n
    for c in range(min(t, n) // LANES * LANES, 0, -LANES):
        if n % c == 0 and all(o % c == 0 for o in others):
            return c
    raise ValueError((t, n, others))


class _Stage:
    def __init__(self, ins, outs, alias, sems, start, finish):
        self.ins, self.outs, self.alias, self.sems, self.start, self.finish = ins, outs, alias, sems, start, finish


def _call(name, body, grid, in_specs, out_specs, out_shape, operands, scratch=(), sem=None, stages=(), sequential=False, prefetch=None):
    n_in, n_out, n_sc = len(in_specs), len(out_specs), len(scratch)
    st_in = [len(s.ins) for s in stages]
    st_out = [len(s.outs) for s in stages]
    st_sem = [len(s.sems) for s in stages]
    n_pre = 0 if prefetch is None else 1
    aliases, ioff, ooff = {}, n_in + n_pre, n_out
    for s, ni, no in zip(stages, st_in, st_out):
        for i, o in s.alias.items():
            aliases[ioff + i] = ooff + o
        ioff, ooff = ioff + ni, ooff + no

    def split(refs, counts):
        out, at = [], 0
        for c in counts:
            out.append(refs[at:at + c])
            at += c
        return out

    def wrapped(*refs):
        ins, sins, outs, souts, sc, ssems = split(refs[n_pre:], [n_in, sum(st_in), n_out, sum(st_out), n_sc, sum(st_sem)])
        parts = list(zip(stages, split(sins, st_in), split(souts, st_out), split(ssems, st_sem)))
        if sequential:
            for s, a, b, c in parts:
                s.start(a, b, c)
                s.finish(a, b, c)
            return
        if parts:
            first = _all_of([pl.program_id(i) == 0 for i in range(len(grid))])
            last = _all_of([pl.program_id(i) == g - 1 for i, g in enumerate(grid)])

            @pl.when(first)
            def _():
                for s, a, b, c in parts:
                    s.start(a, b, c)

        body(*ins, *outs, *sc)
        if parts:
            @pl.when(last)
            def _():
                for s, a, b, c in parts:
                    s.finish(a, b, c)

    if stages or sem is None:
        sem = ("arbitrary",) * len(grid)
    specs = dict(grid=grid, in_specs=list(in_specs) + [ANY] * sum(st_in), out_specs=list(out_specs) + [ANY] * sum(st_out),
                 scratch_shapes=list(scratch) + [x for s in stages for x in s.sems])
    if prefetch is not None:
        specs = dict(grid_spec=pltpu.PrefetchScalarGridSpec(num_scalar_prefetch=1, **specs))
    res = pl.pallas_call(
        wrapped, name=name, out_shape=list(out_shape) + [o for s in stages for o in s.outs], input_output_aliases=aliases,
        compiler_params=pltpu.CompilerParams(dimension_semantics=sem, vmem_limit_bytes=VMEM_LIMIT), **specs,
    )(*([prefetch] if n_pre else []), *operands, *[a for s in stages for a in s.ins])
    res = list(res)
    return res[:n_out], split(res[n_out:], st_out)


def _all_of(conds):
    out = conds[0]
    for c in conds[1:]:
        out = out & c
    return out


def _coords():
    return lax.axis_index("x"), lax.axis_index("y"), lax.axis_index("c")


def _other_chips(x, y):
    return ((1 - x, y), (x, 1 - y), (1 - x, 1 - y))


def _lin(dev):
    return 4 * dev[0] + 2 * dev[1] + dev[2]


def _piece(total, lo, n, units=16):
    assert total % units == 0
    return lo * (total // units), n * (total // units)


FLOWS = 4


def _split(lo, cnt):
    k = next(k for k in (FLOWS, 2, 1) if cnt % (2 * SUBLANES * k) == 0)
    return [(lo + i * (cnt // k), cnt // k) for i in range(k)]


def _remote(src, dst, send, recv, to):
    return pltpu.make_async_remote_copy(src_ref=src, dst_ref=dst, send_sem=send, recv_sem=recv, device_id=to, device_id_type=MESH)


def _placer(kind, n, lo, cnt):
    def place(ref, dev):
        if kind == "col":
            return ref.at[pl.ds(lo, cnt), pl.ds(pl.multiple_of(_lin(dev) * n, LANES), n)]
        return ref.at[pl.ds(pl.multiple_of(_lin(dev) * n + lo, 2 * SUBLANES), cnt), :]
    return place


def _spread_stage(shard, full, kind, piece=(0, 16)):
    n = shard.shape[1] if kind == "col" else shard.shape[0]
    lo, cnt = _piece(shard.shape[0], *piece)
    parts = _split(lo, cnt)

    def copies(ins, outs, sems):
        send, recv, local = sems
        x, y, c = _coords()
        me = (x, y, c)
        peers = [(x, y, 1 - c), (1 - x, y, c), (x, 1 - y, c)]
        own = pltpu.make_async_copy(ins[0].at[pl.ds(lo, cnt), :], _placer(kind, n, lo, cnt)(outs[0], me), local)
        out, arrive = [], []
        for k, t in reversed(list(enumerate(peers))):
            for i, (plo, pcnt) in enumerate(parts):
                src, place = ins[0].at[pl.ds(plo, pcnt), :], _placer(kind, n, plo, pcnt)
                out.append(_remote(src, place(outs[0], me), send.at[i, k], recv.at[i, k], t))
                arrive.append(_remote(src, place(outs[0], t), send.at[i, k], recv.at[i, k], t))
        return own, out, arrive

    def start(ins, outs, sems):
        own, out, _ = copies(ins, outs, sems)
        for cp in out:
            cp.start()
        own.start()

    def finish(ins, outs, sems):
        own, out, arrive = copies(ins, outs, sems)
        for cp in arrive:
            cp.wait_recv()
        for cp in out:
            cp.wait_send()
        own.wait()

    return _Stage([shard, full], [jax.ShapeDtypeStruct(full.shape, full.dtype)], {1: 0},
                  [pltpu.SemaphoreType.DMA((len(parts), 3)), pltpu.SemaphoreType.DMA((len(parts), 3)), pltpu.SemaphoreType.DMA], start, finish)


def _relay_stage(full, kind, piece=(0, 16)):
    n = (full.shape[1] if kind == "col" else full.shape[0]) // N_DEV
    lo, cnt = _piece(full.shape[0] if kind == "col" else n, *piece)
    half = cnt // 2
    assert half % (2 * SUBLANES) == 0, (cnt, kind)
    tops, bottoms = _split(lo, half), _split(lo + half, half)

    def copies(ins, outs, sems):
        send, recv = sems
        x, y, c = _coords()
        xn, yn, dg = (1 - x, y, c), (x, 1 - y, c), (1 - x, 1 - y, c)
        out, arrive, k = [], [], 0
        for came_from, to, parts in ((yn, xn, tops), (xn, yn, bottoms)):
            for plo, pcnt in parts:
                place = _placer(kind, n, plo, pcnt)
                out.append(_remote(place(outs[0], came_from), place(outs[0], came_from), send.at[k], recv.at[k], to))
                arrive.append(_remote(place(outs[0], dg), place(outs[0], dg), send.at[k], recv.at[k], to))
                k += 1
        return out, arrive

    def start(ins, outs, sems):
        for cp in copies(ins, outs, sems)[0]:
            cp.start()

    def finish(ins, outs, sems):
        out, arrive = copies(ins, outs, sems)
        for cp in arrive:
            cp.wait_recv()
        for cp in out:
            cp.wait_send()

    return _Stage([full], [jax.ShapeDtypeStruct(full.shape, full.dtype)], {0: 0},
                  [pltpu.SemaphoreType.DMA((len(tops) + len(bottoms),)), pltpu.SemaphoreType.DMA((len(tops) + len(bottoms),))], start, finish)


def _forward_stage(full, kind, piece=(0, 16)):
    n = (full.shape[1] if kind == "col" else full.shape[0]) // N_DEV
    lo, cnt = _piece(full.shape[0] if kind == "col" else n, *piece)
    place = _placer(kind, n, lo, cnt)

    def copies(ins, outs, sems):
        send, recv = sems
        x, y, c = _coords()
        chips = _other_chips(x, y)
        out = [_remote(place(outs[0], (*chip, c)), place(outs[0], (*chip, c)), send.at[k], recv.at[k], (x, y, 1 - c)) for k, chip in enumerate(chips)]
        arrive = [_remote(place(outs[0], (*chip, 1 - c)), place(outs[0], (*chip, 1 - c)), send.at[k], recv.at[k], (x, y, 1 - c))
                  for k, chip in enumerate(chips)]
        return out, arrive

    def start(ins, outs, sems):
        for cp in copies(ins, outs, sems)[0]:
            cp.start()

    def finish(ins, outs, sems):
        out, arrive = copies(ins, outs, sems)
        for cp in arrive:
            cp.wait_recv()
        for cp in out:
            cp.wait_send()

    return _Stage([full], [jax.ShapeDtypeStruct(full.shape, full.dtype)], {0: 0},
                  [pltpu.SemaphoreType.DMA((3,)), pltpu.SemaphoreType.DMA((3,))], start, finish)


def _to_sibling_stage(theirs):
    def copies(ins, outs, sems):
        send, recv = sems
        x, y, c = _coords()
        return [_remote(ins[0].at[q], outs[0].at[q], send.at[q], recv.at[q], (x, y, 1 - c)) for q in range(N_CHIPS)]

    def start(ins, outs, sems):
        for cp in copies(ins, outs, sems):
            cp.start()

    def finish(ins, outs, sems):
        for cp in copies(ins, outs, sems):
            cp.wait()

    return _Stage([theirs], [jax.ShapeDtypeStruct(theirs.shape, BF16)], {},
                  [pltpu.SemaphoreType.DMA((N_CHIPS,)), pltpu.SemaphoreType.DMA((N_CHIPS,))], start, finish)


def _to_chips_stage(pair, dst, transit, piece=(0, 16)):
    lo, cnt = _piece(pair.shape[1], *piece)
    half = cnt // 2
    assert half % (2 * SUBLANES) == 0, cnt
    whole, tops, bottoms = _split(lo, cnt), _split(lo, half), _split(lo + half, half)
    nsem = 2 * len(whole) + len(tops) + len(bottoms)

    def copies(ins, outs, sems):
        send, recv, local = sems
        x, y, c = _coords()
        mine, qx, qy, qd = 2 * x + y, 2 * (1 - x) + y, 2 * x + 1 - y, 2 * (1 - x) + 1 - y
        xn, yn = (1 - x, y, c), (x, 1 - y, c)
        pair_ref, (dst_ref, transit_ref) = ins[0], outs
        own = pltpu.make_async_copy(pair_ref.at[mine, pl.ds(lo, cnt), :], dst_ref.at[mine, pl.ds(lo, cnt), :], local)
        out, arrive, k = [], [], 0
        for q, to, parts in ((qx, xn, whole), (qy, yn, whole)):
            for plo, pcnt in parts:
                rows = pl.ds(plo, pcnt)
                out.append(_remote(pair_ref.at[q, rows, :], dst_ref.at[mine, rows, :], send.at[k], recv.at[k], to))
                arrive.append(_remote(pair_ref.at[q, rows, :], dst_ref.at[q, rows, :], send.at[k], recv.at[k], to))
                k += 1
        for slot, to, parts in ((0, xn, tops), (1, yn, bottoms)):
            for plo, pcnt in parts:
                rows = pl.ds(plo, pcnt)
                out.append(_remote(pair_ref.at[qd, rows, :], transit_ref.at[slot, rows, :], send.at[k], recv.at[k], to))
                arrive.append(_remote(pair_ref.at[qd, rows, :], transit_ref.at[slot, rows, :], send.at[k], recv.at[k], to))
                k += 1
        return own, out, arrive

    def start(ins, outs, sems):
        own, out, _ = copies(ins, outs, sems)
        own.start()
        for cp in out:
            cp.start()

    def finish(ins, outs, sems):
        own, out, arrive = copies(ins, outs, sems)
        for cp in arrive:
            cp.wait_recv()
        for cp in out:
            cp.wait_send()
        own.wait()

    return _Stage([pair, dst, transit], [jax.ShapeDtypeStruct(dst.shape, dst.dtype), jax.ShapeDtypeStruct(transit.shape, transit.dtype)],
                  {1: 0, 2: 1}, [pltpu.SemaphoreType.DMA((nsem,)), pltpu.SemaphoreType.DMA((nsem,)), pltpu.SemaphoreType.DMA], start, finish)


def _to_chips_relay_stage(dst, transit, piece=(0, 16)):
    lo, cnt = _piece(dst.shape[1], *piece)
    half = cnt // 2
    tops, bottoms = _split(lo, half), _split(lo + half, half)

    def copies(ins, outs, sems):
        send, recv = sems
        x, y, c = _coords()
        qx, qy, qd = 2 * (1 - x) + y, 2 * x + 1 - y, 2 * (1 - x) + 1 - y
        xn, yn = (1 - x, y, c), (x, 1 - y, c)
        transit_ref, dst_ref = ins[1], outs[0]
        out, arrive, k = [], [], 0
        for slot, q, to, parts in ((0, qx, yn, tops), (1, qy, xn, bottoms)):
            for plo, pcnt in parts:
                rows = pl.ds(plo, pcnt)
                out.append(_remote(transit_ref.at[slot, rows, :], dst_ref.at[q, rows, :], send.at[k], recv.at[k], to))
                arrive.append(_remote(transit_ref.at[slot, rows, :], dst_ref.at[qd, rows, :], send.at[k], recv.at[k], to))
                k += 1
        return out, arrive

    def start(ins, outs, sems):
        for cp in copies(ins, outs, sems)[0]:
            cp.start()

    def finish(ins, outs, sems):
        out, arrive = copies(ins, outs, sems)
        for cp in arrive:
            cp.wait_recv()
        for cp in out:
            cp.wait_send()

    return _Stage([dst, transit], [jax.ShapeDtypeStruct(dst.shape, dst.dtype)], {0: 0},
                  [pltpu.SemaphoreType.DMA((len(tops) + len(bottoms),)), pltpu.SemaphoreType.DMA((len(tops) + len(bottoms),))], start, finish)


def _fuse(parts, ins, outs, alias):
    parts = [p for p in parts if p is not None]
    sems = [x for st, _, _ in parts for x in st.sems]

    def run(which):
        def go(i, o, s):
            refs, at = list(i) + list(o), 0
            for st, pi, po in parts:
                getattr(st, which)([refs[k] for k in pi], [refs[k] for k in po], s[at:at + len(st.sems)])
                at += len(st.sems)
        return go

    return _Stage(ins, [jax.ShapeDtypeStruct(o.shape, o.dtype) for o in outs], alias, sems, run("start"), run("finish"))


def _gather_stage(shard, full, kind, new=None, relay=None, forward=None):
    return _fuse([(_spread_stage(shard, full, kind, new), [0, 1], [2]) if new else None,
                  (_relay_stage(full, kind, relay), [1], [2]) if relay else None,
                  (_forward_stage(full, kind, forward), [1], [2]) if forward else None], [shard, full], [full], {1: 0})


def _reduce_stage(pair, dst, transit, new=None, relay=None):
    return _fuse([(_to_chips_stage(pair, dst, transit, new), [0, 1, 2], [3, 4]) if new else None,
                  (_to_chips_relay_stage(dst, transit, relay), [3, 4], [3]) if relay else None], [pair, dst, transit], [dst, transit],
                 {1: 0, 2: 1})


def _small_stage(part):
    def copies(ins, outs, sems):
        send, recv, local = sems
        x, y, c = _coords()
        me = (x, y, c)
        own = pltpu.make_async_copy(ins[0], outs[0].at[_lin(me)], local)
        peers = [(1 - x if k & 4 else x, 1 - y if k & 2 else y, 1 - c if k & 1 else c) for k in range(1, N_DEV)]
        out = [_remote(ins[0], outs[0].at[_lin(me)], send.at[k], recv.at[k], t) for k, t in enumerate(peers)]
        arrive = [_remote(ins[0], outs[0].at[_lin(t)], send.at[k], recv.at[k], t) for k, t in enumerate(peers)]
        return own, out, arrive

    def start(ins, outs, sems):
        own, out, _ = copies(ins, outs, sems)
        own.start()
        for cp in out:
            cp.start()

    def finish(ins, outs, sems):
        own, out, arrive = copies(ins, outs, sems)
        for cp in arrive:
            cp.wait_recv()
        for cp in out:
            cp.wait_send()
        own.wait()

    return _Stage([part], [jax.ShapeDtypeStruct((N_DEV, *part.shape), F32)], {},
                  [pltpu.SemaphoreType.DMA((N_DEV - 1,)), pltpu.SemaphoreType.DMA((N_DEV - 1,)), pltpu.SemaphoreType.DMA], start, finish)


def _comm_only(name, stages):
    return _call(name, lambda: None, (1,), [], [], [], [], stages=stages, sequential=True)[1]


_GELU_C = math.sqrt(2.0 / math.pi)


def _gelu(x):
    return 0.5 * x * (1.0 + jnp.tanh(_GELU_C * (x + 0.044715 * x * x * x)))


def _gelu_grad(x):
    t = jnp.tanh(_GELU_C * (x + 0.044715 * x * x * x))
    return 0.5 * (1.0 + t) + 0.5 * x * (1.0 - t * t) * (_GELU_C * (1.0 + 3.0 * 0.044715 * x * x))


def _sigmoid(x):
    return 1.0 / (1.0 + jnp.exp(-x))


def _dot(a, b, mode):
    dims = {"nn": (((1,), (0,)), ((), ())), "nt": (((1,), (1,)), ((), ())), "tn": (((0,), (0,)), ((), ()))}[mode]
    return lax.dot_general(a.astype(BF16), b.astype(BF16), dims, preferred_element_type=F32)


def _matmul(name, a, b, mode, outs, epi=None, extras=(), colsums=(), tm=2048, tn=512, tk=2048, b_off=0, n=None, m_off=0, m=None, stages=()):
    if mode == "tn":
        kk, mfull = a.shape
    else:
        mfull, kk = a.shape
    m = mfull if m is None else m
    n = (b.shape[0] if mode == "nt" else b.shape[1]) if n is None else n
    tm, tk = _pick(tm, m, m_off), _pick(tk, kk)
    tn = _pick(tn, n, b_off, *[off for _, _, off in extras])
    boff, moff = b_off // tn, m_off // tm
    nm, nn_, nk = m // tm, n // tn, kk // tk
    col_major = bool(colsums)
    grid = (nn_, nm, nk) if col_major else (nm, nn_, nk)

    def imap(f):
        if col_major:
            return lambda g0, g1, k: f(g1, g0, k)
        return f

    a_spec = (pl.BlockSpec((tk, tm), imap(lambda i, j, k: (k, i + moff))) if mode == "tn"
              else pl.BlockSpec((tm, tk), imap(lambda i, j, k: (i + moff, k))))
    b_spec = (pl.BlockSpec((tn, tk), imap(lambda i, j, k: (j + boff, k))) if mode == "nt"
              else pl.BlockSpec((tk, tn), imap(lambda i, j, k: (k, j + boff))))
    in_specs, operands = [a_spec, b_spec], [a, b]
    for arr, kind, off in extras:
        o = off // tn
        if kind == "mn":
            in_specs.append(pl.BlockSpec((tm, tn), imap(lambda i, j, k, o=o: (i + moff, j + o))))
        else:
            in_specs.append(pl.BlockSpec((1, tn), imap(lambda i, j, k, o=o: (0, j + o))))
        operands.append(arr)
    out_shape = [jax.ShapeDtypeStruct((m, n), dt) for dt in outs] + [jax.ShapeDtypeStruct((1, n), F32) for _ in colsums]
    out_specs = ([pl.BlockSpec((tm, tn), imap(lambda i, j, k: (i, j))) for _ in outs]
                 + [pl.BlockSpec((1, tn), imap(lambda i, j, k: (0, j))) for _ in colsums])
    n_ex, n_out, n_cs = len(extras), len(outs), len(colsums)

    def body(*refs):
        a_ref, b_ref = refs[:2]
        ex_refs = refs[2:2 + n_ex]
        out_refs = refs[2 + n_ex:2 + n_ex + n_out]
        cs_refs = refs[2 + n_ex + n_out:2 + n_ex + n_out + n_cs]
        part = _dot(a_ref[...], b_ref[...], mode)

        def finish(acc):
            res = epi(acc, *[r[...] for r in ex_refs]) if epi is not None else (acc,)
            for r, v in zip(out_refs, res[:n_out]):
                r[...] = v.astype(r.dtype)
            if n_cs:
                @pl.when(pl.program_id(1) == 0)
                def _():
                    for r in cs_refs:
                        r[...] = jnp.zeros_like(r)

                for r, idx in zip(cs_refs, colsums):
                    r[...] += jnp.sum(res[idx], axis=0, keepdims=True)

        if nk == 1:
            finish(part)
        else:
            acc_ref = refs[-1]
            k = pl.program_id(2)

            @pl.when(k == 0)
            def _():
                acc_ref[...] = part

            @pl.when(k > 0)
            def _():
                acc_ref[...] += part

            @pl.when(k == nk - 1)
            def _():
                finish(acc_ref[...])

    sem = ("arbitrary", "arbitrary", "arbitrary") if col_major else ("parallel", "parallel", "arbitrary")
    return _call(name, body, grid, in_specs, out_specs, out_shape, operands,
                 scratch=[pltpu.VMEM((tm, tn), F32)] if nk > 1 else [], sem=sem, stages=stages)


def _row_spec(tr, c):
    return pl.BlockSpec((tr, c), lambda i: (i, 0))


def _fix_spec(shape):
    return pl.BlockSpec(shape, lambda *_: tuple(0 for _ in shape))


def _cast_bf16(name, x, tr=512):
    r, c = x.shape
    tr = _pick(tr, r)

    def body(x_ref, o_ref):
        o_ref[...] = x_ref[...].astype(BF16)

    return _call(name, body, (r // tr,), [_row_spec(tr, c)], [_row_spec(tr, c)], [jax.ShapeDtypeStruct((r, c), BF16)], [x],
                 sem=("parallel",))[0][0]


def _layer_norm_stats(x):
    mean = jnp.mean(x, axis=-1, keepdims=True)
    xc = x - mean
    var = jnp.mean(xc * xc, axis=-1, keepdims=True)
    rstd = lax.rsqrt(var + LN_EPS)
    return xc * rstd, rstd


def _layer_norm_bwd(dxhat, xhat, rstd):
    m1 = jnp.mean(dxhat, axis=-1, keepdims=True)
    m2 = jnp.mean(dxhat * xhat, axis=-1, keepdims=True)
    return rstd * (dxhat - m1 - xhat * m2)


def _ln1_fwd(pre1, g1, b1, tr=256, stages=()):
    s, d = pre1.shape
    tr = _pick(tr, s)

    def body(p_ref, g_ref, b_ref, xh_ref, rs_ref, h_ref):
        xhat, rstd = _layer_norm_stats(p_ref[...])
        xh_ref[...] = xhat
        rs_ref[...] = rstd
        h_ref[...] = (xhat * g_ref[...] + b_ref[...]).astype(BF16)

    return _call("ln1_fwd", body, (s // tr,), [_row_spec(tr, d), _fix_spec((1, d)), _fix_spec((1, d))],
                 [_row_spec(tr, d), _row_spec(tr, 1), _row_spec(tr, d)],
                 [jax.ShapeDtypeStruct((s, d), F32), jax.ShapeDtypeStruct((s, 1), F32), jax.ShapeDtypeStruct((s, d), BF16)],
                 [pre1, g1, b1], sem=("parallel",), stages=stages)


def _ln2_loss_bwd(ff, xhat1, g1, b1, g2, b2, target, tr=256):
    s, d = ff.shape
    tr = _pick(tr, s)

    def body(ff_ref, xh1_ref, g1_ref, b1_ref, g2_ref, b2_ref, t_ref, dp_ref, dpb_ref, dg_ref, db_ref, dbf_ref, loss_ref):
        @pl.when(pl.program_id(0) == 0)
        def _():
            dg_ref[...] = jnp.zeros_like(dg_ref)
            db_ref[...] = jnp.zeros_like(db_ref)
            dbf_ref[...] = jnp.zeros_like(dbf_ref)
            loss_ref[...] = jnp.zeros_like(loss_ref)

        h1 = xh1_ref[...] * g1_ref[...] + b1_ref[...]
        xhat, rstd = _layer_norm_stats(ALPHA * h1 + ff_ref[...])
        err = xhat * g2_ref[...] + b2_ref[...] - t_ref[...]
        row = jnp.mean(err * err, axis=-1, keepdims=True)
        loss_ref[...] += 0.5 * jnp.sum(row, axis=0, keepdims=True)
        dy = err / d
        dg_ref[...] += jnp.sum(dy * xhat, axis=0, keepdims=True)
        db_ref[...] += jnp.sum(dy, axis=0, keepdims=True)
        dpre = _layer_norm_bwd(dy * g2_ref[...], xhat, rstd)
        dbf_ref[...] += jnp.sum(dpre, axis=0, keepdims=True)
        dp_ref[...] = dpre
        dpb_ref[...] = dpre.astype(BF16)

    vec = _fix_spec((1, d))
    return _call("ln2_loss_bwd", body, (s // tr,), [_row_spec(tr, d), _row_spec(tr, d), vec, vec, vec, vec, _row_spec(tr, d)],
                 [_row_spec(tr, d), _row_spec(tr, d), vec, vec, vec, _fix_spec((1, 1))],
                 [jax.ShapeDtypeStruct((s, d), F32), jax.ShapeDtypeStruct((s, d), BF16)]
                 + [jax.ShapeDtypeStruct((1, d), F32)] * 3 + [jax.ShapeDtypeStruct((1, 1), F32)],
                 [ff, xhat1, g1, b1, g2, b2, target])[0]


def _ln1_bwd(dh1, xhat1, rstd1, g1, tr=256, stages=()):
    s, d = dh1.shape
    tr = _pick(tr, s)

    def body(dh_ref, xh_ref, rs_ref, g_ref, dp_ref, dpb_ref, dg_ref, db_ref):
        @pl.when(pl.program_id(0) == 0)
        def _():
            dg_ref[...] = jnp.zeros_like(dg_ref)
            db_ref[...] = jnp.zeros_like(db_ref)

        dh, xhat = dh_ref[...], xh_ref[...]
        dg_ref[...] += jnp.sum(dh * xhat, axis=0, keepdims=True)
        db_ref[...] += jnp.sum(dh, axis=0, keepdims=True)
        dpre = _layer_norm_bwd(dh * g_ref[...], xhat, rs_ref[...])
        dp_ref[...] = dpre
        dpb_ref[...] = dpre.astype(BF16)

    vec = _fix_spec((1, d))
    return _call("ln1_bwd", body, (s // tr,), [_row_spec(tr, d), _row_spec(tr, d), _row_spec(tr, 1), vec],
                 [_row_spec(tr, d), _row_spec(tr, d), vec, vec],
                 [jax.ShapeDtypeStruct((s, d), F32), jax.ShapeDtypeStruct((s, d), BF16)] + [jax.ShapeDtypeStruct((1, d), F32)] * 2,
                 [dh1, xhat1, rstd1, g1], stages=stages)


def _to_perm(x):
    return x.reshape(SEQ // N_SUB, N_SUB, -1).transpose(1, 0, 2).reshape(SEQ, -1)


def _from_perm(x):
    return x.reshape(N_SUB, SEQ // N_SUB, -1).transpose(1, 0, 2).reshape(SEQ, -1)


def _local_index(p):
    rho = np.arange(BLOCK)
    if p == 0:
        return 16 * (rho % 8) + rho // 8
    if p == 1:
        return 4 * (rho % 32) + rho // 32
    return rho


def _tile_view(x, p):
    c = x.shape[1]
    if p == 1:
        return x.reshape(4, 4, BLOCK, c)
    return x.reshape(N_SUB, BLOCK, c)


def _view_shape(c, p):
    return (4, 4, BLOCK, c) if p == 1 else (N_SUB, BLOCK, c)


def _tile_spec(p, width, col, shift=0):
    nblk = SEQ // DILATIONS[p] // BLOCK

    def blk(n):
        return jnp.clip(n + shift, 0, nblk - 1)

    if p == 0:
        return pl.BlockSpec((N_SUB, SUBLANES, width), lambda s, n: (0, blk(n), col))
    if p == 1:
        return pl.BlockSpec((4, None, 32, width), lambda s, n: (0, s, blk(n), col))
    return pl.BlockSpec((None, BLOCK, width), lambda s, n: (s, 0, col))


def _tile_grid(p):
    return ((1, 16), (4, 4), (16, 1))[p]


def _t5_bucket(n):
    max_exact = N_BUCKETS // 2
    nf = np.maximum(n, 1).astype(np.float32)
    large = max_exact + (np.log(nf / np.float32(max_exact)) / np.float32(math.log(MAX_DISTANCE / max_exact))
                         * np.float32(N_BUCKETS - max_exact)).astype(np.int32)
    large = np.minimum(large, N_BUCKETS - 1)
    return np.where(n < max_exact, n, large).astype(np.int32)


def _bucket_tables():
    tabs = np.zeros((3, 2, BLOCK, BLOCK), np.int32)
    for p, d in enumerate(DILATIONS):
        i = _local_index(p)
        diff = i[:, None] - i[None, :]
        tabs[p, 0] = np.where(diff <= 0, _t5_bucket((BLOCK + diff) * d), -1)
        tabs[p, 1] = np.where(diff >= 0, _t5_bucket(np.maximum(diff, 0) * d), -1)
    return tabs


def _bias_expand(rel_bias, buckets):
    nh = N_HEADS

    def body(rb_ref, bk_ref, o_ref):
        for w in range(2):
            bk = bk_ref[0, w]
            for h in range(nh):
                val = jnp.zeros((BLOCK, BLOCK), F32)
                for b in range(N_BUCKETS):
                    val = jnp.where(bk == b, rb_ref[b, h], val)
                o_ref[0, h, w] = jnp.where(bk < 0, NEG_INF, val)

    return _call("bias_expand", body, (3,),
                 [pl.BlockSpec(memory_space=pltpu.SMEM), pl.BlockSpec((1, 2, BLOCK, BLOCK), lambda p: (p, 0, 0, 0))],
                 [pl.BlockSpec((1, nh, 2, BLOCK, BLOCK), lambda p: (p, 0, 0, 0, 0))],
                 [jax.ShapeDtypeStruct((3, nh, 2, BLOCK, BLOCK), F32)], [rel_bias, buckets], sem=("parallel",))[0][0]


def _heads_to_lanes(cols):
    lane = lax.broadcasted_iota(I32, (BLOCK, LANES), 1)
    out = jnp.zeros((BLOCK, LANES), F32)
    for h, c in enumerate(cols):
        out = jnp.where(lane == h, c, out)
    return out


def _attn_fwd(qkv, bias, p, stages=()):
    d_a = _d_a()
    has_prev = SEQ // DILATIONS[p] // BLOCK > 1
    scale = HEAD_DIM ** -0.5
    view = _tile_view(qkv, p)

    def body(q_ref, kc_ref, kp_ref, vc_ref, vp_ref, b_ref, o_ref, l_ref):
        n = pl.program_id(1)
        q_all = q_ref[...].reshape(BLOCK, d_a).astype(BF16)
        kc_all = kc_ref[...].reshape(BLOCK, d_a).astype(BF16)
        vc_all = vc_ref[...].reshape(BLOCK, d_a).astype(BF16)
        if has_prev:
            kp_all = kp_ref[...].reshape(BLOCK, d_a).astype(BF16)
            vp_all = vp_ref[...].reshape(BLOCK, d_a).astype(BF16)
        outs, lses = [], []
        for h in range(N_HEADS):
            sl = slice(h * HEAD_DIM, (h + 1) * HEAD_DIM)
            q = q_all[:, sl]
            sc = _dot(q, kc_all[:, sl], "nt") * scale + b_ref[0, h, 1]
            m = jnp.max(sc, axis=-1, keepdims=True)
            if has_prev:
                sp = _dot(q, kp_all[:, sl], "nt") * scale + b_ref[0, h, 0]
                sp = jnp.where(n > 0, sp, NEG_INF)
                m = jnp.maximum(m, jnp.max(sp, axis=-1, keepdims=True))
                pp = jnp.exp(sp - m)
            pc = jnp.exp(sc - m)
            den = jnp.sum(pc, axis=-1, keepdims=True)
            o = _dot(pc, vc_all[:, sl], "nn")
            if has_prev:
                den = den + jnp.sum(pp, axis=-1, keepdims=True)
                o = o + _dot(pp, vp_all[:, sl], "nn")
            outs.append(o / den)
            lses.append(m + jnp.log(den))
        o_ref[...] = jnp.concatenate(outs, axis=-1).reshape(o_ref.shape)
        l_ref[...] = _heads_to_lanes(lses).reshape(l_ref.shape)

    (o, l), st = _call(
        f"attn_fwd{p}", body, _tile_grid(p),
        [_tile_spec(p, d_a, 0), _tile_spec(p, d_a, 1), _tile_spec(p, d_a, 1, -1), _tile_spec(p, d_a, 2), _tile_spec(p, d_a, 2, -1),
         pl.BlockSpec((1, N_HEADS, 2, BLOCK, BLOCK), lambda s, n: (p, 0, 0, 0, 0))],
        [_tile_spec(p, d_a, 0), _tile_spec(p, LANES, 0)],
        [jax.ShapeDtypeStruct(_view_shape(d_a, p), F32), jax.ShapeDtypeStruct(_view_shape(LANES, p), F32)],
        [view, view, view, view, view, bias], sem=("parallel", "parallel"), stages=stages)
    return (o.reshape(SEQ, d_a), l.reshape(SEQ, LANES)), st


def _attn_combine(os_, ls_, tr=256, stages=()):
    d_a = _d_a()
    tr = _pick(tr, SEQ)

    def body(o0, o1, o2, l0, l1, l2, a_ref, ab_ref, lt_ref):
        l = [l0[...], l1[...], l2[...]]
        m = jnp.maximum(jnp.maximum(l[0], l[1]), l[2])
        w = [jnp.exp(x - m) for x in l]
        tot = w[0] + w[1] + w[2]
        lt_ref[...] = m + jnp.log(tot)
        w = [x / tot for x in w]
        for h in range(N_HEADS):
            sl = slice(h * HEAD_DIM, (h + 1) * HEAD_DIM)
            acc = w[0][:, h:h + 1] * o0[:, sl] + w[1][:, h:h + 1] * o1[:, sl] + w[2][:, h:h + 1] * o2[:, sl]
            a_ref[:, sl] = acc
            ab_ref[:, sl] = acc.astype(BF16)

    return _call("attn_combine", body, (SEQ // tr,), [_row_spec(tr, d_a)] * 3 + [_row_spec(tr, LANES)] * 3,
                 [_row_spec(tr, d_a), _row_spec(tr, d_a), _row_spec(tr, LANES)],
                 [jax.ShapeDtypeStruct((SEQ, d_a), F32), jax.ShapeDtypeStruct((SEQ, d_a), BF16), jax.ShapeDtypeStruct((SEQ, LANES), F32)],
                 [*os_, *ls_], sem=("parallel",), stages=stages)


def _attn_delta(dattn, attn, tr=256):
    d_a = _d_a()
    tr = _pick(tr, SEQ)

    def body(d_ref, a_ref, o_ref):
        prod = d_ref[...] * a_ref[...]
        lane = lax.broadcasted_iota(I32, (tr, LANES), 1)
        out = jnp.zeros((tr, LANES), F32)
        for h in range(N_HEADS):
            out = jnp.where(lane == h, jnp.sum(prod[:, h * HEAD_DIM:(h + 1) * HEAD_DIM], axis=-1, keepdims=True), out)
        o_ref[...] = out

    return _call("attn_delta", body, (SEQ // tr,), [_row_spec(tr, d_a)] * 2, [_row_spec(tr, LANES)],
                 [jax.ShapeDtypeStruct((SEQ, LANES), F32)], [dattn, attn], sem=("parallel",))[0][0]


def _attn_bwd(qkv, dattn, lse, delta, bias, p, stages=()):
    d_a = _d_a()
    nblk = SEQ // DILATIONS[p] // BLOCK
    has_next = nblk > 1
    scale = HEAD_DIM ** -0.5
    qv, dov, lv, tv = (_tile_view(x, p) for x in (qkv, dattn, lse, delta))

    def body(q_ref, qn_ref, k_ref, v_ref, do_ref, don_ref, l_ref, ln_ref, t_ref, tn_ref, b_ref, dq_ref, dk_ref, dv_ref, db_ref, carry_ref):
        j = pl.program_id(1)

        @pl.when((pl.program_id(0) == 0) & (j == 0))
        def _():
            db_ref[...] = jnp.zeros_like(db_ref)

        k_all = k_ref[...].reshape(BLOCK, d_a).astype(BF16)
        v_all = v_ref[...].reshape(BLOCK, d_a).astype(BF16)

        def side(qr, dor, lr, tr_, w):
            q_all = qr[...].reshape(BLOCK, d_a).astype(BF16)
            do_all = dor[...].reshape(BLOCK, d_a).astype(BF16)
            l_all = lr[...].reshape(BLOCK, LANES)
            t_all = tr_[...].reshape(BLOCK, LANES)
            dqs, dks, dvs = [], [], []
            for h in range(N_HEADS):
                sl = slice(h * HEAD_DIM, (h + 1) * HEAD_DIM)
                s = _dot(q_all[:, sl], k_all[:, sl], "nt") * scale + b_ref[0, h, w]
                pr = jnp.exp(s - l_all[:, h:h + 1])
                dp = _dot(do_all[:, sl], v_all[:, sl], "nt")
                ds = pr * (dp - t_all[:, h:h + 1])
                db_ref[h, w] += ds
                dqs.append(_dot(ds, k_all[:, sl], "nn") * scale)
                dks.append(_dot(ds, q_all[:, sl], "tn") * scale)
                dvs.append(_dot(pr, do_all[:, sl], "tn"))
            return [jnp.concatenate(x, axis=-1) for x in (dqs, dks, dvs)]

        dq_c, dk_c, dv_c = side(q_ref, do_ref, l_ref, t_ref, 1)
        if has_next:
            dq_ref[...] = (jnp.where(j > 0, carry_ref[...], 0.0) + dq_c).reshape(dq_ref.shape)
            not_last = j < nblk - 1

            @pl.when(not_last)
            def _():
                dq_n, dk_n, dv_n = side(qn_ref, don_ref, ln_ref, tn_ref, 0)
                carry_ref[...] = dq_n
                dk_ref[...] = (dk_c + dk_n).reshape(dk_ref.shape)
                dv_ref[...] = (dv_c + dv_n).reshape(dv_ref.shape)

            @pl.when(jnp.logical_not(not_last))
            def _():
                dk_ref[...] = dk_c.reshape(dk_ref.shape)
                dv_ref[...] = dv_c.reshape(dv_ref.shape)
        else:
            dq_ref[...] = dq_c.reshape(dq_ref.shape)
            dk_ref[...] = dk_c.reshape(dk_ref.shape)
            dv_ref[...] = dv_c.reshape(dv_ref.shape)

    def big(col, shift=0):
        return _tile_spec(p, d_a, col, shift)

    def small(shift=0):
        return _tile_spec(p, LANES, 0, shift)

    (dq, dk, dv, dbias), st = _call(
        f"attn_bwd{p}", body, _tile_grid(p),
        [big(0), big(0, 1), big(1), big(2), big(0), big(0, 1), small(), small(1), small(), small(1),
         pl.BlockSpec((1, N_HEADS, 2, BLOCK, BLOCK), lambda s, n: (p, 0, 0, 0, 0))],
        [big(0), big(0), big(0), pl.BlockSpec((N_HEADS, 2, BLOCK, BLOCK), lambda s, n: (0, 0, 0, 0))],
        [jax.ShapeDtypeStruct(_view_shape(d_a, p), F32)] * 3 + [jax.ShapeDtypeStruct((N_HEADS, 2, BLOCK, BLOCK), F32)],
        [qv, qv, qv, qv, dov, dov, lv, lv, tv, tv, bias], scratch=[pltpu.VMEM((BLOCK, d_a), F32)], stages=stages)
    return (dq.reshape(SEQ, d_a), dk.reshape(SEQ, d_a), dv.reshape(SEQ, d_a), dbias), st


def _rel_bias_grad(dbias, buckets):
    nh = N_HEADS

    def body(d0, d1, d2, bk_ref, o_ref, t_ref):
        ds = (d0, d1, d2)

        def per_bucket(b, carry):
            for h in range(nh):
                acc = jnp.zeros((BLOCK, BLOCK), F32)
                for p in range(3):
                    for w in range(2):
                        acc = acc + jnp.where(bk_ref[p, w] == b, ds[p][h, w], 0.0)
                t_ref[pl.ds(b * nh + h, 1), :] = jnp.sum(acc, axis=0, keepdims=True)
            return carry

        lax.fori_loop(0, N_BUCKETS, per_bucket, 0)
        o_ref[...] = jnp.sum(t_ref[...], axis=-1, keepdims=True)

    return _call("rel_bias_grad", body, (1,), [_fix_spec((nh, 2, BLOCK, BLOCK))] * 3 + [_fix_spec((3, 2, BLOCK, BLOCK))],
                 [_fix_spec((N_BUCKETS * nh, 1))], [jax.ShapeDtypeStruct((N_BUCKETS * nh, 1), F32)], [*dbias, buckets],
                 scratch=[pltpu.VMEM((N_BUCKETS * nh, LANES), F32)])[0][0]


def _gmlp_fwd(rest, gain, bias, ws, bs, causal, stages=()):
    d_b = _d_b()

    def body(u_ref, v_ref, g_ref, b_ref, ws_ref, bs_ref, c_ref, o_ref):
        u = u_ref[...].reshape(BLOCK, d_b)
        xhat, _ = _layer_norm_stats(_gelu(v_ref[...].reshape(BLOCK, d_b)))
        vn = (xhat * g_ref[...] + b_ref[...]).astype(BF16)
        outs = []
        for g in range(N_GROUPS):
            sl = slice(g * BLOCK, (g + 1) * BLOCK)
            w = jnp.where(c_ref[...] > 0, ws_ref[g], 0.0)
            z = _dot(w, vn[:, sl], "nn") + bs_ref[:, g:g + 1]
            outs.append(_gelu(u[:, sl]) * z)
        o_ref[...] = jnp.concatenate(outs, axis=-1).reshape(o_ref.shape)

    (out,), st = _call(
        "gmlp_fwd", body, (1, SEQ // BLOCK),
        [_tile_spec(0, d_b, 0), _tile_spec(0, d_b, 1), _fix_spec((1, d_b)), _fix_spec((1, d_b)),
         _fix_spec((N_GROUPS, BLOCK, BLOCK)), _fix_spec((BLOCK, N_GROUPS)), _fix_spec((BLOCK, BLOCK))],
        [_tile_spec(0, d_b, 0)], [jax.ShapeDtypeStruct(_view_shape(d_b, 0), F32)],
        [_tile_view(rest, 0), _tile_view(rest, 0), gain, bias, ws, bs, causal], sem=("parallel", "parallel"), stages=stages)
    return out.reshape(SEQ, d_b), st


def _gmlp_bwd(rest, dgmlp, gain, bias, ws, bs, causal, stages=()):
    d_b = _d_b()
    nchunk = SEQ // BLOCK

    def body(u_ref, v_ref, dg_ref, g_ref, b_ref, ws_ref, bs_ref, c_ref, du_ref, dv_ref, dws_ref, dbs_ref, dgain_ref, dbias_ref):
        c = pl.program_id(1)

        @pl.when(c == 0)
        def _():
            dws_ref[...] = jnp.zeros_like(dws_ref)
            dbs_ref[...] = jnp.zeros_like(dbs_ref)
            dgain_ref[...] = jnp.zeros_like(dgain_ref)
            dbias_ref[...] = jnp.zeros_like(dbias_ref)

        u = u_ref[...].reshape(BLOCK, d_b)
        v = v_ref[...].reshape(BLOCK, d_b)
        dgm = dg_ref[...].reshape(BLOCK, d_b)
        xhat, rstd = _layer_norm_stats(_gelu(v))
        vn = (xhat * g_ref[...] + b_ref[...]).astype(BF16)
        lane = lax.broadcasted_iota(I32, (BLOCK, LANES), 1)
        dus, dvns = [], []
        dbs = dbs_ref[...]
        for g in range(N_GROUPS):
            sl = slice(g * BLOCK, (g + 1) * BLOCK)
            w = jnp.where(c_ref[...] > 0, ws_ref[g], 0.0).astype(BF16)
            z = _dot(w, vn[:, sl], "nn") + bs_ref[:, g:g + 1]
            dz = dgm[:, sl] * _gelu(u[:, sl])
            dus.append(dgm[:, sl] * z * _gelu_grad(u[:, sl]))
            dws_ref[g] += _dot(dz, vn[:, sl], "nt")
            dbs = dbs + jnp.where(lane == g, jnp.sum(dz, axis=-1, keepdims=True), 0.0)
            dvns.append(_dot(w, dz, "tn"))
        dbs_ref[...] = dbs
        dvn = jnp.concatenate(dvns, axis=-1)
        dgain_ref[...] += jnp.sum(dvn * xhat, axis=0, keepdims=True)
        dbias_ref[...] += jnp.sum(dvn, axis=0, keepdims=True)
        dvg = _layer_norm_bwd(dvn * g_ref[...], xhat, rstd)
        du_ref[...] = jnp.concatenate(dus, axis=-1).reshape(du_ref.shape)
        dv_ref[...] = (dvg * _gelu_grad(v)).reshape(dv_ref.shape)

        @pl.when(c == nchunk - 1)
        def _():
            for g in range(N_GROUPS):
                dws_ref[g] = jnp.where(c_ref[...] > 0, dws_ref[g], 0.0)

    (du, dv, dws, dbs, dgain, dbias), st = _call(
        "gmlp_bwd", body, (1, nchunk),
        [_tile_spec(0, d_b, 0), _tile_spec(0, d_b, 1), _tile_spec(0, d_b, 0), _fix_spec((1, d_b)), _fix_spec((1, d_b)),
         _fix_spec((N_GROUPS, BLOCK, BLOCK)), _fix_spec((BLOCK, N_GROUPS)), _fix_spec((BLOCK, BLOCK))],
        [_tile_spec(0, d_b, 0), _tile_spec(0, d_b, 0), _fix_spec((N_GROUPS, BLOCK, BLOCK)), _fix_spec((BLOCK, LANES)),
         _fix_spec((1, d_b)), _fix_spec((1, d_b))],
        [jax.ShapeDtypeStruct(_view_shape(d_b, 0), F32)] * 2
        + [jax.ShapeDtypeStruct((N_GROUPS, BLOCK, BLOCK), F32), jax.ShapeDtypeStruct((BLOCK, LANES), F32)]
        + [jax.ShapeDtypeStruct((1, d_b), F32)] * 2,
        [_tile_view(rest, 0), _tile_view(rest, 0), _tile_view(dgmlp, 0), gain, bias, ws, bs, causal], stages=stages)
    return (du.reshape(SEQ, d_b), dv.reshape(SEQ, d_b), dws, dbs, dgain, dbias), st


def _assemble_dproj(dqkv, du, dv, dga, dgb, tr=128):
    d_a, d_b, d_in = _d_a(), _d_b(), _d_in()
    tr = _pick(tr, SEQ)

    def body(*refs):
        att, (du_ref, dv_ref, dga_ref, dgb_ref, o_ref) = refs[:9], refs[9:]
        for i in range(3):
            o_ref[:, i * d_a:(i + 1) * d_a] = (att[3 * i][...] + att[3 * i + 1][...] + att[3 * i + 2][...]).astype(BF16)
        o_ref[:, 3 * d_a:3 * d_a + d_b] = du_ref[...].astype(BF16)
        o_ref[:, 3 * d_a + d_b:3 * d_a + 2 * d_b] = dv_ref[...].astype(BF16)
        o_ref[:, 3 * d_a + 2 * d_b:3 * d_a + 2 * d_b + D_MODEL] = dga_ref[...]
        o_ref[:, 3 * d_a + 2 * d_b + D_MODEL:] = dgb_ref[...]

    return _call("assemble_dproj", body, (SEQ // tr,), [_row_spec(tr, d_a)] * 9 + [_row_spec(tr, d_b)] * 2 + [_row_spec(tr, D_MODEL)] * 2,
                 [_row_spec(tr, d_in)], [jax.ShapeDtypeStruct((SEQ, d_in), BF16)], [*dqkv, du, dv, dga, dgb], sem=("parallel",))[0][0]


def _dw(name, a, b, kind, core, mine, add=None, tn=1152, stages=()):
    s, m = a.shape
    n = b.shape[1]
    rs, cs = (m, n // N_DEV) if kind == "col" else (m // N_DEV, n)
    tn = _pick(tn if kind == "col" else 512, cs)
    nj = cs // tn

    def shard(q, c_ref):
        return 2 * q + (c_ref[0] if mine else 1 - c_ref[0])

    if kind == "col":
        a_spec = pl.BlockSpec((s, m), lambda q, j, c_ref: (0, 0))
        b_spec = pl.BlockSpec((s, tn), lambda q, j, c_ref: (0, shard(q, c_ref) * nj + j))
    else:
        a_spec = pl.BlockSpec((s, rs), lambda q, j, c_ref: (0, shard(q, c_ref)))
        b_spec = pl.BlockSpec((s, tn), lambda q, j, c_ref: (0, j))
    o_spec = pl.BlockSpec((None, rs, tn), lambda q, j, c_ref: (q, 0, j))

    def body(a_ref, b_ref, *rest):
        acc = _dot(a_ref[...], b_ref[...], "tn")
        if add is not None:
            acc = acc + rest[0][...].astype(F32)
        rest[-1][...] = acc.astype(BF16)

    (out,), st = _call(name, body, (N_CHIPS, nj), [a_spec, b_spec] + ([o_spec] if add is not None else []), [o_spec],
                       [jax.ShapeDtypeStruct((N_CHIPS, rs, cs), BF16)], [a, b] + ([add] if add is not None else []),
                       sem=("parallel", "parallel"), stages=stages, prefetch=core)
    return out, st


def _adamw(w, g, m, v):
    m = ADAM_B1 * m + (1.0 - ADAM_B1) * g
    v = ADAM_B2 * v + (1.0 - ADAM_B2) * (g * g)
    m_hat = m / (1.0 - ADAM_B1 ** ADAM_STEP)
    v_hat = v / (1.0 - ADAM_B2 ** ADAM_STEP)
    delta = -ADAM_LR * (m_hat / (jnp.sqrt(v_hat) + ADAM_EPS) + ADAM_WD * w)
    return delta, m, v


def _adam_shard(name, chip_sums, w, m, v, tr=256, stages=()):
    rs, cs = w.shape
    tr = _pick(tr, rs)

    def body(s_ref, w_ref, m_ref, v_ref, g_ref, d_ref, nm_ref, nv_ref):
        g = s_ref[0].astype(F32)
        for q in range(1, N_CHIPS):
            g = g + s_ref[q].astype(F32)
        d, nm, nv = _adamw(w_ref[...], g, m_ref[...], v_ref[...])
        g_ref[...], d_ref[...], nm_ref[...], nv_ref[...] = g, d, nm, nv

    spec = _row_spec(tr, cs)
    return _call(name, body, (rs // tr,), [pl.BlockSpec((N_CHIPS, tr, cs), lambda i: (0, i, 0)), spec, spec, spec], [spec] * 4,
                 [jax.ShapeDtypeStruct((rs, cs), F32)] * 4, [chip_sums, w, m, v], sem=("parallel",), stages=stages)


def _adam_small(parts, w, m, v):
    rows = w.shape[0]

    def body(p_ref, w_ref, m_ref, v_ref, g_ref, d_ref, nm_ref, nv_ref):
        g = p_ref[0]
        for j in range(1, N_DEV):
            g = g + p_ref[j]
        d, nm, nv = _adamw(w_ref[...], g, m_ref[...], v_ref[...])
        g_ref[...], d_ref[...], nm_ref[...], nv_ref[...] = g, d, nm, nv

    spec = _fix_spec((rows, LANES))
    return _call("adam_small", body, (1,), [_fix_spec((N_DEV, rows, LANES)), spec, spec, spec], [spec] * 4,
                 [jax.ShapeDtypeStruct((rows, LANES), F32)] * 4, [parts, w, m, v])[0]


def _small_sizes():
    d_b = _d_b()
    return (("loss", 1), ("rel_bias", N_BUCKETS * N_HEADS), ("ln_v_gain", d_b), ("ln_v_bias", d_b),
            ("w_spatial", N_GROUPS * BLOCK * BLOCK), ("b_spatial", N_GROUPS * BLOCK), ("ln1_gain", D_MODEL), ("ln1_bias", D_MODEL),
            ("b_ff1", D_FF), ("b_ff2", D_MODEL), ("ln2_gain", D_MODEL), ("ln2_bias", D_MODEL))


def _pack(vals):
    pieces = []
    for name, size in _small_sizes():
        flat = vals[name].reshape(-1).astype(F32)
        padded = -(-size // (SUBLANES * LANES)) * SUBLANES * LANES
        pieces.append(jnp.pad(flat, (0, padded - size)).reshape(-1, LANES))
    return jnp.concatenate(pieces, axis=0)


def _unpack(buf):
    out, row = {}, 0
    for name, size in _small_sizes():
        rows = -(-size // (SUBLANES * LANES)) * SUBLANES
        out[name] = buf[row:row + rows].reshape(-1)[:size]
        row += rows
    return out


def kernel(x, w_in, rel_bias, ln_v_gain, ln_v_bias, w_spatial, b_spatial, w_proj_a, w_proj_b, w_out, ln1_gain, ln1_bias, w_ff1, b_ff1, w_ff2, b_ff2, ln2_gain, ln2_bias, loss_target, m_w_in, m_rel_bias, m_ln_v_gain, m_ln_v_bias, m_w_spatial, m_b_spatial, m_w_proj_a, m_w_proj_b, m_w_out, m_ln1_gain, m_ln1_bias, m_w_ff1, m_b_ff1, m_w_ff2, m_b_ff2, m_ln2_gain, m_ln2_bias, v_w_in, v_rel_bias, v_ln_v_gain, v_ln_v_bias, v_w_spatial, v_b_spatial, v_w_proj_a, v_w_proj_b, v_w_out, v_ln1_gain, v_ln1_bias, v_w_ff1, v_b_ff1, v_w_ff2, v_b_ff2, v_ln2_gain, v_ln2_bias):
    d_a, d_b, d_in = _d_a(), _d_b(), _d_in()
    weights = dict(w_in=w_in, rel_bias=rel_bias, ln_v_gain=ln_v_gain, ln_v_bias=ln_v_bias, w_spatial=w_spatial, b_spatial=b_spatial,
                   w_proj_a=w_proj_a, w_proj_b=w_proj_b, w_out=w_out, ln1_gain=ln1_gain, ln1_bias=ln1_bias, w_ff1=w_ff1, b_ff1=b_ff1,
                   w_ff2=w_ff2, b_ff2=b_ff2, ln2_gain=ln2_gain, ln2_bias=ln2_bias)
    mom1 = dict(w_in=m_w_in, rel_bias=m_rel_bias, ln_v_gain=m_ln_v_gain, ln_v_bias=m_ln_v_bias, w_spatial=m_w_spatial,
                b_spatial=m_b_spatial, w_proj_a=m_w_proj_a, w_proj_b=m_w_proj_b, w_out=m_w_out, ln1_gain=m_ln1_gain,
                ln1_bias=m_ln1_bias, w_ff1=m_w_ff1, b_ff1=m_b_ff1, w_ff2=m_w_ff2, b_ff2=m_b_ff2, ln2_gain=m_ln2_gain, ln2_bias=m_ln2_bias)
    mom2 = dict(w_in=v_w_in, rel_bias=v_rel_bias, ln_v_gain=v_ln_v_gain, ln_v_bias=v_ln_v_bias, w_spatial=v_w_spatial,
                b_spatial=v_b_spatial, w_proj_a=v_w_proj_a, w_proj_b=v_w_proj_b, w_out=v_w_out, ln1_gain=v_ln1_gain,
                ln1_bias=v_ln1_bias, w_ff1=v_w_ff1, b_ff1=v_b_ff1, w_ff2=v_w_ff2, b_ff2=v_b_ff2, ln2_gain=v_ln2_gain, ln2_bias=v_ln2_bias)

    shard = {n: _cast_bf16(f"cast_{n}", weights[n][0]) for n in KINDS}
    full, sent = {}, {n: (0, 0, 0) for n in KINDS}
    for n, kind in KINDS.items():
        r, c = shard[n].shape
        full[n] = lax.empty((r, c * N_DEV) if kind == "col" else (r * N_DEV, c), BF16)

    def keep(table, n):
        def store(outs):
            table[n] = outs[0]
        return store

    def gathering(**new):
        stages = []
        for n in KINDS:
            out, relayed, passed = sent[n]
            units = new.get(n, 0)
            if units or relayed < out or passed < relayed:
                st = _gather_stage(shard[n], full[n], KINDS[n], (out, units) if units else None,
                                   (relayed, out - relayed) if relayed < out else None, (passed, relayed - passed) if passed < relayed else None)
                st.store = keep(full, n)
                sent[n] = (out + units, out, relayed)
                stages.append(st)
        return stages

    def settle(stages, outs):
        for st, o in zip(stages, outs):
            st.store(o)

    def alone(name, stages):
        settle(stages, _comm_only(name, stages))

    def here(n):
        assert sent[n] == (16, 16, 16), (n, sent[n])
        return full[n]

    alone("gather_w_in_near", gathering(w_in=16))
    alone("gather_w_in_relay", gathering())
    alone("gather_w_in_sibling", gathering())

    xs = _to_perm(x[0])
    target = _to_perm(loss_target[0])
    xb = _cast_bf16("cast_x", xs)
    g8 = BLOCK // N_SUB
    ws_t = w_spatial[0].reshape(N_GROUPS, g8, N_SUB, g8, N_SUB).transpose(0, 2, 1, 4, 3).reshape(N_GROUPS, BLOCK, BLOCK)
    bs_t = b_spatial[0].reshape(N_GROUPS, g8, N_SUB).transpose(2, 1, 0).reshape(BLOCK, N_GROUPS)
    idx = _local_index(0)
    causal = jnp.asarray((idx[:, None] >= idx[None, :]).astype(np.float32))
    buckets = jnp.asarray(_bucket_tables())
    bias = _bias_expand(rel_bias, buckets)

    hosted = gathering(w_proj_a=16, w_proj_b=16, w_ff1=1)
    (qkv,), st = _matmul("proj_qkv", xb, here("w_in"), "nn", [F32], n=3 * d_a, stages=hosted)
    settle(hosted, st)
    hosted = gathering(w_out=16, w_ff1=4)
    (rest,), st = _matmul("proj_rest", xb, here("w_in"), "nn", [F32], b_off=3 * d_a, n=d_in - 3 * d_a, stages=hosted)
    settle(hosted, st)
    fwd = []
    for p in range(3):
        hosted = gathering(w_ff1=(3, 5, 3)[p])
        res, st = _attn_fwd(qkv, bias, p, stages=hosted)
        settle(hosted, st)
        fwd.append(res)
    hosted = gathering(w_ff2=1)
    (attn, attn_b, lse), st = _attn_combine([o for o, _ in fwd], [l for _, l in fwd], stages=hosted)
    settle(hosted, st)
    hosted = gathering(w_ff2=2)
    gmlp, st = _gmlp_fwd(rest, ln_v_gain, ln_v_bias, ws_t, bs_t, causal, stages=hosted)
    settle(hosted, st)
    hosted = gathering(w_ff2=3)
    (ya,), st = _matmul("proj_a", attn_b, here("w_proj_a"), "nn", [F32], stages=hosted)
    settle(hosted, st)
    gate_a, gate_b = 2 * d_b, 2 * d_b + D_MODEL

    def merge(acc, ya_, ga, gb):
        return acc, _sigmoid(ga) * ya_ + _sigmoid(gb) * acc

    hosted = gathering(w_ff2=5)
    (yb, merged), st = _matmul("proj_b_merge", gmlp, here("w_proj_b"), "nn", [F32, BF16], merge,
                               [(ya, "mn", 0), (rest, "mn", gate_a), (rest, "mn", gate_b)], tn=256, stages=hosted)
    settle(hosted, st)
    hosted = gathering(w_ff2=3)
    (pre1,), st = _matmul("out_proj", merged, here("w_out"), "nn", [F32], lambda acc, x_: (ALPHA * x_ + acc,), [(xs, "mn", 0)], stages=hosted)
    settle(hosted, st)
    hosted = gathering(w_ff2=2)
    (xhat1, rstd1, h1b), st = _ln1_fwd(pre1, ln1_gain, ln1_bias, stages=hosted)
    settle(hosted, st)

    def relu2(acc, b_):
        r = jnp.maximum(acc + b_, 0.0)
        return r, r * r

    hosted = gathering()
    (relu, fb), st = _matmul("ff1", h1b, here("w_ff1"), "nn", [F32, BF16], relu2, [(b_ff1, "row", 0)], stages=hosted)
    settle(hosted, st)
    alone("gather_w_ff2_sibling", gathering())
    (ff,), _ = _matmul("ff2", fb, here("w_ff2"), "nn", [F32], lambda acc, b_: (acc + b_,), [(b_ff2, "row", 0)], tn=1024, tk=1024)

    core = lax.axis_index("c").astype(I32).reshape(1)
    factors, theirs, sib, pair, chips, transit, reduced = {}, {}, {}, {}, {}, {}, {}

    def grad_for_sibling(n, a, b, stages=()):
        factors[n] = (a, b)
        theirs[n], outs = _dw(f"dw_{n}_sibling", a, b, KINDS[n], core, False, stages=stages)
        settle(stages, outs)

    def to_sibling(n):
        st = _to_sibling_stage(theirs[n])
        st.store = keep(sib, n)
        return st

    def grad_own(n, stages=()):
        pair[n], outs = _dw(f"dw_{n}_own", *factors[n], KINDS[n], core, True, add=sib[n], stages=stages)
        settle(stages, outs)
        chips[n] = lax.empty(pair[n].shape, BF16)
        transit[n] = lax.empty((2, *pair[n].shape[1:]), BF16)
        reduced[n] = (0, 0)

    def reducing(**new):
        stages = []
        for n in list(reduced):
            out, relayed = reduced[n]
            units = new.get(n, 0)
            if units or relayed < out:
                st = _reduce_stage(pair[n], chips[n], transit[n], (out, units) if units else None,
                                   (relayed, out - relayed) if relayed < out else None)

                def store(outs, n=n):
                    chips[n], transit[n] = outs

                st.store = store
                reduced[n] = (out + units, out)
                stages.append(st)
        return stages

    def summed(n):
        assert reduced[n] == (16, 16), (n, reduced[n])
        return chips[n]

    dpre2, dpre2b, g_ln2_gain, g_ln2_bias, g_b_ff2, loss_part = _ln2_loss_bwd(ff, xhat1, ln1_gain, ln1_bias, ln2_gain, ln2_bias, target)
    grad_for_sibling("w_ff2", fb, dpre2b)

    def relu2_bwd(acc, r):
        da = acc * (2.0 * r)
        return da, da

    hosted = [to_sibling("w_ff2")]
    (dab, g_b_ff1), st = _matmul("d_ff1", dpre2b, here("w_ff2"), "nt", [BF16], relu2_bwd, [(relu, "mn", 0)], colsums=(1,), stages=hosted)
    settle(hosted, st)
    grad_own("w_ff2")
    grad_for_sibling("w_ff1", h1b, dab, reducing(w_ff2=6))
    hosted = reducing(w_ff2=10) + [to_sibling("w_ff1")]
    (dh1,), st = _matmul("d_h1", dab, here("w_ff1"), "nt", [F32], lambda acc, d_: (acc + ALPHA * d_,), [(dpre2, "mn", 0)], stages=hosted)
    settle(hosted, st)
    grad_own("w_ff1", reducing())
    hosted = reducing(w_ff1=4)
    (dpre1, dpre1b, g_ln1_gain, g_ln1_bias), st = _ln1_bwd(dh1, xhat1, rstd1, ln1_gain, stages=hosted)
    settle(hosted, st)
    grad_for_sibling("w_out", merged, dpre1b, reducing(w_ff1=2))

    def merge_bwd(acc, ga, gb, ya_, yb_):
        sa, sb = _sigmoid(ga), _sigmoid(gb)
        return acc * sa, acc * sb, acc * ya_ * (sa * (1.0 - sa)), acc * yb_ * (sb * (1.0 - sb))

    hosted = reducing(w_ff1=8) + [to_sibling("w_out")]
    (dya, dyb, dga, dgb), st = _matmul("d_merge", dpre1b, here("w_out"), "nt", [BF16] * 4, merge_bwd,
                                       [(rest, "mn", gate_a), (rest, "mn", gate_b), (ya, "mn", 0), (yb, "mn", 0)], tn=256, stages=hosted)
    settle(hosted, st)
    grad_own("w_out", reducing())
    grad_for_sibling("w_proj_a", attn_b, dya, reducing(w_ff1=2))
    grad_for_sibling("w_proj_b", gmlp, dyb, reducing())
    hosted = [to_sibling("w_proj_a"), to_sibling("w_proj_b")]
    (dattn,), st = _matmul("d_attn", dya, here("w_proj_a"), "nt", [F32], stages=hosted)
    settle(hosted, st)
    grad_own("w_proj_a")
    grad_own("w_proj_b")
    hosted = reducing(w_out=16)
    (dgmlp,), st = _matmul("d_gmlp", dyb, here("w_proj_b"), "nt", [F32], stages=hosted)
    settle(hosted, st)
    hosted = reducing(w_proj_a=16)
    (du, dvb, dws_t, dbs_t, g_lnv_gain, g_lnv_bias), st = _gmlp_bwd(rest, dgmlp, ln_v_gain, ln_v_bias, ws_t, bs_t, causal, stages=hosted)
    settle(hosted, st)
    delta = _attn_delta(dattn, attn)
    bwd = []
    for p in range(3):
        hosted = reducing(w_proj_b=16) if p == 0 else reducing()
        res, st = _attn_bwd(qkv, dattn, lse, delta, bias, p, stages=hosted)
        settle(hosted, st)
        bwd.append(res)
    g_rel_bias = _rel_bias_grad([b[3] for b in bwd], buckets)
    dproj = _assemble_dproj([b[i] for i in range(3) for b in bwd], du, dvb, dga, dgb)

    g_w_spatial = dws_t.reshape(N_GROUPS, N_SUB, g8, N_SUB, g8).transpose(0, 2, 1, 4, 3)
    g_b_spatial = dbs_t[:, :N_GROUPS].reshape(N_SUB, g8, N_GROUPS).transpose(2, 1, 0)
    part = _pack(dict(loss=loss_part, rel_bias=g_rel_bias, ln_v_gain=g_lnv_gain, ln_v_bias=g_lnv_bias, w_spatial=g_w_spatial,
                      b_spatial=g_b_spatial, ln1_gain=g_ln1_gain, ln1_bias=g_ln1_bias, b_ff1=g_b_ff1, b_ff2=g_b_ff2,
                      ln2_gain=g_ln2_gain, ln2_bias=g_ln2_bias))
    small = _small_stage(part)
    small.store = keep(sib, "small")
    grad_for_sibling("w_in", xb, dproj, [small])
    parts = sib["small"]
    half = SEQ // 2

    def add_residual(acc, d_):
        return (acc + ALPHA * d_,)

    hosted = [to_sibling("w_in")]
    (dx0,), st = _matmul("d_x0", dproj, here("w_in"), "nt", [F32], add_residual, [(dpre1, "mn", 0)], tm=1024, tn=1024, tk=3072, m=half,
                         stages=hosted)
    settle(hosted, st)
    grad_own("w_in")
    hosted = reducing(w_in=8)
    (dx1,), st = _matmul("d_x1", dproj, here("w_in"), "nt", [F32], add_residual, [(dpre1, "mn", 0)], tm=1024, tn=1024, tk=3072, m_off=half, m=half,
                         stages=hosted)
    settle(hosted, st)
    grad_x = _from_perm(jnp.concatenate([dx0, dx1], axis=0))[None]

    out_g, out_d, out_m, out_v = {}, {}, {}, {}
    for n, units in (("w_ff2", 4), ("w_ff1", 4), ("w_out", 0), ("w_proj_a", 0), ("w_proj_b", 0), ("w_in", 0)):
        hosted = reducing(w_in=units) if n != "w_in" else []
        (g, d, nm, nv), st = _adam_shard(f"adam_{n}", summed(n), weights[n][0], mom1[n][0], mom2[n][0], stages=hosted)
        settle(hosted, st)
        out_g[n], out_d[n], out_m[n], out_v[n] = g[None], d[None], nm[None], nv[None]

    zero = jnp.zeros((1,), F32)
    sg, sd, sm, sv = (_unpack(b) for b in _adam_small(
        parts, _pack({**weights, "loss": zero}), _pack({**mom1, "loss": zero}), _pack({**mom2, "loss": zero})))
    for n in WEIGHT_ORDER:
        if n not in KINDS:
            shape = weights[n].shape
            out_g[n], out_d[n], out_m[n], out_v[n] = (t[n].reshape(shape) for t in (sg, sd, sm, sv))
    loss = sg["loss"].reshape(())
    return (loss, grad_x, *[out_g[n] for n in WEIGHT_ORDER], *[out_d[n] for n in WEIGHT_ORDER],
            *[out_m[n] for n in WEIGHT_ORDER], *[out_v[n] for n in WEIGHT_ORDER])
```

```python
import math

import jax
import jax.numpy as jnp
import numpy as np
from jax import lax
from jax.experimental import pallas as pl
from jax.experimental.pallas import tpu as pltpu

F32 = jnp.float32
BF16 = jnp.bfloat16
I32 = jnp.int32

SEQ = 2048
D_MODEL = 2048
HEAD_DIM = 128
N_HEADS = 8
N_GROUPS = 8
D_FF = 4 * D_MODEL
BLOCK = 128
DILATIONS = (1, 4, 16)
N_BUCKETS = 32
MAX_DISTANCE = 2048
ALPHA = 2.0 ** 0.25
LN_EPS = 1e-5
NEG_INF = -1e30
N_DEV = 8
N_CHIPS = 4
N_SUB = 16
ADAM_LR, ADAM_B1, ADAM_B2, ADAM_EPS, ADAM_WD, ADAM_STEP = 0.001, 0.9, 0.999, 1e-08, 0.01, 10
LANES = 128
SUBLANES = 8
VMEM_LIMIT = 56 * 1024 * 1024
MESH = pl.DeviceIdType.MESH
ANY = pl.BlockSpec(memory_space=pl.ANY)
WEIGHT_ORDER = ("w_in", "rel_bias", "ln_v_gain", "ln_v_bias", "w_spatial", "b_spatial", "w_proj_a", "w_proj_b", "w_out",
                "ln1_gain", "ln1_bias", "w_ff1", "b_ff1", "w_ff2", "b_ff2", "ln2_gain", "ln2_bias")
KINDS = {"w_in": "col", "w_proj_a": "col", "w_proj_b": "col", "w_out": "row", "w_ff1": "col", "w_ff2": "row"}


def _d_a():
    return N_HEADS * HEAD_DIM


def _d_b():
    return N_GROUPS * BLOCK


def _d_in():
    return 3 * _d_a() + 2 * _d_b() + 2 * D_MODEL


def _pick(t, n, *others):
    if n <= t and all(o % n == 0 for o in others):
        return n
    for c in range(min(t, n) // LANES * LANES, 0, -LANES):
        if n % c == 0 and all(o % c == 0 for o in others):
            return c
    raise ValueError((t, n, others))


class _Stage:
    def __init__(self, ins, outs, alias, sems, start, finish):
        self.ins, self.outs, self.alias, self.sems, self.start, self.finish = ins, outs, alias, sems, start, finish


def _call(name, body, grid, in_specs, out_specs, out_shape, operands, scratch=(), sem=None, stages=(), sequential=False, prefetch=None):
    n_in, n_out, n_sc = len(in_specs), len(out_specs), len(scratch)
    st_in = [len(s.ins) for s in stages]
    st_out = [len(s.outs) for s in stages]
    st_sem = [len(s.sems) for s in stages]
    n_pre = 0 if prefetch is None else 1
    aliases, ioff, ooff = {}, n_in + n_pre, n_out
    for s, ni, no in zip(stages, st_in, st_out):
        for i, o in s.alias.items():
            aliases[ioff + i] = ooff + o
        ioff, ooff = ioff + ni, ooff + no

    def split(refs, counts):
        out, at = [], 0
        for c in counts:
            out.append(refs[at:at + c])
            at += c
        return out

    def wrapped(*refs):
        ins, sins, outs, souts, sc, ssems = split(refs[n_pre:], [n_in, sum(st_in), n_out, sum(st_out), n_sc, sum(st_sem)])
        parts = list(zip(stages, split(sins, st_in), split(souts, st_out), split(ssems, st_sem)))
        if sequential:
            for s, a, b, c in parts:
                s.start(a, b, c)
                s.finish(a, b, c)
            return
        if parts:
            first = _all_of([pl.program_id(i) == 0 for i in range(len(grid))])
            last = _all_of([pl.program_id(i) == g - 1 for i, g in enumerate(grid)])

            @pl.when(first)
            def _():
                for s, a, b, c in parts:
                    s.start(a, b, c)

        body(*ins, *outs, *sc)
        if parts:
            @pl.when(last)
            def _():
                for s, a, b, c in parts:
                    s.finish(a, b, c)

    if stages or sem is None:
        sem = ("arbitrary",) * len(grid)
    specs = dict(grid=grid, in_specs=list(in_specs) + [ANY] * sum(st_in), out_specs=list(out_specs) + [ANY] * sum(st_out),
                 scratch_shapes=list(scratch) + [x for s in stages for x in s.sems])
    if prefetch is not None:
        specs = dict(grid_spec=pltpu.PrefetchScalarGridSpec(num_scalar_prefetch=1, **specs))
    res = pl.pallas_call(
        wrapped, name=name, out_shape=list(out_shape) + [o for s in stages for o in s.outs], input_output_aliases=aliases,
        compiler_params=pltpu.CompilerParams(dimension_semantics=sem, vmem_limit_bytes=VMEM_LIMIT), **specs,
    )(*([prefetch] if n_pre else []), *operands, *[a for s in stages for a in s.ins])
    res = list(res)
    return res[:n_out], split(res[n_out:], st_out)


def _all_of(conds):
    out = conds[0]
    for c in conds[1:]:
        out = out & c
    return out


def _coords():
    return lax.axis_index("x"), lax.axis_index("y"), lax.axis_index("c")


def _other_chips(x, y):
    return ((1 - x, y), (x, 1 - y), (1 - x, 1 - y))


def _lin(dev):
    return 4 * dev[0] + 2 * dev[1] + dev[2]


def _piece(total, lo, n, units=16):
    assert total % units == 0
    return lo * (total // units), n * (total // units)


FLOWS = 4


def _split(lo, cnt):
    k = next(k for k in (FLOWS, 2, 1) if cnt % (2 * SUBLANES * k) == 0)
    return [(lo + i * (cnt // k), cnt // k) for i in range(k)]


def _remote(src, dst, send, recv, to):
    return pltpu.make_async_remote_copy(src_ref=src, dst_ref=dst, send_sem=send, recv_sem=recv, device_id=to, device_id_type=MESH)


def _placer(kind, n, lo, cnt):
    def place(ref, dev):
        if kind == "col":
            return ref.at[pl.ds(lo, cnt), pl.ds(pl.multiple_of(_lin(dev) * n, LANES), n)]
        return ref.at[pl.ds(pl.multiple_of(_lin(dev) * n + lo, 2 * SUBLANES), cnt), :]
    return place


def _spread_stage(shard, full, kind, piece=(0, 16), home=False):
    n = shard.shape[1] if kind == "col" else shard.shape[0]
    lo, cnt = _piece(shard.shape[0], *piece)
    parts = _split(lo, cnt)
    npeers = 1 if home else 2

    def copies(ins, outs, sems):
        send, recv, local = sems
        x, y, c = _coords()
        me = (x, y, c)
        peers = [(x, y, 1 - c)] if home else [(1 - x, y, c), (x, 1 - y, c)]
        own = pltpu.make_async_copy(ins[0].at[pl.ds(lo, cnt), :], _placer(kind, n, lo, cnt)(outs[0], me), local)
        out, arrive = [], []
        for k, t in enumerate(peers):
            for i, (plo, pcnt) in enumerate(parts):
                src, place = ins[0].at[pl.ds(plo, pcnt), :], _placer(kind, n, plo, pcnt)
                out.append(_remote(src, place(outs[0], me), send.at[i, k], recv.at[i, k], t))
                arrive.append(_remote(src, place(outs[0], t), send.at[i, k], recv.at[i, k], t))
        return own, out, arrive

    def start(ins, outs, sems):
        own, out, _ = copies(ins, outs, sems)
        for cp in out:
            cp.start()
        if home:
            own.start()

    def finish(ins, outs, sems):
        own, out, arrive = copies(ins, outs, sems)
        for cp in arrive:
            cp.wait_recv()
        for cp in out:
            cp.wait_send()
        if home:
            own.wait()

    return _Stage([shard, full], [jax.ShapeDtypeStruct(full.shape, full.dtype)], {1: 0},
                  [pltpu.SemaphoreType.DMA((len(parts), npeers)), pltpu.SemaphoreType.DMA((len(parts), npeers)), pltpu.SemaphoreType.DMA],
                  start, finish)


def _relay_stage(full, kind, piece=(0, 16)):
    n = (full.shape[1] if kind == "col" else full.shape[0]) // N_DEV
    lo, cnt = _piece(full.shape[0] if kind == "col" else n, *piece)
    half = cnt // 2
    assert half % (2 * SUBLANES) == 0, (cnt, kind)
    tops, bottoms = _split(lo, half), _split(lo + half, half)

    def copies(ins, outs, sems):
        send, recv = sems
        x, y, c = _coords()
        xn, yn, dg = (1 - x, y, c), (x, 1 - y, c), (1 - x, 1 - y, c)
        out, arrive, k = [], [], 0
        for came_from, to, parts in ((yn, xn, tops), (xn, yn, bottoms)):
            for plo, pcnt in parts:
                place = _placer(kind, n, plo, pcnt)
                out.append(_remote(place(outs[0], came_from), place(outs[0], came_from), send.at[k], recv.at[k], to))
                arrive.append(_remote(place(outs[0], dg), place(outs[0], dg), send.at[k], recv.at[k], to))
                k += 1
        return out, arrive

    def start(ins, outs, sems):
        for cp in copies(ins, outs, sems)[0]:
            cp.start()

    def finish(ins, outs, sems):
        out, arrive = copies(ins, outs, sems)
        for cp in arrive:
            cp.wait_recv()
        for cp in out:
            cp.wait_send()

    return _Stage([full], [jax.ShapeDtypeStruct(full.shape, full.dtype)], {0: 0},
                  [pltpu.SemaphoreType.DMA((len(tops) + len(bottoms),)), pltpu.SemaphoreType.DMA((len(tops) + len(bottoms),))], start, finish)


def _forward_stage(full, kind, piece=(0, 16)):
    n = (full.shape[1] if kind == "col" else full.shape[0]) // N_DEV
    lo, cnt = _piece(full.shape[0] if kind == "col" else n, *piece)
    place = _placer(kind, n, lo, cnt)

    def copies(ins, outs, sems):
        send, recv = sems
        x, y, c = _coords()
        chips = _other_chips(x, y)
        out = [_remote(place(outs[0], (*chip, c)), place(outs[0], (*chip, c)), send.at[k], recv.at[k], (x, y, 1 - c)) for k, chip in enumerate(chips)]
        arrive = [_remote(place(outs[0], (*chip, 1 - c)), place(outs[0], (*chip, 1 - c)), send.at[k], recv.at[k], (x, y, 1 - c))
                  for k, chip in enumerate(chips)]
        return out, arrive

    def start(ins, outs, sems):
        for cp in copies(ins, outs, sems)[0]:
            cp.start()

    def finish(ins, outs, sems):
        out, arrive = copies(ins, outs, sems)
        for cp in arrive:
            cp.wait_recv()
        for cp in out:
            cp.wait_send()

    return _Stage([full], [jax.ShapeDtypeStruct(full.shape, full.dtype)], {0: 0},
                  [pltpu.SemaphoreType.DMA((3,)), pltpu.SemaphoreType.DMA((3,))], start, finish)


def _to_sibling_stage(theirs):
    def copies(ins, outs, sems):
        send, recv = sems
        x, y, c = _coords()
        return [_remote(ins[0].at[q], outs[0].at[q], send.at[q], recv.at[q], (x, y, 1 - c)) for q in range(N_CHIPS)]

    def start(ins, outs, sems):
        for cp in copies(ins, outs, sems):
            cp.start()

    def finish(ins, outs, sems):
        for cp in copies(ins, outs, sems):
            cp.wait()

    return _Stage([theirs], [jax.ShapeDtypeStruct(theirs.shape, BF16)], {},
                  [pltpu.SemaphoreType.DMA((N_CHIPS,)), pltpu.SemaphoreType.DMA((N_CHIPS,))], start, finish)


def _to_chips_stage(pair, dst, transit, piece=(0, 16)):
    lo, cnt = _piece(pair.shape[1], *piece)
    half = cnt // 2
    assert half % (2 * SUBLANES) == 0, cnt
    whole, tops, bottoms = _split(lo, cnt), _split(lo, half), _split(lo + half, half)
    nsem = 2 * len(whole) + len(tops) + len(bottoms)

    def copies(ins, outs, sems):
        send, recv, local = sems
        x, y, c = _coords()
        mine, qx, qy, qd = 2 * x + y, 2 * (1 - x) + y, 2 * x + 1 - y, 2 * (1 - x) + 1 - y
        xn, yn = (1 - x, y, c), (x, 1 - y, c)
        pair_ref, (dst_ref, transit_ref) = ins[0], outs
        own = pltpu.make_async_copy(pair_ref.at[mine, pl.ds(lo, cnt), :], dst_ref.at[mine, pl.ds(lo, cnt), :], local)
        out, arrive, k = [], [], 0
        for q, to, parts in ((qx, xn, whole), (qy, yn, whole)):
            for plo, pcnt in parts:
                rows = pl.ds(plo, pcnt)
                out.append(_remote(pair_ref.at[q, rows, :], dst_ref.at[mine, rows, :], send.at[k], recv.at[k], to))
                arrive.append(_remote(pair_ref.at[q, rows, :], dst_ref.at[q, rows, :], send.at[k], recv.at[k], to))
                k += 1
        for slot, to, parts in ((0, xn, tops), (1, yn, bottoms)):
            for plo, pcnt in parts:
                rows = pl.ds(plo, pcnt)
                out.append(_remote(pair_ref.at[qd, rows, :], transit_ref.at[slot, rows, :], send.at[k], recv.at[k], to))
                arrive.append(_remote(pair_ref.at[qd, rows, :], transit_ref.at[slot, rows, :], send.at[k], recv.at[k], to))
                k += 1
        return own, out, arrive

    def start(ins, outs, sems):
        own, out, _ = copies(ins, outs, sems)
        own.start()
        for cp in out:
            cp.start()

    def finish(ins, outs, sems):
        own, out, arrive = copies(ins, outs, sems)
        for cp in arrive:
            cp.wait_recv()
        for cp in out:
            cp.wait_send()
        own.wait()

    return _Stage([pair, dst, transit], [jax.ShapeDtypeStruct(dst.shape, dst.dtype), jax.ShapeDtypeStruct(transit.shape, transit.dtype)],
                  {1: 0, 2: 1}, [pltpu.SemaphoreType.DMA((nsem,)), pltpu.SemaphoreType.DMA((nsem,)), pltpu.SemaphoreType.DMA], start, finish)


def _to_chips_relay_stage(dst, transit, piece=(0, 16)):
    lo, cnt = _piece(dst.shape[1], *piece)
    half = cnt // 2
    tops, bottoms = _split(lo, half), _split(lo + half, half)

    def copies(ins, outs, sems):
        send, recv = sems
        x, y, c = _coords()
        qx, qy, qd = 2 * (1 - x) + y, 2 * x + 1 - y, 2 * (1 - x) + 1 - y
        xn, yn = (1 - x, y, c), (x, 1 - y, c)
        transit_ref, dst_ref = ins[1], outs[0]
        out, arrive, k = [], [], 0
        for slot, q, to, parts in ((0, qx, yn, tops), (1, qy, xn, bottoms)):
            for plo, pcnt in parts:
                rows = pl.ds(plo, pcnt)
                out.append(_remote(transit_ref.at[slot, rows, :], dst_ref.at[q, rows, :], send.at[k], recv.at[k], to))
                arrive.append(_remote(transit_ref.at[slot, rows, :], dst_ref.at[qd, rows, :], send.at[k], recv.at[k], to))
                k += 1
        return out, arrive

    def start(ins, outs, sems):
        for cp in copies(ins, outs, sems)[0]:
            cp.start()

    def finish(ins, outs, sems):
        out, arrive = copies(ins, outs, sems)
        for cp in arrive:
            cp.wait_recv()
        for cp in out:
            cp.wait_send()

    return _Stage([dst, transit], [jax.ShapeDtypeStruct(dst.shape, dst.dtype)], {0: 0},
                  [pltpu.SemaphoreType.DMA((len(tops) + len(bottoms),)), pltpu.SemaphoreType.DMA((len(tops) + len(bottoms),))], start, finish)


def _fuse(parts, ins, outs, alias):
    parts = [p for p in parts if p is not None]
    sems = [x for st, _, _ in parts for x in st.sems]

    def run(which):
        def go(i, o, s):
            refs, at = list(i) + list(o), 0
            for st, pi, po in parts:
                getattr(st, which)([refs[k] for k in pi], [refs[k] for k in po], s[at:at + len(st.sems)])
                at += len(st.sems)
        return go

    return _Stage(ins, [jax.ShapeDtypeStruct(o.shape, o.dtype) for o in outs], alias, sems, run("start"), run("finish"))


def _gather_stage(shard, full, kind, new=None, relay=None, forward=None):
    return _fuse([(_spread_stage(shard, full, kind, new), [0, 1], [2]) if new else None,
                  (_relay_stage(full, kind, relay), [1], [2]) if relay else None,
                  (_spread_stage(shard, full, kind, relay, home=True), [0, 1], [2]) if relay else None,
                  (_forward_stage(full, kind, forward), [1], [2]) if forward else None], [shard, full], [full], {1: 0})


def _reduce_stage(pair, dst, transit, new=None, relay=None):
    return _fuse([(_to_chips_stage(pair, dst, transit, new), [0, 1, 2], [3, 4]) if new else None,
                  (_to_chips_relay_stage(dst, transit, relay), [3, 4], [3]) if relay else None], [pair, dst, transit], [dst, transit],
                 {1: 0, 2: 1})


def _small_stage(part):
    def copies(ins, outs, sems):
        send, recv, local = sems
        x, y, c = _coords()
        me = (x, y, c)
        own = pltpu.make_async_copy(ins[0], outs[0].at[_lin(me)], local)
        peers = [(1 - x if k & 4 else x, 1 - y if k & 2 else y, 1 - c if k & 1 else c) for k in range(1, N_DEV)]
        out = [_remote(ins[0], outs[0].at[_lin(me)], send.at[k], recv.at[k], t) for k, t in enumerate(peers)]
        arrive = [_remote(ins[0], outs[0].at[_lin(t)], send.at[k], recv.at[k], t) for k, t in enumerate(peers)]
        return own, out, arrive

    def start(ins, outs, sems):
        own, out, _ = copies(ins, outs, sems)
        own.start()
        for cp in out:
            cp.start()

    def finish(ins, outs, sems):
        own, out, arrive = copies(ins, outs, sems)
        for cp in arrive:
            cp.wait_recv()
        for cp in out:
            cp.wait_send()
        own.wait()

    return _Stage([part], [jax.ShapeDtypeStruct((N_DEV, *part.shape), F32)], {},
                  [pltpu.SemaphoreType.DMA((N_DEV - 1,)), pltpu.SemaphoreType.DMA((N_DEV - 1,)), pltpu.SemaphoreType.DMA], start, finish)


def _comm_only(name, stages):
    return _call(name, lambda: None, (1,), [], [], [], [], stages=stages, sequential=True)[1]


_GELU_C = math.sqrt(2.0 / math.pi)


def _gelu(x):
    return 0.5 * x * (1.0 + jnp.tanh(_GELU_C * (x + 0.044715 * x * x * x)))


def _gelu_grad(x):
    t = jnp.tanh(_GELU_C * (x + 0.044715 * x * x * x))
    return 0.5 * (1.0 + t) + 0.5 * x * (1.0 - t * t) * (_GELU_C * (1.0 + 3.0 * 0.044715 * x * x))


def _sigmoid(x):
    return 1.0 / (1.0 + jnp.exp(-x))


def _dot(a, b, mode):
    dims = {"nn": (((1,), (0,)), ((), ())), "nt": (((1,), (1,)), ((), ())), "tn": (((0,), (0,)), ((), ()))}[mode]
    return lax.dot_general(a.astype(BF16), b.astype(BF16), dims, preferred_element_type=F32)


def _matmul(name, a, b, mode, outs, epi=None, extras=(), colsums=(), tm=2048, tn=512, tk=2048, b_off=0, n=None, m_off=0, m=None, stages=()):
    if mode == "tn":
        kk, mfull = a.shape
    else:
        mfull, kk = a.shape
    m = mfull if m is None else m
    n = (b.shape[0] if mode == "nt" else b.shape[1]) if n is None else n
    tm, tk = _pick(tm, m, m_off), _pick(tk, kk)
    tn = _pick(tn, n, b_off, *[off for _, _, off in extras])
    boff, moff = b_off // tn, m_off // tm
    nm, nn_, nk = m // tm, n // tn, kk // tk
    col_major = bool(colsums)
    grid = (nn_, nm, nk) if col_major else (nm, nn_, nk)

    def imap(f):
        if col_major:
            return lambda g0, g1, k: f(g1, g0, k)
        return f

    a_spec = (pl.BlockSpec((tk, tm), imap(lambda i, j, k: (k, i + moff))) if mode == "tn"
              else pl.BlockSpec((tm, tk), imap(lambda i, j, k: (i + moff, k))))
    b_spec = (pl.BlockSpec((tn, tk), imap(lambda i, j, k: (j + boff, k))) if mode == "nt"
              else pl.BlockSpec((tk, tn), imap(lambda i, j, k: (k, j + boff))))
    in_specs, operands = [a_spec, b_spec], [a, b]
    for arr, kind, off in extras:
        o = off // tn
        if kind == "mn":
            in_specs.append(pl.BlockSpec((tm, tn), imap(lambda i, j, k, o=o: (i + moff, j + o))))
        else:
            in_specs.append(pl.BlockSpec((1, tn), imap(lambda i, j, k, o=o: (0, j + o))))
        operands.append(arr)
    out_shape = [jax.ShapeDtypeStruct((m, n), dt) for dt in outs] + [jax.ShapeDtypeStruct((1, n), F32) for _ in colsums]
    out_specs = ([pl.BlockSpec((tm, tn), imap(lambda i, j, k: (i, j))) for _ in outs]
                 + [pl.BlockSpec((1, tn), imap(lambda i, j, k: (0, j))) for _ in colsums])
    n_ex, n_out, n_cs = len(extras), len(outs), len(colsums)

    def body(*refs):
        a_ref, b_ref = refs[:2]
        ex_refs = refs[2:2 + n_ex]
        out_refs = refs[2 + n_ex:2 + n_ex + n_out]
        cs_refs = refs[2 + n_ex + n_out:2 + n_ex + n_out + n_cs]
        part = _dot(a_ref[...], b_ref[...], mode)

        def finish(acc):
            res = epi(acc, *[r[...] for r in ex_refs]) if epi is not None else (acc,)
            for r, v in zip(out_refs, res[:n_out]):
                r[...] = v.astype(r.dtype)
            if n_cs:
                @pl.when(pl.program_id(1) == 0)
                def _():
                    for r in cs_refs:
                        r[...] = jnp.zeros_like(r)

                for r, idx in zip(cs_refs, colsums):
                    r[...] += jnp.sum(res[idx], axis=0, keepdims=True)

        if nk == 1:
            finish(part)
        else:
            acc_ref = refs[-1]
            k = pl.program_id(2)

            @pl.when(k == 0)
            def _():
                acc_ref[...] = part

            @pl.when(k > 0)
            def _():
                acc_ref[...] += part

            @pl.when(k == nk - 1)
            def _():
                finish(acc_ref[...])

    sem = ("arbitrary", "arbitrary", "arbitrary") if col_major else ("parallel", "parallel", "arbitrary")
    return _call(name, body, grid, in_specs, out_specs, out_shape, operands,
                 scratch=[pltpu.VMEM((tm, tn), F32)] if nk > 1 else [], sem=sem, stages=stages)


def _row_spec(tr, c):
    return pl.BlockSpec((tr, c), lambda i: (i, 0))


def _fix_spec(shape):
    return pl.BlockSpec(shape, lambda *_: tuple(0 for _ in shape))


def _cast_bf16(name, x, tr=512):
    r, c = x.shape
    tr = _pick(tr, r)

    def body(x_ref, o_ref):
        o_ref[...] = x_ref[...].astype(BF16)

    return _call(name, body, (r // tr,), [_row_spec(tr, c)], [_row_spec(tr, c)], [jax.ShapeDtypeStruct((r, c), BF16)], [x],
                 sem=("parallel",))[0][0]


def _layer_norm_stats(x):
    mean = jnp.mean(x, axis=-1, keepdims=True)
    xc = x - mean
    var = jnp.mean(xc * xc, axis=-1, keepdims=True)
    rstd = lax.rsqrt(var + LN_EPS)
    return xc * rstd, rstd


def _layer_norm_bwd(dxhat, xhat, rstd):
    m1 = jnp.mean(dxhat, axis=-1, keepdims=True)
    m2 = jnp.mean(dxhat * xhat, axis=-1, keepdims=True)
    return rstd * (dxhat - m1 - xhat * m2)


def _ln1_fwd(pre1, g1, b1, tr=256, stages=()):
    s, d = pre1.shape
    tr = _pick(tr, s)

    def body(p_ref, g_ref, b_ref, xh_ref, rs_ref, h_ref):
        xhat, rstd = _layer_norm_stats(p_ref[...])
        xh_ref[...] = xhat
        rs_ref[...] = rstd
        h_ref[...] = (xhat * g_ref[...] + b_ref[...]).astype(BF16)

    return _call("ln1_fwd", body, (s // tr,), [_row_spec(tr, d), _fix_spec((1, d)), _fix_spec((1, d))],
                 [_row_spec(tr, d), _row_spec(tr, 1), _row_spec(tr, d)],
                 [jax.ShapeDtypeStruct((s, d), F32), jax.ShapeDtypeStruct((s, 1), F32), jax.ShapeDtypeStruct((s, d), BF16)],
                 [pre1, g1, b1], sem=("parallel",), stages=stages)


def _ln2_loss_bwd(ff, xhat1, g1, b1, g2, b2, target, tr=256):
    s, d = ff.shape
    tr = _pick(tr, s)

    def body(ff_ref, xh1_ref, g1_ref, b1_ref, g2_ref, b2_ref, t_ref, dp_ref, dpb_ref, dg_ref, db_ref, dbf_ref, loss_ref):
        @pl.when(pl.program_id(0) == 0)
        def _():
            dg_ref[...] = jnp.zeros_like(dg_ref)
            db_ref[...] = jnp.zeros_like(db_ref)
            dbf_ref[...] = jnp.zeros_like(dbf_ref)
            loss_ref[...] = jnp.zeros_like(loss_ref)

        h1 = xh1_ref[...] * g1_ref[...] + b1_ref[...]
        xhat, rstd = _layer_norm_stats(ALPHA * h1 + ff_ref[...])
        err = xhat * g2_ref[...] + b2_ref[...] - t_ref[...]
        row = jnp.mean(err * err, axis=-1, keepdims=True)
        loss_ref[...] += 0.5 * jnp.sum(row, axis=0, keepdims=True)
        dy = err / d
        dg_ref[...] += jnp.sum(dy * xhat, axis=0, keepdims=True)
        db_ref[...] += jnp.sum(dy, axis=0, keepdims=True)
        dpre = _layer_norm_bwd(dy * g2_ref[...], xhat, rstd)
        dbf_ref[...] += jnp.sum(dpre, axis=0, keepdims=True)
        dp_ref[...] = dpre
        dpb_ref[...] = dpre.astype(BF16)

    vec = _fix_spec((1, d))
    return _call("ln2_loss_bwd", body, (s // tr,), [_row_spec(tr, d), _row_spec(tr, d), vec, vec, vec, vec, _row_spec(tr, d)],
                 [_row_spec(tr, d), _row_spec(tr, d), vec, vec, vec, _fix_spec((1, 1))],
                 [jax.ShapeDtypeStruct((s, d), F32), jax.ShapeDtypeStruct((s, d), BF16)]
                 + [jax.ShapeDtypeStruct((1, d), F32)] * 3 + [jax.ShapeDtypeStruct((1, 1), F32)],
                 [ff, xhat1, g1, b1, g2, b2, target])[0]


def _ln1_bwd(dh1, xhat1, rstd1, g1, tr=256, stages=()):
    s, d = dh1.shape
    tr = _pick(tr, s)

    def body(dh_ref, xh_ref, rs_ref, g_ref, dp_ref, dpb_ref, dg_ref, db_ref):
        @pl.when(pl.program_id(0) == 0)
        def _():
            dg_ref[...] = jnp.zeros_like(dg_ref)
            db_ref[...] = jnp.zeros_like(db_ref)

        dh, xhat = dh_ref[...], xh_ref[...]
        dg_ref[...] += jnp.sum(dh * xhat, axis=0, keepdims=True)
        db_ref[...] += jnp.sum(dh, axis=0, keepdims=True)
        dpre = _layer_norm_bwd(dh * g_ref[...], xhat, rs_ref[...])
        dp_ref[...] = dpre
        dpb_ref[...] = dpre.astype(BF16)

    vec = _fix_spec((1, d))
    return _call("ln1_bwd", body, (s // tr,), [_row_spec(tr, d), _row_spec(tr, d), _row_spec(tr, 1), vec],
                 [_row_spec(tr, d), _row_spec(tr, d), vec, vec],
                 [jax.ShapeDtypeStruct((s, d), F32), jax.ShapeDtypeStruct((s, d), BF16)] + [jax.ShapeDtypeStruct((1, d), F32)] * 2,
                 [dh1, xhat1, rstd1, g1], stages=stages)


def _to_perm(x):
    return x.reshape(SEQ // N_SUB, N_SUB, -1).transpose(1, 0, 2).reshape(SEQ, -1)


def _from_perm(x):
    return x.reshape(N_SUB, SEQ // N_SUB, -1).transpose(1, 0, 2).reshape(SEQ, -1)


def _local_index(p):
    rho = np.arange(BLOCK)
    if p == 0:
        return 16 * (rho % 8) + rho // 8
    if p == 1:
        return 4 * (rho % 32) + rho // 32
    return rho


def _tile_view(x, p):
    c = x.shape[1]
    if p == 1:
        return x.reshape(4, 4, BLOCK, c)
    return x.reshape(N_SUB, BLOCK, c)


def _view_shape(c, p):
    return (4, 4, BLOCK, c) if p == 1 else (N_SUB, BLOCK, c)


def _tile_spec(p, width, col, shift=0):
    nblk = SEQ // DILATIONS[p] // BLOCK

    def blk(n):
        return jnp.clip(n + shift, 0, nblk - 1)

    if p == 0:
        return pl.BlockSpec((N_SUB, SUBLANES, width), lambda s, n: (0, blk(n), col))
    if p == 1:
        return pl.BlockSpec((4, None, 32, width), lambda s, n: (0, s, blk(n), col))
    return pl.BlockSpec((None, BLOCK, width), lambda s, n: (s, 0, col))


def _tile_grid(p):
    return ((1, 16), (4, 4), (16, 1))[p]


def _t5_bucket(n):
    max_exact = N_BUCKETS // 2
    nf = np.maximum(n, 1).astype(np.float32)
    large = max_exact + (np.log(nf / np.float32(max_exact)) / np.float32(math.log(MAX_DISTANCE / max_exact))
                         * np.float32(N_BUCKETS - max_exact)).astype(np.int32)
    large = np.minimum(large, N_BUCKETS - 1)
    return np.where(n < max_exact, n, large).astype(np.int32)


def _bucket_tables():
    tabs = np.zeros((3, 2, BLOCK, BLOCK), np.int32)
    for p, d in enumerate(DILATIONS):
        i = _local_index(p)
        diff = i[:, None] - i[None, :]
        tabs[p, 0] = np.where(diff <= 0, _t5_bucket((BLOCK + diff) * d), -1)
        tabs[p, 1] = np.where(diff >= 0, _t5_bucket(np.maximum(diff, 0) * d), -1)
    return tabs


def _bias_expand(rel_bias, buckets):
    nh = N_HEADS

    def body(rb_ref, bk_ref, o_ref):
        for w in range(2):
            bk = bk_ref[0, w]
            for h in range(nh):
                val = jnp.zeros((BLOCK, BLOCK), F32)
                for b in range(N_BUCKETS):
                    val = jnp.where(bk == b, rb_ref[b, h], val)
                o_ref[0, h, w] = jnp.where(bk < 0, NEG_INF, val)

    return _call("bias_expand", body, (3,),
                 [pl.BlockSpec(memory_space=pltpu.SMEM), pl.BlockSpec((1, 2, BLOCK, BLOCK), lambda p: (p, 0, 0, 0))],
                 [pl.BlockSpec((1, nh, 2, BLOCK, BLOCK), lambda p: (p, 0, 0, 0, 0))],
                 [jax.ShapeDtypeStruct((3, nh, 2, BLOCK, BLOCK), F32)], [rel_bias, buckets], sem=("parallel",))[0][0]


def _heads_to_lanes(cols):
    lane = lax.broadcasted_iota(I32, (BLOCK, LANES), 1)
    out = jnp.zeros((BLOCK, LANES), F32)
    for h, c in enumerate(cols):
        out = jnp.where(lane == h, c, out)
    return out


def _attn_fwd(qkv, bias, p, stages=()):
    d_a = _d_a()
    has_prev = SEQ // DILATIONS[p] // BLOCK > 1
    scale = HEAD_DIM ** -0.5
    view = _tile_view(qkv, p)

    def body(q_ref, kc_ref, kp_ref, vc_ref, vp_ref, b_ref, o_ref, l_ref):
        n = pl.program_id(1)
        q_all = q_ref[...].reshape(BLOCK, d_a).astype(BF16)
        kc_all = kc_ref[...].reshape(BLOCK, d_a).astype(BF16)
        vc_all = vc_ref[...].reshape(BLOCK, d_a).astype(BF16)
        if has_prev:
            kp_all = kp_ref[...].reshape(BLOCK, d_a).astype(BF16)
            vp_all = vp_ref[...].reshape(BLOCK, d_a).astype(BF16)
        outs, lses = [], []
        for h in range(N_HEADS):
            sl = slice(h * HEAD_DIM, (h + 1) * HEAD_DIM)
            q = q_all[:, sl]
            sc = _dot(q, kc_all[:, sl], "nt") * scale + b_ref[0, h, 1]
            m = jnp.max(sc, axis=-1, keepdims=True)
            if has_prev:
                sp = _dot(q, kp_all[:, sl], "nt") * scale + b_ref[0, h, 0]
                sp = jnp.where(n > 0, sp, NEG_INF)
                m = jnp.maximum(m, jnp.max(sp, axis=-1, keepdims=True))
                pp = jnp.exp(sp - m)
            pc = jnp.exp(sc - m)
            den = jnp.sum(pc, axis=-1, keepdims=True)
            o = _dot(pc, vc_all[:, sl], "nn")
            if has_prev:
                den = den + jnp.sum(pp, axis=-1, keepdims=True)
                o = o + _dot(pp, vp_all[:, sl], "nn")
            outs.append(o / den)
            lses.append(m + jnp.log(den))
        o_ref[...] = jnp.concatenate(outs, axis=-1).reshape(o_ref.shape)
        l_ref[...] = _heads_to_lanes(lses).reshape(l_ref.shape)

    (o, l), st = _call(
        f"attn_fwd{p}", body, _tile_grid(p),
        [_tile_spec(p, d_a, 0), _tile_spec(p, d_a, 1), _tile_spec(p, d_a, 1, -1), _tile_spec(p, d_a, 2), _tile_spec(p, d_a, 2, -1),
         pl.BlockSpec((1, N_HEADS, 2, BLOCK, BLOCK), lambda s, n: (p, 0, 0, 0, 0))],
        [_tile_spec(p, d_a, 0), _tile_spec(p, LANES, 0)],
        [jax.ShapeDtypeStruct(_view_shape(d_a, p), F32), jax.ShapeDtypeStruct(_view_shape(LANES, p), F32)],
        [view, view, view, view, view, bias], sem=("parallel", "parallel"), stages=stages)
    return (o.reshape(SEQ, d_a), l.reshape(SEQ, LANES)), st


def _attn_combine(os_, ls_, tr=256, stages=()):
    d_a = _d_a()
    tr = _pick(tr, SEQ)

    def body(o0, o1, o2, l0, l1, l2, a_ref, ab_ref, lt_ref):
        l = [l0[...], l1[...], l2[...]]
        m = jnp.maximum(jnp.maximum(l[0], l[1]), l[2])
        w = [jnp.exp(x - m) for x in l]
        tot = w[0] + w[1] + w[2]
        lt_ref[...] = m + jnp.log(tot)
        w = [x / tot for x in w]
        for h in range(N_HEADS):
            sl = slice(h * HEAD_DIM, (h + 1) * HEAD_DIM)
            acc = w[0][:, h:h + 1] * o0[:, sl] + w[1][:, h:h + 1] * o1[:, sl] + w[2][:, h:h + 1] * o2[:, sl]
            a_ref[:, sl] = acc
            ab_ref[:, sl] = acc.astype(BF16)

    return _call("attn_combine", body, (SEQ // tr,), [_row_spec(tr, d_a)] * 3 + [_row_spec(tr, LANES)] * 3,
                 [_row_spec(tr, d_a), _row_spec(tr, d_a), _row_spec(tr, LANES)],
                 [jax.ShapeDtypeStruct((SEQ, d_a), F32), jax.ShapeDtypeStruct((SEQ, d_a), BF16), jax.ShapeDtypeStruct((SEQ, LANES), F32)],
                 [*os_, *ls_], sem=("parallel",), stages=stages)


def _attn_delta(dattn, attn, tr=256):
    d_a = _d_a()
    tr = _pick(tr, SEQ)

    def body(d_ref, a_ref, o_ref):
        prod = d_ref[...] * a_ref[...]
        lane = lax.broadcasted_iota(I32, (tr, LANES), 1)
        out = jnp.zeros((tr, LANES), F32)
        for h in range(N_HEADS):
            out = jnp.where(lane == h, jnp.sum(prod[:, h * HEAD_DIM:(h + 1) * HEAD_DIM], axis=-1, keepdims=True), out)
        o_ref[...] = out

    return _call("attn_delta", body, (SEQ // tr,), [_row_spec(tr, d_a)] * 2, [_row_spec(tr, LANES)],
                 [jax.ShapeDtypeStruct((SEQ, LANES), F32)], [dattn, attn], sem=("parallel",))[0][0]


def _attn_bwd(qkv, dattn, lse, delta, bias, p, stages=()):
    d_a = _d_a()
    nblk = SEQ // DILATIONS[p] // BLOCK
    has_next = nblk > 1
    scale = HEAD_DIM ** -0.5
    qv, dov, lv, tv = (_tile_view(x, p) for x in (qkv, dattn, lse, delta))

    def body(q_ref, qn_ref, k_ref, v_ref, do_ref, don_ref, l_ref, ln_ref, t_ref, tn_ref, b_ref, dq_ref, dk_ref, dv_ref, db_ref, carry_ref):
        j = pl.program_id(1)

        @pl.when((pl.program_id(0) == 0) & (j == 0))
        def _():
            db_ref[...] = jnp.zeros_like(db_ref)

        k_all = k_ref[...].reshape(BLOCK, d_a).astype(BF16)
        v_all = v_ref[...].reshape(BLOCK, d_a).astype(BF16)

        def side(qr, dor, lr, tr_, w):
            q_all = qr[...].reshape(BLOCK, d_a).astype(BF16)
            do_all = dor[...].reshape(BLOCK, d_a).astype(BF16)
            l_all = lr[...].reshape(BLOCK, LANES)
            t_all = tr_[...].reshape(BLOCK, LANES)
            dqs, dks, dvs = [], [], []
            for h in range(N_HEADS):
                sl = slice(h * HEAD_DIM, (h + 1) * HEAD_DIM)
                s = _dot(q_all[:, sl], k_all[:, sl], "nt") * scale + b_ref[0, h, w]
                pr = jnp.exp(s - l_all[:, h:h + 1])
                dp = _dot(do_all[:, sl], v_all[:, sl], "nt")
                ds = pr * (dp - t_all[:, h:h + 1])
                db_ref[h, w] += ds
                dqs.append(_dot(ds, k_all[:, sl], "nn") * scale)
                dks.append(_dot(ds, q_all[:, sl], "tn") * scale)
                dvs.append(_dot(pr, do_all[:, sl], "tn"))
            return [jnp.concatenate(x, axis=-1) for x in (dqs, dks, dvs)]

        dq_c, dk_c, dv_c = side(q_ref, do_ref, l_ref, t_ref, 1)
        if has_next:
            dq_ref[...] = (jnp.where(j > 0, carry_ref[...], 0.0) + dq_c).reshape(dq_ref.shape)
            not_last = j < nblk - 1

            @pl.when(not_last)
            def _():
                dq_n, dk_n, dv_n = side(qn_ref, don_ref, ln_ref, tn_ref, 0)
                carry_ref[...] = dq_n
                dk_ref[...] = (dk_c + dk_n).reshape(dk_ref.shape)
                dv_ref[...] = (dv_c + dv_n).reshape(dv_ref.shape)

            @pl.when(jnp.logical_not(not_last))
            def _():
                dk_ref[...] = dk_c.reshape(dk_ref.shape)
                dv_ref[...] = dv_c.reshape(dv_ref.shape)
        else:
            dq_ref[...] = dq_c.reshape(dq_ref.shape)
            dk_ref[...] = dk_c.reshape(dk_ref.shape)
            dv_ref[...] = dv_c.reshape(dv_ref.shape)

    def big(col, shift=0):
        return _tile_spec(p, d_a, col, shift)

    def small(shift=0):
        return _tile_spec(p, LANES, 0, shift)

    (dq, dk, dv, dbias), st = _call(
        f"attn_bwd{p}", body, _tile_grid(p),
        [big(0), big(0, 1), big(1), big(2), big(0), big(0, 1), small(), small(1), small(), small(1),
         pl.BlockSpec((1, N_HEADS, 2, BLOCK, BLOCK), lambda s, n: (p, 0, 0, 0, 0))],
        [big(0), big(0), big(0), pl.BlockSpec((N_HEADS, 2, BLOCK, BLOCK), lambda s, n: (0, 0, 0, 0))],
        [jax.ShapeDtypeStruct(_view_shape(d_a, p), F32)] * 3 + [jax.ShapeDtypeStruct((N_HEADS, 2, BLOCK, BLOCK), F32)],
        [qv, qv, qv, qv, dov, dov, lv, lv, tv, tv, bias], scratch=[pltpu.VMEM((BLOCK, d_a), F32)], stages=stages)
    return (dq.reshape(SEQ, d_a), dk.reshape(SEQ, d_a), dv.reshape(SEQ, d_a), dbias), st


def _rel_bias_grad(dbias, buckets):
    nh = N_HEADS

    def body(d0, d1, d2, bk_ref, o_ref, t_ref):
        ds = (d0, d1, d2)

        def per_bucket(b, carry):
            for h in range(nh):
                acc = jnp.zeros((BLOCK, BLOCK), F32)
                for p in range(3):
                    for w in range(2):
                        acc = acc + jnp.where(bk_ref[p, w] == b, ds[p][h, w], 0.0)
                t_ref[pl.ds(b * nh + h, 1), :] = jnp.sum(acc, axis=0, keepdims=True)
            return carry

        lax.fori_loop(0, N_BUCKETS, per_bucket, 0)
        o_ref[...] = jnp.sum(t_ref[...], axis=-1, keepdims=True)

    return _call("rel_bias_grad", body, (1,), [_fix_spec((nh, 2, BLOCK, BLOCK))] * 3 + [_fix_spec((3, 2, BLOCK, BLOCK))],
                 [_fix_spec((N_BUCKETS * nh, 1))], [jax.ShapeDtypeStruct((N_BUCKETS * nh, 1), F32)], [*dbias, buckets],
                 scratch=[pltpu.VMEM((N_BUCKETS * nh, LANES), F32)])[0][0]


def _gmlp_fwd(rest, gain, bias, ws, bs, causal, stages=()):
    d_b = _d_b()

    def body(u_ref, v_ref, g_ref, b_ref, ws_ref, bs_ref, c_ref, o_ref):
        u = u_ref[...].reshape(BLOCK, d_b)
        xhat, _ = _layer_norm_stats(_gelu(v_ref[...].reshape(BLOCK, d_b)))
        vn = (xhat * g_ref[...] + b_ref[...]).astype(BF16)
        outs = []
        for g in range(N_GROUPS):
            sl = slice(g * BLOCK, (g + 1) * BLOCK)
            w = jnp.where(c_ref[...] > 0, ws_ref[g], 0.0)
            z = _dot(w, vn[:, sl], "nn") + bs_ref[:, g:g + 1]
            outs.append(_gelu(u[:, sl]) * z)
        o_ref[...] = jnp.concatenate(outs, axis=-1).reshape(o_ref.shape)

    (out,), st = _call(
        "gmlp_fwd", body, (1, SEQ // BLOCK),
        [_tile_spec(0, d_b, 0), _tile_spec(0, d_b, 1), _fix_spec((1, d_b)), _fix_spec((1, d_b)),
         _fix_spec((N_GROUPS, BLOCK, BLOCK)), _fix_spec((BLOCK, N_GROUPS)), _fix_spec((BLOCK, BLOCK))],
        [_tile_spec(0, d_b, 0)], [jax.ShapeDtypeStruct(_view_shape(d_b, 0), F32)],
        [_tile_view(rest, 0), _tile_view(rest, 0), gain, bias, ws, bs, causal], sem=("parallel", "parallel"), stages=stages)
    return out.reshape(SEQ, d_b), st


def _gmlp_bwd(rest, dgmlp, gain, bias, ws, bs, causal, stages=()):
    d_b = _d_b()
    nchunk = SEQ // BLOCK

    def body(u_ref, v_ref, dg_ref, g_ref, b_ref, ws_ref, bs_ref, c_ref, du_ref, dv_ref, dws_ref, dbs_ref, dgain_ref, dbias_ref):
        c = pl.program_id(1)

        @pl.when(c == 0)
        def _():
            dws_ref[...] = jnp.zeros_like(dws_ref)
            dbs_ref[...] = jnp.zeros_like(dbs_ref)
            dgain_ref[...] = jnp.zeros_like(dgain_ref)
            dbias_ref[...] = jnp.zeros_like(dbias_ref)

        u = u_ref[...].reshape(BLOCK, d_b)
        v = v_ref[...].reshape(BLOCK, d_b)
        dgm = dg_ref[...].reshape(BLOCK, d_b)
        xhat, rstd = _layer_norm_stats(_gelu(v))
        vn = (xhat * g_ref[...] + b_ref[...]).astype(BF16)
        lane = lax.broadcasted_iota(I32, (BLOCK, LANES), 1)
        dus, dvns = [], []
        dbs = dbs_ref[...]
        for g in range(N_GROUPS):
            sl = slice(g * BLOCK, (g + 1) * BLOCK)
            w = jnp.where(c_ref[...] > 0, ws_ref[g], 0.0).astype(BF16)
            z = _dot(w, vn[:, sl], "nn") + bs_ref[:, g:g + 1]
            dz = dgm[:, sl] * _gelu(u[:, sl])
            dus.append(dgm[:, sl] * z * _gelu_grad(u[:, sl]))
            dws_ref[g] += _dot(dz, vn[:, sl], "nt")
            dbs = dbs + jnp.where(lane == g, jnp.sum(dz, axis=-1, keepdims=True), 0.0)
            dvns.append(_dot(w, dz, "tn"))
        dbs_ref[...] = dbs
        dvn = jnp.concatenate(dvns, axis=-1)
        dgain_ref[...] += jnp.sum(dvn * xhat, axis=0, keepdims=True)
        dbias_ref[...] += jnp.sum(dvn, axis=0, keepdims=True)
        dvg = _layer_norm_bwd(dvn * g_ref[...], xhat, rstd)
        du_ref[...] = jnp.concatenate(dus, axis=-1).reshape(du_ref.shape)
        dv_ref[...] = (dvg * _gelu_grad(v)).reshape(dv_ref.shape)

        @pl.when(c == nchunk - 1)
        def _():
            for g in range(N_GROUPS):
                dws_ref[g] = jnp.where(c_ref[...] > 0, dws_ref[g], 0.0)

    (du, dv, dws, dbs, dgain, dbias), st = _call(
        "gmlp_bwd", body, (1, nchunk),
        [_tile_spec(0, d_b, 0), _tile_spec(0, d_b, 1), _tile_spec(0, d_b, 0), _fix_spec((1, d_b)), _fix_spec((1, d_b)),
         _fix_spec((N_GROUPS, BLOCK, BLOCK)), _fix_spec((BLOCK, N_GROUPS)), _fix_spec((BLOCK, BLOCK))],
        [_tile_spec(0, d_b, 0), _tile_spec(0, d_b, 0), _fix_spec((N_GROUPS, BLOCK, BLOCK)), _fix_spec((BLOCK, LANES)),
         _fix_spec((1, d_b)), _fix_spec((1, d_b))],
        [jax.ShapeDtypeStruct(_view_shape(d_b, 0), F32)] * 2
        + [jax.ShapeDtypeStruct((N_GROUPS, BLOCK, BLOCK), F32), jax.ShapeDtypeStruct((BLOCK, LANES), F32)]
        + [jax.ShapeDtypeStruct((1, d_b), F32)] * 2,
        [_tile_view(rest, 0), _tile_view(rest, 0), _tile_view(dgmlp, 0), gain, bias, ws, bs, causal], stages=stages)
    return (du.reshape(SEQ, d_b), dv.reshape(SEQ, d_b), dws, dbs, dgain, dbias), st


def _assemble_dproj(dqkv, du, dv, dga, dgb, tr=128):
    d_a, d_b, d_in = _d_a(), _d_b(), _d_in()
    tr = _pick(tr, SEQ)

    def body(*refs):
        att, (du_ref, dv_ref, dga_ref, dgb_ref, o_ref) = refs[:9], refs[9:]
        for i in range(3):
            o_ref[:, i * d_a:(i + 1) * d_a] = (att[3 * i][...] + att[3 * i + 1][...] + att[3 * i + 2][...]).astype(BF16)
        o_ref[:, 3 * d_a:3 * d_a + d_b] = du_ref[...].astype(BF16)
        o_ref[:, 3 * d_a + d_b:3 * d_a + 2 * d_b] = dv_ref[...].astype(BF16)
        o_ref[:, 3 * d_a + 2 * d_b:3 * d_a + 2 * d_b + D_MODEL] = dga_ref[...]
        o_ref[:, 3 * d_a + 2 * d_b + D_MODEL:] = dgb_ref[...]

    return _call("assemble_dproj", body, (SEQ // tr,), [_row_spec(tr, d_a)] * 9 + [_row_spec(tr, d_b)] * 2 + [_row_spec(tr, D_MODEL)] * 2,
                 [_row_spec(tr, d_in)], [jax.ShapeDtypeStruct((SEQ, d_in), BF16)], [*dqkv, du, dv, dga, dgb], sem=("parallel",))[0][0]


def _dw(name, a, b, kind, core, mine, add=None, tn=1152, stages=()):
    s, m = a.shape
    n = b.shape[1]
    rs, cs = (m, n // N_DEV) if kind == "col" else (m // N_DEV, n)
    tn = _pick(tn if kind == "col" else 512, cs)
    nj = cs // tn

    def shard(q, c_ref):
        return 2 * q + (c_ref[0] if mine else 1 - c_ref[0])

    if kind == "col":
        a_spec = pl.BlockSpec((s, m), lambda q, j, c_ref: (0, 0))
        b_spec = pl.BlockSpec((s, tn), lambda q, j, c_ref: (0, shard(q, c_ref) * nj + j))
    else:
        a_spec = pl.BlockSpec((s, rs), lambda q, j, c_ref: (0, shard(q, c_ref)))
        b_spec = pl.BlockSpec((s, tn), lambda q, j, c_ref: (0, j))
    o_spec = pl.BlockSpec((None, rs, tn), lambda q, j, c_ref: (q, 0, j))

    def body(a_ref, b_ref, *rest):
        acc = _dot(a_ref[...], b_ref[...], "tn")
        if add is not None:
            acc = acc + rest[0][...].astype(F32)
        rest[-1][...] = acc.astype(BF16)

    (out,), st = _call(name, body, (N_CHIPS, nj), [a_spec, b_spec] + ([o_spec] if add is not None else []), [o_spec],
                       [jax.ShapeDtypeStruct((N_CHIPS, rs, cs), BF16)], [a, b] + ([add] if add is not None else []),
                       sem=("parallel", "parallel"), stages=stages, prefetch=core)
    return out, st


def _adamw(w, g, m, v):
    m = ADAM_B1 * m + (1.0 - ADAM_B1) * g
    v = ADAM_B2 * v + (1.0 - ADAM_B2) * (g * g)
    m_hat = m / (1.0 - ADAM_B1 ** ADAM_STEP)
    v_hat = v / (1.0 - ADAM_B2 ** ADAM_STEP)
    delta = -ADAM_LR * (m_hat / (jnp.sqrt(v_hat) + ADAM_EPS) + ADAM_WD * w)
    return delta, m, v


def _adam_shard(name, chip_sums, w, m, v, tr=256, stages=()):
    rs, cs = w.shape
    tr = _pick(tr, rs)

    def body(s_ref, w_ref, m_ref, v_ref, g_ref, d_ref, nm_ref, nv_ref):
        g = s_ref[0].astype(F32)
        for q in range(1, N_CHIPS):
            g = g + s_ref[q].astype(F32)
        d, nm, nv = _adamw(w_ref[...], g, m_ref[...], v_ref[...])
        g_ref[...], d_ref[...], nm_ref[...], nv_ref[...] = g, d, nm, nv

    spec = _row_spec(tr, cs)
    return _call(name, body, (rs // tr,), [pl.BlockSpec((N_CHIPS, tr, cs), lambda i: (0, i, 0)), spec, spec, spec], [spec] * 4,
                 [jax.ShapeDtypeStruct((rs, cs), F32)] * 4, [chip_sums, w, m, v], sem=("parallel",), stages=stages)


def _adam_small(parts, w, m, v):
    rows = w.shape[0]

    def body(p_ref, w_ref, m_ref, v_ref, g_ref, d_ref, nm_ref, nv_ref):
        g = p_ref[0]
        for j in range(1, N_DEV):
            g = g + p_ref[j]
        d, nm, nv = _adamw(w_ref[...], g, m_ref[...], v_ref[...])
        g_ref[...], d_ref[...], nm_ref[...], nv_ref[...] = g, d, nm, nv

    spec = _fix_spec((rows, LANES))
    return _call("adam_small", body, (1,), [_fix_spec((N_DEV, rows, LANES)), spec, spec, spec], [spec] * 4,
                 [jax.ShapeDtypeStruct((rows, LANES), F32)] * 4, [parts, w, m, v])[0]


def _small_sizes():
    d_b = _d_b()
    return (("loss", 1), ("rel_bias", N_BUCKETS * N_HEADS), ("ln_v_gain", d_b), ("ln_v_bias", d_b),
            ("w_spatial", N_GROUPS * BLOCK * BLOCK), ("b_spatial", N_GROUPS * BLOCK), ("ln1_gain", D_MODEL), ("ln1_bias", D_MODEL),
            ("b_ff1", D_FF), ("b_ff2", D_MODEL), ("ln2_gain", D_MODEL), ("ln2_bias", D_MODEL))


def _pack(vals):
    pieces = []
    for name, size in _small_sizes():
        flat = vals[name].reshape(-1).astype(F32)
        padded = -(-size // (SUBLANES * LANES)) * SUBLANES * LANES
        pieces.append(jnp.pad(flat, (0, padded - size)).reshape(-1, LANES))
    return jnp.concatenate(pieces, axis=0)


def _unpack(buf):
    out, row = {}, 0
    for name, size in _small_sizes():
        rows = -(-size // (SUBLANES * LANES)) * SUBLANES
        out[name] = buf[row:row + rows].reshape(-1)[:size]
        row += rows
    return out


def kernel(x, w_in, rel_bias, ln_v_gain, ln_v_bias, w_spatial, b_spatial, w_proj_a, w_proj_b, w_out, ln1_gain, ln1_bias, w_ff1, b_ff1, w_ff2, b_ff2, ln2_gain, ln2_bias, loss_target, m_w_in, m_rel_bias, m_ln_v_gain, m_ln_v_bias, m_w_spatial, m_b_spatial, m_w_proj_a, m_w_proj_b, m_w_out, m_ln1_gain, m_ln1_bias, m_w_ff1, m_b_ff1, m_w_ff2, m_b_ff2, m_ln2_gain, m_ln2_bias, v_w_in, v_rel_bias, v_ln_v_gain, v_ln_v_bias, v_w_spatial, v_b_spatial, v_w_proj_a, v_w_proj_b, v_w_out, v_ln1_gain, v_ln1_bias, v_w_ff1, v_b_ff1, v_w_ff2, v_b_ff2, v_ln2_gain, v_ln2_bias):
    d_a, d_b, d_in = _d_a(), _d_b(), _d_in()
    weights = dict(w_in=w_in, rel_bias=rel_bias, ln_v_gain=ln_v_gain, ln_v_bias=ln_v_bias, w_spatial=w_spatial, b_spatial=b_spatial,
                   w_proj_a=w_proj_a, w_proj_b=w_proj_b, w_out=w_out, ln1_gain=ln1_gain, ln1_bias=ln1_bias, w_ff1=w_ff1, b_ff1=b_ff1,
                   w_ff2=w_ff2, b_ff2=b_ff2, ln2_gain=ln2_gain, ln2_bias=ln2_bias)
    mom1 = dict(w_in=m_w_in, rel_bias=m_rel_bias, ln_v_gain=m_ln_v_gain, ln_v_bias=m_ln_v_bias, w_spatial=m_w_spatial,
                b_spatial=m_b_spatial, w_proj_a=m_w_proj_a, w_proj_b=m_w_proj_b, w_out=m_w_out, ln1_gain=m_ln1_gain,
                ln1_bias=m_ln1_bias, w_ff1=m_w_ff1, b_ff1=m_b_ff1, w_ff2=m_w_ff2, b_ff2=m_b_ff2, ln2_gain=m_ln2_gain, ln2_bias=m_ln2_bias)
    mom2 = dict(w_in=v_w_in, rel_bias=v_rel_bias, ln_v_gain=v_ln_v_gain, ln_v_bias=v_ln_v_bias, w_spatial=v_w_spatial,
                b_spatial=v_b_spatial, w_proj_a=v_w_proj_a, w_proj_b=v_w_proj_b, w_out=v_w_out, ln1_gain=v_ln1_gain,
                ln1_bias=v_ln1_bias, w_ff1=v_w_ff1, b_ff1=v_b_ff1, w_ff2=v_w_ff2, b_ff2=v_b_ff2, ln2_gain=v_ln2_gain, ln2_bias=v_ln2_bias)

    shard = {n: _cast_bf16(f"cast_{n}", weights[n][0]) for n in KINDS}
    full, sent = {}, {n: (0, 0, 0) for n in KINDS}
    for n, kind in KINDS.items():
        r, c = shard[n].shape
        full[n] = lax.empty((r, c * N_DEV) if kind == "col" else (r * N_DEV, c), BF16)

    def keep(table, n):
        def store(outs):
            table[n] = outs[0]
        return store

    def gathering(**new):
        stages = []
        for n in KINDS:
            out, relayed, passed = sent[n]
            units = new.get(n, 0)
            if units or relayed < out or passed < relayed:
                st = _gather_stage(shard[n], full[n], KINDS[n], (out, units) if units else None,
                                   (relayed, out - relayed) if relayed < out else None, (passed, relayed - passed) if passed < relayed else None)
                st.store = keep(full, n)
                sent[n] = (out + units, out, relayed)
                stages.append(st)
        return stages

    def settle(stages, outs):
        for st, o in zip(stages, outs):
            st.store(o)

    def alone(name, stages):
        settle(stages, _comm_only(name, stages))

    def here(n):
        assert sent[n] == (16, 16, 16), (n, sent[n])
        return full[n]

    alone("gather_w_in_near", gathering(w_in=16))
    alone("gather_w_in_relay", gathering())
    alone("gather_w_in_sibling", gathering())

    xs = _to_perm(x[0])
    target = _to_perm(loss_target[0])
    xb = _cast_bf16("cast_x", xs)
    g8 = BLOCK // N_SUB
    ws_t = w_spatial[0].reshape(N_GROUPS, g8, N_SUB, g8, N_SUB).transpose(0, 2, 1, 4, 3).reshape(N_GROUPS, BLOCK, BLOCK)
    bs_t = b_spatial[0].reshape(N_GROUPS, g8, N_SUB).transpose(2, 1, 0).reshape(BLOCK, N_GROUPS)
    idx = _local_index(0)
    causal = jnp.asarray((idx[:, None] >= idx[None, :]).astype(np.float32))
    buckets = jnp.asarray(_bucket_tables())
    bias = _bias_expand(rel_bias, buckets)

    hosted = gathering(w_proj_a=16, w_proj_b=16, w_ff1=1)
    (qkv,), st = _matmul("proj_qkv", xb, here("w_in"), "nn", [F32], n=3 * d_a, stages=hosted)
    settle(hosted, st)
    hosted = gathering(w_out=16, w_ff1=4)
    (rest,), st = _matmul("proj_rest", xb, here("w_in"), "nn", [F32], b_off=3 * d_a, n=d_in - 3 * d_a, stages=hosted)
    settle(hosted, st)
    fwd = []
    for p in range(3):
        hosted = gathering(w_ff1=(3, 5, 3)[p])
        res, st = _attn_fwd(qkv, bias, p, stages=hosted)
        settle(hosted, st)
        fwd.append(res)
    hosted = gathering(w_ff2=1)
    (attn, attn_b, lse), st = _attn_combine([o for o, _ in fwd], [l for _, l in fwd], stages=hosted)
    settle(hosted, st)
    hosted = gathering(w_ff2=2)
    gmlp, st = _gmlp_fwd(rest, ln_v_gain, ln_v_bias, ws_t, bs_t, causal, stages=hosted)
    settle(hosted, st)
    hosted = gathering(w_ff2=3)
    (ya,), st = _matmul("proj_a", attn_b, here("w_proj_a"), "nn", [F32], stages=hosted)
    settle(hosted, st)
    gate_a, gate_b = 2 * d_b, 2 * d_b + D_MODEL

    def merge(acc, ya_, ga, gb):
        return acc, _sigmoid(ga) * ya_ + _sigmoid(gb) * acc

    hosted = gathering(w_ff2=5)
    (yb, merged), st = _matmul("proj_b_merge", gmlp, here("w_proj_b"), "nn", [F32, BF16], merge,
                               [(ya, "mn", 0), (rest, "mn", gate_a), (rest, "mn", gate_b)], tn=256, stages=hosted)
    settle(hosted, st)
    hosted = gathering(w_ff2=3)
    (pre1,), st = _matmul("out_proj", merged, here("w_out"), "nn", [F32], lambda acc, x_: (ALPHA * x_ + acc,), [(xs, "mn", 0)], stages=hosted)
    settle(hosted, st)
    hosted = gathering(w_ff2=2)
    (xhat1, rstd1, h1b), st = _ln1_fwd(pre1, ln1_gain, ln1_bias, stages=hosted)
    settle(hosted, st)

    def relu2(acc, b_):
        r = jnp.maximum(acc + b_, 0.0)
        return r, r * r

    hosted = gathering()
    (relu, fb), st = _matmul("ff1", h1b, here("w_ff1"), "nn", [F32, BF16], relu2, [(b_ff1, "row", 0)], stages=hosted)
    settle(hosted, st)
    alone("gather_w_ff2_sibling", gathering())
    (ff,), _ = _matmul("ff2", fb, here("w_ff2"), "nn", [F32], lambda acc, b_: (acc + b_,), [(b_ff2, "row", 0)], tn=1024, tk=1024)

    core = lax.axis_index("c").astype(I32).reshape(1)
    factors, theirs, sib, pair, chips, transit, reduced = {}, {}, {}, {}, {}, {}, {}

    def grad_for_sibling(n, a, b, stages=()):
        factors[n] = (a, b)
        theirs[n], outs = _dw(f"dw_{n}_sibling", a, b, KINDS[n], core, False, stages=stages)
        settle(stages, outs)

    def to_sibling(n):
        st = _to_sibling_stage(theirs[n])
        st.store = keep(sib, n)
        return st

    def grad_own(n, stages=()):
        pair[n], outs = _dw(f"dw_{n}_own", *factors[n], KINDS[n], core, True, add=sib[n], stages=stages)
        settle(stages, outs)
        chips[n] = lax.empty(pair[n].shape, BF16)
        transit[n] = lax.empty((2, *pair[n].shape[1:]), BF16)
        reduced[n] = (0, 0)

    def reducing(**new):
        stages = []
        for n in list(reduced):
            out, relayed = reduced[n]
            units = new.get(n, 0)
            if units or relayed < out:
                st = _reduce_stage(pair[n], chips[n], transit[n], (out, units) if units else None,
                                   (relayed, out - relayed) if relayed < out else None)

                def store(outs, n=n):
                    chips[n], transit[n] = outs

                st.store = store
                reduced[n] = (out + units, out)
                stages.append(st)
        return stages

    def summed(n):
        assert reduced[n] == (16, 16), (n, reduced[n])
        return chips[n]

    dpre2, dpre2b, g_ln2_gain, g_ln2_bias, g_b_ff2, loss_part = _ln2_loss_bwd(ff, xhat1, ln1_gain, ln1_bias, ln2_gain, ln2_bias, target)
    grad_for_sibling("w_ff2", fb, dpre2b)

    def relu2_bwd(acc, r):
        da = acc * (2.0 * r)
        return da, da

    hosted = [to_sibling("w_ff2")]
    (dab, g_b_ff1), st = _matmul("d_ff1", dpre2b, here("w_ff2"), "nt", [BF16], relu2_bwd, [(relu, "mn", 0)], colsums=(1,), stages=hosted)
    settle(hosted, st)
    grad_own("w_ff2")
    grad_for_sibling("w_ff1", h1b, dab, reducing(w_ff2=6))
    hosted = reducing(w_ff2=10) + [to_sibling("w_ff1")]
    (dh1,), st = _matmul("d_h1", dab, here("w_ff1"), "nt", [F32], lambda acc, d_: (acc + ALPHA * d_,), [(dpre2, "mn", 0)], stages=hosted)
    settle(hosted, st)
    grad_own("w_ff1", reducing())
    hosted = reducing(w_ff1=4)
    (dpre1, dpre1b, g_ln1_gain, g_ln1_bias), st = _ln1_bwd(dh1, xhat1, rstd1, ln1_gain, stages=hosted)
    settle(hosted, st)
    grad_for_sibling("w_out", merged, dpre1b, reducing(w_ff1=2))

    def merge_bwd(acc, ga, gb, ya_, yb_):
        sa, sb = _sigmoid(ga), _sigmoid(gb)
        return acc * sa, acc * sb, acc * ya_ * (sa * (1.0 - sa)), acc * yb_ * (sb * (1.0 - sb))

    hosted = reducing(w_ff1=8) + [to_sibling("w_out")]
    (dya, dyb, dga, dgb), st = _matmul("d_merge", dpre1b, here("w_out"), "nt", [BF16] * 4, merge_bwd,
                                       [(rest, "mn", gate_a), (rest, "mn", gate_b), (ya, "mn", 0), (yb, "mn", 0)], tn=256, stages=hosted)
    settle(hosted, st)
    grad_own("w_out", reducing())
    grad_for_sibling("w_proj_a", attn_b, dya, reducing(w_ff1=2))
    grad_for_sibling("w_proj_b", gmlp, dyb, reducing())
    hosted = [to_sibling("w_proj_a"), to_sibling("w_proj_b")]
    (dattn,), st = _matmul("d_attn", dya, here("w_proj_a"), "nt", [F32], stages=hosted)
    settle(hosted, st)
    grad_own("w_proj_a")
    grad_own("w_proj_b")
    hosted = reducing(w_out=16)
    (dgmlp,), st = _matmul("d_gmlp", dyb, here("w_proj_b"), "nt", [F32], stages=hosted)
    settle(hosted, st)
    hosted = reducing(w_proj_a=16)
    (du, dvb, dws_t, dbs_t, g_lnv_gain, g_lnv_bias), st = _gmlp_bwd(rest, dgmlp, ln_v_gain, ln_v_bias, ws_t, bs_t, causal, stages=hosted)
    settle(hosted, st)
    delta = _attn_delta(dattn, attn)
    bwd = []
    for p in range(3):
        hosted = reducing(w_proj_b=16) if p == 0 else reducing()
        res, st = _attn_bwd(qkv, dattn, lse, delta, bias, p, stages=hosted)
        settle(hosted, st)
        bwd.append(res)
    g_rel_bias = _rel_bias_grad([b[3] for b in bwd], buckets)
    dproj = _assemble_dproj([b[i] for i in range(3) for b in bwd], du, dvb, dga, dgb)

    g_w_spatial = dws_t.reshape(N_GROUPS, N_SUB, g8, N_SUB, g8).transpose(0, 2, 1, 4, 3)
    g_b_spatial = dbs_t[:, :N_GROUPS].reshape(N_SUB, g8, N_GROUPS).transpose(2, 1, 0)
    part = _pack(dict(loss=loss_part, rel_bias=g_rel_bias, ln_v_gain=g_lnv_gain, ln_v_bias=g_lnv_bias, w_spatial=g_w_spatial,
                      b_spatial=g_b_spatial, ln1_gain=g_ln1_gain, ln1_bias=g_ln1_bias, b_ff1=g_b_ff1, b_ff2=g_b_ff2,
                      ln2_gain=g_ln2_gain, ln2_bias=g_ln2_bias))
    small = _small_stage(part)
    small.store = keep(sib, "small")
    grad_for_sibling("w_in", xb, dproj, [small])
    parts = sib["small"]
    half = SEQ // 2

    def add_residual(acc, d_):
        return (acc + ALPHA * d_,)

    hosted = [to_sibling("w_in")]
    (dx0,), st = _matmul("d_x0", dproj, here("w_in"), "nt", [F32], add_residual, [(dpre1, "mn", 0)], tm=1024, tn=1024, tk=3072, m=half,
                         stages=hosted)
    settle(hosted, st)
    grad_own("w_in")
    hosted = reducing(w_in=8)
    (dx1,), st = _matmul("d_x1", dproj, here("w_in"), "nt", [F32], add_residual, [(dpre1, "mn", 0)], tm=1024, tn=1024, tk=3072, m_off=half, m=half,
                         stages=hosted)
    settle(hosted, st)
    grad_x = _from_perm(jnp.concatenate([dx0, dx1], axis=0))[None]

    out_g, out_d, out_m, out_v = {}, {}, {}, {}
    for n, units in (("w_ff2", 4), ("w_ff1", 4), ("w_out", 0), ("w_proj_a", 0), ("w_proj_b", 0), ("w_in", 0)):
        hosted = reducing(w_in=units) if n != "w_in" else []
        (g, d, nm, nv), st = _adam_shard(f"adam_{n}", summed(n), weights[n][0], mom1[n][0], mom2[n][0], stages=hosted)
        settle(hosted, st)
        out_g[n], out_d[n], out_m[n], out_v[n] = g[None], d[None], nm[None], nv[None]

    zero = jnp.zeros((1,), F32)
    sg, sd, sm, sv = (_unpack(b) for b in _adam_small(
        parts, _pack({**weights, "loss": zero}), _pack({**mom1, "loss": zero}), _pack({**mom2, "loss": zero})))
    for n in WEIGHT_ORDER:
        if n not in KINDS:
            shape = weights[n].shape
            out_g[n], out_d[n], out_m[n], out_v[n] = (t[n].reshape(shape) for t in (sg, sd, sm, sv))
    loss = sg["loss"].reshape(())
    return (loss, grad_x, *[out_g[n] for n in WEIGHT_ORDER], *[out_d[n] for n in WEIGHT_ORDER],
            *[out_m[n] for n in WEIGHT_ORDER], *[out_v[n] for n in WEIGHT_ORDER])
```

```python
import math

import jax
import jax.numpy as jnp
import numpy as np
from jax import lax
from jax.experimental import pallas as pl
from jax.experimental.pallas import tpu as pltpu

F32 = jnp.float32
BF16 = jnp.bfloat16
I32 = jnp.int32

SEQ = 2048
D_MODEL = 2048
HEAD_DIM = 128
N_HEADS = 8
N_GROUPS = 8
D_FF = 4 * D_MODEL
BLOCK = 128
DILATIONS = (1, 4, 16)
N_BUCKETS = 32
MAX_DISTANCE = 2048
ALPHA = 2.0 ** 0.25
LN_EPS = 1e-5
NEG_INF = -1e30
N_DEV = 8
N_CHIPS = 4
N_SUB = 16
ADAM_LR, ADAM_B1, ADAM_B2, ADAM_EPS, ADAM_WD, ADAM_STEP = 0.001, 0.9, 0.999, 1e-08, 0.01, 10
LANES = 128
SUBLANES = 8
VMEM_LIMIT = 56 * 1024 * 1024
MESH = pl.DeviceIdType.MESH
ANY = pl.BlockSpec(memory_space=pl.ANY)
WEIGHT_ORDER = ("w_in", "rel_bias", "ln_v_gain", "ln_v_bias", "w_spatial", "b_spatial", "w_proj_a", "w_proj_b", "w_out",
                "ln1_gain", "ln1_bias", "w_ff1", "b_ff1", "w_ff2", "b_ff2", "ln2_gain", "ln2_bias")
KINDS = {"w_in": "col", "w_proj_a": "col", "w_proj_b": "col", "w_out": "row", "w_ff1": "col", "w_ff2": "row"}


def _d_a():
    return N_HEADS * HEAD_DIM


def _d_b():
    return N_GROUPS * BLOCK


def _d_in():
    return 3 * _d_a() + 2 * _d_b() + 2 * D_MODEL


def _pick(t, n, *others):
    if n <= t and all(o % n == 0 for o in others):
        return n
    for c in range(min(t, n) // LANES * LANES, 0, -LANES):
        if n % c == 0 and all(o % c == 0 for o in others):
            return c
    raise ValueError((t, n, others))


class _Stage:
    def __init__(self, ins, outs, alias, sems, start, finish):
        self.ins, self.outs, self.alias, self.sems, self.start, self.finish = ins, outs, alias, sems, start, finish


def _call(name, body, grid, in_specs, out_specs, out_shape, operands, scratch=(), sem=None, stages=(), sequential=False, prefetch=None,
          shown=False):
    n_in, n_out, n_sc = len(in_specs), len(out_specs), len(scratch)
    st_in = [len(s.ins) for s in stages]
    st_out = [len(s.outs) for s in stages]
    st_sem = [len(s.sems) for s in stages]
    n_pre = 0 if prefetch is None else 1
    aliases, ioff, ooff = {}, n_in + n_pre, n_out
    for s, ni, no in zip(stages, st_in, st_out):
        for i, o in s.alias.items():
            aliases[ioff + i] = ooff + o
        ioff, ooff = ioff + ni, ooff + no

    def split(refs, counts):
        out, at = [], 0
        for c in counts:
            out.append(refs[at:at + c])
            at += c
        return out

    def wrapped(*refs):
        ins, sins, outs, souts, sc, ssems = split(refs[n_pre:], [n_in, sum(st_in), n_out, sum(st_out), n_sc, sum(st_sem)])
        parts = list(zip(stages, split(sins, st_in), split(souts, st_out), split(ssems, st_sem)))
        if sequential:
            for s, a, b, c in parts:
                s.start(a, b, c)
                s.finish(a, b, c)
            return
        if parts:
            first = _all_of([pl.program_id(i) == 0 for i in range(len(grid))])
            last = _all_of([pl.program_id(i) == g - 1 for i, g in enumerate(grid)])

            @pl.when(first)
            def _():
                for s, a, b, c in parts:
                    s.start(a, b, c)

        body(*(refs[:n_pre] if shown else ()), *ins, *outs, *sc)
        if parts:
            @pl.when(last)
            def _():
                for s, a, b, c in parts:
                    s.finish(a, b, c)

    if stages or sem is None:
        sem = ("arbitrary",) * len(grid)
    specs = dict(grid=grid, in_specs=list(in_specs) + [ANY] * sum(st_in), out_specs=list(out_specs) + [ANY] * sum(st_out),
                 scratch_shapes=list(scratch) + [x for s in stages for x in s.sems])
    if prefetch is not None:
        specs = dict(grid_spec=pltpu.PrefetchScalarGridSpec(num_scalar_prefetch=1, **specs))
    res = pl.pallas_call(
        wrapped, name=name, out_shape=list(out_shape) + [o for s in stages for o in s.outs], input_output_aliases=aliases,
        compiler_params=pltpu.CompilerParams(dimension_semantics=sem, vmem_limit_bytes=VMEM_LIMIT), **specs,
    )(*([prefetch] if n_pre else []), *operands, *[a for s in stages for a in s.ins])
    res = list(res)
    return res[:n_out], split(res[n_out:], st_out)


def _all_of(conds):
    out = conds[0]
    for c in conds[1:]:
        out = out & c
    return out


def _coords():
    return lax.axis_index("x"), lax.axis_index("y"), lax.axis_index("c")


def _other_chips(x, y):
    return ((1 - x, y), (x, 1 - y), (1 - x, 1 - y))


def _lin(dev):
    return 4 * dev[0] + 2 * dev[1] + dev[2]


def _piece(total, lo, n, units=16):
    assert total % units == 0
    return lo * (total // units), n * (total // units)


FLOWS = 4


def _split(lo, cnt):
    k = next(k for k in (FLOWS, 2, 1) if cnt % (2 * SUBLANES * k) == 0)
    return [(lo + i * (cnt // k), cnt // k) for i in range(k)]


def _remote(src, dst, send, recv, to):
    return pltpu.make_async_remote_copy(src_ref=src, dst_ref=dst, send_sem=send, recv_sem=recv, device_id=to, device_id_type=MESH)


def _placer(kind, n, lo, cnt):
    def place(ref, dev):
        if kind == "col":
            return ref.at[pl.ds(lo, cnt), pl.ds(pl.multiple_of(_lin(dev) * n, LANES), n)]
        return ref.at[pl.ds(pl.multiple_of(_lin(dev) * n + lo, 2 * SUBLANES), cnt), :]
    return place


def _spread_stage(full, kind, piece=(0, 16), home=False):
    n = (full.shape[1] if kind == "col" else full.shape[0]) // N_DEV
    lo, cnt = _piece(full.shape[0] if kind == "col" else n, *piece)
    parts = _split(lo, cnt)
    npeers = 1 if home else 2

    def copies(ins, outs, sems):
        send, recv = sems
        x, y, c = _coords()
        me = (x, y, c)
        peers = [(x, y, 1 - c)] if home else [(1 - x, y, c), (x, 1 - y, c)]
        out, arrive = [], []
        for k, t in enumerate(peers):
            for i, (plo, pcnt) in enumerate(parts):
                place = _placer(kind, n, plo, pcnt)
                out.append(_remote(place(outs[0], me), place(outs[0], me), send.at[i, k], recv.at[i, k], t))
                arrive.append(_remote(place(outs[0], t), place(outs[0], t), send.at[i, k], recv.at[i, k], t))
        return out, arrive

    def start(ins, outs, sems):
        for cp in copies(ins, outs, sems)[0]:
            cp.start()

    def finish(ins, outs, sems):
        out, arrive = copies(ins, outs, sems)
        for cp in arrive:
            cp.wait_recv()
        for cp in out:
            cp.wait_send()

    return _Stage([full], [jax.ShapeDtypeStruct(full.shape, full.dtype)], {0: 0},
                  [pltpu.SemaphoreType.DMA((len(parts), npeers)), pltpu.SemaphoreType.DMA((len(parts), npeers))], start, finish)


def _relay_stage(full, kind, piece=(0, 16)):
    n = (full.shape[1] if kind == "col" else full.shape[0]) // N_DEV
    lo, cnt = _piece(full.shape[0] if kind == "col" else n, *piece)
    half = cnt // 2
    assert half % (2 * SUBLANES) == 0, (cnt, kind)
    tops, bottoms = _split(lo, half), _split(lo + half, half)

    def copies(ins, outs, sems):
        send, recv = sems
        x, y, c = _coords()
        xn, yn, dg = (1 - x, y, c), (x, 1 - y, c), (1 - x, 1 - y, c)
        out, arrive, k = [], [], 0
        for came_from, to, parts in ((yn, xn, tops), (xn, yn, bottoms)):
            for plo, pcnt in parts:
                place = _placer(kind, n, plo, pcnt)
                out.append(_remote(place(outs[0], came_from), place(outs[0], came_from), send.at[k], recv.at[k], to))
                arrive.append(_remote(place(outs[0], dg), place(outs[0], dg), send.at[k], recv.at[k], to))
                k += 1
        return out, arrive

    def start(ins, outs, sems):
        for cp in copies(ins, outs, sems)[0]:
            cp.start()

    def finish(ins, outs, sems):
        out, arrive = copies(ins, outs, sems)
        for cp in arrive:
            cp.wait_recv()
        for cp in out:
            cp.wait_send()

    return _Stage([full], [jax.ShapeDtypeStruct(full.shape, full.dtype)], {0: 0},
                  [pltpu.SemaphoreType.DMA((len(tops) + len(bottoms),)), pltpu.SemaphoreType.DMA((len(tops) + len(bottoms),))], start, finish)


def _forward_stage(full, kind, piece=(0, 16)):
    n = (full.shape[1] if kind == "col" else full.shape[0]) // N_DEV
    lo, cnt = _piece(full.shape[0] if kind == "col" else n, *piece)
    place = _placer(kind, n, lo, cnt)

    def copies(ins, outs, sems):
        send, recv = sems
        x, y, c = _coords()
        chips = _other_chips(x, y)
        out = [_remote(place(outs[0], (*chip, c)), place(outs[0], (*chip, c)), send.at[k], recv.at[k], (x, y, 1 - c)) for k, chip in enumerate(chips)]
        arrive = [_remote(place(outs[0], (*chip, 1 - c)), place(outs[0], (*chip, 1 - c)), send.at[k], recv.at[k], (x, y, 1 - c))
                  for k, chip in enumerate(chips)]
        return out, arrive

    def start(ins, outs, sems):
        for cp in copies(ins, outs, sems)[0]:
            cp.start()

    def finish(ins, outs, sems):
        out, arrive = copies(ins, outs, sems)
        for cp in arrive:
            cp.wait_recv()
        for cp in out:
            cp.wait_send()

    return _Stage([full], [jax.ShapeDtypeStruct(full.shape, full.dtype)], {0: 0},
                  [pltpu.SemaphoreType.DMA((3,)), pltpu.SemaphoreType.DMA((3,))], start, finish)


def _to_sibling_stage(theirs):
    def copies(ins, outs, sems):
        send, recv = sems
        x, y, c = _coords()
        return [_remote(ins[0].at[q], outs[0].at[q], send.at[q], recv.at[q], (x, y, 1 - c)) for q in range(N_CHIPS)]

    def start(ins, outs, sems):
        for cp in copies(ins, outs, sems):
            cp.start()

    def finish(ins, outs, sems):
        for cp in copies(ins, outs, sems):
            cp.wait()

    return _Stage([theirs], [jax.ShapeDtypeStruct(theirs.shape, BF16)], {},
                  [pltpu.SemaphoreType.DMA((N_CHIPS,)), pltpu.SemaphoreType.DMA((N_CHIPS,))], start, finish)


def _to_chips_stage(pair, dst, transit, piece=(0, 16)):
    lo, cnt = _piece(pair.shape[1], *piece)
    half = cnt // 2
    assert half % (2 * SUBLANES) == 0, cnt
    whole, tops, bottoms = _split(lo, cnt), _split(lo, half), _split(lo + half, half)
    nsem = 2 * len(whole) + len(tops) + len(bottoms)

    def copies(ins, outs, sems):
        send, recv = sems
        x, y, c = _coords()
        mine, qx, qy, qd = 2 * x + y, 2 * (1 - x) + y, 2 * x + 1 - y, 2 * (1 - x) + 1 - y
        xn, yn = (1 - x, y, c), (x, 1 - y, c)
        pair_ref, (dst_ref, transit_ref) = ins[0], outs
        out, arrive, k = [], [], 0
        for q, to, parts in ((qx, xn, whole), (qy, yn, whole)):
            for plo, pcnt in parts:
                rows = pl.ds(plo, pcnt)
                out.append(_remote(pair_ref.at[q, rows, :], dst_ref.at[mine, rows, :], send.at[k], recv.at[k], to))
                arrive.append(_remote(pair_ref.at[q, rows, :], dst_ref.at[q, rows, :], send.at[k], recv.at[k], to))
                k += 1
        for slot, to, parts in ((0, xn, tops), (1, yn, bottoms)):
            for plo, pcnt in parts:
                rows = pl.ds(plo, pcnt)
                out.append(_remote(pair_ref.at[qd, rows, :], transit_ref.at[slot, rows, :], send.at[k], recv.at[k], to))
                arrive.append(_remote(pair_ref.at[qd, rows, :], transit_ref.at[slot, rows, :], send.at[k], recv.at[k], to))
                k += 1
        return out, arrive

    def start(ins, outs, sems):
        for cp in copies(ins, outs, sems)[0]:
            cp.start()

    def finish(ins, outs, sems):
        out, arrive = copies(ins, outs, sems)
        for cp in arrive:
            cp.wait_recv()
        for cp in out:
            cp.wait_send()

    return _Stage([pair, dst, transit], [jax.ShapeDtypeStruct(dst.shape, dst.dtype), jax.ShapeDtypeStruct(transit.shape, transit.dtype)],
                  {1: 0, 2: 1}, [pltpu.SemaphoreType.DMA((nsem,)), pltpu.SemaphoreType.DMA((nsem,))], start, finish)


def _to_chips_relay_stage(dst, transit, piece=(0, 16)):
    lo, cnt = _piece(dst.shape[1], *piece)
    half = cnt // 2
    tops, bottoms = _split(lo, half), _split(lo + half, half)

    def copies(ins, outs, sems):
        send, recv = sems
        x, y, c = _coords()
        qx, qy, qd = 2 * (1 - x) + y, 2 * x + 1 - y, 2 * (1 - x) + 1 - y
        xn, yn = (1 - x, y, c), (x, 1 - y, c)
        transit_ref, dst_ref = ins[1], outs[0]
        out, arrive, k = [], [], 0
        for slot, q, to, parts in ((0, qx, yn, tops), (1, qy, xn, bottoms)):
            for plo, pcnt in parts:
                rows = pl.ds(plo, pcnt)
                out.append(_remote(transit_ref.at[slot, rows, :], dst_ref.at[q, rows, :], send.at[k], recv.at[k], to))
                arrive.append(_remote(transit_ref.at[slot, rows, :], dst_ref.at[qd, rows, :], send.at[k], recv.at[k], to))
                k += 1
        return out, arrive

    def start(ins, outs, sems):
        for cp in copies(ins, outs, sems)[0]:
            cp.start()

    def finish(ins, outs, sems):
        out, arrive = copies(ins, outs, sems)
        for cp in arrive:
            cp.wait_recv()
        for cp in out:
            cp.wait_send()

    return _Stage([dst, transit], [jax.ShapeDtypeStruct(dst.shape, dst.dtype)], {0: 0},
                  [pltpu.SemaphoreType.DMA((len(tops) + len(bottoms),)), pltpu.SemaphoreType.DMA((len(tops) + len(bottoms),))], start, finish)


def _fuse(parts, ins, outs, alias):
    parts = [p for p in parts if p is not None]
    sems = [x for st, _, _ in parts for x in st.sems]

    def run(which):
        def go(i, o, s):
            refs, at = list(i) + list(o), 0
            for st, pi, po in parts:
                getattr(st, which)([refs[k] for k in pi], [refs[k] for k in po], s[at:at + len(st.sems)])
                at += len(st.sems)
        return go

    return _Stage(ins, [jax.ShapeDtypeStruct(o.shape, o.dtype) for o in outs], alias, sems, run("start"), run("finish"))


def _gather_stage(full, kind, new=None, relay=None, forward=None):
    return _fuse([(_spread_stage(full, kind, new), [0], [1]) if new else None,
                  (_relay_stage(full, kind, relay), [0], [1]) if relay else None,
                  (_spread_stage(full, kind, relay, home=True), [0], [1]) if relay else None,
                  (_forward_stage(full, kind, forward), [0], [1]) if forward else None], [full], [full], {0: 0})


def _reduce_stage(pair, dst, transit, new=None, relay=None):
    return _fuse([(_to_chips_stage(pair, dst, transit, new), [0, 1, 2], [3, 4]) if new else None,
                  (_to_chips_relay_stage(dst, transit, relay), [3, 4], [3]) if relay else None], [pair, dst, transit], [dst, transit],
                 {1: 0, 2: 1})


def _small_stage(part):
    def copies(ins, outs, sems):
        send, recv = sems
        x, y, c = _coords()
        me = (x, y, c)
        peers = [(1 - x if k & 4 else x, 1 - y if k & 2 else y, 1 - c if k & 1 else c) for k in range(1, N_DEV)]
        out = [_remote(ins[0], outs[0].at[_lin(me)], send.at[k], recv.at[k], t) for k, t in enumerate(peers)]
        arrive = [_remote(ins[0], outs[0].at[_lin(t)], send.at[k], recv.at[k], t) for k, t in enumerate(peers)]
        return out, arrive

    def start(ins, outs, sems):
        for cp in copies(ins, outs, sems)[0]:
            cp.start()

    def finish(ins, outs, sems):
        out, arrive = copies(ins, outs, sems)
        for cp in arrive:
            cp.wait_recv()
        for cp in out:
            cp.wait_send()

    return _Stage([part], [jax.ShapeDtypeStruct((N_DEV, *part.shape), F32)], {},
                  [pltpu.SemaphoreType.DMA((N_DEV - 1,)), pltpu.SemaphoreType.DMA((N_DEV - 1,))], start, finish)


def _comm_only(name, stages):
    return _call(name, lambda: None, (1,), [], [], [], [], stages=stages, sequential=True)[1]


_GELU_C = math.sqrt(2.0 / math.pi)


def _gelu(x):
    return 0.5 * x * (1.0 + jnp.tanh(_GELU_C * (x + 0.044715 * x * x * x)))


def _gelu_grad(x):
    t = jnp.tanh(_GELU_C * (x + 0.044715 * x * x * x))
    return 0.5 * (1.0 + t) + 0.5 * x * (1.0 - t * t) * (_GELU_C * (1.0 + 3.0 * 0.044715 * x * x))


def _sigmoid(x):
    return 1.0 / (1.0 + jnp.exp(-x))


def _dot(a, b, mode):
    dims = {"nn": (((1,), (0,)), ((), ())), "nt": (((1,), (1,)), ((), ())), "tn": (((0,), (0,)), ((), ()))}[mode]
    return lax.dot_general(a.astype(BF16), b.astype(BF16), dims, preferred_element_type=F32)


def _matmul(name, a, b, mode, outs, epi=None, extras=(), colsums=(), tm=2048, tn=512, tk=2048, b_off=0, n=None, m_off=0, m=None, stages=()):
    if mode == "tn":
        kk, mfull = a.shape
    else:
        mfull, kk = a.shape
    m = mfull if m is None else m
    n = (b.shape[0] if mode == "nt" else b.shape[1]) if n is None else n
    tm, tk = _pick(tm, m, m_off), _pick(tk, kk)
    tn = _pick(tn, n, b_off, *[off for _, _, off in extras])
    boff, moff = b_off // tn, m_off // tm
    nm, nn_, nk = m // tm, n // tn, kk // tk
    col_major = bool(colsums)
    grid = (nn_, nm, nk) if col_major else (nm, nn_, nk)

    def imap(f):
        if col_major:
            return lambda g0, g1, k: f(g1, g0, k)
        return f

    a_spec = (pl.BlockSpec((tk, tm), imap(lambda i, j, k: (k, i + moff))) if mode == "tn"
              else pl.BlockSpec((tm, tk), imap(lambda i, j, k: (i + moff, k))))
    b_spec = (pl.BlockSpec((tn, tk), imap(lambda i, j, k: (j + boff, k))) if mode == "nt"
              else pl.BlockSpec((tk, tn), imap(lambda i, j, k: (k, j + boff))))
    in_specs, operands = [a_spec, b_spec], [a, b]
    for arr, kind, off in extras:
        o = off // tn
        if kind == "mn":
            in_specs.append(pl.BlockSpec((tm, tn), imap(lambda i, j, k, o=o: (i + moff, j + o))))
        else:
            in_specs.append(pl.BlockSpec((1, tn), imap(lambda i, j, k, o=o: (0, j + o))))
        operands.append(arr)
    out_shape = [jax.ShapeDtypeStruct((m, n), dt) for dt in outs] + [jax.ShapeDtypeStruct((1, n), F32) for _ in colsums]
    out_specs = ([pl.BlockSpec((tm, tn), imap(lambda i, j, k: (i, j))) for _ in outs]
                 + [pl.BlockSpec((1, tn), imap(lambda i, j, k: (0, j))) for _ in colsums])
    n_ex, n_out, n_cs = len(extras), len(outs), len(colsums)

    def body(*refs):
        a_ref, b_ref = refs[:2]
        ex_refs = refs[2:2 + n_ex]
        out_refs = refs[2 + n_ex:2 + n_ex + n_out]
        cs_refs = refs[2 + n_ex + n_out:2 + n_ex + n_out + n_cs]
        part = _dot(a_ref[...], b_ref[...], mode)

        def finish(acc):
            res = epi(acc, *[r[...] for r in ex_refs]) if epi is not None else (acc,)
            for r, v in zip(out_refs, res[:n_out]):
                r[...] = v.astype(r.dtype)
            if n_cs:
                @pl.when(pl.program_id(1) == 0)
                def _():
                    for r in cs_refs:
                        r[...] = jnp.zeros_like(r)

                for r, idx in zip(cs_refs, colsums):
                    r[...] += jnp.sum(res[idx], axis=0, keepdims=True)

        if nk == 1:
            finish(part)
        else:
            acc_ref = refs[-1]
            k = pl.program_id(2)

            @pl.when(k == 0)
            def _():
                acc_ref[...] = part

            @pl.when(k > 0)
            def _():
                acc_ref[...] += part

            @pl.when(k == nk - 1)
            def _():
                finish(acc_ref[...])

    sem = ("arbitrary", "arbitrary", "arbitrary") if col_major else ("parallel", "parallel", "arbitrary")
    return _call(name, body, grid, in_specs, out_specs, out_shape, operands,
                 scratch=[pltpu.VMEM((tm, tn), F32)] if nk > 1 else [], sem=sem, stages=stages)


def _row_spec(tr, c):
    return pl.BlockSpec((tr, c), lambda i: (i, 0))


def _fix_spec(shape):
    return pl.BlockSpec(shape, lambda *_: tuple(0 for _ in shape))


def _cast_bf16(name, x, tr=512):
    r, c = x.shape
    tr = _pick(tr, r)

    def body(x_ref, o_ref):
        o_ref[...] = x_ref[...].astype(BF16)

    return _call(name, body, (r // tr,), [_row_spec(tr, c)], [_row_spec(tr, c)], [jax.ShapeDtypeStruct((r, c), BF16)], [x],
                 sem=("parallel",))[0][0]


def _cast_into_place(name, w, kind, me, tr=512):
    r, c = w.shape
    tr = _pick(tr, r)
    nb = r // tr
    if kind == "col":
        o_spec = pl.BlockSpec((tr, c), lambda i, me_ref: (i, me_ref[0]))
        shape = (r, c * N_DEV)
    else:
        o_spec = pl.BlockSpec((tr, c), lambda i, me_ref: (me_ref[0] * nb + i, 0))
        shape = (r * N_DEV, c)

    def body(x_ref, o_ref):
        o_ref[...] = x_ref[...].astype(BF16)

    return _call(name, body, (nb,), [pl.BlockSpec((tr, c), lambda i, me_ref: (i, 0))], [o_spec], [jax.ShapeDtypeStruct(shape, BF16)], [w],
                 sem=("parallel",), prefetch=me)[0][0]


def _layer_norm_stats(x):
    mean = jnp.mean(x, axis=-1, keepdims=True)
    xc = x - mean
    var = jnp.mean(xc * xc, axis=-1, keepdims=True)
    rstd = lax.rsqrt(var + LN_EPS)
    return xc * rstd, rstd


def _layer_norm_bwd(dxhat, xhat, rstd):
    m1 = jnp.mean(dxhat, axis=-1, keepdims=True)
    m2 = jnp.mean(dxhat * xhat, axis=-1, keepdims=True)
    return rstd * (dxhat - m1 - xhat * m2)


def _ln1_fwd(pre1, g1, b1, tr=256, stages=()):
    s, d = pre1.shape
    tr = _pick(tr, s)

    def body(p_ref, g_ref, b_ref, xh_ref, rs_ref, h_ref):
        xhat, rstd = _layer_norm_stats(p_ref[...])
        xh_ref[...] = xhat
        rs_ref[...] = rstd
        h_ref[...] = (xhat * g_ref[...] + b_ref[...]).astype(BF16)

    return _call("ln1_fwd", body, (s // tr,), [_row_spec(tr, d), _fix_spec((1, d)), _fix_spec((1, d))],
                 [_row_spec(tr, d), _row_spec(tr, 1), _row_spec(tr, d)],
                 [jax.ShapeDtypeStruct((s, d), F32), jax.ShapeDtypeStruct((s, 1), F32), jax.ShapeDtypeStruct((s, d), BF16)],
                 [pre1, g1, b1], sem=("parallel",), stages=stages)


def _ln2_loss_bwd(ff, xhat1, g1, b1, g2, b2, target, tr=256):
    s, d = ff.shape
    tr = _pick(tr, s)

    def body(ff_ref, xh1_ref, g1_ref, b1_ref, g2_ref, b2_ref, t_ref, dp_ref, dpb_ref, dg_ref, db_ref, dbf_ref, loss_ref):
        @pl.when(pl.program_id(0) == 0)
        def _():
            dg_ref[...] = jnp.zeros_like(dg_ref)
            db_ref[...] = jnp.zeros_like(db_ref)
            dbf_ref[...] = jnp.zeros_like(dbf_ref)
            loss_ref[...] = jnp.zeros_like(loss_ref)

        h1 = xh1_ref[...] * g1_ref[...] + b1_ref[...]
        xhat, rstd = _layer_norm_stats(ALPHA * h1 + ff_ref[...])
        err = xhat * g2_ref[...] + b2_ref[...] - t_ref[...]
        row = jnp.mean(err * err, axis=-1, keepdims=True)
        loss_ref[...] += 0.5 * jnp.sum(row, axis=0, keepdims=True)
        dy = err / d
        dg_ref[...] += jnp.sum(dy * xhat, axis=0, keepdims=True)
        db_ref[...] += jnp.sum(dy, axis=0, keepdims=True)
        dpre = _layer_norm_bwd(dy * g2_ref[...], xhat, rstd)
        dbf_ref[...] += jnp.sum(dpre, axis=0, keepdims=True)
        dp_ref[...] = dpre
        dpb_ref[...] = dpre.astype(BF16)

    vec = _fix_spec((1, d))
    return _call("ln2_loss_bwd", body, (s // tr,), [_row_spec(tr, d), _row_spec(tr, d), vec, vec, vec, vec, _row_spec(tr, d)],
                 [_row_spec(tr, d), _row_spec(tr, d), vec, vec, vec, _fix_spec((1, 1))],
                 [jax.ShapeDtypeStruct((s, d), F32), jax.ShapeDtypeStruct((s, d), BF16)]
                 + [jax.ShapeDtypeStruct((1, d), F32)] * 3 + [jax.ShapeDtypeStruct((1, 1), F32)],
                 [ff, xhat1, g1, b1, g2, b2, target])[0]


def _ln1_bwd(dh1, xhat1, rstd1, g1, tr=256, stages=()):
    s, d = dh1.shape
    tr = _pick(tr, s)

    def body(dh_ref, xh_ref, rs_ref, g_ref, dp_ref, dpb_ref, dg_ref, db_ref):
        @pl.when(pl.program_id(0) == 0)
        def _():
            dg_ref[...] = jnp.zeros_like(dg_ref)
            db_ref[...] = jnp.zeros_like(db_ref)

        dh, xhat = dh_ref[...], xh_ref[...]
        dg_ref[...] += jnp.sum(dh * xhat, axis=0, keepdims=True)
        db_ref[...] += jnp.sum(dh, axis=0, keepdims=True)
        dpre = _layer_norm_bwd(dh * g_ref[...], xhat, rs_ref[...])
        dp_ref[...] = dpre
        dpb_ref[...] = dpre.astype(BF16)

    vec = _fix_spec((1, d))
    return _call("ln1_bwd", body, (s // tr,), [_row_spec(tr, d), _row_spec(tr, d), _row_spec(tr, 1), vec],
                 [_row_spec(tr, d), _row_spec(tr, d), vec, vec],
                 [jax.ShapeDtypeStruct((s, d), F32), jax.ShapeDtypeStruct((s, d), BF16)] + [jax.ShapeDtypeStruct((1, d), F32)] * 2,
                 [dh1, xhat1, rstd1, g1], stages=stages)


def _to_perm(x):
    return x.reshape(SEQ // N_SUB, N_SUB, -1).transpose(1, 0, 2).reshape(SEQ, -1)


def _from_perm(x):
    return x.reshape(N_SUB, SEQ // N_SUB, -1).transpose(1, 0, 2).reshape(SEQ, -1)


def _local_index(p):
    rho = np.arange(BLOCK)
    if p == 0:
        return 16 * (rho % 8) + rho // 8
    if p == 1:
        return 4 * (rho % 32) + rho // 32
    return rho


def _tile_view(x, p):
    c = x.shape[1]
    if p == 1:
        return x.reshape(4, 4, BLOCK, c)
    return x.reshape(N_SUB, BLOCK, c)


def _view_shape(c, p):
    return (4, 4, BLOCK, c) if p == 1 else (N_SUB, BLOCK, c)


def _tile_spec(p, width, col, shift=0):
    nblk = SEQ // DILATIONS[p] // BLOCK

    def blk(n):
        return jnp.clip(n + shift, 0, nblk - 1)

    if p == 0:
        return pl.BlockSpec((N_SUB, SUBLANES, width), lambda s, n: (0, blk(n), col))
    if p == 1:
        return pl.BlockSpec((4, None, 32, width), lambda s, n: (0, s, blk(n), col))
    return pl.BlockSpec((None, BLOCK, width), lambda s, n: (s, 0, col))


def _tile_grid(p):
    return ((1, 16), (4, 4), (16, 1))[p]


def _t5_bucket(n):
    max_exact = N_BUCKETS // 2
    nf = np.maximum(n, 1).astype(np.float32)
    large = max_exact + (np.log(nf / np.float32(max_exact)) / np.float32(math.log(MAX_DISTANCE / max_exact))
                         * np.float32(N_BUCKETS - max_exact)).astype(np.int32)
    large = np.minimum(large, N_BUCKETS - 1)
    return np.where(n < max_exact, n, large).astype(np.int32)


def _bucket_tables():
    tabs = np.zeros((3, 2, BLOCK, BLOCK), np.int32)
    for p, d in enumerate(DILATIONS):
        i = _local_index(p)
        diff = i[:, None] - i[None, :]
        tabs[p, 0] = np.where(diff <= 0, _t5_bucket((BLOCK + diff) * d), -1)
        tabs[p, 1] = np.where(diff >= 0, _t5_bucket(np.maximum(diff, 0) * d), -1)
    return tabs


def _bias_expand(rel_bias, buckets):
    nh = N_HEADS

    def body(rb_ref, bk_ref, o_ref):
        for w in range(2):
            bk = bk_ref[0, w]
            for h in range(nh):
                val = jnp.zeros((BLOCK, BLOCK), F32)
                for b in range(N_BUCKETS):
                    val = jnp.where(bk == b, rb_ref[b, h], val)
                o_ref[0, h, w] = jnp.where(bk < 0, NEG_INF, val)

    return _call("bias_expand", body, (3,),
                 [pl.BlockSpec(memory_space=pltpu.SMEM), pl.BlockSpec((1, 2, BLOCK, BLOCK), lambda p: (p, 0, 0, 0))],
                 [pl.BlockSpec((1, nh, 2, BLOCK, BLOCK), lambda p: (p, 0, 0, 0, 0))],
                 [jax.ShapeDtypeStruct((3, nh, 2, BLOCK, BLOCK), F32)], [rel_bias, buckets], sem=("parallel",))[0][0]


def _heads_to_lanes(cols):
    lane = lax.broadcasted_iota(I32, (BLOCK, LANES), 1)
    out = jnp.zeros((BLOCK, LANES), F32)
    for h, c in enumerate(cols):
        out = jnp.where(lane == h, c, out)
    return out


def _attn_fwd(qkv, bias, p, stages=()):
    d_a = _d_a()
    has_prev = SEQ // DILATIONS[p] // BLOCK > 1
    scale = HEAD_DIM ** -0.5
    view = _tile_view(qkv, p)

    def body(q_ref, kc_ref, kp_ref, vc_ref, vp_ref, b_ref, o_ref, l_ref):
        n = pl.program_id(1)
        q_all = q_ref[...].reshape(BLOCK, d_a).astype(BF16)
        kc_all = kc_ref[...].reshape(BLOCK, d_a).astype(BF16)
        vc_all = vc_ref[...].reshape(BLOCK, d_a).astype(BF16)
        if has_prev:
            kp_all = kp_ref[...].reshape(BLOCK, d_a).astype(BF16)
            vp_all = vp_ref[...].reshape(BLOCK, d_a).astype(BF16)
        outs, lses = [], []
        for h in range(N_HEADS):
            sl = slice(h * HEAD_DIM, (h + 1) * HEAD_DIM)
            q = q_all[:, sl]
            sc = _dot(q, kc_all[:, sl], "nt") * scale + b_ref[0, h, 1]
            m = jnp.max(sc, axis=-1, keepdims=True)
            if has_prev:
                sp = _dot(q, kp_all[:, sl], "nt") * scale + b_ref[0, h, 0]
                sp = jnp.where(n > 0, sp, NEG_INF)
                m = jnp.maximum(m, jnp.max(sp, axis=-1, keepdims=True))
                pp = jnp.exp(sp - m)
            pc = jnp.exp(sc - m)
            den = jnp.sum(pc, axis=-1, keepdims=True)
            o = _dot(pc, vc_all[:, sl], "nn")
            if has_prev:
                den = den + jnp.sum(pp, axis=-1, keepdims=True)
                o = o + _dot(pp, vp_all[:, sl], "nn")
            outs.append(o / den)
            lses.append(m + jnp.log(den))
        o_ref[...] = jnp.concatenate(outs, axis=-1).reshape(o_ref.shape)
        l_ref[...] = _heads_to_lanes(lses).reshape(l_ref.shape)

    (o, l), st = _call(
        f"attn_fwd{p}", body, _tile_grid(p),
        [_tile_spec(p, d_a, 0), _tile_spec(p, d_a, 1), _tile_spec(p, d_a, 1, -1), _tile_spec(p, d_a, 2), _tile_spec(p, d_a, 2, -1),
         pl.BlockSpec((1, N_HEADS, 2, BLOCK, BLOCK), lambda s, n: (p, 0, 0, 0, 0))],
        [_tile_spec(p, d_a, 0), _tile_spec(p, LANES, 0)],
        [jax.ShapeDtypeStruct(_view_shape(d_a, p), F32), jax.ShapeDtypeStruct(_view_shape(LANES, p), F32)],
        [view, view, view, view, view, bias], sem=("parallel", "parallel"), stages=stages)
    return (o.reshape(SEQ, d_a), l.reshape(SEQ, LANES)), st


def _attn_combine(os_, ls_, tr=256, stages=()):
    d_a = _d_a()
    tr = _pick(tr, SEQ)

    def body(o0, o1, o2, l0, l1, l2, a_ref, ab_ref, lt_ref):
        l = [l0[...], l1[...], l2[...]]
        m = jnp.maximum(jnp.maximum(l[0], l[1]), l[2])
        w = [jnp.exp(x - m) for x in l]
        tot = w[0] + w[1] + w[2]
        lt_ref[...] = m + jnp.log(tot)
        w = [x / tot for x in w]
        for h in range(N_HEADS):
            sl = slice(h * HEAD_DIM, (h + 1) * HEAD_DIM)
            acc = w[0][:, h:h + 1] * o0[:, sl] + w[1][:, h:h + 1] * o1[:, sl] + w[2][:, h:h + 1] * o2[:, sl]
            a_ref[:, sl] = acc
            ab_ref[:, sl] = acc.astype(BF16)

    return _call("attn_combine", body, (SEQ // tr,), [_row_spec(tr, d_a)] * 3 + [_row_spec(tr, LANES)] * 3,
                 [_row_spec(tr, d_a), _row_spec(tr, d_a), _row_spec(tr, LANES)],
                 [jax.ShapeDtypeStruct((SEQ, d_a), F32), jax.ShapeDtypeStruct((SEQ, d_a), BF16), jax.ShapeDtypeStruct((SEQ, LANES), F32)],
                 [*os_, *ls_], sem=("parallel",), stages=stages)


def _attn_delta(dattn, attn, tr=256):
    d_a = _d_a()
    tr = _pick(tr, SEQ)

    def body(d_ref, a_ref, o_ref):
        prod = d_ref[...] * a_ref[...]
        lane = lax.broadcasted_iota(I32, (tr, LANES), 1)
        out = jnp.zeros((tr, LANES), F32)
        for h in range(N_HEADS):
            out = jnp.where(lane == h, jnp.sum(prod[:, h * HEAD_DIM:(h + 1) * HEAD_DIM], axis=-1, keepdims=True), out)
        o_ref[...] = out

    return _call("attn_delta", body, (SEQ // tr,), [_row_spec(tr, d_a)] * 2, [_row_spec(tr, LANES)],
                 [jax.ShapeDtypeStruct((SEQ, LANES), F32)], [dattn, attn], sem=("parallel",))[0][0]


def _attn_bwd(qkv, dattn, lse, delta, bias, p, stages=()):
    d_a = _d_a()
    nblk = SEQ // DILATIONS[p] // BLOCK
    has_next = nblk > 1
    scale = HEAD_DIM ** -0.5
    qv, dov, lv, tv = (_tile_view(x, p) for x in (qkv, dattn, lse, delta))

    def body(q_ref, qn_ref, k_ref, v_ref, do_ref, don_ref, l_ref, ln_ref, t_ref, tn_ref, b_ref, dq_ref, dk_ref, dv_ref, db_ref, carry_ref):
        j = pl.program_id(1)

        @pl.when((pl.program_id(0) == 0) & (j == 0))
        def _():
            db_ref[...] = jnp.zeros_like(db_ref)

        k_all = k_ref[...].reshape(BLOCK, d_a).astype(BF16)
        v_all = v_ref[...].reshape(BLOCK, d_a).astype(BF16)

        def side(qr, dor, lr, tr_, w):
            q_all = qr[...].reshape(BLOCK, d_a).astype(BF16)
            do_all = dor[...].reshape(BLOCK, d_a).astype(BF16)
            l_all = lr[...].reshape(BLOCK, LANES)
            t_all = tr_[...].reshape(BLOCK, LANES)
            dqs, dks, dvs = [], [], []
            for h in range(N_HEADS):
                sl = slice(h * HEAD_DIM, (h + 1) * HEAD_DIM)
                s = _dot(q_all[:, sl], k_all[:, sl], "nt") * scale + b_ref[0, h, w]
                pr = jnp.exp(s - l_all[:, h:h + 1])
                dp = _dot(do_all[:, sl], v_all[:, sl], "nt")
                ds = pr * (dp - t_all[:, h:h + 1])
                db_ref[h, w] += ds
                dqs.append(_dot(ds, k_all[:, sl], "nn") * scale)
                dks.append(_dot(ds, q_all[:, sl], "tn") * scale)
                dvs.append(_dot(pr, do_all[:, sl], "tn"))
            return [jnp.concatenate(x, axis=-1) for x in (dqs, dks, dvs)]

        dq_c, dk_c, dv_c = side(q_ref, do_ref, l_ref, t_ref, 1)
        if has_next:
            dq_ref[...] = (jnp.where(j > 0, carry_ref[...], 0.0) + dq_c).reshape(dq_ref.shape)
            not_last = j < nblk - 1

            @pl.when(not_last)
            def _():
                dq_n, dk_n, dv_n = side(qn_ref, don_ref, ln_ref, tn_ref, 0)
                carry_ref[...] = dq_n
                dk_ref[...] = (dk_c + dk_n).reshape(dk_ref.shape)
                dv_ref[...] = (dv_c + dv_n).reshape(dv_ref.shape)

            @pl.when(jnp.logical_not(not_last))
            def _():
                dk_ref[...] = dk_c.reshape(dk_ref.shape)
                dv_ref[...] = dv_c.reshape(dv_ref.shape)
        else:
            dq_ref[...] = dq_c.reshape(dq_ref.shape)
            dk_ref[...] = dk_c.reshape(dk_ref.shape)
            dv_ref[...] = dv_c.reshape(dv_ref.shape)

    def big(col, shift=0):
        return _tile_spec(p, d_a, col, shift)

    def small(shift=0):
        return _tile_spec(p, LANES, 0, shift)

    (dq, dk, dv, dbias), st = _call(
        f"attn_bwd{p}", body, _tile_grid(p),
        [big(0), big(0, 1), big(1), big(2), big(0), big(0, 1), small(), small(1), small(), small(1),
         pl.BlockSpec((1, N_HEADS, 2, BLOCK, BLOCK), lambda s, n: (p, 0, 0, 0, 0))],
        [big(0), big(0), big(0), pl.BlockSpec((N_HEADS, 2, BLOCK, BLOCK), lambda s, n: (0, 0, 0, 0))],
        [jax.ShapeDtypeStruct(_view_shape(d_a, p), F32)] * 3 + [jax.ShapeDtypeStruct((N_HEADS, 2, BLOCK, BLOCK), F32)],
        [qv, qv, qv, qv, dov, dov, lv, lv, tv, tv, bias], scratch=[pltpu.VMEM((BLOCK, d_a), F32)], stages=stages)
    return (dq.reshape(SEQ, d_a), dk.reshape(SEQ, d_a), dv.reshape(SEQ, d_a), dbias), st


def _rel_bias_grad(dbias, buckets):
    nh = N_HEADS

    def body(d0, d1, d2, bk_ref, o_ref, t_ref):
        ds = (d0, d1, d2)

        def per_bucket(b, carry):
            for h in range(nh):
                acc = jnp.zeros((BLOCK, BLOCK), F32)
                for p in range(3):
                    for w in range(2):
                        acc = acc + jnp.where(bk_ref[p, w] == b, ds[p][h, w], 0.0)
                t_ref[pl.ds(b * nh + h, 1), :] = jnp.sum(acc, axis=0, keepdims=True)
            return carry

        lax.fori_loop(0, N_BUCKETS, per_bucket, 0)
        o_ref[...] = jnp.sum(t_ref[...], axis=-1, keepdims=True)

    return _call("rel_bias_grad", body, (1,), [_fix_spec((nh, 2, BLOCK, BLOCK))] * 3 + [_fix_spec((3, 2, BLOCK, BLOCK))],
                 [_fix_spec((N_BUCKETS * nh, 1))], [jax.ShapeDtypeStruct((N_BUCKETS * nh, 1), F32)], [*dbias, buckets],
                 scratch=[pltpu.VMEM((N_BUCKETS * nh, LANES), F32)])[0][0]


def _gmlp_fwd(rest, gain, bias, ws, bs, causal, stages=()):
    d_b = _d_b()

    def body(u_ref, v_ref, g_ref, b_ref, ws_ref, bs_ref, c_ref, o_ref):
        u = u_ref[...].reshape(BLOCK, d_b)
        xhat, _ = _layer_norm_stats(_gelu(v_ref[...].reshape(BLOCK, d_b)))
        vn = (xhat * g_ref[...] + b_ref[...]).astype(BF16)
        outs = []
        for g in range(N_GROUPS):
            sl = slice(g * BLOCK, (g + 1) * BLOCK)
            w = jnp.where(c_ref[...] > 0, ws_ref[g], 0.0)
            z = _dot(w, vn[:, sl], "nn") + bs_ref[:, g:g + 1]
            outs.append(_gelu(u[:, sl]) * z)
        o_ref[...] = jnp.concatenate(outs, axis=-1).reshape(o_ref.shape)

    (out,), st = _call(
        "gmlp_fwd", body, (1, SEQ // BLOCK),
        [_tile_spec(0, d_b, 0), _tile_spec(0, d_b, 1), _fix_spec((1, d_b)), _fix_spec((1, d_b)),
         _fix_spec((N_GROUPS, BLOCK, BLOCK)), _fix_spec((BLOCK, N_GROUPS)), _fix_spec((BLOCK, BLOCK))],
        [_tile_spec(0, d_b, 0)], [jax.ShapeDtypeStruct(_view_shape(d_b, 0), F32)],
        [_tile_view(rest, 0), _tile_view(rest, 0), gain, bias, ws, bs, causal], sem=("parallel", "parallel"), stages=stages)
    return out.reshape(SEQ, d_b), st


def _gmlp_bwd(rest, dgmlp, gain, bias, ws, bs, causal, stages=()):
    d_b = _d_b()
    nchunk = SEQ // BLOCK

    def body(u_ref, v_ref, dg_ref, g_ref, b_ref, ws_ref, bs_ref, c_ref, du_ref, dv_ref, dws_ref, dbs_ref, dgain_ref, dbias_ref):
        c = pl.program_id(1)

        @pl.when(c == 0)
        def _():
            dws_ref[...] = jnp.zeros_like(dws_ref)
            dbs_ref[...] = jnp.zeros_like(dbs_ref)
            dgain_ref[...] = jnp.zeros_like(dgain_ref)
            dbias_ref[...] = jnp.zeros_like(dbias_ref)

        u = u_ref[...].reshape(BLOCK, d_b)
        v = v_ref[...].reshape(BLOCK, d_b)
        dgm = dg_ref[...].reshape(BLOCK, d_b)
        xhat, rstd = _layer_norm_stats(_gelu(v))
        vn = (xhat * g_ref[...] + b_ref[...]).astype(BF16)
        lane = lax.broadcasted_iota(I32, (BLOCK, LANES), 1)
        dus, dvns = [], []
        dbs = dbs_ref[...]
        for g in range(N_GROUPS):
            sl = slice(g * BLOCK, (g + 1) * BLOCK)
            w = jnp.where(c_ref[...] > 0, ws_ref[g], 0.0).astype(BF16)
            z = _dot(w, vn[:, sl], "nn") + bs_ref[:, g:g + 1]
            dz = dgm[:, sl] * _gelu(u[:, sl])
            dus.append(dgm[:, sl] * z * _gelu_grad(u[:, sl]))
            dws_ref[g] += _dot(dz, vn[:, sl], "nt")
            dbs = dbs + jnp.where(lane == g, jnp.sum(dz, axis=-1, keepdims=True), 0.0)
            dvns.append(_dot(w, dz, "tn"))
        dbs_ref[...] = dbs
        dvn = jnp.concatenate(dvns, axis=-1)
        dgain_ref[...] += jnp.sum(dvn * xhat, axis=0, keepdims=True)
        dbias_ref[...] += jnp.sum(dvn, axis=0, keepdims=True)
        dvg = _layer_norm_bwd(dvn * g_ref[...], xhat, rstd)
        du_ref[...] = jnp.concatenate(dus, axis=-1).reshape(du_ref.shape)
        dv_ref[...] = (dvg * _gelu_grad(v)).reshape(dv_ref.shape)

        @pl.when(c == nchunk - 1)
        def _():
            for g in range(N_GROUPS):
                dws_ref[g] = jnp.where(c_ref[...] > 0, dws_ref[g], 0.0)

    (du, dv, dws, dbs, dgain, dbias), st = _call(
        "gmlp_bwd", body, (1, nchunk),
        [_tile_spec(0, d_b, 0), _tile_spec(0, d_b, 1), _tile_spec(0, d_b, 0), _fix_spec((1, d_b)), _fix_spec((1, d_b)),
         _fix_spec((N_GROUPS, BLOCK, BLOCK)), _fix_spec((BLOCK, N_GROUPS)), _fix_spec((BLOCK, BLOCK))],
        [_tile_spec(0, d_b, 0), _tile_spec(0, d_b, 0), _fix_spec((N_GROUPS, BLOCK, BLOCK)), _fix_spec((BLOCK, LANES)),
         _fix_spec((1, d_b)), _fix_spec((1, d_b))],
        [jax.ShapeDtypeStruct(_view_shape(d_b, 0), F32)] * 2
        + [jax.ShapeDtypeStruct((N_GROUPS, BLOCK, BLOCK), F32), jax.ShapeDtypeStruct((BLOCK, LANES), F32)]
        + [jax.ShapeDtypeStruct((1, d_b), F32)] * 2,
        [_tile_view(rest, 0), _tile_view(rest, 0), _tile_view(dgmlp, 0), gain, bias, ws, bs, causal], stages=stages)
    return (du.reshape(SEQ, d_b), dv.reshape(SEQ, d_b), dws, dbs, dgain, dbias), st


def _assemble_dproj(dqkv, du, dv, dga, dgb, tr=128):
    d_a, d_b, d_in = _d_a(), _d_b(), _d_in()
    tr = _pick(tr, SEQ)

    def body(*refs):
        att, (du_ref, dv_ref, dga_ref, dgb_ref, o_ref) = refs[:9], refs[9:]
        for i in range(3):
            o_ref[:, i * d_a:(i + 1) * d_a] = (att[3 * i][...] + att[3 * i + 1][...] + att[3 * i + 2][...]).astype(BF16)
        o_ref[:, 3 * d_a:3 * d_a + d_b] = du_ref[...].astype(BF16)
        o_ref[:, 3 * d_a + d_b:3 * d_a + 2 * d_b] = dv_ref[...].astype(BF16)
        o_ref[:, 3 * d_a + 2 * d_b:3 * d_a + 2 * d_b + D_MODEL] = dga_ref[...]
        o_ref[:, 3 * d_a + 2 * d_b + D_MODEL:] = dgb_ref[...]

    return _call("assemble_dproj", body, (SEQ // tr,), [_row_spec(tr, d_a)] * 9 + [_row_spec(tr, d_b)] * 2 + [_row_spec(tr, D_MODEL)] * 2,
                 [_row_spec(tr, d_in)], [jax.ShapeDtypeStruct((SEQ, d_in), BF16)], [*dqkv, du, dv, dga, dgb], sem=("parallel",))[0][0]


def _dw(name, a, b, kind, core, mine, add=None, tn=1152, stages=()):
    s, m = a.shape
    n = b.shape[1]
    rs, cs = (m, n // N_DEV) if kind == "col" else (m // N_DEV, n)
    tn = _pick(tn if kind == "col" else 512, cs)
    nj = cs // tn

    def shard(q, c_ref):
        return 2 * q + (c_ref[0] if mine else 1 - c_ref[0])

    if kind == "col":
        a_spec = pl.BlockSpec((s, m), lambda q, j, c_ref: (0, 0))
        b_spec = pl.BlockSpec((s, tn), lambda q, j, c_ref: (0, shard(q, c_ref) * nj + j))
    else:
        a_spec = pl.BlockSpec((s, rs), lambda q, j, c_ref: (0, shard(q, c_ref)))
        b_spec = pl.BlockSpec((s, tn), lambda q, j, c_ref: (0, j))
    o_spec = pl.BlockSpec((None, rs, tn), lambda q, j, c_ref: (q, 0, j))

    def body(a_ref, b_ref, *rest):
        acc = _dot(a_ref[...], b_ref[...], "tn")
        if add is not None:
            acc = acc + rest[0][...].astype(F32)
        rest[-1][...] = acc.astype(BF16)

    (out,), st = _call(name, body, (N_CHIPS, nj), [a_spec, b_spec] + ([o_spec] if add is not None else []), [o_spec],
                       [jax.ShapeDtypeStruct((N_CHIPS, rs, cs), BF16)], [a, b] + ([add] if add is not None else []),
                       sem=("parallel", "parallel"), stages=stages, prefetch=core)
    return out, st


def _adamw(w, g, m, v):
    m = ADAM_B1 * m + (1.0 - ADAM_B1) * g
    v = ADAM_B2 * v + (1.0 - ADAM_B2) * (g * g)
    m_hat = m / (1.0 - ADAM_B1 ** ADAM_STEP)
    v_hat = v / (1.0 - ADAM_B2 ** ADAM_STEP)
    delta = -ADAM_LR * (m_hat / (jnp.sqrt(v_hat) + ADAM_EPS) + ADAM_WD * w)
    return delta, m, v


def _adam_shard(name, pair, chip_sums, chip, w, m, v, tr=256, stages=()):
    rs, cs = w.shape
    tr = _pick(tr, rs)

    def body(chip_ref, own_ref, *refs):
        slots, (w_ref, m_ref, v_ref, g_ref, d_ref, nm_ref, nv_ref) = refs[:N_CHIPS], refs[N_CHIPS:]
        g = None
        for q in range(N_CHIPS):
            term = jnp.where(chip_ref[0] == q, own_ref[...], slots[q][...]).astype(F32)
            g = term if g is None else g + term
        d, nm, nv = _adamw(w_ref[...], g, m_ref[...], v_ref[...])
        g_ref[...], d_ref[...], nm_ref[...], nv_ref[...] = g, d, nm, nv

    def slot(q):
        return pl.BlockSpec((None, tr, cs), lambda i, c_ref: (jnp.where(c_ref[0] == q, (q + 1) % N_CHIPS, q), i, 0))

    spec = pl.BlockSpec((tr, cs), lambda i, c_ref: (i, 0))
    return _call(name, body, (rs // tr,),
                 [pl.BlockSpec((None, tr, cs), lambda i, c_ref: (c_ref[0], i, 0))] + [slot(q) for q in range(N_CHIPS)] + [spec, spec, spec],
                 [spec] * 4, [jax.ShapeDtypeStruct((rs, cs), F32)] * 4, [pair] + [chip_sums] * N_CHIPS + [w, m, v],
                 sem=("parallel",), stages=stages, prefetch=chip, shown=True)


def _adam_small(part, parts, me, w, m, v):
    rows = w.shape[0]

    def body(me_ref, own_ref, *refs):
        slots, (w_ref, m_ref, v_ref, g_ref, d_ref, nm_ref, nv_ref) = refs[:N_DEV], refs[N_DEV:]
        g = None
        for j in range(N_DEV):
            term = jnp.where(me_ref[0] == j, own_ref[...], slots[j][...])
            g = term if g is None else g + term
        d, nm, nv = _adamw(w_ref[...], g, m_ref[...], v_ref[...])
        g_ref[...], d_ref[...], nm_ref[...], nv_ref[...] = g, d, nm, nv

    def slot(j):
        return pl.BlockSpec((None, rows, LANES), lambda i, me_ref: (jnp.where(me_ref[0] == j, (j + 1) % N_DEV, j), 0, 0))

    spec = _fix_spec((rows, LANES))
    return _call("adam_small", body, (1,), [spec] + [slot(j) for j in range(N_DEV)] + [spec, spec, spec], [spec] * 4,
                 [jax.ShapeDtypeStruct((rows, LANES), F32)] * 4, [part] + [parts] * N_DEV + [w, m, v], prefetch=me, shown=True)[0]


def _small_sizes():
    d_b = _d_b()
    return (("loss", 1), ("rel_bias", N_BUCKETS * N_HEADS), ("ln_v_gain", d_b), ("ln_v_bias", d_b),
            ("w_spatial", N_GROUPS * BLOCK * BLOCK), ("b_spatial", N_GROUPS * BLOCK), ("ln1_gain", D_MODEL), ("ln1_bias", D_MODEL),
            ("b_ff1", D_FF), ("b_ff2", D_MODEL), ("ln2_gain", D_MODEL), ("ln2_bias", D_MODEL))


def _pack(vals):
    pieces = []
    for name, size in _small_sizes():
        flat = vals[name].reshape(-1).astype(F32)
        padded = -(-size // (SUBLANES * LANES)) * SUBLANES * LANES
        pieces.append(jnp.pad(flat, (0, padded - size)).reshape(-1, LANES))
    return jnp.concatenate(pieces, axis=0)


def _unpack(buf):
    out, row = {}, 0
    for name, size in _small_sizes():
        rows = -(-size // (SUBLANES * LANES)) * SUBLANES
        out[name] = buf[row:row + rows].reshape(-1)[:size]
        row += rows
    return out


def kernel(x, w_in, rel_bias, ln_v_gain, ln_v_bias, w_spatial, b_spatial, w_proj_a, w_proj_b, w_out, ln1_gain, ln1_bias, w_ff1, b_ff1, w_ff2, b_ff2, ln2_gain, ln2_bias, loss_target, m_w_in, m_rel_bias, m_ln_v_gain, m_ln_v_bias, m_w_spatial, m_b_spatial, m_w_proj_a, m_w_proj_b, m_w_out, m_ln1_gain, m_ln1_bias, m_w_ff1, m_b_ff1, m_w_ff2, m_b_ff2, m_ln2_gain, m_ln2_bias, v_w_in, v_rel_bias, v_ln_v_gain, v_ln_v_bias, v_w_spatial, v_b_spatial, v_w_proj_a, v_w_proj_b, v_w_out, v_ln1_gain, v_ln1_bias, v_w_ff1, v_b_ff1, v_w_ff2, v_b_ff2, v_ln2_gain, v_ln2_bias):
    d_a, d_b, d_in = _d_a(), _d_b(), _d_in()
    weights = dict(w_in=w_in, rel_bias=rel_bias, ln_v_gain=ln_v_gain, ln_v_bias=ln_v_bias, w_spatial=w_spatial, b_spatial=b_spatial,
                   w_proj_a=w_proj_a, w_proj_b=w_proj_b, w_out=w_out, ln1_gain=ln1_gain, ln1_bias=ln1_bias, w_ff1=w_ff1, b_ff1=b_ff1,
                   w_ff2=w_ff2, b_ff2=b_ff2, ln2_gain=ln2_gain, ln2_bias=ln2_bias)
    mom1 = dict(w_in=m_w_in, rel_bias=m_rel_bias, ln_v_gain=m_ln_v_gain, ln_v_bias=m_ln_v_bias, w_spatial=m_w_spatial,
                b_spatial=m_b_spatial, w_proj_a=m_w_proj_a, w_proj_b=m_w_proj_b, w_out=m_w_out, ln1_gain=m_ln1_gain,
                ln1_bias=m_ln1_bias, w_ff1=m_w_ff1, b_ff1=m_b_ff1, w_ff2=m_w_ff2, b_ff2=m_b_ff2, ln2_gain=m_ln2_gain, ln2_bias=m_ln2_bias)
    mom2 = dict(w_in=v_w_in, rel_bias=v_rel_bias, ln_v_gain=v_ln_v_gain, ln_v_bias=v_ln_v_bias, w_spatial=v_w_spatial,
                b_spatial=v_b_spatial, w_proj_a=v_w_proj_a, w_proj_b=v_w_proj_b, w_out=v_w_out, ln1_gain=v_ln1_gain,
                ln1_bias=v_ln1_bias, w_ff1=v_w_ff1, b_ff1=v_b_ff1, w_ff2=v_w_ff2, b_ff2=v_b_ff2, ln2_gain=v_ln2_gain, ln2_bias=v_ln2_bias)

    mx, my, mc = _coords()
    me = (4 * mx + 2 * my + mc).astype(I32).reshape(1)
    chip = (2 * mx + my).astype(I32).reshape(1)
    full = {n: _cast_into_place(f"cast_{n}", weights[n][0], KINDS[n], me) for n in KINDS}
    sent = {n: (0, 0, 0) for n in KINDS}

    def keep(table, n):
        def store(outs):
            table[n] = outs[0]
        return store

    def gathering(**new):
        stages = []
        for n in KINDS:
            out, relayed, passed = sent[n]
            units = new.get(n, 0)
            if units or relayed < out or passed < relayed:
                st = _gather_stage(full[n], KINDS[n], (out, units) if units else None,
                                   (relayed, out - relayed) if relayed < out else None, (passed, relayed - passed) if passed < relayed else None)
                st.store = keep(full, n)
                sent[n] = (out + units, out, relayed)
                stages.append(st)
        return stages

    def settle(stages, outs):
        for st, o in zip(stages, outs):
            st.store(o)

    def alone(name, stages):
        settle(stages, _comm_only(name, stages))

    def here(n):
        assert sent[n] == (16, 16, 16), (n, sent[n])
        return full[n]

    alone("gather_w_in_near", gathering(w_in=16))
    alone("gather_w_in_relay", gathering())
    alone("gather_w_in_sibling", gathering())

    xs = _to_perm(x[0])
    target = _to_perm(loss_target[0])
    xb = _cast_bf16("cast_x", xs)
    g8 = BLOCK // N_SUB
    ws_t = w_spatial[0].reshape(N_GROUPS, g8, N_SUB, g8, N_SUB).transpose(0, 2, 1, 4, 3).reshape(N_GROUPS, BLOCK, BLOCK)
    bs_t = b_spatial[0].reshape(N_GROUPS, g8, N_SUB).transpose(2, 1, 0).reshape(BLOCK, N_GROUPS)
    idx = _local_index(0)
    causal = jnp.asarray((idx[:, None] >= idx[None, :]).astype(np.float32))
    buckets = jnp.asarray(_bucket_tables())
    bias = _bias_expand(rel_bias, buckets)

    hosted = gathering(w_proj_a=16, w_proj_b=16, w_ff1=1)
    (qkv,), st = _matmul("proj_qkv", xb, here("w_in"), "nn", [F32], n=3 * d_a, stages=hosted)
    settle(hosted, st)
    hosted = gathering(w_out=16, w_ff1=4)
    (rest,), st = _matmul("proj_rest", xb, here("w_in"), "nn", [F32], b_off=3 * d_a, n=d_in - 3 * d_a, stages=hosted)
    settle(hosted, st)
    fwd = []
    for p in range(3):
        hosted = gathering(w_ff1=(3, 5, 3)[p])
        res, st = _attn_fwd(qkv, bias, p, stages=hosted)
        settle(hosted, st)
        fwd.append(res)
    hosted = gathering(w_ff2=1)
    (attn, attn_b, lse), st = _attn_combine([o for o, _ in fwd], [l for _, l in fwd], stages=hosted)
    settle(hosted, st)
    hosted = gathering(w_ff2=2)
    gmlp, st = _gmlp_fwd(rest, ln_v_gain, ln_v_bias, ws_t, bs_t, causal, stages=hosted)
    settle(hosted, st)
    hosted = gathering(w_ff2=3)
    (ya,), st = _matmul("proj_a", attn_b, here("w_proj_a"), "nn", [F32], stages=hosted)
    settle(hosted, st)
    gate_a, gate_b = 2 * d_b, 2 * d_b + D_MODEL

    def merge(acc, ya_, ga, gb):
        return acc, _sigmoid(ga) * ya_ + _sigmoid(gb) * acc

    hosted = gathering(w_ff2=5)
    (yb, merged), st = _matmul("proj_b_merge", gmlp, here("w_proj_b"), "nn", [F32, BF16], merge,
                               [(ya, "mn", 0), (rest, "mn", gate_a), (rest, "mn", gate_b)], tn=256, stages=hosted)
    settle(hosted, st)
    hosted = gathering(w_ff2=3)
    (pre1,), st = _matmul("out_proj", merged, here("w_out"), "nn", [F32], lambda acc, x_: (ALPHA * x_ + acc,), [(xs, "mn", 0)], stages=hosted)
    settle(hosted, st)
    hosted = gathering(w_ff2=2)
    (xhat1, rstd1, h1b), st = _ln1_fwd(pre1, ln1_gain, ln1_bias, stages=hosted)
    settle(hosted, st)

    def relu2(acc, b_):
        r = jnp.maximum(acc + b_, 0.0)
        return r, r * r

    hosted = gathering()
    (relu, fb), st = _matmul("ff1", h1b, here("w_ff1"), "nn", [F32, BF16], relu2, [(b_ff1, "row", 0)], stages=hosted)
    settle(hosted, st)
    alone("gather_w_ff2_sibling", gathering())
    (ff,), _ = _matmul("ff2", fb, here("w_ff2"), "nn", [F32], lambda acc, b_: (acc + b_,), [(b_ff2, "row", 0)], tn=1024, tk=1024)

    core = lax.axis_index("c").astype(I32).reshape(1)
    factors, theirs, sib, pair, chips, transit, reduced = {}, {}, {}, {}, {}, {}, {}

    def grad_for_sibling(n, a, b, stages=()):
        factors[n] = (a, b)
        theirs[n], outs = _dw(f"dw_{n}_sibling", a, b, KINDS[n], core, False, stages=stages)
        settle(stages, outs)

    def to_sibling(n):
        st = _to_sibling_stage(theirs[n])
        st.store = keep(sib, n)
        return st

    def grad_own(n, stages=()):
        pair[n], outs = _dw(f"dw_{n}_own", *factors[n], KINDS[n], core, True, add=sib[n], stages=stages)
        settle(stages, outs)
        chips[n] = lax.empty(pair[n].shape, BF16)
        transit[n] = lax.empty((2, *pair[n].shape[1:]), BF16)
        reduced[n] = (0, 0)

    def reducing(**new):
        stages = []
        for n in list(reduced):
            out, relayed = reduced[n]
            units = new.get(n, 0)
            if units or relayed < out:
                st = _reduce_stage(pair[n], chips[n], transit[n], (out, units) if units else None,
                                   (relayed, out - relayed) if relayed < out else None)

                def store(outs, n=n):
                    chips[n], transit[n] = outs

                st.store = store
                reduced[n] = (out + units, out)
                stages.append(st)
        return stages

    def summed(n):
        assert reduced[n] == (16, 16), (n, reduced[n])
        return chips[n]

    dpre2, dpre2b, g_ln2_gain, g_ln2_bias, g_b_ff2, loss_part = _ln2_loss_bwd(ff, xhat1, ln1_gain, ln1_bias, ln2_gain, ln2_bias, target)
    grad_for_sibling("w_ff2", fb, dpre2b)

    def relu2_bwd(acc, r):
        da = acc * (2.0 * r)
        return da, da

    hosted = [to_sibling("w_ff2")]
    (dab, g_b_ff1), st = _matmul("d_ff1", dpre2b, here("w_ff2"), "nt", [BF16], relu2_bwd, [(relu, "mn", 0)], colsums=(1,), stages=hosted)
    settle(hosted, st)
    grad_own("w_ff2")
    grad_for_sibling("w_ff1", h1b, dab, reducing(w_ff2=6))
    hosted = reducing(w_ff2=10) + [to_sibling("w_ff1")]
    (dh1,), st = _matmul("d_h1", dab, here("w_ff1"), "nt", [F32], lambda acc, d_: (acc + ALPHA * d_,), [(dpre2, "mn", 0)], stages=hosted)
    settle(hosted, st)
    grad_own("w_ff1", reducing())
    hosted = reducing(w_ff1=4)
    (dpre1, dpre1b, g_ln1_gain, g_ln1_bias), st = _ln1_bwd(dh1, xhat1, rstd1, ln1_gain, stages=hosted)
    settle(hosted, st)
    grad_for_sibling("w_out", merged, dpre1b, reducing(w_ff1=2))

    def merge_bwd(acc, ga, gb, ya_, yb_):
        sa, sb = _sigmoid(ga), _sigmoid(gb)
        return acc * sa, acc * sb, acc * ya_ * (sa * (1.0 - sa)), acc * yb_ * (sb * (1.0 - sb))

    hosted = reducing(w_ff1=8) + [to_sibling("w_out")]
    (dya, dyb, dga, dgb), st = _matmul("d_merge", dpre1b, here("w_out"), "nt", [BF16] * 4, merge_bwd,
                                       [(rest, "mn", gate_a), (rest, "mn", gate_b), (ya, "mn", 0), (yb, "mn", 0)], tn=256, stages=hosted)
    settle(hosted, st)
    grad_own("w_out", reducing())
    grad_for_sibling("w_proj_a", attn_b, dya, reducing(w_ff1=2))
    grad_for_sibling("w_proj_b", gmlp, dyb, reducing())
    hosted = [to_sibling("w_proj_a"), to_sibling("w_proj_b")]
    (dattn,), st = _matmul("d_attn", dya, here("w_proj_a"), "nt", [F32], stages=hosted)
    settle(hosted, st)
    grad_own("w_proj_a")
    grad_own("w_proj_b")
    hosted = reducing(w_out=16)
    (dgmlp,), st = _matmul("d_gmlp", dyb, here("w_proj_b"), "nt", [F32], stages=hosted)
    settle(hosted, st)
    hosted = reducing(w_proj_a=16)
    (du, dvb, dws_t, dbs_t, g_lnv_gain, g_lnv_bias), st = _gmlp_bwd(rest, dgmlp, ln_v_gain, ln_v_bias, ws_t, bs_t, causal, stages=hosted)
    settle(hosted, st)
    delta = _attn_delta(dattn, attn)
    bwd = []
    for p in range(3):
        hosted = reducing(w_proj_b=16) if p == 0 else reducing()
        res, st = _attn_bwd(qkv, dattn, lse, delta, bias, p, stages=hosted)
        settle(hosted, st)
        bwd.append(res)
    g_rel_bias = _rel_bias_grad([b[3] for b in bwd], buckets)
    dproj = _assemble_dproj([b[i] for i in range(3) for b in bwd], du, dvb, dga, dgb)

    g_w_spatial = dws_t.reshape(N_GROUPS, N_SUB, g8, N_SUB, g8).transpose(0, 2, 1, 4, 3)
    g_b_spatial = dbs_t[:, :N_GROUPS].reshape(N_SUB, g8, N_GROUPS).transpose(2, 1, 0)
    part = _pack(dict(loss=loss_part, rel_bias=g_rel_bias, ln_v_gain=g_lnv_gain, ln_v_bias=g_lnv_bias, w_spatial=g_w_spatial,
                      b_spatial=g_b_spatial, ln1_gain=g_ln1_gain, ln1_bias=g_ln1_bias, b_ff1=g_b_ff1, b_ff2=g_b_ff2,
                      ln2_gain=g_ln2_gain, ln2_bias=g_ln2_bias))
    small = _small_stage(part)
    small.store = keep(sib, "small")
    grad_for_sibling("w_in", xb, dproj, [small])
    parts = sib["small"]
    half = SEQ // 2

    def add_residual(acc, d_):
        return (acc + ALPHA * d_,)

    hosted = [to_sibling("w_in")]
    (dx0,), st = _matmul("d_x0", dproj, here("w_in"), "nt", [F32], add_residual, [(dpre1, "mn", 0)], tm=1024, tn=1024, tk=3072, m=half,
                         stages=hosted)
    settle(hosted, st)
    grad_own("w_in")
    hosted = reducing(w_in=8)
    (dx1,), st = _matmul("d_x1", dproj, here("w_in"), "nt", [F32], add_residual, [(dpre1, "mn", 0)], tm=1024, tn=1024, tk=3072, m_off=half, m=half,
                         stages=hosted)
    settle(hosted, st)
    grad_x = _from_perm(jnp.concatenate([dx0, dx1], axis=0))[None]

    out_g, out_d, out_m, out_v = {}, {}, {}, {}
    for n, units in (("w_ff2", 4), ("w_ff1", 4), ("w_out", 0), ("w_proj_a", 0), ("w_proj_b", 0), ("w_in", 0)):
        hosted = reducing(w_in=units) if n != "w_in" else []
        (g, d, nm, nv), st = _adam_shard(f"adam_{n}", pair[n], summed(n), chip, weights[n][0], mom1[n][0], mom2[n][0], stages=hosted)
        settle(hosted, st)
        out_g[n], out_d[n], out_m[n], out_v[n] = g[None], d[None], nm[None], nv[None]

    zero = jnp.zeros((1,), F32)
    sg, sd, sm, sv = (_unpack(b) for b in _adam_small(
        part, parts, me, _pack({**weights, "loss": zero}), _pack({**mom1, "loss": zero}), _pack({**mom2, "loss": zero})))
    for n in WEIGHT_ORDER:
        if n not in KINDS:
            shape = weights[n].shape
            out_g[n], out_d[n], out_m[n], out_v[n] = (t[n].reshape(shape) for t in (sg, sd, sm, sv))
    loss = sg["loss"].reshape(())
    return (loss, grad_x, *[out_g[n] for n in WEIGHT_ORDER], *[out_d[n] for n in WEIGHT_ORDER],
            *[out_m[n] for n in WEIGHT_ORDER], *[out_v[n] for n in WEIGHT_ORDER])
```

```python
import math

import jax
import jax.numpy as jnp
import numpy as np
from jax import lax
from jax.experimental import pallas as pl
from jax.experimental.pallas import tpu as pltpu

F32 = jnp.float32
BF16 = jnp.bfloat16
I32 = jnp.int32

SEQ = 2048
D_MODEL = 2048
HEAD_DIM = 128
N_HEADS = 8
N_GROUPS = 8
D_FF = 4 * D_MODEL
BLOCK = 128
DILATIONS = (1, 4, 16)
N_BUCKETS = 32
MAX_DISTANCE = 2048
ALPHA = 2.0 ** 0.25
LN_EPS = 1e-5
NEG_INF = -1e30
N_DEV = 8
N_CHIPS = 4
N_SUB = 16
ADAM_LR, ADAM_B1, ADAM_B2, ADAM_EPS, ADAM_WD, ADAM_STEP = 0.001, 0.9, 0.999, 1e-08, 0.01, 10
LANES = 128
SUBLANES = 8
VMEM_LIMIT = 56 * 1024 * 1024
MESH = pl.DeviceIdType.MESH
ANY = pl.BlockSpec(memory_space=pl.ANY)
WEIGHT_ORDER = ("w_in", "rel_bias", "ln_v_gain", "ln_v_bias", "w_spatial", "b_spatial", "w_proj_a", "w_proj_b", "w_out",
                "ln1_gain", "ln1_bias", "w_ff1", "b_ff1", "w_ff2", "b_ff2", "ln2_gain", "ln2_bias")
KINDS = {"w_in": "col", "w_proj_a": "col", "w_proj_b": "col", "w_out": "row", "w_ff1": "col", "w_ff2": "row"}


def _d_a():
    return N_HEADS * HEAD_DIM


def _d_b():
    return N_GROUPS * BLOCK


def _d_in():
    return 3 * _d_a() + 2 * _d_b() + 2 * D_MODEL


def _pick(t, n, *others):
    if n <= t and all(o % n == 0 for o in others):
        return n
    for c in range(min(t, n) // LANES * LANES, 0, -LANES):
        if n % c == 0 and all(o % c == 0 for o in others):
            return c
    raise ValueError((t, n, others))


class _Stage:
    def __init__(self, ins, outs, alias, sems, start, finish):
        self.ins, self.outs, self.alias, self.sems, self.start, self.finish = ins, outs, alias, sems, start, finish


def _call(name, body, grid, in_specs, out_specs, out_shape, operands, scratch=(), sem=None, stages=(), sequential=False, prefetch=None,
          shown=False):
    n_in, n_out, n_sc = len(in_specs), len(out_specs), len(scratch)
    st_in = [len(s.ins) for s in stages]
    st_out = [len(s.outs) for s in stages]
    st_sem = [len(s.sems) for s in stages]
    n_pre = 0 if prefetch is None else 1
    aliases, ioff, ooff = {}, n_in + n_pre, n_out
    for s, ni, no in zip(stages, st_in, st_out):
        for i, o in s.alias.items():
            aliases[ioff + i] = ooff + o
        ioff, ooff = ioff + ni, ooff + no

    def split(refs, counts):
        out, at = [], 0
        for c in counts:
            out.append(refs[at:at + c])
            at += c
        return out

    def wrapped(*refs):
        ins, sins, outs, souts, sc, ssems = split(refs[n_pre:], [n_in, sum(st_in), n_out, sum(st_out), n_sc, sum(st_sem)])
        parts = list(zip(stages, split(sins, st_in), split(souts, st_out), split(ssems, st_sem)))
        if sequential:
            for s, a, b, c in parts:
                s.start(a, b, c)
                s.finish(a, b, c)
            return
        if parts:
            first = _all_of([pl.program_id(i) == 0 for i in range(len(grid))])
            last = _all_of([pl.program_id(i) == g - 1 for i, g in enumerate(grid)])

            @pl.when(first)
            def _():
                for s, a, b, c in parts:
                    s.start(a, b, c)

        body(*(refs[:n_pre] if shown else ()), *ins, *outs, *sc)
        if parts:
            @pl.when(last)
            def _():
                for s, a, b, c in parts:
                    s.finish(a, b, c)

    if stages or sem is None:
        sem = ("arbitrary",) * len(grid)
    specs = dict(grid=grid, in_specs=list(in_specs) + [ANY] * sum(st_in), out_specs=list(out_specs) + [ANY] * sum(st_out),
                 scratch_shapes=list(scratch) + [x for s in stages for x in s.sems])
    if prefetch is not None:
        specs = dict(grid_spec=pltpu.PrefetchScalarGridSpec(num_scalar_prefetch=1, **specs))
    res = pl.pallas_call(
        wrapped, name=name, out_shape=list(out_shape) + [o for s in stages for o in s.outs], input_output_aliases=aliases,
        compiler_params=pltpu.CompilerParams(dimension_semantics=sem, vmem_limit_bytes=VMEM_LIMIT), **specs,
    )(*([prefetch] if n_pre else []), *operands, *[a for s in stages for a in s.ins])
    res = list(res)
    return res[:n_out], split(res[n_out:], st_out)


def _all_of(conds):
    out = conds[0]
    for c in conds[1:]:
        out = out & c
    return out


def _coords():
    return lax.axis_index("x"), lax.axis_index("y"), lax.axis_index("c")


def _other_chips(x, y):
    return ((1 - x, y), (x, 1 - y), (1 - x, 1 - y))


def _lin(dev):
    return 4 * dev[0] + 2 * dev[1] + dev[2]


def _piece(total, lo, n, units=16):
    assert total % units == 0
    return lo * (total // units), n * (total // units)


FLOWS = 4


def _split(lo, cnt):
    k = next(k for k in (FLOWS, 2, 1) if cnt % (2 * SUBLANES * k) == 0)
    return [(lo + i * (cnt // k), cnt // k) for i in range(k)]


def _remote(src, dst, send, recv, to):
    return pltpu.make_async_remote_copy(src_ref=src, dst_ref=dst, send_sem=send, recv_sem=recv, device_id=to, device_id_type=MESH)


def _placer(kind, n, lo, cnt):
    def place(ref, dev):
        if kind == "col":
            return ref.at[pl.ds(lo, cnt), pl.ds(pl.multiple_of(_lin(dev) * n, LANES), n)]
        return ref.at[pl.ds(pl.multiple_of(_lin(dev) * n + lo, 2 * SUBLANES), cnt), :]
    return place


def _spread_stage(full, kind, piece=(0, 16), home=False):
    n = (full.shape[1] if kind == "col" else full.shape[0]) // N_DEV
    lo, cnt = _piece(full.shape[0] if kind == "col" else n, *piece)
    parts = _split(lo, cnt)
    npeers = 1 if home else 2

    def copies(ins, outs, sems):
        send, recv = sems
        x, y, c = _coords()
        me = (x, y, c)
        peers = [(x, y, 1 - c)] if home else [(1 - x, y, c), (x, 1 - y, c)]
        out, arrive = [], []
        for k, t in enumerate(peers):
            for i, (plo, pcnt) in enumerate(parts):
                place = _placer(kind, n, plo, pcnt)
                out.append(_remote(place(outs[0], me), place(outs[0], me), send.at[i, k], recv.at[i, k], t))
                arrive.append(_remote(place(outs[0], t), place(outs[0], t), send.at[i, k], recv.at[i, k], t))
        return out, arrive

    def start(ins, outs, sems):
        for cp in copies(ins, outs, sems)[0]:
            cp.start()

    def finish(ins, outs, sems):
        out, arrive = copies(ins, outs, sems)
        for cp in arrive:
            cp.wait_recv()
        for cp in out:
            cp.wait_send()

    return _Stage([full], [jax.ShapeDtypeStruct(full.shape, full.dtype)], {0: 0},
                  [pltpu.SemaphoreType.DMA((len(parts), npeers)), pltpu.SemaphoreType.DMA((len(parts), npeers))], start, finish)


def _relay_stage(full, kind, piece=(0, 16)):
    n = (full.shape[1] if kind == "col" else full.shape[0]) // N_DEV
    lo, cnt = _piece(full.shape[0] if kind == "col" else n, *piece)
    half = cnt // 2
    assert half % (2 * SUBLANES) == 0, (cnt, kind)
    tops, bottoms = _split(lo, half), _split(lo + half, half)

    def copies(ins, outs, sems):
        send, recv = sems
        x, y, c = _coords()
        xn, yn, dg = (1 - x, y, c), (x, 1 - y, c), (1 - x, 1 - y, c)
        out, arrive, k = [], [], 0
        for came_from, to, parts in ((yn, xn, tops), (xn, yn, bottoms)):
            for plo, pcnt in parts:
                place = _placer(kind, n, plo, pcnt)
                out.append(_remote(place(outs[0], came_from), place(outs[0], came_from), send.at[k], recv.at[k], to))
                arrive.append(_remote(place(outs[0], dg), place(outs[0], dg), send.at[k], recv.at[k], to))
                k += 1
        return out, arrive

    def start(ins, outs, sems):
        for cp in copies(ins, outs, sems)[0]:
            cp.start()

    def finish(ins, outs, sems):
        out, arrive = copies(ins, outs, sems)
        for cp in arrive:
            cp.wait_recv()
        for cp in out:
            cp.wait_send()

    return _Stage([full], [jax.ShapeDtypeStruct(full.shape, full.dtype)], {0: 0},
                  [pltpu.SemaphoreType.DMA((len(tops) + len(bottoms),)), pltpu.SemaphoreType.DMA((len(tops) + len(bottoms),))], start, finish)


def _forward_stage(full, kind, piece=(0, 16)):
    n = (full.shape[1] if kind == "col" else full.shape[0]) // N_DEV
    lo, cnt = _piece(full.shape[0] if kind == "col" else n, *piece)
    place = _placer(kind, n, lo, cnt)

    def copies(ins, outs, sems):
        send, recv = sems
        x, y, c = _coords()
        chips = _other_chips(x, y)
        out = [_remote(place(outs[0], (*chip, c)), place(outs[0], (*chip, c)), send.at[k], recv.at[k], (x, y, 1 - c)) for k, chip in enumerate(chips)]
        arrive = [_remote(place(outs[0], (*chip, 1 - c)), place(outs[0], (*chip, 1 - c)), send.at[k], recv.at[k], (x, y, 1 - c))
                  for k, chip in enumerate(chips)]
        return out, arrive

    def start(ins, outs, sems):
        for cp in copies(ins, outs, sems)[0]:
            cp.start()

    def finish(ins, outs, sems):
        out, arrive = copies(ins, outs, sems)
        for cp in arrive:
            cp.wait_recv()
        for cp in out:
            cp.wait_send()

    return _Stage([full], [jax.ShapeDtypeStruct(full.shape, full.dtype)], {0: 0},
                  [pltpu.SemaphoreType.DMA((3,)), pltpu.SemaphoreType.DMA((3,))], start, finish)


def _to_sibling_stage(theirs):
    def copies(ins, outs, sems):
        send, recv = sems
        x, y, c = _coords()
        return [_remote(ins[0].at[q], outs[0].at[q], send.at[q], recv.at[q], (x, y, 1 - c)) for q in range(N_CHIPS)]

    def start(ins, outs, sems):
        for cp in copies(ins, outs, sems):
            cp.start()

    def finish(ins, outs, sems):
        for cp in copies(ins, outs, sems):
            cp.wait()

    return _Stage([theirs], [jax.ShapeDtypeStruct(theirs.shape, BF16)], {},
                  [pltpu.SemaphoreType.DMA((N_CHIPS,)), pltpu.SemaphoreType.DMA((N_CHIPS,))], start, finish)


def _to_chips_stage(pair, dst, piece=(0, 16)):
    lo, cnt = _piece(pair.shape[1], *piece)
    parts = _split(lo, cnt)
    nsem = 3 * len(parts)

    def copies(ins, outs, sems):
        send, recv = sems
        x, y, c = _coords()
        mine = 2 * x + y
        out, arrive, k = [], [], 0
        for px, py in _other_chips(x, y):
            for plo, pcnt in parts:
                rows = pl.ds(plo, pcnt)
                out.append(_remote(ins[0].at[2 * px + py, rows, :], outs[0].at[mine, rows, :], send.at[k], recv.at[k], (px, py, c)))
                arrive.append(_remote(ins[0].at[2 * px + py, rows, :], outs[0].at[2 * px + py, rows, :], send.at[k], recv.at[k], (px, py, c)))
                k += 1
        return out, arrive

    def start(ins, outs, sems):
        for cp in copies(ins, outs, sems)[0]:
            cp.start()

    def finish(ins, outs, sems):
        out, arrive = copies(ins, outs, sems)
        for cp in arrive:
            cp.wait_recv()
        for cp in out:
            cp.wait_send()

    return _Stage([pair, dst], [jax.ShapeDtypeStruct(dst.shape, dst.dtype)], {1: 0},
                  [pltpu.SemaphoreType.DMA((nsem,)), pltpu.SemaphoreType.DMA((nsem,))], start, finish)


def _fuse(parts, ins, outs, alias):
    parts = [p for p in parts if p is not None]
    sems = [x for st, _, _ in parts for x in st.sems]

    def run(which):
        def go(i, o, s):
            refs, at = list(i) + list(o), 0
            for st, pi, po in parts:
                getattr(st, which)([refs[k] for k in pi], [refs[k] for k in po], s[at:at + len(st.sems)])
                at += len(st.sems)
        return go

    return _Stage(ins, [jax.ShapeDtypeStruct(o.shape, o.dtype) for o in outs], alias, sems, run("start"), run("finish"))


def _gather_stage(full, kind, new=None, relay=None, forward=None):
    return _fuse([(_spread_stage(full, kind, new), [0], [1]) if new else None,
                  (_relay_stage(full, kind, relay), [0], [1]) if relay else None,
                  (_spread_stage(full, kind, relay, home=True), [0], [1]) if relay else None,
                  (_forward_stage(full, kind, forward), [0], [1]) if forward else None], [full], [full], {0: 0})


def _small_stage(part):
    def copies(ins, outs, sems):
        send, recv = sems
        x, y, c = _coords()
        me = (x, y, c)
        peers = [(1 - x if k & 4 else x, 1 - y if k & 2 else y, 1 - c if k & 1 else c) for k in range(1, N_DEV)]
        out = [_remote(ins[0], outs[0].at[_lin(me)], send.at[k], recv.at[k], t) for k, t in enumerate(peers)]
        arrive = [_remote(ins[0], outs[0].at[_lin(t)], send.at[k], recv.at[k], t) for k, t in enumerate(peers)]
        return out, arrive

    def start(ins, outs, sems):
        for cp in copies(ins, outs, sems)[0]:
            cp.start()

    def finish(ins, outs, sems):
        out, arrive = copies(ins, outs, sems)
        for cp in arrive:
            cp.wait_recv()
        for cp in out:
            cp.wait_send()

    return _Stage([part], [jax.ShapeDtypeStruct((N_DEV, *part.shape), F32)], {},
                  [pltpu.SemaphoreType.DMA((N_DEV - 1,)), pltpu.SemaphoreType.DMA((N_DEV - 1,))], start, finish)


def _comm_only(name, stages):
    return _call(name, lambda: None, (1,), [], [], [], [], stages=stages, sequential=True)[1]


_GELU_C = math.sqrt(2.0 / math.pi)


def _gelu(x):
    return 0.5 * x * (1.0 + jnp.tanh(_GELU_C * (x + 0.044715 * x * x * x)))


def _gelu_grad(x):
    t = jnp.tanh(_GELU_C * (x + 0.044715 * x * x * x))
    return 0.5 * (1.0 + t) + 0.5 * x * (1.0 - t * t) * (_GELU_C * (1.0 + 3.0 * 0.044715 * x * x))


def _sigmoid(x):
    return 1.0 / (1.0 + jnp.exp(-x))


def _dot(a, b, mode):
    dims = {"nn": (((1,), (0,)), ((), ())), "nt": (((1,), (1,)), ((), ())), "tn": (((0,), (0,)), ((), ()))}[mode]
    return lax.dot_general(a.astype(BF16), b.astype(BF16), dims, preferred_element_type=F32)


def _matmul(name, a, b, mode, outs, epi=None, extras=(), colsums=(), tm=2048, tn=512, tk=2048, b_off=0, n=None, m_off=0, m=None, stages=()):
    if mode == "tn":
        kk, mfull = a.shape
    else:
        mfull, kk = a.shape
    m = mfull if m is None else m
    n = (b.shape[0] if mode == "nt" else b.shape[1]) if n is None else n
    tm, tk = _pick(tm, m, m_off), _pick(tk, kk)
    tn = _pick(tn, n, b_off, *[off for _, _, off in extras])
    boff, moff = b_off // tn, m_off // tm
    nm, nn_, nk = m // tm, n // tn, kk // tk
    col_major = bool(colsums)
    grid = (nn_, nm, nk) if col_major else (nm, nn_, nk)

    def imap(f):
        if col_major:
            return lambda g0, g1, k: f(g1, g0, k)
        return f

    a_spec = (pl.BlockSpec((tk, tm), imap(lambda i, j, k: (k, i + moff))) if mode == "tn"
              else pl.BlockSpec((tm, tk), imap(lambda i, j, k: (i + moff, k))))
    b_spec = (pl.BlockSpec((tn, tk), imap(lambda i, j, k: (j + boff, k))) if mode == "nt"
              else pl.BlockSpec((tk, tn), imap(lambda i, j, k: (k, j + boff))))
    in_specs, operands = [a_spec, b_spec], [a, b]
    for arr, kind, off in extras:
        o = off // tn
        if kind == "mn":
            in_specs.append(pl.BlockSpec((tm, tn), imap(lambda i, j, k, o=o: (i + moff, j + o))))
        else:
            in_specs.append(pl.BlockSpec((1, tn), imap(lambda i, j, k, o=o: (0, j + o))))
        operands.append(arr)
    out_shape = [jax.ShapeDtypeStruct((m, n), dt) for dt in outs] + [jax.ShapeDtypeStruct((1, n), F32) for _ in colsums]
    out_specs = ([pl.BlockSpec((tm, tn), imap(lambda i, j, k: (i, j))) for _ in outs]
                 + [pl.BlockSpec((1, tn), imap(lambda i, j, k: (0, j))) for _ in colsums])
    n_ex, n_out, n_cs = len(extras), len(outs), len(colsums)

    def body(*refs):
        a_ref, b_ref = refs[:2]
        ex_refs = refs[2:2 + n_ex]
        out_refs = refs[2 + n_ex:2 + n_ex + n_out]
        cs_refs = refs[2 + n_ex + n_out:2 + n_ex + n_out + n_cs]
        part = _dot(a_ref[...], b_ref[...], mode)

        def finish(acc):
            res = epi(acc, *[r[...] for r in ex_refs]) if epi is not None else (acc,)
            for r, v in zip(out_refs, res[:n_out]):
                r[...] = v.astype(r.dtype)
            if n_cs:
                @pl.when(pl.program_id(1) == 0)
                def _():
                    for r in cs_refs:
                        r[...] = jnp.zeros_like(r)

                for r, idx in zip(cs_refs, colsums):
                    r[...] += jnp.sum(res[idx], axis=0, keepdims=True)

        if nk == 1:
            finish(part)
        else:
            acc_ref = refs[-1]
            k = pl.program_id(2)

            @pl.when(k == 0)
            def _():
                acc_ref[...] = part

            @pl.when(k > 0)
            def _():
                acc_ref[...] += part

            @pl.when(k == nk - 1)
            def _():
                finish(acc_ref[...])

    sem = ("arbitrary", "arbitrary", "arbitrary") if col_major else ("parallel", "parallel", "arbitrary")
    return _call(name, body, grid, in_specs, out_specs, out_shape, operands,
                 scratch=[pltpu.VMEM((tm, tn), F32)] if nk > 1 else [], sem=sem, stages=stages)


def _row_spec(tr, c):
    return pl.BlockSpec((tr, c), lambda i: (i, 0))


def _fix_spec(shape):
    return pl.BlockSpec(shape, lambda *_: tuple(0 for _ in shape))


def _cast_bf16(name, x, tr=512):
    r, c = x.shape
    tr = _pick(tr, r)

    def body(x_ref, o_ref):
        o_ref[...] = x_ref[...].astype(BF16)

    return _call(name, body, (r // tr,), [_row_spec(tr, c)], [_row_spec(tr, c)], [jax.ShapeDtypeStruct((r, c), BF16)], [x],
                 sem=("parallel",))[0][0]


def _cast_into_place(name, w, kind, me, tr=512):
    r, c = w.shape
    tr = _pick(tr, r)
    nb = r // tr
    if kind == "col":
        o_spec = pl.BlockSpec((tr, c), lambda i, me_ref: (i, me_ref[0]))
        shape = (r, c * N_DEV)
    else:
        o_spec = pl.BlockSpec((tr, c), lambda i, me_ref: (me_ref[0] * nb + i, 0))
        shape = (r * N_DEV, c)

    def body(x_ref, o_ref):
        o_ref[...] = x_ref[...].astype(BF16)

    return _call(name, body, (nb,), [pl.BlockSpec((tr, c), lambda i, me_ref: (i, 0))], [o_spec], [jax.ShapeDtypeStruct(shape, BF16)], [w],
                 sem=("parallel",), prefetch=me)[0][0]


def _layer_norm_stats(x):
    mean = jnp.mean(x, axis=-1, keepdims=True)
    xc = x - mean
    var = jnp.mean(xc * xc, axis=-1, keepdims=True)
    rstd = lax.rsqrt(var + LN_EPS)
    return xc * rstd, rstd


def _layer_norm_bwd(dxhat, xhat, rstd):
    m1 = jnp.mean(dxhat, axis=-1, keepdims=True)
    m2 = jnp.mean(dxhat * xhat, axis=-1, keepdims=True)
    return rstd * (dxhat - m1 - xhat * m2)


def _ln1_fwd(pre1, g1, b1, tr=256, stages=()):
    s, d = pre1.shape
    tr = _pick(tr, s)

    def body(p_ref, g_ref, b_ref, xh_ref, rs_ref, h_ref):
        xhat, rstd = _layer_norm_stats(p_ref[...])
        xh_ref[...] = xhat
        rs_ref[...] = rstd
        h_ref[...] = (xhat * g_ref[...] + b_ref[...]).astype(BF16)

    return _call("ln1_fwd", body, (s // tr,), [_row_spec(tr, d), _fix_spec((1, d)), _fix_spec((1, d))],
                 [_row_spec(tr, d), _row_spec(tr, 1), _row_spec(tr, d)],
                 [jax.ShapeDtypeStruct((s, d), F32), jax.ShapeDtypeStruct((s, 1), F32), jax.ShapeDtypeStruct((s, d), BF16)],
                 [pre1, g1, b1], sem=("parallel",), stages=stages)


def _ln2_loss_bwd(ff, xhat1, g1, b1, g2, b2, target, tr=256):
    s, d = ff.shape
    tr = _pick(tr, s)

    def body(ff_ref, xh1_ref, g1_ref, b1_ref, g2_ref, b2_ref, t_ref, dp_ref, dpb_ref, dg_ref, db_ref, dbf_ref, loss_ref):
        @pl.when(pl.program_id(0) == 0)
        def _():
            dg_ref[...] = jnp.zeros_like(dg_ref)
            db_ref[...] = jnp.zeros_like(db_ref)
            dbf_ref[...] = jnp.zeros_like(dbf_ref)
            loss_ref[...] = jnp.zeros_like(loss_ref)

        h1 = xh1_ref[...] * g1_ref[...] + b1_ref[...]
        xhat, rstd = _layer_norm_stats(ALPHA * h1 + ff_ref[...])
        err = xhat * g2_ref[...] + b2_ref[...] - t_ref[...]
        row = jnp.mean(err * err, axis=-1, keepdims=True)
        loss_ref[...] += 0.5 * jnp.sum(row, axis=0, keepdims=True)
        dy = err / d
        dg_ref[...] += jnp.sum(dy * xhat, axis=0, keepdims=True)
        db_ref[...] += jnp.sum(dy, axis=0, keepdims=True)
        dpre = _layer_norm_bwd(dy * g2_ref[...], xhat, rstd)
        dbf_ref[...] += jnp.sum(dpre, axis=0, keepdims=True)
        dp_ref[...] = dpre
        dpb_ref[...] = dpre.astype(BF16)

    vec = _fix_spec((1, d))
    return _call("ln2_loss_bwd", body, (s // tr,), [_row_spec(tr, d), _row_spec(tr, d), vec, vec, vec, vec, _row_spec(tr, d)],
                 [_row_spec(tr, d), _row_spec(tr, d), vec, vec, vec, _fix_spec((1, 1))],
                 [jax.ShapeDtypeStruct((s, d), F32), jax.ShapeDtypeStruct((s, d), BF16)]
                 + [jax.ShapeDtypeStruct((1, d), F32)] * 3 + [jax.ShapeDtypeStruct((1, 1), F32)],
                 [ff, xhat1, g1, b1, g2, b2, target])[0]


def _ln1_bwd(dh1, xhat1, rstd1, g1, tr=256, stages=()):
    s, d = dh1.shape
    tr = _pick(tr, s)

    def body(dh_ref, xh_ref, rs_ref, g_ref, dp_ref, dpb_ref, dg_ref, db_ref):
        @pl.when(pl.program_id(0) == 0)
        def _():
            dg_ref[...] = jnp.zeros_like(dg_ref)
            db_ref[...] = jnp.zeros_like(db_ref)

        dh, xhat = dh_ref[...], xh_ref[...]
        dg_ref[...] += jnp.sum(dh * xhat, axis=0, keepdims=True)
        db_ref[...] += jnp.sum(dh, axis=0, keepdims=True)
        dpre = _layer_norm_bwd(dh * g_ref[...], xhat, rs_ref[...])
        dp_ref[...] = dpre
        dpb_ref[...] = dpre.astype(BF16)

    vec = _fix_spec((1, d))
    return _call("ln1_bwd", body, (s // tr,), [_row_spec(tr, d), _row_spec(tr, d), _row_spec(tr, 1), vec],
                 [_row_spec(tr, d), _row_spec(tr, d), vec, vec],
                 [jax.ShapeDtypeStruct((s, d), F32), jax.ShapeDtypeStruct((s, d), BF16)] + [jax.ShapeDtypeStruct((1, d), F32)] * 2,
                 [dh1, xhat1, rstd1, g1], stages=stages)


def _to_perm(x):
    return x.reshape(SEQ // N_SUB, N_SUB, -1).transpose(1, 0, 2).reshape(SEQ, -1)


def _from_perm(x):
    return x.reshape(N_SUB, SEQ // N_SUB, -1).transpose(1, 0, 2).reshape(SEQ, -1)


def _local_index(p):
    rho = np.arange(BLOCK)
    if p == 0:
        return 16 * (rho % 8) + rho // 8
    if p == 1:
        return 4 * (rho % 32) + rho // 32
    return rho


def _tile_view(x, p):
    c = x.shape[1]
    if p == 1:
        return x.reshape(4, 4, BLOCK, c)
    return x.reshape(N_SUB, BLOCK, c)


def _view_shape(c, p):
    return (4, 4, BLOCK, c) if p == 1 else (N_SUB, BLOCK, c)


def _tile_spec(p, width, col, shift=0):
    nblk = SEQ // DILATIONS[p] // BLOCK

    def blk(n):
        return jnp.clip(n + shift, 0, nblk - 1)

    if p == 0:
        return pl.BlockSpec((N_SUB, SUBLANES, width), lambda s, n: (0, blk(n), col))
    if p == 1:
        return pl.BlockSpec((4, None, 32, width), lambda s, n: (0, s, blk(n), col))
    return pl.BlockSpec((None, BLOCK, width), lambda s, n: (s, 0, col))


def _tile_grid(p):
    return ((1, 16), (4, 4), (16, 1))[p]


def _t5_bucket(n):
    max_exact = N_BUCKETS // 2
    nf = np.maximum(n, 1).astype(np.float32)
    large = max_exact + (np.log(nf / np.float32(max_exact)) / np.float32(math.log(MAX_DISTANCE / max_exact))
                         * np.float32(N_BUCKETS - max_exact)).astype(np.int32)
    large = np.minimum(large, N_BUCKETS - 1)
    return np.where(n < max_exact, n, large).astype(np.int32)


def _bucket_tables():
    tabs = np.zeros((3, 2, BLOCK, BLOCK), np.int32)
    for p, d in enumerate(DILATIONS):
        i = _local_index(p)
        diff = i[:, None] - i[None, :]
        tabs[p, 0] = np.where(diff <= 0, _t5_bucket((BLOCK + diff) * d), -1)
        tabs[p, 1] = np.where(diff >= 0, _t5_bucket(np.maximum(diff, 0) * d), -1)
    return tabs


def _bias_expand(rel_bias, buckets):
    nh = N_HEADS

    def body(rb_ref, bk_ref, o_ref):
        for w in range(2):
            bk = bk_ref[0, w]
            for h in range(nh):
                val = jnp.zeros((BLOCK, BLOCK), F32)
                for b in range(N_BUCKETS):
                    val = jnp.where(bk == b, rb_ref[b, h], val)
                o_ref[0, h, w] = jnp.where(bk < 0, NEG_INF, val)

    return _call("bias_expand", body, (3,),
                 [pl.BlockSpec(memory_space=pltpu.SMEM), pl.BlockSpec((1, 2, BLOCK, BLOCK), lambda p: (p, 0, 0, 0))],
                 [pl.BlockSpec((1, nh, 2, BLOCK, BLOCK), lambda p: (p, 0, 0, 0, 0))],
                 [jax.ShapeDtypeStruct((3, nh, 2, BLOCK, BLOCK), F32)], [rel_bias, buckets], sem=("parallel",))[0][0]


def _heads_to_lanes(cols):
    lane = lax.broadcasted_iota(I32, (BLOCK, LANES), 1)
    out = jnp.zeros((BLOCK, LANES), F32)
    for h, c in enumerate(cols):
        out = jnp.where(lane == h, c, out)
    return out


def _attn_fwd(qkv, bias, p, stages=()):
    d_a = _d_a()
    has_prev = SEQ // DILATIONS[p] // BLOCK > 1
    scale = HEAD_DIM ** -0.5
    view = _tile_view(qkv, p)

    def body(q_ref, kc_ref, kp_ref, vc_ref, vp_ref, b_ref, o_ref, l_ref):
        n = pl.program_id(1)
        q_all = q_ref[...].reshape(BLOCK, d_a).astype(BF16)
        kc_all = kc_ref[...].reshape(BLOCK, d_a).astype(BF16)
        vc_all = vc_ref[...].reshape(BLOCK, d_a).astype(BF16)
        if has_prev:
            kp_all = kp_ref[...].reshape(BLOCK, d_a).astype(BF16)
            vp_all = vp_ref[...].reshape(BLOCK, d_a).astype(BF16)
        outs, lses = [], []
        for h in range(N_HEADS):
            sl = slice(h * HEAD_DIM, (h + 1) * HEAD_DIM)
            q = q_all[:, sl]
            sc = _dot(q, kc_all[:, sl], "nt") * scale + b_ref[0, h, 1]
            m = jnp.max(sc, axis=-1, keepdims=True)
            if has_prev:
                sp = _dot(q, kp_all[:, sl], "nt") * scale + b_ref[0, h, 0]
                sp = jnp.where(n > 0, sp, NEG_INF)
                m = jnp.maximum(m, jnp.max(sp, axis=-1, keepdims=True))
                pp = jnp.exp(sp - m)
            pc = jnp.exp(sc - m)
            den = jnp.sum(pc, axis=-1, keepdims=True)
            o = _dot(pc, vc_all[:, sl], "nn")
            if has_prev:
                den = den + jnp.sum(pp, axis=-1, keepdims=True)
                o = o + _dot(pp, vp_all[:, sl], "nn")
            outs.append(o / den)
            lses.append(m + jnp.log(den))
        o_ref[...] = jnp.concatenate(outs, axis=-1).reshape(o_ref.shape)
        l_ref[...] = _heads_to_lanes(lses).reshape(l_ref.shape)

    (o, l), st = _call(
        f"attn_fwd{p}", body, _tile_grid(p),
        [_tile_spec(p, d_a, 0), _tile_spec(p, d_a, 1), _tile_spec(p, d_a, 1, -1), _tile_spec(p, d_a, 2), _tile_spec(p, d_a, 2, -1),
         pl.BlockSpec((1, N_HEADS, 2, BLOCK, BLOCK), lambda s, n: (p, 0, 0, 0, 0))],
        [_tile_spec(p, d_a, 0), _tile_spec(p, LANES, 0)],
        [jax.ShapeDtypeStruct(_view_shape(d_a, p), F32), jax.ShapeDtypeStruct(_view_shape(LANES, p), F32)],
        [view, view, view, view, view, bias], sem=("parallel", "parallel"), stages=stages)
    return (o.reshape(SEQ, d_a), l.reshape(SEQ, LANES)), st


def _attn_combine(os_, ls_, tr=256, stages=()):
    d_a = _d_a()
    tr = _pick(tr, SEQ)

    def body(o0, o1, o2, l0, l1, l2, a_ref, ab_ref, lt_ref):
        l = [l0[...], l1[...], l2[...]]
        m = jnp.maximum(jnp.maximum(l[0], l[1]), l[2])
        w = [jnp.exp(x - m) for x in l]
        tot = w[0] + w[1] + w[2]
        lt_ref[...] = m + jnp.log(tot)
        w = [x / tot for x in w]
        for h in range(N_HEADS):
            sl = slice(h * HEAD_DIM, (h + 1) * HEAD_DIM)
            acc = w[0][:, h:h + 1] * o0[:, sl] + w[1][:, h:h + 1] * o1[:, sl] + w[2][:, h:h + 1] * o2[:, sl]
            a_ref[:, sl] = acc
            ab_ref[:, sl] = acc.astype(BF16)

    return _call("attn_combine", body, (SEQ // tr,), [_row_spec(tr, d_a)] * 3 + [_row_spec(tr, LANES)] * 3,
                 [_row_spec(tr, d_a), _row_spec(tr, d_a), _row_spec(tr, LANES)],
                 [jax.ShapeDtypeStruct((SEQ, d_a), F32), jax.ShapeDtypeStruct((SEQ, d_a), BF16), jax.ShapeDtypeStruct((SEQ, LANES), F32)],
                 [*os_, *ls_], sem=("parallel",), stages=stages)


def _attn_delta(dattn, attn, tr=256):
    d_a = _d_a()
    tr = _pick(tr, SEQ)

    def body(d_ref, a_ref, o_ref):
        prod = d_ref[...] * a_ref[...]
        lane = lax.broadcasted_iota(I32, (tr, LANES), 1)
        out = jnp.zeros((tr, LANES), F32)
        for h in range(N_HEADS):
            out = jnp.where(lane == h, jnp.sum(prod[:, h * HEAD_DIM:(h + 1) * HEAD_DIM], axis=-1, keepdims=True), out)
        o_ref[...] = out

    return _call("attn_delta", body, (SEQ // tr,), [_row_spec(tr, d_a)] * 2, [_row_spec(tr, LANES)],
                 [jax.ShapeDtypeStruct((SEQ, LANES), F32)], [dattn, attn], sem=("parallel",))[0][0]


def _attn_bwd(qkv, dattn, lse, delta, bias, p, stages=()):
    d_a = _d_a()
    nblk = SEQ // DILATIONS[p] // BLOCK
    has_next = nblk > 1
    scale = HEAD_DIM ** -0.5
    qv, dov, lv, tv = (_tile_view(x, p) for x in (qkv, dattn, lse, delta))

    def body(q_ref, qn_ref, k_ref, v_ref, do_ref, don_ref, l_ref, ln_ref, t_ref, tn_ref, b_ref, dq_ref, dk_ref, dv_ref, db_ref, carry_ref):
        j = pl.program_id(1)

        @pl.when((pl.program_id(0) == 0) & (j == 0))
        def _():
            db_ref[...] = jnp.zeros_like(db_ref)

        k_all = k_ref[...].reshape(BLOCK, d_a).astype(BF16)
        v_all = v_ref[...].reshape(BLOCK, d_a).astype(BF16)

        def side(qr, dor, lr, tr_, w):
            q_all = qr[...].reshape(BLOCK, d_a).astype(BF16)
            do_all = dor[...].reshape(BLOCK, d_a).astype(BF16)
            l_all = lr[...].reshape(BLOCK, LANES)
            t_all = tr_[...].reshape(BLOCK, LANES)
            dqs, dks, dvs = [], [], []
            for h in range(N_HEADS):
                sl = slice(h * HEAD_DIM, (h + 1) * HEAD_DIM)
                s = _dot(q_all[:, sl], k_all[:, sl], "nt") * scale + b_ref[0, h, w]
                pr = jnp.exp(s - l_all[:, h:h + 1])
                dp = _dot(do_all[:, sl], v_all[:, sl], "nt")
                ds = pr * (dp - t_all[:, h:h + 1])
                db_ref[h, w] += ds
                dqs.append(_dot(ds, k_all[:, sl], "nn") * scale)
                dks.append(_dot(ds, q_all[:, sl], "tn") * scale)
                dvs.append(_dot(pr, do_all[:, sl], "tn"))
            return [jnp.concatenate(x, axis=-1) for x in (dqs, dks, dvs)]

        dq_c, dk_c, dv_c = side(q_ref, do_ref, l_ref, t_ref, 1)
        if has_next:
            dq_ref[...] = (jnp.where(j > 0, carry_ref[...], 0.0) + dq_c).reshape(dq_ref.shape)
            not_last = j < nblk - 1

            @pl.when(not_last)
            def _():
                dq_n, dk_n, dv_n = side(qn_ref, don_ref, ln_ref, tn_ref, 0)
                carry_ref[...] = dq_n
                dk_ref[...] = (dk_c + dk_n).reshape(dk_ref.shape)
                dv_ref[...] = (dv_c + dv_n).reshape(dv_ref.shape)

            @pl.when(jnp.logical_not(not_last))
            def _():
                dk_ref[...] = dk_c.reshape(dk_ref.shape)
                dv_ref[...] = dv_c.reshape(dv_ref.shape)
        else:
            dq_ref[...] = dq_c.reshape(dq_ref.shape)
            dk_ref[...] = dk_c.reshape(dk_ref.shape)
            dv_ref[...] = dv_c.reshape(dv_ref.shape)

    def big(col, shift=0):
        return _tile_spec(p, d_a, col, shift)

    def small(shift=0):
        return _tile_spec(p, LANES, 0, shift)

    (dq, dk, dv, dbias), st = _call(
        f"attn_bwd{p}", body, _tile_grid(p),
        [big(0), big(0, 1), big(1), big(2), big(0), big(0, 1), small(), small(1), small(), small(1),
         pl.BlockSpec((1, N_HEADS, 2, BLOCK, BLOCK), lambda s, n: (p, 0, 0, 0, 0))],
        [big(0), big(0), big(0), pl.BlockSpec((N_HEADS, 2, BLOCK, BLOCK), lambda s, n: (0, 0, 0, 0))],
        [jax.ShapeDtypeStruct(_view_shape(d_a, p), F32)] * 3 + [jax.ShapeDtypeStruct((N_HEADS, 2, BLOCK, BLOCK), F32)],
        [qv, qv, qv, qv, dov, dov, lv, lv, tv, tv, bias], scratch=[pltpu.VMEM((BLOCK, d_a), F32)], stages=stages)
    return (dq.reshape(SEQ, d_a), dk.reshape(SEQ, d_a), dv.reshape(SEQ, d_a), dbias), st


def _rel_bias_grad(dbias, buckets):
    nh = N_HEADS

    def body(d0, d1, d2, bk_ref, o_ref, t_ref):
        ds = (d0, d1, d2)

        def per_bucket(b, carry):
            for h in range(nh):
                acc = jnp.zeros((BLOCK, BLOCK), F32)
                for p in range(3):
                    for w in range(2):
                        acc = acc + jnp.where(bk_ref[p, w] == b, ds[p][h, w], 0.0)
                t_ref[pl.ds(b * nh + h, 1), :] = jnp.sum(acc, axis=0, keepdims=True)
            return carry

        lax.fori_loop(0, N_BUCKETS, per_bucket, 0)
        o_ref[...] = jnp.sum(t_ref[...], axis=-1, keepdims=True)

    return _call("rel_bias_grad", body, (1,), [_fix_spec((nh, 2, BLOCK, BLOCK))] * 3 + [_fix_spec((3, 2, BLOCK, BLOCK))],
                 [_fix_spec((N_BUCKETS * nh, 1))], [jax.ShapeDtypeStruct((N_BUCKETS * nh, 1), F32)], [*dbias, buckets],
                 scratch=[pltpu.VMEM((N_BUCKETS * nh, LANES), F32)])[0][0]


def _gmlp_fwd(rest, gain, bias, ws, bs, causal, stages=()):
    d_b = _d_b()

    def body(u_ref, v_ref, g_ref, b_ref, ws_ref, bs_ref, c_ref, o_ref):
        u = u_ref[...].reshape(BLOCK, d_b)
        xhat, _ = _layer_norm_stats(_gelu(v_ref[...].reshape(BLOCK, d_b)))
        vn = (xhat * g_ref[...] + b_ref[...]).astype(BF16)
        outs = []
        for g in range(N_GROUPS):
            sl = slice(g * BLOCK, (g + 1) * BLOCK)
            w = jnp.where(c_ref[...] > 0, ws_ref[g], 0.0)
            z = _dot(w, vn[:, sl], "nn") + bs_ref[:, g:g + 1]
            outs.append(_gelu(u[:, sl]) * z)
        o_ref[...] = jnp.concatenate(outs, axis=-1).reshape(o_ref.shape)

    (out,), st = _call(
        "gmlp_fwd", body, (1, SEQ // BLOCK),
        [_tile_spec(0, d_b, 0), _tile_spec(0, d_b, 1), _fix_spec((1, d_b)), _fix_spec((1, d_b)),
         _fix_spec((N_GROUPS, BLOCK, BLOCK)), _fix_spec((BLOCK, N_GROUPS)), _fix_spec((BLOCK, BLOCK))],
        [_tile_spec(0, d_b, 0)], [jax.ShapeDtypeStruct(_view_shape(d_b, 0), F32)],
        [_tile_view(rest, 0), _tile_view(rest, 0), gain, bias, ws, bs, causal], sem=("parallel", "parallel"), stages=stages)
    return out.reshape(SEQ, d_b), st


def _gmlp_bwd(rest, dgmlp, gain, bias, ws, bs, causal, stages=()):
    d_b = _d_b()
    nchunk = SEQ // BLOCK

    def body(u_ref, v_ref, dg_ref, g_ref, b_ref, ws_ref, bs_ref, c_ref, du_ref, dv_ref, dws_ref, dbs_ref, dgain_ref, dbias_ref):
        c = pl.program_id(1)

        @pl.when(c == 0)
        def _():
            dws_ref[...] = jnp.zeros_like(dws_ref)
            dbs_ref[...] = jnp.zeros_like(dbs_ref)
            dgain_ref[...] = jnp.zeros_like(dgain_ref)
            dbias_ref[...] = jnp.zeros_like(dbias_ref)

        u = u_ref[...].reshape(BLOCK, d_b)
        v = v_ref[...].reshape(BLOCK, d_b)
        dgm = dg_ref[...].reshape(BLOCK, d_b)
        xhat, rstd = _layer_norm_stats(_gelu(v))
        vn = (xhat * g_ref[...] + b_ref[...]).astype(BF16)
        lane = lax.broadcasted_iota(I32, (BLOCK, LANES), 1)
        dus, dvns = [], []
        dbs = dbs_ref[...]
        for g in range(N_GROUPS):
            sl = slice(g * BLOCK, (g + 1) * BLOCK)
            w = jnp.where(c_ref[...] > 0, ws_ref[g], 0.0).astype(BF16)
            z = _dot(w, vn[:, sl], "nn") + bs_ref[:, g:g + 1]
            dz = dgm[:, sl] * _gelu(u[:, sl])
            dus.append(dgm[:, sl] * z * _gelu_grad(u[:, sl]))
            dws_ref[g] += _dot(dz, vn[:, sl], "nt")
            dbs = dbs + jnp.where(lane == g, jnp.sum(dz, axis=-1, keepdims=True), 0.0)
            dvns.append(_dot(w, dz, "tn"))
        dbs_ref[...] = dbs
        dvn = jnp.concatenate(dvns, axis=-1)
        dgain_ref[...] += jnp.sum(dvn * xhat, axis=0, keepdims=True)
        dbias_ref[...] += jnp.sum(dvn, axis=0, keepdims=True)
        dvg = _layer_norm_bwd(dvn * g_ref[...], xhat, rstd)
        du_ref[...] = jnp.concatenate(dus, axis=-1).reshape(du_ref.shape)
        dv_ref[...] = (dvg * _gelu_grad(v)).reshape(dv_ref.shape)

        @pl.when(c == nchunk - 1)
        def _():
            for g in range(N_GROUPS):
                dws_ref[g] = jnp.where(c_ref[...] > 0, dws_ref[g], 0.0)

    (du, dv, dws, dbs, dgain, dbias), st = _call(
        "gmlp_bwd", body, (1, nchunk),
        [_tile_spec(0, d_b, 0), _tile_spec(0, d_b, 1), _tile_spec(0, d_b, 0), _fix_spec((1, d_b)), _fix_spec((1, d_b)),
         _fix_spec((N_GROUPS, BLOCK, BLOCK)), _fix_spec((BLOCK, N_GROUPS)), _fix_spec((BLOCK, BLOCK))],
        [_tile_spec(0, d_b, 0), _tile_spec(0, d_b, 0), _fix_spec((N_GROUPS, BLOCK, BLOCK)), _fix_spec((BLOCK, LANES)),
         _fix_spec((1, d_b)), _fix_spec((1, d_b))],
        [jax.ShapeDtypeStruct(_view_shape(d_b, 0), F32)] * 2
        + [jax.ShapeDtypeStruct((N_GROUPS, BLOCK, BLOCK), F32), jax.ShapeDtypeStruct((BLOCK, LANES), F32)]
        + [jax.ShapeDtypeStruct((1, d_b), F32)] * 2,
        [_tile_view(rest, 0), _tile_view(rest, 0), _tile_view(dgmlp, 0), gain, bias, ws, bs, causal], stages=stages)
    return (du.reshape(SEQ, d_b), dv.reshape(SEQ, d_b), dws, dbs, dgain, dbias), st


def _assemble_dproj(dqkv, du, dv, dga, dgb, tr=128):
    d_a, d_b, d_in = _d_a(), _d_b(), _d_in()
    tr = _pick(tr, SEQ)

    def body(*refs):
        att, (du_ref, dv_ref, dga_ref, dgb_ref, o_ref) = refs[:9], refs[9:]
        for i in range(3):
            o_ref[:, i * d_a:(i + 1) * d_a] = (att[3 * i][...] + att[3 * i + 1][...] + att[3 * i + 2][...]).astype(BF16)
        o_ref[:, 3 * d_a:3 * d_a + d_b] = du_ref[...].astype(BF16)
        o_ref[:, 3 * d_a + d_b:3 * d_a + 2 * d_b] = dv_ref[...].astype(BF16)
        o_ref[:, 3 * d_a + 2 * d_b:3 * d_a + 2 * d_b + D_MODEL] = dga_ref[...]
        o_ref[:, 3 * d_a + 2 * d_b + D_MODEL:] = dgb_ref[...]

    return _call("assemble_dproj", body, (SEQ // tr,), [_row_spec(tr, d_a)] * 9 + [_row_spec(tr, d_b)] * 2 + [_row_spec(tr, D_MODEL)] * 2,
                 [_row_spec(tr, d_in)], [jax.ShapeDtypeStruct((SEQ, d_in), BF16)], [*dqkv, du, dv, dga, dgb], sem=("parallel",))[0][0]


def _dw(name, a, b, kind, core, mine, add=None, tn=1152, stages=()):
    s, m = a.shape
    n = b.shape[1]
    rs, cs = (m, n // N_DEV) if kind == "col" else (m // N_DEV, n)
    tn = _pick(tn if kind == "col" else 512, cs)
    nj = cs // tn

    def shard(q, c_ref):
        return 2 * q + (c_ref[0] if mine else 1 - c_ref[0])

    if kind == "col":
        a_spec = pl.BlockSpec((s, m), lambda q, j, c_ref: (0, 0))
        b_spec = pl.BlockSpec((s, tn), lambda q, j, c_ref: (0, shard(q, c_ref) * nj + j))
    else:
        a_spec = pl.BlockSpec((s, rs), lambda q, j, c_ref: (0, shard(q, c_ref)))
        b_spec = pl.BlockSpec((s, tn), lambda q, j, c_ref: (0, j))
    o_spec = pl.BlockSpec((None, rs, tn), lambda q, j, c_ref: (q, 0, j))

    def body(a_ref, b_ref, *rest):
        acc = _dot(a_ref[...], b_ref[...], "tn")
        if add is not None:
            acc = acc + rest[0][...].astype(F32)
        rest[-1][...] = acc.astype(BF16)

    (out,), st = _call(name, body, (N_CHIPS, nj), [a_spec, b_spec] + ([o_spec] if add is not None else []), [o_spec],
                       [jax.ShapeDtypeStruct((N_CHIPS, rs, cs), BF16)], [a, b] + ([add] if add is not None else []),
                       sem=("parallel", "parallel"), stages=stages, prefetch=core)
    return out, st


def _adamw(w, g, m, v):
    m = ADAM_B1 * m + (1.0 - ADAM_B1) * g
    v = ADAM_B2 * v + (1.0 - ADAM_B2) * (g * g)
    m_hat = m / (1.0 - ADAM_B1 ** ADAM_STEP)
    v_hat = v / (1.0 - ADAM_B2 ** ADAM_STEP)
    delta = -ADAM_LR * (m_hat / (jnp.sqrt(v_hat) + ADAM_EPS) + ADAM_WD * w)
    return delta, m, v


def _adam_shard(name, pair, chip_sums, chip, w, m, v, tr=256, stages=()):
    rs, cs = w.shape
    tr = _pick(tr, rs)

    def body(chip_ref, own_ref, *refs):
        slots, (w_ref, m_ref, v_ref, g_ref, d_ref, nm_ref, nv_ref) = refs[:N_CHIPS], refs[N_CHIPS:]
        g = None
        for q in range(N_CHIPS):
            term = jnp.where(chip_ref[0] == q, own_ref[...], slots[q][...]).astype(F32)
            g = term if g is None else g + term
        d, nm, nv = _adamw(w_ref[...], g, m_ref[...], v_ref[...])
        g_ref[...], d_ref[...], nm_ref[...], nv_ref[...] = g, d, nm, nv

    def slot(q):
        return pl.BlockSpec((None, tr, cs), lambda i, c_ref: (jnp.where(c_ref[0] == q, (q + 1) % N_CHIPS, q), i, 0))

    spec = pl.BlockSpec((tr, cs), lambda i, c_ref: (i, 0))
    return _call(name, body, (rs // tr,),
                 [pl.BlockSpec((None, tr, cs), lambda i, c_ref: (c_ref[0], i, 0))] + [slot(q) for q in range(N_CHIPS)] + [spec, spec, spec],
                 [spec] * 4, [jax.ShapeDtypeStruct((rs, cs), F32)] * 4, [pair] + [chip_sums] * N_CHIPS + [w, m, v],
                 sem=("parallel",), stages=stages, prefetch=chip, shown=True)


def _adam_small(part, parts, me, w, m, v):
    rows = w.shape[0]

    def body(me_ref, own_ref, *refs):
        slots, (w_ref, m_ref, v_ref, g_ref, d_ref, nm_ref, nv_ref) = refs[:N_DEV], refs[N_DEV:]
        g = None
        for j in range(N_DEV):
            term = jnp.where(me_ref[0] == j, own_ref[...], slots[j][...])
            g = term if g is None else g + term
        d, nm, nv = _adamw(w_ref[...], g, m_ref[...], v_ref[...])
        g_ref[...], d_ref[...], nm_ref[...], nv_ref[...] = g, d, nm, nv

    def slot(j):
        return pl.BlockSpec((None, rows, LANES), lambda i, me_ref: (jnp.where(me_ref[0] == j, (j + 1) % N_DEV, j), 0, 0))

    spec = _fix_spec((rows, LANES))
    return _call("adam_small", body, (1,), [spec] + [slot(j) for j in range(N_DEV)] + [spec, spec, spec], [spec] * 4,
                 [jax.ShapeDtypeStruct((rows, LANES), F32)] * 4, [part] + [parts] * N_DEV + [w, m, v], prefetch=me, shown=True)[0]


def _small_sizes():
    d_b = _d_b()
    return (("loss", 1), ("rel_bias", N_BUCKETS * N_HEADS), ("ln_v_gain", d_b), ("ln_v_bias", d_b),
            ("w_spatial", N_GROUPS * BLOCK * BLOCK), ("b_spatial", N_GROUPS * BLOCK), ("ln1_gain", D_MODEL), ("ln1_bias", D_MODEL),
            ("b_ff1", D_FF), ("b_ff2", D_MODEL), ("ln2_gain", D_MODEL), ("ln2_bias", D_MODEL))


def _pack(vals):
    pieces = []
    for name, size in _small_sizes():
        flat = vals[name].reshape(-1).astype(F32)
        padded = -(-size // (SUBLANES * LANES)) * SUBLANES * LANES
        pieces.append(jnp.pad(flat, (0, padded - size)).reshape(-1, LANES))
    return jnp.concatenate(pieces, axis=0)


def _unpack(buf):
    out, row = {}, 0
    for name, size in _small_sizes():
        rows = -(-size // (SUBLANES * LANES)) * SUBLANES
        out[name] = buf[row:row + rows].reshape(-1)[:size]
        row += rows
    return out


def kernel(x, w_in, rel_bias, ln_v_gain, ln_v_bias, w_spatial, b_spatial, w_proj_a, w_proj_b, w_out, ln1_gain, ln1_bias, w_ff1, b_ff1, w_ff2, b_ff2, ln2_gain, ln2_bias, loss_target, m_w_in, m_rel_bias, m_ln_v_gain, m_ln_v_bias, m_w_spatial, m_b_spatial, m_w_proj_a, m_w_proj_b, m_w_out, m_ln1_gain, m_ln1_bias, m_w_ff1, m_b_ff1, m_w_ff2, m_b_ff2, m_ln2_gain, m_ln2_bias, v_w_in, v_rel_bias, v_ln_v_gain, v_ln_v_bias, v_w_spatial, v_b_spatial, v_w_proj_a, v_w_proj_b, v_w_out, v_ln1_gain, v_ln1_bias, v_w_ff1, v_b_ff1, v_w_ff2, v_b_ff2, v_ln2_gain, v_ln2_bias):
    d_a, d_b, d_in = _d_a(), _d_b(), _d_in()
    weights = dict(w_in=w_in, rel_bias=rel_bias, ln_v_gain=ln_v_gain, ln_v_bias=ln_v_bias, w_spatial=w_spatial, b_spatial=b_spatial,
                   w_proj_a=w_proj_a, w_proj_b=w_proj_b, w_out=w_out, ln1_gain=ln1_gain, ln1_bias=ln1_bias, w_ff1=w_ff1, b_ff1=b_ff1,
                   w_ff2=w_ff2, b_ff2=b_ff2, ln2_gain=ln2_gain, ln2_bias=ln2_bias)
    mom1 = dict(w_in=m_w_in, rel_bias=m_rel_bias, ln_v_gain=m_ln_v_gain, ln_v_bias=m_ln_v_bias, w_spatial=m_w_spatial,
                b_spatial=m_b_spatial, w_proj_a=m_w_proj_a, w_proj_b=m_w_proj_b, w_out=m_w_out, ln1_gain=m_ln1_gain,
                ln1_bias=m_ln1_bias, w_ff1=m_w_ff1, b_ff1=m_b_ff1, w_ff2=m_w_ff2, b_ff2=m_b_ff2, ln2_gain=m_ln2_gain, ln2_bias=m_ln2_bias)
    mom2 = dict(w_in=v_w_in, rel_bias=v_rel_bias, ln_v_gain=v_ln_v_gain, ln_v_bias=v_ln_v_bias, w_spatial=v_w_spatial,
                b_spatial=v_b_spatial, w_proj_a=v_w_proj_a, w_proj_b=v_w_proj_b, w_out=v_w_out, ln1_gain=v_ln1_gain,
                ln1_bias=v_ln1_bias, w_ff1=v_w_ff1, b_ff1=v_b_ff1, w_ff2=v_w_ff2, b_ff2=v_b_ff2, ln2_gain=v_ln2_gain, ln2_bias=v_ln2_bias)

    mx, my, mc = _coords()
    me = (4 * mx + 2 * my + mc).astype(I32).reshape(1)
    chip = (2 * mx + my).astype(I32).reshape(1)
    full = {n: _cast_into_place(f"cast_{n}", weights[n][0], KINDS[n], me) for n in KINDS}
    sent = {n: (0, 0, 0) for n in KINDS}

    def keep(table, n):
        def store(outs):
            table[n] = outs[0]
        return store

    def gathering(**new):
        stages = []
        for n in KINDS:
            out, relayed, passed = sent[n]
            units = new.get(n, 0)
            if units or relayed < out or passed < relayed:
                st = _gather_stage(full[n], KINDS[n], (out, units) if units else None,
                                   (relayed, out - relayed) if relayed < out else None, (passed, relayed - passed) if passed < relayed else None)
                st.store = keep(full, n)
                sent[n] = (out + units, out, relayed)
                stages.append(st)
        return stages

    def settle(stages, outs):
        for st, o in zip(stages, outs):
            st.store(o)

    def alone(name, stages):
        settle(stages, _comm_only(name, stages))

    def here(n):
        assert sent[n] == (16, 16, 16), (n, sent[n])
        return full[n]

    alone("gather_w_in_near", gathering(w_in=16))
    alone("gather_w_in_relay", gathering())
    alone("gather_w_in_sibling", gathering())

    xs = _to_perm(x[0])
    target = _to_perm(loss_target[0])
    xb = _cast_bf16("cast_x", xs)
    g8 = BLOCK // N_SUB
    ws_t = w_spatial[0].reshape(N_GROUPS, g8, N_SUB, g8, N_SUB).transpose(0, 2, 1, 4, 3).reshape(N_GROUPS, BLOCK, BLOCK)
    bs_t = b_spatial[0].reshape(N_GROUPS, g8, N_SUB).transpose(2, 1, 0).reshape(BLOCK, N_GROUPS)
    idx = _local_index(0)
    causal = jnp.asarray((idx[:, None] >= idx[None, :]).astype(np.float32))
    buckets = jnp.asarray(_bucket_tables())
    bias = _bias_expand(rel_bias, buckets)

    hosted = gathering(w_proj_a=16, w_proj_b=16, w_ff1=1)
    (qkv,), st = _matmul("proj_qkv", xb, here("w_in"), "nn", [F32], n=3 * d_a, stages=hosted)
    settle(hosted, st)
    hosted = gathering(w_out=16, w_ff1=4)
    (rest,), st = _matmul("proj_rest", xb, here("w_in"), "nn", [F32], b_off=3 * d_a, n=d_in - 3 * d_a, stages=hosted)
    settle(hosted, st)
    fwd = []
    for p in range(3):
        hosted = gathering(w_ff1=(3, 5, 3)[p])
        res, st = _attn_fwd(qkv, bias, p, stages=hosted)
        settle(hosted, st)
        fwd.append(res)
    hosted = gathering(w_ff2=1)
    (attn, attn_b, lse), st = _attn_combine([o for o, _ in fwd], [l for _, l in fwd], stages=hosted)
    settle(hosted, st)
    hosted = gathering(w_ff2=2)
    gmlp, st = _gmlp_fwd(rest, ln_v_gain, ln_v_bias, ws_t, bs_t, causal, stages=hosted)
    settle(hosted, st)
    hosted = gathering(w_ff2=3)
    (ya,), st = _matmul("proj_a", attn_b, here("w_proj_a"), "nn", [F32], stages=hosted)
    settle(hosted, st)
    gate_a, gate_b = 2 * d_b, 2 * d_b + D_MODEL

    def merge(acc, ya_, ga, gb):
        return acc, _sigmoid(ga) * ya_ + _sigmoid(gb) * acc

    hosted = gathering(w_ff2=5)
    (yb, merged), st = _matmul("proj_b_merge", gmlp, here("w_proj_b"), "nn", [F32, BF16], merge,
                               [(ya, "mn", 0), (rest, "mn", gate_a), (rest, "mn", gate_b)], tn=256, stages=hosted)
    settle(hosted, st)
    hosted = gathering(w_ff2=3)
    (pre1,), st = _matmul("out_proj", merged, here("w_out"), "nn", [F32], lambda acc, x_: (ALPHA * x_ + acc,), [(xs, "mn", 0)], stages=hosted)
    settle(hosted, st)
    hosted = gathering(w_ff2=2)
    (xhat1, rstd1, h1b), st = _ln1_fwd(pre1, ln1_gain, ln1_bias, stages=hosted)
    settle(hosted, st)

    def relu2(acc, b_):
        r = jnp.maximum(acc + b_, 0.0)
        return r, r * r

    hosted = gathering()
    (relu, fb), st = _matmul("ff1", h1b, here("w_ff1"), "nn", [F32, BF16], relu2, [(b_ff1, "row", 0)], stages=hosted)
    settle(hosted, st)
    alone("gather_w_ff2_sibling", gathering())
    (ff,), _ = _matmul("ff2", fb, here("w_ff2"), "nn", [F32], lambda acc, b_: (acc + b_,), [(b_ff2, "row", 0)], tn=1024, tk=1024)

    core = lax.axis_index("c").astype(I32).reshape(1)
    factors, theirs, sib, pair, chips, reduced = {}, {}, {}, {}, {}, {}

    def grad_for_sibling(n, a, b, stages=()):
        factors[n] = (a, b)
        theirs[n], outs = _dw(f"dw_{n}_sibling", a, b, KINDS[n], core, False, stages=stages)
        settle(stages, outs)

    def to_sibling(n):
        st = _to_sibling_stage(theirs[n])
        st.store = keep(sib, n)
        return st

    def grad_own(n, stages=()):
        pair[n], outs = _dw(f"dw_{n}_own", *factors[n], KINDS[n], core, True, add=sib[n], stages=stages)
        settle(stages, outs)
        chips[n] = lax.empty(pair[n].shape, BF16)
        reduced[n] = 0

    def reducing(**new):
        stages = []
        for n, units in new.items():
            st = _to_chips_stage(pair[n], chips[n], (reduced[n], units))
            st.store = keep(chips, n)
            reduced[n] += units
            stages.append(st)
        return stages

    def summed(n):
        assert reduced[n] == 16, (n, reduced[n])
        return chips[n]

    dpre2, dpre2b, g_ln2_gain, g_ln2_bias, g_b_ff2, loss_part = _ln2_loss_bwd(ff, xhat1, ln1_gain, ln1_bias, ln2_gain, ln2_bias, target)
    grad_for_sibling("w_ff2", fb, dpre2b)

    def relu2_bwd(acc, r):
        da = acc * (2.0 * r)
        return da, da

    hosted = [to_sibling("w_ff2")]
    (dab, g_b_ff1), st = _matmul("d_ff1", dpre2b, here("w_ff2"), "nt", [BF16], relu2_bwd, [(relu, "mn", 0)], colsums=(1,), stages=hosted)
    settle(hosted, st)
    grad_own("w_ff2")
    grad_for_sibling("w_ff1", h1b, dab, reducing(w_ff2=3))
    hosted = reducing(w_ff2=8) + [to_sibling("w_ff1")]
    (dh1,), st = _matmul("d_h1", dab, here("w_ff1"), "nt", [F32], lambda acc, d_: (acc + ALPHA * d_,), [(dpre2, "mn", 0)], stages=hosted)
    settle(hosted, st)
    grad_own("w_ff1", reducing(w_ff2=4))
    hosted = reducing(w_ff2=1)
    (dpre1, dpre1b, g_ln1_gain, g_ln1_bias), st = _ln1_bwd(dh1, xhat1, rstd1, ln1_gain, stages=hosted)
    settle(hosted, st)
    grad_for_sibling("w_out", merged, dpre1b, reducing(w_ff1=1))

    def merge_bwd(acc, ga, gb, ya_, yb_):
        sa, sb = _sigmoid(ga), _sigmoid(gb)
        return acc * sa, acc * sb, acc * ya_ * (sa * (1.0 - sa)), acc * yb_ * (sb * (1.0 - sb))

    hosted = reducing(w_ff1=6) + [to_sibling("w_out")]
    (dya, dyb, dga, dgb), st = _matmul("d_merge", dpre1b, here("w_out"), "nt", [BF16] * 4, merge_bwd,
                                       [(rest, "mn", gate_a), (rest, "mn", gate_b), (ya, "mn", 0), (yb, "mn", 0)], tn=256, stages=hosted)
    settle(hosted, st)
    grad_own("w_out", reducing(w_ff1=1))
    grad_for_sibling("w_proj_a", attn_b, dya)
    grad_for_sibling("w_proj_b", gmlp, dyb)
    hosted = reducing(w_ff1=1) + [to_sibling("w_proj_a"), to_sibling("w_proj_b")]
    (dattn,), st = _matmul("d_attn", dya, here("w_proj_a"), "nt", [F32], stages=hosted)
    settle(hosted, st)
    grad_own("w_proj_a")
    grad_own("w_proj_b")
    hosted = reducing(w_ff1=1)
    (dgmlp,), st = _matmul("d_gmlp", dyb, here("w_proj_b"), "nt", [F32], stages=hosted)
    settle(hosted, st)
    hosted = reducing(w_ff1=2)
    (du, dvb, dws_t, dbs_t, g_lnv_gain, g_lnv_bias), st = _gmlp_bwd(rest, dgmlp, ln_v_gain, ln_v_bias, ws_t, bs_t, causal, stages=hosted)
    settle(hosted, st)
    delta = _attn_delta(dattn, attn)
    bwd = []
    for p in range(3):
        hosted = reducing(**({"w_ff1": 4}, {"w_out": 16}, {"w_proj_a": 16, "w_proj_b": 16})[p])
        res, st = _attn_bwd(qkv, dattn, lse, delta, bias, p, stages=hosted)
        settle(hosted, st)
        bwd.append(res)
    g_rel_bias = _rel_bias_grad([b[3] for b in bwd], buckets)
    dproj = _assemble_dproj([b[i] for i in range(3) for b in bwd], du, dvb, dga, dgb)

    g_w_spatial = dws_t.reshape(N_GROUPS, N_SUB, g8, N_SUB, g8).transpose(0, 2, 1, 4, 3)
    g_b_spatial = dbs_t[:, :N_GROUPS].reshape(N_SUB, g8, N_GROUPS).transpose(2, 1, 0)
    part = _pack(dict(loss=loss_part, rel_bias=g_rel_bias, ln_v_gain=g_lnv_gain, ln_v_bias=g_lnv_bias, w_spatial=g_w_spatial,
                      b_spatial=g_b_spatial, ln1_gain=g_ln1_gain, ln1_bias=g_ln1_bias, b_ff1=g_b_ff1, b_ff2=g_b_ff2,
                      ln2_gain=g_ln2_gain, ln2_bias=g_ln2_bias))
    small = _small_stage(part)
    small.store = keep(sib, "small")
    grad_for_sibling("w_in", xb, dproj, [small])
    parts = sib["small"]
    half = SEQ // 2

    def add_residual(acc, d_):
        return (acc + ALPHA * d_,)

    hosted = [to_sibling("w_in")]
    (dx0,), st = _matmul("d_x0", dproj, here("w_in"), "nt", [F32], add_residual, [(dpre1, "mn", 0)], tm=1024, tn=1024, tk=3072, m=half,
                         stages=hosted)
    settle(hosted, st)
    grad_own("w_in")
    hosted = reducing(w_in=4)
    (dx1,), st = _matmul("d_x1", dproj, here("w_in"), "nt", [F32], add_residual, [(dpre1, "mn", 0)], tm=1024, tn=1024, tk=3072, m_off=half, m=half,
                         stages=hosted)
    settle(hosted, st)
    grad_x = _from_perm(jnp.concatenate([dx0, dx1], axis=0))[None]

    out_g, out_d, out_m, out_v = {}, {}, {}, {}
    for n, units in (("w_ff2", 3), ("w_ff1", 3), ("w_out", 3), ("w_proj_a", 2), ("w_proj_b", 1), ("w_in", 0)):
        hosted = reducing(w_in=units) if units else []
        (g, d, nm, nv), st = _adam_shard(f"adam_{n}", pair[n], summed(n), chip, weights[n][0], mom1[n][0], mom2[n][0], stages=hosted)
        settle(hosted, st)
        out_g[n], out_d[n], out_m[n], out_v[n] = g[None], d[None], nm[None], nv[None]

    zero = jnp.zeros((1,), F32)
    sg, sd, sm, sv = (_unpack(b) for b in _adam_small(
        part, parts, me, _pack({**weights, "loss": zero}), _pack({**mom1, "loss": zero}), _pack({**mom2, "loss": zero})))
    for n in WEIGHT_ORDER:
        if n not in KINDS:
            shape = weights[n].shape
            out_g[n], out_d[n], out_m[n], out_v[n] = (t[n].reshape(shape) for t in (sg, sd, sm, sv))
    loss = sg["loss"].reshape(())
    return (loss, grad_x, *[out_g[n] for n in WEIGHT_ORDER], *[out_d[n] for n in WEIGHT_ORDER],
            *[out_m[n] for n in WEIGHT_ORDER], *[out_v[n] for n in WEIGHT_ORDER])
```

```python
import math

import jax
import jax.numpy as jnp
import numpy as np
from jax import lax
from jax.experimental import pallas as pl
from jax.experimental.pallas import tpu as pltpu

F32 = jnp.float32
BF16 = jnp.bfloat16
I32 = jnp.int32

SEQ = 2048
D_MODEL = 2048
HEAD_DIM = 128
N_HEADS = 8
N_GROUPS = 8
D_FF = 4 * D_MODEL
BLOCK = 128
DILATIONS = (1, 4, 16)
N_BUCKETS = 32
MAX_DISTANCE = 2048
ALPHA = 2.0 ** 0.25
LN_EPS = 1e-5
NEG_INF = -1e30
N_DEV = 8
N_CHIPS = 4
N_SUB = 16
ADAM_LR, ADAM_B1, ADAM_B2, ADAM_EPS, ADAM_WD, ADAM_STEP = 0.001, 0.9, 0.999, 1e-08, 0.01, 10
LANES = 128
SUBLANES = 8
VMEM_LIMIT = 56 * 1024 * 1024
MESH = pl.DeviceIdType.MESH
ANY = pl.BlockSpec(memory_space=pl.ANY)
WEIGHT_ORDER = ("w_in", "rel_bias", "ln_v_gain", "ln_v_bias", "w_spatial", "b_spatial", "w_proj_a", "w_proj_b", "w_out",
                "ln1_gain", "ln1_bias", "w_ff1", "b_ff1", "w_ff2", "b_ff2", "ln2_gain", "ln2_bias")
KINDS = {"w_in": "col", "w_proj_a": "col", "w_proj_b": "col", "w_out": "row", "w_ff1": "col", "w_ff2": "row"}


def _d_a():
    return N_HEADS * HEAD_DIM


def _d_b():
    return N_GROUPS * BLOCK


def _d_in():
    return 3 * _d_a() + 2 * _d_b() + 2 * D_MODEL


def _pick(t, n, *others):
    if n <= t and all(o % n == 0 for o in others):
        return n
    for c in range(min(t, n) // LANES * LANES, 0, -LANES):
        if n % c == 0 and all(o % c == 0 for o in others):
            return c
    raise ValueError((t, n, others))


class _Stage:
    def __init__(self, ins, outs, alias, sems, start, finish):
        self.ins, self.outs, self.alias, self.sems, self.start, self.finish = ins, outs, alias, sems, start, finish


def _call(name, body, grid, in_specs, out_specs, out_shape, operands, scratch=(), sem=None, stages=(), sequential=False, prefetch=None,
          shown=False):
    n_in, n_out, n_sc = len(in_specs), len(out_specs), len(scratch)
    st_in = [len(s.ins) for s in stages]
    st_out = [len(s.outs) for s in stages]
    st_sem = [len(s.sems) for s in stages]
    n_pre = 0 if prefetch is None else 1
    aliases, ioff, ooff = {}, n_in + n_pre, n_out
    for s, ni, no in zip(stages, st_in, st_out):
        for i, o in s.alias.items():
            aliases[ioff + i] = ooff + o
        ioff, ooff = ioff + ni, ooff + no

    def split(refs, counts):
        out, at = [], 0
        for c in counts:
            out.append(refs[at:at + c])
            at += c
        return out

    def wrapped(*refs):
        ins, sins, outs, souts, sc, ssems = split(refs[n_pre:], [n_in, sum(st_in), n_out, sum(st_out), n_sc, sum(st_sem)])
        parts = list(zip(stages, split(sins, st_in), split(souts, st_out), split(ssems, st_sem)))
        if sequential:
            for s, a, b, c in parts:
                s.start(a, b, c)
                s.finish(a, b, c)
            return
        if parts:
            first = _all_of([pl.program_id(i) == 0 for i in range(len(grid))])
            last = _all_of([pl.program_id(i) == g - 1 for i, g in enumerate(grid)])

            @pl.when(first)
            def _():
                for s, a, b, c in parts:
                    s.start(a, b, c)

        body(*(refs[:n_pre] if shown else ()), *ins, *outs, *sc)
        if parts:
            @pl.when(last)
            def _():
                for s, a, b, c in parts:
                    s.finish(a, b, c)

    if stages or sem is None:
        sem = ("arbitrary",) * len(grid)
    specs = dict(grid=grid, in_specs=list(in_specs) + [ANY] * sum(st_in), out_specs=list(out_specs) + [ANY] * sum(st_out),
                 scratch_shapes=list(scratch) + [x for s in stages for x in s.sems])
    if prefetch is not None:
        specs = dict(grid_spec=pltpu.PrefetchScalarGridSpec(num_scalar_prefetch=1, **specs))
    res = pl.pallas_call(
        wrapped, name=name, out_shape=list(out_shape) + [o for s in stages for o in s.outs], input_output_aliases=aliases,
        compiler_params=pltpu.CompilerParams(dimension_semantics=sem, vmem_limit_bytes=VMEM_LIMIT), **specs,
    )(*([prefetch] if n_pre else []), *operands, *[a for s in stages for a in s.ins])
    res = list(res)
    return res[:n_out], split(res[n_out:], st_out)


def _all_of(conds):
    out = conds[0]
    for c in conds[1:]:
        out = out & c
    return out


def _coords():
    return lax.axis_index("x"), lax.axis_index("y"), lax.axis_index("c")


def _other_chips(x, y):
    return ((1 - x, y), (x, 1 - y), (1 - x, 1 - y))


def _lin(dev):
    return 4 * dev[0] + 2 * dev[1] + dev[2]


def _piece(total, lo, n, units=16):
    assert total % units == 0
    return lo * (total // units), n * (total // units)


FLOWS = 4


def _split(lo, cnt):
    k = next(k for k in (FLOWS, 2, 1) if cnt % (2 * SUBLANES * k) == 0)
    return [(lo + i * (cnt // k), cnt // k) for i in range(k)]


def _remote(src, dst, send, recv, to):
    return pltpu.make_async_remote_copy(src_ref=src, dst_ref=dst, send_sem=send, recv_sem=recv, device_id=to, device_id_type=MESH)


def _placer(kind, n, lo, cnt):
    def place(ref, dev):
        if kind == "col":
            return ref.at[pl.ds(lo, cnt), pl.ds(pl.multiple_of(_lin(dev) * n, LANES), n)]
        return ref.at[pl.ds(pl.multiple_of(_lin(dev) * n + lo, 2 * SUBLANES), cnt), :]
    return place


def _spread_stage(full, kind, piece=(0, 16), home=False):
    n = (full.shape[1] if kind == "col" else full.shape[0]) // N_DEV
    lo, cnt = _piece(full.shape[0] if kind == "col" else n, *piece)
    parts = _split(lo, cnt)
    npeers = 1 if home else 2

    def copies(ins, outs, sems):
        send, recv = sems
        x, y, c = _coords()
        me = (x, y, c)
        peers = [(x, y, 1 - c)] if home else [(1 - x, y, c), (x, 1 - y, c)]
        out, arrive = [], []
        for k, t in enumerate(peers):
            for i, (plo, pcnt) in enumerate(parts):
                place = _placer(kind, n, plo, pcnt)
                out.append(_remote(place(outs[0], me), place(outs[0], me), send.at[i, k], recv.at[i, k], t))
                arrive.append(_remote(place(outs[0], t), place(outs[0], t), send.at[i, k], recv.at[i, k], t))
        return out, arrive

    def start(ins, outs, sems):
        for cp in copies(ins, outs, sems)[0]:
            cp.start()

    def finish(ins, outs, sems):
        out, arrive = copies(ins, outs, sems)
        for cp in arrive:
            cp.wait_recv()
        for cp in out:
            cp.wait_send()

    return _Stage([full], [jax.ShapeDtypeStruct(full.shape, full.dtype)], {0: 0},
                  [pltpu.SemaphoreType.DMA((len(parts), npeers)), pltpu.SemaphoreType.DMA((len(parts), npeers))], start, finish)


def _relay_stage(full, kind, piece=(0, 16)):
    n = (full.shape[1] if kind == "col" else full.shape[0]) // N_DEV
    lo, cnt = _piece(full.shape[0] if kind == "col" else n, *piece)
    half = cnt // 2
    assert half % (2 * SUBLANES) == 0, (cnt, kind)
    tops, bottoms = _split(lo, half), _split(lo + half, half)

    def copies(ins, outs, sems):
        send, recv = sems
        x, y, c = _coords()
        xn, yn, dg = (1 - x, y, c), (x, 1 - y, c), (1 - x, 1 - y, c)
        out, arrive, k = [], [], 0
        for came_from, to, parts in ((yn, xn, tops), (xn, yn, bottoms)):
            for plo, pcnt in parts:
                place = _placer(kind, n, plo, pcnt)
                out.append(_remote(place(outs[0], came_from), place(outs[0], came_from), send.at[k], recv.at[k], to))
                arrive.append(_remote(place(outs[0], dg), place(outs[0], dg), send.at[k], recv.at[k], to))
                k += 1
        return out, arrive

    def start(ins, outs, sems):
        for cp in copies(ins, outs, sems)[0]:
            cp.start()

    def finish(ins, outs, sems):
        out, arrive = copies(ins, outs, sems)
        for cp in arrive:
            cp.wait_recv()
        for cp in out:
            cp.wait_send()

    return _Stage([full], [jax.ShapeDtypeStruct(full.shape, full.dtype)], {0: 0},
                  [pltpu.SemaphoreType.DMA((len(tops) + len(bottoms),)), pltpu.SemaphoreType.DMA((len(tops) + len(bottoms),))], start, finish)


def _forward_stage(full, kind, piece=(0, 16)):
    n = (full.shape[1] if kind == "col" else full.shape[0]) // N_DEV
    lo, cnt = _piece(full.shape[0] if kind == "col" else n, *piece)
    place = _placer(kind, n, lo, cnt)

    def copies(ins, outs, sems):
        send, recv = sems
        x, y, c = _coords()
        chips = _other_chips(x, y)
        out = [_remote(place(outs[0], (*chip, c)), place(outs[0], (*chip, c)), send.at[k], recv.at[k], (x, y, 1 - c)) for k, chip in enumerate(chips)]
        arrive = [_remote(place(outs[0], (*chip, 1 - c)), place(outs[0], (*chip, 1 - c)), send.at[k], recv.at[k], (x, y, 1 - c))
                  for k, chip in enumerate(chips)]
        return out, arrive

    def start(ins, outs, sems):
        for cp in copies(ins, outs, sems)[0]:
            cp.start()

    def finish(ins, outs, sems):
        out, arrive = copies(ins, outs, sems)
        for cp in arrive:
            cp.wait_recv()
        for cp in out:
            cp.wait_send()

    return _Stage([full], [jax.ShapeDtypeStruct(full.shape, full.dtype)], {0: 0},
                  [pltpu.SemaphoreType.DMA((3,)), pltpu.SemaphoreType.DMA((3,))], start, finish)


def _to_sibling_stage(theirs):
    def copies(ins, outs, sems):
        send, recv = sems
        x, y, c = _coords()
        return [_remote(ins[0].at[q], outs[0].at[q], send.at[q], recv.at[q], (x, y, 1 - c)) for q in range(N_CHIPS)]

    def start(ins, outs, sems):
        for cp in copies(ins, outs, sems):
            cp.start()

    def finish(ins, outs, sems):
        for cp in copies(ins, outs, sems):
            cp.wait()

    return _Stage([theirs], [jax.ShapeDtypeStruct(theirs.shape, BF16)], {},
                  [pltpu.SemaphoreType.DMA((N_CHIPS,)), pltpu.SemaphoreType.DMA((N_CHIPS,))], start, finish)


def _to_chips_stage(pair, dst, piece=(0, 16)):
    lo, cnt = _piece(pair.shape[1], *piece)
    parts = _split(lo, cnt)
    nsem = 3 * len(parts)

    def copies(ins, outs, sems):
        send, recv = sems
        x, y, c = _coords()
        mine = 2 * x + y
        out, arrive, k = [], [], 0
        for px, py in _other_chips(x, y):
            for plo, pcnt in parts:
                rows = pl.ds(plo, pcnt)
                out.append(_remote(ins[0].at[2 * px + py, rows, :], outs[0].at[mine, rows, :], send.at[k], recv.at[k], (px, py, c)))
                arrive.append(_remote(ins[0].at[2 * px + py, rows, :], outs[0].at[2 * px + py, rows, :], send.at[k], recv.at[k], (px, py, c)))
                k += 1
        return out, arrive

    def start(ins, outs, sems):
        for cp in copies(ins, outs, sems)[0]:
            cp.start()

    def finish(ins, outs, sems):
        out, arrive = copies(ins, outs, sems)
        for cp in arrive:
            cp.wait_recv()
        for cp in out:
            cp.wait_send()

    return _Stage([pair, dst], [jax.ShapeDtypeStruct(dst.shape, dst.dtype)], {1: 0},
                  [pltpu.SemaphoreType.DMA((nsem,)), pltpu.SemaphoreType.DMA((nsem,))], start, finish)


def _fuse(parts, ins, outs, alias):
    parts = [p for p in parts if p is not None]
    sems = [x for st, _, _ in parts for x in st.sems]

    def run(which):
        def go(i, o, s):
            refs, at = list(i) + list(o), 0
            for st, pi, po in parts:
                getattr(st, which)([refs[k] for k in pi], [refs[k] for k in po], s[at:at + len(st.sems)])
                at += len(st.sems)
        return go

    return _Stage(ins, [jax.ShapeDtypeStruct(o.shape, o.dtype) for o in outs], alias, sems, run("start"), run("finish"))


def _gather_stage(full, kind, new=None, relay=None, forward=None):
    return _fuse([(_spread_stage(full, kind, new), [0], [1]) if new else None,
                  (_relay_stage(full, kind, relay), [0], [1]) if relay else None,
                  (_spread_stage(full, kind, relay, home=True), [0], [1]) if relay else None,
                  (_forward_stage(full, kind, forward), [0], [1]) if forward else None], [full], [full], {0: 0})


def _small_stage(part):
    def copies(ins, outs, sems):
        send, recv = sems
        x, y, c = _coords()
        me = (x, y, c)
        peers = [(1 - x if k & 4 else x, 1 - y if k & 2 else y, 1 - c if k & 1 else c) for k in range(1, N_DEV)]
        out = [_remote(ins[0], outs[0].at[_lin(me)], send.at[k], recv.at[k], t) for k, t in enumerate(peers)]
        arrive = [_remote(ins[0], outs[0].at[_lin(t)], send.at[k], recv.at[k], t) for k, t in enumerate(peers)]
        return out, arrive

    def start(ins, outs, sems):
        for cp in copies(ins, outs, sems)[0]:
            cp.start()

    def finish(ins, outs, sems):
        out, arrive = copies(ins, outs, sems)
        for cp in arrive:
            cp.wait_recv()
        for cp in out:
            cp.wait_send()

    return _Stage([part], [jax.ShapeDtypeStruct((N_DEV, *part.shape), F32)], {},
                  [pltpu.SemaphoreType.DMA((N_DEV - 1,)), pltpu.SemaphoreType.DMA((N_DEV - 1,))], start, finish)


def _comm_only(name, stages):
    return _call(name, lambda: None, (1,), [], [], [], [], stages=stages, sequential=True)[1]


_GELU_C = math.sqrt(2.0 / math.pi)


def _gelu(x):
    return 0.5 * x * (1.0 + jnp.tanh(_GELU_C * (x + 0.044715 * x * x * x)))


def _gelu_grad(x):
    t = jnp.tanh(_GELU_C * (x + 0.044715 * x * x * x))
    return 0.5 * (1.0 + t) + 0.5 * x * (1.0 - t * t) * (_GELU_C * (1.0 + 3.0 * 0.044715 * x * x))


def _sigmoid(x):
    return 1.0 / (1.0 + jnp.exp(-x))


def _dot(a, b, mode):
    dims = {"nn": (((1,), (0,)), ((), ())), "nt": (((1,), (1,)), ((), ())), "tn": (((0,), (0,)), ((), ()))}[mode]
    return lax.dot_general(a.astype(BF16), b.astype(BF16), dims, preferred_element_type=F32)


def _matmul(name, a, b, mode, outs, epi=None, extras=(), colsums=(), tm=2048, tn=512, tk=2048, b_off=0, n=None, m_off=0, m=None, stages=()):
    if mode == "tn":
        kk, mfull = a.shape
    else:
        mfull, kk = a.shape
    m = mfull if m is None else m
    n = (b.shape[0] if mode == "nt" else b.shape[1]) if n is None else n
    tm, tk = _pick(tm, m, m_off), _pick(tk, kk)
    tn = _pick(tn, n, b_off, *[off for _, _, off in extras])
    boff, moff = b_off // tn, m_off // tm
    nm, nn_, nk = m // tm, n // tn, kk // tk
    col_major = bool(colsums)
    grid = (nn_, nm, nk) if col_major else (nm, nn_, nk)

    def imap(f):
        if col_major:
            return lambda g0, g1, k: f(g1, g0, k)
        return f

    a_spec = (pl.BlockSpec((tk, tm), imap(lambda i, j, k: (k, i + moff))) if mode == "tn"
              else pl.BlockSpec((tm, tk), imap(lambda i, j, k: (i + moff, k))))
    b_spec = (pl.BlockSpec((tn, tk), imap(lambda i, j, k: (j + boff, k))) if mode == "nt"
              else pl.BlockSpec((tk, tn), imap(lambda i, j, k: (k, j + boff))))
    in_specs, operands = [a_spec, b_spec], [a, b]
    for arr, kind, off in extras:
        o = off // tn
        if kind == "mn":
            in_specs.append(pl.BlockSpec((tm, tn), imap(lambda i, j, k, o=o: (i + moff, j + o))))
        else:
            in_specs.append(pl.BlockSpec((1, tn), imap(lambda i, j, k, o=o: (0, j + o))))
        operands.append(arr)
    out_shape = [jax.ShapeDtypeStruct((m, n), dt) for dt in outs] + [jax.ShapeDtypeStruct((1, n), F32) for _ in colsums]
    out_specs = ([pl.BlockSpec((tm, tn), imap(lambda i, j, k: (i, j))) for _ in outs]
                 + [pl.BlockSpec((1, tn), imap(lambda i, j, k: (0, j))) for _ in colsums])
    n_ex, n_out, n_cs = len(extras), len(outs), len(colsums)

    def body(*refs):
        a_ref, b_ref = refs[:2]
        ex_refs = refs[2:2 + n_ex]
        out_refs = refs[2 + n_ex:2 + n_ex + n_out]
        cs_refs = refs[2 + n_ex + n_out:2 + n_ex + n_out + n_cs]
        part = _dot(a_ref[...], b_ref[...], mode)

        def finish(acc):
            res = epi(acc, *[r[...] for r in ex_refs]) if epi is not None else (acc,)
            for r, v in zip(out_refs, res[:n_out]):
                r[...] = v.astype(r.dtype)
            if n_cs:
                @pl.when(pl.program_id(1) == 0)
                def _():
                    for r in cs_refs:
                        r[...] = jnp.zeros_like(r)

                for r, idx in zip(cs_refs, colsums):
                    r[...] += jnp.sum(res[idx], axis=0, keepdims=True)

        if nk == 1:
            finish(part)
        else:
            acc_ref = refs[-1]
            k = pl.program_id(2)

            @pl.when(k == 0)
            def _():
                acc_ref[...] = part

            @pl.when(k > 0)
            def _():
                acc_ref[...] += part

            @pl.when(k == nk - 1)
            def _():
                finish(acc_ref[...])

    sem = ("arbitrary", "arbitrary", "arbitrary") if col_major else ("parallel", "parallel", "arbitrary")
    return _call(name, body, grid, in_specs, out_specs, out_shape, operands,
                 scratch=[pltpu.VMEM((tm, tn), F32)] if nk > 1 else [], sem=sem, stages=stages)


def _row_spec(tr, c):
    return pl.BlockSpec((tr, c), lambda i: (i, 0))


def _fix_spec(shape):
    return pl.BlockSpec(shape, lambda *_: tuple(0 for _ in shape))


def _cast_bf16(name, x, tr=512):
    r, c = x.shape
    tr = _pick(tr, r)

    def body(x_ref, o_ref):
        o_ref[...] = x_ref[...].astype(BF16)

    return _call(name, body, (r // tr,), [_row_spec(tr, c)], [_row_spec(tr, c)], [jax.ShapeDtypeStruct((r, c), BF16)], [x],
                 sem=("parallel",))[0][0]


def _cast_into_place(name, w, kind, me, tr=512):
    r, c = w.shape
    tr = _pick(tr, r)
    nb = r // tr
    if kind == "col":
        o_spec = pl.BlockSpec((tr, c), lambda i, me_ref: (i, me_ref[0]))
        shape = (r, c * N_DEV)
    else:
        o_spec = pl.BlockSpec((tr, c), lambda i, me_ref: (me_ref[0] * nb + i, 0))
        shape = (r * N_DEV, c)

    def body(x_ref, o_ref):
        o_ref[...] = x_ref[...].astype(BF16)

    return _call(name, body, (nb,), [pl.BlockSpec((tr, c), lambda i, me_ref: (i, 0))], [o_spec], [jax.ShapeDtypeStruct(shape, BF16)], [w],
                 sem=("parallel",), prefetch=me)[0][0]


def _layer_norm_stats(x):
    mean = jnp.mean(x, axis=-1, keepdims=True)
    xc = x - mean
    var = jnp.mean(xc * xc, axis=-1, keepdims=True)
    rstd = lax.rsqrt(var + LN_EPS)
    return xc * rstd, rstd


def _layer_norm_bwd(dxhat, xhat, rstd):
    m1 = jnp.mean(dxhat, axis=-1, keepdims=True)
    m2 = jnp.mean(dxhat * xhat, axis=-1, keepdims=True)
    return rstd * (dxhat - m1 - xhat * m2)


def _ln1_fwd(pre1, g1, b1, tr=256, stages=()):
    s, d = pre1.shape
    tr = _pick(tr, s)

    def body(p_ref, g_ref, b_ref, xh_ref, rs_ref, h_ref):
        xhat, rstd = _layer_norm_stats(p_ref[...])
        xh_ref[...] = xhat
        rs_ref[...] = rstd
        h_ref[...] = (xhat * g_ref[...] + b_ref[...]).astype(BF16)

    return _call("ln1_fwd", body, (s // tr,), [_row_spec(tr, d), _fix_spec((1, d)), _fix_spec((1, d))],
                 [_row_spec(tr, d), _row_spec(tr, 1), _row_spec(tr, d)],
                 [jax.ShapeDtypeStruct((s, d), F32), jax.ShapeDtypeStruct((s, 1), F32), jax.ShapeDtypeStruct((s, d), BF16)],
                 [pre1, g1, b1], sem=("parallel",), stages=stages)


def _ln2_loss_bwd(ff, xhat1, g1, b1, g2, b2, target, tr=256):
    s, d = ff.shape
    tr = _pick(tr, s)

    def body(ff_ref, xh1_ref, g1_ref, b1_ref, g2_ref, b2_ref, t_ref, dp_ref, dpb_ref, dg_ref, db_ref, dbf_ref, loss_ref):
        @pl.when(pl.program_id(0) == 0)
        def _():
            dg_ref[...] = jnp.zeros_like(dg_ref)
            db_ref[...] = jnp.zeros_like(db_ref)
            dbf_ref[...] = jnp.zeros_like(dbf_ref)
            loss_ref[...] = jnp.zeros_like(loss_ref)

        h1 = xh1_ref[...] * g1_ref[...] + b1_ref[...]
        xhat, rstd = _layer_norm_stats(ALPHA * h1 + ff_ref[...])
        err = xhat * g2_ref[...] + b2_ref[...] - t_ref[...]
        row = jnp.mean(err * err, axis=-1, keepdims=True)
        loss_ref[...] += 0.5 * jnp.sum(row, axis=0, keepdims=True)
        dy = err / d
        dg_ref[...] += jnp.sum(dy * xhat, axis=0, keepdims=True)
        db_ref[...] += jnp.sum(dy, axis=0, keepdims=True)
        dpre = _layer_norm_bwd(dy * g2_ref[...], xhat, rstd)
        dbf_ref[...] += jnp.sum(dpre, axis=0, keepdims=True)
        dp_ref[...] = dpre
        dpb_ref[...] = dpre.astype(BF16)

    vec = _fix_spec((1, d))
    return _call("ln2_loss_bwd", body, (s // tr,), [_row_spec(tr, d), _row_spec(tr, d), vec, vec, vec, vec, _row_spec(tr, d)],
                 [_row_spec(tr, d), _row_spec(tr, d), vec, vec, vec, _fix_spec((1, 1))],
                 [jax.ShapeDtypeStruct((s, d), F32), jax.ShapeDtypeStruct((s, d), BF16)]
                 + [jax.ShapeDtypeStruct((1, d), F32)] * 3 + [jax.ShapeDtypeStruct((1, 1), F32)],
                 [ff, xhat1, g1, b1, g2, b2, target])[0]


def _ln1_bwd(dh1, xhat1, rstd1, g1, tr=256, stages=()):
    s, d = dh1.shape
    tr = _pick(tr, s)

    def body(dh_ref, xh_ref, rs_ref, g_ref, dp_ref, dpb_ref, dg_ref, db_ref):
        @pl.when(pl.program_id(0) == 0)
        def _():
            dg_ref[...] = jnp.zeros_like(dg_ref)
            db_ref[...] = jnp.zeros_like(db_ref)

        dh, xhat = dh_ref[...], xh_ref[...]
        dg_ref[...] += jnp.sum(dh * xhat, axis=0, keepdims=True)
        db_ref[...] += jnp.sum(dh, axis=0, keepdims=True)
        dpre = _layer_norm_bwd(dh * g_ref[...], xhat, rs_ref[...])
        dp_ref[...] = dpre
        dpb_ref[...] = dpre.astype(BF16)

    vec = _fix_spec((1, d))
    return _call("ln1_bwd", body, (s // tr,), [_row_spec(tr, d), _row_spec(tr, d), _row_spec(tr, 1), vec],
                 [_row_spec(tr, d), _row_spec(tr, d), vec, vec],
                 [jax.ShapeDtypeStruct((s, d), F32), jax.ShapeDtypeStruct((s, d), BF16)] + [jax.ShapeDtypeStruct((1, d), F32)] * 2,
                 [dh1, xhat1, rstd1, g1], stages=stages)


def _to_perm(x):
    return x.reshape(SEQ // N_SUB, N_SUB, -1).transpose(1, 0, 2).reshape(SEQ, -1)


def _from_perm(x):
    return x.reshape(N_SUB, SEQ // N_SUB, -1).transpose(1, 0, 2).reshape(SEQ, -1)


def _local_index(p):
    rho = np.arange(BLOCK)
    if p == 0:
        return 16 * (rho % 8) + rho // 8
    if p == 1:
        return 4 * (rho % 32) + rho // 32
    return rho


def _tile_view(x, p):
    c = x.shape[1]
    if p == 1:
        return x.reshape(4, 4, BLOCK, c)
    return x.reshape(N_SUB, BLOCK, c)


def _view_shape(c, p):
    return (4, 4, BLOCK, c) if p == 1 else (N_SUB, BLOCK, c)


def _tile_spec(p, width, col, shift=0):
    nblk = SEQ // DILATIONS[p] // BLOCK

    def blk(n):
        return jnp.clip(n + shift, 0, nblk - 1)

    if p == 0:
        return pl.BlockSpec((N_SUB, SUBLANES, width), lambda s, n: (0, blk(n), col))
    if p == 1:
        return pl.BlockSpec((4, None, 32, width), lambda s, n: (0, s, blk(n), col))
    return pl.BlockSpec((None, BLOCK, width), lambda s, n: (s, 0, col))


def _tile_grid(p):
    return ((1, 16), (4, 4), (16, 1))[p]


def _t5_bucket(n):
    max_exact = N_BUCKETS // 2
    nf = np.maximum(n, 1).astype(np.float32)
    large = max_exact + (np.log(nf / np.float32(max_exact)) / np.float32(math.log(MAX_DISTANCE / max_exact))
                         * np.float32(N_BUCKETS - max_exact)).astype(np.int32)
    large = np.minimum(large, N_BUCKETS - 1)
    return np.where(n < max_exact, n, large).astype(np.int32)


def _bucket_tables():
    tabs = np.zeros((3, 2, BLOCK, BLOCK), np.int32)
    for p, d in enumerate(DILATIONS):
        i = _local_index(p)
        diff = i[:, None] - i[None, :]
        tabs[p, 0] = np.where(diff <= 0, _t5_bucket((BLOCK + diff) * d), -1)
        tabs[p, 1] = np.where(diff >= 0, _t5_bucket(np.maximum(diff, 0) * d), -1)
    return tabs


def _bias_expand(rel_bias, buckets):
    nh = N_HEADS

    def body(rb_ref, bk_ref, o_ref):
        for w in range(2):
            bk = bk_ref[0, w]
            for h in range(nh):
                val = jnp.zeros((BLOCK, BLOCK), F32)
                for b in range(N_BUCKETS):
                    val = jnp.where(bk == b, rb_ref[b, h], val)
                o_ref[0, h, w] = jnp.where(bk < 0, NEG_INF, val)

    return _call("bias_expand", body, (3,),
                 [pl.BlockSpec(memory_space=pltpu.SMEM), pl.BlockSpec((1, 2, BLOCK, BLOCK), lambda p: (p, 0, 0, 0))],
                 [pl.BlockSpec((1, nh, 2, BLOCK, BLOCK), lambda p: (p, 0, 0, 0, 0))],
                 [jax.ShapeDtypeStruct((3, nh, 2, BLOCK, BLOCK), F32)], [rel_bias, buckets], sem=("parallel",))[0][0]


def _heads_to_lanes(cols):
    lane = lax.broadcasted_iota(I32, (BLOCK, LANES), 1)
    out = jnp.zeros((BLOCK, LANES), F32)
    for h, c in enumerate(cols):
        out = jnp.where(lane == h, c, out)
    return out


def _attn_fwd(qkv, bias, p, stages=()):
    d_a = _d_a()
    has_prev = SEQ // DILATIONS[p] // BLOCK > 1
    scale = HEAD_DIM ** -0.5
    view = _tile_view(qkv, p)

    width = 2 * BLOCK if has_prev else BLOCK

    def body(q_ref, kc_ref, kp_ref, vc_ref, vp_ref, b_ref, o_ref, l_ref, s_ref, p_ref):
        n = pl.program_id(1)
        q_all = q_ref[...].reshape(BLOCK, d_a).astype(BF16)
        k_all = kc_ref[...].reshape(BLOCK, d_a).astype(BF16)
        v_all = vc_ref[...].reshape(BLOCK, d_a).astype(BF16)
        if has_prev:
            k_all = jnp.concatenate([kp_ref[...].reshape(BLOCK, d_a).astype(BF16), k_all], axis=0)
            v_all = jnp.concatenate([vp_ref[...].reshape(BLOCK, d_a).astype(BF16), v_all], axis=0)
            no_prev = (lax.broadcasted_iota(I32, (BLOCK, width), 1) < BLOCK) & (n == 0)
        for h in range(N_HEADS):
            sl = slice(h * HEAD_DIM, (h + 1) * HEAD_DIM)
            s = _dot(q_all[:, sl], k_all[:, sl], "nt") * scale
            if has_prev:
                s = jnp.where(no_prev, NEG_INF, s + jnp.concatenate([b_ref[0, h, 0], b_ref[0, h, 1]], axis=1))
            else:
                s = s + b_ref[0, h, 1]
            s_ref[h] = s
        dens, lses = [], []
        for h in range(N_HEADS):
            s = s_ref[h]
            m = jnp.max(s, axis=-1, keepdims=True)
            pr = jnp.exp(s - m)
            den = jnp.sum(pr, axis=-1, keepdims=True)
            p_ref[h] = pr.astype(BF16)
            dens.append(den)
            lses.append(m + jnp.log(den))
        for h in range(N_HEADS):
            sl = slice(h * HEAD_DIM, (h + 1) * HEAD_DIM)
            o_ref[..., sl] = (_dot(p_ref[h], v_all[:, sl], "nn") / dens[h]).reshape(*o_ref.shape[:-1], HEAD_DIM)
        l_ref[...] = _heads_to_lanes(lses).reshape(l_ref.shape)

    (o, l), st = _call(
        f"attn_fwd{p}", body, _tile_grid(p),
        [_tile_spec(p, d_a, 0), _tile_spec(p, d_a, 1), _tile_spec(p, d_a, 1, -1), _tile_spec(p, d_a, 2), _tile_spec(p, d_a, 2, -1),
         pl.BlockSpec((1, N_HEADS, 2, BLOCK, BLOCK), lambda s, n: (p, 0, 0, 0, 0))],
        [_tile_spec(p, d_a, 0), _tile_spec(p, LANES, 0)],
        [jax.ShapeDtypeStruct(_view_shape(d_a, p), F32), jax.ShapeDtypeStruct(_view_shape(LANES, p), F32)],
        [view, view, view, view, view, bias], scratch=[pltpu.VMEM((N_HEADS, BLOCK, width), F32), pltpu.VMEM((N_HEADS, BLOCK, width), BF16)],
        sem=("parallel", "parallel"), stages=stages)
    return (o.reshape(SEQ, d_a), l.reshape(SEQ, LANES)), st


def _attn_combine(os_, ls_, tr=256, stages=()):
    d_a = _d_a()
    tr = _pick(tr, SEQ)

    def body(o0, o1, o2, l0, l1, l2, a_ref, ab_ref, lt_ref):
        l = [l0[...], l1[...], l2[...]]
        m = jnp.maximum(jnp.maximum(l[0], l[1]), l[2])
        w = [jnp.exp(x - m) for x in l]
        tot = w[0] + w[1] + w[2]
        lt_ref[...] = m + jnp.log(tot)
        w = [x / tot for x in w]
        for h in range(N_HEADS):
            sl = slice(h * HEAD_DIM, (h + 1) * HEAD_DIM)
            acc = w[0][:, h:h + 1] * o0[:, sl] + w[1][:, h:h + 1] * o1[:, sl] + w[2][:, h:h + 1] * o2[:, sl]
            a_ref[:, sl] = acc
            ab_ref[:, sl] = acc.astype(BF16)

    return _call("attn_combine", body, (SEQ // tr,), [_row_spec(tr, d_a)] * 3 + [_row_spec(tr, LANES)] * 3,
                 [_row_spec(tr, d_a), _row_spec(tr, d_a), _row_spec(tr, LANES)],
                 [jax.ShapeDtypeStruct((SEQ, d_a), F32), jax.ShapeDtypeStruct((SEQ, d_a), BF16), jax.ShapeDtypeStruct((SEQ, LANES), F32)],
                 [*os_, *ls_], sem=("parallel",), stages=stages)


def _attn_delta(dattn, attn, tr=256):
    d_a = _d_a()
    tr = _pick(tr, SEQ)

    def body(d_ref, a_ref, o_ref):
        prod = d_ref[...] * a_ref[...]
        lane = lax.broadcasted_iota(I32, (tr, LANES), 1)
        out = jnp.zeros((tr, LANES), F32)
        for h in range(N_HEADS):
            out = jnp.where(lane == h, jnp.sum(prod[:, h * HEAD_DIM:(h + 1) * HEAD_DIM], axis=-1, keepdims=True), out)
        o_ref[...] = out

    return _call("attn_delta", body, (SEQ // tr,), [_row_spec(tr, d_a)] * 2, [_row_spec(tr, LANES)],
                 [jax.ShapeDtypeStruct((SEQ, LANES), F32)], [dattn, attn], sem=("parallel",))[0][0]


def _attn_bwd(qkv, dattn, lse, delta, bias, p, stages=()):
    d_a = _d_a()
    nblk = SEQ // DILATIONS[p] // BLOCK
    has_next = nblk > 1
    scale = HEAD_DIM ** -0.5
    qv, dov, lv, tv = (_tile_view(x, p) for x in (qkv, dattn, lse, delta))

    def body(q_ref, qn_ref, k_ref, v_ref, do_ref, don_ref, l_ref, ln_ref, t_ref, tn_ref, b_ref, dq_ref, dk_ref, dv_ref, db_ref, carry_ref):
        j = pl.program_id(1)

        @pl.when((pl.program_id(0) == 0) & (j == 0))
        def _():
            db_ref[...] = jnp.zeros_like(db_ref)

        k_all = k_ref[...].reshape(BLOCK, d_a).astype(BF16)
        v_all = v_ref[...].reshape(BLOCK, d_a).astype(BF16)

        def side(qr, dor, lr, tr_, w):
            q_all = qr[...].reshape(BLOCK, d_a).astype(BF16)
            do_all = dor[...].reshape(BLOCK, d_a).astype(BF16)
            l_all = lr[...].reshape(BLOCK, LANES)
            t_all = tr_[...].reshape(BLOCK, LANES)
            dqs, dks, dvs = [], [], []
            for h in range(N_HEADS):
                sl = slice(h * HEAD_DIM, (h + 1) * HEAD_DIM)
                s = _dot(q_all[:, sl], k_all[:, sl], "nt") * scale + b_ref[0, h, w]
                pr = jnp.exp(s - l_all[:, h:h + 1])
                dp = _dot(do_all[:, sl], v_all[:, sl], "nt")
                ds = pr * (dp - t_all[:, h:h + 1])
                db_ref[h, w] += ds
                dqs.append(_dot(ds, k_all[:, sl], "nn") * scale)
                dks.append(_dot(ds, q_all[:, sl], "tn") * scale)
                dvs.append(_dot(pr, do_all[:, sl], "tn"))
            return [jnp.concatenate(x, axis=-1) for x in (dqs, dks, dvs)]

        dq_c, dk_c, dv_c = side(q_ref, do_ref, l_ref, t_ref, 1)
        if has_next:
            dq_ref[...] = (jnp.where(j > 0, carry_ref[...], 0.0) + dq_c).reshape(dq_ref.shape)
            not_last = j < nblk - 1

            @pl.when(not_last)
            def _():
                dq_n, dk_n, dv_n = side(qn_ref, don_ref, ln_ref, tn_ref, 0)
                carry_ref[...] = dq_n
                dk_ref[...] = (dk_c + dk_n).reshape(dk_ref.shape)
                dv_ref[...] = (dv_c + dv_n).reshape(dv_ref.shape)

            @pl.when(jnp.logical_not(not_last))
            def _():
                dk_ref[...] = dk_c.reshape(dk_ref.shape)
                dv_ref[...] = dv_c.reshape(dv_ref.shape)
        else:
            dq_ref[...] = dq_c.reshape(dq_ref.shape)
            dk_ref[...] = dk_c.reshape(dk_ref.shape)
            dv_ref[...] = dv_c.reshape(dv_ref.shape)

    def big(col, shift=0):
        return _tile_spec(p, d_a, col, shift)

    def small(shift=0):
        return _tile_spec(p, LANES, 0, shift)

    (dq, dk, dv, dbias), st = _call(
        f"attn_bwd{p}", body, _tile_grid(p),
        [big(0), big(0, 1), big(1), big(2), big(0), big(0, 1), small(), small(1), small(), small(1),
         pl.BlockSpec((1, N_HEADS, 2, BLOCK, BLOCK), lambda s, n: (p, 0, 0, 0, 0))],
        [big(0), big(0), big(0), pl.BlockSpec((N_HEADS, 2, BLOCK, BLOCK), lambda s, n: (0, 0, 0, 0))],
        [jax.ShapeDtypeStruct(_view_shape(d_a, p), F32)] * 3 + [jax.ShapeDtypeStruct((N_HEADS, 2, BLOCK, BLOCK), F32)],
        [qv, qv, qv, qv, dov, dov, lv, lv, tv, tv, bias], scratch=[pltpu.VMEM((BLOCK, d_a), F32)], stages=stages)
    return (dq.reshape(SEQ, d_a), dk.reshape(SEQ, d_a), dv.reshape(SEQ, d_a), dbias), st


def _rel_bias_grad(dbias, buckets):
    nh = N_HEADS

    def body(d0, d1, d2, bk_ref, o_ref, t_ref):
        ds = (d0, d1, d2)

        def per_bucket(b, carry):
            for h in range(nh):
                acc = jnp.zeros((BLOCK, BLOCK), F32)
                for p in range(3):
                    for w in range(2):
                        acc = acc + jnp.where(bk_ref[p, w] == b, ds[p][h, w], 0.0)
                t_ref[pl.ds(b * nh + h, 1), :] = jnp.sum(acc, axis=0, keepdims=True)
            return carry

        lax.fori_loop(0, N_BUCKETS, per_bucket, 0)
        o_ref[...] = jnp.sum(t_ref[...], axis=-1, keepdims=True)

    return _call("rel_bias_grad", body, (1,), [_fix_spec((nh, 2, BLOCK, BLOCK))] * 3 + [_fix_spec((3, 2, BLOCK, BLOCK))],
                 [_fix_spec((N_BUCKETS * nh, 1))], [jax.ShapeDtypeStruct((N_BUCKETS * nh, 1), F32)], [*dbias, buckets],
                 scratch=[pltpu.VMEM((N_BUCKETS * nh, LANES), F32)])[0][0]


def _gmlp_fwd(rest, gain, bias, ws, bs, causal, stages=()):
    d_b = _d_b()

    def body(u_ref, v_ref, g_ref, b_ref, ws_ref, bs_ref, c_ref, o_ref):
        u = u_ref[...].reshape(BLOCK, d_b)
        xhat, _ = _layer_norm_stats(_gelu(v_ref[...].reshape(BLOCK, d_b)))
        vn = (xhat * g_ref[...] + b_ref[...]).astype(BF16)
        outs = []
        for g in range(N_GROUPS):
            sl = slice(g * BLOCK, (g + 1) * BLOCK)
            w = jnp.where(c_ref[...] > 0, ws_ref[g], 0.0)
            z = _dot(w, vn[:, sl], "nn") + bs_ref[:, g:g + 1]
            outs.append(_gelu(u[:, sl]) * z)
        o_ref[...] = jnp.concatenate(outs, axis=-1).reshape(o_ref.shape)

    (out,), st = _call(
        "gmlp_fwd", body, (1, SEQ // BLOCK),
        [_tile_spec(0, d_b, 0), _tile_spec(0, d_b, 1), _fix_spec((1, d_b)), _fix_spec((1, d_b)),
         _fix_spec((N_GROUPS, BLOCK, BLOCK)), _fix_spec((BLOCK, N_GROUPS)), _fix_spec((BLOCK, BLOCK))],
        [_tile_spec(0, d_b, 0)], [jax.ShapeDtypeStruct(_view_shape(d_b, 0), F32)],
        [_tile_view(rest, 0), _tile_view(rest, 0), gain, bias, ws, bs, causal], sem=("parallel", "parallel"), stages=stages)
    return out.reshape(SEQ, d_b), st


def _gmlp_bwd(rest, dgmlp, gain, bias, ws, bs, causal, stages=()):
    d_b = _d_b()
    nchunk = SEQ // BLOCK

    def body(u_ref, v_ref, dg_ref, g_ref, b_ref, ws_ref, bs_ref, c_ref, du_ref, dv_ref, dws_ref, dbs_ref, dgain_ref, dbias_ref):
        c = pl.program_id(1)

        @pl.when(c == 0)
        def _():
            dws_ref[...] = jnp.zeros_like(dws_ref)
            dbs_ref[...] = jnp.zeros_like(dbs_ref)
            dgain_ref[...] = jnp.zeros_like(dgain_ref)
            dbias_ref[...] = jnp.zeros_like(dbias_ref)

        u = u_ref[...].reshape(BLOCK, d_b)
        v = v_ref[...].reshape(BLOCK, d_b)
        dgm = dg_ref[...].reshape(BLOCK, d_b)
        xhat, rstd = _layer_norm_stats(_gelu(v))
        vn = (xhat * g_ref[...] + b_ref[...]).astype(BF16)
        lane = lax.broadcasted_iota(I32, (BLOCK, LANES), 1)
        dus, dvns = [], []
        dbs = dbs_ref[...]
        for g in range(N_GROUPS):
            sl = slice(g * BLOCK, (g + 1) * BLOCK)
            w = jnp.where(c_ref[...] > 0, ws_ref[g], 0.0).astype(BF16)
            z = _dot(w, vn[:, sl], "nn") + bs_ref[:, g:g + 1]
            dz = dgm[:, sl] * _gelu(u[:, sl])
            dus.append(dgm[:, sl] * z * _gelu_grad(u[:, sl]))
            dws_ref[g] += _dot(dz, vn[:, sl], "nt")
            dbs = dbs + jnp.where(lane == g, jnp.sum(dz, axis=-1, keepdims=True), 0.0)
            dvns.append(_dot(w, dz, "tn"))
        dbs_ref[...] = dbs
        dvn = jnp.concatenate(dvns, axis=-1)
        dgain_ref[...] += jnp.sum(dvn * xhat, axis=0, keepdims=True)
        dbias_ref[...] += jnp.sum(dvn, axis=0, keepdims=True)
        dvg = _layer_norm_bwd(dvn * g_ref[...], xhat, rstd)
        du_ref[...] = jnp.concatenate(dus, axis=-1).reshape(du_ref.shape)
        dv_ref[...] = (dvg * _gelu_grad(v)).reshape(dv_ref.shape)

        @pl.when(c == nchunk - 1)
        def _():
            for g in range(N_GROUPS):
                dws_ref[g] = jnp.where(c_ref[...] > 0, dws_ref[g], 0.0)

    (du, dv, dws, dbs, dgain, dbias), st = _call(
        "gmlp_bwd", body, (1, nchunk),
        [_tile_spec(0, d_b, 0), _tile_spec(0, d_b, 1), _tile_spec(0, d_b, 0), _fix_spec((1, d_b)), _fix_spec((1, d_b)),
         _fix_spec((N_GROUPS, BLOCK, BLOCK)), _fix_spec((BLOCK, N_GROUPS)), _fix_spec((BLOCK, BLOCK))],
        [_tile_spec(0, d_b, 0), _tile_spec(0, d_b, 0), _fix_spec((N_GROUPS, BLOCK, BLOCK)), _fix_spec((BLOCK, LANES)),
         _fix_spec((1, d_b)), _fix_spec((1, d_b))],
        [jax.ShapeDtypeStruct(_view_shape(d_b, 0), F32)] * 2
        + [jax.ShapeDtypeStruct((N_GROUPS, BLOCK, BLOCK), F32), jax.ShapeDtypeStruct((BLOCK, LANES), F32)]
        + [jax.ShapeDtypeStruct((1, d_b), F32)] * 2,
        [_tile_view(rest, 0), _tile_view(rest, 0), _tile_view(dgmlp, 0), gain, bias, ws, bs, causal], stages=stages)
    return (du.reshape(SEQ, d_b), dv.reshape(SEQ, d_b), dws, dbs, dgain, dbias), st


def _assemble_dproj(dqkv, du, dv, dga, dgb, tr=128):
    d_a, d_b, d_in = _d_a(), _d_b(), _d_in()
    tr = _pick(tr, SEQ)

    def body(*refs):
        att, (du_ref, dv_ref, dga_ref, dgb_ref, o_ref) = refs[:9], refs[9:]
        for i in range(3):
            o_ref[:, i * d_a:(i + 1) * d_a] = (att[3 * i][...] + att[3 * i + 1][...] + att[3 * i + 2][...]).astype(BF16)
        o_ref[:, 3 * d_a:3 * d_a + d_b] = du_ref[...].astype(BF16)
        o_ref[:, 3 * d_a + d_b:3 * d_a + 2 * d_b] = dv_ref[...].astype(BF16)
        o_ref[:, 3 * d_a + 2 * d_b:3 * d_a + 2 * d_b + D_MODEL] = dga_ref[...]
        o_ref[:, 3 * d_a + 2 * d_b + D_MODEL:] = dgb_ref[...]

    return _call("assemble_dproj", body, (SEQ // tr,), [_row_spec(tr, d_a)] * 9 + [_row_spec(tr, d_b)] * 2 + [_row_spec(tr, D_MODEL)] * 2,
                 [_row_spec(tr, d_in)], [jax.ShapeDtypeStruct((SEQ, d_in), BF16)], [*dqkv, du, dv, dga, dgb], sem=("parallel",))[0][0]


def _dw(name, a, b, kind, core, mine, add=None, tn=1152, stages=()):
    s, m = a.shape
    n = b.shape[1]
    rs, cs = (m, n // N_DEV) if kind == "col" else (m // N_DEV, n)
    tn = _pick(tn if kind == "col" else 512, cs)
    nj = cs // tn

    def shard(q, c_ref):
        return 2 * q + (c_ref[0] if mine else 1 - c_ref[0])

    if kind == "col":
        a_spec = pl.BlockSpec((s, m), lambda q, j, c_ref: (0, 0))
        b_spec = pl.BlockSpec((s, tn), lambda q, j, c_ref: (0, shard(q, c_ref) * nj + j))
    else:
        a_spec = pl.BlockSpec((s, rs), lambda q, j, c_ref: (0, shard(q, c_ref)))
        b_spec = pl.BlockSpec((s, tn), lambda q, j, c_ref: (0, j))
    o_spec = pl.BlockSpec((None, rs, tn), lambda q, j, c_ref: (q, 0, j))

    def body(a_ref, b_ref, *rest):
        acc = _dot(a_ref[...], b_ref[...], "tn")
        if add is not None:
            acc = acc + rest[0][...].astype(F32)
        rest[-1][...] = acc.astype(BF16)

    (out,), st = _call(name, body, (N_CHIPS, nj), [a_spec, b_spec] + ([o_spec] if add is not None else []), [o_spec],
                       [jax.ShapeDtypeStruct((N_CHIPS, rs, cs), BF16)], [a, b] + ([add] if add is not None else []),
                       sem=("parallel", "parallel"), stages=stages, prefetch=core)
    return out, st


def _adamw(w, g, m, v):
    m = ADAM_B1 * m + (1.0 - ADAM_B1) * g
    v = ADAM_B2 * v + (1.0 - ADAM_B2) * (g * g)
    m_hat = m / (1.0 - ADAM_B1 ** ADAM_STEP)
    v_hat = v / (1.0 - ADAM_B2 ** ADAM_STEP)
    delta = -ADAM_LR * (m_hat / (jnp.sqrt(v_hat) + ADAM_EPS) + ADAM_WD * w)
    return delta, m, v


def _adam_shard(name, pair, chip_sums, chip, w, m, v, tr=256, stages=()):
    rs, cs = w.shape
    tr = _pick(tr, rs)

    def body(chip_ref, own_ref, *refs):
        slots, (w_ref, m_ref, v_ref, g_ref, d_ref, nm_ref, nv_ref) = refs[:N_CHIPS], refs[N_CHIPS:]
        g = None
        for q in range(N_CHIPS):
            term = jnp.where(chip_ref[0] == q, own_ref[...], slots[q][...]).astype(F32)
            g = term if g is None else g + term
        d, nm, nv = _adamw(w_ref[...], g, m_ref[...], v_ref[...])
        g_ref[...], d_ref[...], nm_ref[...], nv_ref[...] = g, d, nm, nv

    def slot(q):
        return pl.BlockSpec((None, tr, cs), lambda i, c_ref: (jnp.where(c_ref[0] == q, (q + 1) % N_CHIPS, q), i, 0))

    spec = pl.BlockSpec((tr, cs), lambda i, c_ref: (i, 0))
    return _call(name, body, (rs // tr,),
                 [pl.BlockSpec((None, tr, cs), lambda i, c_ref: (c_ref[0], i, 0))] + [slot(q) for q in range(N_CHIPS)] + [spec, spec, spec],
                 [spec] * 4, [jax.ShapeDtypeStruct((rs, cs), F32)] * 4, [pair] + [chip_sums] * N_CHIPS + [w, m, v],
                 sem=("parallel",), stages=stages, prefetch=chip, shown=True)


def _adam_small(part, parts, me, w, m, v):
    rows = w.shape[0]

    def body(me_ref, own_ref, *refs):
        slots, (w_ref, m_ref, v_ref, g_ref, d_ref, nm_ref, nv_ref) = refs[:N_DEV], refs[N_DEV:]
        g = None
        for j in range(N_DEV):
            term = jnp.where(me_ref[0] == j, own_ref[...], slots[j][...])
            g = term if g is None else g + term
        d, nm, nv = _adamw(w_ref[...], g, m_ref[...], v_ref[...])
        g_ref[...], d_ref[...], nm_ref[...], nv_ref[...] = g, d, nm, nv

    def slot(j):
        return pl.BlockSpec((None, rows, LANES), lambda i, me_ref: (jnp.where(me_ref[0] == j, (j + 1) % N_DEV, j), 0, 0))

    spec = _fix_spec((rows, LANES))
    return _call("adam_small", body, (1,), [spec] + [slot(j) for j in range(N_DEV)] + [spec, spec, spec], [spec] * 4,
                 [jax.ShapeDtypeStruct((rows, LANES), F32)] * 4, [part] + [parts] * N_DEV + [w, m, v], prefetch=me, shown=True)[0]


def _small_sizes():
    d_b = _d_b()
    return (("loss", 1), ("rel_bias", N_BUCKETS * N_HEADS), ("ln_v_gain", d_b), ("ln_v_bias", d_b),
            ("w_spatial", N_GROUPS * BLOCK * BLOCK), ("b_spatial", N_GROUPS * BLOCK), ("ln1_gain", D_MODEL), ("ln1_bias", D_MODEL),
            ("b_ff1", D_FF), ("b_ff2", D_MODEL), ("ln2_gain", D_MODEL), ("ln2_bias", D_MODEL))


def _pack(vals):
    pieces = []
    for name, size in _small_sizes():
        flat = vals[name].reshape(-1).astype(F32)
        padded = -(-size // (SUBLANES * LANES)) * SUBLANES * LANES
        pieces.append(jnp.pad(flat, (0, padded - size)).reshape(-1, LANES))
    return jnp.concatenate(pieces, axis=0)


def _unpack(buf):
    out, row = {}, 0
    for name, size in _small_sizes():
        rows = -(-size // (SUBLANES * LANES)) * SUBLANES
        out[name] = buf[row:row + rows].reshape(-1)[:size]
        row += rows
    return out


def kernel(x, w_in, rel_bias, ln_v_gain, ln_v_bias, w_spatial, b_spatial, w_proj_a, w_proj_b, w_out, ln1_gain, ln1_bias, w_ff1, b_ff1, w_ff2, b_ff2, ln2_gain, ln2_bias, loss_target, m_w_in, m_rel_bias, m_ln_v_gain, m_ln_v_bias, m_w_spatial, m_b_spatial, m_w_proj_a, m_w_proj_b, m_w_out, m_ln1_gain, m_ln1_bias, m_w_ff1, m_b_ff1, m_w_ff2, m_b_ff2, m_ln2_gain, m_ln2_bias, v_w_in, v_rel_bias, v_ln_v_gain, v_ln_v_bias, v_w_spatial, v_b_spatial, v_w_proj_a, v_w_proj_b, v_w_out, v_ln1_gain, v_ln1_bias, v_w_ff1, v_b_ff1, v_w_ff2, v_b_ff2, v_ln2_gain, v_ln2_bias):
    d_a, d_b, d_in = _d_a(), _d_b(), _d_in()
    weights = dict(w_in=w_in, rel_bias=rel_bias, ln_v_gain=ln_v_gain, ln_v_bias=ln_v_bias, w_spatial=w_spatial, b_spatial=b_spatial,
                   w_proj_a=w_proj_a, w_proj_b=w_proj_b, w_out=w_out, ln1_gain=ln1_gain, ln1_bias=ln1_bias, w_ff1=w_ff1, b_ff1=b_ff1,
                   w_ff2=w_ff2, b_ff2=b_ff2, ln2_gain=ln2_gain, ln2_bias=ln2_bias)
    mom1 = dict(w_in=m_w_in, rel_bias=m_rel_bias, ln_v_gain=m_ln_v_gain, ln_v_bias=m_ln_v_bias, w_spatial=m_w_spatial,
                b_spatial=m_b_spatial, w_proj_a=m_w_proj_a, w_proj_b=m_w_proj_b, w_out=m_w_out, ln1_gain=m_ln1_gain,
                ln1_bias=m_ln1_bias, w_ff1=m_w_ff1, b_ff1=m_b_ff1, w_ff2=m_w_ff2, b_ff2=m_b_ff2, ln2_gain=m_ln2_gain, ln2_bias=m_ln2_bias)
    mom2 = dict(w_in=v_w_in, rel_bias=v_rel_bias, ln_v_gain=v_ln_v_gain, ln_v_bias=v_ln_v_bias, w_spatial=v_w_spatial,
                b_spatial=v_b_spatial, w_proj_a=v_w_proj_a, w_proj_b=v_w_proj_b, w_out=v_w_out, ln1_gain=v_ln1_gain,
                ln1_bias=v_ln1_bias, w_ff1=v_w_ff1, b_ff1=v_b_ff1, w_ff2=v_w_ff2, b_ff2=v_b_ff2, ln2_gain=v_ln2_gain, ln2_bias=v_ln2_bias)

    mx, my, mc = _coords()
    me = (4 * mx + 2 * my + mc).astype(I32).reshape(1)
    chip = (2 * mx + my).astype(I32).reshape(1)
    full = {n: _cast_into_place(f"cast_{n}", weights[n][0], KINDS[n], me) for n in KINDS}
    sent = {n: (0, 0, 0) for n in KINDS}

    def keep(table, n):
        def store(outs):
            table[n] = outs[0]
        return store

    def gathering(**new):
        stages = []
        for n in KINDS:
            out, relayed, passed = sent[n]
            units = new.get(n, 0)
            if units or relayed < out or passed < relayed:
                st = _gather_stage(full[n], KINDS[n], (out, units) if units else None,
                                   (relayed, out - relayed) if relayed < out else None, (passed, relayed - passed) if passed < relayed else None)
                st.store = keep(full, n)
                sent[n] = (out + units, out, relayed)
                stages.append(st)
        return stages

    def settle(stages, outs):
        for st, o in zip(stages, outs):
            st.store(o)

    def alone(name, stages):
        settle(stages, _comm_only(name, stages))

    def here(n):
        assert sent[n] == (16, 16, 16), (n, sent[n])
        return full[n]

    alone("gather_w_in_near", gathering(w_in=16))
    alone("gather_w_in_relay", gathering())
    alone("gather_w_in_sibling", gathering())

    xs = _to_perm(x[0])
    target = _to_perm(loss_target[0])
    xb = _cast_bf16("cast_x", xs)
    g8 = BLOCK // N_SUB
    ws_t = w_spatial[0].reshape(N_GROUPS, g8, N_SUB, g8, N_SUB).transpose(0, 2, 1, 4, 3).reshape(N_GROUPS, BLOCK, BLOCK)
    bs_t = b_spatial[0].reshape(N_GROUPS, g8, N_SUB).transpose(2, 1, 0).reshape(BLOCK, N_GROUPS)
    idx = _local_index(0)
    causal = jnp.asarray((idx[:, None] >= idx[None, :]).astype(np.float32))
    buckets = jnp.asarray(_bucket_tables())
    bias = _bias_expand(rel_bias, buckets)

    hosted = gathering(w_proj_a=16, w_proj_b=16, w_ff1=1)
    (qkv,), st = _matmul("proj_qkv", xb, here("w_in"), "nn", [F32], n=3 * d_a, stages=hosted)
    settle(hosted, st)
    hosted = gathering(w_out=16, w_ff1=4)
    (rest,), st = _matmul("proj_rest", xb, here("w_in"), "nn", [F32], b_off=3 * d_a, n=d_in - 3 * d_a, stages=hosted)
    settle(hosted, st)
    fwd = []
    for p in range(3):
        hosted = gathering(w_ff1=(3, 5, 3)[p])
        res, st = _attn_fwd(qkv, bias, p, stages=hosted)
        settle(hosted, st)
        fwd.append(res)
    hosted = gathering(w_ff2=1)
    (attn, attn_b, lse), st = _attn_combine([o for o, _ in fwd], [l for _, l in fwd], stages=hosted)
    settle(hosted, st)
    hosted = gathering(w_ff2=2)
    gmlp, st = _gmlp_fwd(rest, ln_v_gain, ln_v_bias, ws_t, bs_t, causal, stages=hosted)
    settle(hosted, st)
    hosted = gathering(w_ff2=3)
    (ya,), st = _matmul("proj_a", attn_b, here("w_proj_a"), "nn", [F32], stages=hosted)
    settle(hosted, st)
    gate_a, gate_b = 2 * d_b, 2 * d_b + D_MODEL

    def merge(acc, ya_, ga, gb):
        return acc, _sigmoid(ga) * ya_ + _sigmoid(gb) * acc

    hosted = gathering(w_ff2=5)
    (yb, merged), st = _matmul("proj_b_merge", gmlp, here("w_proj_b"), "nn", [F32, BF16], merge,
                               [(ya, "mn", 0), (rest, "mn", gate_a), (rest, "mn", gate_b)], tn=256, stages=hosted)
    settle(hosted, st)
    hosted = gathering(w_ff2=3)
    (pre1,), st = _matmul("out_proj", merged, here("w_out"), "nn", [F32], lambda acc, x_: (ALPHA * x_ + acc,), [(xs, "mn", 0)], stages=hosted)
    settle(hosted, st)
    hosted = gathering(w_ff2=2)
    (xhat1, rstd1, h1b), st = _ln1_fwd(pre1, ln1_gain, ln1_bias, stages=hosted)
    settle(hosted, st)

    def relu2(acc, b_):
        r = jnp.maximum(acc + b_, 0.0)
        return r, r * r

    hosted = gathering()
    (relu, fb), st = _matmul("ff1", h1b, here("w_ff1"), "nn", [F32, BF16], relu2, [(b_ff1, "row", 0)], stages=hosted)
    settle(hosted, st)
    alone("gather_w_ff2_sibling", gathering())
    (ff,), _ = _matmul("ff2", fb, here("w_ff2"), "nn", [F32], lambda acc, b_: (acc + b_,), [(b_ff2, "row", 0)], tn=1024, tk=1024)

    core = lax.axis_index("c").astype(I32).reshape(1)
    factors, theirs, sib, pair, chips, reduced = {}, {}, {}, {}, {}, {}

    def grad_for_sibling(n, a, b, stages=()):
        factors[n] = (a, b)
        theirs[n], outs = _dw(f"dw_{n}_sibling", a, b, KINDS[n], core, False, stages=stages)
        settle(stages, outs)

    def to_sibling(n):
        st = _to_sibling_stage(theirs[n])
        st.store = keep(sib, n)
        return st

    def grad_own(n, stages=()):
        pair[n], outs = _dw(f"dw_{n}_own", *factors[n], KINDS[n], core, True, add=sib[n], stages=stages)
        settle(stages, outs)
        chips[n] = lax.empty(pair[n].shape, BF16)
        reduced[n] = 0

    def reducing(**new):
        stages = []
        for n, units in new.items():
            st = _to_chips_stage(pair[n], chips[n], (reduced[n], units))
            st.store = keep(chips, n)
            reduced[n] += units
            stages.append(st)
        return stages

    def summed(n):
        assert reduced[n] == 16, (n, reduced[n])
        return chips[n]

    dpre2, dpre2b, g_ln2_gain, g_ln2_bias, g_b_ff2, loss_part = _ln2_loss_bwd(ff, xhat1, ln1_gain, ln1_bias, ln2_gain, ln2_bias, target)
    grad_for_sibling("w_ff2", fb, dpre2b)

    def relu2_bwd(acc, r):
        da = acc * (2.0 * r)
        return da, da

    hosted = [to_sibling("w_ff2")]
    (dab, g_b_ff1), st = _matmul("d_ff1", dpre2b, here("w_ff2"), "nt", [BF16], relu2_bwd, [(relu, "mn", 0)], colsums=(1,), stages=hosted)
    settle(hosted, st)
    grad_own("w_ff2")
    grad_for_sibling("w_ff1", h1b, dab, reducing(w_ff2=3))
    hosted = reducing(w_ff2=8) + [to_sibling("w_ff1")]
    (dh1,), st = _matmul("d_h1", dab, here("w_ff1"), "nt", [F32], lambda acc, d_: (acc + ALPHA * d_,), [(dpre2, "mn", 0)], stages=hosted)
    settle(hosted, st)
    grad_own("w_ff1", reducing(w_ff2=4))
    hosted = reducing(w_ff2=1)
    (dpre1, dpre1b, g_ln1_gain, g_ln1_bias), st = _ln1_bwd(dh1, xhat1, rstd1, ln1_gain, stages=hosted)
    settle(hosted, st)
    grad_for_sibling("w_out", merged, dpre1b, reducing(w_ff1=1))

    def merge_bwd(acc, ga, gb, ya_, yb_):
        sa, sb = _sigmoid(ga), _sigmoid(gb)
        return acc * sa, acc * sb, acc * ya_ * (sa * (1.0 - sa)), acc * yb_ * (sb * (1.0 - sb))

    hosted = reducing(w_ff1=6) + [to_sibling("w_out")]
    (dya, dyb, dga, dgb), st = _matmul("d_merge", dpre1b, here("w_out"), "nt", [BF16] * 4, merge_bwd,
                                       [(rest, "mn", gate_a), (rest, "mn", gate_b), (ya, "mn", 0), (yb, "mn", 0)], tn=256, stages=hosted)
    settle(hosted, st)
    grad_own("w_out", reducing(w_ff1=1))
    grad_for_sibling("w_proj_a", attn_b, dya)
    grad_for_sibling("w_proj_b", gmlp, dyb)
    hosted = reducing(w_ff1=1) + [to_sibling("w_proj_a"), to_sibling("w_proj_b")]
    (dattn,), st = _matmul("d_attn", dya, here("w_proj_a"), "nt", [F32], stages=hosted)
    settle(hosted, st)
    grad_own("w_proj_a")
    grad_own("w_proj_b")
    hosted = reducing(w_ff1=1)
    (dgmlp,), st = _matmul("d_gmlp", dyb, here("w_proj_b"), "nt", [F32], stages=hosted)
    settle(hosted, st)
    hosted = reducing(w_ff1=2)
    (du, dvb, dws_t, dbs_t, g_lnv_gain, g_lnv_bias), st = _gmlp_bwd(rest, dgmlp, ln_v_gain, ln_v_bias, ws_t, bs_t, causal, stages=hosted)
    settle(hosted, st)
    delta = _attn_delta(dattn, attn)
    bwd = []
    for p in range(3):
        hosted = reducing(**({"w_ff1": 4}, {"w_out": 16}, {"w_proj_a": 16, "w_proj_b": 16})[p])
        res, st = _attn_bwd(qkv, dattn, lse, delta, bias, p, stages=hosted)
        settle(hosted, st)
        bwd.append(res)
    g_rel_bias = _rel_bias_grad([b[3] for b in bwd], buckets)
    dproj = _assemble_dproj([b[i] for i in range(3) for b in bwd], du, dvb, dga, dgb)

    g_w_spatial = dws_t.reshape(N_GROUPS, N_SUB, g8, N_SUB, g8).transpose(0, 2, 1, 4, 3)
    g_b_spatial = dbs_t[:, :N_GROUPS].reshape(N_SUB, g8, N_GROUPS).transpose(2, 1, 0)
    part = _pack(dict(loss=loss_part, rel_bias=g_rel_bias, ln_v_gain=g_lnv_gain, ln_v_bias=g_lnv_bias, w_spatial=g_w_spatial,
                      b_spatial=g_b_spatial, ln1_gain=g_ln1_gain, ln1_bias=g_ln1_bias, b_ff1=g_b_ff1, b_ff2=g_b_ff2,
                      ln2_gain=g_ln2_gain, ln2_bias=g_ln2_bias))
    small = _small_stage(part)
    small.store = keep(sib, "small")
    grad_for_sibling("w_in", xb, dproj, [small])
    parts = sib["small"]
    first = SEQ // 4

    def add_residual(acc, d_):
        return (acc + ALPHA * d_,)

    hosted = [to_sibling("w_in")]
    (dx0,), st = _matmul("d_x0", dproj, here("w_in"), "nt", [F32], add_residual, [(dpre1, "mn", 0)], tm=512, tn=1024, tk=3072, m=first,
                         stages=hosted)
    settle(hosted, st)
    grad_own("w_in")
    hosted = reducing(w_in=7)
    (dx1,), st = _matmul("d_x1", dproj, here("w_in"), "nt", [F32], add_residual, [(dpre1, "mn", 0)], tm=512, tn=1024, tk=3072,
                         m_off=first, m=SEQ - first, stages=hosted)
    settle(hosted, st)
    grad_x = _from_perm(jnp.concatenate([dx0, dx1], axis=0))[None]

    out_g, out_d, out_m, out_v = {}, {}, {}, {}
    for n, units in (("w_ff2", 3), ("w_ff1", 3), ("w_out", 2), ("w_proj_a", 1), ("w_proj_b", 0), ("w_in", 0)):
        hosted = reducing(w_in=units) if units else []
        (g, d, nm, nv), st = _adam_shard(f"adam_{n}", pair[n], summed(n), chip, weights[n][0], mom1[n][0], mom2[n][0], stages=hosted)
        settle(hosted, st)
        out_g[n], out_d[n], out_m[n], out_v[n] = g[None], d[None], nm[None], nv[None]

    zero = jnp.zeros((1,), F32)
    sg, sd, sm, sv = (_unpack(b) for b in _adam_small(
        part, parts, me, _pack({**weights, "loss": zero}), _pack({**mom1, "loss": zero}), _pack({**mom2, "loss": zero})))
    for n in WEIGHT_ORDER:
        if n not in KINDS:
            shape = weights[n].shape
            out_g[n], out_d[n], out_m[n], out_v[n] = (t[n].reshape(shape) for t in (sg, sd, sm, sv))
    loss = sg["loss"].reshape(())
    return (loss, grad_x, *[out_g[n] for n in WEIGHT_ORDER], *[out_d[n] for n in WEIGHT_ORDER],
            *[out_m[n] for n in WEIGHT_ORDER], *[out_v[n] for n in WEIGHT_ORDER])
```

```python
import math

import jax
import jax.numpy as jnp
import numpy as np
from jax import lax
from jax.experimental import pallas as pl
from jax.experimental.pallas import tpu as pltpu

F32 = jnp.float32
BF16 = jnp.bfloat16
I32 = jnp.int32

SEQ = 2048
D_MODEL = 2048
HEAD_DIM = 128
N_HEADS = 8
N_GROUPS = 8
D_FF = 4 * D_MODEL
BLOCK = 128
DILATIONS = (1, 4, 16)
N_BUCKETS = 32
MAX_DISTANCE = 2048
ALPHA = 2.0 ** 0.25
LN_EPS = 1e-5
NEG_INF = -1e30
N_DEV = 8
N_CHIPS = 4
N_SUB = 16
ADAM_LR, ADAM_B1, ADAM_B2, ADAM_EPS, ADAM_WD, ADAM_STEP = 0.001, 0.9, 0.999, 1e-08, 0.01, 10
LANES = 128
SUBLANES = 8
VMEM_LIMIT = 56 * 1024 * 1024
MESH = pl.DeviceIdType.MESH
ANY = pl.BlockSpec(memory_space=pl.ANY)
WEIGHT_ORDER = ("w_in", "rel_bias", "ln_v_gain", "ln_v_bias", "w_spatial", "b_spatial", "w_proj_a", "w_proj_b", "w_out",
                "ln1_gain", "ln1_bias", "w_ff1", "b_ff1", "w_ff2", "b_ff2", "ln2_gain", "ln2_bias")
KINDS = {"w_in": "col", "w_proj_a": "col", "w_proj_b": "col", "w_out": "row", "w_ff1": "col", "w_ff2": "row"}


def _d_a():
    return N_HEADS * HEAD_DIM


def _d_b():
    return N_GROUPS * BLOCK


def _d_in():
    return 3 * _d_a() + 2 * _d_b() + 2 * D_MODEL


def _pick(t, n, *others):
    if n <= t and all(o % n == 0 for o in others):
        return n
    for c in range(min(t, n) // LANES * LANES, 0, -LANES):
        if n % c == 0 and all(o % c == 0 for o in others):
            return c
    raise ValueError((t, n, others))


class _Stage:
    def __init__(self, ins, outs, alias, sems, start, finish):
        self.ins, self.outs, self.alias, self.sems, self.start, self.finish = ins, outs, alias, sems, start, finish


def _call(name, body, grid, in_specs, out_specs, out_shape, operands, scratch=(), sem=None, stages=(), sequential=False, prefetch=None,
          shown=False, alias=None):
    n_in, n_out, n_sc = len(in_specs), len(out_specs), len(scratch)
    st_in = [len(s.ins) for s in stages]
    st_out = [len(s.outs) for s in stages]
    st_sem = [len(s.sems) for s in stages]
    n_pre = 0 if prefetch is None else 1
    aliases, ioff, ooff = {i + n_pre: o for i, o in (alias or {}).items()}, n_in + n_pre, n_out
    for s, ni, no in zip(stages, st_in, st_out):
        for i, o in s.alias.items():
            aliases[ioff + i] = ooff + o
        ioff, ooff = ioff + ni, ooff + no

    def split(refs, counts):
        out, at = [], 0
        for c in counts:
            out.append(refs[at:at + c])
            at += c
        return out

    def wrapped(*refs):
        ins, sins, outs, souts, sc, ssems = split(refs[n_pre:], [n_in, sum(st_in), n_out, sum(st_out), n_sc, sum(st_sem)])
        parts = list(zip(stages, split(sins, st_in), split(souts, st_out), split(ssems, st_sem)))
        if sequential:
            for s, a, b, c in parts:
                s.start(a, b, c)
                s.finish(a, b, c)
            return
        if parts:
            first = _all_of([pl.program_id(i) == 0 for i in range(len(grid))])
            last = _all_of([pl.program_id(i) == g - 1 for i, g in enumerate(grid)])

            @pl.when(first)
            def _():
                for s, a, b, c in parts:
                    s.start(a, b, c)

        body(*(refs[:n_pre] if shown else ()), *ins, *outs, *sc)
        if parts:
            @pl.when(last)
            def _():
                for s, a, b, c in parts:
                    s.finish(a, b, c)

    if stages or sem is None:
        sem = ("arbitrary",) * len(grid)
    specs = dict(grid=grid, in_specs=list(in_specs) + [ANY] * sum(st_in), out_specs=list(out_specs) + [ANY] * sum(st_out),
                 scratch_shapes=list(scratch) + [x for s in stages for x in s.sems])
    if prefetch is not None:
        specs = dict(grid_spec=pltpu.PrefetchScalarGridSpec(num_scalar_prefetch=1, **specs))
    res = pl.pallas_call(
        wrapped, name=name, out_shape=list(out_shape) + [o for s in stages for o in s.outs], input_output_aliases=aliases,
        compiler_params=pltpu.CompilerParams(dimension_semantics=sem, vmem_limit_bytes=VMEM_LIMIT), **specs,
    )(*([prefetch] if n_pre else []), *operands, *[a for s in stages for a in s.ins])
    res = list(res)
    return res[:n_out], split(res[n_out:], st_out)


def _all_of(conds):
    out = conds[0]
    for c in conds[1:]:
        out = out & c
    return out


def _coords():
    return lax.axis_index("x"), lax.axis_index("y"), lax.axis_index("c")


def _other_chips(x, y):
    return ((1 - x, y), (x, 1 - y), (1 - x, 1 - y))


def _lin(dev):
    return 4 * dev[0] + 2 * dev[1] + dev[2]


def _piece(total, lo, n, units=16):
    assert total % units == 0
    return lo * (total // units), n * (total // units)


FLOWS = 4


def _split(lo, cnt):
    k = next(k for k in (FLOWS, 2, 1) if cnt % (2 * SUBLANES * k) == 0)
    return [(lo + i * (cnt // k), cnt // k) for i in range(k)]


def _remote(src, dst, send, recv, to):
    return pltpu.make_async_remote_copy(src_ref=src, dst_ref=dst, send_sem=send, recv_sem=recv, device_id=to, device_id_type=MESH)


def _placer(kind, n, lo, cnt):
    def place(ref, dev):
        if kind == "col":
            return ref.at[pl.ds(lo, cnt), pl.ds(pl.multiple_of(_lin(dev) * n, LANES), n)]
        return ref.at[pl.ds(pl.multiple_of(_lin(dev) * n + lo, 2 * SUBLANES), cnt), :]
    return place


def _spread_stage(full, kind, piece=(0, 16), home=False):
    n = (full.shape[1] if kind == "col" else full.shape[0]) // N_DEV
    lo, cnt = _piece(full.shape[0] if kind == "col" else n, *piece)
    parts = _split(lo, cnt)
    npeers = 1 if home else 2

    def copies(ins, outs, sems):
        send, recv = sems
        x, y, c = _coords()
        me = (x, y, c)
        peers = [(x, y, 1 - c)] if home else [(1 - x, y, c), (x, 1 - y, c)]
        out, arrive = [], []
        for k, t in enumerate(peers):
            for i, (plo, pcnt) in enumerate(parts):
                place = _placer(kind, n, plo, pcnt)
                out.append(_remote(place(outs[0], me), place(outs[0], me), send.at[i, k], recv.at[i, k], t))
                arrive.append(_remote(place(outs[0], t), place(outs[0], t), send.at[i, k], recv.at[i, k], t))
        return out, arrive

    def start(ins, outs, sems):
        for cp in copies(ins, outs, sems)[0]:
            cp.start()

    def finish(ins, outs, sems):
        out, arrive = copies(ins, outs, sems)
        for cp in arrive:
            cp.wait_recv()
        for cp in out:
            cp.wait_send()

    return _Stage([full], [jax.ShapeDtypeStruct(full.shape, full.dtype)], {0: 0},
                  [pltpu.SemaphoreType.DMA((len(parts), npeers)), pltpu.SemaphoreType.DMA((len(parts), npeers))], start, finish)


def _relay_stage(full, kind, piece=(0, 16)):
    n = (full.shape[1] if kind == "col" else full.shape[0]) // N_DEV
    lo, cnt = _piece(full.shape[0] if kind == "col" else n, *piece)
    half = cnt // 2
    assert half % (2 * SUBLANES) == 0, (cnt, kind)
    tops, bottoms = _split(lo, half), _split(lo + half, half)

    def copies(ins, outs, sems):
        send, recv = sems
        x, y, c = _coords()
        xn, yn, dg = (1 - x, y, c), (x, 1 - y, c), (1 - x, 1 - y, c)
        out, arrive, k = [], [], 0
        for came_from, to, parts in ((yn, xn, tops), (xn, yn, bottoms)):
            for plo, pcnt in parts:
                place = _placer(kind, n, plo, pcnt)
                out.append(_remote(place(outs[0], came_from), place(outs[0], came_from), send.at[k], recv.at[k], to))
                arrive.append(_remote(place(outs[0], dg), place(outs[0], dg), send.at[k], recv.at[k], to))
                k += 1
        return out, arrive

    def start(ins, outs, sems):
        for cp in copies(ins, outs, sems)[0]:
            cp.start()

    def finish(ins, outs, sems):
        out, arrive = copies(ins, outs, sems)
        for cp in arrive:
            cp.wait_recv()
        for cp in out:
            cp.wait_send()

    return _Stage([full], [jax.ShapeDtypeStruct(full.shape, full.dtype)], {0: 0},
                  [pltpu.SemaphoreType.DMA((len(tops) + len(bottoms),)), pltpu.SemaphoreType.DMA((len(tops) + len(bottoms),))], start, finish)


def _forward_stage(full, kind, piece=(0, 16)):
    n = (full.shape[1] if kind == "col" else full.shape[0]) // N_DEV
    lo, cnt = _piece(full.shape[0] if kind == "col" else n, *piece)
    place = _placer(kind, n, lo, cnt)

    def copies(ins, outs, sems):
        send, recv = sems
        x, y, c = _coords()
        chips = _other_chips(x, y)
        out = [_remote(place(outs[0], (*chip, c)), place(outs[0], (*chip, c)), send.at[k], recv.at[k], (x, y, 1 - c)) for k, chip in enumerate(chips)]
        arrive = [_remote(place(outs[0], (*chip, 1 - c)), place(outs[0], (*chip, 1 - c)), send.at[k], recv.at[k], (x, y, 1 - c))
                  for k, chip in enumerate(chips)]
        return out, arrive

    def start(ins, outs, sems):
        for cp in copies(ins, outs, sems)[0]:
            cp.start()

    def finish(ins, outs, sems):
        out, arrive = copies(ins, outs, sems)
        for cp in arrive:
            cp.wait_recv()
        for cp in out:
            cp.wait_send()

    return _Stage([full], [jax.ShapeDtypeStruct(full.shape, full.dtype)], {0: 0},
                  [pltpu.SemaphoreType.DMA((3,)), pltpu.SemaphoreType.DMA((3,))], start, finish)


def _to_sibling_stage(theirs):
    def copies(ins, outs, sems):
        send, recv = sems
        x, y, c = _coords()
        return [_remote(ins[0].at[q], outs[0].at[q], send.at[q], recv.at[q], (x, y, 1 - c)) for q in range(N_CHIPS)]

    def start(ins, outs, sems):
        for cp in copies(ins, outs, sems):
            cp.start()

    def finish(ins, outs, sems):
        for cp in copies(ins, outs, sems):
            cp.wait()

    return _Stage([theirs], [jax.ShapeDtypeStruct(theirs.shape, BF16)], {},
                  [pltpu.SemaphoreType.DMA((N_CHIPS,)), pltpu.SemaphoreType.DMA((N_CHIPS,))], start, finish)


def _to_chips_stage(pair, dst, piece=(0, 16)):
    lo, cnt = _piece(pair.shape[1], *piece)
    parts = _split(lo, cnt)
    nsem = 3 * len(parts)

    def copies(ins, outs, sems):
        send, recv = sems
        x, y, c = _coords()
        mine = 2 * x + y
        out, arrive, k = [], [], 0
        for px, py in _other_chips(x, y):
            for plo, pcnt in parts:
                rows = pl.ds(plo, pcnt)
                out.append(_remote(ins[0].at[2 * px + py, rows, :], outs[0].at[mine, rows, :], send.at[k], recv.at[k], (px, py, c)))
                arrive.append(_remote(ins[0].at[2 * px + py, rows, :], outs[0].at[2 * px + py, rows, :], send.at[k], recv.at[k], (px, py, c)))
                k += 1
        return out, arrive

    def start(ins, outs, sems):
        for cp in copies(ins, outs, sems)[0]:
            cp.start()

    def finish(ins, outs, sems):
        out, arrive = copies(ins, outs, sems)
        for cp in arrive:
            cp.wait_recv()
        for cp in out:
            cp.wait_send()

    return _Stage([pair, dst], [jax.ShapeDtypeStruct(dst.shape, dst.dtype)], {1: 0},
                  [pltpu.SemaphoreType.DMA((nsem,)), pltpu.SemaphoreType.DMA((nsem,))], start, finish)


def _fuse(parts, ins, outs, alias):
    parts = [p for p in parts if p is not None]
    sems = [x for st, _, _ in parts for x in st.sems]

    def run(which):
        def go(i, o, s):
            refs, at = list(i) + list(o), 0
            for st, pi, po in parts:
                getattr(st, which)([refs[k] for k in pi], [refs[k] for k in po], s[at:at + len(st.sems)])
                at += len(st.sems)
        return go

    return _Stage(ins, [jax.ShapeDtypeStruct(o.shape, o.dtype) for o in outs], alias, sems, run("start"), run("finish"))


def _gather_stage(full, kind, new=None, relay=None, forward=None):
    return _fuse([(_spread_stage(full, kind, new), [0], [1]) if new else None,
                  (_relay_stage(full, kind, relay), [0], [1]) if relay else None,
                  (_spread_stage(full, kind, relay, home=True), [0], [1]) if relay else None,
                  (_forward_stage(full, kind, forward), [0], [1]) if forward else None], [full], [full], {0: 0})


def _small_stage(part):
    def copies(ins, outs, sems):
        send, recv = sems
        x, y, c = _coords()
        me = (x, y, c)
        peers = [(1 - x if k & 4 else x, 1 - y if k & 2 else y, 1 - c if k & 1 else c) for k in range(1, N_DEV)]
        out = [_remote(ins[0], outs[0].at[_lin(me)], send.at[k], recv.at[k], t) for k, t in enumerate(peers)]
        arrive = [_remote(ins[0], outs[0].at[_lin(t)], send.at[k], recv.at[k], t) for k, t in enumerate(peers)]
        return out, arrive

    def start(ins, outs, sems):
        for cp in copies(ins, outs, sems)[0]:
            cp.start()

    def finish(ins, outs, sems):
        out, arrive = copies(ins, outs, sems)
        for cp in arrive:
            cp.wait_recv()
        for cp in out:
            cp.wait_send()

    return _Stage([part], [jax.ShapeDtypeStruct((N_DEV, *part.shape), F32)], {},
                  [pltpu.SemaphoreType.DMA((N_DEV - 1,)), pltpu.SemaphoreType.DMA((N_DEV - 1,))], start, finish)


def _comm_only(name, stages):
    return _call(name, lambda: None, (1,), [], [], [], [], stages=stages, sequential=True)[1]


_GELU_C = math.sqrt(2.0 / math.pi)


def _gelu(x):
    return 0.5 * x * (1.0 + jnp.tanh(_GELU_C * (x + 0.044715 * x * x * x)))


def _gelu_grad(x):
    t = jnp.tanh(_GELU_C * (x + 0.044715 * x * x * x))
    return 0.5 * (1.0 + t) + 0.5 * x * (1.0 - t * t) * (_GELU_C * (1.0 + 3.0 * 0.044715 * x * x))


def _sigmoid(x):
    return 1.0 / (1.0 + jnp.exp(-x))


def _dot(a, b, mode):
    dims = {"nn": (((1,), (0,)), ((), ())), "nt": (((1,), (1,)), ((), ())), "tn": (((0,), (0,)), ((), ()))}[mode]
    return lax.dot_general(a.astype(BF16), b.astype(BF16), dims, preferred_element_type=F32)


def _matmul(name, a, b, mode, outs, epi=None, extras=(), colsums=(), tm=2048, tn=512, tk=2048, b_off=0, n=None, m_off=0, m=None, stages=()):
    if mode == "tn":
        kk, mfull = a.shape
    else:
        mfull, kk = a.shape
    m = mfull if m is None else m
    n = (b.shape[0] if mode == "nt" else b.shape[1]) if n is None else n
    tm, tk = _pick(tm, m, m_off), _pick(tk, kk)
    tn = _pick(tn, n, b_off, *[off for _, _, off in extras])
    boff, moff = b_off // tn, m_off // tm
    nm, nn_, nk = m // tm, n // tn, kk // tk
    col_major = bool(colsums)
    grid = (nn_, nm, nk) if col_major else (nm, nn_, nk)

    def imap(f):
        if col_major:
            return lambda g0, g1, k: f(g1, g0, k)
        return f

    a_spec = (pl.BlockSpec((tk, tm), imap(lambda i, j, k: (k, i + moff))) if mode == "tn"
              else pl.BlockSpec((tm, tk), imap(lambda i, j, k: (i + moff, k))))
    b_spec = (pl.BlockSpec((tn, tk), imap(lambda i, j, k: (j + boff, k))) if mode == "nt"
              else pl.BlockSpec((tk, tn), imap(lambda i, j, k: (k, j + boff))))
    in_specs, operands = [a_spec, b_spec], [a, b]
    for arr, kind, off in extras:
        o = off // tn
        if kind == "mn":
            in_specs.append(pl.BlockSpec((tm, tn), imap(lambda i, j, k, o=o: (i + moff, j + o))))
        else:
            in_specs.append(pl.BlockSpec((1, tn), imap(lambda i, j, k, o=o: (0, j + o))))
        operands.append(arr)
    out_shape = [jax.ShapeDtypeStruct((m, n), dt) for dt in outs] + [jax.ShapeDtypeStruct((1, n), F32) for _ in colsums]
    out_specs = ([pl.BlockSpec((tm, tn), imap(lambda i, j, k: (i, j))) for _ in outs]
                 + [pl.BlockSpec((1, tn), imap(lambda i, j, k: (0, j))) for _ in colsums])
    n_ex, n_out, n_cs = len(extras), len(outs), len(colsums)

    def body(*refs):
        a_ref, b_ref = refs[:2]
        ex_refs = refs[2:2 + n_ex]
        out_refs = refs[2 + n_ex:2 + n_ex + n_out]
        cs_refs = refs[2 + n_ex + n_out:2 + n_ex + n_out + n_cs]
        part = _dot(a_ref[...], b_ref[...], mode)

        def finish(acc):
            res = epi(acc, *[r[...] for r in ex_refs]) if epi is not None else (acc,)
            for r, v in zip(out_refs, res[:n_out]):
                r[...] = v.astype(r.dtype)
            if n_cs:
                @pl.when(pl.program_id(1) == 0)
                def _():
                    for r in cs_refs:
                        r[...] = jnp.zeros_like(r)

                for r, idx in zip(cs_refs, colsums):
                    r[...] += jnp.sum(res[idx], axis=0, keepdims=True)

        if nk == 1:
            finish(part)
        else:
            acc_ref = refs[-1]
            k = pl.program_id(2)

            @pl.when(k == 0)
            def _():
                acc_ref[...] = part

            @pl.when(k > 0)
            def _():
                acc_ref[...] += part

            @pl.when(k == nk - 1)
            def _():
                finish(acc_ref[...])

    sem = ("arbitrary", "arbitrary", "arbitrary") if col_major else ("parallel", "parallel", "arbitrary")
    return _call(name, body, grid, in_specs, out_specs, out_shape, operands,
                 scratch=[pltpu.VMEM((tm, tn), F32)] if nk > 1 else [], sem=sem, stages=stages)


def _project_shards(name, a, b, which, into=None, stages=()):
    m, kk = a.shape
    n = b.shape[1]
    tn = n // N_DEV
    o_spec = pl.BlockSpec((m, tn), lambda s, w_ref: (0, w_ref[s]))

    def body(a_ref, b_ref, *rest):
        rest[-1][...] = _dot(a_ref[...], b_ref[...], "nn")

    (out,), st = _call(name, body, (which.shape[0],),
                       [pl.BlockSpec((m, kk), lambda s, w_ref: (0, 0)), pl.BlockSpec((kk, tn), lambda s, w_ref: (0, w_ref[s]))]
                       + ([ANY] if into is not None else []), [o_spec], [jax.ShapeDtypeStruct((m, n), F32)],
                       [a, b] + ([into] if into is not None else []), sem=("arbitrary",), stages=stages, prefetch=which,
                       alias={2: 0} if into is not None else None)
    return out, st


def _row_spec(tr, c):
    return pl.BlockSpec((tr, c), lambda i: (i, 0))


def _fix_spec(shape):
    return pl.BlockSpec(shape, lambda *_: tuple(0 for _ in shape))


def _cast_bf16(name, x, tr=512):
    r, c = x.shape
    tr = _pick(tr, r)

    def body(x_ref, o_ref):
        o_ref[...] = x_ref[...].astype(BF16)

    return _call(name, body, (r // tr,), [_row_spec(tr, c)], [_row_spec(tr, c)], [jax.ShapeDtypeStruct((r, c), BF16)], [x],
                 sem=("parallel",))[0][0]


def _cast_into_place(name, w, kind, me, tr=512):
    r, c = w.shape
    tr = _pick(tr, r)
    nb = r // tr
    if kind == "col":
        o_spec = pl.BlockSpec((tr, c), lambda i, me_ref: (i, me_ref[0]))
        shape = (r, c * N_DEV)
    else:
        o_spec = pl.BlockSpec((tr, c), lambda i, me_ref: (me_ref[0] * nb + i, 0))
        shape = (r * N_DEV, c)

    def body(x_ref, o_ref):
        o_ref[...] = x_ref[...].astype(BF16)

    return _call(name, body, (nb,), [pl.BlockSpec((tr, c), lambda i, me_ref: (i, 0))], [o_spec], [jax.ShapeDtypeStruct(shape, BF16)], [w],
                 sem=("parallel",), prefetch=me)[0][0]


def _layer_norm_stats(x):
    mean = jnp.mean(x, axis=-1, keepdims=True)
    xc = x - mean
    var = jnp.mean(xc * xc, axis=-1, keepdims=True)
    rstd = lax.rsqrt(var + LN_EPS)
    return xc * rstd, rstd


def _layer_norm_bwd(dxhat, xhat, rstd):
    m1 = jnp.mean(dxhat, axis=-1, keepdims=True)
    m2 = jnp.mean(dxhat * xhat, axis=-1, keepdims=True)
    return rstd * (dxhat - m1 - xhat * m2)


def _ln1_fwd(pre1, g1, b1, tr=256, stages=()):
    s, d = pre1.shape
    tr = _pick(tr, s)

    def body(p_ref, g_ref, b_ref, xh_ref, rs_ref, h_ref):
        xhat, rstd = _layer_norm_stats(p_ref[...])
        xh_ref[...] = xhat
        rs_ref[...] = rstd
        h_ref[...] = (xhat * g_ref[...] + b_ref[...]).astype(BF16)

    return _call("ln1_fwd", body, (s // tr,), [_row_spec(tr, d), _fix_spec((1, d)), _fix_spec((1, d))],
                 [_row_spec(tr, d), _row_spec(tr, 1), _row_spec(tr, d)],
                 [jax.ShapeDtypeStruct((s, d), F32), jax.ShapeDtypeStruct((s, 1), F32), jax.ShapeDtypeStruct((s, d), BF16)],
                 [pre1, g1, b1], sem=("parallel",), stages=stages)


def _ln2_loss_bwd(ff, xhat1, g1, b1, g2, b2, target, tr=256):
    s, d = ff.shape
    tr = _pick(tr, s)

    def body(ff_ref, xh1_ref, g1_ref, b1_ref, g2_ref, b2_ref, t_ref, dp_ref, dpb_ref, dg_ref, db_ref, dbf_ref, loss_ref):
        @pl.when(pl.program_id(0) == 0)
        def _():
            dg_ref[...] = jnp.zeros_like(dg_ref)
            db_ref[...] = jnp.zeros_like(db_ref)
            dbf_ref[...] = jnp.zeros_like(dbf_ref)
            loss_ref[...] = jnp.zeros_like(loss_ref)

        h1 = xh1_ref[...] * g1_ref[...] + b1_ref[...]
        xhat, rstd = _layer_norm_stats(ALPHA * h1 + ff_ref[...])
        err = xhat * g2_ref[...] + b2_ref[...] - t_ref[...]
        row = jnp.mean(err * err, axis=-1, keepdims=True)
        loss_ref[...] += 0.5 * jnp.sum(row, axis=0, keepdims=True)
        dy = err / d
        dg_ref[...] += jnp.sum(dy * xhat, axis=0, keepdims=True)
        db_ref[...] += jnp.sum(dy, axis=0, keepdims=True)
        dpre = _layer_norm_bwd(dy * g2_ref[...], xhat, rstd)
        dbf_ref[...] += jnp.sum(dpre, axis=0, keepdims=True)
        dp_ref[...] = dpre
        dpb_ref[...] = dpre.astype(BF16)

    vec = _fix_spec((1, d))
    return _call("ln2_loss_bwd", body, (s // tr,), [_row_spec(tr, d), _row_spec(tr, d), vec, vec, vec, vec, _row_spec(tr, d)],
                 [_row_spec(tr, d), _row_spec(tr, d), vec, vec, vec, _fix_spec((1, 1))],
                 [jax.ShapeDtypeStruct((s, d), F32), jax.ShapeDtypeStruct((s, d), BF16)]
                 + [jax.ShapeDtypeStruct((1, d), F32)] * 3 + [jax.ShapeDtypeStruct((1, 1), F32)],
                 [ff, xhat1, g1, b1, g2, b2, target])[0]


def _ln1_bwd(dh1, xhat1, rstd1, g1, tr=256, stages=()):
    s, d = dh1.shape
    tr = _pick(tr, s)

    def body(dh_ref, xh_ref, rs_ref, g_ref, dp_ref, dpb_ref, dg_ref, db_ref):
        @pl.when(pl.program_id(0) == 0)
        def _():
            dg_ref[...] = jnp.zeros_like(dg_ref)
            db_ref[...] = jnp.zeros_like(db_ref)

        dh, xhat = dh_ref[...], xh_ref[...]
        dg_ref[...] += jnp.sum(dh * xhat, axis=0, keepdims=True)
        db_ref[...] += jnp.sum(dh, axis=0, keepdims=True)
        dpre = _layer_norm_bwd(dh * g_ref[...], xhat, rs_ref[...])
        dp_ref[...] = dpre
        dpb_ref[...] = dpre.astype(BF16)

    vec = _fix_spec((1, d))
    return _call("ln1_bwd", body, (s // tr,), [_row_spec(tr, d), _row_spec(tr, d), _row_spec(tr, 1), vec],
                 [_row_spec(tr, d), _row_spec(tr, d), vec, vec],
                 [jax.ShapeDtypeStruct((s, d), F32), jax.ShapeDtypeStruct((s, d), BF16)] + [jax.ShapeDtypeStruct((1, d), F32)] * 2,
                 [dh1, xhat1, rstd1, g1], stages=stages)


def _to_perm(x):
    return x.reshape(SEQ // N_SUB, N_SUB, -1).transpose(1, 0, 2).reshape(SEQ, -1)


def _from_perm(x):
    return x.reshape(N_SUB, SEQ // N_SUB, -1).transpose(1, 0, 2).reshape(SEQ, -1)


def _local_index(p):
    rho = np.arange(BLOCK)
    if p == 0:
        return 16 * (rho % 8) + rho // 8
    if p == 1:
        return 4 * (rho % 32) + rho // 32
    return rho


def _tile_view(x, p):
    c = x.shape[1]
    if p == 1:
        return x.reshape(4, 4, BLOCK, c)
    return x.reshape(N_SUB, BLOCK, c)


def _view_shape(c, p):
    return (4, 4, BLOCK, c) if p == 1 else (N_SUB, BLOCK, c)


def _tile_spec(p, width, col, shift=0):
    nblk = SEQ // DILATIONS[p] // BLOCK

    def blk(n):
        return jnp.clip(n + shift, 0, nblk - 1)

    if p == 0:
        return pl.BlockSpec((N_SUB, SUBLANES, width), lambda s, n: (0, blk(n), col))
    if p == 1:
        return pl.BlockSpec((4, None, 32, width), lambda s, n: (0, s, blk(n), col))
    return pl.BlockSpec((None, BLOCK, width), lambda s, n: (s, 0, col))


def _tile_grid(p):
    return ((1, 16), (4, 4), (16, 1))[p]


def _t5_bucket(n):
    max_exact = N_BUCKETS // 2
    nf = np.maximum(n, 1).astype(np.float32)
    large = max_exact + (np.log(nf / np.float32(max_exact)) / np.float32(math.log(MAX_DISTANCE / max_exact))
                         * np.float32(N_BUCKETS - max_exact)).astype(np.int32)
    large = np.minimum(large, N_BUCKETS - 1)
    return np.where(n < max_exact, n, large).astype(np.int32)


def _bucket_tables():
    tabs = np.zeros((3, 2, BLOCK, BLOCK), np.int32)
    for p, d in enumerate(DILATIONS):
        i = _local_index(p)
        diff = i[:, None] - i[None, :]
        tabs[p, 0] = np.where(diff <= 0, _t5_bucket((BLOCK + diff) * d), -1)
        tabs[p, 1] = np.where(diff >= 0, _t5_bucket(np.maximum(diff, 0) * d), -1)
    return tabs


def _bias_expand(rel_bias, buckets):
    nh = N_HEADS

    def body(rb_ref, bk_ref, o_ref):
        for w in range(2):
            bk = bk_ref[0, w]
            for h in range(nh):
                val = jnp.zeros((BLOCK, BLOCK), F32)
                for b in range(N_BUCKETS):
                    val = jnp.where(bk == b, rb_ref[b, h], val)
                o_ref[0, h, w] = jnp.where(bk < 0, NEG_INF, val)

    return _call("bias_expand", body, (3,),
                 [pl.BlockSpec(memory_space=pltpu.SMEM), pl.BlockSpec((1, 2, BLOCK, BLOCK), lambda p: (p, 0, 0, 0))],
                 [pl.BlockSpec((1, nh, 2, BLOCK, BLOCK), lambda p: (p, 0, 0, 0, 0))],
                 [jax.ShapeDtypeStruct((3, nh, 2, BLOCK, BLOCK), F32)], [rel_bias, buckets], sem=("parallel",))[0][0]


def _heads_to_lanes(cols):
    lane = lax.broadcasted_iota(I32, (BLOCK, LANES), 1)
    out = jnp.zeros((BLOCK, LANES), F32)
    for h, c in enumerate(cols):
        out = jnp.where(lane == h, c, out)
    return out


def _attn_fwd(qkv, bias, p, stages=()):
    d_a = _d_a()
    has_prev = SEQ // DILATIONS[p] // BLOCK > 1
    scale = HEAD_DIM ** -0.5
    view = _tile_view(qkv, p)

    width = 2 * BLOCK if has_prev else BLOCK

    def body(q_ref, kc_ref, kp_ref, vc_ref, vp_ref, b_ref, o_ref, l_ref, s_ref, p_ref):
        n = pl.program_id(1)
        q_all = q_ref[...].reshape(BLOCK, d_a).astype(BF16)
        k_all = kc_ref[...].reshape(BLOCK, d_a).astype(BF16)
        v_all = vc_ref[...].reshape(BLOCK, d_a).astype(BF16)
        if has_prev:
            k_all = jnp.concatenate([kp_ref[...].reshape(BLOCK, d_a).astype(BF16), k_all], axis=0)
            v_all = jnp.concatenate([vp_ref[...].reshape(BLOCK, d_a).astype(BF16), v_all], axis=0)
            no_prev = (lax.broadcasted_iota(I32, (BLOCK, width), 1) < BLOCK) & (n == 0)
        for h in range(N_HEADS):
            sl = slice(h * HEAD_DIM, (h + 1) * HEAD_DIM)
            s = _dot(q_all[:, sl], k_all[:, sl], "nt") * scale
            if has_prev:
                s = jnp.where(no_prev, NEG_INF, s + jnp.concatenate([b_ref[0, h, 0], b_ref[0, h, 1]], axis=1))
            else:
                s = s + b_ref[0, h, 1]
            s_ref[h] = s
        dens, lses = [], []
        for h in range(N_HEADS):
            s = s_ref[h]
            m = jnp.max(s, axis=-1, keepdims=True)
            pr = jnp.exp(s - m)
            den = jnp.sum(pr, axis=-1, keepdims=True)
            p_ref[h] = pr.astype(BF16)
            dens.append(den)
            lses.append(m + jnp.log(den))
        for h in range(N_HEADS):
            sl = slice(h * HEAD_DIM, (h + 1) * HEAD_DIM)
            o_ref[..., sl] = (_dot(p_ref[h], v_all[:, sl], "nn") / dens[h]).reshape(*o_ref.shape[:-1], HEAD_DIM)
        l_ref[...] = _heads_to_lanes(lses).reshape(l_ref.shape)

    (o, l), st = _call(
        f"attn_fwd{p}", body, _tile_grid(p),
        [_tile_spec(p, d_a, 0), _tile_spec(p, d_a, 1), _tile_spec(p, d_a, 1, -1), _tile_spec(p, d_a, 2), _tile_spec(p, d_a, 2, -1),
         pl.BlockSpec((1, N_HEADS, 2, BLOCK, BLOCK), lambda s, n: (p, 0, 0, 0, 0))],
        [_tile_spec(p, d_a, 0), _tile_spec(p, LANES, 0)],
        [jax.ShapeDtypeStruct(_view_shape(d_a, p), F32), jax.ShapeDtypeStruct(_view_shape(LANES, p), F32)],
        [view, view, view, view, view, bias], scratch=[pltpu.VMEM((N_HEADS, BLOCK, width), F32), pltpu.VMEM((N_HEADS, BLOCK, width), BF16)],
        sem=("parallel", "parallel"), stages=stages)
    return (o.reshape(SEQ, d_a), l.reshape(SEQ, LANES)), st


def _attn_combine(os_, ls_, tr=256, stages=()):
    d_a = _d_a()
    tr = _pick(tr, SEQ)

    def body(o0, o1, o2, l0, l1, l2, a_ref, ab_ref, lt_ref):
        l = [l0[...], l1[...], l2[...]]
        m = jnp.maximum(jnp.maximum(l[0], l[1]), l[2])
        w = [jnp.exp(x - m) for x in l]
        tot = w[0] + w[1] + w[2]
        lt_ref[...] = m + jnp.log(tot)
        w = [x / tot for x in w]
        for h in range(N_HEADS):
            sl = slice(h * HEAD_DIM, (h + 1) * HEAD_DIM)
            acc = w[0][:, h:h + 1] * o0[:, sl] + w[1][:, h:h + 1] * o1[:, sl] + w[2][:, h:h + 1] * o2[:, sl]
            a_ref[:, sl] = acc
            ab_ref[:, sl] = acc.astype(BF16)

    return _call("attn_combine", body, (SEQ // tr,), [_row_spec(tr, d_a)] * 3 + [_row_spec(tr, LANES)] * 3,
                 [_row_spec(tr, d_a), _row_spec(tr, d_a), _row_spec(tr, LANES)],
                 [jax.ShapeDtypeStruct((SEQ, d_a), F32), jax.ShapeDtypeStruct((SEQ, d_a), BF16), jax.ShapeDtypeStruct((SEQ, LANES), F32)],
                 [*os_, *ls_], sem=("parallel",), stages=stages)


def _attn_delta(dattn, attn, tr=256):
    d_a = _d_a()
    tr = _pick(tr, SEQ)

    def body(d_ref, a_ref, o_ref):
        prod = d_ref[...] * a_ref[...]
        lane = lax.broadcasted_iota(I32, (tr, LANES), 1)
        out = jnp.zeros((tr, LANES), F32)
        for h in range(N_HEADS):
            out = jnp.where(lane == h, jnp.sum(prod[:, h * HEAD_DIM:(h + 1) * HEAD_DIM], axis=-1, keepdims=True), out)
        o_ref[...] = out

    return _call("attn_delta", body, (SEQ // tr,), [_row_spec(tr, d_a)] * 2, [_row_spec(tr, LANES)],
                 [jax.ShapeDtypeStruct((SEQ, LANES), F32)], [dattn, attn], sem=("parallel",))[0][0]


def _attn_bwd(qkv, dattn, lse, delta, bias, p, stages=()):
    d_a = _d_a()
    nblk = SEQ // DILATIONS[p] // BLOCK
    has_next = nblk > 1
    scale = HEAD_DIM ** -0.5
    qv, dov, lv, tv = (_tile_view(x, p) for x in (qkv, dattn, lse, delta))

    rows = 2 * BLOCK if has_next else BLOCK

    def body(q_ref, qn_ref, k_ref, v_ref, do_ref, don_ref, l_ref, ln_ref, t_ref, tn_ref, b_ref, dq_ref, dk_ref, dv_ref, db_ref,
             carry_ref, s_ref, dp_ref, p_ref, ds_ref):
        j = pl.program_id(1)

        @pl.when((pl.program_id(0) == 0) & (j == 0))
        def _():
            db_ref[...] = jnp.zeros_like(db_ref)

        def both(cur, nxt, width, dtype):
            cur = cur[...].reshape(BLOCK, width).astype(dtype)
            return jnp.concatenate([cur, nxt[...].reshape(BLOCK, width).astype(dtype)], axis=0) if has_next else cur

        k_all = k_ref[...].reshape(BLOCK, d_a).astype(BF16)
        v_all = v_ref[...].reshape(BLOCK, d_a).astype(BF16)
        q_all, do_all = both(q_ref, qn_ref, d_a, BF16), both(do_ref, don_ref, d_a, BF16)
        l_all, t_all = both(l_ref, ln_ref, LANES, F32), both(t_ref, tn_ref, LANES, F32)
        if has_next:
            no_next = (lax.broadcasted_iota(I32, (rows, BLOCK), 0) >= BLOCK) & (j == nblk - 1)
        for h in range(N_HEADS):
            sl = slice(h * HEAD_DIM, (h + 1) * HEAD_DIM)
            s = _dot(q_all[:, sl], k_all[:, sl], "nt") * scale
            if has_next:
                s = jnp.where(no_next, NEG_INF, s + jnp.concatenate([b_ref[0, h, 1], b_ref[0, h, 0]], axis=0))
            else:
                s = s + b_ref[0, h, 1]
            s_ref[h] = s
            dp_ref[h] = _dot(do_all[:, sl], v_all[:, sl], "nt")
        for h in range(N_HEADS):
            pr = jnp.exp(s_ref[h] - l_all[:, h:h + 1])
            ds = pr * (dp_ref[h] - t_all[:, h:h + 1])
            db_ref[h, 1] += ds[:BLOCK]
            if has_next:
                db_ref[h, 0] += ds[BLOCK:]
            p_ref[h] = pr.astype(BF16)
            ds_ref[h] = ds.astype(BF16)
        for h in range(N_HEADS):
            sl = slice(h * HEAD_DIM, (h + 1) * HEAD_DIM)
            dq = _dot(ds_ref[h], k_all[:, sl], "nn") * scale
            mine = dq[:BLOCK]
            if has_next:
                mine = mine + jnp.where(j > 0, carry_ref[:, sl], 0.0)
            dq_ref[..., sl] = mine.reshape(*dq_ref.shape[:-1], HEAD_DIM)
            if has_next:
                carry_ref[:, sl] = dq[BLOCK:]
            dk_ref[..., sl] = (_dot(ds_ref[h], q_all[:, sl], "tn") * scale).reshape(*dk_ref.shape[:-1], HEAD_DIM)
            dv_ref[..., sl] = _dot(p_ref[h], do_all[:, sl], "tn").reshape(*dv_ref.shape[:-1], HEAD_DIM)

    def big(col, shift=0):
        return _tile_spec(p, d_a, col, shift)

    def small(shift=0):
        return _tile_spec(p, LANES, 0, shift)

    (dq, dk, dv, dbias), st = _call(
        f"attn_bwd{p}", body, _tile_grid(p),
        [big(0), big(0, 1), big(1), big(2), big(0), big(0, 1), small(), small(1), small(), small(1),
         pl.BlockSpec((1, N_HEADS, 2, BLOCK, BLOCK), lambda s, n: (p, 0, 0, 0, 0))],
        [big(0), big(0), big(0), pl.BlockSpec((N_HEADS, 2, BLOCK, BLOCK), lambda s, n: (0, 0, 0, 0))],
        [jax.ShapeDtypeStruct(_view_shape(d_a, p), F32)] * 3 + [jax.ShapeDtypeStruct((N_HEADS, 2, BLOCK, BLOCK), F32)],
        [qv, qv, qv, qv, dov, dov, lv, lv, tv, tv, bias],
        scratch=[pltpu.VMEM((BLOCK, d_a), F32), pltpu.VMEM((N_HEADS, rows, BLOCK), F32), pltpu.VMEM((N_HEADS, rows, BLOCK), F32),
                 pltpu.VMEM((N_HEADS, rows, BLOCK), BF16), pltpu.VMEM((N_HEADS, rows, BLOCK), BF16)], stages=stages)
    return (dq.reshape(SEQ, d_a), dk.reshape(SEQ, d_a), dv.reshape(SEQ, d_a), dbias), st


def _rel_bias_grad(dbias, buckets):
    nh = N_HEADS

    def body(d0, d1, d2, bk_ref, o_ref, t_ref):
        ds = (d0, d1, d2)

        def per_bucket(b, carry):
            for h in range(nh):
                acc = jnp.zeros((BLOCK, BLOCK), F32)
                for p in range(3):
                    for w in range(2):
                        acc = acc + jnp.where(bk_ref[p, w] == b, ds[p][h, w], 0.0)
                t_ref[pl.ds(b * nh + h, 1), :] = jnp.sum(acc, axis=0, keepdims=True)
            return carry

        lax.fori_loop(0, N_BUCKETS, per_bucket, 0)
        o_ref[...] = jnp.sum(t_ref[...], axis=-1, keepdims=True)

    return _call("rel_bias_grad", body, (1,), [_fix_spec((nh, 2, BLOCK, BLOCK))] * 3 + [_fix_spec((3, 2, BLOCK, BLOCK))],
                 [_fix_spec((N_BUCKETS * nh, 1))], [jax.ShapeDtypeStruct((N_BUCKETS * nh, 1), F32)], [*dbias, buckets],
                 scratch=[pltpu.VMEM((N_BUCKETS * nh, LANES), F32)])[0][0]


def _gmlp_fwd(rest, col0, gain, bias, ws, bs, causal, stages=()):
    d_b = _d_b()

    def body(u_ref, v_ref, g_ref, b_ref, ws_ref, bs_ref, c_ref, o_ref):
        u = u_ref[...].reshape(BLOCK, d_b)
        xhat, _ = _layer_norm_stats(_gelu(v_ref[...].reshape(BLOCK, d_b)))
        vn = (xhat * g_ref[...] + b_ref[...]).astype(BF16)
        outs = []
        for g in range(N_GROUPS):
            sl = slice(g * BLOCK, (g + 1) * BLOCK)
            w = jnp.where(c_ref[...] > 0, ws_ref[g], 0.0)
            z = _dot(w, vn[:, sl], "nn") + bs_ref[:, g:g + 1]
            outs.append(_gelu(u[:, sl]) * z)
        o_ref[...] = jnp.concatenate(outs, axis=-1).reshape(o_ref.shape)

    (out,), st = _call(
        "gmlp_fwd", body, (1, SEQ // BLOCK),
        [_tile_spec(0, d_b, col0), _tile_spec(0, d_b, col0 + 1), _fix_spec((1, d_b)), _fix_spec((1, d_b)),
         _fix_spec((N_GROUPS, BLOCK, BLOCK)), _fix_spec((BLOCK, N_GROUPS)), _fix_spec((BLOCK, BLOCK))],
        [_tile_spec(0, d_b, 0)], [jax.ShapeDtypeStruct(_view_shape(d_b, 0), F32)],
        [_tile_view(rest, 0), _tile_view(rest, 0), gain, bias, ws, bs, causal], sem=("parallel", "parallel"), stages=stages)
    return out.reshape(SEQ, d_b), st


def _gmlp_bwd(rest, col0, dgmlp, gain, bias, ws, bs, causal, stages=()):
    d_b = _d_b()
    nchunk = SEQ // BLOCK

    def body(u_ref, v_ref, dg_ref, g_ref, b_ref, ws_ref, bs_ref, c_ref, du_ref, dv_ref, dws_ref, dbs_ref, dgain_ref, dbias_ref):
        c = pl.program_id(1)

        @pl.when(c == 0)
        def _():
            dws_ref[...] = jnp.zeros_like(dws_ref)
            dbs_ref[...] = jnp.zeros_like(dbs_ref)
            dgain_ref[...] = jnp.zeros_like(dgain_ref)
            dbias_ref[...] = jnp.zeros_like(dbias_ref)

        u = u_ref[...].reshape(BLOCK, d_b)
        v = v_ref[...].reshape(BLOCK, d_b)
        dgm = dg_ref[...].reshape(BLOCK, d_b)
        xhat, rstd = _layer_norm_stats(_gelu(v))
        vn = (xhat * g_ref[...] + b_ref[...]).astype(BF16)
        lane = lax.broadcasted_iota(I32, (BLOCK, LANES), 1)
        dus, dvns = [], []
        dbs = dbs_ref[...]
        for g in range(N_GROUPS):
            sl = slice(g * BLOCK, (g + 1) * BLOCK)
            w = jnp.where(c_ref[...] > 0, ws_ref[g], 0.0).astype(BF16)
            z = _dot(w, vn[:, sl], "nn") + bs_ref[:, g:g + 1]
            dz = dgm[:, sl] * _gelu(u[:, sl])
            dus.append(dgm[:, sl] * z * _gelu_grad(u[:, sl]))
            dws_ref[g] += _dot(dz, vn[:, sl], "nt")
            dbs = dbs + jnp.where(lane == g, jnp.sum(dz, axis=-1, keepdims=True), 0.0)
            dvns.append(_dot(w, dz, "tn"))
        dbs_ref[...] = dbs
        dvn = jnp.concatenate(dvns, axis=-1)
        dgain_ref[...] += jnp.sum(dvn * xhat, axis=0, keepdims=True)
        dbias_ref[...] += jnp.sum(dvn, axis=0, keepdims=True)
        dvg = _layer_norm_bwd(dvn * g_ref[...], xhat, rstd)
        du_ref[...] = jnp.concatenate(dus, axis=-1).reshape(du_ref.shape)
        dv_ref[...] = (dvg * _gelu_grad(v)).reshape(dv_ref.shape)

        @pl.when(c == nchunk - 1)
        def _():
            for g in range(N_GROUPS):
                dws_ref[g] = jnp.where(c_ref[...] > 0, dws_ref[g], 0.0)

    (du, dv, dws, dbs, dgain, dbias), st = _call(
        "gmlp_bwd", body, (1, nchunk),
        [_tile_spec(0, d_b, col0), _tile_spec(0, d_b, col0 + 1), _tile_spec(0, d_b, 0), _fix_spec((1, d_b)), _fix_spec((1, d_b)),
         _fix_spec((N_GROUPS, BLOCK, BLOCK)), _fix_spec((BLOCK, N_GROUPS)), _fix_spec((BLOCK, BLOCK))],
        [_tile_spec(0, d_b, 0), _tile_spec(0, d_b, 0), _fix_spec((N_GROUPS, BLOCK, BLOCK)), _fix_spec((BLOCK, LANES)),
         _fix_spec((1, d_b)), _fix_spec((1, d_b))],
        [jax.ShapeDtypeStruct(_view_shape(d_b, 0), F32)] * 2
        + [jax.ShapeDtypeStruct((N_GROUPS, BLOCK, BLOCK), F32), jax.ShapeDtypeStruct((BLOCK, LANES), F32)]
        + [jax.ShapeDtypeStruct((1, d_b), F32)] * 2,
        [_tile_view(rest, 0), _tile_view(rest, 0), _tile_view(dgmlp, 0), gain, bias, ws, bs, causal], stages=stages)
    return (du.reshape(SEQ, d_b), dv.reshape(SEQ, d_b), dws, dbs, dgain, dbias), st


def _assemble_dproj(dqkv, du, dv, dga, dgb, tr=128):
    d_a, d_b, d_in = _d_a(), _d_b(), _d_in()
    tr = _pick(tr, SEQ)

    def body(*refs):
        att, (du_ref, dv_ref, dga_ref, dgb_ref, o_ref) = refs[:9], refs[9:]
        for i in range(3):
            o_ref[:, i * d_a:(i + 1) * d_a] = (att[3 * i][...] + att[3 * i + 1][...] + att[3 * i + 2][...]).astype(BF16)
        o_ref[:, 3 * d_a:3 * d_a + d_b] = du_ref[...].astype(BF16)
        o_ref[:, 3 * d_a + d_b:3 * d_a + 2 * d_b] = dv_ref[...].astype(BF16)
        o_ref[:, 3 * d_a + 2 * d_b:3 * d_a + 2 * d_b + D_MODEL] = dga_ref[...]
        o_ref[:, 3 * d_a + 2 * d_b + D_MODEL:] = dgb_ref[...]

    return _call("assemble_dproj", body, (SEQ // tr,), [_row_spec(tr, d_a)] * 9 + [_row_spec(tr, d_b)] * 2 + [_row_spec(tr, D_MODEL)] * 2,
                 [_row_spec(tr, d_in)], [jax.ShapeDtypeStruct((SEQ, d_in), BF16)], [*dqkv, du, dv, dga, dgb], sem=("parallel",))[0][0]


def _dw(name, a, b, kind, core, mine, add=None, tn=1152, stages=()):
    s, m = a.shape
    n = b.shape[1]
    rs, cs = (m, n // N_DEV) if kind == "col" else (m // N_DEV, n)
    tn = _pick(tn if kind == "col" else 512, cs)
    nj = cs // tn

    def shard(q, c_ref):
        return 2 * q + (c_ref[0] if mine else 1 - c_ref[0])

    if kind == "col":
        a_spec = pl.BlockSpec((s, m), lambda q, j, c_ref: (0, 0))
        b_spec = pl.BlockSpec((s, tn), lambda q, j, c_ref: (0, shard(q, c_ref) * nj + j))
    else:
        a_spec = pl.BlockSpec((s, rs), lambda q, j, c_ref: (0, shard(q, c_ref)))
        b_spec = pl.BlockSpec((s, tn), lambda q, j, c_ref: (0, j))
    o_spec = pl.BlockSpec((None, rs, tn), lambda q, j, c_ref: (q, 0, j))

    def body(a_ref, b_ref, *rest):
        acc = _dot(a_ref[...], b_ref[...], "tn")
        if add is not None:
            acc = acc + rest[0][...].astype(F32)
        rest[-1][...] = acc.astype(BF16)

    (out,), st = _call(name, body, (N_CHIPS, nj), [a_spec, b_spec] + ([o_spec] if add is not None else []), [o_spec],
                       [jax.ShapeDtypeStruct((N_CHIPS, rs, cs), BF16)], [a, b] + ([add] if add is not None else []),
                       sem=("parallel", "parallel"), stages=stages, prefetch=core)
    return out, st


def _adamw(w, g, m, v):
    m = ADAM_B1 * m + (1.0 - ADAM_B1) * g
    v = ADAM_B2 * v + (1.0 - ADAM_B2) * (g * g)
    m_hat = m / (1.0 - ADAM_B1 ** ADAM_STEP)
    v_hat = v / (1.0 - ADAM_B2 ** ADAM_STEP)
    delta = -ADAM_LR * (m_hat / (jnp.sqrt(v_hat) + ADAM_EPS) + ADAM_WD * w)
    return delta, m, v


def _adam_shard(name, pair, chip_sums, chip, w, m, v, tr=256, stages=()):
    rs, cs = w.shape
    tr = _pick(tr, rs)

    def body(chip_ref, own_ref, *refs):
        slots, (w_ref, m_ref, v_ref, g_ref, d_ref, nm_ref, nv_ref) = refs[:N_CHIPS], refs[N_CHIPS:]
        g = None
        for q in range(N_CHIPS):
            term = jnp.where(chip_ref[0] == q, own_ref[...], slots[q][...]).astype(F32)
            g = term if g is None else g + term
        d, nm, nv = _adamw(w_ref[...], g, m_ref[...], v_ref[...])
        g_ref[...], d_ref[...], nm_ref[...], nv_ref[...] = g, d, nm, nv

    def slot(q):
        return pl.BlockSpec((None, tr, cs), lambda i, c_ref: (jnp.where(c_ref[0] == q, (q + 1) % N_CHIPS, q), i, 0))

    spec = pl.BlockSpec((tr, cs), lambda i, c_ref: (i, 0))
    return _call(name, body, (rs // tr,),
                 [pl.BlockSpec((None, tr, cs), lambda i, c_ref: (c_ref[0], i, 0))] + [slot(q) for q in range(N_CHIPS)] + [spec, spec, spec],
                 [spec] * 4, [jax.ShapeDtypeStruct((rs, cs), F32)] * 4, [pair] + [chip_sums] * N_CHIPS + [w, m, v],
                 sem=("parallel",), stages=stages, prefetch=chip, shown=True)


def _adam_small(part, parts, me, w, m, v):
    rows = w.shape[0]

    def body(me_ref, own_ref, *refs):
        slots, (w_ref, m_ref, v_ref, g_ref, d_ref, nm_ref, nv_ref) = refs[:N_DEV], refs[N_DEV:]
        g = None
        for j in range(N_DEV):
            term = jnp.where(me_ref[0] == j, own_ref[...], slots[j][...])
            g = term if g is None else g + term
        d, nm, nv = _adamw(w_ref[...], g, m_ref[...], v_ref[...])
        g_ref[...], d_ref[...], nm_ref[...], nv_ref[...] = g, d, nm, nv

    def slot(j):
        return pl.BlockSpec((None, rows, LANES), lambda i, me_ref: (jnp.where(me_ref[0] == j, (j + 1) % N_DEV, j), 0, 0))

    spec = _fix_spec((rows, LANES))
    return _call("adam_small", body, (1,), [spec] + [slot(j) for j in range(N_DEV)] + [spec, spec, spec], [spec] * 4,
                 [jax.ShapeDtypeStruct((rows, LANES), F32)] * 4, [part] + [parts] * N_DEV + [w, m, v], prefetch=me, shown=True)[0]


def _small_sizes():
    d_b = _d_b()
    return (("loss", 1), ("rel_bias", N_BUCKETS * N_HEADS), ("ln_v_gain", d_b), ("ln_v_bias", d_b),
            ("w_spatial", N_GROUPS * BLOCK * BLOCK), ("b_spatial", N_GROUPS * BLOCK), ("ln1_gain", D_MODEL), ("ln1_bias", D_MODEL),
            ("b_ff1", D_FF), ("b_ff2", D_MODEL), ("ln2_gain", D_MODEL), ("ln2_bias", D_MODEL))


def _pack(vals):
    pieces = []
    for name, size in _small_sizes():
        flat = vals[name].reshape(-1).astype(F32)
        padded = -(-size // (SUBLANES * LANES)) * SUBLANES * LANES
        pieces.append(jnp.pad(flat, (0, padded - size)).reshape(-1, LANES))
    return jnp.concatenate(pieces, axis=0)


def _unpack(buf):
    out, row = {}, 0
    for name, size in _small_sizes():
        rows = -(-size // (SUBLANES * LANES)) * SUBLANES
        out[name] = buf[row:row + rows].reshape(-1)[:size]
        row += rows
    return out


def kernel(x, w_in, rel_bias, ln_v_gain, ln_v_bias, w_spatial, b_spatial, w_proj_a, w_proj_b, w_out, ln1_gain, ln1_bias, w_ff1, b_ff1, w_ff2, b_ff2, ln2_gain, ln2_bias, loss_target, m_w_in, m_rel_bias, m_ln_v_gain, m_ln_v_bias, m_w_spatial, m_b_spatial, m_w_proj_a, m_w_proj_b, m_w_out, m_ln1_gain, m_ln1_bias, m_w_ff1, m_b_ff1, m_w_ff2, m_b_ff2, m_ln2_gain, m_ln2_bias, v_w_in, v_rel_bias, v_ln_v_gain, v_ln_v_bias, v_w_spatial, v_b_spatial, v_w_proj_a, v_w_proj_b, v_w_out, v_ln1_gain, v_ln1_bias, v_w_ff1, v_b_ff1, v_w_ff2, v_b_ff2, v_ln2_gain, v_ln2_bias):
    d_a, d_b, d_in = _d_a(), _d_b(), _d_in()
    weights = dict(w_in=w_in, rel_bias=rel_bias, ln_v_gain=ln_v_gain, ln_v_bias=ln_v_bias, w_spatial=w_spatial, b_spatial=b_spatial,
                   w_proj_a=w_proj_a, w_proj_b=w_proj_b, w_out=w_out, ln1_gain=ln1_gain, ln1_bias=ln1_bias, w_ff1=w_ff1, b_ff1=b_ff1,
                   w_ff2=w_ff2, b_ff2=b_ff2, ln2_gain=ln2_gain, ln2_bias=ln2_bias)
    mom1 = dict(w_in=m_w_in, rel_bias=m_rel_bias, ln_v_gain=m_ln_v_gain, ln_v_bias=m_ln_v_bias, w_spatial=m_w_spatial,
                b_spatial=m_b_spatial, w_proj_a=m_w_proj_a, w_proj_b=m_w_proj_b, w_out=m_w_out, ln1_gain=m_ln1_gain,
                ln1_bias=m_ln1_bias, w_ff1=m_w_ff1, b_ff1=m_b_ff1, w_ff2=m_w_ff2, b_ff2=m_b_ff2, ln2_gain=m_ln2_gain, ln2_bias=m_ln2_bias)
    mom2 = dict(w_in=v_w_in, rel_bias=v_rel_bias, ln_v_gain=v_ln_v_gain, ln_v_bias=v_ln_v_bias, w_spatial=v_w_spatial,
                b_spatial=v_b_spatial, w_proj_a=v_w_proj_a, w_proj_b=v_w_proj_b, w_out=v_w_out, ln1_gain=v_ln1_gain,
                ln1_bias=v_ln1_bias, w_ff1=v_w_ff1, b_ff1=v_b_ff1, w_ff2=v_w_ff2, b_ff2=v_b_ff2, ln2_gain=v_ln2_gain, ln2_bias=v_ln2_bias)

    mx, my, mc = _coords()
    me = (4 * mx + 2 * my + mc).astype(I32).reshape(1)
    chip = (2 * mx + my).astype(I32).reshape(1)
    full = {n: _cast_into_place(f"cast_{n}", weights[n][0], KINDS[n], me) for n in KINDS}
    sent = {n: (0, 0, 0) for n in KINDS}

    def keep(table, n):
        def store(outs):
            table[n] = outs[0]
        return store

    def gathering(**new):
        stages = []
        for n in KINDS:
            out, relayed, passed = sent[n]
            units = new.get(n, 0)
            if units or relayed < out or passed < relayed:
                st = _gather_stage(full[n], KINDS[n], (out, units) if units else None,
                                   (relayed, out - relayed) if relayed < out else None, (passed, relayed - passed) if passed < relayed else None)
                st.store = keep(full, n)
                sent[n] = (out + units, out, relayed)
                stages.append(st)
        return stages

    def settle(stages, outs):
        for st, o in zip(stages, outs):
            st.store(o)

    def alone(name, stages):
        settle(stages, _comm_only(name, stages))

    def here(n):
        assert sent[n] == (16, 16, 16), (n, sent[n])
        return full[n]

    alone("gather_w_in_near", gathering(w_in=16))
    alone("gather_w_in_relay", gathering())
    alone("gather_w_in_sibling", gathering())

    xs = _to_perm(x[0])
    target = _to_perm(loss_target[0])
    xb = _cast_bf16("cast_x", xs)
    g8 = BLOCK // N_SUB
    ws_t = w_spatial[0].reshape(N_GROUPS, g8, N_SUB, g8, N_SUB).transpose(0, 2, 1, 4, 3).reshape(N_GROUPS, BLOCK, BLOCK)
    bs_t = b_spatial[0].reshape(N_GROUPS, g8, N_SUB).transpose(2, 1, 0).reshape(BLOCK, N_GROUPS)
    idx = _local_index(0)
    causal = jnp.asarray((idx[:, None] >= idx[None, :]).astype(np.float32))
    buckets = jnp.asarray(_bucket_tables())
    bias = _bias_expand(rel_bias, buckets)

    hosted = gathering(w_proj_a=16, w_proj_b=16, w_ff1=1)
    (qkv,), st = _matmul("proj_qkv", xb, here("w_in"), "nn", [F32], n=3 * d_a, stages=hosted)
    settle(hosted, st)
    hosted = gathering(w_out=16, w_ff1=4)
    (rest,), st = _matmul("proj_rest", xb, here("w_in"), "nn", [F32], b_off=3 * d_a, n=d_in - 3 * d_a, stages=hosted)
    settle(hosted, st)
    fwd = []
    for p in range(3):
        hosted = gathering(w_ff1=(3, 5, 3)[p])
        res, st = _attn_fwd(qkv, bias, p, stages=hosted)
        settle(hosted, st)
        fwd.append(res)
    hosted = gathering(w_ff2=1)
    (attn, attn_b, lse), st = _attn_combine([o for o, _ in fwd], [l for _, l in fwd], stages=hosted)
    settle(hosted, st)
    hosted = gathering(w_ff2=2)
    gmlp, st = _gmlp_fwd(rest, 0, ln_v_gain, ln_v_bias, ws_t, bs_t, causal, stages=hosted)
    settle(hosted, st)
    hosted = gathering(w_ff2=3)
    (ya,), st = _matmul("proj_a", attn_b, here("w_proj_a"), "nn", [F32], stages=hosted)
    settle(hosted, st)
    gate_a, gate_b = 2 * d_b, 2 * d_b + D_MODEL

    def merge(acc, ya_, ga, gb):
        return acc, _sigmoid(ga) * ya_ + _sigmoid(gb) * acc

    hosted = gathering(w_ff2=5)
    (yb, merged), st = _matmul("proj_b_merge", gmlp, here("w_proj_b"), "nn", [F32, BF16], merge,
                               [(ya, "mn", 0), (rest, "mn", gate_a), (rest, "mn", gate_b)], tn=256, stages=hosted)
    settle(hosted, st)
    hosted = gathering(w_ff2=3)
    (pre1,), st = _matmul("out_proj", merged, here("w_out"), "nn", [F32], lambda acc, x_: (ALPHA * x_ + acc,), [(xs, "mn", 0)], stages=hosted)
    settle(hosted, st)
    hosted = gathering(w_ff2=2)
    (xhat1, rstd1, h1b), st = _ln1_fwd(pre1, ln1_gain, ln1_bias, stages=hosted)
    settle(hosted, st)

    def relu2(acc, b_):
        r = jnp.maximum(acc + b_, 0.0)
        return r, r * r

    hosted = gathering()
    (relu, fb), st = _matmul("ff1", h1b, here("w_ff1"), "nn", [F32, BF16], relu2, [(b_ff1, "row", 0)], stages=hosted)
    settle(hosted, st)
    alone("gather_w_ff2_sibling", gathering())
    (ff,), _ = _matmul("ff2", fb, here("w_ff2"), "nn", [F32], lambda acc, b_: (acc + b_,), [(b_ff2, "row", 0)], tn=1024, tk=1024)

    core = lax.axis_index("c").astype(I32).reshape(1)
    factors, theirs, sib, pair, chips, reduced = {}, {}, {}, {}, {}, {}

    def grad_for_sibling(n, a, b, stages=()):
        factors[n] = (a, b)
        theirs[n], outs = _dw(f"dw_{n}_sibling", a, b, KINDS[n], core, False, stages=stages)
        settle(stages, outs)

    def to_sibling(n):
        st = _to_sibling_stage(theirs[n])
        st.store = keep(sib, n)
        return st

    def grad_own(n, stages=()):
        pair[n], outs = _dw(f"dw_{n}_own", *factors[n], KINDS[n], core, True, add=sib[n], stages=stages)
        settle(stages, outs)
        chips[n] = lax.empty(pair[n].shape, BF16)
        reduced[n] = 0

    def reducing(**new):
        stages = []
        for n, units in new.items():
            st = _to_chips_stage(pair[n], chips[n], (reduced[n], units))
            st.store = keep(chips, n)
            reduced[n] += units
            stages.append(st)
        return stages

    def summed(n):
        assert reduced[n] == 16, (n, reduced[n])
        return chips[n]

    dpre2, dpre2b, g_ln2_gain, g_ln2_bias, g_b_ff2, loss_part = _ln2_loss_bwd(ff, xhat1, ln1_gain, ln1_bias, ln2_gain, ln2_bias, target)
    grad_for_sibling("w_ff2", fb, dpre2b)

    def relu2_bwd(acc, r):
        da = acc * (2.0 * r)
        return da, da

    hosted = [to_sibling("w_ff2")]
    (dab, g_b_ff1), st = _matmul("d_ff1", dpre2b, here("w_ff2"), "nt", [BF16], relu2_bwd, [(relu, "mn", 0)], colsums=(1,), stages=hosted)
    settle(hosted, st)
    grad_own("w_ff2")
    grad_for_sibling("w_ff1", h1b, dab, reducing(w_ff2=3))
    hosted = reducing(w_ff2=8) + [to_sibling("w_ff1")]
    (dh1,), st = _matmul("d_h1", dab, here("w_ff1"), "nt", [F32], lambda acc, d_: (acc + ALPHA * d_,), [(dpre2, "mn", 0)], stages=hosted)
    settle(hosted, st)
    grad_own("w_ff1", reducing(w_ff2=4))
    hosted = reducing(w_ff2=1)
    (dpre1, dpre1b, g_ln1_gain, g_ln1_bias), st = _ln1_bwd(dh1, xhat1, rstd1, ln1_gain, stages=hosted)
    settle(hosted, st)
    grad_for_sibling("w_out", merged, dpre1b, reducing(w_ff1=1))

    def merge_bwd(acc, ga, gb, ya_, yb_):
        sa, sb = _sigmoid(ga), _sigmoid(gb)
        return acc * sa, acc * sb, acc * ya_ * (sa * (1.0 - sa)), acc * yb_ * (sb * (1.0 - sb))

    hosted = reducing(w_ff1=6) + [to_sibling("w_out")]
    (dya, dyb, dga, dgb), st = _matmul("d_merge", dpre1b, here("w_out"), "nt", [BF16] * 4, merge_bwd,
                                       [(rest, "mn", gate_a), (rest, "mn", gate_b), (ya, "mn", 0), (yb, "mn", 0)], tn=256, stages=hosted)
    settle(hosted, st)
    grad_own("w_out", reducing(w_ff1=1))
    grad_for_sibling("w_proj_a", attn_b, dya)
    grad_for_sibling("w_proj_b", gmlp, dyb)
    hosted = reducing(w_ff1=1) + [to_sibling("w_proj_a"), to_sibling("w_proj_b")]
    (dattn,), st = _matmul("d_attn", dya, here("w_proj_a"), "nt", [F32], stages=hosted)
    settle(hosted, st)
    grad_own("w_proj_a")
    grad_own("w_proj_b")
    hosted = reducing(w_ff1=1)
    (dgmlp,), st = _matmul("d_gmlp", dyb, here("w_proj_b"), "nt", [F32], stages=hosted)
    settle(hosted, st)
    hosted = reducing(w_ff1=2)
    (du, dvb, dws_t, dbs_t, g_lnv_gain, g_lnv_bias), st = _gmlp_bwd(rest, 0, dgmlp, ln_v_gain, ln_v_bias, ws_t, bs_t, causal, stages=hosted)
    settle(hosted, st)
    delta = _attn_delta(dattn, attn)
    bwd = []
    for p in range(3):
        hosted = reducing(**({"w_ff1": 4}, {"w_out": 16}, {"w_proj_a": 16, "w_proj_b": 16})[p])
        res, st = _attn_bwd(qkv, dattn, lse, delta, bias, p, stages=hosted)
        settle(hosted, st)
        bwd.append(res)
    g_rel_bias = _rel_bias_grad([b[3] for b in bwd], buckets)
    dproj = _assemble_dproj([b[i] for i in range(3) for b in bwd], du, dvb, dga, dgb)

    g_w_spatial = dws_t.reshape(N_GROUPS, N_SUB, g8, N_SUB, g8).transpose(0, 2, 1, 4, 3)
    g_b_spatial = dbs_t[:, :N_GROUPS].reshape(N_SUB, g8, N_GROUPS).transpose(2, 1, 0)
    part = _pack(dict(loss=loss_part, rel_bias=g_rel_bias, ln_v_gain=g_lnv_gain, ln_v_bias=g_lnv_bias, w_spatial=g_w_spatial,
                      b_spatial=g_b_spatial, ln1_gain=g_ln1_gain, ln1_bias=g_ln1_bias, b_ff1=g_b_ff1, b_ff2=g_b_ff2,
                      ln2_gain=g_ln2_gain, ln2_bias=g_ln2_bias))
    small = _small_stage(part)
    small.store = keep(sib, "small")
    grad_for_sibling("w_in", xb, dproj, [small])
    parts = sib["small"]
    first = SEQ // 4

    def add_residual(acc, d_):
        return (acc + ALPHA * d_,)

    hosted = [to_sibling("w_in")]
    (dx0,), st = _matmul("d_x0", dproj, here("w_in"), "nt", [F32], add_residual, [(dpre1, "mn", 0)], tm=512, tn=1024, tk=3072, m=first,
                         stages=hosted)
    settle(hosted, st)
    grad_own("w_in")
    hosted = reducing(w_in=7)
    (dx1,), st = _matmul("d_x1", dproj, here("w_in"), "nt", [F32], add_residual, [(dpre1, "mn", 0)], tm=512, tn=1024, tk=3072,
                         m_off=first, m=SEQ - first, stages=hosted)
    settle(hosted, st)
    grad_x = _from_perm(jnp.concatenate([dx0, dx1], axis=0))[None]

    out_g, out_d, out_m, out_v = {}, {}, {}, {}
    for n, units in (("w_ff2", 3), ("w_ff1", 3), ("w_out", 2), ("w_proj_a", 1), ("w_proj_b", 0), ("w_in", 0)):
        hosted = reducing(w_in=units) if units else []
        (g, d, nm, nv), st = _adam_shard(f"adam_{n}", pair[n], summed(n), chip, weights[n][0], mom1[n][0], mom2[n][0], stages=hosted)
        settle(hosted, st)
        out_g[n], out_d[n], out_m[n], out_v[n] = g[None], d[None], nm[None], nv[None]

    zero = jnp.zeros((1,), F32)
    sg, sd, sm, sv = (_unpack(b) for b in _adam_small(
        part, parts, me, _pack({**weights, "loss": zero}), _pack({**mom1, "loss": zero}), _pack({**mom2, "loss": zero})))
    for n in WEIGHT_ORDER:
        if n not in KINDS:
            shape = weights[n].shape
            out_g[n], out_d[n], out_m[n], out_v[n] = (t[n].reshape(shape) for t in (sg, sd, sm, sv))
    loss = sg["loss"].reshape(())
    return (loss, grad_x, *[out_g[n] for n in WEIGHT_ORDER], *[out_d[n] for n in WEIGHT_ORDER],
            *[out_m[n] for n in WEIGHT_ORDER], *[out_v[n] for n in WEIGHT_ORDER])
```

```python
import math

import jax
import jax.numpy as jnp
import numpy as np
from jax import lax
from jax.experimental import pallas as pl
from jax.experimental.pallas import tpu as pltpu

F32 = jnp.float32
BF16 = jnp.bfloat16
I32 = jnp.int32

SEQ = 2048
D_MODEL = 2048
HEAD_DIM = 128
N_HEADS = 8
N_GROUPS = 8
D_FF = 4 * D_MODEL
BLOCK = 128
DILATIONS = (1, 4, 16)
N_BUCKETS = 32
MAX_DISTANCE = 2048
ALPHA = 2.0 ** 0.25
LN_EPS = 1e-5
NEG_INF = -1e30
N_DEV = 8
N_CHIPS = 4
N_SUB = 16
ADAM_LR, ADAM_B1, ADAM_B2, ADAM_EPS, ADAM_WD, ADAM_STEP = 0.001, 0.9, 0.999, 1e-08, 0.01, 10
LANES = 128
SUBLANES = 8
VMEM_LIMIT = 56 * 1024 * 1024
MESH = pl.DeviceIdType.MESH
ANY = pl.BlockSpec(memory_space=pl.ANY)
WEIGHT_ORDER = ("w_in", "rel_bias", "ln_v_gain", "ln_v_bias", "w_spatial", "b_spatial", "w_proj_a", "w_proj_b", "w_out",
                "ln1_gain", "ln1_bias", "w_ff1", "b_ff1", "w_ff2", "b_ff2", "ln2_gain", "ln2_bias")
KINDS = {"w_in": "col", "w_proj_a": "col", "w_proj_b": "col", "w_out": "row", "w_ff1": "col", "w_ff2": "row"}


def _d_a():
    return N_HEADS * HEAD_DIM


def _d_b():
    return N_GROUPS * BLOCK


def _d_in():
    return 3 * _d_a() + 2 * _d_b() + 2 * D_MODEL


def _pick(t, n, *others):
    if n <= t and all(o % n == 0 for o in others):
        return n
    for c in range(min(t, n) // LANES * LANES, 0, -LANES):
        if n % c == 0 and all(o % c == 0 for o in others):
            return c
    raise ValueError((t, n, others))


class _Stage:
    def __init__(self, ins, outs, alias, sems, start, finish):
        self.ins, self.outs, self.alias, self.sems, self.start, self.finish = ins, outs, alias, sems, start, finish


def _call(name, body, grid, in_specs, out_specs, out_shape, operands, scratch=(), sem=None, stages=(), sequential=False, prefetch=None,
          shown=False, alias=None):
    n_in, n_out, n_sc = len(in_specs), len(out_specs), len(scratch)
    st_in = [len(s.ins) for s in stages]
    st_out = [len(s.outs) for s in stages]
    st_sem = [len(s.sems) for s in stages]
    n_pre = 0 if prefetch is None else 1
    aliases, ioff, ooff = {i + n_pre: o for i, o in (alias or {}).items()}, n_in + n_pre, n_out
    for s, ni, no in zip(stages, st_in, st_out):
        for i, o in s.alias.items():
            aliases[ioff + i] = ooff + o
        ioff, ooff = ioff + ni, ooff + no

    def split(refs, counts):
        out, at = [], 0
        for c in counts:
            out.append(refs[at:at + c])
            at += c
        return out

    def wrapped(*refs):
        ins, sins, outs, souts, sc, ssems = split(refs[n_pre:], [n_in, sum(st_in), n_out, sum(st_out), n_sc, sum(st_sem)])
        parts = list(zip(stages, split(sins, st_in), split(souts, st_out), split(ssems, st_sem)))
        if sequential:
            for s, a, b, c in parts:
                s.start(a, b, c)
                s.finish(a, b, c)
            return
        if parts:
            first = _all_of([pl.program_id(i) == 0 for i in range(len(grid))])
            last = _all_of([pl.program_id(i) == g - 1 for i, g in enumerate(grid)])

            @pl.when(first)
            def _():
                for s, a, b, c in parts:
                    s.start(a, b, c)

        body(*(refs[:n_pre] if shown else ()), *ins, *outs, *sc)
        if parts:
            @pl.when(last)
            def _():
                for s, a, b, c in parts:
                    s.finish(a, b, c)

    if stages or sem is None:
        sem = ("arbitrary",) * len(grid)
    specs = dict(grid=grid, in_specs=list(in_specs) + [ANY] * sum(st_in), out_specs=list(out_specs) + [ANY] * sum(st_out),
                 scratch_shapes=list(scratch) + [x for s in stages for x in s.sems])
    if prefetch is not None:
        specs = dict(grid_spec=pltpu.PrefetchScalarGridSpec(num_scalar_prefetch=1, **specs))
    res = pl.pallas_call(
        wrapped, name=name, out_shape=list(out_shape) + [o for s in stages for o in s.outs], input_output_aliases=aliases,
        compiler_params=pltpu.CompilerParams(dimension_semantics=sem, vmem_limit_bytes=VMEM_LIMIT), **specs,
    )(*([prefetch] if n_pre else []), *operands, *[a for s in stages for a in s.ins])
    res = list(res)
    return res[:n_out], split(res[n_out:], st_out)


def _all_of(conds):
    out = conds[0]
    for c in conds[1:]:
        out = out & c
    return out


def _coords():
    return lax.axis_index("x"), lax.axis_index("y"), lax.axis_index("c")


def _other_chips(x, y):
    return ((1 - x, y), (x, 1 - y), (1 - x, 1 - y))


def _lin(dev):
    return 4 * dev[0] + 2 * dev[1] + dev[2]


def _piece(total, lo, n, units=16):
    assert total % units == 0
    return lo * (total // units), n * (total // units)


FLOWS = 4


def _split(lo, cnt):
    k = next(k for k in (FLOWS, 2, 1) if cnt % (2 * SUBLANES * k) == 0)
    return [(lo + i * (cnt // k), cnt // k) for i in range(k)]


def _remote(src, dst, send, recv, to):
    return pltpu.make_async_remote_copy(src_ref=src, dst_ref=dst, send_sem=send, recv_sem=recv, device_id=to, device_id_type=MESH)


def _placer(kind, n, lo, cnt):
    def place(ref, dev):
        if kind == "col":
            return ref.at[pl.ds(lo, cnt), pl.ds(pl.multiple_of(_lin(dev) * n, LANES), n)]
        return ref.at[pl.ds(pl.multiple_of(_lin(dev) * n + lo, 2 * SUBLANES), cnt), :]
    return place


def _spread_stage(full, kind, piece=(0, 16), home=False):
    n = (full.shape[1] if kind == "col" else full.shape[0]) // N_DEV
    lo, cnt = _piece(full.shape[0] if kind == "col" else n, *piece)
    parts = _split(lo, cnt)
    npeers = 1 if home else 2

    def copies(ins, outs, sems):
        send, recv = sems
        x, y, c = _coords()
        me = (x, y, c)
        peers = [(x, y, 1 - c)] if home else [(1 - x, y, c), (x, 1 - y, c)]
        out, arrive = [], []
        for k, t in enumerate(peers):
            for i, (plo, pcnt) in enumerate(parts):
                place = _placer(kind, n, plo, pcnt)
                out.append(_remote(place(outs[0], me), place(outs[0], me), send.at[i, k], recv.at[i, k], t))
                arrive.append(_remote(place(outs[0], t), place(outs[0], t), send.at[i, k], recv.at[i, k], t))
        return out, arrive

    def start(ins, outs, sems):
        for cp in copies(ins, outs, sems)[0]:
            cp.start()

    def finish(ins, outs, sems):
        out, arrive = copies(ins, outs, sems)
        for cp in arrive:
            cp.wait_recv()
        for cp in out:
            cp.wait_send()

    return _Stage([full], [jax.ShapeDtypeStruct(full.shape, full.dtype)], {0: 0},
                  [pltpu.SemaphoreType.DMA((len(parts), npeers)), pltpu.SemaphoreType.DMA((len(parts), npeers))], start, finish)


def _relay_stage(full, kind, piece=(0, 16)):
    n = (full.shape[1] if kind == "col" else full.shape[0]) // N_DEV
    lo, cnt = _piece(full.shape[0] if kind == "col" else n, *piece)
    half = cnt // 2
    assert half % (2 * SUBLANES) == 0, (cnt, kind)
    tops, bottoms = _split(lo, half), _split(lo + half, half)

    def copies(ins, outs, sems):
        send, recv = sems
        x, y, c = _coords()
        xn, yn, dg = (1 - x, y, c), (x, 1 - y, c), (1 - x, 1 - y, c)
        out, arrive, k = [], [], 0
        for came_from, to, parts in ((yn, xn, tops), (xn, yn, bottoms)):
            for plo, pcnt in parts:
                place = _placer(kind, n, plo, pcnt)
                out.append(_remote(place(outs[0], came_from), place(outs[0], came_from), send.at[k], recv.at[k], to))
                arrive.append(_remote(place(outs[0], dg), place(outs[0], dg), send.at[k], recv.at[k], to))
                k += 1
        return out, arrive

    def start(ins, outs, sems):
        for cp in copies(ins, outs, sems)[0]:
            cp.start()

    def finish(ins, outs, sems):
        out, arrive = copies(ins, outs, sems)
        for cp in arrive:
            cp.wait_recv()
        for cp in out:
            cp.wait_send()

    return _Stage([full], [jax.ShapeDtypeStruct(full.shape, full.dtype)], {0: 0},
                  [pltpu.SemaphoreType.DMA((len(tops) + len(bottoms),)), pltpu.SemaphoreType.DMA((len(tops) + len(bottoms),))], start, finish)


def _forward_stage(full, kind, piece=(0, 16)):
    n = (full.shape[1] if kind == "col" else full.shape[0]) // N_DEV
    lo, cnt = _piece(full.shape[0] if kind == "col" else n, *piece)
    place = _placer(kind, n, lo, cnt)

    def copies(ins, outs, sems):
        send, recv = sems
        x, y, c = _coords()
        chips = _other_chips(x, y)
        out = [_remote(place(outs[0], (*chip, c)), place(outs[0], (*chip, c)), send.at[k], recv.at[k], (x, y, 1 - c)) for k, chip in enumerate(chips)]
        arrive = [_remote(place(outs[0], (*chip, 1 - c)), place(outs[0], (*chip, 1 - c)), send.at[k], recv.at[k], (x, y, 1 - c))
                  for k, chip in enumerate(chips)]
        return out, arrive

    def start(ins, outs, sems):
        for cp in copies(ins, outs, sems)[0]:
            cp.start()

    def finish(ins, outs, sems):
        out, arrive = copies(ins, outs, sems)
        for cp in arrive:
            cp.wait_recv()
        for cp in out:
            cp.wait_send()

    return _Stage([full], [jax.ShapeDtypeStruct(full.shape, full.dtype)], {0: 0},
                  [pltpu.SemaphoreType.DMA((3,)), pltpu.SemaphoreType.DMA((3,))], start, finish)


def _to_sibling_stage(theirs):
    def copies(ins, outs, sems):
        send, recv = sems
        x, y, c = _coords()
        return [_remote(ins[0].at[q], outs[0].at[q], send.at[q], recv.at[q], (x, y, 1 - c)) for q in range(N_CHIPS)]

    def start(ins, outs, sems):
        for cp in copies(ins, outs, sems):
            cp.start()

    def finish(ins, outs, sems):
        for cp in copies(ins, outs, sems):
            cp.wait()

    return _Stage([theirs], [jax.ShapeDtypeStruct(theirs.shape, BF16)], {},
                  [pltpu.SemaphoreType.DMA((N_CHIPS,)), pltpu.SemaphoreType.DMA((N_CHIPS,))], start, finish)


def _to_chips_stage(pair, dst, piece=(0, 16)):
    lo, cnt = _piece(pair.shape[1], *piece)
    parts = _split(lo, cnt)
    nsem = 3 * len(parts)

    def copies(ins, outs, sems):
        send, recv = sems
        x, y, c = _coords()
        mine = 2 * x + y
        out, arrive, k = [], [], 0
        for px, py in _other_chips(x, y):
            for plo, pcnt in parts:
                rows = pl.ds(plo, pcnt)
                out.append(_remote(ins[0].at[2 * px + py, rows, :], outs[0].at[mine, rows, :], send.at[k], recv.at[k], (px, py, c)))
                arrive.append(_remote(ins[0].at[2 * px + py, rows, :], outs[0].at[2 * px + py, rows, :], send.at[k], recv.at[k], (px, py, c)))
                k += 1
        return out, arrive

    def start(ins, outs, sems):
        for cp in copies(ins, outs, sems)[0]:
            cp.start()

    def finish(ins, outs, sems):
        out, arrive = copies(ins, outs, sems)
        for cp in arrive:
            cp.wait_recv()
        for cp in out:
            cp.wait_send()

    return _Stage([pair, dst], [jax.ShapeDtypeStruct(dst.shape, dst.dtype)], {1: 0},
                  [pltpu.SemaphoreType.DMA((nsem,)), pltpu.SemaphoreType.DMA((nsem,))], start, finish)


def _fuse(parts, ins, outs, alias):
    parts = [p for p in parts if p is not None]
    sems = [x for st, _, _ in parts for x in st.sems]

    def run(which):
        def go(i, o, s):
            refs, at = list(i) + list(o), 0
            for st, pi, po in parts:
                getattr(st, which)([refs[k] for k in pi], [refs[k] for k in po], s[at:at + len(st.sems)])
                at += len(st.sems)
        return go

    return _Stage(ins, [jax.ShapeDtypeStruct(o.shape, o.dtype) for o in outs], alias, sems, run("start"), run("finish"))


def _gather_stage(full, kind, new=None, relay=None, forward=None):
    return _fuse([(_spread_stage(full, kind, new), [0], [1]) if new else None,
                  (_relay_stage(full, kind, relay), [0], [1]) if relay else None,
                  (_spread_stage(full, kind, relay, home=True), [0], [1]) if relay else None,
                  (_forward_stage(full, kind, forward), [0], [1]) if forward else None], [full], [full], {0: 0})


def _small_stage(part):
    def copies(ins, outs, sems):
        send, recv = sems
        x, y, c = _coords()
        me = (x, y, c)
        peers = [(1 - x if k & 4 else x, 1 - y if k & 2 else y, 1 - c if k & 1 else c) for k in range(1, N_DEV)]
        out = [_remote(ins[0], outs[0].at[_lin(me)], send.at[k], recv.at[k], t) for k, t in enumerate(peers)]
        arrive = [_remote(ins[0], outs[0].at[_lin(t)], send.at[k], recv.at[k], t) for k, t in enumerate(peers)]
        return out, arrive

    def start(ins, outs, sems):
        for cp in copies(ins, outs, sems)[0]:
            cp.start()

    def finish(ins, outs, sems):
        out, arrive = copies(ins, outs, sems)
        for cp in arrive:
            cp.wait_recv()
        for cp in out:
            cp.wait_send()

    return _Stage([part], [jax.ShapeDtypeStruct((N_DEV, *part.shape), F32)], {},
                  [pltpu.SemaphoreType.DMA((N_DEV - 1,)), pltpu.SemaphoreType.DMA((N_DEV - 1,))], start, finish)


def _comm_only(name, stages):
    return _call(name, lambda: None, (1,), [], [], [], [], stages=stages, sequential=True)[1]


_GELU_C = math.sqrt(2.0 / math.pi)


def _gelu(x):
    return 0.5 * x * (1.0 + jnp.tanh(_GELU_C * (x + 0.044715 * x * x * x)))


def _gelu_grad(x):
    t = jnp.tanh(_GELU_C * (x + 0.044715 * x * x * x))
    return 0.5 * (1.0 + t) + 0.5 * x * (1.0 - t * t) * (_GELU_C * (1.0 + 3.0 * 0.044715 * x * x))


def _sigmoid(x):
    return 1.0 / (1.0 + jnp.exp(-x))


def _dot(a, b, mode):
    dims = {"nn": (((1,), (0,)), ((), ())), "nt": (((1,), (1,)), ((), ())), "tn": (((0,), (0,)), ((), ()))}[mode]
    return lax.dot_general(a.astype(BF16), b.astype(BF16), dims, preferred_element_type=F32)


def _matmul(name, a, b, mode, outs, epi=None, extras=(), colsums=(), tm=2048, tn=512, tk=2048, b_off=0, n=None, m_off=0, m=None, stages=()):
    if mode == "tn":
        kk, mfull = a.shape
    else:
        mfull, kk = a.shape
    m = mfull if m is None else m
    n = (b.shape[0] if mode == "nt" else b.shape[1]) if n is None else n
    tm, tk = _pick(tm, m, m_off), _pick(tk, kk)
    tn = _pick(tn, n, b_off, *[off for _, _, off in extras])
    boff, moff = b_off // tn, m_off // tm
    nm, nn_, nk = m // tm, n // tn, kk // tk
    col_major = bool(colsums)
    grid = (nn_, nm, nk) if col_major else (nm, nn_, nk)

    def imap(f):
        if col_major:
            return lambda g0, g1, k: f(g1, g0, k)
        return f

    a_spec = (pl.BlockSpec((tk, tm), imap(lambda i, j, k: (k, i + moff))) if mode == "tn"
              else pl.BlockSpec((tm, tk), imap(lambda i, j, k: (i + moff, k))))
    b_spec = (pl.BlockSpec((tn, tk), imap(lambda i, j, k: (j + boff, k))) if mode == "nt"
              else pl.BlockSpec((tk, tn), imap(lambda i, j, k: (k, j + boff))))
    in_specs, operands = [a_spec, b_spec], [a, b]
    for arr, kind, off in extras:
        o = off // tn
        if kind == "mn":
            in_specs.append(pl.BlockSpec((tm, tn), imap(lambda i, j, k, o=o: (i + moff, j + o))))
        else:
            in_specs.append(pl.BlockSpec((1, tn), imap(lambda i, j, k, o=o: (0, j + o))))
        operands.append(arr)
    out_shape = [jax.ShapeDtypeStruct((m, n), dt) for dt in outs] + [jax.ShapeDtypeStruct((1, n), F32) for _ in colsums]
    out_specs = ([pl.BlockSpec((tm, tn), imap(lambda i, j, k: (i, j))) for _ in outs]
                 + [pl.BlockSpec((1, tn), imap(lambda i, j, k: (0, j))) for _ in colsums])
    n_ex, n_out, n_cs = len(extras), len(outs), len(colsums)

    def body(*refs):
        a_ref, b_ref = refs[:2]
        ex_refs = refs[2:2 + n_ex]
        out_refs = refs[2 + n_ex:2 + n_ex + n_out]
        cs_refs = refs[2 + n_ex + n_out:2 + n_ex + n_out + n_cs]
        part = _dot(a_ref[...], b_ref[...], mode)

        def finish(acc):
            res = epi(acc, *[r[...] for r in ex_refs]) if epi is not None else (acc,)
            for r, v in zip(out_refs, res[:n_out]):
                r[...] = v.astype(r.dtype)
            if n_cs:
                @pl.when(pl.program_id(1) == 0)
                def _():
                    for r in cs_refs:
                        r[...] = jnp.zeros_like(r)

                for r, idx in zip(cs_refs, colsums):
                    r[...] += jnp.sum(res[idx], axis=0, keepdims=True)

        if nk == 1:
            finish(part)
        else:
            acc_ref = refs[-1]
            k = pl.program_id(2)

            @pl.when(k == 0)
            def _():
                acc_ref[...] = part

            @pl.when(k > 0)
            def _():
                acc_ref[...] += part

            @pl.when(k == nk - 1)
            def _():
                finish(acc_ref[...])

    sem = ("arbitrary", "arbitrary", "arbitrary") if col_major else ("parallel", "parallel", "arbitrary")
    return _call(name, body, grid, in_specs, out_specs, out_shape, operands,
                 scratch=[pltpu.VMEM((tm, tn), F32)] if nk > 1 else [], sem=sem, stages=stages)


def _project_shards(name, a, b, which, into=None, stages=()):
    m, kk = a.shape
    n = b.shape[1]
    tn = n // N_DEV
    o_spec = pl.BlockSpec((m, tn), lambda s, w_ref: (0, w_ref[s]))

    def body(a_ref, b_ref, *rest):
        rest[-1][...] = _dot(a_ref[...], b_ref[...], "nn")

    (out,), st = _call(name, body, (which.shape[0],),
                       [pl.BlockSpec((m, kk), lambda s, w_ref: (0, 0)), pl.BlockSpec((kk, tn), lambda s, w_ref: (0, w_ref[s]))]
                       + ([ANY] if into is not None else []), [o_spec], [jax.ShapeDtypeStruct((m, n), F32)],
                       [a, b] + ([into] if into is not None else []), sem=("arbitrary",), stages=stages, prefetch=which,
                       alias={2: 0} if into is not None else None)
    return out, st


def _row_spec(tr, c):
    return pl.BlockSpec((tr, c), lambda i: (i, 0))


def _fix_spec(shape):
    return pl.BlockSpec(shape, lambda *_: tuple(0 for _ in shape))


def _cast_bf16(name, x, tr=512):
    r, c = x.shape
    tr = _pick(tr, r)

    def body(x_ref, o_ref):
        o_ref[...] = x_ref[...].astype(BF16)

    return _call(name, body, (r // tr,), [_row_spec(tr, c)], [_row_spec(tr, c)], [jax.ShapeDtypeStruct((r, c), BF16)], [x],
                 sem=("parallel",))[0][0]


def _cast_into_place(name, w, kind, me, tr=512):
    r, c = w.shape
    tr = _pick(tr, r)
    nb = r // tr
    if kind == "col":
        o_spec = pl.BlockSpec((tr, c), lambda i, me_ref: (i, me_ref[0]))
        shape = (r, c * N_DEV)
    else:
        o_spec = pl.BlockSpec((tr, c), lambda i, me_ref: (me_ref[0] * nb + i, 0))
        shape = (r * N_DEV, c)

    def body(x_ref, o_ref):
        o_ref[...] = x_ref[...].astype(BF16)

    return _call(name, body, (nb,), [pl.BlockSpec((tr, c), lambda i, me_ref: (i, 0))], [o_spec], [jax.ShapeDtypeStruct(shape, BF16)], [w],
                 sem=("parallel",), prefetch=me)[0][0]


def _layer_norm_stats(x):
    mean = jnp.mean(x, axis=-1, keepdims=True)
    xc = x - mean
    var = jnp.mean(xc * xc, axis=-1, keepdims=True)
    rstd = lax.rsqrt(var + LN_EPS)
    return xc * rstd, rstd


def _layer_norm_bwd(dxhat, xhat, rstd):
    m1 = jnp.mean(dxhat, axis=-1, keepdims=True)
    m2 = jnp.mean(dxhat * xhat, axis=-1, keepdims=True)
    return rstd * (dxhat - m1 - xhat * m2)


def _ln1_fwd(pre1, g1, b1, tr=256, stages=()):
    s, d = pre1.shape
    tr = _pick(tr, s)

    def body(p_ref, g_ref, b_ref, xh_ref, rs_ref, h_ref):
        xhat, rstd = _layer_norm_stats(p_ref[...])
        xh_ref[...] = xhat
        rs_ref[...] = rstd
        h_ref[...] = (xhat * g_ref[...] + b_ref[...]).astype(BF16)

    return _call("ln1_fwd", body, (s // tr,), [_row_spec(tr, d), _fix_spec((1, d)), _fix_spec((1, d))],
                 [_row_spec(tr, d), _row_spec(tr, 1), _row_spec(tr, d)],
                 [jax.ShapeDtypeStruct((s, d), F32), jax.ShapeDtypeStruct((s, 1), F32), jax.ShapeDtypeStruct((s, d), BF16)],
                 [pre1, g1, b1], sem=("parallel",), stages=stages)


def _ln2_loss_bwd(ff, xhat1, g1, b1, g2, b2, target, tr=256):
    s, d = ff.shape
    tr = _pick(tr, s)

    def body(ff_ref, xh1_ref, g1_ref, b1_ref, g2_ref, b2_ref, t_ref, dp_ref, dpb_ref, dg_ref, db_ref, dbf_ref, loss_ref):
        @pl.when(pl.program_id(0) == 0)
        def _():
            dg_ref[...] = jnp.zeros_like(dg_ref)
            db_ref[...] = jnp.zeros_like(db_ref)
            dbf_ref[...] = jnp.zeros_like(dbf_ref)
            loss_ref[...] = jnp.zeros_like(loss_ref)

        h1 = xh1_ref[...] * g1_ref[...] + b1_ref[...]
        xhat, rstd = _layer_norm_stats(ALPHA * h1 + ff_ref[...])
        err = xhat * g2_ref[...] + b2_ref[...] - t_ref[...]
        row = jnp.mean(err * err, axis=-1, keepdims=True)
        loss_ref[...] += 0.5 * jnp.sum(row, axis=0, keepdims=True)
        dy = err / d
        dg_ref[...] += jnp.sum(dy * xhat, axis=0, keepdims=True)
        db_ref[...] += jnp.sum(dy, axis=0, keepdims=True)
        dpre = _layer_norm_bwd(dy * g2_ref[...], xhat, rstd)
        dbf_ref[...] += jnp.sum(dpre, axis=0, keepdims=True)
        dp_ref[...] = dpre
        dpb_ref[...] = dpre.astype(BF16)

    vec = _fix_spec((1, d))
    return _call("ln2_loss_bwd", body, (s // tr,), [_row_spec(tr, d), _row_spec(tr, d), vec, vec, vec, vec, _row_spec(tr, d)],
                 [_row_spec(tr, d), _row_spec(tr, d), vec, vec, vec, _fix_spec((1, 1))],
                 [jax.ShapeDtypeStruct((s, d), F32), jax.ShapeDtypeStruct((s, d), BF16)]
                 + [jax.ShapeDtypeStruct((1, d), F32)] * 3 + [jax.ShapeDtypeStruct((1, 1), F32)],
                 [ff, xhat1, g1, b1, g2, b2, target])[0]


def _ln1_bwd(dh1, xhat1, rstd1, g1, tr=256, stages=()):
    s, d = dh1.shape
    tr = _pick(tr, s)

    def body(dh_ref, xh_ref, rs_ref, g_ref, dp_ref, dpb_ref, dg_ref, db_ref):
        @pl.when(pl.program_id(0) == 0)
        def _():
            dg_ref[...] = jnp.zeros_like(dg_ref)
            db_ref[...] = jnp.zeros_like(db_ref)

        dh, xhat = dh_ref[...], xh_ref[...]
        dg_ref[...] += jnp.sum(dh * xhat, axis=0, keepdims=True)
        db_ref[...] += jnp.sum(dh, axis=0, keepdims=True)
        dpre = _layer_norm_bwd(dh * g_ref[...], xhat, rs_ref[...])
        dp_ref[...] = dpre
        dpb_ref[...] = dpre.astype(BF16)

    vec = _fix_spec((1, d))
    return _call("ln1_bwd", body, (s // tr,), [_row_spec(tr, d), _row_spec(tr, d), _row_spec(tr, 1), vec],
                 [_row_spec(tr, d), _row_spec(tr, d), vec, vec],
                 [jax.ShapeDtypeStruct((s, d), F32), jax.ShapeDtypeStruct((s, d), BF16)] + [jax.ShapeDtypeStruct((1, d), F32)] * 2,
                 [dh1, xhat1, rstd1, g1], stages=stages)


def _to_perm(x):
    return x.reshape(SEQ // N_SUB, N_SUB, -1).transpose(1, 0, 2).reshape(SEQ, -1)


def _from_perm(x):
    return x.reshape(N_SUB, SEQ // N_SUB, -1).transpose(1, 0, 2).reshape(SEQ, -1)


def _local_index(p):
    rho = np.arange(BLOCK)
    if p == 0:
        return 16 * (rho % 8) + rho // 8
    if p == 1:
        return 4 * (rho % 32) + rho // 32
    return rho


def _tile_view(x, p):
    c = x.shape[1]
    if p == 1:
        return x.reshape(4, 4, BLOCK, c)
    return x.reshape(N_SUB, BLOCK, c)


def _view_shape(c, p):
    return (4, 4, BLOCK, c) if p == 1 else (N_SUB, BLOCK, c)


def _tile_spec(p, width, col, shift=0):
    nblk = SEQ // DILATIONS[p] // BLOCK

    def blk(n):
        return jnp.clip(n + shift, 0, nblk - 1)

    if p == 0:
        return pl.BlockSpec((N_SUB, SUBLANES, width), lambda s, n: (0, blk(n), col))
    if p == 1:
        return pl.BlockSpec((4, None, 32, width), lambda s, n: (0, s, blk(n), col))
    return pl.BlockSpec((None, BLOCK, width), lambda s, n: (s, 0, col))


def _tile_grid(p):
    return ((1, 16), (4, 4), (16, 1))[p]


def _t5_bucket(n):
    max_exact = N_BUCKETS // 2
    nf = np.maximum(n, 1).astype(np.float32)
    large = max_exact + (np.log(nf / np.float32(max_exact)) / np.float32(math.log(MAX_DISTANCE / max_exact))
                         * np.float32(N_BUCKETS - max_exact)).astype(np.int32)
    large = np.minimum(large, N_BUCKETS - 1)
    return np.where(n < max_exact, n, large).astype(np.int32)


def _bucket_tables():
    tabs = np.zeros((3, 2, BLOCK, BLOCK), np.int32)
    for p, d in enumerate(DILATIONS):
        i = _local_index(p)
        diff = i[:, None] - i[None, :]
        tabs[p, 0] = np.where(diff <= 0, _t5_bucket((BLOCK + diff) * d), -1)
        tabs[p, 1] = np.where(diff >= 0, _t5_bucket(np.maximum(diff, 0) * d), -1)
    return tabs


def _bias_expand(rel_bias, buckets):
    nh = N_HEADS

    def body(rb_ref, bk_ref, o_ref):
        for w in range(2):
            bk = bk_ref[0, w]
            for h in range(nh):
                val = jnp.zeros((BLOCK, BLOCK), F32)
                for b in range(N_BUCKETS):
                    val = jnp.where(bk == b, rb_ref[b, h], val)
                o_ref[0, h, w] = jnp.where(bk < 0, NEG_INF, val)

    return _call("bias_expand", body, (3,),
                 [pl.BlockSpec(memory_space=pltpu.SMEM), pl.BlockSpec((1, 2, BLOCK, BLOCK), lambda p: (p, 0, 0, 0))],
                 [pl.BlockSpec((1, nh, 2, BLOCK, BLOCK), lambda p: (p, 0, 0, 0, 0))],
                 [jax.ShapeDtypeStruct((3, nh, 2, BLOCK, BLOCK), F32)], [rel_bias, buckets], sem=("parallel",))[0][0]


def _heads_to_lanes(cols):
    lane = lax.broadcasted_iota(I32, (BLOCK, LANES), 1)
    out = jnp.zeros((BLOCK, LANES), F32)
    for h, c in enumerate(cols):
        out = jnp.where(lane == h, c, out)
    return out


def _attn_fwd(qkv, bias, p, stages=()):
    d_a = _d_a()
    has_prev = SEQ // DILATIONS[p] // BLOCK > 1
    scale = HEAD_DIM ** -0.5
    view = _tile_view(qkv, p)

    width = 2 * BLOCK if has_prev else BLOCK

    def body(q_ref, kc_ref, kp_ref, vc_ref, vp_ref, b_ref, o_ref, l_ref, s_ref, p_ref):
        n = pl.program_id(1)
        q_all = q_ref[...].reshape(BLOCK, d_a).astype(BF16)
        k_all = kc_ref[...].reshape(BLOCK, d_a).astype(BF16)
        v_all = vc_ref[...].reshape(BLOCK, d_a).astype(BF16)
        if has_prev:
            k_all = jnp.concatenate([kp_ref[...].reshape(BLOCK, d_a).astype(BF16), k_all], axis=0)
            v_all = jnp.concatenate([vp_ref[...].reshape(BLOCK, d_a).astype(BF16), v_all], axis=0)
            no_prev = (lax.broadcasted_iota(I32, (BLOCK, width), 1) < BLOCK) & (n == 0)
        for h in range(N_HEADS):
            sl = slice(h * HEAD_DIM, (h + 1) * HEAD_DIM)
            s = _dot(q_all[:, sl], k_all[:, sl], "nt") * scale
            if has_prev:
                s = jnp.where(no_prev, NEG_INF, s + jnp.concatenate([b_ref[0, h, 0], b_ref[0, h, 1]], axis=1))
            else:
                s = s + b_ref[0, h, 1]
            s_ref[h] = s
        dens, lses = [], []
        for h in range(N_HEADS):
            s = s_ref[h]
            m = jnp.max(s, axis=-1, keepdims=True)
            pr = jnp.exp(s - m)
            den = jnp.sum(pr, axis=-1, keepdims=True)
            p_ref[h] = pr.astype(BF16)
            dens.append(den)
            lses.append(m + jnp.log(den))
        for h in range(N_HEADS):
            sl = slice(h * HEAD_DIM, (h + 1) * HEAD_DIM)
            o_ref[..., sl] = (_dot(p_ref[h], v_all[:, sl], "nn") / dens[h]).reshape(*o_ref.shape[:-1], HEAD_DIM)
        l_ref[...] = _heads_to_lanes(lses).reshape(l_ref.shape)

    (o, l), st = _call(
        f"attn_fwd{p}", body, _tile_grid(p),
        [_tile_spec(p, d_a, 0), _tile_spec(p, d_a, 1), _tile_spec(p, d_a, 1, -1), _tile_spec(p, d_a, 2), _tile_spec(p, d_a, 2, -1),
         pl.BlockSpec((1, N_HEADS, 2, BLOCK, BLOCK), lambda s, n: (p, 0, 0, 0, 0))],
        [_tile_spec(p, d_a, 0), _tile_spec(p, LANES, 0)],
        [jax.ShapeDtypeStruct(_view_shape(d_a, p), F32), jax.ShapeDtypeStruct(_view_shape(LANES, p), F32)],
        [view, view, view, view, view, bias], scratch=[pltpu.VMEM((N_HEADS, BLOCK, width), F32), pltpu.VMEM((N_HEADS, BLOCK, width), BF16)],
        sem=("parallel", "parallel"), stages=stages)
    return (o.reshape(SEQ, d_a), l.reshape(SEQ, LANES)), st


def _attn_combine(os_, ls_, tr=256, stages=()):
    d_a = _d_a()
    tr = _pick(tr, SEQ)

    def body(o0, o1, o2, l0, l1, l2, a_ref, ab_ref, lt_ref):
        l = [l0[...], l1[...], l2[...]]
        m = jnp.maximum(jnp.maximum(l[0], l[1]), l[2])
        w = [jnp.exp(x - m) for x in l]
        tot = w[0] + w[1] + w[2]
        lt_ref[...] = m + jnp.log(tot)
        w = [x / tot for x in w]
        for h in range(N_HEADS):
            sl = slice(h * HEAD_DIM, (h + 1) * HEAD_DIM)
            acc = w[0][:, h:h + 1] * o0[:, sl] + w[1][:, h:h + 1] * o1[:, sl] + w[2][:, h:h + 1] * o2[:, sl]
            a_ref[:, sl] = acc
            ab_ref[:, sl] = acc.astype(BF16)

    return _call("attn_combine", body, (SEQ // tr,), [_row_spec(tr, d_a)] * 3 + [_row_spec(tr, LANES)] * 3,
                 [_row_spec(tr, d_a), _row_spec(tr, d_a), _row_spec(tr, LANES)],
                 [jax.ShapeDtypeStruct((SEQ, d_a), F32), jax.ShapeDtypeStruct((SEQ, d_a), BF16), jax.ShapeDtypeStruct((SEQ, LANES), F32)],
                 [*os_, *ls_], sem=("parallel",), stages=stages)


def _attn_delta(dattn, attn, tr=256):
    d_a = _d_a()
    tr = _pick(tr, SEQ)

    def body(d_ref, a_ref, o_ref):
        prod = d_ref[...] * a_ref[...]
        lane = lax.broadcasted_iota(I32, (tr, LANES), 1)
        out = jnp.zeros((tr, LANES), F32)
        for h in range(N_HEADS):
            out = jnp.where(lane == h, jnp.sum(prod[:, h * HEAD_DIM:(h + 1) * HEAD_DIM], axis=-1, keepdims=True), out)
        o_ref[...] = out

    return _call("attn_delta", body, (SEQ // tr,), [_row_spec(tr, d_a)] * 2, [_row_spec(tr, LANES)],
                 [jax.ShapeDtypeStruct((SEQ, LANES), F32)], [dattn, attn], sem=("parallel",))[0][0]


def _attn_bwd(qkv, dattn, lse, delta, bias, p, stages=()):
    d_a = _d_a()
    nblk = SEQ // DILATIONS[p] // BLOCK
    has_next = nblk > 1
    scale = HEAD_DIM ** -0.5
    qv, dov, lv, tv = (_tile_view(x, p) for x in (qkv, dattn, lse, delta))

    rows = 2 * BLOCK if has_next else BLOCK

    def body(q_ref, qn_ref, k_ref, v_ref, do_ref, don_ref, l_ref, ln_ref, t_ref, tn_ref, b_ref, dq_ref, dk_ref, dv_ref, db_ref,
             carry_ref, s_ref, dp_ref, p_ref, ds_ref):
        j = pl.program_id(1)

        @pl.when((pl.program_id(0) == 0) & (j == 0))
        def _():
            db_ref[...] = jnp.zeros_like(db_ref)

        def both(cur, nxt, width, dtype):
            cur = cur[...].reshape(BLOCK, width).astype(dtype)
            return jnp.concatenate([cur, nxt[...].reshape(BLOCK, width).astype(dtype)], axis=0) if has_next else cur

        k_all = k_ref[...].reshape(BLOCK, d_a).astype(BF16)
        v_all = v_ref[...].reshape(BLOCK, d_a).astype(BF16)
        q_all, do_all = both(q_ref, qn_ref, d_a, BF16), both(do_ref, don_ref, d_a, BF16)
        l_all, t_all = both(l_ref, ln_ref, LANES, F32), both(t_ref, tn_ref, LANES, F32)
        if has_next:
            no_next = (lax.broadcasted_iota(I32, (rows, BLOCK), 0) >= BLOCK) & (j == nblk - 1)
        for h in range(N_HEADS):
            sl = slice(h * HEAD_DIM, (h + 1) * HEAD_DIM)
            s = _dot(q_all[:, sl], k_all[:, sl], "nt") * scale
            if has_next:
                s = jnp.where(no_next, NEG_INF, s + jnp.concatenate([b_ref[0, h, 1], b_ref[0, h, 0]], axis=0))
            else:
                s = s + b_ref[0, h, 1]
            s_ref[h] = s
            dp_ref[h] = _dot(do_all[:, sl], v_all[:, sl], "nt")
        for h in range(N_HEADS):
            pr = jnp.exp(s_ref[h] - l_all[:, h:h + 1])
            ds = pr * (dp_ref[h] - t_all[:, h:h + 1])
            db_ref[h, 1] += ds[:BLOCK]
            if has_next:
                db_ref[h, 0] += ds[BLOCK:]
            p_ref[h] = pr.astype(BF16)
            ds_ref[h] = ds.astype(BF16)
        for h in range(N_HEADS):
            sl = slice(h * HEAD_DIM, (h + 1) * HEAD_DIM)
            dq = _dot(ds_ref[h], k_all[:, sl], "nn") * scale
            mine = dq[:BLOCK]
            if has_next:
                mine = mine + jnp.where(j > 0, carry_ref[:, sl], 0.0)
            dq_ref[..., sl] = mine.reshape(*dq_ref.shape[:-1], HEAD_DIM)
            if has_next:
                carry_ref[:, sl] = dq[BLOCK:]
            dk_ref[..., sl] = (_dot(ds_ref[h], q_all[:, sl], "tn") * scale).reshape(*dk_ref.shape[:-1], HEAD_DIM)
            dv_ref[..., sl] = _dot(p_ref[h], do_all[:, sl], "tn").reshape(*dv_ref.shape[:-1], HEAD_DIM)

    def big(col, shift=0):
        return _tile_spec(p, d_a, col, shift)

    def small(shift=0):
        return _tile_spec(p, LANES, 0, shift)

    (dq, dk, dv, dbias), st = _call(
        f"attn_bwd{p}", body, _tile_grid(p),
        [big(0), big(0, 1), big(1), big(2), big(0), big(0, 1), small(), small(1), small(), small(1),
         pl.BlockSpec((1, N_HEADS, 2, BLOCK, BLOCK), lambda s, n: (p, 0, 0, 0, 0))],
        [big(0), big(0), big(0), pl.BlockSpec((N_HEADS, 2, BLOCK, BLOCK), lambda s, n: (0, 0, 0, 0))],
        [jax.ShapeDtypeStruct(_view_shape(d_a, p), F32)] * 3 + [jax.ShapeDtypeStruct((N_HEADS, 2, BLOCK, BLOCK), F32)],
        [qv, qv, qv, qv, dov, dov, lv, lv, tv, tv, bias],
        scratch=[pltpu.VMEM((BLOCK, d_a), F32), pltpu.VMEM((N_HEADS, rows, BLOCK), F32), pltpu.VMEM((N_HEADS, rows, BLOCK), F32),
                 pltpu.VMEM((N_HEADS, rows, BLOCK), BF16), pltpu.VMEM((N_HEADS, rows, BLOCK), BF16)], stages=stages)
    return (dq.reshape(SEQ, d_a), dk.reshape(SEQ, d_a), dv.reshape(SEQ, d_a), dbias), st


def _rel_bias_grad(dbias, buckets):
    nh = N_HEADS

    def body(d0, d1, d2, bk_ref, o_ref, t_ref):
        ds = (d0, d1, d2)

        def per_bucket(b, carry):
            for h in range(nh):
                acc = jnp.zeros((BLOCK, BLOCK), F32)
                for p in range(3):
                    for w in range(2):
                        acc = acc + jnp.where(bk_ref[p, w] == b, ds[p][h, w], 0.0)
                t_ref[pl.ds(b * nh + h, 1), :] = jnp.sum(acc, axis=0, keepdims=True)
            return carry

        lax.fori_loop(0, N_BUCKETS, per_bucket, 0)
        o_ref[...] = jnp.sum(t_ref[...], axis=-1, keepdims=True)

    return _call("rel_bias_grad", body, (1,), [_fix_spec((nh, 2, BLOCK, BLOCK))] * 3 + [_fix_spec((3, 2, BLOCK, BLOCK))],
                 [_fix_spec((N_BUCKETS * nh, 1))], [jax.ShapeDtypeStruct((N_BUCKETS * nh, 1), F32)], [*dbias, buckets],
                 scratch=[pltpu.VMEM((N_BUCKETS * nh, LANES), F32)])[0][0]


def _gmlp_fwd(rest, col0, gain, bias, ws, bs, causal, stages=()):
    d_b = _d_b()

    def body(u_ref, v_ref, g_ref, b_ref, ws_ref, bs_ref, c_ref, o_ref):
        u = u_ref[...].reshape(BLOCK, d_b)
        xhat, _ = _layer_norm_stats(_gelu(v_ref[...].reshape(BLOCK, d_b)))
        vn = (xhat * g_ref[...] + b_ref[...]).astype(BF16)
        outs = []
        for g in range(N_GROUPS):
            sl = slice(g * BLOCK, (g + 1) * BLOCK)
            w = jnp.where(c_ref[...] > 0, ws_ref[g], 0.0)
            z = _dot(w, vn[:, sl], "nn") + bs_ref[:, g:g + 1]
            outs.append(_gelu(u[:, sl]) * z)
        o_ref[...] = jnp.concatenate(outs, axis=-1).reshape(o_ref.shape)

    (out,), st = _call(
        "gmlp_fwd", body, (1, SEQ // BLOCK),
        [_tile_spec(0, d_b, col0), _tile_spec(0, d_b, col0 + 1), _fix_spec((1, d_b)), _fix_spec((1, d_b)),
         _fix_spec((N_GROUPS, BLOCK, BLOCK)), _fix_spec((BLOCK, N_GROUPS)), _fix_spec((BLOCK, BLOCK))],
        [_tile_spec(0, d_b, 0)], [jax.ShapeDtypeStruct(_view_shape(d_b, 0), F32)],
        [_tile_view(rest, 0), _tile_view(rest, 0), gain, bias, ws, bs, causal], sem=("parallel", "parallel"), stages=stages)
    return out.reshape(SEQ, d_b), st


def _gmlp_bwd(rest, col0, dgmlp, gain, bias, ws, bs, causal, stages=()):
    d_b = _d_b()
    nchunk = SEQ // BLOCK

    def body(u_ref, v_ref, dg_ref, g_ref, b_ref, ws_ref, bs_ref, c_ref, du_ref, dv_ref, dws_ref, dbs_ref, dgain_ref, dbias_ref):
        c = pl.program_id(1)

        @pl.when(c == 0)
        def _():
            dws_ref[...] = jnp.zeros_like(dws_ref)
            dbs_ref[...] = jnp.zeros_like(dbs_ref)
            dgain_ref[...] = jnp.zeros_like(dgain_ref)
            dbias_ref[...] = jnp.zeros_like(dbias_ref)

        u = u_ref[...].reshape(BLOCK, d_b)
        v = v_ref[...].reshape(BLOCK, d_b)
        dgm = dg_ref[...].reshape(BLOCK, d_b)
        xhat, rstd = _layer_norm_stats(_gelu(v))
        vn = (xhat * g_ref[...] + b_ref[...]).astype(BF16)
        lane = lax.broadcasted_iota(I32, (BLOCK, LANES), 1)
        dus, dvns = [], []
        dbs = dbs_ref[...]
        for g in range(N_GROUPS):
            sl = slice(g * BLOCK, (g + 1) * BLOCK)
            w = jnp.where(c_ref[...] > 0, ws_ref[g], 0.0).astype(BF16)
            z = _dot(w, vn[:, sl], "nn") + bs_ref[:, g:g + 1]
            dz = dgm[:, sl] * _gelu(u[:, sl])
            dus.append(dgm[:, sl] * z * _gelu_grad(u[:, sl]))
            dws_ref[g] += _dot(dz, vn[:, sl], "nt")
            dbs = dbs + jnp.where(lane == g, jnp.sum(dz, axis=-1, keepdims=True), 0.0)
            dvns.append(_dot(w, dz, "tn"))
        dbs_ref[...] = dbs
        dvn = jnp.concatenate(dvns, axis=-1)
        dgain_ref[...] += jnp.sum(dvn * xhat, axis=0, keepdims=True)
        dbias_ref[...] += jnp.sum(dvn, axis=0, keepdims=True)
        dvg = _layer_norm_bwd(dvn * g_ref[...], xhat, rstd)
        du_ref[...] = jnp.concatenate(dus, axis=-1).reshape(du_ref.shape)
        dv_ref[...] = (dvg * _gelu_grad(v)).reshape(dv_ref.shape)

        @pl.when(c == nchunk - 1)
        def _():
            for g in range(N_GROUPS):
                dws_ref[g] = jnp.where(c_ref[...] > 0, dws_ref[g], 0.0)

    (du, dv, dws, dbs, dgain, dbias), st = _call(
        "gmlp_bwd", body, (1, nchunk),
        [_tile_spec(0, d_b, col0), _tile_spec(0, d_b, col0 + 1), _tile_spec(0, d_b, 0), _fix_spec((1, d_b)), _fix_spec((1, d_b)),
         _fix_spec((N_GROUPS, BLOCK, BLOCK)), _fix_spec((BLOCK, N_GROUPS)), _fix_spec((BLOCK, BLOCK))],
        [_tile_spec(0, d_b, 0), _tile_spec(0, d_b, 0), _fix_spec((N_GROUPS, BLOCK, BLOCK)), _fix_spec((BLOCK, LANES)),
         _fix_spec((1, d_b)), _fix_spec((1, d_b))],
        [jax.ShapeDtypeStruct(_view_shape(d_b, 0), F32)] * 2
        + [jax.ShapeDtypeStruct((N_GROUPS, BLOCK, BLOCK), F32), jax.ShapeDtypeStruct((BLOCK, LANES), F32)]
        + [jax.ShapeDtypeStruct((1, d_b), F32)] * 2,
        [_tile_view(rest, 0), _tile_view(rest, 0), _tile_view(dgmlp, 0), gain, bias, ws, bs, causal], stages=stages)
    return (du.reshape(SEQ, d_b), dv.reshape(SEQ, d_b), dws, dbs, dgain, dbias), st


def _assemble_dproj(dqkv, du, dv, dga, dgb, tr=128, stages=()):
    d_a, d_b, d_in = _d_a(), _d_b(), _d_in()
    tr = _pick(tr, SEQ)

    def body(*refs):
        att, (du_ref, dv_ref, dga_ref, dgb_ref, o_ref) = refs[:9], refs[9:]
        for i in range(3):
            o_ref[:, i * d_a:(i + 1) * d_a] = (att[3 * i][...] + att[3 * i + 1][...] + att[3 * i + 2][...]).astype(BF16)
        o_ref[:, 3 * d_a:3 * d_a + d_b] = du_ref[...].astype(BF16)
        o_ref[:, 3 * d_a + d_b:3 * d_a + 2 * d_b] = dv_ref[...].astype(BF16)
        o_ref[:, 3 * d_a + 2 * d_b:3 * d_a + 2 * d_b + D_MODEL] = dga_ref[...]
        o_ref[:, 3 * d_a + 2 * d_b + D_MODEL:] = dgb_ref[...]

    return _call("assemble_dproj", body, (SEQ // tr,), [_row_spec(tr, d_a)] * 9 + [_row_spec(tr, d_b)] * 2 + [_row_spec(tr, D_MODEL)] * 2,
                 [_row_spec(tr, d_in)], [jax.ShapeDtypeStruct((SEQ, d_in), BF16)], [*dqkv, du, dv, dga, dgb], sem=("parallel",), stages=stages)


def _dw(name, a, b, kind, core, mine, add=None, tn=1152, stages=()):
    s, m = a.shape
    n = b.shape[1]
    rs, cs = (m, n // N_DEV) if kind == "col" else (m // N_DEV, n)
    tn = _pick(tn if kind == "col" else 512, cs)
    nj = cs // tn

    def shard(q, c_ref):
        return 2 * q + (c_ref[0] if mine else 1 - c_ref[0])

    if kind == "col":
        a_spec = pl.BlockSpec((s, m), lambda q, j, c_ref: (0, 0))
        b_spec = pl.BlockSpec((s, tn), lambda q, j, c_ref: (0, shard(q, c_ref) * nj + j))
    else:
        a_spec = pl.BlockSpec((s, rs), lambda q, j, c_ref: (0, shard(q, c_ref)))
        b_spec = pl.BlockSpec((s, tn), lambda q, j, c_ref: (0, j))
    o_spec = pl.BlockSpec((None, rs, tn), lambda q, j, c_ref: (q, 0, j))

    def body(a_ref, b_ref, *rest):
        acc = _dot(a_ref[...], b_ref[...], "tn")
        if add is not None:
            acc = acc + rest[0][...].astype(F32)
        rest[-1][...] = acc.astype(BF16)

    (out,), st = _call(name, body, (N_CHIPS, nj), [a_spec, b_spec] + ([o_spec] if add is not None else []), [o_spec],
                       [jax.ShapeDtypeStruct((N_CHIPS, rs, cs), BF16)], [a, b] + ([add] if add is not None else []),
                       sem=("parallel", "parallel"), stages=stages, prefetch=core)
    return out, st


def _adamw(w, g, m, v):
    m = ADAM_B1 * m + (1.0 - ADAM_B1) * g
    v = ADAM_B2 * v + (1.0 - ADAM_B2) * (g * g)
    m_hat = m / (1.0 - ADAM_B1 ** ADAM_STEP)
    v_hat = v / (1.0 - ADAM_B2 ** ADAM_STEP)
    delta = -ADAM_LR * (m_hat / (jnp.sqrt(v_hat) + ADAM_EPS) + ADAM_WD * w)
    return delta, m, v


def _adam_shard(name, pair, chip_sums, chip, w, m, v, tr=256, rows=None, into=None, stages=()):
    rs, cs = w.shape
    lo, cnt = rows or (0, rs)
    tr = _pick(tr, cnt, lo)
    first = lo // tr

    def body(chip_ref, own_ref, *refs):
        slots, (w_ref, m_ref, v_ref), (g_ref, d_ref, nm_ref, nv_ref) = refs[:N_CHIPS], refs[N_CHIPS:N_CHIPS + 3], refs[-4:]
        g = None
        for q in range(N_CHIPS):
            term = jnp.where(chip_ref[0] == q, own_ref[...], slots[q][...]).astype(F32)
            g = term if g is None else g + term
        d, nm, nv = _adamw(w_ref[...], g, m_ref[...], v_ref[...])
        g_ref[...], d_ref[...], nm_ref[...], nv_ref[...] = g, d, nm, nv

    def slot(q):
        return pl.BlockSpec((None, tr, cs), lambda i, c_ref: (jnp.where(c_ref[0] == q, (q + 1) % N_CHIPS, q), i + first, 0))

    spec = pl.BlockSpec((tr, cs), lambda i, c_ref: (i + first, 0))
    n_in = 1 + N_CHIPS + 3
    return _call(name, body, (cnt // tr,),
                 [pl.BlockSpec((None, tr, cs), lambda i, c_ref: (c_ref[0], i + first, 0))] + [slot(q) for q in range(N_CHIPS)]
                 + [spec, spec, spec] + ([ANY] * 4 if into is not None else []),
                 [spec] * 4, [jax.ShapeDtypeStruct((rs, cs), F32)] * 4, [pair] + [chip_sums] * N_CHIPS + [w, m, v] + list(into or ()),
                 sem=("parallel",), stages=stages, prefetch=chip, shown=True,
                 alias={n_in + k: k for k in range(4)} if into is not None else None)


def _adam_small(part, parts, me, w, m, v):
    rows = w.shape[0]

    def body(me_ref, own_ref, *refs):
        slots, (w_ref, m_ref, v_ref, g_ref, d_ref, nm_ref, nv_ref) = refs[:N_DEV], refs[N_DEV:]
        g = None
        for j in range(N_DEV):
            term = jnp.where(me_ref[0] == j, own_ref[...], slots[j][...])
            g = term if g is None else g + term
        d, nm, nv = _adamw(w_ref[...], g, m_ref[...], v_ref[...])
        g_ref[...], d_ref[...], nm_ref[...], nv_ref[...] = g, d, nm, nv

    def slot(j):
        return pl.BlockSpec((None, rows, LANES), lambda i, me_ref: (jnp.where(me_ref[0] == j, (j + 1) % N_DEV, j), 0, 0))

    spec = _fix_spec((rows, LANES))
    return _call("adam_small", body, (1,), [spec] + [slot(j) for j in range(N_DEV)] + [spec, spec, spec], [spec] * 4,
                 [jax.ShapeDtypeStruct((rows, LANES), F32)] * 4, [part] + [parts] * N_DEV + [w, m, v], prefetch=me, shown=True)[0]


def _small_sizes():
    d_b = _d_b()
    return (("loss", 1), ("rel_bias", N_BUCKETS * N_HEADS), ("ln_v_gain", d_b), ("ln_v_bias", d_b),
            ("w_spatial", N_GROUPS * BLOCK * BLOCK), ("b_spatial", N_GROUPS * BLOCK), ("ln1_gain", D_MODEL), ("ln1_bias", D_MODEL),
            ("b_ff1", D_FF), ("b_ff2", D_MODEL), ("ln2_gain", D_MODEL), ("ln2_bias", D_MODEL))


def _pack(vals):
    pieces = []
    for name, size in _small_sizes():
        flat = vals[name].reshape(-1).astype(F32)
        padded = -(-size // (SUBLANES * LANES)) * SUBLANES * LANES
        pieces.append(jnp.pad(flat, (0, padded - size)).reshape(-1, LANES))
    return jnp.concatenate(pieces, axis=0)


def _unpack(buf):
    out, row = {}, 0
    for name, size in _small_sizes():
        rows = -(-size // (SUBLANES * LANES)) * SUBLANES
        out[name] = buf[row:row + rows].reshape(-1)[:size]
        row += rows
    return out


def kernel(x, w_in, rel_bias, ln_v_gain, ln_v_bias, w_spatial, b_spatial, w_proj_a, w_proj_b, w_out, ln1_gain, ln1_bias, w_ff1, b_ff1, w_ff2, b_ff2, ln2_gain, ln2_bias, loss_target, m_w_in, m_rel_bias, m_ln_v_gain, m_ln_v_bias, m_w_spatial, m_b_spatial, m_w_proj_a, m_w_proj_b, m_w_out, m_ln1_gain, m_ln1_bias, m_w_ff1, m_b_ff1, m_w_ff2, m_b_ff2, m_ln2_gain, m_ln2_bias, v_w_in, v_rel_bias, v_ln_v_gain, v_ln_v_bias, v_w_spatial, v_b_spatial, v_w_proj_a, v_w_proj_b, v_w_out, v_ln1_gain, v_ln1_bias, v_w_ff1, v_b_ff1, v_w_ff2, v_b_ff2, v_ln2_gain, v_ln2_bias):
    d_a, d_b, d_in = _d_a(), _d_b(), _d_in()
    weights = dict(w_in=w_in, rel_bias=rel_bias, ln_v_gain=ln_v_gain, ln_v_bias=ln_v_bias, w_spatial=w_spatial, b_spatial=b_spatial,
                   w_proj_a=w_proj_a, w_proj_b=w_proj_b, w_out=w_out, ln1_gain=ln1_gain, ln1_bias=ln1_bias, w_ff1=w_ff1, b_ff1=b_ff1,
                   w_ff2=w_ff2, b_ff2=b_ff2, ln2_gain=ln2_gain, ln2_bias=ln2_bias)
    mom1 = dict(w_in=m_w_in, rel_bias=m_rel_bias, ln_v_gain=m_ln_v_gain, ln_v_bias=m_ln_v_bias, w_spatial=m_w_spatial,
                b_spatial=m_b_spatial, w_proj_a=m_w_proj_a, w_proj_b=m_w_proj_b, w_out=m_w_out, ln1_gain=m_ln1_gain,
                ln1_bias=m_ln1_bias, w_ff1=m_w_ff1, b_ff1=m_b_ff1, w_ff2=m_w_ff2, b_ff2=m_b_ff2, ln2_gain=m_ln2_gain, ln2_bias=m_ln2_bias)
    mom2 = dict(w_in=v_w_in, rel_bias=v_rel_bias, ln_v_gain=v_ln_v_gain, ln_v_bias=v_ln_v_bias, w_spatial=v_w_spatial,
                b_spatial=v_b_spatial, w_proj_a=v_w_proj_a, w_proj_b=v_w_proj_b, w_out=v_w_out, ln1_gain=v_ln1_gain,
                ln1_bias=v_ln1_bias, w_ff1=v_w_ff1, b_ff1=v_b_ff1, w_ff2=v_w_ff2, b_ff2=v_b_ff2, ln2_gain=v_ln2_gain, ln2_bias=v_ln2_bias)

    mx, my, mc = _coords()
    me = (4 * mx + 2 * my + mc).astype(I32).reshape(1)
    chip = (2 * mx + my).astype(I32).reshape(1)
    full = {n: _cast_into_place(f"cast_{n}", weights[n][0], KINDS[n], me) for n in KINDS}
    sent = {n: (0, 0, 0) for n in KINDS}

    def keep(table, n):
        def store(outs):
            table[n] = outs[0]
        return store

    def gathering(**new):
        stages = []
        for n in KINDS:
            out, relayed, passed = sent[n]
            units = new.get(n, 0)
            if units or relayed < out or passed < relayed:
                st = _gather_stage(full[n], KINDS[n], (out, units) if units else None,
                                   (relayed, out - relayed) if relayed < out else None, (passed, relayed - passed) if passed < relayed else None)
                st.store = keep(full, n)
                sent[n] = (out + units, out, relayed)
                stages.append(st)
        return stages

    def settle(stages, outs):
        for st, o in zip(stages, outs):
            st.store(o)

    def alone(name, stages):
        settle(stages, _comm_only(name, stages))

    def here(n):
        assert sent[n] == (16, 16, 16), (n, sent[n])
        return full[n]

    alone("gather_w_in_near", gathering(w_in=16))
    alone("gather_w_in_relay", gathering())
    alone("gather_w_in_sibling", gathering())

    xs = _to_perm(x[0])
    target = _to_perm(loss_target[0])
    xb = _cast_bf16("cast_x", xs)
    g8 = BLOCK // N_SUB
    ws_t = w_spatial[0].reshape(N_GROUPS, g8, N_SUB, g8, N_SUB).transpose(0, 2, 1, 4, 3).reshape(N_GROUPS, BLOCK, BLOCK)
    bs_t = b_spatial[0].reshape(N_GROUPS, g8, N_SUB).transpose(2, 1, 0).reshape(BLOCK, N_GROUPS)
    idx = _local_index(0)
    causal = jnp.asarray((idx[:, None] >= idx[None, :]).astype(np.float32))
    buckets = jnp.asarray(_bucket_tables())
    bias = _bias_expand(rel_bias, buckets)

    hosted = gathering(w_proj_a=16, w_proj_b=16, w_ff1=1)
    (qkv,), st = _matmul("proj_qkv", xb, here("w_in"), "nn", [F32], n=3 * d_a, stages=hosted)
    settle(hosted, st)
    hosted = gathering(w_out=16, w_ff1=4)
    (rest,), st = _matmul("proj_rest", xb, here("w_in"), "nn", [F32], b_off=3 * d_a, n=d_in - 3 * d_a, stages=hosted)
    settle(hosted, st)
    fwd = []
    for p in range(3):
        hosted = gathering(w_ff1=(3, 5, 3)[p])
        res, st = _attn_fwd(qkv, bias, p, stages=hosted)
        settle(hosted, st)
        fwd.append(res)
    hosted = gathering(w_ff2=1)
    (attn, attn_b, lse), st = _attn_combine([o for o, _ in fwd], [l for _, l in fwd], stages=hosted)
    settle(hosted, st)
    hosted = gathering(w_ff2=2)
    gmlp, st = _gmlp_fwd(rest, 0, ln_v_gain, ln_v_bias, ws_t, bs_t, causal, stages=hosted)
    settle(hosted, st)
    hosted = gathering(w_ff2=3)
    (ya,), st = _matmul("proj_a", attn_b, here("w_proj_a"), "nn", [F32], stages=hosted)
    settle(hosted, st)
    gate_a, gate_b = 2 * d_b, 2 * d_b + D_MODEL

    def merge(acc, ya_, ga, gb):
        return acc, _sigmoid(ga) * ya_ + _sigmoid(gb) * acc

    hosted = gathering(w_ff2=5)
    (yb, merged), st = _matmul("proj_b_merge", gmlp, here("w_proj_b"), "nn", [F32, BF16], merge,
                               [(ya, "mn", 0), (rest, "mn", gate_a), (rest, "mn", gate_b)], tn=256, stages=hosted)
    settle(hosted, st)
    hosted = gathering(w_ff2=3)
    (pre1,), st = _matmul("out_proj", merged, here("w_out"), "nn", [F32], lambda acc, x_: (ALPHA * x_ + acc,), [(xs, "mn", 0)], stages=hosted)
    settle(hosted, st)
    hosted = gathering(w_ff2=2)
    (xhat1, rstd1, h1b), st = _ln1_fwd(pre1, ln1_gain, ln1_bias, stages=hosted)
    settle(hosted, st)

    def relu2(acc, b_):
        r = jnp.maximum(acc + b_, 0.0)
        return r, r * r

    hosted = gathering()
    (relu, fb), st = _matmul("ff1", h1b, here("w_ff1"), "nn", [F32, BF16], relu2, [(b_ff1, "row", 0)], stages=hosted)
    settle(hosted, st)
    alone("gather_w_ff2_sibling", gathering())
    (ff,), _ = _matmul("ff2", fb, here("w_ff2"), "nn", [F32], lambda acc, b_: (acc + b_,), [(b_ff2, "row", 0)], tn=1024, tk=1024)

    core = lax.axis_index("c").astype(I32).reshape(1)
    factors, theirs, sib, pair, chips, reduced = {}, {}, {}, {}, {}, {}

    def grad_for_sibling(n, a, b, stages=()):
        factors[n] = (a, b)
        theirs[n], outs = _dw(f"dw_{n}_sibling", a, b, KINDS[n], core, False, stages=stages)
        settle(stages, outs)

    def to_sibling(n):
        st = _to_sibling_stage(theirs[n])
        st.store = keep(sib, n)
        return st

    def grad_own(n, stages=()):
        pair[n], outs = _dw(f"dw_{n}_own", *factors[n], KINDS[n], core, True, add=sib[n], stages=stages)
        settle(stages, outs)
        chips[n] = lax.empty(pair[n].shape, BF16)
        reduced[n] = 0

    def reducing(**new):
        stages = []
        for n, units in new.items():
            st = _to_chips_stage(pair[n], chips[n], (reduced[n], units))
            st.store = keep(chips, n)
            reduced[n] += units
            stages.append(st)
        return stages

    def summed(n):
        assert reduced[n] == 16, (n, reduced[n])
        return chips[n]

    dpre2, dpre2b, g_ln2_gain, g_ln2_bias, g_b_ff2, loss_part = _ln2_loss_bwd(ff, xhat1, ln1_gain, ln1_bias, ln2_gain, ln2_bias, target)
    grad_for_sibling("w_ff2", fb, dpre2b)

    def relu2_bwd(acc, r):
        da = acc * (2.0 * r)
        return da, da

    hosted = [to_sibling("w_ff2")]
    (dab, g_b_ff1), st = _matmul("d_ff1", dpre2b, here("w_ff2"), "nt", [BF16], relu2_bwd, [(relu, "mn", 0)], colsums=(1,), stages=hosted)
    settle(hosted, st)
    grad_own("w_ff2")
    grad_for_sibling("w_ff1", h1b, dab, reducing(w_ff2=3))
    hosted = reducing(w_ff2=8) + [to_sibling("w_ff1")]
    (dh1,), st = _matmul("d_h1", dab, here("w_ff1"), "nt", [F32], lambda acc, d_: (acc + ALPHA * d_,), [(dpre2, "mn", 0)], stages=hosted)
    settle(hosted, st)
    grad_own("w_ff1", reducing(w_ff2=4))
    hosted = reducing(w_ff2=1)
    (dpre1, dpre1b, g_ln1_gain, g_ln1_bias), st = _ln1_bwd(dh1, xhat1, rstd1, ln1_gain, stages=hosted)
    settle(hosted, st)
    grad_for_sibling("w_out", merged, dpre1b, reducing(w_ff1=1))

    def merge_bwd(acc, ga, gb, ya_, yb_):
        sa, sb = _sigmoid(ga), _sigmoid(gb)
        return acc * sa, acc * sb, acc * ya_ * (sa * (1.0 - sa)), acc * yb_ * (sb * (1.0 - sb))

    hosted = reducing(w_ff1=6) + [to_sibling("w_out")]
    (dya, dyb, dga, dgb), st = _matmul("d_merge", dpre1b, here("w_out"), "nt", [BF16] * 4, merge_bwd,
                                       [(rest, "mn", gate_a), (rest, "mn", gate_b), (ya, "mn", 0), (yb, "mn", 0)], tn=256, stages=hosted)
    settle(hosted, st)
    grad_own("w_out", reducing(w_ff1=1))
    grad_for_sibling("w_proj_a", attn_b, dya)
    grad_for_sibling("w_proj_b", gmlp, dyb)
    hosted = reducing(w_ff1=1) + [to_sibling("w_proj_a"), to_sibling("w_proj_b")]
    (dattn,), st = _matmul("d_attn", dya, here("w_proj_a"), "nt", [F32], stages=hosted)
    settle(hosted, st)
    grad_own("w_proj_a")
    grad_own("w_proj_b")
    hosted = reducing(w_ff1=1)
    (dgmlp,), st = _matmul("d_gmlp", dyb, here("w_proj_b"), "nt", [F32], stages=hosted)
    settle(hosted, st)
    hosted = reducing(w_ff1=2)
    (du, dvb, dws_t, dbs_t, g_lnv_gain, g_lnv_bias), st = _gmlp_bwd(rest, 0, dgmlp, ln_v_gain, ln_v_bias, ws_t, bs_t, causal, stages=hosted)
    settle(hosted, st)
    delta = _attn_delta(dattn, attn)
    bwd = []
    for p in range(3):
        hosted = reducing(**({"w_ff1": 3}, {"w_ff1": 1, "w_out": 8}, {"w_out": 8, "w_proj_a": 8})[p])
        res, st = _attn_bwd(qkv, dattn, lse, delta, bias, p, stages=hosted)
        settle(hosted, st)
        bwd.append(res)
    g_rel_bias = _rel_bias_grad([b[3] for b in bwd], buckets)
    hosted = reducing(w_proj_a=8, w_proj_b=16)
    (dproj,), st = _assemble_dproj([b[i] for i in range(3) for b in bwd], du, dvb, dga, dgb, stages=hosted)
    settle(hosted, st)

    g_w_spatial = dws_t.reshape(N_GROUPS, N_SUB, g8, N_SUB, g8).transpose(0, 2, 1, 4, 3)
    g_b_spatial = dbs_t[:, :N_GROUPS].reshape(N_SUB, g8, N_GROUPS).transpose(2, 1, 0)
    part = _pack(dict(loss=loss_part, rel_bias=g_rel_bias, ln_v_gain=g_lnv_gain, ln_v_bias=g_lnv_bias, w_spatial=g_w_spatial,
                      b_spatial=g_b_spatial, ln1_gain=g_ln1_gain, ln1_bias=g_ln1_bias, b_ff1=g_b_ff1, b_ff2=g_b_ff2,
                      ln2_gain=g_ln2_gain, ln2_bias=g_ln2_bias))
    small = _small_stage(part)
    small.store = keep(sib, "small")
    grad_for_sibling("w_in", xb, dproj, [small])
    parts = sib["small"]
    first = SEQ // 4

    def add_residual(acc, d_):
        return (acc + ALPHA * d_,)

    hosted = [to_sibling("w_in")]
    (dx0,), st = _matmul("d_x0", dproj, here("w_in"), "nt", [F32], add_residual, [(dpre1, "mn", 0)], tm=512, tn=1024, tk=3072, m=first,
                         stages=hosted)
    settle(hosted, st)
    grad_own("w_in")
    hosted = reducing(w_in=8)
    (dx1,), st = _matmul("d_x1", dproj, here("w_in"), "nt", [F32], add_residual, [(dpre1, "mn", 0)], tm=512, tn=1024, tk=3072,
                         m_off=first, m=SEQ - first, stages=hosted)
    settle(hosted, st)
    grad_x = _from_perm(jnp.concatenate([dx0, dx1], axis=0))[None]

    early, pair["w_in_late"], chips["w_in_late"] = chips["w_in"], pair["w_in"], lax.empty(chips["w_in"].shape, BF16)
    reduced["w_in_late"] = 8
    w_in_rows = weights["w_in"].shape[1]
    out_g, out_d, out_m, out_v = {}, {}, {}, {}
    done = None
    for n, units in (("w_in", 2), ("w_ff2", 2), ("w_ff1", 2), ("w_out", 1), ("w_proj_a", 1), ("w_proj_b", 0), ("w_in_late", 0)):
        hosted = reducing(w_in_late=units) if units else []
        if n == "w_in":
            done, st = _adam_shard("adam_w_in_early", pair[n], early, chip, weights[n][0], mom1[n][0], mom2[n][0],
                                   rows=(0, w_in_rows // 2), stages=hosted)
        elif n == "w_in_late":
            done, st = _adam_shard("adam_w_in_late", pair[n], summed(n), chip, weights["w_in"][0], mom1["w_in"][0], mom2["w_in"][0],
                                   rows=(w_in_rows // 2, w_in_rows // 2), into=done, stages=hosted)
            out_g["w_in"], out_d["w_in"], out_m["w_in"], out_v["w_in"] = (t[None] for t in done)
        else:
            (g, d, nm, nv), st = _adam_shard(f"adam_{n}", pair[n], summed(n), chip, weights[n][0], mom1[n][0], mom2[n][0], stages=hosted)
            out_g[n], out_d[n], out_m[n], out_v[n] = g[None], d[None], nm[None], nv[None]
        settle(hosted, st)

    zero = jnp.zeros((1,), F32)
    sg, sd, sm, sv = (_unpack(b) for b in _adam_small(
        part, parts, me, _pack({**weights, "loss": zero}), _pack({**mom1, "loss": zero}), _pack({**mom2, "loss": zero})))
    for n in WEIGHT_ORDER:
        if n not in KINDS:
            shape = weights[n].shape
            out_g[n], out_d[n], out_m[n], out_v[n] = (t[n].reshape(shape) for t in (sg, sd, sm, sv))
    loss = sg["loss"].reshape(())
    return (loss, grad_x, *[out_g[n] for n in WEIGHT_ORDER], *[out_d[n] for n in WEIGHT_ORDER],
            *[out_m[n] for n in WEIGHT_ORDER], *[out_v[n] for n in WEIGHT_ORDER])
```

```python
import math

import jax
import jax.numpy as jnp
import numpy as np
from jax import lax
from jax.experimental import pallas as pl
from jax.experimental.pallas import tpu as pltpu

F32 = jnp.float32
BF16 = jnp.bfloat16
I32 = jnp.int32

SEQ = 2048
D_MODEL = 2048
HEAD_DIM = 128
N_HEADS = 8
N_GROUPS = 8
D_FF = 4 * D_MODEL
BLOCK = 128
DILATIONS = (1, 4, 16)
N_BUCKETS = 32
MAX_DISTANCE = 2048
ALPHA = 2.0 ** 0.25
LN_EPS = 1e-5
NEG_INF = -1e30
N_DEV = 8
N_CHIPS = 4
N_SUB = 16
ADAM_LR, ADAM_B1, ADAM_B2, ADAM_EPS, ADAM_WD, ADAM_STEP = 0.001, 0.9, 0.999, 1e-08, 0.01, 10
LANES = 128
SUBLANES = 8
VMEM_LIMIT = 56 * 1024 * 1024
MESH = pl.DeviceIdType.MESH
ANY = pl.BlockSpec(memory_space=pl.ANY)
WEIGHT_ORDER = ("w_in", "rel_bias", "ln_v_gain", "ln_v_bias", "w_spatial", "b_spatial", "w_proj_a", "w_proj_b", "w_out",
                "ln1_gain", "ln1_bias", "w_ff1", "b_ff1", "w_ff2", "b_ff2", "ln2_gain", "ln2_bias")
KINDS = {"w_in": "col", "w_proj_a": "col", "w_proj_b": "col", "w_out": "row", "w_ff1": "col", "w_ff2": "row"}


def _d_a():
    return N_HEADS * HEAD_DIM


def _d_b():
    return N_GROUPS * BLOCK


def _d_in():
    return 3 * _d_a() + 2 * _d_b() + 2 * D_MODEL


def _pick(t, n, *others):
    if n <= t and all(o % n == 0 for o in others):
        return n
    for c in range(min(t, n) // LANES * LANES, 0, -LANES):
        if n % c == 0 and all(o % c == 0 for o in others):
            return c
    raise ValueError((t, n, others))


class _Stage:
    def __init__(self, ins, outs, alias, sems, start, finish):
        self.ins, self.outs, self.alias, self.sems, self.start, self.finish = ins, outs, alias, sems, start, finish


def _call(name, body, grid, in_specs, out_specs, out_shape, operands, scratch=(), sem=None, stages=(), sequential=False, prefetch=None,
          shown=False, alias=None):
    n_in, n_out, n_sc = len(in_specs), len(out_specs), len(scratch)
    st_in = [len(s.ins) for s in stages]
    st_out = [len(s.outs) for s in stages]
    st_sem = [len(s.sems) for s in stages]
    n_pre = 0 if prefetch is None else 1
    aliases, ioff, ooff = {i + n_pre: o for i, o in (alias or {}).items()}, n_in + n_pre, n_out
    for s, ni, no in zip(stages, st_in, st_out):
        for i, o in s.alias.items():
            aliases[ioff + i] = ooff + o
        ioff, ooff = ioff + ni, ooff + no

    def split(refs, counts):
        out, at = [], 0
        for c in counts:
            out.append(refs[at:at + c])
            at += c
        return out

    def wrapped(*refs):
        ins, sins, outs, souts, sc, ssems = split(refs[n_pre:], [n_in, sum(st_in), n_out, sum(st_out), n_sc, sum(st_sem)])
        parts = list(zip(stages, split(sins, st_in), split(souts, st_out), split(ssems, st_sem)))
        if sequential:
            for s, a, b, c in parts:
                s.start(a, b, c)
                s.finish(a, b, c)
            return
        if parts:
            first = _all_of([pl.program_id(i) == 0 for i in range(len(grid))])
            last = _all_of([pl.program_id(i) == g - 1 for i, g in enumerate(grid)])

            @pl.when(first)
            def _():
                for s, a, b, c in parts:
                    s.start(a, b, c)

        body(*(refs[:n_pre] if shown else ()), *ins, *outs, *sc)
        if parts:
            @pl.when(last)
            def _():
                for s, a, b, c in parts:
                    s.finish(a, b, c)

    if stages or sem is None:
        sem = ("arbitrary",) * len(grid)
    specs = dict(grid=grid, in_specs=list(in_specs) + [ANY] * sum(st_in), out_specs=list(out_specs) + [ANY] * sum(st_out),
                 scratch_shapes=list(scratch) + [x for s in stages for x in s.sems])
    if prefetch is not None:
        specs = dict(grid_spec=pltpu.PrefetchScalarGridSpec(num_scalar_prefetch=1, **specs))
    res = pl.pallas_call(
        wrapped, name=name, out_shape=list(out_shape) + [o for s in stages for o in s.outs], input_output_aliases=aliases,
        compiler_params=pltpu.CompilerParams(dimension_semantics=sem, vmem_limit_bytes=VMEM_LIMIT), **specs,
    )(*([prefetch] if n_pre else []), *operands, *[a for s in stages for a in s.ins])
    res = list(res)
    return res[:n_out], split(res[n_out:], st_out)


def _all_of(conds):
    out = conds[0]
    for c in conds[1:]:
        out = out & c
    return out


def _coords():
    return lax.axis_index("x"), lax.axis_index("y"), lax.axis_index("c")


def _other_chips(x, y):
    return ((1 - x, y), (x, 1 - y), (1 - x, 1 - y))


def _lin(dev):
    return 4 * dev[0] + 2 * dev[1] + dev[2]


def _piece(total, lo, n, units=16):
    assert total % units == 0
    return lo * (total // units), n * (total // units)


FLOWS = 4


def _split(lo, cnt):
    k = next(k for k in (FLOWS, 2, 1) if cnt % (2 * SUBLANES * k) == 0)
    return [(lo + i * (cnt // k), cnt // k) for i in range(k)]


def _remote(src, dst, send, recv, to):
    return pltpu.make_async_remote_copy(src_ref=src, dst_ref=dst, send_sem=send, recv_sem=recv, device_id=to, device_id_type=MESH)


def _placer(kind, n, lo, cnt):
    def place(ref, dev):
        if kind == "col":
            return ref.at[pl.ds(lo, cnt), pl.ds(pl.multiple_of(_lin(dev) * n, LANES), n)]
        return ref.at[pl.ds(pl.multiple_of(_lin(dev) * n + lo, 2 * SUBLANES), cnt), :]
    return place


def _spread_stage(full, kind, piece=(0, 16), home=False):
    n = (full.shape[1] if kind == "col" else full.shape[0]) // N_DEV
    lo, cnt = _piece(full.shape[0] if kind == "col" else n, *piece)
    parts = _split(lo, cnt)
    npeers = 1 if home else 2

    def copies(ins, outs, sems):
        send, recv = sems
        x, y, c = _coords()
        me = (x, y, c)
        peers = [(x, y, 1 - c)] if home else [(1 - x, y, c), (x, 1 - y, c)]
        out, arrive = [], []
        for k, t in enumerate(peers):
            for i, (plo, pcnt) in enumerate(parts):
                place = _placer(kind, n, plo, pcnt)
                out.append(_remote(place(outs[0], me), place(outs[0], me), send.at[i, k], recv.at[i, k], t))
                arrive.append(_remote(place(outs[0], t), place(outs[0], t), send.at[i, k], recv.at[i, k], t))
        return out, arrive

    def start(ins, outs, sems):
        for cp in copies(ins, outs, sems)[0]:
            cp.start()

    def finish(ins, outs, sems):
        out, arrive = copies(ins, outs, sems)
        for cp in arrive:
            cp.wait_recv()
        for cp in out:
            cp.wait_send()

    return _Stage([full], [jax.ShapeDtypeStruct(full.shape, full.dtype)], {0: 0},
                  [pltpu.SemaphoreType.DMA((len(parts), npeers)), pltpu.SemaphoreType.DMA((len(parts), npeers))], start, finish)


def _relay_stage(full, kind, piece=(0, 16)):
    n = (full.shape[1] if kind == "col" else full.shape[0]) // N_DEV
    lo, cnt = _piece(full.shape[0] if kind == "col" else n, *piece)
    half = cnt // 2
    assert half % (2 * SUBLANES) == 0, (cnt, kind)
    tops, bottoms = _split(lo, half), _split(lo + half, half)

    def copies(ins, outs, sems):
        send, recv = sems
        x, y, c = _coords()
        xn, yn, dg = (1 - x, y, c), (x, 1 - y, c), (1 - x, 1 - y, c)
        out, arrive, k = [], [], 0
        for came_from, to, parts in ((yn, xn, tops), (xn, yn, bottoms)):
            for plo, pcnt in parts:
                place = _placer(kind, n, plo, pcnt)
                out.append(_remote(place(outs[0], came_from), place(outs[0], came_from), send.at[k], recv.at[k], to))
                arrive.append(_remote(place(outs[0], dg), place(outs[0], dg), send.at[k], recv.at[k], to))
                k += 1
        return out, arrive

    def start(ins, outs, sems):
        for cp in copies(ins, outs, sems)[0]:
            cp.start()

    def finish(ins, outs, sems):
        out, arrive = copies(ins, outs, sems)
        for cp in arrive:
            cp.wait_recv()
        for cp in out:
            cp.wait_send()

    return _Stage([full], [jax.ShapeDtypeStruct(full.shape, full.dtype)], {0: 0},
                  [pltpu.SemaphoreType.DMA((len(tops) + len(bottoms),)), pltpu.SemaphoreType.DMA((len(tops) + len(bottoms),))], start, finish)


def _forward_stage(full, kind, piece=(0, 16)):
    n = (full.shape[1] if kind == "col" else full.shape[0]) // N_DEV
    lo, cnt = _piece(full.shape[0] if kind == "col" else n, *piece)
    place = _placer(kind, n, lo, cnt)

    def copies(ins, outs, sems):
        send, recv = sems
        x, y, c = _coords()
        chips = _other_chips(x, y)
        out = [_remote(place(outs[0], (*chip, c)), place(outs[0], (*chip, c)), send.at[k], recv.at[k], (x, y, 1 - c)) for k, chip in enumerate(chips)]
        arrive = [_remote(place(outs[0], (*chip, 1 - c)), place(outs[0], (*chip, 1 - c)), send.at[k], recv.at[k], (x, y, 1 - c))
                  for k, chip in enumerate(chips)]
        return out, arrive

    def start(ins, outs, sems):
        for cp in copies(ins, outs, sems)[0]:
            cp.start()

    def finish(ins, outs, sems):
        out, arrive = copies(ins, outs, sems)
        for cp in arrive:
            cp.wait_recv()
        for cp in out:
            cp.wait_send()

    return _Stage([full], [jax.ShapeDtypeStruct(full.shape, full.dtype)], {0: 0},
                  [pltpu.SemaphoreType.DMA((3,)), pltpu.SemaphoreType.DMA((3,))], start, finish)


def _to_sibling_stage(theirs):
    def copies(ins, outs, sems):
        send, recv = sems
        x, y, c = _coords()
        return [_remote(ins[0].at[q], outs[0].at[q], send.at[q], recv.at[q], (x, y, 1 - c)) for q in range(N_CHIPS)]

    def start(ins, outs, sems):
        for cp in copies(ins, outs, sems):
            cp.start()

    def finish(ins, outs, sems):
        for cp in copies(ins, outs, sems):
            cp.wait()

    return _Stage([theirs], [jax.ShapeDtypeStruct(theirs.shape, BF16)], {},
                  [pltpu.SemaphoreType.DMA((N_CHIPS,)), pltpu.SemaphoreType.DMA((N_CHIPS,))], start, finish)


def _to_chips_stage(pair, dst, piece=(0, 16)):
    lo, cnt = _piece(pair.shape[1], *piece)
    parts = _split(lo, cnt)
    nsem = 3 * len(parts)

    def copies(ins, outs, sems):
        send, recv = sems
        x, y, c = _coords()
        mine = 2 * x + y
        out, arrive, k = [], [], 0
        for px, py in _other_chips(x, y):
            for plo, pcnt in parts:
                rows = pl.ds(plo, pcnt)
                out.append(_remote(ins[0].at[2 * px + py, rows, :], outs[0].at[mine, rows, :], send.at[k], recv.at[k], (px, py, c)))
                arrive.append(_remote(ins[0].at[2 * px + py, rows, :], outs[0].at[2 * px + py, rows, :], send.at[k], recv.at[k], (px, py, c)))
                k += 1
        return out, arrive

    def start(ins, outs, sems):
        for cp in copies(ins, outs, sems)[0]:
            cp.start()

    def finish(ins, outs, sems):
        out, arrive = copies(ins, outs, sems)
        for cp in arrive:
            cp.wait_recv()
        for cp in out:
            cp.wait_send()

    return _Stage([pair, dst], [jax.ShapeDtypeStruct(dst.shape, dst.dtype)], {1: 0},
                  [pltpu.SemaphoreType.DMA((nsem,)), pltpu.SemaphoreType.DMA((nsem,))], start, finish)


def _fuse(parts, ins, outs, alias):
    parts = [p for p in parts if p is not None]
    sems = [x for st, _, _ in parts for x in st.sems]

    def run(which):
        def go(i, o, s):
            refs, at = list(i) + list(o), 0
            for st, pi, po in parts:
                getattr(st, which)([refs[k] for k in pi], [refs[k] for k in po], s[at:at + len(st.sems)])
                at += len(st.sems)
        return go

    return _Stage(ins, [jax.ShapeDtypeStruct(o.shape, o.dtype) for o in outs], alias, sems, run("start"), run("finish"))


def _gather_stage(full, kind, new=None, relay=None, forward=None):
    return _fuse([(_spread_stage(full, kind, new), [0], [1]) if new else None,
                  (_relay_stage(full, kind, relay), [0], [1]) if relay else None,
                  (_spread_stage(full, kind, relay, home=True), [0], [1]) if relay else None,
                  (_forward_stage(full, kind, forward), [0], [1]) if forward else None], [full], [full], {0: 0})


def _small_stage(part):
    def copies(ins, outs, sems):
        send, recv = sems
        x, y, c = _coords()
        me = (x, y, c)
        peers = [(1 - x if k & 4 else x, 1 - y if k & 2 else y, 1 - c if k & 1 else c) for k in range(1, N_DEV)]
        out = [_remote(ins[0], outs[0].at[_lin(me)], send.at[k], recv.at[k], t) for k, t in enumerate(peers)]
        arrive = [_remote(ins[0], outs[0].at[_lin(t)], send.at[k], recv.at[k], t) for k, t in enumerate(peers)]
        return out, arrive

    def start(ins, outs, sems):
        for cp in copies(ins, outs, sems)[0]:
            cp.start()

    def finish(ins, outs, sems):
        out, arrive = copies(ins, outs, sems)
        for cp in arrive:
            cp.wait_recv()
        for cp in out:
            cp.wait_send()

    return _Stage([part], [jax.ShapeDtypeStruct((N_DEV, *part.shape), F32)], {},
                  [pltpu.SemaphoreType.DMA((N_DEV - 1,)), pltpu.SemaphoreType.DMA((N_DEV - 1,))], start, finish)


def _comm_only(name, stages):
    return _call(name, lambda: None, (1,), [], [], [], [], stages=stages, sequential=True)[1]


_GELU_C = math.sqrt(2.0 / math.pi)


def _gelu(x):
    return 0.5 * x * (1.0 + jnp.tanh(_GELU_C * (x + 0.044715 * x * x * x)))


def _gelu_grad(x):
    t = jnp.tanh(_GELU_C * (x + 0.044715 * x * x * x))
    return 0.5 * (1.0 + t) + 0.5 * x * (1.0 - t * t) * (_GELU_C * (1.0 + 3.0 * 0.044715 * x * x))


def _sigmoid(x):
    return 1.0 / (1.0 + jnp.exp(-x))


def _dot(a, b, mode):
    dims = {"nn": (((1,), (0,)), ((), ())), "nt": (((1,), (1,)), ((), ())), "tn": (((0,), (0,)), ((), ()))}[mode]
    return lax.dot_general(a.astype(BF16), b.astype(BF16), dims, preferred_element_type=F32)


def _matmul(name, a, b, mode, outs, epi=None, extras=(), colsums=(), tm=2048, tn=512, tk=2048, b_off=0, n=None, m_off=0, m=None, stages=()):
    if mode == "tn":
        kk, mfull = a.shape
    else:
        mfull, kk = a.shape
    m = mfull if m is None else m
    n = (b.shape[0] if mode == "nt" else b.shape[1]) if n is None else n
    tm, tk = _pick(tm, m, m_off), _pick(tk, kk)
    tn = _pick(tn, n, b_off, *[off for _, _, off in extras])
    boff, moff = b_off // tn, m_off // tm
    nm, nn_, nk = m // tm, n // tn, kk // tk
    col_major = bool(colsums)
    grid = (nn_, nm, nk) if col_major else (nm, nn_, nk)

    def imap(f):
        if col_major:
            return lambda g0, g1, k: f(g1, g0, k)
        return f

    a_spec = (pl.BlockSpec((tk, tm), imap(lambda i, j, k: (k, i + moff))) if mode == "tn"
              else pl.BlockSpec((tm, tk), imap(lambda i, j, k: (i + moff, k))))
    b_spec = (pl.BlockSpec((tn, tk), imap(lambda i, j, k: (j + boff, k))) if mode == "nt"
              else pl.BlockSpec((tk, tn), imap(lambda i, j, k: (k, j + boff))))
    in_specs, operands = [a_spec, b_spec], [a, b]
    for arr, kind, off in extras:
        o = off // tn
        if kind == "mn":
            in_specs.append(pl.BlockSpec((tm, tn), imap(lambda i, j, k, o=o: (i + moff, j + o))))
        else:
            in_specs.append(pl.BlockSpec((1, tn), imap(lambda i, j, k, o=o: (0, j + o))))
        operands.append(arr)
    out_shape = [jax.ShapeDtypeStruct((m, n), dt) for dt in outs] + [jax.ShapeDtypeStruct((1, n), F32) for _ in colsums]
    out_specs = ([pl.BlockSpec((tm, tn), imap(lambda i, j, k: (i, j))) for _ in outs]
                 + [pl.BlockSpec((1, tn), imap(lambda i, j, k: (0, j))) for _ in colsums])
    n_ex, n_out, n_cs = len(extras), len(outs), len(colsums)

    def body(*refs):
        a_ref, b_ref = refs[:2]
        ex_refs = refs[2:2 + n_ex]
        out_refs = refs[2 + n_ex:2 + n_ex + n_out]
        cs_refs = refs[2 + n_ex + n_out:2 + n_ex + n_out + n_cs]
        part = _dot(a_ref[...], b_ref[...], mode)

        def finish(acc):
            res = epi(acc, *[r[...] for r in ex_refs]) if epi is not None else (acc,)
            for r, v in zip(out_refs, res[:n_out]):
                r[...] = v.astype(r.dtype)
            if n_cs:
                @pl.when(pl.program_id(1) == 0)
                def _():
                    for r in cs_refs:
                        r[...] = jnp.zeros_like(r)

                for r, idx in zip(cs_refs, colsums):
                    r[...] += jnp.sum(res[idx], axis=0, keepdims=True)

        if nk == 1:
            finish(part)
        else:
            acc_ref = refs[-1]
            k = pl.program_id(2)

            @pl.when(k == 0)
            def _():
                acc_ref[...] = part

            @pl.when(k > 0)
            def _():
                acc_ref[...] += part

            @pl.when(k == nk - 1)
            def _():
                finish(acc_ref[...])

    sem = ("arbitrary", "arbitrary", "arbitrary") if col_major else ("parallel", "parallel", "arbitrary")
    return _call(name, body, grid, in_specs, out_specs, out_shape, operands,
                 scratch=[pltpu.VMEM((tm, tn), F32)] if nk > 1 else [], sem=sem, stages=stages)


def _project_shards(name, a, b, which, into=None, stages=()):
    m, kk = a.shape
    n = b.shape[1]
    tn = n // N_DEV
    o_spec = pl.BlockSpec((m, tn), lambda s, w_ref: (0, w_ref[s]))

    def body(a_ref, b_ref, *rest):
        rest[-1][...] = _dot(a_ref[...], b_ref[...], "nn")

    (out,), st = _call(name, body, (which.shape[0],),
                       [pl.BlockSpec((m, kk), lambda s, w_ref: (0, 0)), pl.BlockSpec((kk, tn), lambda s, w_ref: (0, w_ref[s]))]
                       + ([ANY] if into is not None else []), [o_spec], [jax.ShapeDtypeStruct((m, n), F32)],
                       [a, b] + ([into] if into is not None else []), sem=("arbitrary",), stages=stages, prefetch=which,
                       alias={2: 0} if into is not None else None)
    return out, st


def _row_spec(tr, c):
    return pl.BlockSpec((tr, c), lambda i: (i, 0))


def _fix_spec(shape):
    return pl.BlockSpec(shape, lambda *_: tuple(0 for _ in shape))


def _cast_bf16(name, x, tr=512):
    r, c = x.shape
    tr = _pick(tr, r)

    def body(x_ref, o_ref):
        o_ref[...] = x_ref[...].astype(BF16)

    return _call(name, body, (r // tr,), [_row_spec(tr, c)], [_row_spec(tr, c)], [jax.ShapeDtypeStruct((r, c), BF16)], [x],
                 sem=("parallel",))[0][0]


def _cast_into_place(name, w, kind, me, tr=512):
    r, c = w.shape
    tr = _pick(tr, r)
    nb = r // tr
    if kind == "col":
        o_spec = pl.BlockSpec((tr, c), lambda i, me_ref: (i, me_ref[0]))
        shape = (r, c * N_DEV)
    else:
        o_spec = pl.BlockSpec((tr, c), lambda i, me_ref: (me_ref[0] * nb + i, 0))
        shape = (r * N_DEV, c)

    def body(x_ref, o_ref):
        o_ref[...] = x_ref[...].astype(BF16)

    return _call(name, body, (nb,), [pl.BlockSpec((tr, c), lambda i, me_ref: (i, 0))], [o_spec], [jax.ShapeDtypeStruct(shape, BF16)], [w],
                 sem=("parallel",), prefetch=me)[0][0]


def _layer_norm_stats(x):
    mean = jnp.mean(x, axis=-1, keepdims=True)
    xc = x - mean
    var = jnp.mean(xc * xc, axis=-1, keepdims=True)
    rstd = lax.rsqrt(var + LN_EPS)
    return xc * rstd, rstd


def _layer_norm_bwd(dxhat, xhat, rstd):
    m1 = jnp.mean(dxhat, axis=-1, keepdims=True)
    m2 = jnp.mean(dxhat * xhat, axis=-1, keepdims=True)
    return rstd * (dxhat - m1 - xhat * m2)


def _ln1_fwd(pre1, g1, b1, tr=256, stages=()):
    s, d = pre1.shape
    tr = _pick(tr, s)

    def body(p_ref, g_ref, b_ref, xh_ref, rs_ref, h_ref):
        xhat, rstd = _layer_norm_stats(p_ref[...])
        xh_ref[...] = xhat
        rs_ref[...] = rstd
        h_ref[...] = (xhat * g_ref[...] + b_ref[...]).astype(BF16)

    return _call("ln1_fwd", body, (s // tr,), [_row_spec(tr, d), _fix_spec((1, d)), _fix_spec((1, d))],
                 [_row_spec(tr, d), _row_spec(tr, 1), _row_spec(tr, d)],
                 [jax.ShapeDtypeStruct((s, d), F32), jax.ShapeDtypeStruct((s, 1), F32), jax.ShapeDtypeStruct((s, d), BF16)],
                 [pre1, g1, b1], sem=("parallel",), stages=stages)


def _ln2_loss_bwd(ff, xhat1, g1, b1, g2, b2, target, tr=256):
    s, d = ff.shape
    tr = _pick(tr, s)

    def body(ff_ref, xh1_ref, g1_ref, b1_ref, g2_ref, b2_ref, t_ref, dp_ref, dpb_ref, dg_ref, db_ref, dbf_ref, loss_ref):
        @pl.when(pl.program_id(0) == 0)
        def _():
            dg_ref[...] = jnp.zeros_like(dg_ref)
            db_ref[...] = jnp.zeros_like(db_ref)
            dbf_ref[...] = jnp.zeros_like(dbf_ref)
            loss_ref[...] = jnp.zeros_like(loss_ref)

        h1 = xh1_ref[...] * g1_ref[...] + b1_ref[...]
        xhat, rstd = _layer_norm_stats(ALPHA * h1 + ff_ref[...])
        err = xhat * g2_ref[...] + b2_ref[...] - t_ref[...]
        row = jnp.mean(err * err, axis=-1, keepdims=True)
        loss_ref[...] += 0.5 * jnp.sum(row, axis=0, keepdims=True)
        dy = err / d
        dg_ref[...] += jnp.sum(dy * xhat, axis=0, keepdims=True)
        db_ref[...] += jnp.sum(dy, axis=0, keepdims=True)
        dpre = _layer_norm_bwd(dy * g2_ref[...], xhat, rstd)
        dbf_ref[...] += jnp.sum(dpre, axis=0, keepdims=True)
        dp_ref[...] = dpre
        dpb_ref[...] = dpre.astype(BF16)

    vec = _fix_spec((1, d))
    return _call("ln2_loss_bwd", body, (s // tr,), [_row_spec(tr, d), _row_spec(tr, d), vec, vec, vec, vec, _row_spec(tr, d)],
                 [_row_spec(tr, d), _row_spec(tr, d), vec, vec, vec, _fix_spec((1, 1))],
                 [jax.ShapeDtypeStruct((s, d), F32), jax.ShapeDtypeStruct((s, d), BF16)]
                 + [jax.ShapeDtypeStruct((1, d), F32)] * 3 + [jax.ShapeDtypeStruct((1, 1), F32)],
                 [ff, xhat1, g1, b1, g2, b2, target])[0]


def _ln1_bwd(dh1, xhat1, rstd1, g1, tr=256, stages=()):
    s, d = dh1.shape
    tr = _pick(tr, s)

    def body(dh_ref, xh_ref, rs_ref, g_ref, dp_ref, dpb_ref, dg_ref, db_ref):
        @pl.when(pl.program_id(0) == 0)
        def _():
            dg_ref[...] = jnp.zeros_like(dg_ref)
            db_ref[...] = jnp.zeros_like(db_ref)

        dh, xhat = dh_ref[...], xh_ref[...]
        dg_ref[...] += jnp.sum(dh * xhat, axis=0, keepdims=True)
        db_ref[...] += jnp.sum(dh, axis=0, keepdims=True)
        dpre = _layer_norm_bwd(dh * g_ref[...], xhat, rs_ref[...])
        dp_ref[...] = dpre
        dpb_ref[...] = dpre.astype(BF16)

    vec = _fix_spec((1, d))
    return _call("ln1_bwd", body, (s // tr,), [_row_spec(tr, d), _row_spec(tr, d), _row_spec(tr, 1), vec],
                 [_row_spec(tr, d), _row_spec(tr, d), vec, vec],
                 [jax.ShapeDtypeStruct((s, d), F32), jax.ShapeDtypeStruct((s, d), BF16)] + [jax.ShapeDtypeStruct((1, d), F32)] * 2,
                 [dh1, xhat1, rstd1, g1], stages=stages)


def _to_perm(x):
    return x.reshape(SEQ // N_SUB, N_SUB, -1).transpose(1, 0, 2).reshape(SEQ, -1)


def _from_perm(x):
    return x.reshape(N_SUB, SEQ // N_SUB, -1).transpose(1, 0, 2).reshape(SEQ, -1)


def _local_index(p):
    rho = np.arange(BLOCK)
    if p == 0:
        return 16 * (rho % 8) + rho // 8
    if p == 1:
        return 4 * (rho % 32) + rho // 32
    return rho


def _tile_view(x, p):
    c = x.shape[1]
    if p == 1:
        return x.reshape(4, 4, BLOCK, c)
    return x.reshape(N_SUB, BLOCK, c)


def _view_shape(c, p):
    return (4, 4, BLOCK, c) if p == 1 else (N_SUB, BLOCK, c)


def _tile_spec(p, width, col, shift=0):
    nblk = SEQ // DILATIONS[p] // BLOCK

    def blk(n):
        return jnp.clip(n + shift, 0, nblk - 1)

    if p == 0:
        return pl.BlockSpec((N_SUB, SUBLANES, width), lambda s, n: (0, blk(n), col))
    if p == 1:
        return pl.BlockSpec((4, None, 32, width), lambda s, n: (0, s, blk(n), col))
    return pl.BlockSpec((None, BLOCK, width), lambda s, n: (s, 0, col))


def _tile_grid(p):
    return ((1, 16), (4, 4), (16, 1))[p]


def _t5_bucket(n):
    max_exact = N_BUCKETS // 2
    nf = np.maximum(n, 1).astype(np.float32)
    large = max_exact + (np.log(nf / np.float32(max_exact)) / np.float32(math.log(MAX_DISTANCE / max_exact))
                         * np.float32(N_BUCKETS - max_exact)).astype(np.int32)
    large = np.minimum(large, N_BUCKETS - 1)
    return np.where(n < max_exact, n, large).astype(np.int32)


def _bucket_tables():
    tabs = np.zeros((3, 2, BLOCK, BLOCK), np.int32)
    for p, d in enumerate(DILATIONS):
        i = _local_index(p)
        diff = i[:, None] - i[None, :]
        tabs[p, 0] = np.where(diff <= 0, _t5_bucket((BLOCK + diff) * d), -1)
        tabs[p, 1] = np.where(diff >= 0, _t5_bucket(np.maximum(diff, 0) * d), -1)
    return tabs


def _bias_expand(rel_bias, buckets):
    nh = N_HEADS

    def body(rb_ref, bk_ref, o_ref):
        for w in range(2):
            bk = bk_ref[0, w]
            for h in range(nh):
                val = jnp.zeros((BLOCK, BLOCK), F32)
                for b in range(N_BUCKETS):
                    val = jnp.where(bk == b, rb_ref[b, h], val)
                o_ref[0, h, w] = jnp.where(bk < 0, NEG_INF, val)

    return _call("bias_expand", body, (3,),
                 [pl.BlockSpec(memory_space=pltpu.SMEM), pl.BlockSpec((1, 2, BLOCK, BLOCK), lambda p: (p, 0, 0, 0))],
                 [pl.BlockSpec((1, nh, 2, BLOCK, BLOCK), lambda p: (p, 0, 0, 0, 0))],
                 [jax.ShapeDtypeStruct((3, nh, 2, BLOCK, BLOCK), F32)], [rel_bias, buckets], sem=("parallel",))[0][0]


def _heads_to_lanes(cols):
    lane = lax.broadcasted_iota(I32, (BLOCK, LANES), 1)
    out = jnp.zeros((BLOCK, LANES), F32)
    for h, c in enumerate(cols):
        out = jnp.where(lane == h, c, out)
    return out


def _attn_fwd(qkv, bias, p, stages=()):
    d_a = _d_a()
    has_prev = SEQ // DILATIONS[p] // BLOCK > 1
    scale = HEAD_DIM ** -0.5
    view = _tile_view(qkv, p)

    width = 2 * BLOCK if has_prev else BLOCK

    def body(q_ref, kc_ref, kp_ref, vc_ref, vp_ref, b_ref, o_ref, l_ref, s_ref, p_ref):
        n = pl.program_id(1)
        q_all = q_ref[...].reshape(BLOCK, d_a).astype(BF16)
        k_all = kc_ref[...].reshape(BLOCK, d_a).astype(BF16)
        v_all = vc_ref[...].reshape(BLOCK, d_a).astype(BF16)
        if has_prev:
            k_all = jnp.concatenate([kp_ref[...].reshape(BLOCK, d_a).astype(BF16), k_all], axis=0)
            v_all = jnp.concatenate([vp_ref[...].reshape(BLOCK, d_a).astype(BF16), v_all], axis=0)
            no_prev = (lax.broadcasted_iota(I32, (BLOCK, width), 1) < BLOCK) & (n == 0)
        for h in range(N_HEADS):
            sl = slice(h * HEAD_DIM, (h + 1) * HEAD_DIM)
            s = _dot(q_all[:, sl], k_all[:, sl], "nt") * scale
            if has_prev:
                s = jnp.where(no_prev, NEG_INF, s + jnp.concatenate([b_ref[0, h, 0], b_ref[0, h, 1]], axis=1))
            else:
                s = s + b_ref[0, h, 1]
            s_ref[h] = s
        dens, lses = [], []
        for h in range(N_HEADS):
            s = s_ref[h]
            m = jnp.max(s, axis=-1, keepdims=True)
            pr = jnp.exp(s - m)
            den = jnp.sum(pr, axis=-1, keepdims=True)
            p_ref[h] = pr.astype(BF16)
            dens.append(den)
            lses.append(m + jnp.log(den))
        for h in range(N_HEADS):
            sl = slice(h * HEAD_DIM, (h + 1) * HEAD_DIM)
            o_ref[..., sl] = (_dot(p_ref[h], v_all[:, sl], "nn") / dens[h]).reshape(*o_ref.shape[:-1], HEAD_DIM)
        l_ref[...] = _heads_to_lanes(lses).reshape(l_ref.shape)

    (o, l), st = _call(
        f"attn_fwd{p}", body, _tile_grid(p),
        [_tile_spec(p, d_a, 0), _tile_spec(p, d_a, 1), _tile_spec(p, d_a, 1, -1), _tile_spec(p, d_a, 2), _tile_spec(p, d_a, 2, -1),
         pl.BlockSpec((1, N_HEADS, 2, BLOCK, BLOCK), lambda s, n: (p, 0, 0, 0, 0))],
        [_tile_spec(p, d_a, 0), _tile_spec(p, LANES, 0)],
        [jax.ShapeDtypeStruct(_view_shape(d_a, p), F32), jax.ShapeDtypeStruct(_view_shape(LANES, p), F32)],
        [view, view, view, view, view, bias], scratch=[pltpu.VMEM((N_HEADS, BLOCK, width), F32), pltpu.VMEM((N_HEADS, BLOCK, width), BF16)],
        sem=("parallel", "parallel"), stages=stages)
    return (o.reshape(SEQ, d_a), l.reshape(SEQ, LANES)), st


def _attn_combine(os_, ls_, tr=256, stages=()):
    d_a = _d_a()
    tr = _pick(tr, SEQ)

    def body(o0, o1, o2, l0, l1, l2, a_ref, ab_ref, lt_ref):
        l = [l0[...], l1[...], l2[...]]
        m = jnp.maximum(jnp.maximum(l[0], l[1]), l[2])
        w = [jnp.exp(x - m) for x in l]
        tot = w[0] + w[1] + w[2]
        lt_ref[...] = m + jnp.log(tot)
        w = [x / tot for x in w]
        for h in range(N_HEADS):
            sl = slice(h * HEAD_DIM, (h + 1) * HEAD_DIM)
            acc = w[0][:, h:h + 1] * o0[:, sl] + w[1][:, h:h + 1] * o1[:, sl] + w[2][:, h:h + 1] * o2[:, sl]
            a_ref[:, sl] = acc
            ab_ref[:, sl] = acc.astype(BF16)

    return _call("attn_combine", body, (SEQ // tr,), [_row_spec(tr, d_a)] * 3 + [_row_spec(tr, LANES)] * 3,
                 [_row_spec(tr, d_a), _row_spec(tr, d_a), _row_spec(tr, LANES)],
                 [jax.ShapeDtypeStruct((SEQ, d_a), F32), jax.ShapeDtypeStruct((SEQ, d_a), BF16), jax.ShapeDtypeStruct((SEQ, LANES), F32)],
                 [*os_, *ls_], sem=("parallel",), stages=stages)


def _attn_delta(dattn, attn, tr=256):
    d_a = _d_a()
    tr = _pick(tr, SEQ)

    def body(d_ref, a_ref, o_ref):
        prod = d_ref[...] * a_ref[...]
        lane = lax.broadcasted_iota(I32, (tr, LANES), 1)
        out = jnp.zeros((tr, LANES), F32)
        for h in range(N_HEADS):
            out = jnp.where(lane == h, jnp.sum(prod[:, h * HEAD_DIM:(h + 1) * HEAD_DIM], axis=-1, keepdims=True), out)
        o_ref[...] = out

    return _call("attn_delta", body, (SEQ // tr,), [_row_spec(tr, d_a)] * 2, [_row_spec(tr, LANES)],
                 [jax.ShapeDtypeStruct((SEQ, LANES), F32)], [dattn, attn], sem=("parallel",))[0][0]


def _attn_bwd(qkv, dattn, lse, delta, bias, p, stages=()):
    d_a = _d_a()
    nblk = SEQ // DILATIONS[p] // BLOCK
    has_next = nblk > 1
    scale = HEAD_DIM ** -0.5
    qv, dov, lv, tv = (_tile_view(x, p) for x in (qkv, dattn, lse, delta))

    rows = 2 * BLOCK if has_next else BLOCK

    def body(q_ref, qn_ref, k_ref, v_ref, do_ref, don_ref, l_ref, ln_ref, t_ref, tn_ref, b_ref, dq_ref, dk_ref, dv_ref, db_ref,
             carry_ref, s_ref, dp_ref, p_ref, ds_ref):
        j = pl.program_id(1)

        @pl.when((pl.program_id(0) == 0) & (j == 0))
        def _():
            db_ref[...] = jnp.zeros_like(db_ref)

        def both(cur, nxt, width, dtype):
            cur = cur[...].reshape(BLOCK, width).astype(dtype)
            return jnp.concatenate([cur, nxt[...].reshape(BLOCK, width).astype(dtype)], axis=0) if has_next else cur

        k_all = k_ref[...].reshape(BLOCK, d_a).astype(BF16)
        v_all = v_ref[...].reshape(BLOCK, d_a).astype(BF16)
        q_all, do_all = both(q_ref, qn_ref, d_a, BF16), both(do_ref, don_ref, d_a, BF16)
        l_all, t_all = both(l_ref, ln_ref, LANES, F32), both(t_ref, tn_ref, LANES, F32)
        if has_next:
            no_next = (lax.broadcasted_iota(I32, (rows, BLOCK), 0) >= BLOCK) & (j == nblk - 1)
        for h in range(N_HEADS):
            sl = slice(h * HEAD_DIM, (h + 1) * HEAD_DIM)
            s = _dot(q_all[:, sl], k_all[:, sl], "nt") * scale
            if has_next:
                s = jnp.where(no_next, NEG_INF, s + jnp.concatenate([b_ref[0, h, 1], b_ref[0, h, 0]], axis=0))
            else:
                s = s + b_ref[0, h, 1]
            s_ref[h] = s
            dp_ref[h] = _dot(do_all[:, sl], v_all[:, sl], "nt")
        for h in range(N_HEADS):
            pr = jnp.exp(s_ref[h] - l_all[:, h:h + 1])
            ds = pr * (dp_ref[h] - t_all[:, h:h + 1])
            db_ref[h, 1] += ds[:BLOCK]
            if has_next:
                db_ref[h, 0] += ds[BLOCK:]
            p_ref[h] = pr.astype(BF16)
            ds_ref[h] = ds.astype(BF16)
        for h in range(N_HEADS):
            sl = slice(h * HEAD_DIM, (h + 1) * HEAD_DIM)
            dq = _dot(ds_ref[h], k_all[:, sl], "nn") * scale
            mine = dq[:BLOCK]
            if has_next:
                mine = mine + jnp.where(j > 0, carry_ref[:, sl], 0.0)
            dq_ref[..., sl] = mine.reshape(*dq_ref.shape[:-1], HEAD_DIM)
            if has_next:
                carry_ref[:, sl] = dq[BLOCK:]
            dk_ref[..., sl] = (_dot(ds_ref[h], q_all[:, sl], "tn") * scale).reshape(*dk_ref.shape[:-1], HEAD_DIM)
            dv_ref[..., sl] = _dot(p_ref[h], do_all[:, sl], "tn").reshape(*dv_ref.shape[:-1], HEAD_DIM)

    def big(col, shift=0):
        return _tile_spec(p, d_a, col, shift)

    def small(shift=0):
        return _tile_spec(p, LANES, 0, shift)

    (dq, dk, dv, dbias), st = _call(
        f"attn_bwd{p}", body, _tile_grid(p),
        [big(0), big(0, 1), big(1), big(2), big(0), big(0, 1), small(), small(1), small(), small(1),
         pl.BlockSpec((1, N_HEADS, 2, BLOCK, BLOCK), lambda s, n: (p, 0, 0, 0, 0))],
        [big(0), big(0), big(0), pl.BlockSpec((N_HEADS, 2, BLOCK, BLOCK), lambda s, n: (0, 0, 0, 0))],
        [jax.ShapeDtypeStruct(_view_shape(d_a, p), F32)] * 3 + [jax.ShapeDtypeStruct((N_HEADS, 2, BLOCK, BLOCK), F32)],
        [qv, qv, qv, qv, dov, dov, lv, lv, tv, tv, bias],
        scratch=[pltpu.VMEM((BLOCK, d_a), F32), pltpu.VMEM((N_HEADS, rows, BLOCK), F32), pltpu.VMEM((N_HEADS, rows, BLOCK), F32),
                 pltpu.VMEM((N_HEADS, rows, BLOCK), BF16), pltpu.VMEM((N_HEADS, rows, BLOCK), BF16)], stages=stages)
    return (dq.reshape(SEQ, d_a), dk.reshape(SEQ, d_a), dv.reshape(SEQ, d_a), dbias), st


def _rel_bias_grad(dbias, buckets):
    nh = N_HEADS

    def body(d0, d1, d2, bk_ref, o_ref, t_ref):
        ds = (d0, d1, d2)

        def per_bucket(b, carry):
            for h in range(nh):
                acc = jnp.zeros((BLOCK, BLOCK), F32)
                for p in range(3):
                    for w in range(2):
                        acc = acc + jnp.where(bk_ref[p, w] == b, ds[p][h, w], 0.0)
                t_ref[pl.ds(b * nh + h, 1), :] = jnp.sum(acc, axis=0, keepdims=True)
            return carry

        lax.fori_loop(0, N_BUCKETS, per_bucket, 0)
        o_ref[...] = jnp.sum(t_ref[...], axis=-1, keepdims=True)

    return _call("rel_bias_grad", body, (1,), [_fix_spec((nh, 2, BLOCK, BLOCK))] * 3 + [_fix_spec((3, 2, BLOCK, BLOCK))],
                 [_fix_spec((N_BUCKETS * nh, 1))], [jax.ShapeDtypeStruct((N_BUCKETS * nh, 1), F32)], [*dbias, buckets],
                 scratch=[pltpu.VMEM((N_BUCKETS * nh, LANES), F32)])[0][0]


def _gmlp_fwd(rest, col0, gain, bias, ws, bs, causal, stages=()):
    d_b = _d_b()

    def body(u_ref, v_ref, g_ref, b_ref, ws_ref, bs_ref, c_ref, o_ref):
        u = u_ref[...].reshape(BLOCK, d_b)
        xhat, _ = _layer_norm_stats(_gelu(v_ref[...].reshape(BLOCK, d_b)))
        vn = (xhat * g_ref[...] + b_ref[...]).astype(BF16)
        outs = []
        for g in range(N_GROUPS):
            sl = slice(g * BLOCK, (g + 1) * BLOCK)
            w = jnp.where(c_ref[...] > 0, ws_ref[g], 0.0)
            z = _dot(w, vn[:, sl], "nn") + bs_ref[:, g:g + 1]
            outs.append(_gelu(u[:, sl]) * z)
        o_ref[...] = jnp.concatenate(outs, axis=-1).reshape(o_ref.shape)

    (out,), st = _call(
        "gmlp_fwd", body, (1, SEQ // BLOCK),
        [_tile_spec(0, d_b, col0), _tile_spec(0, d_b, col0 + 1), _fix_spec((1, d_b)), _fix_spec((1, d_b)),
         _fix_spec((N_GROUPS, BLOCK, BLOCK)), _fix_spec((BLOCK, N_GROUPS)), _fix_spec((BLOCK, BLOCK))],
        [_tile_spec(0, d_b, 0)], [jax.ShapeDtypeStruct(_view_shape(d_b, 0), F32)],
        [_tile_view(rest, 0), _tile_view(rest, 0), gain, bias, ws, bs, causal], sem=("parallel", "parallel"), stages=stages)
    return out.reshape(SEQ, d_b), st


def _gmlp_bwd(rest, col0, dgmlp, gain, bias, ws, bs, causal, stages=()):
    d_b = _d_b()
    nchunk = SEQ // BLOCK

    def body(u_ref, v_ref, dg_ref, g_ref, b_ref, ws_ref, bs_ref, c_ref, du_ref, dv_ref, dws_ref, dbs_ref, dgain_ref, dbias_ref):
        c = pl.program_id(1)

        @pl.when(c == 0)
        def _():
            dws_ref[...] = jnp.zeros_like(dws_ref)
            dbs_ref[...] = jnp.zeros_like(dbs_ref)
            dgain_ref[...] = jnp.zeros_like(dgain_ref)
            dbias_ref[...] = jnp.zeros_like(dbias_ref)

        u = u_ref[...].reshape(BLOCK, d_b)
        v = v_ref[...].reshape(BLOCK, d_b)
        dgm = dg_ref[...].reshape(BLOCK, d_b)
        xhat, rstd = _layer_norm_stats(_gelu(v))
        vn = (xhat * g_ref[...] + b_ref[...]).astype(BF16)
        lane = lax.broadcasted_iota(I32, (BLOCK, LANES), 1)
        dus, dvns = [], []
        dbs = dbs_ref[...]
        for g in range(N_GROUPS):
            sl = slice(g * BLOCK, (g + 1) * BLOCK)
            w = jnp.where(c_ref[...] > 0, ws_ref[g], 0.0).astype(BF16)
            z = _dot(w, vn[:, sl], "nn") + bs_ref[:, g:g + 1]
            dz = dgm[:, sl] * _gelu(u[:, sl])
            dus.append(dgm[:, sl] * z * _gelu_grad(u[:, sl]))
            dws_ref[g] += _dot(dz, vn[:, sl], "nt")
            dbs = dbs + jnp.where(lane == g, jnp.sum(dz, axis=-1, keepdims=True), 0.0)
            dvns.append(_dot(w, dz, "tn"))
        dbs_ref[...] = dbs
        dvn = jnp.concatenate(dvns, axis=-1)
        dgain_ref[...] += jnp.sum(dvn * xhat, axis=0, keepdims=True)
        dbias_ref[...] += jnp.sum(dvn, axis=0, keepdims=True)
        dvg = _layer_norm_bwd(dvn * g_ref[...], xhat, rstd)
        du_ref[...] = jnp.concatenate(dus, axis=-1).reshape(du_ref.shape)
        dv_ref[...] = (dvg * _gelu_grad(v)).reshape(dv_ref.shape)

        @pl.when(c == nchunk - 1)
        def _():
            for g in range(N_GROUPS):
                dws_ref[g] = jnp.where(c_ref[...] > 0, dws_ref[g], 0.0)

    (du, dv, dws, dbs, dgain, dbias), st = _call(
        "gmlp_bwd", body, (1, nchunk),
        [_tile_spec(0, d_b, col0), _tile_spec(0, d_b, col0 + 1), _tile_spec(0, d_b, 0), _fix_spec((1, d_b)), _fix_spec((1, d_b)),
         _fix_spec((N_GROUPS, BLOCK, BLOCK)), _fix_spec((BLOCK, N_GROUPS)), _fix_spec((BLOCK, BLOCK))],
        [_tile_spec(0, d_b, 0), _tile_spec(0, d_b, 0), _fix_spec((N_GROUPS, BLOCK, BLOCK)), _fix_spec((BLOCK, LANES)),
         _fix_spec((1, d_b)), _fix_spec((1, d_b))],
        [jax.ShapeDtypeStruct(_view_shape(d_b, 0), F32)] * 2
        + [jax.ShapeDtypeStruct((N_GROUPS, BLOCK, BLOCK), F32), jax.ShapeDtypeStruct((BLOCK, LANES), F32)]
        + [jax.ShapeDtypeStruct((1, d_b), F32)] * 2,
        [_tile_view(rest, 0), _tile_view(rest, 0), _tile_view(dgmlp, 0), gain, bias, ws, bs, causal], stages=stages)
    return (du.reshape(SEQ, d_b), dv.reshape(SEQ, d_b), dws, dbs, dgain, dbias), st


def _assemble_dproj(dqkv, du, dv, dga, dgb, tr=128, stages=()):
    d_a, d_b, d_in = _d_a(), _d_b(), _d_in()
    tr = _pick(tr, SEQ)

    def body(*refs):
        att, (du_ref, dv_ref, dga_ref, dgb_ref, o_ref) = refs[:9], refs[9:]
        for i in range(3):
            o_ref[:, i * d_a:(i + 1) * d_a] = (att[3 * i][...] + att[3 * i + 1][...] + att[3 * i + 2][...]).astype(BF16)
        o_ref[:, 3 * d_a:3 * d_a + d_b] = du_ref[...].astype(BF16)
        o_ref[:, 3 * d_a + d_b:3 * d_a + 2 * d_b] = dv_ref[...].astype(BF16)
        o_ref[:, 3 * d_a + 2 * d_b:3 * d_a + 2 * d_b + D_MODEL] = dga_ref[...]
        o_ref[:, 3 * d_a + 2 * d_b + D_MODEL:] = dgb_ref[...]

    return _call("assemble_dproj", body, (SEQ // tr,), [_row_spec(tr, d_a)] * 9 + [_row_spec(tr, d_b)] * 2 + [_row_spec(tr, D_MODEL)] * 2,
                 [_row_spec(tr, d_in)], [jax.ShapeDtypeStruct((SEQ, d_in), BF16)], [*dqkv, du, dv, dga, dgb], sem=("parallel",), stages=stages)


def _dw(name, a, b, kind, core, mine, add=None, tn=1152, stages=()):
    s, m = a.shape
    n = b.shape[1]
    rs, cs = (m, n // N_DEV) if kind == "col" else (m // N_DEV, n)
    tn = _pick(tn if kind == "col" else 512, cs)
    nj = cs // tn

    def shard(q, c_ref):
        return 2 * q + (c_ref[0] if mine else 1 - c_ref[0])

    if kind == "col":
        a_spec = pl.BlockSpec((s, m), lambda q, j, c_ref: (0, 0))
        b_spec = pl.BlockSpec((s, tn), lambda q, j, c_ref: (0, shard(q, c_ref) * nj + j))
    else:
        a_spec = pl.BlockSpec((s, rs), lambda q, j, c_ref: (0, shard(q, c_ref)))
        b_spec = pl.BlockSpec((s, tn), lambda q, j, c_ref: (0, j))
    o_spec = pl.BlockSpec((None, rs, tn), lambda q, j, c_ref: (q, 0, j))

    def body(a_ref, b_ref, *rest):
        acc = _dot(a_ref[...], b_ref[...], "tn")
        if add is not None:
            acc = acc + rest[0][...].astype(F32)
        rest[-1][...] = acc.astype(BF16)

    (out,), st = _call(name, body, (N_CHIPS, nj), [a_spec, b_spec] + ([o_spec] if add is not None else []), [o_spec],
                       [jax.ShapeDtypeStruct((N_CHIPS, rs, cs), BF16)], [a, b] + ([add] if add is not None else []),
                       sem=("parallel", "parallel"), stages=stages, prefetch=core)
    return out, st


def _adamw(w, g, m, v):
    m = ADAM_B1 * m + (1.0 - ADAM_B1) * g
    v = ADAM_B2 * v + (1.0 - ADAM_B2) * (g * g)
    m_hat = m / (1.0 - ADAM_B1 ** ADAM_STEP)
    v_hat = v / (1.0 - ADAM_B2 ** ADAM_STEP)
    delta = -ADAM_LR * (m_hat / (jnp.sqrt(v_hat) + ADAM_EPS) + ADAM_WD * w)
    return delta, m, v


def _adam_shard(name, pair, chip_sums, chip, w, m, v, tr=256, rows=None, into=None, stages=()):
    rs, cs = w.shape
    lo, cnt = rows or (0, rs)
    tr = _pick(tr, cnt, lo)
    first = lo // tr

    def body(chip_ref, own_ref, *refs):
        slots, (w_ref, m_ref, v_ref), (g_ref, d_ref, nm_ref, nv_ref) = refs[:N_CHIPS], refs[N_CHIPS:N_CHIPS + 3], refs[-4:]
        g = None
        for q in range(N_CHIPS):
            term = jnp.where(chip_ref[0] == q, own_ref[...], slots[q][...]).astype(F32)
            g = term if g is None else g + term
        d, nm, nv = _adamw(w_ref[...], g, m_ref[...], v_ref[...])
        g_ref[...], d_ref[...], nm_ref[...], nv_ref[...] = g, d, nm, nv

    def slot(q):
        return pl.BlockSpec((None, tr, cs), lambda i, c_ref: (jnp.where(c_ref[0] == q, (q + 1) % N_CHIPS, q), i + first, 0))

    spec = pl.BlockSpec((tr, cs), lambda i, c_ref: (i + first, 0))
    n_in = 1 + N_CHIPS + 3
    return _call(name, body, (cnt // tr,),
                 [pl.BlockSpec((None, tr, cs), lambda i, c_ref: (c_ref[0], i + first, 0))] + [slot(q) for q in range(N_CHIPS)]
                 + [spec, spec, spec] + ([ANY] * 4 if into is not None else []),
                 [spec] * 4, [jax.ShapeDtypeStruct((rs, cs), F32)] * 4, [pair] + [chip_sums] * N_CHIPS + [w, m, v] + list(into or ()),
                 sem=("parallel",), stages=stages, prefetch=chip, shown=True,
                 alias={n_in + k: k for k in range(4)} if into is not None else None)


def _adam_small(part, parts, me, w, m, v):
    rows = w.shape[0]

    def body(me_ref, own_ref, *refs):
        slots, (w_ref, m_ref, v_ref, g_ref, d_ref, nm_ref, nv_ref) = refs[:N_DEV], refs[N_DEV:]
        g = None
        for j in range(N_DEV):
            term = jnp.where(me_ref[0] == j, own_ref[...], slots[j][...])
            g = term if g is None else g + term
        d, nm, nv = _adamw(w_ref[...], g, m_ref[...], v_ref[...])
        g_ref[...], d_ref[...], nm_ref[...], nv_ref[...] = g, d, nm, nv

    def slot(j):
        return pl.BlockSpec((None, rows, LANES), lambda i, me_ref: (jnp.where(me_ref[0] == j, (j + 1) % N_DEV, j), 0, 0))

    spec = _fix_spec((rows, LANES))
    return _call("adam_small", body, (1,), [spec] + [slot(j) for j in range(N_DEV)] + [spec, spec, spec], [spec] * 4,
                 [jax.ShapeDtypeStruct((rows, LANES), F32)] * 4, [part] + [parts] * N_DEV + [w, m, v], prefetch=me, shown=True)[0]


def _small_sizes():
    d_b = _d_b()
    return (("loss", 1), ("rel_bias", N_BUCKETS * N_HEADS), ("ln_v_gain", d_b), ("ln_v_bias", d_b),
            ("w_spatial", N_GROUPS * BLOCK * BLOCK), ("b_spatial", N_GROUPS * BLOCK), ("ln1_gain", D_MODEL), ("ln1_bias", D_MODEL),
            ("b_ff1", D_FF), ("b_ff2", D_MODEL), ("ln2_gain", D_MODEL), ("ln2_bias", D_MODEL))


def _pack(vals):
    pieces = []
    for name, size in _small_sizes():
        flat = vals[name].reshape(-1).astype(F32)
        padded = -(-size // (SUBLANES * LANES)) * SUBLANES * LANES
        pieces.append(jnp.pad(flat, (0, padded - size)).reshape(-1, LANES))
    return jnp.concatenate(pieces, axis=0)


def _unpack(buf):
    out, row = {}, 0
    for name, size in _small_sizes():
        rows = -(-size // (SUBLANES * LANES)) * SUBLANES
        out[name] = buf[row:row + rows].reshape(-1)[:size]
        row += rows
    return out


def kernel(x, w_in, rel_bias, ln_v_gain, ln_v_bias, w_spatial, b_spatial, w_proj_a, w_proj_b, w_out, ln1_gain, ln1_bias, w_ff1, b_ff1, w_ff2, b_ff2, ln2_gain, ln2_bias, loss_target, m_w_in, m_rel_bias, m_ln_v_gain, m_ln_v_bias, m_w_spatial, m_b_spatial, m_w_proj_a, m_w_proj_b, m_w_out, m_ln1_gain, m_ln1_bias, m_w_ff1, m_b_ff1, m_w_ff2, m_b_ff2, m_ln2_gain, m_ln2_bias, v_w_in, v_rel_bias, v_ln_v_gain, v_ln_v_bias, v_w_spatial, v_b_spatial, v_w_proj_a, v_w_proj_b, v_w_out, v_ln1_gain, v_ln1_bias, v_w_ff1, v_b_ff1, v_w_ff2, v_b_ff2, v_ln2_gain, v_ln2_bias):
    d_a, d_b, d_in = _d_a(), _d_b(), _d_in()
    weights = dict(w_in=w_in, rel_bias=rel_bias, ln_v_gain=ln_v_gain, ln_v_bias=ln_v_bias, w_spatial=w_spatial, b_spatial=b_spatial,
                   w_proj_a=w_proj_a, w_proj_b=w_proj_b, w_out=w_out, ln1_gain=ln1_gain, ln1_bias=ln1_bias, w_ff1=w_ff1, b_ff1=b_ff1,
                   w_ff2=w_ff2, b_ff2=b_ff2, ln2_gain=ln2_gain, ln2_bias=ln2_bias)
    mom1 = dict(w_in=m_w_in, rel_bias=m_rel_bias, ln_v_gain=m_ln_v_gain, ln_v_bias=m_ln_v_bias, w_spatial=m_w_spatial,
                b_spatial=m_b_spatial, w_proj_a=m_w_proj_a, w_proj_b=m_w_proj_b, w_out=m_w_out, ln1_gain=m_ln1_gain,
                ln1_bias=m_ln1_bias, w_ff1=m_w_ff1, b_ff1=m_b_ff1, w_ff2=m_w_ff2, b_ff2=m_b_ff2, ln2_gain=m_ln2_gain, ln2_bias=m_ln2_bias)
    mom2 = dict(w_in=v_w_in, rel_bias=v_rel_bias, ln_v_gain=v_ln_v_gain, ln_v_bias=v_ln_v_bias, w_spatial=v_w_spatial,
                b_spatial=v_b_spatial, w_proj_a=v_w_proj_a, w_proj_b=v_w_proj_b, w_out=v_w_out, ln1_gain=v_ln1_gain,
                ln1_bias=v_ln1_bias, w_ff1=v_w_ff1, b_ff1=v_b_ff1, w_ff2=v_w_ff2, b_ff2=v_b_ff2, ln2_gain=v_ln2_gain, ln2_bias=v_ln2_bias)

    mx, my, mc = _coords()
    me = (4 * mx + 2 * my + mc).astype(I32).reshape(1)
    chip = (2 * mx + my).astype(I32).reshape(1)
    full = {n: _cast_into_place(f"cast_{n}", weights[n][0], KINDS[n], me) for n in KINDS}
    sent = {n: (0, 0, 0) for n in KINDS}

    def keep(table, n):
        def store(outs):
            table[n] = outs[0]
        return store

    def gathering(**new):
        stages = []
        for n in KINDS:
            out, relayed, passed = sent[n]
            units = new.get(n, 0)
            if units or relayed < out or passed < relayed:
                st = _gather_stage(full[n], KINDS[n], (out, units) if units else None,
                                   (relayed, out - relayed) if relayed < out else None, (passed, relayed - passed) if passed < relayed else None)
                st.store = keep(full, n)
                sent[n] = (out + units, out, relayed)
                stages.append(st)
        return stages

    def settle(stages, outs):
        for st, o in zip(stages, outs):
            st.store(o)

    def alone(name, stages):
        settle(stages, _comm_only(name, stages))

    def here(n):
        assert sent[n] == (16, 16, 16), (n, sent[n])
        return full[n]

    alone("gather_w_in_near", gathering(w_in=16))
    alone("gather_w_in_relay", gathering())
    alone("gather_w_in_sibling", gathering())

    xs = _to_perm(x[0])
    target = _to_perm(loss_target[0])
    xb = _cast_bf16("cast_x", xs)
    g8 = BLOCK // N_SUB
    ws_t = w_spatial[0].reshape(N_GROUPS, g8, N_SUB, g8, N_SUB).transpose(0, 2, 1, 4, 3).reshape(N_GROUPS, BLOCK, BLOCK)
    bs_t = b_spatial[0].reshape(N_GROUPS, g8, N_SUB).transpose(2, 1, 0).reshape(BLOCK, N_GROUPS)
    idx = _local_index(0)
    causal = jnp.asarray((idx[:, None] >= idx[None, :]).astype(np.float32))
    buckets = jnp.asarray(_bucket_tables())
    bias = _bias_expand(rel_bias, buckets)

    hosted = gathering(w_proj_a=16, w_proj_b=16, w_ff1=1)
    (qkv,), st = _matmul("proj_qkv", xb, here("w_in"), "nn", [F32], n=3 * d_a, stages=hosted)
    settle(hosted, st)
    hosted = gathering(w_out=16, w_ff1=4)
    (rest,), st = _matmul("proj_rest", xb, here("w_in"), "nn", [F32], b_off=3 * d_a, n=d_in - 3 * d_a, stages=hosted)
    settle(hosted, st)
    fwd = []
    for p in range(3):
        hosted = gathering(w_ff1=(3, 5, 3)[p])
        res, st = _attn_fwd(qkv, bias, p, stages=hosted)
        settle(hosted, st)
        fwd.append(res)
    hosted = gathering(w_ff2=1)
    (attn, attn_b, lse), st = _attn_combine([o for o, _ in fwd], [l for _, l in fwd], stages=hosted)
    settle(hosted, st)
    hosted = gathering(w_ff2=2)
    gmlp, st = _gmlp_fwd(rest, 0, ln_v_gain, ln_v_bias, ws_t, bs_t, causal, stages=hosted)
    settle(hosted, st)
    hosted = gathering(w_ff2=3)
    (ya,), st = _matmul("proj_a", attn_b, here("w_proj_a"), "nn", [F32], stages=hosted)
    settle(hosted, st)
    gate_a, gate_b = 2 * d_b, 2 * d_b + D_MODEL

    def merge(acc, ya_, ga, gb):
        return acc, _sigmoid(ga) * ya_ + _sigmoid(gb) * acc

    hosted = gathering(w_ff2=5)
    (yb, merged), st = _matmul("proj_b_merge", gmlp, here("w_proj_b"), "nn", [F32, BF16], merge,
                               [(ya, "mn", 0), (rest, "mn", gate_a), (rest, "mn", gate_b)], tn=256, stages=hosted)
    settle(hosted, st)
    hosted = gathering(w_ff2=3)
    (pre1,), st = _matmul("out_proj", merged, here("w_out"), "nn", [F32], lambda acc, x_: (ALPHA * x_ + acc,), [(xs, "mn", 0)], stages=hosted)
    settle(hosted, st)
    hosted = gathering(w_ff2=2)
    (xhat1, rstd1, h1b), st = _ln1_fwd(pre1, ln1_gain, ln1_bias, stages=hosted)
    settle(hosted, st)

    def relu2(acc, b_):
        r = jnp.maximum(acc + b_, 0.0)
        return r, r * r

    hosted = gathering()
    (relu, fb), st = _matmul("ff1", h1b, here("w_ff1"), "nn", [F32, BF16], relu2, [(b_ff1, "row", 0)], stages=hosted)
    settle(hosted, st)
    alone("gather_w_ff2_sibling", gathering())
    (ff,), _ = _matmul("ff2", fb, here("w_ff2"), "nn", [F32], lambda acc, b_: (acc + b_,), [(b_ff2, "row", 0)], tn=1024, tk=1024)

    core = lax.axis_index("c").astype(I32).reshape(1)
    factors, theirs, sib, pair, chips, reduced = {}, {}, {}, {}, {}, {}

    def grad_for_sibling(n, a, b, stages=()):
        factors[n] = (a, b)
        theirs[n], outs = _dw(f"dw_{n}_sibling", a, b, KINDS[n], core, False, stages=stages)
        settle(stages, outs)

    def to_sibling(n):
        st = _to_sibling_stage(theirs[n])
        st.store = keep(sib, n)
        return st

    def grad_own(n, stages=()):
        pair[n], outs = _dw(f"dw_{n}_own", *factors[n], KINDS[n], core, True, add=sib[n], stages=stages)
        settle(stages, outs)
        chips[n] = lax.empty(pair[n].shape, BF16)
        reduced[n] = 0

    def reducing(**new):
        stages = []
        for n, units in new.items():
            st = _to_chips_stage(pair[n], chips[n], (reduced[n], units))
            st.store = keep(chips, n)
            reduced[n] += units
            stages.append(st)
        return stages

    def summed(n):
        assert reduced[n] == 16, (n, reduced[n])
        return chips[n]

    dpre2, dpre2b, g_ln2_gain, g_ln2_bias, g_b_ff2, loss_part = _ln2_loss_bwd(ff, xhat1, ln1_gain, ln1_bias, ln2_gain, ln2_bias, target)
    grad_for_sibling("w_ff2", fb, dpre2b)

    def relu2_bwd(acc, r):
        da = acc * (2.0 * r)
        return da, da

    hosted = [to_sibling("w_ff2")]
    (dab, g_b_ff1), st = _matmul("d_ff1", dpre2b, here("w_ff2"), "nt", [BF16], relu2_bwd, [(relu, "mn", 0)], colsums=(1,), stages=hosted)
    settle(hosted, st)
    grad_own("w_ff2")
    grad_for_sibling("w_ff1", h1b, dab, reducing(w_ff2=3))
    hosted = reducing(w_ff2=8) + [to_sibling("w_ff1")]
    (dh1,), st = _matmul("d_h1", dab, here("w_ff1"), "nt", [F32], lambda acc, d_: (acc + ALPHA * d_,), [(dpre2, "mn", 0)], stages=hosted)
    settle(hosted, st)
    grad_own("w_ff1", reducing(w_ff2=4))
    hosted = reducing(w_ff2=1)
    (dpre1, dpre1b, g_ln1_gain, g_ln1_bias), st = _ln1_bwd(dh1, xhat1, rstd1, ln1_gain, stages=hosted)
    settle(hosted, st)
    grad_for_sibling("w_out", merged, dpre1b, reducing(w_ff1=1))

    def merge_bwd(acc, ga, gb, ya_, yb_):
        sa, sb = _sigmoid(ga), _sigmoid(gb)
        return acc * sa, acc * sb, acc * ya_ * (sa * (1.0 - sa)), acc * yb_ * (sb * (1.0 - sb))

    hosted = reducing(w_ff1=6) + [to_sibling("w_out")]
    (dya, dyb, dga, dgb), st = _matmul("d_merge", dpre1b, here("w_out"), "nt", [BF16] * 4, merge_bwd,
                                       [(rest, "mn", gate_a), (rest, "mn", gate_b), (ya, "mn", 0), (yb, "mn", 0)], tn=256, stages=hosted)
    settle(hosted, st)
    grad_own("w_out", reducing(w_ff1=1))
    grad_for_sibling("w_proj_a", attn_b, dya)
    grad_for_sibling("w_proj_b", gmlp, dyb)
    hosted = reducing(w_ff1=1) + [to_sibling("w_proj_a"), to_sibling("w_proj_b")]
    (dattn,), st = _matmul("d_attn", dya, here("w_proj_a"), "nt", [F32], stages=hosted)
    settle(hosted, st)
    grad_own("w_proj_a")
    grad_own("w_proj_b")
    hosted = reducing(w_ff1=1)
    (dgmlp,), st = _matmul("d_gmlp", dyb, here("w_proj_b"), "nt", [F32], stages=hosted)
    settle(hosted, st)
    hosted = reducing(w_ff1=2)
    (du, dvb, dws_t, dbs_t, g_lnv_gain, g_lnv_bias), st = _gmlp_bwd(rest, 0, dgmlp, ln_v_gain, ln_v_bias, ws_t, bs_t, causal, stages=hosted)
    settle(hosted, st)
    delta = _attn_delta(dattn, attn)
    bwd = []
    for p in range(3):
        hosted = reducing(**({"w_ff1": 3}, {"w_ff1": 1, "w_out": 8}, {"w_out": 8, "w_proj_a": 8})[p])
        res, st = _attn_bwd(qkv, dattn, lse, delta, bias, p, stages=hosted)
        settle(hosted, st)
        bwd.append(res)
    g_rel_bias = _rel_bias_grad([b[3] for b in bwd], buckets)
    hosted = reducing(w_proj_a=8, w_proj_b=16)
    (dproj,), st = _assemble_dproj([b[i] for i in range(3) for b in bwd], du, dvb, dga, dgb, stages=hosted)
    settle(hosted, st)

    g_w_spatial = dws_t.reshape(N_GROUPS, N_SUB, g8, N_SUB, g8).transpose(0, 2, 1, 4, 3)
    g_b_spatial = dbs_t[:, :N_GROUPS].reshape(N_SUB, g8, N_GROUPS).transpose(2, 1, 0)
    part = _pack(dict(loss=loss_part, rel_bias=g_rel_bias, ln_v_gain=g_lnv_gain, ln_v_bias=g_lnv_bias, w_spatial=g_w_spatial,
                      b_spatial=g_b_spatial, ln1_gain=g_ln1_gain, ln1_bias=g_ln1_bias, b_ff1=g_b_ff1, b_ff2=g_b_ff2,
                      ln2_gain=g_ln2_gain, ln2_bias=g_ln2_bias))
    small = _small_stage(part)
    small.store = keep(sib, "small")
    grad_for_sibling("w_in", xb, dproj, [small])
    parts = sib["small"]
    first = SEQ // 4

    def add_residual(acc, d_):
        return (acc + ALPHA * d_,)

    hosted = [to_sibling("w_in")]
    (dx0,), st = _matmul("d_x0", dproj, here("w_in"), "nt", [F32], add_residual, [(dpre1, "mn", 0)], tm=512, tn=1024, tk=3072, m=first,
                         stages=hosted)
    settle(hosted, st)
    grad_own("w_in")
    hosted = reducing(w_in=7)
    (dx1,), st = _matmul("d_x1", dproj, here("w_in"), "nt", [F32], add_residual, [(dpre1, "mn", 0)], tm=512, tn=1024, tk=3072,
                         m_off=first, m=SEQ - first, stages=hosted)
    settle(hosted, st)
    grad_x = _from_perm(jnp.concatenate([dx0, dx1], axis=0))[None]

    out_g, out_d, out_m, out_v = {}, {}, {}, {}
    for n, units in (("w_out", 3), ("w_proj_a", 3), ("w_proj_b", 3), ("w_ff2", 0), ("w_ff1", 0), ("w_in", 0)):
        hosted = reducing(w_in=units) if units else []
        (g, d, nm, nv), st = _adam_shard(f"adam_{n}", pair[n], summed(n), chip, weights[n][0], mom1[n][0], mom2[n][0], stages=hosted)
        settle(hosted, st)
        out_g[n], out_d[n], out_m[n], out_v[n] = g[None], d[None], nm[None], nv[None]

    zero = jnp.zeros((1,), F32)
    sg, sd, sm, sv = (_unpack(b) for b in _adam_small(
        part, parts, me, _pack({**weights, "loss": zero}), _pack({**mom1, "loss": zero}), _pack({**mom2, "loss": zero})))
    for n in WEIGHT_ORDER:
        if n not in KINDS:
            shape = weights[n].shape
            out_g[n], out_d[n], out_m[n], out_v[n] = (t[n].reshape(shape) for t in (sg, sd, sm, sv))
    loss = sg["loss"].reshape(())
    return (loss, grad_x, *[out_g[n] for n in WEIGHT_ORDER], *[out_d[n] for n in WEIGHT_ORDER],
            *[out_m[n] for n in WEIGHT_ORDER], *[out_v[n] for n in WEIGHT_ORDER])
```

```python
import math

import jax
import jax.numpy as jnp
import numpy as np
from jax import lax
from jax.experimental import pallas as pl
from jax.experimental.pallas import tpu as pltpu

F32 = jnp.float32
BF16 = jnp.bfloat16
I32 = jnp.int32

SEQ = 2048
D_MODEL = 2048
HEAD_DIM = 128
N_HEADS = 8
N_GROUPS = 8
D_FF = 4 * D_MODEL
BLOCK = 128
DILATIONS = (1, 4, 16)
N_BUCKETS = 32
MAX_DISTANCE = 2048
ALPHA = 2.0 ** 0.25
LN_EPS = 1e-5
NEG_INF = -1e30
N_DEV = 8
N_CHIPS = 4
N_SUB = 16
ADAM_LR, ADAM_B1, ADAM_B2, ADAM_EPS, ADAM_WD, ADAM_STEP = 0.001, 0.9, 0.999, 1e-08, 0.01, 10
LANES = 128
SUBLANES = 8
VMEM_LIMIT = 56 * 1024 * 1024
MESH = pl.DeviceIdType.MESH
ANY = pl.BlockSpec(memory_space=pl.ANY)
WEIGHT_ORDER = ("w_in", "rel_bias", "ln_v_gain", "ln_v_bias", "w_spatial", "b_spatial", "w_proj_a", "w_proj_b", "w_out",
                "ln1_gain", "ln1_bias", "w_ff1", "b_ff1", "w_ff2", "b_ff2", "ln2_gain", "ln2_bias")
KINDS = {"w_in": "col", "w_proj_a": "col", "w_proj_b": "col", "w_out": "row", "w_ff1": "col", "w_ff2": "row"}


def _d_a():
    return N_HEADS * HEAD_DIM


def _d_b():
    return N_GROUPS * BLOCK


def _d_in():
    return 3 * _d_a() + 2 * _d_b() + 2 * D_MODEL


def _pick(t, n, *others):
    if n <= t and all(o % n == 0 for o in others):
        return n
    for c in range(min(t, n) // LANES * LANES, 0, -LANES):
        if n % c == 0 and all(o % c == 0 for o in others):
            return c
    raise ValueError((t, n, others))


class _Stage:
    def __init__(self, ins, outs, alias, sems, start, finish):
        self.ins, self.outs, self.alias, self.sems, self.start, self.finish = ins, outs, alias, sems, start, finish


def _call(name, body, grid, in_specs, out_specs, out_shape, operands, scratch=(), sem=None, stages=(), sequential=False, prefetch=None,
          shown=False, alias=None):
    n_in, n_out, n_sc = len(in_specs), len(out_specs), len(scratch)
    st_in = [len(s.ins) for s in stages]
    st_out = [len(s.outs) for s in stages]
    st_sem = [len(s.sems) for s in stages]
    n_pre = 0 if prefetch is None else 1
    aliases, ioff, ooff = {i + n_pre: o for i, o in (alias or {}).items()}, n_in + n_pre, n_out
    for s, ni, no in zip(stages, st_in, st_out):
        for i, o in s.alias.items():
            aliases[ioff + i] = ooff + o
        ioff, ooff = ioff + ni, ooff + no

    def split(refs, counts):
        out, at = [], 0
        for c in counts:
            out.append(refs[at:at + c])
            at += c
        return out

    def wrapped(*refs):
        ins, sins, outs, souts, sc, ssems = split(refs[n_pre:], [n_in, sum(st_in), n_out, sum(st_out), n_sc, sum(st_sem)])
        parts = list(zip(stages, split(sins, st_in), split(souts, st_out), split(ssems, st_sem)))
        if sequential:
            for s, a, b, c in parts:
                s.start(a, b, c)
            for s, a, b, c in parts:
                s.finish(a, b, c)
            return
        if parts:
            first = _all_of([pl.program_id(i) == 0 for i in range(len(grid))])
            last = _all_of([pl.program_id(i) == g - 1 for i, g in enumerate(grid)])

            @pl.when(first)
            def _():
                for s, a, b, c in parts:
                    s.start(a, b, c)

        body(*(refs[:n_pre] if shown else ()), *ins, *outs, *sc)
        if parts:
            @pl.when(last)
            def _():
                for s, a, b, c in parts:
                    s.finish(a, b, c)

    if stages or sem is None:
        sem = ("arbitrary",) * len(grid)
    specs = dict(grid=grid, in_specs=list(in_specs) + [ANY] * sum(st_in), out_specs=list(out_specs) + [ANY] * sum(st_out),
                 scratch_shapes=list(scratch) + [x for s in stages for x in s.sems])
    if prefetch is not None:
        specs = dict(grid_spec=pltpu.PrefetchScalarGridSpec(num_scalar_prefetch=1, **specs))
    res = pl.pallas_call(
        wrapped, name=name, out_shape=list(out_shape) + [o for s in stages for o in s.outs], input_output_aliases=aliases,
        compiler_params=pltpu.CompilerParams(dimension_semantics=sem, vmem_limit_bytes=VMEM_LIMIT), **specs,
    )(*([prefetch] if n_pre else []), *operands, *[a for s in stages for a in s.ins])
    res = list(res)
    return res[:n_out], split(res[n_out:], st_out)


def _all_of(conds):
    out = conds[0]
    for c in conds[1:]:
        out = out & c
    return out


def _coords():
    return lax.axis_index("x"), lax.axis_index("y"), lax.axis_index("c")


def _other_chips(x, y):
    return ((1 - x, y), (x, 1 - y), (1 - x, 1 - y))


def _lin(dev):
    return 4 * dev[0] + 2 * dev[1] + dev[2]


def _piece(total, lo, n, units=16):
    assert total % units == 0
    return lo * (total // units), n * (total // units)


FLOWS = 4


def _split(lo, cnt):
    k = next(k for k in (FLOWS, 2, 1) if cnt % (2 * SUBLANES * k) == 0)
    return [(lo + i * (cnt // k), cnt // k) for i in range(k)]


def _remote(src, dst, send, recv, to):
    return pltpu.make_async_remote_copy(src_ref=src, dst_ref=dst, send_sem=send, recv_sem=recv, device_id=to, device_id_type=MESH)


def _placer(kind, n, lo, cnt):
    def place(ref, dev):
        if kind == "col":
            return ref.at[pl.ds(lo, cnt), pl.ds(pl.multiple_of(_lin(dev) * n, LANES), n)]
        return ref.at[pl.ds(pl.multiple_of(_lin(dev) * n + lo, 2 * SUBLANES), cnt), :]
    return place


def _spread_stage(full, kind, piece=(0, 16), home=False):
    n = (full.shape[1] if kind == "col" else full.shape[0]) // N_DEV
    lo, cnt = _piece(full.shape[0] if kind == "col" else n, *piece)
    parts = _split(lo, cnt)
    npeers = 1 if home else 2

    def copies(ins, outs, sems):
        send, recv = sems
        x, y, c = _coords()
        me = (x, y, c)
        peers = [(x, y, 1 - c)] if home else [(1 - x, y, c), (x, 1 - y, c)]
        out, arrive = [], []
        for k, t in enumerate(peers):
            for i, (plo, pcnt) in enumerate(parts):
                place = _placer(kind, n, plo, pcnt)
                out.append(_remote(place(outs[0], me), place(outs[0], me), send.at[i, k], recv.at[i, k], t))
                arrive.append(_remote(place(outs[0], t), place(outs[0], t), send.at[i, k], recv.at[i, k], t))
        return out, arrive

    def start(ins, outs, sems):
        for cp in copies(ins, outs, sems)[0]:
            cp.start()

    def finish(ins, outs, sems):
        out, arrive = copies(ins, outs, sems)
        for cp in arrive:
            cp.wait_recv()
        for cp in out:
            cp.wait_send()

    return _Stage([full], [jax.ShapeDtypeStruct(full.shape, full.dtype)], {0: 0},
                  [pltpu.SemaphoreType.DMA((len(parts), npeers)), pltpu.SemaphoreType.DMA((len(parts), npeers))], start, finish)


def _relay_stage(full, kind, piece=(0, 16)):
    n = (full.shape[1] if kind == "col" else full.shape[0]) // N_DEV
    lo, cnt = _piece(full.shape[0] if kind == "col" else n, *piece)
    half = cnt // 2
    assert half % (2 * SUBLANES) == 0, (cnt, kind)
    tops, bottoms = _split(lo, half), _split(lo + half, half)

    def copies(ins, outs, sems):
        send, recv = sems
        x, y, c = _coords()
        xn, yn, dg = (1 - x, y, c), (x, 1 - y, c), (1 - x, 1 - y, c)
        out, arrive, k = [], [], 0
        for came_from, to, parts in ((yn, xn, tops), (xn, yn, bottoms)):
            for plo, pcnt in parts:
                place = _placer(kind, n, plo, pcnt)
                out.append(_remote(place(outs[0], came_from), place(outs[0], came_from), send.at[k], recv.at[k], to))
                arrive.append(_remote(place(outs[0], dg), place(outs[0], dg), send.at[k], recv.at[k], to))
                k += 1
        return out, arrive

    def start(ins, outs, sems):
        for cp in copies(ins, outs, sems)[0]:
            cp.start()

    def finish(ins, outs, sems):
        out, arrive = copies(ins, outs, sems)
        for cp in arrive:
            cp.wait_recv()
        for cp in out:
            cp.wait_send()

    return _Stage([full], [jax.ShapeDtypeStruct(full.shape, full.dtype)], {0: 0},
                  [pltpu.SemaphoreType.DMA((len(tops) + len(bottoms),)), pltpu.SemaphoreType.DMA((len(tops) + len(bottoms),))], start, finish)


def _forward_stage(full, kind, piece=(0, 16)):
    n = (full.shape[1] if kind == "col" else full.shape[0]) // N_DEV
    lo, cnt = _piece(full.shape[0] if kind == "col" else n, *piece)
    place = _placer(kind, n, lo, cnt)

    def copies(ins, outs, sems):
        send, recv = sems
        x, y, c = _coords()
        chips = _other_chips(x, y)
        out = [_remote(place(outs[0], (*chip, c)), place(outs[0], (*chip, c)), send.at[k], recv.at[k], (x, y, 1 - c)) for k, chip in enumerate(chips)]
        arrive = [_remote(place(outs[0], (*chip, 1 - c)), place(outs[0], (*chip, 1 - c)), send.at[k], recv.at[k], (x, y, 1 - c))
                  for k, chip in enumerate(chips)]
        return out, arrive

    def start(ins, outs, sems):
        for cp in copies(ins, outs, sems)[0]:
            cp.start()

    def finish(ins, outs, sems):
        out, arrive = copies(ins, outs, sems)
        for cp in arrive:
            cp.wait_recv()
        for cp in out:
            cp.wait_send()

    return _Stage([full], [jax.ShapeDtypeStruct(full.shape, full.dtype)], {0: 0},
                  [pltpu.SemaphoreType.DMA((3,)), pltpu.SemaphoreType.DMA((3,))], start, finish)


def _to_sibling_stage(theirs):
    def copies(ins, outs, sems):
        send, recv = sems
        x, y, c = _coords()
        return [_remote(ins[0].at[q], outs[0].at[q], send.at[q], recv.at[q], (x, y, 1 - c)) for q in range(N_CHIPS)]

    def start(ins, outs, sems):
        for cp in copies(ins, outs, sems):
            cp.start()

    def finish(ins, outs, sems):
        for cp in copies(ins, outs, sems):
            cp.wait()

    return _Stage([theirs], [jax.ShapeDtypeStruct(theirs.shape, BF16)], {},
                  [pltpu.SemaphoreType.DMA((N_CHIPS,)), pltpu.SemaphoreType.DMA((N_CHIPS,))], start, finish)


def _to_chips_stage(pair, dst, piece=(0, 16)):
    lo, cnt = _piece(pair.shape[1], *piece)
    parts = _split(lo, cnt)
    nsem = 3 * len(parts)

    def copies(ins, outs, sems):
        send, recv = sems
        x, y, c = _coords()
        mine = 2 * x + y
        out, arrive, k = [], [], 0
        for px, py in _other_chips(x, y):
            for plo, pcnt in parts:
                rows = pl.ds(plo, pcnt)
                out.append(_remote(ins[0].at[2 * px + py, rows, :], outs[0].at[mine, rows, :], send.at[k], recv.at[k], (px, py, c)))
                arrive.append(_remote(ins[0].at[2 * px + py, rows, :], outs[0].at[2 * px + py, rows, :], send.at[k], recv.at[k], (px, py, c)))
                k += 1
        return out, arrive

    def start(ins, outs, sems):
        for cp in copies(ins, outs, sems)[0]:
            cp.start()

    def finish(ins, outs, sems):
        out, arrive = copies(ins, outs, sems)
        for cp in arrive:
            cp.wait_recv()
        for cp in out:
            cp.wait_send()

    return _Stage([pair, dst], [jax.ShapeDtypeStruct(dst.shape, dst.dtype)], {1: 0},
                  [pltpu.SemaphoreType.DMA((nsem,)), pltpu.SemaphoreType.DMA((nsem,))], start, finish)


def _fuse(parts, ins, outs, alias):
    parts = [p for p in parts if p is not None]
    sems = [x for st, _, _ in parts for x in st.sems]

    def run(which):
        def go(i, o, s):
            refs, at = list(i) + list(o), 0
            for st, pi, po in parts:
                getattr(st, which)([refs[k] for k in pi], [refs[k] for k in po], s[at:at + len(st.sems)])
                at += len(st.sems)
        return go

    return _Stage(ins, [jax.ShapeDtypeStruct(o.shape, o.dtype) for o in outs], alias, sems, run("start"), run("finish"))


def _gather_stage(full, kind, new=None, relay=None, forward=None):
    return _fuse([(_spread_stage(full, kind, new), [0], [1]) if new else None,
                  (_relay_stage(full, kind, relay), [0], [1]) if relay else None,
                  (_spread_stage(full, kind, relay, home=True), [0], [1]) if relay else None,
                  (_forward_stage(full, kind, forward), [0], [1]) if forward else None], [full], [full], {0: 0})


def _small_stage(part):
    def copies(ins, outs, sems):
        send, recv = sems
        x, y, c = _coords()
        me = (x, y, c)
        peers = [(1 - x if k & 4 else x, 1 - y if k & 2 else y, 1 - c if k & 1 else c) for k in range(1, N_DEV)]
        out = [_remote(ins[0], outs[0].at[_lin(me)], send.at[k], recv.at[k], t) for k, t in enumerate(peers)]
        arrive = [_remote(ins[0], outs[0].at[_lin(t)], send.at[k], recv.at[k], t) for k, t in enumerate(peers)]
        return out, arrive

    def start(ins, outs, sems):
        for cp in copies(ins, outs, sems)[0]:
            cp.start()

    def finish(ins, outs, sems):
        out, arrive = copies(ins, outs, sems)
        for cp in arrive:
            cp.wait_recv()
        for cp in out:
            cp.wait_send()

    return _Stage([part], [jax.ShapeDtypeStruct((N_DEV, *part.shape), F32)], {},
                  [pltpu.SemaphoreType.DMA((N_DEV - 1,)), pltpu.SemaphoreType.DMA((N_DEV - 1,))], start, finish)


def _comm_only(name, stages):
    return _call(name, lambda: None, (1,), [], [], [], [], stages=stages, sequential=True)[1]


_GELU_C = math.sqrt(2.0 / math.pi)


def _gelu(x):
    return 0.5 * x * (1.0 + jnp.tanh(_GELU_C * (x + 0.044715 * x * x * x)))


def _gelu_grad(x):
    t = jnp.tanh(_GELU_C * (x + 0.044715 * x * x * x))
    return 0.5 * (1.0 + t) + 0.5 * x * (1.0 - t * t) * (_GELU_C * (1.0 + 3.0 * 0.044715 * x * x))


def _sigmoid(x):
    return 1.0 / (1.0 + jnp.exp(-x))


def _dot(a, b, mode):
    dims = {"nn": (((1,), (0,)), ((), ())), "nt": (((1,), (1,)), ((), ())), "tn": (((0,), (0,)), ((), ()))}[mode]
    return lax.dot_general(a.astype(BF16), b.astype(BF16), dims, preferred_element_type=F32)


def _matmul(name, a, b, mode, outs, epi=None, extras=(), colsums=(), tm=2048, tn=512, tk=2048, b_off=0, n=None, m_off=0, m=None, stages=()):
    if mode == "tn":
        kk, mfull = a.shape
    else:
        mfull, kk = a.shape
    m = mfull if m is None else m
    n = (b.shape[0] if mode == "nt" else b.shape[1]) if n is None else n
    tm, tk = _pick(tm, m, m_off), _pick(tk, kk)
    tn = _pick(tn, n, b_off, *[off for _, _, off in extras])
    boff, moff = b_off // tn, m_off // tm
    nm, nn_, nk = m // tm, n // tn, kk // tk
    col_major = bool(colsums)
    grid = (nn_, nm, nk) if col_major else (nm, nn_, nk)

    def imap(f):
        if col_major:
            return lambda g0, g1, k: f(g1, g0, k)
        return f

    a_spec = (pl.BlockSpec((tk, tm), imap(lambda i, j, k: (k, i + moff))) if mode == "tn"
              else pl.BlockSpec((tm, tk), imap(lambda i, j, k: (i + moff, k))))
    b_spec = (pl.BlockSpec((tn, tk), imap(lambda i, j, k: (j + boff, k))) if mode == "nt"
              else pl.BlockSpec((tk, tn), imap(lambda i, j, k: (k, j + boff))))
    in_specs, operands = [a_spec, b_spec], [a, b]
    for arr, kind, off in extras:
        o = off // tn
        if kind == "mn":
            in_specs.append(pl.BlockSpec((tm, tn), imap(lambda i, j, k, o=o: (i + moff, j + o))))
        else:
            in_specs.append(pl.BlockSpec((1, tn), imap(lambda i, j, k, o=o: (0, j + o))))
        operands.append(arr)
    out_shape = [jax.ShapeDtypeStruct((m, n), dt) for dt in outs] + [jax.ShapeDtypeStruct((1, n), F32) for _ in colsums]
    out_specs = ([pl.BlockSpec((tm, tn), imap(lambda i, j, k: (i, j))) for _ in outs]
                 + [pl.BlockSpec((1, tn), imap(lambda i, j, k: (0, j))) for _ in colsums])
    n_ex, n_out, n_cs = len(extras), len(outs), len(colsums)

    def body(*refs):
        a_ref, b_ref = refs[:2]
        ex_refs = refs[2:2 + n_ex]
        out_refs = refs[2 + n_ex:2 + n_ex + n_out]
        cs_refs = refs[2 + n_ex + n_out:2 + n_ex + n_out + n_cs]
        part = _dot(a_ref[...], b_ref[...], mode)

        def finish(acc):
            res = epi(acc, *[r[...] for r in ex_refs]) if epi is not None else (acc,)
            for r, v in zip(out_refs, res[:n_out]):
                r[...] = v.astype(r.dtype)
            if n_cs:
                @pl.when(pl.program_id(1) == 0)
                def _():
                    for r in cs_refs:
                        r[...] = jnp.zeros_like(r)

                for r, idx in zip(cs_refs, colsums):
                    r[...] += jnp.sum(res[idx], axis=0, keepdims=True)

        if nk == 1:
            finish(part)
        else:
            acc_ref = refs[-1]
            k = pl.program_id(2)

            @pl.when(k == 0)
            def _():
                acc_ref[...] = part

            @pl.when(k > 0)
            def _():
                acc_ref[...] += part

            @pl.when(k == nk - 1)
            def _():
                finish(acc_ref[...])

    sem = ("arbitrary", "arbitrary", "arbitrary") if col_major else ("parallel", "parallel", "arbitrary")
    return _call(name, body, grid, in_specs, out_specs, out_shape, operands,
                 scratch=[pltpu.VMEM((tm, tn), F32)] if nk > 1 else [], sem=sem, stages=stages)


def _project_shards(name, a, b, which, into=None, stages=()):
    m, kk = a.shape
    n = b.shape[1]
    tn = n // N_DEV
    o_spec = pl.BlockSpec((m, tn), lambda s, w_ref: (0, w_ref[s]))

    def body(a_ref, b_ref, *rest):
        rest[-1][...] = _dot(a_ref[...], b_ref[...], "nn")

    (out,), st = _call(name, body, (which.shape[0],),
                       [pl.BlockSpec((m, kk), lambda s, w_ref: (0, 0)), pl.BlockSpec((kk, tn), lambda s, w_ref: (0, w_ref[s]))]
                       + ([ANY] if into is not None else []), [o_spec], [jax.ShapeDtypeStruct((m, n), F32)],
                       [a, b] + ([into] if into is not None else []), sem=("arbitrary",), stages=stages, prefetch=which,
                       alias={2: 0} if into is not None else None)
    return out, st


def _row_spec(tr, c):
    return pl.BlockSpec((tr, c), lambda i: (i, 0))


def _fix_spec(shape):
    return pl.BlockSpec(shape, lambda *_: tuple(0 for _ in shape))


def _cast_bf16(name, x, tr=512):
    r, c = x.shape
    tr = _pick(tr, r)

    def body(x_ref, o_ref):
        o_ref[...] = x_ref[...].astype(BF16)

    return _call(name, body, (r // tr,), [_row_spec(tr, c)], [_row_spec(tr, c)], [jax.ShapeDtypeStruct((r, c), BF16)], [x],
                 sem=("parallel",))[0][0]


def _cast_into_place(name, w, kind, me, tr=512):
    r, c = w.shape
    tr = _pick(tr, r)
    nb = r // tr
    if kind == "col":
        o_spec = pl.BlockSpec((tr, c), lambda i, me_ref: (i, me_ref[0]))
        shape = (r, c * N_DEV)
    else:
        o_spec = pl.BlockSpec((tr, c), lambda i, me_ref: (me_ref[0] * nb + i, 0))
        shape = (r * N_DEV, c)

    def body(x_ref, o_ref):
        o_ref[...] = x_ref[...].astype(BF16)

    return _call(name, body, (nb,), [pl.BlockSpec((tr, c), lambda i, me_ref: (i, 0))], [o_spec], [jax.ShapeDtypeStruct(shape, BF16)], [w],
                 sem=("parallel",), prefetch=me)[0][0]


def _layer_norm_stats(x):
    mean = jnp.mean(x, axis=-1, keepdims=True)
    xc = x - mean
    var = jnp.mean(xc * xc, axis=-1, keepdims=True)
    rstd = lax.rsqrt(var + LN_EPS)
    return xc * rstd, rstd


def _layer_norm_bwd(dxhat, xhat, rstd):
    m1 = jnp.mean(dxhat, axis=-1, keepdims=True)
    m2 = jnp.mean(dxhat * xhat, axis=-1, keepdims=True)
    return rstd * (dxhat - m1 - xhat * m2)


def _ln1_fwd(pre1, g1, b1, tr=256, stages=()):
    s, d = pre1.shape
    tr = _pick(tr, s)

    def body(p_ref, g_ref, b_ref, xh_ref, rs_ref, h_ref):
        xhat, rstd = _layer_norm_stats(p_ref[...])
        xh_ref[...] = xhat
        rs_ref[...] = rstd
        h_ref[...] = (xhat * g_ref[...] + b_ref[...]).astype(BF16)

    return _call("ln1_fwd", body, (s // tr,), [_row_spec(tr, d), _fix_spec((1, d)), _fix_spec((1, d))],
                 [_row_spec(tr, d), _row_spec(tr, 1), _row_spec(tr, d)],
                 [jax.ShapeDtypeStruct((s, d), F32), jax.ShapeDtypeStruct((s, 1), F32), jax.ShapeDtypeStruct((s, d), BF16)],
                 [pre1, g1, b1], sem=("parallel",), stages=stages)


def _ln2_loss_bwd(ff, xhat1, g1, b1, g2, b2, target, tr=256):
    s, d = ff.shape
    tr = _pick(tr, s)

    def body(ff_ref, xh1_ref, g1_ref, b1_ref, g2_ref, b2_ref, t_ref, dp_ref, dpb_ref, dg_ref, db_ref, dbf_ref, loss_ref):
        @pl.when(pl.program_id(0) == 0)
        def _():
            dg_ref[...] = jnp.zeros_like(dg_ref)
            db_ref[...] = jnp.zeros_like(db_ref)
            dbf_ref[...] = jnp.zeros_like(dbf_ref)
            loss_ref[...] = jnp.zeros_like(loss_ref)

        h1 = xh1_ref[...] * g1_ref[...] + b1_ref[...]
        xhat, rstd = _layer_norm_stats(ALPHA * h1 + ff_ref[...])
        err = xhat * g2_ref[...] + b2_ref[...] - t_ref[...]
        row = jnp.mean(err * err, axis=-1, keepdims=True)
        loss_ref[...] += 0.5 * jnp.sum(row, axis=0, keepdims=True)
        dy = err / d
        dg_ref[...] += jnp.sum(dy * xhat, axis=0, keepdims=True)
        db_ref[...] += jnp.sum(dy, axis=0, keepdims=True)
        dpre = _layer_norm_bwd(dy * g2_ref[...], xhat, rstd)
        dbf_ref[...] += jnp.sum(dpre, axis=0, keepdims=True)
        dp_ref[...] = dpre
        dpb_ref[...] = dpre.astype(BF16)

    vec = _fix_spec((1, d))
    return _call("ln2_loss_bwd", body, (s // tr,), [_row_spec(tr, d), _row_spec(tr, d), vec, vec, vec, vec, _row_spec(tr, d)],
                 [_row_spec(tr, d), _row_spec(tr, d), vec, vec, vec, _fix_spec((1, 1))],
                 [jax.ShapeDtypeStruct((s, d), F32), jax.ShapeDtypeStruct((s, d), BF16)]
                 + [jax.ShapeDtypeStruct((1, d), F32)] * 3 + [jax.ShapeDtypeStruct((1, 1), F32)],
                 [ff, xhat1, g1, b1, g2, b2, target])[0]


def _ln1_bwd(dh1, xhat1, rstd1, g1, tr=256, stages=()):
    s, d = dh1.shape
    tr = _pick(tr, s)

    def body(dh_ref, xh_ref, rs_ref, g_ref, dp_ref, dpb_ref, dg_ref, db_ref):
        @pl.when(pl.program_id(0) == 0)
        def _():
            dg_ref[...] = jnp.zeros_like(dg_ref)
            db_ref[...] = jnp.zeros_like(db_ref)

        dh, xhat = dh_ref[...], xh_ref[...]
        dg_ref[...] += jnp.sum(dh * xhat, axis=0, keepdims=True)
        db_ref[...] += jnp.sum(dh, axis=0, keepdims=True)
        dpre = _layer_norm_bwd(dh * g_ref[...], xhat, rs_ref[...])
        dp_ref[...] = dpre
        dpb_ref[...] = dpre.astype(BF16)

    vec = _fix_spec((1, d))
    return _call("ln1_bwd", body, (s // tr,), [_row_spec(tr, d), _row_spec(tr, d), _row_spec(tr, 1), vec],
                 [_row_spec(tr, d), _row_spec(tr, d), vec, vec],
                 [jax.ShapeDtypeStruct((s, d), F32), jax.ShapeDtypeStruct((s, d), BF16)] + [jax.ShapeDtypeStruct((1, d), F32)] * 2,
                 [dh1, xhat1, rstd1, g1], stages=stages)


def _to_perm(x):
    return x.reshape(SEQ // N_SUB, N_SUB, -1).transpose(1, 0, 2).reshape(SEQ, -1)


def _from_perm(x):
    return x.reshape(N_SUB, SEQ // N_SUB, -1).transpose(1, 0, 2).reshape(SEQ, -1)


def _local_index(p):
    rho = np.arange(BLOCK)
    if p == 0:
        return 16 * (rho % 8) + rho // 8
    if p == 1:
        return 4 * (rho % 32) + rho // 32
    return rho


def _tile_view(x, p):
    c = x.shape[1]
    if p == 1:
        return x.reshape(4, 4, BLOCK, c)
    return x.reshape(N_SUB, BLOCK, c)


def _view_shape(c, p):
    return (4, 4, BLOCK, c) if p == 1 else (N_SUB, BLOCK, c)


def _tile_spec(p, width, col, shift=0):
    nblk = SEQ // DILATIONS[p] // BLOCK

    def blk(n):
        return jnp.clip(n + shift, 0, nblk - 1)

    if p == 0:
        return pl.BlockSpec((N_SUB, SUBLANES, width), lambda s, n: (0, blk(n), col))
    if p == 1:
        return pl.BlockSpec((4, None, 32, width), lambda s, n: (0, s, blk(n), col))
    return pl.BlockSpec((None, BLOCK, width), lambda s, n: (s, 0, col))


def _tile_grid(p):
    return ((1, 16), (4, 4), (16, 1))[p]


def _t5_bucket(n):
    max_exact = N_BUCKETS // 2
    nf = np.maximum(n, 1).astype(np.float32)
    large = max_exact + (np.log(nf / np.float32(max_exact)) / np.float32(math.log(MAX_DISTANCE / max_exact))
                         * np.float32(N_BUCKETS - max_exact)).astype(np.int32)
    large = np.minimum(large, N_BUCKETS - 1)
    return np.where(n < max_exact, n, large).astype(np.int32)


def _bucket_tables():
    tabs = np.zeros((3, 2, BLOCK, BLOCK), np.int32)
    for p, d in enumerate(DILATIONS):
        i = _local_index(p)
        diff = i[:, None] - i[None, :]
        tabs[p, 0] = np.where(diff <= 0, _t5_bucket((BLOCK + diff) * d), -1)
        tabs[p, 1] = np.where(diff >= 0, _t5_bucket(np.maximum(diff, 0) * d), -1)
    return tabs


def _bias_expand(rel_bias, buckets):
    nh = N_HEADS

    def body(rb_ref, bk_ref, o_ref):
        for w in range(2):
            bk = bk_ref[0, w]
            for h in range(nh):
                val = jnp.zeros((BLOCK, BLOCK), F32)
                for b in range(N_BUCKETS):
                    val = jnp.where(bk == b, rb_ref[b, h], val)
                o_ref[0, h, w] = jnp.where(bk < 0, NEG_INF, val)

    return _call("bias_expand", body, (3,),
                 [pl.BlockSpec(memory_space=pltpu.SMEM), pl.BlockSpec((1, 2, BLOCK, BLOCK), lambda p: (p, 0, 0, 0))],
                 [pl.BlockSpec((1, nh, 2, BLOCK, BLOCK), lambda p: (p, 0, 0, 0, 0))],
                 [jax.ShapeDtypeStruct((3, nh, 2, BLOCK, BLOCK), F32)], [rel_bias, buckets], sem=("parallel",))[0][0]


def _heads_to_lanes(cols):
    lane = lax.broadcasted_iota(I32, (BLOCK, LANES), 1)
    out = jnp.zeros((BLOCK, LANES), F32)
    for h, c in enumerate(cols):
        out = jnp.where(lane == h, c, out)
    return out


def _attn_fwd(qkv, bias, p, stages=()):
    d_a = _d_a()
    has_prev = SEQ // DILATIONS[p] // BLOCK > 1
    scale = HEAD_DIM ** -0.5
    view = _tile_view(qkv, p)

    width = 2 * BLOCK if has_prev else BLOCK

    def body(q_ref, kc_ref, kp_ref, vc_ref, vp_ref, b_ref, o_ref, l_ref, s_ref, p_ref):
        n = pl.program_id(1)
        q_all = q_ref[...].reshape(BLOCK, d_a).astype(BF16)
        k_all = kc_ref[...].reshape(BLOCK, d_a).astype(BF16)
        v_all = vc_ref[...].reshape(BLOCK, d_a).astype(BF16)
        if has_prev:
            k_all = jnp.concatenate([kp_ref[...].reshape(BLOCK, d_a).astype(BF16), k_all], axis=0)
            v_all = jnp.concatenate([vp_ref[...].reshape(BLOCK, d_a).astype(BF16), v_all], axis=0)
            no_prev = (lax.broadcasted_iota(I32, (BLOCK, width), 1) < BLOCK) & (n == 0)
        for h in range(N_HEADS):
            sl = slice(h * HEAD_DIM, (h + 1) * HEAD_DIM)
            s = _dot(q_all[:, sl], k_all[:, sl], "nt") * scale
            if has_prev:
                s = jnp.where(no_prev, NEG_INF, s + jnp.concatenate([b_ref[0, h, 0], b_ref[0, h, 1]], axis=1))
            else:
                s = s + b_ref[0, h, 1]
            s_ref[h] = s
        dens, lses = [], []
        for h in range(N_HEADS):
            s = s_ref[h]
            m = jnp.max(s, axis=-1, keepdims=True)
            pr = jnp.exp(s - m)
            den = jnp.sum(pr, axis=-1, keepdims=True)
            p_ref[h] = pr.astype(BF16)
            dens.append(den)
            lses.append(m + jnp.log(den))
        for h in range(N_HEADS):
            sl = slice(h * HEAD_DIM, (h + 1) * HEAD_DIM)
            o_ref[..., sl] = (_dot(p_ref[h], v_all[:, sl], "nn") / dens[h]).reshape(*o_ref.shape[:-1], HEAD_DIM)
        l_ref[...] = _heads_to_lanes(lses).reshape(l_ref.shape)

    (o, l), st = _call(
        f"attn_fwd{p}", body, _tile_grid(p),
        [_tile_spec(p, d_a, 0), _tile_spec(p, d_a, 1), _tile_spec(p, d_a, 1, -1), _tile_spec(p, d_a, 2), _tile_spec(p, d_a, 2, -1),
         pl.BlockSpec((1, N_HEADS, 2, BLOCK, BLOCK), lambda s, n: (p, 0, 0, 0, 0))],
        [_tile_spec(p, d_a, 0), _tile_spec(p, LANES, 0)],
        [jax.ShapeDtypeStruct(_view_shape(d_a, p), F32), jax.ShapeDtypeStruct(_view_shape(LANES, p), F32)],
        [view, view, view, view, view, bias], scratch=[pltpu.VMEM((N_HEADS, BLOCK, width), F32), pltpu.VMEM((N_HEADS, BLOCK, width), BF16)],
        sem=("parallel", "parallel"), stages=stages)
    return (o.reshape(SEQ, d_a), l.reshape(SEQ, LANES)), st


def _attn_combine(os_, ls_, tr=256, stages=()):
    d_a = _d_a()
    tr = _pick(tr, SEQ)

    def body(o0, o1, o2, l0, l1, l2, a_ref, ab_ref, lt_ref):
        l = [l0[...], l1[...], l2[...]]
        m = jnp.maximum(jnp.maximum(l[0], l[1]), l[2])
        w = [jnp.exp(x - m) for x in l]
        tot = w[0] + w[1] + w[2]
        lt_ref[...] = m + jnp.log(tot)
        w = [x / tot for x in w]
        for h in range(N_HEADS):
            sl = slice(h * HEAD_DIM, (h + 1) * HEAD_DIM)
            acc = w[0][:, h:h + 1] * o0[:, sl] + w[1][:, h:h + 1] * o1[:, sl] + w[2][:, h:h + 1] * o2[:, sl]
            a_ref[:, sl] = acc
            ab_ref[:, sl] = acc.astype(BF16)

    return _call("attn_combine", body, (SEQ // tr,), [_row_spec(tr, d_a)] * 3 + [_row_spec(tr, LANES)] * 3,
                 [_row_spec(tr, d_a), _row_spec(tr, d_a), _row_spec(tr, LANES)],
                 [jax.ShapeDtypeStruct((SEQ, d_a), F32), jax.ShapeDtypeStruct((SEQ, d_a), BF16), jax.ShapeDtypeStruct((SEQ, LANES), F32)],
                 [*os_, *ls_], sem=("parallel",), stages=stages)


def _attn_delta(dattn, attn, tr=256):
    d_a = _d_a()
    tr = _pick(tr, SEQ)

    def body(d_ref, a_ref, o_ref):
        prod = d_ref[...] * a_ref[...]
        lane = lax.broadcasted_iota(I32, (tr, LANES), 1)
        out = jnp.zeros((tr, LANES), F32)
        for h in range(N_HEADS):
            out = jnp.where(lane == h, jnp.sum(prod[:, h * HEAD_DIM:(h + 1) * HEAD_DIM], axis=-1, keepdims=True), out)
        o_ref[...] = out

    return _call("attn_delta", body, (SEQ // tr,), [_row_spec(tr, d_a)] * 2, [_row_spec(tr, LANES)],
                 [jax.ShapeDtypeStruct((SEQ, LANES), F32)], [dattn, attn], sem=("parallel",))[0][0]


def _attn_bwd(qkv, dattn, lse, delta, bias, p, stages=()):
    d_a = _d_a()
    nblk = SEQ // DILATIONS[p] // BLOCK
    has_next = nblk > 1
    scale = HEAD_DIM ** -0.5
    qv, dov, lv, tv = (_tile_view(x, p) for x in (qkv, dattn, lse, delta))

    rows = 2 * BLOCK if has_next else BLOCK

    def body(q_ref, qn_ref, k_ref, v_ref, do_ref, don_ref, l_ref, ln_ref, t_ref, tn_ref, b_ref, dq_ref, dk_ref, dv_ref, db_ref,
             carry_ref, s_ref, dp_ref, p_ref, ds_ref):
        j = pl.program_id(1)

        @pl.when((pl.program_id(0) == 0) & (j == 0))
        def _():
            db_ref[...] = jnp.zeros_like(db_ref)

        def both(cur, nxt, width, dtype):
            cur = cur[...].reshape(BLOCK, width).astype(dtype)
            return jnp.concatenate([cur, nxt[...].reshape(BLOCK, width).astype(dtype)], axis=0) if has_next else cur

        k_all = k_ref[...].reshape(BLOCK, d_a).astype(BF16)
        v_all = v_ref[...].reshape(BLOCK, d_a).astype(BF16)
        q_all, do_all = both(q_ref, qn_ref, d_a, BF16), both(do_ref, don_ref, d_a, BF16)
        l_all, t_all = both(l_ref, ln_ref, LANES, F32), both(t_ref, tn_ref, LANES, F32)
        if has_next:
            no_next = (lax.broadcasted_iota(I32, (rows, BLOCK), 0) >= BLOCK) & (j == nblk - 1)
        for h in range(N_HEADS):
            sl = slice(h * HEAD_DIM, (h + 1) * HEAD_DIM)
            s = _dot(q_all[:, sl], k_all[:, sl], "nt") * scale
            if has_next:
                s = jnp.where(no_next, NEG_INF, s + jnp.concatenate([b_ref[0, h, 1], b_ref[0, h, 0]], axis=0))
            else:
                s = s + b_ref[0, h, 1]
            s_ref[h] = s
            dp_ref[h] = _dot(do_all[:, sl], v_all[:, sl], "nt")
        for h in range(N_HEADS):
            pr = jnp.exp(s_ref[h] - l_all[:, h:h + 1])
            ds = pr * (dp_ref[h] - t_all[:, h:h + 1])
            db_ref[h, 1] += ds[:BLOCK]
            if has_next:
                db_ref[h, 0] += ds[BLOCK:]
            p_ref[h] = pr.astype(BF16)
            ds_ref[h] = ds.astype(BF16)
        for h in range(N_HEADS):
            sl = slice(h * HEAD_DIM, (h + 1) * HEAD_DIM)
            dq = _dot(ds_ref[h], k_all[:, sl], "nn") * scale
            mine = dq[:BLOCK]
            if has_next:
                mine = mine + jnp.where(j > 0, carry_ref[:, sl], 0.0)
            dq_ref[..., sl] = mine.reshape(*dq_ref.shape[:-1], HEAD_DIM)
            if has_next:
                carry_ref[:, sl] = dq[BLOCK:]
            dk_ref[..., sl] = (_dot(ds_ref[h], q_all[:, sl], "tn") * scale).reshape(*dk_ref.shape[:-1], HEAD_DIM)
            dv_ref[..., sl] = _dot(p_ref[h], do_all[:, sl], "tn").reshape(*dv_ref.shape[:-1], HEAD_DIM)

    def big(col, shift=0):
        return _tile_spec(p, d_a, col, shift)

    def small(shift=0):
        return _tile_spec(p, LANES, 0, shift)

    (dq, dk, dv, dbias), st = _call(
        f"attn_bwd{p}", body, _tile_grid(p),
        [big(0), big(0, 1), big(1), big(2), big(0), big(0, 1), small(), small(1), small(), small(1),
         pl.BlockSpec((1, N_HEADS, 2, BLOCK, BLOCK), lambda s, n: (p, 0, 0, 0, 0))],
        [big(0), big(0), big(0), pl.BlockSpec((N_HEADS, 2, BLOCK, BLOCK), lambda s, n: (0, 0, 0, 0))],
        [jax.ShapeDtypeStruct(_view_shape(d_a, p), F32)] * 3 + [jax.ShapeDtypeStruct((N_HEADS, 2, BLOCK, BLOCK), F32)],
        [qv, qv, qv, qv, dov, dov, lv, lv, tv, tv, bias],
        scratch=[pltpu.VMEM((BLOCK, d_a), F32), pltpu.VMEM((N_HEADS, rows, BLOCK), F32), pltpu.VMEM((N_HEADS, rows, BLOCK), F32),
                 pltpu.VMEM((N_HEADS, rows, BLOCK), BF16), pltpu.VMEM((N_HEADS, rows, BLOCK), BF16)], stages=stages)
    return (dq.reshape(SEQ, d_a), dk.reshape(SEQ, d_a), dv.reshape(SEQ, d_a), dbias), st


def _rel_bias_grad(dbias, buckets):
    nh = N_HEADS

    def body(d0, d1, d2, bk_ref, o_ref, t_ref):
        ds = (d0, d1, d2)

        def per_bucket(b, carry):
            for h in range(nh):
                acc = jnp.zeros((BLOCK, BLOCK), F32)
                for p in range(3):
                    for w in range(2):
                        acc = acc + jnp.where(bk_ref[p, w] == b, ds[p][h, w], 0.0)
                t_ref[pl.ds(b * nh + h, 1), :] = jnp.sum(acc, axis=0, keepdims=True)
            return carry

        lax.fori_loop(0, N_BUCKETS, per_bucket, 0)
        o_ref[...] = jnp.sum(t_ref[...], axis=-1, keepdims=True)

    return _call("rel_bias_grad", body, (1,), [_fix_spec((nh, 2, BLOCK, BLOCK))] * 3 + [_fix_spec((3, 2, BLOCK, BLOCK))],
                 [_fix_spec((N_BUCKETS * nh, 1))], [jax.ShapeDtypeStruct((N_BUCKETS * nh, 1), F32)], [*dbias, buckets],
                 scratch=[pltpu.VMEM((N_BUCKETS * nh, LANES), F32)])[0][0]


def _gmlp_fwd(rest, col0, gain, bias, ws, bs, causal, stages=()):
    d_b = _d_b()

    def body(u_ref, v_ref, g_ref, b_ref, ws_ref, bs_ref, c_ref, o_ref):
        u = u_ref[...].reshape(BLOCK, d_b)
        xhat, _ = _layer_norm_stats(_gelu(v_ref[...].reshape(BLOCK, d_b)))
        vn = (xhat * g_ref[...] + b_ref[...]).astype(BF16)
        outs = []
        for g in range(N_GROUPS):
            sl = slice(g * BLOCK, (g + 1) * BLOCK)
            w = jnp.where(c_ref[...] > 0, ws_ref[g], 0.0)
            z = _dot(w, vn[:, sl], "nn") + bs_ref[:, g:g + 1]
            outs.append(_gelu(u[:, sl]) * z)
        o_ref[...] = jnp.concatenate(outs, axis=-1).reshape(o_ref.shape)

    (out,), st = _call(
        "gmlp_fwd", body, (1, SEQ // BLOCK),
        [_tile_spec(0, d_b, col0), _tile_spec(0, d_b, col0 + 1), _fix_spec((1, d_b)), _fix_spec((1, d_b)),
         _fix_spec((N_GROUPS, BLOCK, BLOCK)), _fix_spec((BLOCK, N_GROUPS)), _fix_spec((BLOCK, BLOCK))],
        [_tile_spec(0, d_b, 0)], [jax.ShapeDtypeStruct(_view_shape(d_b, 0), F32)],
        [_tile_view(rest, 0), _tile_view(rest, 0), gain, bias, ws, bs, causal], sem=("parallel", "parallel"), stages=stages)
    return out.reshape(SEQ, d_b), st


def _gmlp_bwd(rest, col0, dgmlp, gain, bias, ws, bs, causal, stages=()):
    d_b = _d_b()
    nchunk = SEQ // BLOCK

    def body(u_ref, v_ref, dg_ref, g_ref, b_ref, ws_ref, bs_ref, c_ref, du_ref, dv_ref, dws_ref, dbs_ref, dgain_ref, dbias_ref):
        c = pl.program_id(1)

        @pl.when(c == 0)
        def _():
            dws_ref[...] = jnp.zeros_like(dws_ref)
            dbs_ref[...] = jnp.zeros_like(dbs_ref)
            dgain_ref[...] = jnp.zeros_like(dgain_ref)
            dbias_ref[...] = jnp.zeros_like(dbias_ref)

        u = u_ref[...].reshape(BLOCK, d_b)
        v = v_ref[...].reshape(BLOCK, d_b)
        dgm = dg_ref[...].reshape(BLOCK, d_b)
        xhat, rstd = _layer_norm_stats(_gelu(v))
        vn = (xhat * g_ref[...] + b_ref[...]).astype(BF16)
        lane = lax.broadcasted_iota(I32, (BLOCK, LANES), 1)
        dus, dvns = [], []
        dbs = dbs_ref[...]
        for g in range(N_GROUPS):
            sl = slice(g * BLOCK, (g + 1) * BLOCK)
            w = jnp.where(c_ref[...] > 0, ws_ref[g], 0.0).astype(BF16)
            z = _dot(w, vn[:, sl], "nn") + bs_ref[:, g:g + 1]
            dz = dgm[:, sl] * _gelu(u[:, sl])
            dus.append(dgm[:, sl] * z * _gelu_grad(u[:, sl]))
            dws_ref[g] += _dot(dz, vn[:, sl], "nt")
            dbs = dbs + jnp.where(lane == g, jnp.sum(dz, axis=-1, keepdims=True), 0.0)
            dvns.append(_dot(w, dz, "tn"))
        dbs_ref[...] = dbs
        dvn = jnp.concatenate(dvns, axis=-1)
        dgain_ref[...] += jnp.sum(dvn * xhat, axis=0, keepdims=True)
        dbias_ref[...] += jnp.sum(dvn, axis=0, keepdims=True)
        dvg = _layer_norm_bwd(dvn * g_ref[...], xhat, rstd)
        du_ref[...] = jnp.concatenate(dus, axis=-1).reshape(du_ref.shape)
        dv_ref[...] = (dvg * _gelu_grad(v)).reshape(dv_ref.shape)

        @pl.when(c == nchunk - 1)
        def _():
            for g in range(N_GROUPS):
                dws_ref[g] = jnp.where(c_ref[...] > 0, dws_ref[g], 0.0)

    (du, dv, dws, dbs, dgain, dbias), st = _call(
        "gmlp_bwd", body, (1, nchunk),
        [_tile_spec(0, d_b, col0), _tile_spec(0, d_b, col0 + 1), _tile_spec(0, d_b, 0), _fix_spec((1, d_b)), _fix_spec((1, d_b)),
         _fix_spec((N_GROUPS, BLOCK, BLOCK)), _fix_spec((BLOCK, N_GROUPS)), _fix_spec((BLOCK, BLOCK))],
        [_tile_spec(0, d_b, 0), _tile_spec(0, d_b, 0), _fix_spec((N_GROUPS, BLOCK, BLOCK)), _fix_spec((BLOCK, LANES)),
         _fix_spec((1, d_b)), _fix_spec((1, d_b))],
        [jax.ShapeDtypeStruct(_view_shape(d_b, 0), F32)] * 2
        + [jax.ShapeDtypeStruct((N_GROUPS, BLOCK, BLOCK), F32), jax.ShapeDtypeStruct((BLOCK, LANES), F32)]
        + [jax.ShapeDtypeStruct((1, d_b), F32)] * 2,
        [_tile_view(rest, 0), _tile_view(rest, 0), _tile_view(dgmlp, 0), gain, bias, ws, bs, causal], stages=stages)
    return (du.reshape(SEQ, d_b), dv.reshape(SEQ, d_b), dws, dbs, dgain, dbias), st


def _assemble_dproj(dqkv, du, dv, dga, dgb, tr=128, stages=()):
    d_a, d_b, d_in = _d_a(), _d_b(), _d_in()
    tr = _pick(tr, SEQ)

    def body(*refs):
        att, (du_ref, dv_ref, dga_ref, dgb_ref, o_ref) = refs[:9], refs[9:]
        for i in range(3):
            o_ref[:, i * d_a:(i + 1) * d_a] = (att[3 * i][...] + att[3 * i + 1][...] + att[3 * i + 2][...]).astype(BF16)
        o_ref[:, 3 * d_a:3 * d_a + d_b] = du_ref[...].astype(BF16)
        o_ref[:, 3 * d_a + d_b:3 * d_a + 2 * d_b] = dv_ref[...].astype(BF16)
        o_ref[:, 3 * d_a + 2 * d_b:3 * d_a + 2 * d_b + D_MODEL] = dga_ref[...]
        o_ref[:, 3 * d_a + 2 * d_b + D_MODEL:] = dgb_ref[...]

    return _call("assemble_dproj", body, (SEQ // tr,), [_row_spec(tr, d_a)] * 9 + [_row_spec(tr, d_b)] * 2 + [_row_spec(tr, D_MODEL)] * 2,
                 [_row_spec(tr, d_in)], [jax.ShapeDtypeStruct((SEQ, d_in), BF16)], [*dqkv, du, dv, dga, dgb], sem=("parallel",), stages=stages)


def _dw(name, a, b, kind, core, mine, add=None, tn=1152, stages=()):
    s, m = a.shape
    n = b.shape[1]
    rs, cs = (m, n // N_DEV) if kind == "col" else (m // N_DEV, n)
    tn = _pick(tn if kind == "col" else 512, cs)
    nj = cs // tn

    def shard(q, c_ref):
        return 2 * q + (c_ref[0] if mine else 1 - c_ref[0])

    if kind == "col":
        a_spec = pl.BlockSpec((s, m), lambda q, j, c_ref: (0, 0))
        b_spec = pl.BlockSpec((s, tn), lambda q, j, c_ref: (0, shard(q, c_ref) * nj + j))
    else:
        a_spec = pl.BlockSpec((s, rs), lambda q, j, c_ref: (0, shard(q, c_ref)))
        b_spec = pl.BlockSpec((s, tn), lambda q, j, c_ref: (0, j))
    o_spec = pl.BlockSpec((None, rs, tn), lambda q, j, c_ref: (q, 0, j))

    def body(a_ref, b_ref, *rest):
        acc = _dot(a_ref[...], b_ref[...], "tn")
        if add is not None:
            acc = acc + rest[0][...].astype(F32)
        rest[-1][...] = acc.astype(BF16)

    (out,), st = _call(name, body, (N_CHIPS, nj), [a_spec, b_spec] + ([o_spec] if add is not None else []), [o_spec],
                       [jax.ShapeDtypeStruct((N_CHIPS, rs, cs), BF16)], [a, b] + ([add] if add is not None else []),
                       sem=("parallel", "parallel"), stages=stages, prefetch=core)
    return out, st


def _adamw(w, g, m, v):
    m = ADAM_B1 * m + (1.0 - ADAM_B1) * g
    v = ADAM_B2 * v + (1.0 - ADAM_B2) * (g * g)
    m_hat = m / (1.0 - ADAM_B1 ** ADAM_STEP)
    v_hat = v / (1.0 - ADAM_B2 ** ADAM_STEP)
    delta = -ADAM_LR * (m_hat / (jnp.sqrt(v_hat) + ADAM_EPS) + ADAM_WD * w)
    return delta, m, v


def _adam_shard(name, pair, chip_sums, chip, w, m, v, tr=256, rows=None, into=None, stages=()):
    rs, cs = w.shape
    lo, cnt = rows or (0, rs)
    tr = _pick(tr, cnt, lo)
    first = lo // tr

    def body(chip_ref, own_ref, *refs):
        slots, (w_ref, m_ref, v_ref), (g_ref, d_ref, nm_ref, nv_ref) = refs[:N_CHIPS], refs[N_CHIPS:N_CHIPS + 3], refs[-4:]
        g = None
        for q in range(N_CHIPS):
            term = jnp.where(chip_ref[0] == q, own_ref[...], slots[q][...]).astype(F32)
            g = term if g is None else g + term
        d, nm, nv = _adamw(w_ref[...], g, m_ref[...], v_ref[...])
        g_ref[...], d_ref[...], nm_ref[...], nv_ref[...] = g, d, nm, nv

    def slot(q):
        return pl.BlockSpec((None, tr, cs), lambda i, c_ref: (jnp.where(c_ref[0] == q, (q + 1) % N_CHIPS, q), i + first, 0))

    spec = pl.BlockSpec((tr, cs), lambda i, c_ref: (i + first, 0))
    n_in = 1 + N_CHIPS + 3
    return _call(name, body, (cnt // tr,),
                 [pl.BlockSpec((None, tr, cs), lambda i, c_ref: (c_ref[0], i + first, 0))] + [slot(q) for q in range(N_CHIPS)]
                 + [spec, spec, spec] + ([ANY] * 4 if into is not None else []),
                 [spec] * 4, [jax.ShapeDtypeStruct((rs, cs), F32)] * 4, [pair] + [chip_sums] * N_CHIPS + [w, m, v] + list(into or ()),
                 sem=("parallel",), stages=stages, prefetch=chip, shown=True,
                 alias={n_in + k: k for k in range(4)} if into is not None else None)


def _adam_small(part, parts, me, w, m, v):
    rows = w.shape[0]

    def body(me_ref, own_ref, *refs):
        slots, (w_ref, m_ref, v_ref, g_ref, d_ref, nm_ref, nv_ref) = refs[:N_DEV], refs[N_DEV:]
        g = None
        for j in range(N_DEV):
            term = jnp.where(me_ref[0] == j, own_ref[...], slots[j][...])
            g = term if g is None else g + term
        d, nm, nv = _adamw(w_ref[...], g, m_ref[...], v_ref[...])
        g_ref[...], d_ref[...], nm_ref[...], nv_ref[...] = g, d, nm, nv

    def slot(j):
        return pl.BlockSpec((None, rows, LANES), lambda i, me_ref: (jnp.where(me_ref[0] == j, (j + 1) % N_DEV, j), 0, 0))

    spec = _fix_spec((rows, LANES))
    return _call("adam_small", body, (1,), [spec] + [slot(j) for j in range(N_DEV)] + [spec, spec, spec], [spec] * 4,
                 [jax.ShapeDtypeStruct((rows, LANES), F32)] * 4, [part] + [parts] * N_DEV + [w, m, v], prefetch=me, shown=True)[0]


def _small_sizes():
    d_b = _d_b()
    return (("loss", 1), ("rel_bias", N_BUCKETS * N_HEADS), ("ln_v_gain", d_b), ("ln_v_bias", d_b),
            ("w_spatial", N_GROUPS * BLOCK * BLOCK), ("b_spatial", N_GROUPS * BLOCK), ("ln1_gain", D_MODEL), ("ln1_bias", D_MODEL),
            ("b_ff1", D_FF), ("b_ff2", D_MODEL), ("ln2_gain", D_MODEL), ("ln2_bias", D_MODEL))


def _pack(vals):
    pieces = []
    for name, size in _small_sizes():
        flat = vals[name].reshape(-1).astype(F32)
        padded = -(-size // (SUBLANES * LANES)) * SUBLANES * LANES
        pieces.append(jnp.pad(flat, (0, padded - size)).reshape(-1, LANES))
    return jnp.concatenate(pieces, axis=0)


def _unpack(buf):
    out, row = {}, 0
    for name, size in _small_sizes():
        rows = -(-size // (SUBLANES * LANES)) * SUBLANES
        out[name] = buf[row:row + rows].reshape(-1)[:size]
        row += rows
    return out


def kernel(x, w_in, rel_bias, ln_v_gain, ln_v_bias, w_spatial, b_spatial, w_proj_a, w_proj_b, w_out, ln1_gain, ln1_bias, w_ff1, b_ff1, w_ff2, b_ff2, ln2_gain, ln2_bias, loss_target, m_w_in, m_rel_bias, m_ln_v_gain, m_ln_v_bias, m_w_spatial, m_b_spatial, m_w_proj_a, m_w_proj_b, m_w_out, m_ln1_gain, m_ln1_bias, m_w_ff1, m_b_ff1, m_w_ff2, m_b_ff2, m_ln2_gain, m_ln2_bias, v_w_in, v_rel_bias, v_ln_v_gain, v_ln_v_bias, v_w_spatial, v_b_spatial, v_w_proj_a, v_w_proj_b, v_w_out, v_ln1_gain, v_ln1_bias, v_w_ff1, v_b_ff1, v_w_ff2, v_b_ff2, v_ln2_gain, v_ln2_bias):
    d_a, d_b, d_in = _d_a(), _d_b(), _d_in()
    weights = dict(w_in=w_in, rel_bias=rel_bias, ln_v_gain=ln_v_gain, ln_v_bias=ln_v_bias, w_spatial=w_spatial, b_spatial=b_spatial,
                   w_proj_a=w_proj_a, w_proj_b=w_proj_b, w_out=w_out, ln1_gain=ln1_gain, ln1_bias=ln1_bias, w_ff1=w_ff1, b_ff1=b_ff1,
                   w_ff2=w_ff2, b_ff2=b_ff2, ln2_gain=ln2_gain, ln2_bias=ln2_bias)
    mom1 = dict(w_in=m_w_in, rel_bias=m_rel_bias, ln_v_gain=m_ln_v_gain, ln_v_bias=m_ln_v_bias, w_spatial=m_w_spatial,
                b_spatial=m_b_spatial, w_proj_a=m_w_proj_a, w_proj_b=m_w_proj_b, w_out=m_w_out, ln1_gain=m_ln1_gain,
                ln1_bias=m_ln1_bias, w_ff1=m_w_ff1, b_ff1=m_b_ff1, w_ff2=m_w_ff2, b_ff2=m_b_ff2, ln2_gain=m_ln2_gain, ln2_bias=m_ln2_bias)
    mom2 = dict(w_in=v_w_in, rel_bias=v_rel_bias, ln_v_gain=v_ln_v_gain, ln_v_bias=v_ln_v_bias, w_spatial=v_w_spatial,
                b_spatial=v_b_spatial, w_proj_a=v_w_proj_a, w_proj_b=v_w_proj_b, w_out=v_w_out, ln1_gain=v_ln1_gain,
                ln1_bias=v_ln1_bias, w_ff1=v_w_ff1, b_ff1=v_b_ff1, w_ff2=v_w_ff2, b_ff2=v_b_ff2, ln2_gain=v_ln2_gain, ln2_bias=v_ln2_bias)

    mx, my, mc = _coords()
    me = (4 * mx + 2 * my + mc).astype(I32).reshape(1)
    chip = (2 * mx + my).astype(I32).reshape(1)
    full = {n: _cast_into_place(f"cast_{n}", weights[n][0], KINDS[n], me) for n in KINDS}
    sent = {n: (0, 0, 0) for n in KINDS}

    def keep(table, n):
        def store(outs):
            table[n] = outs[0]
        return store

    def gathering(**new):
        stages = []
        for n in KINDS:
            out, relayed, passed = sent[n]
            units = new.get(n, 0)
            if units or relayed < out or passed < relayed:
                st = _gather_stage(full[n], KINDS[n], (out, units) if units else None,
                                   (relayed, out - relayed) if relayed < out else None, (passed, relayed - passed) if passed < relayed else None)
                st.store = keep(full, n)
                sent[n] = (out + units, out, relayed)
                stages.append(st)
        return stages

    def settle(stages, outs):
        for st, o in zip(stages, outs):
            st.store(o)

    def alone(name, stages):
        settle(stages, _comm_only(name, stages))

    def here(n):
        assert sent[n] == (16, 16, 16), (n, sent[n])
        return full[n]

    alone("gather_w_in_near", gathering(w_in=16))
    alone("gather_w_in_relay", gathering())
    alone("gather_w_in_sibling", gathering(w_proj_a=16, w_proj_b=16))

    xs = _to_perm(x[0])
    target = _to_perm(loss_target[0])
    xb = _cast_bf16("cast_x", xs)
    g8 = BLOCK // N_SUB
    ws_t = w_spatial[0].reshape(N_GROUPS, g8, N_SUB, g8, N_SUB).transpose(0, 2, 1, 4, 3).reshape(N_GROUPS, BLOCK, BLOCK)
    bs_t = b_spatial[0].reshape(N_GROUPS, g8, N_SUB).transpose(2, 1, 0).reshape(BLOCK, N_GROUPS)
    idx = _local_index(0)
    causal = jnp.asarray((idx[:, None] >= idx[None, :]).astype(np.float32))
    buckets = jnp.asarray(_bucket_tables())
    bias = _bias_expand(rel_bias, buckets)

    hosted = gathering(w_out=8, w_ff1=1)
    (qkv,), st = _matmul("proj_qkv", xb, here("w_in"), "nn", [F32], n=3 * d_a, stages=hosted)
    settle(hosted, st)
    hosted = gathering(w_out=8, w_ff1=6)
    (rest,), st = _matmul("proj_rest", xb, here("w_in"), "nn", [F32], b_off=3 * d_a, n=d_in - 3 * d_a, stages=hosted)
    settle(hosted, st)
    fwd = []
    for p in range(3):
        hosted = gathering(**({"w_ff1": 2}, {"w_ff1": 5}, {"w_ff1": 2, "w_ff2": 1})[p])
        res, st = _attn_fwd(qkv, bias, p, stages=hosted)
        settle(hosted, st)
        fwd.append(res)
    hosted = gathering(w_ff2=1)
    (attn, attn_b, lse), st = _attn_combine([o for o, _ in fwd], [l for _, l in fwd], stages=hosted)
    settle(hosted, st)
    hosted = gathering(w_ff2=2)
    gmlp, st = _gmlp_fwd(rest, 0, ln_v_gain, ln_v_bias, ws_t, bs_t, causal, stages=hosted)
    settle(hosted, st)
    hosted = gathering(w_ff2=3)
    (ya,), st = _matmul("proj_a", attn_b, here("w_proj_a"), "nn", [F32], stages=hosted)
    settle(hosted, st)
    gate_a, gate_b = 2 * d_b, 2 * d_b + D_MODEL

    def merge(acc, ya_, ga, gb):
        return acc, _sigmoid(ga) * ya_ + _sigmoid(gb) * acc

    hosted = gathering(w_ff2=5)
    (yb, merged), st = _matmul("proj_b_merge", gmlp, here("w_proj_b"), "nn", [F32, BF16], merge,
                               [(ya, "mn", 0), (rest, "mn", gate_a), (rest, "mn", gate_b)], tn=256, stages=hosted)
    settle(hosted, st)
    hosted = gathering(w_ff2=3)
    (pre1,), st = _matmul("out_proj", merged, here("w_out"), "nn", [F32], lambda acc, x_: (ALPHA * x_ + acc,), [(xs, "mn", 0)], stages=hosted)
    settle(hosted, st)
    hosted = gathering(w_ff2=1)
    (xhat1, rstd1, h1b), st = _ln1_fwd(pre1, ln1_gain, ln1_bias, stages=hosted)
    settle(hosted, st)

    def relu2(acc, b_):
        r = jnp.maximum(acc + b_, 0.0)
        return r, r * r

    hosted = gathering()
    (relu, fb), st = _matmul("ff1", h1b, here("w_ff1"), "nn", [F32, BF16], relu2, [(b_ff1, "row", 0)], stages=hosted)
    settle(hosted, st)
    alone("gather_w_ff2_sibling", gathering())
    (ff,), _ = _matmul("ff2", fb, here("w_ff2"), "nn", [F32], lambda acc, b_: (acc + b_,), [(b_ff2, "row", 0)], tn=1024, tk=1024)

    core = lax.axis_index("c").astype(I32).reshape(1)
    factors, theirs, sib, pair, chips, reduced = {}, {}, {}, {}, {}, {}

    def grad_for_sibling(n, a, b, stages=()):
        factors[n] = (a, b)
        theirs[n], outs = _dw(f"dw_{n}_sibling", a, b, KINDS[n], core, False, stages=stages)
        settle(stages, outs)

    def to_sibling(n):
        st = _to_sibling_stage(theirs[n])
        st.store = keep(sib, n)
        return st

    def grad_own(n, stages=()):
        pair[n], outs = _dw(f"dw_{n}_own", *factors[n], KINDS[n], core, True, add=sib[n], stages=stages)
        settle(stages, outs)
        chips[n] = lax.empty(pair[n].shape, BF16)
        reduced[n] = 0

    def reducing(**new):
        stages = []
        for n, units in new.items():
            st = _to_chips_stage(pair[n], chips[n], (reduced[n], units))
            st.store = keep(chips, n)
            reduced[n] += units
            stages.append(st)
        return stages

    def summed(n):
        assert reduced[n] == 16, (n, reduced[n])
        return chips[n]

    dpre2, dpre2b, g_ln2_gain, g_ln2_bias, g_b_ff2, loss_part = _ln2_loss_bwd(ff, xhat1, ln1_gain, ln1_bias, ln2_gain, ln2_bias, target)
    grad_for_sibling("w_ff2", fb, dpre2b)

    def relu2_bwd(acc, r):
        da = acc * (2.0 * r)
        return da, da

    hosted = [to_sibling("w_ff2")]
    (dab, g_b_ff1), st = _matmul("d_ff1", dpre2b, here("w_ff2"), "nt", [BF16], relu2_bwd, [(relu, "mn", 0)], colsums=(1,), stages=hosted)
    settle(hosted, st)
    grad_own("w_ff2")
    grad_for_sibling("w_ff1", h1b, dab, reducing(w_ff2=3))
    hosted = reducing(w_ff2=8) + [to_sibling("w_ff1")]
    (dh1,), st = _matmul("d_h1", dab, here("w_ff1"), "nt", [F32], lambda acc, d_: (acc + ALPHA * d_,), [(dpre2, "mn", 0)], stages=hosted)
    settle(hosted, st)
    grad_own("w_ff1", reducing(w_ff2=4))
    hosted = reducing(w_ff2=1)
    (dpre1, dpre1b, g_ln1_gain, g_ln1_bias), st = _ln1_bwd(dh1, xhat1, rstd1, ln1_gain, stages=hosted)
    settle(hosted, st)
    grad_for_sibling("w_out", merged, dpre1b, reducing(w_ff1=1))

    def merge_bwd(acc, ga, gb, ya_, yb_):
        sa, sb = _sigmoid(ga), _sigmoid(gb)
        return acc * sa, acc * sb, acc * ya_ * (sa * (1.0 - sa)), acc * yb_ * (sb * (1.0 - sb))

    hosted = reducing(w_ff1=6) + [to_sibling("w_out")]
    (dya, dyb, dga, dgb), st = _matmul("d_merge", dpre1b, here("w_out"), "nt", [BF16] * 4, merge_bwd,
                                       [(rest, "mn", gate_a), (rest, "mn", gate_b), (ya, "mn", 0), (yb, "mn", 0)], tn=256, stages=hosted)
    settle(hosted, st)
    grad_own("w_out", reducing(w_ff1=1))
    grad_for_sibling("w_proj_a", attn_b, dya)
    grad_for_sibling("w_proj_b", gmlp, dyb)
    hosted = reducing(w_ff1=1) + [to_sibling("w_proj_a"), to_sibling("w_proj_b")]
    (dattn,), st = _matmul("d_attn", dya, here("w_proj_a"), "nt", [F32], stages=hosted)
    settle(hosted, st)
    grad_own("w_proj_a")
    grad_own("w_proj_b")
    hosted = reducing(w_ff1=1)
    (dgmlp,), st = _matmul("d_gmlp", dyb, here("w_proj_b"), "nt", [F32], stages=hosted)
    settle(hosted, st)
    hosted = reducing(w_ff1=2)
    (du, dvb, dws_t, dbs_t, g_lnv_gain, g_lnv_bias), st = _gmlp_bwd(rest, 0, dgmlp, ln_v_gain, ln_v_bias, ws_t, bs_t, causal, stages=hosted)
    settle(hosted, st)
    delta = _attn_delta(dattn, attn)
    bwd = []
    for p in range(3):
        hosted = reducing(**({"w_ff1": 3}, {"w_ff1": 1, "w_out": 8}, {"w_out": 8, "w_proj_a": 8})[p])
        res, st = _attn_bwd(qkv, dattn, lse, delta, bias, p, stages=hosted)
        settle(hosted, st)
        bwd.append(res)
    g_rel_bias = _rel_bias_grad([b[3] for b in bwd], buckets)
    hosted = reducing(w_proj_a=8, w_proj_b=16)
    (dproj,), st = _assemble_dproj([b[i] for i in range(3) for b in bwd], du, dvb, dga, dgb, stages=hosted)
    settle(hosted, st)

    g_w_spatial = dws_t.reshape(N_GROUPS, N_SUB, g8, N_SUB, g8).transpose(0, 2, 1, 4, 3)
    g_b_spatial = dbs_t[:, :N_GROUPS].reshape(N_SUB, g8, N_GROUPS).transpose(2, 1, 0)
    part = _pack(dict(loss=loss_part, rel_bias=g_rel_bias, ln_v_gain=g_lnv_gain, ln_v_bias=g_lnv_bias, w_spatial=g_w_spatial,
                      b_spatial=g_b_spatial, ln1_gain=g_ln1_gain, ln1_bias=g_ln1_bias, b_ff1=g_b_ff1, b_ff2=g_b_ff2,
                      ln2_gain=g_ln2_gain, ln2_bias=g_ln2_bias))
    small = _small_stage(part)
    small.store = keep(sib, "small")
    grad_for_sibling("w_in", xb, dproj, [small])
    parts = sib["small"]
    most = 3 * SEQ // 4

    def add_residual(acc, d_):
        return (acc + ALPHA * d_,)

    hosted = [to_sibling("w_in")]
    (dx_last,), st = _matmul("d_x0", dproj, here("w_in"), "nt", [F32], add_residual, [(dpre1, "mn", 0)], tm=512, tn=1024, tk=3072,
                             m_off=most, m=SEQ - most, stages=hosted)
    settle(hosted, st)
    grad_own("w_in")
    hosted = reducing(w_in=7)
    (dx_most,), st = _matmul("d_x1", dproj, here("w_in"), "nt", [F32], add_residual, [(dpre1, "mn", 0)], tm=most, tn=512, tk=3072,
                             m=most, stages=hosted)
    settle(hosted, st)
    grad_x = _from_perm(jnp.concatenate([dx_most, dx_last], axis=0))[None]

    out_g, out_d, out_m, out_v = {}, {}, {}, {}
    for n, units in (("w_out", 3), ("w_proj_a", 3), ("w_proj_b", 3), ("w_ff2", 0), ("w_ff1", 0), ("w_in", 0)):
        hosted = reducing(w_in=units) if units else []
        (g, d, nm, nv), st = _adam_shard(f"adam_{n}", pair[n], summed(n), chip, weights[n][0], mom1[n][0], mom2[n][0], stages=hosted)
        settle(hosted, st)
        out_g[n], out_d[n], out_m[n], out_v[n] = g[None], d[None], nm[None], nv[None]

    zero = jnp.zeros((1,), F32)
    sg, sd, sm, sv = (_unpack(b) for b in _adam_small(
        part, parts, me, _pack({**weights, "loss": zero}), _pack({**mom1, "loss": zero}), _pack({**mom2, "loss": zero})))
    for n in WEIGHT_ORDER:
        if n not in KINDS:
            shape = weights[n].shape
            out_g[n], out_d[n], out_m[n], out_v[n] = (t[n].reshape(shape) for t in (sg, sd, sm, sv))
    loss = sg["loss"].reshape(())
    return (loss, grad_x, *[out_g[n] for n in WEIGHT_ORDER], *[out_d[n] for n in WEIGHT_ORDER],
            *[out_m[n] for n in WEIGHT_ORDER], *[out_v[n] for n in WEIGHT_ORDER])
```

```python
import math

import jax
import jax.numpy as jnp
import numpy as np
from jax import lax
from jax.experimental import pallas as pl
from jax.experimental.pallas import tpu as pltpu

F32 = jnp.float32
BF16 = jnp.bfloat16
I32 = jnp.int32

SEQ = 2048
D_MODEL = 2048
HEAD_DIM = 128
N_HEADS = 8
N_GROUPS = 8
D_FF = 4 * D_MODEL
BLOCK = 128
DILATIONS = (1, 4, 16)
N_BUCKETS = 32
MAX_DISTANCE = 2048
ALPHA = 2.0 ** 0.25
LN_EPS = 1e-5
NEG_INF = -1e30
N_DEV = 8
N_CHIPS = 4
N_SUB = 16
ADAM_LR, ADAM_B1, ADAM_B2, ADAM_EPS, ADAM_WD, ADAM_STEP = 0.001, 0.9, 0.999, 1e-08, 0.01, 10
LANES = 128
SUBLANES = 8
VMEM_LIMIT = 56 * 1024 * 1024
MESH = pl.DeviceIdType.MESH
ANY = pl.BlockSpec(memory_space=pl.ANY)
WEIGHT_ORDER = ("w_in", "rel_bias", "ln_v_gain", "ln_v_bias", "w_spatial", "b_spatial", "w_proj_a", "w_proj_b", "w_out",
                "ln1_gain", "ln1_bias", "w_ff1", "b_ff1", "w_ff2", "b_ff2", "ln2_gain", "ln2_bias")
KINDS = {"w_in": "col", "w_proj_a": "col", "w_proj_b": "col", "w_out": "row", "w_ff1": "col", "w_ff2": "row"}


def _d_a():
    return N_HEADS * HEAD_DIM


def _d_b():
    return N_GROUPS * BLOCK


def _d_in():
    return 3 * _d_a() + 2 * _d_b() + 2 * D_MODEL


def _pick(t, n, *others):
    if n <= t and all(o % n == 0 for o in others):
        return n
    for c in range(min(t, n) // LANES * LANES, 0, -LANES):
        if n % c == 0 and all(o % c == 0 for o in others):
            return c
    raise ValueError((t, n, others))


class _Stage:
    def __init__(self, ins, outs, alias, sems, start, finish):
        self.ins, self.outs, self.alias, self.sems, self.start, self.finish = ins, outs, alias, sems, start, finish


def _call(name, body, grid, in_specs, out_specs, out_shape, operands, scratch=(), sem=None, stages=(), sequential=False, prefetch=None,
          shown=False, alias=None):
    n_in, n_out, n_sc = len(in_specs), len(out_specs), len(scratch)
    st_in = [len(s.ins) for s in stages]
    st_out = [len(s.outs) for s in stages]
    st_sem = [len(s.sems) for s in stages]
    n_pre = 0 if prefetch is None else 1
    aliases, ioff, ooff = {i + n_pre: o for i, o in (alias or {}).items()}, n_in + n_pre, n_out
    for s, ni, no in zip(stages, st_in, st_out):
        for i, o in s.alias.items():
            aliases[ioff + i] = ooff + o
        ioff, ooff = ioff + ni, ooff + no

    def split(refs, counts):
        out, at = [], 0
        for c in counts:
            out.append(refs[at:at + c])
            at += c
        return out

    def wrapped(*refs):
        ins, sins, outs, souts, sc, ssems = split(refs[n_pre:], [n_in, sum(st_in), n_out, sum(st_out), n_sc, sum(st_sem)])
        parts = list(zip(stages, split(sins, st_in), split(souts, st_out), split(ssems, st_sem)))
        if sequential:
            for s, a, b, c in parts:
                s.start(a, b, c)
            for s, a, b, c in parts:
                s.finish(a, b, c)
            return
        if parts:
            first = _all_of([pl.program_id(i) == 0 for i in range(len(grid))])
            last = _all_of([pl.program_id(i) == g - 1 for i, g in enumerate(grid)])

            @pl.when(first)
            def _():
                for s, a, b, c in parts:
                    s.start(a, b, c)

        body(*(refs[:n_pre] if shown else ()), *ins, *outs, *sc)
        if parts:
            @pl.when(last)
            def _():
                for s, a, b, c in parts:
                    s.finish(a, b, c)

    if stages or sem is None:
        sem = ("arbitrary",) * len(grid)
    specs = dict(grid=grid, in_specs=list(in_specs) + [ANY] * sum(st_in), out_specs=list(out_specs) + [ANY] * sum(st_out),
                 scratch_shapes=list(scratch) + [x for s in stages for x in s.sems])
    if prefetch is not None:
        specs = dict(grid_spec=pltpu.PrefetchScalarGridSpec(num_scalar_prefetch=1, **specs))
    res = pl.pallas_call(
        wrapped, name=name, out_shape=list(out_shape) + [o for s in stages for o in s.outs], input_output_aliases=aliases,
        compiler_params=pltpu.CompilerParams(dimension_semantics=sem, vmem_limit_bytes=VMEM_LIMIT), **specs,
    )(*([prefetch] if n_pre else []), *operands, *[a for s in stages for a in s.ins])
    res = list(res)
    return res[:n_out], split(res[n_out:], st_out)


def _all_of(conds):
    out = conds[0]
    for c in conds[1:]:
        out = out & c
    return out


def _coords():
    return lax.axis_index("x"), lax.axis_index("y"), lax.axis_index("c")


def _other_chips(x, y):
    return ((1 - x, y), (x, 1 - y), (1 - x, 1 - y))


def _lin(dev):
    return 4 * dev[0] + 2 * dev[1] + dev[2]


def _piece(total, lo, n, units=16):
    assert total % units == 0
    return lo * (total // units), n * (total // units)


FLOWS = 4


def _split(lo, cnt):
    k = next(k for k in (FLOWS, 2, 1) if cnt % (2 * SUBLANES * k) == 0)
    return [(lo + i * (cnt // k), cnt // k) for i in range(k)]


def _remote(src, dst, send, recv, to):
    return pltpu.make_async_remote_copy(src_ref=src, dst_ref=dst, send_sem=send, recv_sem=recv, device_id=to, device_id_type=MESH)


def _placer(kind, n, lo, cnt):
    def place(ref, dev):
        if kind == "col":
            return ref.at[pl.ds(lo, cnt), pl.ds(pl.multiple_of(_lin(dev) * n, LANES), n)]
        return ref.at[pl.ds(pl.multiple_of(_lin(dev) * n + lo, 2 * SUBLANES), cnt), :]
    return place


def _spread_stage(full, kind, piece=(0, 16), home=False):
    n = (full.shape[1] if kind == "col" else full.shape[0]) // N_DEV
    lo, cnt = _piece(full.shape[0] if kind == "col" else n, *piece)
    parts = _split(lo, cnt)
    npeers = 1 if home else 2

    def copies(ins, outs, sems):
        send, recv = sems
        x, y, c = _coords()
        me = (x, y, c)
        peers = [(x, y, 1 - c)] if home else [(1 - x, y, c), (x, 1 - y, c)]
        out, arrive = [], []
        for k, t in enumerate(peers):
            for i, (plo, pcnt) in enumerate(parts):
                place = _placer(kind, n, plo, pcnt)
                out.append(_remote(place(outs[0], me), place(outs[0], me), send.at[i, k], recv.at[i, k], t))
                arrive.append(_remote(place(outs[0], t), place(outs[0], t), send.at[i, k], recv.at[i, k], t))
        return out, arrive

    def start(ins, outs, sems):
        for cp in copies(ins, outs, sems)[0]:
            cp.start()

    def finish(ins, outs, sems):
        out, arrive = copies(ins, outs, sems)
        for cp in arrive:
            cp.wait_recv()
        for cp in out:
            cp.wait_send()

    return _Stage([full], [jax.ShapeDtypeStruct(full.shape, full.dtype)], {0: 0},
                  [pltpu.SemaphoreType.DMA((len(parts), npeers)), pltpu.SemaphoreType.DMA((len(parts), npeers))], start, finish)


def _relay_stage(full, kind, piece=(0, 16)):
    n = (full.shape[1] if kind == "col" else full.shape[0]) // N_DEV
    lo, cnt = _piece(full.shape[0] if kind == "col" else n, *piece)
    half = cnt // 2
    assert half % (2 * SUBLANES) == 0, (cnt, kind)
    tops, bottoms = _split(lo, half), _split(lo + half, half)

    def copies(ins, outs, sems):
        send, recv = sems
        x, y, c = _coords()
        xn, yn, dg = (1 - x, y, c), (x, 1 - y, c), (1 - x, 1 - y, c)
        out, arrive, k = [], [], 0
        for came_from, to, parts in ((yn, xn, tops), (xn, yn, bottoms)):
            for plo, pcnt in parts:
                place = _placer(kind, n, plo, pcnt)
                out.append(_remote(place(outs[0], came_from), place(outs[0], came_from), send.at[k], recv.at[k], to))
                arrive.append(_remote(place(outs[0], dg), place(outs[0], dg), send.at[k], recv.at[k], to))
                k += 1
        return out, arrive

    def start(ins, outs, sems):
        for cp in copies(ins, outs, sems)[0]:
            cp.start()

    def finish(ins, outs, sems):
        out, arrive = copies(ins, outs, sems)
        for cp in arrive:
            cp.wait_recv()
        for cp in out:
            cp.wait_send()

    return _Stage([full], [jax.ShapeDtypeStruct(full.shape, full.dtype)], {0: 0},
                  [pltpu.SemaphoreType.DMA((len(tops) + len(bottoms),)), pltpu.SemaphoreType.DMA((len(tops) + len(bottoms),))], start, finish)


def _forward_stage(full, kind, piece=(0, 16)):
    n = (full.shape[1] if kind == "col" else full.shape[0]) // N_DEV
    lo, cnt = _piece(full.shape[0] if kind == "col" else n, *piece)
    place = _placer(kind, n, lo, cnt)

    def copies(ins, outs, sems):
        send, recv = sems
        x, y, c = _coords()
        chips = _other_chips(x, y)
        out = [_remote(place(outs[0], (*chip, c)), place(outs[0], (*chip, c)), send.at[k], recv.at[k], (x, y, 1 - c)) for k, chip in enumerate(chips)]
        arrive = [_remote(place(outs[0], (*chip, 1 - c)), place(outs[0], (*chip, 1 - c)), send.at[k], recv.at[k], (x, y, 1 - c))
                  for k, chip in enumerate(chips)]
        return out, arrive

    def start(ins, outs, sems):
        for cp in copies(ins, outs, sems)[0]:
            cp.start()

    def finish(ins, outs, sems):
        out, arrive = copies(ins, outs, sems)
        for cp in arrive:
            cp.wait_recv()
        for cp in out:
            cp.wait_send()

    return _Stage([full], [jax.ShapeDtypeStruct(full.shape, full.dtype)], {0: 0},
                  [pltpu.SemaphoreType.DMA((3,)), pltpu.SemaphoreType.DMA((3,))], start, finish)


def _to_sibling_stage(theirs):
    def copies(ins, outs, sems):
        send, recv = sems
        x, y, c = _coords()
        return [_remote(ins[0].at[q], outs[0].at[q], send.at[q], recv.at[q], (x, y, 1 - c)) for q in range(N_CHIPS)]

    def start(ins, outs, sems):
        for cp in copies(ins, outs, sems):
            cp.start()

    def finish(ins, outs, sems):
        for cp in copies(ins, outs, sems):
            cp.wait()

    return _Stage([theirs], [jax.ShapeDtypeStruct(theirs.shape, BF16)], {},
                  [pltpu.SemaphoreType.DMA((N_CHIPS,)), pltpu.SemaphoreType.DMA((N_CHIPS,))], start, finish)


def _to_chips_stage(pair, dst, piece=(0, 16)):
    lo, cnt = _piece(pair.shape[1], *piece)
    parts = _split(lo, cnt)
    nsem = 3 * len(parts)

    def copies(ins, outs, sems):
        send, recv = sems
        x, y, c = _coords()
        mine = 2 * x + y
        out, arrive, k = [], [], 0
        for px, py in _other_chips(x, y):
            for plo, pcnt in parts:
                rows = pl.ds(plo, pcnt)
                out.append(_remote(ins[0].at[2 * px + py, rows, :], outs[0].at[mine, rows, :], send.at[k], recv.at[k], (px, py, c)))
                arrive.append(_remote(ins[0].at[2 * px + py, rows, :], outs[0].at[2 * px + py, rows, :], send.at[k], recv.at[k], (px, py, c)))
                k += 1
        return out, arrive

    def start(ins, outs, sems):
        for cp in copies(ins, outs, sems)[0]:
            cp.start()

    def finish(ins, outs, sems):
        out, arrive = copies(ins, outs, sems)
        for cp in arrive:
            cp.wait_recv()
        for cp in out:
            cp.wait_send()

    return _Stage([pair, dst], [jax.ShapeDtypeStruct(dst.shape, dst.dtype)], {1: 0},
                  [pltpu.SemaphoreType.DMA((nsem,)), pltpu.SemaphoreType.DMA((nsem,))], start, finish)


def _fuse(parts, ins, outs, alias):
    parts = [p for p in parts if p is not None]
    sems = [x for st, _, _ in parts for x in st.sems]

    def run(which):
        def go(i, o, s):
            refs, at = list(i) + list(o), 0
            for st, pi, po in parts:
                getattr(st, which)([refs[k] for k in pi], [refs[k] for k in po], s[at:at + len(st.sems)])
                at += len(st.sems)
        return go

    return _Stage(ins, [jax.ShapeDtypeStruct(o.shape, o.dtype) for o in outs], alias, sems, run("start"), run("finish"))


def _gather_stage(full, kind, new=None, relay=None, forward=None):
    return _fuse([(_spread_stage(full, kind, new), [0], [1]) if new else None,
                  (_relay_stage(full, kind, relay), [0], [1]) if relay else None,
                  (_spread_stage(full, kind, relay, home=True), [0], [1]) if relay else None,
                  (_forward_stage(full, kind, forward), [0], [1]) if forward else None], [full], [full], {0: 0})


def _small_stage(part, dst, rows):
    piece = pl.ds(*rows)

    def copies(ins, outs, sems):
        send, recv = sems
        x, y, c = _coords()
        me = (x, y, c)
        peers = [(1 - x if k & 4 else x, 1 - y if k & 2 else y, 1 - c if k & 1 else c) for k in range(1, N_DEV)]
        out = [_remote(ins[0].at[piece, :], outs[0].at[_lin(me), piece, :], send.at[k], recv.at[k], t) for k, t in enumerate(peers)]
        arrive = [_remote(ins[0].at[piece, :], outs[0].at[_lin(t), piece, :], send.at[k], recv.at[k], t) for k, t in enumerate(peers)]
        return out, arrive

    def start(ins, outs, sems):
        for cp in copies(ins, outs, sems)[0]:
            cp.start()

    def finish(ins, outs, sems):
        out, arrive = copies(ins, outs, sems)
        for cp in arrive:
            cp.wait_recv()
        for cp in out:
            cp.wait_send()

    return _Stage([part, dst], [jax.ShapeDtypeStruct(dst.shape, F32)], {1: 0},
                  [pltpu.SemaphoreType.DMA((N_DEV - 1,)), pltpu.SemaphoreType.DMA((N_DEV - 1,))], start, finish)


def _comm_only(name, stages):
    return _call(name, lambda: None, (1,), [], [], [], [], stages=stages, sequential=True)[1]


_GELU_C = math.sqrt(2.0 / math.pi)


def _gelu(x):
    return 0.5 * x * (1.0 + jnp.tanh(_GELU_C * (x + 0.044715 * x * x * x)))


def _gelu_grad(x):
    t = jnp.tanh(_GELU_C * (x + 0.044715 * x * x * x))
    return 0.5 * (1.0 + t) + 0.5 * x * (1.0 - t * t) * (_GELU_C * (1.0 + 3.0 * 0.044715 * x * x))


def _sigmoid(x):
    return 1.0 / (1.0 + jnp.exp(-x))


def _dot(a, b, mode):
    dims = {"nn": (((1,), (0,)), ((), ())), "nt": (((1,), (1,)), ((), ())), "tn": (((0,), (0,)), ((), ()))}[mode]
    return lax.dot_general(a.astype(BF16), b.astype(BF16), dims, preferred_element_type=F32)


def _matmul(name, a, b, mode, outs, epi=None, extras=(), colsums=(), tm=2048, tn=512, tk=2048, b_off=0, n=None, m_off=0, m=None, stages=()):
    if mode == "tn":
        kk, mfull = a.shape
    else:
        mfull, kk = a.shape
    m = mfull if m is None else m
    n = (b.shape[0] if mode == "nt" else b.shape[1]) if n is None else n
    tm, tk = _pick(tm, m, m_off), _pick(tk, kk)
    tn = _pick(tn, n, b_off, *[off for _, _, off in extras])
    boff, moff = b_off // tn, m_off // tm
    nm, nn_, nk = m // tm, n // tn, kk // tk
    col_major = bool(colsums)
    grid = (nn_, nm, nk) if col_major else (nm, nn_, nk)

    def imap(f):
        if col_major:
            return lambda g0, g1, k: f(g1, g0, k)
        return f

    a_spec = (pl.BlockSpec((tk, tm), imap(lambda i, j, k: (k, i + moff))) if mode == "tn"
              else pl.BlockSpec((tm, tk), imap(lambda i, j, k: (i + moff, k))))
    b_spec = (pl.BlockSpec((tn, tk), imap(lambda i, j, k: (j + boff, k))) if mode == "nt"
              else pl.BlockSpec((tk, tn), imap(lambda i, j, k: (k, j + boff))))
    in_specs, operands = [a_spec, b_spec], [a, b]
    for arr, kind, off in extras:
        o = off // tn
        if kind == "mn":
            in_specs.append(pl.BlockSpec((tm, tn), imap(lambda i, j, k, o=o: (i + moff, j + o))))
        else:
            in_specs.append(pl.BlockSpec((1, tn), imap(lambda i, j, k, o=o: (0, j + o))))
        operands.append(arr)
    out_shape = [jax.ShapeDtypeStruct((m, n), dt) for dt in outs] + [jax.ShapeDtypeStruct((1, n), F32) for _ in colsums]
    out_specs = ([pl.BlockSpec((tm, tn), imap(lambda i, j, k: (i, j))) for _ in outs]
                 + [pl.BlockSpec((1, tn), imap(lambda i, j, k: (0, j))) for _ in colsums])
    n_ex, n_out, n_cs = len(extras), len(outs), len(colsums)

    def body(*refs):
        a_ref, b_ref = refs[:2]
        ex_refs = refs[2:2 + n_ex]
        out_refs = refs[2 + n_ex:2 + n_ex + n_out]
        cs_refs = refs[2 + n_ex + n_out:2 + n_ex + n_out + n_cs]
        part = _dot(a_ref[...], b_ref[...], mode)

        def finish(acc):
            res = epi(acc, *[r[...] for r in ex_refs]) if epi is not None else (acc,)
            for r, v in zip(out_refs, res[:n_out]):
                r[...] = v.astype(r.dtype)
            if n_cs:
                @pl.when(pl.program_id(1) == 0)
                def _():
                    for r in cs_refs:
                        r[...] = jnp.zeros_like(r)

                for r, idx in zip(cs_refs, colsums):
                    r[...] += jnp.sum(res[idx], axis=0, keepdims=True)

        if nk == 1:
            finish(part)
        else:
            acc_ref = refs[-1]
            k = pl.program_id(2)

            @pl.when(k == 0)
            def _():
                acc_ref[...] = part

            @pl.when(k > 0)
            def _():
                acc_ref[...] += part

            @pl.when(k == nk - 1)
            def _():
                finish(acc_ref[...])

    sem = ("arbitrary", "arbitrary", "arbitrary") if col_major else ("parallel", "parallel", "arbitrary")
    return _call(name, body, grid, in_specs, out_specs, out_shape, operands,
                 scratch=[pltpu.VMEM((tm, tn), F32)] if nk > 1 else [], sem=sem, stages=stages)


def _project_shards(name, a, b, which, into=None, stages=()):
    m, kk = a.shape
    n = b.shape[1]
    tn = n // N_DEV
    o_spec = pl.BlockSpec((m, tn), lambda s, w_ref: (0, w_ref[s]))

    def body(a_ref, b_ref, *rest):
        rest[-1][...] = _dot(a_ref[...], b_ref[...], "nn")

    (out,), st = _call(name, body, (which.shape[0],),
                       [pl.BlockSpec((m, kk), lambda s, w_ref: (0, 0)), pl.BlockSpec((kk, tn), lambda s, w_ref: (0, w_ref[s]))]
                       + ([ANY] if into is not None else []), [o_spec], [jax.ShapeDtypeStruct((m, n), F32)],
                       [a, b] + ([into] if into is not None else []), sem=("arbitrary",), stages=stages, prefetch=which,
                       alias={2: 0} if into is not None else None)
    return out, st


def _row_spec(tr, c):
    return pl.BlockSpec((tr, c), lambda i: (i, 0))


def _fix_spec(shape):
    return pl.BlockSpec(shape, lambda *_: tuple(0 for _ in shape))


def _cast_bf16(name, x, tr=512):
    r, c = x.shape
    tr = _pick(tr, r)

    def body(x_ref, o_ref):
        o_ref[...] = x_ref[...].astype(BF16)

    return _call(name, body, (r // tr,), [_row_spec(tr, c)], [_row_spec(tr, c)], [jax.ShapeDtypeStruct((r, c), BF16)], [x],
                 sem=("parallel",))[0][0]


def _cast_into_place(name, w, kind, me, tr=512):
    r, c = w.shape
    tr = _pick(tr, r)
    nb = r // tr
    if kind == "col":
        o_spec = pl.BlockSpec((tr, c), lambda i, me_ref: (i, me_ref[0]))
        shape = (r, c * N_DEV)
    else:
        o_spec = pl.BlockSpec((tr, c), lambda i, me_ref: (me_ref[0] * nb + i, 0))
        shape = (r * N_DEV, c)

    def body(x_ref, o_ref):
        o_ref[...] = x_ref[...].astype(BF16)

    return _call(name, body, (nb,), [pl.BlockSpec((tr, c), lambda i, me_ref: (i, 0))], [o_spec], [jax.ShapeDtypeStruct(shape, BF16)], [w],
                 sem=("parallel",), prefetch=me)[0][0]


def _layer_norm_stats(x):
    mean = jnp.mean(x, axis=-1, keepdims=True)
    xc = x - mean
    var = jnp.mean(xc * xc, axis=-1, keepdims=True)
    rstd = lax.rsqrt(var + LN_EPS)
    return xc * rstd, rstd


def _layer_norm_bwd(dxhat, xhat, rstd):
    m1 = jnp.mean(dxhat, axis=-1, keepdims=True)
    m2 = jnp.mean(dxhat * xhat, axis=-1, keepdims=True)
    return rstd * (dxhat - m1 - xhat * m2)


def _ln1_fwd(pre1, g1, b1, tr=256, stages=()):
    s, d = pre1.shape
    tr = _pick(tr, s)

    def body(p_ref, g_ref, b_ref, xh_ref, rs_ref, h_ref):
        xhat, rstd = _layer_norm_stats(p_ref[...])
        xh_ref[...] = xhat
        rs_ref[...] = rstd
        h_ref[...] = (xhat * g_ref[...] + b_ref[...]).astype(BF16)

    return _call("ln1_fwd", body, (s // tr,), [_row_spec(tr, d), _fix_spec((1, d)), _fix_spec((1, d))],
                 [_row_spec(tr, d), _row_spec(tr, 1), _row_spec(tr, d)],
                 [jax.ShapeDtypeStruct((s, d), F32), jax.ShapeDtypeStruct((s, 1), F32), jax.ShapeDtypeStruct((s, d), BF16)],
                 [pre1, g1, b1], sem=("parallel",), stages=stages)


def _ln2_loss_bwd(ff, xhat1, g1, b1, g2, b2, target, tr=256):
    s, d = ff.shape
    tr = _pick(tr, s)

    def body(ff_ref, xh1_ref, g1_ref, b1_ref, g2_ref, b2_ref, t_ref, dp_ref, dpb_ref, dg_ref, db_ref, dbf_ref, loss_ref):
        @pl.when(pl.program_id(0) == 0)
        def _():
            dg_ref[...] = jnp.zeros_like(dg_ref)
            db_ref[...] = jnp.zeros_like(db_ref)
            dbf_ref[...] = jnp.zeros_like(dbf_ref)
            loss_ref[...] = jnp.zeros_like(loss_ref)

        h1 = xh1_ref[...] * g1_ref[...] + b1_ref[...]
        xhat, rstd = _layer_norm_stats(ALPHA * h1 + ff_ref[...])
        err = xhat * g2_ref[...] + b2_ref[...] - t_ref[...]
        row = jnp.mean(err * err, axis=-1, keepdims=True)
        loss_ref[...] += 0.5 * jnp.sum(row, axis=0, keepdims=True)
        dy = err / d
        dg_ref[...] += jnp.sum(dy * xhat, axis=0, keepdims=True)
        db_ref[...] += jnp.sum(dy, axis=0, keepdims=True)
        dpre = _layer_norm_bwd(dy * g2_ref[...], xhat, rstd)
        dbf_ref[...] += jnp.sum(dpre, axis=0, keepdims=True)
        dp_ref[...] = dpre
        dpb_ref[...] = dpre.astype(BF16)

    vec = _fix_spec((1, d))
    return _call("ln2_loss_bwd", body, (s // tr,), [_row_spec(tr, d), _row_spec(tr, d), vec, vec, vec, vec, _row_spec(tr, d)],
                 [_row_spec(tr, d), _row_spec(tr, d), vec, vec, vec, _fix_spec((1, 1))],
                 [jax.ShapeDtypeStruct((s, d), F32), jax.ShapeDtypeStruct((s, d), BF16)]
                 + [jax.ShapeDtypeStruct((1, d), F32)] * 3 + [jax.ShapeDtypeStruct((1, 1), F32)],
                 [ff, xhat1, g1, b1, g2, b2, target])[0]


def _ln1_bwd(dh1, xhat1, rstd1, g1, tr=256, stages=()):
    s, d = dh1.shape
    tr = _pick(tr, s)

    def body(dh_ref, xh_ref, rs_ref, g_ref, dp_ref, dpb_ref, dg_ref, db_ref):
        @pl.when(pl.program_id(0) == 0)
        def _():
            dg_ref[...] = jnp.zeros_like(dg_ref)
            db_ref[...] = jnp.zeros_like(db_ref)

        dh, xhat = dh_ref[...], xh_ref[...]
        dg_ref[...] += jnp.sum(dh * xhat, axis=0, keepdims=True)
        db_ref[...] += jnp.sum(dh, axis=0, keepdims=True)
        dpre = _layer_norm_bwd(dh * g_ref[...], xhat, rs_ref[...])
        dp_ref[...] = dpre
        dpb_ref[...] = dpre.astype(BF16)

    vec = _fix_spec((1, d))
    return _call("ln1_bwd", body, (s // tr,), [_row_spec(tr, d), _row_spec(tr, d), _row_spec(tr, 1), vec],
                 [_row_spec(tr, d), _row_spec(tr, d), vec, vec],
                 [jax.ShapeDtypeStruct((s, d), F32), jax.ShapeDtypeStruct((s, d), BF16)] + [jax.ShapeDtypeStruct((1, d), F32)] * 2,
                 [dh1, xhat1, rstd1, g1], stages=stages)


def _to_perm(x):
    return x.reshape(SEQ // N_SUB, N_SUB, -1).transpose(1, 0, 2).reshape(SEQ, -1)


def _from_perm(x):
    return x.reshape(N_SUB, SEQ // N_SUB, -1).transpose(1, 0, 2).reshape(SEQ, -1)


def _local_index(p):
    rho = np.arange(BLOCK)
    if p == 0:
        return 16 * (rho % 8) + rho // 8
    if p == 1:
        return 4 * (rho % 32) + rho // 32
    return rho


def _tile_view(x, p):
    c = x.shape[1]
    if p == 1:
        return x.reshape(4, 4, BLOCK, c)
    return x.reshape(N_SUB, BLOCK, c)


def _view_shape(c, p):
    return (4, 4, BLOCK, c) if p == 1 else (N_SUB, BLOCK, c)


def _tile_spec(p, width, col, shift=0):
    nblk = SEQ // DILATIONS[p] // BLOCK

    def blk(n):
        return jnp.clip(n + shift, 0, nblk - 1)

    if p == 0:
        return pl.BlockSpec((N_SUB, SUBLANES, width), lambda s, n: (0, blk(n), col))
    if p == 1:
        return pl.BlockSpec((4, None, 32, width), lambda s, n: (0, s, blk(n), col))
    return pl.BlockSpec((None, BLOCK, width), lambda s, n: (s, 0, col))


def _tile_grid(p):
    return ((1, 16), (4, 4), (16, 1))[p]


def _t5_bucket(n):
    max_exact = N_BUCKETS // 2
    nf = np.maximum(n, 1).astype(np.float32)
    large = max_exact + (np.log(nf / np.float32(max_exact)) / np.float32(math.log(MAX_DISTANCE / max_exact))
                         * np.float32(N_BUCKETS - max_exact)).astype(np.int32)
    large = np.minimum(large, N_BUCKETS - 1)
    return np.where(n < max_exact, n, large).astype(np.int32)


def _bucket_tables():
    tabs = np.zeros((3, 2, BLOCK, BLOCK), np.int32)
    for p, d in enumerate(DILATIONS):
        i = _local_index(p)
        diff = i[:, None] - i[None, :]
        tabs[p, 0] = np.where(diff <= 0, _t5_bucket((BLOCK + diff) * d), -1)
        tabs[p, 1] = np.where(diff >= 0, _t5_bucket(np.maximum(diff, 0) * d), -1)
    return tabs


def _bias_expand(rel_bias, buckets):
    nh = N_HEADS

    def body(rb_ref, bk_ref, o_ref):
        for w in range(2):
            bk = bk_ref[0, w]
            for h in range(nh):
                val = jnp.zeros((BLOCK, BLOCK), F32)
                for b in range(N_BUCKETS):
                    val = jnp.where(bk == b, rb_ref[b, h], val)
                o_ref[0, h, w] = jnp.where(bk < 0, NEG_INF, val)

    return _call("bias_expand", body, (3,),
                 [pl.BlockSpec(memory_space=pltpu.SMEM), pl.BlockSpec((1, 2, BLOCK, BLOCK), lambda p: (p, 0, 0, 0))],
                 [pl.BlockSpec((1, nh, 2, BLOCK, BLOCK), lambda p: (p, 0, 0, 0, 0))],
                 [jax.ShapeDtypeStruct((3, nh, 2, BLOCK, BLOCK), F32)], [rel_bias, buckets], sem=("parallel",))[0][0]


def _heads_to_lanes(cols):
    lane = lax.broadcasted_iota(I32, (BLOCK, LANES), 1)
    out = jnp.zeros((BLOCK, LANES), F32)
    for h, c in enumerate(cols):
        out = jnp.where(lane == h, c, out)
    return out


def _attn_fwd(qkv, bias, p, stages=()):
    d_a = _d_a()
    has_prev = SEQ // DILATIONS[p] // BLOCK > 1
    scale = HEAD_DIM ** -0.5
    view = _tile_view(qkv, p)

    width = 2 * BLOCK if has_prev else BLOCK

    def body(q_ref, kc_ref, kp_ref, vc_ref, vp_ref, b_ref, o_ref, l_ref, s_ref, p_ref):
        n = pl.program_id(1)
        q_all = q_ref[...].reshape(BLOCK, d_a).astype(BF16)
        k_all = kc_ref[...].reshape(BLOCK, d_a).astype(BF16)
        v_all = vc_ref[...].reshape(BLOCK, d_a).astype(BF16)
        if has_prev:
            k_all = jnp.concatenate([kp_ref[...].reshape(BLOCK, d_a).astype(BF16), k_all], axis=0)
            v_all = jnp.concatenate([vp_ref[...].reshape(BLOCK, d_a).astype(BF16), v_all], axis=0)
            no_prev = (lax.broadcasted_iota(I32, (BLOCK, width), 1) < BLOCK) & (n == 0)
        for h in range(N_HEADS):
            sl = slice(h * HEAD_DIM, (h + 1) * HEAD_DIM)
            s = _dot(q_all[:, sl], k_all[:, sl], "nt") * scale
            if has_prev:
                s = jnp.where(no_prev, NEG_INF, s + jnp.concatenate([b_ref[0, h, 0], b_ref[0, h, 1]], axis=1))
            else:
                s = s + b_ref[0, h, 1]
            s_ref[h] = s
        dens, lses = [], []
        for h in range(N_HEADS):
            s = s_ref[h]
            m = jnp.max(s, axis=-1, keepdims=True)
            pr = jnp.exp(s - m)
            den = jnp.sum(pr, axis=-1, keepdims=True)
            p_ref[h] = pr.astype(BF16)
            dens.append(den)
            lses.append(m + jnp.log(den))
        for h in range(N_HEADS):
            sl = slice(h * HEAD_DIM, (h + 1) * HEAD_DIM)
            o_ref[..., sl] = (_dot(p_ref[h], v_all[:, sl], "nn") / dens[h]).reshape(*o_ref.shape[:-1], HEAD_DIM)
        l_ref[...] = _heads_to_lanes(lses).reshape(l_ref.shape)

    (o, l), st = _call(
        f"attn_fwd{p}", body, _tile_grid(p),
        [_tile_spec(p, d_a, 0), _tile_spec(p, d_a, 1), _tile_spec(p, d_a, 1, -1), _tile_spec(p, d_a, 2), _tile_spec(p, d_a, 2, -1),
         pl.BlockSpec((1, N_HEADS, 2, BLOCK, BLOCK), lambda s, n: (p, 0, 0, 0, 0))],
        [_tile_spec(p, d_a, 0), _tile_spec(p, LANES, 0)],
        [jax.ShapeDtypeStruct(_view_shape(d_a, p), F32), jax.ShapeDtypeStruct(_view_shape(LANES, p), F32)],
        [view, view, view, view, view, bias], scratch=[pltpu.VMEM((N_HEADS, BLOCK, width), F32), pltpu.VMEM((N_HEADS, BLOCK, width), BF16)],
        sem=("parallel", "parallel"), stages=stages)
    return (o.reshape(SEQ, d_a), l.reshape(SEQ, LANES)), st


def _attn_combine(os_, ls_, tr=256, stages=()):
    d_a = _d_a()
    tr = _pick(tr, SEQ)

    def body(o0, o1, o2, l0, l1, l2, a_ref, ab_ref, lt_ref):
        l = [l0[...], l1[...], l2[...]]
        m = jnp.maximum(jnp.maximum(l[0], l[1]), l[2])
        w = [jnp.exp(x - m) for x in l]
        tot = w[0] + w[1] + w[2]
        lt_ref[...] = m + jnp.log(tot)
        w = [x / tot for x in w]
        for h in range(N_HEADS):
            sl = slice(h * HEAD_DIM, (h + 1) * HEAD_DIM)
            acc = w[0][:, h:h + 1] * o0[:, sl] + w[1][:, h:h + 1] * o1[:, sl] + w[2][:, h:h + 1] * o2[:, sl]
            a_ref[:, sl] = acc
            ab_ref[:, sl] = acc.astype(BF16)

    return _call("attn_combine", body, (SEQ // tr,), [_row_spec(tr, d_a)] * 3 + [_row_spec(tr, LANES)] * 3,
                 [_row_spec(tr, d_a), _row_spec(tr, d_a), _row_spec(tr, LANES)],
                 [jax.ShapeDtypeStruct((SEQ, d_a), F32), jax.ShapeDtypeStruct((SEQ, d_a), BF16), jax.ShapeDtypeStruct((SEQ, LANES), F32)],
                 [*os_, *ls_], sem=("parallel",), stages=stages)


def _attn_delta(dattn, attn, tr=256):
    d_a = _d_a()
    tr = _pick(tr, SEQ)

    def body(d_ref, a_ref, o_ref):
        prod = d_ref[...] * a_ref[...]
        lane = lax.broadcasted_iota(I32, (tr, LANES), 1)
        out = jnp.zeros((tr, LANES), F32)
        for h in range(N_HEADS):
            out = jnp.where(lane == h, jnp.sum(prod[:, h * HEAD_DIM:(h + 1) * HEAD_DIM], axis=-1, keepdims=True), out)
        o_ref[...] = out

    return _call("attn_delta", body, (SEQ // tr,), [_row_spec(tr, d_a)] * 2, [_row_spec(tr, LANES)],
                 [jax.ShapeDtypeStruct((SEQ, LANES), F32)], [dattn, attn], sem=("parallel",))[0][0]


def _attn_bwd(qkv, dattn, lse, delta, bias, p, stages=()):
    d_a = _d_a()
    nblk = SEQ // DILATIONS[p] // BLOCK
    has_next = nblk > 1
    scale = HEAD_DIM ** -0.5
    qv, dov, lv, tv = (_tile_view(x, p) for x in (qkv, dattn, lse, delta))

    rows = 2 * BLOCK if has_next else BLOCK

    def body(q_ref, qn_ref, k_ref, v_ref, do_ref, don_ref, l_ref, ln_ref, t_ref, tn_ref, b_ref, dq_ref, dk_ref, dv_ref, db_ref,
             carry_ref, s_ref, dp_ref, p_ref, ds_ref):
        j = pl.program_id(1)

        @pl.when((pl.program_id(0) == 0) & (j == 0))
        def _():
            db_ref[...] = jnp.zeros_like(db_ref)

        def both(cur, nxt, width, dtype):
            cur = cur[...].reshape(BLOCK, width).astype(dtype)
            return jnp.concatenate([cur, nxt[...].reshape(BLOCK, width).astype(dtype)], axis=0) if has_next else cur

        k_all = k_ref[...].reshape(BLOCK, d_a).astype(BF16)
        v_all = v_ref[...].reshape(BLOCK, d_a).astype(BF16)
        q_all, do_all = both(q_ref, qn_ref, d_a, BF16), both(do_ref, don_ref, d_a, BF16)
        l_all, t_all = both(l_ref, ln_ref, LANES, F32), both(t_ref, tn_ref, LANES, F32)
        if has_next:
            no_next = (lax.broadcasted_iota(I32, (rows, BLOCK), 0) >= BLOCK) & (j == nblk - 1)
        for h in range(N_HEADS):
            sl = slice(h * HEAD_DIM, (h + 1) * HEAD_DIM)
            s = _dot(q_all[:, sl], k_all[:, sl], "nt") * scale
            if has_next:
                s = jnp.where(no_next, NEG_INF, s + jnp.concatenate([b_ref[0, h, 1], b_ref[0, h, 0]], axis=0))
            else:
                s = s + b_ref[0, h, 1]
            s_ref[h] = s
            dp_ref[h] = _dot(do_all[:, sl], v_all[:, sl], "nt")
        for h in range(N_HEADS):
            pr = jnp.exp(s_ref[h] - l_all[:, h:h + 1])
            ds = pr * (dp_ref[h] - t_all[:, h:h + 1])
            db_ref[h, 1] += ds[:BLOCK]
            if has_next:
                db_ref[h, 0] += ds[BLOCK:]
            p_ref[h] = pr.astype(BF16)
            ds_ref[h] = ds.astype(BF16)
        for h in range(N_HEADS):
            sl = slice(h * HEAD_DIM, (h + 1) * HEAD_DIM)
            dq = _dot(ds_ref[h], k_all[:, sl], "nn") * scale
            mine = dq[:BLOCK]
            if has_next:
                mine = mine + jnp.where(j > 0, carry_ref[:, sl], 0.0)
            dq_ref[..., sl] = mine.reshape(*dq_ref.shape[:-1], HEAD_DIM)
            if has_next:
                carry_ref[:, sl] = dq[BLOCK:]
            dk_ref[..., sl] = (_dot(ds_ref[h], q_all[:, sl], "tn") * scale).reshape(*dk_ref.shape[:-1], HEAD_DIM)
            dv_ref[..., sl] = _dot(p_ref[h], do_all[:, sl], "tn").reshape(*dv_ref.shape[:-1], HEAD_DIM)

    def big(col, shift=0):
        return _tile_spec(p, d_a, col, shift)

    def small(shift=0):
        return _tile_spec(p, LANES, 0, shift)

    (dq, dk, dv, dbias), st = _call(
        f"attn_bwd{p}", body, _tile_grid(p),
        [big(0), big(0, 1), big(1), big(2), big(0), big(0, 1), small(), small(1), small(), small(1),
         pl.BlockSpec((1, N_HEADS, 2, BLOCK, BLOCK), lambda s, n: (p, 0, 0, 0, 0))],
        [big(0), big(0), big(0), pl.BlockSpec((N_HEADS, 2, BLOCK, BLOCK), lambda s, n: (0, 0, 0, 0))],
        [jax.ShapeDtypeStruct(_view_shape(d_a, p), F32)] * 3 + [jax.ShapeDtypeStruct((N_HEADS, 2, BLOCK, BLOCK), F32)],
        [qv, qv, qv, qv, dov, dov, lv, lv, tv, tv, bias],
        scratch=[pltpu.VMEM((BLOCK, d_a), F32), pltpu.VMEM((N_HEADS, rows, BLOCK), F32), pltpu.VMEM((N_HEADS, rows, BLOCK), F32),
                 pltpu.VMEM((N_HEADS, rows, BLOCK), BF16), pltpu.VMEM((N_HEADS, rows, BLOCK), BF16)], stages=stages)
    return (dq.reshape(SEQ, d_a), dk.reshape(SEQ, d_a), dv.reshape(SEQ, d_a), dbias), st


def _rel_bias_grad(dbias, buckets):
    nh = N_HEADS

    def body(d0, d1, d2, bk_ref, o_ref, t_ref):
        ds = (d0, d1, d2)

        def per_bucket(b, carry):
            for h in range(nh):
                acc = jnp.zeros((BLOCK, BLOCK), F32)
                for p in range(3):
                    for w in range(2):
                        acc = acc + jnp.where(bk_ref[p, w] == b, ds[p][h, w], 0.0)
                t_ref[pl.ds(b * nh + h, 1), :] = jnp.sum(acc, axis=0, keepdims=True)
            return carry

        lax.fori_loop(0, N_BUCKETS, per_bucket, 0)
        o_ref[...] = jnp.sum(t_ref[...], axis=-1, keepdims=True)

    return _call("rel_bias_grad", body, (1,), [_fix_spec((nh, 2, BLOCK, BLOCK))] * 3 + [_fix_spec((3, 2, BLOCK, BLOCK))],
                 [_fix_spec((N_BUCKETS * nh, 1))], [jax.ShapeDtypeStruct((N_BUCKETS * nh, 1), F32)], [*dbias, buckets],
                 scratch=[pltpu.VMEM((N_BUCKETS * nh, LANES), F32)])[0][0]


def _gmlp_fwd(rest, col0, gain, bias, ws, bs, causal, stages=()):
    d_b = _d_b()

    def body(u_ref, v_ref, g_ref, b_ref, ws_ref, bs_ref, c_ref, o_ref):
        u = u_ref[...].reshape(BLOCK, d_b)
        xhat, _ = _layer_norm_stats(_gelu(v_ref[...].reshape(BLOCK, d_b)))
        vn = (xhat * g_ref[...] + b_ref[...]).astype(BF16)
        outs = []
        for g in range(N_GROUPS):
            sl = slice(g * BLOCK, (g + 1) * BLOCK)
            w = jnp.where(c_ref[...] > 0, ws_ref[g], 0.0)
            z = _dot(w, vn[:, sl], "nn") + bs_ref[:, g:g + 1]
            outs.append(_gelu(u[:, sl]) * z)
        o_ref[...] = jnp.concatenate(outs, axis=-1).reshape(o_ref.shape)

    (out,), st = _call(
        "gmlp_fwd", body, (1, SEQ // BLOCK),
        [_tile_spec(0, d_b, col0), _tile_spec(0, d_b, col0 + 1), _fix_spec((1, d_b)), _fix_spec((1, d_b)),
         _fix_spec((N_GROUPS, BLOCK, BLOCK)), _fix_spec((BLOCK, N_GROUPS)), _fix_spec((BLOCK, BLOCK))],
        [_tile_spec(0, d_b, 0)], [jax.ShapeDtypeStruct(_view_shape(d_b, 0), F32)],
        [_tile_view(rest, 0), _tile_view(rest, 0), gain, bias, ws, bs, causal], sem=("parallel", "parallel"), stages=stages)
    return out.reshape(SEQ, d_b), st


def _gmlp_bwd(rest, col0, dgmlp, gain, bias, ws, bs, causal, stages=()):
    d_b = _d_b()
    nchunk = SEQ // BLOCK

    def body(u_ref, v_ref, dg_ref, g_ref, b_ref, ws_ref, bs_ref, c_ref, du_ref, dv_ref, dws_ref, dbs_ref, dgain_ref, dbias_ref):
        c = pl.program_id(1)

        @pl.when(c == 0)
        def _():
            dws_ref[...] = jnp.zeros_like(dws_ref)
            dbs_ref[...] = jnp.zeros_like(dbs_ref)
            dgain_ref[...] = jnp.zeros_like(dgain_ref)
            dbias_ref[...] = jnp.zeros_like(dbias_ref)

        u = u_ref[...].reshape(BLOCK, d_b)
        v = v_ref[...].reshape(BLOCK, d_b)
        dgm = dg_ref[...].reshape(BLOCK, d_b)
        xhat, rstd = _layer_norm_stats(_gelu(v))
        vn = (xhat * g_ref[...] + b_ref[...]).astype(BF16)
        lane = lax.broadcasted_iota(I32, (BLOCK, LANES), 1)
        dus, dvns = [], []
        dbs = dbs_ref[...]
        for g in range(N_GROUPS):
            sl = slice(g * BLOCK, (g + 1) * BLOCK)
            w = jnp.where(c_ref[...] > 0, ws_ref[g], 0.0).astype(BF16)
            z = _dot(w, vn[:, sl], "nn") + bs_ref[:, g:g + 1]
            dz = dgm[:, sl] * _gelu(u[:, sl])
            dus.append(dgm[:, sl] * z * _gelu_grad(u[:, sl]))
            dws_ref[g] += _dot(dz, vn[:, sl], "nt")
            dbs = dbs + jnp.where(lane == g, jnp.sum(dz, axis=-1, keepdims=True), 0.0)
            dvns.append(_dot(w, dz, "tn"))
        dbs_ref[...] = dbs
        dvn = jnp.concatenate(dvns, axis=-1)
        dgain_ref[...] += jnp.sum(dvn * xhat, axis=0, keepdims=True)
        dbias_ref[...] += jnp.sum(dvn, axis=0, keepdims=True)
        dvg = _layer_norm_bwd(dvn * g_ref[...], xhat, rstd)
        du_ref[...] = jnp.concatenate(dus, axis=-1).reshape(du_ref.shape)
        dv_ref[...] = (dvg * _gelu_grad(v)).reshape(dv_ref.shape)

        @pl.when(c == nchunk - 1)
        def _():
            for g in range(N_GROUPS):
                dws_ref[g] = jnp.where(c_ref[...] > 0, dws_ref[g], 0.0)

    (du, dv, dws, dbs, dgain, dbias), st = _call(
        "gmlp_bwd", body, (1, nchunk),
        [_tile_spec(0, d_b, col0), _tile_spec(0, d_b, col0 + 1), _tile_spec(0, d_b, 0), _fix_spec((1, d_b)), _fix_spec((1, d_b)),
         _fix_spec((N_GROUPS, BLOCK, BLOCK)), _fix_spec((BLOCK, N_GROUPS)), _fix_spec((BLOCK, BLOCK))],
        [_tile_spec(0, d_b, 0), _tile_spec(0, d_b, 0), _fix_spec((N_GROUPS, BLOCK, BLOCK)), _fix_spec((BLOCK, LANES)),
         _fix_spec((1, d_b)), _fix_spec((1, d_b))],
        [jax.ShapeDtypeStruct(_view_shape(d_b, 0), F32)] * 2
        + [jax.ShapeDtypeStruct((N_GROUPS, BLOCK, BLOCK), F32), jax.ShapeDtypeStruct((BLOCK, LANES), F32)]
        + [jax.ShapeDtypeStruct((1, d_b), F32)] * 2,
        [_tile_view(rest, 0), _tile_view(rest, 0), _tile_view(dgmlp, 0), gain, bias, ws, bs, causal], stages=stages)
    return (du.reshape(SEQ, d_b), dv.reshape(SEQ, d_b), dws, dbs, dgain, dbias), st


def _assemble_dproj(dqkv, du, dv, dga, dgb, tr=128, stages=()):
    d_a, d_b, d_in = _d_a(), _d_b(), _d_in()
    tr = _pick(tr, SEQ)

    def body(*refs):
        att, (du_ref, dv_ref, dga_ref, dgb_ref, o_ref) = refs[:9], refs[9:]
        for i in range(3):
            o_ref[:, i * d_a:(i + 1) * d_a] = (att[3 * i][...] + att[3 * i + 1][...] + att[3 * i + 2][...]).astype(BF16)
        o_ref[:, 3 * d_a:3 * d_a + d_b] = du_ref[...].astype(BF16)
        o_ref[:, 3 * d_a + d_b:3 * d_a + 2 * d_b] = dv_ref[...].astype(BF16)
        o_ref[:, 3 * d_a + 2 * d_b:3 * d_a + 2 * d_b + D_MODEL] = dga_ref[...]
        o_ref[:, 3 * d_a + 2 * d_b + D_MODEL:] = dgb_ref[...]

    return _call("assemble_dproj", body, (SEQ // tr,), [_row_spec(tr, d_a)] * 9 + [_row_spec(tr, d_b)] * 2 + [_row_spec(tr, D_MODEL)] * 2,
                 [_row_spec(tr, d_in)], [jax.ShapeDtypeStruct((SEQ, d_in), BF16)], [*dqkv, du, dv, dga, dgb], sem=("parallel",), stages=stages)


def _dw(name, a, b, kind, core, mine, add=None, tn=1152, half=None, stages=()):
    s, m = a.shape
    n = b.shape[1]
    if half is not None:
        m //= 2
    rs, cs = (m, n // N_DEV) if kind == "col" else (m // N_DEV, n)
    tn = _pick(tn if kind == "col" else 512, cs)
    nj = cs // tn

    def shard(q, c_ref):
        return 2 * q + (c_ref[0] if mine else 1 - c_ref[0])

    if kind == "col":
        a_spec = pl.BlockSpec((s, m), lambda q, j, c_ref: (0, half or 0))
        b_spec = pl.BlockSpec((s, tn), lambda q, j, c_ref: (0, shard(q, c_ref) * nj + j))
    else:
        a_spec = pl.BlockSpec((s, rs), lambda q, j, c_ref: (0, shard(q, c_ref)))
        b_spec = pl.BlockSpec((s, tn), lambda q, j, c_ref: (0, j))
    o_spec = pl.BlockSpec((None, rs, tn), lambda q, j, c_ref: (q, 0, j))

    def body(a_ref, b_ref, *rest):
        acc = _dot(a_ref[...], b_ref[...], "tn")
        if add is not None:
            acc = acc + rest[0][...].astype(F32)
        rest[-1][...] = acc.astype(BF16)

    (out,), st = _call(name, body, (N_CHIPS, nj), [a_spec, b_spec] + ([o_spec] if add is not None else []), [o_spec],
                       [jax.ShapeDtypeStruct((N_CHIPS, rs, cs), BF16)], [a, b] + ([add] if add is not None else []),
                       sem=("parallel", "parallel"), stages=stages, prefetch=core)
    return out, st


def _adamw(w, g, m, v):
    m = ADAM_B1 * m + (1.0 - ADAM_B1) * g
    v = ADAM_B2 * v + (1.0 - ADAM_B2) * (g * g)
    m_hat = m / (1.0 - ADAM_B1 ** ADAM_STEP)
    v_hat = v / (1.0 - ADAM_B2 ** ADAM_STEP)
    delta = -ADAM_LR * (m_hat / (jnp.sqrt(v_hat) + ADAM_EPS) + ADAM_WD * w)
    return delta, m, v


def _adam_shard(name, pair, chip_sums, chip, w, m, v, tr=256, rows=None, into=None, stages=()):
    rs, cs = w.shape
    lo, cnt = rows or (0, rs)
    assert pair.shape[1] == cnt
    tr = _pick(tr, cnt, lo)
    first = lo // tr

    def body(chip_ref, own_ref, *refs):
        slots, (w_ref, m_ref, v_ref), (g_ref, d_ref, nm_ref, nv_ref) = refs[:N_CHIPS], refs[N_CHIPS:N_CHIPS + 3], refs[-4:]
        g = None
        for q in range(N_CHIPS):
            term = jnp.where(chip_ref[0] == q, own_ref[...], slots[q][...]).astype(F32)
            g = term if g is None else g + term
        d, nm, nv = _adamw(w_ref[...], g, m_ref[...], v_ref[...])
        g_ref[...], d_ref[...], nm_ref[...], nv_ref[...] = g, d, nm, nv

    def slot(q):
        return pl.BlockSpec((None, tr, cs), lambda i, c_ref: (jnp.where(c_ref[0] == q, (q + 1) % N_CHIPS, q), i, 0))

    spec = pl.BlockSpec((tr, cs), lambda i, c_ref: (i + first, 0))
    n_in = 1 + N_CHIPS + 3
    return _call(name, body, (cnt // tr,),
                 [pl.BlockSpec((None, tr, cs), lambda i, c_ref: (c_ref[0], i, 0))] + [slot(q) for q in range(N_CHIPS)]
                 + [spec, spec, spec] + ([ANY] * 4 if into is not None else []),
                 [spec] * 4, [jax.ShapeDtypeStruct((rs, cs), F32)] * 4, [pair] + [chip_sums] * N_CHIPS + [w, m, v] + list(into or ()),
                 sem=("parallel",), stages=stages, prefetch=chip, shown=True,
                 alias={n_in + k: k for k in range(4)} if into is not None else None)


def _adam_small(part, parts, me, w, m, v):
    rows = w.shape[0]

    def body(me_ref, own_ref, *refs):
        slots, (w_ref, m_ref, v_ref, g_ref, d_ref, nm_ref, nv_ref) = refs[:N_DEV], refs[N_DEV:]
        g = None
        for j in range(N_DEV):
            term = jnp.where(me_ref[0] == j, own_ref[...], slots[j][...])
            g = term if g is None else g + term
        d, nm, nv = _adamw(w_ref[...], g, m_ref[...], v_ref[...])
        g_ref[...], d_ref[...], nm_ref[...], nv_ref[...] = g, d, nm, nv

    def slot(j):
        return pl.BlockSpec((None, rows, LANES), lambda i, me_ref: (jnp.where(me_ref[0] == j, (j + 1) % N_DEV, j), 0, 0))

    spec = _fix_spec((rows, LANES))
    return _call("adam_small", body, (1,), [spec] + [slot(j) for j in range(N_DEV)] + [spec, spec, spec], [spec] * 4,
                 [jax.ShapeDtypeStruct((rows, LANES), F32)] * 4, [part] + [parts] * N_DEV + [w, m, v], prefetch=me, shown=True)[0]


def _small_sizes():
    d_b = _d_b()
    return (("loss", 1), ("rel_bias", N_BUCKETS * N_HEADS), ("ln_v_gain", d_b), ("ln_v_bias", d_b),
            ("w_spatial", N_GROUPS * BLOCK * BLOCK), ("b_spatial", N_GROUPS * BLOCK), ("ln1_gain", D_MODEL), ("ln1_bias", D_MODEL),
            ("b_ff1", D_FF), ("b_ff2", D_MODEL), ("ln2_gain", D_MODEL), ("ln2_bias", D_MODEL))


def _pack(vals):
    pieces = []
    for name, size in _small_sizes():
        flat = vals[name].reshape(-1).astype(F32)
        padded = -(-size // (SUBLANES * LANES)) * SUBLANES * LANES
        pieces.append(jnp.pad(flat, (0, padded - size)).reshape(-1, LANES))
    return jnp.concatenate(pieces, axis=0)


def _unpack(buf):
    out, row = {}, 0
    for name, size in _small_sizes():
        rows = -(-size // (SUBLANES * LANES)) * SUBLANES
        out[name] = buf[row:row + rows].reshape(-1)[:size]
        row += rows
    return out


def kernel(x, w_in, rel_bias, ln_v_gain, ln_v_bias, w_spatial, b_spatial, w_proj_a, w_proj_b, w_out, ln1_gain, ln1_bias, w_ff1, b_ff1, w_ff2, b_ff2, ln2_gain, ln2_bias, loss_target, m_w_in, m_rel_bias, m_ln_v_gain, m_ln_v_bias, m_w_spatial, m_b_spatial, m_w_proj_a, m_w_proj_b, m_w_out, m_ln1_gain, m_ln1_bias, m_w_ff1, m_b_ff1, m_w_ff2, m_b_ff2, m_ln2_gain, m_ln2_bias, v_w_in, v_rel_bias, v_ln_v_gain, v_ln_v_bias, v_w_spatial, v_b_spatial, v_w_proj_a, v_w_proj_b, v_w_out, v_ln1_gain, v_ln1_bias, v_w_ff1, v_b_ff1, v_w_ff2, v_b_ff2, v_ln2_gain, v_ln2_bias):
    d_a, d_b, d_in = _d_a(), _d_b(), _d_in()
    weights = dict(w_in=w_in, rel_bias=rel_bias, ln_v_gain=ln_v_gain, ln_v_bias=ln_v_bias, w_spatial=w_spatial, b_spatial=b_spatial,
                   w_proj_a=w_proj_a, w_proj_b=w_proj_b, w_out=w_out, ln1_gain=ln1_gain, ln1_bias=ln1_bias, w_ff1=w_ff1, b_ff1=b_ff1,
                   w_ff2=w_ff2, b_ff2=b_ff2, ln2_gain=ln2_gain, ln2_bias=ln2_bias)
    mom1 = dict(w_in=m_w_in, rel_bias=m_rel_bias, ln_v_gain=m_ln_v_gain, ln_v_bias=m_ln_v_bias, w_spatial=m_w_spatial,
                b_spatial=m_b_spatial, w_proj_a=m_w_proj_a, w_proj_b=m_w_proj_b, w_out=m_w_out, ln1_gain=m_ln1_gain,
                ln1_bias=m_ln1_bias, w_ff1=m_w_ff1, b_ff1=m_b_ff1, w_ff2=m_w_ff2, b_ff2=m_b_ff2, ln2_gain=m_ln2_gain, ln2_bias=m_ln2_bias)
    mom2 = dict(w_in=v_w_in, rel_bias=v_rel_bias, ln_v_gain=v_ln_v_gain, ln_v_bias=v_ln_v_bias, w_spatial=v_w_spatial,
                b_spatial=v_b_spatial, w_proj_a=v_w_proj_a, w_proj_b=v_w_proj_b, w_out=v_w_out, ln1_gain=v_ln1_gain,
                ln1_bias=v_ln1_bias, w_ff1=v_w_ff1, b_ff1=v_b_ff1, w_ff2=v_w_ff2, b_ff2=v_b_ff2, ln2_gain=v_ln2_gain, ln2_bias=v_ln2_bias)

    mx, my, mc = _coords()
    me = (4 * mx + 2 * my + mc).astype(I32).reshape(1)
    chip = (2 * mx + my).astype(I32).reshape(1)
    full = {n: _cast_into_place(f"cast_{n}", weights[n][0], KINDS[n], me) for n in KINDS}
    sent = {n: (0, 0, 0) for n in KINDS}

    def keep(table, n):
        def store(outs):
            table[n] = outs[0]
        return store

    def gathering(**new):
        stages = []
        for n in KINDS:
            out, relayed, passed = sent[n]
            units = new.get(n, 0)
            if units or relayed < out or passed < relayed:
                st = _gather_stage(full[n], KINDS[n], (out, units) if units else None,
                                   (relayed, out - relayed) if relayed < out else None, (passed, relayed - passed) if passed < relayed else None)
                st.store = keep(full, n)
                sent[n] = (out + units, out, relayed)
                stages.append(st)
        return stages

    def settle(stages, outs):
        for st, o in zip(stages, outs):
            st.store(o)

    def alone(name, stages):
        settle(stages, _comm_only(name, stages))

    def here(n):
        assert sent[n] == (16, 16, 16), (n, sent[n])
        return full[n]

    alone("gather_w_in_near", gathering(w_in=16))
    alone("gather_w_in_relay", gathering())
    alone("gather_w_in_sibling", gathering(w_proj_a=16, w_proj_b=16))

    xs = _to_perm(x[0])
    target = _to_perm(loss_target[0])
    xb = _cast_bf16("cast_x", xs)
    g8 = BLOCK // N_SUB
    ws_t = w_spatial[0].reshape(N_GROUPS, g8, N_SUB, g8, N_SUB).transpose(0, 2, 1, 4, 3).reshape(N_GROUPS, BLOCK, BLOCK)
    bs_t = b_spatial[0].reshape(N_GROUPS, g8, N_SUB).transpose(2, 1, 0).reshape(BLOCK, N_GROUPS)
    idx = _local_index(0)
    causal = jnp.asarray((idx[:, None] >= idx[None, :]).astype(np.float32))
    buckets = jnp.asarray(_bucket_tables())
    bias = _bias_expand(rel_bias, buckets)

    hosted = gathering(w_out=8, w_ff1=1)
    (qkv,), st = _matmul("proj_qkv", xb, here("w_in"), "nn", [F32], n=3 * d_a, stages=hosted)
    settle(hosted, st)
    hosted = gathering(w_out=8, w_ff1=6)
    (rest,), st = _matmul("proj_rest", xb, here("w_in"), "nn", [F32], b_off=3 * d_a, n=d_in - 3 * d_a, stages=hosted)
    settle(hosted, st)
    fwd = []
    for p in range(3):
        hosted = gathering(**({"w_ff1": 2}, {"w_ff1": 5}, {"w_ff1": 2, "w_ff2": 1})[p])
        res, st = _attn_fwd(qkv, bias, p, stages=hosted)
        settle(hosted, st)
        fwd.append(res)
    hosted = gathering(w_ff2=1)
    (attn, attn_b, lse), st = _attn_combine([o for o, _ in fwd], [l for _, l in fwd], stages=hosted)
    settle(hosted, st)
    hosted = gathering(w_ff2=2)
    gmlp, st = _gmlp_fwd(rest, 0, ln_v_gain, ln_v_bias, ws_t, bs_t, causal, stages=hosted)
    settle(hosted, st)
    hosted = gathering(w_ff2=3)
    (ya,), st = _matmul("proj_a", attn_b, here("w_proj_a"), "nn", [F32], stages=hosted)
    settle(hosted, st)
    gate_a, gate_b = 2 * d_b, 2 * d_b + D_MODEL

    def merge(acc, ya_, ga, gb):
        return acc, _sigmoid(ga) * ya_ + _sigmoid(gb) * acc

    hosted = gathering(w_ff2=5)
    (yb, merged), st = _matmul("proj_b_merge", gmlp, here("w_proj_b"), "nn", [F32, BF16], merge,
                               [(ya, "mn", 0), (rest, "mn", gate_a), (rest, "mn", gate_b)], tn=256, stages=hosted)
    settle(hosted, st)
    hosted = gathering(w_ff2=3)
    (pre1,), st = _matmul("out_proj", merged, here("w_out"), "nn", [F32], lambda acc, x_: (ALPHA * x_ + acc,), [(xs, "mn", 0)], stages=hosted)
    settle(hosted, st)
    hosted = gathering(w_ff2=1)
    (xhat1, rstd1, h1b), st = _ln1_fwd(pre1, ln1_gain, ln1_bias, stages=hosted)
    settle(hosted, st)

    def relu2(acc, b_):
        r = jnp.maximum(acc + b_, 0.0)
        return r, r * r

    hosted = gathering()
    (relu, fb), st = _matmul("ff1", h1b, here("w_ff1"), "nn", [F32, BF16], relu2, [(b_ff1, "row", 0)], stages=hosted)
    settle(hosted, st)
    alone("gather_w_ff2_sibling", gathering())
    (ff,), _ = _matmul("ff2", fb, here("w_ff2"), "nn", [F32], lambda acc, b_: (acc + b_,), [(b_ff2, "row", 0)], tn=1024, tk=1024)

    core = lax.axis_index("c").astype(I32).reshape(1)
    factors, theirs, sib, pair, chips, reduced = {}, {}, {}, {}, {}, {}

    def grad_for_sibling(n, a, b, stages=(), half=None, of=None):
        factors[n] = (a, b, KINDS[of or n], half)
        theirs[n], outs = _dw(f"dw_{n}_sibling", a, b, KINDS[of or n], core, False, half=half, stages=stages)
        settle(stages, outs)

    def to_sibling(n):
        st = _to_sibling_stage(theirs[n])
        st.store = keep(sib, n)
        return st

    def grad_own(n, stages=()):
        a, b, kind, half = factors[n]
        pair[n], outs = _dw(f"dw_{n}_own", a, b, kind, core, True, add=sib[n], half=half, stages=stages)
        settle(stages, outs)
        chips[n] = lax.empty(pair[n].shape, BF16)
        reduced[n] = 0

    def reducing(**new):
        stages = []
        for n, units in new.items():
            st = _to_chips_stage(pair[n], chips[n], (reduced[n], units))
            st.store = keep(chips, n)
            reduced[n] += units
            stages.append(st)
        return stages

    def summed(n):
        assert reduced[n] == 16, (n, reduced[n])
        return chips[n]

    dpre2, dpre2b, g_ln2_gain, g_ln2_bias, g_b_ff2, loss_part = _ln2_loss_bwd(ff, xhat1, ln1_gain, ln1_bias, ln2_gain, ln2_bias, target)
    grad_for_sibling("w_ff2", fb, dpre2b)

    def relu2_bwd(acc, r):
        da = acc * (2.0 * r)
        return da, da

    hosted = [to_sibling("w_ff2")]
    (dab, g_b_ff1), st = _matmul("d_ff1", dpre2b, here("w_ff2"), "nt", [BF16], relu2_bwd, [(relu, "mn", 0)], colsums=(1,), stages=hosted)
    settle(hosted, st)
    grad_own("w_ff2")
    grad_for_sibling("w_ff1", h1b, dab, reducing(w_ff2=3))
    hosted = reducing(w_ff2=8) + [to_sibling("w_ff1")]
    (dh1,), st = _matmul("d_h1", dab, here("w_ff1"), "nt", [F32], lambda acc, d_: (acc + ALPHA * d_,), [(dpre2, "mn", 0)], stages=hosted)
    settle(hosted, st)
    grad_own("w_ff1", reducing(w_ff2=4))
    hosted = reducing(w_ff2=1)
    (dpre1, dpre1b, g_ln1_gain, g_ln1_bias), st = _ln1_bwd(dh1, xhat1, rstd1, ln1_gain, stages=hosted)
    settle(hosted, st)
    grad_for_sibling("w_out", merged, dpre1b, reducing(w_ff1=1))

    def merge_bwd(acc, ga, gb, ya_, yb_):
        sa, sb = _sigmoid(ga), _sigmoid(gb)
        return acc * sa, acc * sb, acc * ya_ * (sa * (1.0 - sa)), acc * yb_ * (sb * (1.0 - sb))

    hosted = reducing(w_ff1=6) + [to_sibling("w_out")]
    (dya, dyb, dga, dgb), st = _matmul("d_merge", dpre1b, here("w_out"), "nt", [BF16] * 4, merge_bwd,
                                       [(rest, "mn", gate_a), (rest, "mn", gate_b), (ya, "mn", 0), (yb, "mn", 0)], tn=256, stages=hosted)
    settle(hosted, st)
    grad_own("w_out", reducing(w_ff1=1))
    grad_for_sibling("w_proj_a", attn_b, dya)
    grad_for_sibling("w_proj_b", gmlp, dyb)
    hosted = reducing(w_ff1=1) + [to_sibling("w_proj_a"), to_sibling("w_proj_b")]
    (dattn,), st = _matmul("d_attn", dya, here("w_proj_a"), "nt", [F32], stages=hosted)
    settle(hosted, st)
    grad_own("w_proj_a")
    grad_own("w_proj_b")
    hosted = reducing(w_ff1=1)
    (dgmlp,), st = _matmul("d_gmlp", dyb, here("w_proj_b"), "nt", [F32], stages=hosted)
    settle(hosted, st)
    hosted = reducing(w_ff1=2)
    (du, dvb, dws_t, dbs_t, g_lnv_gain, g_lnv_bias), st = _gmlp_bwd(rest, 0, dgmlp, ln_v_gain, ln_v_bias, ws_t, bs_t, causal, stages=hosted)
    settle(hosted, st)
    delta = _attn_delta(dattn, attn)
    bwd = []
    for p in range(3):
        hosted = reducing(**({"w_ff1": 3}, {"w_ff1": 1, "w_out": 8}, {"w_out": 8, "w_proj_a": 8})[p])
        res, st = _attn_bwd(qkv, dattn, lse, delta, bias, p, stages=hosted)
        settle(hosted, st)
        bwd.append(res)
    g_rel_bias = _rel_bias_grad([b[3] for b in bwd], buckets)
    hosted = reducing(w_proj_a=8, w_proj_b=16)
    (dproj,), st = _assemble_dproj([b[i] for i in range(3) for b in bwd], du, dvb, dga, dgb, stages=hosted)
    settle(hosted, st)

    g_w_spatial = dws_t.reshape(N_GROUPS, N_SUB, g8, N_SUB, g8).transpose(0, 2, 1, 4, 3)
    g_b_spatial = dbs_t[:, :N_GROUPS].reshape(N_SUB, g8, N_GROUPS).transpose(2, 1, 0)
    part = _pack(dict(loss=loss_part, rel_bias=g_rel_bias, ln_v_gain=g_lnv_gain, ln_v_bias=g_lnv_bias, w_spatial=g_w_spatial,
                      b_spatial=g_b_spatial, ln1_gain=g_ln1_gain, ln1_bias=g_ln1_bias, b_ff1=g_b_ff1, b_ff2=g_b_ff2,
                      ln2_gain=g_ln2_gain, ln2_bias=g_ln2_bias))
    sib["small"] = lax.empty((N_DEV, *part.shape), F32)
    cut = part.shape[0] // 2 // SUBLANES * SUBLANES

    def small(rows):
        st = _small_stage(part, sib["small"], rows)
        st.store = keep(sib, "small")
        return st

    grad_for_sibling("w_in_top", xb, dproj, [small((0, cut))], half=0, of="w_in")
    grad_for_sibling("w_in_bot", xb, dproj, [small((cut, part.shape[0] - cut)), to_sibling("w_in_top")], half=1, of="w_in")
    parts = sib["small"]
    grad_own("w_in_top", [to_sibling("w_in_bot")])
    grad_own("w_in_bot", reducing(w_in_top=4))

    def add_residual(acc, d_):
        return (acc + ALPHA * d_,)

    hosted = reducing(w_in_top=12, w_in_bot=7)
    (dx,), st = _matmul("d_x", dproj, here("w_in"), "nt", [F32], add_residual, [(dpre1, "mn", 0)], tn=512, tk=3072, stages=hosted)
    settle(hosted, st)
    grad_x = _from_perm(dx)[None]

    out_g, out_d, out_m, out_v = {}, {}, {}, {}
    for n, units in (("w_out", 3), ("w_proj_a", 3), ("w_proj_b", 3), ("w_ff2", 0), ("w_ff1", 0)):
        hosted = reducing(w_in_bot=units) if units else []
        (g, d, nm, nv), st = _adam_shard(f"adam_{n}", pair[n], summed(n), chip, weights[n][0], mom1[n][0], mom2[n][0], stages=hosted)
        settle(hosted, st)
        out_g[n], out_d[n], out_m[n], out_v[n] = g[None], d[None], nm[None], nv[None]
    rows = weights["w_in"].shape[1] // 2
    done = None
    for i, n in enumerate(("w_in_top", "w_in_bot")):
        done, _ = _adam_shard(f"adam_{n}", pair[n], summed(n), chip, weights["w_in"][0], mom1["w_in"][0], mom2["w_in"][0],
                              rows=(i * rows, rows), into=done)
    out_g["w_in"], out_d["w_in"], out_m["w_in"], out_v["w_in"] = (t[None] for t in done)

    zero = jnp.zeros((1,), F32)
    sg, sd, sm, sv = (_unpack(b) for b in _adam_small(
        part, parts, me, _pack({**weights, "loss": zero}), _pack({**mom1, "loss": zero}), _pack({**mom2, "loss": zero})))
    for n in WEIGHT_ORDER:
        if n not in KINDS:
            shape = weights[n].shape
            out_g[n], out_d[n], out_m[n], out_v[n] = (t[n].reshape(shape) for t in (sg, sd, sm, sv))
    loss = sg["loss"].reshape(())
    return (loss, grad_x, *[out_g[n] for n in WEIGHT_ORDER], *[out_d[n] for n in WEIGHT_ORDER],
            *[out_m[n] for n in WEIGHT_ORDER], *[out_v[n] for n in WEIGHT_ORDER])
```

```python
import math

import jax
import jax.numpy as jnp
import numpy as np
from jax import lax
from jax.experimental import pallas as pl
from jax.experimental.pallas import tpu as pltpu

F32 = jnp.float32
BF16 = jnp.bfloat16
I32 = jnp.int32

SEQ = 2048
D_MODEL = 2048
HEAD_DIM = 128
N_HEADS = 8
N_GROUPS = 8
D_FF = 4 * D_MODEL
BLOCK = 128
DILATIONS = (1, 4, 16)
N_BUCKETS = 32
MAX_DISTANCE = 2048
ALPHA = 2.0 ** 0.25
LN_EPS = 1e-5
NEG_INF = -1e30
N_DEV = 8
N_CHIPS = 4
N_SUB = 16
ADAM_LR, ADAM_B1, ADAM_B2, ADAM_EPS, ADAM_WD, ADAM_STEP = 0.001, 0.9, 0.999, 1e-08, 0.01, 10
LANES = 128
SUBLANES = 8
VMEM_LIMIT = 56 * 1024 * 1024
MESH = pl.DeviceIdType.MESH
ANY = pl.BlockSpec(memory_space=pl.ANY)
WEIGHT_ORDER = ("w_in", "rel_bias", "ln_v_gain", "ln_v_bias", "w_spatial", "b_spatial", "w_proj_a", "w_proj_b", "w_out",
                "ln1_gain", "ln1_bias", "w_ff1", "b_ff1", "w_ff2", "b_ff2", "ln2_gain", "ln2_bias")
KINDS = {"w_in": "col", "w_proj_a": "col", "w_proj_b": "col", "w_out": "row", "w_ff1": "col", "w_ff2": "row"}


def _d_a():
    return N_HEADS * HEAD_DIM


def _d_b():
    return N_GROUPS * BLOCK


def _d_in():
    return 3 * _d_a() + 2 * _d_b() + 2 * D_MODEL


def _pick(t, n, *others):
    if n <= t and all(o % n == 0 for o in others):
        return n
    for c in range(min(t, n) // LANES * LANES, 0, -LANES):
        if n % c == 0 and all(o % c == 0 for o in others):
            return c
    raise ValueError((t, n, others))


class _Stage:
    def __init__(self, ins, outs, alias, sems, start, finish):
        self.ins, self.outs, self.alias, self.sems, self.start, self.finish = ins, outs, alias, sems, start, finish


def _call(name, body, grid, in_specs, out_specs, out_shape, operands, scratch=(), sem=None, stages=(), sequential=False, prefetch=None,
          shown=False, alias=None):
    n_in, n_out, n_sc = len(in_specs), len(out_specs), len(scratch)
    st_in = [len(s.ins) for s in stages]
    st_out = [len(s.outs) for s in stages]
    st_sem = [len(s.sems) for s in stages]
    n_pre = 0 if prefetch is None else 1
    aliases, ioff, ooff = {i + n_pre: o for i, o in (alias or {}).items()}, n_in + n_pre, n_out
    for s, ni, no in zip(stages, st_in, st_out):
        for i, o in s.alias.items():
            aliases[ioff + i] = ooff + o
        ioff, ooff = ioff + ni, ooff + no

    def split(refs, counts):
        out, at = [], 0
        for c in counts:
            out.append(refs[at:at + c])
            at += c
        return out

    def wrapped(*refs):
        ins, sins, outs, souts, sc, ssems = split(refs[n_pre:], [n_in, sum(st_in), n_out, sum(st_out), n_sc, sum(st_sem)])
        parts = list(zip(stages, split(sins, st_in), split(souts, st_out), split(ssems, st_sem)))
        if sequential:
            for s, a, b, c in parts:
                s.start(a, b, c)
            for s, a, b, c in parts:
                s.finish(a, b, c)
            return
        if parts:
            first = _all_of([pl.program_id(i) == 0 for i in range(len(grid))])
            last = _all_of([pl.program_id(i) == g - 1 for i, g in enumerate(grid)])

            @pl.when(first)
            def _():
                for s, a, b, c in parts:
                    s.start(a, b, c)

        body(*(refs[:n_pre] if shown else ()), *ins, *outs, *sc)
        if parts:
            @pl.when(last)
            def _():
                for s, a, b, c in parts:
                    s.finish(a, b, c)

    if stages or sem is None:
        sem = ("arbitrary",) * len(grid)
    specs = dict(grid=grid, in_specs=list(in_specs) + [ANY] * sum(st_in), out_specs=list(out_specs) + [ANY] * sum(st_out),
                 scratch_shapes=list(scratch) + [x for s in stages for x in s.sems])
    if prefetch is not None:
        specs = dict(grid_spec=pltpu.PrefetchScalarGridSpec(num_scalar_prefetch=1, **specs))
    res = pl.pallas_call(
        wrapped, name=name, out_shape=list(out_shape) + [o for s in stages for o in s.outs], input_output_aliases=aliases,
        compiler_params=pltpu.CompilerParams(dimension_semantics=sem, vmem_limit_bytes=VMEM_LIMIT), **specs,
    )(*([prefetch] if n_pre else []), *operands, *[a for s in stages for a in s.ins])
    res = list(res)
    return res[:n_out], split(res[n_out:], st_out)


def _all_of(conds):
    out = conds[0]
    for c in conds[1:]:
        out = out & c
    return out


def _coords():
    return lax.axis_index("x"), lax.axis_index("y"), lax.axis_index("c")


def _other_chips(x, y):
    return ((1 - x, y), (x, 1 - y), (1 - x, 1 - y))


def _lin(dev):
    return 4 * dev[0] + 2 * dev[1] + dev[2]


def _piece(total, lo, n, units=16):
    assert total % units == 0
    return lo * (total // units), n * (total // units)


FLOWS = 4


def _split(lo, cnt):
    k = next(k for k in (FLOWS, 2, 1) if cnt % (2 * SUBLANES * k) == 0)
    return [(lo + i * (cnt // k), cnt // k) for i in range(k)]


def _remote(src, dst, send, recv, to):
    return pltpu.make_async_remote_copy(src_ref=src, dst_ref=dst, send_sem=send, recv_sem=recv, device_id=to, device_id_type=MESH)


def _placer(kind, n, lo, cnt):
    def place(ref, dev):
        if kind == "col":
            return ref.at[pl.ds(lo, cnt), pl.ds(pl.multiple_of(_lin(dev) * n, LANES), n)]
        return ref.at[pl.ds(pl.multiple_of(_lin(dev) * n + lo, 2 * SUBLANES), cnt), :]
    return place


def _spread_stage(full, kind, piece=(0, 16), home=False):
    n = (full.shape[1] if kind == "col" else full.shape[0]) // N_DEV
    lo, cnt = _piece(full.shape[0] if kind == "col" else n, *piece)
    parts = _split(lo, cnt)
    npeers = 1 if home else 2

    def copies(ins, outs, sems):
        send, recv = sems
        x, y, c = _coords()
        me = (x, y, c)
        peers = [(x, y, 1 - c)] if home else [(1 - x, y, c), (x, 1 - y, c)]
        out, arrive = [], []
        for k, t in enumerate(peers):
            for i, (plo, pcnt) in enumerate(parts):
                place = _placer(kind, n, plo, pcnt)
                out.append(_remote(place(outs[0], me), place(outs[0], me), send.at[i, k], recv.at[i, k], t))
                arrive.append(_remote(place(outs[0], t), place(outs[0], t), send.at[i, k], recv.at[i, k], t))
        return out, arrive

    def start(ins, outs, sems):
        for cp in copies(ins, outs, sems)[0]:
            cp.start()

    def finish(ins, outs, sems):
        out, arrive = copies(ins, outs, sems)
        for cp in arrive:
            cp.wait_recv()
        for cp in out:
            cp.wait_send()

    return _Stage([full], [jax.ShapeDtypeStruct(full.shape, full.dtype)], {0: 0},
                  [pltpu.SemaphoreType.DMA((len(parts), npeers)), pltpu.SemaphoreType.DMA((len(parts), npeers))], start, finish)


def _relay_stage(full, kind, piece=(0, 16)):
    n = (full.shape[1] if kind == "col" else full.shape[0]) // N_DEV
    lo, cnt = _piece(full.shape[0] if kind == "col" else n, *piece)
    half = cnt // 2
    assert half % (2 * SUBLANES) == 0, (cnt, kind)
    tops, bottoms = _split(lo, half), _split(lo + half, half)

    def copies(ins, outs, sems):
        send, recv = sems
        x, y, c = _coords()
        xn, yn, dg = (1 - x, y, c), (x, 1 - y, c), (1 - x, 1 - y, c)
        out, arrive, k = [], [], 0
        for came_from, to, parts in ((yn, xn, tops), (xn, yn, bottoms)):
            for plo, pcnt in parts:
                place = _placer(kind, n, plo, pcnt)
                out.append(_remote(place(outs[0], came_from), place(outs[0], came_from), send.at[k], recv.at[k], to))
                arrive.append(_remote(place(outs[0], dg), place(outs[0], dg), send.at[k], recv.at[k], to))
                k += 1
        return out, arrive

    def start(ins, outs, sems):
        for cp in copies(ins, outs, sems)[0]:
            cp.start()

    def finish(ins, outs, sems):
        out, arrive = copies(ins, outs, sems)
        for cp in arrive:
            cp.wait_recv()
        for cp in out:
            cp.wait_send()

    return _Stage([full], [jax.ShapeDtypeStruct(full.shape, full.dtype)], {0: 0},
                  [pltpu.SemaphoreType.DMA((len(tops) + len(bottoms),)), pltpu.SemaphoreType.DMA((len(tops) + len(bottoms),))], start, finish)


def _forward_stage(full, kind, piece=(0, 16)):
    n = (full.shape[1] if kind == "col" else full.shape[0]) // N_DEV
    lo, cnt = _piece(full.shape[0] if kind == "col" else n, *piece)
    place = _placer(kind, n, lo, cnt)

    def copies(ins, outs, sems):
        send, recv = sems
        x, y, c = _coords()
        chips = _other_chips(x, y)
        out = [_remote(place(outs[0], (*chip, c)), place(outs[0], (*chip, c)), send.at[k], recv.at[k], (x, y, 1 - c)) for k, chip in enumerate(chips)]
        arrive = [_remote(place(outs[0], (*chip, 1 - c)), place(outs[0], (*chip, 1 - c)), send.at[k], recv.at[k], (x, y, 1 - c))
                  for k, chip in enumerate(chips)]
        return out, arrive

    def start(ins, outs, sems):
        for cp in copies(ins, outs, sems)[0]:
            cp.start()

    def finish(ins, outs, sems):
        out, arrive = copies(ins, outs, sems)
        for cp in arrive:
            cp.wait_recv()
        for cp in out:
            cp.wait_send()

    return _Stage([full], [jax.ShapeDtypeStruct(full.shape, full.dtype)], {0: 0},
                  [pltpu.SemaphoreType.DMA((3,)), pltpu.SemaphoreType.DMA((3,))], start, finish)


def _to_sibling_stage(theirs):
    def copies(ins, outs, sems):
        send, recv = sems
        x, y, c = _coords()
        return [_remote(ins[0].at[q], outs[0].at[q], send.at[q], recv.at[q], (x, y, 1 - c)) for q in range(N_CHIPS)]

    def start(ins, outs, sems):
        for cp in copies(ins, outs, sems):
            cp.start()

    def finish(ins, outs, sems):
        for cp in copies(ins, outs, sems):
            cp.wait()

    return _Stage([theirs], [jax.ShapeDtypeStruct(theirs.shape, BF16)], {},
                  [pltpu.SemaphoreType.DMA((N_CHIPS,)), pltpu.SemaphoreType.DMA((N_CHIPS,))], start, finish)


def _to_chips_stage(pair, dst, piece=(0, 16)):
    lo, cnt = _piece(pair.shape[1], *piece)
    parts = _split(lo, cnt)
    nsem = 3 * len(parts)

    def copies(ins, outs, sems):
        send, recv = sems
        x, y, c = _coords()
        mine = 2 * x + y
        out, arrive, k = [], [], 0
        for px, py in _other_chips(x, y):
            for plo, pcnt in parts:
                rows = pl.ds(plo, pcnt)
                out.append(_remote(ins[0].at[2 * px + py, rows, :], outs[0].at[mine, rows, :], send.at[k], recv.at[k], (px, py, c)))
                arrive.append(_remote(ins[0].at[2 * px + py, rows, :], outs[0].at[2 * px + py, rows, :], send.at[k], recv.at[k], (px, py, c)))
                k += 1
        return out, arrive

    def start(ins, outs, sems):
        for cp in copies(ins, outs, sems)[0]:
            cp.start()

    def finish(ins, outs, sems):
        out, arrive = copies(ins, outs, sems)
        for cp in arrive:
            cp.wait_recv()
        for cp in out:
            cp.wait_send()

    return _Stage([pair, dst], [jax.ShapeDtypeStruct(dst.shape, dst.dtype)], {1: 0},
                  [pltpu.SemaphoreType.DMA((nsem,)), pltpu.SemaphoreType.DMA((nsem,))], start, finish)


def _fuse(parts, ins, outs, alias):
    parts = [p for p in parts if p is not None]
    sems = [x for st, _, _ in parts for x in st.sems]

    def run(which):
        def go(i, o, s):
            refs, at = list(i) + list(o), 0
            for st, pi, po in parts:
                getattr(st, which)([refs[k] for k in pi], [refs[k] for k in po], s[at:at + len(st.sems)])
                at += len(st.sems)
        return go

    return _Stage(ins, [jax.ShapeDtypeStruct(o.shape, o.dtype) for o in outs], alias, sems, run("start"), run("finish"))


def _gather_stage(full, kind, new=None, relay=None, forward=None):
    return _fuse([(_spread_stage(full, kind, new), [0], [1]) if new else None,
                  (_relay_stage(full, kind, relay), [0], [1]) if relay else None,
                  (_spread_stage(full, kind, relay, home=True), [0], [1]) if relay else None,
                  (_forward_stage(full, kind, forward), [0], [1]) if forward else None], [full], [full], {0: 0})


def _small_stage(part, dst, rows):
    piece = pl.ds(*rows)

    def copies(ins, outs, sems):
        send, recv = sems
        x, y, c = _coords()
        me = (x, y, c)
        peers = [(1 - x if k & 4 else x, 1 - y if k & 2 else y, 1 - c if k & 1 else c) for k in range(1, N_DEV)]
        out = [_remote(ins[0].at[piece, :], outs[0].at[_lin(me), piece, :], send.at[k], recv.at[k], t) for k, t in enumerate(peers)]
        arrive = [_remote(ins[0].at[piece, :], outs[0].at[_lin(t), piece, :], send.at[k], recv.at[k], t) for k, t in enumerate(peers)]
        return out, arrive

    def start(ins, outs, sems):
        for cp in copies(ins, outs, sems)[0]:
            cp.start()

    def finish(ins, outs, sems):
        out, arrive = copies(ins, outs, sems)
        for cp in arrive:
            cp.wait_recv()
        for cp in out:
            cp.wait_send()

    return _Stage([part, dst], [jax.ShapeDtypeStruct(dst.shape, F32)], {1: 0},
                  [pltpu.SemaphoreType.DMA((N_DEV - 1,)), pltpu.SemaphoreType.DMA((N_DEV - 1,))], start, finish)


def _comm_only(name, stages):
    return _call(name, lambda: None, (1,), [], [], [], [], stages=stages, sequential=True)[1]


_GELU_C = math.sqrt(2.0 / math.pi)


def _gelu(x):
    return 0.5 * x * (1.0 + jnp.tanh(_GELU_C * (x + 0.044715 * x * x * x)))


def _gelu_grad(x):
    t = jnp.tanh(_GELU_C * (x + 0.044715 * x * x * x))
    return 0.5 * (1.0 + t) + 0.5 * x * (1.0 - t * t) * (_GELU_C * (1.0 + 3.0 * 0.044715 * x * x))


def _sigmoid(x):
    return 1.0 / (1.0 + jnp.exp(-x))


def _dot(a, b, mode):
    dims = {"nn": (((1,), (0,)), ((), ())), "nt": (((1,), (1,)), ((), ())), "tn": (((0,), (0,)), ((), ()))}[mode]
    return lax.dot_general(a.astype(BF16), b.astype(BF16), dims, preferred_element_type=F32)


def _matmul(name, a, b, mode, outs, epi=None, extras=(), colsums=(), tm=2048, tn=512, tk=2048, b_off=0, n=None, m_off=0, m=None, stages=()):
    if mode == "tn":
        kk, mfull = a.shape
    else:
        mfull, kk = a.shape
    m = mfull if m is None else m
    n = (b.shape[0] if mode == "nt" else b.shape[1]) if n is None else n
    tm, tk = _pick(tm, m, m_off), _pick(tk, kk)
    tn = _pick(tn, n, b_off, *[off for _, _, off in extras])
    boff, moff = b_off // tn, m_off // tm
    nm, nn_, nk = m // tm, n // tn, kk // tk
    col_major = bool(colsums)
    grid = (nn_, nm, nk) if col_major else (nm, nn_, nk)

    def imap(f):
        if col_major:
            return lambda g0, g1, k: f(g1, g0, k)
        return f

    a_spec = (pl.BlockSpec((tk, tm), imap(lambda i, j, k: (k, i + moff))) if mode == "tn"
              else pl.BlockSpec((tm, tk), imap(lambda i, j, k: (i + moff, k))))
    b_spec = (pl.BlockSpec((tn, tk), imap(lambda i, j, k: (j + boff, k))) if mode == "nt"
              else pl.BlockSpec((tk, tn), imap(lambda i, j, k: (k, j + boff))))
    in_specs, operands = [a_spec, b_spec], [a, b]
    for arr, kind, off in extras:
        o = off // tn
        if kind == "mn":
            in_specs.append(pl.BlockSpec((tm, tn), imap(lambda i, j, k, o=o: (i + moff, j + o))))
        else:
            in_specs.append(pl.BlockSpec((1, tn), imap(lambda i, j, k, o=o: (0, j + o))))
        operands.append(arr)
    out_shape = [jax.ShapeDtypeStruct((m, n), dt) for dt in outs] + [jax.ShapeDtypeStruct((1, n), F32) for _ in colsums]
    out_specs = ([pl.BlockSpec((tm, tn), imap(lambda i, j, k: (i, j))) for _ in outs]
                 + [pl.BlockSpec((1, tn), imap(lambda i, j, k: (0, j))) for _ in colsums])
    n_ex, n_out, n_cs = len(extras), len(outs), len(colsums)

    def body(*refs):
        a_ref, b_ref = refs[:2]
        ex_refs = refs[2:2 + n_ex]
        out_refs = refs[2 + n_ex:2 + n_ex + n_out]
        cs_refs = refs[2 + n_ex + n_out:2 + n_ex + n_out + n_cs]
        part = _dot(a_ref[...], b_ref[...], mode)

        def finish(acc):
            res = epi(acc, *[r[...] for r in ex_refs]) if epi is not None else (acc,)
            for r, v in zip(out_refs, res[:n_out]):
                r[...] = v.astype(r.dtype)
            if n_cs:
                @pl.when(pl.program_id(1) == 0)
                def _():
                    for r in cs_refs:
                        r[...] = jnp.zeros_like(r)

                for r, idx in zip(cs_refs, colsums):
                    r[...] += jnp.sum(res[idx], axis=0, keepdims=True)

        if nk == 1:
            finish(part)
        else:
            acc_ref = refs[-1]
            k = pl.program_id(2)

            @pl.when(k == 0)
            def _():
                acc_ref[...] = part

            @pl.when(k > 0)
            def _():
                acc_ref[...] += part

            @pl.when(k == nk - 1)
            def _():
                finish(acc_ref[...])

    sem = ("arbitrary", "arbitrary", "arbitrary") if col_major else ("parallel", "parallel", "arbitrary")
    return _call(name, body, grid, in_specs, out_specs, out_shape, operands,
                 scratch=[pltpu.VMEM((tm, tn), F32)] if nk > 1 else [], sem=sem, stages=stages)


def _project_shards(name, a, b, which, into=None, stages=()):
    m, kk = a.shape
    n = b.shape[1]
    tn = n // N_DEV
    o_spec = pl.BlockSpec((m, tn), lambda s, w_ref: (0, w_ref[s]))

    def body(a_ref, b_ref, *rest):
        rest[-1][...] = _dot(a_ref[...], b_ref[...], "nn")

    (out,), st = _call(name, body, (which.shape[0],),
                       [pl.BlockSpec((m, kk), lambda s, w_ref: (0, 0)), pl.BlockSpec((kk, tn), lambda s, w_ref: (0, w_ref[s]))]
                       + ([ANY] if into is not None else []), [o_spec], [jax.ShapeDtypeStruct((m, n), F32)],
                       [a, b] + ([into] if into is not None else []), sem=("arbitrary",), stages=stages, prefetch=which,
                       alias={2: 0} if into is not None else None)
    return out, st


def _row_spec(tr, c):
    return pl.BlockSpec((tr, c), lambda i: (i, 0))


def _fix_spec(shape):
    return pl.BlockSpec(shape, lambda *_: tuple(0 for _ in shape))


def _cast_bf16(name, x, tr=512):
    r, c = x.shape
    tr = _pick(tr, r)

    def body(x_ref, o_ref):
        o_ref[...] = x_ref[...].astype(BF16)

    return _call(name, body, (r // tr,), [_row_spec(tr, c)], [_row_spec(tr, c)], [jax.ShapeDtypeStruct((r, c), BF16)], [x],
                 sem=("parallel",))[0][0]


def _cast_into_place(name, w, kind, me, tr=512):
    r, c = w.shape
    tr = _pick(tr, r)
    nb = r // tr
    if kind == "col":
        o_spec = pl.BlockSpec((tr, c), lambda i, me_ref: (i, me_ref[0]))
        shape = (r, c * N_DEV)
    else:
        o_spec = pl.BlockSpec((tr, c), lambda i, me_ref: (me_ref[0] * nb + i, 0))
        shape = (r * N_DEV, c)

    def body(x_ref, o_ref):
        o_ref[...] = x_ref[...].astype(BF16)

    return _call(name, body, (nb,), [pl.BlockSpec((tr, c), lambda i, me_ref: (i, 0))], [o_spec], [jax.ShapeDtypeStruct(shape, BF16)], [w],
                 sem=("parallel",), prefetch=me)[0][0]


def _layer_norm_stats(x):
    mean = jnp.mean(x, axis=-1, keepdims=True)
    xc = x - mean
    var = jnp.mean(xc * xc, axis=-1, keepdims=True)
    rstd = lax.rsqrt(var + LN_EPS)
    return xc * rstd, rstd


def _layer_norm_bwd(dxhat, xhat, rstd):
    m1 = jnp.mean(dxhat, axis=-1, keepdims=True)
    m2 = jnp.mean(dxhat * xhat, axis=-1, keepdims=True)
    return rstd * (dxhat - m1 - xhat * m2)


def _ln1_fwd(pre1, g1, b1, tr=256, stages=()):
    s, d = pre1.shape
    tr = _pick(tr, s)

    def body(p_ref, g_ref, b_ref, xh_ref, rs_ref, h_ref):
        xhat, rstd = _layer_norm_stats(p_ref[...])
        xh_ref[...] = xhat
        rs_ref[...] = rstd
        h_ref[...] = (xhat * g_ref[...] + b_ref[...]).astype(BF16)

    return _call("ln1_fwd", body, (s // tr,), [_row_spec(tr, d), _fix_spec((1, d)), _fix_spec((1, d))],
                 [_row_spec(tr, d), _row_spec(tr, 1), _row_spec(tr, d)],
                 [jax.ShapeDtypeStruct((s, d), F32), jax.ShapeDtypeStruct((s, 1), F32), jax.ShapeDtypeStruct((s, d), BF16)],
                 [pre1, g1, b1], sem=("parallel",), stages=stages)


def _ln2_loss_bwd(ff, xhat1, g1, b1, g2, b2, target, tr=256):
    s, d = ff.shape
    tr = _pick(tr, s)

    def body(ff_ref, xh1_ref, g1_ref, b1_ref, g2_ref, b2_ref, t_ref, dp_ref, dpb_ref, dg_ref, db_ref, dbf_ref, loss_ref):
        @pl.when(pl.program_id(0) == 0)
        def _():
            dg_ref[...] = jnp.zeros_like(dg_ref)
            db_ref[...] = jnp.zeros_like(db_ref)
            dbf_ref[...] = jnp.zeros_like(dbf_ref)
            loss_ref[...] = jnp.zeros_like(loss_ref)

        h1 = xh1_ref[...] * g1_ref[...] + b1_ref[...]
        xhat, rstd = _layer_norm_stats(ALPHA * h1 + ff_ref[...])
        err = xhat * g2_ref[...] + b2_ref[...] - t_ref[...]
        row = jnp.mean(err * err, axis=-1, keepdims=True)
        loss_ref[...] += 0.5 * jnp.sum(row, axis=0, keepdims=True)
        dy = err / d
        dg_ref[...] += jnp.sum(dy * xhat, axis=0, keepdims=True)
        db_ref[...] += jnp.sum(dy, axis=0, keepdims=True)
        dpre = _layer_norm_bwd(dy * g2_ref[...], xhat, rstd)
        dbf_ref[...] += jnp.sum(dpre, axis=0, keepdims=True)
        dp_ref[...] = dpre
        dpb_ref[...] = dpre.astype(BF16)

    vec = _fix_spec((1, d))
    return _call("ln2_loss_bwd", body, (s // tr,), [_row_spec(tr, d), _row_spec(tr, d), vec, vec, vec, vec, _row_spec(tr, d)],
                 [_row_spec(tr, d), _row_spec(tr, d), vec, vec, vec, _fix_spec((1, 1))],
                 [jax.ShapeDtypeStruct((s, d), F32), jax.ShapeDtypeStruct((s, d), BF16)]
                 + [jax.ShapeDtypeStruct((1, d), F32)] * 3 + [jax.ShapeDtypeStruct((1, 1), F32)],
                 [ff, xhat1, g1, b1, g2, b2, target])[0]


def _ln1_bwd(dh1, xhat1, rstd1, g1, tr=256, stages=()):
    s, d = dh1.shape
    tr = _pick(tr, s)

    def body(dh_ref, xh_ref, rs_ref, g_ref, dp_ref, dpb_ref, dg_ref, db_ref):
        @pl.when(pl.program_id(0) == 0)
        def _():
            dg_ref[...] = jnp.zeros_like(dg_ref)
            db_ref[...] = jnp.zeros_like(db_ref)

        dh, xhat = dh_ref[...], xh_ref[...]
        dg_ref[...] += jnp.sum(dh * xhat, axis=0, keepdims=True)
        db_ref[...] += jnp.sum(dh, axis=0, keepdims=True)
        dpre = _layer_norm_bwd(dh * g_ref[...], xhat, rs_ref[...])
        dp_ref[...] = dpre
        dpb_ref[...] = dpre.astype(BF16)

    vec = _fix_spec((1, d))
    return _call("ln1_bwd", body, (s // tr,), [_row_spec(tr, d), _row_spec(tr, d), _row_spec(tr, 1), vec],
                 [_row_spec(tr, d), _row_spec(tr, d), vec, vec],
                 [jax.ShapeDtypeStruct((s, d), F32), jax.ShapeDtypeStruct((s, d), BF16)] + [jax.ShapeDtypeStruct((1, d), F32)] * 2,
                 [dh1, xhat1, rstd1, g1], stages=stages)


def _to_perm(x):
    return x.reshape(SEQ // N_SUB, N_SUB, -1).transpose(1, 0, 2).reshape(SEQ, -1)


def _from_perm(x):
    return x.reshape(N_SUB, SEQ // N_SUB, -1).transpose(1, 0, 2).reshape(SEQ, -1)


def _local_index(p):
    rho = np.arange(BLOCK)
    if p == 0:
        return 16 * (rho % 8) + rho // 8
    if p == 1:
        return 4 * (rho % 32) + rho // 32
    return rho


def _tile_view(x, p):
    c = x.shape[1]
    if p == 1:
        return x.reshape(4, 4, BLOCK, c)
    return x.reshape(N_SUB, BLOCK, c)


def _view_shape(c, p):
    return (4, 4, BLOCK, c) if p == 1 else (N_SUB, BLOCK, c)


def _tile_spec(p, width, col, shift=0):
    nblk = SEQ // DILATIONS[p] // BLOCK

    def blk(n):
        return jnp.clip(n + shift, 0, nblk - 1)

    if p == 0:
        return pl.BlockSpec((N_SUB, SUBLANES, width), lambda s, n: (0, blk(n), col))
    if p == 1:
        return pl.BlockSpec((4, None, 32, width), lambda s, n: (0, s, blk(n), col))
    return pl.BlockSpec((None, BLOCK, width), lambda s, n: (s, 0, col))


def _tile_grid(p):
    return ((1, 16), (4, 4), (16, 1))[p]


def _t5_bucket(n):
    max_exact = N_BUCKETS // 2
    nf = np.maximum(n, 1).astype(np.float32)
    large = max_exact + (np.log(nf / np.float32(max_exact)) / np.float32(math.log(MAX_DISTANCE / max_exact))
                         * np.float32(N_BUCKETS - max_exact)).astype(np.int32)
    large = np.minimum(large, N_BUCKETS - 1)
    return np.where(n < max_exact, n, large).astype(np.int32)


def _bucket_tables():
    tabs = np.zeros((3, 2, BLOCK, BLOCK), np.int32)
    for p, d in enumerate(DILATIONS):
        i = _local_index(p)
        diff = i[:, None] - i[None, :]
        tabs[p, 0] = np.where(diff <= 0, _t5_bucket((BLOCK + diff) * d), -1)
        tabs[p, 1] = np.where(diff >= 0, _t5_bucket(np.maximum(diff, 0) * d), -1)
    return tabs


def _bias_expand(rel_bias, buckets):
    nh = N_HEADS

    def body(rb_ref, bk_ref, o_ref):
        for w in range(2):
            bk = bk_ref[0, w]
            for h in range(nh):
                val = jnp.zeros((BLOCK, BLOCK), F32)
                for b in range(N_BUCKETS):
                    val = jnp.where(bk == b, rb_ref[b, h], val)
                o_ref[0, h, w] = jnp.where(bk < 0, NEG_INF, val)

    return _call("bias_expand", body, (3,),
                 [pl.BlockSpec(memory_space=pltpu.SMEM), pl.BlockSpec((1, 2, BLOCK, BLOCK), lambda p: (p, 0, 0, 0))],
                 [pl.BlockSpec((1, nh, 2, BLOCK, BLOCK), lambda p: (p, 0, 0, 0, 0))],
                 [jax.ShapeDtypeStruct((3, nh, 2, BLOCK, BLOCK), F32)], [rel_bias, buckets], sem=("parallel",))[0][0]


def _heads_to_lanes(cols):
    lane = lax.broadcasted_iota(I32, (BLOCK, LANES), 1)
    out = jnp.zeros((BLOCK, LANES), F32)
    for h, c in enumerate(cols):
        out = jnp.where(lane == h, c, out)
    return out


def _attn_fwd(qkv, bias, p, stages=()):
    d_a = _d_a()
    has_prev = SEQ // DILATIONS[p] // BLOCK > 1
    scale = HEAD_DIM ** -0.5
    view = _tile_view(qkv, p)

    width = 2 * BLOCK if has_prev else BLOCK

    def body(q_ref, kc_ref, kp_ref, vc_ref, vp_ref, b_ref, o_ref, l_ref, s_ref, p_ref):
        n = pl.program_id(1)
        q_all = q_ref[...].reshape(BLOCK, d_a).astype(BF16)
        k_all = kc_ref[...].reshape(BLOCK, d_a).astype(BF16)
        v_all = vc_ref[...].reshape(BLOCK, d_a).astype(BF16)
        if has_prev:
            k_all = jnp.concatenate([kp_ref[...].reshape(BLOCK, d_a).astype(BF16), k_all], axis=0)
            v_all = jnp.concatenate([vp_ref[...].reshape(BLOCK, d_a).astype(BF16), v_all], axis=0)
            no_prev = (lax.broadcasted_iota(I32, (BLOCK, width), 1) < BLOCK) & (n == 0)
        for h in range(N_HEADS):
            sl = slice(h * HEAD_DIM, (h + 1) * HEAD_DIM)
            s = _dot(q_all[:, sl], k_all[:, sl], "nt") * scale
            if has_prev:
                s = jnp.where(no_prev, NEG_INF, s + jnp.concatenate([b_ref[0, h, 0], b_ref[0, h, 1]], axis=1))
            else:
                s = s + b_ref[0, h, 1]
            s_ref[h] = s
        dens, lses = [], []
        for h in range(N_HEADS):
            s = s_ref[h]
            m = jnp.max(s, axis=-1, keepdims=True)
            pr = jnp.exp(s - m)
            den = jnp.sum(pr, axis=-1, keepdims=True)
            p_ref[h] = pr.astype(BF16)
            dens.append(den)
            lses.append(m + jnp.log(den))
        for h in range(N_HEADS):
            sl = slice(h * HEAD_DIM, (h + 1) * HEAD_DIM)
            o_ref[..., sl] = (_dot(p_ref[h], v_all[:, sl], "nn") / dens[h]).reshape(*o_ref.shape[:-1], HEAD_DIM)
        l_ref[...] = _heads_to_lanes(lses).reshape(l_ref.shape)

    (o, l), st = _call(
        f"attn_fwd{p}", body, _tile_grid(p),
        [_tile_spec(p, d_a, 0), _tile_spec(p, d_a, 1), _tile_spec(p, d_a, 1, -1), _tile_spec(p, d_a, 2), _tile_spec(p, d_a, 2, -1),
         pl.BlockSpec((1, N_HEADS, 2, BLOCK, BLOCK), lambda s, n: (p, 0, 0, 0, 0))],
        [_tile_spec(p, d_a, 0), _tile_spec(p, LANES, 0)],
        [jax.ShapeDtypeStruct(_view_shape(d_a, p), F32), jax.ShapeDtypeStruct(_view_shape(LANES, p), F32)],
        [view, view, view, view, view, bias], scratch=[pltpu.VMEM((N_HEADS, BLOCK, width), F32), pltpu.VMEM((N_HEADS, BLOCK, width), BF16)],
        sem=("parallel", "parallel"), stages=stages)
    return (o.reshape(SEQ, d_a), l.reshape(SEQ, LANES)), st


def _attn_combine(os_, ls_, tr=256, stages=()):
    d_a = _d_a()
    tr = _pick(tr, SEQ)

    def body(o0, o1, o2, l0, l1, l2, a_ref, ab_ref, lt_ref):
        l = [l0[...], l1[...], l2[...]]
        m = jnp.maximum(jnp.maximum(l[0], l[1]), l[2])
        w = [jnp.exp(x - m) for x in l]
        tot = w[0] + w[1] + w[2]
        lt_ref[...] = m + jnp.log(tot)
        w = [x / tot for x in w]
        for h in range(N_HEADS):
            sl = slice(h * HEAD_DIM, (h + 1) * HEAD_DIM)
            acc = w[0][:, h:h + 1] * o0[:, sl] + w[1][:, h:h + 1] * o1[:, sl] + w[2][:, h:h + 1] * o2[:, sl]
            a_ref[:, sl] = acc
            ab_ref[:, sl] = acc.astype(BF16)

    return _call("attn_combine", body, (SEQ // tr,), [_row_spec(tr, d_a)] * 3 + [_row_spec(tr, LANES)] * 3,
                 [_row_spec(tr, d_a), _row_spec(tr, d_a), _row_spec(tr, LANES)],
                 [jax.ShapeDtypeStruct((SEQ, d_a), F32), jax.ShapeDtypeStruct((SEQ, d_a), BF16), jax.ShapeDtypeStruct((SEQ, LANES), F32)],
                 [*os_, *ls_], sem=("parallel",), stages=stages)


def _attn_delta(dattn, attn, tr=256):
    d_a = _d_a()
    tr = _pick(tr, SEQ)

    def body(d_ref, a_ref, o_ref):
        prod = d_ref[...] * a_ref[...]
        lane = lax.broadcasted_iota(I32, (tr, LANES), 1)
        out = jnp.zeros((tr, LANES), F32)
        for h in range(N_HEADS):
            out = jnp.where(lane == h, jnp.sum(prod[:, h * HEAD_DIM:(h + 1) * HEAD_DIM], axis=-1, keepdims=True), out)
        o_ref[...] = out

    return _call("attn_delta", body, (SEQ // tr,), [_row_spec(tr, d_a)] * 2, [_row_spec(tr, LANES)],
                 [jax.ShapeDtypeStruct((SEQ, LANES), F32)], [dattn, attn], sem=("parallel",))[0][0]


def _attn_bwd(qkv, dattn, lse, delta, bias, p, stages=()):
    d_a = _d_a()
    nblk = SEQ // DILATIONS[p] // BLOCK
    has_next = nblk > 1
    scale = HEAD_DIM ** -0.5
    qv, dov, lv, tv = (_tile_view(x, p) for x in (qkv, dattn, lse, delta))

    rows = 2 * BLOCK if has_next else BLOCK

    def body(q_ref, qn_ref, k_ref, v_ref, do_ref, don_ref, l_ref, ln_ref, t_ref, tn_ref, b_ref, dq_ref, dk_ref, dv_ref, db_ref,
             carry_ref, s_ref, dp_ref, p_ref, ds_ref):
        j = pl.program_id(1)

        @pl.when((pl.program_id(0) == 0) & (j == 0))
        def _():
            db_ref[...] = jnp.zeros_like(db_ref)

        def both(cur, nxt, width, dtype):
            cur = cur[...].reshape(BLOCK, width).astype(dtype)
            return jnp.concatenate([cur, nxt[...].reshape(BLOCK, width).astype(dtype)], axis=0) if has_next else cur

        k_all = k_ref[...].reshape(BLOCK, d_a).astype(BF16)
        v_all = v_ref[...].reshape(BLOCK, d_a).astype(BF16)
        q_all, do_all = both(q_ref, qn_ref, d_a, BF16), both(do_ref, don_ref, d_a, BF16)
        l_all, t_all = both(l_ref, ln_ref, LANES, F32), both(t_ref, tn_ref, LANES, F32)
        if has_next:
            no_next = (lax.broadcasted_iota(I32, (rows, BLOCK), 0) >= BLOCK) & (j == nblk - 1)
        for h in range(N_HEADS):
            sl = slice(h * HEAD_DIM, (h + 1) * HEAD_DIM)
            s = _dot(q_all[:, sl], k_all[:, sl], "nt") * scale
            if has_next:
                s = jnp.where(no_next, NEG_INF, s + jnp.concatenate([b_ref[0, h, 1], b_ref[0, h, 0]], axis=0))
            else:
                s = s + b_ref[0, h, 1]
            s_ref[h] = s
            dp_ref[h] = _dot(do_all[:, sl], v_all[:, sl], "nt")
        for h in range(N_HEADS):
            pr = jnp.exp(s_ref[h] - l_all[:, h:h + 1])
            ds = pr * (dp_ref[h] - t_all[:, h:h + 1])
            db_ref[h, 1] += ds[:BLOCK]
            if has_next:
                db_ref[h, 0] += ds[BLOCK:]
            p_ref[h] = pr.astype(BF16)
            ds_ref[h] = ds.astype(BF16)
        for h in range(N_HEADS):
            sl = slice(h * HEAD_DIM, (h + 1) * HEAD_DIM)
            dq = _dot(ds_ref[h], k_all[:, sl], "nn") * scale
            mine = dq[:BLOCK]
            if has_next:
                mine = mine + jnp.where(j > 0, carry_ref[:, sl], 0.0)
            dq_ref[..., sl] = mine.reshape(*dq_ref.shape[:-1], HEAD_DIM)
            if has_next:
                carry_ref[:, sl] = dq[BLOCK:]
            dk_ref[..., sl] = (_dot(ds_ref[h], q_all[:, sl], "tn") * scale).reshape(*dk_ref.shape[:-1], HEAD_DIM)
            dv_ref[..., sl] = _dot(p_ref[h], do_all[:, sl], "tn").reshape(*dv_ref.shape[:-1], HEAD_DIM)

    def big(col, shift=0):
        return _tile_spec(p, d_a, col, shift)

    def small(shift=0):
        return _tile_spec(p, LANES, 0, shift)

    (dq, dk, dv, dbias), st = _call(
        f"attn_bwd{p}", body, _tile_grid(p),
        [big(0), big(0, 1), big(1), big(2), big(0), big(0, 1), small(), small(1), small(), small(1),
         pl.BlockSpec((1, N_HEADS, 2, BLOCK, BLOCK), lambda s, n: (p, 0, 0, 0, 0))],
        [big(0), big(0), big(0), pl.BlockSpec((N_HEADS, 2, BLOCK, BLOCK), lambda s, n: (0, 0, 0, 0))],
        [jax.ShapeDtypeStruct(_view_shape(d_a, p), F32)] * 3 + [jax.ShapeDtypeStruct((N_HEADS, 2, BLOCK, BLOCK), F32)],
        [qv, qv, qv, qv, dov, dov, lv, lv, tv, tv, bias],
        scratch=[pltpu.VMEM((BLOCK, d_a), F32), pltpu.VMEM((N_HEADS, rows, BLOCK), F32), pltpu.VMEM((N_HEADS, rows, BLOCK), F32),
                 pltpu.VMEM((N_HEADS, rows, BLOCK), BF16), pltpu.VMEM((N_HEADS, rows, BLOCK), BF16)], stages=stages)
    return (dq.reshape(SEQ, d_a), dk.reshape(SEQ, d_a), dv.reshape(SEQ, d_a), dbias), st


def _rel_bias_grad(dbias, buckets):
    nh = N_HEADS

    def body(d0, d1, d2, bk_ref, o_ref, t_ref):
        ds = (d0, d1, d2)

        def per_bucket(b, carry):
            for h in range(nh):
                acc = jnp.zeros((BLOCK, BLOCK), F32)
                for p in range(3):
                    for w in range(2):
                        acc = acc + jnp.where(bk_ref[p, w] == b, ds[p][h, w], 0.0)
                t_ref[pl.ds(b * nh + h, 1), :] = jnp.sum(acc, axis=0, keepdims=True)
            return carry

        lax.fori_loop(0, N_BUCKETS, per_bucket, 0)
        o_ref[...] = jnp.sum(t_ref[...], axis=-1, keepdims=True)

    return _call("rel_bias_grad", body, (1,), [_fix_spec((nh, 2, BLOCK, BLOCK))] * 3 + [_fix_spec((3, 2, BLOCK, BLOCK))],
                 [_fix_spec((N_BUCKETS * nh, 1))], [jax.ShapeDtypeStruct((N_BUCKETS * nh, 1), F32)], [*dbias, buckets],
                 scratch=[pltpu.VMEM((N_BUCKETS * nh, LANES), F32)])[0][0]


def _gmlp_fwd(rest, col0, gain, bias, ws, bs, causal, stages=()):
    d_b = _d_b()

    def body(u_ref, v_ref, g_ref, b_ref, ws_ref, bs_ref, c_ref, o_ref):
        u = u_ref[...].reshape(BLOCK, d_b)
        xhat, _ = _layer_norm_stats(_gelu(v_ref[...].reshape(BLOCK, d_b)))
        vn = (xhat * g_ref[...] + b_ref[...]).astype(BF16)
        outs = []
        for g in range(N_GROUPS):
            sl = slice(g * BLOCK, (g + 1) * BLOCK)
            w = jnp.where(c_ref[...] > 0, ws_ref[g], 0.0)
            z = _dot(w, vn[:, sl], "nn") + bs_ref[:, g:g + 1]
            outs.append(_gelu(u[:, sl]) * z)
        o_ref[...] = jnp.concatenate(outs, axis=-1).reshape(o_ref.shape)

    (out,), st = _call(
        "gmlp_fwd", body, (1, SEQ // BLOCK),
        [_tile_spec(0, d_b, col0), _tile_spec(0, d_b, col0 + 1), _fix_spec((1, d_b)), _fix_spec((1, d_b)),
         _fix_spec((N_GROUPS, BLOCK, BLOCK)), _fix_spec((BLOCK, N_GROUPS)), _fix_spec((BLOCK, BLOCK))],
        [_tile_spec(0, d_b, 0)], [jax.ShapeDtypeStruct(_view_shape(d_b, 0), F32)],
        [_tile_view(rest, 0), _tile_view(rest, 0), gain, bias, ws, bs, causal], sem=("parallel", "parallel"), stages=stages)
    return out.reshape(SEQ, d_b), st


def _gmlp_bwd(rest, col0, dgmlp, gain, bias, ws, bs, causal, stages=()):
    d_b = _d_b()
    nchunk = SEQ // BLOCK

    def body(u_ref, v_ref, dg_ref, g_ref, b_ref, ws_ref, bs_ref, c_ref, du_ref, dv_ref, dws_ref, dbs_ref, dgain_ref, dbias_ref):
        c = pl.program_id(1)

        @pl.when(c == 0)
        def _():
            dws_ref[...] = jnp.zeros_like(dws_ref)
            dbs_ref[...] = jnp.zeros_like(dbs_ref)
            dgain_ref[...] = jnp.zeros_like(dgain_ref)
            dbias_ref[...] = jnp.zeros_like(dbias_ref)

        u = u_ref[...].reshape(BLOCK, d_b)
        v = v_ref[...].reshape(BLOCK, d_b)
        dgm = dg_ref[...].reshape(BLOCK, d_b)
        xhat, rstd = _layer_norm_stats(_gelu(v))
        vn = (xhat * g_ref[...] + b_ref[...]).astype(BF16)
        lane = lax.broadcasted_iota(I32, (BLOCK, LANES), 1)
        dus, dvns = [], []
        dbs = dbs_ref[...]
        for g in range(N_GROUPS):
            sl = slice(g * BLOCK, (g + 1) * BLOCK)
            w = jnp.where(c_ref[...] > 0, ws_ref[g], 0.0).astype(BF16)
            z = _dot(w, vn[:, sl], "nn") + bs_ref[:, g:g + 1]
            dz = dgm[:, sl] * _gelu(u[:, sl])
            dus.append(dgm[:, sl] * z * _gelu_grad(u[:, sl]))
            dws_ref[g] += _dot(dz, vn[:, sl], "nt")
            dbs = dbs + jnp.where(lane == g, jnp.sum(dz, axis=-1, keepdims=True), 0.0)
            dvns.append(_dot(w, dz, "tn"))
        dbs_ref[...] = dbs
        dvn = jnp.concatenate(dvns, axis=-1)
        dgain_ref[...] += jnp.sum(dvn * xhat, axis=0, keepdims=True)
        dbias_ref[...] += jnp.sum(dvn, axis=0, keepdims=True)
        dvg = _layer_norm_bwd(dvn * g_ref[...], xhat, rstd)
        du_ref[...] = jnp.concatenate(dus, axis=-1).reshape(du_ref.shape)
        dv_ref[...] = (dvg * _gelu_grad(v)).reshape(dv_ref.shape)

        @pl.when(c == nchunk - 1)
        def _():
            for g in range(N_GROUPS):
                dws_ref[g] = jnp.where(c_ref[...] > 0, dws_ref[g], 0.0)

    (du, dv, dws, dbs, dgain, dbias), st = _call(
        "gmlp_bwd", body, (1, nchunk),
        [_tile_spec(0, d_b, col0), _tile_spec(0, d_b, col0 + 1), _tile_spec(0, d_b, 0), _fix_spec((1, d_b)), _fix_spec((1, d_b)),
         _fix_spec((N_GROUPS, BLOCK, BLOCK)), _fix_spec((BLOCK, N_GROUPS)), _fix_spec((BLOCK, BLOCK))],
        [_tile_spec(0, d_b, 0), _tile_spec(0, d_b, 0), _fix_spec((N_GROUPS, BLOCK, BLOCK)), _fix_spec((BLOCK, LANES)),
         _fix_spec((1, d_b)), _fix_spec((1, d_b))],
        [jax.ShapeDtypeStruct(_view_shape(d_b, 0), F32)] * 2
        + [jax.ShapeDtypeStruct((N_GROUPS, BLOCK, BLOCK), F32), jax.ShapeDtypeStruct((BLOCK, LANES), F32)]
        + [jax.ShapeDtypeStruct((1, d_b), F32)] * 2,
        [_tile_view(rest, 0), _tile_view(rest, 0), _tile_view(dgmlp, 0), gain, bias, ws, bs, causal], stages=stages)
    return (du.reshape(SEQ, d_b), dv.reshape(SEQ, d_b), dws, dbs, dgain, dbias), st


def _assemble_dproj(dqkv, du, dv, dga, dgb, tr=128, stages=()):
    d_a, d_b, d_in = _d_a(), _d_b(), _d_in()
    tr = _pick(tr, SEQ)

    def body(*refs):
        att, (du_ref, dv_ref, dga_ref, dgb_ref, o_ref) = refs[:9], refs[9:]
        for i in range(3):
            o_ref[:, i * d_a:(i + 1) * d_a] = (att[3 * i][...] + att[3 * i + 1][...] + att[3 * i + 2][...]).astype(BF16)
        o_ref[:, 3 * d_a:3 * d_a + d_b] = du_ref[...].astype(BF16)
        o_ref[:, 3 * d_a + d_b:3 * d_a + 2 * d_b] = dv_ref[...].astype(BF16)
        o_ref[:, 3 * d_a + 2 * d_b:3 * d_a + 2 * d_b + D_MODEL] = dga_ref[...]
        o_ref[:, 3 * d_a + 2 * d_b + D_MODEL:] = dgb_ref[...]

    return _call("assemble_dproj", body, (SEQ // tr,), [_row_spec(tr, d_a)] * 9 + [_row_spec(tr, d_b)] * 2 + [_row_spec(tr, D_MODEL)] * 2,
                 [_row_spec(tr, d_in)], [jax.ShapeDtypeStruct((SEQ, d_in), BF16)], [*dqkv, du, dv, dga, dgb], sem=("parallel",), stages=stages)


def _dw(name, a, b, kind, core, mine, add=None, tn=1152, half=None, stages=()):
    s, m = a.shape
    n = b.shape[1]
    if half is not None:
        m //= 2
    rs, cs = (m, n // N_DEV) if kind == "col" else (m // N_DEV, n)
    tn = _pick(tn if kind == "col" else 512, cs)
    nj = cs // tn

    def shard(q, c_ref):
        return 2 * q + (c_ref[0] if mine else 1 - c_ref[0])

    if kind == "col":
        a_spec = pl.BlockSpec((s, m), lambda q, j, c_ref: (0, half or 0))
        b_spec = pl.BlockSpec((s, tn), lambda q, j, c_ref: (0, shard(q, c_ref) * nj + j))
    else:
        a_spec = pl.BlockSpec((s, rs), lambda q, j, c_ref: (0, shard(q, c_ref)))
        b_spec = pl.BlockSpec((s, tn), lambda q, j, c_ref: (0, j))
    o_spec = pl.BlockSpec((None, rs, tn), lambda q, j, c_ref: (q, 0, j))

    def body(a_ref, b_ref, *rest):
        acc = _dot(a_ref[...], b_ref[...], "tn")
        if add is not None:
            acc = acc + rest[0][...].astype(F32)
        rest[-1][...] = acc.astype(BF16)

    (out,), st = _call(name, body, (N_CHIPS, nj), [a_spec, b_spec] + ([o_spec] if add is not None else []), [o_spec],
                       [jax.ShapeDtypeStruct((N_CHIPS, rs, cs), BF16)], [a, b] + ([add] if add is not None else []),
                       sem=("parallel", "parallel"), stages=stages, prefetch=core)
    return out, st


def _adamw(w, g, m, v):
    m = ADAM_B1 * m + (1.0 - ADAM_B1) * g
    v = ADAM_B2 * v + (1.0 - ADAM_B2) * (g * g)
    m_hat = m / (1.0 - ADAM_B1 ** ADAM_STEP)
    v_hat = v / (1.0 - ADAM_B2 ** ADAM_STEP)
    delta = -ADAM_LR * (m_hat / (jnp.sqrt(v_hat) + ADAM_EPS) + ADAM_WD * w)
    return delta, m, v


def _adam_shard(name, pair, chip_sums, chip, w, m, v, tr=256, rows=None, into=None, stages=()):
    rs, cs = w.shape
    lo, cnt = rows or (0, rs)
    assert pair.shape[1] == cnt
    tr = _pick(tr, cnt, lo)
    first = lo // tr

    def body(chip_ref, own_ref, *refs):
        slots, (w_ref, m_ref, v_ref), (g_ref, d_ref, nm_ref, nv_ref) = refs[:N_CHIPS], refs[N_CHIPS:N_CHIPS + 3], refs[-4:]
        g = None
        for q in range(N_CHIPS):
            term = jnp.where(chip_ref[0] == q, own_ref[...], slots[q][...]).astype(F32)
            g = term if g is None else g + term
        d, nm, nv = _adamw(w_ref[...], g, m_ref[...], v_ref[...])
        g_ref[...], d_ref[...], nm_ref[...], nv_ref[...] = g, d, nm, nv

    def slot(q):
        return pl.BlockSpec((None, tr, cs), lambda i, c_ref: (jnp.where(c_ref[0] == q, (q + 1) % N_CHIPS, q), i, 0))

    spec = pl.BlockSpec((tr, cs), lambda i, c_ref: (i + first, 0))
    n_in = 1 + N_CHIPS + 3
    return _call(name, body, (cnt // tr,),
                 [pl.BlockSpec((None, tr, cs), lambda i, c_ref: (c_ref[0], i, 0))] + [slot(q) for q in range(N_CHIPS)]
                 + [spec, spec, spec] + ([ANY] * 4 if into is not None else []),
                 [spec] * 4, [jax.ShapeDtypeStruct((rs, cs), F32)] * 4, [pair] + [chip_sums] * N_CHIPS + [w, m, v] + list(into or ()),
                 sem=("parallel",), stages=stages, prefetch=chip, shown=True,
                 alias={n_in + k: k for k in range(4)} if into is not None else None)


def _adam_small(part, parts, me, w, m, v):
    rows = w.shape[0]

    def body(me_ref, own_ref, *refs):
        slots, (w_ref, m_ref, v_ref, g_ref, d_ref, nm_ref, nv_ref) = refs[:N_DEV], refs[N_DEV:]
        g = None
        for j in range(N_DEV):
            term = jnp.where(me_ref[0] == j, own_ref[...], slots[j][...])
            g = term if g is None else g + term
        d, nm, nv = _adamw(w_ref[...], g, m_ref[...], v_ref[...])
        g_ref[...], d_ref[...], nm_ref[...], nv_ref[...] = g, d, nm, nv

    def slot(j):
        return pl.BlockSpec((None, rows, LANES), lambda i, me_ref: (jnp.where(me_ref[0] == j, (j + 1) % N_DEV, j), 0, 0))

    spec = _fix_spec((rows, LANES))
    return _call("adam_small", body, (1,), [spec] + [slot(j) for j in range(N_DEV)] + [spec, spec, spec], [spec] * 4,
                 [jax.ShapeDtypeStruct((rows, LANES), F32)] * 4, [part] + [parts] * N_DEV + [w, m, v], prefetch=me, shown=True)[0]


def _small_sizes():
    d_b = _d_b()
    return (("loss", 1), ("rel_bias", N_BUCKETS * N_HEADS), ("ln_v_gain", d_b), ("ln_v_bias", d_b),
            ("w_spatial", N_GROUPS * BLOCK * BLOCK), ("b_spatial", N_GROUPS * BLOCK), ("ln1_gain", D_MODEL), ("ln1_bias", D_MODEL),
            ("b_ff1", D_FF), ("b_ff2", D_MODEL), ("ln2_gain", D_MODEL), ("ln2_bias", D_MODEL))


def _pack(vals):
    pieces = []
    for name, size in _small_sizes():
        flat = vals[name].reshape(-1).astype(F32)
        padded = -(-size // (SUBLANES * LANES)) * SUBLANES * LANES
        pieces.append(jnp.pad(flat, (0, padded - size)).reshape(-1, LANES))
    return jnp.concatenate(pieces, axis=0)


def _unpack(buf):
    out, row = {}, 0
    for name, size in _small_sizes():
        rows = -(-size // (SUBLANES * LANES)) * SUBLANES
        out[name] = buf[row:row + rows].reshape(-1)[:size]
        row += rows
    return out


def kernel(x, w_in, rel_bias, ln_v_gain, ln_v_bias, w_spatial, b_spatial, w_proj_a, w_proj_b, w_out, ln1_gain, ln1_bias, w_ff1, b_ff1, w_ff2, b_ff2, ln2_gain, ln2_bias, loss_target, m_w_in, m_rel_bias, m_ln_v_gain, m_ln_v_bias, m_w_spatial, m_b_spatial, m_w_proj_a, m_w_proj_b, m_w_out, m_ln1_gain, m_ln1_bias, m_w_ff1, m_b_ff1, m_w_ff2, m_b_ff2, m_ln2_gain, m_ln2_bias, v_w_in, v_rel_bias, v_ln_v_gain, v_ln_v_bias, v_w_spatial, v_b_spatial, v_w_proj_a, v_w_proj_b, v_w_out, v_ln1_gain, v_ln1_bias, v_w_ff1, v_b_ff1, v_w_ff2, v_b_ff2, v_ln2_gain, v_ln2_bias):
    d_a, d_b, d_in = _d_a(), _d_b(), _d_in()
    weights = dict(w_in=w_in, rel_bias=rel_bias, ln_v_gain=ln_v_gain, ln_v_bias=ln_v_bias, w_spatial=w_spatial, b_spatial=b_spatial,
                   w_proj_a=w_proj_a, w_proj_b=w_proj_b, w_out=w_out, ln1_gain=ln1_gain, ln1_bias=ln1_bias, w_ff1=w_ff1, b_ff1=b_ff1,
                   w_ff2=w_ff2, b_ff2=b_ff2, ln2_gain=ln2_gain, ln2_bias=ln2_bias)
    mom1 = dict(w_in=m_w_in, rel_bias=m_rel_bias, ln_v_gain=m_ln_v_gain, ln_v_bias=m_ln_v_bias, w_spatial=m_w_spatial,
                b_spatial=m_b_spatial, w_proj_a=m_w_proj_a, w_proj_b=m_w_proj_b, w_out=m_w_out, ln1_gain=m_ln1_gain,
                ln1_bias=m_ln1_bias, w_ff1=m_w_ff1, b_ff1=m_b_ff1, w_ff2=m_w_ff2, b_ff2=m_b_ff2, ln2_gain=m_ln2_gain, ln2_bias=m_ln2_bias)
    mom2 = dict(w_in=v_w_in, rel_bias=v_rel_bias, ln_v_gain=v_ln_v_gain, ln_v_bias=v_ln_v_bias, w_spatial=v_w_spatial,
                b_spatial=v_b_spatial, w_proj_a=v_w_proj_a, w_proj_b=v_w_proj_b, w_out=v_w_out, ln1_gain=v_ln1_gain,
                ln1_bias=v_ln1_bias, w_ff1=v_w_ff1, b_ff1=v_b_ff1, w_ff2=v_w_ff2, b_ff2=v_b_ff2, ln2_gain=v_ln2_gain, ln2_bias=v_ln2_bias)

    mx, my, mc = _coords()
    me = (4 * mx + 2 * my + mc).astype(I32).reshape(1)
    chip = (2 * mx + my).astype(I32).reshape(1)
    full = {n: _cast_into_place(f"cast_{n}", weights[n][0], KINDS[n], me) for n in KINDS}
    sent = {n: (0, 0, 0) for n in KINDS}

    def keep(table, n):
        def store(outs):
            table[n] = outs[0]
        return store

    def gathering(**new):
        stages = []
        for n in KINDS:
            out, relayed, passed = sent[n]
            units = new.get(n, 0)
            if units or relayed < out or passed < relayed:
                st = _gather_stage(full[n], KINDS[n], (out, units) if units else None,
                                   (relayed, out - relayed) if relayed < out else None, (passed, relayed - passed) if passed < relayed else None)
                st.store = keep(full, n)
                sent[n] = (out + units, out, relayed)
                stages.append(st)
        return stages

    def settle(stages, outs):
        for st, o in zip(stages, outs):
            st.store(o)

    def alone(name, stages):
        settle(stages, _comm_only(name, stages))

    def here(n):
        assert sent[n] == (16, 16, 16), (n, sent[n])
        return full[n]

    alone("gather_w_in_near", gathering(w_in=16))
    alone("gather_w_in_relay", gathering())
    alone("gather_w_in_sibling", gathering(w_proj_a=16, w_proj_b=16))

    xs = _to_perm(x[0])
    target = _to_perm(loss_target[0])
    xb = _cast_bf16("cast_x", xs)
    g8 = BLOCK // N_SUB
    ws_t = w_spatial[0].reshape(N_GROUPS, g8, N_SUB, g8, N_SUB).transpose(0, 2, 1, 4, 3).reshape(N_GROUPS, BLOCK, BLOCK)
    bs_t = b_spatial[0].reshape(N_GROUPS, g8, N_SUB).transpose(2, 1, 0).reshape(BLOCK, N_GROUPS)
    idx = _local_index(0)
    causal = jnp.asarray((idx[:, None] >= idx[None, :]).astype(np.float32))
    buckets = jnp.asarray(_bucket_tables())
    bias = _bias_expand(rel_bias, buckets)

    hosted = gathering(w_out=8, w_ff1=1)
    (qkv,), st = _matmul("proj_qkv", xb, here("w_in"), "nn", [F32], n=3 * d_a, stages=hosted)
    settle(hosted, st)
    hosted = gathering(w_out=8, w_ff1=6)
    (rest,), st = _matmul("proj_rest", xb, here("w_in"), "nn", [F32], b_off=3 * d_a, n=d_in - 3 * d_a, stages=hosted)
    settle(hosted, st)
    fwd = []
    for p in range(3):
        hosted = gathering(**({"w_ff1": 2}, {"w_ff1": 5}, {"w_ff1": 2, "w_ff2": 1})[p])
        res, st = _attn_fwd(qkv, bias, p, stages=hosted)
        settle(hosted, st)
        fwd.append(res)
    hosted = gathering(w_ff2=1)
    (attn, attn_b, lse), st = _attn_combine([o for o, _ in fwd], [l for _, l in fwd], stages=hosted)
    settle(hosted, st)
    hosted = gathering(w_ff2=2)
    gmlp, st = _gmlp_fwd(rest, 0, ln_v_gain, ln_v_bias, ws_t, bs_t, causal, stages=hosted)
    settle(hosted, st)
    hosted = gathering(w_ff2=3)
    (ya,), st = _matmul("proj_a", attn_b, here("w_proj_a"), "nn", [BF16], stages=hosted)
    settle(hosted, st)
    gate_a, gate_b = 2 * d_b, 2 * d_b + D_MODEL

    def merge(acc, ya_, ga, gb):
        return acc, _sigmoid(ga) * ya_ + _sigmoid(gb) * acc

    hosted = gathering(w_ff2=5)
    (yb, merged), st = _matmul("proj_b_merge", gmlp, here("w_proj_b"), "nn", [BF16, BF16], merge,
                               [(ya, "mn", 0), (rest, "mn", gate_a), (rest, "mn", gate_b)], tn=256, stages=hosted)
    settle(hosted, st)
    hosted = gathering(w_ff2=3)
    (pre1,), st = _matmul("out_proj", merged, here("w_out"), "nn", [F32], lambda acc, x_: (ALPHA * x_ + acc,), [(xs, "mn", 0)], stages=hosted)
    settle(hosted, st)
    hosted = gathering(w_ff2=1)
    (xhat1, rstd1, h1b), st = _ln1_fwd(pre1, ln1_gain, ln1_bias, stages=hosted)
    settle(hosted, st)

    def relu2(acc, b_):
        r = jnp.maximum(acc + b_, 0.0)
        return r, r * r

    hosted = gathering()
    (relu, fb), st = _matmul("ff1", h1b, here("w_ff1"), "nn", [BF16, BF16], relu2, [(b_ff1, "row", 0)], stages=hosted)
    settle(hosted, st)
    alone("gather_w_ff2_sibling", gathering())
    (ff,), _ = _matmul("ff2", fb, here("w_ff2"), "nn", [F32], lambda acc, b_: (acc + b_,), [(b_ff2, "row", 0)], tn=512, tk=2048)

    core = lax.axis_index("c").astype(I32).reshape(1)
    factors, theirs, sib, pair, chips, reduced = {}, {}, {}, {}, {}, {}

    def grad_for_sibling(n, a, b, stages=(), half=None, of=None):
        factors[n] = (a, b, KINDS[of or n], half)
        theirs[n], outs = _dw(f"dw_{n}_sibling", a, b, KINDS[of or n], core, False, half=half, stages=stages)
        settle(stages, outs)

    def to_sibling(n):
        st = _to_sibling_stage(theirs[n])
        st.store = keep(sib, n)
        return st

    def grad_own(n, stages=()):
        a, b, kind, half = factors[n]
        pair[n], outs = _dw(f"dw_{n}_own", a, b, kind, core, True, add=sib[n], half=half, stages=stages)
        settle(stages, outs)
        chips[n] = lax.empty(pair[n].shape, BF16)
        reduced[n] = 0

    def reducing(**new):
        stages = []
        for n, units in new.items():
            st = _to_chips_stage(pair[n], chips[n], (reduced[n], units))
            st.store = keep(chips, n)
            reduced[n] += units
            stages.append(st)
        return stages

    def summed(n):
        assert reduced[n] == 16, (n, reduced[n])
        return chips[n]

    dpre2, dpre2b, g_ln2_gain, g_ln2_bias, g_b_ff2, loss_part = _ln2_loss_bwd(ff, xhat1, ln1_gain, ln1_bias, ln2_gain, ln2_bias, target)
    grad_for_sibling("w_ff2", fb, dpre2b)

    def relu2_bwd(acc, r):
        da = acc * (2.0 * r)
        return da, da

    hosted = [to_sibling("w_ff2")]
    (dab, g_b_ff1), st = _matmul("d_ff1", dpre2b, here("w_ff2"), "nt", [BF16], relu2_bwd, [(relu, "mn", 0)], colsums=(1,), stages=hosted)
    settle(hosted, st)
    grad_own("w_ff2")
    grad_for_sibling("w_ff1", h1b, dab, reducing(w_ff2=3))
    hosted = reducing(w_ff2=8) + [to_sibling("w_ff1")]
    (dh1,), st = _matmul("d_h1", dab, here("w_ff1"), "nt", [F32], lambda acc, d_: (acc + ALPHA * d_,), [(dpre2, "mn", 0)], stages=hosted)
    settle(hosted, st)
    grad_own("w_ff1", reducing(w_ff2=4))
    hosted = reducing(w_ff2=1)
    (dpre1, dpre1b, g_ln1_gain, g_ln1_bias), st = _ln1_bwd(dh1, xhat1, rstd1, ln1_gain, stages=hosted)
    settle(hosted, st)
    grad_for_sibling("w_out", merged, dpre1b, reducing(w_ff1=1))

    def merge_bwd(acc, ga, gb, ya_, yb_):
        sa, sb = _sigmoid(ga), _sigmoid(gb)
        return acc * sa, acc * sb, acc * ya_ * (sa * (1.0 - sa)), acc * yb_ * (sb * (1.0 - sb))

    hosted = reducing(w_ff1=6) + [to_sibling("w_out")]
    (dya, dyb, dga, dgb), st = _matmul("d_merge", dpre1b, here("w_out"), "nt", [BF16] * 4, merge_bwd,
                                       [(rest, "mn", gate_a), (rest, "mn", gate_b), (ya, "mn", 0), (yb, "mn", 0)], tn=256, stages=hosted)
    settle(hosted, st)
    grad_own("w_out", reducing(w_ff1=1))
    grad_for_sibling("w_proj_a", attn_b, dya)
    grad_for_sibling("w_proj_b", gmlp, dyb)
    hosted = reducing(w_ff1=1) + [to_sibling("w_proj_a"), to_sibling("w_proj_b")]
    (dattn,), st = _matmul("d_attn", dya, here("w_proj_a"), "nt", [F32], stages=hosted)
    settle(hosted, st)
    grad_own("w_proj_a")
    grad_own("w_proj_b")
    hosted = reducing(w_ff1=1)
    (dgmlp,), st = _matmul("d_gmlp", dyb, here("w_proj_b"), "nt", [F32], stages=hosted)
    settle(hosted, st)
    hosted = reducing(w_ff1=2)
    (du, dvb, dws_t, dbs_t, g_lnv_gain, g_lnv_bias), st = _gmlp_bwd(rest, 0, dgmlp, ln_v_gain, ln_v_bias, ws_t, bs_t, causal, stages=hosted)
    settle(hosted, st)
    delta = _attn_delta(dattn, attn)
    bwd = []
    for p in range(3):
        hosted = reducing(**({"w_ff1": 3}, {"w_ff1": 1, "w_out": 8}, {"w_out": 8, "w_proj_a": 8})[p])
        res, st = _attn_bwd(qkv, dattn, lse, delta, bias, p, stages=hosted)
        settle(hosted, st)
        bwd.append(res)
    g_rel_bias = _rel_bias_grad([b[3] for b in bwd], buckets)
    hosted = reducing(w_proj_a=8, w_proj_b=16)
    (dproj,), st = _assemble_dproj([b[i] for i in range(3) for b in bwd], du, dvb, dga, dgb, stages=hosted)
    settle(hosted, st)

    g_w_spatial = dws_t.reshape(N_GROUPS, N_SUB, g8, N_SUB, g8).transpose(0, 2, 1, 4, 3)
    g_b_spatial = dbs_t[:, :N_GROUPS].reshape(N_SUB, g8, N_GROUPS).transpose(2, 1, 0)
    part = _pack(dict(loss=loss_part, rel_bias=g_rel_bias, ln_v_gain=g_lnv_gain, ln_v_bias=g_lnv_bias, w_spatial=g_w_spatial,
                      b_spatial=g_b_spatial, ln1_gain=g_ln1_gain, ln1_bias=g_ln1_bias, b_ff1=g_b_ff1, b_ff2=g_b_ff2,
                      ln2_gain=g_ln2_gain, ln2_bias=g_ln2_bias))
    sib["small"] = lax.empty((N_DEV, *part.shape), F32)
    cut = part.shape[0] // 2 // SUBLANES * SUBLANES

    def small(rows):
        st = _small_stage(part, sib["small"], rows)
        st.store = keep(sib, "small")
        return st

    grad_for_sibling("w_in_top", xb, dproj, [small((0, cut))], half=0, of="w_in")
    grad_for_sibling("w_in_bot", xb, dproj, [small((cut, part.shape[0] - cut)), to_sibling("w_in_top")], half=1, of="w_in")
    parts = sib["small"]
    grad_own("w_in_top", [to_sibling("w_in_bot")])
    grad_own("w_in_bot", reducing(w_in_top=4))

    def add_residual(acc, d_):
        return (acc + ALPHA * d_,)

    hosted = reducing(w_in_top=12, w_in_bot=7)
    (dx,), st = _matmul("d_x", dproj, here("w_in"), "nt", [F32], add_residual, [(dpre1, "mn", 0)], tn=512, tk=3072, stages=hosted)
    settle(hosted, st)
    grad_x = _from_perm(dx)[None]

    out_g, out_d, out_m, out_v = {}, {}, {}, {}
    for n, units in (("w_out", 3), ("w_proj_a", 3), ("w_proj_b", 3), ("w_ff2", 0), ("w_ff1", 0)):
        hosted = reducing(w_in_bot=units) if units else []
        (g, d, nm, nv), st = _adam_shard(f"adam_{n}", pair[n], summed(n), chip, weights[n][0], mom1[n][0], mom2[n][0], stages=hosted)
        settle(hosted, st)
        out_g[n], out_d[n], out_m[n], out_v[n] = g[None], d[None], nm[None], nv[None]
    rows = weights["w_in"].shape[1] // 2
    done = None
    for i, n in enumerate(("w_in_top", "w_in_bot")):
        done, _ = _adam_shard(f"adam_{n}", pair[n], summed(n), chip, weights["w_in"][0], mom1["w_in"][0], mom2["w_in"][0],
                              rows=(i * rows, rows), into=done)
    out_g["w_in"], out_d["w_in"], out_m["w_in"], out_v["w_in"] = (t[None] for t in done)

    zero = jnp.zeros((1,), F32)
    sg, sd, sm, sv = (_unpack(b) for b in _adam_small(
        part, parts, me, _pack({**weights, "loss": zero}), _pack({**mom1, "loss": zero}), _pack({**mom2, "loss": zero})))
    for n in WEIGHT_ORDER:
        if n not in KINDS:
            shape = weights[n].shape
            out_g[n], out_d[n], out_m[n], out_v[n] = (t[n].reshape(shape) for t in (sg, sd, sm, sv))
    loss = sg["loss"].reshape(())
    return (loss, grad_x, *[out_g[n] for n in WEIGHT_ORDER], *[out_d[n] for n in WEIGHT_ORDER],
            *[out_m[n] for n in WEIGHT_ORDER], *[out_v[n] for n in WEIGHT_ORDER])
```

```python
import math

import jax
import jax.numpy as jnp
import numpy as np
from jax import lax
from jax.experimental import pallas as pl
from jax.experimental.pallas import tpu as pltpu

F32 = jnp.float32
BF16 = jnp.bfloat16
I32 = jnp.int32

SEQ = 2048
D_MODEL = 2048
HEAD_DIM = 128
N_HEADS = 8
N_GROUPS = 8
D_FF = 4 * D_MODEL
BLOCK = 128
DILATIONS = (1, 4, 16)
N_BUCKETS = 32
MAX_DISTANCE = 2048
ALPHA = 2.0 ** 0.25
LN_EPS = 1e-5
NEG_INF = -1e30
N_DEV = 8
N_CHIPS = 4
N_SUB = 16
ADAM_LR, ADAM_B1, ADAM_B2, ADAM_EPS, ADAM_WD, ADAM_STEP = 0.001, 0.9, 0.999, 1e-08, 0.01, 10
LANES = 128
SUBLANES = 8
VMEM_LIMIT = 56 * 1024 * 1024
MESH = pl.DeviceIdType.MESH
ANY = pl.BlockSpec(memory_space=pl.ANY)
WEIGHT_ORDER = ("w_in", "rel_bias", "ln_v_gain", "ln_v_bias", "w_spatial", "b_spatial", "w_proj_a", "w_proj_b", "w_out",
                "ln1_gain", "ln1_bias", "w_ff1", "b_ff1", "w_ff2", "b_ff2", "ln2_gain", "ln2_bias")
KINDS = {"w_in": "col", "w_proj_a": "col", "w_proj_b": "col", "w_out": "row", "w_ff1": "col", "w_ff2": "row"}


def _d_a():
    return N_HEADS * HEAD_DIM


def _d_b():
    return N_GROUPS * BLOCK


def _d_in():
    return 3 * _d_a() + 2 * _d_b() + 2 * D_MODEL


def _pick(t, n, *others):
    if n <= t and all(o % n == 0 for o in others):
        return n
    for c in range(min(t, n) // LANES * LANES, 0, -LANES):
        if n % c == 0 and all(o % c == 0 for o in others):
            return c
    raise ValueError((t, n, others))


class _Stage:
    def __init__(self, ins, outs, alias, sems, start, finish):
        self.ins, self.outs, self.alias, self.sems, self.start, self.finish = ins, outs, alias, sems, start, finish


def _call(name, body, grid, in_specs, out_specs, out_shape, operands, scratch=(), sem=None, stages=(), sequential=False, prefetch=None,
          shown=False, alias=None):
    n_in, n_out, n_sc = len(in_specs), len(out_specs), len(scratch)
    st_in = [len(s.ins) for s in stages]
    st_out = [len(s.outs) for s in stages]
    st_sem = [len(s.sems) for s in stages]
    n_pre = 0 if prefetch is None else 1
    aliases, ioff, ooff = {i + n_pre: o for i, o in (alias or {}).items()}, n_in + n_pre, n_out
    for s, ni, no in zip(stages, st_in, st_out):
        for i, o in s.alias.items():
            aliases[ioff + i] = ooff + o
        ioff, ooff = ioff + ni, ooff + no

    def split(refs, counts):
        out, at = [], 0
        for c in counts:
            out.append(refs[at:at + c])
            at += c
        return out

    def wrapped(*refs):
        ins, sins, outs, souts, sc, ssems = split(refs[n_pre:], [n_in, sum(st_in), n_out, sum(st_out), n_sc, sum(st_sem)])
        parts = list(zip(stages, split(sins, st_in), split(souts, st_out), split(ssems, st_sem)))
        if sequential:
            for s, a, b, c in parts:
                s.start(a, b, c)
            for s, a, b, c in parts:
                s.finish(a, b, c)
            return
        if parts:
            first = _all_of([pl.program_id(i) == 0 for i in range(len(grid))])
            last = _all_of([pl.program_id(i) == g - 1 for i, g in enumerate(grid)])

            @pl.when(first)
            def _():
                for s, a, b, c in parts:
                    s.start(a, b, c)

        body(*(refs[:n_pre] if shown else ()), *ins, *outs, *sc)
        if parts:
            @pl.when(last)
            def _():
                for s, a, b, c in parts:
                    s.finish(a, b, c)

    if stages or sem is None:
        sem = ("arbitrary",) * len(grid)
    specs = dict(grid=grid, in_specs=list(in_specs) + [ANY] * sum(st_in), out_specs=list(out_specs) + [ANY] * sum(st_out),
                 scratch_shapes=list(scratch) + [x for s in stages for x in s.sems])
    if prefetch is not None:
        specs = dict(grid_spec=pltpu.PrefetchScalarGridSpec(num_scalar_prefetch=1, **specs))
    res = pl.pallas_call(
        wrapped, name=name, out_shape=list(out_shape) + [o for s in stages for o in s.outs], input_output_aliases=aliases,
        compiler_params=pltpu.CompilerParams(dimension_semantics=sem, vmem_limit_bytes=VMEM_LIMIT), **specs,
    )(*([prefetch] if n_pre else []), *operands, *[a for s in stages for a in s.ins])
    res = list(res)
    return res[:n_out], split(res[n_out:], st_out)


def _all_of(conds):
    out = conds[0]
    for c in conds[1:]:
        out = out & c
    return out


def _coords():
    return lax.axis_index("x"), lax.axis_index("y"), lax.axis_index("c")


def _other_chips(x, y):
    return ((1 - x, y), (x, 1 - y), (1 - x, 1 - y))


def _lin(dev):
    return 4 * dev[0] + 2 * dev[1] + dev[2]


def _piece(total, lo, n, units=16):
    assert total % units == 0
    return lo * (total // units), n * (total // units)


FLOWS = 4


def _split(lo, cnt):
    k = next(k for k in (FLOWS, 2, 1) if cnt % (2 * SUBLANES * k) == 0)
    return [(lo + i * (cnt // k), cnt // k) for i in range(k)]


def _remote(src, dst, send, recv, to):
    return pltpu.make_async_remote_copy(src_ref=src, dst_ref=dst, send_sem=send, recv_sem=recv, device_id=to, device_id_type=MESH)


def _placer(kind, n, lo, cnt):
    def place(ref, dev):
        if kind == "col":
            return ref.at[pl.ds(lo, cnt), pl.ds(pl.multiple_of(_lin(dev) * n, LANES), n)]
        return ref.at[pl.ds(pl.multiple_of(_lin(dev) * n + lo, 2 * SUBLANES), cnt), :]
    return place


def _spread_stage(full, kind, piece=(0, 16), home=False):
    n = (full.shape[1] if kind == "col" else full.shape[0]) // N_DEV
    lo, cnt = _piece(full.shape[0] if kind == "col" else n, *piece)
    parts = _split(lo, cnt)
    npeers = 1 if home else 2

    def copies(ins, outs, sems):
        send, recv = sems
        x, y, c = _coords()
        me = (x, y, c)
        peers = [(x, y, 1 - c)] if home else [(1 - x, y, c), (x, 1 - y, c)]
        out, arrive = [], []
        for k, t in enumerate(peers):
            for i, (plo, pcnt) in enumerate(parts):
                place = _placer(kind, n, plo, pcnt)
                out.append(_remote(place(outs[0], me), place(outs[0], me), send.at[i, k], recv.at[i, k], t))
                arrive.append(_remote(place(outs[0], t), place(outs[0], t), send.at[i, k], recv.at[i, k], t))
        return out, arrive

    def start(ins, outs, sems):
        for cp in copies(ins, outs, sems)[0]:
            cp.start()

    def finish(ins, outs, sems):
        out, arrive = copies(ins, outs, sems)
        for cp in arrive:
            cp.wait_recv()
        for cp in out:
            cp.wait_send()

    return _Stage([full], [jax.ShapeDtypeStruct(full.shape, full.dtype)], {0: 0},
                  [pltpu.SemaphoreType.DMA((len(parts), npeers)), pltpu.SemaphoreType.DMA((len(parts), npeers))], start, finish)


def _relay_stage(full, kind, piece=(0, 16)):
    n = (full.shape[1] if kind == "col" else full.shape[0]) // N_DEV
    lo, cnt = _piece(full.shape[0] if kind == "col" else n, *piece)
    half = cnt // 2
    assert half % (2 * SUBLANES) == 0, (cnt, kind)
    tops, bottoms = _split(lo, half), _split(lo + half, half)

    def copies(ins, outs, sems):
        send, recv = sems
        x, y, c = _coords()
        xn, yn, dg = (1 - x, y, c), (x, 1 - y, c), (1 - x, 1 - y, c)
        out, arrive, k = [], [], 0
        for came_from, to, parts in ((yn, xn, tops), (xn, yn, bottoms)):
            for plo, pcnt in parts:
                place = _placer(kind, n, plo, pcnt)
                out.append(_remote(place(outs[0], came_from), place(outs[0], came_from), send.at[k], recv.at[k], to))
                arrive.append(_remote(place(outs[0], dg), place(outs[0], dg), send.at[k], recv.at[k], to))
                k += 1
        return out, arrive

    def start(ins, outs, sems):
        for cp in copies(ins, outs, sems)[0]:
            cp.start()

    def finish(ins, outs, sems):
        out, arrive = copies(ins, outs, sems)
        for cp in arrive:
            cp.wait_recv()
        for cp in out:
            cp.wait_send()

    return _Stage([full], [jax.ShapeDtypeStruct(full.shape, full.dtype)], {0: 0},
                  [pltpu.SemaphoreType.DMA((len(tops) + len(bottoms),)), pltpu.SemaphoreType.DMA((len(tops) + len(bottoms),))], start, finish)


def _forward_stage(full, kind, piece=(0, 16)):
    n = (full.shape[1] if kind == "col" else full.shape[0]) // N_DEV
    lo, cnt = _piece(full.shape[0] if kind == "col" else n, *piece)
    place = _placer(kind, n, lo, cnt)

    def copies(ins, outs, sems):
        send, recv = sems
        x, y, c = _coords()
        chips = _other_chips(x, y)
        out = [_remote(place(outs[0], (*chip, c)), place(outs[0], (*chip, c)), send.at[k], recv.at[k], (x, y, 1 - c)) for k, chip in enumerate(chips)]
        arrive = [_remote(place(outs[0], (*chip, 1 - c)), place(outs[0], (*chip, 1 - c)), send.at[k], recv.at[k], (x, y, 1 - c))
                  for k, chip in enumerate(chips)]
        return out, arrive

    def start(ins, outs, sems):
        for cp in copies(ins, outs, sems)[0]:
            cp.start()

    def finish(ins, outs, sems):
        out, arrive = copies(ins, outs, sems)
        for cp in arrive:
            cp.wait_recv()
        for cp in out:
            cp.wait_send()

    return _Stage([full], [jax.ShapeDtypeStruct(full.shape, full.dtype)], {0: 0},
                  [pltpu.SemaphoreType.DMA((3,)), pltpu.SemaphoreType.DMA((3,))], start, finish)


def _to_sibling_stage(theirs):
    def copies(ins, outs, sems):
        send, recv = sems
        x, y, c = _coords()
        return [_remote(ins[0].at[q], outs[0].at[q], send.at[q], recv.at[q], (x, y, 1 - c)) for q in range(N_CHIPS)]

    def start(ins, outs, sems):
        for cp in copies(ins, outs, sems):
            cp.start()

    def finish(ins, outs, sems):
        for cp in copies(ins, outs, sems):
            cp.wait()

    return _Stage([theirs], [jax.ShapeDtypeStruct(theirs.shape, BF16)], {},
                  [pltpu.SemaphoreType.DMA((N_CHIPS,)), pltpu.SemaphoreType.DMA((N_CHIPS,))], start, finish)


def _to_chips_stage(pair, dst, piece=(0, 16)):
    lo, cnt = _piece(pair.shape[1], *piece)
    parts = _split(lo, cnt)
    nsem = 3 * len(parts)

    def copies(ins, outs, sems):
        send, recv = sems
        x, y, c = _coords()
        mine = 2 * x + y
        out, arrive, k = [], [], 0
        for px, py in _other_chips(x, y):
            for plo, pcnt in parts:
                rows = pl.ds(plo, pcnt)
                out.append(_remote(ins[0].at[2 * px + py, rows, :], outs[0].at[mine, rows, :], send.at[k], recv.at[k], (px, py, c)))
                arrive.append(_remote(ins[0].at[2 * px + py, rows, :], outs[0].at[2 * px + py, rows, :], send.at[k], recv.at[k], (px, py, c)))
                k += 1
        return out, arrive

    def start(ins, outs, sems):
        for cp in copies(ins, outs, sems)[0]:
            cp.start()

    def finish(ins, outs, sems):
        out, arrive = copies(ins, outs, sems)
        for cp in arrive:
            cp.wait_recv()
        for cp in out:
            cp.wait_send()

    return _Stage([pair, dst], [jax.ShapeDtypeStruct(dst.shape, dst.dtype)], {1: 0},
                  [pltpu.SemaphoreType.DMA((nsem,)), pltpu.SemaphoreType.DMA((nsem,))], start, finish)


def _fuse(parts, ins, outs, alias):
    parts = [p for p in parts if p is not None]
    sems = [x for st, _, _ in parts for x in st.sems]

    def run(which):
        def go(i, o, s):
            refs, at = list(i) + list(o), 0
            for st, pi, po in parts:
                getattr(st, which)([refs[k] for k in pi], [refs[k] for k in po], s[at:at + len(st.sems)])
                at += len(st.sems)
        return go

    return _Stage(ins, [jax.ShapeDtypeStruct(o.shape, o.dtype) for o in outs], alias, sems, run("start"), run("finish"))


def _gather_stage(full, kind, new=None, relay=None, forward=None):
    return _fuse([(_spread_stage(full, kind, new), [0], [1]) if new else None,
                  (_relay_stage(full, kind, relay), [0], [1]) if relay else None,
                  (_spread_stage(full, kind, relay, home=True), [0], [1]) if relay else None,
                  (_forward_stage(full, kind, forward), [0], [1]) if forward else None], [full], [full], {0: 0})


def _small_stage(part, dst, rows):
    piece = pl.ds(*rows)

    def copies(ins, outs, sems):
        send, recv = sems
        x, y, c = _coords()
        me = (x, y, c)
        peers = [(1 - x if k & 4 else x, 1 - y if k & 2 else y, 1 - c if k & 1 else c) for k in range(1, N_DEV)]
        out = [_remote(ins[0].at[piece, :], outs[0].at[_lin(me), piece, :], send.at[k], recv.at[k], t) for k, t in enumerate(peers)]
        arrive = [_remote(ins[0].at[piece, :], outs[0].at[_lin(t), piece, :], send.at[k], recv.at[k], t) for k, t in enumerate(peers)]
        return out, arrive

    def start(ins, outs, sems):
        for cp in copies(ins, outs, sems)[0]:
            cp.start()

    def finish(ins, outs, sems):
        out, arrive = copies(ins, outs, sems)
        for cp in arrive:
            cp.wait_recv()
        for cp in out:
            cp.wait_send()

    return _Stage([part, dst], [jax.ShapeDtypeStruct(dst.shape, F32)], {1: 0},
                  [pltpu.SemaphoreType.DMA((N_DEV - 1,)), pltpu.SemaphoreType.DMA((N_DEV - 1,))], start, finish)


def _comm_only(name, stages):
    return _call(name, lambda: None, (1,), [], [], [], [], stages=stages, sequential=True)[1]


_GELU_C = math.sqrt(2.0 / math.pi)


def _gelu(x):
    return 0.5 * x * (1.0 + jnp.tanh(_GELU_C * (x + 0.044715 * x * x * x)))


def _gelu_grad(x):
    t = jnp.tanh(_GELU_C * (x + 0.044715 * x * x * x))
    return 0.5 * (1.0 + t) + 0.5 * x * (1.0 - t * t) * (_GELU_C * (1.0 + 3.0 * 0.044715 * x * x))


def _sigmoid(x):
    return 1.0 / (1.0 + jnp.exp(-x))


def _dot(a, b, mode):
    dims = {"nn": (((1,), (0,)), ((), ())), "nt": (((1,), (1,)), ((), ())), "tn": (((0,), (0,)), ((), ()))}[mode]
    return lax.dot_general(a.astype(BF16), b.astype(BF16), dims, preferred_element_type=F32)


def _matmul(name, a, b, mode, outs, epi=None, extras=(), colsums=(), tm=2048, tn=512, tk=2048, b_off=0, n=None, m_off=0, m=None, stages=()):
    if mode == "tn":
        kk, mfull = a.shape
    else:
        mfull, kk = a.shape
    m = mfull if m is None else m
    n = (b.shape[0] if mode == "nt" else b.shape[1]) if n is None else n
    tm, tk = _pick(tm, m, m_off), _pick(tk, kk)
    tn = _pick(tn, n, b_off, *[off for _, _, off in extras])
    boff, moff = b_off // tn, m_off // tm
    nm, nn_, nk = m // tm, n // tn, kk // tk
    col_major = bool(colsums)
    grid = (nn_, nm, nk) if col_major else (nm, nn_, nk)

    def imap(f):
        if col_major:
            return lambda g0, g1, k: f(g1, g0, k)
        return f

    a_spec = (pl.BlockSpec((tk, tm), imap(lambda i, j, k: (k, i + moff))) if mode == "tn"
              else pl.BlockSpec((tm, tk), imap(lambda i, j, k: (i + moff, k))))
    b_spec = (pl.BlockSpec((tn, tk), imap(lambda i, j, k: (j + boff, k))) if mode == "nt"
              else pl.BlockSpec((tk, tn), imap(lambda i, j, k: (k, j + boff))))
    in_specs, operands = [a_spec, b_spec], [a, b]
    for arr, kind, off in extras:
        o = off // tn
        if kind == "mn":
            in_specs.append(pl.BlockSpec((tm, tn), imap(lambda i, j, k, o=o: (i + moff, j + o))))
        else:
            in_specs.append(pl.BlockSpec((1, tn), imap(lambda i, j, k, o=o: (0, j + o))))
        operands.append(arr)
    out_shape = [jax.ShapeDtypeStruct((m, n), dt) for dt in outs] + [jax.ShapeDtypeStruct((1, n), F32) for _ in colsums]
    out_specs = ([pl.BlockSpec((tm, tn), imap(lambda i, j, k: (i, j))) for _ in outs]
                 + [pl.BlockSpec((1, tn), imap(lambda i, j, k: (0, j))) for _ in colsums])
    n_ex, n_out, n_cs = len(extras), len(outs), len(colsums)

    def body(*refs):
        a_ref, b_ref = refs[:2]
        ex_refs = refs[2:2 + n_ex]
        out_refs = refs[2 + n_ex:2 + n_ex + n_out]
        cs_refs = refs[2 + n_ex + n_out:2 + n_ex + n_out + n_cs]
        part = _dot(a_ref[...], b_ref[...], mode)

        def finish(acc):
            res = epi(acc, *[r[...] for r in ex_refs]) if epi is not None else (acc,)
            for r, v in zip(out_refs, res[:n_out]):
                r[...] = v.astype(r.dtype)
            if n_cs:
                @pl.when(pl.program_id(1) == 0)
                def _():
                    for r in cs_refs:
                        r[...] = jnp.zeros_like(r)

                for r, idx in zip(cs_refs, colsums):
                    r[...] += jnp.sum(res[idx], axis=0, keepdims=True)

        if nk == 1:
            finish(part)
        else:
            acc_ref = refs[-1]
            k = pl.program_id(2)

            @pl.when(k == 0)
            def _():
                acc_ref[...] = part

            @pl.when(k > 0)
            def _():
                acc_ref[...] += part

            @pl.when(k == nk - 1)
            def _():
                finish(acc_ref[...])

    sem = ("arbitrary", "arbitrary", "arbitrary") if col_major else ("parallel", "parallel", "arbitrary")
    return _call(name, body, grid, in_specs, out_specs, out_shape, operands,
                 scratch=[pltpu.VMEM((tm, tn), F32)] if nk > 1 else [], sem=sem, stages=stages)


def _project_shards(name, a, b, which, into=None, stages=()):
    m, kk = a.shape
    n = b.shape[1]
    tn = n // N_DEV
    o_spec = pl.BlockSpec((m, tn), lambda s, w_ref: (0, w_ref[s]))

    def body(a_ref, b_ref, *rest):
        rest[-1][...] = _dot(a_ref[...], b_ref[...], "nn")

    (out,), st = _call(name, body, (which.shape[0],),
                       [pl.BlockSpec((m, kk), lambda s, w_ref: (0, 0)), pl.BlockSpec((kk, tn), lambda s, w_ref: (0, w_ref[s]))]
                       + ([ANY] if into is not None else []), [o_spec], [jax.ShapeDtypeStruct((m, n), F32)],
                       [a, b] + ([into] if into is not None else []), sem=("arbitrary",), stages=stages, prefetch=which,
                       alias={2: 0} if into is not None else None)
    return out, st


def _row_spec(tr, c):
    return pl.BlockSpec((tr, c), lambda i: (i, 0))


def _fix_spec(shape):
    return pl.BlockSpec(shape, lambda *_: tuple(0 for _ in shape))


def _cast_bf16(name, x, tr=512, stages=()):
    r, c = x.shape
    tr = _pick(tr, r)

    def body(x_ref, o_ref):
        o_ref[...] = x_ref[...].astype(BF16)

    (out,), st = _call(name, body, (r // tr,), [_row_spec(tr, c)], [_row_spec(tr, c)], [jax.ShapeDtypeStruct((r, c), BF16)], [x],
                       sem=("parallel",), stages=stages)
    return out, st


def _cast_into_place(name, w, kind, me, tr=512, stages=()):
    r, c = w.shape
    tr = _pick(tr, r)
    nb = r // tr
    if kind == "col":
        o_spec = pl.BlockSpec((tr, c), lambda i, me_ref: (i, me_ref[0]))
        shape = (r, c * N_DEV)
    else:
        o_spec = pl.BlockSpec((tr, c), lambda i, me_ref: (me_ref[0] * nb + i, 0))
        shape = (r * N_DEV, c)

    def body(x_ref, o_ref):
        o_ref[...] = x_ref[...].astype(BF16)

    (out,), st = _call(name, body, (nb,), [pl.BlockSpec((tr, c), lambda i, me_ref: (i, 0))], [o_spec], [jax.ShapeDtypeStruct(shape, BF16)],
                       [w], sem=("parallel",), prefetch=me, stages=stages)
    return out, st


def _layer_norm_stats(x):
    mean = jnp.mean(x, axis=-1, keepdims=True)
    xc = x - mean
    var = jnp.mean(xc * xc, axis=-1, keepdims=True)
    rstd = lax.rsqrt(var + LN_EPS)
    return xc * rstd, rstd


def _layer_norm_bwd(dxhat, xhat, rstd):
    m1 = jnp.mean(dxhat, axis=-1, keepdims=True)
    m2 = jnp.mean(dxhat * xhat, axis=-1, keepdims=True)
    return rstd * (dxhat - m1 - xhat * m2)


def _ln1_fwd(pre1, g1, b1, tr=256, stages=()):
    s, d = pre1.shape
    tr = _pick(tr, s)

    def body(p_ref, g_ref, b_ref, xh_ref, rs_ref, h_ref):
        xhat, rstd = _layer_norm_stats(p_ref[...])
        xh_ref[...] = xhat
        rs_ref[...] = rstd
        h_ref[...] = (xhat * g_ref[...] + b_ref[...]).astype(BF16)

    return _call("ln1_fwd", body, (s // tr,), [_row_spec(tr, d), _fix_spec((1, d)), _fix_spec((1, d))],
                 [_row_spec(tr, d), _row_spec(tr, 1), _row_spec(tr, d)],
                 [jax.ShapeDtypeStruct((s, d), F32), jax.ShapeDtypeStruct((s, 1), F32), jax.ShapeDtypeStruct((s, d), BF16)],
                 [pre1, g1, b1], sem=("parallel",), stages=stages)


def _ln2_loss_bwd(ff, xhat1, g1, b1, g2, b2, target, tr=256):
    s, d = ff.shape
    tr = _pick(tr, s)

    def body(ff_ref, xh1_ref, g1_ref, b1_ref, g2_ref, b2_ref, t_ref, dp_ref, dpb_ref, dg_ref, db_ref, dbf_ref, loss_ref):
        @pl.when(pl.program_id(0) == 0)
        def _():
            dg_ref[...] = jnp.zeros_like(dg_ref)
            db_ref[...] = jnp.zeros_like(db_ref)
            dbf_ref[...] = jnp.zeros_like(dbf_ref)
            loss_ref[...] = jnp.zeros_like(loss_ref)

        h1 = xh1_ref[...] * g1_ref[...] + b1_ref[...]
        xhat, rstd = _layer_norm_stats(ALPHA * h1 + ff_ref[...])
        err = xhat * g2_ref[...] + b2_ref[...] - t_ref[...]
        row = jnp.mean(err * err, axis=-1, keepdims=True)
        loss_ref[...] += 0.5 * jnp.sum(row, axis=0, keepdims=True)
        dy = err / d
        dg_ref[...] += jnp.sum(dy * xhat, axis=0, keepdims=True)
        db_ref[...] += jnp.sum(dy, axis=0, keepdims=True)
        dpre = _layer_norm_bwd(dy * g2_ref[...], xhat, rstd)
        dbf_ref[...] += jnp.sum(dpre, axis=0, keepdims=True)
        dp_ref[...] = dpre
        dpb_ref[...] = dpre.astype(BF16)

    vec = _fix_spec((1, d))
    return _call("ln2_loss_bwd", body, (s // tr,), [_row_spec(tr, d), _row_spec(tr, d), vec, vec, vec, vec, _row_spec(tr, d)],
                 [_row_spec(tr, d), _row_spec(tr, d), vec, vec, vec, _fix_spec((1, 1))],
                 [jax.ShapeDtypeStruct((s, d), F32), jax.ShapeDtypeStruct((s, d), BF16)]
                 + [jax.ShapeDtypeStruct((1, d), F32)] * 3 + [jax.ShapeDtypeStruct((1, 1), F32)],
                 [ff, xhat1, g1, b1, g2, b2, target])[0]


def _ln1_bwd(dh1, xhat1, rstd1, g1, tr=256, stages=()):
    s, d = dh1.shape
    tr = _pick(tr, s)

    def body(dh_ref, xh_ref, rs_ref, g_ref, dp_ref, dpb_ref, dg_ref, db_ref):
        @pl.when(pl.program_id(0) == 0)
        def _():
            dg_ref[...] = jnp.zeros_like(dg_ref)
            db_ref[...] = jnp.zeros_like(db_ref)

        dh, xhat = dh_ref[...], xh_ref[...]
        dg_ref[...] += jnp.sum(dh * xhat, axis=0, keepdims=True)
        db_ref[...] += jnp.sum(dh, axis=0, keepdims=True)
        dpre = _layer_norm_bwd(dh * g_ref[...], xhat, rs_ref[...])
        dp_ref[...] = dpre
        dpb_ref[...] = dpre.astype(BF16)

    vec = _fix_spec((1, d))
    return _call("ln1_bwd", body, (s // tr,), [_row_spec(tr, d), _row_spec(tr, d), _row_spec(tr, 1), vec],
                 [_row_spec(tr, d), _row_spec(tr, d), vec, vec],
                 [jax.ShapeDtypeStruct((s, d), F32), jax.ShapeDtypeStruct((s, d), BF16)] + [jax.ShapeDtypeStruct((1, d), F32)] * 2,
                 [dh1, xhat1, rstd1, g1], stages=stages)


def _to_perm(x):
    return x.reshape(SEQ // N_SUB, N_SUB, -1).transpose(1, 0, 2).reshape(SEQ, -1)


def _from_perm(x):
    return x.reshape(N_SUB, SEQ // N_SUB, -1).transpose(1, 0, 2).reshape(SEQ, -1)


def _local_index(p):
    rho = np.arange(BLOCK)
    if p == 0:
        return 16 * (rho % 8) + rho // 8
    if p == 1:
        return 4 * (rho % 32) + rho // 32
    return rho


def _tile_view(x, p):
    c = x.shape[1]
    if p == 1:
        return x.reshape(4, 4, BLOCK, c)
    return x.reshape(N_SUB, BLOCK, c)


def _view_shape(c, p):
    return (4, 4, BLOCK, c) if p == 1 else (N_SUB, BLOCK, c)


def _tile_spec(p, width, col, shift=0):
    nblk = SEQ // DILATIONS[p] // BLOCK

    def blk(n):
        return jnp.clip(n + shift, 0, nblk - 1)

    if p == 0:
        return pl.BlockSpec((N_SUB, SUBLANES, width), lambda s, n: (0, blk(n), col))
    if p == 1:
        return pl.BlockSpec((4, None, 32, width), lambda s, n: (0, s, blk(n), col))
    return pl.BlockSpec((None, BLOCK, width), lambda s, n: (s, 0, col))


def _tile_grid(p):
    return ((1, 16), (4, 4), (16, 1))[p]


def _t5_bucket(n):
    max_exact = N_BUCKETS // 2
    nf = np.maximum(n, 1).astype(np.float32)
    large = max_exact + (np.log(nf / np.float32(max_exact)) / np.float32(math.log(MAX_DISTANCE / max_exact))
                         * np.float32(N_BUCKETS - max_exact)).astype(np.int32)
    large = np.minimum(large, N_BUCKETS - 1)
    return np.where(n < max_exact, n, large).astype(np.int32)


def _bucket_tables():
    tabs = np.zeros((3, 2, BLOCK, BLOCK), np.int32)
    for p, d in enumerate(DILATIONS):
        i = _local_index(p)
        diff = i[:, None] - i[None, :]
        tabs[p, 0] = np.where(diff <= 0, _t5_bucket((BLOCK + diff) * d), -1)
        tabs[p, 1] = np.where(diff >= 0, _t5_bucket(np.maximum(diff, 0) * d), -1)
    return tabs


def _bias_expand(rel_bias, buckets, stages=()):
    nh = N_HEADS

    def body(rb_ref, bk_ref, o_ref):
        for w in range(2):
            bk = bk_ref[0, w]
            for h in range(nh):
                val = jnp.zeros((BLOCK, BLOCK), F32)
                for b in range(N_BUCKETS):
                    val = jnp.where(bk == b, rb_ref[b, h], val)
                o_ref[0, h, w] = jnp.where(bk < 0, NEG_INF, val)

    return _call("bias_expand", body, (3,),
                 [pl.BlockSpec(memory_space=pltpu.SMEM), pl.BlockSpec((1, 2, BLOCK, BLOCK), lambda p: (p, 0, 0, 0))],
                 [pl.BlockSpec((1, nh, 2, BLOCK, BLOCK), lambda p: (p, 0, 0, 0, 0))],
                 [jax.ShapeDtypeStruct((3, nh, 2, BLOCK, BLOCK), F32)], [rel_bias, buckets], sem=("parallel",), stages=stages)


def _heads_to_lanes(cols):
    lane = lax.broadcasted_iota(I32, (BLOCK, LANES), 1)
    out = jnp.zeros((BLOCK, LANES), F32)
    for h, c in enumerate(cols):
        out = jnp.where(lane == h, c, out)
    return out


def _attn_fwd(qkv, bias, p, stages=()):
    d_a = _d_a()
    has_prev = SEQ // DILATIONS[p] // BLOCK > 1
    scale = HEAD_DIM ** -0.5
    view = _tile_view(qkv, p)

    width = 2 * BLOCK if has_prev else BLOCK

    def body(q_ref, kc_ref, kp_ref, vc_ref, vp_ref, b_ref, o_ref, l_ref, s_ref, p_ref):
        n = pl.program_id(1)
        q_all = q_ref[...].reshape(BLOCK, d_a).astype(BF16)
        k_all = kc_ref[...].reshape(BLOCK, d_a).astype(BF16)
        v_all = vc_ref[...].reshape(BLOCK, d_a).astype(BF16)
        if has_prev:
            k_all = jnp.concatenate([kp_ref[...].reshape(BLOCK, d_a).astype(BF16), k_all], axis=0)
            v_all = jnp.concatenate([vp_ref[...].reshape(BLOCK, d_a).astype(BF16), v_all], axis=0)
            no_prev = (lax.broadcasted_iota(I32, (BLOCK, width), 1) < BLOCK) & (n == 0)
        for h in range(N_HEADS):
            sl = slice(h * HEAD_DIM, (h + 1) * HEAD_DIM)
            s = _dot(q_all[:, sl], k_all[:, sl], "nt") * scale
            if has_prev:
                s = jnp.where(no_prev, NEG_INF, s + jnp.concatenate([b_ref[0, h, 0], b_ref[0, h, 1]], axis=1))
            else:
                s = s + b_ref[0, h, 1]
            s_ref[h] = s
        dens, lses = [], []
        for h in range(N_HEADS):
            s = s_ref[h]
            m = jnp.max(s, axis=-1, keepdims=True)
            pr = jnp.exp(s - m)
            den = jnp.sum(pr, axis=-1, keepdims=True)
            p_ref[h] = pr.astype(BF16)
            dens.append(den)
            lses.append(m + jnp.log(den))
        for h in range(N_HEADS):
            sl = slice(h * HEAD_DIM, (h + 1) * HEAD_DIM)
            o_ref[..., sl] = (_dot(p_ref[h], v_all[:, sl], "nn") / dens[h]).reshape(*o_ref.shape[:-1], HEAD_DIM)
        l_ref[...] = _heads_to_lanes(lses).reshape(l_ref.shape)

    (o, l), st = _call(
        f"attn_fwd{p}", body, _tile_grid(p),
        [_tile_spec(p, d_a, 0), _tile_spec(p, d_a, 1), _tile_spec(p, d_a, 1, -1), _tile_spec(p, d_a, 2), _tile_spec(p, d_a, 2, -1),
         pl.BlockSpec((1, N_HEADS, 2, BLOCK, BLOCK), lambda s, n: (p, 0, 0, 0, 0))],
        [_tile_spec(p, d_a, 0), _tile_spec(p, LANES, 0)],
        [jax.ShapeDtypeStruct(_view_shape(d_a, p), F32), jax.ShapeDtypeStruct(_view_shape(LANES, p), F32)],
        [view, view, view, view, view, bias], scratch=[pltpu.VMEM((N_HEADS, BLOCK, width), F32), pltpu.VMEM((N_HEADS, BLOCK, width), BF16)],
        sem=("parallel", "parallel"), stages=stages)
    return (o.reshape(SEQ, d_a), l.reshape(SEQ, LANES)), st


def _attn_combine(os_, ls_, tr=256, stages=()):
    d_a = _d_a()
    tr = _pick(tr, SEQ)

    def body(o0, o1, o2, l0, l1, l2, a_ref, ab_ref, lt_ref):
        l = [l0[...], l1[...], l2[...]]
        m = jnp.maximum(jnp.maximum(l[0], l[1]), l[2])
        w = [jnp.exp(x - m) for x in l]
        tot = w[0] + w[1] + w[2]
        lt_ref[...] = m + jnp.log(tot)
        w = [x / tot for x in w]
        for h in range(N_HEADS):
            sl = slice(h * HEAD_DIM, (h + 1) * HEAD_DIM)
            acc = w[0][:, h:h + 1] * o0[:, sl] + w[1][:, h:h + 1] * o1[:, sl] + w[2][:, h:h + 1] * o2[:, sl]
            a_ref[:, sl] = acc
            ab_ref[:, sl] = acc.astype(BF16)

    return _call("attn_combine", body, (SEQ // tr,), [_row_spec(tr, d_a)] * 3 + [_row_spec(tr, LANES)] * 3,
                 [_row_spec(tr, d_a), _row_spec(tr, d_a), _row_spec(tr, LANES)],
                 [jax.ShapeDtypeStruct((SEQ, d_a), F32), jax.ShapeDtypeStruct((SEQ, d_a), BF16), jax.ShapeDtypeStruct((SEQ, LANES), F32)],
                 [*os_, *ls_], sem=("parallel",), stages=stages)


def _attn_delta(dattn, attn, tr=256):
    d_a = _d_a()
    tr = _pick(tr, SEQ)

    def body(d_ref, a_ref, o_ref):
        prod = d_ref[...] * a_ref[...]
        lane = lax.broadcasted_iota(I32, (tr, LANES), 1)
        out = jnp.zeros((tr, LANES), F32)
        for h in range(N_HEADS):
            out = jnp.where(lane == h, jnp.sum(prod[:, h * HEAD_DIM:(h + 1) * HEAD_DIM], axis=-1, keepdims=True), out)
        o_ref[...] = out

    return _call("attn_delta", body, (SEQ // tr,), [_row_spec(tr, d_a)] * 2, [_row_spec(tr, LANES)],
                 [jax.ShapeDtypeStruct((SEQ, LANES), F32)], [dattn, attn], sem=("parallel",))[0][0]


def _attn_bwd(qkv, dattn, lse, delta, bias, p, stages=()):
    d_a = _d_a()
    nblk = SEQ // DILATIONS[p] // BLOCK
    has_next = nblk > 1
    scale = HEAD_DIM ** -0.5
    qv, dov, lv, tv = (_tile_view(x, p) for x in (qkv, dattn, lse, delta))

    rows = 2 * BLOCK if has_next else BLOCK

    def body(q_ref, qn_ref, k_ref, v_ref, do_ref, don_ref, l_ref, ln_ref, t_ref, tn_ref, b_ref, dq_ref, dk_ref, dv_ref, db_ref,
             carry_ref, s_ref, dp_ref, p_ref, ds_ref):
        j = pl.program_id(1)

        @pl.when((pl.program_id(0) == 0) & (j == 0))
        def _():
            db_ref[...] = jnp.zeros_like(db_ref)

        def both(cur, nxt, width, dtype):
            cur = cur[...].reshape(BLOCK, width).astype(dtype)
            return jnp.concatenate([cur, nxt[...].reshape(BLOCK, width).astype(dtype)], axis=0) if has_next else cur

        k_all = k_ref[...].reshape(BLOCK, d_a).astype(BF16)
        v_all = v_ref[...].reshape(BLOCK, d_a).astype(BF16)
        q_all, do_all = both(q_ref, qn_ref, d_a, BF16), both(do_ref, don_ref, d_a, BF16)
        l_all, t_all = both(l_ref, ln_ref, LANES, F32), both(t_ref, tn_ref, LANES, F32)
        if has_next:
            no_next = (lax.broadcasted_iota(I32, (rows, BLOCK), 0) >= BLOCK) & (j == nblk - 1)
        for h in range(N_HEADS):
            sl = slice(h * HEAD_DIM, (h + 1) * HEAD_DIM)
            s = _dot(q_all[:, sl], k_all[:, sl], "nt") * scale
            if has_next:
                s = jnp.where(no_next, NEG_INF, s + jnp.concatenate([b_ref[0, h, 1], b_ref[0, h, 0]], axis=0))
            else:
                s = s + b_ref[0, h, 1]
            s_ref[h] = s
            dp_ref[h] = _dot(do_all[:, sl], v_all[:, sl], "nt")
        for h in range(N_HEADS):
            pr = jnp.exp(s_ref[h] - l_all[:, h:h + 1])
            ds = pr * (dp_ref[h] - t_all[:, h:h + 1])
            db_ref[h, 1] += ds[:BLOCK]
            if has_next:
                db_ref[h, 0] += ds[BLOCK:]
            p_ref[h] = pr.astype(BF16)
            ds_ref[h] = ds.astype(BF16)
        for h in range(N_HEADS):
            sl = slice(h * HEAD_DIM, (h + 1) * HEAD_DIM)
            dq = _dot(ds_ref[h], k_all[:, sl], "nn") * scale
            mine = dq[:BLOCK]
            if has_next:
                mine = mine + jnp.where(j > 0, carry_ref[:, sl], 0.0)
            dq_ref[..., sl] = mine.reshape(*dq_ref.shape[:-1], HEAD_DIM)
            if has_next:
                carry_ref[:, sl] = dq[BLOCK:]
            dk_ref[..., sl] = (_dot(ds_ref[h], q_all[:, sl], "tn") * scale).reshape(*dk_ref.shape[:-1], HEAD_DIM)
            dv_ref[..., sl] = _dot(p_ref[h], do_all[:, sl], "tn").reshape(*dv_ref.shape[:-1], HEAD_DIM)

    def big(col, shift=0):
        return _tile_spec(p, d_a, col, shift)

    def small(shift=0):
        return _tile_spec(p, LANES, 0, shift)

    (dq, dk, dv, dbias), st = _call(
        f"attn_bwd{p}", body, _tile_grid(p),
        [big(0), big(0, 1), big(1), big(2), big(0), big(0, 1), small(), small(1), small(), small(1),
         pl.BlockSpec((1, N_HEADS, 2, BLOCK, BLOCK), lambda s, n: (p, 0, 0, 0, 0))],
        [big(0), big(0), big(0), pl.BlockSpec((N_HEADS, 2, BLOCK, BLOCK), lambda s, n: (0, 0, 0, 0))],
        [jax.ShapeDtypeStruct(_view_shape(d_a, p), F32)] * 3 + [jax.ShapeDtypeStruct((N_HEADS, 2, BLOCK, BLOCK), F32)],
        [qv, qv, qv, qv, dov, dov, lv, lv, tv, tv, bias],
        scratch=[pltpu.VMEM((BLOCK, d_a), F32), pltpu.VMEM((N_HEADS, rows, BLOCK), F32), pltpu.VMEM((N_HEADS, rows, BLOCK), F32),
                 pltpu.VMEM((N_HEADS, rows, BLOCK), BF16), pltpu.VMEM((N_HEADS, rows, BLOCK), BF16)], stages=stages)
    return (dq.reshape(SEQ, d_a), dk.reshape(SEQ, d_a), dv.reshape(SEQ, d_a), dbias), st


def _rel_bias_grad(dbias, buckets):
    nh = N_HEADS

    def body(d0, d1, d2, bk_ref, o_ref, t_ref):
        ds = (d0, d1, d2)

        def per_bucket(b, carry):
            for h in range(nh):
                acc = jnp.zeros((BLOCK, BLOCK), F32)
                for p in range(3):
                    for w in range(2):
                        acc = acc + jnp.where(bk_ref[p, w] == b, ds[p][h, w], 0.0)
                t_ref[pl.ds(b * nh + h, 1), :] = jnp.sum(acc, axis=0, keepdims=True)
            return carry

        lax.fori_loop(0, N_BUCKETS, per_bucket, 0)
        o_ref[...] = jnp.sum(t_ref[...], axis=-1, keepdims=True)

    return _call("rel_bias_grad", body, (1,), [_fix_spec((nh, 2, BLOCK, BLOCK))] * 3 + [_fix_spec((3, 2, BLOCK, BLOCK))],
                 [_fix_spec((N_BUCKETS * nh, 1))], [jax.ShapeDtypeStruct((N_BUCKETS * nh, 1), F32)], [*dbias, buckets],
                 scratch=[pltpu.VMEM((N_BUCKETS * nh, LANES), F32)])[0][0]


def _gmlp_fwd(rest, col0, gain, bias, ws, bs, causal, stages=()):
    d_b = _d_b()

    def body(u_ref, v_ref, g_ref, b_ref, ws_ref, bs_ref, c_ref, o_ref):
        u = u_ref[...].reshape(BLOCK, d_b)
        xhat, _ = _layer_norm_stats(_gelu(v_ref[...].reshape(BLOCK, d_b)))
        vn = (xhat * g_ref[...] + b_ref[...]).astype(BF16)
        outs = []
        for g in range(N_GROUPS):
            sl = slice(g * BLOCK, (g + 1) * BLOCK)
            w = jnp.where(c_ref[...] > 0, ws_ref[g], 0.0)
            z = _dot(w, vn[:, sl], "nn") + bs_ref[:, g:g + 1]
            outs.append(_gelu(u[:, sl]) * z)
        o_ref[...] = jnp.concatenate(outs, axis=-1).reshape(o_ref.shape)

    (out,), st = _call(
        "gmlp_fwd", body, (1, SEQ // BLOCK),
        [_tile_spec(0, d_b, col0), _tile_spec(0, d_b, col0 + 1), _fix_spec((1, d_b)), _fix_spec((1, d_b)),
         _fix_spec((N_GROUPS, BLOCK, BLOCK)), _fix_spec((BLOCK, N_GROUPS)), _fix_spec((BLOCK, BLOCK))],
        [_tile_spec(0, d_b, 0)], [jax.ShapeDtypeStruct(_view_shape(d_b, 0), F32)],
        [_tile_view(rest, 0), _tile_view(rest, 0), gain, bias, ws, bs, causal], sem=("parallel", "parallel"), stages=stages)
    return out.reshape(SEQ, d_b), st


def _gmlp_bwd(rest, col0, dgmlp, gain, bias, ws, bs, causal, stages=()):
    d_b = _d_b()
    nchunk = SEQ // BLOCK

    def body(u_ref, v_ref, dg_ref, g_ref, b_ref, ws_ref, bs_ref, c_ref, du_ref, dv_ref, dws_ref, dbs_ref, dgain_ref, dbias_ref):
        c = pl.program_id(1)

        @pl.when(c == 0)
        def _():
            dws_ref[...] = jnp.zeros_like(dws_ref)
            dbs_ref[...] = jnp.zeros_like(dbs_ref)
            dgain_ref[...] = jnp.zeros_like(dgain_ref)
            dbias_ref[...] = jnp.zeros_like(dbias_ref)

        u = u_ref[...].reshape(BLOCK, d_b)
        v = v_ref[...].reshape(BLOCK, d_b)
        dgm = dg_ref[...].reshape(BLOCK, d_b)
        xhat, rstd = _layer_norm_stats(_gelu(v))
        vn = (xhat * g_ref[...] + b_ref[...]).astype(BF16)
        lane = lax.broadcasted_iota(I32, (BLOCK, LANES), 1)
        dus, dvns = [], []
        dbs = dbs_ref[...]
        for g in range(N_GROUPS):
            sl = slice(g * BLOCK, (g + 1) * BLOCK)
            w = jnp.where(c_ref[...] > 0, ws_ref[g], 0.0).astype(BF16)
            z = _dot(w, vn[:, sl], "nn") + bs_ref[:, g:g + 1]
            dz = dgm[:, sl] * _gelu(u[:, sl])
            dus.append(dgm[:, sl] * z * _gelu_grad(u[:, sl]))
            dws_ref[g] += _dot(dz, vn[:, sl], "nt")
            dbs = dbs + jnp.where(lane == g, jnp.sum(dz, axis=-1, keepdims=True), 0.0)
            dvns.append(_dot(w, dz, "tn"))
        dbs_ref[...] = dbs
        dvn = jnp.concatenate(dvns, axis=-1)
        dgain_ref[...] += jnp.sum(dvn * xhat, axis=0, keepdims=True)
        dbias_ref[...] += jnp.sum(dvn, axis=0, keepdims=True)
        dvg = _layer_norm_bwd(dvn * g_ref[...], xhat, rstd)
        du_ref[...] = jnp.concatenate(dus, axis=-1).reshape(du_ref.shape)
        dv_ref[...] = (dvg * _gelu_grad(v)).reshape(dv_ref.shape)

        @pl.when(c == nchunk - 1)
        def _():
            for g in range(N_GROUPS):
                dws_ref[g] = jnp.where(c_ref[...] > 0, dws_ref[g], 0.0)

    (du, dv, dws, dbs, dgain, dbias), st = _call(
        "gmlp_bwd", body, (1, nchunk),
        [_tile_spec(0, d_b, col0), _tile_spec(0, d_b, col0 + 1), _tile_spec(0, d_b, 0), _fix_spec((1, d_b)), _fix_spec((1, d_b)),
         _fix_spec((N_GROUPS, BLOCK, BLOCK)), _fix_spec((BLOCK, N_GROUPS)), _fix_spec((BLOCK, BLOCK))],
        [_tile_spec(0, d_b, 0), _tile_spec(0, d_b, 0), _fix_spec((N_GROUPS, BLOCK, BLOCK)), _fix_spec((BLOCK, LANES)),
         _fix_spec((1, d_b)), _fix_spec((1, d_b))],
        [jax.ShapeDtypeStruct(_view_shape(d_b, 0), F32)] * 2
        + [jax.ShapeDtypeStruct((N_GROUPS, BLOCK, BLOCK), F32), jax.ShapeDtypeStruct((BLOCK, LANES), F32)]
        + [jax.ShapeDtypeStruct((1, d_b), F32)] * 2,
        [_tile_view(rest, 0), _tile_view(rest, 0), _tile_view(dgmlp, 0), gain, bias, ws, bs, causal], stages=stages)
    return (du.reshape(SEQ, d_b), dv.reshape(SEQ, d_b), dws, dbs, dgain, dbias), st


def _assemble_dproj(dqkv, du, dv, dga, dgb, tr=128, stages=()):
    d_a, d_b, d_in = _d_a(), _d_b(), _d_in()
    tr = _pick(tr, SEQ)

    def body(*refs):
        att, (du_ref, dv_ref, dga_ref, dgb_ref, o_ref) = refs[:9], refs[9:]
        for i in range(3):
            o_ref[:, i * d_a:(i + 1) * d_a] = (att[3 * i][...] + att[3 * i + 1][...] + att[3 * i + 2][...]).astype(BF16)
        o_ref[:, 3 * d_a:3 * d_a + d_b] = du_ref[...].astype(BF16)
        o_ref[:, 3 * d_a + d_b:3 * d_a + 2 * d_b] = dv_ref[...].astype(BF16)
        o_ref[:, 3 * d_a + 2 * d_b:3 * d_a + 2 * d_b + D_MODEL] = dga_ref[...]
        o_ref[:, 3 * d_a + 2 * d_b + D_MODEL:] = dgb_ref[...]

    return _call("assemble_dproj", body, (SEQ // tr,), [_row_spec(tr, d_a)] * 9 + [_row_spec(tr, d_b)] * 2 + [_row_spec(tr, D_MODEL)] * 2,
                 [_row_spec(tr, d_in)], [jax.ShapeDtypeStruct((SEQ, d_in), BF16)], [*dqkv, du, dv, dga, dgb], sem=("parallel",), stages=stages)


def _dw(name, a, b, kind, core, mine, add=None, tn=1152, half=None, stages=()):
    s, m = a.shape
    n = b.shape[1]
    if half is not None:
        m //= 2
    rs, cs = (m, n // N_DEV) if kind == "col" else (m // N_DEV, n)
    tn = _pick(tn if cs % tn == 0 else 512, cs)
    nj = cs // tn

    def shard(q, c_ref):
        return 2 * q + (c_ref[0] if mine else 1 - c_ref[0])

    if kind == "col":
        a_spec = pl.BlockSpec((s, m), lambda q, j, c_ref: (0, half or 0))
        b_spec = pl.BlockSpec((s, tn), lambda q, j, c_ref: (0, shard(q, c_ref) * nj + j))
    else:
        a_spec = pl.BlockSpec((s, rs), lambda q, j, c_ref: (0, shard(q, c_ref)))
        b_spec = pl.BlockSpec((s, tn), lambda q, j, c_ref: (0, j))
    o_spec = pl.BlockSpec((None, rs, tn), lambda q, j, c_ref: (q, 0, j))

    def body(a_ref, b_ref, *rest):
        acc = _dot(a_ref[...], b_ref[...], "tn")
        if add is not None:
            acc = acc + rest[0][...].astype(F32)
        rest[-1][...] = acc.astype(BF16)

    (out,), st = _call(name, body, (N_CHIPS, nj), [a_spec, b_spec] + ([o_spec] if add is not None else []), [o_spec],
                       [jax.ShapeDtypeStruct((N_CHIPS, rs, cs), BF16)], [a, b] + ([add] if add is not None else []),
                       sem=("parallel", "parallel"), stages=stages, prefetch=core)
    return out, st


def _adamw(w, g, m, v):
    m = ADAM_B1 * m + (1.0 - ADAM_B1) * g
    v = ADAM_B2 * v + (1.0 - ADAM_B2) * (g * g)
    m_hat = m / (1.0 - ADAM_B1 ** ADAM_STEP)
    v_hat = v / (1.0 - ADAM_B2 ** ADAM_STEP)
    delta = -ADAM_LR * (m_hat / (jnp.sqrt(v_hat) + ADAM_EPS) + ADAM_WD * w)
    return delta, m, v


def _adam_shard(name, pair, chip_sums, chip, w, m, v, tr=256, rows=None, into=None, stages=()):
    rs, cs = w.shape
    lo, cnt = rows or (0, rs)
    assert pair.shape[1] == cnt
    tr = _pick(tr, cnt, lo)
    first = lo // tr

    def body(chip_ref, own_ref, *refs):
        slots, (w_ref, m_ref, v_ref), (g_ref, d_ref, nm_ref, nv_ref) = refs[:N_CHIPS], refs[N_CHIPS:N_CHIPS + 3], refs[-4:]
        g = None
        for q in range(N_CHIPS):
            term = jnp.where(chip_ref[0] == q, own_ref[...], slots[q][...]).astype(F32)
            g = term if g is None else g + term
        d, nm, nv = _adamw(w_ref[...], g, m_ref[...], v_ref[...])
        g_ref[...], d_ref[...], nm_ref[...], nv_ref[...] = g, d, nm, nv

    def slot(q):
        return pl.BlockSpec((None, tr, cs), lambda i, c_ref: (jnp.where(c_ref[0] == q, (q + 1) % N_CHIPS, q), i, 0))

    spec = pl.BlockSpec((tr, cs), lambda i, c_ref: (i + first, 0))
    n_in = 1 + N_CHIPS + 3
    return _call(name, body, (cnt // tr,),
                 [pl.BlockSpec((None, tr, cs), lambda i, c_ref: (c_ref[0], i, 0))] + [slot(q) for q in range(N_CHIPS)]
                 + [spec, spec, spec] + ([ANY] * 4 if into is not None else []),
                 [spec] * 4, [jax.ShapeDtypeStruct((rs, cs), F32)] * 4, [pair] + [chip_sums] * N_CHIPS + [w, m, v] + list(into or ()),
                 sem=("parallel",), stages=stages, prefetch=chip, shown=True,
                 alias={n_in + k: k for k in range(4)} if into is not None else None)


def _adam_small(part, parts, me, w, m, v):
    rows = w.shape[0]

    def body(me_ref, own_ref, *refs):
        slots, (w_ref, m_ref, v_ref, g_ref, d_ref, nm_ref, nv_ref) = refs[:N_DEV], refs[N_DEV:]
        g = None
        for j in range(N_DEV):
            term = jnp.where(me_ref[0] == j, own_ref[...], slots[j][...])
            g = term if g is None else g + term
        d, nm, nv = _adamw(w_ref[...], g, m_ref[...], v_ref[...])
        g_ref[...], d_ref[...], nm_ref[...], nv_ref[...] = g, d, nm, nv

    def slot(j):
        return pl.BlockSpec((None, rows, LANES), lambda i, me_ref: (jnp.where(me_ref[0] == j, (j + 1) % N_DEV, j), 0, 0))

    spec = _fix_spec((rows, LANES))
    return _call("adam_small", body, (1,), [spec] + [slot(j) for j in range(N_DEV)] + [spec, spec, spec], [spec] * 4,
                 [jax.ShapeDtypeStruct((rows, LANES), F32)] * 4, [part] + [parts] * N_DEV + [w, m, v], prefetch=me, shown=True)[0]


def _small_sizes():
    d_b = _d_b()
    return (("loss", 1), ("rel_bias", N_BUCKETS * N_HEADS), ("ln_v_gain", d_b), ("ln_v_bias", d_b),
            ("w_spatial", N_GROUPS * BLOCK * BLOCK), ("b_spatial", N_GROUPS * BLOCK), ("ln1_gain", D_MODEL), ("ln1_bias", D_MODEL),
            ("b_ff1", D_FF), ("b_ff2", D_MODEL), ("ln2_gain", D_MODEL), ("ln2_bias", D_MODEL))


def _pack(vals):
    pieces = []
    for name, size in _small_sizes():
        flat = vals[name].reshape(-1).astype(F32)
        padded = -(-size // (SUBLANES * LANES)) * SUBLANES * LANES
        pieces.append(jnp.pad(flat, (0, padded - size)).reshape(-1, LANES))
    return jnp.concatenate(pieces, axis=0)


def _unpack(buf):
    out, row = {}, 0
    for name, size in _small_sizes():
        rows = -(-size // (SUBLANES * LANES)) * SUBLANES
        out[name] = buf[row:row + rows].reshape(-1)[:size]
        row += rows
    return out


def kernel(x, w_in, rel_bias, ln_v_gain, ln_v_bias, w_spatial, b_spatial, w_proj_a, w_proj_b, w_out, ln1_gain, ln1_bias, w_ff1, b_ff1, w_ff2, b_ff2, ln2_gain, ln2_bias, loss_target, m_w_in, m_rel_bias, m_ln_v_gain, m_ln_v_bias, m_w_spatial, m_b_spatial, m_w_proj_a, m_w_proj_b, m_w_out, m_ln1_gain, m_ln1_bias, m_w_ff1, m_b_ff1, m_w_ff2, m_b_ff2, m_ln2_gain, m_ln2_bias, v_w_in, v_rel_bias, v_ln_v_gain, v_ln_v_bias, v_w_spatial, v_b_spatial, v_w_proj_a, v_w_proj_b, v_w_out, v_ln1_gain, v_ln1_bias, v_w_ff1, v_b_ff1, v_w_ff2, v_b_ff2, v_ln2_gain, v_ln2_bias):
    d_a, d_b, d_in = _d_a(), _d_b(), _d_in()
    weights = dict(w_in=w_in, rel_bias=rel_bias, ln_v_gain=ln_v_gain, ln_v_bias=ln_v_bias, w_spatial=w_spatial, b_spatial=b_spatial,
                   w_proj_a=w_proj_a, w_proj_b=w_proj_b, w_out=w_out, ln1_gain=ln1_gain, ln1_bias=ln1_bias, w_ff1=w_ff1, b_ff1=b_ff1,
                   w_ff2=w_ff2, b_ff2=b_ff2, ln2_gain=ln2_gain, ln2_bias=ln2_bias)
    mom1 = dict(w_in=m_w_in, rel_bias=m_rel_bias, ln_v_gain=m_ln_v_gain, ln_v_bias=m_ln_v_bias, w_spatial=m_w_spatial,
                b_spatial=m_b_spatial, w_proj_a=m_w_proj_a, w_proj_b=m_w_proj_b, w_out=m_w_out, ln1_gain=m_ln1_gain,
                ln1_bias=m_ln1_bias, w_ff1=m_w_ff1, b_ff1=m_b_ff1, w_ff2=m_w_ff2, b_ff2=m_b_ff2, ln2_gain=m_ln2_gain, ln2_bias=m_ln2_bias)
    mom2 = dict(w_in=v_w_in, rel_bias=v_rel_bias, ln_v_gain=v_ln_v_gain, ln_v_bias=v_ln_v_bias, w_spatial=v_w_spatial,
                b_spatial=v_b_spatial, w_proj_a=v_w_proj_a, w_proj_b=v_w_proj_b, w_out=v_w_out, ln1_gain=v_ln1_gain,
                ln1_bias=v_ln1_bias, w_ff1=v_w_ff1, b_ff1=v_b_ff1, w_ff2=v_w_ff2, b_ff2=v_b_ff2, ln2_gain=v_ln2_gain, ln2_bias=v_ln2_bias)

    mx, my, mc = _coords()
    me = (4 * mx + 2 * my + mc).astype(I32).reshape(1)
    chip = (2 * mx + my).astype(I32).reshape(1)
    full = {n: _cast_into_place(f"cast_{n}", weights[n][0], KINDS[n], me)[0] for n in KINDS if n != "w_ff1"}
    sent = {n: (0, 0, 0) for n in KINDS}

    def keep(table, n):
        def store(outs):
            table[n] = outs[0]
        return store

    def gathering(**new):
        stages = []
        for n in KINDS:
            out, relayed, passed = sent[n]
            units = new.get(n, 0)
            if units or relayed < out or passed < relayed:
                st = _gather_stage(full[n], KINDS[n], (out, units) if units else None,
                                   (relayed, out - relayed) if relayed < out else None, (passed, relayed - passed) if passed < relayed else None)
                st.store = keep(full, n)
                sent[n] = (out + units, out, relayed)
                stages.append(st)
        return stages

    def settle(stages, outs):
        for st, o in zip(stages, outs):
            st.store(o)

    def alone(name, stages):
        settle(stages, _comm_only(name, stages))

    def here(n):
        assert sent[n] == (16, 16, 16), (n, sent[n])
        return full[n]

    hosted = gathering(w_in=16)
    full["w_ff1"], st = _cast_into_place("cast_w_ff1", weights["w_ff1"][0], KINDS["w_ff1"], me, stages=hosted)
    settle(hosted, st)
    xs = _to_perm(x[0])
    target = _to_perm(loss_target[0])
    hosted = gathering()
    xb, st = _cast_bf16("cast_x", xs, stages=hosted)
    settle(hosted, st)
    g8 = BLOCK // N_SUB
    ws_t = w_spatial[0].reshape(N_GROUPS, g8, N_SUB, g8, N_SUB).transpose(0, 2, 1, 4, 3).reshape(N_GROUPS, BLOCK, BLOCK)
    bs_t = b_spatial[0].reshape(N_GROUPS, g8, N_SUB).transpose(2, 1, 0).reshape(BLOCK, N_GROUPS)
    idx = _local_index(0)
    causal = jnp.asarray((idx[:, None] >= idx[None, :]).astype(np.float32))
    buckets = jnp.asarray(_bucket_tables())
    hosted = gathering(w_proj_a=16, w_proj_b=16)
    (bias,), st = _bias_expand(rel_bias, buckets, stages=hosted)
    settle(hosted, st)

    hosted = gathering(w_out=8, w_ff1=1)
    (qkv,), st = _matmul("proj_qkv", xb, here("w_in"), "nn", [F32], n=3 * d_a, stages=hosted)
    settle(hosted, st)
    hosted = gathering(w_out=8, w_ff1=6)
    (rest,), st = _matmul("proj_rest", xb, here("w_in"), "nn", [F32], b_off=3 * d_a, n=d_in - 3 * d_a, stages=hosted)
    settle(hosted, st)
    fwd = []
    for p in range(3):
        hosted = gathering(**({"w_ff1": 2}, {"w_ff1": 5}, {"w_ff1": 2, "w_ff2": 1})[p])
        res, st = _attn_fwd(qkv, bias, p, stages=hosted)
        settle(hosted, st)
        fwd.append(res)
    hosted = gathering(w_ff2=1)
    (attn, attn_b, lse), st = _attn_combine([o for o, _ in fwd], [l for _, l in fwd], stages=hosted)
    settle(hosted, st)
    hosted = gathering(w_ff2=2)
    gmlp, st = _gmlp_fwd(rest, 0, ln_v_gain, ln_v_bias, ws_t, bs_t, causal, stages=hosted)
    settle(hosted, st)
    hosted = gathering(w_ff2=3)
    (ya,), st = _matmul("proj_a", attn_b, here("w_proj_a"), "nn", [BF16], stages=hosted)
    settle(hosted, st)
    gate_a, gate_b = 2 * d_b, 2 * d_b + D_MODEL

    def merge(acc, ya_, ga, gb):
        return acc, _sigmoid(ga) * ya_ + _sigmoid(gb) * acc

    hosted = gathering(w_ff2=5)
    (yb, merged), st = _matmul("proj_b_merge", gmlp, here("w_proj_b"), "nn", [BF16, BF16], merge,
                               [(ya, "mn", 0), (rest, "mn", gate_a), (rest, "mn", gate_b)], tn=256, stages=hosted)
    settle(hosted, st)
    hosted = gathering(w_ff2=3)
    (pre1,), st = _matmul("out_proj", merged, here("w_out"), "nn", [F32], lambda acc, x_: (ALPHA * x_ + acc,), [(xs, "mn", 0)], stages=hosted)
    settle(hosted, st)
    hosted = gathering(w_ff2=1)
    (xhat1, rstd1, h1b), st = _ln1_fwd(pre1, ln1_gain, ln1_bias, stages=hosted)
    settle(hosted, st)

    def relu2(acc, b_):
        r = jnp.maximum(acc + b_, 0.0)
        return r, r * r

    hosted = gathering()
    (relu, fb), st = _matmul("ff1", h1b, here("w_ff1"), "nn", [BF16, BF16], relu2, [(b_ff1, "row", 0)], stages=hosted)
    settle(hosted, st)
    alone("gather_w_ff2_sibling", gathering())
    (ff,), _ = _matmul("ff2", fb, here("w_ff2"), "nn", [F32], lambda acc, b_: (acc + b_,), [(b_ff2, "row", 0)], tn=512, tk=2048)

    core = lax.axis_index("c").astype(I32).reshape(1)
    factors, theirs, sib, pair, chips, reduced = {}, {}, {}, {}, {}, {}

    def grad_for_sibling(n, a, b, stages=(), half=None, of=None):
        factors[n] = (a, b, KINDS[of or n], half)
        theirs[n], outs = _dw(f"dw_{n}_sibling", a, b, KINDS[of or n], core, False, half=half, stages=stages)
        settle(stages, outs)

    def to_sibling(n):
        st = _to_sibling_stage(theirs[n])
        st.store = keep(sib, n)
        return st

    def grad_own(n, stages=()):
        a, b, kind, half = factors[n]
        pair[n], outs = _dw(f"dw_{n}_own", a, b, kind, core, True, add=sib[n], half=half, stages=stages)
        settle(stages, outs)
        chips[n] = lax.empty(pair[n].shape, BF16)
        reduced[n] = 0

    def reducing(**new):
        stages = []
        for n, units in new.items():
            st = _to_chips_stage(pair[n], chips[n], (reduced[n], units))
            st.store = keep(chips, n)
            reduced[n] += units
            stages.append(st)
        return stages

    def summed(n):
        assert reduced[n] == 16, (n, reduced[n])
        return chips[n]

    dpre2, dpre2b, g_ln2_gain, g_ln2_bias, g_b_ff2, loss_part = _ln2_loss_bwd(ff, xhat1, ln1_gain, ln1_bias, ln2_gain, ln2_bias, target)
    grad_for_sibling("w_ff2", fb, dpre2b)

    def relu2_bwd(acc, r):
        da = acc * (2.0 * r)
        return da, da

    hosted = [to_sibling("w_ff2")]
    (dab, g_b_ff1), st = _matmul("d_ff1", dpre2b, here("w_ff2"), "nt", [BF16], relu2_bwd, [(relu, "mn", 0)], colsums=(1,), stages=hosted)
    settle(hosted, st)
    grad_own("w_ff2")
    grad_for_sibling("w_ff1", h1b, dab, reducing(w_ff2=3))
    hosted = reducing(w_ff2=8) + [to_sibling("w_ff1")]
    (dh1,), st = _matmul("d_h1", dab, here("w_ff1"), "nt", [F32], lambda acc, d_: (acc + ALPHA * d_,), [(dpre2, "mn", 0)], stages=hosted)
    settle(hosted, st)
    grad_own("w_ff1", reducing(w_ff2=4))
    hosted = reducing(w_ff2=1)
    (dpre1, dpre1b, g_ln1_gain, g_ln1_bias), st = _ln1_bwd(dh1, xhat1, rstd1, ln1_gain, stages=hosted)
    settle(hosted, st)
    grad_for_sibling("w_out", merged, dpre1b, reducing(w_ff1=1))

    def merge_bwd(acc, ga, gb, ya_, yb_):
        sa, sb = _sigmoid(ga), _sigmoid(gb)
        return acc * sa, acc * sb, acc * ya_ * (sa * (1.0 - sa)), acc * yb_ * (sb * (1.0 - sb))

    hosted = reducing(w_ff1=6) + [to_sibling("w_out")]
    (dya, dyb, dga, dgb), st = _matmul("d_merge", dpre1b, here("w_out"), "nt", [BF16] * 4, merge_bwd,
                                       [(rest, "mn", gate_a), (rest, "mn", gate_b), (ya, "mn", 0), (yb, "mn", 0)], tn=256, stages=hosted)
    settle(hosted, st)
    grad_own("w_out", reducing(w_ff1=1))
    grad_for_sibling("w_proj_a", attn_b, dya)
    grad_for_sibling("w_proj_b", gmlp, dyb)
    hosted = reducing(w_ff1=1) + [to_sibling("w_proj_a"), to_sibling("w_proj_b")]
    (dattn,), st = _matmul("d_attn", dya, here("w_proj_a"), "nt", [F32], stages=hosted)
    settle(hosted, st)
    grad_own("w_proj_a")
    grad_own("w_proj_b")
    hosted = reducing(w_ff1=1)
    (dgmlp,), st = _matmul("d_gmlp", dyb, here("w_proj_b"), "nt", [F32], stages=hosted)
    settle(hosted, st)
    hosted = reducing(w_ff1=2)
    (du, dvb, dws_t, dbs_t, g_lnv_gain, g_lnv_bias), st = _gmlp_bwd(rest, 0, dgmlp, ln_v_gain, ln_v_bias, ws_t, bs_t, causal, stages=hosted)
    settle(hosted, st)
    delta = _attn_delta(dattn, attn)
    bwd = []
    for p in range(3):
        hosted = reducing(**({"w_ff1": 3}, {"w_ff1": 1, "w_out": 8}, {"w_out": 8, "w_proj_a": 8})[p])
        res, st = _attn_bwd(qkv, dattn, lse, delta, bias, p, stages=hosted)
        settle(hosted, st)
        bwd.append(res)
    g_rel_bias = _rel_bias_grad([b[3] for b in bwd], buckets)
    hosted = reducing(w_proj_a=8, w_proj_b=16)
    (dproj,), st = _assemble_dproj([b[i] for i in range(3) for b in bwd], du, dvb, dga, dgb, stages=hosted)
    settle(hosted, st)

    g_w_spatial = dws_t.reshape(N_GROUPS, N_SUB, g8, N_SUB, g8).transpose(0, 2, 1, 4, 3)
    g_b_spatial = dbs_t[:, :N_GROUPS].reshape(N_SUB, g8, N_GROUPS).transpose(2, 1, 0)
    part = _pack(dict(loss=loss_part, rel_bias=g_rel_bias, ln_v_gain=g_lnv_gain, ln_v_bias=g_lnv_bias, w_spatial=g_w_spatial,
                      b_spatial=g_b_spatial, ln1_gain=g_ln1_gain, ln1_bias=g_ln1_bias, b_ff1=g_b_ff1, b_ff2=g_b_ff2,
                      ln2_gain=g_ln2_gain, ln2_bias=g_ln2_bias))
    sib["small"] = lax.empty((N_DEV, *part.shape), F32)
    cut = part.shape[0] // 2 // SUBLANES * SUBLANES

    def small(rows):
        st = _small_stage(part, sib["small"], rows)
        st.store = keep(sib, "small")
        return st

    grad_for_sibling("w_in_top", xb, dproj, [small((0, cut))], half=0, of="w_in")
    grad_for_sibling("w_in_bot", xb, dproj, [small((cut, part.shape[0] - cut)), to_sibling("w_in_top")], half=1, of="w_in")
    parts = sib["small"]
    grad_own("w_in_top", [to_sibling("w_in_bot")])
    grad_own("w_in_bot", reducing(w_in_top=4))

    def add_residual(acc, d_):
        return (acc + ALPHA * d_,)

    hosted = reducing(w_in_top=12, w_in_bot=7)
    (dx,), st = _matmul("d_x", dproj, here("w_in"), "nt", [F32], add_residual, [(dpre1, "mn", 0)], tn=512, tk=3072, stages=hosted)
    settle(hosted, st)
    grad_x = _from_perm(dx)[None]

    out_g, out_d, out_m, out_v = {}, {}, {}, {}
    for n, units in (("w_out", 3), ("w_proj_a", 3), ("w_proj_b", 3), ("w_ff2", 0), ("w_ff1", 0)):
        hosted = reducing(w_in_bot=units) if units else []
        (g, d, nm, nv), st = _adam_shard(f"adam_{n}", pair[n], summed(n), chip, weights[n][0], mom1[n][0], mom2[n][0], stages=hosted)
        settle(hosted, st)
        out_g[n], out_d[n], out_m[n], out_v[n] = g[None], d[None], nm[None], nv[None]
    rows = weights["w_in"].shape[1] // 2
    done = None
    for i, n in enumerate(("w_in_top", "w_in_bot")):
        done, _ = _adam_shard(f"adam_{n}", pair[n], summed(n), chip, weights["w_in"][0], mom1["w_in"][0], mom2["w_in"][0],
                              rows=(i * rows, rows), into=done)
    out_g["w_in"], out_d["w_in"], out_m["w_in"], out_v["w_in"] = (t[None] for t in done)

    zero = jnp.zeros((1,), F32)
    sg, sd, sm, sv = (_unpack(b) for b in _adam_small(
        part, parts, me, _pack({**weights, "loss": zero}), _pack({**mom1, "loss": zero}), _pack({**mom2, "loss": zero})))
    for n in WEIGHT_ORDER:
        if n not in KINDS:
            shape = weights[n].shape
            out_g[n], out_d[n], out_m[n], out_v[n] = (t[n].reshape(shape) for t in (sg, sd, sm, sv))
    loss = sg["loss"].reshape(())
    return (loss, grad_x, *[out_g[n] for n in WEIGHT_ORDER], *[out_d[n] for n in WEIGHT_ORDER],
            *[out_m[n] for n in WEIGHT_ORDER], *[out_v[n] for n in WEIGHT_ORDER])
```

```python
import math

import jax
import jax.numpy as jnp
import numpy as np
from jax import lax
from jax.experimental import pallas as pl
from jax.experimental.pallas import tpu as pltpu

F32 = jnp.float32
BF16 = jnp.bfloat16
I32 = jnp.int32

SEQ = 2048
D_MODEL = 2048
HEAD_DIM = 128
N_HEADS = 8
N_GROUPS = 8
D_FF = 4 * D_MODEL
BLOCK = 128
DILATIONS = (1, 4, 16)
N_BUCKETS = 32
MAX_DISTANCE = 2048
ALPHA = 2.0 ** 0.25
LN_EPS = 1e-5
NEG_INF = -1e30
N_DEV = 8
N_CHIPS = 4
N_SUB = 16
ADAM_LR, ADAM_B1, ADAM_B2, ADAM_EPS, ADAM_WD, ADAM_STEP = 0.001, 0.9, 0.999, 1e-08, 0.01, 10
LANES = 128
SUBLANES = 8
VMEM_LIMIT = 56 * 1024 * 1024
MESH = pl.DeviceIdType.MESH
ANY = pl.BlockSpec(memory_space=pl.ANY)
WEIGHT_ORDER = ("w_in", "rel_bias", "ln_v_gain", "ln_v_bias", "w_spatial", "b_spatial", "w_proj_a", "w_proj_b", "w_out",
                "ln1_gain", "ln1_bias", "w_ff1", "b_ff1", "w_ff2", "b_ff2", "ln2_gain", "ln2_bias")
KINDS = {"w_in": "col", "w_proj_a": "col", "w_proj_b": "col", "w_out": "row", "w_ff1": "col", "w_ff2": "row"}


def _d_a():
    return N_HEADS * HEAD_DIM


def _d_b():
    return N_GROUPS * BLOCK


def _d_in():
    return 3 * _d_a() + 2 * _d_b() + 2 * D_MODEL


def _pick(t, n, *others):
    if n <= t and all(o % n == 0 for o in others):
        return n
    for c in range(min(t, n) // LANES * LANES, 0, -LANES):
        if n % c == 0 and all(o % c == 0 for o in others):
            return c
    raise ValueError((t, n, others))


class _Stage:
    def __init__(self, ins, outs, alias, sems, start, finish):
        self.ins, self.outs, self.alias, self.sems, self.start, self.finish = ins, outs, alias, sems, start, finish


def _call(name, body, grid, in_specs, out_specs, out_shape, operands, scratch=(), sem=None, stages=(), sequential=False, prefetch=None,
          shown=False, alias=None):
    n_in, n_out, n_sc = len(in_specs), len(out_specs), len(scratch)
    st_in = [len(s.ins) for s in stages]
    st_out = [len(s.outs) for s in stages]
    st_sem = [len(s.sems) for s in stages]
    n_pre = 0 if prefetch is None else 1
    aliases, ioff, ooff = {i + n_pre: o for i, o in (alias or {}).items()}, n_in + n_pre, n_out
    for s, ni, no in zip(stages, st_in, st_out):
        for i, o in s.alias.items():
            aliases[ioff + i] = ooff + o
        ioff, ooff = ioff + ni, ooff + no

    def split(refs, counts):
        out, at = [], 0
        for c in counts:
            out.append(refs[at:at + c])
            at += c
        return out

    def wrapped(*refs):
        ins, sins, outs, souts, sc, ssems = split(refs[n_pre:], [n_in, sum(st_in), n_out, sum(st_out), n_sc, sum(st_sem)])
        parts = list(zip(stages, split(sins, st_in), split(souts, st_out), split(ssems, st_sem)))
        if sequential:
            for s, a, b, c in parts:
                s.start(a, b, c)
            for s, a, b, c in parts:
                s.finish(a, b, c)
            return
        if parts:
            first = _all_of([pl.program_id(i) == 0 for i in range(len(grid))])
            last = _all_of([pl.program_id(i) == g - 1 for i, g in enumerate(grid)])

            @pl.when(first)
            def _():
                for s, a, b, c in parts:
                    s.start(a, b, c)

        body(*(refs[:n_pre] if shown else ()), *ins, *outs, *sc)
        if parts:
            @pl.when(last)
            def _():
                for s, a, b, c in parts:
                    s.finish(a, b, c)

    if stages or sem is None:
        sem = ("arbitrary",) * len(grid)
    specs = dict(grid=grid, in_specs=list(in_specs) + [ANY] * sum(st_in), out_specs=list(out_specs) + [ANY] * sum(st_out),
                 scratch_shapes=list(scratch) + [x for s in stages for x in s.sems])
    if prefetch is not None:
        specs = dict(grid_spec=pltpu.PrefetchScalarGridSpec(num_scalar_prefetch=1, **specs))
    res = pl.pallas_call(
        wrapped, name=name, out_shape=list(out_shape) + [o for s in stages for o in s.outs], input_output_aliases=aliases,
        compiler_params=pltpu.CompilerParams(dimension_semantics=sem, vmem_limit_bytes=VMEM_LIMIT), **specs,
    )(*([prefetch] if n_pre else []), *operands, *[a for s in stages for a in s.ins])
    res = list(res)
    return res[:n_out], split(res[n_out:], st_out)


def _all_of(conds):
    out = conds[0]
    for c in conds[1:]:
        out = out & c
    return out


def _coords():
    return lax.axis_index("x"), lax.axis_index("y"), lax.axis_index("c")


def _other_chips(x, y):
    return ((1 - x, y), (x, 1 - y), (1 - x, 1 - y))


def _lin(dev):
    return 4 * dev[0] + 2 * dev[1] + dev[2]


def _piece(total, lo, n, units=16):
    assert total % units == 0
    return lo * (total // units), n * (total // units)


FLOWS = 4


def _split(lo, cnt):
    k = next(k for k in (FLOWS, 2, 1) if cnt % (2 * SUBLANES * k) == 0)
    return [(lo + i * (cnt // k), cnt // k) for i in range(k)]


def _remote(src, dst, send, recv, to):
    return pltpu.make_async_remote_copy(src_ref=src, dst_ref=dst, send_sem=send, recv_sem=recv, device_id=to, device_id_type=MESH)


def _placer(kind, n, lo, cnt):
    def place(ref, dev):
        if kind == "col":
            return ref.at[pl.ds(lo, cnt), pl.ds(pl.multiple_of(_lin(dev) * n, LANES), n)]
        return ref.at[pl.ds(pl.multiple_of(_lin(dev) * n + lo, 2 * SUBLANES), cnt), :]
    return place


def _spread_stage(full, kind, piece=(0, 16), home=False):
    n = (full.shape[1] if kind == "col" else full.shape[0]) // N_DEV
    lo, cnt = _piece(full.shape[0] if kind == "col" else n, *piece)
    parts = _split(lo, cnt)
    npeers = 1 if home else 2

    def copies(ins, outs, sems):
        send, recv = sems
        x, y, c = _coords()
        me = (x, y, c)
        peers = [(x, y, 1 - c)] if home else [(1 - x, y, c), (x, 1 - y, c)]
        out, arrive = [], []
        for k, t in enumerate(peers):
            for i, (plo, pcnt) in enumerate(parts):
                place = _placer(kind, n, plo, pcnt)
                out.append(_remote(place(outs[0], me), place(outs[0], me), send.at[i, k], recv.at[i, k], t))
                arrive.append(_remote(place(outs[0], t), place(outs[0], t), send.at[i, k], recv.at[i, k], t))
        return out, arrive

    def start(ins, outs, sems):
        for cp in copies(ins, outs, sems)[0]:
            cp.start()

    def finish(ins, outs, sems):
        out, arrive = copies(ins, outs, sems)
        for cp in arrive:
            cp.wait_recv()
        for cp in out:
            cp.wait_send()

    return _Stage([full], [jax.ShapeDtypeStruct(full.shape, full.dtype)], {0: 0},
                  [pltpu.SemaphoreType.DMA((len(parts), npeers)), pltpu.SemaphoreType.DMA((len(parts), npeers))], start, finish)


def _relay_stage(full, kind, piece=(0, 16)):
    n = (full.shape[1] if kind == "col" else full.shape[0]) // N_DEV
    lo, cnt = _piece(full.shape[0] if kind == "col" else n, *piece)
    half = cnt // 2
    assert half % (2 * SUBLANES) == 0, (cnt, kind)
    tops, bottoms = _split(lo, half), _split(lo + half, half)

    def copies(ins, outs, sems):
        send, recv = sems
        x, y, c = _coords()
        xn, yn, dg = (1 - x, y, c), (x, 1 - y, c), (1 - x, 1 - y, c)
        out, arrive, k = [], [], 0
        for came_from, to, parts in ((yn, xn, tops), (xn, yn, bottoms)):
            for plo, pcnt in parts:
                place = _placer(kind, n, plo, pcnt)
                out.append(_remote(place(outs[0], came_from), place(outs[0], came_from), send.at[k], recv.at[k], to))
                arrive.append(_remote(place(outs[0], dg), place(outs[0], dg), send.at[k], recv.at[k], to))
                k += 1
        return out, arrive

    def start(ins, outs, sems):
        for cp in copies(ins, outs, sems)[0]:
            cp.start()

    def finish(ins, outs, sems):
        out, arrive = copies(ins, outs, sems)
        for cp in arrive:
            cp.wait_recv()
        for cp in out:
            cp.wait_send()

    return _Stage([full], [jax.ShapeDtypeStruct(full.shape, full.dtype)], {0: 0},
                  [pltpu.SemaphoreType.DMA((len(tops) + len(bottoms),)), pltpu.SemaphoreType.DMA((len(tops) + len(bottoms),))], start, finish)


def _forward_stage(full, kind, piece=(0, 16)):
    n = (full.shape[1] if kind == "col" else full.shape[0]) // N_DEV
    lo, cnt = _piece(full.shape[0] if kind == "col" else n, *piece)
    place = _placer(kind, n, lo, cnt)

    def copies(ins, outs, sems):
        send, recv = sems
        x, y, c = _coords()
        chips = _other_chips(x, y)
        out = [_remote(place(outs[0], (*chip, c)), place(outs[0], (*chip, c)), send.at[k], recv.at[k], (x, y, 1 - c)) for k, chip in enumerate(chips)]
        arrive = [_remote(place(outs[0], (*chip, 1 - c)), place(outs[0], (*chip, 1 - c)), send.at[k], recv.at[k], (x, y, 1 - c))
                  for k, chip in enumerate(chips)]
        return out, arrive

    def start(ins, outs, sems):
        for cp in copies(ins, outs, sems)[0]:
            cp.start()

    def finish(ins, outs, sems):
        out, arrive = copies(ins, outs, sems)
        for cp in arrive:
            cp.wait_recv()
        for cp in out:
            cp.wait_send()

    return _Stage([full], [jax.ShapeDtypeStruct(full.shape, full.dtype)], {0: 0},
                  [pltpu.SemaphoreType.DMA((3,)), pltpu.SemaphoreType.DMA((3,))], start, finish)


def _to_sibling_stage(theirs):
    def copies(ins, outs, sems):
        send, recv = sems
        x, y, c = _coords()
        return [_remote(ins[0].at[q], outs[0].at[q], send.at[q], recv.at[q], (x, y, 1 - c)) for q in range(N_CHIPS)]

    def start(ins, outs, sems):
        for cp in copies(ins, outs, sems):
            cp.start()

    def finish(ins, outs, sems):
        for cp in copies(ins, outs, sems):
            cp.wait()

    return _Stage([theirs], [jax.ShapeDtypeStruct(theirs.shape, BF16)], {},
                  [pltpu.SemaphoreType.DMA((N_CHIPS,)), pltpu.SemaphoreType.DMA((N_CHIPS,))], start, finish)


def _to_chips_stage(pair, dst, piece=(0, 16)):
    lo, cnt = _piece(pair.shape[1], *piece)
    parts = _split(lo, cnt)
    nsem = 3 * len(parts)

    def copies(ins, outs, sems):
        send, recv = sems
        x, y, c = _coords()
        mine = 2 * x + y
        out, arrive, k = [], [], 0
        for px, py in _other_chips(x, y):
            for plo, pcnt in parts:
                rows = pl.ds(plo, pcnt)
                out.append(_remote(ins[0].at[2 * px + py, rows, :], outs[0].at[mine, rows, :], send.at[k], recv.at[k], (px, py, c)))
                arrive.append(_remote(ins[0].at[2 * px + py, rows, :], outs[0].at[2 * px + py, rows, :], send.at[k], recv.at[k], (px, py, c)))
                k += 1
        return out, arrive

    def start(ins, outs, sems):
        for cp in copies(ins, outs, sems)[0]:
            cp.start()

    def finish(ins, outs, sems):
        out, arrive = copies(ins, outs, sems)
        for cp in arrive:
            cp.wait_recv()
        for cp in out:
            cp.wait_send()

    return _Stage([pair, dst], [jax.ShapeDtypeStruct(dst.shape, dst.dtype)], {1: 0},
                  [pltpu.SemaphoreType.DMA((nsem,)), pltpu.SemaphoreType.DMA((nsem,))], start, finish)


def _fuse(parts, ins, outs, alias):
    parts = [p for p in parts if p is not None]
    sems = [x for st, _, _ in parts for x in st.sems]

    def run(which):
        def go(i, o, s):
            refs, at = list(i) + list(o), 0
            for st, pi, po in parts:
                getattr(st, which)([refs[k] for k in pi], [refs[k] for k in po], s[at:at + len(st.sems)])
                at += len(st.sems)
        return go

    return _Stage(ins, [jax.ShapeDtypeStruct(o.shape, o.dtype) for o in outs], alias, sems, run("start"), run("finish"))


def _gather_stage(full, kind, new=None, relay=None, forward=None):
    return _fuse([(_spread_stage(full, kind, new), [0], [1]) if new else None,
                  (_relay_stage(full, kind, relay), [0], [1]) if relay else None,
                  (_spread_stage(full, kind, relay, home=True), [0], [1]) if relay else None,
                  (_forward_stage(full, kind, forward), [0], [1]) if forward else None], [full], [full], {0: 0})


def _small_stage(part, dst, rows):
    piece = pl.ds(*rows)

    def copies(ins, outs, sems):
        send, recv = sems
        x, y, c = _coords()
        me = (x, y, c)
        peers = [(1 - x if k & 4 else x, 1 - y if k & 2 else y, 1 - c if k & 1 else c) for k in range(1, N_DEV)]
        out = [_remote(ins[0].at[piece, :], outs[0].at[_lin(me), piece, :], send.at[k], recv.at[k], t) for k, t in enumerate(peers)]
        arrive = [_remote(ins[0].at[piece, :], outs[0].at[_lin(t), piece, :], send.at[k], recv.at[k], t) for k, t in enumerate(peers)]
        return out, arrive

    def start(ins, outs, sems):
        for cp in copies(ins, outs, sems)[0]:
            cp.start()

    def finish(ins, outs, sems):
        out, arrive = copies(ins, outs, sems)
        for cp in arrive:
            cp.wait_recv()
        for cp in out:
            cp.wait_send()

    return _Stage([part, dst], [jax.ShapeDtypeStruct(dst.shape, F32)], {1: 0},
                  [pltpu.SemaphoreType.DMA((N_DEV - 1,)), pltpu.SemaphoreType.DMA((N_DEV - 1,))], start, finish)


def _comm_only(name, stages):
    return _call(name, lambda: None, (1,), [], [], [], [], stages=stages, sequential=True)[1]


_GELU_C = math.sqrt(2.0 / math.pi)


def _gelu(x):
    return 0.5 * x * (1.0 + jnp.tanh(_GELU_C * (x + 0.044715 * x * x * x)))


def _gelu_grad(x):
    t = jnp.tanh(_GELU_C * (x + 0.044715 * x * x * x))
    return 0.5 * (1.0 + t) + 0.5 * x * (1.0 - t * t) * (_GELU_C * (1.0 + 3.0 * 0.044715 * x * x))


def _sigmoid(x):
    return 1.0 / (1.0 + jnp.exp(-x))


def _dot(a, b, mode):
    dims = {"nn": (((1,), (0,)), ((), ())), "nt": (((1,), (1,)), ((), ())), "tn": (((0,), (0,)), ((), ()))}[mode]
    return lax.dot_general(a.astype(BF16), b.astype(BF16), dims, preferred_element_type=F32)


def _matmul(name, a, b, mode, outs, epi=None, extras=(), colsums=(), tm=2048, tn=512, tk=2048, b_off=0, n=None, m_off=0, m=None, k_outer=False, stages=()):
    if mode == "tn":
        kk, mfull = a.shape
    else:
        mfull, kk = a.shape
    m = mfull if m is None else m
    n = (b.shape[0] if mode == "nt" else b.shape[1]) if n is None else n
    tm, tk = _pick(tm, m, m_off), _pick(tk, kk)
    tn = _pick(tn, n, b_off, *[off for _, _, off in extras])
    boff, moff = b_off // tn, m_off // tm
    nm, nn_, nk = m // tm, n // tn, kk // tk
    col_major = bool(colsums)
    k_outer = k_outer and nk > 1
    assert not (k_outer and col_major)
    grid = (nn_, nm, nk) if col_major else (nm, nk, nn_) if k_outer else (nm, nn_, nk)

    def imap(f):
        if col_major:
            return lambda g0, g1, k: f(g1, g0, k)
        if k_outer:
            return lambda i, k, j: f(i, j, k)
        return f

    def when_done(j, k):
        return jnp.where(k == nk - 1, j, 0) if k_outer else j

    a_spec = (pl.BlockSpec((tk, tm), imap(lambda i, j, k: (k, i + moff))) if mode == "tn"
              else pl.BlockSpec((tm, tk), imap(lambda i, j, k: (i + moff, k))))
    b_spec = (pl.BlockSpec((tn, tk), imap(lambda i, j, k: (j + boff, k))) if mode == "nt"
              else pl.BlockSpec((tk, tn), imap(lambda i, j, k: (k, j + boff))))
    in_specs, operands = [a_spec, b_spec], [a, b]
    for arr, kind, off in extras:
        o = off // tn
        if kind == "mn":
            in_specs.append(pl.BlockSpec((tm, tn), imap(lambda i, j, k, o=o: (i + moff, when_done(j, k) + o))))
        else:
            in_specs.append(pl.BlockSpec((1, tn), imap(lambda i, j, k, o=o: (0, j + o))))
        operands.append(arr)
    out_shape = [jax.ShapeDtypeStruct((m, n), dt) for dt in outs] + [jax.ShapeDtypeStruct((1, n), F32) for _ in colsums]
    out_specs = ([pl.BlockSpec((tm, tn), imap(lambda i, j, k: (i, when_done(j, k)))) for _ in outs]
                 + [pl.BlockSpec((1, tn), imap(lambda i, j, k: (0, j))) for _ in colsums])
    n_ex, n_out, n_cs = len(extras), len(outs), len(colsums)

    def body(*refs):
        a_ref, b_ref = refs[:2]
        ex_refs = refs[2:2 + n_ex]
        out_refs = refs[2 + n_ex:2 + n_ex + n_out]
        cs_refs = refs[2 + n_ex + n_out:2 + n_ex + n_out + n_cs]
        part = _dot(a_ref[...], b_ref[...], mode)

        def finish(acc):
            res = epi(acc, *[r[...] for r in ex_refs]) if epi is not None else (acc,)
            for r, v in zip(out_refs, res[:n_out]):
                r[...] = v.astype(r.dtype)
            if n_cs:
                @pl.when(pl.program_id(1) == 0)
                def _():
                    for r in cs_refs:
                        r[...] = jnp.zeros_like(r)

                for r, idx in zip(cs_refs, colsums):
                    r[...] += jnp.sum(res[idx], axis=0, keepdims=True)

        if nk == 1:
            finish(part)
        elif k_outer:
            acc_ref = refs[-1]
            k, j = pl.program_id(1), pl.program_id(2)

            @pl.when(k == 0)
            def _():
                acc_ref[j] = part

            @pl.when((k > 0) & (k < nk - 1))
            def _():
                acc_ref[j] += part

            @pl.when(k == nk - 1)
            def _():
                finish(acc_ref[j] + part)
        else:
            acc_ref = refs[-1]
            k = pl.program_id(2)

            @pl.when(k == 0)
            def _():
                acc_ref[...] = part

            @pl.when(k > 0)
            def _():
                acc_ref[...] += part

            @pl.when(k == nk - 1)
            def _():
                finish(acc_ref[...])

    sem = (("arbitrary", "arbitrary", "arbitrary") if col_major else ("parallel", "arbitrary", "arbitrary") if k_outer
           else ("parallel", "parallel", "arbitrary"))
    scratch = [] if nk == 1 else [pltpu.VMEM((nn_, tm, tn), F32)] if k_outer else [pltpu.VMEM((tm, tn), F32)]
    return _call(name, body, grid, in_specs, out_specs, out_shape, operands, scratch=scratch, sem=sem, stages=stages)


def _project_shards(name, a, b, which, into=None, stages=()):
    m, kk = a.shape
    n = b.shape[1]
    tn = n // N_DEV
    o_spec = pl.BlockSpec((m, tn), lambda s, w_ref: (0, w_ref[s]))

    def body(a_ref, b_ref, *rest):
        rest[-1][...] = _dot(a_ref[...], b_ref[...], "nn")

    (out,), st = _call(name, body, (which.shape[0],),
                       [pl.BlockSpec((m, kk), lambda s, w_ref: (0, 0)), pl.BlockSpec((kk, tn), lambda s, w_ref: (0, w_ref[s]))]
                       + ([ANY] if into is not None else []), [o_spec], [jax.ShapeDtypeStruct((m, n), F32)],
                       [a, b] + ([into] if into is not None else []), sem=("arbitrary",), stages=stages, prefetch=which,
                       alias={2: 0} if into is not None else None)
    return out, st


def _row_spec(tr, c):
    return pl.BlockSpec((tr, c), lambda i: (i, 0))


def _fix_spec(shape):
    return pl.BlockSpec(shape, lambda *_: tuple(0 for _ in shape))


def _cast_bf16(name, x, tr=512, stages=()):
    r, c = x.shape
    tr = _pick(tr, r)

    def body(x_ref, o_ref):
        o_ref[...] = x_ref[...].astype(BF16)

    (out,), st = _call(name, body, (r // tr,), [_row_spec(tr, c)], [_row_spec(tr, c)], [jax.ShapeDtypeStruct((r, c), BF16)], [x],
                       sem=("parallel",), stages=stages)
    return out, st


def _cast_into_place(name, w, kind, me, tr=512, stages=()):
    r, c = w.shape
    tr = _pick(tr, r)
    nb = r // tr
    if kind == "col":
        o_spec = pl.BlockSpec((tr, c), lambda i, me_ref: (i, me_ref[0]))
        shape = (r, c * N_DEV)
    else:
        o_spec = pl.BlockSpec((tr, c), lambda i, me_ref: (me_ref[0] * nb + i, 0))
        shape = (r * N_DEV, c)

    def body(x_ref, o_ref):
        o_ref[...] = x_ref[...].astype(BF16)

    (out,), st = _call(name, body, (nb,), [pl.BlockSpec((tr, c), lambda i, me_ref: (i, 0))], [o_spec], [jax.ShapeDtypeStruct(shape, BF16)],
                       [w], sem=("parallel",), prefetch=me, stages=stages)
    return out, st


def _layer_norm_stats(x):
    mean = jnp.mean(x, axis=-1, keepdims=True)
    xc = x - mean
    var = jnp.mean(xc * xc, axis=-1, keepdims=True)
    rstd = lax.rsqrt(var + LN_EPS)
    return xc * rstd, rstd


def _layer_norm_bwd(dxhat, xhat, rstd):
    m1 = jnp.mean(dxhat, axis=-1, keepdims=True)
    m2 = jnp.mean(dxhat * xhat, axis=-1, keepdims=True)
    return rstd * (dxhat - m1 - xhat * m2)


def _ln1_fwd(pre1, g1, b1, tr=256, stages=()):
    s, d = pre1.shape
    tr = _pick(tr, s)

    def body(p_ref, g_ref, b_ref, xh_ref, rs_ref, h_ref):
        xhat, rstd = _layer_norm_stats(p_ref[...])
        xh_ref[...] = xhat
        rs_ref[...] = rstd
        h_ref[...] = (xhat * g_ref[...] + b_ref[...]).astype(BF16)

    return _call("ln1_fwd", body, (s // tr,), [_row_spec(tr, d), _fix_spec((1, d)), _fix_spec((1, d))],
                 [_row_spec(tr, d), _row_spec(tr, 1), _row_spec(tr, d)],
                 [jax.ShapeDtypeStruct((s, d), F32), jax.ShapeDtypeStruct((s, 1), F32), jax.ShapeDtypeStruct((s, d), BF16)],
                 [pre1, g1, b1], sem=("parallel",), stages=stages)


def _ln2_loss_bwd(ff, xhat1, g1, b1, g2, b2, target, tr=256):
    s, d = ff.shape
    tr = _pick(tr, s)

    def body(ff_ref, xh1_ref, g1_ref, b1_ref, g2_ref, b2_ref, t_ref, dp_ref, dpb_ref, dg_ref, db_ref, dbf_ref, loss_ref):
        @pl.when(pl.program_id(0) == 0)
        def _():
            dg_ref[...] = jnp.zeros_like(dg_ref)
            db_ref[...] = jnp.zeros_like(db_ref)
            dbf_ref[...] = jnp.zeros_like(dbf_ref)
            loss_ref[...] = jnp.zeros_like(loss_ref)

        h1 = xh1_ref[...] * g1_ref[...] + b1_ref[...]
        xhat, rstd = _layer_norm_stats(ALPHA * h1 + ff_ref[...])
        err = xhat * g2_ref[...] + b2_ref[...] - t_ref[...]
        row = jnp.mean(err * err, axis=-1, keepdims=True)
        loss_ref[...] += 0.5 * jnp.sum(row, axis=0, keepdims=True)
        dy = err / d
        dg_ref[...] += jnp.sum(dy * xhat, axis=0, keepdims=True)
        db_ref[...] += jnp.sum(dy, axis=0, keepdims=True)
        dpre = _layer_norm_bwd(dy * g2_ref[...], xhat, rstd)
        dbf_ref[...] += jnp.sum(dpre, axis=0, keepdims=True)
        dp_ref[...] = dpre
        dpb_ref[...] = dpre.astype(BF16)

    vec = _fix_spec((1, d))
    return _call("ln2_loss_bwd", body, (s // tr,), [_row_spec(tr, d), _row_spec(tr, d), vec, vec, vec, vec, _row_spec(tr, d)],
                 [_row_spec(tr, d), _row_spec(tr, d), vec, vec, vec, _fix_spec((1, 1))],
                 [jax.ShapeDtypeStruct((s, d), F32), jax.ShapeDtypeStruct((s, d), BF16)]
                 + [jax.ShapeDtypeStruct((1, d), F32)] * 3 + [jax.ShapeDtypeStruct((1, 1), F32)],
                 [ff, xhat1, g1, b1, g2, b2, target])[0]


def _ln1_bwd(dh1, xhat1, rstd1, g1, tr=256, stages=()):
    s, d = dh1.shape
    tr = _pick(tr, s)

    def body(dh_ref, xh_ref, rs_ref, g_ref, dp_ref, dpb_ref, dg_ref, db_ref):
        @pl.when(pl.program_id(0) == 0)
        def _():
            dg_ref[...] = jnp.zeros_like(dg_ref)
            db_ref[...] = jnp.zeros_like(db_ref)

        dh, xhat = dh_ref[...], xh_ref[...]
        dg_ref[...] += jnp.sum(dh * xhat, axis=0, keepdims=True)
        db_ref[...] += jnp.sum(dh, axis=0, keepdims=True)
        dpre = _layer_norm_bwd(dh * g_ref[...], xhat, rs_ref[...])
        dp_ref[...] = dpre
        dpb_ref[...] = dpre.astype(BF16)

    vec = _fix_spec((1, d))
    return _call("ln1_bwd", body, (s // tr,), [_row_spec(tr, d), _row_spec(tr, d), _row_spec(tr, 1), vec],
                 [_row_spec(tr, d), _row_spec(tr, d), vec, vec],
                 [jax.ShapeDtypeStruct((s, d), F32), jax.ShapeDtypeStruct((s, d), BF16)] + [jax.ShapeDtypeStruct((1, d), F32)] * 2,
                 [dh1, xhat1, rstd1, g1], stages=stages)


def _to_perm(x):
    return x.reshape(SEQ // N_SUB, N_SUB, -1).transpose(1, 0, 2).reshape(SEQ, -1)


def _from_perm(x):
    return x.reshape(N_SUB, SEQ // N_SUB, -1).transpose(1, 0, 2).reshape(SEQ, -1)


def _local_index(p):
    rho = np.arange(BLOCK)
    if p == 0:
        return 16 * (rho % 8) + rho // 8
    if p == 1:
        return 4 * (rho % 32) + rho // 32
    return rho


def _tile_view(x, p):
    c = x.shape[1]
    if p == 1:
        return x.reshape(4, 4, BLOCK, c)
    return x.reshape(N_SUB, BLOCK, c)


def _view_shape(c, p):
    return (4, 4, BLOCK, c) if p == 1 else (N_SUB, BLOCK, c)


def _tile_spec(p, width, col, shift=0):
    nblk = SEQ // DILATIONS[p] // BLOCK

    def blk(n):
        return jnp.clip(n + shift, 0, nblk - 1)

    if p == 0:
        return pl.BlockSpec((N_SUB, SUBLANES, width), lambda s, n: (0, blk(n), col))
    if p == 1:
        return pl.BlockSpec((4, None, 32, width), lambda s, n: (0, s, blk(n), col))
    return pl.BlockSpec((None, BLOCK, width), lambda s, n: (s, 0, col))


def _tile_grid(p):
    return ((1, 16), (4, 4), (16, 1))[p]


def _t5_bucket(n):
    max_exact = N_BUCKETS // 2
    nf = np.maximum(n, 1).astype(np.float32)
    large = max_exact + (np.log(nf / np.float32(max_exact)) / np.float32(math.log(MAX_DISTANCE / max_exact))
                         * np.float32(N_BUCKETS - max_exact)).astype(np.int32)
    large = np.minimum(large, N_BUCKETS - 1)
    return np.where(n < max_exact, n, large).astype(np.int32)


def _bucket_tables():
    tabs = np.zeros((3, 2, BLOCK, BLOCK), np.int32)
    for p, d in enumerate(DILATIONS):
        i = _local_index(p)
        diff = i[:, None] - i[None, :]
        tabs[p, 0] = np.where(diff <= 0, _t5_bucket((BLOCK + diff) * d), -1)
        tabs[p, 1] = np.where(diff >= 0, _t5_bucket(np.maximum(diff, 0) * d), -1)
    return tabs


def _bias_expand(rel_bias, buckets, stages=()):
    nh = N_HEADS

    def body(rb_ref, bk_ref, o_ref):
        for w in range(2):
            bk = bk_ref[0, w]
            for h in range(nh):
                val = jnp.zeros((BLOCK, BLOCK), F32)
                for b in range(N_BUCKETS):
                    val = jnp.where(bk == b, rb_ref[b, h], val)
                o_ref[0, h, w] = jnp.where(bk < 0, NEG_INF, val)

    return _call("bias_expand", body, (3,),
                 [pl.BlockSpec(memory_space=pltpu.SMEM), pl.BlockSpec((1, 2, BLOCK, BLOCK), lambda p: (p, 0, 0, 0))],
                 [pl.BlockSpec((1, nh, 2, BLOCK, BLOCK), lambda p: (p, 0, 0, 0, 0))],
                 [jax.ShapeDtypeStruct((3, nh, 2, BLOCK, BLOCK), F32)], [rel_bias, buckets], sem=("parallel",), stages=stages)


def _heads_to_lanes(cols):
    lane = lax.broadcasted_iota(I32, (BLOCK, LANES), 1)
    out = jnp.zeros((BLOCK, LANES), F32)
    for h, c in enumerate(cols):
        out = jnp.where(lane == h, c, out)
    return out


def _attn_fwd(qkv, bias, p, stages=()):
    d_a = _d_a()
    has_prev = SEQ // DILATIONS[p] // BLOCK > 1
    scale = HEAD_DIM ** -0.5
    view = _tile_view(qkv, p)

    width = 2 * BLOCK if has_prev else BLOCK

    def body(q_ref, kc_ref, kp_ref, vc_ref, vp_ref, b_ref, o_ref, l_ref, s_ref, p_ref):
        n = pl.program_id(1)
        q_all = q_ref[...].reshape(BLOCK, d_a).astype(BF16)
        k_all = kc_ref[...].reshape(BLOCK, d_a).astype(BF16)
        v_all = vc_ref[...].reshape(BLOCK, d_a).astype(BF16)
        if has_prev:
            k_all = jnp.concatenate([kp_ref[...].reshape(BLOCK, d_a).astype(BF16), k_all], axis=0)
            v_all = jnp.concatenate([vp_ref[...].reshape(BLOCK, d_a).astype(BF16), v_all], axis=0)
            no_prev = (lax.broadcasted_iota(I32, (BLOCK, width), 1) < BLOCK) & (n == 0)
        for h in range(N_HEADS):
            sl = slice(h * HEAD_DIM, (h + 1) * HEAD_DIM)
            s = _dot(q_all[:, sl], k_all[:, sl], "nt") * scale
            if has_prev:
                s = jnp.where(no_prev, NEG_INF, s + jnp.concatenate([b_ref[0, h, 0], b_ref[0, h, 1]], axis=1))
            else:
                s = s + b_ref[0, h, 1]
            s_ref[h] = s
        dens, lses = [], []
        for h in range(N_HEADS):
            s = s_ref[h]
            m = jnp.max(s, axis=-1, keepdims=True)
            pr = jnp.exp(s - m)
            den = jnp.sum(pr, axis=-1, keepdims=True)
            p_ref[h] = pr.astype(BF16)
            dens.append(den)
            lses.append(m + jnp.log(den))
        for h in range(N_HEADS):
            sl = slice(h * HEAD_DIM, (h + 1) * HEAD_DIM)
            o_ref[..., sl] = (_dot(p_ref[h], v_all[:, sl], "nn") / dens[h]).reshape(*o_ref.shape[:-1], HEAD_DIM)
        l_ref[...] = _heads_to_lanes(lses).reshape(l_ref.shape)

    (o, l), st = _call(
        f"attn_fwd{p}", body, _tile_grid(p),
        [_tile_spec(p, d_a, 0), _tile_spec(p, d_a, 1), _tile_spec(p, d_a, 1, -1), _tile_spec(p, d_a, 2), _tile_spec(p, d_a, 2, -1),
         pl.BlockSpec((1, N_HEADS, 2, BLOCK, BLOCK), lambda s, n: (p, 0, 0, 0, 0))],
        [_tile_spec(p, d_a, 0), _tile_spec(p, LANES, 0)],
        [jax.ShapeDtypeStruct(_view_shape(d_a, p), F32), jax.ShapeDtypeStruct(_view_shape(LANES, p), F32)],
        [view, view, view, view, view, bias], scratch=[pltpu.VMEM((N_HEADS, BLOCK, width), F32), pltpu.VMEM((N_HEADS, BLOCK, width), BF16)],
        sem=("parallel", "parallel"), stages=stages)
    return (o.reshape(SEQ, d_a), l.reshape(SEQ, LANES)), st


def _attn_combine(os_, ls_, tr=256, stages=()):
    d_a = _d_a()
    tr = _pick(tr, SEQ)

    def body(o0, o1, o2, l0, l1, l2, a_ref, ab_ref, lt_ref):
        l = [l0[...], l1[...], l2[...]]
        m = jnp.maximum(jnp.maximum(l[0], l[1]), l[2])
        w = [jnp.exp(x - m) for x in l]
        tot = w[0] + w[1] + w[2]
        lt_ref[...] = m + jnp.log(tot)
        w = [x / tot for x in w]
        for h in range(N_HEADS):
            sl = slice(h * HEAD_DIM, (h + 1) * HEAD_DIM)
            acc = w[0][:, h:h + 1] * o0[:, sl] + w[1][:, h:h + 1] * o1[:, sl] + w[2][:, h:h + 1] * o2[:, sl]
            a_ref[:, sl] = acc
            ab_ref[:, sl] = acc.astype(BF16)

    return _call("attn_combine", body, (SEQ // tr,), [_row_spec(tr, d_a)] * 3 + [_row_spec(tr, LANES)] * 3,
                 [_row_spec(tr, d_a), _row_spec(tr, d_a), _row_spec(tr, LANES)],
                 [jax.ShapeDtypeStruct((SEQ, d_a), F32), jax.ShapeDtypeStruct((SEQ, d_a), BF16), jax.ShapeDtypeStruct((SEQ, LANES), F32)],
                 [*os_, *ls_], sem=("parallel",), stages=stages)


def _attn_delta(dattn, attn, tr=256):
    d_a = _d_a()
    tr = _pick(tr, SEQ)

    def body(d_ref, a_ref, o_ref):
        prod = d_ref[...] * a_ref[...]
        lane = lax.broadcasted_iota(I32, (tr, LANES), 1)
        out = jnp.zeros((tr, LANES), F32)
        for h in range(N_HEADS):
            out = jnp.where(lane == h, jnp.sum(prod[:, h * HEAD_DIM:(h + 1) * HEAD_DIM], axis=-1, keepdims=True), out)
        o_ref[...] = out

    return _call("attn_delta", body, (SEQ // tr,), [_row_spec(tr, d_a)] * 2, [_row_spec(tr, LANES)],
                 [jax.ShapeDtypeStruct((SEQ, LANES), F32)], [dattn, attn], sem=("parallel",))[0][0]


def _attn_bwd(qkv, dattn, lse, delta, bias, p, stages=()):
    d_a = _d_a()
    nblk = SEQ // DILATIONS[p] // BLOCK
    has_next = nblk > 1
    scale = HEAD_DIM ** -0.5
    qv, dov, lv, tv = (_tile_view(x, p) for x in (qkv, dattn, lse, delta))

    rows = 2 * BLOCK if has_next else BLOCK

    def body(q_ref, qn_ref, k_ref, v_ref, do_ref, don_ref, l_ref, ln_ref, t_ref, tn_ref, b_ref, dq_ref, dk_ref, dv_ref, db_ref,
             carry_ref, s_ref, dp_ref, p_ref, ds_ref):
        j = pl.program_id(1)

        @pl.when((pl.program_id(0) == 0) & (j == 0))
        def _():
            db_ref[...] = jnp.zeros_like(db_ref)

        def both(cur, nxt, width, dtype):
            cur = cur[...].reshape(BLOCK, width).astype(dtype)
            return jnp.concatenate([cur, nxt[...].reshape(BLOCK, width).astype(dtype)], axis=0) if has_next else cur

        k_all = k_ref[...].reshape(BLOCK, d_a).astype(BF16)
        v_all = v_ref[...].reshape(BLOCK, d_a).astype(BF16)
        q_all, do_all = both(q_ref, qn_ref, d_a, BF16), both(do_ref, don_ref, d_a, BF16)
        l_all, t_all = both(l_ref, ln_ref, LANES, F32), both(t_ref, tn_ref, LANES, F32)
        if has_next:
            no_next = (lax.broadcasted_iota(I32, (rows, BLOCK), 0) >= BLOCK) & (j == nblk - 1)
        for h in range(N_HEADS):
            sl = slice(h * HEAD_DIM, (h + 1) * HEAD_DIM)
            s = _dot(q_all[:, sl], k_all[:, sl], "nt") * scale
            if has_next:
                s = jnp.where(no_next, NEG_INF, s + jnp.concatenate([b_ref[0, h, 1], b_ref[0, h, 0]], axis=0))
            else:
                s = s + b_ref[0, h, 1]
            s_ref[h] = s
            dp_ref[h] = _dot(do_all[:, sl], v_all[:, sl], "nt")
        for h in range(N_HEADS):
            pr = jnp.exp(s_ref[h] - l_all[:, h:h + 1])
            ds = pr * (dp_ref[h] - t_all[:, h:h + 1])
            db_ref[h, 1] += ds[:BLOCK]
            if has_next:
                db_ref[h, 0] += ds[BLOCK:]
            p_ref[h] = pr.astype(BF16)
            ds_ref[h] = ds.astype(BF16)
        for h in range(N_HEADS):
            sl = slice(h * HEAD_DIM, (h + 1) * HEAD_DIM)
            dq = _dot(ds_ref[h], k_all[:, sl], "nn") * scale
            mine = dq[:BLOCK]
            if has_next:
                mine = mine + jnp.where(j > 0, carry_ref[:, sl], 0.0)
            dq_ref[..., sl] = mine.reshape(*dq_ref.shape[:-1], HEAD_DIM)
            if has_next:
                carry_ref[:, sl] = dq[BLOCK:]
            dk_ref[..., sl] = (_dot(ds_ref[h], q_all[:, sl], "tn") * scale).reshape(*dk_ref.shape[:-1], HEAD_DIM)
            dv_ref[..., sl] = _dot(p_ref[h], do_all[:, sl], "tn").reshape(*dv_ref.shape[:-1], HEAD_DIM)

    def big(col, shift=0):
        return _tile_spec(p, d_a, col, shift)

    def small(shift=0):
        return _tile_spec(p, LANES, 0, shift)

    (dq, dk, dv, dbias), st = _call(
        f"attn_bwd{p}", body, _tile_grid(p),
        [big(0), big(0, 1), big(1), big(2), big(0), big(0, 1), small(), small(1), small(), small(1),
         pl.BlockSpec((1, N_HEADS, 2, BLOCK, BLOCK), lambda s, n: (p, 0, 0, 0, 0))],
        [big(0), big(0), big(0), pl.BlockSpec((N_HEADS, 2, BLOCK, BLOCK), lambda s, n: (0, 0, 0, 0))],
        [jax.ShapeDtypeStruct(_view_shape(d_a, p), F32)] * 3 + [jax.ShapeDtypeStruct((N_HEADS, 2, BLOCK, BLOCK), F32)],
        [qv, qv, qv, qv, dov, dov, lv, lv, tv, tv, bias],
        scratch=[pltpu.VMEM((BLOCK, d_a), F32), pltpu.VMEM((N_HEADS, rows, BLOCK), F32), pltpu.VMEM((N_HEADS, rows, BLOCK), F32),
                 pltpu.VMEM((N_HEADS, rows, BLOCK), BF16), pltpu.VMEM((N_HEADS, rows, BLOCK), BF16)], stages=stages)
    return (dq.reshape(SEQ, d_a), dk.reshape(SEQ, d_a), dv.reshape(SEQ, d_a), dbias), st


def _rel_bias_grad(dbias, buckets):
    nh = N_HEADS

    def body(d0, d1, d2, bk_ref, o_ref, t_ref):
        ds = (d0, d1, d2)

        def per_bucket(b, carry):
            for h in range(nh):
                acc = jnp.zeros((BLOCK, BLOCK), F32)
                for p in range(3):
                    for w in range(2):
                        acc = acc + jnp.where(bk_ref[p, w] == b, ds[p][h, w], 0.0)
                t_ref[pl.ds(b * nh + h, 1), :] = jnp.sum(acc, axis=0, keepdims=True)
            return carry

        lax.fori_loop(0, N_BUCKETS, per_bucket, 0)
        o_ref[...] = jnp.sum(t_ref[...], axis=-1, keepdims=True)

    return _call("rel_bias_grad", body, (1,), [_fix_spec((nh, 2, BLOCK, BLOCK))] * 3 + [_fix_spec((3, 2, BLOCK, BLOCK))],
                 [_fix_spec((N_BUCKETS * nh, 1))], [jax.ShapeDtypeStruct((N_BUCKETS * nh, 1), F32)], [*dbias, buckets],
                 scratch=[pltpu.VMEM((N_BUCKETS * nh, LANES), F32)])[0][0]


def _gmlp_fwd(rest, col0, gain, bias, ws, bs, causal, stages=()):
    d_b = _d_b()

    def body(u_ref, v_ref, g_ref, b_ref, ws_ref, bs_ref, c_ref, o_ref):
        u = u_ref[...].reshape(BLOCK, d_b)
        xhat, _ = _layer_norm_stats(_gelu(v_ref[...].reshape(BLOCK, d_b)))
        vn = (xhat * g_ref[...] + b_ref[...]).astype(BF16)
        outs = []
        for g in range(N_GROUPS):
            sl = slice(g * BLOCK, (g + 1) * BLOCK)
            w = jnp.where(c_ref[...] > 0, ws_ref[g], 0.0)
            z = _dot(w, vn[:, sl], "nn") + bs_ref[:, g:g + 1]
            outs.append(_gelu(u[:, sl]) * z)
        o_ref[...] = jnp.concatenate(outs, axis=-1).reshape(o_ref.shape)

    (out,), st = _call(
        "gmlp_fwd", body, (1, SEQ // BLOCK),
        [_tile_spec(0, d_b, col0), _tile_spec(0, d_b, col0 + 1), _fix_spec((1, d_b)), _fix_spec((1, d_b)),
         _fix_spec((N_GROUPS, BLOCK, BLOCK)), _fix_spec((BLOCK, N_GROUPS)), _fix_spec((BLOCK, BLOCK))],
        [_tile_spec(0, d_b, 0)], [jax.ShapeDtypeStruct(_view_shape(d_b, 0), F32)],
        [_tile_view(rest, 0), _tile_view(rest, 0), gain, bias, ws, bs, causal], sem=("parallel", "parallel"), stages=stages)
    return out.reshape(SEQ, d_b), st


def _gmlp_bwd(rest, col0, dgmlp, gain, bias, ws, bs, causal, stages=()):
    d_b = _d_b()
    nchunk = SEQ // BLOCK

    def body(u_ref, v_ref, dg_ref, g_ref, b_ref, ws_ref, bs_ref, c_ref, du_ref, dv_ref, dws_ref, dbs_ref, dgain_ref, dbias_ref):
        c = pl.program_id(1)

        @pl.when(c == 0)
        def _():
            dws_ref[...] = jnp.zeros_like(dws_ref)
            dbs_ref[...] = jnp.zeros_like(dbs_ref)
            dgain_ref[...] = jnp.zeros_like(dgain_ref)
            dbias_ref[...] = jnp.zeros_like(dbias_ref)

        u = u_ref[...].reshape(BLOCK, d_b)
        v = v_ref[...].reshape(BLOCK, d_b)
        dgm = dg_ref[...].reshape(BLOCK, d_b)
        xhat, rstd = _layer_norm_stats(_gelu(v))
        vn = (xhat * g_ref[...] + b_ref[...]).astype(BF16)
        lane = lax.broadcasted_iota(I32, (BLOCK, LANES), 1)
        dus, dvns = [], []
        dbs = dbs_ref[...]
        for g in range(N_GROUPS):
            sl = slice(g * BLOCK, (g + 1) * BLOCK)
            w = jnp.where(c_ref[...] > 0, ws_ref[g], 0.0).astype(BF16)
            z = _dot(w, vn[:, sl], "nn") + bs_ref[:, g:g + 1]
            dz = dgm[:, sl] * _gelu(u[:, sl])
            dus.append(dgm[:, sl] * z * _gelu_grad(u[:, sl]))
            dws_ref[g] += _dot(dz, vn[:, sl], "nt")
            dbs = dbs + jnp.where(lane == g, jnp.sum(dz, axis=-1, keepdims=True), 0.0)
            dvns.append(_dot(w, dz, "tn"))
        dbs_ref[...] = dbs
        dvn = jnp.concatenate(dvns, axis=-1)
        dgain_ref[...] += jnp.sum(dvn * xhat, axis=0, keepdims=True)
        dbias_ref[...] += jnp.sum(dvn, axis=0, keepdims=True)
        dvg = _layer_norm_bwd(dvn * g_ref[...], xhat, rstd)
        du_ref[...] = jnp.concatenate(dus, axis=-1).reshape(du_ref.shape)
        dv_ref[...] = (dvg * _gelu_grad(v)).reshape(dv_ref.shape)

        @pl.when(c == nchunk - 1)
        def _():
            for g in range(N_GROUPS):
                dws_ref[g] = jnp.where(c_ref[...] > 0, dws_ref[g], 0.0)

    (du, dv, dws, dbs, dgain, dbias), st = _call(
        "gmlp_bwd", body, (1, nchunk),
        [_tile_spec(0, d_b, col0), _tile_spec(0, d_b, col0 + 1), _tile_spec(0, d_b, 0), _fix_spec((1, d_b)), _fix_spec((1, d_b)),
         _fix_spec((N_GROUPS, BLOCK, BLOCK)), _fix_spec((BLOCK, N_GROUPS)), _fix_spec((BLOCK, BLOCK))],
        [_tile_spec(0, d_b, 0), _tile_spec(0, d_b, 0), _fix_spec((N_GROUPS, BLOCK, BLOCK)), _fix_spec((BLOCK, LANES)),
         _fix_spec((1, d_b)), _fix_spec((1, d_b))],
        [jax.ShapeDtypeStruct(_view_shape(d_b, 0), F32)] * 2
        + [jax.ShapeDtypeStruct((N_GROUPS, BLOCK, BLOCK), F32), jax.ShapeDtypeStruct((BLOCK, LANES), F32)]
        + [jax.ShapeDtypeStruct((1, d_b), F32)] * 2,
        [_tile_view(rest, 0), _tile_view(rest, 0), _tile_view(dgmlp, 0), gain, bias, ws, bs, causal], stages=stages)
    return (du.reshape(SEQ, d_b), dv.reshape(SEQ, d_b), dws, dbs, dgain, dbias), st


def _assemble_dproj(dqkv, du, dv, dga, dgb, tr=128, stages=()):
    d_a, d_b, d_in = _d_a(), _d_b(), _d_in()
    tr = _pick(tr, SEQ)

    def body(*refs):
        att, (du_ref, dv_ref, dga_ref, dgb_ref, o_ref) = refs[:9], refs[9:]
        for i in range(3):
            o_ref[:, i * d_a:(i + 1) * d_a] = (att[3 * i][...] + att[3 * i + 1][...] + att[3 * i + 2][...]).astype(BF16)
        o_ref[:, 3 * d_a:3 * d_a + d_b] = du_ref[...].astype(BF16)
        o_ref[:, 3 * d_a + d_b:3 * d_a + 2 * d_b] = dv_ref[...].astype(BF16)
        o_ref[:, 3 * d_a + 2 * d_b:3 * d_a + 2 * d_b + D_MODEL] = dga_ref[...]
        o_ref[:, 3 * d_a + 2 * d_b + D_MODEL:] = dgb_ref[...]

    return _call("assemble_dproj", body, (SEQ // tr,), [_row_spec(tr, d_a)] * 9 + [_row_spec(tr, d_b)] * 2 + [_row_spec(tr, D_MODEL)] * 2,
                 [_row_spec(tr, d_in)], [jax.ShapeDtypeStruct((SEQ, d_in), BF16)], [*dqkv, du, dv, dga, dgb], sem=("parallel",), stages=stages)


def _dw(name, a, b, kind, core, mine, add=None, tn=1152, half=None, stages=()):
    s, m = a.shape
    n = b.shape[1]
    if half is not None:
        m //= 2
    rs, cs = (m, n // N_DEV) if kind == "col" else (m // N_DEV, n)
    tn = _pick(tn if cs % tn == 0 else 512, cs)
    nj = cs // tn

    def shard(q, c_ref):
        return 2 * q + (c_ref[0] if mine else 1 - c_ref[0])

    if kind == "col":
        a_spec = pl.BlockSpec((s, m), lambda q, j, c_ref: (0, half or 0))
        b_spec = pl.BlockSpec((s, tn), lambda q, j, c_ref: (0, shard(q, c_ref) * nj + j))
    else:
        a_spec = pl.BlockSpec((s, rs), lambda q, j, c_ref: (0, shard(q, c_ref)))
        b_spec = pl.BlockSpec((s, tn), lambda q, j, c_ref: (0, j))
    o_spec = pl.BlockSpec((None, rs, tn), lambda q, j, c_ref: (q, 0, j))

    def body(a_ref, b_ref, *rest):
        acc = _dot(a_ref[...], b_ref[...], "tn")
        if add is not None:
            acc = acc + rest[0][...].astype(F32)
        rest[-1][...] = acc.astype(BF16)

    (out,), st = _call(name, body, (N_CHIPS, nj), [a_spec, b_spec] + ([o_spec] if add is not None else []), [o_spec],
                       [jax.ShapeDtypeStruct((N_CHIPS, rs, cs), BF16)], [a, b] + ([add] if add is not None else []),
                       sem=("parallel", "parallel"), stages=stages, prefetch=core)
    return out, st


def _adamw(w, g, m, v):
    m = ADAM_B1 * m + (1.0 - ADAM_B1) * g
    v = ADAM_B2 * v + (1.0 - ADAM_B2) * (g * g)
    m_hat = m / (1.0 - ADAM_B1 ** ADAM_STEP)
    v_hat = v / (1.0 - ADAM_B2 ** ADAM_STEP)
    delta = -ADAM_LR * (m_hat / (jnp.sqrt(v_hat) + ADAM_EPS) + ADAM_WD * w)
    return delta, m, v


def _adam_shard(name, pair, chip_sums, chip, w, m, v, tr=256, rows=None, into=None, stages=()):
    rs, cs = w.shape
    lo, cnt = rows or (0, rs)
    assert pair.shape[1] == cnt
    tr = _pick(tr, cnt, lo)
    first = lo // tr

    def body(chip_ref, own_ref, *refs):
        slots, (w_ref, m_ref, v_ref), (g_ref, d_ref, nm_ref, nv_ref) = refs[:N_CHIPS], refs[N_CHIPS:N_CHIPS + 3], refs[-4:]
        g = None
        for q in range(N_CHIPS):
            term = jnp.where(chip_ref[0] == q, own_ref[...], slots[q][...]).astype(F32)
            g = term if g is None else g + term
        d, nm, nv = _adamw(w_ref[...], g, m_ref[...], v_ref[...])
        g_ref[...], d_ref[...], nm_ref[...], nv_ref[...] = g, d, nm, nv

    def slot(q):
        return pl.BlockSpec((None, tr, cs), lambda i, c_ref: (jnp.where(c_ref[0] == q, (q + 1) % N_CHIPS, q), i, 0))

    spec = pl.BlockSpec((tr, cs), lambda i, c_ref: (i + first, 0))
    n_in = 1 + N_CHIPS + 3
    return _call(name, body, (cnt // tr,),
                 [pl.BlockSpec((None, tr, cs), lambda i, c_ref: (c_ref[0], i, 0))] + [slot(q) for q in range(N_CHIPS)]
                 + [spec, spec, spec] + ([ANY] * 4 if into is not None else []),
                 [spec] * 4, [jax.ShapeDtypeStruct((rs, cs), F32)] * 4, [pair] + [chip_sums] * N_CHIPS + [w, m, v] + list(into or ()),
                 sem=("parallel",), stages=stages, prefetch=chip, shown=True,
                 alias={n_in + k: k for k in range(4)} if into is not None else None)


def _adam_small(part, parts, me, w, m, v):
    rows = w.shape[0]

    def body(me_ref, own_ref, *refs):
        slots, (w_ref, m_ref, v_ref, g_ref, d_ref, nm_ref, nv_ref) = refs[:N_DEV], refs[N_DEV:]
        g = None
        for j in range(N_DEV):
            term = jnp.where(me_ref[0] == j, own_ref[...], slots[j][...])
            g = term if g is None else g + term
        d, nm, nv = _adamw(w_ref[...], g, m_ref[...], v_ref[...])
        g_ref[...], d_ref[...], nm_ref[...], nv_ref[...] = g, d, nm, nv

    def slot(j):
        return pl.BlockSpec((None, rows, LANES), lambda i, me_ref: (jnp.where(me_ref[0] == j, (j + 1) % N_DEV, j), 0, 0))

    spec = _fix_spec((rows, LANES))
    return _call("adam_small", body, (1,), [spec] + [slot(j) for j in range(N_DEV)] + [spec, spec, spec], [spec] * 4,
                 [jax.ShapeDtypeStruct((rows, LANES), F32)] * 4, [part] + [parts] * N_DEV + [w, m, v], prefetch=me, shown=True)[0]


def _small_sizes():
    d_b = _d_b()
    return (("loss", 1), ("rel_bias", N_BUCKETS * N_HEADS), ("ln_v_gain", d_b), ("ln_v_bias", d_b),
            ("w_spatial", N_GROUPS * BLOCK * BLOCK), ("b_spatial", N_GROUPS * BLOCK), ("ln1_gain", D_MODEL), ("ln1_bias", D_MODEL),
            ("b_ff1", D_FF), ("b_ff2", D_MODEL), ("ln2_gain", D_MODEL), ("ln2_bias", D_MODEL))


def _pack(vals):
    pieces = []
    for name, size in _small_sizes():
        flat = vals[name].reshape(-1).astype(F32)
        padded = -(-size // (SUBLANES * LANES)) * SUBLANES * LANES
        pieces.append(jnp.pad(flat, (0, padded - size)).reshape(-1, LANES))
    return jnp.concatenate(pieces, axis=0)


def _unpack(buf):
    out, row = {}, 0
    for name, size in _small_sizes():
        rows = -(-size // (SUBLANES * LANES)) * SUBLANES
        out[name] = buf[row:row + rows].reshape(-1)[:size]
        row += rows
    return out


def kernel(x, w_in, rel_bias, ln_v_gain, ln_v_bias, w_spatial, b_spatial, w_proj_a, w_proj_b, w_out, ln1_gain, ln1_bias, w_ff1, b_ff1, w_ff2, b_ff2, ln2_gain, ln2_bias, loss_target, m_w_in, m_rel_bias, m_ln_v_gain, m_ln_v_bias, m_w_spatial, m_b_spatial, m_w_proj_a, m_w_proj_b, m_w_out, m_ln1_gain, m_ln1_bias, m_w_ff1, m_b_ff1, m_w_ff2, m_b_ff2, m_ln2_gain, m_ln2_bias, v_w_in, v_rel_bias, v_ln_v_gain, v_ln_v_bias, v_w_spatial, v_b_spatial, v_w_proj_a, v_w_proj_b, v_w_out, v_ln1_gain, v_ln1_bias, v_w_ff1, v_b_ff1, v_w_ff2, v_b_ff2, v_ln2_gain, v_ln2_bias):
    d_a, d_b, d_in = _d_a(), _d_b(), _d_in()
    weights = dict(w_in=w_in, rel_bias=rel_bias, ln_v_gain=ln_v_gain, ln_v_bias=ln_v_bias, w_spatial=w_spatial, b_spatial=b_spatial,
                   w_proj_a=w_proj_a, w_proj_b=w_proj_b, w_out=w_out, ln1_gain=ln1_gain, ln1_bias=ln1_bias, w_ff1=w_ff1, b_ff1=b_ff1,
                   w_ff2=w_ff2, b_ff2=b_ff2, ln2_gain=ln2_gain, ln2_bias=ln2_bias)
    mom1 = dict(w_in=m_w_in, rel_bias=m_rel_bias, ln_v_gain=m_ln_v_gain, ln_v_bias=m_ln_v_bias, w_spatial=m_w_spatial,
                b_spatial=m_b_spatial, w_proj_a=m_w_proj_a, w_proj_b=m_w_proj_b, w_out=m_w_out, ln1_gain=m_ln1_gain,
                ln1_bias=m_ln1_bias, w_ff1=m_w_ff1, b_ff1=m_b_ff1, w_ff2=m_w_ff2, b_ff2=m_b_ff2, ln2_gain=m_ln2_gain, ln2_bias=m_ln2_bias)
    mom2 = dict(w_in=v_w_in, rel_bias=v_rel_bias, ln_v_gain=v_ln_v_gain, ln_v_bias=v_ln_v_bias, w_spatial=v_w_spatial,
                b_spatial=v_b_spatial, w_proj_a=v_w_proj_a, w_proj_b=v_w_proj_b, w_out=v_w_out, ln1_gain=v_ln1_gain,
                ln1_bias=v_ln1_bias, w_ff1=v_w_ff1, b_ff1=v_b_ff1, w_ff2=v_w_ff2, b_ff2=v_b_ff2, ln2_gain=v_ln2_gain, ln2_bias=v_ln2_bias)

    mx, my, mc = _coords()
    me = (4 * mx + 2 * my + mc).astype(I32).reshape(1)
    chip = (2 * mx + my).astype(I32).reshape(1)
    full = {n: _cast_into_place(f"cast_{n}", weights[n][0], KINDS[n], me)[0] for n in KINDS if n != "w_ff1"}
    sent = {n: (0, 0, 0) for n in KINDS}

    def keep(table, n):
        def store(outs):
            table[n] = outs[0]
        return store

    def gathering(**new):
        stages = []
        for n in KINDS:
            out, relayed, passed = sent[n]
            units = new.get(n, 0)
            if units or relayed < out or passed < relayed:
                st = _gather_stage(full[n], KINDS[n], (out, units) if units else None,
                                   (relayed, out - relayed) if relayed < out else None, (passed, relayed - passed) if passed < relayed else None)
                st.store = keep(full, n)
                sent[n] = (out + units, out, relayed)
                stages.append(st)
        return stages

    def settle(stages, outs):
        for st, o in zip(stages, outs):
            st.store(o)

    def alone(name, stages):
        settle(stages, _comm_only(name, stages))

    def here(n):
        assert sent[n] == (16, 16, 16), (n, sent[n])
        return full[n]

    hosted = gathering(w_in=16)
    full["w_ff1"], st = _cast_into_place("cast_w_ff1", weights["w_ff1"][0], KINDS["w_ff1"], me, stages=hosted)
    settle(hosted, st)
    xs = _to_perm(x[0])
    target = _to_perm(loss_target[0])
    hosted = gathering()
    xb, st = _cast_bf16("cast_x", xs, stages=hosted)
    settle(hosted, st)
    g8 = BLOCK // N_SUB
    ws_t = w_spatial[0].reshape(N_GROUPS, g8, N_SUB, g8, N_SUB).transpose(0, 2, 1, 4, 3).reshape(N_GROUPS, BLOCK, BLOCK)
    bs_t = b_spatial[0].reshape(N_GROUPS, g8, N_SUB).transpose(2, 1, 0).reshape(BLOCK, N_GROUPS)
    idx = _local_index(0)
    causal = jnp.asarray((idx[:, None] >= idx[None, :]).astype(np.float32))
    buckets = jnp.asarray(_bucket_tables())
    hosted = gathering(w_proj_a=16, w_proj_b=16)
    (bias,), st = _bias_expand(rel_bias, buckets, stages=hosted)
    settle(hosted, st)

    hosted = gathering(w_out=8, w_ff1=1)
    (qkv,), st = _matmul("proj_qkv", xb, here("w_in"), "nn", [F32], n=3 * d_a, stages=hosted)
    settle(hosted, st)
    hosted = gathering(w_out=8, w_ff1=6)
    (rest,), st = _matmul("proj_rest", xb, here("w_in"), "nn", [F32], b_off=3 * d_a, n=d_in - 3 * d_a, stages=hosted)
    settle(hosted, st)
    fwd = []
    for p in range(3):
        hosted = gathering(**({"w_ff1": 2}, {"w_ff1": 5}, {"w_ff1": 2, "w_ff2": 1})[p])
        res, st = _attn_fwd(qkv, bias, p, stages=hosted)
        settle(hosted, st)
        fwd.append(res)
    hosted = gathering(w_ff2=1)
    (attn, attn_b, lse), st = _attn_combine([o for o, _ in fwd], [l for _, l in fwd], stages=hosted)
    settle(hosted, st)
    hosted = gathering(w_ff2=2)
    gmlp, st = _gmlp_fwd(rest, 0, ln_v_gain, ln_v_bias, ws_t, bs_t, causal, stages=hosted)
    settle(hosted, st)
    hosted = gathering(w_ff2=3)
    (ya,), st = _matmul("proj_a", attn_b, here("w_proj_a"), "nn", [BF16], stages=hosted)
    settle(hosted, st)
    gate_a, gate_b = 2 * d_b, 2 * d_b + D_MODEL

    def merge(acc, ya_, ga, gb):
        return acc, _sigmoid(ga) * ya_ + _sigmoid(gb) * acc

    hosted = gathering(w_ff2=5)
    (yb, merged), st = _matmul("proj_b_merge", gmlp, here("w_proj_b"), "nn", [BF16, BF16], merge,
                               [(ya, "mn", 0), (rest, "mn", gate_a), (rest, "mn", gate_b)], tn=256, stages=hosted)
    settle(hosted, st)
    hosted = gathering(w_ff2=3)
    (pre1,), st = _matmul("out_proj", merged, here("w_out"), "nn", [F32], lambda acc, x_: (ALPHA * x_ + acc,), [(xs, "mn", 0)], stages=hosted)
    settle(hosted, st)
    hosted = gathering(w_ff2=1)
    (xhat1, rstd1, h1b), st = _ln1_fwd(pre1, ln1_gain, ln1_bias, stages=hosted)
    settle(hosted, st)

    def relu2(acc, b_):
        r = jnp.maximum(acc + b_, 0.0)
        return r, r * r

    hosted = gathering()
    (relu, fb), st = _matmul("ff1", h1b, here("w_ff1"), "nn", [BF16, BF16], relu2, [(b_ff1, "row", 0)], stages=hosted)
    settle(hosted, st)
    alone("gather_w_ff2_sibling", gathering())
    (ff,), _ = _matmul("ff2", fb, here("w_ff2"), "nn", [F32], lambda acc, b_: (acc + b_,), [(b_ff2, "row", 0)], tn=512, tk=2048, k_outer=True)

    core = lax.axis_index("c").astype(I32).reshape(1)
    factors, theirs, sib, pair, chips, reduced = {}, {}, {}, {}, {}, {}

    def grad_for_sibling(n, a, b, stages=(), half=None, of=None):
        factors[n] = (a, b, KINDS[of or n], half)
        theirs[n], outs = _dw(f"dw_{n}_sibling", a, b, KINDS[of or n], core, False, half=half, stages=stages)
        settle(stages, outs)

    def to_sibling(n):
        st = _to_sibling_stage(theirs[n])
        st.store = keep(sib, n)
        return st

    def grad_own(n, stages=()):
        a, b, kind, half = factors[n]
        pair[n], outs = _dw(f"dw_{n}_own", a, b, kind, core, True, add=sib[n], half=half, stages=stages)
        settle(stages, outs)
        chips[n] = lax.empty(pair[n].shape, BF16)
        reduced[n] = 0

    def reducing(**new):
        stages = []
        for n, units in new.items():
            st = _to_chips_stage(pair[n], chips[n], (reduced[n], units))
            st.store = keep(chips, n)
            reduced[n] += units
            stages.append(st)
        return stages

    def summed(n):
        assert reduced[n] == 16, (n, reduced[n])
        return chips[n]

    dpre2, dpre2b, g_ln2_gain, g_ln2_bias, g_b_ff2, loss_part = _ln2_loss_bwd(ff, xhat1, ln1_gain, ln1_bias, ln2_gain, ln2_bias, target)
    grad_for_sibling("w_ff2", fb, dpre2b)

    def relu2_bwd(acc, r):
        da = acc * (2.0 * r)
        return da, da

    hosted = [to_sibling("w_ff2")]
    (dab, g_b_ff1), st = _matmul("d_ff1", dpre2b, here("w_ff2"), "nt", [BF16], relu2_bwd, [(relu, "mn", 0)], colsums=(1,), stages=hosted)
    settle(hosted, st)
    grad_own("w_ff2")
    grad_for_sibling("w_ff1", h1b, dab, reducing(w_ff2=3))
    hosted = reducing(w_ff2=8) + [to_sibling("w_ff1")]
    (dh1,), st = _matmul("d_h1", dab, here("w_ff1"), "nt", [F32], lambda acc, d_: (acc + ALPHA * d_,), [(dpre2, "mn", 0)], tk=1024, k_outer=True, stages=hosted)
    settle(hosted, st)
    grad_own("w_ff1", reducing(w_ff2=4))
    hosted = reducing(w_ff2=1)
    (dpre1, dpre1b, g_ln1_gain, g_ln1_bias), st = _ln1_bwd(dh1, xhat1, rstd1, ln1_gain, stages=hosted)
    settle(hosted, st)
    grad_for_sibling("w_out", merged, dpre1b, reducing(w_ff1=1))

    def merge_bwd(acc, ga, gb, ya_, yb_):
        sa, sb = _sigmoid(ga), _sigmoid(gb)
        return acc * sa, acc * sb, acc * ya_ * (sa * (1.0 - sa)), acc * yb_ * (sb * (1.0 - sb))

    hosted = reducing(w_ff1=6) + [to_sibling("w_out")]
    (dya, dyb, dga, dgb), st = _matmul("d_merge", dpre1b, here("w_out"), "nt", [BF16] * 4, merge_bwd,
                                       [(rest, "mn", gate_a), (rest, "mn", gate_b), (ya, "mn", 0), (yb, "mn", 0)], tn=256, stages=hosted)
    settle(hosted, st)
    grad_own("w_out", reducing(w_ff1=1))
    grad_for_sibling("w_proj_a", attn_b, dya)
    grad_for_sibling("w_proj_b", gmlp, dyb)
    hosted = reducing(w_ff1=1) + [to_sibling("w_proj_a"), to_sibling("w_proj_b")]
    (dattn,), st = _matmul("d_attn", dya, here("w_proj_a"), "nt", [F32], stages=hosted)
    settle(hosted, st)
    grad_own("w_proj_a")
    grad_own("w_proj_b")
    hosted = reducing(w_ff1=1)
    (dgmlp,), st = _matmul("d_gmlp", dyb, here("w_proj_b"), "nt", [F32], stages=hosted)
    settle(hosted, st)
    hosted = reducing(w_ff1=2)
    (du, dvb, dws_t, dbs_t, g_lnv_gain, g_lnv_bias), st = _gmlp_bwd(rest, 0, dgmlp, ln_v_gain, ln_v_bias, ws_t, bs_t, causal, stages=hosted)
    settle(hosted, st)
    delta = _attn_delta(dattn, attn)
    bwd = []
    for p in range(3):
        hosted = reducing(**({"w_ff1": 3}, {"w_ff1": 1, "w_out": 8}, {"w_out": 8, "w_proj_a": 8})[p])
        res, st = _attn_bwd(qkv, dattn, lse, delta, bias, p, stages=hosted)
        settle(hosted, st)
        bwd.append(res)
    g_rel_bias = _rel_bias_grad([b[3] for b in bwd], buckets)
    hosted = reducing(w_proj_a=8, w_proj_b=16)
    (dproj,), st = _assemble_dproj([b[i] for i in range(3) for b in bwd], du, dvb, dga, dgb, stages=hosted)
    settle(hosted, st)

    g_w_spatial = dws_t.reshape(N_GROUPS, N_SUB, g8, N_SUB, g8).transpose(0, 2, 1, 4, 3)
    g_b_spatial = dbs_t[:, :N_GROUPS].reshape(N_SUB, g8, N_GROUPS).transpose(2, 1, 0)
    part = _pack(dict(loss=loss_part, rel_bias=g_rel_bias, ln_v_gain=g_lnv_gain, ln_v_bias=g_lnv_bias, w_spatial=g_w_spatial,
                      b_spatial=g_b_spatial, ln1_gain=g_ln1_gain, ln1_bias=g_ln1_bias, b_ff1=g_b_ff1, b_ff2=g_b_ff2,
                      ln2_gain=g_ln2_gain, ln2_bias=g_ln2_bias))
    sib["small"] = lax.empty((N_DEV, *part.shape), F32)
    cut = part.shape[0] // 2 // SUBLANES * SUBLANES

    def small(rows):
        st = _small_stage(part, sib["small"], rows)
        st.store = keep(sib, "small")
        return st

    grad_for_sibling("w_in_top", xb, dproj, [small((0, cut))], half=0, of="w_in")
    grad_for_sibling("w_in_bot", xb, dproj, [small((cut, part.shape[0] - cut)), to_sibling("w_in_top")], half=1, of="w_in")
    parts = sib["small"]
    grad_own("w_in_top", [to_sibling("w_in_bot")])
    grad_own("w_in_bot", reducing(w_in_top=4))

    def add_residual(acc, d_):
        return (acc + ALPHA * d_,)

    hosted = reducing(w_in_top=12, w_in_bot=7)
    (dx,), st = _matmul("d_x", dproj, here("w_in"), "nt", [F32], add_residual, [(dpre1, "mn", 0)], tn=512, tk=3072, stages=hosted)
    settle(hosted, st)
    grad_x = _from_perm(dx)[None]

    out_g, out_d, out_m, out_v = {}, {}, {}, {}
    for n, units in (("w_out", 3), ("w_proj_a", 3), ("w_proj_b", 3), ("w_ff2", 0), ("w_ff1", 0)):
        hosted = reducing(w_in_bot=units) if units else []
        (g, d, nm, nv), st = _adam_shard(f"adam_{n}", pair[n], summed(n), chip, weights[n][0], mom1[n][0], mom2[n][0], stages=hosted)
        settle(hosted, st)
        out_g[n], out_d[n], out_m[n], out_v[n] = g[None], d[None], nm[None], nv[None]
    rows = weights["w_in"].shape[1] // 2
    done = None
    for i, n in enumerate(("w_in_top", "w_in_bot")):
        done, _ = _adam_shard(f"adam_{n}", pair[n], summed(n), chip, weights["w_in"][0], mom1["w_in"][0], mom2["w_in"][0],
                              rows=(i * rows, rows), into=done)
    out_g["w_in"], out_d["w_in"], out_m["w_in"], out_v["w_in"] = (t[None] for t in done)

    zero = jnp.zeros((1,), F32)
    sg, sd, sm, sv = (_unpack(b) for b in _adam_small(
        part, parts, me, _pack({**weights, "loss": zero}), _pack({**mom1, "loss": zero}), _pack({**mom2, "loss": zero})))
    for n in WEIGHT_ORDER:
        if n not in KINDS:
            shape = weights[n].shape
            out_g[n], out_d[n], out_m[n], out_v[n] = (t[n].reshape(shape) for t in (sg, sd, sm, sv))
    loss = sg["loss"].reshape(())
    return (loss, grad_x, *[out_g[n] for n in WEIGHT_ORDER], *[out_d[n] for n in WEIGHT_ORDER],
            *[out_m[n] for n in WEIGHT_ORDER], *[out_v[n] for n in WEIGHT_ORDER])
```

```python
import math

import jax
import jax.numpy as jnp
import numpy as np
from jax import lax
from jax.experimental import pallas as pl
from jax.experimental.pallas import tpu as pltpu

F32 = jnp.float32
BF16 = jnp.bfloat16
I32 = jnp.int32

SEQ = 2048
D_MODEL = 2048
HEAD_DIM = 128
N_HEADS = 8
N_GROUPS = 8
D_FF = 4 * D_MODEL
BLOCK = 128
DILATIONS = (1, 4, 16)
N_BUCKETS = 32
MAX_DISTANCE = 2048
ALPHA = 2.0 ** 0.25
LN_EPS = 1e-5
NEG_INF = -1e30
N_DEV = 8
N_CHIPS = 4
N_SUB = 16
ADAM_LR, ADAM_B1, ADAM_B2, ADAM_EPS, ADAM_WD, ADAM_STEP = 0.001, 0.9, 0.999, 1e-08, 0.01, 10
LANES = 128
SUBLANES = 8
VMEM_LIMIT = 56 * 1024 * 1024
MESH = pl.DeviceIdType.MESH
ANY = pl.BlockSpec(memory_space=pl.ANY)
WEIGHT_ORDER = ("w_in", "rel_bias", "ln_v_gain", "ln_v_bias", "w_spatial", "b_spatial", "w_proj_a", "w_proj_b", "w_out",
                "ln1_gain", "ln1_bias", "w_ff1", "b_ff1", "w_ff2", "b_ff2", "ln2_gain", "ln2_bias")
KINDS = {"w_in": "col", "w_proj_a": "col", "w_proj_b": "col", "w_out": "row", "w_ff1": "col", "w_ff2": "row"}


def _d_a():
    return N_HEADS * HEAD_DIM


def _d_b():
    return N_GROUPS * BLOCK


def _d_in():
    return 3 * _d_a() + 2 * _d_b() + 2 * D_MODEL


def _pick(t, n, *others):
    if n <= t and all(o % n == 0 for o in others):
        return n
    for c in range(min(t, n) // LANES * LANES, 0, -LANES):
        if n % c == 0 and all(o % c == 0 for o in others):
            return c
    raise ValueError((t, n, others))


class _Stage:
    def __init__(self, ins, outs, alias, sems, start, finish):
        self.ins, self.outs, self.alias, self.sems, self.start, self.finish = ins, outs, alias, sems, start, finish


def _call(name, body, grid, in_specs, out_specs, out_shape, operands, scratch=(), sem=None, stages=(), sequential=False, prefetch=None,
          shown=False, alias=None):
    n_in, n_out, n_sc = len(in_specs), len(out_specs), len(scratch)
    st_in = [len(s.ins) for s in stages]
    st_out = [len(s.outs) for s in stages]
    st_sem = [len(s.sems) for s in stages]
    n_pre = 0 if prefetch is None else 1
    aliases, ioff, ooff = {i + n_pre: o for i, o in (alias or {}).items()}, n_in + n_pre, n_out
    for s, ni, no in zip(stages, st_in, st_out):
        for i, o in s.alias.items():
            aliases[ioff + i] = ooff + o
        ioff, ooff = ioff + ni, ooff + no

    def split(refs, counts):
        out, at = [], 0
        for c in counts:
            out.append(refs[at:at + c])
            at += c
        return out

    def wrapped(*refs):
        ins, sins, outs, souts, sc, ssems = split(refs[n_pre:], [n_in, sum(st_in), n_out, sum(st_out), n_sc, sum(st_sem)])
        parts = list(zip(stages, split(sins, st_in), split(souts, st_out), split(ssems, st_sem)))
        if sequential:
            for s, a, b, c in parts:
                s.start(a, b, c)
            for s, a, b, c in parts:
                s.finish(a, b, c)
            return
        if parts:
            first = _all_of([pl.program_id(i) == 0 for i in range(len(grid))])
            last = _all_of([pl.program_id(i) == g - 1 for i, g in enumerate(grid)])

            @pl.when(first)
            def _():
                for s, a, b, c in parts:
                    s.start(a, b, c)

        body(*(refs[:n_pre] if shown else ()), *ins, *outs, *sc)
        if parts:
            @pl.when(last)
            def _():
                for s, a, b, c in parts:
                    s.finish(a, b, c)

    if stages or sem is None:
        sem = ("arbitrary",) * len(grid)
    specs = dict(grid=grid, in_specs=list(in_specs) + [ANY] * sum(st_in), out_specs=list(out_specs) + [ANY] * sum(st_out),
                 scratch_shapes=list(scratch) + [x for s in stages for x in s.sems])
    if prefetch is not None:
        specs = dict(grid_spec=pltpu.PrefetchScalarGridSpec(num_scalar_prefetch=1, **specs))
    res = pl.pallas_call(
        wrapped, name=name, out_shape=list(out_shape) + [o for s in stages for o in s.outs], input_output_aliases=aliases,
        compiler_params=pltpu.CompilerParams(dimension_semantics=sem, vmem_limit_bytes=VMEM_LIMIT), **specs,
    )(*([prefetch] if n_pre else []), *operands, *[a for s in stages for a in s.ins])
    res = list(res)
    return res[:n_out], split(res[n_out:], st_out)


def _all_of(conds):
    out = conds[0]
    for c in conds[1:]:
        out = out & c
    return out


def _coords():
    return lax.axis_index("x"), lax.axis_index("y"), lax.axis_index("c")


def _other_chips(x, y):
    return ((1 - x, y), (x, 1 - y), (1 - x, 1 - y))


def _lin(dev):
    return 4 * dev[0] + 2 * dev[1] + dev[2]


def _piece(total, lo, n, units=16):
    assert total % units == 0
    return lo * (total // units), n * (total // units)


FLOWS = 4


def _split(lo, cnt):
    k = next(k for k in (FLOWS, 2, 1) if cnt % (2 * SUBLANES * k) == 0)
    return [(lo + i * (cnt // k), cnt // k) for i in range(k)]


def _remote(src, dst, send, recv, to):
    return pltpu.make_async_remote_copy(src_ref=src, dst_ref=dst, send_sem=send, recv_sem=recv, device_id=to, device_id_type=MESH)


def _placer(kind, n, lo, cnt):
    def place(ref, dev):
        if kind == "col":
            return ref.at[pl.ds(lo, cnt), pl.ds(pl.multiple_of(_lin(dev) * n, LANES), n)]
        return ref.at[pl.ds(pl.multiple_of(_lin(dev) * n + lo, 2 * SUBLANES), cnt), :]
    return place


def _spread_stage(full, kind, piece=(0, 16), home=False):
    n = (full.shape[1] if kind == "col" else full.shape[0]) // N_DEV
    lo, cnt = _piece(full.shape[0] if kind == "col" else n, *piece)
    parts = _split(lo, cnt)
    npeers = 1 if home else 2

    def copies(ins, outs, sems):
        send, recv = sems
        x, y, c = _coords()
        me = (x, y, c)
        peers = [(x, y, 1 - c)] if home else [(1 - x, y, c), (x, 1 - y, c)]
        out, arrive = [], []
        for k, t in enumerate(peers):
            for i, (plo, pcnt) in enumerate(parts):
                place = _placer(kind, n, plo, pcnt)
                out.append(_remote(place(outs[0], me), place(outs[0], me), send.at[i, k], recv.at[i, k], t))
                arrive.append(_remote(place(outs[0], t), place(outs[0], t), send.at[i, k], recv.at[i, k], t))
        return out, arrive

    def start(ins, outs, sems):
        for cp in copies(ins, outs, sems)[0]:
            cp.start()

    def finish(ins, outs, sems):
        out, arrive = copies(ins, outs, sems)
        for cp in arrive:
            cp.wait_recv()
        for cp in out:
            cp.wait_send()

    return _Stage([full], [jax.ShapeDtypeStruct(full.shape, full.dtype)], {0: 0},
                  [pltpu.SemaphoreType.DMA((len(parts), npeers)), pltpu.SemaphoreType.DMA((len(parts), npeers))], start, finish)


def _relay_stage(full, kind, piece=(0, 16)):
    n = (full.shape[1] if kind == "col" else full.shape[0]) // N_DEV
    lo, cnt = _piece(full.shape[0] if kind == "col" else n, *piece)
    half = cnt // 2
    assert half % (2 * SUBLANES) == 0, (cnt, kind)
    tops, bottoms = _split(lo, half), _split(lo + half, half)

    def copies(ins, outs, sems):
        send, recv = sems
        x, y, c = _coords()
        xn, yn, dg = (1 - x, y, c), (x, 1 - y, c), (1 - x, 1 - y, c)
        out, arrive, k = [], [], 0
        for came_from, to, parts in ((yn, xn, tops), (xn, yn, bottoms)):
            for plo, pcnt in parts:
                place = _placer(kind, n, plo, pcnt)
                out.append(_remote(place(outs[0], came_from), place(outs[0], came_from), send.at[k], recv.at[k], to))
                arrive.append(_remote(place(outs[0], dg), place(outs[0], dg), send.at[k], recv.at[k], to))
                k += 1
        return out, arrive

    def start(ins, outs, sems):
        for cp in copies(ins, outs, sems)[0]:
            cp.start()

    def finish(ins, outs, sems):
        out, arrive = copies(ins, outs, sems)
        for cp in arrive:
            cp.wait_recv()
        for cp in out:
            cp.wait_send()

    return _Stage([full], [jax.ShapeDtypeStruct(full.shape, full.dtype)], {0: 0},
                  [pltpu.SemaphoreType.DMA((len(tops) + len(bottoms),)), pltpu.SemaphoreType.DMA((len(tops) + len(bottoms),))], start, finish)


def _forward_stage(full, kind, piece=(0, 16)):
    n = (full.shape[1] if kind == "col" else full.shape[0]) // N_DEV
    lo, cnt = _piece(full.shape[0] if kind == "col" else n, *piece)
    place = _placer(kind, n, lo, cnt)

    def copies(ins, outs, sems):
        send, recv = sems
        x, y, c = _coords()
        chips = _other_chips(x, y)
        out = [_remote(place(outs[0], (*chip, c)), place(outs[0], (*chip, c)), send.at[k], recv.at[k], (x, y, 1 - c)) for k, chip in enumerate(chips)]
        arrive = [_remote(place(outs[0], (*chip, 1 - c)), place(outs[0], (*chip, 1 - c)), send.at[k], recv.at[k], (x, y, 1 - c))
                  for k, chip in enumerate(chips)]
        return out, arrive

    def start(ins, outs, sems):
        for cp in copies(ins, outs, sems)[0]:
            cp.start()

    def finish(ins, outs, sems):
        out, arrive = copies(ins, outs, sems)
        for cp in arrive:
            cp.wait_recv()
        for cp in out:
            cp.wait_send()

    return _Stage([full], [jax.ShapeDtypeStruct(full.shape, full.dtype)], {0: 0},
                  [pltpu.SemaphoreType.DMA((3,)), pltpu.SemaphoreType.DMA((3,))], start, finish)


def _to_sibling_stage(theirs):
    def copies(ins, outs, sems):
        send, recv = sems
        x, y, c = _coords()
        return [_remote(ins[0].at[q], outs[0].at[q], send.at[q], recv.at[q], (x, y, 1 - c)) for q in range(N_CHIPS)]

    def start(ins, outs, sems):
        for cp in copies(ins, outs, sems):
            cp.start()

    def finish(ins, outs, sems):
        for cp in copies(ins, outs, sems):
            cp.wait()

    return _Stage([theirs], [jax.ShapeDtypeStruct(theirs.shape, BF16)], {},
                  [pltpu.SemaphoreType.DMA((N_CHIPS,)), pltpu.SemaphoreType.DMA((N_CHIPS,))], start, finish)


def _to_chips_stage(pair, dst, piece=(0, 16)):
    lo, cnt = _piece(pair.shape[1], *piece)
    parts = _split(lo, cnt)
    nsem = 3 * len(parts)

    def copies(ins, outs, sems):
        send, recv = sems
        x, y, c = _coords()
        mine = 2 * x + y
        out, arrive, k = [], [], 0
        for px, py in _other_chips(x, y):
            for plo, pcnt in parts:
                rows = pl.ds(plo, pcnt)
                out.append(_remote(ins[0].at[2 * px + py, rows, :], outs[0].at[mine, rows, :], send.at[k], recv.at[k], (px, py, c)))
                arrive.append(_remote(ins[0].at[2 * px + py, rows, :], outs[0].at[2 * px + py, rows, :], send.at[k], recv.at[k], (px, py, c)))
                k += 1
        return out, arrive

    def start(ins, outs, sems):
        for cp in copies(ins, outs, sems)[0]:
            cp.start()

    def finish(ins, outs, sems):
        out, arrive = copies(ins, outs, sems)
        for cp in arrive:
            cp.wait_recv()
        for cp in out:
            cp.wait_send()

    return _Stage([pair, dst], [jax.ShapeDtypeStruct(dst.shape, dst.dtype)], {1: 0},
                  [pltpu.SemaphoreType.DMA((nsem,)), pltpu.SemaphoreType.DMA((nsem,))], start, finish)


def _fuse(parts, ins, outs, alias):
    parts = [p for p in parts if p is not None]
    sems = [x for st, _, _ in parts for x in st.sems]

    def run(which):
        def go(i, o, s):
            refs, at = list(i) + list(o), 0
            for st, pi, po in parts:
                getattr(st, which)([refs[k] for k in pi], [refs[k] for k in po], s[at:at + len(st.sems)])
                at += len(st.sems)
        return go

    return _Stage(ins, [jax.ShapeDtypeStruct(o.shape, o.dtype) for o in outs], alias, sems, run("start"), run("finish"))


def _gather_stage(full, kind, new=None, relay=None, forward=None):
    return _fuse([(_spread_stage(full, kind, new), [0], [1]) if new else None,
                  (_relay_stage(full, kind, relay), [0], [1]) if relay else None,
                  (_spread_stage(full, kind, relay, home=True), [0], [1]) if relay else None,
                  (_forward_stage(full, kind, forward), [0], [1]) if forward else None], [full], [full], {0: 0})


def _small_stage(part, dst, rows):
    piece = pl.ds(*rows)

    def copies(ins, outs, sems):
        send, recv = sems
        x, y, c = _coords()
        me = (x, y, c)
        peers = [(1 - x if k & 4 else x, 1 - y if k & 2 else y, 1 - c if k & 1 else c) for k in range(1, N_DEV)]
        out = [_remote(ins[0].at[piece, :], outs[0].at[_lin(me), piece, :], send.at[k], recv.at[k], t) for k, t in enumerate(peers)]
        arrive = [_remote(ins[0].at[piece, :], outs[0].at[_lin(t), piece, :], send.at[k], recv.at[k], t) for k, t in enumerate(peers)]
        return out, arrive

    def start(ins, outs, sems):
        for cp in copies(ins, outs, sems)[0]:
            cp.start()

    def finish(ins, outs, sems):
        out, arrive = copies(ins, outs, sems)
        for cp in arrive:
            cp.wait_recv()
        for cp in out:
            cp.wait_send()

    return _Stage([part, dst], [jax.ShapeDtypeStruct(dst.shape, F32)], {1: 0},
                  [pltpu.SemaphoreType.DMA((N_DEV - 1,)), pltpu.SemaphoreType.DMA((N_DEV - 1,))], start, finish)


def _comm_only(name, stages):
    return _call(name, lambda: None, (1,), [], [], [], [], stages=stages, sequential=True)[1]


_GELU_C = math.sqrt(2.0 / math.pi)


def _gelu(x):
    return 0.5 * x * (1.0 + jnp.tanh(_GELU_C * (x + 0.044715 * x * x * x)))


def _gelu_grad(x):
    t = jnp.tanh(_GELU_C * (x + 0.044715 * x * x * x))
    return 0.5 * (1.0 + t) + 0.5 * x * (1.0 - t * t) * (_GELU_C * (1.0 + 3.0 * 0.044715 * x * x))


def _sigmoid(x):
    return 1.0 / (1.0 + jnp.exp(-x))


def _dot(a, b, mode):
    dims = {"nn": (((1,), (0,)), ((), ())), "nt": (((1,), (1,)), ((), ())), "tn": (((0,), (0,)), ((), ()))}[mode]
    return lax.dot_general(a.astype(BF16), b.astype(BF16), dims, preferred_element_type=F32)


def _matmul(name, a, b, mode, outs, epi=None, extras=(), colsums=(), tm=2048, tn=512, tk=2048, b_off=0, n=None, m_off=0, m=None, stages=()):
    if mode == "tn":
        kk, mfull = a.shape
    else:
        mfull, kk = a.shape
    m = mfull if m is None else m
    n = (b.shape[0] if mode == "nt" else b.shape[1]) if n is None else n
    tm, tk = _pick(tm, m, m_off), _pick(tk, kk)
    tn = _pick(tn, n, b_off, *[off for _, _, off in extras])
    boff, moff = b_off // tn, m_off // tm
    nm, nn_, nk = m // tm, n // tn, kk // tk
    col_major = bool(colsums)
    grid = (nn_, nm, nk) if col_major else (nm, nn_, nk)

    def imap(f):
        if col_major:
            return lambda g0, g1, k: f(g1, g0, k)
        return f

    a_spec = (pl.BlockSpec((tk, tm), imap(lambda i, j, k: (k, i + moff))) if mode == "tn"
              else pl.BlockSpec((tm, tk), imap(lambda i, j, k: (i + moff, k))))
    b_spec = (pl.BlockSpec((tn, tk), imap(lambda i, j, k: (j + boff, k))) if mode == "nt"
              else pl.BlockSpec((tk, tn), imap(lambda i, j, k: (k, j + boff))))
    in_specs, operands = [a_spec, b_spec], [a, b]
    for arr, kind, off in extras:
        o = off // tn
        if kind == "mn":
            in_specs.append(pl.BlockSpec((tm, tn), imap(lambda i, j, k, o=o: (i + moff, j + o))))
        else:
            in_specs.append(pl.BlockSpec((1, tn), imap(lambda i, j, k, o=o: (0, j + o))))
        operands.append(arr)
    out_shape = [jax.ShapeDtypeStruct((m, n), dt) for dt in outs] + [jax.ShapeDtypeStruct((1, n), F32) for _ in colsums]
    out_specs = ([pl.BlockSpec((tm, tn), imap(lambda i, j, k: (i, j))) for _ in outs]
                 + [pl.BlockSpec((1, tn), imap(lambda i, j, k: (0, j))) for _ in colsums])
    n_ex, n_out, n_cs = len(extras), len(outs), len(colsums)

    def body(*refs):
        a_ref, b_ref = refs[:2]
        ex_refs = refs[2:2 + n_ex]
        out_refs = refs[2 + n_ex:2 + n_ex + n_out]
        cs_refs = refs[2 + n_ex + n_out:2 + n_ex + n_out + n_cs]
        part = _dot(a_ref[...], b_ref[...], mode)

        def finish(acc):
            res = epi(acc, *[r[...] for r in ex_refs]) if epi is not None else (acc,)
            for r, v in zip(out_refs, res[:n_out]):
                r[...] = v.astype(r.dtype)
            if n_cs:
                @pl.when(pl.program_id(1) == 0)
                def _():
                    for r in cs_refs:
                        r[...] = jnp.zeros_like(r)

                for r, idx in zip(cs_refs, colsums):
                    r[...] += jnp.sum(res[idx], axis=0, keepdims=True)

        if nk == 1:
            finish(part)
        else:
            acc_ref = refs[-1]
            k = pl.program_id(2)

            @pl.when(k == 0)
            def _():
                acc_ref[...] = part

            @pl.when(k > 0)
            def _():
                acc_ref[...] += part

            @pl.when(k == nk - 1)
            def _():
                finish(acc_ref[...])

    sem = ("arbitrary", "arbitrary", "arbitrary") if col_major else ("parallel", "parallel", "arbitrary")
    return _call(name, body, grid, in_specs, out_specs, out_shape, operands,
                 scratch=[pltpu.VMEM((tm, tn), F32)] if nk > 1 else [], sem=sem, stages=stages)


def _project_shards(name, a, b, which, into=None, stages=()):
    m, kk = a.shape
    n = b.shape[1]
    tn = n // N_DEV
    o_spec = pl.BlockSpec((m, tn), lambda s, w_ref: (0, w_ref[s]))

    def body(a_ref, b_ref, *rest):
        rest[-1][...] = _dot(a_ref[...], b_ref[...], "nn")

    (out,), st = _call(name, body, (which.shape[0],),
                       [pl.BlockSpec((m, kk), lambda s, w_ref: (0, 0)), pl.BlockSpec((kk, tn), lambda s, w_ref: (0, w_ref[s]))]
                       + ([ANY] if into is not None else []), [o_spec], [jax.ShapeDtypeStruct((m, n), F32)],
                       [a, b] + ([into] if into is not None else []), sem=("arbitrary",), stages=stages, prefetch=which,
                       alias={2: 0} if into is not None else None)
    return out, st


def _row_spec(tr, c):
    return pl.BlockSpec((tr, c), lambda i: (i, 0))


def _fix_spec(shape):
    return pl.BlockSpec(shape, lambda *_: tuple(0 for _ in shape))


def _cast_bf16(name, x, tr=512, stages=()):
    r, c = x.shape
    tr = _pick(tr, r)

    def body(x_ref, o_ref):
        o_ref[...] = x_ref[...].astype(BF16)

    (out,), st = _call(name, body, (r // tr,), [_row_spec(tr, c)], [_row_spec(tr, c)], [jax.ShapeDtypeStruct((r, c), BF16)], [x],
                       sem=("parallel",), stages=stages)
    return out, st


def _perm_in(name, x, target, tc=LANES, stages=()):
    s, d = x.shape
    tc = _pick(tc, d)
    rows = s // N_SUB

    def body(x_ref, t_ref, xs_ref, xb_ref, ts_ref):
        for r in range(N_SUB):
            piece = pl.ds(r * rows, rows)
            v = x_ref[pl.ds(r, rows, stride=N_SUB), :]
            xs_ref[piece, :] = v
            xb_ref[piece, :] = v.astype(BF16)
            ts_ref[piece, :] = t_ref[pl.ds(r, rows, stride=N_SUB), :]

    col = pl.BlockSpec((s, tc), lambda j: (0, j))
    (xs, xb, ts), st = _call(name, body, (d // tc,), [col, col], [col, col, col],
                             [jax.ShapeDtypeStruct((s, d), F32), jax.ShapeDtypeStruct((s, d), BF16), jax.ShapeDtypeStruct((s, d), F32)],
                             [x, target], sem=("parallel",), stages=stages)
    return xs, xb, ts, st


def _cast_into_place(name, w, kind, me, tr=512, stages=()):
    r, c = w.shape
    tr = _pick(tr, r)
    nb = r // tr
    if kind == "col":
        o_spec = pl.BlockSpec((tr, c), lambda i, me_ref: (i, me_ref[0]))
        shape = (r, c * N_DEV)
    else:
        o_spec = pl.BlockSpec((tr, c), lambda i, me_ref: (me_ref[0] * nb + i, 0))
        shape = (r * N_DEV, c)

    def body(x_ref, o_ref):
        o_ref[...] = x_ref[...].astype(BF16)

    (out,), st = _call(name, body, (nb,), [pl.BlockSpec((tr, c), lambda i, me_ref: (i, 0))], [o_spec], [jax.ShapeDtypeStruct(shape, BF16)],
                       [w], sem=("parallel",), prefetch=me, stages=stages)
    return out, st


def _layer_norm_stats(x):
    mean = jnp.mean(x, axis=-1, keepdims=True)
    xc = x - mean
    var = jnp.mean(xc * xc, axis=-1, keepdims=True)
    rstd = lax.rsqrt(var + LN_EPS)
    return xc * rstd, rstd


def _layer_norm_bwd(dxhat, xhat, rstd):
    m1 = jnp.mean(dxhat, axis=-1, keepdims=True)
    m2 = jnp.mean(dxhat * xhat, axis=-1, keepdims=True)
    return rstd * (dxhat - m1 - xhat * m2)


def _ln1_fwd(pre1, g1, b1, tr=256, stages=()):
    s, d = pre1.shape
    tr = _pick(tr, s)

    def body(p_ref, g_ref, b_ref, xh_ref, rs_ref, h_ref):
        xhat, rstd = _layer_norm_stats(p_ref[...])
        xh_ref[...] = xhat
        rs_ref[...] = rstd
        h_ref[...] = (xhat * g_ref[...] + b_ref[...]).astype(BF16)

    return _call("ln1_fwd", body, (s // tr,), [_row_spec(tr, d), _fix_spec((1, d)), _fix_spec((1, d))],
                 [_row_spec(tr, d), _row_spec(tr, 1), _row_spec(tr, d)],
                 [jax.ShapeDtypeStruct((s, d), F32), jax.ShapeDtypeStruct((s, 1), F32), jax.ShapeDtypeStruct((s, d), BF16)],
                 [pre1, g1, b1], sem=("parallel",), stages=stages)


def _ln2_loss_bwd(ff, xhat1, g1, b1, g2, b2, target, tr=256):
    s, d = ff.shape
    tr = _pick(tr, s)

    def body(ff_ref, xh1_ref, g1_ref, b1_ref, g2_ref, b2_ref, t_ref, dp_ref, dpb_ref, dg_ref, db_ref, dbf_ref, loss_ref):
        @pl.when(pl.program_id(0) == 0)
        def _():
            dg_ref[...] = jnp.zeros_like(dg_ref)
            db_ref[...] = jnp.zeros_like(db_ref)
            dbf_ref[...] = jnp.zeros_like(dbf_ref)
            loss_ref[...] = jnp.zeros_like(loss_ref)

        h1 = xh1_ref[...] * g1_ref[...] + b1_ref[...]
        xhat, rstd = _layer_norm_stats(ALPHA * h1 + ff_ref[...])
        err = xhat * g2_ref[...] + b2_ref[...] - t_ref[...]
        row = jnp.mean(err * err, axis=-1, keepdims=True)
        loss_ref[...] += 0.5 * jnp.sum(row, axis=0, keepdims=True)
        dy = err / d
        dg_ref[...] += jnp.sum(dy * xhat, axis=0, keepdims=True)
        db_ref[...] += jnp.sum(dy, axis=0, keepdims=True)
        dpre = _layer_norm_bwd(dy * g2_ref[...], xhat, rstd)
        dbf_ref[...] += jnp.sum(dpre, axis=0, keepdims=True)
        dp_ref[...] = dpre
        dpb_ref[...] = dpre.astype(BF16)

    vec = _fix_spec((1, d))
    return _call("ln2_loss_bwd", body, (s // tr,), [_row_spec(tr, d), _row_spec(tr, d), vec, vec, vec, vec, _row_spec(tr, d)],
                 [_row_spec(tr, d), _row_spec(tr, d), vec, vec, vec, _fix_spec((1, 1))],
                 [jax.ShapeDtypeStruct((s, d), F32), jax.ShapeDtypeStruct((s, d), BF16)]
                 + [jax.ShapeDtypeStruct((1, d), F32)] * 3 + [jax.ShapeDtypeStruct((1, 1), F32)],
                 [ff, xhat1, g1, b1, g2, b2, target])[0]


def _ln1_bwd(dh1, xhat1, rstd1, g1, tr=256, stages=()):
    s, d = dh1.shape
    tr = _pick(tr, s)

    def body(dh_ref, xh_ref, rs_ref, g_ref, dp_ref, dpb_ref, dg_ref, db_ref):
        @pl.when(pl.program_id(0) == 0)
        def _():
            dg_ref[...] = jnp.zeros_like(dg_ref)
            db_ref[...] = jnp.zeros_like(db_ref)

        dh, xhat = dh_ref[...], xh_ref[...]
        dg_ref[...] += jnp.sum(dh * xhat, axis=0, keepdims=True)
        db_ref[...] += jnp.sum(dh, axis=0, keepdims=True)
        dpre = _layer_norm_bwd(dh * g_ref[...], xhat, rs_ref[...])
        dp_ref[...] = dpre
        dpb_ref[...] = dpre.astype(BF16)

    vec = _fix_spec((1, d))
    return _call("ln1_bwd", body, (s // tr,), [_row_spec(tr, d), _row_spec(tr, d), _row_spec(tr, 1), vec],
                 [_row_spec(tr, d), _row_spec(tr, d), vec, vec],
                 [jax.ShapeDtypeStruct((s, d), F32), jax.ShapeDtypeStruct((s, d), BF16)] + [jax.ShapeDtypeStruct((1, d), F32)] * 2,
                 [dh1, xhat1, rstd1, g1], stages=stages)


def _from_perm(x):
    return x.reshape(N_SUB, SEQ // N_SUB, -1).transpose(1, 0, 2).reshape(SEQ, -1)


def _local_index(p):
    rho = np.arange(BLOCK)
    if p == 0:
        return 16 * (rho % 8) + rho // 8
    if p == 1:
        return 4 * (rho % 32) + rho // 32
    return rho


def _tile_view(x, p):
    c = x.shape[1]
    if p == 1:
        return x.reshape(4, 4, BLOCK, c)
    return x.reshape(N_SUB, BLOCK, c)


def _view_shape(c, p):
    return (4, 4, BLOCK, c) if p == 1 else (N_SUB, BLOCK, c)


def _tile_spec(p, width, col, shift=0):
    nblk = SEQ // DILATIONS[p] // BLOCK

    def blk(n):
        return jnp.clip(n + shift, 0, nblk - 1)

    if p == 0:
        return pl.BlockSpec((N_SUB, SUBLANES, width), lambda s, n: (0, blk(n), col))
    if p == 1:
        return pl.BlockSpec((4, None, 32, width), lambda s, n: (0, s, blk(n), col))
    return pl.BlockSpec((None, BLOCK, width), lambda s, n: (s, 0, col))


def _tile_grid(p):
    return ((1, 16), (4, 4), (16, 1))[p]


def _t5_bucket(n):
    max_exact = N_BUCKETS // 2
    nf = np.maximum(n, 1).astype(np.float32)
    large = max_exact + (np.log(nf / np.float32(max_exact)) / np.float32(math.log(MAX_DISTANCE / max_exact))
                         * np.float32(N_BUCKETS - max_exact)).astype(np.int32)
    large = np.minimum(large, N_BUCKETS - 1)
    return np.where(n < max_exact, n, large).astype(np.int32)


def _bucket_tables():
    tabs = np.zeros((3, 2, BLOCK, BLOCK), np.int32)
    for p, d in enumerate(DILATIONS):
        i = _local_index(p)
        diff = i[:, None] - i[None, :]
        tabs[p, 0] = np.where(diff <= 0, _t5_bucket((BLOCK + diff) * d), -1)
        tabs[p, 1] = np.where(diff >= 0, _t5_bucket(np.maximum(diff, 0) * d), -1)
    return tabs


def _bias_expand(rel_bias, buckets, stages=()):
    nh = N_HEADS

    def body(rb_ref, bk_ref, o_ref):
        for w in range(2):
            bk = bk_ref[0, w]
            for h in range(nh):
                val = jnp.zeros((BLOCK, BLOCK), F32)
                for b in range(N_BUCKETS):
                    val = jnp.where(bk == b, rb_ref[b, h], val)
                o_ref[0, h, w] = jnp.where(bk < 0, NEG_INF, val)

    return _call("bias_expand", body, (3,),
                 [pl.BlockSpec(memory_space=pltpu.SMEM), pl.BlockSpec((1, 2, BLOCK, BLOCK), lambda p: (p, 0, 0, 0))],
                 [pl.BlockSpec((1, nh, 2, BLOCK, BLOCK), lambda p: (p, 0, 0, 0, 0))],
                 [jax.ShapeDtypeStruct((3, nh, 2, BLOCK, BLOCK), F32)], [rel_bias, buckets], sem=("parallel",), stages=stages)


def _heads_to_lanes(cols):
    lane = lax.broadcasted_iota(I32, (BLOCK, LANES), 1)
    out = jnp.zeros((BLOCK, LANES), F32)
    for h, c in enumerate(cols):
        out = jnp.where(lane == h, c, out)
    return out


def _attn_fwd(qkv, bias, p, stages=()):
    d_a = _d_a()
    has_prev = SEQ // DILATIONS[p] // BLOCK > 1
    scale = HEAD_DIM ** -0.5
    view = _tile_view(qkv, p)

    width = 2 * BLOCK if has_prev else BLOCK

    def body(q_ref, kc_ref, kp_ref, vc_ref, vp_ref, b_ref, o_ref, l_ref, s_ref, p_ref):
        n = pl.program_id(1)
        q_all = q_ref[...].reshape(BLOCK, d_a).astype(BF16)
        k_all = kc_ref[...].reshape(BLOCK, d_a).astype(BF16)
        v_all = vc_ref[...].reshape(BLOCK, d_a).astype(BF16)
        if has_prev:
            k_all = jnp.concatenate([kp_ref[...].reshape(BLOCK, d_a).astype(BF16), k_all], axis=0)
            v_all = jnp.concatenate([vp_ref[...].reshape(BLOCK, d_a).astype(BF16), v_all], axis=0)
            no_prev = (lax.broadcasted_iota(I32, (BLOCK, width), 1) < BLOCK) & (n == 0)
        for h in range(N_HEADS):
            sl = slice(h * HEAD_DIM, (h + 1) * HEAD_DIM)
            s = _dot(q_all[:, sl], k_all[:, sl], "nt") * scale
            if has_prev:
                s = jnp.where(no_prev, NEG_INF, s + jnp.concatenate([b_ref[0, h, 0], b_ref[0, h, 1]], axis=1))
            else:
                s = s + b_ref[0, h, 1]
            s_ref[h] = s
        dens, lses = [], []
        for h in range(N_HEADS):
            s = s_ref[h]
            m = jnp.max(s, axis=-1, keepdims=True)
            pr = jnp.exp(s - m)
            den = jnp.sum(pr, axis=-1, keepdims=True)
            p_ref[h] = pr.astype(BF16)
            dens.append(den)
            lses.append(m + jnp.log(den))
        for h in range(N_HEADS):
            sl = slice(h * HEAD_DIM, (h + 1) * HEAD_DIM)
            o_ref[..., sl] = (_dot(p_ref[h], v_all[:, sl], "nn") / dens[h]).reshape(*o_ref.shape[:-1], HEAD_DIM)
        l_ref[...] = _heads_to_lanes(lses).reshape(l_ref.shape)

    (o, l), st = _call(
        f"attn_fwd{p}", body, _tile_grid(p),
        [_tile_spec(p, d_a, 0), _tile_spec(p, d_a, 1), _tile_spec(p, d_a, 1, -1), _tile_spec(p, d_a, 2), _tile_spec(p, d_a, 2, -1),
         pl.BlockSpec((1, N_HEADS, 2, BLOCK, BLOCK), lambda s, n: (p, 0, 0, 0, 0))],
        [_tile_spec(p, d_a, 0), _tile_spec(p, LANES, 0)],
        [jax.ShapeDtypeStruct(_view_shape(d_a, p), F32), jax.ShapeDtypeStruct(_view_shape(LANES, p), F32)],
        [view, view, view, view, view, bias], scratch=[pltpu.VMEM((N_HEADS, BLOCK, width), F32), pltpu.VMEM((N_HEADS, BLOCK, width), BF16)],
        sem=("parallel", "parallel"), stages=stages)
    return (o.reshape(SEQ, d_a), l.reshape(SEQ, LANES)), st


def _attn_combine(os_, ls_, tr=256, stages=()):
    d_a = _d_a()
    tr = _pick(tr, SEQ)

    def body(o0, o1, o2, l0, l1, l2, a_ref, ab_ref, lt_ref):
        l = [l0[...], l1[...], l2[...]]
        m = jnp.maximum(jnp.maximum(l[0], l[1]), l[2])
        w = [jnp.exp(x - m) for x in l]
        tot = w[0] + w[1] + w[2]
        lt_ref[...] = m + jnp.log(tot)
        w = [x / tot for x in w]
        for h in range(N_HEADS):
            sl = slice(h * HEAD_DIM, (h + 1) * HEAD_DIM)
            acc = w[0][:, h:h + 1] * o0[:, sl] + w[1][:, h:h + 1] * o1[:, sl] + w[2][:, h:h + 1] * o2[:, sl]
            a_ref[:, sl] = acc
            ab_ref[:, sl] = acc.astype(BF16)

    return _call("attn_combine", body, (SEQ // tr,), [_row_spec(tr, d_a)] * 3 + [_row_spec(tr, LANES)] * 3,
                 [_row_spec(tr, d_a), _row_spec(tr, d_a), _row_spec(tr, LANES)],
                 [jax.ShapeDtypeStruct((SEQ, d_a), F32), jax.ShapeDtypeStruct((SEQ, d_a), BF16), jax.ShapeDtypeStruct((SEQ, LANES), F32)],
                 [*os_, *ls_], sem=("parallel",), stages=stages)


def _attn_delta(dattn, attn, tr=256):
    d_a = _d_a()
    tr = _pick(tr, SEQ)

    def body(d_ref, a_ref, o_ref):
        prod = d_ref[...] * a_ref[...]
        lane = lax.broadcasted_iota(I32, (tr, LANES), 1)
        out = jnp.zeros((tr, LANES), F32)
        for h in range(N_HEADS):
            out = jnp.where(lane == h, jnp.sum(prod[:, h * HEAD_DIM:(h + 1) * HEAD_DIM], axis=-1, keepdims=True), out)
        o_ref[...] = out

    return _call("attn_delta", body, (SEQ // tr,), [_row_spec(tr, d_a)] * 2, [_row_spec(tr, LANES)],
                 [jax.ShapeDtypeStruct((SEQ, LANES), F32)], [dattn, attn], sem=("parallel",))[0][0]


def _attn_bwd(qkv, dattn, lse, delta, bias, p, stages=()):
    d_a = _d_a()
    nblk = SEQ // DILATIONS[p] // BLOCK
    has_next = nblk > 1
    scale = HEAD_DIM ** -0.5
    qv, dov, lv, tv = (_tile_view(x, p) for x in (qkv, dattn, lse, delta))

    rows = 2 * BLOCK if has_next else BLOCK

    def body(q_ref, qn_ref, k_ref, v_ref, do_ref, don_ref, l_ref, ln_ref, t_ref, tn_ref, b_ref, dq_ref, dk_ref, dv_ref, db_ref,
             carry_ref, s_ref, dp_ref, p_ref, ds_ref):
        j = pl.program_id(1)

        @pl.when((pl.program_id(0) == 0) & (j == 0))
        def _():
            db_ref[...] = jnp.zeros_like(db_ref)

        def both(cur, nxt, width, dtype):
            cur = cur[...].reshape(BLOCK, width).astype(dtype)
            return jnp.concatenate([cur, nxt[...].reshape(BLOCK, width).astype(dtype)], axis=0) if has_next else cur

        k_all = k_ref[...].reshape(BLOCK, d_a).astype(BF16)
        v_all = v_ref[...].reshape(BLOCK, d_a).astype(BF16)
        q_all, do_all = both(q_ref, qn_ref, d_a, BF16), both(do_ref, don_ref, d_a, BF16)
        l_all, t_all = both(l_ref, ln_ref, LANES, F32), both(t_ref, tn_ref, LANES, F32)
        if has_next:
            no_next = (lax.broadcasted_iota(I32, (rows, BLOCK), 0) >= BLOCK) & (j == nblk - 1)
        for h in range(N_HEADS):
            sl = slice(h * HEAD_DIM, (h + 1) * HEAD_DIM)
            s = _dot(q_all[:, sl], k_all[:, sl], "nt") * scale
            if has_next:
                s = jnp.where(no_next, NEG_INF, s + jnp.concatenate([b_ref[0, h, 1], b_ref[0, h, 0]], axis=0))
            else:
                s = s + b_ref[0, h, 1]
            s_ref[h] = s
            dp_ref[h] = _dot(do_all[:, sl], v_all[:, sl], "nt")
        for h in range(N_HEADS):
            pr = jnp.exp(s_ref[h] - l_all[:, h:h + 1])
            ds = pr * (dp_ref[h] - t_all[:, h:h + 1])
            db_ref[h, 1] += ds[:BLOCK]
            if has_next:
                db_ref[h, 0] += ds[BLOCK:]
            p_ref[h] = pr.astype(BF16)
            ds_ref[h] = ds.astype(BF16)
        for h in range(N_HEADS):
            sl = slice(h * HEAD_DIM, (h + 1) * HEAD_DIM)
            dq = _dot(ds_ref[h], k_all[:, sl], "nn") * scale
            mine = dq[:BLOCK]
            if has_next:
                mine = mine + jnp.where(j > 0, carry_ref[:, sl], 0.0)
            dq_ref[..., sl] = mine.reshape(*dq_ref.shape[:-1], HEAD_DIM)
            if has_next:
                carry_ref[:, sl] = dq[BLOCK:]
            dk_ref[..., sl] = (_dot(ds_ref[h], q_all[:, sl], "tn") * scale).reshape(*dk_ref.shape[:-1], HEAD_DIM)
            dv_ref[..., sl] = _dot(p_ref[h], do_all[:, sl], "tn").reshape(*dv_ref.shape[:-1], HEAD_DIM)

    def big(col, shift=0):
        return _tile_spec(p, d_a, col, shift)

    def small(shift=0):
        return _tile_spec(p, LANES, 0, shift)

    (dq, dk, dv, dbias), st = _call(
        f"attn_bwd{p}", body, _tile_grid(p),
        [big(0), big(0, 1), big(1), big(2), big(0), big(0, 1), small(), small(1), small(), small(1),
         pl.BlockSpec((1, N_HEADS, 2, BLOCK, BLOCK), lambda s, n: (p, 0, 0, 0, 0))],
        [big(0), big(0), big(0), pl.BlockSpec((N_HEADS, 2, BLOCK, BLOCK), lambda s, n: (0, 0, 0, 0))],
        [jax.ShapeDtypeStruct(_view_shape(d_a, p), F32)] * 3 + [jax.ShapeDtypeStruct((N_HEADS, 2, BLOCK, BLOCK), F32)],
        [qv, qv, qv, qv, dov, dov, lv, lv, tv, tv, bias],
        scratch=[pltpu.VMEM((BLOCK, d_a), F32), pltpu.VMEM((N_HEADS, rows, BLOCK), F32), pltpu.VMEM((N_HEADS, rows, BLOCK), F32),
                 pltpu.VMEM((N_HEADS, rows, BLOCK), BF16), pltpu.VMEM((N_HEADS, rows, BLOCK), BF16)], stages=stages)
    return (dq.reshape(SEQ, d_a), dk.reshape(SEQ, d_a), dv.reshape(SEQ, d_a), dbias), st


def _rel_bias_grad(dbias, buckets):
    nh = N_HEADS

    def body(d0, d1, d2, bk_ref, o_ref, t_ref):
        ds = (d0, d1, d2)

        def per_bucket(b, carry):
            for h in range(nh):
                acc = jnp.zeros((BLOCK, BLOCK), F32)
                for p in range(3):
                    for w in range(2):
                        acc = acc + jnp.where(bk_ref[p, w] == b, ds[p][h, w], 0.0)
                t_ref[pl.ds(b * nh + h, 1), :] = jnp.sum(acc, axis=0, keepdims=True)
            return carry

        lax.fori_loop(0, N_BUCKETS, per_bucket, 0)
        o_ref[...] = jnp.sum(t_ref[...], axis=-1, keepdims=True)

    return _call("rel_bias_grad", body, (1,), [_fix_spec((nh, 2, BLOCK, BLOCK))] * 3 + [_fix_spec((3, 2, BLOCK, BLOCK))],
                 [_fix_spec((N_BUCKETS * nh, 1))], [jax.ShapeDtypeStruct((N_BUCKETS * nh, 1), F32)], [*dbias, buckets],
                 scratch=[pltpu.VMEM((N_BUCKETS * nh, LANES), F32)])[0][0]


def _gmlp_fwd(rest, col0, gain, bias, ws, bs, causal, stages=()):
    d_b = _d_b()

    def body(u_ref, v_ref, g_ref, b_ref, ws_ref, bs_ref, c_ref, o_ref):
        u = u_ref[...].reshape(BLOCK, d_b)
        xhat, _ = _layer_norm_stats(_gelu(v_ref[...].reshape(BLOCK, d_b)))
        vn = (xhat * g_ref[...] + b_ref[...]).astype(BF16)
        outs = []
        for g in range(N_GROUPS):
            sl = slice(g * BLOCK, (g + 1) * BLOCK)
            w = jnp.where(c_ref[...] > 0, ws_ref[g], 0.0)
            z = _dot(w, vn[:, sl], "nn") + bs_ref[:, g:g + 1]
            outs.append(_gelu(u[:, sl]) * z)
        o_ref[...] = jnp.concatenate(outs, axis=-1).reshape(o_ref.shape)

    (out,), st = _call(
        "gmlp_fwd", body, (1, SEQ // BLOCK),
        [_tile_spec(0, d_b, col0), _tile_spec(0, d_b, col0 + 1), _fix_spec((1, d_b)), _fix_spec((1, d_b)),
         _fix_spec((N_GROUPS, BLOCK, BLOCK)), _fix_spec((BLOCK, N_GROUPS)), _fix_spec((BLOCK, BLOCK))],
        [_tile_spec(0, d_b, 0)], [jax.ShapeDtypeStruct(_view_shape(d_b, 0), F32)],
        [_tile_view(rest, 0), _tile_view(rest, 0), gain, bias, ws, bs, causal], sem=("parallel", "parallel"), stages=stages)
    return out.reshape(SEQ, d_b), st


def _gmlp_bwd(rest, col0, dgmlp, gain, bias, ws, bs, causal, stages=()):
    d_b = _d_b()
    nchunk = SEQ // BLOCK

    def body(u_ref, v_ref, dg_ref, g_ref, b_ref, ws_ref, bs_ref, c_ref, du_ref, dv_ref, dws_ref, dbs_ref, dgain_ref, dbias_ref):
        c = pl.program_id(1)

        @pl.when(c == 0)
        def _():
            dws_ref[...] = jnp.zeros_like(dws_ref)
            dbs_ref[...] = jnp.zeros_like(dbs_ref)
            dgain_ref[...] = jnp.zeros_like(dgain_ref)
            dbias_ref[...] = jnp.zeros_like(dbias_ref)

        u = u_ref[...].reshape(BLOCK, d_b)
        v = v_ref[...].reshape(BLOCK, d_b)
        dgm = dg_ref[...].reshape(BLOCK, d_b)
        xhat, rstd = _layer_norm_stats(_gelu(v))
        vn = (xhat * g_ref[...] + b_ref[...]).astype(BF16)
        lane = lax.broadcasted_iota(I32, (BLOCK, LANES), 1)
        dus, dvns = [], []
        dbs = dbs_ref[...]
        for g in range(N_GROUPS):
            sl = slice(g * BLOCK, (g + 1) * BLOCK)
            w = jnp.where(c_ref[...] > 0, ws_ref[g], 0.0).astype(BF16)
            z = _dot(w, vn[:, sl], "nn") + bs_ref[:, g:g + 1]
            dz = dgm[:, sl] * _gelu(u[:, sl])
            dus.append(dgm[:, sl] * z * _gelu_grad(u[:, sl]))
            dws_ref[g] += _dot(dz, vn[:, sl], "nt")
            dbs = dbs + jnp.where(lane == g, jnp.sum(dz, axis=-1, keepdims=True), 0.0)
            dvns.append(_dot(w, dz, "tn"))
        dbs_ref[...] = dbs
        dvn = jnp.concatenate(dvns, axis=-1)
        dgain_ref[...] += jnp.sum(dvn * xhat, axis=0, keepdims=True)
        dbias_ref[...] += jnp.sum(dvn, axis=0, keepdims=True)
        dvg = _layer_norm_bwd(dvn * g_ref[...], xhat, rstd)
        du_ref[...] = jnp.concatenate(dus, axis=-1).reshape(du_ref.shape)
        dv_ref[...] = (dvg * _gelu_grad(v)).reshape(dv_ref.shape)

        @pl.when(c == nchunk - 1)
        def _():
            for g in range(N_GROUPS):
                dws_ref[g] = jnp.where(c_ref[...] > 0, dws_ref[g], 0.0)

    (du, dv, dws, dbs, dgain, dbias), st = _call(
        "gmlp_bwd", body, (1, nchunk),
        [_tile_spec(0, d_b, col0), _tile_spec(0, d_b, col0 + 1), _tile_spec(0, d_b, 0), _fix_spec((1, d_b)), _fix_spec((1, d_b)),
         _fix_spec((N_GROUPS, BLOCK, BLOCK)), _fix_spec((BLOCK, N_GROUPS)), _fix_spec((BLOCK, BLOCK))],
        [_tile_spec(0, d_b, 0), _tile_spec(0, d_b, 0), _fix_spec((N_GROUPS, BLOCK, BLOCK)), _fix_spec((BLOCK, LANES)),
         _fix_spec((1, d_b)), _fix_spec((1, d_b))],
        [jax.ShapeDtypeStruct(_view_shape(d_b, 0), F32)] * 2
        + [jax.ShapeDtypeStruct((N_GROUPS, BLOCK, BLOCK), F32), jax.ShapeDtypeStruct((BLOCK, LANES), F32)]
        + [jax.ShapeDtypeStruct((1, d_b), F32)] * 2,
        [_tile_view(rest, 0), _tile_view(rest, 0), _tile_view(dgmlp, 0), gain, bias, ws, bs, causal], stages=stages)
    return (du.reshape(SEQ, d_b), dv.reshape(SEQ, d_b), dws, dbs, dgain, dbias), st


def _assemble_dproj(dqkv, du, dv, dga, dgb, tr=128, stages=()):
    d_a, d_b, d_in = _d_a(), _d_b(), _d_in()
    tr = _pick(tr, SEQ)

    def body(*refs):
        att, (du_ref, dv_ref, dga_ref, dgb_ref, o_ref) = refs[:9], refs[9:]
        for i in range(3):
            o_ref[:, i * d_a:(i + 1) * d_a] = (att[3 * i][...] + att[3 * i + 1][...] + att[3 * i + 2][...]).astype(BF16)
        o_ref[:, 3 * d_a:3 * d_a + d_b] = du_ref[...].astype(BF16)
        o_ref[:, 3 * d_a + d_b:3 * d_a + 2 * d_b] = dv_ref[...].astype(BF16)
        o_ref[:, 3 * d_a + 2 * d_b:3 * d_a + 2 * d_b + D_MODEL] = dga_ref[...]
        o_ref[:, 3 * d_a + 2 * d_b + D_MODEL:] = dgb_ref[...]

    return _call("assemble_dproj", body, (SEQ // tr,), [_row_spec(tr, d_a)] * 9 + [_row_spec(tr, d_b)] * 2 + [_row_spec(tr, D_MODEL)] * 2,
                 [_row_spec(tr, d_in)], [jax.ShapeDtypeStruct((SEQ, d_in), BF16)], [*dqkv, du, dv, dga, dgb], sem=("parallel",), stages=stages)


def _dw(name, a, b, kind, core, mine, add=None, tn=1152, half=None, stages=()):
    s, m = a.shape
    n = b.shape[1]
    if half is not None:
        m //= 2
    rs, cs = (m, n // N_DEV) if kind == "col" else (m // N_DEV, n)
    tn = _pick(tn if cs % tn == 0 else 512, cs)
    nj = cs // tn

    def shard(q, c_ref):
        return 2 * q + (c_ref[0] if mine else 1 - c_ref[0])

    if kind == "col":
        a_spec = pl.BlockSpec((s, m), lambda q, j, c_ref: (0, half or 0))
        b_spec = pl.BlockSpec((s, tn), lambda q, j, c_ref: (0, shard(q, c_ref) * nj + j))
    else:
        a_spec = pl.BlockSpec((s, rs), lambda q, j, c_ref: (0, shard(q, c_ref)))
        b_spec = pl.BlockSpec((s, tn), lambda q, j, c_ref: (0, j))
    o_spec = pl.BlockSpec((None, rs, tn), lambda q, j, c_ref: (q, 0, j))

    def body(a_ref, b_ref, *rest):
        acc = _dot(a_ref[...], b_ref[...], "tn")
        if add is not None:
            acc = acc + rest[0][...].astype(F32)
        rest[-1][...] = acc.astype(BF16)

    (out,), st = _call(name, body, (N_CHIPS, nj), [a_spec, b_spec] + ([o_spec] if add is not None else []), [o_spec],
                       [jax.ShapeDtypeStruct((N_CHIPS, rs, cs), BF16)], [a, b] + ([add] if add is not None else []),
                       sem=("parallel", "parallel"), stages=stages, prefetch=core)
    return out, st


def _adamw(w, g, m, v):
    m = ADAM_B1 * m + (1.0 - ADAM_B1) * g
    v = ADAM_B2 * v + (1.0 - ADAM_B2) * (g * g)
    m_hat = m / (1.0 - ADAM_B1 ** ADAM_STEP)
    v_hat = v / (1.0 - ADAM_B2 ** ADAM_STEP)
    delta = -ADAM_LR * (m_hat / (jnp.sqrt(v_hat) + ADAM_EPS) + ADAM_WD * w)
    return delta, m, v


def _adam_shard(name, pair, chip_sums, chip, w, m, v, tr=256, rows=None, into=None, stages=()):
    rs, cs = w.shape
    lo, cnt = rows or (0, rs)
    assert pair.shape[1] == cnt
    tr = _pick(tr, cnt, lo)
    first = lo // tr

    def body(chip_ref, own_ref, *refs):
        slots, (w_ref, m_ref, v_ref), (g_ref, d_ref, nm_ref, nv_ref) = refs[:N_CHIPS], refs[N_CHIPS:N_CHIPS + 3], refs[-4:]
        g = None
        for q in range(N_CHIPS):
            term = jnp.where(chip_ref[0] == q, own_ref[...], slots[q][...]).astype(F32)
            g = term if g is None else g + term
        d, nm, nv = _adamw(w_ref[...], g, m_ref[...], v_ref[...])
        g_ref[...], d_ref[...], nm_ref[...], nv_ref[...] = g, d, nm, nv

    def slot(q):
        return pl.BlockSpec((None, tr, cs), lambda i, c_ref: (jnp.where(c_ref[0] == q, (q + 1) % N_CHIPS, q), i, 0))

    spec = pl.BlockSpec((tr, cs), lambda i, c_ref: (i + first, 0))
    n_in = 1 + N_CHIPS + 3
    return _call(name, body, (cnt // tr,),
                 [pl.BlockSpec((None, tr, cs), lambda i, c_ref: (c_ref[0], i, 0))] + [slot(q) for q in range(N_CHIPS)]
                 + [spec, spec, spec] + ([ANY] * 4 if into is not None else []),
                 [spec] * 4, [jax.ShapeDtypeStruct((rs, cs), F32)] * 4, [pair] + [chip_sums] * N_CHIPS + [w, m, v] + list(into or ()),
                 sem=("parallel",), stages=stages, prefetch=chip, shown=True,
                 alias={n_in + k: k for k in range(4)} if into is not None else None)


def _adam_small(part, parts, me, w, m, v):
    rows = w.shape[0]

    def body(me_ref, own_ref, *refs):
        slots, (w_ref, m_ref, v_ref, g_ref, d_ref, nm_ref, nv_ref) = refs[:N_DEV], refs[N_DEV:]
        g = None
        for j in range(N_DEV):
            term = jnp.where(me_ref[0] == j, own_ref[...], slots[j][...])
            g = term if g is None else g + term
        d, nm, nv = _adamw(w_ref[...], g, m_ref[...], v_ref[...])
        g_ref[...], d_ref[...], nm_ref[...], nv_ref[...] = g, d, nm, nv

    def slot(j):
        return pl.BlockSpec((None, rows, LANES), lambda i, me_ref: (jnp.where(me_ref[0] == j, (j + 1) % N_DEV, j), 0, 0))

    spec = _fix_spec((rows, LANES))
    return _call("adam_small", body, (1,), [spec] + [slot(j) for j in range(N_DEV)] + [spec, spec, spec], [spec] * 4,
                 [jax.ShapeDtypeStruct((rows, LANES), F32)] * 4, [part] + [parts] * N_DEV + [w, m, v], prefetch=me, shown=True)[0]


def _small_sizes():
    d_b = _d_b()
    return (("loss", 1), ("rel_bias", N_BUCKETS * N_HEADS), ("ln_v_gain", d_b), ("ln_v_bias", d_b),
            ("w_spatial", N_GROUPS * BLOCK * BLOCK), ("b_spatial", N_GROUPS * BLOCK), ("ln1_gain", D_MODEL), ("ln1_bias", D_MODEL),
            ("b_ff1", D_FF), ("b_ff2", D_MODEL), ("ln2_gain", D_MODEL), ("ln2_bias", D_MODEL))


def _pack(vals):
    pieces = []
    for name, size in _small_sizes():
        flat = vals[name].reshape(-1).astype(F32)
        padded = -(-size // (SUBLANES * LANES)) * SUBLANES * LANES
        pieces.append(jnp.pad(flat, (0, padded - size)).reshape(-1, LANES))
    return jnp.concatenate(pieces, axis=0)


def _unpack(buf):
    out, row = {}, 0
    for name, size in _small_sizes():
        rows = -(-size // (SUBLANES * LANES)) * SUBLANES
        out[name] = buf[row:row + rows].reshape(-1)[:size]
        row += rows
    return out


def kernel(x, w_in, rel_bias, ln_v_gain, ln_v_bias, w_spatial, b_spatial, w_proj_a, w_proj_b, w_out, ln1_gain, ln1_bias, w_ff1, b_ff1, w_ff2, b_ff2, ln2_gain, ln2_bias, loss_target, m_w_in, m_rel_bias, m_ln_v_gain, m_ln_v_bias, m_w_spatial, m_b_spatial, m_w_proj_a, m_w_proj_b, m_w_out, m_ln1_gain, m_ln1_bias, m_w_ff1, m_b_ff1, m_w_ff2, m_b_ff2, m_ln2_gain, m_ln2_bias, v_w_in, v_rel_bias, v_ln_v_gain, v_ln_v_bias, v_w_spatial, v_b_spatial, v_w_proj_a, v_w_proj_b, v_w_out, v_ln1_gain, v_ln1_bias, v_w_ff1, v_b_ff1, v_w_ff2, v_b_ff2, v_ln2_gain, v_ln2_bias):
    d_a, d_b, d_in = _d_a(), _d_b(), _d_in()
    weights = dict(w_in=w_in, rel_bias=rel_bias, ln_v_gain=ln_v_gain, ln_v_bias=ln_v_bias, w_spatial=w_spatial, b_spatial=b_spatial,
                   w_proj_a=w_proj_a, w_proj_b=w_proj_b, w_out=w_out, ln1_gain=ln1_gain, ln1_bias=ln1_bias, w_ff1=w_ff1, b_ff1=b_ff1,
                   w_ff2=w_ff2, b_ff2=b_ff2, ln2_gain=ln2_gain, ln2_bias=ln2_bias)
    mom1 = dict(w_in=m_w_in, rel_bias=m_rel_bias, ln_v_gain=m_ln_v_gain, ln_v_bias=m_ln_v_bias, w_spatial=m_w_spatial,
                b_spatial=m_b_spatial, w_proj_a=m_w_proj_a, w_proj_b=m_w_proj_b, w_out=m_w_out, ln1_gain=m_ln1_gain,
                ln1_bias=m_ln1_bias, w_ff1=m_w_ff1, b_ff1=m_b_ff1, w_ff2=m_w_ff2, b_ff2=m_b_ff2, ln2_gain=m_ln2_gain, ln2_bias=m_ln2_bias)
    mom2 = dict(w_in=v_w_in, rel_bias=v_rel_bias, ln_v_gain=v_ln_v_gain, ln_v_bias=v_ln_v_bias, w_spatial=v_w_spatial,
                b_spatial=v_b_spatial, w_proj_a=v_w_proj_a, w_proj_b=v_w_proj_b, w_out=v_w_out, ln1_gain=v_ln1_gain,
                ln1_bias=v_ln1_bias, w_ff1=v_w_ff1, b_ff1=v_b_ff1, w_ff2=v_w_ff2, b_ff2=v_b_ff2, ln2_gain=v_ln2_gain, ln2_bias=v_ln2_bias)

    mx, my, mc = _coords()
    me = (4 * mx + 2 * my + mc).astype(I32).reshape(1)
    chip = (2 * mx + my).astype(I32).reshape(1)
    full = {n: _cast_into_place(f"cast_{n}", weights[n][0], KINDS[n], me)[0] for n in KINDS if n != "w_ff1"}
    sent = {n: (0, 0, 0) for n in KINDS}

    def keep(table, n):
        def store(outs):
            table[n] = outs[0]
        return store

    def gathering(**new):
        stages = []
        for n in KINDS:
            out, relayed, passed = sent[n]
            units = new.get(n, 0)
            if units or relayed < out or passed < relayed:
                st = _gather_stage(full[n], KINDS[n], (out, units) if units else None,
                                   (relayed, out - relayed) if relayed < out else None, (passed, relayed - passed) if passed < relayed else None)
                st.store = keep(full, n)
                sent[n] = (out + units, out, relayed)
                stages.append(st)
        return stages

    def settle(stages, outs):
        for st, o in zip(stages, outs):
            st.store(o)

    def alone(name, stages):
        settle(stages, _comm_only(name, stages))

    def here(n):
        assert sent[n] == (16, 16, 16), (n, sent[n])
        return full[n]

    hosted = gathering(w_in=16)
    full["w_ff1"], st = _cast_into_place("cast_w_ff1", weights["w_ff1"][0], KINDS["w_ff1"], me, stages=hosted)
    settle(hosted, st)
    hosted = gathering()
    xs, xb, target, st = _perm_in("cast_x", x[0], loss_target[0], stages=hosted)
    settle(hosted, st)
    g8 = BLOCK // N_SUB
    ws_t = w_spatial[0].reshape(N_GROUPS, g8, N_SUB, g8, N_SUB).transpose(0, 2, 1, 4, 3).reshape(N_GROUPS, BLOCK, BLOCK)
    bs_t = b_spatial[0].reshape(N_GROUPS, g8, N_SUB).transpose(2, 1, 0).reshape(BLOCK, N_GROUPS)
    idx = _local_index(0)
    causal = jnp.asarray((idx[:, None] >= idx[None, :]).astype(np.float32))
    buckets = jnp.asarray(_bucket_tables())
    hosted = gathering(w_proj_a=16, w_proj_b=16)
    (bias,), st = _bias_expand(rel_bias, buckets, stages=hosted)
    settle(hosted, st)

    hosted = gathering(w_out=8, w_ff1=1)
    (qkv,), st = _matmul("proj_qkv", xb, here("w_in"), "nn", [F32], n=3 * d_a, stages=hosted)
    settle(hosted, st)
    hosted = gathering(w_out=8, w_ff1=6)
    (rest,), st = _matmul("proj_rest", xb, here("w_in"), "nn", [F32], b_off=3 * d_a, n=d_in - 3 * d_a, stages=hosted)
    settle(hosted, st)
    fwd = []
    for p in range(3):
        hosted = gathering(**({"w_ff1": 2}, {"w_ff1": 5}, {"w_ff1": 2, "w_ff2": 1})[p])
        res, st = _attn_fwd(qkv, bias, p, stages=hosted)
        settle(hosted, st)
        fwd.append(res)
    hosted = gathering(w_ff2=1)
    (attn, attn_b, lse), st = _attn_combine([o for o, _ in fwd], [l for _, l in fwd], stages=hosted)
    settle(hosted, st)
    hosted = gathering(w_ff2=2)
    gmlp, st = _gmlp_fwd(rest, 0, ln_v_gain, ln_v_bias, ws_t, bs_t, causal, stages=hosted)
    settle(hosted, st)
    hosted = gathering(w_ff2=3)
    (ya,), st = _matmul("proj_a", attn_b, here("w_proj_a"), "nn", [BF16], stages=hosted)
    settle(hosted, st)
    gate_a, gate_b = 2 * d_b, 2 * d_b + D_MODEL

    def merge(acc, ya_, ga, gb):
        return acc, _sigmoid(ga) * ya_ + _sigmoid(gb) * acc

    hosted = gathering(w_ff2=5)
    (yb, merged), st = _matmul("proj_b_merge", gmlp, here("w_proj_b"), "nn", [BF16, BF16], merge,
                               [(ya, "mn", 0), (rest, "mn", gate_a), (rest, "mn", gate_b)], tn=256, stages=hosted)
    settle(hosted, st)
    hosted = gathering(w_ff2=3)
    (pre1,), st = _matmul("out_proj", merged, here("w_out"), "nn", [F32], lambda acc, x_: (ALPHA * x_ + acc,), [(xs, "mn", 0)], stages=hosted)
    settle(hosted, st)
    hosted = gathering(w_ff2=1)
    (xhat1, rstd1, h1b), st = _ln1_fwd(pre1, ln1_gain, ln1_bias, stages=hosted)
    settle(hosted, st)

    def relu2(acc, b_):
        r = jnp.maximum(acc + b_, 0.0)
        return r, r * r

    hosted = gathering()
    (relu, fb), st = _matmul("ff1", h1b, here("w_ff1"), "nn", [BF16, BF16], relu2, [(b_ff1, "row", 0)], stages=hosted)
    settle(hosted, st)
    alone("gather_w_ff2_sibling", gathering())
    (ff,), _ = _matmul("ff2", fb, here("w_ff2"), "nn", [F32], lambda acc, b_: (acc + b_,), [(b_ff2, "row", 0)], tn=512, tk=2048)

    core = lax.axis_index("c").astype(I32).reshape(1)
    factors, theirs, sib, pair, chips, reduced = {}, {}, {}, {}, {}, {}

    def grad_for_sibling(n, a, b, stages=(), half=None, of=None):
        factors[n] = (a, b, KINDS[of or n], half)
        theirs[n], outs = _dw(f"dw_{n}_sibling", a, b, KINDS[of or n], core, False, half=half, stages=stages)
        settle(stages, outs)

    def to_sibling(n):
        st = _to_sibling_stage(theirs[n])
        st.store = keep(sib, n)
        return st

    def grad_own(n, stages=()):
        a, b, kind, half = factors[n]
        pair[n], outs = _dw(f"dw_{n}_own", a, b, kind, core, True, add=sib[n], half=half, stages=stages)
        settle(stages, outs)
        chips[n] = lax.empty(pair[n].shape, BF16)
        reduced[n] = 0

    def reducing(**new):
        stages = []
        for n, units in new.items():
            st = _to_chips_stage(pair[n], chips[n], (reduced[n], units))
            st.store = keep(chips, n)
            reduced[n] += units
            stages.append(st)
        return stages

    def summed(n):
        assert reduced[n] == 16, (n, reduced[n])
        return chips[n]

    dpre2, dpre2b, g_ln2_gain, g_ln2_bias, g_b_ff2, loss_part = _ln2_loss_bwd(ff, xhat1, ln1_gain, ln1_bias, ln2_gain, ln2_bias, target)
    grad_for_sibling("w_ff2", fb, dpre2b)

    def relu2_bwd(acc, r):
        da = acc * (2.0 * r)
        return da, da

    hosted = [to_sibling("w_ff2")]
    (dab, g_b_ff1), st = _matmul("d_ff1", dpre2b, here("w_ff2"), "nt", [BF16], relu2_bwd, [(relu, "mn", 0)], colsums=(1,), stages=hosted)
    settle(hosted, st)
    grad_own("w_ff2")
    grad_for_sibling("w_ff1", h1b, dab, reducing(w_ff2=3))
    hosted = reducing(w_ff2=8) + [to_sibling("w_ff1")]
    (dh1,), st = _matmul("d_h1", dab, here("w_ff1"), "nt", [F32], lambda acc, d_: (acc + ALPHA * d_,), [(dpre2, "mn", 0)], stages=hosted)
    settle(hosted, st)
    grad_own("w_ff1", reducing(w_ff2=4))
    hosted = reducing(w_ff2=1)
    (dpre1, dpre1b, g_ln1_gain, g_ln1_bias), st = _ln1_bwd(dh1, xhat1, rstd1, ln1_gain, stages=hosted)
    settle(hosted, st)
    grad_for_sibling("w_out", merged, dpre1b, reducing(w_ff1=1))

    def merge_bwd(acc, ga, gb, ya_, yb_):
        sa, sb = _sigmoid(ga), _sigmoid(gb)
        return acc * sa, acc * sb, acc * ya_ * (sa * (1.0 - sa)), acc * yb_ * (sb * (1.0 - sb))

    hosted = reducing(w_ff1=6) + [to_sibling("w_out")]
    (dya, dyb, dga, dgb), st = _matmul("d_merge", dpre1b, here("w_out"), "nt", [BF16] * 4, merge_bwd,
                                       [(rest, "mn", gate_a), (rest, "mn", gate_b), (ya, "mn", 0), (yb, "mn", 0)], tn=256, stages=hosted)
    settle(hosted, st)
    grad_own("w_out", reducing(w_ff1=1))
    grad_for_sibling("w_proj_a", attn_b, dya)
    grad_for_sibling("w_proj_b", gmlp, dyb)
    hosted = reducing(w_ff1=1) + [to_sibling("w_proj_a"), to_sibling("w_proj_b")]
    (dattn,), st = _matmul("d_attn", dya, here("w_proj_a"), "nt", [F32], stages=hosted)
    settle(hosted, st)
    grad_own("w_proj_a")
    grad_own("w_proj_b")
    hosted = reducing(w_ff1=1)
    (dgmlp,), st = _matmul("d_gmlp", dyb, here("w_proj_b"), "nt", [F32], stages=hosted)
    settle(hosted, st)
    hosted = reducing(w_ff1=2)
    (du, dvb, dws_t, dbs_t, g_lnv_gain, g_lnv_bias), st = _gmlp_bwd(rest, 0, dgmlp, ln_v_gain, ln_v_bias, ws_t, bs_t, causal, stages=hosted)
    settle(hosted, st)
    delta = _attn_delta(dattn, attn)
    bwd = []
    for p in range(3):
        hosted = reducing(**({"w_ff1": 3}, {"w_ff1": 1, "w_out": 8}, {"w_out": 8, "w_proj_a": 8})[p])
        res, st = _attn_bwd(qkv, dattn, lse, delta, bias, p, stages=hosted)
        settle(hosted, st)
        bwd.append(res)
    g_rel_bias = _rel_bias_grad([b[3] for b in bwd], buckets)
    hosted = reducing(w_proj_a=8, w_proj_b=16)
    (dproj,), st = _assemble_dproj([b[i] for i in range(3) for b in bwd], du, dvb, dga, dgb, stages=hosted)
    settle(hosted, st)

    g_w_spatial = dws_t.reshape(N_GROUPS, N_SUB, g8, N_SUB, g8).transpose(0, 2, 1, 4, 3)
    g_b_spatial = dbs_t[:, :N_GROUPS].reshape(N_SUB, g8, N_GROUPS).transpose(2, 1, 0)
    part = _pack(dict(loss=loss_part, rel_bias=g_rel_bias, ln_v_gain=g_lnv_gain, ln_v_bias=g_lnv_bias, w_spatial=g_w_spatial,
                      b_spatial=g_b_spatial, ln1_gain=g_ln1_gain, ln1_bias=g_ln1_bias, b_ff1=g_b_ff1, b_ff2=g_b_ff2,
                      ln2_gain=g_ln2_gain, ln2_bias=g_ln2_bias))
    sib["small"] = lax.empty((N_DEV, *part.shape), F32)
    cut = part.shape[0] // 2 // SUBLANES * SUBLANES

    def small(rows):
        st = _small_stage(part, sib["small"], rows)
        st.store = keep(sib, "small")
        return st

    grad_for_sibling("w_in_top", xb, dproj, [small((0, cut))], half=0, of="w_in")
    grad_for_sibling("w_in_bot", xb, dproj, [small((cut, part.shape[0] - cut)), to_sibling("w_in_top")], half=1, of="w_in")
    parts = sib["small"]
    grad_own("w_in_top", [to_sibling("w_in_bot")])
    grad_own("w_in_bot", reducing(w_in_top=4))

    def add_residual(acc, d_):
        return (acc + ALPHA * d_,)

    hosted = reducing(w_in_top=12, w_in_bot=7)
    (dx,), st = _matmul("d_x", dproj, here("w_in"), "nt", [F32], add_residual, [(dpre1, "mn", 0)], tn=512, tk=3072, stages=hosted)
    settle(hosted, st)
    grad_x = _from_perm(dx)[None]

    out_g, out_d, out_m, out_v = {}, {}, {}, {}
    for n, units in (("w_out", 3), ("w_proj_a", 3), ("w_proj_b", 3), ("w_ff2", 0), ("w_ff1", 0)):
        hosted = reducing(w_in_bot=units) if units else []
        (g, d, nm, nv), st = _adam_shard(f"adam_{n}", pair[n], summed(n), chip, weights[n][0], mom1[n][0], mom2[n][0], stages=hosted)
        settle(hosted, st)
        out_g[n], out_d[n], out_m[n], out_v[n] = g[None], d[None], nm[None], nv[None]
    rows = weights["w_in"].shape[1] // 2
    done = None
    for i, n in enumerate(("w_in_top", "w_in_bot")):
        done, _ = _adam_shard(f"adam_{n}", pair[n], summed(n), chip, weights["w_in"][0], mom1["w_in"][0], mom2["w_in"][0],
                              rows=(i * rows, rows), into=done)
    out_g["w_in"], out_d["w_in"], out_m["w_in"], out_v["w_in"] = (t[None] for t in done)

    zero = jnp.zeros((1,), F32)
    sg, sd, sm, sv = (_unpack(b) for b in _adam_small(
        part, parts, me, _pack({**weights, "loss": zero}), _pack({**mom1, "loss": zero}), _pack({**mom2, "loss": zero})))
    for n in WEIGHT_ORDER:
        if n not in KINDS:
            shape = weights[n].shape
            out_g[n], out_d[n], out_m[n], out_v[n] = (t[n].reshape(shape) for t in (sg, sd, sm, sv))
    loss = sg["loss"].reshape(())
    return (loss, grad_x, *[out_g[n] for n in WEIGHT_ORDER], *[out_d[n] for n in WEIGHT_ORDER],
            *[out_m[n] for n in WEIGHT_ORDER], *[out_v[n] for n in WEIGHT_ORDER])
```

```python
import math

import jax
import jax.numpy as jnp
import numpy as np
from jax import lax
from jax.experimental import pallas as pl
from jax.experimental.pallas import tpu as pltpu

F32 = jnp.float32
BF16 = jnp.bfloat16
I32 = jnp.int32

SEQ = 2048
D_MODEL = 2048
HEAD_DIM = 128
N_HEADS = 8
N_GROUPS = 8
D_FF = 4 * D_MODEL
BLOCK = 128
DILATIONS = (1, 4, 16)
N_BUCKETS = 32
MAX_DISTANCE = 2048
ALPHA = 2.0 ** 0.25
LN_EPS = 1e-5
NEG_INF = -1e30
N_DEV = 8
N_CHIPS = 4
N_SUB = 16
ADAM_LR, ADAM_B1, ADAM_B2, ADAM_EPS, ADAM_WD, ADAM_STEP = 0.001, 0.9, 0.999, 1e-08, 0.01, 10
LANES = 128
SUBLANES = 8
VMEM_LIMIT = 56 * 1024 * 1024
MESH = pl.DeviceIdType.MESH
ANY = pl.BlockSpec(memory_space=pl.ANY)
WEIGHT_ORDER = ("w_in", "rel_bias", "ln_v_gain", "ln_v_bias", "w_spatial", "b_spatial", "w_proj_a", "w_proj_b", "w_out",
                "ln1_gain", "ln1_bias", "w_ff1", "b_ff1", "w_ff2", "b_ff2", "ln2_gain", "ln2_bias")
KINDS = {"w_in": "col", "w_proj_a": "col", "w_proj_b": "col", "w_out": "row", "w_ff1": "col", "w_ff2": "row"}


def _d_a():
    return N_HEADS * HEAD_DIM


def _d_b():
    return N_GROUPS * BLOCK


def _d_in():
    return 3 * _d_a() + 2 * _d_b() + 2 * D_MODEL


def _pick(t, n, *others):
    if n <= t and all(o % n == 0 for o in others):
        return n
    for c in range(min(t, n) // LANES * LANES, 0, -LANES):
        if n % c == 0 and all(o % c == 0 for o in others):
            return c
    raise ValueError((t, n, others))


class _Stage:
    def __init__(self, ins, outs, alias, sems, start, finish):
        self.ins, self.outs, self.alias, self.sems, self.start, self.finish = ins, outs, alias, sems, start, finish


def _call(name, body, grid, in_specs, out_specs, out_shape, operands, scratch=(), sem=None, stages=(), sequential=False, prefetch=None,
          shown=False, alias=None):
    n_in, n_out, n_sc = len(in_specs), len(out_specs), len(scratch)
    st_in = [len(s.ins) for s in stages]
    st_out = [len(s.outs) for s in stages]
    st_sem = [len(s.sems) for s in stages]
    n_pre = 0 if prefetch is None else 1
    aliases, ioff, ooff = {i + n_pre: o for i, o in (alias or {}).items()}, n_in + n_pre, n_out
    for s, ni, no in zip(stages, st_in, st_out):
        for i, o in s.alias.items():
            aliases[ioff + i] = ooff + o
        ioff, ooff = ioff + ni, ooff + no

    def split(refs, counts):
        out, at = [], 0
        for c in counts:
            out.append(refs[at:at + c])
            at += c
        return out

    def wrapped(*refs):
        ins, sins, outs, souts, sc, ssems = split(refs[n_pre:], [n_in, sum(st_in), n_out, sum(st_out), n_sc, sum(st_sem)])
        parts = list(zip(stages, split(sins, st_in), split(souts, st_out), split(ssems, st_sem)))
        if sequential:
            for s, a, b, c in parts:
                s.start(a, b, c)
            for s, a, b, c in parts:
                s.finish(a, b, c)
            return
        if parts:
            first = _all_of([pl.program_id(i) == 0 for i in range(len(grid))])
            last = _all_of([pl.program_id(i) == g - 1 for i, g in enumerate(grid)])

            @pl.when(first)
            def _():
                for s, a, b, c in parts:
                    s.start(a, b, c)

        body(*(refs[:n_pre] if shown else ()), *ins, *outs, *sc)
        if parts:
            @pl.when(last)
            def _():
                for s, a, b, c in parts:
                    s.finish(a, b, c)

    if stages or sem is None:
        sem = ("arbitrary",) * len(grid)
    specs = dict(grid=grid, in_specs=list(in_specs) + [ANY] * sum(st_in), out_specs=list(out_specs) + [ANY] * sum(st_out),
                 scratch_shapes=list(scratch) + [x for s in stages for x in s.sems])
    if prefetch is not None:
        specs = dict(grid_spec=pltpu.PrefetchScalarGridSpec(num_scalar_prefetch=1, **specs))
    res = pl.pallas_call(
        wrapped, name=name, out_shape=list(out_shape) + [o for s in stages for o in s.outs], input_output_aliases=aliases,
        compiler_params=pltpu.CompilerParams(dimension_semantics=sem, vmem_limit_bytes=VMEM_LIMIT), **specs,
    )(*([prefetch] if n_pre else []), *operands, *[a for s in stages for a in s.ins])
    res = list(res)
    return res[:n_out], split(res[n_out:], st_out)


def _all_of(conds):
    out = conds[0]
    for c in conds[1:]:
        out = out & c
    return out


def _coords():
    return lax.axis_index("x"), lax.axis_index("y"), lax.axis_index("c")


def _other_chips(x, y):
    return ((1 - x, y), (x, 1 - y), (1 - x, 1 - y))


def _lin(dev):
    return 4 * dev[0] + 2 * dev[1] + dev[2]


def _piece(total, lo, n, units=16):
    assert total % units == 0
    return lo * (total // units), n * (total // units)


FLOWS = 4


def _split(lo, cnt):
    k = next(k for k in (FLOWS, 2, 1) if cnt % (2 * SUBLANES * k) == 0)
    return [(lo + i * (cnt // k), cnt // k) for i in range(k)]


def _remote(src, dst, send, recv, to):
    return pltpu.make_async_remote_copy(src_ref=src, dst_ref=dst, send_sem=send, recv_sem=recv, device_id=to, device_id_type=MESH)


def _placer(kind, n, lo, cnt):
    def place(ref, dev):
        if kind == "col":
            return ref.at[pl.ds(lo, cnt), pl.ds(pl.multiple_of(_lin(dev) * n, LANES), n)]
        return ref.at[pl.ds(pl.multiple_of(_lin(dev) * n + lo, 2 * SUBLANES), cnt), :]
    return place


def _spread_stage(full, kind, piece=(0, 16), home=False):
    n = (full.shape[1] if kind == "col" else full.shape[0]) // N_DEV
    lo, cnt = _piece(full.shape[0] if kind == "col" else n, *piece)
    parts = _split(lo, cnt)
    npeers = 1 if home else 2

    def copies(ins, outs, sems):
        send, recv = sems
        x, y, c = _coords()
        me = (x, y, c)
        peers = [(x, y, 1 - c)] if home else [(1 - x, y, c), (x, 1 - y, c)]
        out, arrive = [], []
        for k, t in enumerate(peers):
            for i, (plo, pcnt) in enumerate(parts):
                place = _placer(kind, n, plo, pcnt)
                out.append(_remote(place(outs[0], me), place(outs[0], me), send.at[i, k], recv.at[i, k], t))
                arrive.append(_remote(place(outs[0], t), place(outs[0], t), send.at[i, k], recv.at[i, k], t))
        return out, arrive

    def start(ins, outs, sems):
        for cp in copies(ins, outs, sems)[0]:
            cp.start()

    def finish(ins, outs, sems):
        out, arrive = copies(ins, outs, sems)
        for cp in arrive:
            cp.wait_recv()
        for cp in out:
            cp.wait_send()

    return _Stage([full], [jax.ShapeDtypeStruct(full.shape, full.dtype)], {0: 0},
                  [pltpu.SemaphoreType.DMA((len(parts), npeers)), pltpu.SemaphoreType.DMA((len(parts), npeers))], start, finish)


def _relay_stage(full, kind, piece=(0, 16)):
    n = (full.shape[1] if kind == "col" else full.shape[0]) // N_DEV
    lo, cnt = _piece(full.shape[0] if kind == "col" else n, *piece)
    half = cnt // 2
    assert half % (2 * SUBLANES) == 0, (cnt, kind)
    tops, bottoms = _split(lo, half), _split(lo + half, half)

    def copies(ins, outs, sems):
        send, recv = sems
        x, y, c = _coords()
        xn, yn, dg = (1 - x, y, c), (x, 1 - y, c), (1 - x, 1 - y, c)
        out, arrive, k = [], [], 0
        for came_from, to, parts in ((yn, xn, tops), (xn, yn, bottoms)):
            for plo, pcnt in parts:
                place = _placer(kind, n, plo, pcnt)
                out.append(_remote(place(outs[0], came_from), place(outs[0], came_from), send.at[k], recv.at[k], to))
                arrive.append(_remote(place(outs[0], dg), place(outs[0], dg), send.at[k], recv.at[k], to))
                k += 1
        return out, arrive

    def start(ins, outs, sems):
        for cp in copies(ins, outs, sems)[0]:
            cp.start()

    def finish(ins, outs, sems):
        out, arrive = copies(ins, outs, sems)
        for cp in arrive:
            cp.wait_recv()
        for cp in out:
            cp.wait_send()

    return _Stage([full], [jax.ShapeDtypeStruct(full.shape, full.dtype)], {0: 0},
                  [pltpu.SemaphoreType.DMA((len(tops) + len(bottoms),)), pltpu.SemaphoreType.DMA((len(tops) + len(bottoms),))], start, finish)


def _forward_stage(full, kind, piece, far):
    n = (full.shape[1] if kind == "col" else full.shape[0]) // N_DEV
    lo, cnt = _piece(full.shape[0] if kind == "col" else n, *piece)
    place = _placer(kind, n, lo, cnt)

    def copies(ins, outs, sems):
        send, recv = sems
        x, y, c = _coords()
        chips = _other_chips(x, y)[2:] if far else _other_chips(x, y)[:2]
        out = [_remote(place(outs[0], (*chip, c)), place(outs[0], (*chip, c)), send.at[k], recv.at[k], (x, y, 1 - c)) for k, chip in enumerate(chips)]
        arrive = [_remote(place(outs[0], (*chip, 1 - c)), place(outs[0], (*chip, 1 - c)), send.at[k], recv.at[k], (x, y, 1 - c))
                  for k, chip in enumerate(chips)]
        return out, arrive

    def start(ins, outs, sems):
        for cp in copies(ins, outs, sems)[0]:
            cp.start()

    def finish(ins, outs, sems):
        out, arrive = copies(ins, outs, sems)
        for cp in arrive:
            cp.wait_recv()
        for cp in out:
            cp.wait_send()

    return _Stage([full], [jax.ShapeDtypeStruct(full.shape, full.dtype)], {0: 0},
                  [pltpu.SemaphoreType.DMA((3,)), pltpu.SemaphoreType.DMA((3,))], start, finish)


def _to_sibling_stage(theirs):
    def copies(ins, outs, sems):
        send, recv = sems
        x, y, c = _coords()
        return [_remote(ins[0].at[q], outs[0].at[q], send.at[q], recv.at[q], (x, y, 1 - c)) for q in range(N_CHIPS)]

    def start(ins, outs, sems):
        for cp in copies(ins, outs, sems):
            cp.start()

    def finish(ins, outs, sems):
        for cp in copies(ins, outs, sems):
            cp.wait()

    return _Stage([theirs], [jax.ShapeDtypeStruct(theirs.shape, BF16)], {},
                  [pltpu.SemaphoreType.DMA((N_CHIPS,)), pltpu.SemaphoreType.DMA((N_CHIPS,))], start, finish)


def _to_chips_stage(pair, dst, piece=(0, 16)):
    lo, cnt = _piece(pair.shape[1], *piece)
    parts = _split(lo, cnt)
    nsem = 3 * len(parts)

    def copies(ins, outs, sems):
        send, recv = sems
        x, y, c = _coords()
        mine = 2 * x + y
        out, arrive, k = [], [], 0
        for px, py in _other_chips(x, y):
            for plo, pcnt in parts:
                rows = pl.ds(plo, pcnt)
                out.append(_remote(ins[0].at[2 * px + py, rows, :], outs[0].at[mine, rows, :], send.at[k], recv.at[k], (px, py, c)))
                arrive.append(_remote(ins[0].at[2 * px + py, rows, :], outs[0].at[2 * px + py, rows, :], send.at[k], recv.at[k], (px, py, c)))
                k += 1
        return out, arrive

    def start(ins, outs, sems):
        for cp in copies(ins, outs, sems)[0]:
            cp.start()

    def finish(ins, outs, sems):
        out, arrive = copies(ins, outs, sems)
        for cp in arrive:
            cp.wait_recv()
        for cp in out:
            cp.wait_send()

    return _Stage([pair, dst], [jax.ShapeDtypeStruct(dst.shape, dst.dtype)], {1: 0},
                  [pltpu.SemaphoreType.DMA((nsem,)), pltpu.SemaphoreType.DMA((nsem,))], start, finish)


def _fuse(parts, ins, outs, alias):
    parts = [p for p in parts if p is not None]
    sems = [x for st, _, _ in parts for x in st.sems]

    def run(which):
        def go(i, o, s):
            refs, at = list(i) + list(o), 0
            for st, pi, po in parts:
                getattr(st, which)([refs[k] for k in pi], [refs[k] for k in po], s[at:at + len(st.sems)])
                at += len(st.sems)
        return go

    return _Stage(ins, [jax.ShapeDtypeStruct(o.shape, o.dtype) for o in outs], alias, sems, run("start"), run("finish"))


def _gather_stage(full, kind, new=None, relay=None, forward=None):
    return _fuse([(_spread_stage(full, kind, new), [0], [1]) if new else None,
                  (_relay_stage(full, kind, relay), [0], [1]) if relay else None,
                  (_spread_stage(full, kind, relay, home=True), [0], [1]) if relay else None,
                  (_forward_stage(full, kind, relay, False), [0], [1]) if relay else None,
                  (_forward_stage(full, kind, forward, True), [0], [1]) if forward else None], [full], [full], {0: 0})


def _small_stage(part, dst, rows):
    piece = pl.ds(*rows)

    def copies(ins, outs, sems):
        send, recv = sems
        x, y, c = _coords()
        me = (x, y, c)
        peers = [(1 - x if k & 4 else x, 1 - y if k & 2 else y, 1 - c if k & 1 else c) for k in range(1, N_DEV)]
        out = [_remote(ins[0].at[piece, :], outs[0].at[_lin(me), piece, :], send.at[k], recv.at[k], t) for k, t in enumerate(peers)]
        arrive = [_remote(ins[0].at[piece, :], outs[0].at[_lin(t), piece, :], send.at[k], recv.at[k], t) for k, t in enumerate(peers)]
        return out, arrive

    def start(ins, outs, sems):
        for cp in copies(ins, outs, sems)[0]:
            cp.start()

    def finish(ins, outs, sems):
        out, arrive = copies(ins, outs, sems)
        for cp in arrive:
            cp.wait_recv()
        for cp in out:
            cp.wait_send()

    return _Stage([part, dst], [jax.ShapeDtypeStruct(dst.shape, F32)], {1: 0},
                  [pltpu.SemaphoreType.DMA((N_DEV - 1,)), pltpu.SemaphoreType.DMA((N_DEV - 1,))], start, finish)


def _comm_only(name, stages):
    return _call(name, lambda: None, (1,), [], [], [], [], stages=stages, sequential=True)[1]


_GELU_C = math.sqrt(2.0 / math.pi)


def _gelu(x):
    return 0.5 * x * (1.0 + jnp.tanh(_GELU_C * (x + 0.044715 * x * x * x)))


def _gelu_grad(x):
    t = jnp.tanh(_GELU_C * (x + 0.044715 * x * x * x))
    return 0.5 * (1.0 + t) + 0.5 * x * (1.0 - t * t) * (_GELU_C * (1.0 + 3.0 * 0.044715 * x * x))


def _sigmoid(x):
    return 1.0 / (1.0 + jnp.exp(-x))


def _dot(a, b, mode):
    dims = {"nn": (((1,), (0,)), ((), ())), "nt": (((1,), (1,)), ((), ())), "tn": (((0,), (0,)), ((), ()))}[mode]
    return lax.dot_general(a.astype(BF16), b.astype(BF16), dims, preferred_element_type=F32)


def _matmul(name, a, b, mode, outs, epi=None, extras=(), colsums=(), tm=2048, tn=512, tk=2048, b_off=0, n=None, m_off=0, m=None, stages=()):
    if mode == "tn":
        kk, mfull = a.shape
    else:
        mfull, kk = a.shape
    m = mfull if m is None else m
    n = (b.shape[0] if mode == "nt" else b.shape[1]) if n is None else n
    tm, tk = _pick(tm, m, m_off), _pick(tk, kk)
    tn = _pick(tn, n, b_off, *[off for _, _, off in extras])
    boff, moff = b_off // tn, m_off // tm
    nm, nn_, nk = m // tm, n // tn, kk // tk
    col_major = bool(colsums)
    grid = (nn_, nm, nk) if col_major else (nm, nn_, nk)

    def imap(f):
        if col_major:
            return lambda g0, g1, k: f(g1, g0, k)
        return f

    a_spec = (pl.BlockSpec((tk, tm), imap(lambda i, j, k: (k, i + moff))) if mode == "tn"
              else pl.BlockSpec((tm, tk), imap(lambda i, j, k: (i + moff, k))))
    b_spec = (pl.BlockSpec((tn, tk), imap(lambda i, j, k: (j + boff, k))) if mode == "nt"
              else pl.BlockSpec((tk, tn), imap(lambda i, j, k: (k, j + boff))))
    in_specs, operands = [a_spec, b_spec], [a, b]
    for arr, kind, off in extras:
        o = off // tn
        if kind == "mn":
            in_specs.append(pl.BlockSpec((tm, tn), imap(lambda i, j, k, o=o: (i + moff, j + o))))
        else:
            in_specs.append(pl.BlockSpec((1, tn), imap(lambda i, j, k, o=o: (0, j + o))))
        operands.append(arr)
    out_shape = [jax.ShapeDtypeStruct((m, n), dt) for dt in outs] + [jax.ShapeDtypeStruct((1, n), F32) for _ in colsums]
    out_specs = ([pl.BlockSpec((tm, tn), imap(lambda i, j, k: (i, j))) for _ in outs]
                 + [pl.BlockSpec((1, tn), imap(lambda i, j, k: (0, j))) for _ in colsums])
    n_ex, n_out, n_cs = len(extras), len(outs), len(colsums)

    def body(*refs):
        a_ref, b_ref = refs[:2]
        ex_refs = refs[2:2 + n_ex]
        out_refs = refs[2 + n_ex:2 + n_ex + n_out]
        cs_refs = refs[2 + n_ex + n_out:2 + n_ex + n_out + n_cs]
        part = _dot(a_ref[...], b_ref[...], mode)

        def finish(acc):
            res = epi(acc, *[r[...] for r in ex_refs]) if epi is not None else (acc,)
            for r, v in zip(out_refs, res[:n_out]):
                r[...] = v.astype(r.dtype)
            if n_cs:
                @pl.when(pl.program_id(1) == 0)
                def _():
                    for r in cs_refs:
                        r[...] = jnp.zeros_like(r)

                for r, idx in zip(cs_refs, colsums):
                    r[...] += jnp.sum(res[idx], axis=0, keepdims=True)

        if nk == 1:
            finish(part)
        else:
            acc_ref = refs[-1]
            k = pl.program_id(2)

            @pl.when(k == 0)
            def _():
                acc_ref[...] = part

            @pl.when(k > 0)
            def _():
                acc_ref[...] += part

            @pl.when(k == nk - 1)
            def _():
                finish(acc_ref[...])

    sem = ("arbitrary", "arbitrary", "arbitrary") if col_major else ("parallel", "parallel", "arbitrary")
    return _call(name, body, grid, in_specs, out_specs, out_shape, operands,
                 scratch=[pltpu.VMEM((tm, tn), F32)] if nk > 1 else [], sem=sem, stages=stages)


def _row_spec(tr, c):
    return pl.BlockSpec((tr, c), lambda i: (i, 0))


def _fix_spec(shape):
    return pl.BlockSpec(shape, lambda *_: tuple(0 for _ in shape))


def _perm_in(name, x, target, tc=LANES, stages=()):
    s, d = x.shape
    tc = _pick(tc, d)
    rows = s // N_SUB

    def body(x_ref, t_ref, xs_ref, xb_ref, ts_ref):
        for r in range(N_SUB):
            piece = pl.ds(r * rows, rows)
            v = x_ref[pl.ds(r, rows, stride=N_SUB), :]
            xs_ref[piece, :] = v
            xb_ref[piece, :] = v.astype(BF16)
            ts_ref[piece, :] = t_ref[pl.ds(r, rows, stride=N_SUB), :]

    col = pl.BlockSpec((s, tc), lambda j: (0, j))
    (xs, xb, ts), st = _call(name, body, (d // tc,), [col, col], [col, col, col],
                             [jax.ShapeDtypeStruct((s, d), F32), jax.ShapeDtypeStruct((s, d), BF16), jax.ShapeDtypeStruct((s, d), F32)],
                             [x, target], sem=("parallel",), stages=stages)
    return xs, xb, ts, st


def _cast_into_place(name, w, kind, me, tr=512, stages=()):
    r, c = w.shape
    tr = _pick(tr, r)
    nb = r // tr
    if kind == "col":
        o_spec = pl.BlockSpec((tr, c), lambda i, me_ref: (i, me_ref[0]))
        shape = (r, c * N_DEV)
    else:
        o_spec = pl.BlockSpec((tr, c), lambda i, me_ref: (me_ref[0] * nb + i, 0))
        shape = (r * N_DEV, c)

    def body(x_ref, o_ref):
        o_ref[...] = x_ref[...].astype(BF16)

    (out,), st = _call(name, body, (nb,), [pl.BlockSpec((tr, c), lambda i, me_ref: (i, 0))], [o_spec], [jax.ShapeDtypeStruct(shape, BF16)],
                       [w], sem=("parallel",), prefetch=me, stages=stages)
    return out, st


def _layer_norm_stats(x):
    mean = jnp.mean(x, axis=-1, keepdims=True)
    xc = x - mean
    var = jnp.mean(xc * xc, axis=-1, keepdims=True)
    rstd = lax.rsqrt(var + LN_EPS)
    return xc * rstd, rstd


def _layer_norm_bwd(dxhat, xhat, rstd):
    m1 = jnp.mean(dxhat, axis=-1, keepdims=True)
    m2 = jnp.mean(dxhat * xhat, axis=-1, keepdims=True)
    return rstd * (dxhat - m1 - xhat * m2)


def _ln1_fwd(pre1, g1, b1, tr=256, stages=()):
    s, d = pre1.shape
    tr = _pick(tr, s)

    def body(p_ref, g_ref, b_ref, xh_ref, rs_ref, h_ref):
        xhat, rstd = _layer_norm_stats(p_ref[...])
        xh_ref[...] = xhat
        rs_ref[...] = rstd
        h_ref[...] = (xhat * g_ref[...] + b_ref[...]).astype(BF16)

    return _call("ln1_fwd", body, (s // tr,), [_row_spec(tr, d), _fix_spec((1, d)), _fix_spec((1, d))],
                 [_row_spec(tr, d), _row_spec(tr, 1), _row_spec(tr, d)],
                 [jax.ShapeDtypeStruct((s, d), F32), jax.ShapeDtypeStruct((s, 1), F32), jax.ShapeDtypeStruct((s, d), BF16)],
                 [pre1, g1, b1], sem=("parallel",), stages=stages)


def _ln2_loss_bwd(ff, xhat1, g1, b1, g2, b2, target, tr=256):
    s, d = ff.shape
    tr = _pick(tr, s)

    def body(ff_ref, xh1_ref, g1_ref, b1_ref, g2_ref, b2_ref, t_ref, dp_ref, dpb_ref, dg_ref, db_ref, dbf_ref, loss_ref):
        @pl.when(pl.program_id(0) == 0)
        def _():
            dg_ref[...] = jnp.zeros_like(dg_ref)
            db_ref[...] = jnp.zeros_like(db_ref)
            dbf_ref[...] = jnp.zeros_like(dbf_ref)
            loss_ref[...] = jnp.zeros_like(loss_ref)

        h1 = xh1_ref[...] * g1_ref[...] + b1_ref[...]
        xhat, rstd = _layer_norm_stats(ALPHA * h1 + ff_ref[...])
        err = xhat * g2_ref[...] + b2_ref[...] - t_ref[...]
        row = jnp.mean(err * err, axis=-1, keepdims=True)
        loss_ref[...] += 0.5 * jnp.sum(row, axis=0, keepdims=True)
        dy = err / d
        dg_ref[...] += jnp.sum(dy * xhat, axis=0, keepdims=True)
        db_ref[...] += jnp.sum(dy, axis=0, keepdims=True)
        dpre = _layer_norm_bwd(dy * g2_ref[...], xhat, rstd)
        dbf_ref[...] += jnp.sum(dpre, axis=0, keepdims=True)
        dp_ref[...] = dpre
        dpb_ref[...] = dpre.astype(BF16)

    vec = _fix_spec((1, d))
    return _call("ln2_loss_bwd", body, (s // tr,), [_row_spec(tr, d), _row_spec(tr, d), vec, vec, vec, vec, _row_spec(tr, d)],
                 [_row_spec(tr, d), _row_spec(tr, d), vec, vec, vec, _fix_spec((1, 1))],
                 [jax.ShapeDtypeStruct((s, d), F32), jax.ShapeDtypeStruct((s, d), BF16)]
                 + [jax.ShapeDtypeStruct((1, d), F32)] * 3 + [jax.ShapeDtypeStruct((1, 1), F32)],
                 [ff, xhat1, g1, b1, g2, b2, target])[0]


def _ln1_bwd(dh1, xhat1, rstd1, g1, tr=256, stages=()):
    s, d = dh1.shape
    tr = _pick(tr, s)

    def body(dh_ref, xh_ref, rs_ref, g_ref, dp_ref, dpb_ref, dg_ref, db_ref):
        @pl.when(pl.program_id(0) == 0)
        def _():
            dg_ref[...] = jnp.zeros_like(dg_ref)
            db_ref[...] = jnp.zeros_like(db_ref)

        dh, xhat = dh_ref[...], xh_ref[...]
        dg_ref[...] += jnp.sum(dh * xhat, axis=0, keepdims=True)
        db_ref[...] += jnp.sum(dh, axis=0, keepdims=True)
        dpre = _layer_norm_bwd(dh * g_ref[...], xhat, rs_ref[...])
        dp_ref[...] = dpre
        dpb_ref[...] = dpre.astype(BF16)

    vec = _fix_spec((1, d))
    return _call("ln1_bwd", body, (s // tr,), [_row_spec(tr, d), _row_spec(tr, d), _row_spec(tr, 1), vec],
                 [_row_spec(tr, d), _row_spec(tr, d), vec, vec],
                 [jax.ShapeDtypeStruct((s, d), F32), jax.ShapeDtypeStruct((s, d), BF16)] + [jax.ShapeDtypeStruct((1, d), F32)] * 2,
                 [dh1, xhat1, rstd1, g1], stages=stages)


def _from_perm(x):
    return x.reshape(N_SUB, SEQ // N_SUB, -1).transpose(1, 0, 2).reshape(SEQ, -1)


def _local_index(p):
    rho = np.arange(BLOCK)
    if p == 0:
        return 16 * (rho % 8) + rho // 8
    if p == 1:
        return 4 * (rho % 32) + rho // 32
    return rho


def _tile_view(x, p):
    c = x.shape[1]
    if p == 1:
        return x.reshape(4, 4, BLOCK, c)
    return x.reshape(N_SUB, BLOCK, c)


def _view_shape(c, p):
    return (4, 4, BLOCK, c) if p == 1 else (N_SUB, BLOCK, c)


def _tile_spec(p, width, col, shift=0):
    nblk = SEQ // DILATIONS[p] // BLOCK

    def blk(n):
        return jnp.clip(n + shift, 0, nblk - 1)

    if p == 0:
        return pl.BlockSpec((N_SUB, SUBLANES, width), lambda s, n: (0, blk(n), col))
    if p == 1:
        return pl.BlockSpec((4, None, 32, width), lambda s, n: (0, s, blk(n), col))
    return pl.BlockSpec((None, BLOCK, width), lambda s, n: (s, 0, col))


def _tile_grid(p):
    return ((1, 16), (4, 4), (16, 1))[p]


def _t5_bucket(n):
    max_exact = N_BUCKETS // 2
    nf = np.maximum(n, 1).astype(np.float32)
    large = max_exact + (np.log(nf / np.float32(max_exact)) / np.float32(math.log(MAX_DISTANCE / max_exact))
                         * np.float32(N_BUCKETS - max_exact)).astype(np.int32)
    large = np.minimum(large, N_BUCKETS - 1)
    return np.where(n < max_exact, n, large).astype(np.int32)


def _bucket_tables():
    tabs = np.zeros((3, 2, BLOCK, BLOCK), np.int32)
    for p, d in enumerate(DILATIONS):
        i = _local_index(p)
        diff = i[:, None] - i[None, :]
        tabs[p, 0] = np.where(diff <= 0, _t5_bucket((BLOCK + diff) * d), -1)
        tabs[p, 1] = np.where(diff >= 0, _t5_bucket(np.maximum(diff, 0) * d), -1)
    return tabs


def _bias_expand(rel_bias, buckets, stages=()):
    nh = N_HEADS

    def body(rb_ref, bk_ref, o_ref):
        for w in range(2):
            bk = bk_ref[0, w]
            for h in range(nh):
                val = jnp.zeros((BLOCK, BLOCK), F32)
                for b in range(N_BUCKETS):
                    val = jnp.where(bk == b, rb_ref[b, h], val)
                o_ref[0, h, w] = jnp.where(bk < 0, NEG_INF, val)

    return _call("bias_expand", body, (3,),
                 [pl.BlockSpec(memory_space=pltpu.SMEM), pl.BlockSpec((1, 2, BLOCK, BLOCK), lambda p: (p, 0, 0, 0))],
                 [pl.BlockSpec((1, nh, 2, BLOCK, BLOCK), lambda p: (p, 0, 0, 0, 0))],
                 [jax.ShapeDtypeStruct((3, nh, 2, BLOCK, BLOCK), F32)], [rel_bias, buckets], sem=("parallel",), stages=stages)


def _heads_to_lanes(cols):
    lane = lax.broadcasted_iota(I32, (BLOCK, LANES), 1)
    out = jnp.zeros((BLOCK, LANES), F32)
    for h, c in enumerate(cols):
        out = jnp.where(lane == h, c, out)
    return out


def _attn_fwd(qkv, bias, p, stages=()):
    d_a = _d_a()
    has_prev = SEQ // DILATIONS[p] // BLOCK > 1
    scale = HEAD_DIM ** -0.5
    view = _tile_view(qkv, p)

    width = 2 * BLOCK if has_prev else BLOCK

    def body(q_ref, kc_ref, kp_ref, vc_ref, vp_ref, b_ref, o_ref, l_ref, s_ref, p_ref):
        n = pl.program_id(1)
        q_all = q_ref[...].reshape(BLOCK, d_a).astype(BF16)
        k_all = kc_ref[...].reshape(BLOCK, d_a).astype(BF16)
        v_all = vc_ref[...].reshape(BLOCK, d_a).astype(BF16)
        if has_prev:
            k_all = jnp.concatenate([kp_ref[...].reshape(BLOCK, d_a).astype(BF16), k_all], axis=0)
            v_all = jnp.concatenate([vp_ref[...].reshape(BLOCK, d_a).astype(BF16), v_all], axis=0)
            no_prev = (lax.broadcasted_iota(I32, (BLOCK, width), 1) < BLOCK) & (n == 0)
        for h in range(N_HEADS):
            sl = slice(h * HEAD_DIM, (h + 1) * HEAD_DIM)
            s = _dot(q_all[:, sl], k_all[:, sl], "nt") * scale
            if has_prev:
                s = jnp.where(no_prev, NEG_INF, s + jnp.concatenate([b_ref[0, h, 0], b_ref[0, h, 1]], axis=1))
            else:
                s = s + b_ref[0, h, 1]
            s_ref[h] = s
        dens, lses = [], []
        for h in range(N_HEADS):
            s = s_ref[h]
            m = jnp.max(s, axis=-1, keepdims=True)
            pr = jnp.exp(s - m)
            den = jnp.sum(pr, axis=-1, keepdims=True)
            p_ref[h] = pr.astype(BF16)
            dens.append(den)
            lses.append(m + jnp.log(den))
        for h in range(N_HEADS):
            sl = slice(h * HEAD_DIM, (h + 1) * HEAD_DIM)
            o_ref[..., sl] = (_dot(p_ref[h], v_all[:, sl], "nn") / dens[h]).reshape(*o_ref.shape[:-1], HEAD_DIM)
        l_ref[...] = _heads_to_lanes(lses).reshape(l_ref.shape)

    (o, l), st = _call(
        f"attn_fwd{p}", body, _tile_grid(p),
        [_tile_spec(p, d_a, 0), _tile_spec(p, d_a, 1), _tile_spec(p, d_a, 1, -1), _tile_spec(p, d_a, 2), _tile_spec(p, d_a, 2, -1),
         pl.BlockSpec((1, N_HEADS, 2, BLOCK, BLOCK), lambda s, n: (p, 0, 0, 0, 0))],
        [_tile_spec(p, d_a, 0), _tile_spec(p, LANES, 0)],
        [jax.ShapeDtypeStruct(_view_shape(d_a, p), F32), jax.ShapeDtypeStruct(_view_shape(LANES, p), F32)],
        [view, view, view, view, view, bias], scratch=[pltpu.VMEM((N_HEADS, BLOCK, width), F32), pltpu.VMEM((N_HEADS, BLOCK, width), BF16)],
        sem=("parallel", "parallel"), stages=stages)
    return (o.reshape(SEQ, d_a), l.reshape(SEQ, LANES)), st


def _attn_combine(os_, ls_, tr=256, stages=()):
    d_a = _d_a()
    tr = _pick(tr, SEQ)

    def body(o0, o1, o2, l0, l1, l2, a_ref, ab_ref, lt_ref):
        l = [l0[...], l1[...], l2[...]]
        m = jnp.maximum(jnp.maximum(l[0], l[1]), l[2])
        w = [jnp.exp(x - m) for x in l]
        tot = w[0] + w[1] + w[2]
        lt_ref[...] = m + jnp.log(tot)
        w = [x / tot for x in w]
        for h in range(N_HEADS):
            sl = slice(h * HEAD_DIM, (h + 1) * HEAD_DIM)
            acc = w[0][:, h:h + 1] * o0[:, sl] + w[1][:, h:h + 1] * o1[:, sl] + w[2][:, h:h + 1] * o2[:, sl]
            a_ref[:, sl] = acc
            ab_ref[:, sl] = acc.astype(BF16)

    return _call("attn_combine", body, (SEQ // tr,), [_row_spec(tr, d_a)] * 3 + [_row_spec(tr, LANES)] * 3,
                 [_row_spec(tr, d_a), _row_spec(tr, d_a), _row_spec(tr, LANES)],
                 [jax.ShapeDtypeStruct((SEQ, d_a), F32), jax.ShapeDtypeStruct((SEQ, d_a), BF16), jax.ShapeDtypeStruct((SEQ, LANES), F32)],
                 [*os_, *ls_], sem=("parallel",), stages=stages)


def _attn_delta(dattn, attn, tr=256):
    d_a = _d_a()
    tr = _pick(tr, SEQ)

    def body(d_ref, a_ref, o_ref):
        prod = d_ref[...] * a_ref[...]
        lane = lax.broadcasted_iota(I32, (tr, LANES), 1)
        out = jnp.zeros((tr, LANES), F32)
        for h in range(N_HEADS):
            out = jnp.where(lane == h, jnp.sum(prod[:, h * HEAD_DIM:(h + 1) * HEAD_DIM], axis=-1, keepdims=True), out)
        o_ref[...] = out

    return _call("attn_delta", body, (SEQ // tr,), [_row_spec(tr, d_a)] * 2, [_row_spec(tr, LANES)],
                 [jax.ShapeDtypeStruct((SEQ, LANES), F32)], [dattn, attn], sem=("parallel",))[0][0]


def _attn_bwd(qkv, dattn, lse, delta, bias, p, stages=()):
    d_a = _d_a()
    nblk = SEQ // DILATIONS[p] // BLOCK
    has_next = nblk > 1
    scale = HEAD_DIM ** -0.5
    qv, dov, lv, tv = (_tile_view(x, p) for x in (qkv, dattn, lse, delta))

    rows = 2 * BLOCK if has_next else BLOCK

    def body(q_ref, qn_ref, k_ref, v_ref, do_ref, don_ref, l_ref, ln_ref, t_ref, tn_ref, b_ref, dq_ref, dk_ref, dv_ref, db_ref,
             carry_ref, s_ref, dp_ref, p_ref, ds_ref):
        j = pl.program_id(1)

        @pl.when((pl.program_id(0) == 0) & (j == 0))
        def _():
            db_ref[...] = jnp.zeros_like(db_ref)

        def both(cur, nxt, width, dtype):
            cur = cur[...].reshape(BLOCK, width).astype(dtype)
            return jnp.concatenate([cur, nxt[...].reshape(BLOCK, width).astype(dtype)], axis=0) if has_next else cur

        k_all = k_ref[...].reshape(BLOCK, d_a).astype(BF16)
        v_all = v_ref[...].reshape(BLOCK, d_a).astype(BF16)
        q_all, do_all = both(q_ref, qn_ref, d_a, BF16), both(do_ref, don_ref, d_a, BF16)
        l_all, t_all = both(l_ref, ln_ref, LANES, F32), both(t_ref, tn_ref, LANES, F32)
        if has_next:
            no_next = (lax.broadcasted_iota(I32, (rows, BLOCK), 0) >= BLOCK) & (j == nblk - 1)
        for h in range(N_HEADS):
            sl = slice(h * HEAD_DIM, (h + 1) * HEAD_DIM)
            s = _dot(q_all[:, sl], k_all[:, sl], "nt") * scale
            if has_next:
                s = jnp.where(no_next, NEG_INF, s + jnp.concatenate([b_ref[0, h, 1], b_ref[0, h, 0]], axis=0))
            else:
                s = s + b_ref[0, h, 1]
            s_ref[h] = s
            dp_ref[h] = _dot(do_all[:, sl], v_all[:, sl], "nt")
        for h in range(N_HEADS):
            pr = jnp.exp(s_ref[h] - l_all[:, h:h + 1])
            ds = pr * (dp_ref[h] - t_all[:, h:h + 1])
            db_ref[h, 1] += ds[:BLOCK]
            if has_next:
                db_ref[h, 0] += ds[BLOCK:]
            p_ref[h] = pr.astype(BF16)
            ds_ref[h] = ds.astype(BF16)
        for h in range(N_HEADS):
            sl = slice(h * HEAD_DIM, (h + 1) * HEAD_DIM)
            dq = _dot(ds_ref[h], k_all[:, sl], "nn") * scale
            mine = dq[:BLOCK]
            if has_next:
                mine = mine + jnp.where(j > 0, carry_ref[:, sl], 0.0)
            dq_ref[..., sl] = mine.reshape(*dq_ref.shape[:-1], HEAD_DIM)
            if has_next:
                carry_ref[:, sl] = dq[BLOCK:]
            dk_ref[..., sl] = (_dot(ds_ref[h], q_all[:, sl], "tn") * scale).reshape(*dk_ref.shape[:-1], HEAD_DIM)
            dv_ref[..., sl] = _dot(p_ref[h], do_all[:, sl], "tn").reshape(*dv_ref.shape[:-1], HEAD_DIM)

    def big(col, shift=0):
        return _tile_spec(p, d_a, col, shift)

    def small(shift=0):
        return _tile_spec(p, LANES, 0, shift)

    (dq, dk, dv, dbias), st = _call(
        f"attn_bwd{p}", body, _tile_grid(p),
        [big(0), big(0, 1), big(1), big(2), big(0), big(0, 1), small(), small(1), small(), small(1),
         pl.BlockSpec((1, N_HEADS, 2, BLOCK, BLOCK), lambda s, n: (p, 0, 0, 0, 0))],
        [big(0), big(0), big(0), pl.BlockSpec((N_HEADS, 2, BLOCK, BLOCK), lambda s, n: (0, 0, 0, 0))],
        [jax.ShapeDtypeStruct(_view_shape(d_a, p), F32)] * 3 + [jax.ShapeDtypeStruct((N_HEADS, 2, BLOCK, BLOCK), F32)],
        [qv, qv, qv, qv, dov, dov, lv, lv, tv, tv, bias],
        scratch=[pltpu.VMEM((BLOCK, d_a), F32), pltpu.VMEM((N_HEADS, rows, BLOCK), F32), pltpu.VMEM((N_HEADS, rows, BLOCK), F32),
                 pltpu.VMEM((N_HEADS, rows, BLOCK), BF16), pltpu.VMEM((N_HEADS, rows, BLOCK), BF16)], stages=stages)
    return (dq.reshape(SEQ, d_a), dk.reshape(SEQ, d_a), dv.reshape(SEQ, d_a), dbias), st


def _rel_bias_grad(dbias, buckets):
    nh = N_HEADS

    def body(d0, d1, d2, bk_ref, o_ref, t_ref):
        ds = (d0, d1, d2)

        def per_bucket(b, carry):
            for h in range(nh):
                acc = jnp.zeros((BLOCK, BLOCK), F32)
                for p in range(3):
                    for w in range(2):
                        acc = acc + jnp.where(bk_ref[p, w] == b, ds[p][h, w], 0.0)
                t_ref[pl.ds(b * nh + h, 1), :] = jnp.sum(acc, axis=0, keepdims=True)
            return carry

        lax.fori_loop(0, N_BUCKETS, per_bucket, 0)
        o_ref[...] = jnp.sum(t_ref[...], axis=-1, keepdims=True)

    return _call("rel_bias_grad", body, (1,), [_fix_spec((nh, 2, BLOCK, BLOCK))] * 3 + [_fix_spec((3, 2, BLOCK, BLOCK))],
                 [_fix_spec((N_BUCKETS * nh, 1))], [jax.ShapeDtypeStruct((N_BUCKETS * nh, 1), F32)], [*dbias, buckets],
                 scratch=[pltpu.VMEM((N_BUCKETS * nh, LANES), F32)])[0][0]


def _gmlp_fwd(rest, col0, gain, bias, ws, bs, causal, stages=()):
    d_b = _d_b()

    def body(u_ref, v_ref, g_ref, b_ref, ws_ref, bs_ref, c_ref, o_ref):
        u = u_ref[...].reshape(BLOCK, d_b)
        xhat, _ = _layer_norm_stats(_gelu(v_ref[...].reshape(BLOCK, d_b)))
        vn = (xhat * g_ref[...] + b_ref[...]).astype(BF16)
        outs = []
        for g in range(N_GROUPS):
            sl = slice(g * BLOCK, (g + 1) * BLOCK)
            w = jnp.where(c_ref[...] > 0, ws_ref[g], 0.0)
            z = _dot(w, vn[:, sl], "nn") + bs_ref[:, g:g + 1]
            outs.append(_gelu(u[:, sl]) * z)
        o_ref[...] = jnp.concatenate(outs, axis=-1).reshape(o_ref.shape)

    (out,), st = _call(
        "gmlp_fwd", body, (1, SEQ // BLOCK),
        [_tile_spec(0, d_b, col0), _tile_spec(0, d_b, col0 + 1), _fix_spec((1, d_b)), _fix_spec((1, d_b)),
         _fix_spec((N_GROUPS, BLOCK, BLOCK)), _fix_spec((BLOCK, N_GROUPS)), _fix_spec((BLOCK, BLOCK))],
        [_tile_spec(0, d_b, 0)], [jax.ShapeDtypeStruct(_view_shape(d_b, 0), F32)],
        [_tile_view(rest, 0), _tile_view(rest, 0), gain, bias, ws, bs, causal], sem=("parallel", "parallel"), stages=stages)
    return out.reshape(SEQ, d_b), st


def _gmlp_bwd(rest, col0, dgmlp, gain, bias, ws, bs, causal, stages=()):
    d_b = _d_b()
    nchunk = SEQ // BLOCK

    def body(u_ref, v_ref, dg_ref, g_ref, b_ref, ws_ref, bs_ref, c_ref, du_ref, dv_ref, dws_ref, dbs_ref, dgain_ref, dbias_ref):
        c = pl.program_id(1)

        @pl.when(c == 0)
        def _():
            dws_ref[...] = jnp.zeros_like(dws_ref)
            dbs_ref[...] = jnp.zeros_like(dbs_ref)
            dgain_ref[...] = jnp.zeros_like(dgain_ref)
            dbias_ref[...] = jnp.zeros_like(dbias_ref)

        u = u_ref[...].reshape(BLOCK, d_b)
        v = v_ref[...].reshape(BLOCK, d_b)
        dgm = dg_ref[...].reshape(BLOCK, d_b)
        xhat, rstd = _layer_norm_stats(_gelu(v))
        vn = (xhat * g_ref[...] + b_ref[...]).astype(BF16)
        lane = lax.broadcasted_iota(I32, (BLOCK, LANES), 1)
        dus, dvns = [], []
        dbs = dbs_ref[...]
        for g in range(N_GROUPS):
            sl = slice(g * BLOCK, (g + 1) * BLOCK)
            w = jnp.where(c_ref[...] > 0, ws_ref[g], 0.0).astype(BF16)
            z = _dot(w, vn[:, sl], "nn") + bs_ref[:, g:g + 1]
            dz = dgm[:, sl] * _gelu(u[:, sl])
            dus.append(dgm[:, sl] * z * _gelu_grad(u[:, sl]))
            dws_ref[g] += _dot(dz, vn[:, sl], "nt")
            dbs = dbs + jnp.where(lane == g, jnp.sum(dz, axis=-1, keepdims=True), 0.0)
            dvns.append(_dot(w, dz, "tn"))
        dbs_ref[...] = dbs
        dvn = jnp.concatenate(dvns, axis=-1)
        dgain_ref[...] += jnp.sum(dvn * xhat, axis=0, keepdims=True)
        dbias_ref[...] += jnp.sum(dvn, axis=0, keepdims=True)
        dvg = _layer_norm_bwd(dvn * g_ref[...], xhat, rstd)
        du_ref[...] = jnp.concatenate(dus, axis=-1).reshape(du_ref.shape)
        dv_ref[...] = (dvg * _gelu_grad(v)).reshape(dv_ref.shape)

        @pl.when(c == nchunk - 1)
        def _():
            for g in range(N_GROUPS):
                dws_ref[g] = jnp.where(c_ref[...] > 0, dws_ref[g], 0.0)

    (du, dv, dws, dbs, dgain, dbias), st = _call(
        "gmlp_bwd", body, (1, nchunk),
        [_tile_spec(0, d_b, col0), _tile_spec(0, d_b, col0 + 1), _tile_spec(0, d_b, 0), _fix_spec((1, d_b)), _fix_spec((1, d_b)),
         _fix_spec((N_GROUPS, BLOCK, BLOCK)), _fix_spec((BLOCK, N_GROUPS)), _fix_spec((BLOCK, BLOCK))],
        [_tile_spec(0, d_b, 0), _tile_spec(0, d_b, 0), _fix_spec((N_GROUPS, BLOCK, BLOCK)), _fix_spec((BLOCK, LANES)),
         _fix_spec((1, d_b)), _fix_spec((1, d_b))],
        [jax.ShapeDtypeStruct(_view_shape(d_b, 0), F32)] * 2
        + [jax.ShapeDtypeStruct((N_GROUPS, BLOCK, BLOCK), F32), jax.ShapeDtypeStruct((BLOCK, LANES), F32)]
        + [jax.ShapeDtypeStruct((1, d_b), F32)] * 2,
        [_tile_view(rest, 0), _tile_view(rest, 0), _tile_view(dgmlp, 0), gain, bias, ws, bs, causal], stages=stages)
    return (du.reshape(SEQ, d_b), dv.reshape(SEQ, d_b), dws, dbs, dgain, dbias), st


def _assemble_dproj(dqkv, du, dv, dga, dgb, tr=128, stages=()):
    d_a, d_b, d_in = _d_a(), _d_b(), _d_in()
    tr = _pick(tr, SEQ)

    def body(*refs):
        att, (du_ref, dv_ref, dga_ref, dgb_ref, o_ref) = refs[:9], refs[9:]
        for i in range(3):
            o_ref[:, i * d_a:(i + 1) * d_a] = (att[3 * i][...] + att[3 * i + 1][...] + att[3 * i + 2][...]).astype(BF16)
        o_ref[:, 3 * d_a:3 * d_a + d_b] = du_ref[...].astype(BF16)
        o_ref[:, 3 * d_a + d_b:3 * d_a + 2 * d_b] = dv_ref[...].astype(BF16)
        o_ref[:, 3 * d_a + 2 * d_b:3 * d_a + 2 * d_b + D_MODEL] = dga_ref[...]
        o_ref[:, 3 * d_a + 2 * d_b + D_MODEL:] = dgb_ref[...]

    return _call("assemble_dproj", body, (SEQ // tr,), [_row_spec(tr, d_a)] * 9 + [_row_spec(tr, d_b)] * 2 + [_row_spec(tr, D_MODEL)] * 2,
                 [_row_spec(tr, d_in)], [jax.ShapeDtypeStruct((SEQ, d_in), BF16)], [*dqkv, du, dv, dga, dgb], sem=("parallel",), stages=stages)


def _dw(name, a, b, kind, core, mine, add=None, tn=1152, half=None, stages=()):
    s, m = a.shape
    n = b.shape[1]
    if half is not None:
        m //= 2
    rs, cs = (m, n // N_DEV) if kind == "col" else (m // N_DEV, n)
    tn = _pick(tn if cs % tn == 0 else 512, cs)
    nj = cs // tn

    def shard(q, c_ref):
        return 2 * q + (c_ref[0] if mine else 1 - c_ref[0])

    if kind == "col":
        a_spec = pl.BlockSpec((s, m), lambda q, j, c_ref: (0, half or 0))
        b_spec = pl.BlockSpec((s, tn), lambda q, j, c_ref: (0, shard(q, c_ref) * nj + j))
    else:
        a_spec = pl.BlockSpec((s, rs), lambda q, j, c_ref: (0, shard(q, c_ref)))
        b_spec = pl.BlockSpec((s, tn), lambda q, j, c_ref: (0, j))
    o_spec = pl.BlockSpec((None, rs, tn), lambda q, j, c_ref: (q, 0, j))

    def body(a_ref, b_ref, *rest):
        acc = _dot(a_ref[...], b_ref[...], "tn")
        if add is not None:
            acc = acc + rest[0][...].astype(F32)
        rest[-1][...] = acc.astype(BF16)

    (out,), st = _call(name, body, (N_CHIPS, nj), [a_spec, b_spec] + ([o_spec] if add is not None else []), [o_spec],
                       [jax.ShapeDtypeStruct((N_CHIPS, rs, cs), BF16)], [a, b] + ([add] if add is not None else []),
                       sem=("parallel", "parallel"), stages=stages, prefetch=core)
    return out, st


def _adamw(w, g, m, v):
    m = ADAM_B1 * m + (1.0 - ADAM_B1) * g
    v = ADAM_B2 * v + (1.0 - ADAM_B2) * (g * g)
    m_hat = m / (1.0 - ADAM_B1 ** ADAM_STEP)
    v_hat = v / (1.0 - ADAM_B2 ** ADAM_STEP)
    delta = -ADAM_LR * (m_hat / (jnp.sqrt(v_hat) + ADAM_EPS) + ADAM_WD * w)
    return delta, m, v


def _adam_shard(name, pair, chip_sums, chip, w, m, v, tr=256, rows=None, into=None, stages=()):
    rs, cs = w.shape
    lo, cnt = rows or (0, rs)
    assert pair.shape[1] == cnt
    tr = _pick(tr, cnt, lo)
    first = lo // tr

    def body(chip_ref, own_ref, *refs):
        slots, (w_ref, m_ref, v_ref), (g_ref, d_ref, nm_ref, nv_ref) = refs[:N_CHIPS], refs[N_CHIPS:N_CHIPS + 3], refs[-4:]
        g = None
        for q in range(N_CHIPS):
            term = jnp.where(chip_ref[0] == q, own_ref[...], slots[q][...]).astype(F32)
            g = term if g is None else g + term
        d, nm, nv = _adamw(w_ref[...], g, m_ref[...], v_ref[...])
        g_ref[...], d_ref[...], nm_ref[...], nv_ref[...] = g, d, nm, nv

    def slot(q):
        return pl.BlockSpec((None, tr, cs), lambda i, c_ref: (jnp.where(c_ref[0] == q, (q + 1) % N_CHIPS, q), i, 0))

    spec = pl.BlockSpec((tr, cs), lambda i, c_ref: (i + first, 0))
    n_in = 1 + N_CHIPS + 3
    return _call(name, body, (cnt // tr,),
                 [pl.BlockSpec((None, tr, cs), lambda i, c_ref: (c_ref[0], i, 0))] + [slot(q) for q in range(N_CHIPS)]
                 + [spec, spec, spec] + ([ANY] * 4 if into is not None else []),
                 [spec] * 4, [jax.ShapeDtypeStruct((rs, cs), F32)] * 4, [pair] + [chip_sums] * N_CHIPS + [w, m, v] + list(into or ()),
                 sem=("parallel",), stages=stages, prefetch=chip, shown=True,
                 alias={n_in + k: k for k in range(4)} if into is not None else None)


def _adam_small(part, parts, me, w, m, v):
    rows = w.shape[0]

    def body(me_ref, own_ref, *refs):
        slots, (w_ref, m_ref, v_ref, g_ref, d_ref, nm_ref, nv_ref) = refs[:N_DEV], refs[N_DEV:]
        g = None
        for j in range(N_DEV):
            term = jnp.where(me_ref[0] == j, own_ref[...], slots[j][...])
            g = term if g is None else g + term
        d, nm, nv = _adamw(w_ref[...], g, m_ref[...], v_ref[...])
        g_ref[...], d_ref[...], nm_ref[...], nv_ref[...] = g, d, nm, nv

    def slot(j):
        return pl.BlockSpec((None, rows, LANES), lambda i, me_ref: (jnp.where(me_ref[0] == j, (j + 1) % N_DEV, j), 0, 0))

    spec = _fix_spec((rows, LANES))
    return _call("adam_small", body, (1,), [spec] + [slot(j) for j in range(N_DEV)] + [spec, spec, spec], [spec] * 4,
                 [jax.ShapeDtypeStruct((rows, LANES), F32)] * 4, [part] + [parts] * N_DEV + [w, m, v], prefetch=me, shown=True)[0]


def _small_sizes():
    d_b = _d_b()
    return (("loss", 1), ("rel_bias", N_BUCKETS * N_HEADS), ("ln_v_gain", d_b), ("ln_v_bias", d_b),
            ("w_spatial", N_GROUPS * BLOCK * BLOCK), ("b_spatial", N_GROUPS * BLOCK), ("ln1_gain", D_MODEL), ("ln1_bias", D_MODEL),
            ("b_ff1", D_FF), ("b_ff2", D_MODEL), ("ln2_gain", D_MODEL), ("ln2_bias", D_MODEL))


def _pack(vals):
    pieces = []
    for name, size in _small_sizes():
        flat = vals[name].reshape(-1).astype(F32)
        padded = -(-size // (SUBLANES * LANES)) * SUBLANES * LANES
        pieces.append(jnp.pad(flat, (0, padded - size)).reshape(-1, LANES))
    return jnp.concatenate(pieces, axis=0)


def _unpack(buf):
    out, row = {}, 0
    for name, size in _small_sizes():
        rows = -(-size // (SUBLANES * LANES)) * SUBLANES
        out[name] = buf[row:row + rows].reshape(-1)[:size]
        row += rows
    return out


def kernel(x, w_in, rel_bias, ln_v_gain, ln_v_bias, w_spatial, b_spatial, w_proj_a, w_proj_b, w_out, ln1_gain, ln1_bias, w_ff1, b_ff1, w_ff2, b_ff2, ln2_gain, ln2_bias, loss_target, m_w_in, m_rel_bias, m_ln_v_gain, m_ln_v_bias, m_w_spatial, m_b_spatial, m_w_proj_a, m_w_proj_b, m_w_out, m_ln1_gain, m_ln1_bias, m_w_ff1, m_b_ff1, m_w_ff2, m_b_ff2, m_ln2_gain, m_ln2_bias, v_w_in, v_rel_bias, v_ln_v_gain, v_ln_v_bias, v_w_spatial, v_b_spatial, v_w_proj_a, v_w_proj_b, v_w_out, v_ln1_gain, v_ln1_bias, v_w_ff1, v_b_ff1, v_w_ff2, v_b_ff2, v_ln2_gain, v_ln2_bias):
    d_a, d_b, d_in = _d_a(), _d_b(), _d_in()
    weights = dict(w_in=w_in, rel_bias=rel_bias, ln_v_gain=ln_v_gain, ln_v_bias=ln_v_bias, w_spatial=w_spatial, b_spatial=b_spatial,
                   w_proj_a=w_proj_a, w_proj_b=w_proj_b, w_out=w_out, ln1_gain=ln1_gain, ln1_bias=ln1_bias, w_ff1=w_ff1, b_ff1=b_ff1,
                   w_ff2=w_ff2, b_ff2=b_ff2, ln2_gain=ln2_gain, ln2_bias=ln2_bias)
    mom1 = dict(w_in=m_w_in, rel_bias=m_rel_bias, ln_v_gain=m_ln_v_gain, ln_v_bias=m_ln_v_bias, w_spatial=m_w_spatial,
                b_spatial=m_b_spatial, w_proj_a=m_w_proj_a, w_proj_b=m_w_proj_b, w_out=m_w_out, ln1_gain=m_ln1_gain,
                ln1_bias=m_ln1_bias, w_ff1=m_w_ff1, b_ff1=m_b_ff1, w_ff2=m_w_ff2, b_ff2=m_b_ff2, ln2_gain=m_ln2_gain, ln2_bias=m_ln2_bias)
    mom2 = dict(w_in=v_w_in, rel_bias=v_rel_bias, ln_v_gain=v_ln_v_gain, ln_v_bias=v_ln_v_bias, w_spatial=v_w_spatial,
                b_spatial=v_b_spatial, w_proj_a=v_w_proj_a, w_proj_b=v_w_proj_b, w_out=v_w_out, ln1_gain=v_ln1_gain,
                ln1_bias=v_ln1_bias, w_ff1=v_w_ff1, b_ff1=v_b_ff1, w_ff2=v_w_ff2, b_ff2=v_b_ff2, ln2_gain=v_ln2_gain, ln2_bias=v_ln2_bias)

    mx, my, mc = _coords()
    me = (4 * mx + 2 * my + mc).astype(I32).reshape(1)
    chip = (2 * mx + my).astype(I32).reshape(1)
    full = {n: _cast_into_place(f"cast_{n}", weights[n][0], KINDS[n], me)[0] for n in KINDS if n != "w_ff1"}
    sent = {n: (0, 0, 0) for n in KINDS}

    def keep(table, n):
        def store(outs):
            table[n] = outs[0]
        return store

    def gathering(**new):
        stages = []
        for n in KINDS:
            out, relayed, passed = sent[n]
            units = new.get(n, 0)
            if units or relayed < out or passed < relayed:
                st = _gather_stage(full[n], KINDS[n], (out, units) if units else None,
                                   (relayed, out - relayed) if relayed < out else None, (passed, relayed - passed) if passed < relayed else None)
                st.store = keep(full, n)
                sent[n] = (out + units, out, relayed)
                stages.append(st)
        return stages

    def settle(stages, outs):
        for st, o in zip(stages, outs):
            st.store(o)

    def alone(name, stages):
        settle(stages, _comm_only(name, stages))

    def here(n):
        assert sent[n] == (16, 16, 16), (n, sent[n])
        return full[n]

    hosted = gathering(w_in=16)
    full["w_ff1"], st = _cast_into_place("cast_w_ff1", weights["w_ff1"][0], KINDS["w_ff1"], me, stages=hosted)
    settle(hosted, st)
    hosted = gathering()
    xs, xb, target, st = _perm_in("cast_x", x[0], loss_target[0], stages=hosted)
    settle(hosted, st)
    g8 = BLOCK // N_SUB
    ws_t = w_spatial[0].reshape(N_GROUPS, g8, N_SUB, g8, N_SUB).transpose(0, 2, 1, 4, 3).reshape(N_GROUPS, BLOCK, BLOCK)
    bs_t = b_spatial[0].reshape(N_GROUPS, g8, N_SUB).transpose(2, 1, 0).reshape(BLOCK, N_GROUPS)
    idx = _local_index(0)
    causal = jnp.asarray((idx[:, None] >= idx[None, :]).astype(np.float32))
    buckets = jnp.asarray(_bucket_tables())
    hosted = gathering(w_proj_a=16, w_proj_b=16)
    (bias,), st = _bias_expand(rel_bias, buckets, stages=hosted)
    settle(hosted, st)

    hosted = gathering(w_out=8, w_ff1=1)
    (qkv,), st = _matmul("proj_qkv", xb, here("w_in"), "nn", [F32], n=3 * d_a, stages=hosted)
    settle(hosted, st)
    hosted = gathering(w_out=8, w_ff1=6)
    (rest,), st = _matmul("proj_rest", xb, here("w_in"), "nn", [F32], b_off=3 * d_a, n=d_in - 3 * d_a, stages=hosted)
    settle(hosted, st)
    fwd = []
    for p in range(3):
        hosted = gathering(**({"w_ff1": 2}, {"w_ff1": 5}, {"w_ff1": 2, "w_ff2": 1})[p])
        res, st = _attn_fwd(qkv, bias, p, stages=hosted)
        settle(hosted, st)
        fwd.append(res)
    hosted = gathering(w_ff2=1)
    (attn, attn_b, lse), st = _attn_combine([o for o, _ in fwd], [l for _, l in fwd], stages=hosted)
    settle(hosted, st)
    hosted = gathering(w_ff2=2)
    gmlp, st = _gmlp_fwd(rest, 0, ln_v_gain, ln_v_bias, ws_t, bs_t, causal, stages=hosted)
    settle(hosted, st)
    hosted = gathering(w_ff2=3)
    (ya,), st = _matmul("proj_a", attn_b, here("w_proj_a"), "nn", [BF16], stages=hosted)
    settle(hosted, st)
    gate_a, gate_b = 2 * d_b, 2 * d_b + D_MODEL

    def merge(acc, ya_, ga, gb):
        return acc, _sigmoid(ga) * ya_ + _sigmoid(gb) * acc

    hosted = gathering(w_ff2=5)
    (yb, merged), st = _matmul("proj_b_merge", gmlp, here("w_proj_b"), "nn", [BF16, BF16], merge,
                               [(ya, "mn", 0), (rest, "mn", gate_a), (rest, "mn", gate_b)], tn=256, stages=hosted)
    settle(hosted, st)
    hosted = gathering(w_ff2=3)
    (pre1,), st = _matmul("out_proj", merged, here("w_out"), "nn", [F32], lambda acc, x_: (ALPHA * x_ + acc,), [(xs, "mn", 0)], stages=hosted)
    settle(hosted, st)
    hosted = gathering(w_ff2=1)
    (xhat1, rstd1, h1b), st = _ln1_fwd(pre1, ln1_gain, ln1_bias, stages=hosted)
    settle(hosted, st)

    def relu2(acc, b_):
        r = jnp.maximum(acc + b_, 0.0)
        return r, r * r

    hosted = gathering()
    (relu, fb), st = _matmul("ff1", h1b, here("w_ff1"), "nn", [BF16, BF16], relu2, [(b_ff1, "row", 0)], stages=hosted)
    settle(hosted, st)
    alone("gather_w_ff2_sibling", gathering())
    (ff,), _ = _matmul("ff2", fb, here("w_ff2"), "nn", [F32], lambda acc, b_: (acc + b_,), [(b_ff2, "row", 0)], tn=512, tk=2048)

    core = lax.axis_index("c").astype(I32).reshape(1)
    factors, theirs, sib, pair, chips, reduced = {}, {}, {}, {}, {}, {}

    def grad_for_sibling(n, a, b, stages=(), half=None, of=None):
        factors[n] = (a, b, KINDS[of or n], half)
        theirs[n], outs = _dw(f"dw_{n}_sibling", a, b, KINDS[of or n], core, False, half=half, stages=stages)
        settle(stages, outs)

    def to_sibling(n):
        st = _to_sibling_stage(theirs[n])
        st.store = keep(sib, n)
        return st

    def grad_own(n, stages=()):
        a, b, kind, half = factors[n]
        pair[n], outs = _dw(f"dw_{n}_own", a, b, kind, core, True, add=sib[n], half=half, stages=stages)
        settle(stages, outs)
        chips[n] = lax.empty(pair[n].shape, BF16)
        reduced[n] = 0

    def reducing(**new):
        stages = []
        for n, units in new.items():
            st = _to_chips_stage(pair[n], chips[n], (reduced[n], units))
            st.store = keep(chips, n)
            reduced[n] += units
            stages.append(st)
        return stages

    def summed(n):
        assert reduced[n] == 16, (n, reduced[n])
        return chips[n]

    dpre2, dpre2b, g_ln2_gain, g_ln2_bias, g_b_ff2, loss_part = _ln2_loss_bwd(ff, xhat1, ln1_gain, ln1_bias, ln2_gain, ln2_bias, target)
    grad_for_sibling("w_ff2", fb, dpre2b)

    def relu2_bwd(acc, r):
        da = acc * (2.0 * r)
        return da, da

    hosted = [to_sibling("w_ff2")]
    (dab, g_b_ff1), st = _matmul("d_ff1", dpre2b, here("w_ff2"), "nt", [BF16], relu2_bwd, [(relu, "mn", 0)], colsums=(1,), stages=hosted)
    settle(hosted, st)
    grad_own("w_ff2")
    grad_for_sibling("w_ff1", h1b, dab, reducing(w_ff2=3))
    hosted = reducing(w_ff2=8) + [to_sibling("w_ff1")]
    (dh1,), st = _matmul("d_h1", dab, here("w_ff1"), "nt", [F32], lambda acc, d_: (acc + ALPHA * d_,), [(dpre2, "mn", 0)], stages=hosted)
    settle(hosted, st)
    grad_own("w_ff1", reducing(w_ff2=4))
    hosted = reducing(w_ff2=1)
    (dpre1, dpre1b, g_ln1_gain, g_ln1_bias), st = _ln1_bwd(dh1, xhat1, rstd1, ln1_gain, stages=hosted)
    settle(hosted, st)
    grad_for_sibling("w_out", merged, dpre1b, reducing(w_ff1=1))

    def merge_bwd(acc, ga, gb, ya_, yb_):
        sa, sb = _sigmoid(ga), _sigmoid(gb)
        return acc * sa, acc * sb, acc * ya_ * (sa * (1.0 - sa)), acc * yb_ * (sb * (1.0 - sb))

    hosted = reducing(w_ff1=6) + [to_sibling("w_out")]
    (dya, dyb, dga, dgb), st = _matmul("d_merge", dpre1b, here("w_out"), "nt", [BF16] * 4, merge_bwd,
                                       [(rest, "mn", gate_a), (rest, "mn", gate_b), (ya, "mn", 0), (yb, "mn", 0)], tn=256, stages=hosted)
    settle(hosted, st)
    grad_own("w_out", reducing(w_ff1=1))
    grad_for_sibling("w_proj_a", attn_b, dya)
    grad_for_sibling("w_proj_b", gmlp, dyb)
    hosted = reducing(w_ff1=1) + [to_sibling("w_proj_a"), to_sibling("w_proj_b")]
    (dattn,), st = _matmul("d_attn", dya, here("w_proj_a"), "nt", [F32], stages=hosted)
    settle(hosted, st)
    grad_own("w_proj_a")
    grad_own("w_proj_b")
    hosted = reducing(w_ff1=1)
    (dgmlp,), st = _matmul("d_gmlp", dyb, here("w_proj_b"), "nt", [F32], stages=hosted)
    settle(hosted, st)
    hosted = reducing(w_ff1=2)
    (du, dvb, dws_t, dbs_t, g_lnv_gain, g_lnv_bias), st = _gmlp_bwd(rest, 0, dgmlp, ln_v_gain, ln_v_bias, ws_t, bs_t, causal, stages=hosted)
    settle(hosted, st)
    delta = _attn_delta(dattn, attn)
    bwd = []
    for p in range(3):
        hosted = reducing(**({"w_ff1": 3}, {"w_ff1": 1, "w_out": 8}, {"w_out": 8, "w_proj_a": 8})[p])
        res, st = _attn_bwd(qkv, dattn, lse, delta, bias, p, stages=hosted)
        settle(hosted, st)
        bwd.append(res)
    g_rel_bias = _rel_bias_grad([b[3] for b in bwd], buckets)
    hosted = reducing(w_proj_a=8, w_proj_b=16)
    (dproj,), st = _assemble_dproj([b[i] for i in range(3) for b in bwd], du, dvb, dga, dgb, stages=hosted)
    settle(hosted, st)

    g_w_spatial = dws_t.reshape(N_GROUPS, N_SUB, g8, N_SUB, g8).transpose(0, 2, 1, 4, 3)
    g_b_spatial = dbs_t[:, :N_GROUPS].reshape(N_SUB, g8, N_GROUPS).transpose(2, 1, 0)
    part = _pack(dict(loss=loss_part, rel_bias=g_rel_bias, ln_v_gain=g_lnv_gain, ln_v_bias=g_lnv_bias, w_spatial=g_w_spatial,
                      b_spatial=g_b_spatial, ln1_gain=g_ln1_gain, ln1_bias=g_ln1_bias, b_ff1=g_b_ff1, b_ff2=g_b_ff2,
                      ln2_gain=g_ln2_gain, ln2_bias=g_ln2_bias))
    sib["small"] = lax.empty((N_DEV, *part.shape), F32)
    cut = part.shape[0] // 2 // SUBLANES * SUBLANES

    def small(rows):
        st = _small_stage(part, sib["small"], rows)
        st.store = keep(sib, "small")
        return st

    grad_for_sibling("w_in_top", xb, dproj, [small((0, cut))], half=0, of="w_in")
    grad_for_sibling("w_in_bot", xb, dproj, [small((cut, part.shape[0] - cut)), to_sibling("w_in_top")], half=1, of="w_in")
    parts = sib["small"]
    grad_own("w_in_top", [to_sibling("w_in_bot")])
    grad_own("w_in_bot", reducing(w_in_top=4))

    def add_residual(acc, d_):
        return (acc + ALPHA * d_,)

    hosted = reducing(w_in_top=12, w_in_bot=7)
    (dx,), st = _matmul("d_x", dproj, here("w_in"), "nt", [F32], add_residual, [(dpre1, "mn", 0)], tn=512, tk=3072, stages=hosted)
    settle(hosted, st)
    grad_x = _from_perm(dx)[None]

    out_g, out_d, out_m, out_v = {}, {}, {}, {}
    for n, units in (("w_out", 3), ("w_proj_a", 3), ("w_proj_b", 3), ("w_ff2", 0), ("w_ff1", 0)):
        hosted = reducing(w_in_bot=units) if units else []
        (g, d, nm, nv), st = _adam_shard(f"adam_{n}", pair[n], summed(n), chip, weights[n][0], mom1[n][0], mom2[n][0], stages=hosted)
        settle(hosted, st)
        out_g[n], out_d[n], out_m[n], out_v[n] = g[None], d[None], nm[None], nv[None]
    rows = weights["w_in"].shape[1] // 2
    done = None
    for i, n in enumerate(("w_in_top", "w_in_bot")):
        done, _ = _adam_shard(f"adam_{n}", pair[n], summed(n), chip, weights["w_in"][0], mom1["w_in"][0], mom2["w_in"][0],
                              rows=(i * rows, rows), into=done)
    out_g["w_in"], out_d["w_in"], out_m["w_in"], out_v["w_in"] = (t[None] for t in done)

    zero = jnp.zeros((1,), F32)
    sg, sd, sm, sv = (_unpack(b) for b in _adam_small(
        part, parts, me, _pack({**weights, "loss": zero}), _pack({**mom1, "loss": zero}), _pack({**mom2, "loss": zero})))
    for n in WEIGHT_ORDER:
        if n not in KINDS:
            shape = weights[n].shape
            out_g[n], out_d[n], out_m[n], out_v[n] = (t[n].reshape(shape) for t in (sg, sd, sm, sv))
    loss = sg["loss"].reshape(())
    return (loss, grad_x, *[out_g[n] for n in WEIGHT_ORDER], *[out_d[n] for n in WEIGHT_ORDER],
            *[out_m[n] for n in WEIGHT_ORDER], *[out_v[n] for n in WEIGHT_ORDER])
```

```python
import math

import jax
import jax.numpy as jnp
import numpy as np
from jax import lax
from jax.experimental import pallas as pl
from jax.experimental.pallas import tpu as pltpu

F32 = jnp.float32
BF16 = jnp.bfloat16
I32 = jnp.int32

SEQ = 2048
D_MODEL = 2048
HEAD_DIM = 128
N_HEADS = 8
N_GROUPS = 8
D_FF = 4 * D_MODEL
BLOCK = 128
DILATIONS = (1, 4, 16)
N_BUCKETS = 32
MAX_DISTANCE = 2048
ALPHA = 2.0 ** 0.25
LN_EPS = 1e-5
NEG_INF = -1e30
N_DEV = 8
N_CHIPS = 4
N_SUB = 16
ADAM_LR, ADAM_B1, ADAM_B2, ADAM_EPS, ADAM_WD, ADAM_STEP = 0.001, 0.9, 0.999, 1e-08, 0.01, 10
LANES = 128
SUBLANES = 8
VMEM_LIMIT = 56 * 1024 * 1024
MESH = pl.DeviceIdType.MESH
ANY = pl.BlockSpec(memory_space=pl.ANY)
WEIGHT_ORDER = ("w_in", "rel_bias", "ln_v_gain", "ln_v_bias", "w_spatial", "b_spatial", "w_proj_a", "w_proj_b", "w_out",
                "ln1_gain", "ln1_bias", "w_ff1", "b_ff1", "w_ff2", "b_ff2", "ln2_gain", "ln2_bias")
KINDS = {"w_in": "col", "w_proj_a": "col", "w_proj_b": "col", "w_out": "row", "w_ff1": "col", "w_ff2": "row"}


def _d_a():
    return N_HEADS * HEAD_DIM


def _d_b():
    return N_GROUPS * BLOCK


def _d_in():
    return 3 * _d_a() + 2 * _d_b() + 2 * D_MODEL


def _pick(t, n, *others):
    if n <= t and all(o % n == 0 for o in others):
        return n
    for c in range(min(t, n) // LANES * LANES, 0, -LANES):
        if n % c == 0 and all(o % c == 0 for o in others):
            return c
    raise ValueError((t, n, others))


class _Stage:
    def __init__(self, ins, outs, alias, sems, start, finish):
        self.ins, self.outs, self.alias, self.sems, self.start, self.finish = ins, outs, alias, sems, start, finish


def _call(name, body, grid, in_specs, out_specs, out_shape, operands, scratch=(), sem=None, stages=(), sequential=False, prefetch=None,
          shown=False, alias=None):
    n_in, n_out, n_sc = len(in_specs), len(out_specs), len(scratch)
    st_in = [len(s.ins) for s in stages]
    st_out = [len(s.outs) for s in stages]
    st_sem = [len(s.sems) for s in stages]
    n_pre = 0 if prefetch is None else 1
    aliases, ioff, ooff = {i + n_pre: o for i, o in (alias or {}).items()}, n_in + n_pre, n_out
    for s, ni, no in zip(stages, st_in, st_out):
        for i, o in s.alias.items():
            aliases[ioff + i] = ooff + o
        ioff, ooff = ioff + ni, ooff + no

    def split(refs, counts):
        out, at = [], 0
        for c in counts:
            out.append(refs[at:at + c])
            at += c
        return out

    def wrapped(*refs):
        ins, sins, outs, souts, sc, ssems = split(refs[n_pre:], [n_in, sum(st_in), n_out, sum(st_out), n_sc, sum(st_sem)])
        parts = list(zip(stages, split(sins, st_in), split(souts, st_out), split(ssems, st_sem)))
        if sequential:
            for s, a, b, c in parts:
                s.start(a, b, c)
            for s, a, b, c in parts:
                s.finish(a, b, c)
            return
        if parts:
            first = _all_of([pl.program_id(i) == 0 for i in range(len(grid))])
            last = _all_of([pl.program_id(i) == g - 1 for i, g in enumerate(grid)])

            @pl.when(first)
            def _():
                for s, a, b, c in parts:
                    s.start(a, b, c)

        body(*(refs[:n_pre] if shown else ()), *ins, *outs, *sc)
        if parts:
            @pl.when(last)
            def _():
                for s, a, b, c in parts:
                    s.finish(a, b, c)

    if stages or sem is None:
        sem = ("arbitrary",) * len(grid)
    specs = dict(grid=grid, in_specs=list(in_specs) + [ANY] * sum(st_in), out_specs=list(out_specs) + [ANY] * sum(st_out),
                 scratch_shapes=list(scratch) + [x for s in stages for x in s.sems])
    if prefetch is not None:
        specs = dict(grid_spec=pltpu.PrefetchScalarGridSpec(num_scalar_prefetch=1, **specs))
    res = pl.pallas_call(
        wrapped, name=name, out_shape=list(out_shape) + [o for s in stages for o in s.outs], input_output_aliases=aliases,
        compiler_params=pltpu.CompilerParams(dimension_semantics=sem, vmem_limit_bytes=VMEM_LIMIT), **specs,
    )(*([prefetch] if n_pre else []), *operands, *[a for s in stages for a in s.ins])
    res = list(res)
    return res[:n_out], split(res[n_out:], st_out)


def _all_of(conds):
    out = conds[0]
    for c in conds[1:]:
        out = out & c
    return out


def _coords():
    return lax.axis_index("x"), lax.axis_index("y"), lax.axis_index("c")


def _other_chips(x, y):
    return ((1 - x, y), (x, 1 - y), (1 - x, 1 - y))


def _lin(dev):
    return 4 * dev[0] + 2 * dev[1] + dev[2]


def _piece(total, lo, n, units=16):
    assert total % units == 0
    return lo * (total // units), n * (total // units)


FLOWS = 4


def _split(lo, cnt):
    k = next(k for k in (FLOWS, 2, 1) if cnt % (2 * SUBLANES * k) == 0)
    return [(lo + i * (cnt // k), cnt // k) for i in range(k)]


def _remote(src, dst, send, recv, to):
    return pltpu.make_async_remote_copy(src_ref=src, dst_ref=dst, send_sem=send, recv_sem=recv, device_id=to, device_id_type=MESH)


def _placer(kind, n, lo, cnt):
    def place(ref, dev):
        if kind == "col":
            return ref.at[pl.ds(lo, cnt), pl.ds(pl.multiple_of(_lin(dev) * n, LANES), n)]
        return ref.at[pl.ds(pl.multiple_of(_lin(dev) * n + lo, 2 * SUBLANES), cnt), :]
    return place


def _spread_stage(full, kind, piece=(0, 16), home=False):
    n = (full.shape[1] if kind == "col" else full.shape[0]) // N_DEV
    lo, cnt = _piece(full.shape[0] if kind == "col" else n, *piece)
    parts = _split(lo, cnt)
    npeers = 1 if home else 2

    def copies(ins, outs, sems):
        send, recv = sems
        x, y, c = _coords()
        me = (x, y, c)
        peers = [(x, y, 1 - c)] if home else [(1 - x, y, c), (x, 1 - y, c)]
        out, arrive = [], []
        for k, t in enumerate(peers):
            for i, (plo, pcnt) in enumerate(parts):
                place = _placer(kind, n, plo, pcnt)
                out.append(_remote(place(outs[0], me), place(outs[0], me), send.at[i, k], recv.at[i, k], t))
                arrive.append(_remote(place(outs[0], t), place(outs[0], t), send.at[i, k], recv.at[i, k], t))
        return out, arrive

    def start(ins, outs, sems):
        for cp in copies(ins, outs, sems)[0]:
            cp.start()

    def finish(ins, outs, sems):
        out, arrive = copies(ins, outs, sems)
        for cp in arrive:
            cp.wait_recv()
        for cp in out:
            cp.wait_send()

    return _Stage([full], [jax.ShapeDtypeStruct(full.shape, full.dtype)], {0: 0},
                  [pltpu.SemaphoreType.DMA((len(parts), npeers)), pltpu.SemaphoreType.DMA((len(parts), npeers))], start, finish)


def _relay_stage(full, kind, piece=(0, 16)):
    n = (full.shape[1] if kind == "col" else full.shape[0]) // N_DEV
    lo, cnt = _piece(full.shape[0] if kind == "col" else n, *piece)
    half = cnt // 2
    assert half % (2 * SUBLANES) == 0, (cnt, kind)
    tops, bottoms = _split(lo, half), _split(lo + half, half)

    def copies(ins, outs, sems):
        send, recv = sems
        x, y, c = _coords()
        xn, yn, dg = (1 - x, y, c), (x, 1 - y, c), (1 - x, 1 - y, c)
        out, arrive, k = [], [], 0
        for came_from, to, parts in ((yn, xn, tops), (xn, yn, bottoms)):
            for plo, pcnt in parts:
                place = _placer(kind, n, plo, pcnt)
                out.append(_remote(place(outs[0], came_from), place(outs[0], came_from), send.at[k], recv.at[k], to))
                arrive.append(_remote(place(outs[0], dg), place(outs[0], dg), send.at[k], recv.at[k], to))
                k += 1
        return out, arrive

    def start(ins, outs, sems):
        for cp in copies(ins, outs, sems)[0]:
            cp.start()

    def finish(ins, outs, sems):
        out, arrive = copies(ins, outs, sems)
        for cp in arrive:
            cp.wait_recv()
        for cp in out:
            cp.wait_send()

    return _Stage([full], [jax.ShapeDtypeStruct(full.shape, full.dtype)], {0: 0},
                  [pltpu.SemaphoreType.DMA((len(tops) + len(bottoms),)), pltpu.SemaphoreType.DMA((len(tops) + len(bottoms),))], start, finish)


def _forward_stage(full, kind, piece, far):
    n = (full.shape[1] if kind == "col" else full.shape[0]) // N_DEV
    lo, cnt = _piece(full.shape[0] if kind == "col" else n, *piece)
    place = _placer(kind, n, lo, cnt)

    def copies(ins, outs, sems):
        send, recv = sems
        x, y, c = _coords()
        chips = _other_chips(x, y)[2:] if far else _other_chips(x, y)[:2]
        out = [_remote(place(outs[0], (*chip, c)), place(outs[0], (*chip, c)), send.at[k], recv.at[k], (x, y, 1 - c)) for k, chip in enumerate(chips)]
        arrive = [_remote(place(outs[0], (*chip, 1 - c)), place(outs[0], (*chip, 1 - c)), send.at[k], recv.at[k], (x, y, 1 - c))
                  for k, chip in enumerate(chips)]
        return out, arrive

    def start(ins, outs, sems):
        for cp in copies(ins, outs, sems)[0]:
            cp.start()

    def finish(ins, outs, sems):
        out, arrive = copies(ins, outs, sems)
        for cp in arrive:
            cp.wait_recv()
        for cp in out:
            cp.wait_send()

    return _Stage([full], [jax.ShapeDtypeStruct(full.shape, full.dtype)], {0: 0},
                  [pltpu.SemaphoreType.DMA((3,)), pltpu.SemaphoreType.DMA((3,))], start, finish)


def _to_sibling_stage(theirs):
    def copies(ins, outs, sems):
        send, recv = sems
        x, y, c = _coords()
        return [_remote(ins[0].at[q], outs[0].at[q], send.at[q], recv.at[q], (x, y, 1 - c)) for q in range(N_CHIPS)]

    def start(ins, outs, sems):
        for cp in copies(ins, outs, sems):
            cp.start()

    def finish(ins, outs, sems):
        for cp in copies(ins, outs, sems):
            cp.wait()

    return _Stage([theirs], [jax.ShapeDtypeStruct(theirs.shape, BF16)], {},
                  [pltpu.SemaphoreType.DMA((N_CHIPS,)), pltpu.SemaphoreType.DMA((N_CHIPS,))], start, finish)


def _to_chips_stage(pair, dst, piece=(0, 16)):
    lo, cnt = _piece(pair.shape[1], *piece)
    parts = _split(lo, cnt)
    nsem = 3 * len(parts)

    def copies(ins, outs, sems):
        send, recv = sems
        x, y, c = _coords()
        mine = 2 * x + y
        out, arrive, k = [], [], 0
        for px, py in _other_chips(x, y):
            for plo, pcnt in parts:
                rows = pl.ds(plo, pcnt)
                out.append(_remote(ins[0].at[2 * px + py, rows, :], outs[0].at[mine, rows, :], send.at[k], recv.at[k], (px, py, c)))
                arrive.append(_remote(ins[0].at[2 * px + py, rows, :], outs[0].at[2 * px + py, rows, :], send.at[k], recv.at[k], (px, py, c)))
                k += 1
        return out, arrive

    def start(ins, outs, sems):
        for cp in copies(ins, outs, sems)[0]:
            cp.start()

    def finish(ins, outs, sems):
        out, arrive = copies(ins, outs, sems)
        for cp in arrive:
            cp.wait_recv()
        for cp in out:
            cp.wait_send()

    return _Stage([pair, dst], [jax.ShapeDtypeStruct(dst.shape, dst.dtype)], {1: 0},
                  [pltpu.SemaphoreType.DMA((nsem,)), pltpu.SemaphoreType.DMA((nsem,))], start, finish)


def _fuse(parts, ins, outs, alias):
    parts = [p for p in parts if p is not None]
    sems = [x for st, _, _ in parts for x in st.sems]

    def run(which):
        def go(i, o, s):
            refs, at = list(i) + list(o), 0
            for st, pi, po in parts:
                getattr(st, which)([refs[k] for k in pi], [refs[k] for k in po], s[at:at + len(st.sems)])
                at += len(st.sems)
        return go

    return _Stage(ins, [jax.ShapeDtypeStruct(o.shape, o.dtype) for o in outs], alias, sems, run("start"), run("finish"))


def _gather_stage(full, kind, new=None, relay=None, forward=None):
    return _fuse([(_spread_stage(full, kind, new), [0], [1]) if new else None,
                  (_relay_stage(full, kind, relay), [0], [1]) if relay else None,
                  (_spread_stage(full, kind, relay, home=True), [0], [1]) if relay else None,
                  (_forward_stage(full, kind, relay, False), [0], [1]) if relay else None,
                  (_forward_stage(full, kind, forward, True), [0], [1]) if forward else None], [full], [full], {0: 0})


def _small_stage(part, dst, rows):
    piece = pl.ds(*rows)

    def copies(ins, outs, sems):
        send, recv = sems
        x, y, c = _coords()
        me = (x, y, c)
        peers = [(1 - x if k & 4 else x, 1 - y if k & 2 else y, 1 - c if k & 1 else c) for k in range(1, N_DEV)]
        out = [_remote(ins[0].at[piece, :], outs[0].at[_lin(me), piece, :], send.at[k], recv.at[k], t) for k, t in enumerate(peers)]
        arrive = [_remote(ins[0].at[piece, :], outs[0].at[_lin(t), piece, :], send.at[k], recv.at[k], t) for k, t in enumerate(peers)]
        return out, arrive

    def start(ins, outs, sems):
        for cp in copies(ins, outs, sems)[0]:
            cp.start()

    def finish(ins, outs, sems):
        out, arrive = copies(ins, outs, sems)
        for cp in arrive:
            cp.wait_recv()
        for cp in out:
            cp.wait_send()

    return _Stage([part, dst], [jax.ShapeDtypeStruct(dst.shape, F32)], {1: 0},
                  [pltpu.SemaphoreType.DMA((N_DEV - 1,)), pltpu.SemaphoreType.DMA((N_DEV - 1,))], start, finish)


def _comm_only(name, stages):
    return _call(name, lambda: None, (1,), [], [], [], [], stages=stages, sequential=True)[1]


_GELU_C = math.sqrt(2.0 / math.pi)


def _gelu(x):
    return 0.5 * x * (1.0 + jnp.tanh(_GELU_C * (x + 0.044715 * x * x * x)))


def _gelu_grad(x):
    t = jnp.tanh(_GELU_C * (x + 0.044715 * x * x * x))
    return 0.5 * (1.0 + t) + 0.5 * x * (1.0 - t * t) * (_GELU_C * (1.0 + 3.0 * 0.044715 * x * x))


def _sigmoid(x):
    return 1.0 / (1.0 + jnp.exp(-x))


def _dot(a, b, mode):
    dims = {"nn": (((1,), (0,)), ((), ())), "nt": (((1,), (1,)), ((), ())), "tn": (((0,), (0,)), ((), ()))}[mode]
    return lax.dot_general(a.astype(BF16), b.astype(BF16), dims, preferred_element_type=F32)


def _matmul(name, a, b, mode, outs, epi=None, extras=(), colsums=(), tm=2048, tn=512, tk=2048, b_off=0, n=None, m_off=0, m=None, stages=()):
    if mode == "tn":
        kk, mfull = a.shape
    else:
        mfull, kk = a.shape
    m = mfull if m is None else m
    n = (b.shape[0] if mode == "nt" else b.shape[1]) if n is None else n
    tm, tk = _pick(tm, m, m_off), _pick(tk, kk)
    tn = _pick(tn, n, b_off, *[off for _, _, off in extras])
    boff, moff = b_off // tn, m_off // tm
    nm, nn_, nk = m // tm, n // tn, kk // tk
    col_major = bool(colsums)
    grid = (nn_, nm, nk) if col_major else (nm, nn_, nk)

    def imap(f):
        if col_major:
            return lambda g0, g1, k: f(g1, g0, k)
        return f

    a_spec = (pl.BlockSpec((tk, tm), imap(lambda i, j, k: (k, i + moff))) if mode == "tn"
              else pl.BlockSpec((tm, tk), imap(lambda i, j, k: (i + moff, k))))
    b_spec = (pl.BlockSpec((tn, tk), imap(lambda i, j, k: (j + boff, k))) if mode == "nt"
              else pl.BlockSpec((tk, tn), imap(lambda i, j, k: (k, j + boff))))
    in_specs, operands = [a_spec, b_spec], [a, b]
    for arr, kind, off in extras:
        o = off // tn
        if kind == "mn":
            in_specs.append(pl.BlockSpec((tm, tn), imap(lambda i, j, k, o=o: (i + moff, j + o))))
        else:
            in_specs.append(pl.BlockSpec((1, tn), imap(lambda i, j, k, o=o: (0, j + o))))
        operands.append(arr)
    out_shape = [jax.ShapeDtypeStruct((m, n), dt) for dt in outs] + [jax.ShapeDtypeStruct((1, n), F32) for _ in colsums]
    out_specs = ([pl.BlockSpec((tm, tn), imap(lambda i, j, k: (i, j))) for _ in outs]
                 + [pl.BlockSpec((1, tn), imap(lambda i, j, k: (0, j))) for _ in colsums])
    n_ex, n_out, n_cs = len(extras), len(outs), len(colsums)

    def body(*refs):
        a_ref, b_ref = refs[:2]
        ex_refs = refs[2:2 + n_ex]
        out_refs = refs[2 + n_ex:2 + n_ex + n_out]
        cs_refs = refs[2 + n_ex + n_out:2 + n_ex + n_out + n_cs]
        part = _dot(a_ref[...], b_ref[...], mode)

        def finish(acc):
            res = epi(acc, *[r[...] for r in ex_refs]) if epi is not None else (acc,)
            for r, v in zip(out_refs, res[:n_out]):
                r[...] = v.astype(r.dtype)
            if n_cs:
                @pl.when(pl.program_id(1) == 0)
                def _():
                    for r in cs_refs:
                        r[...] = jnp.zeros_like(r)

                for r, idx in zip(cs_refs, colsums):
                    r[...] += jnp.sum(res[idx], axis=0, keepdims=True)

        if nk == 1:
            finish(part)
        else:
            acc_ref = refs[-1]
            k = pl.program_id(2)

            @pl.when(k == 0)
            def _():
                acc_ref[...] = part

            @pl.when(k > 0)
            def _():
                acc_ref[...] += part

            @pl.when(k == nk - 1)
            def _():
                finish(acc_ref[...])

    sem = ("arbitrary", "arbitrary", "arbitrary") if col_major else ("parallel", "parallel", "arbitrary")
    return _call(name, body, grid, in_specs, out_specs, out_shape, operands,
                 scratch=[pltpu.VMEM((tm, tn), F32)] if nk > 1 else [], sem=sem, stages=stages)


def _row_spec(tr, c):
    return pl.BlockSpec((tr, c), lambda i: (i, 0))


def _fix_spec(shape):
    return pl.BlockSpec(shape, lambda *_: tuple(0 for _ in shape))


def _perm_in(name, x, target, tc=LANES, stages=()):
    s, d = x.shape
    tc = _pick(tc, d)
    rows = s // N_SUB

    def body(x_ref, t_ref, xs_ref, xb_ref, ts_ref):
        for r in range(N_SUB):
            piece = pl.ds(r * rows, rows)
            v = x_ref[pl.ds(r, rows, stride=N_SUB), :]
            xs_ref[piece, :] = v
            xb_ref[piece, :] = v.astype(BF16)
            ts_ref[piece, :] = t_ref[pl.ds(r, rows, stride=N_SUB), :]

    col = pl.BlockSpec((s, tc), lambda j: (0, j))
    (xs, xb, ts), st = _call(name, body, (d // tc,), [col, col], [col, col, col],
                             [jax.ShapeDtypeStruct((s, d), F32), jax.ShapeDtypeStruct((s, d), BF16), jax.ShapeDtypeStruct((s, d), F32)],
                             [x, target], sem=("parallel",), stages=stages)
    return xs, xb, ts, st


def _cast_into_place(name, w, kind, me, tr=512, stages=()):
    r, c = w.shape
    tr = _pick(tr, r)
    nb = r // tr
    if kind == "col":
        o_spec = pl.BlockSpec((tr, c), lambda i, me_ref: (i, me_ref[0]))
        shape = (r, c * N_DEV)
    else:
        o_spec = pl.BlockSpec((tr, c), lambda i, me_ref: (me_ref[0] * nb + i, 0))
        shape = (r * N_DEV, c)

    def body(x_ref, o_ref):
        o_ref[...] = x_ref[...].astype(BF16)

    (out,), st = _call(name, body, (nb,), [pl.BlockSpec((tr, c), lambda i, me_ref: (i, 0))], [o_spec], [jax.ShapeDtypeStruct(shape, BF16)],
                       [w], sem=("parallel",), prefetch=me, stages=stages)
    return out, st


def _layer_norm_stats(x):
    mean = jnp.mean(x, axis=-1, keepdims=True)
    xc = x - mean
    var = jnp.mean(xc * xc, axis=-1, keepdims=True)
    rstd = lax.rsqrt(var + LN_EPS)
    return xc * rstd, rstd


def _layer_norm_bwd(dxhat, xhat, rstd):
    m1 = jnp.mean(dxhat, axis=-1, keepdims=True)
    m2 = jnp.mean(dxhat * xhat, axis=-1, keepdims=True)
    return rstd * (dxhat - m1 - xhat * m2)


def _ln1_fwd(pre1, g1, b1, tr=256, stages=()):
    s, d = pre1.shape
    tr = _pick(tr, s)

    def body(p_ref, g_ref, b_ref, xh_ref, rs_ref, h_ref):
        xhat, rstd = _layer_norm_stats(p_ref[...])
        xh_ref[...] = xhat
        rs_ref[...] = rstd
        h_ref[...] = (xhat * g_ref[...] + b_ref[...]).astype(BF16)

    return _call("ln1_fwd", body, (s // tr,), [_row_spec(tr, d), _fix_spec((1, d)), _fix_spec((1, d))],
                 [_row_spec(tr, d), _row_spec(tr, 1), _row_spec(tr, d)],
                 [jax.ShapeDtypeStruct((s, d), F32), jax.ShapeDtypeStruct((s, 1), F32), jax.ShapeDtypeStruct((s, d), BF16)],
                 [pre1, g1, b1], sem=("parallel",), stages=stages)


def _ln2_loss_bwd(ff, xhat1, g1, b1, g2, b2, target, tr=256):
    s, d = ff.shape
    tr = _pick(tr, s)

    def body(ff_ref, xh1_ref, g1_ref, b1_ref, g2_ref, b2_ref, t_ref, dp_ref, dpb_ref, dg_ref, db_ref, dbf_ref, loss_ref):
        @pl.when(pl.program_id(0) == 0)
        def _():
            dg_ref[...] = jnp.zeros_like(dg_ref)
            db_ref[...] = jnp.zeros_like(db_ref)
            dbf_ref[...] = jnp.zeros_like(dbf_ref)
            loss_ref[...] = jnp.zeros_like(loss_ref)

        h1 = xh1_ref[...] * g1_ref[...] + b1_ref[...]
        xhat, rstd = _layer_norm_stats(ALPHA * h1 + ff_ref[...])
        err = xhat * g2_ref[...] + b2_ref[...] - t_ref[...]
        row = jnp.mean(err * err, axis=-1, keepdims=True)
        loss_ref[...] += 0.5 * jnp.sum(row, axis=0, keepdims=True)
        dy = err / d
        dg_ref[...] += jnp.sum(dy * xhat, axis=0, keepdims=True)
        db_ref[...] += jnp.sum(dy, axis=0, keepdims=True)
        dpre = _layer_norm_bwd(dy * g2_ref[...], xhat, rstd)
        dbf_ref[...] += jnp.sum(dpre, axis=0, keepdims=True)
        dp_ref[...] = dpre
        dpb_ref[...] = dpre.astype(BF16)

    vec = _fix_spec((1, d))
    return _call("ln2_loss_bwd", body, (s // tr,), [_row_spec(tr, d), _row_spec(tr, d), vec, vec, vec, vec, _row_spec(tr, d)],
                 [_row_spec(tr, d), _row_spec(tr, d), vec, vec, vec, _fix_spec((1, 1))],
                 [jax.ShapeDtypeStruct((s, d), F32), jax.ShapeDtypeStruct((s, d), BF16)]
                 + [jax.ShapeDtypeStruct((1, d), F32)] * 3 + [jax.ShapeDtypeStruct((1, 1), F32)],
                 [ff, xhat1, g1, b1, g2, b2, target])[0]


def _ln1_bwd(dh1, xhat1, rstd1, g1, tr=256, stages=()):
    s, d = dh1.shape
    tr = _pick(tr, s)

    def body(dh_ref, xh_ref, rs_ref, g_ref, dp_ref, dpb_ref, dg_ref, db_ref):
        @pl.when(pl.program_id(0) == 0)
        def _():
            dg_ref[...] = jnp.zeros_like(dg_ref)
            db_ref[...] = jnp.zeros_like(db_ref)

        dh, xhat = dh_ref[...], xh_ref[...]
        dg_ref[...] += jnp.sum(dh * xhat, axis=0, keepdims=True)
        db_ref[...] += jnp.sum(dh, axis=0, keepdims=True)
        dpre = _layer_norm_bwd(dh * g_ref[...], xhat, rs_ref[...])
        dp_ref[...] = dpre
        dpb_ref[...] = dpre.astype(BF16)

    vec = _fix_spec((1, d))
    return _call("ln1_bwd", body, (s // tr,), [_row_spec(tr, d), _row_spec(tr, d), _row_spec(tr, 1), vec],
                 [_row_spec(tr, d), _row_spec(tr, d), vec, vec],
                 [jax.ShapeDtypeStruct((s, d), F32), jax.ShapeDtypeStruct((s, d), BF16)] + [jax.ShapeDtypeStruct((1, d), F32)] * 2,
                 [dh1, xhat1, rstd1, g1], stages=stages)


def _from_perm(x):
    return x.reshape(N_SUB, SEQ // N_SUB, -1).transpose(1, 0, 2).reshape(SEQ, -1)


def _local_index(p):
    rho = np.arange(BLOCK)
    if p == 0:
        return 16 * (rho % 8) + rho // 8
    if p == 1:
        return 4 * (rho % 32) + rho // 32
    return rho


def _tile_view(x, p):
    c = x.shape[1]
    if p == 1:
        return x.reshape(4, 4, BLOCK, c)
    return x.reshape(N_SUB, BLOCK, c)


def _view_shape(c, p):
    return (4, 4, BLOCK, c) if p == 1 else (N_SUB, BLOCK, c)


def _tile_spec(p, width, col, shift=0):
    nblk = SEQ // DILATIONS[p] // BLOCK

    def blk(n):
        return jnp.clip(n + shift, 0, nblk - 1)

    if p == 0:
        return pl.BlockSpec((N_SUB, SUBLANES, width), lambda s, n: (0, blk(n), col))
    if p == 1:
        return pl.BlockSpec((4, None, 32, width), lambda s, n: (0, s, blk(n), col))
    return pl.BlockSpec((None, BLOCK, width), lambda s, n: (s, 0, col))


def _tile_grid(p):
    return ((1, 16), (4, 4), (16, 1))[p]


def _t5_bucket(n):
    max_exact = N_BUCKETS // 2
    nf = np.maximum(n, 1).astype(np.float32)
    large = max_exact + (np.log(nf / np.float32(max_exact)) / np.float32(math.log(MAX_DISTANCE / max_exact))
                         * np.float32(N_BUCKETS - max_exact)).astype(np.int32)
    large = np.minimum(large, N_BUCKETS - 1)
    return np.where(n < max_exact, n, large).astype(np.int32)


def _bucket_tables():
    tabs = np.zeros((3, 2, BLOCK, BLOCK), np.int32)
    for p, d in enumerate(DILATIONS):
        i = _local_index(p)
        diff = i[:, None] - i[None, :]
        tabs[p, 0] = np.where(diff <= 0, _t5_bucket((BLOCK + diff) * d), -1)
        tabs[p, 1] = np.where(diff >= 0, _t5_bucket(np.maximum(diff, 0) * d), -1)
    return tabs


def _bias_expand(rel_bias, buckets, stages=()):
    nh = N_HEADS

    def body(rb_ref, bk_ref, o_ref):
        for w in range(2):
            bk = bk_ref[0, w]
            for h in range(nh):
                val = jnp.zeros((BLOCK, BLOCK), F32)
                for b in range(N_BUCKETS):
                    val = jnp.where(bk == b, rb_ref[b, h], val)
                o_ref[0, h, w] = jnp.where(bk < 0, NEG_INF, val)

    return _call("bias_expand", body, (3,),
                 [pl.BlockSpec(memory_space=pltpu.SMEM), pl.BlockSpec((1, 2, BLOCK, BLOCK), lambda p: (p, 0, 0, 0))],
                 [pl.BlockSpec((1, nh, 2, BLOCK, BLOCK), lambda p: (p, 0, 0, 0, 0))],
                 [jax.ShapeDtypeStruct((3, nh, 2, BLOCK, BLOCK), F32)], [rel_bias, buckets], sem=("parallel",), stages=stages)


def _heads_to_lanes(cols):
    lane = lax.broadcasted_iota(I32, (BLOCK, LANES), 1)
    out = jnp.zeros((BLOCK, LANES), F32)
    for h, c in enumerate(cols):
        out = jnp.where(lane == h, c, out)
    return out


def _attn_fwd(qkv, bias, p, stages=()):
    d_a = _d_a()
    has_prev = SEQ // DILATIONS[p] // BLOCK > 1
    scale = HEAD_DIM ** -0.5
    view = _tile_view(qkv, p)

    width = 2 * BLOCK if has_prev else BLOCK

    def body(q_ref, kc_ref, kp_ref, vc_ref, vp_ref, b_ref, o_ref, l_ref, s_ref, p_ref):
        n = pl.program_id(1)
        q_all = q_ref[...].reshape(BLOCK, d_a).astype(BF16)
        k_all = kc_ref[...].reshape(BLOCK, d_a).astype(BF16)
        v_all = vc_ref[...].reshape(BLOCK, d_a).astype(BF16)
        if has_prev:
            k_all = jnp.concatenate([kp_ref[...].reshape(BLOCK, d_a).astype(BF16), k_all], axis=0)
            v_all = jnp.concatenate([vp_ref[...].reshape(BLOCK, d_a).astype(BF16), v_all], axis=0)
            no_prev = (lax.broadcasted_iota(I32, (BLOCK, width), 1) < BLOCK) & (n == 0)
        for h in range(N_HEADS):
            sl = slice(h * HEAD_DIM, (h + 1) * HEAD_DIM)
            s = _dot(q_all[:, sl], k_all[:, sl], "nt") * scale
            if has_prev:
                s = jnp.where(no_prev, NEG_INF, s + jnp.concatenate([b_ref[0, h, 0], b_ref[0, h, 1]], axis=1))
            else:
                s = s + b_ref[0, h, 1]
            s_ref[h] = s
        dens, lses = [], []
        for h in range(N_HEADS):
            s = s_ref[h]
            m = jnp.max(s, axis=-1, keepdims=True)
            pr = jnp.exp(s - m)
            den = jnp.sum(pr, axis=-1, keepdims=True)
            p_ref[h] = pr.astype(BF16)
            dens.append(den)
            lses.append(m + jnp.log(den))
        for h in range(N_HEADS):
            sl = slice(h * HEAD_DIM, (h + 1) * HEAD_DIM)
            o_ref[..., sl] = (_dot(p_ref[h], v_all[:, sl], "nn") / dens[h]).reshape(*o_ref.shape[:-1], HEAD_DIM)
        l_ref[...] = _heads_to_lanes(lses).reshape(l_ref.shape)

    (o, l), st = _call(
        f"attn_fwd{p}", body, _tile_grid(p),
        [_tile_spec(p, d_a, 0), _tile_spec(p, d_a, 1), _tile_spec(p, d_a, 1, -1), _tile_spec(p, d_a, 2), _tile_spec(p, d_a, 2, -1),
         pl.BlockSpec((1, N_HEADS, 2, BLOCK, BLOCK), lambda s, n: (p, 0, 0, 0, 0))],
        [_tile_spec(p, d_a, 0), _tile_spec(p, LANES, 0)],
        [jax.ShapeDtypeStruct(_view_shape(d_a, p), F32), jax.ShapeDtypeStruct(_view_shape(LANES, p), F32)],
        [view, view, view, view, view, bias], scratch=[pltpu.VMEM((N_HEADS, BLOCK, width), F32), pltpu.VMEM((N_HEADS, BLOCK, width), BF16)],
        sem=("parallel", "parallel"), stages=stages)
    return (o.reshape(SEQ, d_a), l.reshape(SEQ, LANES)), st


def _attn_combine(os_, ls_, tr=256, stages=()):
    d_a = _d_a()
    tr = _pick(tr, SEQ)

    def body(o0, o1, o2, l0, l1, l2, a_ref, ab_ref, lt_ref):
        l = [l0[...], l1[...], l2[...]]
        m = jnp.maximum(jnp.maximum(l[0], l[1]), l[2])
        w = [jnp.exp(x - m) for x in l]
        tot = w[0] + w[1] + w[2]
        lt_ref[...] = m + jnp.log(tot)
        w = [x / tot for x in w]
        for h in range(N_HEADS):
            sl = slice(h * HEAD_DIM, (h + 1) * HEAD_DIM)
            acc = w[0][:, h:h + 1] * o0[:, sl] + w[1][:, h:h + 1] * o1[:, sl] + w[2][:, h:h + 1] * o2[:, sl]
            a_ref[:, sl] = acc
            ab_ref[:, sl] = acc.astype(BF16)

    return _call("attn_combine", body, (SEQ // tr,), [_row_spec(tr, d_a)] * 3 + [_row_spec(tr, LANES)] * 3,
                 [_row_spec(tr, d_a), _row_spec(tr, d_a), _row_spec(tr, LANES)],
                 [jax.ShapeDtypeStruct((SEQ, d_a), F32), jax.ShapeDtypeStruct((SEQ, d_a), BF16), jax.ShapeDtypeStruct((SEQ, LANES), F32)],
                 [*os_, *ls_], sem=("parallel",), stages=stages)


def _attn_delta(dattn, attn, tr=256):
    d_a = _d_a()
    tr = _pick(tr, SEQ)

    def body(d_ref, a_ref, o_ref):
        prod = d_ref[...] * a_ref[...]
        lane = lax.broadcasted_iota(I32, (tr, LANES), 1)
        out = jnp.zeros((tr, LANES), F32)
        for h in range(N_HEADS):
            out = jnp.where(lane == h, jnp.sum(prod[:, h * HEAD_DIM:(h + 1) * HEAD_DIM], axis=-1, keepdims=True), out)
        o_ref[...] = out

    return _call("attn_delta", body, (SEQ // tr,), [_row_spec(tr, d_a)] * 2, [_row_spec(tr, LANES)],
                 [jax.ShapeDtypeStruct((SEQ, LANES), F32)], [dattn, attn], sem=("parallel",))[0][0]


def _attn_bwd(qkv, dattn, lse, delta, bias, p, stages=()):
    d_a = _d_a()
    nblk = SEQ // DILATIONS[p] // BLOCK
    has_next = nblk > 1
    scale = HEAD_DIM ** -0.5
    qv, dov, lv, tv = (_tile_view(x, p) for x in (qkv, dattn, lse, delta))

    rows = 2 * BLOCK if has_next else BLOCK

    def body(q_ref, qn_ref, k_ref, v_ref, do_ref, don_ref, l_ref, ln_ref, t_ref, tn_ref, b_ref, dq_ref, dk_ref, dv_ref, db_ref,
             carry_ref, s_ref, dp_ref, p_ref, ds_ref):
        j = pl.program_id(1)

        @pl.when((pl.program_id(0) == 0) & (j == 0))
        def _():
            db_ref[...] = jnp.zeros_like(db_ref)

        def both(cur, nxt, width, dtype):
            cur = cur[...].reshape(BLOCK, width).astype(dtype)
            return jnp.concatenate([cur, nxt[...].reshape(BLOCK, width).astype(dtype)], axis=0) if has_next else cur

        k_all = k_ref[...].reshape(BLOCK, d_a).astype(BF16)
        v_all = v_ref[...].reshape(BLOCK, d_a).astype(BF16)
        q_all, do_all = both(q_ref, qn_ref, d_a, BF16), both(do_ref, don_ref, d_a, BF16)
        l_all, t_all = both(l_ref, ln_ref, LANES, F32), both(t_ref, tn_ref, LANES, F32)
        if has_next:
            no_next = (lax.broadcasted_iota(I32, (rows, BLOCK), 0) >= BLOCK) & (j == nblk - 1)
        for h in range(N_HEADS):
            sl = slice(h * HEAD_DIM, (h + 1) * HEAD_DIM)
            s = _dot(q_all[:, sl], k_all[:, sl], "nt") * scale
            if has_next:
                s = jnp.where(no_next, NEG_INF, s + jnp.concatenate([b_ref[0, h, 1], b_ref[0, h, 0]], axis=0))
            else:
                s = s + b_ref[0, h, 1]
            s_ref[h] = s
            dp_ref[h] = _dot(do_all[:, sl], v_all[:, sl], "nt")
        for h in range(N_HEADS):
            pr = jnp.exp(s_ref[h] - l_all[:, h:h + 1])
            ds = pr * (dp_ref[h] - t_all[:, h:h + 1])
            db_ref[h, 1] += ds[:BLOCK]
            if has_next:
                db_ref[h, 0] += ds[BLOCK:]
            p_ref[h] = pr.astype(BF16)
            ds_ref[h] = ds.astype(BF16)
        for h in range(N_HEADS):
            sl = slice(h * HEAD_DIM, (h + 1) * HEAD_DIM)
            dq = _dot(ds_ref[h], k_all[:, sl], "nn") * scale
            mine = dq[:BLOCK]
            if has_next:
                mine = mine + jnp.where(j > 0, carry_ref[:, sl], 0.0)
            dq_ref[..., sl] = mine.reshape(*dq_ref.shape[:-1], HEAD_DIM)
            if has_next:
                carry_ref[:, sl] = dq[BLOCK:]
            dk_ref[..., sl] = (_dot(ds_ref[h], q_all[:, sl], "tn") * scale).reshape(*dk_ref.shape[:-1], HEAD_DIM)
            dv_ref[..., sl] = _dot(p_ref[h], do_all[:, sl], "tn").reshape(*dv_ref.shape[:-1], HEAD_DIM)

    def big(col, shift=0):
        return _tile_spec(p, d_a, col, shift)

    def small(shift=0):
        return _tile_spec(p, LANES, 0, shift)

    (dq, dk, dv, dbias), st = _call(
        f"attn_bwd{p}", body, _tile_grid(p),
        [big(0), big(0, 1), big(1), big(2), big(0), big(0, 1), small(), small(1), small(), small(1),
         pl.BlockSpec((1, N_HEADS, 2, BLOCK, BLOCK), lambda s, n: (p, 0, 0, 0, 0))],
        [big(0), big(0), big(0), pl.BlockSpec((N_HEADS, 2, BLOCK, BLOCK), lambda s, n: (0, 0, 0, 0))],
        [jax.ShapeDtypeStruct(_view_shape(d_a, p), F32)] * 3 + [jax.ShapeDtypeStruct((N_HEADS, 2, BLOCK, BLOCK), F32)],
        [qv, qv, qv, qv, dov, dov, lv, lv, tv, tv, bias],
        scratch=[pltpu.VMEM((BLOCK, d_a), F32), pltpu.VMEM((N_HEADS, rows, BLOCK), F32), pltpu.VMEM((N_HEADS, rows, BLOCK), F32),
                 pltpu.VMEM((N_HEADS, rows, BLOCK), BF16), pltpu.VMEM((N_HEADS, rows, BLOCK), BF16)], stages=stages)
    return (dq.reshape(SEQ, d_a), dk.reshape(SEQ, d_a), dv.reshape(SEQ, d_a), dbias), st


def _rel_bias_grad(dbias, buckets):
    nh = N_HEADS

    def body(d0, d1, d2, bk_ref, o_ref, t_ref):
        ds = (d0, d1, d2)

        def per_bucket(b, carry):
            for h in range(nh):
                acc = jnp.zeros((BLOCK, BLOCK), F32)
                for p in range(3):
                    for w in range(2):
                        acc = acc + jnp.where(bk_ref[p, w] == b, ds[p][h, w], 0.0)
                t_ref[pl.ds(b * nh + h, 1), :] = jnp.sum(acc, axis=0, keepdims=True)
            return carry

        lax.fori_loop(0, N_BUCKETS, per_bucket, 0)
        o_ref[...] = jnp.sum(t_ref[...], axis=-1, keepdims=True)

    return _call("rel_bias_grad", body, (1,), [_fix_spec((nh, 2, BLOCK, BLOCK))] * 3 + [_fix_spec((3, 2, BLOCK, BLOCK))],
                 [_fix_spec((N_BUCKETS * nh, 1))], [jax.ShapeDtypeStruct((N_BUCKETS * nh, 1), F32)], [*dbias, buckets],
                 scratch=[pltpu.VMEM((N_BUCKETS * nh, LANES), F32)])[0][0]


def _gmlp_fwd(rest, col0, gain, bias, ws, bs, causal, stages=()):
    d_b = _d_b()

    def body(u_ref, v_ref, g_ref, b_ref, ws_ref, bs_ref, c_ref, o_ref):
        u = u_ref[...].reshape(BLOCK, d_b)
        xhat, _ = _layer_norm_stats(_gelu(v_ref[...].reshape(BLOCK, d_b)))
        vn = (xhat * g_ref[...] + b_ref[...]).astype(BF16)
        outs = []
        for g in range(N_GROUPS):
            sl = slice(g * BLOCK, (g + 1) * BLOCK)
            w = jnp.where(c_ref[...] > 0, ws_ref[g], 0.0)
            z = _dot(w, vn[:, sl], "nn") + bs_ref[:, g:g + 1]
            outs.append(_gelu(u[:, sl]) * z)
        o_ref[...] = jnp.concatenate(outs, axis=-1).reshape(o_ref.shape)

    (out,), st = _call(
        "gmlp_fwd", body, (1, SEQ // BLOCK),
        [_tile_spec(0, d_b, col0), _tile_spec(0, d_b, col0 + 1), _fix_spec((1, d_b)), _fix_spec((1, d_b)),
         _fix_spec((N_GROUPS, BLOCK, BLOCK)), _fix_spec((BLOCK, N_GROUPS)), _fix_spec((BLOCK, BLOCK))],
        [_tile_spec(0, d_b, 0)], [jax.ShapeDtypeStruct(_view_shape(d_b, 0), F32)],
        [_tile_view(rest, 0), _tile_view(rest, 0), gain, bias, ws, bs, causal], sem=("parallel", "parallel"), stages=stages)
    return out.reshape(SEQ, d_b), st


def _gmlp_bwd(rest, col0, dgmlp, gain, bias, ws, bs, causal, stages=()):
    d_b = _d_b()
    nchunk = SEQ // BLOCK

    def body(u_ref, v_ref, dg_ref, g_ref, b_ref, ws_ref, bs_ref, c_ref, du_ref, dv_ref, dws_ref, dbs_ref, dgain_ref, dbias_ref):
        c = pl.program_id(1)

        @pl.when(c == 0)
        def _():
            dws_ref[...] = jnp.zeros_like(dws_ref)
            dbs_ref[...] = jnp.zeros_like(dbs_ref)
            dgain_ref[...] = jnp.zeros_like(dgain_ref)
            dbias_ref[...] = jnp.zeros_like(dbias_ref)

        u = u_ref[...].reshape(BLOCK, d_b)
        v = v_ref[...].reshape(BLOCK, d_b)
        dgm = dg_ref[...].reshape(BLOCK, d_b)
        xhat, rstd = _layer_norm_stats(_gelu(v))
        vn = (xhat * g_ref[...] + b_ref[...]).astype(BF16)
        lane = lax.broadcasted_iota(I32, (BLOCK, LANES), 1)
        dus, dvns = [], []
        dbs = dbs_ref[...]
        for g in range(N_GROUPS):
            sl = slice(g * BLOCK, (g + 1) * BLOCK)
            w = jnp.where(c_ref[...] > 0, ws_ref[g], 0.0).astype(BF16)
            z = _dot(w, vn[:, sl], "nn") + bs_ref[:, g:g + 1]
            dz = dgm[:, sl] * _gelu(u[:, sl])
            dus.append(dgm[:, sl] * z * _gelu_grad(u[:, sl]))
            dws_ref[g] += _dot(dz, vn[:, sl], "nt")
            dbs = dbs + jnp.where(lane == g, jnp.sum(dz, axis=-1, keepdims=True), 0.0)
            dvns.append(_dot(w, dz, "tn"))
        dbs_ref[...] = dbs
        dvn = jnp.concatenate(dvns, axis=-1)
        dgain_ref[...] += jnp.sum(dvn * xhat, axis=0, keepdims=True)
        dbias_ref[...] += jnp.sum(dvn, axis=0, keepdims=True)
        dvg = _layer_norm_bwd(dvn * g_ref[...], xhat, rstd)
        du_ref[...] = jnp.concatenate(dus, axis=-1).reshape(du_ref.shape)
        dv_ref[...] = (dvg * _gelu_grad(v)).reshape(dv_ref.shape)

        @pl.when(c == nchunk - 1)
        def _():
            for g in range(N_GROUPS):
                dws_ref[g] = jnp.where(c_ref[...] > 0, dws_ref[g], 0.0)

    (du, dv, dws, dbs, dgain, dbias), st = _call(
        "gmlp_bwd", body, (1, nchunk),
        [_tile_spec(0, d_b, col0), _tile_spec(0, d_b, col0 + 1), _tile_spec(0, d_b, 0), _fix_spec((1, d_b)), _fix_spec((1, d_b)),
         _fix_spec((N_GROUPS, BLOCK, BLOCK)), _fix_spec((BLOCK, N_GROUPS)), _fix_spec((BLOCK, BLOCK))],
        [_tile_spec(0, d_b, 0), _tile_spec(0, d_b, 0), _fix_spec((N_GROUPS, BLOCK, BLOCK)), _fix_spec((BLOCK, LANES)),
         _fix_spec((1, d_b)), _fix_spec((1, d_b))],
        [jax.ShapeDtypeStruct(_view_shape(d_b, 0), F32)] * 2
        + [jax.ShapeDtypeStruct((N_GROUPS, BLOCK, BLOCK), F32), jax.ShapeDtypeStruct((BLOCK, LANES), F32)]
        + [jax.ShapeDtypeStruct((1, d_b), F32)] * 2,
        [_tile_view(rest, 0), _tile_view(rest, 0), _tile_view(dgmlp, 0), gain, bias, ws, bs, causal], stages=stages)
    return (du.reshape(SEQ, d_b), dv.reshape(SEQ, d_b), dws, dbs, dgain, dbias), st


def _assemble_dproj(dqkv, du, dv, dga, dgb, tr=128, stages=()):
    d_a, d_b, d_in = _d_a(), _d_b(), _d_in()
    tr = _pick(tr, SEQ)

    def body(*refs):
        att, (du_ref, dv_ref, dga_ref, dgb_ref, o_ref) = refs[:9], refs[9:]
        for i in range(3):
            o_ref[:, i * d_a:(i + 1) * d_a] = (att[3 * i][...] + att[3 * i + 1][...] + att[3 * i + 2][...]).astype(BF16)
        o_ref[:, 3 * d_a:3 * d_a + d_b] = du_ref[...].astype(BF16)
        o_ref[:, 3 * d_a + d_b:3 * d_a + 2 * d_b] = dv_ref[...].astype(BF16)
        o_ref[:, 3 * d_a + 2 * d_b:3 * d_a + 2 * d_b + D_MODEL] = dga_ref[...]
        o_ref[:, 3 * d_a + 2 * d_b + D_MODEL:] = dgb_ref[...]

    return _call("assemble_dproj", body, (SEQ // tr,), [_row_spec(tr, d_a)] * 9 + [_row_spec(tr, d_b)] * 2 + [_row_spec(tr, D_MODEL)] * 2,
                 [_row_spec(tr, d_in)], [jax.ShapeDtypeStruct((SEQ, d_in), BF16)], [*dqkv, du, dv, dga, dgb], sem=("parallel",), stages=stages)


def _dw(name, a, b, kind, core, mine, add=None, tn=1152, half=None, stages=()):
    s, m = a.shape
    n = b.shape[1]
    if half is not None:
        m //= 2
    rs, cs = (m, n // N_DEV) if kind == "col" else (m // N_DEV, n)
    tn = _pick(tn if cs % tn == 0 else 512, cs)
    nj = cs // tn

    def shard(q, c_ref):
        return 2 * q + (c_ref[0] if mine else 1 - c_ref[0])

    if kind == "col":
        a_spec = pl.BlockSpec((s, m), lambda q, j, c_ref: (0, half or 0))
        b_spec = pl.BlockSpec((s, tn), lambda q, j, c_ref: (0, shard(q, c_ref) * nj + j))
    else:
        a_spec = pl.BlockSpec((s, rs), lambda q, j, c_ref: (0, shard(q, c_ref)))
        b_spec = pl.BlockSpec((s, tn), lambda q, j, c_ref: (0, j))
    o_spec = pl.BlockSpec((None, rs, tn), lambda q, j, c_ref: (q, 0, j))

    def body(a_ref, b_ref, *rest):
        acc = _dot(a_ref[...], b_ref[...], "tn")
        if add is not None:
            acc = acc + rest[0][...].astype(F32)
        rest[-1][...] = acc.astype(BF16)

    (out,), st = _call(name, body, (N_CHIPS, nj), [a_spec, b_spec] + ([o_spec] if add is not None else []), [o_spec],
                       [jax.ShapeDtypeStruct((N_CHIPS, rs, cs), BF16)], [a, b] + ([add] if add is not None else []),
                       sem=("parallel", "parallel"), stages=stages, prefetch=core)
    return out, st


def _adamw(w, g, m, v):
    m = ADAM_B1 * m + (1.0 - ADAM_B1) * g
    v = ADAM_B2 * v + (1.0 - ADAM_B2) * (g * g)
    m_hat = m / (1.0 - ADAM_B1 ** ADAM_STEP)
    v_hat = v / (1.0 - ADAM_B2 ** ADAM_STEP)
    delta = -ADAM_LR * (m_hat / (jnp.sqrt(v_hat) + ADAM_EPS) + ADAM_WD * w)
    return delta, m, v


def _adam_shard(name, pair, chip_sums, chip, w, m, v, tr=256, rows=None, into=None, stages=()):
    rs, cs = w.shape
    lo, cnt = rows or (0, rs)
    assert pair.shape[1] == cnt
    tr = _pick(tr, cnt, lo)
    first = lo // tr

    def body(chip_ref, own_ref, *refs):
        slots, (w_ref, m_ref, v_ref), (g_ref, d_ref, nm_ref, nv_ref) = refs[:N_CHIPS], refs[N_CHIPS:N_CHIPS + 3], refs[-4:]
        g = None
        for q in range(N_CHIPS):
            term = jnp.where(chip_ref[0] == q, own_ref[...], slots[q][...]).astype(F32)
            g = term if g is None else g + term
        d, nm, nv = _adamw(w_ref[...], g, m_ref[...], v_ref[...])
        g_ref[...], d_ref[...], nm_ref[...], nv_ref[...] = g, d, nm, nv

    def slot(q):
        return pl.BlockSpec((None, tr, cs), lambda i, c_ref: (jnp.where(c_ref[0] == q, (q + 1) % N_CHIPS, q), i, 0))

    spec = pl.BlockSpec((tr, cs), lambda i, c_ref: (i + first, 0))
    n_in = 1 + N_CHIPS + 3
    return _call(name, body, (cnt // tr,),
                 [pl.BlockSpec((None, tr, cs), lambda i, c_ref: (c_ref[0], i, 0))] + [slot(q) for q in range(N_CHIPS)]
                 + [spec, spec, spec] + ([ANY] * 4 if into is not None else []),
                 [spec] * 4, [jax.ShapeDtypeStruct((rs, cs), F32)] * 4, [pair] + [chip_sums] * N_CHIPS + [w, m, v] + list(into or ()),
                 sem=("parallel",), stages=stages, prefetch=chip, shown=True,
                 alias={n_in + k: k for k in range(4)} if into is not None else None)


def _adam_small(part, parts, me, w, m, v):
    rows = w.shape[0]

    def body(me_ref, own_ref, *refs):
        slots, (w_ref, m_ref, v_ref, g_ref, d_ref, nm_ref, nv_ref) = refs[:N_DEV], refs[N_DEV:]
        g = None
        for j in range(N_DEV):
            term = jnp.where(me_ref[0] == j, own_ref[...], slots[j][...])
            g = term if g is None else g + term
        d, nm, nv = _adamw(w_ref[...], g, m_ref[...], v_ref[...])
        g_ref[...], d_ref[...], nm_ref[...], nv_ref[...] = g, d, nm, nv

    def slot(j):
        return pl.BlockSpec((None, rows, LANES), lambda i, me_ref: (jnp.where(me_ref[0] == j, (j + 1) % N_DEV, j), 0, 0))

    spec = _fix_spec((rows, LANES))
    return _call("adam_small", body, (1,), [spec] + [slot(j) for j in range(N_DEV)] + [spec, spec, spec], [spec] * 4,
                 [jax.ShapeDtypeStruct((rows, LANES), F32)] * 4, [part] + [parts] * N_DEV + [w, m, v], prefetch=me, shown=True)[0]


def _small_sizes():
    d_b = _d_b()
    return (("loss", 1), ("rel_bias", N_BUCKETS * N_HEADS), ("ln_v_gain", d_b), ("ln_v_bias", d_b),
            ("w_spatial", N_GROUPS * BLOCK * BLOCK), ("b_spatial", N_GROUPS * BLOCK), ("ln1_gain", D_MODEL), ("ln1_bias", D_MODEL),
            ("b_ff1", D_FF), ("b_ff2", D_MODEL), ("ln2_gain", D_MODEL), ("ln2_bias", D_MODEL))


def _pack(vals):
    pieces = []
    for name, size in _small_sizes():
        flat = vals[name].reshape(-1).astype(F32)
        padded = -(-size // (SUBLANES * LANES)) * SUBLANES * LANES
        pieces.append(jnp.pad(flat, (0, padded - size)).reshape(-1, LANES))
    return jnp.concatenate(pieces, axis=0)


def _unpack(buf):
    out, row = {}, 0
    for name, size in _small_sizes():
        rows = -(-size // (SUBLANES * LANES)) * SUBLANES
        out[name] = buf[row:row + rows].reshape(-1)[:size]
        row += rows
    return out


def kernel(x, w_in, rel_bias, ln_v_gain, ln_v_bias, w_spatial, b_spatial, w_proj_a, w_proj_b, w_out, ln1_gain, ln1_bias, w_ff1, b_ff1, w_ff2, b_ff2, ln2_gain, ln2_bias, loss_target, m_w_in, m_rel_bias, m_ln_v_gain, m_ln_v_bias, m_w_spatial, m_b_spatial, m_w_proj_a, m_w_proj_b, m_w_out, m_ln1_gain, m_ln1_bias, m_w_ff1, m_b_ff1, m_w_ff2, m_b_ff2, m_ln2_gain, m_ln2_bias, v_w_in, v_rel_bias, v_ln_v_gain, v_ln_v_bias, v_w_spatial, v_b_spatial, v_w_proj_a, v_w_proj_b, v_w_out, v_ln1_gain, v_ln1_bias, v_w_ff1, v_b_ff1, v_w_ff2, v_b_ff2, v_ln2_gain, v_ln2_bias):
    d_a, d_b, d_in = _d_a(), _d_b(), _d_in()
    weights = dict(w_in=w_in, rel_bias=rel_bias, ln_v_gain=ln_v_gain, ln_v_bias=ln_v_bias, w_spatial=w_spatial, b_spatial=b_spatial,
                   w_proj_a=w_proj_a, w_proj_b=w_proj_b, w_out=w_out, ln1_gain=ln1_gain, ln1_bias=ln1_bias, w_ff1=w_ff1, b_ff1=b_ff1,
                   w_ff2=w_ff2, b_ff2=b_ff2, ln2_gain=ln2_gain, ln2_bias=ln2_bias)
    mom1 = dict(w_in=m_w_in, rel_bias=m_rel_bias, ln_v_gain=m_ln_v_gain, ln_v_bias=m_ln_v_bias, w_spatial=m_w_spatial,
                b_spatial=m_b_spatial, w_proj_a=m_w_proj_a, w_proj_b=m_w_proj_b, w_out=m_w_out, ln1_gain=m_ln1_gain,
                ln1_bias=m_ln1_bias, w_ff1=m_w_ff1, b_ff1=m_b_ff1, w_ff2=m_w_ff2, b_ff2=m_b_ff2, ln2_gain=m_ln2_gain, ln2_bias=m_ln2_bias)
    mom2 = dict(w_in=v_w_in, rel_bias=v_rel_bias, ln_v_gain=v_ln_v_gain, ln_v_bias=v_ln_v_bias, w_spatial=v_w_spatial,
                b_spatial=v_b_spatial, w_proj_a=v_w_proj_a, w_proj_b=v_w_proj_b, w_out=v_w_out, ln1_gain=v_ln1_gain,
                ln1_bias=v_ln1_bias, w_ff1=v_w_ff1, b_ff1=v_b_ff1, w_ff2=v_w_ff2, b_ff2=v_b_ff2, ln2_gain=v_ln2_gain, ln2_bias=v_ln2_bias)

    mx, my, mc = _coords()
    me = (4 * mx + 2 * my + mc).astype(I32).reshape(1)
    chip = (2 * mx + my).astype(I32).reshape(1)
    full = {n: _cast_into_place(f"cast_{n}", weights[n][0], KINDS[n], me)[0] for n in KINDS if n != "w_ff1"}
    sent = {n: (0, 0, 0) for n in KINDS}

    def keep(table, n):
        def store(outs):
            table[n] = outs[0]
        return store

    def gathering(**new):
        stages = []
        for n in KINDS:
            out, relayed, passed = sent[n]
            units = new.get(n, 0)
            if units or relayed < out or passed < relayed:
                st = _gather_stage(full[n], KINDS[n], (out, units) if units else None,
                                   (relayed, out - relayed) if relayed < out else None, (passed, relayed - passed) if passed < relayed else None)
                st.store = keep(full, n)
                sent[n] = (out + units, out, relayed)
                stages.append(st)
        return stages

    def settle(stages, outs):
        for st, o in zip(stages, outs):
            st.store(o)

    def alone(name, stages):
        settle(stages, _comm_only(name, stages))

    def here(n):
        assert sent[n] == (16, 16, 16), (n, sent[n])
        return full[n]

    hosted = gathering(w_in=16)
    full["w_ff1"], st = _cast_into_place("cast_w_ff1", weights["w_ff1"][0], KINDS["w_ff1"], me, stages=hosted)
    settle(hosted, st)
    hosted = gathering()
    xs, xb, target, st = _perm_in("cast_x", x[0], loss_target[0], stages=hosted)
    settle(hosted, st)
    g8 = BLOCK // N_SUB
    ws_t = w_spatial[0].reshape(N_GROUPS, g8, N_SUB, g8, N_SUB).transpose(0, 2, 1, 4, 3).reshape(N_GROUPS, BLOCK, BLOCK)
    bs_t = b_spatial[0].reshape(N_GROUPS, g8, N_SUB).transpose(2, 1, 0).reshape(BLOCK, N_GROUPS)
    idx = _local_index(0)
    causal = jnp.asarray((idx[:, None] >= idx[None, :]).astype(np.float32))
    buckets = jnp.asarray(_bucket_tables())
    hosted = gathering(w_proj_a=16, w_proj_b=16)
    (bias,), st = _bias_expand(rel_bias, buckets, stages=hosted)
    settle(hosted, st)

    hosted = gathering(w_out=8, w_ff1=1)
    (qkv,), st = _matmul("proj_qkv", xb, here("w_in"), "nn", [F32], n=3 * d_a, stages=hosted)
    settle(hosted, st)
    hosted = gathering(w_out=8, w_ff1=6)
    (rest,), st = _matmul("proj_rest", xb, here("w_in"), "nn", [F32], b_off=3 * d_a, n=d_in - 3 * d_a, stages=hosted)
    settle(hosted, st)
    fwd = []
    for p in range(3):
        hosted = gathering(**({"w_ff1": 2}, {"w_ff1": 5}, {"w_ff1": 2, "w_ff2": 1})[p])
        res, st = _attn_fwd(qkv, bias, p, stages=hosted)
        settle(hosted, st)
        fwd.append(res)
    hosted = gathering(w_ff2=1)
    (attn, attn_b, lse), st = _attn_combine([o for o, _ in fwd], [l for _, l in fwd], stages=hosted)
    settle(hosted, st)
    hosted = gathering(w_ff2=2)
    gmlp, st = _gmlp_fwd(rest, 0, ln_v_gain, ln_v_bias, ws_t, bs_t, causal, stages=hosted)
    settle(hosted, st)
    hosted = gathering(w_ff2=3)
    (ya,), st = _matmul("proj_a", attn_b, here("w_proj_a"), "nn", [BF16], stages=hosted)
    settle(hosted, st)
    gate_a, gate_b = 2 * d_b, 2 * d_b + D_MODEL

    def merge(acc, ya_, ga, gb):
        return acc, _sigmoid(ga) * ya_ + _sigmoid(gb) * acc

    hosted = gathering(w_ff2=5)
    (yb, merged), st = _matmul("proj_b_merge", gmlp, here("w_proj_b"), "nn", [BF16, BF16], merge,
                               [(ya, "mn", 0), (rest, "mn", gate_a), (rest, "mn", gate_b)], tn=256, stages=hosted)
    settle(hosted, st)
    hosted = gathering(w_ff2=3)
    (pre1,), st = _matmul("out_proj", merged, here("w_out"), "nn", [F32], lambda acc, x_: (ALPHA * x_ + acc,), [(xs, "mn", 0)], stages=hosted)
    settle(hosted, st)
    hosted = gathering(w_ff2=1)
    (xhat1, rstd1, h1b), st = _ln1_fwd(pre1, ln1_gain, ln1_bias, stages=hosted)
    settle(hosted, st)

    def relu2(acc, b_):
        r = jnp.maximum(acc + b_, 0.0)
        return r, r * r

    hosted = gathering()
    (relu, fb), st = _matmul("ff1", h1b, here("w_ff1"), "nn", [BF16, BF16], relu2, [(b_ff1, "row", 0)], stages=hosted)
    settle(hosted, st)
    alone("gather_w_ff2_sibling", gathering())
    (ff,), _ = _matmul("ff2", fb, here("w_ff2"), "nn", [F32], lambda acc, b_: (acc + b_,), [(b_ff2, "row", 0)], tn=512, tk=2048)

    core = lax.axis_index("c").astype(I32).reshape(1)
    factors, theirs, sib, pair, chips, reduced = {}, {}, {}, {}, {}, {}

    def grad_for_sibling(n, a, b, stages=(), half=None, of=None):
        factors[n] = (a, b, KINDS[of or n], half)
        theirs[n], outs = _dw(f"dw_{n}_sibling", a, b, KINDS[of or n], core, False, half=half, stages=stages)
        settle(stages, outs)

    def to_sibling(n):
        st = _to_sibling_stage(theirs[n])
        st.store = keep(sib, n)
        return st

    def grad_own(n, stages=()):
        a, b, kind, half = factors[n]
        pair[n], outs = _dw(f"dw_{n}_own", a, b, kind, core, True, add=sib[n], half=half, stages=stages)
        settle(stages, outs)
        chips[n] = lax.empty(pair[n].shape, BF16)
        reduced[n] = 0

    def reducing(**new):
        stages = []
        for n, units in new.items():
            st = _to_chips_stage(pair[n], chips[n], (reduced[n], units))
            st.store = keep(chips, n)
            reduced[n] += units
            stages.append(st)
        return stages

    def summed(n):
        assert reduced[n] == 16, (n, reduced[n])
        return chips[n]

    dpre2, dpre2b, g_ln2_gain, g_ln2_bias, g_b_ff2, loss_part = _ln2_loss_bwd(ff, xhat1, ln1_gain, ln1_bias, ln2_gain, ln2_bias, target)
    grad_for_sibling("w_ff2", fb, dpre2b)

    def relu2_bwd(acc, r):
        da = acc * (2.0 * r)
        return da, da

    hosted = [to_sibling("w_ff2")]
    (dab, g_b_ff1), st = _matmul("d_ff1", dpre2b, here("w_ff2"), "nt", [BF16], relu2_bwd, [(relu, "mn", 0)], colsums=(1,), stages=hosted)
    settle(hosted, st)
    grad_own("w_ff2")
    grad_for_sibling("w_ff1", h1b, dab, reducing(w_ff2=3))
    hosted = reducing(w_ff2=8) + [to_sibling("w_ff1")]
    (dh1,), st = _matmul("d_h1", dab, here("w_ff1"), "nt", [F32], lambda acc, d_: (acc + ALPHA * d_,), [(dpre2, "mn", 0)], stages=hosted)
    settle(hosted, st)
    grad_own("w_ff1", reducing(w_ff2=4))
    hosted = reducing(w_ff2=1, w_ff1=2)
    (dpre1, dpre1b, g_ln1_gain, g_ln1_bias), st = _ln1_bwd(dh1, xhat1, rstd1, ln1_gain, stages=hosted)
    settle(hosted, st)
    grad_for_sibling("w_out", merged, dpre1b)

    def merge_bwd(acc, ga, gb, ya_, yb_):
        sa, sb = _sigmoid(ga), _sigmoid(gb)
        return acc * sa, acc * sb, acc * ya_ * (sa * (1.0 - sa)), acc * yb_ * (sb * (1.0 - sb))

    hosted = reducing(w_ff1=6) + [to_sibling("w_out")]
    (dya, dyb, dga, dgb), st = _matmul("d_merge", dpre1b, here("w_out"), "nt", [BF16] * 4, merge_bwd,
                                       [(rest, "mn", gate_a), (rest, "mn", gate_b), (ya, "mn", 0), (yb, "mn", 0)], tn=256, stages=hosted)
    settle(hosted, st)
    grad_own("w_out")
    grad_for_sibling("w_proj_a", attn_b, dya)
    grad_for_sibling("w_proj_b", gmlp, dyb)
    hosted = reducing(w_ff1=1) + [to_sibling("w_proj_a"), to_sibling("w_proj_b")]
    (dattn,), st = _matmul("d_attn", dya, here("w_proj_a"), "nt", [F32], stages=hosted)
    settle(hosted, st)
    grad_own("w_proj_a")
    grad_own("w_proj_b")
    hosted = reducing(w_ff1=1)
    (dgmlp,), st = _matmul("d_gmlp", dyb, here("w_proj_b"), "nt", [F32], stages=hosted)
    settle(hosted, st)
    hosted = reducing(w_ff1=2)
    (du, dvb, dws_t, dbs_t, g_lnv_gain, g_lnv_bias), st = _gmlp_bwd(rest, 0, dgmlp, ln_v_gain, ln_v_bias, ws_t, bs_t, causal, stages=hosted)
    settle(hosted, st)
    delta = _attn_delta(dattn, attn)
    bwd = []
    for p in range(3):
        hosted = reducing(**({"w_ff1": 3}, {"w_ff1": 1, "w_out": 8}, {"w_out": 8, "w_proj_a": 8})[p])
        res, st = _attn_bwd(qkv, dattn, lse, delta, bias, p, stages=hosted)
        settle(hosted, st)
        bwd.append(res)
    g_rel_bias = _rel_bias_grad([b[3] for b in bwd], buckets)
    hosted = reducing(w_proj_a=8, w_proj_b=16)
    (dproj,), st = _assemble_dproj([b[i] for i in range(3) for b in bwd], du, dvb, dga, dgb, stages=hosted)
    settle(hosted, st)

    g_w_spatial = dws_t.reshape(N_GROUPS, N_SUB, g8, N_SUB, g8).transpose(0, 2, 1, 4, 3)
    g_b_spatial = dbs_t[:, :N_GROUPS].reshape(N_SUB, g8, N_GROUPS).transpose(2, 1, 0)
    part = _pack(dict(loss=loss_part, rel_bias=g_rel_bias, ln_v_gain=g_lnv_gain, ln_v_bias=g_lnv_bias, w_spatial=g_w_spatial,
                      b_spatial=g_b_spatial, ln1_gain=g_ln1_gain, ln1_bias=g_ln1_bias, b_ff1=g_b_ff1, b_ff2=g_b_ff2,
                      ln2_gain=g_ln2_gain, ln2_bias=g_ln2_bias))
    sib["small"] = lax.empty((N_DEV, *part.shape), F32)
    cut = part.shape[0] // 2 // SUBLANES * SUBLANES

    def small(rows):
        st = _small_stage(part, sib["small"], rows)
        st.store = keep(sib, "small")
        return st

    grad_for_sibling("w_in_top", xb, dproj, [small((0, cut))], half=0, of="w_in")
    grad_for_sibling("w_in_bot", xb, dproj, [small((cut, part.shape[0] - cut)), to_sibling("w_in_top")], half=1, of="w_in")
    parts = sib["small"]
    grad_own("w_in_top", [to_sibling("w_in_bot")])
    grad_own("w_in_bot", reducing(w_in_top=4))

    def add_residual(acc, d_):
        return (acc + ALPHA * d_,)

    hosted = reducing(w_in_top=12, w_in_bot=7)
    (dx,), st = _matmul("d_x", dproj, here("w_in"), "nt", [F32], add_residual, [(dpre1, "mn", 0)], tn=512, tk=3072, stages=hosted)
    settle(hosted, st)
    grad_x = _from_perm(dx)[None]

    out_g, out_d, out_m, out_v = {}, {}, {}, {}
    for n, units in (("w_out", 3), ("w_proj_a", 3), ("w_proj_b", 3), ("w_ff2", 0), ("w_ff1", 0)):
        hosted = reducing(w_in_bot=units) if units else []
        (g, d, nm, nv), st = _adam_shard(f"adam_{n}", pair[n], summed(n), chip, weights[n][0], mom1[n][0], mom2[n][0], stages=hosted)
        settle(hosted, st)
        out_g[n], out_d[n], out_m[n], out_v[n] = g[None], d[None], nm[None], nv[None]
    rows = weights["w_in"].shape[1] // 2
    done = None
    for i, n in enumerate(("w_in_top", "w_in_bot")):
        done, _ = _adam_shard(f"adam_{n}", pair[n], summed(n), chip, weights["w_in"][0], mom1["w_in"][0], mom2["w_in"][0],
                              rows=(i * rows, rows), into=done)
    out_g["w_in"], out_d["w_in"], out_m["w_in"], out_v["w_in"] = (t[None] for t in done)

    zero = jnp.zeros((1,), F32)
    sg, sd, sm, sv = (_unpack(b) for b in _adam_small(
        part, parts, me, _pack({**weights, "loss": zero}), _pack({**mom1, "loss": zero}), _pack({**mom2, "loss": zero})))
    for n in WEIGHT_ORDER:
        if n not in KINDS:
            shape = weights[n].shape
            out_g[n], out_d[n], out_m[n], out_v[n] = (t[n].reshape(shape) for t in (sg, sd, sm, sv))
    loss = sg["loss"].reshape(())
    return (loss, grad_x, *[out_g[n] for n in WEIGHT_ORDER], *[out_d[n] for n in WEIGHT_ORDER],
            *[out_m[n] for n in WEIGHT_ORDER], *[out_v[n] for n in WEIGHT_ORDER])
```
